```python
import math
import jax, jax.numpy as jnp
from jax import lax
import numpy as np

D_MODEL = 1024
BATCH = 8
SEQ = 4096
DEPTH = 2

D_RG = D_MODEL
RG_HEADS = 8
RG_HEAD_DIM = D_RG // RG_HEADS
D_ML = D_MODEL
ML_HEADS = 4
ML_HEAD_DIM = D_ML // ML_HEADS
D_MIX = D_RG + D_ML
D_IN = 2 * D_RG + 3 * D_ML
CONV_WIDTH = 4
RG_C = 8.0
ML_CHUNK = 128
EPS = 1e-6

kernel_name = "hymba_style_rglru_mlstm_hybrid"


def rms_norm(x, g):
    x32 = x.astype(jnp.float32)
    y = x32 * lax.rsqrt(jnp.mean(x32 * x32, axis=-1, keepdims=True) + EPS)
    return (y * g.astype(jnp.float32)).astype(x.dtype)


def causal_depthwise_conv(x, w, b):
    ch = x.shape[-1]
    y = lax.conv_general_dilated(
        x, w[:, None, :].astype(x.dtype), window_strides=(1,),
        padding=[(CONV_WIDTH - 1, 0)], dimension_numbers=("NWC", "WIO", "NWC"),
        feature_group_count=ch)
    return y + b.astype(x.dtype)


def block_diag(x, w):
    h, dh, dout = w.shape
    xb = x.reshape(x.shape[:-1] + (h, dh))
    return jnp.einsum("bshd,hde->bshe", xb, w).reshape(x.shape[:-1] + (h * dout,))


def rglru(x, w_a, b_a, w_x, b_x, lam):
    r = jax.nn.sigmoid(block_diag(x, w_a) + b_a).astype(jnp.float32)
    i = jax.nn.sigmoid(block_diag(x, w_x) + b_x).astype(jnp.float32)
    log_a = -RG_C * r * jax.nn.softplus(-lam.astype(jnp.float32))
    a = jnp.exp(log_a)
    u = jnp.sqrt(-jnp.expm1(2.0 * log_a)) * (i * x.astype(jnp.float32))

    def combine(lhs, rhs):
        a1, b1 = lhs
        a2, b2 = rhs
        return a1 * a2, a2 * b1 + b2

    _, h = lax.associative_scan(combine, (a, u), axis=1)
    return h.astype(x.dtype)


def mlstm_chunkwise(q, k, v, log_i, log_f):
    bsz, s_len, nh, dh = q.shape
    nc = s_len // ML_CHUNK

    def to_chunks(t):
        return t.reshape(bsz, nc, ML_CHUNK, nh, dh).transpose(1, 0, 3, 2, 4)

    def gate_chunks(t):
        return t.reshape(bsz, nc, ML_CHUNK, nh).transpose(1, 0, 3, 2)

    causal = jnp.tril(jnp.ones((ML_CHUNK, ML_CHUNK), dtype=bool))

    def step(carry, xs):
        c_st, n_st, m_st = carry
        qc, kc, vc, li, lf = xs
        b = jnp.cumsum(lf, axis=-1)
        b_last = b[..., -1]
        d = jnp.where(causal, b[..., :, None] - b[..., None, :] + li[..., None, :], -jnp.inf)
        m_inter = b + m_st[..., None]
        m_t = jnp.maximum(m_inter, jnp.max(d, axis=-1))
        w_intra = jnp.exp(d - m_t[..., None])
        w_inter = jnp.exp(m_inter - m_t)
        s = jnp.einsum("bhtk,bhsk->bhts", qc, kc) * w_intra
        num = (jnp.einsum("bhts,bhsv->bhtv", s, vc)
               + w_inter[..., None] * jnp.einsum("bhtk,bhkv->bhtv", qc, c_st))
        den = jnp.sum(s, axis=-1) + w_inter * jnp.einsum("bhtk,bhk->bht", qc, n_st)
        h = num / jnp.maximum(jnp.abs(den), jnp.exp(-m_t))[..., None]
        g = b_last[..., None] - b + li
        m_new = jnp.maximum(b_last + m_st, jnp.max(g, axis=-1))
        w_state = jnp.exp(g - m_new[..., None])
        decay = jnp.exp(b_last + m_st - m_new)
        kw = kc * w_state[..., None]
        c_new = decay[..., None, None] * c_st + jnp.einsum("bhsk,bhsv->bhkv", kw, vc)
        n_new = decay[..., None] * n_st + jnp.sum(kw, axis=2)
        return (c_new, n_new, m_new), h

    init = (jnp.zeros((bsz, nh, dh, dh), jnp.float32),
            jnp.zeros((bsz, nh, dh), jnp.float32),
            jnp.zeros((bsz, nh), jnp.float32))
    _, h = lax.scan(step, init, (to_chunks(q), to_chunks(k), to_chunks(v),
                                 gate_chunks(log_i), gate_chunks(log_f)))
    return h.transpose(1, 0, 3, 2, 4).reshape(bsz, s_len, nh, dh)


def mlstm_branch(xm, o_pre, conv_w, conv_b, w_q, w_k, w_v, w_if, b_if, head_g):
    bsz, s_len, _ = xm.shape
    xc = jax.nn.silu(causal_depthwise_conv(xm, conv_w, conv_b))
    q = block_diag(xc, w_q)
    k = block_diag(xc, w_k)
    v = block_diag(xm, w_v)
    gates = (jnp.concatenate([q, k, v], axis=-1) @ w_if + b_if).astype(jnp.float32)
    log_i = gates[..., :ML_HEADS]
    log_f = jax.nn.log_sigmoid(gates[..., ML_HEADS:])

    def heads(t):
        return t.reshape(bsz, s_len, ML_HEADS, ML_HEAD_DIM).astype(jnp.float32)

    cell = mlstm_chunkwise(heads(q), heads(k) * (ML_HEAD_DIM ** -0.5), heads(v), log_i, log_f)
    h = jax.nn.sigmoid(heads(o_pre)) * cell
    h = h * lax.rsqrt(jnp.mean(h * h, axis=-1, keepdims=True) + EPS)
    return (h.reshape(bsz, s_len, D_ML) * head_g.astype(jnp.float32)).astype(xm.dtype)


def _fwd_setup_inputs(seed: int = 0) -> dict:
    key = jax.random.key(seed)
    ks = jax.random.split(key, 24)
    f32 = jnp.float32

    def nrm(k, shape, scale):
        return jax.random.normal(k, shape, f32) * scale

    x = jax.random.normal(ks[0], (BATCH, SEQ, D_MODEL), f32)
    c = jax.random.normal(ks[1], (BATCH, D_MODEL), f32)
    norm_g = 1.0 + nrm(ks[2], (DEPTH, D_MODEL), 0.02)
    w_ada = nrm(ks[3], (DEPTH, D_MODEL, 3 * D_MODEL), 0.3 * D_MODEL ** -0.5)
    b_ada = nrm(ks[4], (DEPTH, 3 * D_MODEL), 0.02)
    w_in = nrm(ks[5], (DEPTH, D_MODEL, D_IN), D_MODEL ** -0.5)
    rg_conv_w = nrm(ks[6], (DEPTH, CONV_WIDTH, D_RG), CONV_WIDTH ** -0.5)
    rg_conv_b = nrm(ks[7], (DEPTH, D_RG), 0.02)
    rg_w_a = nrm(ks[8], (DEPTH, RG_HEADS, RG_HEAD_DIM, RG_HEAD_DIM), RG_HEAD_DIM ** -0.5)
    rg_b_a = nrm(ks[9], (DEPTH, D_RG), 0.02)
    rg_w_x = nrm(ks[10], (DEPTH, RG_HEADS, RG_HEAD_DIM, RG_HEAD_DIM), RG_HEAD_DIM ** -0.5)
    rg_b_x = nrm(ks[11], (DEPTH, D_RG), 0.02)
    a_c = jax.random.uniform(ks[12], (DEPTH, D_RG), f32, 0.9, 0.999)
    a0 = a_c ** (1.0 / RG_C)
    rg_lambda = jnp.log(a0) - jnp.log1p(-a0)
    ml_conv_w = nrm(ks[13], (DEPTH, CONV_WIDTH, D_ML), CONV_WIDTH ** -0.5)
    ml_conv_b = nrm(ks[14], (DEPTH, D_ML), 0.02)
    ml_w_q = nrm(ks[15], (DEPTH, ML_HEADS, ML_HEAD_DIM, ML_HEAD_DIM), ML_HEAD_DIM ** -0.5)
    ml_w_k = nrm(ks[16], (DEPTH, ML_HEADS, ML_HEAD_DIM, ML_HEAD_DIM), ML_HEAD_DIM ** -0.5)
    ml_w_v = nrm(ks[17], (DEPTH, ML_HEADS, ML_HEAD_DIM, ML_HEAD_DIM), ML_HEAD_DIM ** -0.5)
    ml_w_if = nrm(ks[18], (DEPTH, 3 * D_ML, 2 * ML_HEADS), 0.1 * (3 * D_ML) ** -0.5)
    b_i = nrm(ks[19], (DEPTH, ML_HEADS), 0.1) - 1.0
    b_f = jnp.linspace(3.0, 6.0, ML_HEADS, dtype=f32)[None, :] + nrm(ks[20], (DEPTH, ML_HEADS), 0.1)
    ml_b_if = jnp.concatenate([b_i, b_f], axis=-1)
    ml_norm_g = 1.0 + nrm(ks[21], (DEPTH, D_ML), 0.02)
    w_out = nrm(ks[22], (DEPTH, D_MIX, D_MODEL), D_MIX ** -0.5)
    final_g = 1.0 + nrm(ks[23], (D_MODEL,), 0.02)
    return {"x": x, "c": c, "norm_g": norm_g, "w_ada": w_ada, "b_ada": b_ada,
            "w_in": w_in, "rg_conv_w": rg_conv_w, "rg_conv_b": rg_conv_b,
            "rg_w_a": rg_w_a, "rg_b_a": rg_b_a, "rg_w_x": rg_w_x, "rg_b_x": rg_b_x,
            "rg_lambda": rg_lambda, "ml_conv_w": ml_conv_w, "ml_conv_b": ml_conv_b,
            "ml_w_q": ml_w_q, "ml_w_k": ml_w_k, "ml_w_v": ml_w_v, "ml_w_if": ml_w_if,
            "ml_b_if": ml_b_if, "ml_norm_g": ml_norm_g, "w_out": w_out, "final_g": final_g}


def _fwd_reference(x, c, norm_g, w_ada, b_ada, w_in, rg_conv_w, rg_conv_b, rg_w_a, rg_b_a,
              rg_w_x, rg_b_x, rg_lambda, ml_conv_w, ml_conv_b, ml_w_q, ml_w_k, ml_w_v,
              ml_w_if, ml_b_if, ml_norm_g, w_out, final_g):
    split_pts = [D_RG, 2 * D_RG, 2 * D_RG + D_ML, 2 * D_RG + 2 * D_ML]
    c_act = jax.nn.silu(c)
    for l in range(DEPTH):
        mod = c_act @ w_ada[l] + b_ada[l]
        shift, scale, gate = jnp.split(mod, 3, axis=-1)
        h = rms_norm(x, norm_g[l]) * (1.0 + scale[:, None, :]) + shift[:, None, :]
        u = h @ w_in[l]
        rg_x, rg_z, ml_x, ml_o, ml_z = jnp.split(u, split_pts, axis=-1)
        y_rg = rglru(causal_depthwise_conv(rg_x, rg_conv_w[l], rg_conv_b[l]),
                     rg_w_a[l], rg_b_a[l], rg_w_x[l], rg_b_x[l], rg_lambda[l]) * jax.nn.silu(rg_z)
        y_ml = mlstm_branch(ml_x, ml_o, ml_conv_w[l], ml_conv_b[l], ml_w_q[l], ml_w_k[l],
                            ml_w_v[l], ml_w_if[l], ml_b_if[l], ml_norm_g[l]) * jax.nn.silu(ml_z)
        y = jnp.concatenate([y_rg, y_ml], axis=-1) @ w_out[l]
        x = x + gate[:, None, :] * y
    return rms_norm(x, final_g)


import jax as _jax
import jax.numpy as _jnp

TWIN_FORMAT = 'train_step'
FWD_PARAMS = ['x', 'c', 'norm_g', 'w_ada', 'b_ada', 'w_in', 'rg_conv_w', 'rg_conv_b', 'rg_w_a', 'rg_b_a', 'rg_w_x', 'rg_b_x', 'rg_lambda', 'ml_conv_w', 'ml_conv_b', 'ml_w_q', 'ml_w_k', 'ml_w_v', 'ml_w_if', 'ml_b_if', 'ml_norm_g', 'w_out', 'final_g']
TWIN_WEIGHTS = ['norm_g', 'w_ada', 'b_ada', 'w_in', 'rg_conv_w', 'rg_conv_b', 'rg_w_a', 'rg_b_a', 'rg_w_x', 'rg_b_x', 'rg_lambda', 'ml_conv_w', 'ml_conv_b', 'ml_w_q', 'ml_w_k', 'ml_w_v', 'ml_w_if', 'ml_b_if', 'ml_norm_g', 'w_out', 'final_g']
TWIN_DIFF_INPUT = 'x'
TWIN_INPUTS = ['x', 'c', 'norm_g', 'w_ada', 'b_ada', 'w_in', 'rg_conv_w', 'rg_conv_b', 'rg_w_a', 'rg_b_a', 'rg_w_x', 'rg_b_x', 'rg_lambda', 'ml_conv_w', 'ml_conv_b', 'ml_w_q', 'ml_w_k', 'ml_w_v', 'ml_w_if', 'ml_b_if', 'ml_norm_g', 'w_out', 'final_g', 'loss_target', 'm_norm_g', 'm_w_ada', 'm_b_ada', 'm_w_in', 'm_rg_conv_w', 'm_rg_conv_b', 'm_rg_w_a', 'm_rg_b_a', 'm_rg_w_x', 'm_rg_b_x', 'm_rg_lambda', 'm_ml_conv_w', 'm_ml_conv_b', 'm_ml_w_q', 'm_ml_w_k', 'm_ml_w_v', 'm_ml_w_if', 'm_ml_b_if', 'm_ml_norm_g', 'm_w_out', 'm_final_g', 'v_norm_g', 'v_w_ada', 'v_b_ada', 'v_w_in', 'v_rg_conv_w', 'v_rg_conv_b', 'v_rg_w_a', 'v_rg_b_a', 'v_rg_w_x', 'v_rg_b_x', 'v_rg_lambda', 'v_ml_conv_w', 'v_ml_conv_b', 'v_ml_w_q', 'v_ml_w_k', 'v_ml_w_v', 'v_ml_w_if', 'v_ml_b_if', 'v_ml_norm_g', 'v_w_out', 'v_final_g']
TWIN_OUTPUTS = ['loss', 'grad_x', 'grad_norm_g', 'grad_w_ada', 'grad_b_ada', 'grad_w_in', 'grad_rg_conv_w', 'grad_rg_conv_b', 'grad_rg_w_a', 'grad_rg_b_a', 'grad_rg_w_x', 'grad_rg_b_x', 'grad_rg_lambda', 'grad_ml_conv_w', 'grad_ml_conv_b', 'grad_ml_w_q', 'grad_ml_w_k', 'grad_ml_w_v', 'grad_ml_w_if', 'grad_ml_b_if', 'grad_ml_norm_g', 'grad_w_out', 'grad_final_g', 'delta_norm_g', 'delta_w_ada', 'delta_b_ada', 'delta_w_in', 'delta_rg_conv_w', 'delta_rg_conv_b', 'delta_rg_w_a', 'delta_rg_b_a', 'delta_rg_w_x', 'delta_rg_b_x', 'delta_rg_lambda', 'delta_ml_conv_w', 'delta_ml_conv_b', 'delta_ml_w_q', 'delta_ml_w_k', 'delta_ml_w_v', 'delta_ml_w_if', 'delta_ml_b_if', 'delta_ml_norm_g', 'delta_w_out', 'delta_final_g', 'new_m_norm_g', 'new_m_w_ada', 'new_m_b_ada', 'new_m_w_in', 'new_m_rg_conv_w', 'new_m_rg_conv_b', 'new_m_rg_w_a', 'new_m_rg_b_a', 'new_m_rg_w_x', 'new_m_rg_b_x', 'new_m_rg_lambda', 'new_m_ml_conv_w', 'new_m_ml_conv_b', 'new_m_ml_w_q', 'new_m_ml_w_k', 'new_m_ml_w_v', 'new_m_ml_w_if', 'new_m_ml_b_if', 'new_m_ml_norm_g', 'new_m_w_out', 'new_m_final_g', 'new_v_norm_g', 'new_v_w_ada', 'new_v_b_ada', 'new_v_w_in', 'new_v_rg_conv_w', 'new_v_rg_conv_b', 'new_v_rg_w_a', 'new_v_rg_b_a', 'new_v_rg_w_x', 'new_v_rg_b_x', 'new_v_rg_lambda', 'new_v_ml_conv_w', 'new_v_ml_conv_b', 'new_v_ml_w_q', 'new_v_ml_w_k', 'new_v_ml_w_v', 'new_v_ml_w_if', 'new_v_ml_b_if', 'new_v_ml_norm_g', 'new_v_w_out', 'new_v_final_g']
TWIN_LEAF_KINDS = {'loss': 'loss', 'grad_x': 'grad_x', 'grad_norm_g': 'grad_w', 'grad_w_ada': 'grad_w', 'grad_b_ada': 'grad_w', 'grad_w_in': 'grad_w', 'grad_rg_conv_w': 'grad_w', 'grad_rg_conv_b': 'grad_w', 'grad_rg_w_a': 'grad_w', 'grad_rg_b_a': 'grad_w', 'grad_rg_w_x': 'grad_w', 'grad_rg_b_x': 'grad_w', 'grad_rg_lambda': 'grad_w', 'grad_ml_conv_w': 'grad_w', 'grad_ml_conv_b': 'grad_w', 'grad_ml_w_q': 'grad_w', 'grad_ml_w_k': 'grad_w', 'grad_ml_w_v': 'grad_w', 'grad_ml_w_if': 'grad_w', 'grad_ml_b_if': 'grad_w', 'grad_ml_norm_g': 'grad_w', 'grad_w_out': 'grad_w', 'grad_final_g': 'grad_w', 'delta_norm_g': 'delta_w', 'delta_w_ada': 'delta_w', 'delta_b_ada': 'delta_w', 'delta_w_in': 'delta_w', 'delta_rg_conv_w': 'delta_w', 'delta_rg_conv_b': 'delta_w', 'delta_rg_w_a': 'delta_w', 'delta_rg_b_a': 'delta_w', 'delta_rg_w_x': 'delta_w', 'delta_rg_b_x': 'delta_w', 'delta_rg_lambda': 'delta_w', 'delta_ml_conv_w': 'delta_w', 'delta_ml_conv_b': 'delta_w', 'delta_ml_w_q': 'delta_w', 'delta_ml_w_k': 'delta_w', 'delta_ml_w_v': 'delta_w', 'delta_ml_w_if': 'delta_w', 'delta_ml_b_if': 'delta_w', 'delta_ml_norm_g': 'delta_w', 'delta_w_out': 'delta_w', 'delta_final_g': 'delta_w', 'new_m_norm_g': 'new_m', 'new_m_w_ada': 'new_m', 'new_m_b_ada': 'new_m', 'new_m_w_in': 'new_m', 'new_m_rg_conv_w': 'new_m', 'new_m_rg_conv_b': 'new_m', 'new_m_rg_w_a': 'new_m', 'new_m_rg_b_a': 'new_m', 'new_m_rg_w_x': 'new_m', 'new_m_rg_b_x': 'new_m', 'new_m_rg_lambda': 'new_m', 'new_m_ml_conv_w': 'new_m', 'new_m_ml_conv_b': 'new_m', 'new_m_ml_w_q': 'new_m', 'new_m_ml_w_k': 'new_m', 'new_m_ml_w_v': 'new_m', 'new_m_ml_w_if': 'new_m', 'new_m_ml_b_if': 'new_m', 'new_m_ml_norm_g': 'new_m', 'new_m_w_out': 'new_m', 'new_m_final_g': 'new_m', 'new_v_norm_g': 'new_v', 'new_v_w_ada': 'new_v', 'new_v_b_ada': 'new_v', 'new_v_w_in': 'new_v', 'new_v_rg_conv_w': 'new_v', 'new_v_rg_conv_b': 'new_v', 'new_v_rg_w_a': 'new_v', 'new_v_rg_b_a': 'new_v', 'new_v_rg_w_x': 'new_v', 'new_v_rg_b_x': 'new_v', 'new_v_rg_lambda': 'new_v', 'new_v_ml_conv_w': 'new_v', 'new_v_ml_conv_b': 'new_v', 'new_v_ml_w_q': 'new_v', 'new_v_ml_w_k': 'new_v', 'new_v_ml_w_v': 'new_v', 'new_v_ml_w_if': 'new_v', 'new_v_ml_b_if': 'new_v', 'new_v_ml_norm_g': 'new_v', 'new_v_w_out': 'new_v', 'new_v_final_g': 'new_v'}


def _forward(args):
    return _fwd_reference(*[args[k] for k in FWD_PARAMS])


def _output_shape():
    def fwd():
        inp = _fwd_setup_inputs(0)
        return _fwd_reference(*[inp[k] for k in FWD_PARAMS])
    out = _jax.eval_shape(fwd)
    return out.shape, out.dtype

N_MICROBATCH = 1
ADAM_LR = 0.001
ADAM_B1 = 0.9
ADAM_B2 = 0.999
ADAM_EPS = 1e-08
ADAM_WD = 0.01
ADAM_STEP = 10
PER_EXAMPLE_BATCH_AXIS = {'x': 0, 'c': 0, 'loss_target': 0}
SHARED_INPUTS = []
_WEIGHT_DTYPES = {'norm_g': _jnp.float32, 'w_ada': _jnp.float32, 'b_ada': _jnp.float32, 'w_in': _jnp.float32, 'rg_conv_w': _jnp.float32, 'rg_conv_b': _jnp.float32, 'rg_w_a': _jnp.float32, 'rg_b_a': _jnp.float32, 'rg_w_x': _jnp.float32, 'rg_b_x': _jnp.float32, 'rg_lambda': _jnp.float32, 'ml_conv_w': _jnp.float32, 'ml_conv_b': _jnp.float32, 'ml_w_q': _jnp.float32, 'ml_w_k': _jnp.float32, 'ml_w_v': _jnp.float32, 'ml_w_if': _jnp.float32, 'ml_b_if': _jnp.float32, 'ml_norm_g': _jnp.float32, 'w_out': _jnp.float32, 'final_g': _jnp.float32}
MOMENT_SCALE = {'norm_g': 3.386550e-02, 'w_ada': 6.290276e-02, 'b_ada': 1.045255e-01, 'w_in': 1.757340e-02, 'rg_conv_w': 1.915378e-02, 'rg_conv_b': 9.875069e-02, 'rg_w_a': 3.155279e-03, 'rg_b_a': 4.132900e-03, 'rg_w_x': 5.665913e-03, 'rg_b_x': 6.924715e-03, 'rg_lambda': 9.545591e-03, 'ml_conv_w': 2.192200e-02, 'ml_conv_b': 2.550503e-02, 'ml_w_q': 1.550980e-02, 'ml_w_k': 1.539158e-02, 'ml_w_v': 1.413874e-02, 'ml_w_if': 9.753343e-02, 'ml_b_if': 5.575081e-02, 'ml_norm_g': 1.407594e-02, 'w_out': 2.423112e-02, 'final_g': 3.198168e+01}


def _to_microbatches(a, axis):
    t = _jnp.moveaxis(a, axis, 0)
    t = t.reshape((N_MICROBATCH, t.shape[0] // N_MICROBATCH) + t.shape[1:])
    return _jnp.moveaxis(t, 1, axis + 1)


def setup_inputs(seed: int = 0) -> dict:
    inp = _fwd_setup_inputs(seed)
    key = _jax.random.fold_in(_jax.random.key(seed), 7919)
    shape, _ = _output_shape()
    out = dict(inp)
    out["loss_target"] = _jax.random.normal(_jax.random.fold_in(key, 0), shape, _jnp.float32)
    for i, name in enumerate(TWIN_WEIGHTS):
        w = inp[name].astype(_jnp.float32)
        if MOMENT_SCALE is None:
            s = _jnp.sqrt(_jnp.mean(_jnp.square(w)) + 1e-30)
        else:
            s = MOMENT_SCALE[name]
        km, kv = _jax.random.split(_jax.random.fold_in(key, i + 1))
        out[name] = w
        out["m_" + name] = s * _jax.random.normal(km, w.shape, _jnp.float32)
        out["v_" + name] = (s * s) * _jax.random.uniform(kv, w.shape, _jnp.float32, 0.5, 1.5)
    if N_MICROBATCH > 1:
        for name, axis in PER_EXAMPLE_BATCH_AXIS.items():
            out[name] = _to_microbatches(out[name], axis)
    return {'x': out['x'], 'c': out['c'], 'norm_g': out['norm_g'], 'w_ada': out['w_ada'], 'b_ada': out['b_ada'], 'w_in': out['w_in'], 'rg_conv_w': out['rg_conv_w'], 'rg_conv_b': out['rg_conv_b'], 'rg_w_a': out['rg_w_a'], 'rg_b_a': out['rg_b_a'], 'rg_w_x': out['rg_w_x'], 'rg_b_x': out['rg_b_x'], 'rg_lambda': out['rg_lambda'], 'ml_conv_w': out['ml_conv_w'], 'ml_conv_b': out['ml_conv_b'], 'ml_w_q': out['ml_w_q'], 'ml_w_k': out['ml_w_k'], 'ml_w_v': out['ml_w_v'], 'ml_w_if': out['ml_w_if'], 'ml_b_if': out['ml_b_if'], 'ml_norm_g': out['ml_norm_g'], 'w_out': out['w_out'], 'final_g': out['final_g'], 'loss_target': out['loss_target'], 'm_norm_g': out['m_norm_g'], 'm_w_ada': out['m_w_ada'], 'm_b_ada': out['m_b_ada'], 'm_w_in': out['m_w_in'], 'm_rg_conv_w': out['m_rg_conv_w'], 'm_rg_conv_b': out['m_rg_conv_b'], 'm_rg_w_a': out['m_rg_w_a'], 'm_rg_b_a': out['m_rg_b_a'], 'm_rg_w_x': out['m_rg_w_x'], 'm_rg_b_x': out['m_rg_b_x'], 'm_rg_lambda': out['m_rg_lambda'], 'm_ml_conv_w': out['m_ml_conv_w'], 'm_ml_conv_b': out['m_ml_conv_b'], 'm_ml_w_q': out['m_ml_w_q'], 'm_ml_w_k': out['m_ml_w_k'], 'm_ml_w_v': out['m_ml_w_v'], 'm_ml_w_if': out['m_ml_w_if'], 'm_ml_b_if': out['m_ml_b_if'], 'm_ml_norm_g': out['m_ml_norm_g'], 'm_w_out': out['m_w_out'], 'm_final_g': out['m_final_g'], 'v_norm_g': out['v_norm_g'], 'v_w_ada': out['v_w_ada'], 'v_b_ada': out['v_b_ada'], 'v_w_in': out['v_w_in'], 'v_rg_conv_w': out['v_rg_conv_w'], 'v_rg_conv_b': out['v_rg_conv_b'], 'v_rg_w_a': out['v_rg_w_a'], 'v_rg_b_a': out['v_rg_b_a'], 'v_rg_w_x': out['v_rg_w_x'], 'v_rg_b_x': out['v_rg_b_x'], 'v_rg_lambda': out['v_rg_lambda'], 'v_ml_conv_w': out['v_ml_conv_w'], 'v_ml_conv_b': out['v_ml_conv_b'], 'v_ml_w_q': out['v_ml_w_q'], 'v_ml_w_k': out['v_ml_w_k'], 'v_ml_w_v': out['v_ml_w_v'], 'v_ml_w_if': out['v_ml_w_if'], 'v_ml_b_if': out['v_ml_b_if'], 'v_ml_norm_g': out['v_ml_norm_g'], 'v_w_out': out['v_w_out'], 'v_final_g': out['v_final_g']}


def _loss(weights, diff, rest, loss_target):
    with _jax.named_scope("forward"):
        args = {**rest, TWIN_DIFF_INPUT: diff, **{k: w.astype(_WEIGHT_DTYPES[k]) for k, w in weights.items()}}
        y = _forward(args)
    with _jax.named_scope("loss_head"):
        err = _jnp.square(y.astype(_jnp.float32) - loss_target)
        return 0.5 * _jnp.sum(_jnp.mean(err, axis=-1)) if err.ndim else 0.5 * err


def _adamw(w, g, m, v):
    m = ADAM_B1 * m + (1.0 - ADAM_B1) * g
    v = ADAM_B2 * v + (1.0 - ADAM_B2) * _jnp.square(g)
    m_hat = m / (1.0 - ADAM_B1 ** ADAM_STEP)
    v_hat = v / (1.0 - ADAM_B2 ** ADAM_STEP)
    delta = -ADAM_LR * (m_hat / (_jnp.sqrt(v_hat) + ADAM_EPS) + ADAM_WD * w)
    return delta, m, v


def reference(x, c, norm_g, w_ada, b_ada, w_in, rg_conv_w, rg_conv_b, rg_w_a, rg_b_a, rg_w_x, rg_b_x, rg_lambda, ml_conv_w, ml_conv_b, ml_w_q, ml_w_k, ml_w_v, ml_w_if, ml_b_if, ml_norm_g, w_out, final_g, loss_target, m_norm_g, m_w_ada, m_b_ada, m_w_in, m_rg_conv_w, m_rg_conv_b, m_rg_w_a, m_rg_b_a, m_rg_w_x, m_rg_b_x, m_rg_lambda, m_ml_conv_w, m_ml_conv_b, m_ml_w_q, m_ml_w_k, m_ml_w_v, m_ml_w_if, m_ml_b_if, m_ml_norm_g, m_w_out, m_final_g, v_norm_g, v_w_ada, v_b_ada, v_w_in, v_rg_conv_w, v_rg_conv_b, v_rg_w_a, v_rg_b_a, v_rg_w_x, v_rg_b_x, v_rg_lambda, v_ml_conv_w, v_ml_conv_b, v_ml_w_q, v_ml_w_k, v_ml_w_v, v_ml_w_if, v_ml_b_if, v_ml_norm_g, v_w_out, v_final_g):
    given = dict(x=x, c=c, norm_g=norm_g, w_ada=w_ada, b_ada=b_ada, w_in=w_in, rg_conv_w=rg_conv_w, rg_conv_b=rg_conv_b, rg_w_a=rg_w_a, rg_b_a=rg_b_a, rg_w_x=rg_w_x, rg_b_x=rg_b_x, rg_lambda=rg_lambda, ml_conv_w=ml_conv_w, ml_conv_b=ml_conv_b, ml_w_q=ml_w_q, ml_w_k=ml_w_k, ml_w_v=ml_w_v, ml_w_if=ml_w_if, ml_b_if=ml_b_if, ml_norm_g=ml_norm_g, w_out=w_out, final_g=final_g, loss_target=loss_target, m_norm_g=m_norm_g, m_w_ada=m_w_ada, m_b_ada=m_b_ada, m_w_in=m_w_in, m_rg_conv_w=m_rg_conv_w, m_rg_conv_b=m_rg_conv_b, m_rg_w_a=m_rg_w_a, m_rg_b_a=m_rg_b_a, m_rg_w_x=m_rg_w_x, m_rg_b_x=m_rg_b_x, m_rg_lambda=m_rg_lambda, m_ml_conv_w=m_ml_conv_w, m_ml_conv_b=m_ml_conv_b, m_ml_w_q=m_ml_w_q, m_ml_w_k=m_ml_w_k, m_ml_w_v=m_ml_w_v, m_ml_w_if=m_ml_w_if, m_ml_b_if=m_ml_b_if, m_ml_norm_g=m_ml_norm_g, m_w_out=m_w_out, m_final_g=m_final_g, v_norm_g=v_norm_g, v_w_ada=v_w_ada, v_b_ada=v_b_ada, v_w_in=v_w_in, v_rg_conv_w=v_rg_conv_w, v_rg_conv_b=v_rg_conv_b, v_rg_w_a=v_rg_w_a, v_rg_b_a=v_rg_b_a, v_rg_w_x=v_rg_w_x, v_rg_b_x=v_rg_b_x, v_rg_lambda=v_rg_lambda, v_ml_conv_w=v_ml_conv_w, v_ml_conv_b=v_ml_conv_b, v_ml_w_q=v_ml_w_q, v_ml_w_k=v_ml_w_k, v_ml_w_v=v_ml_w_v, v_ml_w_if=v_ml_w_if, v_ml_b_if=v_ml_b_if, v_ml_norm_g=v_ml_norm_g, v_w_out=v_w_out, v_final_g=v_final_g)
    weights = {n: given[n] for n in TWIN_WEIGHTS}
    shared = {n: given[n] for n in SHARED_INPUTS}
    per_example = {n: given[n] for n in ['x', 'c']}
    grad_fn = _jax.value_and_grad(_loss, argnums=(0, 1))

    def one_microbatch(ex, loss_target):
        ex = dict(ex)
        diff = ex.pop(TWIN_DIFF_INPUT)
        return grad_fn(weights, diff, {**shared, **ex}, loss_target)

    if N_MICROBATCH == 1:
        loss, (grad_w, grad_x) = one_microbatch(per_example, given["loss_target"])
    else:
        def body(carry, xs):
            loss_sum, grad_sum = carry
            l_k, (gw_k, gx_k) = one_microbatch(xs[0], xs[1])
            with _jax.named_scope("update"):
                return (loss_sum + l_k, _jax.tree.map(_jnp.add, grad_sum, gw_k)), gx_k

        init = (_jnp.zeros((), _jnp.float32), _jax.tree.map(_jnp.zeros_like, weights))
        (loss, grad_w), grad_x = _jax.lax.scan(body, init, (per_example, given["loss_target"]))
    with _jax.named_scope("update"):
        delta_w, new_m, new_v = {}, {}, {}
        for n in TWIN_WEIGHTS:
            delta_w[n], new_m[n], new_v[n] = _adamw(weights[n], grad_w[n], given["m_" + n], given["v_" + n])
    return (loss, grad_x, *[grad_w[n] for n in TWIN_WEIGHTS], *[delta_w[n] for n in TWIN_WEIGHTS],
            *[new_m[n] for n in TWIN_WEIGHTS], *[new_v[n] for n in TWIN_WEIGHTS])
```

```python
import functools

import jax
import jax.numpy as jnp
from jax import lax
from jax.experimental import pallas as pl
from jax.experimental.pallas import tpu as pltpu

F32 = jnp.float32
BF16 = jnp.bfloat16
MESH_AXES = ("x", "y", "c")
N_DEV = 8
EPS = 1e-6
RG_C = 8.0
ML_CHUNK = 128
CONV_WIDTH = 4
ADAM_LR = 0.001
ADAM_B1 = 0.9
ADAM_B2 = 0.999
ADAM_EPS = 1e-08
ADAM_WD = 0.01
ADAM_STEP = 10
NEG_BIG = -1e30
LANES = 128
SUBLANES = 8
VMEM_LIMIT = 56 * 1024 * 1024
HI = lax.Precision.HIGHEST


def _params(n_grid):
    return pltpu.CompilerParams(dimension_semantics=("arbitrary",) * n_grid, vmem_limit_bytes=VMEM_LIMIT)


def _mm(a, b):
    return jnp.dot(a.astype(BF16), b.astype(BF16), preferred_element_type=F32)


def _mm_nt(a, b):
    return lax.dot_general(a.astype(BF16), b.astype(BF16), (((1,), (1,)), ((), ())), preferred_element_type=F32)


def _mm_tn(a, b):
    return lax.dot_general(a.astype(BF16), b.astype(BF16), (((0,), (0,)), ((), ())), preferred_element_type=F32)


def _mm_hi(a, b):
    return jnp.dot(a, b, precision=HI, preferred_element_type=F32)


def _sigmoid(x):
    return 1.0 / (1.0 + jnp.exp(-x))


def _softplus(x):
    return jnp.maximum(x, 0.0) + jnp.log(1.0 + jnp.exp(-jnp.abs(x)))


def _neg_expm1(x):
    poly = -x * (1.0 + x * (0.5 + x * (1.0 / 6.0 + x * (1.0 / 24.0 + x * (1.0 / 120.0)))))
    return jnp.where(jnp.abs(x) < 0.05, poly, 1.0 - jnp.exp(x))


def _iota(shape, dim):
    return lax.broadcasted_iota(jnp.int32, shape, dim)


def _colsum(x):
    return jnp.sum(x, axis=0, keepdims=True)


def _rowsum(x):
    return jnp.sum(x, axis=1, keepdims=True)


def _col(x, j):
    return _rowsum(jnp.where(_iota(x.shape, 1) == j, x, 0.0))


def _row(x, j):
    return _colsum(jnp.where(_iota(x.shape, 0) == j, x, 0.0))


def _shift_down(x, j, prev8):
    if j == 0:
        return x
    t = x.shape[0]
    main = jnp.where(_iota(x.shape, 0) >= j, pltpu.roll(x, j, 0), 0.0)
    fix = jnp.where(_iota(prev8.shape, 0) < j, pltpu.roll(prev8, j, 0), 0.0)
    return jnp.concatenate([main[0:SUBLANES] + fix, main[SUBLANES:t]], axis=0)


def _shift_up(x, j, next8):
    if j == 0:
        return x
    t = x.shape[0]
    main = jnp.where(_iota(x.shape, 0) < t - j, pltpu.roll(x, t - j, 0), 0.0)
    fix = jnp.where(_iota(next8.shape, 0) >= SUBLANES - j, pltpu.roll(next8, SUBLANES - j, 0), 0.0)
    return jnp.concatenate([main[0:t - SUBLANES], main[t - SUBLANES:t] + fix], axis=0)


def _conv(x, prev8, w_ref):
    y = w_ref[CONV_WIDTH - 1:CONV_WIDTH, :] * x
    for j in range(1, CONV_WIDTH):
        y = y + w_ref[CONV_WIDTH - 1 - j:CONV_WIDTH - j, :] * _shift_down(x, j, prev8)
    return y


def _conv_bwd_x(dy, next8, w_ref):
    dx = w_ref[CONV_WIDTH - 1:CONV_WIDTH, :] * dy
    for j in range(1, CONV_WIDTH):
        dx = dx + w_ref[CONV_WIDTH - 1 - j:CONV_WIDTH - j, :] * _shift_up(dy, j, next8)
    return dx


def _scan_fwd(a, b):
    t = a.shape[0]
    row = _iota(a.shape, 0)
    d = 1
    while d < t:
        keep = row >= d
        a_s = jnp.where(keep, pltpu.roll(a, d, 0), 1.0)
        b_s = jnp.where(keep, pltpu.roll(b, d, 0), 0.0)
        b = a * b_s + b
        a = a * a_s
        d *= 2
    return a, b


def _scan_rev(a, b):
    t = a.shape[0]
    row = _iota(a.shape, 0)
    d = 1
    while d < t:
        keep = row < t - d
        a_s = jnp.where(keep, pltpu.roll(a, t - d, 0), 1.0)
        b_s = jnp.where(keep, pltpu.roll(b, t - d, 0), 0.0)
        b = a * b_s + b
        a = a * a_s
        d *= 2
    return a, b


def _blockdiag(x, w_ref, transpose_w=False):
    nh, dh, _ = w_ref.shape
    outs = []
    for h in range(nh):
        xs = x[:, h * dh:(h + 1) * dh]
        outs.append(_mm_nt(xs, w_ref[h]) if transpose_w else _mm(xs, w_ref[h]))
    return jnp.concatenate(outs, axis=1)


def _rg_gates(xc, wa_ref, ba_ref, wx_ref, bx_ref, lam_ref):
    r = _sigmoid(_blockdiag(xc, wa_ref) + ba_ref[...])
    ig = _sigmoid(_blockdiag(xc, wx_ref) + bx_ref[...])
    sp = _softplus(-lam_ref[...])
    log_a = -RG_C * r * sp
    a = jnp.exp(log_a)
    beta = jnp.sqrt(_neg_expm1(2.0 * log_a))
    return r, ig, sp, a, beta


def _bcast8(row):
    return jnp.broadcast_to(row, (SUBLANES, row.shape[1]))


def _full(shape):
    nd = len(shape)
    return pl.BlockSpec(shape, lambda *_: (0,) * nd)


def _mod_call(c, w_ada_g, b_ada):
    nd, nl, d, w = w_ada_g.shape

    def body(c_ref, w_ref, b_ref, mod_ref, cact_ref):
        cv = c_ref[...]
        ca = _bcast8(cv * _sigmoid(cv))
        cact_ref[...] = ca
        mod_ref[0, 0] = _mm(ca, w_ref[0, 0]) + b_ref[0, 0]

    mod, cact = pl.pallas_call(
        body, name="adaln_mod", grid=(nl, nd),
        in_specs=[_full((1, d)),
                  pl.BlockSpec((1, 1, d, w), lambda l, j: (j, l, 0, 0)),
                  pl.BlockSpec((1, 1, 1, w), lambda l, j: (l, j, 0, 0))],
        out_specs=[pl.BlockSpec((1, 1, SUBLANES, w), lambda l, j: (l, j, 0, 0)), _full((SUBLANES, d))],
        out_shape=[jax.ShapeDtypeStruct((nl, nd, SUBLANES, w), F32), jax.ShapeDtypeStruct((SUBLANES, d), F32)],
        compiler_params=_params(2),
    )(c, w_ada_g, b_ada.reshape(nl, nd, 1, w))
    return mod[:, :, 0, :].reshape(nl, nd * w), cact


def _in_fwd(x, ng, scale, shift, w_in_g, layer, tile):
    s, d = x.shape
    nd, _, _, w = w_in_g.shape

    def body(x_ref, ng_ref, sc_ref, sh_ref, w_ref, u_ref, h_ref):
        xv = x_ref[...]
        rs = lax.rsqrt(jnp.mean(xv * xv, axis=1, keepdims=True) + EPS)
        hb = (xv * rs * ng_ref[...] * (1.0 + sc_ref[...]) + sh_ref[...]).astype(BF16)
        h_ref[...] = hb
        for j in range(nd):
            u_ref[:, j * w:(j + 1) * w] = jnp.dot(hb, w_ref[j, 0], preferred_element_type=F32)

    return pl.pallas_call(
        body, name="in_proj_fwd", grid=(s // tile,),
        in_specs=[pl.BlockSpec((tile, d), lambda i: (i, 0)), _full((1, d)), _full((1, d)), _full((1, d)),
                  pl.BlockSpec((nd, 1, d, w), lambda i: (0, layer, 0, 0))],
        out_specs=[pl.BlockSpec((tile, nd * w), lambda i: (i, 0)), pl.BlockSpec((tile, d), lambda i: (i, 0))],
        out_shape=[jax.ShapeDtypeStruct((s, nd * w), F32), jax.ShapeDtypeStruct((s, d), BF16)],
        compiler_params=_params(1),
    )(x, ng, scale, shift, w_in_g)


def _rg_fwd(u, d, conv_w, conv_b, w_a, b_a, w_x, b_x, lam, tile):
    s = u.shape[0]

    def body(x_ref, z_ref, cw_ref, cb_ref, wa_ref, ba_ref, wx_ref, bx_ref, lam_ref, h_ref, y_ref, prev8, hcar):
        @pl.when(pl.program_id(0) == 0)
        def _():
            prev8[...] = jnp.zeros_like(prev8)
            hcar[...] = jnp.zeros_like(hcar)

        x = x_ref[...]
        xc = _conv(x, prev8[...], cw_ref) + cb_ref[...]
        prev8[...] = x[tile - SUBLANES:tile, :]
        _, ig, _, a, beta = _rg_gates(xc, wa_ref, ba_ref, wx_ref, bx_ref, lam_ref)
        acum, bsum = _scan_fwd(a, beta * ig * xc)
        h = bsum + acum * hcar[SUBLANES - 1:SUBLANES, :]
        hcar[...] = h[tile - SUBLANES:tile, :]
        h_ref[...] = h
        z = z_ref[...]
        y_ref[...] = h * z * _sigmoid(z)

    vec = _full((1, d))
    return pl.pallas_call(
        body, name="rglru_fwd", grid=(s // tile,),
        in_specs=[pl.BlockSpec((tile, d), lambda i: (i, 0)), pl.BlockSpec((tile, d), lambda i: (i, 1)),
                  _full(conv_w.shape), vec, _full(w_a.shape), vec, _full(w_x.shape), vec, vec],
        out_specs=[pl.BlockSpec((tile, d), lambda i: (i, 0))] * 2,
        out_shape=[jax.ShapeDtypeStruct((s, d), F32)] * 2,
        scratch_shapes=[pltpu.VMEM((SUBLANES, d), F32), pltpu.VMEM((SUBLANES, d), F32)],
        compiler_params=_params(1),
    )(u, u, conv_w, conv_b, w_a, b_a, w_x, b_x, lam)


def _ml_pre(u, d, conv_w, conv_b, w_q, w_k, w_v, wif, bif, tile):
    s = u.shape[0]
    nh = w_q.shape[0]

    def body(x_ref, cw_ref, cb_ref, wq_ref, wk_ref, wv_ref, wif_ref, bif_ref, q_ref, k_ref, v_ref, g_ref, prev8):
        @pl.when(pl.program_id(0) == 0)
        def _():
            prev8[...] = jnp.zeros_like(prev8)

        x = x_ref[...]
        pre = _conv(x, prev8[...], cw_ref) + cb_ref[...]
        prev8[...] = x[tile - SUBLANES:tile, :]
        xc = pre * _sigmoid(pre)
        q = _blockdiag(xc, wq_ref)
        k = _blockdiag(xc, wk_ref)
        v = _blockdiag(x, wv_ref)
        q_ref[...] = q
        k_ref[...] = k
        v_ref[...] = v
        g = _mm(q, wif_ref[0:d, :]) + _mm(k, wif_ref[d:2 * d, :]) + _mm(v, wif_ref[2 * d:3 * d, :]) + bif_ref[...]
        lane = _iota(g.shape, 1)
        gl = jnp.where(lane < 4, g, jnp.where(lane < 8, -_softplus(-g), 0.0))
        tri = jnp.where(_iota((ML_CHUNK, ML_CHUNK), 1) <= _iota((ML_CHUNK, ML_CHUNK), 0), 1.0, 0.0)
        cums = [_mm_hi(tri, gl[c * ML_CHUNK:(c + 1) * ML_CHUNK, :]) for c in range(tile // ML_CHUNK)]
        cum = cums[0] if len(cums) == 1 else jnp.concatenate(cums, axis=0)
        g_ref[...] = gl + jnp.where((lane >= 8) & (lane < 12), pltpu.roll(cum, 4, 1), 0.0)

    vec = _full((1, d))
    return pl.pallas_call(
        body, name="mlstm_proj_fwd", grid=(s // tile,),
        in_specs=[pl.BlockSpec((tile, d), lambda i: (i, 2)), _full(conv_w.shape), vec,
                  _full(w_q.shape), _full(w_k.shape), _full(w_v.shape), _full(wif.shape), _full((1, LANES))],
        out_specs=[pl.BlockSpec((tile, d), lambda i: (i, 0))] * 3 + [pl.BlockSpec((tile, LANES), lambda i: (i, 0))],
        out_shape=[jax.ShapeDtypeStruct((s, d), F32)] * 3 + [jax.ShapeDtypeStruct((s, LANES), F32)],
        scratch_shapes=[pltpu.VMEM((SUBLANES, d), F32)],
        compiler_params=_params(1),
    )(u, conv_w, conv_b, w_q, w_k, w_v, wif, bif)


def _cell_chunk(h, nh, q_ref, k_ref, v_ref, gc, gr, m_prev, c_h, n_h, m_t=None):
    lc = ML_CHUNK
    dh = q_ref.shape[1] // nh
    sl = slice(h * dh, (h + 1) * dh)
    qh = q_ref[:, sl]
    kh = k_ref[:, sl] * (dh ** -0.5)
    vh = v_ref[:, sl]
    li_c = _col(gc, h)
    b_c = _col(gc, 8 + h)
    lib_r = _row(gr, h) - _row(gr, 8 + h)
    b_last = _colsum(jnp.where(_iota((lc, 1), 0) == lc - 1, b_c, 0.0))
    causal = _iota((lc, lc), 1) <= _iota((lc, lc), 0)
    dmat = jnp.where(causal, b_c + lib_r, NEG_BIG)
    m_inter = b_c + m_prev
    if m_t is None:
        m_t = jnp.maximum(m_inter, jnp.max(dmat, axis=1, keepdims=True))
    w_intra = jnp.exp(dmat - m_t)
    w_inter = jnp.exp(m_inter - m_t)
    amat = _mm_nt(qh, kh)
    smat = amat * w_intra
    qc = _mm(qh, c_h)
    qn = _rowsum(qh * n_h)
    den = _rowsum(smat) + w_inter * qn
    gst = b_last - b_c + li_c
    m_new = jnp.maximum(b_last + m_prev, jnp.max(gst, axis=0, keepdims=True))
    w_state = jnp.exp(gst - m_new)
    decay = jnp.exp(b_last + m_prev - m_new)
    return dict(sl=sl, qh=qh, kh=kh, vh=vh, m_t=m_t, w_intra=w_intra, w_inter=w_inter, smat=smat, qc=qc, qn=qn,
                den=den, m_new=m_new, w_state=w_state, decay=decay)


def _ml_cell_fwd(q, k, v, gcol, grow, u, ng, nh):
    s, d = q.shape
    lc = ML_CHUNK
    nc = s // lc
    dh = d // nh

    def body(q_ref, k_ref, v_ref, gc_ref, gr_ref, o_ref, z_ref, ng_ref,
             cell_ref, y_ref, cs_ref, ns_ref, ms_ref, mt_ref, c_sc, n_sc, m_sc):
        @pl.when(pl.program_id(0) == 0)
        def _():
            c_sc[...] = jnp.zeros_like(c_sc)
            n_sc[...] = jnp.zeros_like(n_sc)
            m_sc[...] = jnp.zeros_like(m_sc)

        gc = gc_ref[...]
        gr = gr_ref[...]
        lane = _iota((lc, LANES), 1)
        mt_acc = jnp.zeros((lc, LANES), F32)
        for h in range(nh):
            c_h = c_sc[h]
            n_h = n_sc[h, 0:1, :]
            m_prev = jnp.max(m_sc[h, 0:1, :], axis=1, keepdims=True)
            cs_ref[0, h] = c_h
            ns_ref[0, h] = n_sc[h]
            ms_ref[0, h] = m_sc[h]
            t = _cell_chunk(h, nh, q_ref, k_ref, v_ref, gc, gr, m_prev, c_h, n_h)
            sl = t["sl"]
            num = _mm(t["smat"], t["vh"]) + t["w_inter"] * t["qc"]
            cell_h = num / jnp.maximum(jnp.abs(t["den"]), jnp.exp(-t["m_t"]))
            mt_acc = jnp.where(lane == h, t["m_t"], mt_acc)
            kw = t["kh"] * t["w_state"]
            c_sc[h] = t["decay"] * c_h + _mm_tn(kw, t["vh"])
            n_sc[h] = _bcast8(t["decay"] * n_h + _colsum(kw))
            m_sc[h] = jnp.broadcast_to(t["m_new"], (SUBLANES, LANES))
            hg = _sigmoid(o_ref[:, sl]) * cell_h
            hn = hg * lax.rsqrt(jnp.mean(hg * hg, axis=1, keepdims=True) + EPS)
            z = z_ref[:, sl]
            cell_ref[:, sl] = cell_h
            y_ref[:, sl] = hn * ng_ref[:, sl] * z * _sigmoid(z)
        mt_ref[...] = mt_acc

    tok = pl.BlockSpec((lc, d), lambda c: (c, 0))
    return pl.pallas_call(
        body, name="mlstm_cell_fwd", grid=(nc,),
        in_specs=[tok, tok, tok, pl.BlockSpec((lc, LANES), lambda c: (c, 0)), pl.BlockSpec((16, lc), lambda c: (0, c)),
                  pl.BlockSpec((lc, d), lambda c: (c, 3)), pl.BlockSpec((lc, d), lambda c: (c, 4)), _full((1, d))],
        out_specs=[tok, tok, pl.BlockSpec((1, nh, dh, dh), lambda c: (c, 0, 0, 0)),
                   pl.BlockSpec((1, nh, SUBLANES, dh), lambda c: (c, 0, 0, 0)),
                   pl.BlockSpec((1, nh, SUBLANES, LANES), lambda c: (c, 0, 0, 0)),
                   pl.BlockSpec((lc, LANES), lambda c: (c, 0))],
        out_shape=[jax.ShapeDtypeStruct((s, d), F32), jax.ShapeDtypeStruct((s, d), F32),
                   jax.ShapeDtypeStruct((nc, nh, dh, dh), F32), jax.ShapeDtypeStruct((nc, nh, SUBLANES, dh), F32),
                   jax.ShapeDtypeStruct((nc, nh, SUBLANES, LANES), F32), jax.ShapeDtypeStruct((s, LANES), F32)],
        scratch_shapes=[pltpu.VMEM((nh, dh, dh), F32), pltpu.VMEM((nh, SUBLANES, dh), F32),
                        pltpu.VMEM((nh, SUBLANES, LANES), F32)],
        compiler_params=_params(1),
    )(q, k, v, gcol, grow, u, u, ng)


def _out_fwd(x, y_rg, y_ml, gate, w_out_g, layer, tile):
    s, d = x.shape
    nd, _, r, _ = w_out_g.shape

    def body(x_ref, yr_ref, ym_ref, g_ref, w_ref, xn_ref, y_ref):
        acc = jnp.zeros((tile, d), F32)
        for j in range(nd):
            src = yr_ref if j * r < d else ym_ref
            off = (j * r) % d
            acc = acc + _mm(src[:, off:off + r], w_ref[j, 0])
        y_ref[...] = acc
        xn_ref[...] = x_ref[...] + g_ref[...] * acc

    tok = pl.BlockSpec((tile, d), lambda i: (i, 0))
    return pl.pallas_call(
        body, name="out_proj_fwd", grid=(s // tile,),
        in_specs=[tok, tok, tok, _full((1, d)), pl.BlockSpec((nd, 1, r, d), lambda i: (0, layer, 0, 0))],
        out_specs=[tok, tok],
        out_shape=[jax.ShapeDtypeStruct((s, d), F32)] * 2,
        compiler_params=_params(1),
    )(x, y_rg, y_ml, gate, w_out_g)


def _loss_call(x, fg, target, tile):
    s, d = x.shape

    def body(x_ref, g_ref, t_ref, dx_ref, loss_ref, gg_ref):
        @pl.when(pl.program_id(0) == 0)
        def _():
            loss_ref[...] = jnp.zeros_like(loss_ref)
            gg_ref[...] = jnp.zeros_like(gg_ref)

        xv = x_ref[...]
        g = g_ref[...]
        rs = lax.rsqrt(jnp.mean(xv * xv, axis=1, keepdims=True) + EPS)
        xh = xv * rs
        e = xh * g - t_ref[...]
        loss_ref[...] += jnp.broadcast_to(_colsum(_rowsum(e * e)) * (0.5 / d), loss_ref.shape)
        dy = e * (1.0 / d)
        gg_ref[...] += _bcast8(_colsum(dy * xh))
        dxh = dy * g
        dx_ref[...] = rs * (dxh - xh * jnp.mean(dxh * xh, axis=1, keepdims=True))

    tok = pl.BlockSpec((tile, d), lambda i: (i, 0))
    return pl.pallas_call(
        body, name="final_norm_loss", grid=(s // tile,),
        in_specs=[tok, _full((1, d)), tok],
        out_specs=[tok, _full((SUBLANES, LANES)), _full((SUBLANES, d))],
        out_shape=[jax.ShapeDtypeStruct((s, d), F32), jax.ShapeDtypeStruct((SUBLANES, LANES), F32),
                   jax.ShapeDtypeStruct((SUBLANES, d), F32)],
        compiler_params=_params(1),
    )(x, fg, target)


def _out_bwd(dxo, gate, y, y_rg, y_ml, w_out_g, layer, tile):
    s, d = dxo.shape
    nd, _, r, _ = w_out_g.shape

    def body(dx_ref, g_ref, y_ref, yr_ref, ym_ref, w_ref, dyr_ref, dym_ref, gw_ref, dg_ref):
        @pl.when(pl.program_id(0) == 0)
        def _():
            gw_ref[...] = jnp.zeros_like(gw_ref)
            dg_ref[...] = jnp.zeros_like(dg_ref)

        dxv = dx_ref[...]
        dg_ref[...] += _bcast8(_colsum(dxv * y_ref[...]))
        dyb = (dxv * g_ref[...]).astype(BF16)
        for j in range(nd):
            src, dst = (yr_ref, dyr_ref) if j * r < d else (ym_ref, dym_ref)
            off = (j * r) % d
            dst[:, off:off + r] = _mm_nt(dyb, w_ref[j, 0])
            gw_ref[j] += _mm_tn(src[:, off:off + r], dyb)

    tok = pl.BlockSpec((tile, d), lambda i: (i, 0))
    return pl.pallas_call(
        body, name="out_proj_bwd", grid=(s // tile,),
        in_specs=[tok, _full((1, d)), tok, tok, tok, pl.BlockSpec((nd, 1, r, d), lambda i: (0, layer, 0, 0))],
        out_specs=[tok, tok, _full((nd, r, d)), _full((SUBLANES, d))],
        out_shape=[jax.ShapeDtypeStruct((s, d), F32)] * 2 + [jax.ShapeDtypeStruct((nd, r, d), F32),
                                                             jax.ShapeDtypeStruct((SUBLANES, d), F32)],
        compiler_params=_params(1),
    )(dxo, gate, y, y_rg, y_ml, w_out_g)


def _ml_cell_bwd(dy_ml, u, cell, q, k, v, gcol, grow, mt, cs, ns, ms, ng, nh):
    s, d = q.shape
    lc = ML_CHUNK
    nc = s // lc
    dh = d // nh

    def body(dy_ref, o_ref, z_ref, cell_ref, q_ref, k_ref, v_ref, gc_ref, gr_ref, mt_ref, cs_ref, ns_ref, ms_ref,
             ng_ref, dq_ref, dk_ref, dv_ref, dg_ref, do_ref, dz_ref, gng_ref, dc_sc, dn_sc):
        @pl.when(pl.program_id(0) == 0)
        def _():
            dc_sc[...] = jnp.zeros_like(dc_sc)
            dn_sc[...] = jnp.zeros_like(dn_sc)
            gng_ref[...] = jnp.zeros_like(gng_ref)

        gc = gc_ref[...]
        gr = gr_ref[...]
        mtv = mt_ref[...]
        lane = _iota((lc, LANES), 1)
        rowv = _iota((lc, 1), 0)
        dg_acc = jnp.zeros((lc, LANES), F32)
        for h in range(nh):
            c_h = cs_ref[0, h]
            n_h = ns_ref[0, h, 0:1, :]
            m_prev = jnp.max(ms_ref[0, h, 0:1, :], axis=1, keepdims=True)
            t = _cell_chunk(h, nh, q_ref, k_ref, v_ref, gc, gr, m_prev, c_h, n_h, m_t=_col(mtv, h))
            sl, qh, kh, vh = t["sl"], t["qh"], t["kh"], t["vh"]
            w_intra, w_inter, smat, w_state, decay = t["w_intra"], t["w_inter"], t["smat"], t["w_state"], t["decay"]
            cell_h = cell_ref[:, sl]
            o = o_ref[:, sl]
            z = z_ref[:, sl]
            dyv = dy_ref[:, sl]
            ngh = ng_ref[:, sl]
            so = _sigmoid(o)
            hg = so * cell_h
            rinv = lax.rsqrt(jnp.mean(hg * hg, axis=1, keepdims=True) + EPS)
            hn = hg * rinv
            sz = _sigmoid(z)
            dz_ref[:, sl] = (dyv * hn * ngh * (sz + z * sz * (1.0 - sz))).astype(BF16)
            dymid = dyv * z * sz
            gng_ref[:, sl] += _bcast8(_colsum(dymid * hn))
            dhn = dymid * ngh
            dhg = rinv * (dhn - hn * jnp.mean(dhn * hn, axis=1, keepdims=True))
            do_ref[:, sl] = (dhg * cell_h * so * (1.0 - so)).astype(BF16)
            dcell = dhg * so
            eneg = jnp.exp(-t["m_t"])
            aden = jnp.abs(t["den"])
            nst = jnp.maximum(aden, eneg)
            dnum = dcell / nst
            dden = jnp.where(aden > eneg, -_rowsum(cell_h * dcell) / nst * jnp.sign(t["den"]), 0.0)
            pmat = _mm_nt(dnum, vh) + dden
            damat = pmat * w_intra
            gmat = pmat * smat
            wdn = w_inter * dnum
            wdd = w_inter * dden
            dqh = _mm(damat, kh) + _mm_nt(wdn, c_h) + wdd * n_h
            dkh = _mm_tn(damat, qh)
            dvh = _mm_tn(smat, dnum)
            dw_inter = _rowsum(dnum * t["qc"]) + dden * t["qn"]
            dcn = dc_sc[h]
            dnn = dn_sc[h, 0:1, :]
            kw = kh * w_state
            dkw = _mm_nt(vh, dcn) + dnn
            dvh = dvh + _mm(kw, dcn)
            dkh = dkh + dkw * w_state
            dgst = _rowsum(dkw * kh) * w_state
            ddecay = _colsum(_rowsum(dcn * c_h)) + _rowsum(dnn * n_h)
            db_last = _colsum(dgst) + ddecay * decay
            rs_g = _rowsum(gmat)
            cs_g = _rowsum(gmat.T)
            db = rs_g - cs_g + dw_inter * w_inter - dgst + jnp.where(rowv == lc - 1, db_last, 0.0)
            dli = cs_g + dgst
            dc_sc[h] = decay * dcn + _mm_tn(qh, wdn)
            dn_sc[h] = _bcast8(decay * dnn + _colsum(qh * wdd))
            dq_ref[:, sl] = dqh
            dk_ref[:, sl] = dkh * (dh ** -0.5)
            dv_ref[:, sl] = dvh
            dg_acc = jnp.where(lane == h, dli, jnp.where(lane == 4 + h, db, dg_acc))
        dg_ref[...] = dg_acc

    rev = lambda c: nc - 1 - c
    tok = pl.BlockSpec((lc, d), lambda c: (rev(c), 0))
    g128 = pl.BlockSpec((lc, LANES), lambda c: (rev(c), 0))
    return pl.pallas_call(
        body, name="mlstm_cell_bwd", grid=(nc,),
        in_specs=[tok, pl.BlockSpec((lc, d), lambda c: (rev(c), 3)), pl.BlockSpec((lc, d), lambda c: (rev(c), 4)),
                  tok, tok, tok, tok, g128, pl.BlockSpec((16, lc), lambda c: (0, rev(c))), g128,
                  pl.BlockSpec((1, nh, dh, dh), lambda c: (rev(c), 0, 0, 0)),
                  pl.BlockSpec((1, nh, SUBLANES, dh), lambda c: (rev(c), 0, 0, 0)),
                  pl.BlockSpec((1, nh, SUBLANES, LANES), lambda c: (rev(c), 0, 0, 0)), _full((1, d))],
        out_specs=[tok, tok, tok, g128, tok, tok, _full((SUBLANES, d))],
        out_shape=[jax.ShapeDtypeStruct((s, d), F32)] * 3 + [jax.ShapeDtypeStruct((s, LANES), F32)]
        + [jax.ShapeDtypeStruct((s, d), BF16)] * 2 + [jax.ShapeDtypeStruct((SUBLANES, d), F32)],
        scratch_shapes=[pltpu.VMEM((nh, dh, dh), F32), pltpu.VMEM((nh, SUBLANES, dh), F32)],
        compiler_params=_params(1),
    )(dy_ml, u, u, cell, q, k, v, gcol, grow, mt, cs, ns, ms, ng)


def _halo_spec(d, tile, nt, col):
    per = tile // SUBLANES
    return pl.BlockSpec((SUBLANES, d), lambda i: (jnp.maximum((nt - 1 - i) * per - 1, 0), col))


def _ml_pre_bwd(dq, dk, dv, dgates, gcol, u, q, k, v, conv_w, conv_b, w_q, w_k, w_v, wif_t, tile):
    s, d = dq.shape
    nt = s // tile
    nh, dh, _ = w_q.shape

    def body(dq_ref, dk_ref, dv_ref, dg_ref, gc_ref, x_ref, halo_ref, q_ref, k_ref, v_ref, cw_ref, cb_ref,
             wq_ref, wk_ref, wv_ref, wift_ref,
             dx_ref, gwq_ref, gwk_ref, gwv_ref, gwif_ref, gbif_ref, gcw_ref, gcb_ref, next8):
        i = pl.program_id(0)

        @pl.when(i == 0)
        def _():
            next8[...] = jnp.zeros_like(next8)
            for ref in (gwq_ref, gwk_ref, gwv_ref, gwif_ref, gbif_ref, gcw_ref, gcb_ref):
                ref[...] = jnp.zeros_like(ref)

        x = x_ref[...]
        halo = halo_ref[...] * jnp.where(i < nt - 1, 1.0, 0.0)
        pre = _conv(x, halo, cw_ref) + cb_ref[...]
        sg = _sigmoid(pre)
        xc = pre * sg
        dgc = dg_ref[...]
        lane = _iota(dgc.shape, 1)
        utri = jnp.where(_iota((ML_CHUNK, ML_CHUNK), 0) <= _iota((ML_CHUNK, ML_CHUNK), 1), 1.0, 0.0)
        rcs = [_mm_hi(utri, dgc[c * ML_CHUNK:(c + 1) * ML_CHUNK, :]) for c in range(tile // ML_CHUNK)]
        rc = rcs[0] if len(rcs) == 1 else jnp.concatenate(rcs, axis=0)
        dgates_v = jnp.where(lane < 4, dgc, jnp.where(lane < 8, rc * (1.0 - jnp.exp(gc_ref[...])), 0.0))
        dgb = dgates_v.astype(BF16)
        gbif_ref[...] += jnp.broadcast_to(_colsum(dgates_v), gbif_ref.shape)
        ext = jnp.dot(dgb, wift_ref[...], preferred_element_type=F32)
        dqt = dq_ref[...] + ext[:, 0:d]
        dkt = dk_ref[...] + ext[:, d:2 * d]
        dvt = dv_ref[...] + ext[:, 2 * d:3 * d]
        gwif_ref[:, 0:d] += _mm_tn(dgb, q_ref[...])
        gwif_ref[:, d:2 * d] += _mm_tn(dgb, k_ref[...])
        gwif_ref[:, 2 * d:3 * d] += _mm_tn(dgb, v_ref[...])
        dxc_parts, dxv_parts = [], []
        for h in range(nh):
            sl = slice(h * dh, (h + 1) * dh)
            gwq_ref[h] += _mm_tn(xc[:, sl], dqt[:, sl])
            gwk_ref[h] += _mm_tn(xc[:, sl], dkt[:, sl])
            gwv_ref[h] += _mm_tn(x[:, sl], dvt[:, sl])
            dxc_parts.append(_mm_nt(dqt[:, sl], wq_ref[h]) + _mm_nt(dkt[:, sl], wk_ref[h]))
            dxv_parts.append(_mm_nt(dvt[:, sl], wv_ref[h]))
        dxc = jnp.concatenate(dxc_parts, axis=1)
        dxv = jnp.concatenate(dxv_parts, axis=1)
        dpre = dxc * (sg + pre * sg * (1.0 - sg))
        gcb_ref[...] += _bcast8(_colsum(dpre))
        for kk in range(CONV_WIDTH):
            gcw_ref[kk:kk + 1, :] += _colsum(dpre * _shift_down(x, CONV_WIDTH - 1 - kk, halo))
        dx_ref[...] = (dxv + _conv_bwd_x(dpre, next8[...], cw_ref)).astype(BF16)
        next8[...] = dpre[0:SUBLANES, :]

    rev = lambda i: nt - 1 - i
    tok = pl.BlockSpec((tile, d), lambda i: (rev(i), 0))
    g128 = pl.BlockSpec((tile, LANES), lambda i: (rev(i), 0))
    wsh = (nh, dh, dh)
    return pl.pallas_call(
        body, name="mlstm_proj_bwd", grid=(nt,),
        in_specs=[tok, tok, tok, g128, g128, pl.BlockSpec((tile, d), lambda i: (rev(i), 2)), _halo_spec(d, tile, nt, 2),
                  tok, tok, tok, _full(conv_w.shape), _full((1, d)), _full(wsh), _full(wsh), _full(wsh),
                  _full(wif_t.shape)],
        out_specs=[tok, _full(wsh), _full(wsh), _full(wsh), _full((LANES, 3 * d)), _full((SUBLANES, LANES)),
                   _full((SUBLANES, d)), _full((SUBLANES, d))],
        out_shape=[jax.ShapeDtypeStruct((s, d), BF16)] + [jax.ShapeDtypeStruct(wsh, F32)] * 3
        + [jax.ShapeDtypeStruct((LANES, 3 * d), F32), jax.ShapeDtypeStruct((SUBLANES, LANES), F32),
           jax.ShapeDtypeStruct((SUBLANES, d), F32), jax.ShapeDtypeStruct((SUBLANES, d), F32)],
        scratch_shapes=[pltpu.VMEM((SUBLANES, d), F32)],
        compiler_params=_params(1),
    )(dq, dk, dv, dgates, gcol, u, u, q, k, v, conv_w, conv_b, w_q, w_k, w_v, wif_t)


def _rg_bwd(dy_rg, u, h_rg, conv_w, conv_b, w_a, b_a, w_x, b_x, lam, tile):
    s, d = dy_rg.shape
    nt = s // tile
    nh, dh, _ = w_a.shape

    def body(dy_ref, x_ref, xhalo_ref, z_ref, h_ref, hhalo_ref, cw_ref, cb_ref, wa_ref, ba_ref, wx_ref, bx_ref, lam_ref,
             dx_ref, dz_ref, gwa_ref, gwx_ref, gba_ref, gbx_ref, glam_ref, gcw_ref, gcb_ref, next8, anext, dnext):
        i = pl.program_id(0)

        @pl.when(i == 0)
        def _():
            for ref in (next8, anext, dnext, gwa_ref, gwx_ref, gba_ref, gbx_ref, glam_ref, gcw_ref, gcb_ref):
                ref[...] = jnp.zeros_like(ref)

        inner = jnp.where(i < nt - 1, 1.0, 0.0)
        x = x_ref[...]
        halo = xhalo_ref[...] * inner
        xc = _conv(x, halo, cw_ref) + cb_ref[...]
        r, ig, sp, a, beta = _rg_gates(xc, wa_ref, ba_ref, wx_ref, bx_ref, lam_ref)
        h = h_ref[...]
        row = _iota(h.shape, 0)
        hprev = jnp.where(row >= 1, pltpu.roll(h, 1, 0), hhalo_ref[SUBLANES - 1:SUBLANES, :] * inner)
        z = z_ref[...]
        sz = _sigmoid(z)
        dyv = dy_ref[...]
        dz_ref[...] = (dyv * h * (sz + z * sz * (1.0 - sz))).astype(BF16)
        a_up = jnp.where(row < tile - 1, pltpu.roll(a, tile - 1, 0), anext[0:1, :])
        acum, bsum = _scan_rev(a_up, dyv * z * sz)
        delta = bsum + acum * dnext[0:1, :]
        anext[...] = a[0:SUBLANES, :]
        dnext[...] = delta[0:SUBLANES, :]
        dla = delta * hprev * a - delta * ig * xc * (a * a / beta)
        glam_ref[...] += _bcast8(_colsum(dla * r) * (RG_C * _sigmoid(-lam_ref[...])))
        dpa = dla * (-RG_C * sp) * r * (1.0 - r)
        dpx = delta * beta * xc * ig * (1.0 - ig)
        gba_ref[...] += _bcast8(_colsum(dpa))
        gbx_ref[...] += _bcast8(_colsum(dpx))
        parts = []
        for hh in range(nh):
            sl = slice(hh * dh, (hh + 1) * dh)
            gwa_ref[hh] += _mm_tn(xc[:, sl], dpa[:, sl])
            gwx_ref[hh] += _mm_tn(xc[:, sl], dpx[:, sl])
            parts.append(_mm_nt(dpa[:, sl], wa_ref[hh]) + _mm_nt(dpx[:, sl], wx_ref[hh]))
        dxc = delta * beta * ig + jnp.concatenate(parts, axis=1)
        gcb_ref[...] += _bcast8(_colsum(dxc))
        for kk in range(CONV_WIDTH):
            gcw_ref[kk:kk + 1, :] += _colsum(dxc * _shift_down(x, CONV_WIDTH - 1 - kk, halo))
        dx_ref[...] = _conv_bwd_x(dxc, next8[...], cw_ref).astype(BF16)
        next8[...] = dxc[0:SUBLANES, :]

    rev = lambda i: nt - 1 - i
    tok = pl.BlockSpec((tile, d), lambda i: (rev(i), 0))
    vec = _full((1, d))
    acc = _full((SUBLANES, d))
    wsh = (nh, dh, dh)
    return pl.pallas_call(
        body, name="rglru_bwd", grid=(nt,),
        in_specs=[tok, tok, _halo_spec(d, tile, nt, 0), pl.BlockSpec((tile, d), lambda i: (rev(i), 1)), tok,
                  _halo_spec(d, tile, nt, 0), _full(conv_w.shape), vec, _full(wsh), vec, _full(wsh), vec, vec],
        out_specs=[tok, tok, _full(wsh), _full(wsh), acc, acc, acc, acc, acc],
        out_shape=[jax.ShapeDtypeStruct((s, d), BF16)] * 2 + [jax.ShapeDtypeStruct(wsh, F32)] * 2
        + [jax.ShapeDtypeStruct((SUBLANES, d), F32)] * 5,
        scratch_shapes=[pltpu.VMEM((SUBLANES, d), F32)] * 3,
        compiler_params=_params(1),
    )(dy_rg, u, u, u, h_rg, h_rg, conv_w, conv_b, w_a, b_a, w_x, b_x, lam)


def _segments(d, w, n_pieces, n_slots):
    bounds = sorted({k * d for k in range(n_pieces + 1)} | {j * w for j in range(n_slots + 1)})
    return [(lo // d, lo % d, lo // w, lo % w, hi - lo) for lo, hi in zip(bounds[:-1], bounds[1:])]


def _in_bwd(pieces, x, dxo, ng, scale, w_in_g, layer, tile):
    s, d = x.shape
    nd, _, _, w = w_in_g.shape
    segs = _segments(d, w, len(pieces), nd)

    def body(*refs):
        p_refs = refs[:len(pieces)]
        x_ref, dxo_ref, ng_ref, sc_ref, w_ref, dx_ref, dsc_ref, dsh_ref, gng_ref = refs[len(pieces):]

        @pl.when(pl.program_id(0) == 0)
        def _():
            for ref in (dsc_ref, dsh_ref, gng_ref):
                ref[...] = jnp.zeros_like(ref)

        dh = jnp.zeros((tile, d), F32)
        for (kk, a, j, b, width) in segs:
            dh = dh + _mm_nt(p_refs[kk][:, a:a + width], w_ref[j, 0, :, b:b + width])
        xv = x_ref[...]
        g = ng_ref[...]
        rs = lax.rsqrt(jnp.mean(xv * xv, axis=1, keepdims=True) + EPS)
        xh = xv * rs
        dsh_ref[...] += _bcast8(_colsum(dh))
        dsc_ref[...] += _bcast8(_colsum(dh * xh * g))
        dhn = dh * (1.0 + sc_ref[...])
        gng_ref[...] += _bcast8(_colsum(dhn * xh))
        dxh = dhn * g
        dx_ref[...] = dxo_ref[...] + rs * (dxh - xh * jnp.mean(dxh * xh, axis=1, keepdims=True))

    tok = pl.BlockSpec((tile, d), lambda i: (i, 0))
    vec = _full((1, d))
    acc = _full((SUBLANES, d))
    return pl.pallas_call(
        body, name="in_proj_bwd_x", grid=(s // tile,),
        in_specs=[tok] * len(pieces) + [tok, tok, vec, vec, pl.BlockSpec((nd, 1, d, w), lambda i: (0, layer, 0, 0))],
        out_specs=[tok, acc, acc, acc],
        out_shape=[jax.ShapeDtypeStruct((s, d), F32)] + [jax.ShapeDtypeStruct((SUBLANES, d), F32)] * 3,
        compiler_params=_params(1),
    )(*pieces, x, dxo, ng, scale, w_in_g)


def _in_bwd_w(pieces, hbf, w, slots, tile):
    s, d = hbf.shape
    nd_all = len(pieces) * d // w
    segs = [sg for sg in _segments(d, w, len(pieces), nd_all) if sg[2] in slots]

    def body(*refs):
        p_refs = refs[:len(pieces)]
        h_ref, gw_ref = refs[len(pieces):]

        @pl.when(pl.program_id(0) == 0)
        def _():
            gw_ref[...] = jnp.zeros_like(gw_ref)

        hv = h_ref[...]
        for (kk, a, j, b, width) in segs:
            gw_ref[j - slots[0], :, b:b + width] += _mm_tn(hv, p_refs[kk][:, a:a + width])

    tok = pl.BlockSpec((tile, d), lambda i: (i, 0))
    return pl.pallas_call(
        body, name="in_proj_bwd_w", grid=(s // tile,),
        in_specs=[tok] * len(pieces) + [tok],
        out_specs=_full((len(slots), d, w)),
        out_shape=jax.ShapeDtypeStruct((len(slots), d, w), F32),
        compiler_params=_params(1),
    )(*pieces, hbf)


def _ada_bwd_w(cact_col, dmod, nd):
    d = cact_col.shape[0]
    w = dmod.shape[1] // nd

    def body(c_ref, m_ref, o_ref):
        o_ref[0] = c_ref[...] * m_ref[...]

    return pl.pallas_call(
        body, name="adaln_bwd_w", grid=(nd,),
        in_specs=[_full((d, 1)), pl.BlockSpec((1, w), lambda j: (0, j))],
        out_specs=pl.BlockSpec((1, d, w), lambda j: (j, 0, 0)),
        out_shape=jax.ShapeDtypeStruct((nd, d, w), F32),
        compiler_params=_params(1),
    )(cact_col, dmod)


def _exchange(arrs, gather, name):
    n = len(arrs)
    outs_shape = [jax.ShapeDtypeStruct((N_DEV,) + a.shape if gather else a.shape, a.dtype) for a in arrs]

    def body(*refs):
        ins, outs = refs[:n], refs[n:2 * n]
        send_sems, recv_sems, local_sems = refs[2 * n:]
        x, y, c = (lax.axis_index(ax) for ax in MESH_AXES)
        me = 4 * x + 2 * y + c
        sends, recvs = [], []
        for flip in range(1, N_DEV):
            px = x ^ ((flip >> 2) & 1)
            py = y ^ ((flip >> 1) & 1)
            pc = c ^ (flip & 1)
            peer = 4 * px + 2 * py + pc
            for kk in range(n):
                sem = kk * (N_DEV - 1) + flip - 1
                src = ins[kk] if gather else ins[kk].at[peer]
                sends.append(pltpu.make_async_remote_copy(
                    src_ref=src, dst_ref=outs[kk].at[me], send_sem=send_sems.at[sem], recv_sem=recv_sems.at[sem],
                    device_id=(px, py, pc), device_id_type=pl.DeviceIdType.MESH))
                recvs.append(pltpu.make_async_remote_copy(
                    src_ref=src, dst_ref=outs[kk].at[peer], send_sem=send_sems.at[sem], recv_sem=recv_sems.at[sem],
                    device_id=(px, py, pc), device_id_type=pl.DeviceIdType.MESH))
        for cp in sends:
            cp.start()
        local = [pltpu.make_async_copy(ins[kk] if gather else ins[kk].at[me], outs[kk].at[me], local_sems.at[kk])
                 for kk in range(n)]
        for cp in local:
            cp.start()
        for cp in recvs:
            cp.wait_recv()
        for cp in sends:
            cp.wait_send()
        for cp in local:
            cp.wait()

    return pl.pallas_call(
        body, name=name,
        in_specs=[pl.BlockSpec(memory_space=pl.ANY)] * n,
        out_specs=[pl.BlockSpec(memory_space=pl.ANY)] * n,
        out_shape=outs_shape,
        scratch_shapes=[pltpu.SemaphoreType.DMA((n * (N_DEV - 1),)), pltpu.SemaphoreType.DMA((n * (N_DEV - 1),)),
                        pltpu.SemaphoreType.DMA((n,))],
    )(*arrs)


def _adam_math(w, g, m, v):
    m = ADAM_B1 * m + (1.0 - ADAM_B1) * g
    v = ADAM_B2 * v + (1.0 - ADAM_B2) * (g * g)
    m_hat = m / (1.0 - ADAM_B1 ** ADAM_STEP)
    v_hat = v / (1.0 - ADAM_B2 ** ADAM_STEP)
    delta = -ADAM_LR * (m_hat / (jnp.sqrt(v_hat) + ADAM_EPS) + ADAM_WD * w)
    return delta, m, v


def _sum_devices(r_ref):
    acc = r_ref[0]
    for p in range(1, r_ref.shape[0]):
        acc = acc + r_ref[p]
    return acc


def _row_tile(rows, cols, n_bufs):
    budget = 24 * 1024 * 1024 // (n_bufs * 2 * cols * 4)
    t = rows
    while t > budget and t % 2 == 0 and (t // 2) % SUBLANES == 0:
        t //= 2
    return t


def _reduce_adam(recvs, w, m, v, name):
    nl, r, c = w.shape
    tr = _row_tile(r, c, N_DEV * nl + 7)
    nt = r // tr

    def body(*refs):
        r_refs = refs[:nl]
        w_ref, m_ref, v_ref, g_ref, d_ref, mo_ref, vo_ref = refs[nl:]
        layer = pl.program_id(0)
        g = _sum_devices(r_refs[0])
        for ll in range(1, nl):
            g = jnp.where(layer == ll, _sum_devices(r_refs[ll]), g)
        delta, m2, v2 = _adam_math(w_ref[0], g, m_ref[0], v_ref[0])
        g_ref[0] = g
        d_ref[0] = delta
        mo_ref[0] = m2
        vo_ref[0] = v2

    def rspec(ll):
        return pl.BlockSpec((N_DEV, tr, c), lambda l, i: (0, jnp.where(l == ll, i, jnp.where(l < ll, 0, nt - 1)), 0))

    blk = pl.BlockSpec((1, tr, c), lambda l, i: (l, i, 0))
    return pl.pallas_call(
        body, name=name, grid=(nl, nt),
        in_specs=[rspec(ll) for ll in range(nl)] + [blk, blk, blk],
        out_specs=[blk] * 4,
        out_shape=[jax.ShapeDtypeStruct((nl, r, c), F32)] * 4,
        compiler_params=_params(2),
    )(*recvs, w, m, v)


def _sum8(recv, name):
    _, r, c = recv.shape

    def body(r_ref, o_ref):
        o_ref[...] = _sum_devices(r_ref)

    return pl.pallas_call(
        body, name=name, grid=(1,),
        in_specs=[_full(recv.shape)], out_specs=_full((r, c)),
        out_shape=jax.ShapeDtypeStruct((r, c), F32), compiler_params=_params(1),
    )(recv)


def _adam_call(w, g, m, v, name):
    r, c = w.shape

    def body(w_ref, g_ref, m_ref, v_ref, d_ref, mo_ref, vo_ref):
        delta, m2, v2 = _adam_math(w_ref[...], g_ref[...], m_ref[...], v_ref[...])
        d_ref[...] = delta
        mo_ref[...] = m2
        vo_ref[...] = v2

    return pl.pallas_call(
        body, name=name, grid=(1,),
        in_specs=[_full((r, c))] * 4, out_specs=[_full((r, c))] * 3,
        out_shape=[jax.ShapeDtypeStruct((r, c), F32)] * 3, compiler_params=_params(1),
    )(w, g, m, v)


def _tile_for(s, want):
    return min(want, s)


def _local_step(x, c, target, wts):
    s, d = x.shape
    nl = wts["w_in_g"].shape[1]
    nd = wts["w_in_g"].shape[0]
    nh_ml = wts["w_qkv"].shape[2]
    t_big = _tile_for(s, 512)
    t_mid = _tile_for(s, 256)

    mod, cact = _mod_call(c, wts["w_ada_g"], wts["b_ada"])
    row = lambda a: a.reshape(1, -1)
    saved = []
    xl = x
    for l in range(nl):
        shift, scale, gate = (row(mod[l, kk * d:(kk + 1) * d]) for kk in range(3))
        u, hbf = _in_fwd(xl, row(wts["norm_g"][l]), scale, shift, wts["w_in_g"], l, t_mid)
        h_rg, y_rg = _rg_fwd(u, d, wts["rg_conv_w"][l], row(wts["rg_conv_b"][l]), wts["rg_w_a_bf"][l],
                             row(wts["rg_b_a"][l]), wts["rg_w_x_bf"][l], row(wts["rg_b_x"][l]),
                             row(wts["rg_lambda"][l]), t_mid)
        q, k, v, gcol = _ml_pre(u, d, wts["ml_conv_w"][l], row(wts["ml_conv_b"][l]), wts["w_qkv"][l, 0],
                                wts["w_qkv"][l, 1], wts["w_qkv"][l, 2], wts["wif_pad"][l], wts["bif_pad"][l], t_mid)
        grow = gcol[:, 0:16].T
        cell, y_ml, cs, ns, ms, mt = _ml_cell_fwd(q, k, v, gcol, grow, u, row(wts["ml_norm_g"][l]), nh_ml)
        x_new, y = _out_fwd(xl, y_rg, y_ml, gate, wts["w_out_g"], l, t_big)
        saved.append(dict(x=xl, u=u, hbf=hbf, h_rg=h_rg, y_rg=y_rg, q=q, k=k, v=v, gcol=gcol, grow=grow, cell=cell,
                          y_ml=y_ml, cs=cs, ns=ns, ms=ms, mt=mt, y=y, scale=scale, gate=gate))
        xl = x_new

    dx, loss_p, g_final = _loss_call(xl, row(wts["final_g"]), target, t_big)
    grads = [None] * nl
    cact_col = cact[0].reshape(d, 1)
    for l in reversed(range(nl)):
        sv = saved[l]
        dy_rg, dy_ml, gw_out, dgate = _out_bwd(dx, sv["gate"], sv["y"], sv["y_rg"], sv["y_ml"], wts["w_out_g"], l, t_big)
        dq, dk, dv, dgates, d_mlo, d_mlz, g_mlng = _ml_cell_bwd(
            dy_ml, sv["u"], sv["cell"], sv["q"], sv["k"], sv["v"], sv["gcol"], sv["grow"], sv["mt"], sv["cs"],
            sv["ns"], sv["ms"], row(wts["ml_norm_g"][l]), nh_ml)
        d_mlx, g_wq, g_wk, g_wv, g_wift, g_bif, g_mlcw, g_mlcb = _ml_pre_bwd(
            dq, dk, dv, dgates, sv["gcol"], sv["u"], sv["q"], sv["k"], sv["v"], wts["ml_conv_w"][l],
            row(wts["ml_conv_b"][l]), wts["w_qkv"][l, 0], wts["w_qkv"][l, 1], wts["w_qkv"][l, 2], wts["wift_pad"][l], t_mid)
        d_rgx, d_rgz, g_wa, g_wx, g_ba, g_bx, g_lam, g_rgcw, g_rgcb = _rg_bwd(
            dy_rg, sv["u"], sv["h_rg"], wts["rg_conv_w"][l], row(wts["rg_conv_b"][l]), wts["rg_w_a_bf"][l],
            row(wts["rg_b_a"][l]), wts["rg_w_x_bf"][l], row(wts["rg_b_x"][l]), row(wts["rg_lambda"][l]), t_mid)
        pieces = [d_rgx, d_rgz, d_mlx, d_mlo, d_mlz]
        dx, dscale, dshift, g_ng = _in_bwd(pieces, sv["x"], dx, row(wts["norm_g"][l]), sv["scale"], wts["w_in_g"], l, t_mid)
        half = nd // 2
        w_cols = wts["w_in_g"].shape[3]
        gw_in = jnp.concatenate([_in_bwd_w(pieces, sv["hbf"], w_cols, tuple(range(0, half)), t_big),
                                 _in_bwd_w(pieces, sv["hbf"], w_cols, tuple(range(half, nd)), t_big)], axis=0)
        dmod = jnp.concatenate([dshift[0:1], dscale[0:1], dgate[0:1]], axis=1)
        gw_ada = _ada_bwd_w(cact_col, dmod, nd)
        grads[l] = dict(w_ada=gw_ada, w_in=gw_in, w_out=gw_out, w_qkv=jnp.stack([g_wq, g_wk, g_wv]),
                        rg_conv_w=g_rgcw[0:CONV_WIDTH], ml_conv_w=g_mlcw[0:CONV_WIDTH], wif_t=g_wift[0:8],
                        norm_g=g_ng[0], b_ada=dmod[0], rg_conv_b=g_rgcb[0], rg_w_a=g_wa, rg_b_a=g_ba[0], rg_w_x=g_wx,
                        rg_b_x=g_bx[0], rg_lambda=g_lam[0], ml_conv_b=g_mlcb[0], ml_b_if=g_bif[0, 0:8],
                        ml_norm_g=g_mlng[0])
    return loss_p[0, 0], dx, grads, g_final[0]


REPLICATED = ("norm_g", "b_ada", "rg_conv_b", "rg_w_a", "rg_b_a", "rg_w_x", "rg_b_x", "rg_lambda", "ml_conv_b",
              "ml_b_if", "ml_norm_g", "final_g")
ROW_ALIGN = N_DEV * SUBLANES


def _to_rows(a):
    flat = a.reshape(-1)
    pad = (-flat.shape[0]) % LANES
    return jnp.pad(flat, (0, pad)).reshape(-1, LANES)


def _pack(arrays):
    rows = jnp.concatenate([_to_rows(a) for a in arrays], axis=0)
    return jnp.pad(rows, ((0, (-rows.shape[0]) % ROW_ALIGN), (0, 0)))


def _unpack(rows, like):
    out, at = [], 0
    for a in like:
        n = -(-a.size // LANES)
        out.append(rows[at:at + n].reshape(-1)[:a.size].reshape(a.shape))
        at += n
    return out


def _small_pack(rg_conv_w, ml_conv_w, ml_w_if):
    nl = rg_conv_w.shape[0]
    wif_t = jnp.swapaxes(ml_w_if, 1, 2).reshape(nl, -1, LANES)
    return jnp.concatenate([rg_conv_w, ml_conv_w, wif_t], axis=1)


def _small_unpack(p, if_rows):
    nl = p.shape[0]
    rg_cw = p[:, 0:CONV_WIDTH]
    ml_cw = p[:, CONV_WIDTH:2 * CONV_WIDTH]
    wif = jnp.swapaxes(p[:, 2 * CONV_WIDTH:].reshape(nl, 8, if_rows), 1, 2)
    return rg_cw, ml_cw, wif


def _assemble_weights(big, small, rep):
    w_ada_g, w_in_g, w_out_g, qkv_g = big
    nd, nl = small.shape[0], small.shape[1]
    d = w_in_g.shape[2]
    dh = qkv_g.shape[3]
    nh = d // dh
    rsh = qkv_g.shape[2] // (3 * nh)
    w_qkv = qkv_g.reshape(nd, nl, 3, nh, rsh, dh).transpose(1, 2, 3, 0, 4, 5).reshape(nl, 3, nh, nd * rsh, dh)
    cw = small[:, :, 0:2 * CONV_WIDTH].reshape(nd, nl, 2, CONV_WIDTH, LANES).transpose(1, 2, 3, 0, 4)
    cw = cw.reshape(nl, 2, CONV_WIDTH, nd * LANES)
    if_rows = (small.shape[2] - 2 * CONV_WIDTH) * LANES // 8
    wif_t = small[:, :, 2 * CONV_WIDTH:].reshape(nd, nl, 8, if_rows).transpose(1, 2, 0, 3).reshape(nl, 8, nd * if_rows)
    wift_pad = jnp.pad(wif_t, ((0, 0), (0, LANES - 8), (0, 0))).astype(BF16)
    wif_pad = jnp.swapaxes(wift_pad, 1, 2)
    bif_pad = jnp.pad(rep["ml_b_if"], ((0, 0), (0, LANES - 8))).reshape(nl, 1, LANES)
    wts = dict(rep)
    wts.update(w_ada_g=w_ada_g, w_in_g=w_in_g, w_out_g=w_out_g, w_qkv=w_qkv, rg_conv_w=cw[:, 0], ml_conv_w=cw[:, 1],
               wif_pad=wif_pad, wift_pad=wift_pad, bif_pad=bif_pad, rg_w_a_bf=rep["rg_w_a"].astype(BF16),
               rg_w_x_bf=rep["rg_w_x"].astype(BF16))
    return wts


def _qkv_slots(g_qkv, nd):
    three, nh, dh, _ = g_qkv.shape
    return g_qkv.reshape(three, nh, nd, dh // nd, dh).transpose(2, 0, 1, 3, 4).reshape(nd, three * nh * (dh // nd), dh)


def _small_slots(g):
    nd = N_DEV
    cw = jnp.stack([g["rg_conv_w"], g["ml_conv_w"]]).reshape(2, CONV_WIDTH, nd, LANES).transpose(2, 0, 1, 3)
    cw = cw.reshape(nd, 2 * CONV_WIDTH, LANES)
    wif = g["wif_t"].reshape(8, nd, -1).transpose(1, 0, 2).reshape(nd, -1, LANES)
    return jnp.concatenate([cw, wif], axis=1)


def kernel(x, c, norm_g, w_ada, b_ada, w_in, rg_conv_w, rg_conv_b, rg_w_a, rg_b_a, rg_w_x, rg_b_x, rg_lambda, ml_conv_w, ml_conv_b, ml_w_q, ml_w_k, ml_w_v, ml_w_if, ml_b_if, ml_norm_g, w_out, final_g, loss_target, m_norm_g, m_w_ada, m_b_ada, m_w_in, m_rg_conv_w, m_rg_conv_b, m_rg_w_a, m_rg_b_a, m_rg_w_x, m_rg_b_x, m_rg_lambda, m_ml_conv_w, m_ml_conv_b, m_ml_w_q, m_ml_w_k, m_ml_w_v, m_ml_w_if, m_ml_b_if, m_ml_norm_g, m_w_out, m_final_g, v_norm_g, v_w_ada, v_b_ada, v_w_in, v_rg_conv_w, v_rg_conv_b, v_rg_w_a, v_rg_b_a, v_rg_w_x, v_rg_b_x, v_rg_lambda, v_ml_conv_w, v_ml_conv_b, v_ml_w_q, v_ml_w_k, v_ml_w_v, v_ml_w_if, v_ml_b_if, v_ml_norm_g, v_w_out, v_final_g):
    given = dict(locals())
    nl = w_in.shape[0]
    rep = {n: given[n] for n in REPLICATED}

    def qkv_shard(prefix):
        return jnp.stack([given[prefix + "ml_w_q"], given[prefix + "ml_w_k"], given[prefix + "ml_w_v"]], axis=1).reshape(
            nl, -1, ml_w_q.shape[-1])

    big = _exchange([w_ada.astype(BF16), w_in.astype(BF16), w_out.astype(BF16), qkv_shard("").astype(BF16)],
                    True, "gather_weights")
    small = _exchange([_small_pack(rg_conv_w, ml_conv_w, ml_w_if)], True, "gather_small_weights")[0]
    wts = _assemble_weights(big, small, rep)

    loss_p, grad_x, grads, g_final = _local_step(x[0], c, loss_target[0], wts)
    loss = lax.psum(loss_p, MESH_AXES)

    keys = ("w_ada", "w_in", "w_out", "w_qkv", "small")
    recv = []
    for l in range(nl):
        g = grads[l]
        recv.append(_exchange([g["w_ada"], g["w_in"], g["w_out"], _qkv_slots(g["w_qkv"], N_DEV), _small_slots(g)],
                              False, "reduce_scatter_layer%d" % l))
    shard = {"": dict(w_ada=w_ada, w_in=w_in, w_out=w_out, w_qkv=qkv_shard(""),
                      small=_small_pack(rg_conv_w, ml_conv_w, ml_w_if))}
    for p in ("m_", "v_"):
        shard[p] = dict(w_ada=given[p + "w_ada"], w_in=given[p + "w_in"], w_out=given[p + "w_out"], w_qkv=qkv_shard(p),
                        small=_small_pack(given[p + "rg_conv_w"], given[p + "ml_conv_w"], given[p + "ml_w_if"]))
    res = {}
    for ki, key in enumerate(keys):
        res[key] = _reduce_adam([recv[l][ki] for l in range(nl)], shard[""][key], shard["m_"][key], shard["v_"][key],
                                "reduce_adam_" + key)

    rep_g = dict(final_g=g_final)
    for n in REPLICATED[:-1]:
        rep_g[n] = jnp.stack([grads[l][n] for l in range(nl)])
    pack_g = _pack([rep_g[n] for n in REPLICATED])
    rows = pack_g.shape[0] // N_DEV
    mine = _sum8(_exchange([pack_g.reshape(N_DEV, rows, LANES)], False, "reduce_scatter_replicated")[0], "sum_replicated")
    g_rep = _exchange([mine], True, "gather_replicated")[0].reshape(N_DEV * rows, LANES)
    rep_like = [rep[n] for n in REPLICATED]
    d_rep, m_rep, v_rep = _adam_call(_pack(rep_like), g_rep, _pack([given["m_" + n] for n in REPLICATED]),
                                     _pack([given["v_" + n] for n in REPLICATED]), "adam_replicated")
    rep_out = [dict(zip(REPLICATED, _unpack(a, rep_like))) for a in (g_rep, d_rep, m_rep, v_rep)]

    if_rows = ml_w_if.shape[1]
    order = ("norm_g", "w_ada", "b_ada", "w_in", "rg_conv_w", "rg_conv_b", "rg_w_a", "rg_b_a", "rg_w_x", "rg_b_x",
             "rg_lambda", "ml_conv_w", "ml_conv_b", "ml_w_q", "ml_w_k", "ml_w_v", "ml_w_if", "ml_b_if", "ml_norm_g",
             "w_out", "final_g")
    outs = [loss, grad_x[None]]
    for kind in range(4):
        qkv = res["w_qkv"][kind].reshape((nl, 3) + ml_w_q.shape[1:])
        rg_cw, ml_cw, wif = _small_unpack(res["small"][kind], if_rows)
        sharded = dict(w_ada=res["w_ada"][kind], w_in=res["w_in"][kind], w_out=res["w_out"][kind], ml_w_q=qkv[:, 0],
                       ml_w_k=qkv[:, 1], ml_w_v=qkv[:, 2], rg_conv_w=rg_cw, ml_conv_w=ml_cw, ml_w_if=wif)
        for n in order:
            outs.append(sharded[n] if n in sharded else rep_out[kind][n])
    return tuple(outs)
```

```python
import functools

import jax
import jax.numpy as jnp
from jax import lax
from jax.experimental import pallas as pl
from jax.experimental.pallas import tpu as pltpu

F32 = jnp.float32
BF16 = jnp.bfloat16
MESH_AXES = ("x", "y", "c")
N_DEV = 8
EPS = 1e-6
RG_C = 8.0
ML_CHUNK = 128
CONV_WIDTH = 4
ADAM_LR = 0.001
ADAM_B1 = 0.9
ADAM_B2 = 0.999
ADAM_EPS = 1e-08
ADAM_WD = 0.01
ADAM_STEP = 10
NEG_BIG = -1e30
LANES = 128
SUBLANES = 8
VMEM_LIMIT = 56 * 1024 * 1024
HI = lax.Precision.HIGHEST


def _params(n_grid):
    return pltpu.CompilerParams(dimension_semantics=("arbitrary",) * n_grid, vmem_limit_bytes=VMEM_LIMIT)


def _mm(a, b):
    return jnp.dot(a.astype(BF16), b.astype(BF16), preferred_element_type=F32)


def _mm_nt(a, b):
    return lax.dot_general(a.astype(BF16), b.astype(BF16), (((1,), (1,)), ((), ())), preferred_element_type=F32)


def _mm_tn(a, b):
    return lax.dot_general(a.astype(BF16), b.astype(BF16), (((0,), (0,)), ((), ())), preferred_element_type=F32)


def _mm_hi(a, b):
    return jnp.dot(a, b, precision=HI, preferred_element_type=F32)


def _sigmoid(x):
    return 1.0 / (1.0 + jnp.exp(-x))


def _softplus(x):
    return jnp.maximum(x, 0.0) + jnp.log(1.0 + jnp.exp(-jnp.abs(x)))


def _neg_expm1(x):
    poly = -x * (1.0 + x * (0.5 + x * (1.0 / 6.0 + x * (1.0 / 24.0 + x * (1.0 / 120.0)))))
    return jnp.where(jnp.abs(x) < 0.05, poly, 1.0 - jnp.exp(x))


def _iota(shape, dim):
    return lax.broadcasted_iota(jnp.int32, shape, dim)


def _colsum(x):
    return jnp.sum(x, axis=0, keepdims=True)


def _rowsum(x):
    return jnp.sum(x, axis=1, keepdims=True)


def _col(x, j):
    return _rowsum(jnp.where(_iota(x.shape, 1) == j, x, 0.0))


def _row(x, j):
    return _colsum(jnp.where(_iota(x.shape, 0) == j, x, 0.0))


def _shift_down(x, j, prev8):
    if j == 0:
        return x
    t = x.shape[0]
    main = jnp.where(_iota(x.shape, 0) >= j, pltpu.roll(x, j, 0), 0.0)
    fix = jnp.where(_iota(prev8.shape, 0) < j, pltpu.roll(prev8, j, 0), 0.0)
    return jnp.concatenate([main[0:SUBLANES] + fix, main[SUBLANES:t]], axis=0)


def _shift_up(x, j, next8):
    if j == 0:
        return x
    t = x.shape[0]
    main = jnp.where(_iota(x.shape, 0) < t - j, pltpu.roll(x, t - j, 0), 0.0)
    fix = jnp.where(_iota(next8.shape, 0) >= SUBLANES - j, pltpu.roll(next8, SUBLANES - j, 0), 0.0)
    return jnp.concatenate([main[0:t - SUBLANES], main[t - SUBLANES:t] + fix], axis=0)


def _conv(x, prev8, w_ref):
    y = w_ref[CONV_WIDTH - 1:CONV_WIDTH, :] * x
    for j in range(1, CONV_WIDTH):
        y = y + w_ref[CONV_WIDTH - 1 - j:CONV_WIDTH - j, :] * _shift_down(x, j, prev8)
    return y


def _conv_bwd_x(dy, next8, w_ref):
    dx = w_ref[CONV_WIDTH - 1:CONV_WIDTH, :] * dy
    for j in range(1, CONV_WIDTH):
        dx = dx + w_ref[CONV_WIDTH - 1 - j:CONV_WIDTH - j, :] * _shift_up(dy, j, next8)
    return dx


def _scan_fwd(a, b):
    t = a.shape[0]
    row = _iota(a.shape, 0)
    d = 1
    while d < t:
        keep = row >= d
        a_s = jnp.where(keep, pltpu.roll(a, d, 0), 1.0)
        b_s = jnp.where(keep, pltpu.roll(b, d, 0), 0.0)
        b = a * b_s + b
        a = a * a_s
        d *= 2
    return a, b


def _scan_rev(a, b):
    t = a.shape[0]
    row = _iota(a.shape, 0)
    d = 1
    while d < t:
        keep = row < t - d
        a_s = jnp.where(keep, pltpu.roll(a, t - d, 0), 1.0)
        b_s = jnp.where(keep, pltpu.roll(b, t - d, 0), 0.0)
        b = a * b_s + b
        a = a * a_s
        d *= 2
    return a, b


def _blockdiag(x, w_ref, transpose_w=False):
    nh, dh, _ = w_ref.shape
    outs = []
    for h in range(nh):
        xs = x[:, h * dh:(h + 1) * dh]
        outs.append(_mm_nt(xs, w_ref[h]) if transpose_w else _mm(xs, w_ref[h]))
    return jnp.concatenate(outs, axis=1)


def _rg_gates(xc, wa_ref, ba_ref, wx_ref, bx_ref, lam_ref):
    r = _sigmoid(_blockdiag(xc, wa_ref) + ba_ref[...])
    ig = _sigmoid(_blockdiag(xc, wx_ref) + bx_ref[...])
    sp = _softplus(-lam_ref[...])
    log_a = -RG_C * r * sp
    a = jnp.exp(log_a)
    beta = jnp.sqrt(_neg_expm1(2.0 * log_a))
    return r, ig, sp, a, beta


def _bcast8(row):
    return jnp.broadcast_to(row, (SUBLANES, row.shape[1]))


def _full(shape):
    nd = len(shape)
    return pl.BlockSpec(shape, lambda *_: (0,) * nd)


def _mod_call(c, w_ada_g, b_ada):
    nd, nl, d, w = w_ada_g.shape

    def body(c_ref, w_ref, b_ref, mod_ref, cact_ref):
        cv = c_ref[...]
        ca = _bcast8(cv * _sigmoid(cv))
        cact_ref[...] = ca
        mod_ref[0, 0] = _mm(ca, w_ref[0, 0]) + b_ref[0, 0]

    mod, cact = pl.pallas_call(
        body, name="adaln_mod", grid=(nl, nd),
        in_specs=[_full((1, d)),
                  pl.BlockSpec((1, 1, d, w), lambda l, j: (j, l, 0, 0)),
                  pl.BlockSpec((1, 1, 1, w), lambda l, j: (l, j, 0, 0))],
        out_specs=[pl.BlockSpec((1, 1, SUBLANES, w), lambda l, j: (l, j, 0, 0)), _full((SUBLANES, d))],
        out_shape=[jax.ShapeDtypeStruct((nl, nd, SUBLANES, w), F32), jax.ShapeDtypeStruct((SUBLANES, d), F32)],
        compiler_params=_params(2),
    )(c, w_ada_g, b_ada.reshape(nl, nd, 1, w))
    return mod[:, :, 0, :].reshape(nl, nd * w), cact


def _in_fwd(x, ng, scale, shift, w_in_g, layer, tile):
    s, d = x.shape
    nd, _, _, w = w_in_g.shape

    def body(x_ref, ng_ref, sc_ref, sh_ref, w_ref, u_ref, h_ref):
        xv = x_ref[...]
        rs = lax.rsqrt(jnp.mean(xv * xv, axis=1, keepdims=True) + EPS)
        hb = (xv * rs * ng_ref[...] * (1.0 + sc_ref[...]) + sh_ref[...]).astype(BF16)
        h_ref[...] = hb
        for j in range(nd):
            u_ref[:, j * w:(j + 1) * w] = jnp.dot(hb, w_ref[j, 0], preferred_element_type=F32)

    return pl.pallas_call(
        body, name="in_proj_fwd", grid=(s // tile,),
        in_specs=[pl.BlockSpec((tile, d), lambda i: (i, 0)), _full((1, d)), _full((1, d)), _full((1, d)),
                  pl.BlockSpec((nd, 1, d, w), lambda i: (0, layer, 0, 0))],
        out_specs=[pl.BlockSpec((tile, nd * w), lambda i: (i, 0)), pl.BlockSpec((tile, d), lambda i: (i, 0))],
        out_shape=[jax.ShapeDtypeStruct((s, nd * w), F32), jax.ShapeDtypeStruct((s, d), BF16)],
        compiler_params=_params(1),
    )(x, ng, scale, shift, w_in_g)


def _rg_fwd(u, d, conv_w, conv_b, w_a, b_a, w_x, b_x, lam, tile):
    s = u.shape[0]

    def body(x_ref, z_ref, cw_ref, cb_ref, wa_ref, ba_ref, wx_ref, bx_ref, lam_ref, h_ref, y_ref, prev8, hcar):
        @pl.when(pl.program_id(0) == 0)
        def _():
            prev8[...] = jnp.zeros_like(prev8)
            hcar[...] = jnp.zeros_like(hcar)

        x = x_ref[...]
        xc = _conv(x, prev8[...], cw_ref) + cb_ref[...]
        prev8[...] = x[tile - SUBLANES:tile, :]
        _, ig, _, a, beta = _rg_gates(xc, wa_ref, ba_ref, wx_ref, bx_ref, lam_ref)
        acum, bsum = _scan_fwd(a, beta * ig * xc)
        h = bsum + acum * hcar[SUBLANES - 1:SUBLANES, :]
        hcar[...] = h[tile - SUBLANES:tile, :]
        h_ref[...] = h
        z = z_ref[...]
        y_ref[...] = h * z * _sigmoid(z)

    vec = _full((1, d))
    return pl.pallas_call(
        body, name="rglru_fwd", grid=(s // tile,),
        in_specs=[pl.BlockSpec((tile, d), lambda i: (i, 0)), pl.BlockSpec((tile, d), lambda i: (i, 1)),
                  _full(conv_w.shape), vec, _full(w_a.shape), vec, _full(w_x.shape), vec, vec],
        out_specs=[pl.BlockSpec((tile, d), lambda i: (i, 0))] * 2,
        out_shape=[jax.ShapeDtypeStruct((s, d), F32)] * 2,
        scratch_shapes=[pltpu.VMEM((SUBLANES, d), F32), pltpu.VMEM((SUBLANES, d), F32)],
        compiler_params=_params(1),
    )(u, u, conv_w, conv_b, w_a, b_a, w_x, b_x, lam)


def _ml_pre(u, d, conv_w, conv_b, w_q, w_k, w_v, wif, bif, tile):
    s = u.shape[0]
    nh = w_q.shape[0]

    def body(x_ref, cw_ref, cb_ref, wq_ref, wk_ref, wv_ref, wif_ref, bif_ref, q_ref, k_ref, v_ref, g_ref, prev8):
        @pl.when(pl.program_id(0) == 0)
        def _():
            prev8[...] = jnp.zeros_like(prev8)

        x = x_ref[...]
        pre = _conv(x, prev8[...], cw_ref) + cb_ref[...]
        prev8[...] = x[tile - SUBLANES:tile, :]
        xc = pre * _sigmoid(pre)
        q = _blockdiag(xc, wq_ref)
        k = _blockdiag(xc, wk_ref)
        v = _blockdiag(x, wv_ref)
        q_ref[...] = q
        k_ref[...] = k
        v_ref[...] = v
        g = _mm(q, wif_ref[0:d, :]) + _mm(k, wif_ref[d:2 * d, :]) + _mm(v, wif_ref[2 * d:3 * d, :]) + bif_ref[...]
        lane = _iota(g.shape, 1)
        gl = jnp.where(lane < 4, g, jnp.where(lane < 8, -_softplus(-g), 0.0))
        tri = jnp.where(_iota((ML_CHUNK, ML_CHUNK), 1) <= _iota((ML_CHUNK, ML_CHUNK), 0), 1.0, 0.0)
        cums = [_mm_hi(tri, gl[c * ML_CHUNK:(c + 1) * ML_CHUNK, :]) for c in range(tile // ML_CHUNK)]
        cum = cums[0] if len(cums) == 1 else jnp.concatenate(cums, axis=0)
        g_ref[...] = gl + jnp.where((lane >= 8) & (lane < 12), pltpu.roll(cum, 4, 1), 0.0)

    vec = _full((1, d))
    return pl.pallas_call(
        body, name="mlstm_proj_fwd", grid=(s // tile,),
        in_specs=[pl.BlockSpec((tile, d), lambda i: (i, 2)), _full(conv_w.shape), vec,
                  _full(w_q.shape), _full(w_k.shape), _full(w_v.shape), _full(wif.shape), _full((1, LANES))],
        out_specs=[pl.BlockSpec((tile, d), lambda i: (i, 0))] * 3 + [pl.BlockSpec((tile, LANES), lambda i: (i, 0))],
        out_shape=[jax.ShapeDtypeStruct((s, d), F32)] * 3 + [jax.ShapeDtypeStruct((s, LANES), F32)],
        scratch_shapes=[pltpu.VMEM((SUBLANES, d), F32)],
        compiler_params=_params(1),
    )(u, conv_w, conv_b, w_q, w_k, w_v, wif, bif)


def _cell_chunk(h, nh, q_ref, k_ref, v_ref, gc, gr, m_prev, c_h, n_h, m_t=None):
    lc = ML_CHUNK
    dh = q_ref.shape[1] // nh
    sl = slice(h * dh, (h + 1) * dh)
    qh = q_ref[:, sl]
    kh = k_ref[:, sl] * (dh ** -0.5)
    vh = v_ref[:, sl]
    li_c = _col(gc, h)
    b_c = _col(gc, 8 + h)
    lib_r = _row(gr, h) - _row(gr, 8 + h)
    b_last = _colsum(jnp.where(_iota((lc, 1), 0) == lc - 1, b_c, 0.0))
    causal = _iota((lc, lc), 1) <= _iota((lc, lc), 0)
    dmat = jnp.where(causal, b_c + lib_r, NEG_BIG)
    m_inter = b_c + m_prev
    if m_t is None:
        m_t = jnp.maximum(m_inter, jnp.max(dmat, axis=1, keepdims=True))
    w_intra = jnp.exp(dmat - m_t)
    w_inter = jnp.exp(m_inter - m_t)
    amat = _mm_nt(qh, kh)
    smat = amat * w_intra
    qc = _mm(qh, c_h)
    qn = _rowsum(qh * n_h)
    den = _rowsum(smat) + w_inter * qn
    gst = b_last - b_c + li_c
    m_new = jnp.maximum(b_last + m_prev, jnp.max(gst, axis=0, keepdims=True))
    w_state = jnp.exp(gst - m_new)
    decay = jnp.exp(b_last + m_prev - m_new)
    return dict(sl=sl, qh=qh, kh=kh, vh=vh, m_t=m_t, w_intra=w_intra, w_inter=w_inter, smat=smat, qc=qc, qn=qn,
                den=den, m_new=m_new, w_state=w_state, decay=decay)


def _ml_cell_fwd(q, k, v, gcol, grow, u, ng, nh):
    s, d = q.shape
    lc = ML_CHUNK
    nc = s // lc
    dh = d // nh

    def body(q_ref, k_ref, v_ref, gc_ref, gr_ref, o_ref, z_ref, ng_ref,
             cell_ref, y_ref, cs_ref, ns_ref, ms_ref, mt_ref, c_sc, n_sc, m_sc):
        @pl.when(pl.program_id(0) == 0)
        def _():
            c_sc[...] = jnp.zeros_like(c_sc)
            n_sc[...] = jnp.zeros_like(n_sc)
            m_sc[...] = jnp.zeros_like(m_sc)

        gc = gc_ref[...]
        gr = gr_ref[...]
        lane = _iota((lc, LANES), 1)
        mt_acc = jnp.zeros((lc, LANES), F32)
        for h in range(nh):
            c_h = c_sc[h]
            n_h = n_sc[h, 0:1, :]
            m_prev = jnp.max(m_sc[h, 0:1, :], axis=1, keepdims=True)
            cs_ref[0, h] = c_h
            ns_ref[0, h] = n_sc[h]
            ms_ref[0, h] = m_sc[h]
            t = _cell_chunk(h, nh, q_ref, k_ref, v_ref, gc, gr, m_prev, c_h, n_h)
            sl = t["sl"]
            num = _mm(t["smat"], t["vh"]) + t["w_inter"] * t["qc"]
            cell_h = num / jnp.maximum(jnp.abs(t["den"]), jnp.exp(-t["m_t"]))
            mt_acc = jnp.where(lane == h, t["m_t"], mt_acc)
            kw = t["kh"] * t["w_state"]
            c_sc[h] = t["decay"] * c_h + _mm_tn(kw, t["vh"])
            n_sc[h] = _bcast8(t["decay"] * n_h + _colsum(kw))
            m_sc[h] = jnp.broadcast_to(t["m_new"], (SUBLANES, LANES))
            hg = _sigmoid(o_ref[:, sl]) * cell_h
            hn = hg * lax.rsqrt(jnp.mean(hg * hg, axis=1, keepdims=True) + EPS)
            z = z_ref[:, sl]
            cell_ref[:, sl] = cell_h
            y_ref[:, sl] = hn * ng_ref[:, sl] * z * _sigmoid(z)
        mt_ref[...] = mt_acc

    tok = pl.BlockSpec((lc, d), lambda c: (c, 0))
    return pl.pallas_call(
        body, name="mlstm_cell_fwd", grid=(nc,),
        in_specs=[tok, tok, tok, pl.BlockSpec((lc, LANES), lambda c: (c, 0)), pl.BlockSpec((16, lc), lambda c: (0, c)),
                  pl.BlockSpec((lc, d), lambda c: (c, 3)), pl.BlockSpec((lc, d), lambda c: (c, 4)), _full((1, d))],
        out_specs=[tok, tok, pl.BlockSpec((1, nh, dh, dh), lambda c: (c, 0, 0, 0)),
                   pl.BlockSpec((1, nh, SUBLANES, dh), lambda c: (c, 0, 0, 0)),
                   pl.BlockSpec((1, nh, SUBLANES, LANES), lambda c: (c, 0, 0, 0)),
                   pl.BlockSpec((lc, LANES), lambda c: (c, 0))],
        out_shape=[jax.ShapeDtypeStruct((s, d), F32), jax.ShapeDtypeStruct((s, d), F32),
                   jax.ShapeDtypeStruct((nc, nh, dh, dh), F32), jax.ShapeDtypeStruct((nc, nh, SUBLANES, dh), F32),
                   jax.ShapeDtypeStruct((nc, nh, SUBLANES, LANES), F32), jax.ShapeDtypeStruct((s, LANES), F32)],
        scratch_shapes=[pltpu.VMEM((nh, dh, dh), F32), pltpu.VMEM((nh, SUBLANES, dh), F32),
                        pltpu.VMEM((nh, SUBLANES, LANES), F32)],
        compiler_params=_params(1),
    )(q, k, v, gcol, grow, u, u, ng)


def _out_fwd(x, y_rg, y_ml, gate, w_out_g, layer, tile):
    s, d = x.shape
    nd, _, r, _ = w_out_g.shape

    def body(x_ref, yr_ref, ym_ref, g_ref, w_ref, xn_ref, y_ref):
        acc = jnp.zeros((tile, d), F32)
        for j in range(nd):
            src = yr_ref if j * r < d else ym_ref
            off = (j * r) % d
            acc = acc + _mm(src[:, off:off + r], w_ref[j, 0])
        y_ref[...] = acc
        xn_ref[...] = x_ref[...] + g_ref[...] * acc

    tok = pl.BlockSpec((tile, d), lambda i: (i, 0))
    return pl.pallas_call(
        body, name="out_proj_fwd", grid=(s // tile,),
        in_specs=[tok, tok, tok, _full((1, d)), pl.BlockSpec((nd, 1, r, d), lambda i: (0, layer, 0, 0))],
        out_specs=[tok, tok],
        out_shape=[jax.ShapeDtypeStruct((s, d), F32)] * 2,
        compiler_params=_params(1),
    )(x, y_rg, y_ml, gate, w_out_g)


def _loss_call(x, fg, target, tile):
    s, d = x.shape

    def body(x_ref, g_ref, t_ref, dx_ref, loss_ref, gg_ref):
        @pl.when(pl.program_id(0) == 0)
        def _():
            loss_ref[...] = jnp.zeros_like(loss_ref)
            gg_ref[...] = jnp.zeros_like(gg_ref)

        xv = x_ref[...]
        g = g_ref[...]
        rs = lax.rsqrt(jnp.mean(xv * xv, axis=1, keepdims=True) + EPS)
        xh = xv * rs
        e = xh * g - t_ref[...]
        loss_ref[...] += jnp.broadcast_to(_colsum(_rowsum(e * e)) * (0.5 / d), loss_ref.shape)
        dy = e * (1.0 / d)
        gg_ref[...] += _bcast8(_colsum(dy * xh))
        dxh = dy * g
        dx_ref[...] = rs * (dxh - xh * jnp.mean(dxh * xh, axis=1, keepdims=True))

    tok = pl.BlockSpec((tile, d), lambda i: (i, 0))
    return pl.pallas_call(
        body, name="final_norm_loss", grid=(s // tile,),
        in_specs=[tok, _full((1, d)), tok],
        out_specs=[tok, _full((SUBLANES, LANES)), _full((SUBLANES, d))],
        out_shape=[jax.ShapeDtypeStruct((s, d), F32), jax.ShapeDtypeStruct((SUBLANES, LANES), F32),
                   jax.ShapeDtypeStruct((SUBLANES, d), F32)],
        compiler_params=_params(1),
    )(x, fg, target)


def _out_bwd(dxo, gate, y, y_rg, y_ml, w_out_g, layer, tile):
    s, d = dxo.shape
    nd, _, r, _ = w_out_g.shape

    def body(dx_ref, g_ref, y_ref, yr_ref, ym_ref, w_ref, dyr_ref, dym_ref, gw_ref, dg_ref):
        @pl.when(pl.program_id(0) == 0)
        def _():
            gw_ref[...] = jnp.zeros_like(gw_ref)
            dg_ref[...] = jnp.zeros_like(dg_ref)

        dxv = dx_ref[...]
        dg_ref[...] += _bcast8(_colsum(dxv * y_ref[...]))
        dyb = (dxv * g_ref[...]).astype(BF16)
        for j in range(nd):
            src, dst = (yr_ref, dyr_ref) if j * r < d else (ym_ref, dym_ref)
            off = (j * r) % d
            dst[:, off:off + r] = _mm_nt(dyb, w_ref[j, 0])
            gw_ref[j] += _mm_tn(src[:, off:off + r], dyb)

    tok = pl.BlockSpec((tile, d), lambda i: (i, 0))
    return pl.pallas_call(
        body, name="out_proj_bwd", grid=(s // tile,),
        in_specs=[tok, _full((1, d)), tok, tok, tok, pl.BlockSpec((nd, 1, r, d), lambda i: (0, layer, 0, 0))],
        out_specs=[tok, tok, _full((nd, r, d)), _full((SUBLANES, d))],
        out_shape=[jax.ShapeDtypeStruct((s, d), F32)] * 2 + [jax.ShapeDtypeStruct((nd, r, d), F32),
                                                             jax.ShapeDtypeStruct((SUBLANES, d), F32)],
        compiler_params=_params(1),
    )(dxo, gate, y, y_rg, y_ml, w_out_g)


def _ml_cell_bwd(dy_ml, u, cell, q, k, v, gcol, grow, mt, cs, ns, ms, ng, nh):
    s, d = q.shape
    lc = ML_CHUNK
    nc = s // lc
    dh = d // nh

    def body(dy_ref, o_ref, z_ref, cell_ref, q_ref, k_ref, v_ref, gc_ref, gr_ref, mt_ref, cs_ref, ns_ref, ms_ref,
             ng_ref, dq_ref, dk_ref, dv_ref, dg_ref, do_ref, dz_ref, gng_ref, dc_sc, dn_sc):
        @pl.when(pl.program_id(0) == 0)
        def _():
            dc_sc[...] = jnp.zeros_like(dc_sc)
            dn_sc[...] = jnp.zeros_like(dn_sc)
            gng_ref[...] = jnp.zeros_like(gng_ref)

        gc = gc_ref[...]
        gr = gr_ref[...]
        mtv = mt_ref[...]
        lane = _iota((lc, LANES), 1)
        rowv = _iota((lc, 1), 0)
        dg_acc = jnp.zeros((lc, LANES), F32)
        for h in range(nh):
            c_h = cs_ref[0, h]
            n_h = ns_ref[0, h, 0:1, :]
            m_prev = jnp.max(ms_ref[0, h, 0:1, :], axis=1, keepdims=True)
            t = _cell_chunk(h, nh, q_ref, k_ref, v_ref, gc, gr, m_prev, c_h, n_h, m_t=_col(mtv, h))
            sl, qh, kh, vh = t["sl"], t["qh"], t["kh"], t["vh"]
            w_intra, w_inter, smat, w_state, decay = t["w_intra"], t["w_inter"], t["smat"], t["w_state"], t["decay"]
            cell_h = cell_ref[:, sl]
            o = o_ref[:, sl]
            z = z_ref[:, sl]
            dyv = dy_ref[:, sl]
            ngh = ng_ref[:, sl]
            so = _sigmoid(o)
            hg = so * cell_h
            rinv = lax.rsqrt(jnp.mean(hg * hg, axis=1, keepdims=True) + EPS)
            hn = hg * rinv
            sz = _sigmoid(z)
            dz_ref[:, sl] = (dyv * hn * ngh * (sz + z * sz * (1.0 - sz))).astype(BF16)
            dymid = dyv * z * sz
            gng_ref[:, sl] += _bcast8(_colsum(dymid * hn))
            dhn = dymid * ngh
            dhg = rinv * (dhn - hn * jnp.mean(dhn * hn, axis=1, keepdims=True))
            do_ref[:, sl] = (dhg * cell_h * so * (1.0 - so)).astype(BF16)
            dcell = dhg * so
            eneg = jnp.exp(-t["m_t"])
            aden = jnp.abs(t["den"])
            nst = jnp.maximum(aden, eneg)
            dnum = dcell / nst
            dden = jnp.where(aden > eneg, -_rowsum(cell_h * dcell) / nst * jnp.sign(t["den"]), 0.0)
            pmat = _mm_nt(dnum, vh) + dden
            damat = pmat * w_intra
            gmat = pmat * smat
            wdn = w_inter * dnum
            wdd = w_inter * dden
            dqh = _mm(damat, kh) + _mm_nt(wdn, c_h) + wdd * n_h
            dkh = _mm_tn(damat, qh)
            dvh = _mm_tn(smat, dnum)
            dw_inter = _rowsum(dnum * t["qc"]) + dden * t["qn"]
            dcn = dc_sc[h]
            dnn = dn_sc[h, 0:1, :]
            kw = kh * w_state
            dkw = _mm_nt(vh, dcn) + dnn
            dvh = dvh + _mm(kw, dcn)
            dkh = dkh + dkw * w_state
            dgst = _rowsum(dkw * kh) * w_state
            ddecay = _colsum(_rowsum(dcn * c_h)) + _rowsum(dnn * n_h)
            db_last = _colsum(dgst) + ddecay * decay
            rs_g = _rowsum(gmat)
            cs_g = _rowsum(gmat.T)
            db = rs_g - cs_g + dw_inter * w_inter - dgst + jnp.where(rowv == lc - 1, db_last, 0.0)
            dli = cs_g + dgst
            dc_sc[h] = decay * dcn + _mm_tn(qh, wdn)
            dn_sc[h] = _bcast8(decay * dnn + _colsum(qh * wdd))
            dq_ref[:, sl] = dqh
            dk_ref[:, sl] = dkh * (dh ** -0.5)
            dv_ref[:, sl] = dvh
            dg_acc = jnp.where(lane == h, dli, jnp.where(lane == 4 + h, db, dg_acc))
        dg_ref[...] = dg_acc

    rev = lambda c: nc - 1 - c
    tok = pl.BlockSpec((lc, d), lambda c: (rev(c), 0))
    g128 = pl.BlockSpec((lc, LANES), lambda c: (rev(c), 0))
    return pl.pallas_call(
        body, name="mlstm_cell_bwd", grid=(nc,),
        in_specs=[tok, pl.BlockSpec((lc, d), lambda c: (rev(c), 3)), pl.BlockSpec((lc, d), lambda c: (rev(c), 4)),
                  tok, tok, tok, tok, g128, pl.BlockSpec((16, lc), lambda c: (0, rev(c))), g128,
                  pl.BlockSpec((1, nh, dh, dh), lambda c: (rev(c), 0, 0, 0)),
                  pl.BlockSpec((1, nh, SUBLANES, dh), lambda c: (rev(c), 0, 0, 0)),
                  pl.BlockSpec((1, nh, SUBLANES, LANES), lambda c: (rev(c), 0, 0, 0)), _full((1, d))],
        out_specs=[tok, tok, tok, g128, tok, tok, _full((SUBLANES, d))],
        out_shape=[jax.ShapeDtypeStruct((s, d), F32)] * 3 + [jax.ShapeDtypeStruct((s, LANES), F32)]
        + [jax.ShapeDtypeStruct((s, d), BF16)] * 2 + [jax.ShapeDtypeStruct((SUBLANES, d), F32)],
        scratch_shapes=[pltpu.VMEM((nh, dh, dh), F32), pltpu.VMEM((nh, SUBLANES, dh), F32)],
        compiler_params=_params(1),
    )(dy_ml, u, u, cell, q, k, v, gcol, grow, mt, cs, ns, ms, ng)


def _halo_spec(d, tile, nt, col):
    per = tile // SUBLANES
    return pl.BlockSpec((SUBLANES, d), lambda i: (jnp.maximum((nt - 1 - i) * per - 1, 0), col))


def _ml_pre_bwd(dq, dk, dv, dgates, gcol, u, q, k, v, conv_w, conv_b, w_q, w_k, w_v, wif_t, tile):
    s, d = dq.shape
    nt = s // tile
    nh, dh, _ = w_q.shape

    def body(dq_ref, dk_ref, dv_ref, dg_ref, gc_ref, x_ref, halo_ref, q_ref, k_ref, v_ref, cw_ref, cb_ref,
             wq_ref, wk_ref, wv_ref, wift_ref,
             dx_ref, gwq_ref, gwk_ref, gwv_ref, gwif_ref, gbif_ref, gcw_ref, gcb_ref, next8):
        i = pl.program_id(0)

        @pl.when(i == 0)
        def _():
            next8[...] = jnp.zeros_like(next8)
            for ref in (gwq_ref, gwk_ref, gwv_ref, gwif_ref, gbif_ref, gcw_ref, gcb_ref):
                ref[...] = jnp.zeros_like(ref)

        x = x_ref[...]
        halo = halo_ref[...] * jnp.where(i < nt - 1, 1.0, 0.0)
        pre = _conv(x, halo, cw_ref) + cb_ref[...]
        sg = _sigmoid(pre)
        xc = pre * sg
        dgc = dg_ref[...]
        lane = _iota(dgc.shape, 1)
        utri = jnp.where(_iota((ML_CHUNK, ML_CHUNK), 0) <= _iota((ML_CHUNK, ML_CHUNK), 1), 1.0, 0.0)
        rcs = [_mm_hi(utri, dgc[c * ML_CHUNK:(c + 1) * ML_CHUNK, :]) for c in range(tile // ML_CHUNK)]
        rc = rcs[0] if len(rcs) == 1 else jnp.concatenate(rcs, axis=0)
        dgates_v = jnp.where(lane < 4, dgc, jnp.where(lane < 8, rc * (1.0 - jnp.exp(gc_ref[...])), 0.0))
        dgb = dgates_v.astype(BF16)
        gbif_ref[...] += jnp.broadcast_to(_colsum(dgates_v), gbif_ref.shape)
        ext = jnp.dot(dgb, wift_ref[...], preferred_element_type=F32)
        dqt = dq_ref[...] + ext[:, 0:d]
        dkt = dk_ref[...] + ext[:, d:2 * d]
        dvt = dv_ref[...] + ext[:, 2 * d:3 * d]
        gwif_ref[:, 0:d] += _mm_tn(dgb, q_ref[...])
        gwif_ref[:, d:2 * d] += _mm_tn(dgb, k_ref[...])
        gwif_ref[:, 2 * d:3 * d] += _mm_tn(dgb, v_ref[...])
        dxc_parts, dxv_parts = [], []
        for h in range(nh):
            sl = slice(h * dh, (h + 1) * dh)
            gwq_ref[h] += _mm_tn(xc[:, sl], dqt[:, sl])
            gwk_ref[h] += _mm_tn(xc[:, sl], dkt[:, sl])
            gwv_ref[h] += _mm_tn(x[:, sl], dvt[:, sl])
            dxc_parts.append(_mm_nt(dqt[:, sl], wq_ref[h]) + _mm_nt(dkt[:, sl], wk_ref[h]))
            dxv_parts.append(_mm_nt(dvt[:, sl], wv_ref[h]))
        dxc = jnp.concatenate(dxc_parts, axis=1)
        dxv = jnp.concatenate(dxv_parts, axis=1)
        dpre = dxc * (sg + pre * sg * (1.0 - sg))
        gcb_ref[...] += _bcast8(_colsum(dpre))
        for kk in range(CONV_WIDTH):
            gcw_ref[kk:kk + 1, :] += _colsum(dpre * _shift_down(x, CONV_WIDTH - 1 - kk, halo))
        dx_ref[...] = (dxv + _conv_bwd_x(dpre, next8[...], cw_ref)).astype(BF16)
        next8[...] = dpre[0:SUBLANES, :]

    rev = lambda i: nt - 1 - i
    tok = pl.BlockSpec((tile, d), lambda i: (rev(i), 0))
    g128 = pl.BlockSpec((tile, LANES), lambda i: (rev(i), 0))
    wsh = (nh, dh, dh)
    return pl.pallas_call(
        body, name="mlstm_proj_bwd", grid=(nt,),
        in_specs=[tok, tok, tok, g128, g128, pl.BlockSpec((tile, d), lambda i: (rev(i), 2)), _halo_spec(d, tile, nt, 2),
                  tok, tok, tok, _full(conv_w.shape), _full((1, d)), _full(wsh), _full(wsh), _full(wsh),
                  _full(wif_t.shape)],
        out_specs=[tok, _full(wsh), _full(wsh), _full(wsh), _full((LANES, 3 * d)), _full((SUBLANES, LANES)),
                   _full((SUBLANES, d)), _full((SUBLANES, d))],
        out_shape=[jax.ShapeDtypeStruct((s, d), BF16)] + [jax.ShapeDtypeStruct(wsh, F32)] * 3
        + [jax.ShapeDtypeStruct((LANES, 3 * d), F32), jax.ShapeDtypeStruct((SUBLANES, LANES), F32),
           jax.ShapeDtypeStruct((SUBLANES, d), F32), jax.ShapeDtypeStruct((SUBLANES, d), F32)],
        scratch_shapes=[pltpu.VMEM((SUBLANES, d), F32)],
        compiler_params=_params(1),
    )(dq, dk, dv, dgates, gcol, u, u, q, k, v, conv_w, conv_b, w_q, w_k, w_v, wif_t)


def _rg_bwd(dy_rg, u, h_rg, conv_w, conv_b, w_a, b_a, w_x, b_x, lam, tile):
    s, d = dy_rg.shape
    nt = s // tile
    nh, dh, _ = w_a.shape

    def body(dy_ref, x_ref, xhalo_ref, z_ref, h_ref, hhalo_ref, cw_ref, cb_ref, wa_ref, ba_ref, wx_ref, bx_ref, lam_ref,
             dx_ref, dz_ref, gwa_ref, gwx_ref, gba_ref, gbx_ref, glam_ref, gcw_ref, gcb_ref, next8, anext, dnext):
        i = pl.program_id(0)

        @pl.when(i == 0)
        def _():
            for ref in (next8, anext, dnext, gwa_ref, gwx_ref, gba_ref, gbx_ref, glam_ref, gcw_ref, gcb_ref):
                ref[...] = jnp.zeros_like(ref)

        inner = jnp.where(i < nt - 1, 1.0, 0.0)
        x = x_ref[...]
        halo = xhalo_ref[...] * inner
        xc = _conv(x, halo, cw_ref) + cb_ref[...]
        r, ig, sp, a, beta = _rg_gates(xc, wa_ref, ba_ref, wx_ref, bx_ref, lam_ref)
        h = h_ref[...]
        row = _iota(h.shape, 0)
        hprev = jnp.where(row >= 1, pltpu.roll(h, 1, 0), hhalo_ref[SUBLANES - 1:SUBLANES, :] * inner)
        z = z_ref[...]
        sz = _sigmoid(z)
        dyv = dy_ref[...]
        dz_ref[...] = (dyv * h * (sz + z * sz * (1.0 - sz))).astype(BF16)
        a_up = jnp.where(row < tile - 1, pltpu.roll(a, tile - 1, 0), anext[0:1, :])
        acum, bsum = _scan_rev(a_up, dyv * z * sz)
        delta = bsum + acum * dnext[0:1, :]
        anext[...] = a[0:SUBLANES, :]
        dnext[...] = delta[0:SUBLANES, :]
        dla = delta * hprev * a - delta * ig * xc * (a * a / beta)
        glam_ref[...] += _bcast8(_colsum(dla * r) * (RG_C * _sigmoid(-lam_ref[...])))
        dpa = dla * (-RG_C * sp) * r * (1.0 - r)
        dpx = delta * beta * xc * ig * (1.0 - ig)
        gba_ref[...] += _bcast8(_colsum(dpa))
        gbx_ref[...] += _bcast8(_colsum(dpx))
        parts = []
        for hh in range(nh):
            sl = slice(hh * dh, (hh + 1) * dh)
            gwa_ref[hh] += _mm_tn(xc[:, sl], dpa[:, sl])
            gwx_ref[hh] += _mm_tn(xc[:, sl], dpx[:, sl])
            parts.append(_mm_nt(dpa[:, sl], wa_ref[hh]) + _mm_nt(dpx[:, sl], wx_ref[hh]))
        dxc = delta * beta * ig + jnp.concatenate(parts, axis=1)
        gcb_ref[...] += _bcast8(_colsum(dxc))
        for kk in range(CONV_WIDTH):
            gcw_ref[kk:kk + 1, :] += _colsum(dxc * _shift_down(x, CONV_WIDTH - 1 - kk, halo))
        dx_ref[...] = _conv_bwd_x(dxc, next8[...], cw_ref).astype(BF16)
        next8[...] = dxc[0:SUBLANES, :]

    rev = lambda i: nt - 1 - i
    tok = pl.BlockSpec((tile, d), lambda i: (rev(i), 0))
    vec = _full((1, d))
    acc = _full((SUBLANES, d))
    wsh = (nh, dh, dh)
    return pl.pallas_call(
        body, name="rglru_bwd", grid=(nt,),
        in_specs=[tok, tok, _halo_spec(d, tile, nt, 0), pl.BlockSpec((tile, d), lambda i: (rev(i), 1)), tok,
                  _halo_spec(d, tile, nt, 0), _full(conv_w.shape), vec, _full(wsh), vec, _full(wsh), vec, vec],
        out_specs=[tok, tok, _full(wsh), _full(wsh), acc, acc, acc, acc, acc],
        out_shape=[jax.ShapeDtypeStruct((s, d), BF16)] * 2 + [jax.ShapeDtypeStruct(wsh, F32)] * 2
        + [jax.ShapeDtypeStruct((SUBLANES, d), F32)] * 5,
        scratch_shapes=[pltpu.VMEM((SUBLANES, d), F32)] * 3,
        compiler_params=_params(1),
    )(dy_rg, u, u, u, h_rg, h_rg, conv_w, conv_b, w_a, b_a, w_x, b_x, lam)


def _segments(d, w, n_pieces, n_slots):
    bounds = sorted({k * d for k in range(n_pieces + 1)} | {j * w for j in range(n_slots + 1)})
    return [(lo // d, lo % d, lo // w, lo % w, hi - lo) for lo, hi in zip(bounds[:-1], bounds[1:])]


def _in_bwd(pieces, x, dxo, ng, scale, w_in_g, layer, tile):
    s, d = x.shape
    nd, _, _, w = w_in_g.shape
    segs = _segments(d, w, len(pieces), nd)

    def body(*refs):
        p_refs = refs[:len(pieces)]
        x_ref, dxo_ref, ng_ref, sc_ref, w_ref, dx_ref, dsc_ref, dsh_ref, gng_ref = refs[len(pieces):]

        @pl.when(pl.program_id(0) == 0)
        def _():
            for ref in (dsc_ref, dsh_ref, gng_ref):
                ref[...] = jnp.zeros_like(ref)

        dh = jnp.zeros((tile, d), F32)
        for (kk, a, j, b, width) in segs:
            dh = dh + _mm_nt(p_refs[kk][:, a:a + width], w_ref[j, 0, :, b:b + width])
        xv = x_ref[...]
        g = ng_ref[...]
        rs = lax.rsqrt(jnp.mean(xv * xv, axis=1, keepdims=True) + EPS)
        xh = xv * rs
        dsh_ref[...] += _bcast8(_colsum(dh))
        dsc_ref[...] += _bcast8(_colsum(dh * xh * g))
        dhn = dh * (1.0 + sc_ref[...])
        gng_ref[...] += _bcast8(_colsum(dhn * xh))
        dxh = dhn * g
        dx_ref[...] = dxo_ref[...] + rs * (dxh - xh * jnp.mean(dxh * xh, axis=1, keepdims=True))

    tok = pl.BlockSpec((tile, d), lambda i: (i, 0))
    vec = _full((1, d))
    acc = _full((SUBLANES, d))
    return pl.pallas_call(
        body, name="in_proj_bwd_x", grid=(s // tile,),
        in_specs=[tok] * len(pieces) + [tok, tok, vec, vec, pl.BlockSpec((nd, 1, d, w), lambda i: (0, layer, 0, 0))],
        out_specs=[tok, acc, acc, acc],
        out_shape=[jax.ShapeDtypeStruct((s, d), F32)] + [jax.ShapeDtypeStruct((SUBLANES, d), F32)] * 3,
        compiler_params=_params(1),
    )(*pieces, x, dxo, ng, scale, w_in_g)


def _in_bwd_w(pieces, hbf, w, slots, tile):
    s, d = hbf.shape
    nd_all = len(pieces) * d // w
    segs = [sg for sg in _segments(d, w, len(pieces), nd_all) if sg[2] in slots]

    def body(*refs):
        p_refs = refs[:len(pieces)]
        h_ref, gw_ref = refs[len(pieces):]

        @pl.when(pl.program_id(0) == 0)
        def _():
            gw_ref[...] = jnp.zeros_like(gw_ref)

        hv = h_ref[...]
        for (kk, a, j, b, width) in segs:
            gw_ref[j - slots[0], :, b:b + width] += _mm_tn(hv, p_refs[kk][:, a:a + width])

    tok = pl.BlockSpec((tile, d), lambda i: (i, 0))
    return pl.pallas_call(
        body, name="in_proj_bwd_w", grid=(s // tile,),
        in_specs=[tok] * len(pieces) + [tok],
        out_specs=_full((len(slots), d, w)),
        out_shape=jax.ShapeDtypeStruct((len(slots), d, w), F32),
        compiler_params=_params(1),
    )(*pieces, hbf)


def _ada_bwd_w(cact_col, dmod, nd):
    d = cact_col.shape[0]
    w = dmod.shape[1] // nd

    def body(c_ref, m_ref, o_ref):
        o_ref[0] = c_ref[...] * m_ref[...]

    return pl.pallas_call(
        body, name="adaln_bwd_w", grid=(nd,),
        in_specs=[_full((d, 1)), pl.BlockSpec((1, w), lambda j: (0, j))],
        out_specs=pl.BlockSpec((1, d, w), lambda j: (j, 0, 0)),
        out_shape=jax.ShapeDtypeStruct((nd, d, w), F32),
        compiler_params=_params(1),
    )(cact_col, dmod)


def _exchange(arrs, gather, name):
    n = len(arrs)
    outs_shape = [jax.ShapeDtypeStruct((N_DEV,) + a.shape if gather else a.shape, a.dtype) for a in arrs]

    def body(*refs):
        ins, outs = refs[:n], refs[n:2 * n]
        send_sems, recv_sems, local_sems = refs[2 * n:]
        x, y, c = (lax.axis_index(ax) for ax in MESH_AXES)
        me = 4 * x + 2 * y + c
        sends, recvs = [], []
        for flip in range(1, N_DEV):
            px = x ^ ((flip >> 2) & 1)
            py = y ^ ((flip >> 1) & 1)
            pc = c ^ (flip & 1)
            peer = 4 * px + 2 * py + pc
            for kk in range(n):
                sem = kk * (N_DEV - 1) + flip - 1
                src = ins[kk] if gather else ins[kk].at[peer]
                sends.append(pltpu.make_async_remote_copy(
                    src_ref=src, dst_ref=outs[kk].at[me], send_sem=send_sems.at[sem], recv_sem=recv_sems.at[sem],
                    device_id=(px, py, pc), device_id_type=pl.DeviceIdType.MESH))
                recvs.append(pltpu.make_async_remote_copy(
                    src_ref=src, dst_ref=outs[kk].at[peer], send_sem=send_sems.at[sem], recv_sem=recv_sems.at[sem],
                    device_id=(px, py, pc), device_id_type=pl.DeviceIdType.MESH))
        for cp in sends:
            cp.start()
        local = [pltpu.make_async_copy(ins[kk] if gather else ins[kk].at[me], outs[kk].at[me], local_sems.at[kk])
                 for kk in range(n)]
        for cp in local:
            cp.start()
        for cp in recvs:
            cp.wait_recv()
        for cp in sends:
            cp.wait_send()
        for cp in local:
            cp.wait()

    return pl.pallas_call(
        body, name=name,
        in_specs=[pl.BlockSpec(memory_space=pl.ANY)] * n,
        out_specs=[pl.BlockSpec(memory_space=pl.ANY)] * n,
        out_shape=outs_shape,
        scratch_shapes=[pltpu.SemaphoreType.DMA((n * (N_DEV - 1),)), pltpu.SemaphoreType.DMA((n * (N_DEV - 1),)),
                        pltpu.SemaphoreType.DMA((n,))],
    )(*arrs)


def _mesh_place():
    x, y, c = (lax.axis_index(ax) for ax in MESH_AXES)
    return x, y, c, (x, y, 1 - c), [(1 - x, y), (x, 1 - y), (1 - x, 1 - y)]


def _remote(src, dst, send_sems, recv_sems, sem, to):
    return pltpu.make_async_remote_copy(src_ref=src, dst_ref=dst, send_sem=send_sems.at[sem], recv_sem=recv_sems.at[sem],
                                        device_id=to, device_id_type=pl.DeviceIdType.MESH)


def _gather_two_level(arrs, name):
    n = len(arrs)
    per = N_DEV - 1

    def body(*refs):
        ins, outs = refs[:n], refs[n:2 * n]
        send_sems, recv_sems, local_sems = refs[2 * n:]
        x, y, c, sibling, chips = _mesh_place()

        def copy(kk, j, block, to, src=None):
            dst = outs[kk].at[4 * block[0] + 2 * block[1] + block[2]]
            return _remote(dst if src is None else src, dst, send_sems, recv_sems, kk * per + j, to)

        me = (x, y, c)
        local = [pltpu.make_async_copy(ins[kk], outs[kk].at[4 * x + 2 * y + c], local_sems.at[kk]) for kk in range(n)]
        first = []
        for j, chip in enumerate(chips):
            first += [copy(kk, 1 + j, me, (*chip, c), src=ins[kk]) for kk in range(n)]
        first += [copy(kk, 0, me, sibling, src=ins[kk]) for kk in range(n)]
        for cp in first + local:
            cp.start()
        passed = []
        for j, chip in enumerate(chips):
            for kk in range(n):
                copy(kk, 1 + j, (*chip, c), me).wait_recv()
                passed.append(copy(kk, 4 + j, (*chip, c), sibling))
                passed[-1].start()
        for kk in range(n):
            copy(kk, 0, sibling, me).wait_recv()
        for j, chip in enumerate(chips):
            for kk in range(n):
                copy(kk, 4 + j, (*chip, 1 - c), me).wait_recv()
        for cp in first + passed:
            cp.wait_send()
        for cp in local:
            cp.wait()

    return pl.pallas_call(
        body, name=name,
        in_specs=[pl.BlockSpec(memory_space=pl.ANY)] * n, out_specs=[pl.BlockSpec(memory_space=pl.ANY)] * n,
        out_shape=[jax.ShapeDtypeStruct((N_DEV,) + a.shape, a.dtype) for a in arrs],
        scratch_shapes=[pltpu.SemaphoreType.DMA((n * per,)), pltpu.SemaphoreType.DMA((n * per,)),
                        pltpu.SemaphoreType.DMA((n,))],
    )(*arrs)


N_CHIPS = N_DEV // 2


def _core_swap(arrs, name):
    n = len(arrs)

    def body(*refs):
        ins, outs = refs[:n], refs[n:2 * n]
        send_sems, recv_sems = refs[2 * n:]
        _, _, c, sibling, _ = _mesh_place()
        copies = [_remote(ins[kk].at[2 * q + (1 - c)], outs[kk].at[q], send_sems, recv_sems, kk * N_CHIPS + q, sibling)
                  for q in range(N_CHIPS) for kk in range(n)]
        for cp in copies:
            cp.start()
        for cp in copies:
            cp.wait_recv()
        for cp in copies:
            cp.wait_send()

    return pl.pallas_call(
        body, name=name,
        in_specs=[pl.BlockSpec(memory_space=pl.ANY)] * n, out_specs=[pl.BlockSpec(memory_space=pl.ANY)] * n,
        out_shape=[jax.ShapeDtypeStruct((N_CHIPS,) + a.shape[1:], a.dtype) for a in arrs],
        scratch_shapes=[pltpu.SemaphoreType.DMA((n * N_CHIPS,)), pltpu.SemaphoreType.DMA((n * N_CHIPS,))],
    )(*arrs)


def _pair_sum(a, other, parity, name):
    _, r, c = a.shape
    tr = _row_tile(r, c, 3)

    def body(p_ref, a_ref, o_ref, s_ref):
        s_ref[...] = (a_ref[...] + o_ref[...]).astype(BF16)

    return pl.pallas_call(
        body, name=name,
        grid_spec=pltpu.PrefetchScalarGridSpec(
            num_scalar_prefetch=1, grid=(N_CHIPS, r // tr),
            in_specs=[pl.BlockSpec((1, tr, c), lambda q, i, p: (2 * q + p[0], i, 0)),
                      pl.BlockSpec((1, tr, c), lambda q, i, p: (q, i, 0))],
            out_specs=pl.BlockSpec((1, tr, c), lambda q, i, p: (q, i, 0))),
        out_shape=jax.ShapeDtypeStruct((N_CHIPS, r, c), BF16),
        compiler_params=_params(2),
    )(parity, a, other)


def _chip_swap(arrs, name):
    n = len(arrs)
    per = N_CHIPS - 1

    def body(*refs):
        ins, outs = refs[:n], refs[n:2 * n]
        send_sems, recv_sems, local_sems = refs[2 * n:]
        x, y, c, _, chips = _mesh_place()
        mine = 2 * x + y
        sends = [_remote(ins[kk].at[2 * chip[0] + chip[1]], outs[kk].at[mine], send_sems, recv_sems, kk * per + j, (*chip, c))
                 for j, chip in enumerate(chips) for kk in range(n)]
        recvs = [_remote(ins[kk].at[mine], outs[kk].at[2 * chip[0] + chip[1]], send_sems, recv_sems, kk * per + j, (*chip, c))
                 for j, chip in enumerate(chips) for kk in range(n)]
        local = [pltpu.make_async_copy(ins[kk].at[mine], outs[kk].at[mine], local_sems.at[kk]) for kk in range(n)]
        for cp in sends + local:
            cp.start()
        for cp in recvs:
            cp.wait_recv()
        for cp in sends:
            cp.wait_send()
        for cp in local:
            cp.wait()

    return pl.pallas_call(
        body, name=name,
        in_specs=[pl.BlockSpec(memory_space=pl.ANY)] * n, out_specs=[pl.BlockSpec(memory_space=pl.ANY)] * n,
        out_shape=[jax.ShapeDtypeStruct(a.shape, a.dtype) for a in arrs],
        scratch_shapes=[pltpu.SemaphoreType.DMA((n * per,)), pltpu.SemaphoreType.DMA((n * per,)),
                        pltpu.SemaphoreType.DMA((n,))],
    )(*arrs)


def _adam_math(w, g, m, v):
    m = ADAM_B1 * m + (1.0 - ADAM_B1) * g
    v = ADAM_B2 * v + (1.0 - ADAM_B2) * (g * g)
    m_hat = m / (1.0 - ADAM_B1 ** ADAM_STEP)
    v_hat = v / (1.0 - ADAM_B2 ** ADAM_STEP)
    delta = -ADAM_LR * (m_hat / (jnp.sqrt(v_hat) + ADAM_EPS) + ADAM_WD * w)
    return delta, m, v


def _sum_devices(r_ref):
    acc = r_ref[0].astype(F32)
    for p in range(1, r_ref.shape[0]):
        acc = acc + r_ref[p].astype(F32)
    return acc


def _row_tile(rows, cols, n_bufs):
    budget = 24 * 1024 * 1024 // (n_bufs * 2 * cols * 4)
    t = rows
    while t > budget and t % 2 == 0 and (t // 2) % SUBLANES == 0:
        t //= 2
    return t


def _reduce_adam(recvs, w, m, v, name):
    nl, r, c = w.shape
    n_part = recvs[0].shape[0]
    tr = _row_tile(r, c, n_part * nl + 7)
    nt = r // tr

    def body(*refs):
        r_refs = refs[:nl]
        w_ref, m_ref, v_ref, g_ref, d_ref, mo_ref, vo_ref = refs[nl:]
        layer = pl.program_id(0)
        g = _sum_devices(r_refs[0])
        for ll in range(1, nl):
            g = jnp.where(layer == ll, _sum_devices(r_refs[ll]), g)
        delta, m2, v2 = _adam_math(w_ref[0], g, m_ref[0], v_ref[0])
        g_ref[0] = g
        d_ref[0] = delta
        mo_ref[0] = m2
        vo_ref[0] = v2

    def rspec(ll):
        return pl.BlockSpec((n_part, tr, c), lambda l, i: (0, jnp.where(l == ll, i, jnp.where(l < ll, 0, nt - 1)), 0))

    blk = pl.BlockSpec((1, tr, c), lambda l, i: (l, i, 0))
    return pl.pallas_call(
        body, name=name, grid=(nl, nt),
        in_specs=[rspec(ll) for ll in range(nl)] + [blk, blk, blk],
        out_specs=[blk] * 4,
        out_shape=[jax.ShapeDtypeStruct((nl, r, c), F32)] * 4,
        compiler_params=_params(2),
    )(*recvs, w, m, v)


def _sum8(recv, name):
    _, r, c = recv.shape

    def body(r_ref, o_ref):
        o_ref[...] = _sum_devices(r_ref)

    return pl.pallas_call(
        body, name=name, grid=(1,),
        in_specs=[_full(recv.shape)], out_specs=_full((r, c)),
        out_shape=jax.ShapeDtypeStruct((r, c), F32), compiler_params=_params(1),
    )(recv)


def _adam_call(w, g, m, v, name):
    r, c = w.shape

    def body(w_ref, g_ref, m_ref, v_ref, d_ref, mo_ref, vo_ref):
        delta, m2, v2 = _adam_math(w_ref[...], g_ref[...], m_ref[...], v_ref[...])
        d_ref[...] = delta
        mo_ref[...] = m2
        vo_ref[...] = v2

    return pl.pallas_call(
        body, name=name, grid=(1,),
        in_specs=[_full((r, c))] * 4, out_specs=[_full((r, c))] * 3,
        out_shape=[jax.ShapeDtypeStruct((r, c), F32)] * 3, compiler_params=_params(1),
    )(w, g, m, v)


def _tile_for(s, want):
    return min(want, s)


def _local_step(x, c, target, wts):
    s, d = x.shape
    nl = wts["w_in_g"].shape[1]
    nd = wts["w_in_g"].shape[0]
    nh_ml = wts["w_qkv"].shape[2]
    t_big = _tile_for(s, 512)
    t_mid = _tile_for(s, 256)

    mod, cact = _mod_call(c, wts["w_ada_g"], wts["b_ada"])
    row = lambda a: a.reshape(1, -1)
    saved = []
    xl = x
    for l in range(nl):
        shift, scale, gate = (row(mod[l, kk * d:(kk + 1) * d]) for kk in range(3))
        u, hbf = _in_fwd(xl, row(wts["norm_g"][l]), scale, shift, wts["w_in_g"], l, t_mid)
        h_rg, y_rg = _rg_fwd(u, d, wts["rg_conv_w"][l], row(wts["rg_conv_b"][l]), wts["rg_w_a_bf"][l],
                             row(wts["rg_b_a"][l]), wts["rg_w_x_bf"][l], row(wts["rg_b_x"][l]),
                             row(wts["rg_lambda"][l]), t_mid)
        q, k, v, gcol = _ml_pre(u, d, wts["ml_conv_w"][l], row(wts["ml_conv_b"][l]), wts["w_qkv"][l, 0],
                                wts["w_qkv"][l, 1], wts["w_qkv"][l, 2], wts["wif_pad"][l], wts["bif_pad"][l], t_mid)
        grow = gcol[:, 0:16].T
        cell, y_ml, cs, ns, ms, mt = _ml_cell_fwd(q, k, v, gcol, grow, u, row(wts["ml_norm_g"][l]), nh_ml)
        x_new, y = _out_fwd(xl, y_rg, y_ml, gate, wts["w_out_g"], l, t_big)
        saved.append(dict(x=xl, u=u, hbf=hbf, h_rg=h_rg, y_rg=y_rg, q=q, k=k, v=v, gcol=gcol, grow=grow, cell=cell,
                          y_ml=y_ml, cs=cs, ns=ns, ms=ms, mt=mt, y=y, scale=scale, gate=gate))
        xl = x_new

    dx, loss_p, g_final = _loss_call(xl, row(wts["final_g"]), target, t_big)
    grads = [None] * nl
    cact_col = cact[0].reshape(d, 1)
    for l in reversed(range(nl)):
        sv = saved[l]
        dy_rg, dy_ml, gw_out, dgate = _out_bwd(dx, sv["gate"], sv["y"], sv["y_rg"], sv["y_ml"], wts["w_out_g"], l, t_big)
        dq, dk, dv, dgates, d_mlo, d_mlz, g_mlng = _ml_cell_bwd(
            dy_ml, sv["u"], sv["cell"], sv["q"], sv["k"], sv["v"], sv["gcol"], sv["grow"], sv["mt"], sv["cs"],
            sv["ns"], sv["ms"], row(wts["ml_norm_g"][l]), nh_ml)
        d_mlx, g_wq, g_wk, g_wv, g_wift, g_bif, g_mlcw, g_mlcb = _ml_pre_bwd(
            dq, dk, dv, dgates, sv["gcol"], sv["u"], sv["q"], sv["k"], sv["v"], wts["ml_conv_w"][l],
            row(wts["ml_conv_b"][l]), wts["w_qkv"][l, 0], wts["w_qkv"][l, 1], wts["w_qkv"][l, 2], wts["wift_pad"][l], t_mid)
        d_rgx, d_rgz, g_wa, g_wx, g_ba, g_bx, g_lam, g_rgcw, g_rgcb = _rg_bwd(
            dy_rg, sv["u"], sv["h_rg"], wts["rg_conv_w"][l], row(wts["rg_conv_b"][l]), wts["rg_w_a_bf"][l],
            row(wts["rg_b_a"][l]), wts["rg_w_x_bf"][l], row(wts["rg_b_x"][l]), row(wts["rg_lambda"][l]), t_mid)
        pieces = [d_rgx, d_rgz, d_mlx, d_mlo, d_mlz]
        dx, dscale, dshift, g_ng = _in_bwd(pieces, sv["x"], dx, row(wts["norm_g"][l]), sv["scale"], wts["w_in_g"], l, t_mid)
        half = nd // 2
        w_cols = wts["w_in_g"].shape[3]
        gw_in = jnp.concatenate([_in_bwd_w(pieces, sv["hbf"], w_cols, tuple(range(0, half)), t_big),
                                 _in_bwd_w(pieces, sv["hbf"], w_cols, tuple(range(half, nd)), t_big)], axis=0)
        dmod = jnp.concatenate([dshift[0:1], dscale[0:1], dgate[0:1]], axis=1)
        gw_ada = _ada_bwd_w(cact_col, dmod, nd)
        grads[l] = dict(w_ada=gw_ada, w_in=gw_in, w_out=gw_out, w_qkv=jnp.stack([g_wq, g_wk, g_wv]),
                        rg_conv_w=g_rgcw[0:CONV_WIDTH], ml_conv_w=g_mlcw[0:CONV_WIDTH], wif_t=g_wift[0:8],
                        norm_g=g_ng[0], b_ada=dmod[0], rg_conv_b=g_rgcb[0], rg_w_a=g_wa, rg_b_a=g_ba[0], rg_w_x=g_wx,
                        rg_b_x=g_bx[0], rg_lambda=g_lam[0], ml_conv_b=g_mlcb[0], ml_b_if=g_bif[0, 0:8],
                        ml_norm_g=g_mlng[0])
    return loss_p[0, 0], dx, grads, g_final[0]


REPLICATED = ("norm_g", "b_ada", "rg_conv_b", "rg_w_a", "rg_b_a", "rg_w_x", "rg_b_x", "rg_lambda", "ml_conv_b",
              "ml_b_if", "ml_norm_g", "final_g")
ROW_ALIGN = N_DEV * SUBLANES


def _to_rows(a):
    flat = a.reshape(-1)
    pad = (-flat.shape[0]) % LANES
    return jnp.pad(flat, (0, pad)).reshape(-1, LANES)


def _pack(arrays):
    rows = jnp.concatenate([_to_rows(a) for a in arrays], axis=0)
    return jnp.pad(rows, ((0, (-rows.shape[0]) % ROW_ALIGN), (0, 0)))


def _unpack(rows, like):
    out, at = [], 0
    for a in like:
        n = -(-a.size // LANES)
        out.append(rows[at:at + n].reshape(-1)[:a.size].reshape(a.shape))
        at += n
    return out


def _small_pack(rg_conv_w, ml_conv_w, ml_w_if):
    nl = rg_conv_w.shape[0]
    wif_t = jnp.swapaxes(ml_w_if, 1, 2).reshape(nl, -1, LANES)
    return jnp.concatenate([rg_conv_w, ml_conv_w, wif_t], axis=1)


def _small_unpack(p, if_rows):
    nl = p.shape[0]
    rg_cw = p[:, 0:CONV_WIDTH]
    ml_cw = p[:, CONV_WIDTH:2 * CONV_WIDTH]
    wif = jnp.swapaxes(p[:, 2 * CONV_WIDTH:].reshape(nl, 8, if_rows), 1, 2)
    return rg_cw, ml_cw, wif


def _assemble_weights(big, small, rep):
    w_ada_g, w_in_g, w_out_g, qkv_g = big
    nd, nl = small.shape[0], small.shape[1]
    d = w_in_g.shape[2]
    dh = qkv_g.shape[3]
    nh = d // dh
    rsh = qkv_g.shape[2] // (3 * nh)
    w_qkv = qkv_g.reshape(nd, nl, 3, nh, rsh, dh).transpose(1, 2, 3, 0, 4, 5).reshape(nl, 3, nh, nd * rsh, dh)
    cw = small[:, :, 0:2 * CONV_WIDTH].reshape(nd, nl, 2, CONV_WIDTH, LANES).transpose(1, 2, 3, 0, 4)
    cw = cw.reshape(nl, 2, CONV_WIDTH, nd * LANES)
    if_rows = (small.shape[2] - 2 * CONV_WIDTH) * LANES // 8
    wif_t = small[:, :, 2 * CONV_WIDTH:].reshape(nd, nl, 8, if_rows).transpose(1, 2, 0, 3).reshape(nl, 8, nd * if_rows)
    wift_pad = jnp.pad(wif_t, ((0, 0), (0, LANES - 8), (0, 0))).astype(BF16)
    wif_pad = jnp.swapaxes(wift_pad, 1, 2)
    bif_pad = jnp.pad(rep["ml_b_if"], ((0, 0), (0, LANES - 8))).reshape(nl, 1, LANES)
    wts = dict(rep)
    wts.update(w_ada_g=w_ada_g, w_in_g=w_in_g, w_out_g=w_out_g, w_qkv=w_qkv, rg_conv_w=cw[:, 0], ml_conv_w=cw[:, 1],
               wif_pad=wif_pad, wift_pad=wift_pad, bif_pad=bif_pad, rg_w_a_bf=rep["rg_w_a"].astype(BF16),
               rg_w_x_bf=rep["rg_w_x"].astype(BF16))
    return wts


def _qkv_slots(g_qkv, nd):
    three, nh, dh, _ = g_qkv.shape
    return g_qkv.reshape(three, nh, nd, dh // nd, dh).transpose(2, 0, 1, 3, 4).reshape(nd, three * nh * (dh // nd), dh)


def _small_slots(g):
    nd = N_DEV
    cw = jnp.stack([g["rg_conv_w"], g["ml_conv_w"]]).reshape(2, CONV_WIDTH, nd, LANES).transpose(2, 0, 1, 3)
    cw = cw.reshape(nd, 2 * CONV_WIDTH, LANES)
    wif = g["wif_t"].reshape(8, nd, -1).transpose(1, 0, 2).reshape(nd, -1, LANES)
    return jnp.concatenate([cw, wif], axis=1)


def kernel(x, c, norm_g, w_ada, b_ada, w_in, rg_conv_w, rg_conv_b, rg_w_a, rg_b_a, rg_w_x, rg_b_x, rg_lambda, ml_conv_w, ml_conv_b, ml_w_q, ml_w_k, ml_w_v, ml_w_if, ml_b_if, ml_norm_g, w_out, final_g, loss_target, m_norm_g, m_w_ada, m_b_ada, m_w_in, m_rg_conv_w, m_rg_conv_b, m_rg_w_a, m_rg_b_a, m_rg_w_x, m_rg_b_x, m_rg_lambda, m_ml_conv_w, m_ml_conv_b, m_ml_w_q, m_ml_w_k, m_ml_w_v, m_ml_w_if, m_ml_b_if, m_ml_norm_g, m_w_out, m_final_g, v_norm_g, v_w_ada, v_b_ada, v_w_in, v_rg_conv_w, v_rg_conv_b, v_rg_w_a, v_rg_b_a, v_rg_w_x, v_rg_b_x, v_rg_lambda, v_ml_conv_w, v_ml_conv_b, v_ml_w_q, v_ml_w_k, v_ml_w_v, v_ml_w_if, v_ml_b_if, v_ml_norm_g, v_w_out, v_final_g):
    given = dict(locals())
    nl = w_in.shape[0]
    rep = {n: given[n] for n in REPLICATED}

    def qkv_shard(prefix):
        return jnp.stack([given[prefix + "ml_w_q"], given[prefix + "ml_w_k"], given[prefix + "ml_w_v"]], axis=1).reshape(
            nl, -1, ml_w_q.shape[-1])

    *big, small = _gather_two_level(
        [w_ada.astype(BF16), w_in.astype(BF16), w_out.astype(BF16), qkv_shard("").astype(BF16),
         _small_pack(rg_conv_w, ml_conv_w, ml_w_if)], "gather_weights")
    wts = _assemble_weights(big, small, rep)

    loss_p, grad_x, grads, g_final = _local_step(x[0], c, loss_target[0], wts)
    loss = lax.psum(loss_p, MESH_AXES)

    keys = ("w_ada", "w_in", "w_out", "w_qkv", "small")
    parity = lax.axis_index("c").astype(jnp.int32).reshape(1)
    recv = []
    for l in range(nl):
        g = grads[l]
        parts = [g["w_ada"], g["w_in"], g["w_out"], _qkv_slots(g["w_qkv"], N_DEV), _small_slots(g)]
        other = _core_swap(parts, "core_swap_layer%d" % l)
        sums = [_pair_sum(a, o, parity, "pair_sum_%s_layer%d" % (key, l)) for key, a, o in zip(keys, parts, other)]
        recv.append(_chip_swap(sums, "chip_swap_layer%d" % l))
    shard = {"": dict(w_ada=w_ada, w_in=w_in, w_out=w_out, w_qkv=qkv_shard(""),
                      small=_small_pack(rg_conv_w, ml_conv_w, ml_w_if))}
    for p in ("m_", "v_"):
        shard[p] = dict(w_ada=given[p + "w_ada"], w_in=given[p + "w_in"], w_out=given[p + "w_out"], w_qkv=qkv_shard(p),
                        small=_small_pack(given[p + "rg_conv_w"], given[p + "ml_conv_w"], given[p + "ml_w_if"]))
    res = {}
    for ki, key in enumerate(keys):
        res[key] = _reduce_adam([recv[l][ki] for l in range(nl)], shard[""][key], shard["m_"][key], shard["v_"][key],
                                "reduce_adam_" + key)

    rep_g = dict(final_g=g_final)
    for n in REPLICATED[:-1]:
        rep_g[n] = jnp.stack([grads[l][n] for l in range(nl)])
    pack_g = _pack([rep_g[n] for n in REPLICATED])
    rows = pack_g.shape[0] // N_DEV
    mine = _sum8(_exchange([pack_g.reshape(N_DEV, rows, LANES)], False, "reduce_scatter_replicated")[0], "sum_replicated")
    g_rep = _exchange([mine], True, "gather_replicated")[0].reshape(N_DEV * rows, LANES)
    rep_like = [rep[n] for n in REPLICATED]
    d_rep, m_rep, v_rep = _adam_call(_pack(rep_like), g_rep, _pack([given["m_" + n] for n in REPLICATED]),
                                     _pack([given["v_" + n] for n in REPLICATED]), "adam_replicated")
    rep_out = [dict(zip(REPLICATED, _unpack(a, rep_like))) for a in (g_rep, d_rep, m_rep, v_rep)]

    if_rows = ml_w_if.shape[1]
    order = ("norm_g", "w_ada", "b_ada", "w_in", "rg_conv_w", "rg_conv_b", "rg_w_a", "rg_b_a", "rg_w_x", "rg_b_x",
             "rg_lambda", "ml_conv_w", "ml_conv_b", "ml_w_q", "ml_w_k", "ml_w_v", "ml_w_if", "ml_b_if", "ml_norm_g",
             "w_out", "final_g")
    outs = [loss, grad_x[None]]
    for kind in range(4):
        qkv = res["w_qkv"][kind].reshape((nl, 3) + ml_w_q.shape[1:])
        rg_cw, ml_cw, wif = _small_unpack(res["small"][kind], if_rows)
        sharded = dict(w_ada=res["w_ada"][kind], w_in=res["w_in"][kind], w_out=res["w_out"][kind], ml_w_q=qkv[:, 0],
                       ml_w_k=qkv[:, 1], ml_w_v=qkv[:, 2], rg_conv_w=rg_cw, ml_conv_w=ml_cw, ml_w_if=wif)
        for n in order:
            outs.append(sharded[n] if n in sharded else rep_out[kind][n])
    return tuple(outs)
```

```python
import functools

import jax
import jax.numpy as jnp
from jax import lax
from jax.experimental import pallas as pl
from jax.experimental.pallas import tpu as pltpu

F32 = jnp.float32
BF16 = jnp.bfloat16
MESH_AXES = ("x", "y", "c")
N_DEV = 8
EPS = 1e-6
RG_C = 8.0
ML_CHUNK = 128
CONV_WIDTH = 4
ADAM_LR = 0.001
ADAM_B1 = 0.9
ADAM_B2 = 0.999
ADAM_EPS = 1e-08
ADAM_WD = 0.01
ADAM_STEP = 10
NEG_BIG = -1e30
LANES = 128
SUBLANES = 8
VMEM_LIMIT = 56 * 1024 * 1024
HI = lax.Precision.HIGHEST


def _params(n_grid):
    return pltpu.CompilerParams(dimension_semantics=("arbitrary",) * n_grid, vmem_limit_bytes=VMEM_LIMIT)


def _mm(a, b):
    return jnp.dot(a.astype(BF16), b.astype(BF16), preferred_element_type=F32)


def _mm_nt(a, b):
    return lax.dot_general(a.astype(BF16), b.astype(BF16), (((1,), (1,)), ((), ())), preferred_element_type=F32)


def _mm_tn(a, b):
    return lax.dot_general(a.astype(BF16), b.astype(BF16), (((0,), (0,)), ((), ())), preferred_element_type=F32)


def _mm_hi(a, b):
    return jnp.dot(a, b, precision=HI, preferred_element_type=F32)


def _sigmoid(x):
    return 1.0 / (1.0 + jnp.exp(-x))


def _softplus(x):
    return jnp.maximum(x, 0.0) + jnp.log(1.0 + jnp.exp(-jnp.abs(x)))


def _neg_expm1(x):
    poly = -x * (1.0 + x * (0.5 + x * (1.0 / 6.0 + x * (1.0 / 24.0 + x * (1.0 / 120.0)))))
    return jnp.where(jnp.abs(x) < 0.05, poly, 1.0 - jnp.exp(x))


def _iota(shape, dim):
    return lax.broadcasted_iota(jnp.int32, shape, dim)


def _colsum(x):
    return jnp.sum(x, axis=0, keepdims=True)


def _rowsum(x):
    return jnp.sum(x, axis=1, keepdims=True)


def _col(x, j):
    return _rowsum(jnp.where(_iota(x.shape, 1) == j, x, 0.0))


def _row(x, j):
    return _colsum(jnp.where(_iota(x.shape, 0) == j, x, 0.0))


def _shift_down(x, j, prev8):
    if j == 0:
        return x
    t = x.shape[0]
    main = jnp.where(_iota(x.shape, 0) >= j, pltpu.roll(x, j, 0), 0.0)
    fix = jnp.where(_iota(prev8.shape, 0) < j, pltpu.roll(prev8, j, 0), 0.0)
    return jnp.concatenate([main[0:SUBLANES] + fix, main[SUBLANES:t]], axis=0)


def _shift_up(x, j, next8):
    if j == 0:
        return x
    t = x.shape[0]
    main = jnp.where(_iota(x.shape, 0) < t - j, pltpu.roll(x, t - j, 0), 0.0)
    fix = jnp.where(_iota(next8.shape, 0) >= SUBLANES - j, pltpu.roll(next8, SUBLANES - j, 0), 0.0)
    return jnp.concatenate([main[0:t - SUBLANES], main[t - SUBLANES:t] + fix], axis=0)


def _conv(x, prev8, w_ref):
    y = w_ref[CONV_WIDTH - 1:CONV_WIDTH, :] * x
    for j in range(1, CONV_WIDTH):
        y = y + w_ref[CONV_WIDTH - 1 - j:CONV_WIDTH - j, :] * _shift_down(x, j, prev8)
    return y


def _conv_bwd_x(dy, next8, w_ref):
    dx = w_ref[CONV_WIDTH - 1:CONV_WIDTH, :] * dy
    for j in range(1, CONV_WIDTH):
        dx = dx + w_ref[CONV_WIDTH - 1 - j:CONV_WIDTH - j, :] * _shift_up(dy, j, next8)
    return dx


def _scan_fwd(a, b):
    t = a.shape[0]
    row = _iota(a.shape, 0)
    d = 1
    while d < t:
        keep = row >= d
        a_s = jnp.where(keep, pltpu.roll(a, d, 0), 1.0)
        b_s = jnp.where(keep, pltpu.roll(b, d, 0), 0.0)
        b = a * b_s + b
        a = a * a_s
        d *= 2
    return a, b


def _scan_rev(a, b):
    t = a.shape[0]
    row = _iota(a.shape, 0)
    d = 1
    while d < t:
        keep = row < t - d
        a_s = jnp.where(keep, pltpu.roll(a, t - d, 0), 1.0)
        b_s = jnp.where(keep, pltpu.roll(b, t - d, 0), 0.0)
        b = a * b_s + b
        a = a * a_s
        d *= 2
    return a, b


def _blockdiag(x, w_ref, transpose_w=False):
    nh, dh, _ = w_ref.shape
    outs = []
    for h in range(nh):
        xs = x[:, h * dh:(h + 1) * dh]
        outs.append(_mm_nt(xs, w_ref[h]) if transpose_w else _mm(xs, w_ref[h]))
    return jnp.concatenate(outs, axis=1)


def _rg_gates(xc, wa_ref, ba_ref, wx_ref, bx_ref, lam_ref):
    r = _sigmoid(_blockdiag(xc, wa_ref) + ba_ref[...])
    ig = _sigmoid(_blockdiag(xc, wx_ref) + bx_ref[...])
    sp = _softplus(-lam_ref[...])
    log_a = -RG_C * r * sp
    a = jnp.exp(log_a)
    beta = jnp.sqrt(_neg_expm1(2.0 * log_a))
    return r, ig, sp, a, beta


def _bcast8(row):
    return jnp.broadcast_to(row, (SUBLANES, row.shape[1]))


def _full(shape):
    nd = len(shape)
    return pl.BlockSpec(shape, lambda *_: (0,) * nd)


class _Comm:
    def __init__(self, arrays, out_shapes, sems, start, finish, aliases=()):
        self.arrays, self.out_shapes, self.sems = list(arrays), list(out_shapes), list(sems)
        self.start, self.finish, self.aliases = start, finish, tuple(aliases)
        self.results = None


def _call(body, comms, *, name, grid, in_specs, out_specs, out_shape, args, scratch_shapes=()):
    comms = [cm for cm in (comms or []) if cm is not None]
    n_in, n_out, n_sc = len(args), len(out_shape), len(scratch_shapes)
    c_arrays = [a for cm in comms for a in cm.arrays]
    c_outs = [o for cm in comms for o in cm.out_shapes]
    c_sems = [sm for cm in comms for sm in cm.sems]
    aliases, a_at, o_at = {}, n_in, n_out
    for cm in comms:
        for (i, j) in cm.aliases:
            aliases[a_at + i] = o_at + j
        a_at += len(cm.arrays)
        o_at += len(cm.out_shapes)

    def wrapped(*refs):
        ins, c_in = refs[:n_in], refs[n_in:n_in + len(c_arrays)]
        at = n_in + len(c_arrays)
        outs, c_out = refs[at:at + n_out], refs[at + n_out:at + n_out + len(c_outs)]
        at += n_out + len(c_outs)
        scr, sems = refs[at:at + n_sc], refs[at + n_sc:]
        views, ia, io, isem = [], 0, 0, 0
        for cm in comms:
            views.append((c_in[ia:ia + len(cm.arrays)], c_out[io:io + len(cm.out_shapes)], sems[isem:isem + len(cm.sems)]))
            ia, io, isem = ia + len(cm.arrays), io + len(cm.out_shapes), isem + len(cm.sems)
        if comms:
            @pl.when(pl.program_id(0) == 0)
            def _():
                for cm, view in zip(comms, views):
                    cm.start(*view)

        body(*ins, *outs, *scr)
        if comms:
            @pl.when(pl.program_id(0) == grid[0] - 1)
            def _():
                for cm, view in zip(comms, views):
                    cm.finish(*view)

    hbm = pl.BlockSpec(memory_space=pl.ANY)
    res = pl.pallas_call(
        wrapped, name=name, grid=grid,
        in_specs=list(in_specs) + [hbm] * len(c_arrays), out_specs=list(out_specs) + [hbm] * len(c_outs),
        out_shape=list(out_shape) + c_outs, scratch_shapes=list(scratch_shapes) + c_sems,
        input_output_aliases=aliases, compiler_params=_params(len(grid)),
    )(*args, *c_arrays)
    at = n_out
    for cm in comms:
        cm.results = list(res[at:at + len(cm.out_shapes)])
        at += len(cm.out_shapes)
    return list(res[:n_out])


def _mod_call(c, w_ada_g, b_ada):
    nd, nl, d, w = w_ada_g.shape

    def body(c_ref, w_ref, b_ref, mod_ref, cact_ref):
        cv = c_ref[...]
        ca = _bcast8(cv * _sigmoid(cv))
        cact_ref[...] = ca
        mod_ref[0, 0] = _mm(ca, w_ref[0, 0]) + b_ref[0, 0]

    mod, cact = pl.pallas_call(
        body, name="adaln_mod", grid=(nl, nd),
        in_specs=[_full((1, d)),
                  pl.BlockSpec((1, 1, d, w), lambda l, j: (j, l, 0, 0)),
                  pl.BlockSpec((1, 1, 1, w), lambda l, j: (l, j, 0, 0))],
        out_specs=[pl.BlockSpec((1, 1, SUBLANES, w), lambda l, j: (l, j, 0, 0)), _full((SUBLANES, d))],
        out_shape=[jax.ShapeDtypeStruct((nl, nd, SUBLANES, w), F32), jax.ShapeDtypeStruct((SUBLANES, d), F32)],
        compiler_params=_params(2),
    )(c, w_ada_g, b_ada.reshape(nl, nd, 1, w))
    return mod[:, :, 0, :].reshape(nl, nd * w), cact


def _in_fwd(x, ng, scale, shift, w_in_g, layer, tile, comms=None):
    s, d = x.shape
    nd, _, _, w = w_in_g.shape

    def body(x_ref, ng_ref, sc_ref, sh_ref, w_ref, u_ref, h_ref):
        xv = x_ref[...]
        rs = lax.rsqrt(jnp.mean(xv * xv, axis=1, keepdims=True) + EPS)
        hb = (xv * rs * ng_ref[...] * (1.0 + sc_ref[...]) + sh_ref[...]).astype(BF16)
        h_ref[...] = hb
        for j in range(nd):
            u_ref[:, j * w:(j + 1) * w] = jnp.dot(hb, w_ref[j, 0], preferred_element_type=F32)

    return _call(
        body, comms, name="in_proj_fwd", grid=(s // tile,),
        in_specs=[pl.BlockSpec((tile, d), lambda i: (i, 0)), _full((1, d)), _full((1, d)), _full((1, d)),
                  pl.BlockSpec((nd, 1, d, w), lambda i: (0, layer, 0, 0))],
        out_specs=[pl.BlockSpec((tile, nd * w), lambda i: (i, 0)), pl.BlockSpec((tile, d), lambda i: (i, 0))],
        out_shape=[jax.ShapeDtypeStruct((s, nd * w), F32), jax.ShapeDtypeStruct((s, d), BF16)],
        args=(x, ng, scale, shift, w_in_g))


def _rg_fwd(u, d, conv_w, conv_b, w_a, b_a, w_x, b_x, lam, tile, comms=None):
    s = u.shape[0]

    def body(x_ref, z_ref, cw_ref, cb_ref, wa_ref, ba_ref, wx_ref, bx_ref, lam_ref, h_ref, y_ref, prev8, hcar):
        @pl.when(pl.program_id(0) == 0)
        def _():
            prev8[...] = jnp.zeros_like(prev8)
            hcar[...] = jnp.zeros_like(hcar)

        x = x_ref[...]
        xc = _conv(x, prev8[...], cw_ref) + cb_ref[...]
        prev8[...] = x[tile - SUBLANES:tile, :]
        _, ig, _, a, beta = _rg_gates(xc, wa_ref, ba_ref, wx_ref, bx_ref, lam_ref)
        acum, bsum = _scan_fwd(a, beta * ig * xc)
        h = bsum + acum * hcar[SUBLANES - 1:SUBLANES, :]
        hcar[...] = h[tile - SUBLANES:tile, :]
        h_ref[...] = h
        z = z_ref[...]
        y_ref[...] = h * z * _sigmoid(z)

    vec = _full((1, d))
    return _call(
        body, comms, name="rglru_fwd", grid=(s // tile,),
        in_specs=[pl.BlockSpec((tile, d), lambda i: (i, 0)), pl.BlockSpec((tile, d), lambda i: (i, 1)),
                  _full(conv_w.shape), vec, _full(w_a.shape), vec, _full(w_x.shape), vec, vec],
        out_specs=[pl.BlockSpec((tile, d), lambda i: (i, 0))] * 2,
        out_shape=[jax.ShapeDtypeStruct((s, d), F32)] * 2,
        scratch_shapes=[pltpu.VMEM((SUBLANES, d), F32), pltpu.VMEM((SUBLANES, d), F32)],
        args=(u, u, conv_w, conv_b, w_a, b_a, w_x, b_x, lam))


def _ml_pre(u, d, conv_w, conv_b, w_q, w_k, w_v, wif, bif, tile, comms=None):
    s = u.shape[0]
    nh = w_q.shape[0]

    def body(x_ref, cw_ref, cb_ref, wq_ref, wk_ref, wv_ref, wif_ref, bif_ref, q_ref, k_ref, v_ref, g_ref, prev8):
        @pl.when(pl.program_id(0) == 0)
        def _():
            prev8[...] = jnp.zeros_like(prev8)

        x = x_ref[...]
        pre = _conv(x, prev8[...], cw_ref) + cb_ref[...]
        prev8[...] = x[tile - SUBLANES:tile, :]
        xc = pre * _sigmoid(pre)
        q = _blockdiag(xc, wq_ref)
        k = _blockdiag(xc, wk_ref)
        v = _blockdiag(x, wv_ref)
        q_ref[...] = q
        k_ref[...] = k
        v_ref[...] = v
        g = _mm(q, wif_ref[0:d, :]) + _mm(k, wif_ref[d:2 * d, :]) + _mm(v, wif_ref[2 * d:3 * d, :]) + bif_ref[...]
        lane = _iota(g.shape, 1)
        gl = jnp.where(lane < 4, g, jnp.where(lane < 8, -_softplus(-g), 0.0))
        tri = jnp.where(_iota((ML_CHUNK, ML_CHUNK), 1) <= _iota((ML_CHUNK, ML_CHUNK), 0), 1.0, 0.0)
        cums = [_mm_hi(tri, gl[c * ML_CHUNK:(c + 1) * ML_CHUNK, :]) for c in range(tile // ML_CHUNK)]
        cum = cums[0] if len(cums) == 1 else jnp.concatenate(cums, axis=0)
        g_ref[...] = gl + jnp.where((lane >= 8) & (lane < 12), pltpu.roll(cum, 4, 1), 0.0)

    vec = _full((1, d))
    return _call(
        body, comms, name="mlstm_proj_fwd", grid=(s // tile,),
        in_specs=[pl.BlockSpec((tile, d), lambda i: (i, 2)), _full(conv_w.shape), vec,
                  _full(w_q.shape), _full(w_k.shape), _full(w_v.shape), _full(wif.shape), _full((1, LANES))],
        out_specs=[pl.BlockSpec((tile, d), lambda i: (i, 0))] * 3 + [pl.BlockSpec((tile, LANES), lambda i: (i, 0))],
        out_shape=[jax.ShapeDtypeStruct((s, d), F32)] * 3 + [jax.ShapeDtypeStruct((s, LANES), F32)],
        scratch_shapes=[pltpu.VMEM((SUBLANES, d), F32)],
        args=(u, conv_w, conv_b, w_q, w_k, w_v, wif, bif))


def _cell_chunk(h, nh, q_ref, k_ref, v_ref, gc, gr, m_prev, c_h, n_h, m_t=None):
    lc = ML_CHUNK
    dh = q_ref.shape[1] // nh
    sl = slice(h * dh, (h + 1) * dh)
    qh = q_ref[:, sl]
    kh = k_ref[:, sl] * (dh ** -0.5)
    vh = v_ref[:, sl]
    li_c = _col(gc, h)
    b_c = _col(gc, 8 + h)
    lib_r = _row(gr, h) - _row(gr, 8 + h)
    b_last = _colsum(jnp.where(_iota((lc, 1), 0) == lc - 1, b_c, 0.0))
    causal = _iota((lc, lc), 1) <= _iota((lc, lc), 0)
    dmat = jnp.where(causal, b_c + lib_r, NEG_BIG)
    m_inter = b_c + m_prev
    if m_t is None:
        m_t = jnp.maximum(m_inter, jnp.max(dmat, axis=1, keepdims=True))
    w_intra = jnp.exp(dmat - m_t)
    w_inter = jnp.exp(m_inter - m_t)
    amat = _mm_nt(qh, kh)
    smat = amat * w_intra
    qc = _mm(qh, c_h)
    qn = _rowsum(qh * n_h)
    den = _rowsum(smat) + w_inter * qn
    gst = b_last - b_c + li_c
    m_new = jnp.maximum(b_last + m_prev, jnp.max(gst, axis=0, keepdims=True))
    w_state = jnp.exp(gst - m_new)
    decay = jnp.exp(b_last + m_prev - m_new)
    return dict(sl=sl, qh=qh, kh=kh, vh=vh, m_t=m_t, w_intra=w_intra, w_inter=w_inter, smat=smat, qc=qc, qn=qn,
                den=den, m_new=m_new, w_state=w_state, decay=decay)


def _ml_cell_fwd(q, k, v, gcol, grow, u, ng, nh, comms=None):
    s, d = q.shape
    lc = ML_CHUNK
    nc = s // lc
    dh = d // nh

    def body(q_ref, k_ref, v_ref, gc_ref, gr_ref, o_ref, z_ref, ng_ref,
             cell_ref, y_ref, cs_ref, ns_ref, ms_ref, mt_ref, c_sc, n_sc, m_sc):
        @pl.when(pl.program_id(0) == 0)
        def _():
            c_sc[...] = jnp.zeros_like(c_sc)
            n_sc[...] = jnp.zeros_like(n_sc)
            m_sc[...] = jnp.zeros_like(m_sc)

        gc = gc_ref[...]
        gr = gr_ref[...]
        lane = _iota((lc, LANES), 1)
        mt_acc = jnp.zeros((lc, LANES), F32)
        for h in range(nh):
            c_h = c_sc[h]
            n_h = n_sc[h, 0:1, :]
            m_prev = jnp.max(m_sc[h, 0:1, :], axis=1, keepdims=True)
            cs_ref[0, h] = c_h
            ns_ref[0, h] = n_sc[h]
            ms_ref[0, h] = m_sc[h]
            t = _cell_chunk(h, nh, q_ref, k_ref, v_ref, gc, gr, m_prev, c_h, n_h)
            sl = t["sl"]
            num = _mm(t["smat"], t["vh"]) + t["w_inter"] * t["qc"]
            cell_h = num / jnp.maximum(jnp.abs(t["den"]), jnp.exp(-t["m_t"]))
            mt_acc = jnp.where(lane == h, t["m_t"], mt_acc)
            kw = t["kh"] * t["w_state"]
            c_sc[h] = t["decay"] * c_h + _mm_tn(kw, t["vh"])
            n_sc[h] = _bcast8(t["decay"] * n_h + _colsum(kw))
            m_sc[h] = jnp.broadcast_to(t["m_new"], (SUBLANES, LANES))
            hg = _sigmoid(o_ref[:, sl]) * cell_h
            hn = hg * lax.rsqrt(jnp.mean(hg * hg, axis=1, keepdims=True) + EPS)
            z = z_ref[:, sl]
            cell_ref[:, sl] = cell_h
            y_ref[:, sl] = hn * ng_ref[:, sl] * z * _sigmoid(z)
        mt_ref[...] = mt_acc

    tok = pl.BlockSpec((lc, d), lambda c: (c, 0))
    return _call(
        body, comms, name="mlstm_cell_fwd", grid=(nc,),
        in_specs=[tok, tok, tok, pl.BlockSpec((lc, LANES), lambda c: (c, 0)), pl.BlockSpec((16, lc), lambda c: (0, c)),
                  pl.BlockSpec((lc, d), lambda c: (c, 3)), pl.BlockSpec((lc, d), lambda c: (c, 4)), _full((1, d))],
        out_specs=[tok, tok, pl.BlockSpec((1, nh, dh, dh), lambda c: (c, 0, 0, 0)),
                   pl.BlockSpec((1, nh, SUBLANES, dh), lambda c: (c, 0, 0, 0)),
                   pl.BlockSpec((1, nh, SUBLANES, LANES), lambda c: (c, 0, 0, 0)),
                   pl.BlockSpec((lc, LANES), lambda c: (c, 0))],
        out_shape=[jax.ShapeDtypeStruct((s, d), F32), jax.ShapeDtypeStruct((s, d), F32),
                   jax.ShapeDtypeStruct((nc, nh, dh, dh), F32), jax.ShapeDtypeStruct((nc, nh, SUBLANES, dh), F32),
                   jax.ShapeDtypeStruct((nc, nh, SUBLANES, LANES), F32), jax.ShapeDtypeStruct((s, LANES), F32)],
        scratch_shapes=[pltpu.VMEM((nh, dh, dh), F32), pltpu.VMEM((nh, SUBLANES, dh), F32),
                        pltpu.VMEM((nh, SUBLANES, LANES), F32)],
        args=(q, k, v, gcol, grow, u, u, ng))


def _out_fwd(x, y_rg, y_ml, gate, w_out_g, layer, tile, comms=None):
    s, d = x.shape
    nd, _, r, _ = w_out_g.shape

    def body(x_ref, yr_ref, ym_ref, g_ref, w_ref, xn_ref, y_ref):
        acc = jnp.zeros((tile, d), F32)
        for j in range(nd):
            src = yr_ref if j * r < d else ym_ref
            off = (j * r) % d
            acc = acc + _mm(src[:, off:off + r], w_ref[j, 0])
        y_ref[...] = acc
        xn_ref[...] = x_ref[...] + g_ref[...] * acc

    tok = pl.BlockSpec((tile, d), lambda i: (i, 0))
    return _call(
        body, comms, name="out_proj_fwd", grid=(s // tile,),
        in_specs=[tok, tok, tok, _full((1, d)), pl.BlockSpec((nd, 1, r, d), lambda i: (0, layer, 0, 0))],
        out_specs=[tok, tok],
        out_shape=[jax.ShapeDtypeStruct((s, d), F32)] * 2,
        args=(x, y_rg, y_ml, gate, w_out_g))


def _loss_call(x, fg, target, tile):
    s, d = x.shape

    def body(x_ref, g_ref, t_ref, dx_ref, loss_ref, gg_ref):
        @pl.when(pl.program_id(0) == 0)
        def _():
            loss_ref[...] = jnp.zeros_like(loss_ref)
            gg_ref[...] = jnp.zeros_like(gg_ref)

        xv = x_ref[...]
        g = g_ref[...]
        rs = lax.rsqrt(jnp.mean(xv * xv, axis=1, keepdims=True) + EPS)
        xh = xv * rs
        e = xh * g - t_ref[...]
        loss_ref[...] += jnp.broadcast_to(_colsum(_rowsum(e * e)) * (0.5 / d), loss_ref.shape)
        dy = e * (1.0 / d)
        gg_ref[...] += _bcast8(_colsum(dy * xh))
        dxh = dy * g
        dx_ref[...] = rs * (dxh - xh * jnp.mean(dxh * xh, axis=1, keepdims=True))

    tok = pl.BlockSpec((tile, d), lambda i: (i, 0))
    return pl.pallas_call(
        body, name="final_norm_loss", grid=(s // tile,),
        in_specs=[tok, _full((1, d)), tok],
        out_specs=[tok, _full((SUBLANES, LANES)), _full((SUBLANES, d))],
        out_shape=[jax.ShapeDtypeStruct((s, d), F32), jax.ShapeDtypeStruct((SUBLANES, LANES), F32),
                   jax.ShapeDtypeStruct((SUBLANES, d), F32)],
        compiler_params=_params(1),
    )(x, fg, target)


def _out_bwd(dxo, gate, y, y_rg, y_ml, w_out_g, layer, tile, comms=None):
    s, d = dxo.shape
    nd, _, r, _ = w_out_g.shape

    def body(dx_ref, g_ref, y_ref, yr_ref, ym_ref, w_ref, dyr_ref, dym_ref, gw_ref, dg_ref):
        @pl.when(pl.program_id(0) == 0)
        def _():
            gw_ref[...] = jnp.zeros_like(gw_ref)
            dg_ref[...] = jnp.zeros_like(dg_ref)

        dxv = dx_ref[...]
        dg_ref[...] += _bcast8(_colsum(dxv * y_ref[...]))
        dyb = (dxv * g_ref[...]).astype(BF16)
        for j in range(nd):
            src, dst = (yr_ref, dyr_ref) if j * r < d else (ym_ref, dym_ref)
            off = (j * r) % d
            dst[:, off:off + r] = _mm_nt(dyb, w_ref[j, 0])
            gw_ref[j] += _mm_tn(src[:, off:off + r], dyb)

    tok = pl.BlockSpec((tile, d), lambda i: (i, 0))
    return _call(
        body, comms, name="out_proj_bwd", grid=(s // tile,),
        in_specs=[tok, _full((1, d)), tok, tok, tok, pl.BlockSpec((nd, 1, r, d), lambda i: (0, layer, 0, 0))],
        out_specs=[tok, tok, _full((nd, r, d)), _full((SUBLANES, d))],
        out_shape=[jax.ShapeDtypeStruct((s, d), F32)] * 2 + [jax.ShapeDtypeStruct((nd, r, d), F32),
                                                             jax.ShapeDtypeStruct((SUBLANES, d), F32)],
        args=(dxo, gate, y, y_rg, y_ml, w_out_g))


def _ml_cell_bwd(dy_ml, u, cell, q, k, v, gcol, grow, mt, cs, ns, ms, ng, nh, comms=None):
    s, d = q.shape
    lc = ML_CHUNK
    nc = s // lc
    dh = d // nh

    def body(dy_ref, o_ref, z_ref, cell_ref, q_ref, k_ref, v_ref, gc_ref, gr_ref, mt_ref, cs_ref, ns_ref, ms_ref,
             ng_ref, dq_ref, dk_ref, dv_ref, dg_ref, do_ref, dz_ref, gng_ref, dc_sc, dn_sc):
        @pl.when(pl.program_id(0) == 0)
        def _():
            dc_sc[...] = jnp.zeros_like(dc_sc)
            dn_sc[...] = jnp.zeros_like(dn_sc)
            gng_ref[...] = jnp.zeros_like(gng_ref)

        gc = gc_ref[...]
        gr = gr_ref[...]
        mtv = mt_ref[...]
        lane = _iota((lc, LANES), 1)
        rowv = _iota((lc, 1), 0)
        dg_acc = jnp.zeros((lc, LANES), F32)
        for h in range(nh):
            c_h = cs_ref[0, h]
            n_h = ns_ref[0, h, 0:1, :]
            m_prev = jnp.max(ms_ref[0, h, 0:1, :], axis=1, keepdims=True)
            t = _cell_chunk(h, nh, q_ref, k_ref, v_ref, gc, gr, m_prev, c_h, n_h, m_t=_col(mtv, h))
            sl, qh, kh, vh = t["sl"], t["qh"], t["kh"], t["vh"]
            w_intra, w_inter, smat, w_state, decay = t["w_intra"], t["w_inter"], t["smat"], t["w_state"], t["decay"]
            cell_h = cell_ref[:, sl]
            o = o_ref[:, sl]
            z = z_ref[:, sl]
            dyv = dy_ref[:, sl]
            ngh = ng_ref[:, sl]
            so = _sigmoid(o)
            hg = so * cell_h
            rinv = lax.rsqrt(jnp.mean(hg * hg, axis=1, keepdims=True) + EPS)
            hn = hg * rinv
            sz = _sigmoid(z)
            dz_ref[:, sl] = (dyv * hn * ngh * (sz + z * sz * (1.0 - sz))).astype(BF16)
            dymid = dyv * z * sz
            gng_ref[:, sl] += _bcast8(_colsum(dymid * hn))
            dhn = dymid * ngh
            dhg = rinv * (dhn - hn * jnp.mean(dhn * hn, axis=1, keepdims=True))
            do_ref[:, sl] = (dhg * cell_h * so * (1.0 - so)).astype(BF16)
            dcell = dhg * so
            eneg = jnp.exp(-t["m_t"])
            aden = jnp.abs(t["den"])
            nst = jnp.maximum(aden, eneg)
            dnum = dcell / nst
            dden = jnp.where(aden > eneg, -_rowsum(cell_h * dcell) / nst * jnp.sign(t["den"]), 0.0)
            pmat = _mm_nt(dnum, vh) + dden
            damat = pmat * w_intra
            gmat = pmat * smat
            wdn = w_inter * dnum
            wdd = w_inter * dden
            dqh = _mm(damat, kh) + _mm_nt(wdn, c_h) + wdd * n_h
            dkh = _mm_tn(damat, qh)
            dvh = _mm_tn(smat, dnum)
            dw_inter = _rowsum(dnum * t["qc"]) + dden * t["qn"]
            dcn = dc_sc[h]
            dnn = dn_sc[h, 0:1, :]
            kw = kh * w_state
            dkw = _mm_nt(vh, dcn) + dnn
            dvh = dvh + _mm(kw, dcn)
            dkh = dkh + dkw * w_state
            dgst = _rowsum(dkw * kh) * w_state
            ddecay = _colsum(_rowsum(dcn * c_h)) + _rowsum(dnn * n_h)
            db_last = _colsum(dgst) + ddecay * decay
            rs_g = _rowsum(gmat)
            cs_g = _rowsum(gmat.T)
            db = rs_g - cs_g + dw_inter * w_inter - dgst + jnp.where(rowv == lc - 1, db_last, 0.0)
            dli = cs_g + dgst
            dc_sc[h] = decay * dcn + _mm_tn(qh, wdn)
            dn_sc[h] = _bcast8(decay * dnn + _colsum(qh * wdd))
            dq_ref[:, sl] = dqh
            dk_ref[:, sl] = dkh * (dh ** -0.5)
            dv_ref[:, sl] = dvh
            dg_acc = jnp.where(lane == h, dli, jnp.where(lane == 4 + h, db, dg_acc))
        dg_ref[...] = dg_acc

    rev = lambda c: nc - 1 - c
    tok = pl.BlockSpec((lc, d), lambda c: (rev(c), 0))
    g128 = pl.BlockSpec((lc, LANES), lambda c: (rev(c), 0))
    return _call(
        body, comms, name="mlstm_cell_bwd", grid=(nc,),
        in_specs=[tok, pl.BlockSpec((lc, d), lambda c: (rev(c), 3)), pl.BlockSpec((lc, d), lambda c: (rev(c), 4)),
                  tok, tok, tok, tok, g128, pl.BlockSpec((16, lc), lambda c: (0, rev(c))), g128,
                  pl.BlockSpec((1, nh, dh, dh), lambda c: (rev(c), 0, 0, 0)),
                  pl.BlockSpec((1, nh, SUBLANES, dh), lambda c: (rev(c), 0, 0, 0)),
                  pl.BlockSpec((1, nh, SUBLANES, LANES), lambda c: (rev(c), 0, 0, 0)), _full((1, d))],
        out_specs=[tok, tok, tok, g128, tok, tok, _full((SUBLANES, d))],
        out_shape=[jax.ShapeDtypeStruct((s, d), F32)] * 3 + [jax.ShapeDtypeStruct((s, LANES), F32)]
        + [jax.ShapeDtypeStruct((s, d), BF16)] * 2 + [jax.ShapeDtypeStruct((SUBLANES, d), F32)],
        scratch_shapes=[pltpu.VMEM((nh, dh, dh), F32), pltpu.VMEM((nh, SUBLANES, dh), F32)],
        args=(dy_ml, u, u, cell, q, k, v, gcol, grow, mt, cs, ns, ms, ng))


def _halo_spec(d, tile, nt, col):
    per = tile // SUBLANES
    return pl.BlockSpec((SUBLANES, d), lambda i: (jnp.maximum((nt - 1 - i) * per - 1, 0), col))


def _ml_pre_bwd(dq, dk, dv, dgates, gcol, u, q, k, v, conv_w, conv_b, w_q, w_k, w_v, wif_t, tile, comms=None):
    s, d = dq.shape
    nt = s // tile
    nh, dh, _ = w_q.shape

    def body(dq_ref, dk_ref, dv_ref, dg_ref, gc_ref, x_ref, halo_ref, q_ref, k_ref, v_ref, cw_ref, cb_ref,
             wq_ref, wk_ref, wv_ref, wift_ref,
             dx_ref, gwq_ref, gwk_ref, gwv_ref, gwif_ref, gbif_ref, gcw_ref, gcb_ref, next8):
        i = pl.program_id(0)

        @pl.when(i == 0)
        def _():
            next8[...] = jnp.zeros_like(next8)
            for ref in (gwq_ref, gwk_ref, gwv_ref, gwif_ref, gbif_ref, gcw_ref, gcb_ref):
                ref[...] = jnp.zeros_like(ref)

        x = x_ref[...]
        halo = halo_ref[...] * jnp.where(i < nt - 1, 1.0, 0.0)
        pre = _conv(x, halo, cw_ref) + cb_ref[...]
        sg = _sigmoid(pre)
        xc = pre * sg
        dgc = dg_ref[...]
        lane = _iota(dgc.shape, 1)
        utri = jnp.where(_iota((ML_CHUNK, ML_CHUNK), 0) <= _iota((ML_CHUNK, ML_CHUNK), 1), 1.0, 0.0)
        rcs = [_mm_hi(utri, dgc[c * ML_CHUNK:(c + 1) * ML_CHUNK, :]) for c in range(tile // ML_CHUNK)]
        rc = rcs[0] if len(rcs) == 1 else jnp.concatenate(rcs, axis=0)
        dgates_v = jnp.where(lane < 4, dgc, jnp.where(lane < 8, rc * (1.0 - jnp.exp(gc_ref[...])), 0.0))
        dgb = dgates_v.astype(BF16)
        gbif_ref[...] += jnp.broadcast_to(_colsum(dgates_v), gbif_ref.shape)
        ext = jnp.dot(dgb, wift_ref[...], preferred_element_type=F32)
        dqt = dq_ref[...] + ext[:, 0:d]
        dkt = dk_ref[...] + ext[:, d:2 * d]
        dvt = dv_ref[...] + ext[:, 2 * d:3 * d]
        gwif_ref[:, 0:d] += _mm_tn(dgb, q_ref[...])
        gwif_ref[:, d:2 * d] += _mm_tn(dgb, k_ref[...])
        gwif_ref[:, 2 * d:3 * d] += _mm_tn(dgb, v_ref[...])
        dxc_parts, dxv_parts = [], []
        for h in range(nh):
            sl = slice(h * dh, (h + 1) * dh)
            gwq_ref[h] += _mm_tn(xc[:, sl], dqt[:, sl])
            gwk_ref[h] += _mm_tn(xc[:, sl], dkt[:, sl])
            gwv_ref[h] += _mm_tn(x[:, sl], dvt[:, sl])
            dxc_parts.append(_mm_nt(dqt[:, sl], wq_ref[h]) + _mm_nt(dkt[:, sl], wk_ref[h]))
            dxv_parts.append(_mm_nt(dvt[:, sl], wv_ref[h]))
        dxc = jnp.concatenate(dxc_parts, axis=1)
        dxv = jnp.concatenate(dxv_parts, axis=1)
        dpre = dxc * (sg + pre * sg * (1.0 - sg))
        gcb_ref[...] += _bcast8(_colsum(dpre))
        for kk in range(CONV_WIDTH):
            gcw_ref[kk:kk + 1, :] += _colsum(dpre * _shift_down(x, CONV_WIDTH - 1 - kk, halo))
        dx_ref[...] = (dxv + _conv_bwd_x(dpre, next8[...], cw_ref)).astype(BF16)
        next8[...] = dpre[0:SUBLANES, :]

    rev = lambda i: nt - 1 - i
    tok = pl.BlockSpec((tile, d), lambda i: (rev(i), 0))
    g128 = pl.BlockSpec((tile, LANES), lambda i: (rev(i), 0))
    wsh = (nh, dh, dh)
    return _call(
        body, comms, name="mlstm_proj_bwd", grid=(nt,),
        in_specs=[tok, tok, tok, g128, g128, pl.BlockSpec((tile, d), lambda i: (rev(i), 2)), _halo_spec(d, tile, nt, 2),
                  tok, tok, tok, _full(conv_w.shape), _full((1, d)), _full(wsh), _full(wsh), _full(wsh),
                  _full(wif_t.shape)],
        out_specs=[tok, _full(wsh), _full(wsh), _full(wsh), _full((LANES, 3 * d)), _full((SUBLANES, LANES)),
                   _full((SUBLANES, d)), _full((SUBLANES, d))],
        out_shape=[jax.ShapeDtypeStruct((s, d), BF16)] + [jax.ShapeDtypeStruct(wsh, F32)] * 3
        + [jax.ShapeDtypeStruct((LANES, 3 * d), F32), jax.ShapeDtypeStruct((SUBLANES, LANES), F32),
           jax.ShapeDtypeStruct((SUBLANES, d), F32), jax.ShapeDtypeStruct((SUBLANES, d), F32)],
        scratch_shapes=[pltpu.VMEM((SUBLANES, d), F32)],
        args=(dq, dk, dv, dgates, gcol, u, u, q, k, v, conv_w, conv_b, w_q, w_k, w_v, wif_t))


def _rg_bwd(dy_rg, u, h_rg, conv_w, conv_b, w_a, b_a, w_x, b_x, lam, tile, comms=None):
    s, d = dy_rg.shape
    nt = s // tile
    nh, dh, _ = w_a.shape

    def body(dy_ref, x_ref, xhalo_ref, z_ref, h_ref, hhalo_ref, cw_ref, cb_ref, wa_ref, ba_ref, wx_ref, bx_ref, lam_ref,
             dx_ref, dz_ref, gwa_ref, gwx_ref, gba_ref, gbx_ref, glam_ref, gcw_ref, gcb_ref, next8, anext, dnext):
        i = pl.program_id(0)

        @pl.when(i == 0)
        def _():
            for ref in (next8, anext, dnext, gwa_ref, gwx_ref, gba_ref, gbx_ref, glam_ref, gcw_ref, gcb_ref):
                ref[...] = jnp.zeros_like(ref)

        inner = jnp.where(i < nt - 1, 1.0, 0.0)
        x = x_ref[...]
        halo = xhalo_ref[...] * inner
        xc = _conv(x, halo, cw_ref) + cb_ref[...]
        r, ig, sp, a, beta = _rg_gates(xc, wa_ref, ba_ref, wx_ref, bx_ref, lam_ref)
        h = h_ref[...]
        row = _iota(h.shape, 0)
        hprev = jnp.where(row >= 1, pltpu.roll(h, 1, 0), hhalo_ref[SUBLANES - 1:SUBLANES, :] * inner)
        z = z_ref[...]
        sz = _sigmoid(z)
        dyv = dy_ref[...]
        dz_ref[...] = (dyv * h * (sz + z * sz * (1.0 - sz))).astype(BF16)
        a_up = jnp.where(row < tile - 1, pltpu.roll(a, tile - 1, 0), anext[0:1, :])
        acum, bsum = _scan_rev(a_up, dyv * z * sz)
        delta = bsum + acum * dnext[0:1, :]
        anext[...] = a[0:SUBLANES, :]
        dnext[...] = delta[0:SUBLANES, :]
        dla = delta * hprev * a - delta * ig * xc * (a * a / beta)
        glam_ref[...] += _bcast8(_colsum(dla * r) * (RG_C * _sigmoid(-lam_ref[...])))
        dpa = dla * (-RG_C * sp) * r * (1.0 - r)
        dpx = delta * beta * xc * ig * (1.0 - ig)
        gba_ref[...] += _bcast8(_colsum(dpa))
        gbx_ref[...] += _bcast8(_colsum(dpx))
        parts = []
        for hh in range(nh):
            sl = slice(hh * dh, (hh + 1) * dh)
            gwa_ref[hh] += _mm_tn(xc[:, sl], dpa[:, sl])
            gwx_ref[hh] += _mm_tn(xc[:, sl], dpx[:, sl])
            parts.append(_mm_nt(dpa[:, sl], wa_ref[hh]) + _mm_nt(dpx[:, sl], wx_ref[hh]))
        dxc = delta * beta * ig + jnp.concatenate(parts, axis=1)
        gcb_ref[...] += _bcast8(_colsum(dxc))
        for kk in range(CONV_WIDTH):
            gcw_ref[kk:kk + 1, :] += _colsum(dxc * _shift_down(x, CONV_WIDTH - 1 - kk, halo))
        dx_ref[...] = _conv_bwd_x(dxc, next8[...], cw_ref).astype(BF16)
        next8[...] = dxc[0:SUBLANES, :]

    rev = lambda i: nt - 1 - i
    tok = pl.BlockSpec((tile, d), lambda i: (rev(i), 0))
    vec = _full((1, d))
    acc = _full((SUBLANES, d))
    wsh = (nh, dh, dh)
    return _call(
        body, comms, name="rglru_bwd", grid=(nt,),
        in_specs=[tok, tok, _halo_spec(d, tile, nt, 0), pl.BlockSpec((tile, d), lambda i: (rev(i), 1)), tok,
                  _halo_spec(d, tile, nt, 0), _full(conv_w.shape), vec, _full(wsh), vec, _full(wsh), vec, vec],
        out_specs=[tok, tok, _full(wsh), _full(wsh), acc, acc, acc, acc, acc],
        out_shape=[jax.ShapeDtypeStruct((s, d), BF16)] * 2 + [jax.ShapeDtypeStruct(wsh, F32)] * 2
        + [jax.ShapeDtypeStruct((SUBLANES, d), F32)] * 5,
        scratch_shapes=[pltpu.VMEM((SUBLANES, d), F32)] * 3,
        args=(dy_rg, u, u, u, h_rg, h_rg, conv_w, conv_b, w_a, b_a, w_x, b_x, lam))


def _segments(d, w, n_pieces, n_slots):
    bounds = sorted({k * d for k in range(n_pieces + 1)} | {j * w for j in range(n_slots + 1)})
    return [(lo // d, lo % d, lo // w, lo % w, hi - lo) for lo, hi in zip(bounds[:-1], bounds[1:])]


def _in_bwd(pieces, x, dxo, ng, scale, w_in_g, layer, tile, comms=None):
    s, d = x.shape
    nd, _, _, w = w_in_g.shape
    segs = _segments(d, w, len(pieces), nd)

    def body(*refs):
        p_refs = refs[:len(pieces)]
        x_ref, dxo_ref, ng_ref, sc_ref, w_ref, dx_ref, dsc_ref, dsh_ref, gng_ref = refs[len(pieces):]

        @pl.when(pl.program_id(0) == 0)
        def _():
            for ref in (dsc_ref, dsh_ref, gng_ref):
                ref[...] = jnp.zeros_like(ref)

        dh = jnp.zeros((tile, d), F32)
        for (kk, a, j, b, width) in segs:
            dh = dh + _mm_nt(p_refs[kk][:, a:a + width], w_ref[j, 0, :, b:b + width])
        xv = x_ref[...]
        g = ng_ref[...]
        rs = lax.rsqrt(jnp.mean(xv * xv, axis=1, keepdims=True) + EPS)
        xh = xv * rs
        dsh_ref[...] += _bcast8(_colsum(dh))
        dsc_ref[...] += _bcast8(_colsum(dh * xh * g))
        dhn = dh * (1.0 + sc_ref[...])
        gng_ref[...] += _bcast8(_colsum(dhn * xh))
        dxh = dhn * g
        dx_ref[...] = dxo_ref[...] + rs * (dxh - xh * jnp.mean(dxh * xh, axis=1, keepdims=True))

    tok = pl.BlockSpec((tile, d), lambda i: (i, 0))
    vec = _full((1, d))
    acc = _full((SUBLANES, d))
    return _call(
        body, comms, name="in_proj_bwd_x", grid=(s // tile,),
        in_specs=[tok] * len(pieces) + [tok, tok, vec, vec, pl.BlockSpec((nd, 1, d, w), lambda i: (0, layer, 0, 0))],
        out_specs=[tok, acc, acc, acc],
        out_shape=[jax.ShapeDtypeStruct((s, d), F32)] + [jax.ShapeDtypeStruct((SUBLANES, d), F32)] * 3,
        args=(*pieces, x, dxo, ng, scale, w_in_g))


def _in_bwd_w(pieces, hbf, w, slots, tile):
    s, d = hbf.shape
    nd_all = len(pieces) * d // w
    segs = [sg for sg in _segments(d, w, len(pieces), nd_all) if sg[2] in slots]

    def body(*refs):
        p_refs = refs[:len(pieces)]
        h_ref, gw_ref = refs[len(pieces):]

        @pl.when(pl.program_id(0) == 0)
        def _():
            gw_ref[...] = jnp.zeros_like(gw_ref)

        hv = h_ref[...]
        for (kk, a, j, b, width) in segs:
            gw_ref[j - slots[0], :, b:b + width] += _mm_tn(hv, p_refs[kk][:, a:a + width])

    tok = pl.BlockSpec((tile, d), lambda i: (i, 0))
    return pl.pallas_call(
        body, name="in_proj_bwd_w", grid=(s // tile,),
        in_specs=[tok] * len(pieces) + [tok],
        out_specs=_full((len(slots), d, w)),
        out_shape=jax.ShapeDtypeStruct((len(slots), d, w), F32),
        compiler_params=_params(1),
    )(*pieces, hbf)


def _ada_bwd_w(cact_col, dmod, nd):
    d = cact_col.shape[0]
    w = dmod.shape[1] // nd

    def body(c_ref, m_ref, o_ref):
        o_ref[0] = c_ref[...] * m_ref[...]

    return pl.pallas_call(
        body, name="adaln_bwd_w", grid=(nd,),
        in_specs=[_full((d, 1)), pl.BlockSpec((1, w), lambda j: (0, j))],
        out_specs=pl.BlockSpec((1, d, w), lambda j: (j, 0, 0)),
        out_shape=jax.ShapeDtypeStruct((nd, d, w), F32),
        compiler_params=_params(1),
    )(cact_col, dmod)


def _exchange(arrs, gather, name):
    n = len(arrs)
    outs_shape = [jax.ShapeDtypeStruct((N_DEV,) + a.shape if gather else a.shape, a.dtype) for a in arrs]

    def body(*refs):
        ins, outs = refs[:n], refs[n:2 * n]
        send_sems, recv_sems, local_sems = refs[2 * n:]
        x, y, c = (lax.axis_index(ax) for ax in MESH_AXES)
        me = 4 * x + 2 * y + c
        sends, recvs = [], []
        for flip in range(1, N_DEV):
            px = x ^ ((flip >> 2) & 1)
            py = y ^ ((flip >> 1) & 1)
            pc = c ^ (flip & 1)
            peer = 4 * px + 2 * py + pc
            for kk in range(n):
                sem = kk * (N_DEV - 1) + flip - 1
                src = ins[kk] if gather else ins[kk].at[peer]
                sends.append(pltpu.make_async_remote_copy(
                    src_ref=src, dst_ref=outs[kk].at[me], send_sem=send_sems.at[sem], recv_sem=recv_sems.at[sem],
                    device_id=(px, py, pc), device_id_type=pl.DeviceIdType.MESH))
                recvs.append(pltpu.make_async_remote_copy(
                    src_ref=src, dst_ref=outs[kk].at[peer], send_sem=send_sems.at[sem], recv_sem=recv_sems.at[sem],
                    device_id=(px, py, pc), device_id_type=pl.DeviceIdType.MESH))
        for cp in sends:
            cp.start()
        local = [pltpu.make_async_copy(ins[kk] if gather else ins[kk].at[me], outs[kk].at[me], local_sems.at[kk])
                 for kk in range(n)]
        for cp in local:
            cp.start()
        for cp in recvs:
            cp.wait_recv()
        for cp in sends:
            cp.wait_send()
        for cp in local:
            cp.wait()

    return pl.pallas_call(
        body, name=name,
        in_specs=[pl.BlockSpec(memory_space=pl.ANY)] * n,
        out_specs=[pl.BlockSpec(memory_space=pl.ANY)] * n,
        out_shape=outs_shape,
        scratch_shapes=[pltpu.SemaphoreType.DMA((n * (N_DEV - 1),)), pltpu.SemaphoreType.DMA((n * (N_DEV - 1),)),
                        pltpu.SemaphoreType.DMA((n,))],
    )(*arrs)


def _mesh_place():
    x, y, c = (lax.axis_index(ax) for ax in MESH_AXES)
    return x, y, c, (x, y, 1 - c), [(1 - x, y), (x, 1 - y), (1 - x, 1 - y)]


def _remote(src, dst, send_sems, recv_sems, sem, to):
    return pltpu.make_async_remote_copy(src_ref=src, dst_ref=dst, send_sem=send_sems.at[sem], recv_sem=recv_sems.at[sem],
                                        device_id=to, device_id_type=pl.DeviceIdType.MESH)


def _gather_two_level(arrs, name):
    n = len(arrs)
    per = N_DEV - 1

    def body(*refs):
        ins, outs = refs[:n], refs[n:2 * n]
        send_sems, recv_sems, local_sems = refs[2 * n:]
        x, y, c, sibling, chips = _mesh_place()

        def copy(kk, j, block, to, src=None):
            dst = outs[kk].at[4 * block[0] + 2 * block[1] + block[2]]
            return _remote(dst if src is None else src, dst, send_sems, recv_sems, kk * per + j, to)

        me = (x, y, c)
        local = [pltpu.make_async_copy(ins[kk], outs[kk].at[4 * x + 2 * y + c], local_sems.at[kk]) for kk in range(n)]
        first = []
        for j, chip in enumerate(chips):
            first += [copy(kk, 1 + j, me, (*chip, c), src=ins[kk]) for kk in range(n)]
        first += [copy(kk, 0, me, sibling, src=ins[kk]) for kk in range(n)]
        for cp in first + local:
            cp.start()
        passed = []
        for j, chip in enumerate(chips):
            for kk in range(n):
                copy(kk, 1 + j, (*chip, c), me).wait_recv()
                passed.append(copy(kk, 4 + j, (*chip, c), sibling))
                passed[-1].start()
        for kk in range(n):
            copy(kk, 0, sibling, me).wait_recv()
        for j, chip in enumerate(chips):
            for kk in range(n):
                copy(kk, 4 + j, (*chip, 1 - c), me).wait_recv()
        for cp in first + passed:
            cp.wait_send()
        for cp in local:
            cp.wait()

    return pl.pallas_call(
        body, name=name,
        in_specs=[pl.BlockSpec(memory_space=pl.ANY)] * n, out_specs=[pl.BlockSpec(memory_space=pl.ANY)] * n,
        out_shape=[jax.ShapeDtypeStruct((N_DEV,) + a.shape, a.dtype) for a in arrs],
        scratch_shapes=[pltpu.SemaphoreType.DMA((n * per,)), pltpu.SemaphoreType.DMA((n * per,)),
                        pltpu.SemaphoreType.DMA((n,))],
    )(*arrs)


N_CHIPS = N_DEV // 2


def _core_swap(arrs, name):
    n = len(arrs)

    def body(*refs):
        ins, outs = refs[:n], refs[n:2 * n]
        send_sems, recv_sems = refs[2 * n:]
        _, _, c, sibling, _ = _mesh_place()
        copies = [_remote(ins[kk].at[2 * q + (1 - c)], outs[kk].at[q], send_sems, recv_sems, kk * N_CHIPS + q, sibling)
                  for q in range(N_CHIPS) for kk in range(n)]
        for cp in copies:
            cp.start()
        for cp in copies:
            cp.wait_recv()
        for cp in copies:
            cp.wait_send()

    return pl.pallas_call(
        body, name=name,
        in_specs=[pl.BlockSpec(memory_space=pl.ANY)] * n, out_specs=[pl.BlockSpec(memory_space=pl.ANY)] * n,
        out_shape=[jax.ShapeDtypeStruct((N_CHIPS,) + a.shape[1:], a.dtype) for a in arrs],
        scratch_shapes=[pltpu.SemaphoreType.DMA((n * N_CHIPS,)), pltpu.SemaphoreType.DMA((n * N_CHIPS,))],
    )(*arrs)


def _pair_sum(a, other, parity, name):
    _, r, c = a.shape
    tr = _row_tile(r, c, 3)

    def body(p_ref, a_ref, o_ref, s_ref):
        s_ref[...] = (a_ref[...] + o_ref[...]).astype(BF16)

    return pl.pallas_call(
        body, name=name,
        grid_spec=pltpu.PrefetchScalarGridSpec(
            num_scalar_prefetch=1, grid=(N_CHIPS, r // tr),
            in_specs=[pl.BlockSpec((1, tr, c), lambda q, i, p: (2 * q + p[0], i, 0)),
                      pl.BlockSpec((1, tr, c), lambda q, i, p: (q, i, 0))],
            out_specs=pl.BlockSpec((1, tr, c), lambda q, i, p: (q, i, 0))),
        out_shape=jax.ShapeDtypeStruct((N_CHIPS, r, c), BF16),
        compiler_params=_params(2),
    )(parity, a, other)


def _chip_swap(arrs, name):
    n = len(arrs)
    per = N_CHIPS - 1

    def body(*refs):
        ins, outs = refs[:n], refs[n:2 * n]
        send_sems, recv_sems, local_sems = refs[2 * n:]
        x, y, c, _, chips = _mesh_place()
        mine = 2 * x + y
        sends = [_remote(ins[kk].at[2 * chip[0] + chip[1]], outs[kk].at[mine], send_sems, recv_sems, kk * per + j, (*chip, c))
                 for j, chip in enumerate(chips) for kk in range(n)]
        recvs = [_remote(ins[kk].at[mine], outs[kk].at[2 * chip[0] + chip[1]], send_sems, recv_sems, kk * per + j, (*chip, c))
                 for j, chip in enumerate(chips) for kk in range(n)]
        local = [pltpu.make_async_copy(ins[kk].at[mine], outs[kk].at[mine], local_sems.at[kk]) for kk in range(n)]
        for cp in sends + local:
            cp.start()
        for cp in recvs:
            cp.wait_recv()
        for cp in sends:
            cp.wait_send()
        for cp in local:
            cp.wait()

    return pl.pallas_call(
        body, name=name,
        in_specs=[pl.BlockSpec(memory_space=pl.ANY)] * n, out_specs=[pl.BlockSpec(memory_space=pl.ANY)] * n,
        out_shape=[jax.ShapeDtypeStruct(a.shape, a.dtype) for a in arrs],
        scratch_shapes=[pltpu.SemaphoreType.DMA((n * per,)), pltpu.SemaphoreType.DMA((n * per,)),
                        pltpu.SemaphoreType.DMA((n,))],
    )(*arrs)


def _adam_math(w, g, m, v):
    m = ADAM_B1 * m + (1.0 - ADAM_B1) * g
    v = ADAM_B2 * v + (1.0 - ADAM_B2) * (g * g)
    m_hat = m / (1.0 - ADAM_B1 ** ADAM_STEP)
    v_hat = v / (1.0 - ADAM_B2 ** ADAM_STEP)
    delta = -ADAM_LR * (m_hat / (jnp.sqrt(v_hat) + ADAM_EPS) + ADAM_WD * w)
    return delta, m, v


def _sum_devices(r_ref):
    acc = r_ref[0].astype(F32)
    for p in range(1, r_ref.shape[0]):
        acc = acc + r_ref[p].astype(F32)
    return acc


def _row_tile(rows, cols, n_bufs):
    budget = 24 * 1024 * 1024 // (n_bufs * 2 * cols * 4)
    t = rows
    while t > budget and t % 2 == 0 and (t // 2) % SUBLANES == 0:
        t //= 2
    return t


def _reduce_adam(recvs, w, m, v, name):
    nl, r, c = w.shape
    n_part = recvs[0].shape[0]
    tr = _row_tile(r, c, n_part * nl + 7)
    nt = r // tr

    def body(*refs):
        r_refs = refs[:nl]
        w_ref, m_ref, v_ref, g_ref, d_ref, mo_ref, vo_ref = refs[nl:]
        layer = pl.program_id(0)
        g = _sum_devices(r_refs[0])
        for ll in range(1, nl):
            g = jnp.where(layer == ll, _sum_devices(r_refs[ll]), g)
        delta, m2, v2 = _adam_math(w_ref[0], g, m_ref[0], v_ref[0])
        g_ref[0] = g
        d_ref[0] = delta
        mo_ref[0] = m2
        vo_ref[0] = v2

    def rspec(ll):
        return pl.BlockSpec((n_part, tr, c), lambda l, i: (0, jnp.where(l == ll, i, jnp.where(l < ll, 0, nt - 1)), 0))

    blk = pl.BlockSpec((1, tr, c), lambda l, i: (l, i, 0))
    return pl.pallas_call(
        body, name=name, grid=(nl, nt),
        in_specs=[rspec(ll) for ll in range(nl)] + [blk, blk, blk],
        out_specs=[blk] * 4,
        out_shape=[jax.ShapeDtypeStruct((nl, r, c), F32)] * 4,
        compiler_params=_params(2),
    )(*recvs, w, m, v)


def _sum8(recv, name):
    _, r, c = recv.shape

    def body(r_ref, o_ref):
        o_ref[...] = _sum_devices(r_ref)

    return pl.pallas_call(
        body, name=name, grid=(1,),
        in_specs=[_full(recv.shape)], out_specs=_full((r, c)),
        out_shape=jax.ShapeDtypeStruct((r, c), F32), compiler_params=_params(1),
    )(recv)


def _adam_call(w, g, m, v, name):
    r, c = w.shape

    def body(w_ref, g_ref, m_ref, v_ref, d_ref, mo_ref, vo_ref):
        delta, m2, v2 = _adam_math(w_ref[...], g_ref[...], m_ref[...], v_ref[...])
        d_ref[...] = delta
        mo_ref[...] = m2
        vo_ref[...] = v2

    return pl.pallas_call(
        body, name=name, grid=(1,),
        in_specs=[_full((r, c))] * 4, out_specs=[_full((r, c))] * 3,
        out_shape=[jax.ShapeDtypeStruct((r, c), F32)] * 3, compiler_params=_params(1),
    )(w, g, m, v)


def _tile_for(s, want):
    return min(want, s)


def _local_step_whole(x, c, target, wts):
    s, d = x.shape
    nl = wts["w_in_g"].shape[1]
    nd = wts["w_in_g"].shape[0]
    nh_ml = wts["w_qkv"].shape[2]
    t_big = _tile_for(s, 512)
    t_mid = _tile_for(s, 256)

    mod, cact = _mod_call(c, wts["w_ada_g"], wts["b_ada"])
    row = lambda a: a.reshape(1, -1)
    saved = []
    xl = x
    for l in range(nl):
        shift, scale, gate = (row(mod[l, kk * d:(kk + 1) * d]) for kk in range(3))
        u, hbf = _in_fwd(xl, row(wts["norm_g"][l]), scale, shift, wts["w_in_g"], l, t_mid)
        h_rg, y_rg = _rg_fwd(u, d, wts["rg_conv_w"][l], row(wts["rg_conv_b"][l]), wts["rg_w_a_bf"][l],
                             row(wts["rg_b_a"][l]), wts["rg_w_x_bf"][l], row(wts["rg_b_x"][l]),
                             row(wts["rg_lambda"][l]), t_mid)
        q, k, v, gcol = _ml_pre(u, d, wts["ml_conv_w"][l], row(wts["ml_conv_b"][l]), wts["w_qkv"][l, 0],
                                wts["w_qkv"][l, 1], wts["w_qkv"][l, 2], wts["wif_pad"][l], wts["bif_pad"][l], t_mid)
        grow = gcol[:, 0:16].T
        cell, y_ml, cs, ns, ms, mt = _ml_cell_fwd(q, k, v, gcol, grow, u, row(wts["ml_norm_g"][l]), nh_ml)
        x_new, y = _out_fwd(xl, y_rg, y_ml, gate, wts["w_out_g"], l, t_big)
        saved.append(dict(x=xl, u=u, hbf=hbf, h_rg=h_rg, y_rg=y_rg, q=q, k=k, v=v, gcol=gcol, grow=grow, cell=cell,
                          y_ml=y_ml, cs=cs, ns=ns, ms=ms, mt=mt, y=y, scale=scale, gate=gate))
        xl = x_new

    dx, loss_p, g_final = _loss_call(xl, row(wts["final_g"]), target, t_big)
    grads = [None] * nl
    cact_col = cact[0].reshape(d, 1)
    for l in reversed(range(nl)):
        sv = saved[l]
        dy_rg, dy_ml, gw_out, dgate = _out_bwd(dx, sv["gate"], sv["y"], sv["y_rg"], sv["y_ml"], wts["w_out_g"], l, t_big)
        dq, dk, dv, dgates, d_mlo, d_mlz, g_mlng = _ml_cell_bwd(
            dy_ml, sv["u"], sv["cell"], sv["q"], sv["k"], sv["v"], sv["gcol"], sv["grow"], sv["mt"], sv["cs"],
            sv["ns"], sv["ms"], row(wts["ml_norm_g"][l]), nh_ml)
        d_mlx, g_wq, g_wk, g_wv, g_wift, g_bif, g_mlcw, g_mlcb = _ml_pre_bwd(
            dq, dk, dv, dgates, sv["gcol"], sv["u"], sv["q"], sv["k"], sv["v"], wts["ml_conv_w"][l],
            row(wts["ml_conv_b"][l]), wts["w_qkv"][l, 0], wts["w_qkv"][l, 1], wts["w_qkv"][l, 2], wts["wift_pad"][l], t_mid)
        d_rgx, d_rgz, g_wa, g_wx, g_ba, g_bx, g_lam, g_rgcw, g_rgcb = _rg_bwd(
            dy_rg, sv["u"], sv["h_rg"], wts["rg_conv_w"][l], row(wts["rg_conv_b"][l]), wts["rg_w_a_bf"][l],
            row(wts["rg_b_a"][l]), wts["rg_w_x_bf"][l], row(wts["rg_b_x"][l]), row(wts["rg_lambda"][l]), t_mid)
        pieces = [d_rgx, d_rgz, d_mlx, d_mlo, d_mlz]
        dx, dscale, dshift, g_ng = _in_bwd(pieces, sv["x"], dx, row(wts["norm_g"][l]), sv["scale"], wts["w_in_g"], l, t_mid)
        half = nd // 2
        w_cols = wts["w_in_g"].shape[3]
        gw_in = jnp.concatenate([_in_bwd_w(pieces, sv["hbf"], w_cols, tuple(range(0, half)), t_big),
                                 _in_bwd_w(pieces, sv["hbf"], w_cols, tuple(range(half, nd)), t_big)], axis=0)
        dmod = jnp.concatenate([dshift[0:1], dscale[0:1], dgate[0:1]], axis=1)
        gw_ada = _ada_bwd_w(cact_col, dmod, nd)
        grads[l] = dict(w_ada=gw_ada, w_in=gw_in, w_out=gw_out, w_qkv=jnp.stack([g_wq, g_wk, g_wv]),
                        rg_conv_w=g_rgcw[0:CONV_WIDTH], ml_conv_w=g_mlcw[0:CONV_WIDTH], wif_t=g_wift[0:8],
                        norm_g=g_ng[0], b_ada=dmod[0], rg_conv_b=g_rgcb[0], rg_w_a=g_wa, rg_b_a=g_ba[0], rg_w_x=g_wx,
                        rg_b_x=g_bx[0], rg_lambda=g_lam[0], ml_conv_b=g_mlcb[0], ml_b_if=g_bif[0, 0:8],
                        ml_norm_g=g_mlng[0])
    return loss_p[0, 0], dx, grads, g_final[0]


REPLICATED = ("norm_g", "b_ada", "rg_conv_b", "rg_w_a", "rg_b_a", "rg_w_x", "rg_b_x", "rg_lambda", "ml_conv_b",
              "ml_b_if", "ml_norm_g", "final_g")
ROW_ALIGN = N_DEV * SUBLANES


def _to_rows(a):
    flat = a.reshape(-1)
    pad = (-flat.shape[0]) % LANES
    return jnp.pad(flat, (0, pad)).reshape(-1, LANES)


def _pack(arrays):
    rows = jnp.concatenate([_to_rows(a) for a in arrays], axis=0)
    return jnp.pad(rows, ((0, (-rows.shape[0]) % ROW_ALIGN), (0, 0)))


def _unpack(rows, like):
    out, at = [], 0
    for a in like:
        n = -(-a.size // LANES)
        out.append(rows[at:at + n].reshape(-1)[:a.size].reshape(a.shape))
        at += n
    return out


def _small_pack(rg_conv_w, ml_conv_w, ml_w_if):
    nl = rg_conv_w.shape[0]
    wif_t = jnp.swapaxes(ml_w_if, 1, 2).reshape(nl, -1, LANES)
    return jnp.concatenate([rg_conv_w, ml_conv_w, wif_t], axis=1)


def _small_unpack(p, if_rows):
    nl = p.shape[0]
    rg_cw = p[:, 0:CONV_WIDTH]
    ml_cw = p[:, CONV_WIDTH:2 * CONV_WIDTH]
    wif = jnp.swapaxes(p[:, 2 * CONV_WIDTH:].reshape(nl, 8, if_rows), 1, 2)
    return rg_cw, ml_cw, wif


def _assemble_weights(big, small, rep):
    w_ada_g, w_in_g, w_out_g, qkv_g = big
    nd, nl = small.shape[0], small.shape[1]
    d = w_in_g.shape[2]
    dh = qkv_g.shape[3]
    nh = d // dh
    rsh = qkv_g.shape[2] // (3 * nh)
    w_qkv = qkv_g.reshape(nd, nl, 3, nh, rsh, dh).transpose(1, 2, 3, 0, 4, 5).reshape(nl, 3, nh, nd * rsh, dh)
    cw = small[:, :, 0:2 * CONV_WIDTH].reshape(nd, nl, 2, CONV_WIDTH, LANES).transpose(1, 2, 3, 0, 4)
    cw = cw.reshape(nl, 2, CONV_WIDTH, nd * LANES)
    if_rows = (small.shape[2] - 2 * CONV_WIDTH) * LANES // 8
    wif_t = small[:, :, 2 * CONV_WIDTH:].reshape(nd, nl, 8, if_rows).transpose(1, 2, 0, 3).reshape(nl, 8, nd * if_rows)
    wift_pad = jnp.pad(wif_t, ((0, 0), (0, LANES - 8), (0, 0))).astype(BF16)
    wif_pad = jnp.swapaxes(wift_pad, 1, 2)
    bif_pad = jnp.pad(rep["ml_b_if"], ((0, 0), (0, LANES - 8))).reshape(nl, 1, LANES)
    wts = dict(rep)
    wts.update(w_ada_g=w_ada_g, w_in_g=w_in_g, w_out_g=w_out_g, w_qkv=w_qkv, rg_conv_w=cw[:, 0], ml_conv_w=cw[:, 1],
               wif_pad=wif_pad, wift_pad=wift_pad, bif_pad=bif_pad, rg_w_a_bf=rep["rg_w_a"].astype(BF16),
               rg_w_x_bf=rep["rg_w_x"].astype(BF16))
    return wts


def _qkv_slots(g_qkv, nd):
    three, nh, dh, _ = g_qkv.shape
    return g_qkv.reshape(three, nh, nd, dh // nd, dh).transpose(2, 0, 1, 3, 4).reshape(nd, three * nh * (dh // nd), dh)


def _small_slots(g):
    nd = N_DEV
    cw = jnp.stack([g["rg_conv_w"], g["ml_conv_w"]]).reshape(2, CONV_WIDTH, nd, LANES).transpose(2, 0, 1, 3)
    cw = cw.reshape(nd, 2 * CONV_WIDTH, LANES)
    wif = g["wif_t"].reshape(8, nd, -1).transpose(1, 0, 2).reshape(nd, -1, LANES)
    return jnp.concatenate([cw, wif], axis=1)


def _kernel_unhosted(x, c, norm_g, w_ada, b_ada, w_in, rg_conv_w, rg_conv_b, rg_w_a, rg_b_a, rg_w_x, rg_b_x, rg_lambda, ml_conv_w, ml_conv_b, ml_w_q, ml_w_k, ml_w_v, ml_w_if, ml_b_if, ml_norm_g, w_out, final_g, loss_target, m_norm_g, m_w_ada, m_b_ada, m_w_in, m_rg_conv_w, m_rg_conv_b, m_rg_w_a, m_rg_b_a, m_rg_w_x, m_rg_b_x, m_rg_lambda, m_ml_conv_w, m_ml_conv_b, m_ml_w_q, m_ml_w_k, m_ml_w_v, m_ml_w_if, m_ml_b_if, m_ml_norm_g, m_w_out, m_final_g, v_norm_g, v_w_ada, v_b_ada, v_w_in, v_rg_conv_w, v_rg_conv_b, v_rg_w_a, v_rg_b_a, v_rg_w_x, v_rg_b_x, v_rg_lambda, v_ml_conv_w, v_ml_conv_b, v_ml_w_q, v_ml_w_k, v_ml_w_v, v_ml_w_if, v_ml_b_if, v_ml_norm_g, v_w_out, v_final_g):
    given = dict(locals())
    nl = w_in.shape[0]
    rep = {n: given[n] for n in REPLICATED}

    def qkv_shard(prefix):
        return jnp.stack([given[prefix + "ml_w_q"], given[prefix + "ml_w_k"], given[prefix + "ml_w_v"]], axis=1).reshape(
            nl, -1, ml_w_q.shape[-1])

    *big, small = _gather_two_level(
        [w_ada.astype(BF16), w_in.astype(BF16), w_out.astype(BF16), qkv_shard("").astype(BF16),
         _small_pack(rg_conv_w, ml_conv_w, ml_w_if)], "gather_weights")
    wts = _assemble_weights(big, small, rep)

    loss_p, grad_x, grads, g_final = _local_step(x[0], c, loss_target[0], wts)
    loss = lax.psum(loss_p, MESH_AXES)

    keys = ("w_ada", "w_in", "w_out", "w_qkv", "small")
    parity = lax.axis_index("c").astype(jnp.int32).reshape(1)
    recv = []
    for l in range(nl):
        g = grads[l]
        parts = [g["w_ada"], g["w_in"], g["w_out"], _qkv_slots(g["w_qkv"], N_DEV), _small_slots(g)]
        other = _core_swap(parts, "core_swap_layer%d" % l)
        sums = [_pair_sum(a, o, parity, "pair_sum_%s_layer%d" % (key, l)) for key, a, o in zip(keys, parts, other)]
        recv.append(_chip_swap(sums, "chip_swap_layer%d" % l))
    shard = {"": dict(w_ada=w_ada, w_in=w_in, w_out=w_out, w_qkv=qkv_shard(""),
                      small=_small_pack(rg_conv_w, ml_conv_w, ml_w_if))}
    for p in ("m_", "v_"):
        shard[p] = dict(w_ada=given[p + "w_ada"], w_in=given[p + "w_in"], w_out=given[p + "w_out"], w_qkv=qkv_shard(p),
                        small=_small_pack(given[p + "rg_conv_w"], given[p + "ml_conv_w"], given[p + "ml_w_if"]))
    res = {}
    for ki, key in enumerate(keys):
        res[key] = _reduce_adam([recv[l][ki] for l in range(nl)], shard[""][key], shard["m_"][key], shard["v_"][key],
                                "reduce_adam_" + key)

    rep_g = dict(final_g=g_final)
    for n in REPLICATED[:-1]:
        rep_g[n] = jnp.stack([grads[l][n] for l in range(nl)])
    pack_g = _pack([rep_g[n] for n in REPLICATED])
    rows = pack_g.shape[0] // N_DEV
    mine = _sum8(_exchange([pack_g.reshape(N_DEV, rows, LANES)], False, "reduce_scatter_replicated")[0], "sum_replicated")
    g_rep = _exchange([mine], True, "gather_replicated")[0].reshape(N_DEV * rows, LANES)
    rep_like = [rep[n] for n in REPLICATED]
    d_rep, m_rep, v_rep = _adam_call(_pack(rep_like), g_rep, _pack([given["m_" + n] for n in REPLICATED]),
                                     _pack([given["v_" + n] for n in REPLICATED]), "adam_replicated")
    rep_out = [dict(zip(REPLICATED, _unpack(a, rep_like))) for a in (g_rep, d_rep, m_rep, v_rep)]

    if_rows = ml_w_if.shape[1]
    order = ("norm_g", "w_ada", "b_ada", "w_in", "rg_conv_w", "rg_conv_b", "rg_w_a", "rg_b_a", "rg_w_x", "rg_b_x",
             "rg_lambda", "ml_conv_w", "ml_conv_b", "ml_w_q", "ml_w_k", "ml_w_v", "ml_w_if", "ml_b_if", "ml_norm_g",
             "w_out", "final_g")
    outs = [loss, grad_x[None]]
    for kind in range(4):
        qkv = res["w_qkv"][kind].reshape((nl, 3) + ml_w_q.shape[1:])
        rg_cw, ml_cw, wif = _small_unpack(res["small"][kind], if_rows)
        sharded = dict(w_ada=res["w_ada"][kind], w_in=res["w_in"][kind], w_out=res["w_out"][kind], ml_w_q=qkv[:, 0],
                       ml_w_k=qkv[:, 1], ml_w_v=qkv[:, 2], rg_conv_w=rg_cw, ml_conv_w=ml_cw, ml_w_if=wif)
        for n in order:
            outs.append(sharded[n] if n in sharded else rep_out[kind][n])
    return tuple(outs)


def _slot(block):
    return 4 * block[0] + 2 * block[1] + block[2]


def _dma_sems(*counts):
    return [pltpu.SemaphoreType.DMA((n,)) for n in counts]


def _start_all(copies):
    for cp in copies:
        cp.start()


def _gather_ici_comm(arrs):
    n = len(arrs)

    def copies(ins, outs, sems):
        send_sems, recv_sems, local_sems = sems
        x, y, c, sibling, chips = _mesh_place()
        me = (x, y, c)
        peers = [(*chip, c) for chip in chips] + [sibling]
        local = [pltpu.make_async_copy(ins[kk], outs[kk].at[_slot(me)], local_sems.at[kk]) for kk in range(n)]
        sends = [_remote(ins[kk], outs[kk].at[_slot(me)], send_sems, recv_sems, kk * 4 + j, peer)
                 for j, peer in enumerate(peers) for kk in range(n)]
        recvs = [_remote(ins[kk], outs[kk].at[_slot(peer)], send_sems, recv_sems, kk * 4 + j, peer)
                 for j, peer in enumerate(peers) for kk in range(n)]
        return local, sends, recvs

    def start(ins, outs, sems):
        local, sends, _ = copies(ins, outs, sems)
        _start_all(sends + local)

    def finish(ins, outs, sems):
        local, sends, recvs = copies(ins, outs, sems)
        for cp in recvs:
            cp.wait_recv()
        for cp in sends:
            cp.wait_send()
        for cp in local:
            cp.wait()

    return _Comm(arrs, [jax.ShapeDtypeStruct((N_DEV,) + a.shape, a.dtype) for a in arrs], _dma_sems(4 * n, 4 * n, n),
                 start, finish)


def _gather_fwd_comm(bufs):
    n = len(bufs)

    def copies(ins, outs, sems):
        send_sems, recv_sems = sems
        _, _, c, sibling, chips = _mesh_place()
        sends = [_remote(ins[kk].at[_slot((*chip, c))], outs[kk].at[_slot((*chip, c))], send_sems, recv_sems, kk * 3 + j, sibling)
                 for j, chip in enumerate(chips) for kk in range(n)]
        recvs = [_remote(ins[kk].at[_slot((*chip, c))], outs[kk].at[_slot((*chip, 1 - c))], send_sems, recv_sems, kk * 3 + j, sibling)
                 for j, chip in enumerate(chips) for kk in range(n)]
        return sends, recvs

    def start(ins, outs, sems):
        _start_all(copies(ins, outs, sems)[0])

    def finish(ins, outs, sems):
        sends, recvs = copies(ins, outs, sems)
        for cp in recvs:
            cp.wait_recv()
        for cp in sends:
            cp.wait_send()

    return _Comm(bufs, [jax.ShapeDtypeStruct(a.shape, a.dtype) for a in bufs], _dma_sems(3 * n, 3 * n), start, finish,
                 aliases=[(i, i) for i in range(n)])


def _core_swap_comm(arrs):
    n = len(arrs)

    def copies(ins, outs, sems):
        send_sems, recv_sems = sems
        _, _, c, sibling, _ = _mesh_place()
        return [_remote(ins[kk].at[2 * q + (1 - c)], outs[kk].at[q], send_sems, recv_sems, kk * N_CHIPS + q, sibling)
                for q in range(N_CHIPS) for kk in range(n)]

    def start(ins, outs, sems):
        _start_all(copies(ins, outs, sems))

    def finish(ins, outs, sems):
        cps = copies(ins, outs, sems)
        for cp in cps:
            cp.wait_recv()
        for cp in cps:
            cp.wait_send()

    return _Comm(arrs, [jax.ShapeDtypeStruct((N_CHIPS,) + a.shape[1:], a.dtype) for a in arrs],
                 _dma_sems(N_CHIPS * n, N_CHIPS * n), start, finish)


def _chip_swap_comm(arrs):
    n = len(arrs)
    per = N_CHIPS - 1

    def copies(ins, outs, sems):
        send_sems, recv_sems, local_sems = sems
        x, y, c, _, chips = _mesh_place()
        mine = 2 * x + y
        sends = [_remote(ins[kk].at[2 * chip[0] + chip[1]], outs[kk].at[mine], send_sems, recv_sems, kk * per + j, (*chip, c))
                 for j, chip in enumerate(chips) for kk in range(n)]
        recvs = [_remote(ins[kk].at[mine], outs[kk].at[2 * chip[0] + chip[1]], send_sems, recv_sems, kk * per + j, (*chip, c))
                 for j, chip in enumerate(chips) for kk in range(n)]
        local = [pltpu.make_async_copy(ins[kk].at[mine], outs[kk].at[mine], local_sems.at[kk]) for kk in range(n)]
        return local, sends, recvs

    def start(ins, outs, sems):
        local, sends, _ = copies(ins, outs, sems)
        _start_all(sends + local)

    def finish(ins, outs, sems):
        local, sends, recvs = copies(ins, outs, sems)
        for cp in recvs:
            cp.wait_recv()
        for cp in sends:
            cp.wait_send()
        for cp in local:
            cp.wait()

    return _Comm(arrs, [jax.ShapeDtypeStruct(a.shape, a.dtype) for a in arrs], _dma_sems(per * n, per * n, n), start, finish)


class _Plan:
    def __init__(self):
        self.hosted, self.after = {}, {}

    def host(self, key, comm, then=None):
        self.hosted.setdefault(key, []).append(comm)
        if then is not None:
            self.after.setdefault(key, []).append(then)

    def comms(self, key):
        return self.hosted.get(key)

    def done(self, key):
        for fn in self.after.pop(key, []):
            fn()


def _layer_fwd(l, xl, mod_l, wl, rep, plan):
    s, d = xl.shape
    t_big, t_mid = _tile_for(s, 512), _tile_for(s, 256)
    nh_ml = rep["ml_b_if"].shape[1] // 2
    row = lambda a: a.reshape(1, -1)
    hosted = lambda name: plan.comms((name, l)) if plan else None
    done = lambda name: plan.done((name, l)) if plan else None
    shift, scale, gate = (row(mod_l[kk * d:(kk + 1) * d]) for kk in range(3))
    u, hbf = _in_fwd(xl, row(rep["norm_g"][l]), scale, shift, wl["w_in_g"], 0, t_mid, hosted("in_proj_fwd"))
    done("in_proj_fwd")
    h_rg, y_rg = _rg_fwd(u, d, wl["rg_conv_w"], row(rep["rg_conv_b"][l]), rep["rg_w_a_bf"][l], row(rep["rg_b_a"][l]),
                         rep["rg_w_x_bf"][l], row(rep["rg_b_x"][l]), row(rep["rg_lambda"][l]), t_mid, hosted("rglru_fwd"))
    done("rglru_fwd")
    q, k, v, gcol = _ml_pre(u, d, wl["ml_conv_w"], row(rep["ml_conv_b"][l]), wl["w_qkv"][0], wl["w_qkv"][1],
                            wl["w_qkv"][2], wl["wif_pad"], wl["bif_pad"], t_mid, hosted("mlstm_proj_fwd"))
    done("mlstm_proj_fwd")
    grow = gcol[:, 0:16].T
    cell, y_ml, cs, ns, ms, mt = _ml_cell_fwd(q, k, v, gcol, grow, u, row(rep["ml_norm_g"][l]), nh_ml,
                                              hosted("mlstm_cell_fwd"))
    done("mlstm_cell_fwd")
    x_new, y = _out_fwd(xl, y_rg, y_ml, gate, wl["w_out_g"], 0, t_big, hosted("out_proj_fwd"))
    done("out_proj_fwd")
    saved = dict(x=xl, u=u, hbf=hbf, h_rg=h_rg, y_rg=y_rg, q=q, k=k, v=v, gcol=gcol, grow=grow, cell=cell, y_ml=y_ml,
                 cs=cs, ns=ns, ms=ms, mt=mt, y=y, scale=scale, gate=gate)
    return x_new, saved


def _layer_bwd(l, dx, sv, wl, rep, cact_col, plan):
    s, d = dx.shape
    t_big, t_mid = _tile_for(s, 512), _tile_for(s, 256)
    nh_ml = rep["ml_b_if"].shape[1] // 2
    nd, _, _, w_cols = wl["w_in_g"].shape
    row = lambda a: a.reshape(1, -1)
    hosted = lambda name: plan.comms((name, l)) if plan else None
    done = lambda name: plan.done((name, l)) if plan else None
    dy_rg, dy_ml, gw_out, dgate = _out_bwd(dx, sv["gate"], sv["y"], sv["y_rg"], sv["y_ml"], wl["w_out_g"], 0, t_big,
                                           hosted("out_proj_bwd"))
    done("out_proj_bwd")
    dq, dk, dv, dgates, d_mlo, d_mlz, g_mlng = _ml_cell_bwd(
        dy_ml, sv["u"], sv["cell"], sv["q"], sv["k"], sv["v"], sv["gcol"], sv["grow"], sv["mt"], sv["cs"], sv["ns"],
        sv["ms"], row(rep["ml_norm_g"][l]), nh_ml, hosted("mlstm_cell_bwd"))
    done("mlstm_cell_bwd")
    d_mlx, g_wq, g_wk, g_wv, g_wift, g_bif, g_mlcw, g_mlcb = _ml_pre_bwd(
        dq, dk, dv, dgates, sv["gcol"], sv["u"], sv["q"], sv["k"], sv["v"], wl["ml_conv_w"], row(rep["ml_conv_b"][l]),
        wl["w_qkv"][0], wl["w_qkv"][1], wl["w_qkv"][2], wl["wift_pad"], t_mid, hosted("mlstm_proj_bwd"))
    done("mlstm_proj_bwd")
    d_rgx, d_rgz, g_wa, g_wx, g_ba, g_bx, g_lam, g_rgcw, g_rgcb = _rg_bwd(
        dy_rg, sv["u"], sv["h_rg"], wl["rg_conv_w"], row(rep["rg_conv_b"][l]), rep["rg_w_a_bf"][l], row(rep["rg_b_a"][l]),
        rep["rg_w_x_bf"][l], row(rep["rg_b_x"][l]), row(rep["rg_lambda"][l]), t_mid, hosted("rglru_bwd"))
    done("rglru_bwd")
    pieces = [d_rgx, d_rgz, d_mlx, d_mlo, d_mlz]
    dx, dscale, dshift, g_ng = _in_bwd(pieces, sv["x"], dx, row(rep["norm_g"][l]), sv["scale"], wl["w_in_g"], 0, t_mid,
                                       hosted("in_proj_bwd_x"))
    done("in_proj_bwd_x")
    half = nd // 2
    gw_in = jnp.concatenate([_in_bwd_w(pieces, sv["hbf"], w_cols, tuple(range(0, half)), t_big),
                             _in_bwd_w(pieces, sv["hbf"], w_cols, tuple(range(half, nd)), t_big)], axis=0)
    dmod = jnp.concatenate([dshift[0:1], dscale[0:1], dgate[0:1]], axis=1)
    gw_ada = _ada_bwd_w(cact_col, dmod, nd)
    grads = dict(w_ada=gw_ada, w_in=gw_in, w_out=gw_out, w_qkv=jnp.stack([g_wq, g_wk, g_wv]),
                 rg_conv_w=g_rgcw[0:CONV_WIDTH], ml_conv_w=g_mlcw[0:CONV_WIDTH], wif_t=g_wift[0:8], norm_g=g_ng[0],
                 b_ada=dmod[0], rg_conv_b=g_rgcb[0], rg_w_a=g_wa, rg_b_a=g_ba[0], rg_w_x=g_wx, rg_b_x=g_bx[0],
                 rg_lambda=g_lam[0], ml_conv_b=g_mlcb[0], ml_b_if=g_bif[0, 0:8], ml_norm_g=g_mlng[0])
    return dx, grads


def _local_step(x, c, target, wts):
    d = x.shape[1]
    nl = wts["w_in_g"].shape[1]
    mod, cact = _mod_call(c, wts["w_ada_g"], wts["b_ada"])
    wl = [dict(w_in_g=wts["w_in_g"][:, l:l + 1], w_out_g=wts["w_out_g"][:, l:l + 1], w_qkv=wts["w_qkv"][l],
               rg_conv_w=wts["rg_conv_w"][l], ml_conv_w=wts["ml_conv_w"][l], wif_pad=wts["wif_pad"][l],
               wift_pad=wts["wift_pad"][l], bif_pad=wts["bif_pad"][l]) for l in range(nl)]
    saved, xl = [], x
    for l in range(nl):
        xl, sv = _layer_fwd(l, xl, mod[l], wl[l], wts, None)
        saved.append(sv)
    dx, loss_p, g_final = _loss_call(xl, wts["final_g"].reshape(1, -1), target, _tile_for(x.shape[0], 512))
    grads = [None] * nl
    for l in reversed(range(nl)):
        dx, grads[l] = _layer_bwd(l, dx, saved[l], wl[l], wts, cact[0].reshape(d, 1), None)
    return loss_p[0, 0], dx, grads, g_final[0]


def _full_qkv(qkv_g, d):
    nd, _, rows3, dh = qkv_g.shape
    nh = d // dh
    rsh = rows3 // (3 * nh)
    return qkv_g.reshape(nd, 3, nh, rsh, dh).transpose(1, 2, 0, 3, 4).reshape(3, nh, nd * rsh, dh)


def _small_weights(small, l, ml_b_if):
    nd = small.shape[0]
    sm = small[:, l]
    cw = sm[:, 0:2 * CONV_WIDTH].reshape(nd, 2, CONV_WIDTH, LANES).transpose(1, 2, 0, 3).reshape(2, CONV_WIDTH, nd * LANES)
    if_rows = (sm.shape[1] - 2 * CONV_WIDTH) * LANES // 8
    wif_t = sm[:, 2 * CONV_WIDTH:].reshape(nd, 8, if_rows).transpose(1, 0, 2).reshape(8, nd * if_rows)
    wift_pad = jnp.pad(wif_t, ((0, LANES - 8), (0, 0))).astype(BF16)
    return dict(rg_conv_w=cw[0], ml_conv_w=cw[1], wift_pad=wift_pad, wif_pad=wift_pad.T,
                bif_pad=jnp.pad(ml_b_if[l], (0, LANES - 8)).reshape(1, LANES))


def kernel(x, c, norm_g, w_ada, b_ada, w_in, rg_conv_w, rg_conv_b, rg_w_a, rg_b_a, rg_w_x, rg_b_x, rg_lambda, ml_conv_w, ml_conv_b, ml_w_q, ml_w_k, ml_w_v, ml_w_if, ml_b_if, ml_norm_g, w_out, final_g, loss_target, m_norm_g, m_w_ada, m_b_ada, m_w_in, m_rg_conv_w, m_rg_conv_b, m_rg_w_a, m_rg_b_a, m_rg_w_x, m_rg_b_x, m_rg_lambda, m_ml_conv_w, m_ml_conv_b, m_ml_w_q, m_ml_w_k, m_ml_w_v, m_ml_w_if, m_ml_b_if, m_ml_norm_g, m_w_out, m_final_g, v_norm_g, v_w_ada, v_b_ada, v_w_in, v_rg_conv_w, v_rg_conv_b, v_rg_w_a, v_rg_b_a, v_rg_w_x, v_rg_b_x, v_rg_lambda, v_ml_conv_w, v_ml_conv_b, v_ml_w_q, v_ml_w_k, v_ml_w_v, v_ml_w_if, v_ml_b_if, v_ml_norm_g, v_w_out, v_final_g):
    given = dict(locals())
    nl = w_in.shape[0]
    d = x.shape[2]
    rep = {n: given[n] for n in REPLICATED}
    rep.update(rg_w_a_bf=rg_w_a.astype(BF16), rg_w_x_bf=rg_w_x.astype(BF16))
    bf = lambda a: a.astype(BF16)

    def qkv_shard(prefix):
        return jnp.stack([given[prefix + "ml_w_q"], given[prefix + "ml_w_k"], given[prefix + "ml_w_v"]], axis=1).reshape(
            nl, -1, ml_w_q.shape[-1])

    def small_shard(prefix):
        return _small_pack(given[prefix + "rg_conv_w"], given[prefix + "ml_conv_w"], given[prefix + "ml_w_if"])

    plan = _Plan()
    qkv = qkv_shard("")
    w_ada_g, w_in_first, small = _gather_two_level([bf(w_ada), bf(w_in[0:1]), small_shard("")], "gather_first")
    wl = [_small_weights(small, l, ml_b_if) for l in range(nl)]
    wl[0]["w_in_g"] = w_in_first

    def gather_behind(arrs, ici_host, fwd_host, then):
        ici = _gather_ici_comm(arrs)

        def pass_on():
            fwd = _gather_fwd_comm(ici.results)
            plan.host(fwd_host, fwd, lambda: then(fwd.results))

        plan.host(ici_host, ici, pass_on)

    def got_out(l):
        return lambda r: wl[l].update(w_out_g=r[0], w_qkv=_full_qkv(r[1], d))

    gather_behind([bf(w_out[0:1]), bf(qkv[0:1])], ("in_proj_fwd", 0), ("rglru_fwd", 0), got_out(0))
    for l in range(1, nl):
        gather_behind([bf(w_in[l:l + 1])], ("rglru_fwd", l - 1), ("mlstm_proj_fwd", l - 1),
                      lambda r, l=l: wl[l].update(w_in_g=r[0]))
        gather_behind([bf(w_out[l:l + 1]), bf(qkv[l:l + 1])], ("mlstm_cell_fwd", l - 1), ("out_proj_fwd", l - 1), got_out(l))

    mod, cact = _mod_call(c, w_ada_g, b_ada)
    saved, xl = [], x[0]
    for l in range(nl):
        xl, sv = _layer_fwd(l, xl, mod[l], wl[l], rep, plan)
        saved.append(sv)
    grad_x, loss_p, g_final = _loss_call(xl, final_g.reshape(1, -1), loss_target[0], _tile_for(xl.shape[0], 512))
    loss = lax.psum(loss_p[0, 0], MESH_AXES)

    keys = ("w_ada", "w_in", "w_out", "w_qkv", "small")
    parity = lax.axis_index("c").astype(jnp.int32).reshape(1)
    grads, recv = [None] * nl, [None] * nl
    cact_col = cact[0].reshape(d, 1)

    def parts_of(g):
        return [g["w_ada"], g["w_in"], g["w_out"], _qkv_slots(g["w_qkv"], N_DEV), _small_slots(g)]

    def pair_sums(l, parts, other):
        return [_pair_sum(a, o, parity, "pair_sum_%s_layer%d" % (key, l)) for key, a, o in zip(keys, parts, other)]

    def reduce_behind(l, host_layer):
        parts = parts_of(grads[l])
        swap = _core_swap_comm(parts)

        def summed():
            sums = pair_sums(l, parts, swap.results)
            big = _chip_swap_comm([sums[1]])
            rest = _chip_swap_comm([sums[0]] + sums[2:])
            plan.host(("mlstm_cell_bwd", host_layer), big)
            plan.host(("rglru_bwd", host_layer), rest,
                      lambda: recv.__setitem__(l, [rest.results[0], big.results[0]] + rest.results[1:]))

        plan.host(("out_proj_bwd", host_layer), swap, summed)

    for l in reversed(range(nl)):
        grad_x, grads[l] = _layer_bwd(l, grad_x, saved[l], wl[l], rep, cact_col, plan)
        if l > 0:
            reduce_behind(l, l - 1)
    parts = parts_of(grads[0])
    recv[0] = _chip_swap(pair_sums(0, parts, _core_swap(parts, "core_swap_last")), "chip_swap_last")

    shard = {p: dict(w_ada=given[p + "w_ada"], w_in=given[p + "w_in"], w_out=given[p + "w_out"], w_qkv=qkv_shard(p),
                     small=small_shard(p)) for p in ("", "m_", "v_")}
    res = {}
    for ki, key in enumerate(keys):
        res[key] = _reduce_adam([recv[l][ki] for l in range(nl)], shard[""][key], shard["m_"][key], shard["v_"][key],
                                "reduce_adam_" + key)

    rep_g = dict(final_g=g_final[0])
    for n in REPLICATED[:-1]:
        rep_g[n] = jnp.stack([grads[l][n] for l in range(nl)])
    pack_g = _pack([rep_g[n] for n in REPLICATED])
    rows = pack_g.shape[0] // N_DEV
    mine = _sum8(_exchange([pack_g.reshape(N_DEV, rows, LANES)], False, "reduce_scatter_replicated")[0], "sum_replicated")
    g_rep = _exchange([mine], True, "gather_replicated")[0].reshape(N_DEV * rows, LANES)
    rep_like = [given[n] for n in REPLICATED]
    d_rep, m_rep, v_rep = _adam_call(_pack(rep_like), g_rep, _pack([given["m_" + n] for n in REPLICATED]),
                                     _pack([given["v_" + n] for n in REPLICATED]), "adam_replicated")
    rep_out = [dict(zip(REPLICATED, _unpack(a, rep_like))) for a in (g_rep, d_rep, m_rep, v_rep)]

    if_rows = ml_w_if.shape[1]
    order = ("norm_g", "w_ada", "b_ada", "w_in", "rg_conv_w", "rg_conv_b", "rg_w_a", "rg_b_a", "rg_w_x", "rg_b_x",
             "rg_lambda", "ml_conv_w", "ml_conv_b", "ml_w_q", "ml_w_k", "ml_w_v", "ml_w_if", "ml_b_if", "ml_norm_g",
             "w_out", "final_g")
    outs = [loss, grad_x[None]]
    for kind in range(4):
        qkv_k = res["w_qkv"][kind].reshape((nl, 3) + ml_w_q.shape[1:])
        rg_cw, ml_cw, wif = _small_unpack(res["small"][kind], if_rows)
        sharded = dict(w_ada=res["w_ada"][kind], w_in=res["w_in"][kind], w_out=res["w_out"][kind], ml_w_q=qkv_k[:, 0],
                       ml_w_k=qkv_k[:, 1], ml_w_v=qkv_k[:, 2], rg_conv_w=rg_cw, ml_conv_w=ml_cw, ml_w_if=wif)
        for n in order:
            outs.append(sharded[n] if n in sharded else rep_out[kind][n])
    return tuple(outs)
```

```python
import functools

import jax
import jax.numpy as jnp
from jax import lax
from jax.experimental import pallas as pl
from jax.experimental.pallas import tpu as pltpu

F32 = jnp.float32
BF16 = jnp.bfloat16
MESH_AXES = ("x", "y", "c")
N_DEV = 8
EPS = 1e-6
RG_C = 8.0
ML_CHUNK = 128
CONV_WIDTH = 4
ADAM_LR = 0.001
ADAM_B1 = 0.9
ADAM_B2 = 0.999
ADAM_EPS = 1e-08
ADAM_WD = 0.01
ADAM_STEP = 10
NEG_BIG = -1e30
LANES = 128
SUBLANES = 8
VMEM_LIMIT = 56 * 1024 * 1024
HI = lax.Precision.HIGHEST


def _params(n_grid):
    return pltpu.CompilerParams(dimension_semantics=("arbitrary",) * n_grid, vmem_limit_bytes=VMEM_LIMIT)


def _mm(a, b):
    return jnp.dot(a.astype(BF16), b.astype(BF16), preferred_element_type=F32)


def _mm_nt(a, b):
    return lax.dot_general(a.astype(BF16), b.astype(BF16), (((1,), (1,)), ((), ())), preferred_element_type=F32)


def _mm_tn(a, b):
    return lax.dot_general(a.astype(BF16), b.astype(BF16), (((0,), (0,)), ((), ())), preferred_element_type=F32)


def _mm_hi(a, b):
    return jnp.dot(a, b, precision=HI, preferred_element_type=F32)


def _sigmoid(x):
    return 1.0 / (1.0 + jnp.exp(-x))


def _softplus(x):
    return jnp.maximum(x, 0.0) + jnp.log(1.0 + jnp.exp(-jnp.abs(x)))


def _neg_expm1(x):
    poly = -x * (1.0 + x * (0.5 + x * (1.0 / 6.0 + x * (1.0 / 24.0 + x * (1.0 / 120.0)))))
    return jnp.where(jnp.abs(x) < 0.05, poly, 1.0 - jnp.exp(x))


def _iota(shape, dim):
    return lax.broadcasted_iota(jnp.int32, shape, dim)


def _colsum(x):
    return jnp.sum(x, axis=0, keepdims=True)


def _rowsum(x):
    return jnp.sum(x, axis=1, keepdims=True)


def _col(x, j):
    return _rowsum(jnp.where(_iota(x.shape, 1) == j, x, 0.0))


def _row(x, j):
    return _colsum(jnp.where(_iota(x.shape, 0) == j, x, 0.0))


def _shift_down(x, j, prev8):
    if j == 0:
        return x
    t = x.shape[0]
    main = jnp.where(_iota(x.shape, 0) >= j, pltpu.roll(x, j, 0), 0.0)
    fix = jnp.where(_iota(prev8.shape, 0) < j, pltpu.roll(prev8, j, 0), 0.0)
    return jnp.concatenate([main[0:SUBLANES] + fix, main[SUBLANES:t]], axis=0)


def _shift_up(x, j, next8):
    if j == 0:
        return x
    t = x.shape[0]
    main = jnp.where(_iota(x.shape, 0) < t - j, pltpu.roll(x, t - j, 0), 0.0)
    fix = jnp.where(_iota(next8.shape, 0) >= SUBLANES - j, pltpu.roll(next8, SUBLANES - j, 0), 0.0)
    return jnp.concatenate([main[0:t - SUBLANES], main[t - SUBLANES:t] + fix], axis=0)


def _conv(x, prev8, w_ref):
    y = w_ref[CONV_WIDTH - 1:CONV_WIDTH, :] * x
    for j in range(1, CONV_WIDTH):
        y = y + w_ref[CONV_WIDTH - 1 - j:CONV_WIDTH - j, :] * _shift_down(x, j, prev8)
    return y


def _conv_bwd_x(dy, next8, w_ref):
    dx = w_ref[CONV_WIDTH - 1:CONV_WIDTH, :] * dy
    for j in range(1, CONV_WIDTH):
        dx = dx + w_ref[CONV_WIDTH - 1 - j:CONV_WIDTH - j, :] * _shift_up(dy, j, next8)
    return dx


def _scan_fwd(a, b):
    t = a.shape[0]
    row = _iota(a.shape, 0)
    d = 1
    while d < t:
        keep = row >= d
        a_s = jnp.where(keep, pltpu.roll(a, d, 0), 1.0)
        b_s = jnp.where(keep, pltpu.roll(b, d, 0), 0.0)
        b = a * b_s + b
        a = a * a_s
        d *= 2
    return a, b


def _scan_rev(a, b):
    t = a.shape[0]
    row = _iota(a.shape, 0)
    d = 1
    while d < t:
        keep = row < t - d
        a_s = jnp.where(keep, pltpu.roll(a, t - d, 0), 1.0)
        b_s = jnp.where(keep, pltpu.roll(b, t - d, 0), 0.0)
        b = a * b_s + b
        a = a * a_s
        d *= 2
    return a, b


def _blockdiag(x, w_ref, transpose_w=False):
    nh, dh, _ = w_ref.shape
    outs = []
    for h in range(nh):
        xs = x[:, h * dh:(h + 1) * dh]
        outs.append(_mm_nt(xs, w_ref[h]) if transpose_w else _mm(xs, w_ref[h]))
    return jnp.concatenate(outs, axis=1)


def _rg_gates(xc, wa_ref, ba_ref, wx_ref, bx_ref, lam_ref):
    r = _sigmoid(_blockdiag(xc, wa_ref) + ba_ref[...])
    ig = _sigmoid(_blockdiag(xc, wx_ref) + bx_ref[...])
    sp = _softplus(-lam_ref[...])
    log_a = -RG_C * r * sp
    a = jnp.exp(log_a)
    beta = jnp.sqrt(_neg_expm1(2.0 * log_a))
    return r, ig, sp, a, beta


def _bcast8(row):
    return jnp.broadcast_to(row, (SUBLANES, row.shape[1]))


def _full(shape):
    nd = len(shape)
    return pl.BlockSpec(shape, lambda *_: (0,) * nd)


class _Comm:
    def __init__(self, arrays, out_shapes, sems, start, finish, aliases=()):
        self.arrays, self.out_shapes, self.sems = list(arrays), list(out_shapes), list(sems)
        self.start, self.finish, self.aliases = start, finish, tuple(aliases)
        self.results = None


def _call(body, comms, *, name, grid, in_specs, out_specs, out_shape, args, scratch_shapes=()):
    comms = [cm for cm in (comms or []) if cm is not None]
    n_in, n_out, n_sc = len(args), len(out_shape), len(scratch_shapes)
    c_arrays = [a for cm in comms for a in cm.arrays]
    c_outs = [o for cm in comms for o in cm.out_shapes]
    c_sems = [sm for cm in comms for sm in cm.sems]
    aliases, a_at, o_at = {}, n_in, n_out
    for cm in comms:
        for (i, j) in cm.aliases:
            aliases[a_at + i] = o_at + j
        a_at += len(cm.arrays)
        o_at += len(cm.out_shapes)

    def wrapped(*refs):
        ins, c_in = refs[:n_in], refs[n_in:n_in + len(c_arrays)]
        at = n_in + len(c_arrays)
        outs, c_out = refs[at:at + n_out], refs[at + n_out:at + n_out + len(c_outs)]
        at += n_out + len(c_outs)
        scr, sems = refs[at:at + n_sc], refs[at + n_sc:]
        views, ia, io, isem = [], 0, 0, 0
        for cm in comms:
            views.append((c_in[ia:ia + len(cm.arrays)], c_out[io:io + len(cm.out_shapes)], sems[isem:isem + len(cm.sems)]))
            ia, io, isem = ia + len(cm.arrays), io + len(cm.out_shapes), isem + len(cm.sems)
        if comms:
            @pl.when(pl.program_id(0) == 0)
            def _():
                for cm, view in zip(comms, views):
                    cm.start(*view)

        body(*ins, *outs, *scr)
        if comms:
            @pl.when(pl.program_id(0) == grid[0] - 1)
            def _():
                for cm, view in zip(comms, views):
                    cm.finish(*view)

    hbm = pl.BlockSpec(memory_space=pl.ANY)
    res = pl.pallas_call(
        wrapped, name=name, grid=grid,
        in_specs=list(in_specs) + [hbm] * len(c_arrays), out_specs=list(out_specs) + [hbm] * len(c_outs),
        out_shape=list(out_shape) + c_outs, scratch_shapes=list(scratch_shapes) + c_sems,
        input_output_aliases=aliases, compiler_params=_params(len(grid)),
    )(*args, *c_arrays)
    at = n_out
    for cm in comms:
        cm.results = list(res[at:at + len(cm.out_shapes)])
        at += len(cm.out_shapes)
    return list(res[:n_out])


def _mod_call(c, w_ada_g, b_ada):
    nd, nl, d, w = w_ada_g.shape

    def body(c_ref, w_ref, b_ref, mod_ref, cact_ref):
        cv = c_ref[...]
        ca = _bcast8(cv * _sigmoid(cv))
        cact_ref[...] = ca
        mod_ref[0, 0] = _mm(ca, w_ref[0, 0]) + b_ref[0, 0]

    mod, cact = pl.pallas_call(
        body, name="adaln_mod", grid=(nl, nd),
        in_specs=[_full((1, d)),
                  pl.BlockSpec((1, 1, d, w), lambda l, j: (j, l, 0, 0)),
                  pl.BlockSpec((1, 1, 1, w), lambda l, j: (l, j, 0, 0))],
        out_specs=[pl.BlockSpec((1, 1, SUBLANES, w), lambda l, j: (l, j, 0, 0)), _full((SUBLANES, d))],
        out_shape=[jax.ShapeDtypeStruct((nl, nd, SUBLANES, w), F32), jax.ShapeDtypeStruct((SUBLANES, d), F32)],
        compiler_params=_params(2),
    )(c, w_ada_g, b_ada.reshape(nl, nd, 1, w))
    return mod[:, :, 0, :].reshape(nl, nd * w), cact


def _in_fwd(x, ng, scale, shift, w_in_g, layer, tile, comms=None):
    s, d = x.shape
    nd, _, _, w = w_in_g.shape

    def body(x_ref, ng_ref, sc_ref, sh_ref, w_ref, u_ref, h_ref):
        xv = x_ref[...]
        rs = lax.rsqrt(jnp.mean(xv * xv, axis=1, keepdims=True) + EPS)
        hb = (xv * rs * ng_ref[...] * (1.0 + sc_ref[...]) + sh_ref[...]).astype(BF16)
        h_ref[...] = hb
        for j in range(nd):
            u_ref[:, j * w:(j + 1) * w] = jnp.dot(hb, w_ref[j, 0], preferred_element_type=F32)

    return _call(
        body, comms, name="in_proj_fwd", grid=(s // tile,),
        in_specs=[pl.BlockSpec((tile, d), lambda i: (i, 0)), _full((1, d)), _full((1, d)), _full((1, d)),
                  pl.BlockSpec((nd, 1, d, w), lambda i: (0, layer, 0, 0))],
        out_specs=[pl.BlockSpec((tile, nd * w), lambda i: (i, 0)), pl.BlockSpec((tile, d), lambda i: (i, 0))],
        out_shape=[jax.ShapeDtypeStruct((s, nd * w), F32), jax.ShapeDtypeStruct((s, d), BF16)],
        args=(x, ng, scale, shift, w_in_g))


def _rg_fwd(u, d, conv_w, conv_b, w_a, b_a, w_x, b_x, lam, tile, comms=None):
    s = u.shape[0]

    def body(x_ref, z_ref, cw_ref, cb_ref, wa_ref, ba_ref, wx_ref, bx_ref, lam_ref, h_ref, y_ref, prev8, hcar):
        @pl.when(pl.program_id(0) == 0)
        def _():
            prev8[...] = jnp.zeros_like(prev8)
            hcar[...] = jnp.zeros_like(hcar)

        x = x_ref[...]
        xc = _conv(x, prev8[...], cw_ref) + cb_ref[...]
        prev8[...] = x[tile - SUBLANES:tile, :]
        _, ig, _, a, beta = _rg_gates(xc, wa_ref, ba_ref, wx_ref, bx_ref, lam_ref)
        acum, bsum = _scan_fwd(a, beta * ig * xc)
        h = bsum + acum * hcar[SUBLANES - 1:SUBLANES, :]
        hcar[...] = h[tile - SUBLANES:tile, :]
        h_ref[...] = h
        z = z_ref[...]
        y_ref[...] = h * z * _sigmoid(z)

    vec = _full((1, d))
    return _call(
        body, comms, name="rglru_fwd", grid=(s // tile,),
        in_specs=[pl.BlockSpec((tile, d), lambda i: (i, 0)), pl.BlockSpec((tile, d), lambda i: (i, 1)),
                  _full(conv_w.shape), vec, _full(w_a.shape), vec, _full(w_x.shape), vec, vec],
        out_specs=[pl.BlockSpec((tile, d), lambda i: (i, 0))] * 2,
        out_shape=[jax.ShapeDtypeStruct((s, d), F32)] * 2,
        scratch_shapes=[pltpu.VMEM((SUBLANES, d), F32), pltpu.VMEM((SUBLANES, d), F32)],
        args=(u, u, conv_w, conv_b, w_a, b_a, w_x, b_x, lam))


def _ml_pre(u, d, conv_w, conv_b, w_q, w_k, w_v, wif, bif, tile, comms=None):
    s = u.shape[0]
    nh = w_q.shape[0]

    def body(x_ref, cw_ref, cb_ref, wq_ref, wk_ref, wv_ref, wif_ref, bif_ref, q_ref, k_ref, v_ref, g_ref, prev8):
        @pl.when(pl.program_id(0) == 0)
        def _():
            prev8[...] = jnp.zeros_like(prev8)

        x = x_ref[...]
        pre = _conv(x, prev8[...], cw_ref) + cb_ref[...]
        prev8[...] = x[tile - SUBLANES:tile, :]
        xc = pre * _sigmoid(pre)
        q = _blockdiag(xc, wq_ref)
        k = _blockdiag(xc, wk_ref)
        v = _blockdiag(x, wv_ref)
        q_ref[...] = q
        k_ref[...] = k
        v_ref[...] = v
        g = _mm(q, wif_ref[0:d, :]) + _mm(k, wif_ref[d:2 * d, :]) + _mm(v, wif_ref[2 * d:3 * d, :]) + bif_ref[...]
        lane = _iota(g.shape, 1)
        gl = jnp.where(lane < 4, g, jnp.where(lane < 8, -_softplus(-g), 0.0))
        tri = jnp.where(_iota((ML_CHUNK, ML_CHUNK), 1) <= _iota((ML_CHUNK, ML_CHUNK), 0), 1.0, 0.0)
        cums = [_mm_hi(tri, gl[c * ML_CHUNK:(c + 1) * ML_CHUNK, :]) for c in range(tile // ML_CHUNK)]
        cum = cums[0] if len(cums) == 1 else jnp.concatenate(cums, axis=0)
        g_ref[...] = gl + jnp.where((lane >= 8) & (lane < 12), pltpu.roll(cum, 4, 1), 0.0)

    vec = _full((1, d))
    return _call(
        body, comms, name="mlstm_proj_fwd", grid=(s // tile,),
        in_specs=[pl.BlockSpec((tile, d), lambda i: (i, 2)), _full(conv_w.shape), vec,
                  _full(w_q.shape), _full(w_k.shape), _full(w_v.shape), _full(wif.shape), _full((1, LANES))],
        out_specs=[pl.BlockSpec((tile, d), lambda i: (i, 0))] * 3 + [pl.BlockSpec((tile, LANES), lambda i: (i, 0))],
        out_shape=[jax.ShapeDtypeStruct((s, d), F32)] * 3 + [jax.ShapeDtypeStruct((s, LANES), F32)],
        scratch_shapes=[pltpu.VMEM((SUBLANES, d), F32)],
        args=(u, conv_w, conv_b, w_q, w_k, w_v, wif, bif))


def _cell_chunk(h, nh, q_ref, k_ref, v_ref, gc, gr, m_prev, c_h, n_h, m_t=None):
    lc = ML_CHUNK
    dh = q_ref.shape[1] // nh
    sl = slice(h * dh, (h + 1) * dh)
    qh = q_ref[:, sl]
    kh = k_ref[:, sl] * (dh ** -0.5)
    vh = v_ref[:, sl]
    li_c = _col(gc, h)
    b_c = _col(gc, 8 + h)
    lib_r = _row(gr, h) - _row(gr, 8 + h)
    b_last = _colsum(jnp.where(_iota((lc, 1), 0) == lc - 1, b_c, 0.0))
    causal = _iota((lc, lc), 1) <= _iota((lc, lc), 0)
    dmat = jnp.where(causal, b_c + lib_r, NEG_BIG)
    m_inter = b_c + m_prev
    if m_t is None:
        m_t = jnp.maximum(m_inter, jnp.max(dmat, axis=1, keepdims=True))
    w_intra = jnp.exp(dmat - m_t)
    w_inter = jnp.exp(m_inter - m_t)
    amat = _mm_nt(qh, kh)
    smat = amat * w_intra
    qc = _mm(qh, c_h)
    qn = _rowsum(qh * n_h)
    den = _rowsum(smat) + w_inter * qn
    gst = b_last - b_c + li_c
    m_new = jnp.maximum(b_last + m_prev, jnp.max(gst, axis=0, keepdims=True))
    w_state = jnp.exp(gst - m_new)
    decay = jnp.exp(b_last + m_prev - m_new)
    return dict(sl=sl, qh=qh, kh=kh, vh=vh, m_t=m_t, w_intra=w_intra, w_inter=w_inter, smat=smat, qc=qc, qn=qn,
                den=den, m_new=m_new, w_state=w_state, decay=decay)


def _ml_cell_fwd(q, k, v, gcol, grow, u, ng, nh, comms=None):
    s, d = q.shape
    lc = ML_CHUNK
    nc = s // lc
    dh = d // nh

    def body(q_ref, k_ref, v_ref, gc_ref, gr_ref, o_ref, z_ref, ng_ref,
             cell_ref, y_ref, cs_ref, ns_ref, ms_ref, mt_ref, c_sc, n_sc, m_sc):
        @pl.when(pl.program_id(0) == 0)
        def _():
            c_sc[...] = jnp.zeros_like(c_sc)
            n_sc[...] = jnp.zeros_like(n_sc)
            m_sc[...] = jnp.zeros_like(m_sc)

        gc = gc_ref[...]
        gr = gr_ref[...]
        lane = _iota((lc, LANES), 1)
        mt_acc = jnp.zeros((lc, LANES), F32)
        for h in range(nh):
            c_h = c_sc[h]
            n_h = n_sc[h, 0:1, :]
            m_prev = jnp.max(m_sc[h, 0:1, :], axis=1, keepdims=True)
            cs_ref[0, h] = c_h
            ns_ref[0, h] = n_sc[h]
            ms_ref[0, h] = m_sc[h]
            t = _cell_chunk(h, nh, q_ref, k_ref, v_ref, gc, gr, m_prev, c_h, n_h)
            sl = t["sl"]
            num = _mm(t["smat"], t["vh"]) + t["w_inter"] * t["qc"]
            cell_h = num / jnp.maximum(jnp.abs(t["den"]), jnp.exp(-t["m_t"]))
            mt_acc = jnp.where(lane == h, t["m_t"], mt_acc)
            kw = t["kh"] * t["w_state"]
            c_sc[h] = t["decay"] * c_h + _mm_tn(kw, t["vh"])
            n_sc[h] = _bcast8(t["decay"] * n_h + _colsum(kw))
            m_sc[h] = jnp.broadcast_to(t["m_new"], (SUBLANES, LANES))
            hg = _sigmoid(o_ref[:, sl]) * cell_h
            hn = hg * lax.rsqrt(jnp.mean(hg * hg, axis=1, keepdims=True) + EPS)
            z = z_ref[:, sl]
            cell_ref[:, sl] = cell_h
            y_ref[:, sl] = hn * ng_ref[:, sl] * z * _sigmoid(z)
        mt_ref[...] = mt_acc

    tok = pl.BlockSpec((lc, d), lambda c: (c, 0))
    return _call(
        body, comms, name="mlstm_cell_fwd", grid=(nc,),
        in_specs=[tok, tok, tok, pl.BlockSpec((lc, LANES), lambda c: (c, 0)), pl.BlockSpec((16, lc), lambda c: (0, c)),
                  pl.BlockSpec((lc, d), lambda c: (c, 3)), pl.BlockSpec((lc, d), lambda c: (c, 4)), _full((1, d))],
        out_specs=[tok, tok, pl.BlockSpec((1, nh, dh, dh), lambda c: (c, 0, 0, 0)),
                   pl.BlockSpec((1, nh, SUBLANES, dh), lambda c: (c, 0, 0, 0)),
                   pl.BlockSpec((1, nh, SUBLANES, LANES), lambda c: (c, 0, 0, 0)),
                   pl.BlockSpec((lc, LANES), lambda c: (c, 0))],
        out_shape=[jax.ShapeDtypeStruct((s, d), F32), jax.ShapeDtypeStruct((s, d), F32),
                   jax.ShapeDtypeStruct((nc, nh, dh, dh), F32), jax.ShapeDtypeStruct((nc, nh, SUBLANES, dh), F32),
                   jax.ShapeDtypeStruct((nc, nh, SUBLANES, LANES), F32), jax.ShapeDtypeStruct((s, LANES), F32)],
        scratch_shapes=[pltpu.VMEM((nh, dh, dh), F32), pltpu.VMEM((nh, SUBLANES, dh), F32),
                        pltpu.VMEM((nh, SUBLANES, LANES), F32)],
        args=(q, k, v, gcol, grow, u, u, ng))


def _out_fwd(x, y_rg, y_ml, gate, w_out_g, layer, tile, comms=None):
    s, d = x.shape
    nd, _, r, _ = w_out_g.shape

    def body(x_ref, yr_ref, ym_ref, g_ref, w_ref, xn_ref, y_ref):
        acc = jnp.zeros((tile, d), F32)
        for j in range(nd):
            src = yr_ref if j * r < d else ym_ref
            off = (j * r) % d
            acc = acc + _mm(src[:, off:off + r], w_ref[j, 0])
        y_ref[...] = acc
        xn_ref[...] = x_ref[...] + g_ref[...] * acc

    tok = pl.BlockSpec((tile, d), lambda i: (i, 0))
    return _call(
        body, comms, name="out_proj_fwd", grid=(s // tile,),
        in_specs=[tok, tok, tok, _full((1, d)), pl.BlockSpec((nd, 1, r, d), lambda i: (0, layer, 0, 0))],
        out_specs=[tok, tok],
        out_shape=[jax.ShapeDtypeStruct((s, d), F32)] * 2,
        args=(x, y_rg, y_ml, gate, w_out_g))


def _loss_call(x, fg, target, tile):
    s, d = x.shape

    def body(x_ref, g_ref, t_ref, dx_ref, loss_ref, gg_ref):
        @pl.when(pl.program_id(0) == 0)
        def _():
            loss_ref[...] = jnp.zeros_like(loss_ref)
            gg_ref[...] = jnp.zeros_like(gg_ref)

        xv = x_ref[...]
        g = g_ref[...]
        rs = lax.rsqrt(jnp.mean(xv * xv, axis=1, keepdims=True) + EPS)
        xh = xv * rs
        e = xh * g - t_ref[...]
        loss_ref[...] += jnp.broadcast_to(_colsum(_rowsum(e * e)) * (0.5 / d), loss_ref.shape)
        dy = e * (1.0 / d)
        gg_ref[...] += _bcast8(_colsum(dy * xh))
        dxh = dy * g
        dx_ref[...] = rs * (dxh - xh * jnp.mean(dxh * xh, axis=1, keepdims=True))

    tok = pl.BlockSpec((tile, d), lambda i: (i, 0))
    return pl.pallas_call(
        body, name="final_norm_loss", grid=(s // tile,),
        in_specs=[tok, _full((1, d)), tok],
        out_specs=[tok, _full((SUBLANES, LANES)), _full((SUBLANES, d))],
        out_shape=[jax.ShapeDtypeStruct((s, d), F32), jax.ShapeDtypeStruct((SUBLANES, LANES), F32),
                   jax.ShapeDtypeStruct((SUBLANES, d), F32)],
        compiler_params=_params(1),
    )(x, fg, target)


def _out_bwd(dxo, gate, y, y_rg, y_ml, w_out_g, layer, tile, comms=None):
    s, d = dxo.shape
    nd, _, r, _ = w_out_g.shape

    def body(dx_ref, g_ref, y_ref, yr_ref, ym_ref, w_ref, dyr_ref, dym_ref, gw_ref, dg_ref):
        @pl.when(pl.program_id(0) == 0)
        def _():
            gw_ref[...] = jnp.zeros_like(gw_ref)
            dg_ref[...] = jnp.zeros_like(dg_ref)

        dxv = dx_ref[...]
        dg_ref[...] += _bcast8(_colsum(dxv * y_ref[...]))
        dyb = (dxv * g_ref[...]).astype(BF16)
        for j in range(nd):
            src, dst = (yr_ref, dyr_ref) if j * r < d else (ym_ref, dym_ref)
            off = (j * r) % d
            dst[:, off:off + r] = _mm_nt(dyb, w_ref[j, 0])
            gw_ref[j] += _mm_tn(src[:, off:off + r], dyb)

    tok = pl.BlockSpec((tile, d), lambda i: (i, 0))
    return _call(
        body, comms, name="out_proj_bwd", grid=(s // tile,),
        in_specs=[tok, _full((1, d)), tok, tok, tok, pl.BlockSpec((nd, 1, r, d), lambda i: (0, layer, 0, 0))],
        out_specs=[tok, tok, _full((nd, r, d)), _full((SUBLANES, d))],
        out_shape=[jax.ShapeDtypeStruct((s, d), F32)] * 2 + [jax.ShapeDtypeStruct((nd, r, d), F32),
                                                             jax.ShapeDtypeStruct((SUBLANES, d), F32)],
        args=(dxo, gate, y, y_rg, y_ml, w_out_g))


def _ml_cell_bwd(dy_ml, u, cell, q, k, v, gcol, grow, mt, cs, ns, ms, ng, nh, comms=None):
    s, d = q.shape
    lc = ML_CHUNK
    nc = s // lc
    dh = d // nh

    def body(dy_ref, o_ref, z_ref, cell_ref, q_ref, k_ref, v_ref, gc_ref, gr_ref, mt_ref, cs_ref, ns_ref, ms_ref,
             ng_ref, dq_ref, dk_ref, dv_ref, dg_ref, do_ref, dz_ref, gng_ref, dc_sc, dn_sc):
        @pl.when(pl.program_id(0) == 0)
        def _():
            dc_sc[...] = jnp.zeros_like(dc_sc)
            dn_sc[...] = jnp.zeros_like(dn_sc)
            gng_ref[...] = jnp.zeros_like(gng_ref)

        gc = gc_ref[...]
        gr = gr_ref[...]
        mtv = mt_ref[...]
        lane = _iota((lc, LANES), 1)
        rowv = _iota((lc, 1), 0)
        dg_acc = jnp.zeros((lc, LANES), F32)
        for h in range(nh):
            c_h = cs_ref[0, h]
            n_h = ns_ref[0, h, 0:1, :]
            m_prev = jnp.max(ms_ref[0, h, 0:1, :], axis=1, keepdims=True)
            t = _cell_chunk(h, nh, q_ref, k_ref, v_ref, gc, gr, m_prev, c_h, n_h, m_t=_col(mtv, h))
            sl, qh, kh, vh = t["sl"], t["qh"], t["kh"], t["vh"]
            w_intra, w_inter, smat, w_state, decay = t["w_intra"], t["w_inter"], t["smat"], t["w_state"], t["decay"]
            cell_h = cell_ref[:, sl]
            o = o_ref[:, sl]
            z = z_ref[:, sl]
            dyv = dy_ref[:, sl]
            ngh = ng_ref[:, sl]
            so = _sigmoid(o)
            hg = so * cell_h
            rinv = lax.rsqrt(jnp.mean(hg * hg, axis=1, keepdims=True) + EPS)
            hn = hg * rinv
            sz = _sigmoid(z)
            dz_ref[:, sl] = (dyv * hn * ngh * (sz + z * sz * (1.0 - sz))).astype(BF16)
            dymid = dyv * z * sz
            gng_ref[:, sl] += _bcast8(_colsum(dymid * hn))
            dhn = dymid * ngh
            dhg = rinv * (dhn - hn * jnp.mean(dhn * hn, axis=1, keepdims=True))
            do_ref[:, sl] = (dhg * cell_h * so * (1.0 - so)).astype(BF16)
            dcell = dhg * so
            eneg = jnp.exp(-t["m_t"])
            aden = jnp.abs(t["den"])
            nst = jnp.maximum(aden, eneg)
            dnum = dcell / nst
            dden = jnp.where(aden > eneg, -_rowsum(cell_h * dcell) / nst * jnp.sign(t["den"]), 0.0)
            pmat = _mm_nt(dnum, vh) + dden
            damat = pmat * w_intra
            gmat = pmat * smat
            wdn = w_inter * dnum
            wdd = w_inter * dden
            dqh = _mm(damat, kh) + _mm_nt(wdn, c_h) + wdd * n_h
            dkh = _mm_tn(damat, qh)
            dvh = _mm_tn(smat, dnum)
            dw_inter = _rowsum(dnum * t["qc"]) + dden * t["qn"]
            dcn = dc_sc[h]
            dnn = dn_sc[h, 0:1, :]
            kw = kh * w_state
            dkw = _mm_nt(vh, dcn) + dnn
            dvh = dvh + _mm(kw, dcn)
            dkh = dkh + dkw * w_state
            dgst = _rowsum(dkw * kh) * w_state
            ddecay = _colsum(_rowsum(dcn * c_h)) + _rowsum(dnn * n_h)
            db_last = _colsum(dgst) + ddecay * decay
            rs_g = _rowsum(gmat)
            cs_g = _rowsum(gmat.T)
            db = rs_g - cs_g + dw_inter * w_inter - dgst + jnp.where(rowv == lc - 1, db_last, 0.0)
            dli = cs_g + dgst
            dc_sc[h] = decay * dcn + _mm_tn(qh, wdn)
            dn_sc[h] = _bcast8(decay * dnn + _colsum(qh * wdd))
            dq_ref[:, sl] = dqh
            dk_ref[:, sl] = dkh * (dh ** -0.5)
            dv_ref[:, sl] = dvh
            dg_acc = jnp.where(lane == h, dli, jnp.where(lane == 4 + h, db, dg_acc))
        dg_ref[...] = dg_acc

    rev = lambda c: nc - 1 - c
    tok = pl.BlockSpec((lc, d), lambda c: (rev(c), 0))
    g128 = pl.BlockSpec((lc, LANES), lambda c: (rev(c), 0))
    return _call(
        body, comms, name="mlstm_cell_bwd", grid=(nc,),
        in_specs=[tok, pl.BlockSpec((lc, d), lambda c: (rev(c), 3)), pl.BlockSpec((lc, d), lambda c: (rev(c), 4)),
                  tok, tok, tok, tok, g128, pl.BlockSpec((16, lc), lambda c: (0, rev(c))), g128,
                  pl.BlockSpec((1, nh, dh, dh), lambda c: (rev(c), 0, 0, 0)),
                  pl.BlockSpec((1, nh, SUBLANES, dh), lambda c: (rev(c), 0, 0, 0)),
                  pl.BlockSpec((1, nh, SUBLANES, LANES), lambda c: (rev(c), 0, 0, 0)), _full((1, d))],
        out_specs=[tok, tok, tok, g128, tok, tok, _full((SUBLANES, d))],
        out_shape=[jax.ShapeDtypeStruct((s, d), F32)] * 3 + [jax.ShapeDtypeStruct((s, LANES), F32)]
        + [jax.ShapeDtypeStruct((s, d), BF16)] * 2 + [jax.ShapeDtypeStruct((SUBLANES, d), F32)],
        scratch_shapes=[pltpu.VMEM((nh, dh, dh), F32), pltpu.VMEM((nh, SUBLANES, dh), F32)],
        args=(dy_ml, u, u, cell, q, k, v, gcol, grow, mt, cs, ns, ms, ng))


def _halo_spec(d, tile, nt, col):
    per = tile // SUBLANES
    return pl.BlockSpec((SUBLANES, d), lambda i: (jnp.maximum((nt - 1 - i) * per - 1, 0), col))


def _ml_pre_bwd(dq, dk, dv, dgates, gcol, u, q, k, v, conv_w, conv_b, w_q, w_k, w_v, wif_t, tile, comms=None):
    s, d = dq.shape
    nt = s // tile
    nh, dh, _ = w_q.shape

    def body(dq_ref, dk_ref, dv_ref, dg_ref, gc_ref, x_ref, halo_ref, q_ref, k_ref, v_ref, cw_ref, cb_ref,
             wq_ref, wk_ref, wv_ref, wift_ref,
             dx_ref, gwq_ref, gwk_ref, gwv_ref, gwif_ref, gbif_ref, gcw_ref, gcb_ref, next8):
        i = pl.program_id(0)

        @pl.when(i == 0)
        def _():
            next8[...] = jnp.zeros_like(next8)
            for ref in (gwq_ref, gwk_ref, gwv_ref, gwif_ref, gbif_ref, gcw_ref, gcb_ref):
                ref[...] = jnp.zeros_like(ref)

        x = x_ref[...]
        halo = halo_ref[...] * jnp.where(i < nt - 1, 1.0, 0.0)
        pre = _conv(x, halo, cw_ref) + cb_ref[...]
        sg = _sigmoid(pre)
        xc = pre * sg
        dgc = dg_ref[...]
        lane = _iota(dgc.shape, 1)
        utri = jnp.where(_iota((ML_CHUNK, ML_CHUNK), 0) <= _iota((ML_CHUNK, ML_CHUNK), 1), 1.0, 0.0)
        rcs = [_mm_hi(utri, dgc[c * ML_CHUNK:(c + 1) * ML_CHUNK, :]) for c in range(tile // ML_CHUNK)]
        rc = rcs[0] if len(rcs) == 1 else jnp.concatenate(rcs, axis=0)
        dgates_v = jnp.where(lane < 4, dgc, jnp.where(lane < 8, rc * (1.0 - jnp.exp(gc_ref[...])), 0.0))
        dgb = dgates_v.astype(BF16)
        gbif_ref[...] += jnp.broadcast_to(_colsum(dgates_v), gbif_ref.shape)
        ext = jnp.dot(dgb, wift_ref[...], preferred_element_type=F32)
        dqt = dq_ref[...] + ext[:, 0:d]
        dkt = dk_ref[...] + ext[:, d:2 * d]
        dvt = dv_ref[...] + ext[:, 2 * d:3 * d]
        gwif_ref[:, 0:d] += _mm_tn(dgb, q_ref[...])
        gwif_ref[:, d:2 * d] += _mm_tn(dgb, k_ref[...])
        gwif_ref[:, 2 * d:3 * d] += _mm_tn(dgb, v_ref[...])
        dxc_parts, dxv_parts = [], []
        for h in range(nh):
            sl = slice(h * dh, (h + 1) * dh)
            gwq_ref[h] += _mm_tn(xc[:, sl], dqt[:, sl])
            gwk_ref[h] += _mm_tn(xc[:, sl], dkt[:, sl])
            gwv_ref[h] += _mm_tn(x[:, sl], dvt[:, sl])
            dxc_parts.append(_mm_nt(dqt[:, sl], wq_ref[h]) + _mm_nt(dkt[:, sl], wk_ref[h]))
            dxv_parts.append(_mm_nt(dvt[:, sl], wv_ref[h]))
        dxc = jnp.concatenate(dxc_parts, axis=1)
        dxv = jnp.concatenate(dxv_parts, axis=1)
        dpre = dxc * (sg + pre * sg * (1.0 - sg))
        gcb_ref[...] += _bcast8(_colsum(dpre))
        for kk in range(CONV_WIDTH):
            gcw_ref[kk:kk + 1, :] += _colsum(dpre * _shift_down(x, CONV_WIDTH - 1 - kk, halo))
        dx_ref[...] = (dxv + _conv_bwd_x(dpre, next8[...], cw_ref)).astype(BF16)
        next8[...] = dpre[0:SUBLANES, :]

    rev = lambda i: nt - 1 - i
    tok = pl.BlockSpec((tile, d), lambda i: (rev(i), 0))
    g128 = pl.BlockSpec((tile, LANES), lambda i: (rev(i), 0))
    wsh = (nh, dh, dh)
    return _call(
        body, comms, name="mlstm_proj_bwd", grid=(nt,),
        in_specs=[tok, tok, tok, g128, g128, pl.BlockSpec((tile, d), lambda i: (rev(i), 2)), _halo_spec(d, tile, nt, 2),
                  tok, tok, tok, _full(conv_w.shape), _full((1, d)), _full(wsh), _full(wsh), _full(wsh),
                  _full(wif_t.shape)],
        out_specs=[tok, _full(wsh), _full(wsh), _full(wsh), _full((LANES, 3 * d)), _full((SUBLANES, LANES)),
                   _full((SUBLANES, d)), _full((SUBLANES, d))],
        out_shape=[jax.ShapeDtypeStruct((s, d), BF16)] + [jax.ShapeDtypeStruct(wsh, F32)] * 3
        + [jax.ShapeDtypeStruct((LANES, 3 * d), F32), jax.ShapeDtypeStruct((SUBLANES, LANES), F32),
           jax.ShapeDtypeStruct((SUBLANES, d), F32), jax.ShapeDtypeStruct((SUBLANES, d), F32)],
        scratch_shapes=[pltpu.VMEM((SUBLANES, d), F32)],
        args=(dq, dk, dv, dgates, gcol, u, u, q, k, v, conv_w, conv_b, w_q, w_k, w_v, wif_t))


def _rg_bwd(dy_rg, u, h_rg, conv_w, conv_b, w_a, b_a, w_x, b_x, lam, tile, comms=None):
    s, d = dy_rg.shape
    nt = s // tile
    nh, dh, _ = w_a.shape

    def body(dy_ref, x_ref, xhalo_ref, z_ref, h_ref, hhalo_ref, cw_ref, cb_ref, wa_ref, ba_ref, wx_ref, bx_ref, lam_ref,
             dx_ref, dz_ref, gwa_ref, gwx_ref, gba_ref, gbx_ref, glam_ref, gcw_ref, gcb_ref, next8, anext, dnext):
        i = pl.program_id(0)

        @pl.when(i == 0)
        def _():
            for ref in (next8, anext, dnext, gwa_ref, gwx_ref, gba_ref, gbx_ref, glam_ref, gcw_ref, gcb_ref):
                ref[...] = jnp.zeros_like(ref)

        inner = jnp.where(i < nt - 1, 1.0, 0.0)
        x = x_ref[...]
        halo = xhalo_ref[...] * inner
        xc = _conv(x, halo, cw_ref) + cb_ref[...]
        r, ig, sp, a, beta = _rg_gates(xc, wa_ref, ba_ref, wx_ref, bx_ref, lam_ref)
        h = h_ref[...]
        row = _iota(h.shape, 0)
        hprev = jnp.where(row >= 1, pltpu.roll(h, 1, 0), hhalo_ref[SUBLANES - 1:SUBLANES, :] * inner)
        z = z_ref[...]
        sz = _sigmoid(z)
        dyv = dy_ref[...]
        dz_ref[...] = (dyv * h * (sz + z * sz * (1.0 - sz))).astype(BF16)
        a_up = jnp.where(row < tile - 1, pltpu.roll(a, tile - 1, 0), anext[0:1, :])
        acum, bsum = _scan_rev(a_up, dyv * z * sz)
        delta = bsum + acum * dnext[0:1, :]
        anext[...] = a[0:SUBLANES, :]
        dnext[...] = delta[0:SUBLANES, :]
        dla = delta * hprev * a - delta * ig * xc * (a * a / beta)
        glam_ref[...] += _bcast8(_colsum(dla * r) * (RG_C * _sigmoid(-lam_ref[...])))
        dpa = dla * (-RG_C * sp) * r * (1.0 - r)
        dpx = delta * beta * xc * ig * (1.0 - ig)
        gba_ref[...] += _bcast8(_colsum(dpa))
        gbx_ref[...] += _bcast8(_colsum(dpx))
        parts = []
        for hh in range(nh):
            sl = slice(hh * dh, (hh + 1) * dh)
            gwa_ref[hh] += _mm_tn(xc[:, sl], dpa[:, sl])
            gwx_ref[hh] += _mm_tn(xc[:, sl], dpx[:, sl])
            parts.append(_mm_nt(dpa[:, sl], wa_ref[hh]) + _mm_nt(dpx[:, sl], wx_ref[hh]))
        dxc = delta * beta * ig + jnp.concatenate(parts, axis=1)
        gcb_ref[...] += _bcast8(_colsum(dxc))
        for kk in range(CONV_WIDTH):
            gcw_ref[kk:kk + 1, :] += _colsum(dxc * _shift_down(x, CONV_WIDTH - 1 - kk, halo))
        dx_ref[...] = _conv_bwd_x(dxc, next8[...], cw_ref).astype(BF16)
        next8[...] = dxc[0:SUBLANES, :]

    rev = lambda i: nt - 1 - i
    tok = pl.BlockSpec((tile, d), lambda i: (rev(i), 0))
    vec = _full((1, d))
    acc = _full((SUBLANES, d))
    wsh = (nh, dh, dh)
    return _call(
        body, comms, name="rglru_bwd", grid=(nt,),
        in_specs=[tok, tok, _halo_spec(d, tile, nt, 0), pl.BlockSpec((tile, d), lambda i: (rev(i), 1)), tok,
                  _halo_spec(d, tile, nt, 0), _full(conv_w.shape), vec, _full(wsh), vec, _full(wsh), vec, vec],
        out_specs=[tok, tok, _full(wsh), _full(wsh), acc, acc, acc, acc, acc],
        out_shape=[jax.ShapeDtypeStruct((s, d), BF16)] * 2 + [jax.ShapeDtypeStruct(wsh, F32)] * 2
        + [jax.ShapeDtypeStruct((SUBLANES, d), F32)] * 5,
        scratch_shapes=[pltpu.VMEM((SUBLANES, d), F32)] * 3,
        args=(dy_rg, u, u, u, h_rg, h_rg, conv_w, conv_b, w_a, b_a, w_x, b_x, lam))


def _segments(d, w, n_pieces, n_slots):
    bounds = sorted({k * d for k in range(n_pieces + 1)} | {j * w for j in range(n_slots + 1)})
    return [(lo // d, lo % d, lo // w, lo % w, hi - lo) for lo, hi in zip(bounds[:-1], bounds[1:])]


def _in_bwd(pieces, x, dxo, ng, scale, w_in_g, layer, tile, comms=None):
    s, d = x.shape
    nd, _, _, w = w_in_g.shape
    segs = _segments(d, w, len(pieces), nd)

    def body(*refs):
        p_refs = refs[:len(pieces)]
        x_ref, dxo_ref, ng_ref, sc_ref, w_ref, dx_ref, dsc_ref, dsh_ref, gng_ref = refs[len(pieces):]

        @pl.when(pl.program_id(0) == 0)
        def _():
            for ref in (dsc_ref, dsh_ref, gng_ref):
                ref[...] = jnp.zeros_like(ref)

        dh = jnp.zeros((tile, d), F32)
        for (kk, a, j, b, width) in segs:
            dh = dh + _mm_nt(p_refs[kk][:, a:a + width], w_ref[j, 0, :, b:b + width])
        xv = x_ref[...]
        g = ng_ref[...]
        rs = lax.rsqrt(jnp.mean(xv * xv, axis=1, keepdims=True) + EPS)
        xh = xv * rs
        dsh_ref[...] += _bcast8(_colsum(dh))
        dsc_ref[...] += _bcast8(_colsum(dh * xh * g))
        dhn = dh * (1.0 + sc_ref[...])
        gng_ref[...] += _bcast8(_colsum(dhn * xh))
        dxh = dhn * g
        dx_ref[...] = dxo_ref[...] + rs * (dxh - xh * jnp.mean(dxh * xh, axis=1, keepdims=True))

    tok = pl.BlockSpec((tile, d), lambda i: (i, 0))
    vec = _full((1, d))
    acc = _full((SUBLANES, d))
    return _call(
        body, comms, name="in_proj_bwd_x", grid=(s // tile,),
        in_specs=[tok] * len(pieces) + [tok, tok, vec, vec, pl.BlockSpec((nd, 1, d, w), lambda i: (0, layer, 0, 0))],
        out_specs=[tok, acc, acc, acc],
        out_shape=[jax.ShapeDtypeStruct((s, d), F32)] + [jax.ShapeDtypeStruct((SUBLANES, d), F32)] * 3,
        args=(*pieces, x, dxo, ng, scale, w_in_g))


def _in_bwd_w(pieces, hbf, w, slots, tile):
    s, d = hbf.shape
    nd_all = len(pieces) * d // w
    segs = [sg for sg in _segments(d, w, len(pieces), nd_all) if sg[2] in slots]

    def body(*refs):
        p_refs = refs[:len(pieces)]
        h_ref, gw_ref = refs[len(pieces):]

        @pl.when(pl.program_id(0) == 0)
        def _():
            gw_ref[...] = jnp.zeros_like(gw_ref)

        hv = h_ref[...]
        for (kk, a, j, b, width) in segs:
            gw_ref[j - slots[0], :, b:b + width] += _mm_tn(hv, p_refs[kk][:, a:a + width])

    tok = pl.BlockSpec((tile, d), lambda i: (i, 0))
    return pl.pallas_call(
        body, name="in_proj_bwd_w", grid=(s // tile,),
        in_specs=[tok] * len(pieces) + [tok],
        out_specs=pl.BlockSpec((len(slots), d, w), lambda i: (0, 0, 0), pipeline_mode=pl.Buffered(1)),
        out_shape=jax.ShapeDtypeStruct((len(slots), d, w), F32),
        compiler_params=_params(1),
    )(*pieces, hbf)


def _ada_bwd_w(cact_col, dmod, nd):
    d = cact_col.shape[0]
    w = dmod.shape[1] // nd

    def body(c_ref, m_ref, o_ref):
        o_ref[0] = c_ref[...] * m_ref[...]

    return pl.pallas_call(
        body, name="adaln_bwd_w", grid=(nd,),
        in_specs=[_full((d, 1)), pl.BlockSpec((1, w), lambda j: (0, j))],
        out_specs=pl.BlockSpec((1, d, w), lambda j: (j, 0, 0)),
        out_shape=jax.ShapeDtypeStruct((nd, d, w), F32),
        compiler_params=_params(1),
    )(cact_col, dmod)


def _exchange(arrs, gather, name):
    n = len(arrs)
    outs_shape = [jax.ShapeDtypeStruct((N_DEV,) + a.shape if gather else a.shape, a.dtype) for a in arrs]

    def body(*refs):
        ins, outs = refs[:n], refs[n:2 * n]
        send_sems, recv_sems, local_sems = refs[2 * n:]
        x, y, c = (lax.axis_index(ax) for ax in MESH_AXES)
        me = 4 * x + 2 * y + c
        sends, recvs = [], []
        for flip in range(1, N_DEV):
            px = x ^ ((flip >> 2) & 1)
            py = y ^ ((flip >> 1) & 1)
            pc = c ^ (flip & 1)
            peer = 4 * px + 2 * py + pc
            for kk in range(n):
                sem = kk * (N_DEV - 1) + flip - 1
                src = ins[kk] if gather else ins[kk].at[peer]
                sends.append(pltpu.make_async_remote_copy(
                    src_ref=src, dst_ref=outs[kk].at[me], send_sem=send_sems.at[sem], recv_sem=recv_sems.at[sem],
                    device_id=(px, py, pc), device_id_type=pl.DeviceIdType.MESH))
                recvs.append(pltpu.make_async_remote_copy(
                    src_ref=src, dst_ref=outs[kk].at[peer], send_sem=send_sems.at[sem], recv_sem=recv_sems.at[sem],
                    device_id=(px, py, pc), device_id_type=pl.DeviceIdType.MESH))
        for cp in sends:
            cp.start()
        local = [pltpu.make_async_copy(ins[kk] if gather else ins[kk].at[me], outs[kk].at[me], local_sems.at[kk])
                 for kk in range(n)]
        for cp in local:
            cp.start()
        for cp in recvs:
            cp.wait_recv()
        for cp in sends:
            cp.wait_send()
        for cp in local:
            cp.wait()

    return pl.pallas_call(
        body, name=name,
        in_specs=[pl.BlockSpec(memory_space=pl.ANY)] * n,
        out_specs=[pl.BlockSpec(memory_space=pl.ANY)] * n,
        out_shape=outs_shape,
        scratch_shapes=[pltpu.SemaphoreType.DMA((n * (N_DEV - 1),)), pltpu.SemaphoreType.DMA((n * (N_DEV - 1),)),
                        pltpu.SemaphoreType.DMA((n,))],
    )(*arrs)


def _mesh_place():
    x, y, c = (lax.axis_index(ax) for ax in MESH_AXES)
    return x, y, c, (x, y, 1 - c), [(1 - x, y), (x, 1 - y), (1 - x, 1 - y)]


def _remote(src, dst, send_sems, recv_sems, sem, to):
    return pltpu.make_async_remote_copy(src_ref=src, dst_ref=dst, send_sem=send_sems.at[sem], recv_sem=recv_sems.at[sem],
                                        device_id=to, device_id_type=pl.DeviceIdType.MESH)


def _gather_two_level(arrs, name):
    n = len(arrs)
    per = N_DEV - 1

    def body(*refs):
        ins, outs = refs[:n], refs[n:2 * n]
        send_sems, recv_sems, local_sems = refs[2 * n:]
        x, y, c, sibling, chips = _mesh_place()

        def copy(kk, j, block, to, src=None):
            dst = outs[kk].at[4 * block[0] + 2 * block[1] + block[2]]
            return _remote(dst if src is None else src, dst, send_sems, recv_sems, kk * per + j, to)

        me = (x, y, c)
        local = [pltpu.make_async_copy(ins[kk], outs[kk].at[4 * x + 2 * y + c], local_sems.at[kk]) for kk in range(n)]
        first = []
        for j, chip in enumerate(chips):
            first += [copy(kk, 1 + j, me, (*chip, c), src=ins[kk]) for kk in range(n)]
        first += [copy(kk, 0, me, sibling, src=ins[kk]) for kk in range(n)]
        for cp in first + local:
            cp.start()
        passed = []
        for j, chip in enumerate(chips):
            for kk in range(n):
                copy(kk, 1 + j, (*chip, c), me).wait_recv()
                passed.append(copy(kk, 4 + j, (*chip, c), sibling))
                passed[-1].start()
        for kk in range(n):
            copy(kk, 0, sibling, me).wait_recv()
        for j, chip in enumerate(chips):
            for kk in range(n):
                copy(kk, 4 + j, (*chip, 1 - c), me).wait_recv()
        for cp in first + passed:
            cp.wait_send()
        for cp in local:
            cp.wait()

    return pl.pallas_call(
        body, name=name,
        in_specs=[pl.BlockSpec(memory_space=pl.ANY)] * n, out_specs=[pl.BlockSpec(memory_space=pl.ANY)] * n,
        out_shape=[jax.ShapeDtypeStruct((N_DEV,) + a.shape, a.dtype) for a in arrs],
        scratch_shapes=[pltpu.SemaphoreType.DMA((n * per,)), pltpu.SemaphoreType.DMA((n * per,)),
                        pltpu.SemaphoreType.DMA((n,))],
    )(*arrs)


N_CHIPS = N_DEV // 2


def _core_swap(arrs, name):
    n = len(arrs)

    def body(*refs):
        ins, outs = refs[:n], refs[n:2 * n]
        send_sems, recv_sems = refs[2 * n:]
        _, _, c, sibling, _ = _mesh_place()
        copies = [_remote(ins[kk].at[2 * q + (1 - c)], outs[kk].at[q], send_sems, recv_sems, kk * N_CHIPS + q, sibling)
                  for q in range(N_CHIPS) for kk in range(n)]
        for cp in copies:
            cp.start()
        for cp in copies:
            cp.wait_recv()
        for cp in copies:
            cp.wait_send()

    return pl.pallas_call(
        body, name=name,
        in_specs=[pl.BlockSpec(memory_space=pl.ANY)] * n, out_specs=[pl.BlockSpec(memory_space=pl.ANY)] * n,
        out_shape=[jax.ShapeDtypeStruct((N_CHIPS,) + a.shape[1:], a.dtype) for a in arrs],
        scratch_shapes=[pltpu.SemaphoreType.DMA((n * N_CHIPS,)), pltpu.SemaphoreType.DMA((n * N_CHIPS,))],
    )(*arrs)


def _pair_sum(a, other, parity, name):
    _, r, c = a.shape
    tr = _row_tile(r, c, 3)

    def body(p_ref, a_ref, o_ref, s_ref):
        s_ref[...] = (a_ref[...] + o_ref[...]).astype(BF16)

    return pl.pallas_call(
        body, name=name,
        grid_spec=pltpu.PrefetchScalarGridSpec(
            num_scalar_prefetch=1, grid=(N_CHIPS, r // tr),
            in_specs=[pl.BlockSpec((1, tr, c), lambda q, i, p: (2 * q + p[0], i, 0)),
                      pl.BlockSpec((1, tr, c), lambda q, i, p: (q, i, 0))],
            out_specs=pl.BlockSpec((1, tr, c), lambda q, i, p: (q, i, 0))),
        out_shape=jax.ShapeDtypeStruct((N_CHIPS, r, c), BF16),
        compiler_params=_params(2),
    )(parity, a, other)


def _chip_swap(arrs, name):
    n = len(arrs)
    per = N_CHIPS - 1

    def body(*refs):
        ins, outs = refs[:n], refs[n:2 * n]
        send_sems, recv_sems, local_sems = refs[2 * n:]
        x, y, c, _, chips = _mesh_place()
        mine = 2 * x + y
        sends = [_remote(ins[kk].at[2 * chip[0] + chip[1]], outs[kk].at[mine], send_sems, recv_sems, kk * per + j, (*chip, c))
                 for j, chip in enumerate(chips) for kk in range(n)]
        recvs = [_remote(ins[kk].at[mine], outs[kk].at[2 * chip[0] + chip[1]], send_sems, recv_sems, kk * per + j, (*chip, c))
                 for j, chip in enumerate(chips) for kk in range(n)]
        local = [pltpu.make_async_copy(ins[kk].at[mine], outs[kk].at[mine], local_sems.at[kk]) for kk in range(n)]
        for cp in sends + local:
            cp.start()
        for cp in recvs:
            cp.wait_recv()
        for cp in sends:
            cp.wait_send()
        for cp in local:
            cp.wait()

    return pl.pallas_call(
        body, name=name,
        in_specs=[pl.BlockSpec(memory_space=pl.ANY)] * n, out_specs=[pl.BlockSpec(memory_space=pl.ANY)] * n,
        out_shape=[jax.ShapeDtypeStruct(a.shape, a.dtype) for a in arrs],
        scratch_shapes=[pltpu.SemaphoreType.DMA((n * per,)), pltpu.SemaphoreType.DMA((n * per,)),
                        pltpu.SemaphoreType.DMA((n,))],
    )(*arrs)


def _adam_math(w, g, m, v):
    m = ADAM_B1 * m + (1.0 - ADAM_B1) * g
    v = ADAM_B2 * v + (1.0 - ADAM_B2) * (g * g)
    m_hat = m / (1.0 - ADAM_B1 ** ADAM_STEP)
    v_hat = v / (1.0 - ADAM_B2 ** ADAM_STEP)
    delta = -ADAM_LR * (m_hat / (jnp.sqrt(v_hat) + ADAM_EPS) + ADAM_WD * w)
    return delta, m, v


def _sum_devices(r_ref):
    acc = r_ref[0].astype(F32)
    for p in range(1, r_ref.shape[0]):
        acc = acc + r_ref[p].astype(F32)
    return acc


def _row_tile(rows, cols, n_bufs):
    budget = 24 * 1024 * 1024 // (n_bufs * 2 * cols * 4)
    t = rows
    while t > budget and t % 2 == 0 and (t // 2) % SUBLANES == 0:
        t //= 2
    return t


def _reduce_adam(recvs, w, m, v, name):
    nl, r, c = w.shape
    n_part = recvs[0].shape[0]
    tr = _row_tile(r, c, n_part * nl + 7)
    nt = r // tr

    def body(*refs):
        r_refs = refs[:nl]
        w_ref, m_ref, v_ref, g_ref, d_ref, mo_ref, vo_ref = refs[nl:]
        layer = pl.program_id(0)
        g = _sum_devices(r_refs[0])
        for ll in range(1, nl):
            g = jnp.where(layer == ll, _sum_devices(r_refs[ll]), g)
        delta, m2, v2 = _adam_math(w_ref[0], g, m_ref[0], v_ref[0])
        g_ref[0] = g
        d_ref[0] = delta
        mo_ref[0] = m2
        vo_ref[0] = v2

    def rspec(ll):
        return pl.BlockSpec((n_part, tr, c), lambda l, i: (0, jnp.where(l == ll, i, jnp.where(l < ll, 0, nt - 1)), 0))

    blk = pl.BlockSpec((1, tr, c), lambda l, i: (l, i, 0))
    return pl.pallas_call(
        body, name=name, grid=(nl, nt),
        in_specs=[rspec(ll) for ll in range(nl)] + [blk, blk, blk],
        out_specs=[blk] * 4,
        out_shape=[jax.ShapeDtypeStruct((nl, r, c), F32)] * 4,
        compiler_params=_params(2),
    )(*recvs, w, m, v)


def _sum8(recv, name):
    _, r, c = recv.shape

    def body(r_ref, o_ref):
        o_ref[...] = _sum_devices(r_ref)

    return pl.pallas_call(
        body, name=name, grid=(1,),
        in_specs=[_full(recv.shape)], out_specs=_full((r, c)),
        out_shape=jax.ShapeDtypeStruct((r, c), F32), compiler_params=_params(1),
    )(recv)


def _adam_call(w, g, m, v, name):
    r, c = w.shape

    def body(w_ref, g_ref, m_ref, v_ref, d_ref, mo_ref, vo_ref):
        delta, m2, v2 = _adam_math(w_ref[...], g_ref[...], m_ref[...], v_ref[...])
        d_ref[...] = delta
        mo_ref[...] = m2
        vo_ref[...] = v2

    return pl.pallas_call(
        body, name=name, grid=(1,),
        in_specs=[_full((r, c))] * 4, out_specs=[_full((r, c))] * 3,
        out_shape=[jax.ShapeDtypeStruct((r, c), F32)] * 3, compiler_params=_params(1),
    )(w, g, m, v)


def _tile_for(s, want):
    return min(want, s)


def _local_step_whole(x, c, target, wts):
    s, d = x.shape
    nl = wts["w_in_g"].shape[1]
    nd = wts["w_in_g"].shape[0]
    nh_ml = wts["w_qkv"].shape[2]
    t_big = _tile_for(s, 512)
    t_mid = _tile_for(s, 256)

    mod, cact = _mod_call(c, wts["w_ada_g"], wts["b_ada"])
    row = lambda a: a.reshape(1, -1)
    saved = []
    xl = x
    for l in range(nl):
        shift, scale, gate = (row(mod[l, kk * d:(kk + 1) * d]) for kk in range(3))
        u, hbf = _in_fwd(xl, row(wts["norm_g"][l]), scale, shift, wts["w_in_g"], l, t_mid)
        h_rg, y_rg = _rg_fwd(u, d, wts["rg_conv_w"][l], row(wts["rg_conv_b"][l]), wts["rg_w_a_bf"][l],
                             row(wts["rg_b_a"][l]), wts["rg_w_x_bf"][l], row(wts["rg_b_x"][l]),
                             row(wts["rg_lambda"][l]), t_mid)
        q, k, v, gcol = _ml_pre(u, d, wts["ml_conv_w"][l], row(wts["ml_conv_b"][l]), wts["w_qkv"][l, 0],
                                wts["w_qkv"][l, 1], wts["w_qkv"][l, 2], wts["wif_pad"][l], wts["bif_pad"][l], t_mid)
        grow = gcol[:, 0:16].T
        cell, y_ml, cs, ns, ms, mt = _ml_cell_fwd(q, k, v, gcol, grow, u, row(wts["ml_norm_g"][l]), nh_ml)
        x_new, y = _out_fwd(xl, y_rg, y_ml, gate, wts["w_out_g"], l, t_big)
        saved.append(dict(x=xl, u=u, hbf=hbf, h_rg=h_rg, y_rg=y_rg, q=q, k=k, v=v, gcol=gcol, grow=grow, cell=cell,
                          y_ml=y_ml, cs=cs, ns=ns, ms=ms, mt=mt, y=y, scale=scale, gate=gate))
        xl = x_new

    dx, loss_p, g_final = _loss_call(xl, row(wts["final_g"]), target, t_big)
    grads = [None] * nl
    cact_col = cact[0].reshape(d, 1)
    for l in reversed(range(nl)):
        sv = saved[l]
        dy_rg, dy_ml, gw_out, dgate = _out_bwd(dx, sv["gate"], sv["y"], sv["y_rg"], sv["y_ml"], wts["w_out_g"], l, t_big)
        dq, dk, dv, dgates, d_mlo, d_mlz, g_mlng = _ml_cell_bwd(
            dy_ml, sv["u"], sv["cell"], sv["q"], sv["k"], sv["v"], sv["gcol"], sv["grow"], sv["mt"], sv["cs"],
            sv["ns"], sv["ms"], row(wts["ml_norm_g"][l]), nh_ml)
        d_mlx, g_wq, g_wk, g_wv, g_wift, g_bif, g_mlcw, g_mlcb = _ml_pre_bwd(
            dq, dk, dv, dgates, sv["gcol"], sv["u"], sv["q"], sv["k"], sv["v"], wts["ml_conv_w"][l],
            row(wts["ml_conv_b"][l]), wts["w_qkv"][l, 0], wts["w_qkv"][l, 1], wts["w_qkv"][l, 2], wts["wift_pad"][l], t_mid)
        d_rgx, d_rgz, g_wa, g_wx, g_ba, g_bx, g_lam, g_rgcw, g_rgcb = _rg_bwd(
            dy_rg, sv["u"], sv["h_rg"], wts["rg_conv_w"][l], row(wts["rg_conv_b"][l]), wts["rg_w_a_bf"][l],
            row(wts["rg_b_a"][l]), wts["rg_w_x_bf"][l], row(wts["rg_b_x"][l]), row(wts["rg_lambda"][l]), t_mid)
        pieces = [d_rgx, d_rgz, d_mlx, d_mlo, d_mlz]
        dx, dscale, dshift, g_ng = _in_bwd(pieces, sv["x"], dx, row(wts["norm_g"][l]), sv["scale"], wts["w_in_g"], l, t_mid)
        half = nd // 2
        w_cols = wts["w_in_g"].shape[3]
        gw_in = jnp.concatenate([_in_bwd_w(pieces, sv["hbf"], w_cols, tuple(range(0, half)), t_big),
                                 _in_bwd_w(pieces, sv["hbf"], w_cols, tuple(range(half, nd)), t_big)], axis=0)
        dmod = jnp.concatenate([dshift[0:1], dscale[0:1], dgate[0:1]], axis=1)
        gw_ada = _ada_bwd_w(cact_col, dmod, nd)
        grads[l] = dict(w_ada=gw_ada, w_in=gw_in, w_out=gw_out, w_qkv=jnp.stack([g_wq, g_wk, g_wv]),
                        rg_conv_w=g_rgcw[0:CONV_WIDTH], ml_conv_w=g_mlcw[0:CONV_WIDTH], wif_t=g_wift[0:8],
                        norm_g=g_ng[0], b_ada=dmod[0], rg_conv_b=g_rgcb[0], rg_w_a=g_wa, rg_b_a=g_ba[0], rg_w_x=g_wx,
                        rg_b_x=g_bx[0], rg_lambda=g_lam[0], ml_conv_b=g_mlcb[0], ml_b_if=g_bif[0, 0:8],
                        ml_norm_g=g_mlng[0])
    return loss_p[0, 0], dx, grads, g_final[0]


REPLICATED = ("norm_g", "b_ada", "rg_conv_b", "rg_w_a", "rg_b_a", "rg_w_x", "rg_b_x", "rg_lambda", "ml_conv_b",
              "ml_b_if", "ml_norm_g", "final_g")
ROW_ALIGN = N_DEV * SUBLANES


def _to_rows(a):
    flat = a.reshape(-1)
    pad = (-flat.shape[0]) % LANES
    return jnp.pad(flat, (0, pad)).reshape(-1, LANES)


def _pack(arrays):
    rows = jnp.concatenate([_to_rows(a) for a in arrays], axis=0)
    return jnp.pad(rows, ((0, (-rows.shape[0]) % ROW_ALIGN), (0, 0)))


def _unpack(rows, like):
    out, at = [], 0
    for a in like:
        n = -(-a.size // LANES)
        out.append(rows[at:at + n].reshape(-1)[:a.size].reshape(a.shape))
        at += n
    return out


def _small_pack(rg_conv_w, ml_conv_w, ml_w_if):
    nl = rg_conv_w.shape[0]
    wif_t = jnp.swapaxes(ml_w_if, 1, 2).reshape(nl, -1, LANES)
    return jnp.concatenate([rg_conv_w, ml_conv_w, wif_t], axis=1)


def _small_unpack(p, if_rows):
    nl = p.shape[0]
    rg_cw = p[:, 0:CONV_WIDTH]
    ml_cw = p[:, CONV_WIDTH:2 * CONV_WIDTH]
    wif = jnp.swapaxes(p[:, 2 * CONV_WIDTH:].reshape(nl, 8, if_rows), 1, 2)
    return rg_cw, ml_cw, wif


def _assemble_weights(big, small, rep):
    w_ada_g, w_in_g, w_out_g, qkv_g = big
    nd, nl = small.shape[0], small.shape[1]
    d = w_in_g.shape[2]
    dh = qkv_g.shape[3]
    nh = d // dh
    rsh = qkv_g.shape[2] // (3 * nh)
    w_qkv = qkv_g.reshape(nd, nl, 3, nh, rsh, dh).transpose(1, 2, 3, 0, 4, 5).reshape(nl, 3, nh, nd * rsh, dh)
    cw = small[:, :, 0:2 * CONV_WIDTH].reshape(nd, nl, 2, CONV_WIDTH, LANES).transpose(1, 2, 3, 0, 4)
    cw = cw.reshape(nl, 2, CONV_WIDTH, nd * LANES)
    if_rows = (small.shape[2] - 2 * CONV_WIDTH) * LANES // 8
    wif_t = small[:, :, 2 * CONV_WIDTH:].reshape(nd, nl, 8, if_rows).transpose(1, 2, 0, 3).reshape(nl, 8, nd * if_rows)
    wift_pad = jnp.pad(wif_t, ((0, 0), (0, LANES - 8), (0, 0))).astype(BF16)
    wif_pad = jnp.swapaxes(wift_pad, 1, 2)
    bif_pad = jnp.pad(rep["ml_b_if"], ((0, 0), (0, LANES - 8))).reshape(nl, 1, LANES)
    wts = dict(rep)
    wts.update(w_ada_g=w_ada_g, w_in_g=w_in_g, w_out_g=w_out_g, w_qkv=w_qkv, rg_conv_w=cw[:, 0], ml_conv_w=cw[:, 1],
               wif_pad=wif_pad, wift_pad=wift_pad, bif_pad=bif_pad, rg_w_a_bf=rep["rg_w_a"].astype(BF16),
               rg_w_x_bf=rep["rg_w_x"].astype(BF16))
    return wts


def _qkv_slots(g_qkv, nd):
    three, nh, dh, _ = g_qkv.shape
    return g_qkv.reshape(three, nh, nd, dh // nd, dh).transpose(2, 0, 1, 3, 4).reshape(nd, three * nh * (dh // nd), dh)


def _small_slots(g):
    nd = N_DEV
    cw = jnp.stack([g["rg_conv_w"], g["ml_conv_w"]]).reshape(2, CONV_WIDTH, nd, LANES).transpose(2, 0, 1, 3)
    cw = cw.reshape(nd, 2 * CONV_WIDTH, LANES)
    wif = g["wif_t"].reshape(8, nd, -1).transpose(1, 0, 2).reshape(nd, -1, LANES)
    return jnp.concatenate([cw, wif], axis=1)


def _kernel_unhosted(x, c, norm_g, w_ada, b_ada, w_in, rg_conv_w, rg_conv_b, rg_w_a, rg_b_a, rg_w_x, rg_b_x, rg_lambda, ml_conv_w, ml_conv_b, ml_w_q, ml_w_k, ml_w_v, ml_w_if, ml_b_if, ml_norm_g, w_out, final_g, loss_target, m_norm_g, m_w_ada, m_b_ada, m_w_in, m_rg_conv_w, m_rg_conv_b, m_rg_w_a, m_rg_b_a, m_rg_w_x, m_rg_b_x, m_rg_lambda, m_ml_conv_w, m_ml_conv_b, m_ml_w_q, m_ml_w_k, m_ml_w_v, m_ml_w_if, m_ml_b_if, m_ml_norm_g, m_w_out, m_final_g, v_norm_g, v_w_ada, v_b_ada, v_w_in, v_rg_conv_w, v_rg_conv_b, v_rg_w_a, v_rg_b_a, v_rg_w_x, v_rg_b_x, v_rg_lambda, v_ml_conv_w, v_ml_conv_b, v_ml_w_q, v_ml_w_k, v_ml_w_v, v_ml_w_if, v_ml_b_if, v_ml_norm_g, v_w_out, v_final_g):
    given = dict(locals())
    nl = w_in.shape[0]
    rep = {n: given[n] for n in REPLICATED}

    def qkv_shard(prefix):
        return jnp.stack([given[prefix + "ml_w_q"], given[prefix + "ml_w_k"], given[prefix + "ml_w_v"]], axis=1).reshape(
            nl, -1, ml_w_q.shape[-1])

    *big, small = _gather_two_level(
        [w_ada.astype(BF16), w_in.astype(BF16), w_out.astype(BF16), qkv_shard("").astype(BF16),
         _small_pack(rg_conv_w, ml_conv_w, ml_w_if)], "gather_weights")
    wts = _assemble_weights(big, small, rep)

    loss_p, grad_x, grads, g_final = _local_step(x[0], c, loss_target[0], wts)
    loss = lax.psum(loss_p, MESH_AXES)

    keys = ("w_ada", "w_in", "w_out", "w_qkv", "small")
    parity = lax.axis_index("c").astype(jnp.int32).reshape(1)
    recv = []
    for l in range(nl):
        g = grads[l]
        parts = [g["w_ada"], g["w_in"], g["w_out"], _qkv_slots(g["w_qkv"], N_DEV), _small_slots(g)]
        other = _core_swap(parts, "core_swap_layer%d" % l)
        sums = [_pair_sum(a, o, parity, "pair_sum_%s_layer%d" % (key, l)) for key, a, o in zip(keys, parts, other)]
        recv.append(_chip_swap(sums, "chip_swap_layer%d" % l))
    shard = {"": dict(w_ada=w_ada, w_in=w_in, w_out=w_out, w_qkv=qkv_shard(""),
                      small=_small_pack(rg_conv_w, ml_conv_w, ml_w_if))}
    for p in ("m_", "v_"):
        shard[p] = dict(w_ada=given[p + "w_ada"], w_in=given[p + "w_in"], w_out=given[p + "w_out"], w_qkv=qkv_shard(p),
                        small=_small_pack(given[p + "rg_conv_w"], given[p + "ml_conv_w"], given[p + "ml_w_if"]))
    res = {}
    for ki, key in enumerate(keys):
        res[key] = _reduce_adam([recv[l][ki] for l in range(nl)], shard[""][key], shard["m_"][key], shard["v_"][key],
                                "reduce_adam_" + key)

    rep_g = dict(final_g=g_final)
    for n in REPLICATED[:-1]:
        rep_g[n] = jnp.stack([grads[l][n] for l in range(nl)])
    pack_g = _pack([rep_g[n] for n in REPLICATED])
    rows = pack_g.shape[0] // N_DEV
    mine = _sum8(_exchange([pack_g.reshape(N_DEV, rows, LANES)], False, "reduce_scatter_replicated")[0], "sum_replicated")
    g_rep = _exchange([mine], True, "gather_replicated")[0].reshape(N_DEV * rows, LANES)
    rep_like = [rep[n] for n in REPLICATED]
    d_rep, m_rep, v_rep = _adam_call(_pack(rep_like), g_rep, _pack([given["m_" + n] for n in REPLICATED]),
                                     _pack([given["v_" + n] for n in REPLICATED]), "adam_replicated")
    rep_out = [dict(zip(REPLICATED, _unpack(a, rep_like))) for a in (g_rep, d_rep, m_rep, v_rep)]

    if_rows = ml_w_if.shape[1]
    order = ("norm_g", "w_ada", "b_ada", "w_in", "rg_conv_w", "rg_conv_b", "rg_w_a", "rg_b_a", "rg_w_x", "rg_b_x",
             "rg_lambda", "ml_conv_w", "ml_conv_b", "ml_w_q", "ml_w_k", "ml_w_v", "ml_w_if", "ml_b_if", "ml_norm_g",
             "w_out", "final_g")
    outs = [loss, grad_x[None]]
    for kind in range(4):
        qkv = res["w_qkv"][kind].reshape((nl, 3) + ml_w_q.shape[1:])
        rg_cw, ml_cw, wif = _small_unpack(res["small"][kind], if_rows)
        sharded = dict(w_ada=res["w_ada"][kind], w_in=res["w_in"][kind], w_out=res["w_out"][kind], ml_w_q=qkv[:, 0],
                       ml_w_k=qkv[:, 1], ml_w_v=qkv[:, 2], rg_conv_w=rg_cw, ml_conv_w=ml_cw, ml_w_if=wif)
        for n in order:
            outs.append(sharded[n] if n in sharded else rep_out[kind][n])
    return tuple(outs)


def _slot(block):
    return 4 * block[0] + 2 * block[1] + block[2]


def _dma_sems(*counts):
    return [pltpu.SemaphoreType.DMA((n,)) for n in counts]


def _start_all(copies):
    for cp in copies:
        cp.start()


def _gather_ici_comm(arrs):
    n = len(arrs)

    def copies(ins, outs, sems):
        send_sems, recv_sems, local_sems = sems
        x, y, c, sibling, chips = _mesh_place()
        me = (x, y, c)
        peers = [(*chip, c) for chip in chips] + [sibling]
        local = [pltpu.make_async_copy(ins[kk], outs[kk].at[_slot(me)], local_sems.at[kk]) for kk in range(n)]
        sends = [_remote(ins[kk], outs[kk].at[_slot(me)], send_sems, recv_sems, kk * 4 + j, peer)
                 for j, peer in enumerate(peers) for kk in range(n)]
        recvs = [_remote(ins[kk], outs[kk].at[_slot(peer)], send_sems, recv_sems, kk * 4 + j, peer)
                 for j, peer in enumerate(peers) for kk in range(n)]
        return local, sends, recvs

    def start(ins, outs, sems):
        local, sends, _ = copies(ins, outs, sems)
        _start_all(sends + local)

    def finish(ins, outs, sems):
        local, sends, recvs = copies(ins, outs, sems)
        for cp in recvs:
            cp.wait_recv()
        for cp in sends:
            cp.wait_send()
        for cp in local:
            cp.wait()

    return _Comm(arrs, [jax.ShapeDtypeStruct((N_DEV,) + a.shape, a.dtype) for a in arrs], _dma_sems(4 * n, 4 * n, n),
                 start, finish)


def _gather_fwd_comm(bufs):
    n = len(bufs)

    def copies(ins, outs, sems):
        send_sems, recv_sems = sems
        _, _, c, sibling, chips = _mesh_place()
        sends = [_remote(ins[kk].at[_slot((*chip, c))], outs[kk].at[_slot((*chip, c))], send_sems, recv_sems, kk * 3 + j, sibling)
                 for j, chip in enumerate(chips) for kk in range(n)]
        recvs = [_remote(ins[kk].at[_slot((*chip, c))], outs[kk].at[_slot((*chip, 1 - c))], send_sems, recv_sems, kk * 3 + j, sibling)
                 for j, chip in enumerate(chips) for kk in range(n)]
        return sends, recvs

    def start(ins, outs, sems):
        _start_all(copies(ins, outs, sems)[0])

    def finish(ins, outs, sems):
        sends, recvs = copies(ins, outs, sems)
        for cp in recvs:
            cp.wait_recv()
        for cp in sends:
            cp.wait_send()

    return _Comm(bufs, [jax.ShapeDtypeStruct(a.shape, a.dtype) for a in bufs], _dma_sems(3 * n, 3 * n), start, finish,
                 aliases=[(i, i) for i in range(n)])


def _core_swap_comm(arrs):
    n = len(arrs)

    def copies(ins, outs, sems):
        send_sems, recv_sems = sems
        _, _, c, sibling, _ = _mesh_place()
        return [_remote(ins[kk].at[2 * q + (1 - c)], outs[kk].at[q], send_sems, recv_sems, kk * N_CHIPS + q, sibling)
                for q in range(N_CHIPS) for kk in range(n)]

    def start(ins, outs, sems):
        _start_all(copies(ins, outs, sems))

    def finish(ins, outs, sems):
        cps = copies(ins, outs, sems)
        for cp in cps:
            cp.wait_recv()
        for cp in cps:
            cp.wait_send()

    return _Comm(arrs, [jax.ShapeDtypeStruct((N_CHIPS,) + a.shape[1:], a.dtype) for a in arrs],
                 _dma_sems(N_CHIPS * n, N_CHIPS * n), start, finish)


def _chip_swap_comm(arrs):
    n = len(arrs)
    per = N_CHIPS - 1

    def copies(ins, outs, sems):
        send_sems, recv_sems, local_sems = sems
        x, y, c, _, chips = _mesh_place()
        mine = 2 * x + y
        sends = [_remote(ins[kk].at[2 * chip[0] + chip[1]], outs[kk].at[mine], send_sems, recv_sems, kk * per + j, (*chip, c))
                 for j, chip in enumerate(chips) for kk in range(n)]
        recvs = [_remote(ins[kk].at[mine], outs[kk].at[2 * chip[0] + chip[1]], send_sems, recv_sems, kk * per + j, (*chip, c))
                 for j, chip in enumerate(chips) for kk in range(n)]
        local = [pltpu.make_async_copy(ins[kk].at[mine], outs[kk].at[mine], local_sems.at[kk]) for kk in range(n)]
        return local, sends, recvs

    def start(ins, outs, sems):
        local, sends, _ = copies(ins, outs, sems)
        _start_all(sends + local)

    def finish(ins, outs, sems):
        local, sends, recvs = copies(ins, outs, sems)
        for cp in recvs:
            cp.wait_recv()
        for cp in sends:
            cp.wait_send()
        for cp in local:
            cp.wait()

    return _Comm(arrs, [jax.ShapeDtypeStruct(a.shape, a.dtype) for a in arrs], _dma_sems(per * n, per * n, n), start, finish)


def _ada_mod(c_all, w_ada, b_cols):
    nl, d, w = w_ada.shape

    def body(c_ref, w_ref, b_ref, m_ref, ca_ref):
        sub = _iota((SUBLANES, d), 0)
        cv = jnp.zeros((SUBLANES, d), F32)
        for b in range(N_DEV):
            cv = jnp.where(sub == b, c_ref[b], cv)
        ca = cv * _sigmoid(cv)
        ca_ref[...] = ca
        m_ref[...] = jnp.zeros_like(m_ref)
        for l in range(nl):
            ml = _mm_hi(ca, w_ref[l]) + b_ref[l:l + 1, :]
            for b in range(N_DEV):
                m_ref[b, l:l + 1, :] = _row(ml, b)

    return pl.pallas_call(
        body, name="adaln_mod_columns", grid=(1,),
        in_specs=[_full(c_all.shape), _full(w_ada.shape), _full(b_cols.shape)],
        out_specs=[_full((N_DEV, SUBLANES, w)), _full((SUBLANES, d))],
        out_shape=[jax.ShapeDtypeStruct((N_DEV, SUBLANES, w), F32), jax.ShapeDtypeStruct((SUBLANES, d), F32)],
        compiler_params=_params(1),
    )(c_all, w_ada, b_cols)


def _ada_grad_adam(cact_t, dmods, w, m, v):
    nl, d, wd = w.shape
    tr = _row_tile(d, wd, 8)

    def body(c_ref, dm_ref, w_ref, m_ref, v_ref, g_ref, d_ref, mo_ref, vo_ref):
        cv = c_ref[...]
        dm = dm_ref[0]
        g = _col(cv, 0) * _row(dm, 0)
        for b in range(1, N_DEV):
            g = g + _col(cv, b) * _row(dm, b)
        delta, m2, v2 = _adam_math(w_ref[0], g, m_ref[0], v_ref[0])
        g_ref[0] = g
        d_ref[0] = delta
        mo_ref[0] = m2
        vo_ref[0] = v2

    blk = pl.BlockSpec((1, tr, wd), lambda l, i: (l, i, 0))
    return pl.pallas_call(
        body, name="adaln_grad_adam", grid=(nl, d // tr),
        in_specs=[pl.BlockSpec((tr, N_DEV), lambda l, i: (i, 0)), pl.BlockSpec((1, N_DEV, wd), lambda l, i: (l, 0, 0)),
                  blk, blk, blk],
        out_specs=[blk] * 4, out_shape=[jax.ShapeDtypeStruct((nl, d, wd), F32)] * 4,
        compiler_params=_params(2),
    )(cact_t, dmods, w, m, v)


class _Plan:
    def __init__(self):
        self.hosted, self.after = {}, {}

    def host(self, key, comm, then=None):
        self.hosted.setdefault(key, []).append(comm)
        if then is not None:
            self.after.setdefault(key, []).append(then)

    def comms(self, key):
        return self.hosted.get(key)

    def done(self, key):
        for fn in self.after.pop(key, []):
            fn()


def _layer_fwd(l, xl, mod_l, wl, rep, plan):
    s, d = xl.shape
    t_big, t_mid = _tile_for(s, 512), _tile_for(s, 256)
    nh_ml = rep["ml_b_if"].shape[1] // 2
    row = lambda a: a.reshape(1, -1)
    hosted = lambda name: plan.comms((name, l)) if plan else None
    done = lambda name: plan.done((name, l)) if plan else None
    shift, scale, gate = (row(mod_l[kk * d:(kk + 1) * d]) for kk in range(3))
    u, hbf = _in_fwd(xl, row(rep["norm_g"][l]), scale, shift, wl["w_in_g"], 0, t_mid, hosted("in_proj_fwd"))
    done("in_proj_fwd")
    h_rg, y_rg = _rg_fwd(u, d, wl["rg_conv_w"], row(rep["rg_conv_b"][l]), rep["rg_w_a_bf"][l], row(rep["rg_b_a"][l]),
                         rep["rg_w_x_bf"][l], row(rep["rg_b_x"][l]), row(rep["rg_lambda"][l]), t_mid, hosted("rglru_fwd"))
    done("rglru_fwd")
    q, k, v, gcol = _ml_pre(u, d, wl["ml_conv_w"], row(rep["ml_conv_b"][l]), wl["w_qkv"][0], wl["w_qkv"][1],
                            wl["w_qkv"][2], wl["wif_pad"], wl["bif_pad"], t_mid, hosted("mlstm_proj_fwd"))
    done("mlstm_proj_fwd")
    grow = gcol[:, 0:16].T
    cell, y_ml, cs, ns, ms, mt = _ml_cell_fwd(q, k, v, gcol, grow, u, row(rep["ml_norm_g"][l]), nh_ml,
                                              hosted("mlstm_cell_fwd"))
    done("mlstm_cell_fwd")
    x_new, y = _out_fwd(xl, y_rg, y_ml, gate, wl["w_out_g"], 0, t_big, hosted("out_proj_fwd"))
    done("out_proj_fwd")
    saved = dict(x=xl, u=u, hbf=hbf, h_rg=h_rg, y_rg=y_rg, q=q, k=k, v=v, gcol=gcol, grow=grow, cell=cell, y_ml=y_ml,
                 cs=cs, ns=ns, ms=ms, mt=mt, y=y, scale=scale, gate=gate)
    return x_new, saved


def _layer_bwd(l, dx, sv, wl, rep, plan):
    s, d = dx.shape
    t_big, t_mid = _tile_for(s, 512), _tile_for(s, 256)
    nh_ml = rep["ml_b_if"].shape[1] // 2
    nd, _, _, w_cols = wl["w_in_g"].shape
    row = lambda a: a.reshape(1, -1)
    hosted = lambda name: plan.comms((name, l)) if plan else None
    done = lambda name: plan.done((name, l)) if plan else None
    dy_rg, dy_ml, gw_out, dgate = _out_bwd(dx, sv["gate"], sv["y"], sv["y_rg"], sv["y_ml"], wl["w_out_g"], 0, t_big,
                                           hosted("out_proj_bwd"))
    done("out_proj_bwd")
    dq, dk, dv, dgates, d_mlo, d_mlz, g_mlng = _ml_cell_bwd(
        dy_ml, sv["u"], sv["cell"], sv["q"], sv["k"], sv["v"], sv["gcol"], sv["grow"], sv["mt"], sv["cs"], sv["ns"],
        sv["ms"], row(rep["ml_norm_g"][l]), nh_ml, hosted("mlstm_cell_bwd"))
    done("mlstm_cell_bwd")
    d_mlx, g_wq, g_wk, g_wv, g_wift, g_bif, g_mlcw, g_mlcb = _ml_pre_bwd(
        dq, dk, dv, dgates, sv["gcol"], sv["u"], sv["q"], sv["k"], sv["v"], wl["ml_conv_w"], row(rep["ml_conv_b"][l]),
        wl["w_qkv"][0], wl["w_qkv"][1], wl["w_qkv"][2], wl["wift_pad"], t_mid, hosted("mlstm_proj_bwd"))
    done("mlstm_proj_bwd")
    d_rgx, d_rgz, g_wa, g_wx, g_ba, g_bx, g_lam, g_rgcw, g_rgcb = _rg_bwd(
        dy_rg, sv["u"], sv["h_rg"], wl["rg_conv_w"], row(rep["rg_conv_b"][l]), rep["rg_w_a_bf"][l], row(rep["rg_b_a"][l]),
        rep["rg_w_x_bf"][l], row(rep["rg_b_x"][l]), row(rep["rg_lambda"][l]), t_mid, hosted("rglru_bwd"))
    done("rglru_bwd")
    pieces = [d_rgx, d_rgz, d_mlx, d_mlo, d_mlz]
    dx, dscale, dshift, g_ng = _in_bwd(pieces, sv["x"], dx, row(rep["norm_g"][l]), sv["scale"], wl["w_in_g"], 0, t_mid,
                                       hosted("in_proj_bwd_x"))
    done("in_proj_bwd_x")
    gw_in = _in_bwd_w(pieces, sv["hbf"], w_cols, tuple(range(nd)), t_big)
    dmod = jnp.concatenate([dshift[0:1], dscale[0:1], dgate[0:1]], axis=1)
    grads = dict(dmod=dmod, w_in=gw_in, w_out=gw_out, w_qkv=jnp.stack([g_wq, g_wk, g_wv]),
                 rg_conv_w=g_rgcw[0:CONV_WIDTH], ml_conv_w=g_mlcw[0:CONV_WIDTH], wif_t=g_wift[0:8], norm_g=g_ng[0],
                 b_ada=dmod[0], rg_conv_b=g_rgcb[0], rg_w_a=g_wa, rg_b_a=g_ba[0], rg_w_x=g_wx, rg_b_x=g_bx[0],
                 rg_lambda=g_lam[0], ml_conv_b=g_mlcb[0], ml_b_if=g_bif[0, 0:8], ml_norm_g=g_mlng[0])
    return dx, grads


def _local_step(x, c, target, wts):
    d = x.shape[1]
    nl = wts["w_in_g"].shape[1]
    mod, cact = _mod_call(c, wts["w_ada_g"], wts["b_ada"])
    wl = [dict(w_in_g=wts["w_in_g"][:, l:l + 1], w_out_g=wts["w_out_g"][:, l:l + 1], w_qkv=wts["w_qkv"][l],
               rg_conv_w=wts["rg_conv_w"][l], ml_conv_w=wts["ml_conv_w"][l], wif_pad=wts["wif_pad"][l],
               wift_pad=wts["wift_pad"][l], bif_pad=wts["bif_pad"][l]) for l in range(nl)]
    saved, xl = [], x
    for l in range(nl):
        xl, sv = _layer_fwd(l, xl, mod[l], wl[l], wts, None)
        saved.append(sv)
    dx, loss_p, g_final = _loss_call(xl, wts["final_g"].reshape(1, -1), target, _tile_for(x.shape[0], 512))
    grads = [None] * nl
    for l in reversed(range(nl)):
        dx, grads[l] = _layer_bwd(l, dx, saved[l], wl[l], wts, None)
        grads[l]["w_ada"] = _ada_bwd_w(cact[0].reshape(d, 1), grads[l]["dmod"], wts["w_ada_g"].shape[0])
        grads[l]["b_ada"] = grads[l]["dmod"][0]
    return loss_p[0, 0], dx, grads, g_final[0]


def _full_qkv(qkv_g, d):
    nd, _, rows3, dh = qkv_g.shape
    nh = d // dh
    rsh = rows3 // (3 * nh)
    return qkv_g.reshape(nd, 3, nh, rsh, dh).transpose(1, 2, 0, 3, 4).reshape(3, nh, nd * rsh, dh)


def _small_weights(small, l, ml_b_if):
    nd = small.shape[0]
    sm = small[:, l]
    cw = sm[:, 0:2 * CONV_WIDTH].reshape(nd, 2, CONV_WIDTH, LANES).transpose(1, 2, 0, 3).reshape(2, CONV_WIDTH, nd * LANES)
    if_rows = (sm.shape[1] - 2 * CONV_WIDTH) * LANES // 8
    wif_t = sm[:, 2 * CONV_WIDTH:].reshape(nd, 8, if_rows).transpose(1, 0, 2).reshape(8, nd * if_rows)
    wift_pad = jnp.pad(wif_t, ((0, LANES - 8), (0, 0))).astype(BF16)
    return dict(rg_conv_w=cw[0], ml_conv_w=cw[1], wift_pad=wift_pad, wif_pad=wift_pad.T,
                bif_pad=jnp.pad(ml_b_if[l], (0, LANES - 8)).reshape(1, LANES))


def kernel(x, c, norm_g, w_ada, b_ada, w_in, rg_conv_w, rg_conv_b, rg_w_a, rg_b_a, rg_w_x, rg_b_x, rg_lambda, ml_conv_w, ml_conv_b, ml_w_q, ml_w_k, ml_w_v, ml_w_if, ml_b_if, ml_norm_g, w_out, final_g, loss_target, m_norm_g, m_w_ada, m_b_ada, m_w_in, m_rg_conv_w, m_rg_conv_b, m_rg_w_a, m_rg_b_a, m_rg_w_x, m_rg_b_x, m_rg_lambda, m_ml_conv_w, m_ml_conv_b, m_ml_w_q, m_ml_w_k, m_ml_w_v, m_ml_w_if, m_ml_b_if, m_ml_norm_g, m_w_out, m_final_g, v_norm_g, v_w_ada, v_b_ada, v_w_in, v_rg_conv_w, v_rg_conv_b, v_rg_w_a, v_rg_b_a, v_rg_w_x, v_rg_b_x, v_rg_lambda, v_ml_conv_w, v_ml_conv_b, v_ml_w_q, v_ml_w_k, v_ml_w_v, v_ml_w_if, v_ml_b_if, v_ml_norm_g, v_w_out, v_final_g):
    given = dict(locals())
    nl = w_in.shape[0]
    d = x.shape[2]
    rep = {n: given[n] for n in REPLICATED}
    rep.update(rg_w_a_bf=rg_w_a.astype(BF16), rg_w_x_bf=rg_w_x.astype(BF16))
    bf = lambda a: a.astype(BF16)

    def qkv_shard(prefix):
        return jnp.stack([given[prefix + "ml_w_q"], given[prefix + "ml_w_k"], given[prefix + "ml_w_v"]], axis=1).reshape(
            nl, -1, ml_w_q.shape[-1])

    def small_shard(prefix):
        return _small_pack(given[prefix + "rg_conv_w"], given[prefix + "ml_conv_w"], given[prefix + "ml_w_if"])

    plan = _Plan()
    qkv = qkv_shard("")
    w_in_first, small = _gather_two_level([bf(w_in[0:1]), small_shard("")], "gather_first")
    wl = [_small_weights(small, l, ml_b_if) for l in range(nl)]
    wl[0]["w_in_g"] = w_in_first

    def gather_behind(arrs, ici_host, fwd_host, then):
        ici = _gather_ici_comm(arrs)

        def pass_on():
            fwd = _gather_fwd_comm(ici.results)
            plan.host(fwd_host, fwd, lambda: then(fwd.results))

        plan.host(ici_host, ici, pass_on)

    def got_out(l):
        return lambda r: wl[l].update(w_out_g=r[0], w_qkv=_full_qkv(r[1], d))

    gather_behind([bf(w_out[0:1]), bf(qkv[0:1])], ("in_proj_fwd", 0), ("rglru_fwd", 0), got_out(0))
    for l in range(1, nl):
        gather_behind([bf(w_in[l:l + 1])], ("rglru_fwd", l - 1), ("mlstm_proj_fwd", l - 1),
                      lambda r, l=l: wl[l].update(w_in_g=r[0]))
        gather_behind([bf(w_out[l:l + 1]), bf(qkv[l:l + 1])], ("mlstm_cell_fwd", l - 1), ("out_proj_fwd", l - 1), got_out(l))

    wcols = w_ada.shape[2]
    c_all = _exchange([jnp.broadcast_to(c, (SUBLANES, d))], True, "gather_condition")[0]
    me = 4 * lax.axis_index("x") + 2 * lax.axis_index("y") + lax.axis_index("c")
    b_cols = jnp.pad(lax.dynamic_slice_in_dim(b_ada, me * wcols, wcols, axis=1), ((0, SUBLANES - nl), (0, 0)))
    mod_cols, cact_all = _ada_mod(c_all, w_ada, b_cols)
    mod_blocks = _exchange([mod_cols], False, "scatter_modulation")[0]
    mod = mod_blocks[:, 0:nl].transpose(1, 0, 2).reshape(nl, N_DEV * wcols)
    saved, xl = [], x[0]
    for l in range(nl):
        xl, sv = _layer_fwd(l, xl, mod[l], wl[l], rep, plan)
        saved.append(sv)
    grad_x, loss_p, g_final = _loss_call(xl, final_g.reshape(1, -1), loss_target[0], _tile_for(xl.shape[0], 512))
    loss = lax.psum(loss_p[0, 0], MESH_AXES)

    keys = ("w_in", "w_out", "w_qkv", "small")
    parity = lax.axis_index("c").astype(jnp.int32).reshape(1)
    grads, recv = [None] * nl, [None] * nl

    def parts_of(g):
        return [g["w_in"], g["w_out"], _qkv_slots(g["w_qkv"], N_DEV), _small_slots(g)]

    def pair_sums(l, parts, other):
        return [_pair_sum(a, o, parity, "pair_sum_%s_layer%d" % (key, l)) for key, a, o in zip(keys, parts, other)]

    def reduce_behind(l, host_layer):
        parts = parts_of(grads[l])
        swap = _core_swap_comm(parts)

        def summed():
            sums = pair_sums(l, parts, swap.results)
            big = _chip_swap_comm([sums[0]])
            rest = _chip_swap_comm(sums[1:])
            plan.host(("mlstm_cell_bwd", host_layer), big)
            plan.host(("rglru_bwd", host_layer), rest, lambda: recv.__setitem__(l, big.results + rest.results))

        plan.host(("out_proj_bwd", host_layer), swap, summed)

    for l in reversed(range(nl)):
        grad_x, grads[l] = _layer_bwd(l, grad_x, saved[l], wl[l], rep, plan)
        if l > 0:
            reduce_behind(l, l - 1)
    parts = parts_of(grads[0])
    recv[0] = _chip_swap(pair_sums(0, parts, _core_swap(parts, "core_swap_last")), "chip_swap_last")

    shard = {p: dict(w_in=given[p + "w_in"], w_out=given[p + "w_out"], w_qkv=qkv_shard(p), small=small_shard(p))
             for p in ("", "m_", "v_")}
    res = {}
    for ki, key in enumerate(keys):
        res[key] = _reduce_adam([recv[l][ki] for l in range(nl)], shard[""][key], shard["m_"][key], shard["v_"][key],
                                "reduce_adam_" + key)

    dmods = jnp.concatenate([grads[l]["dmod"] for l in range(nl)], axis=0)
    dmod_blocks = jnp.pad(dmods.reshape(nl, N_DEV, wcols).transpose(1, 0, 2), ((0, 0), (0, SUBLANES - nl), (0, 0)))
    dmod_all = _exchange([dmod_blocks], False, "scatter_dmod")[0][:, 0:nl].transpose(1, 0, 2)
    res["w_ada"] = _ada_grad_adam(cact_all.T, dmod_all, w_ada, m_w_ada, v_w_ada)

    rep_g = dict(final_g=g_final[0])
    for n in REPLICATED[:-1]:
        rep_g[n] = jnp.stack([grads[l][n] for l in range(nl)])
    pack_g = _pack([rep_g[n] for n in REPLICATED])
    rows = pack_g.shape[0] // N_DEV
    mine = _sum8(_exchange([pack_g.reshape(N_DEV, rows, LANES)], False, "reduce_scatter_replicated")[0], "sum_replicated")
    g_rep = _exchange([mine], True, "gather_replicated")[0].reshape(N_DEV * rows, LANES)
    rep_like = [given[n] for n in REPLICATED]
    d_rep, m_rep, v_rep = _adam_call(_pack(rep_like), g_rep, _pack([given["m_" + n] for n in REPLICATED]),
                                     _pack([given["v_" + n] for n in REPLICATED]), "adam_replicated")
    rep_out = [dict(zip(REPLICATED, _unpack(a, rep_like))) for a in (g_rep, d_rep, m_rep, v_rep)]

    if_rows = ml_w_if.shape[1]
    order = ("norm_g", "w_ada", "b_ada", "w_in", "rg_conv_w", "rg_conv_b", "rg_w_a", "rg_b_a", "rg_w_x", "rg_b_x",
             "rg_lambda", "ml_conv_w", "ml_conv_b", "ml_w_q", "ml_w_k", "ml_w_v", "ml_w_if", "ml_b_if", "ml_norm_g",
             "w_out", "final_g")
    outs = [loss, grad_x[None]]
    for kind in range(4):
        qkv_k = res["w_qkv"][kind].reshape((nl, 3) + ml_w_q.shape[1:])
        rg_cw, ml_cw, wif = _small_unpack(res["small"][kind], if_rows)
        sharded = dict(w_ada=res["w_ada"][kind], w_in=res["w_in"][kind], w_out=res["w_out"][kind], ml_w_q=qkv_k[:, 0],
                       ml_w_k=qkv_k[:, 1], ml_w_v=qkv_k[:, 2], rg_conv_w=rg_cw, ml_conv_w=ml_cw, ml_w_if=wif)
        for n in order:
            outs.append(sharded[n] if n in sharded else rep_out[kind][n])
    return tuple(outs)
```

```python
import functools

import jax
import jax.numpy as jnp
from jax import lax
from jax.experimental import pallas as pl
from jax.experimental.pallas import tpu as pltpu

F32 = jnp.float32
BF16 = jnp.bfloat16
MESH_AXES = ("x", "y", "c")
N_DEV = 8
EPS = 1e-6
RG_C = 8.0
ML_CHUNK = 128
CONV_WIDTH = 4
ADAM_LR = 0.001
ADAM_B1 = 0.9
ADAM_B2 = 0.999
ADAM_EPS = 1e-08
ADAM_WD = 0.01
ADAM_STEP = 10
NEG_BIG = -1e30
LANES = 128
SUBLANES = 8
VMEM_LIMIT = 56 * 1024 * 1024
HI = lax.Precision.HIGHEST


def _params(n_grid):
    return pltpu.CompilerParams(dimension_semantics=("arbitrary",) * n_grid, vmem_limit_bytes=VMEM_LIMIT)


def _mm(a, b):
    return jnp.dot(a.astype(BF16), b.astype(BF16), preferred_element_type=F32)


def _mm_nt(a, b):
    return lax.dot_general(a.astype(BF16), b.astype(BF16), (((1,), (1,)), ((), ())), preferred_element_type=F32)


def _mm_tn(a, b):
    return lax.dot_general(a.astype(BF16), b.astype(BF16), (((0,), (0,)), ((), ())), preferred_element_type=F32)


def _mm_hi(a, b):
    return jnp.dot(a, b, precision=HI, preferred_element_type=F32)


def _sigmoid(x):
    return 1.0 / (1.0 + jnp.exp(-x))


def _softplus(x):
    return jnp.maximum(x, 0.0) + jnp.log(1.0 + jnp.exp(-jnp.abs(x)))


def _neg_expm1(x):
    poly = -x * (1.0 + x * (0.5 + x * (1.0 / 6.0 + x * (1.0 / 24.0 + x * (1.0 / 120.0)))))
    return jnp.where(jnp.abs(x) < 0.05, poly, 1.0 - jnp.exp(x))


def _iota(shape, dim):
    return lax.broadcasted_iota(jnp.int32, shape, dim)


def _colsum(x):
    return jnp.sum(x, axis=0, keepdims=True)


def _rowsum(x):
    return jnp.sum(x, axis=1, keepdims=True)


def _col(x, j):
    return _rowsum(jnp.where(_iota(x.shape, 1) == j, x, 0.0))


def _row(x, j):
    return _colsum(jnp.where(_iota(x.shape, 0) == j, x, 0.0))


def _shift_down(x, j, prev8):
    if j == 0:
        return x
    t = x.shape[0]
    main = jnp.where(_iota(x.shape, 0) >= j, pltpu.roll(x, j, 0), 0.0)
    fix = jnp.where(_iota(prev8.shape, 0) < j, pltpu.roll(prev8, j, 0), 0.0)
    return jnp.concatenate([main[0:SUBLANES] + fix, main[SUBLANES:t]], axis=0)


def _shift_up(x, j, next8):
    if j == 0:
        return x
    t = x.shape[0]
    main = jnp.where(_iota(x.shape, 0) < t - j, pltpu.roll(x, t - j, 0), 0.0)
    fix = jnp.where(_iota(next8.shape, 0) >= SUBLANES - j, pltpu.roll(next8, SUBLANES - j, 0), 0.0)
    return jnp.concatenate([main[0:t - SUBLANES], main[t - SUBLANES:t] + fix], axis=0)


def _conv(x, prev8, w_ref):
    y = w_ref[CONV_WIDTH - 1:CONV_WIDTH, :] * x
    for j in range(1, CONV_WIDTH):
        y = y + w_ref[CONV_WIDTH - 1 - j:CONV_WIDTH - j, :] * _shift_down(x, j, prev8)
    return y


def _conv_bwd_x(dy, next8, w_ref):
    dx = w_ref[CONV_WIDTH - 1:CONV_WIDTH, :] * dy
    for j in range(1, CONV_WIDTH):
        dx = dx + w_ref[CONV_WIDTH - 1 - j:CONV_WIDTH - j, :] * _shift_up(dy, j, next8)
    return dx


def _scan_fwd(a, b):
    t = a.shape[0]
    row = _iota(a.shape, 0)
    d = 1
    while d < t:
        keep = row >= d
        a_s = jnp.where(keep, pltpu.roll(a, d, 0), 1.0)
        b_s = jnp.where(keep, pltpu.roll(b, d, 0), 0.0)
        b = a * b_s + b
        a = a * a_s
        d *= 2
    return a, b


def _scan_rev(a, b):
    t = a.shape[0]
    row = _iota(a.shape, 0)
    d = 1
    while d < t:
        keep = row < t - d
        a_s = jnp.where(keep, pltpu.roll(a, t - d, 0), 1.0)
        b_s = jnp.where(keep, pltpu.roll(b, t - d, 0), 0.0)
        b = a * b_s + b
        a = a * a_s
        d *= 2
    return a, b


def _blockdiag(x, w_ref, transpose_w=False):
    nh, dh, _ = w_ref.shape
    outs = []
    for h in range(nh):
        xs = x[:, h * dh:(h + 1) * dh]
        outs.append(_mm_nt(xs, w_ref[h]) if transpose_w else _mm(xs, w_ref[h]))
    return jnp.concatenate(outs, axis=1)


def _rg_gates(xc, wa_ref, ba_ref, wx_ref, bx_ref, lam_ref):
    r = _sigmoid(_blockdiag(xc, wa_ref) + ba_ref[...])
    ig = _sigmoid(_blockdiag(xc, wx_ref) + bx_ref[...])
    sp = _softplus(-lam_ref[...])
    log_a = -RG_C * r * sp
    a = jnp.exp(log_a)
    beta = jnp.sqrt(_neg_expm1(2.0 * log_a))
    return r, ig, sp, a, beta


def _bcast8(row):
    return jnp.broadcast_to(row, (SUBLANES, row.shape[1]))


def _full(shape):
    nd = len(shape)
    return pl.BlockSpec(shape, lambda *_: (0,) * nd)


class _Comm:
    def __init__(self, arrays, out_shapes, sems, start, finish, aliases=()):
        self.arrays, self.out_shapes, self.sems = list(arrays), list(out_shapes), list(sems)
        self.start, self.finish, self.aliases = start, finish, tuple(aliases)
        self.results = None


def _call(body, comms, *, name, grid, in_specs, out_specs, out_shape, args, scratch_shapes=(), aliases=None):
    comms = [cm for cm in (comms or []) if cm is not None]
    n_in, n_out, n_sc = len(args), len(out_shape), len(scratch_shapes)
    c_arrays = [a for cm in comms for a in cm.arrays]
    c_outs = [o for cm in comms for o in cm.out_shapes]
    c_sems = [sm for cm in comms for sm in cm.sems]
    aliases, a_at, o_at = dict(aliases or {}), n_in, n_out
    for cm in comms:
        for (i, j) in cm.aliases:
            aliases[a_at + i] = o_at + j
        a_at += len(cm.arrays)
        o_at += len(cm.out_shapes)

    def wrapped(*refs):
        ins, c_in = refs[:n_in], refs[n_in:n_in + len(c_arrays)]
        at = n_in + len(c_arrays)
        outs, c_out = refs[at:at + n_out], refs[at + n_out:at + n_out + len(c_outs)]
        at += n_out + len(c_outs)
        scr, sems = refs[at:at + n_sc], refs[at + n_sc:]
        views, ia, io, isem = [], 0, 0, 0
        for cm in comms:
            views.append((c_in[ia:ia + len(cm.arrays)], c_out[io:io + len(cm.out_shapes)], sems[isem:isem + len(cm.sems)]))
            ia, io, isem = ia + len(cm.arrays), io + len(cm.out_shapes), isem + len(cm.sems)
        if comms:
            @pl.when(pl.program_id(0) == 0)
            def _():
                for cm, view in zip(comms, views):
                    cm.start(*view)

        body(*ins, *outs, *scr)
        if comms:
            @pl.when(pl.program_id(0) == grid[0] - 1)
            def _():
                for cm, view in zip(comms, views):
                    cm.finish(*view)

    hbm = pl.BlockSpec(memory_space=pl.ANY)
    res = pl.pallas_call(
        wrapped, name=name, grid=grid,
        in_specs=list(in_specs) + [hbm] * len(c_arrays), out_specs=list(out_specs) + [hbm] * len(c_outs),
        out_shape=list(out_shape) + c_outs, scratch_shapes=list(scratch_shapes) + c_sems,
        input_output_aliases=aliases, compiler_params=_params(len(grid)),
    )(*args, *c_arrays)
    at = n_out
    for cm in comms:
        cm.results = list(res[at:at + len(cm.out_shapes)])
        at += len(cm.out_shapes)
    return list(res[:n_out])


def _mod_call(c, w_ada_g, b_ada):
    nd, nl, d, w = w_ada_g.shape

    def body(c_ref, w_ref, b_ref, mod_ref, cact_ref):
        cv = c_ref[...]
        ca = _bcast8(cv * _sigmoid(cv))
        cact_ref[...] = ca
        mod_ref[0, 0] = _mm(ca, w_ref[0, 0]) + b_ref[0, 0]

    mod, cact = pl.pallas_call(
        body, name="adaln_mod", grid=(nl, nd),
        in_specs=[_full((1, d)),
                  pl.BlockSpec((1, 1, d, w), lambda l, j: (j, l, 0, 0)),
                  pl.BlockSpec((1, 1, 1, w), lambda l, j: (l, j, 0, 0))],
        out_specs=[pl.BlockSpec((1, 1, SUBLANES, w), lambda l, j: (l, j, 0, 0)), _full((SUBLANES, d))],
        out_shape=[jax.ShapeDtypeStruct((nl, nd, SUBLANES, w), F32), jax.ShapeDtypeStruct((SUBLANES, d), F32)],
        compiler_params=_params(2),
    )(c, w_ada_g, b_ada.reshape(nl, nd, 1, w))
    return mod[:, :, 0, :].reshape(nl, nd * w), cact


def _in_fwd(x, ng, scale, shift, w_in_g, layer, tile, comms=None):
    s, d = x.shape
    nd, _, _, w = w_in_g.shape

    def body(x_ref, ng_ref, sc_ref, sh_ref, w_ref, u_ref, h_ref):
        xv = x_ref[...]
        rs = lax.rsqrt(jnp.mean(xv * xv, axis=1, keepdims=True) + EPS)
        hb = (xv * rs * ng_ref[...] * (1.0 + sc_ref[...]) + sh_ref[...]).astype(BF16)
        h_ref[...] = hb
        for j in range(nd):
            u_ref[:, j * w:(j + 1) * w] = jnp.dot(hb, w_ref[j, 0], preferred_element_type=F32)

    return _call(
        body, comms, name="in_proj_fwd", grid=(s // tile,),
        in_specs=[pl.BlockSpec((tile, d), lambda i: (i, 0)), _full((1, d)), _full((1, d)), _full((1, d)),
                  pl.BlockSpec((nd, 1, d, w), lambda i: (0, layer, 0, 0))],
        out_specs=[pl.BlockSpec((tile, nd * w), lambda i: (i, 0)), pl.BlockSpec((tile, d), lambda i: (i, 0))],
        out_shape=[jax.ShapeDtypeStruct((s, nd * w), F32), jax.ShapeDtypeStruct((s, d), BF16)],
        args=(x, ng, scale, shift, w_in_g))


def _rg_fwd(u, d, conv_w, conv_b, w_a, b_a, w_x, b_x, lam, tile, comms=None):
    s = u.shape[0]

    def body(x_ref, z_ref, cw_ref, cb_ref, wa_ref, ba_ref, wx_ref, bx_ref, lam_ref, h_ref, y_ref, prev8, hcar):
        @pl.when(pl.program_id(0) == 0)
        def _():
            prev8[...] = jnp.zeros_like(prev8)
            hcar[...] = jnp.zeros_like(hcar)

        x = x_ref[...]
        xc = _conv(x, prev8[...], cw_ref) + cb_ref[...]
        prev8[...] = x[tile - SUBLANES:tile, :]
        _, ig, _, a, beta = _rg_gates(xc, wa_ref, ba_ref, wx_ref, bx_ref, lam_ref)
        acum, bsum = _scan_fwd(a, beta * ig * xc)
        h = bsum + acum * hcar[SUBLANES - 1:SUBLANES, :]
        hcar[...] = h[tile - SUBLANES:tile, :]
        h_ref[...] = h
        z = z_ref[...]
        y_ref[...] = h * z * _sigmoid(z)

    vec = _full((1, d))
    return _call(
        body, comms, name="rglru_fwd", grid=(s // tile,),
        in_specs=[pl.BlockSpec((tile, d), lambda i: (i, 0)), pl.BlockSpec((tile, d), lambda i: (i, 1)),
                  _full(conv_w.shape), vec, _full(w_a.shape), vec, _full(w_x.shape), vec, vec],
        out_specs=[pl.BlockSpec((tile, d), lambda i: (i, 0))] * 2,
        out_shape=[jax.ShapeDtypeStruct((s, d), F32)] * 2,
        scratch_shapes=[pltpu.VMEM((SUBLANES, d), F32), pltpu.VMEM((SUBLANES, d), F32)],
        args=(u, u, conv_w, conv_b, w_a, b_a, w_x, b_x, lam))


def _ml_pre(u, d, conv_w, conv_b, w_q, w_k, w_v, wif, bif, tile, comms=None):
    s = u.shape[0]
    nh = w_q.shape[0]

    def body(x_ref, cw_ref, cb_ref, wq_ref, wk_ref, wv_ref, wif_ref, bif_ref, q_ref, k_ref, v_ref, g_ref, prev8):
        @pl.when(pl.program_id(0) == 0)
        def _():
            prev8[...] = jnp.zeros_like(prev8)

        x = x_ref[...]
        pre = _conv(x, prev8[...], cw_ref) + cb_ref[...]
        prev8[...] = x[tile - SUBLANES:tile, :]
        xc = pre * _sigmoid(pre)
        q = _blockdiag(xc, wq_ref)
        k = _blockdiag(xc, wk_ref)
        v = _blockdiag(x, wv_ref)
        q_ref[...] = q
        k_ref[...] = k
        v_ref[...] = v
        g = _mm(q, wif_ref[0:d, :]) + _mm(k, wif_ref[d:2 * d, :]) + _mm(v, wif_ref[2 * d:3 * d, :]) + bif_ref[...]
        lane = _iota(g.shape, 1)
        gl = jnp.where(lane < 4, g, jnp.where(lane < 8, -_softplus(-g), 0.0))
        tri = jnp.where(_iota((ML_CHUNK, ML_CHUNK), 1) <= _iota((ML_CHUNK, ML_CHUNK), 0), 1.0, 0.0)
        cums = [_mm_hi(tri, gl[c * ML_CHUNK:(c + 1) * ML_CHUNK, :]) for c in range(tile // ML_CHUNK)]
        cum = cums[0] if len(cums) == 1 else jnp.concatenate(cums, axis=0)
        g_ref[...] = gl + jnp.where((lane >= 8) & (lane < 12), pltpu.roll(cum, 4, 1), 0.0)

    vec = _full((1, d))
    return _call(
        body, comms, name="mlstm_proj_fwd", grid=(s // tile,),
        in_specs=[pl.BlockSpec((tile, d), lambda i: (i, 2)), _full(conv_w.shape), vec,
                  _full(w_q.shape), _full(w_k.shape), _full(w_v.shape), _full(wif.shape), _full((1, LANES))],
        out_specs=[pl.BlockSpec((tile, d), lambda i: (i, 0))] * 3 + [pl.BlockSpec((tile, LANES), lambda i: (i, 0))],
        out_shape=[jax.ShapeDtypeStruct((s, d), F32)] * 3 + [jax.ShapeDtypeStruct((s, LANES), F32)],
        scratch_shapes=[pltpu.VMEM((SUBLANES, d), F32)],
        args=(u, conv_w, conv_b, w_q, w_k, w_v, wif, bif))


def _cell_chunk(h, nh, q_ref, k_ref, v_ref, gc, gr, m_prev, c_h, n_h, m_t=None):
    lc = ML_CHUNK
    dh = q_ref.shape[1] // nh
    sl = slice(h * dh, (h + 1) * dh)
    qh = q_ref[:, sl]
    kh = k_ref[:, sl] * (dh ** -0.5)
    vh = v_ref[:, sl]
    li_c = _col(gc, h)
    b_c = _col(gc, 8 + h)
    lib_r = _row(gr, h) - _row(gr, 8 + h)
    b_last = _colsum(jnp.where(_iota((lc, 1), 0) == lc - 1, b_c, 0.0))
    causal = _iota((lc, lc), 1) <= _iota((lc, lc), 0)
    dmat = jnp.where(causal, b_c + lib_r, NEG_BIG)
    m_inter = b_c + m_prev
    if m_t is None:
        m_t = jnp.maximum(m_inter, jnp.max(dmat, axis=1, keepdims=True))
    w_intra = jnp.exp(dmat - m_t)
    w_inter = jnp.exp(m_inter - m_t)
    amat = _mm_nt(qh, kh)
    smat = amat * w_intra
    qc = _mm(qh, c_h)
    qn = _rowsum(qh * n_h)
    den = _rowsum(smat) + w_inter * qn
    gst = b_last - b_c + li_c
    m_new = jnp.maximum(b_last + m_prev, jnp.max(gst, axis=0, keepdims=True))
    w_state = jnp.exp(gst - m_new)
    decay = jnp.exp(b_last + m_prev - m_new)
    return dict(sl=sl, qh=qh, kh=kh, vh=vh, m_t=m_t, w_intra=w_intra, w_inter=w_inter, smat=smat, qc=qc, qn=qn,
                den=den, m_new=m_new, w_state=w_state, decay=decay)


def _ml_cell_fwd(q, k, v, gcol, grow, u, ng, nh, comms=None):
    s, d = q.shape
    lc = ML_CHUNK
    nc = s // lc
    dh = d // nh

    def body(q_ref, k_ref, v_ref, gc_ref, gr_ref, o_ref, z_ref, ng_ref,
             cell_ref, y_ref, cs_ref, ns_ref, ms_ref, mt_ref, c_sc, n_sc, m_sc):
        @pl.when(pl.program_id(0) == 0)
        def _():
            c_sc[...] = jnp.zeros_like(c_sc)
            n_sc[...] = jnp.zeros_like(n_sc)
            m_sc[...] = jnp.zeros_like(m_sc)

        gc = gc_ref[...]
        gr = gr_ref[...]
        lane = _iota((lc, LANES), 1)
        mt_acc = jnp.zeros((lc, LANES), F32)
        for h in range(nh):
            c_h = c_sc[h]
            n_h = n_sc[h, 0:1, :]
            m_prev = jnp.max(m_sc[h, 0:1, :], axis=1, keepdims=True)
            cs_ref[0, h] = c_h
            ns_ref[0, h] = n_sc[h]
            ms_ref[0, h] = m_sc[h]
            t = _cell_chunk(h, nh, q_ref, k_ref, v_ref, gc, gr, m_prev, c_h, n_h)
            sl = t["sl"]
            num = _mm(t["smat"], t["vh"]) + t["w_inter"] * t["qc"]
            cell_h = num / jnp.maximum(jnp.abs(t["den"]), jnp.exp(-t["m_t"]))
            mt_acc = jnp.where(lane == h, t["m_t"], mt_acc)
            kw = t["kh"] * t["w_state"]
            c_sc[h] = t["decay"] * c_h + _mm_tn(kw, t["vh"])
            n_sc[h] = _bcast8(t["decay"] * n_h + _colsum(kw))
            m_sc[h] = jnp.broadcast_to(t["m_new"], (SUBLANES, LANES))
            hg = _sigmoid(o_ref[:, sl]) * cell_h
            hn = hg * lax.rsqrt(jnp.mean(hg * hg, axis=1, keepdims=True) + EPS)
            z = z_ref[:, sl]
            cell_ref[:, sl] = cell_h
            y_ref[:, sl] = hn * ng_ref[:, sl] * z * _sigmoid(z)
        mt_ref[...] = mt_acc

    tok = pl.BlockSpec((lc, d), lambda c: (c, 0))
    return _call(
        body, comms, name="mlstm_cell_fwd", grid=(nc,),
        in_specs=[tok, tok, tok, pl.BlockSpec((lc, LANES), lambda c: (c, 0)), pl.BlockSpec((16, lc), lambda c: (0, c)),
                  pl.BlockSpec((lc, d), lambda c: (c, 3)), pl.BlockSpec((lc, d), lambda c: (c, 4)), _full((1, d))],
        out_specs=[tok, tok, pl.BlockSpec((1, nh, dh, dh), lambda c: (c, 0, 0, 0)),
                   pl.BlockSpec((1, nh, SUBLANES, dh), lambda c: (c, 0, 0, 0)),
                   pl.BlockSpec((1, nh, SUBLANES, LANES), lambda c: (c, 0, 0, 0)),
                   pl.BlockSpec((lc, LANES), lambda c: (c, 0))],
        out_shape=[jax.ShapeDtypeStruct((s, d), F32), jax.ShapeDtypeStruct((s, d), F32),
                   jax.ShapeDtypeStruct((nc, nh, dh, dh), F32), jax.ShapeDtypeStruct((nc, nh, SUBLANES, dh), F32),
                   jax.ShapeDtypeStruct((nc, nh, SUBLANES, LANES), F32), jax.ShapeDtypeStruct((s, LANES), F32)],
        scratch_shapes=[pltpu.VMEM((nh, dh, dh), F32), pltpu.VMEM((nh, SUBLANES, dh), F32),
                        pltpu.VMEM((nh, SUBLANES, LANES), F32)],
        args=(q, k, v, gcol, grow, u, u, ng))


def _out_fwd(x, y_rg, y_ml, gate, w_out_g, layer, tile, comms=None):
    s, d = x.shape
    nd, _, r, _ = w_out_g.shape

    def body(x_ref, yr_ref, ym_ref, g_ref, w_ref, xn_ref, y_ref):
        acc = jnp.zeros((tile, d), F32)
        for j in range(nd):
            src = yr_ref if j * r < d else ym_ref
            off = (j * r) % d
            acc = acc + _mm(src[:, off:off + r], w_ref[j, 0])
        y_ref[...] = acc
        xn_ref[...] = x_ref[...] + g_ref[...] * acc

    tok = pl.BlockSpec((tile, d), lambda i: (i, 0))
    return _call(
        body, comms, name="out_proj_fwd", grid=(s // tile,),
        in_specs=[tok, tok, tok, _full((1, d)), pl.BlockSpec((nd, 1, r, d), lambda i: (0, layer, 0, 0))],
        out_specs=[tok, tok],
        out_shape=[jax.ShapeDtypeStruct((s, d), F32)] * 2,
        args=(x, y_rg, y_ml, gate, w_out_g))


def _loss_call(x, fg, target, tile):
    s, d = x.shape

    def body(x_ref, g_ref, t_ref, dx_ref, loss_ref, gg_ref):
        @pl.when(pl.program_id(0) == 0)
        def _():
            loss_ref[...] = jnp.zeros_like(loss_ref)
            gg_ref[...] = jnp.zeros_like(gg_ref)

        xv = x_ref[...]
        g = g_ref[...]
        rs = lax.rsqrt(jnp.mean(xv * xv, axis=1, keepdims=True) + EPS)
        xh = xv * rs
        e = xh * g - t_ref[...]
        loss_ref[...] += jnp.broadcast_to(_colsum(_rowsum(e * e)) * (0.5 / d), loss_ref.shape)
        dy = e * (1.0 / d)
        gg_ref[...] += _bcast8(_colsum(dy * xh))
        dxh = dy * g
        dx_ref[...] = rs * (dxh - xh * jnp.mean(dxh * xh, axis=1, keepdims=True))

    tok = pl.BlockSpec((tile, d), lambda i: (i, 0))
    return pl.pallas_call(
        body, name="final_norm_loss", grid=(s // tile,),
        in_specs=[tok, _full((1, d)), tok],
        out_specs=[tok, _full((SUBLANES, LANES)), _full((SUBLANES, d))],
        out_shape=[jax.ShapeDtypeStruct((s, d), F32), jax.ShapeDtypeStruct((SUBLANES, LANES), F32),
                   jax.ShapeDtypeStruct((SUBLANES, d), F32)],
        compiler_params=_params(1),
    )(x, fg, target)


def _out_bwd(dxo, gate, y, y_rg, y_ml, w_out_g, layer, tile, comms=None):
    s, d = dxo.shape
    nd, _, r, _ = w_out_g.shape

    def body(dx_ref, g_ref, y_ref, yr_ref, ym_ref, w_ref, dyr_ref, dym_ref, gw_ref, dg_ref):
        @pl.when(pl.program_id(0) == 0)
        def _():
            gw_ref[...] = jnp.zeros_like(gw_ref)
            dg_ref[...] = jnp.zeros_like(dg_ref)

        dxv = dx_ref[...]
        dg_ref[...] += _bcast8(_colsum(dxv * y_ref[...]))
        dyb = (dxv * g_ref[...]).astype(BF16)
        for j in range(nd):
            src, dst = (yr_ref, dyr_ref) if j * r < d else (ym_ref, dym_ref)
            off = (j * r) % d
            dst[:, off:off + r] = _mm_nt(dyb, w_ref[j, 0])
            gw_ref[j] += _mm_tn(src[:, off:off + r], dyb)

    tok = pl.BlockSpec((tile, d), lambda i: (i, 0))
    return _call(
        body, comms, name="out_proj_bwd", grid=(s // tile,),
        in_specs=[tok, _full((1, d)), tok, tok, tok, pl.BlockSpec((nd, 1, r, d), lambda i: (0, layer, 0, 0))],
        out_specs=[tok, tok, _full((nd, r, d)), _full((SUBLANES, d))],
        out_shape=[jax.ShapeDtypeStruct((s, d), F32)] * 2 + [jax.ShapeDtypeStruct((nd, r, d), F32),
                                                             jax.ShapeDtypeStruct((SUBLANES, d), F32)],
        args=(dxo, gate, y, y_rg, y_ml, w_out_g))


def _ml_cell_bwd(dy_ml, u, cell, q, k, v, gcol, grow, mt, cs, ns, ms, ng, nh, comms=None):
    s, d = q.shape
    lc = ML_CHUNK
    nc = s // lc
    dh = d // nh

    def body(dy_ref, o_ref, z_ref, cell_ref, q_ref, k_ref, v_ref, gc_ref, gr_ref, mt_ref, cs_ref, ns_ref, ms_ref,
             ng_ref, dq_ref, dk_ref, dv_ref, dg_ref, do_ref, dz_ref, gng_ref, dc_sc, dn_sc):
        @pl.when(pl.program_id(0) == 0)
        def _():
            dc_sc[...] = jnp.zeros_like(dc_sc)
            dn_sc[...] = jnp.zeros_like(dn_sc)
            gng_ref[...] = jnp.zeros_like(gng_ref)

        gc = gc_ref[...]
        gr = gr_ref[...]
        mtv = mt_ref[...]
        lane = _iota((lc, LANES), 1)
        rowv = _iota((lc, 1), 0)
        dg_acc = jnp.zeros((lc, LANES), F32)
        for h in range(nh):
            c_h = cs_ref[0, h]
            n_h = ns_ref[0, h, 0:1, :]
            m_prev = jnp.max(ms_ref[0, h, 0:1, :], axis=1, keepdims=True)
            t = _cell_chunk(h, nh, q_ref, k_ref, v_ref, gc, gr, m_prev, c_h, n_h, m_t=_col(mtv, h))
            sl, qh, kh, vh = t["sl"], t["qh"], t["kh"], t["vh"]
            w_intra, w_inter, smat, w_state, decay = t["w_intra"], t["w_inter"], t["smat"], t["w_state"], t["decay"]
            cell_h = cell_ref[:, sl]
            o = o_ref[:, sl]
            z = z_ref[:, sl]
            dyv = dy_ref[:, sl]
            ngh = ng_ref[:, sl]
            so = _sigmoid(o)
            hg = so * cell_h
            rinv = lax.rsqrt(jnp.mean(hg * hg, axis=1, keepdims=True) + EPS)
            hn = hg * rinv
            sz = _sigmoid(z)
            dz_ref[:, sl] = (dyv * hn * ngh * (sz + z * sz * (1.0 - sz))).astype(BF16)
            dymid = dyv * z * sz
            gng_ref[:, sl] += _bcast8(_colsum(dymid * hn))
            dhn = dymid * ngh
            dhg = rinv * (dhn - hn * jnp.mean(dhn * hn, axis=1, keepdims=True))
            do_ref[:, sl] = (dhg * cell_h * so * (1.0 - so)).astype(BF16)
            dcell = dhg * so
            eneg = jnp.exp(-t["m_t"])
            aden = jnp.abs(t["den"])
            nst = jnp.maximum(aden, eneg)
            dnum = dcell / nst
            dden = jnp.where(aden > eneg, -_rowsum(cell_h * dcell) / nst * jnp.sign(t["den"]), 0.0)
            pmat = _mm_nt(dnum, vh) + dden
            damat = pmat * w_intra
            gmat = pmat * smat
            wdn = w_inter * dnum
            wdd = w_inter * dden
            dqh = _mm(damat, kh) + _mm_nt(wdn, c_h) + wdd * n_h
            dkh = _mm_tn(damat, qh)
            dvh = _mm_tn(smat, dnum)
            dw_inter = _rowsum(dnum * t["qc"]) + dden * t["qn"]
            dcn = dc_sc[h]
            dnn = dn_sc[h, 0:1, :]
            kw = kh * w_state
            dkw = _mm_nt(vh, dcn) + dnn
            dvh = dvh + _mm(kw, dcn)
            dkh = dkh + dkw * w_state
            dgst = _rowsum(dkw * kh) * w_state
            ddecay = _colsum(_rowsum(dcn * c_h)) + _rowsum(dnn * n_h)
            db_last = _colsum(dgst) + ddecay * decay
            rs_g = _rowsum(gmat)
            cs_g = _rowsum(gmat.T)
            db = rs_g - cs_g + dw_inter * w_inter - dgst + jnp.where(rowv == lc - 1, db_last, 0.0)
            dli = cs_g + dgst
            dc_sc[h] = decay * dcn + _mm_tn(qh, wdn)
            dn_sc[h] = _bcast8(decay * dnn + _colsum(qh * wdd))
            dq_ref[:, sl] = dqh
            dk_ref[:, sl] = dkh * (dh ** -0.5)
            dv_ref[:, sl] = dvh
            dg_acc = jnp.where(lane == h, dli, jnp.where(lane == 4 + h, db, dg_acc))
        dg_ref[...] = dg_acc

    rev = lambda c: nc - 1 - c
    tok = pl.BlockSpec((lc, d), lambda c: (rev(c), 0))
    g128 = pl.BlockSpec((lc, LANES), lambda c: (rev(c), 0))
    return _call(
        body, comms, name="mlstm_cell_bwd", grid=(nc,),
        in_specs=[tok, pl.BlockSpec((lc, d), lambda c: (rev(c), 3)), pl.BlockSpec((lc, d), lambda c: (rev(c), 4)),
                  tok, tok, tok, tok, g128, pl.BlockSpec((16, lc), lambda c: (0, rev(c))), g128,
                  pl.BlockSpec((1, nh, dh, dh), lambda c: (rev(c), 0, 0, 0)),
                  pl.BlockSpec((1, nh, SUBLANES, dh), lambda c: (rev(c), 0, 0, 0)),
                  pl.BlockSpec((1, nh, SUBLANES, LANES), lambda c: (rev(c), 0, 0, 0)), _full((1, d))],
        out_specs=[tok, tok, tok, g128, tok, tok, _full((SUBLANES, d))],
        out_shape=[jax.ShapeDtypeStruct((s, d), F32)] * 3 + [jax.ShapeDtypeStruct((s, LANES), F32)]
        + [jax.ShapeDtypeStruct((s, d), BF16)] * 2 + [jax.ShapeDtypeStruct((SUBLANES, d), F32)],
        scratch_shapes=[pltpu.VMEM((nh, dh, dh), F32), pltpu.VMEM((nh, SUBLANES, dh), F32)],
        args=(dy_ml, u, u, cell, q, k, v, gcol, grow, mt, cs, ns, ms, ng))


def _halo_spec(d, tile, nt, col):
    per = tile // SUBLANES
    return pl.BlockSpec((SUBLANES, d), lambda i: (jnp.maximum((nt - 1 - i) * per - 1, 0), col))


def _ml_pre_bwd(dq, dk, dv, dgates, gcol, u, q, k, v, conv_w, conv_b, w_q, w_k, w_v, wif_t, tile, comms=None):
    s, d = dq.shape
    nt = s // tile
    nh, dh, _ = w_q.shape

    def body(dq_ref, dk_ref, dv_ref, dg_ref, gc_ref, x_ref, halo_ref, q_ref, k_ref, v_ref, cw_ref, cb_ref,
             wq_ref, wk_ref, wv_ref, wift_ref,
             dx_ref, gwq_ref, gwk_ref, gwv_ref, gwif_ref, gbif_ref, gcw_ref, gcb_ref, next8):
        i = pl.program_id(0)

        @pl.when(i == 0)
        def _():
            next8[...] = jnp.zeros_like(next8)
            for ref in (gwq_ref, gwk_ref, gwv_ref, gwif_ref, gbif_ref, gcw_ref, gcb_ref):
                ref[...] = jnp.zeros_like(ref)

        x = x_ref[...]
        halo = halo_ref[...] * jnp.where(i < nt - 1, 1.0, 0.0)
        pre = _conv(x, halo, cw_ref) + cb_ref[...]
        sg = _sigmoid(pre)
        xc = pre * sg
        dgc = dg_ref[...]
        lane = _iota(dgc.shape, 1)
        utri = jnp.where(_iota((ML_CHUNK, ML_CHUNK), 0) <= _iota((ML_CHUNK, ML_CHUNK), 1), 1.0, 0.0)
        rcs = [_mm_hi(utri, dgc[c * ML_CHUNK:(c + 1) * ML_CHUNK, :]) for c in range(tile // ML_CHUNK)]
        rc = rcs[0] if len(rcs) == 1 else jnp.concatenate(rcs, axis=0)
        dgates_v = jnp.where(lane < 4, dgc, jnp.where(lane < 8, rc * (1.0 - jnp.exp(gc_ref[...])), 0.0))
        dgb = dgates_v.astype(BF16)
        gbif_ref[...] += jnp.broadcast_to(_colsum(dgates_v), gbif_ref.shape)
        ext = jnp.dot(dgb, wift_ref[...], preferred_element_type=F32)
        dqt = dq_ref[...] + ext[:, 0:d]
        dkt = dk_ref[...] + ext[:, d:2 * d]
        dvt = dv_ref[...] + ext[:, 2 * d:3 * d]
        gwif_ref[:, 0:d] += _mm_tn(dgb, q_ref[...])
        gwif_ref[:, d:2 * d] += _mm_tn(dgb, k_ref[...])
        gwif_ref[:, 2 * d:3 * d] += _mm_tn(dgb, v_ref[...])
        dxc_parts, dxv_parts = [], []
        for h in range(nh):
            sl = slice(h * dh, (h + 1) * dh)
            gwq_ref[h] += _mm_tn(xc[:, sl], dqt[:, sl])
            gwk_ref[h] += _mm_tn(xc[:, sl], dkt[:, sl])
            gwv_ref[h] += _mm_tn(x[:, sl], dvt[:, sl])
            dxc_parts.append(_mm_nt(dqt[:, sl], wq_ref[h]) + _mm_nt(dkt[:, sl], wk_ref[h]))
            dxv_parts.append(_mm_nt(dvt[:, sl], wv_ref[h]))
        dxc = jnp.concatenate(dxc_parts, axis=1)
        dxv = jnp.concatenate(dxv_parts, axis=1)
        dpre = dxc * (sg + pre * sg * (1.0 - sg))
        gcb_ref[...] += _bcast8(_colsum(dpre))
        for kk in range(CONV_WIDTH):
            gcw_ref[kk:kk + 1, :] += _colsum(dpre * _shift_down(x, CONV_WIDTH - 1 - kk, halo))
        dx_ref[...] = (dxv + _conv_bwd_x(dpre, next8[...], cw_ref)).astype(BF16)
        next8[...] = dpre[0:SUBLANES, :]

    rev = lambda i: nt - 1 - i
    tok = pl.BlockSpec((tile, d), lambda i: (rev(i), 0))
    g128 = pl.BlockSpec((tile, LANES), lambda i: (rev(i), 0))
    wsh = (nh, dh, dh)
    return _call(
        body, comms, name="mlstm_proj_bwd", grid=(nt,),
        in_specs=[tok, tok, tok, g128, g128, pl.BlockSpec((tile, d), lambda i: (rev(i), 2)), _halo_spec(d, tile, nt, 2),
                  tok, tok, tok, _full(conv_w.shape), _full((1, d)), _full(wsh), _full(wsh), _full(wsh),
                  _full(wif_t.shape)],
        out_specs=[tok, _full(wsh), _full(wsh), _full(wsh), _full((LANES, 3 * d)), _full((SUBLANES, LANES)),
                   _full((SUBLANES, d)), _full((SUBLANES, d))],
        out_shape=[jax.ShapeDtypeStruct((s, d), BF16)] + [jax.ShapeDtypeStruct(wsh, F32)] * 3
        + [jax.ShapeDtypeStruct((LANES, 3 * d), F32), jax.ShapeDtypeStruct((SUBLANES, LANES), F32),
           jax.ShapeDtypeStruct((SUBLANES, d), F32), jax.ShapeDtypeStruct((SUBLANES, d), F32)],
        scratch_shapes=[pltpu.VMEM((SUBLANES, d), F32)],
        args=(dq, dk, dv, dgates, gcol, u, u, q, k, v, conv_w, conv_b, w_q, w_k, w_v, wif_t))


def _rg_bwd(dy_rg, u, h_rg, conv_w, conv_b, w_a, b_a, w_x, b_x, lam, tile, comms=None):
    s, d = dy_rg.shape
    nt = s // tile
    nh, dh, _ = w_a.shape

    def body(dy_ref, x_ref, xhalo_ref, z_ref, h_ref, hhalo_ref, cw_ref, cb_ref, wa_ref, ba_ref, wx_ref, bx_ref, lam_ref,
             dx_ref, dz_ref, gwa_ref, gwx_ref, gba_ref, gbx_ref, glam_ref, gcw_ref, gcb_ref, next8, anext, dnext):
        i = pl.program_id(0)

        @pl.when(i == 0)
        def _():
            for ref in (next8, anext, dnext, gwa_ref, gwx_ref, gba_ref, gbx_ref, glam_ref, gcw_ref, gcb_ref):
                ref[...] = jnp.zeros_like(ref)

        inner = jnp.where(i < nt - 1, 1.0, 0.0)
        x = x_ref[...]
        halo = xhalo_ref[...] * inner
        xc = _conv(x, halo, cw_ref) + cb_ref[...]
        r, ig, sp, a, beta = _rg_gates(xc, wa_ref, ba_ref, wx_ref, bx_ref, lam_ref)
        h = h_ref[...]
        row = _iota(h.shape, 0)
        hprev = jnp.where(row >= 1, pltpu.roll(h, 1, 0), hhalo_ref[SUBLANES - 1:SUBLANES, :] * inner)
        z = z_ref[...]
        sz = _sigmoid(z)
        dyv = dy_ref[...]
        dz_ref[...] = (dyv * h * (sz + z * sz * (1.0 - sz))).astype(BF16)
        a_up = jnp.where(row < tile - 1, pltpu.roll(a, tile - 1, 0), anext[0:1, :])
        acum, bsum = _scan_rev(a_up, dyv * z * sz)
        delta = bsum + acum * dnext[0:1, :]
        anext[...] = a[0:SUBLANES, :]
        dnext[...] = delta[0:SUBLANES, :]
        dla = delta * hprev * a - delta * ig * xc * (a * a / beta)
        glam_ref[...] += _bcast8(_colsum(dla * r) * (RG_C * _sigmoid(-lam_ref[...])))
        dpa = dla * (-RG_C * sp) * r * (1.0 - r)
        dpx = delta * beta * xc * ig * (1.0 - ig)
        gba_ref[...] += _bcast8(_colsum(dpa))
        gbx_ref[...] += _bcast8(_colsum(dpx))
        parts = []
        for hh in range(nh):
            sl = slice(hh * dh, (hh + 1) * dh)
            gwa_ref[hh] += _mm_tn(xc[:, sl], dpa[:, sl])
            gwx_ref[hh] += _mm_tn(xc[:, sl], dpx[:, sl])
            parts.append(_mm_nt(dpa[:, sl], wa_ref[hh]) + _mm_nt(dpx[:, sl], wx_ref[hh]))
        dxc = delta * beta * ig + jnp.concatenate(parts, axis=1)
        gcb_ref[...] += _bcast8(_colsum(dxc))
        for kk in range(CONV_WIDTH):
            gcw_ref[kk:kk + 1, :] += _colsum(dxc * _shift_down(x, CONV_WIDTH - 1 - kk, halo))
        dx_ref[...] = _conv_bwd_x(dxc, next8[...], cw_ref).astype(BF16)
        next8[...] = dxc[0:SUBLANES, :]

    rev = lambda i: nt - 1 - i
    tok = pl.BlockSpec((tile, d), lambda i: (rev(i), 0))
    vec = _full((1, d))
    acc = _full((SUBLANES, d))
    wsh = (nh, dh, dh)
    return _call(
        body, comms, name="rglru_bwd", grid=(nt,),
        in_specs=[tok, tok, _halo_spec(d, tile, nt, 0), pl.BlockSpec((tile, d), lambda i: (rev(i), 1)), tok,
                  _halo_spec(d, tile, nt, 0), _full(conv_w.shape), vec, _full(wsh), vec, _full(wsh), vec, vec],
        out_specs=[tok, tok, _full(wsh), _full(wsh), acc, acc, acc, acc, acc],
        out_shape=[jax.ShapeDtypeStruct((s, d), BF16)] * 2 + [jax.ShapeDtypeStruct(wsh, F32)] * 2
        + [jax.ShapeDtypeStruct((SUBLANES, d), F32)] * 5,
        scratch_shapes=[pltpu.VMEM((SUBLANES, d), F32)] * 3,
        args=(dy_rg, u, u, u, h_rg, h_rg, conv_w, conv_b, w_a, b_a, w_x, b_x, lam))


def _segments(d, w, n_pieces, n_slots):
    bounds = sorted({k * d for k in range(n_pieces + 1)} | {j * w for j in range(n_slots + 1)})
    return [(lo // d, lo % d, lo // w, lo % w, hi - lo) for lo, hi in zip(bounds[:-1], bounds[1:])]


def _in_bwd(pieces, x, dxo, ng, scale, w_in_g, layer, tile, comms=None, tiles=None, prev=None):
    s, d = x.shape
    nd, _, _, w = w_in_g.shape
    segs = _segments(d, w, len(pieces), nd)
    first, count = tiles or (0, s // tile)
    n_p = len(pieces)

    def body(*refs):
        p_refs = refs[:n_p]
        x_ref, dxo_ref, ng_ref, sc_ref, w_ref = refs[n_p:n_p + 5]
        dx_ref, dsc_ref, dsh_ref, gng_ref = refs[-4:]

        @pl.when(pl.program_id(0) == 0)
        def _():
            for k, ref in enumerate((dsc_ref, dsh_ref, gng_ref)):
                ref[...] = jnp.zeros_like(ref) if prev is None else refs[n_p + 6 + k][...]

        dh = jnp.zeros((tile, d), F32)
        for (kk, a, j, b, width) in segs:
            dh = dh + _mm_nt(p_refs[kk][:, a:a + width], w_ref[j, 0, :, b:b + width])
        xv = x_ref[...]
        g = ng_ref[...]
        rs = lax.rsqrt(jnp.mean(xv * xv, axis=1, keepdims=True) + EPS)
        xh = xv * rs
        dsh_ref[...] += _bcast8(_colsum(dh))
        dsc_ref[...] += _bcast8(_colsum(dh * xh * g))
        dhn = dh * (1.0 + sc_ref[...])
        gng_ref[...] += _bcast8(_colsum(dhn * xh))
        dxh = dhn * g
        dx_ref[...] = dxo_ref[...] + rs * (dxh - xh * jnp.mean(dxh * xh, axis=1, keepdims=True))

    tok = pl.BlockSpec((tile, d), lambda i: (i + first, 0))
    vec = _full((1, d))
    acc = _full((SUBLANES, d))
    more_specs = [] if prev is None else [pl.BlockSpec(memory_space=pl.ANY), acc, acc, acc]
    return _call(
        body, comms, name="in_proj_bwd_x", grid=(count,),
        in_specs=[tok] * n_p + [tok, tok, vec, vec, pl.BlockSpec((nd, 1, d, w), lambda i: (0, layer, 0, 0))] + more_specs,
        out_specs=[tok, acc, acc, acc],
        out_shape=[jax.ShapeDtypeStruct((s, d), F32)] + [jax.ShapeDtypeStruct((SUBLANES, d), F32)] * 3,
        args=(*pieces, x, dxo, ng, scale, w_in_g) + (() if prev is None else tuple(prev)),
        aliases={} if prev is None else {n_p + 5: 0})


def _in_bwd_w(pieces, hbf, w, slots, tile, comms=None):
    s, d = hbf.shape
    nd_all = len(pieces) * d // w
    segs = [sg for sg in _segments(d, w, len(pieces), nd_all) if sg[2] in slots]

    def body(*refs):
        p_refs = refs[:len(pieces)]
        h_ref, gw_ref = refs[len(pieces):]

        @pl.when(pl.program_id(0) == 0)
        def _():
            gw_ref[...] = jnp.zeros_like(gw_ref)

        hv = h_ref[...]
        for (kk, a, j, b, width) in segs:
            gw_ref[j - slots[0], :, b:b + width] += _mm_tn(hv, p_refs[kk][:, a:a + width])

    tok = pl.BlockSpec((tile, d), lambda i: (i, 0))
    return _call(
        body, comms, name="in_proj_bwd_w", grid=(s // tile,),
        in_specs=[tok] * len(pieces) + [tok],
        out_specs=[pl.BlockSpec((len(slots), d, w), lambda i: (0, 0, 0), pipeline_mode=pl.Buffered(1))],
        out_shape=[jax.ShapeDtypeStruct((len(slots), d, w), F32)],
        args=(*pieces, hbf))[0]


def _ada_bwd_w(cact_col, dmod, nd):
    d = cact_col.shape[0]
    w = dmod.shape[1] // nd

    def body(c_ref, m_ref, o_ref):
        o_ref[0] = c_ref[...] * m_ref[...]

    return pl.pallas_call(
        body, name="adaln_bwd_w", grid=(nd,),
        in_specs=[_full((d, 1)), pl.BlockSpec((1, w), lambda j: (0, j))],
        out_specs=pl.BlockSpec((1, d, w), lambda j: (j, 0, 0)),
        out_shape=jax.ShapeDtypeStruct((nd, d, w), F32),
        compiler_params=_params(1),
    )(cact_col, dmod)


def _exchange(arrs, gather, name):
    n = len(arrs)
    outs_shape = [jax.ShapeDtypeStruct((N_DEV,) + a.shape if gather else a.shape, a.dtype) for a in arrs]

    def body(*refs):
        ins, outs = refs[:n], refs[n:2 * n]
        send_sems, recv_sems, local_sems = refs[2 * n:]
        x, y, c = (lax.axis_index(ax) for ax in MESH_AXES)
        me = 4 * x + 2 * y + c
        sends, recvs = [], []
        for flip in range(1, N_DEV):
            px = x ^ ((flip >> 2) & 1)
            py = y ^ ((flip >> 1) & 1)
            pc = c ^ (flip & 1)
            peer = 4 * px + 2 * py + pc
            for kk in range(n):
                sem = kk * (N_DEV - 1) + flip - 1
                src = ins[kk] if gather else ins[kk].at[peer]
                sends.append(pltpu.make_async_remote_copy(
                    src_ref=src, dst_ref=outs[kk].at[me], send_sem=send_sems.at[sem], recv_sem=recv_sems.at[sem],
                    device_id=(px, py, pc), device_id_type=pl.DeviceIdType.MESH))
                recvs.append(pltpu.make_async_remote_copy(
                    src_ref=src, dst_ref=outs[kk].at[peer], send_sem=send_sems.at[sem], recv_sem=recv_sems.at[sem],
                    device_id=(px, py, pc), device_id_type=pl.DeviceIdType.MESH))
        for cp in sends:
            cp.start()
        local = [pltpu.make_async_copy(ins[kk] if gather else ins[kk].at[me], outs[kk].at[me], local_sems.at[kk])
                 for kk in range(n)]
        for cp in local:
            cp.start()
        for cp in recvs:
            cp.wait_recv()
        for cp in sends:
            cp.wait_send()
        for cp in local:
            cp.wait()

    return pl.pallas_call(
        body, name=name,
        in_specs=[pl.BlockSpec(memory_space=pl.ANY)] * n,
        out_specs=[pl.BlockSpec(memory_space=pl.ANY)] * n,
        out_shape=outs_shape,
        scratch_shapes=[pltpu.SemaphoreType.DMA((n * (N_DEV - 1),)), pltpu.SemaphoreType.DMA((n * (N_DEV - 1),)),
                        pltpu.SemaphoreType.DMA((n,))],
    )(*arrs)


def _mesh_place():
    x, y, c = (lax.axis_index(ax) for ax in MESH_AXES)
    return x, y, c, (x, y, 1 - c), [(1 - x, y), (x, 1 - y), (1 - x, 1 - y)]


def _remote(src, dst, send_sems, recv_sems, sem, to):
    return pltpu.make_async_remote_copy(src_ref=src, dst_ref=dst, send_sem=send_sems.at[sem], recv_sem=recv_sems.at[sem],
                                        device_id=to, device_id_type=pl.DeviceIdType.MESH)


def _gather_two_level(arrs, name):
    n = len(arrs)
    per = N_DEV - 1

    def body(*refs):
        ins, outs = refs[:n], refs[n:2 * n]
        send_sems, recv_sems, local_sems = refs[2 * n:]
        x, y, c, sibling, chips = _mesh_place()

        def copy(kk, j, block, to, src=None):
            dst = outs[kk].at[4 * block[0] + 2 * block[1] + block[2]]
            return _remote(dst if src is None else src, dst, send_sems, recv_sems, kk * per + j, to)

        me = (x, y, c)
        local = [pltpu.make_async_copy(ins[kk], outs[kk].at[4 * x + 2 * y + c], local_sems.at[kk]) for kk in range(n)]
        first = []
        for j, chip in enumerate(chips):
            first += [copy(kk, 1 + j, me, (*chip, c), src=ins[kk]) for kk in range(n)]
        first += [copy(kk, 0, me, sibling, src=ins[kk]) for kk in range(n)]
        for cp in first + local:
            cp.start()
        passed = []
        for j, chip in enumerate(chips):
            for kk in range(n):
                copy(kk, 1 + j, (*chip, c), me).wait_recv()
                passed.append(copy(kk, 4 + j, (*chip, c), sibling))
                passed[-1].start()
        for kk in range(n):
            copy(kk, 0, sibling, me).wait_recv()
        for j, chip in enumerate(chips):
            for kk in range(n):
                copy(kk, 4 + j, (*chip, 1 - c), me).wait_recv()
        for cp in first + passed:
            cp.wait_send()
        for cp in local:
            cp.wait()

    return pl.pallas_call(
        body, name=name,
        in_specs=[pl.BlockSpec(memory_space=pl.ANY)] * n, out_specs=[pl.BlockSpec(memory_space=pl.ANY)] * n,
        out_shape=[jax.ShapeDtypeStruct((N_DEV,) + a.shape, a.dtype) for a in arrs],
        scratch_shapes=[pltpu.SemaphoreType.DMA((n * per,)), pltpu.SemaphoreType.DMA((n * per,)),
                        pltpu.SemaphoreType.DMA((n,))],
    )(*arrs)


N_CHIPS = N_DEV // 2


def _core_swap(arrs, name):
    n = len(arrs)

    def body(*refs):
        ins, outs = refs[:n], refs[n:2 * n]
        send_sems, recv_sems = refs[2 * n:]
        _, _, c, sibling, _ = _mesh_place()
        copies = [_remote(ins[kk].at[2 * q + (1 - c)], outs[kk].at[q], send_sems, recv_sems, kk * N_CHIPS + q, sibling)
                  for q in range(N_CHIPS) for kk in range(n)]
        for cp in copies:
            cp.start()
        for cp in copies:
            cp.wait_recv()
        for cp in copies:
            cp.wait_send()

    return pl.pallas_call(
        body, name=name,
        in_specs=[pl.BlockSpec(memory_space=pl.ANY)] * n, out_specs=[pl.BlockSpec(memory_space=pl.ANY)] * n,
        out_shape=[jax.ShapeDtypeStruct((N_CHIPS,) + a.shape[1:], a.dtype) for a in arrs],
        scratch_shapes=[pltpu.SemaphoreType.DMA((n * N_CHIPS,)), pltpu.SemaphoreType.DMA((n * N_CHIPS,))],
    )(*arrs)


def _pair_sum(a, other, parity, name):
    _, r, c = a.shape
    tr = _row_tile(r, c, 3)

    def body(p_ref, a_ref, o_ref, s_ref):
        s_ref[...] = (a_ref[...] + o_ref[...]).astype(BF16)

    return pl.pallas_call(
        body, name=name,
        grid_spec=pltpu.PrefetchScalarGridSpec(
            num_scalar_prefetch=1, grid=(N_CHIPS, r // tr),
            in_specs=[pl.BlockSpec((1, tr, c), lambda q, i, p: (2 * q + p[0], i, 0)),
                      pl.BlockSpec((1, tr, c), lambda q, i, p: (q, i, 0))],
            out_specs=pl.BlockSpec((1, tr, c), lambda q, i, p: (q, i, 0))),
        out_shape=jax.ShapeDtypeStruct((N_CHIPS, r, c), BF16),
        compiler_params=_params(2),
    )(parity, a, other)


def _chip_swap(arrs, name):
    n = len(arrs)
    per = N_CHIPS - 1

    def body(*refs):
        ins, outs = refs[:n], refs[n:2 * n]
        send_sems, recv_sems, local_sems = refs[2 * n:]
        x, y, c, _, chips = _mesh_place()
        mine = 2 * x + y
        sends = [_remote(ins[kk].at[2 * chip[0] + chip[1]], outs[kk].at[mine], send_sems, recv_sems, kk * per + j, (*chip, c))
                 for j, chip in enumerate(chips) for kk in range(n)]
        recvs = [_remote(ins[kk].at[mine], outs[kk].at[2 * chip[0] + chip[1]], send_sems, recv_sems, kk * per + j, (*chip, c))
                 for j, chip in enumerate(chips) for kk in range(n)]
        local = [pltpu.make_async_copy(ins[kk].at[mine], outs[kk].at[mine], local_sems.at[kk]) for kk in range(n)]
        for cp in sends + local:
            cp.start()
        for cp in recvs:
            cp.wait_recv()
        for cp in sends:
            cp.wait_send()
        for cp in local:
            cp.wait()

    return pl.pallas_call(
        body, name=name,
        in_specs=[pl.BlockSpec(memory_space=pl.ANY)] * n, out_specs=[pl.BlockSpec(memory_space=pl.ANY)] * n,
        out_shape=[jax.ShapeDtypeStruct(a.shape, a.dtype) for a in arrs],
        scratch_shapes=[pltpu.SemaphoreType.DMA((n * per,)), pltpu.SemaphoreType.DMA((n * per,)),
                        pltpu.SemaphoreType.DMA((n,))],
    )(*arrs)


def _adam_math(w, g, m, v):
    m = ADAM_B1 * m + (1.0 - ADAM_B1) * g
    v = ADAM_B2 * v + (1.0 - ADAM_B2) * (g * g)
    m_hat = m / (1.0 - ADAM_B1 ** ADAM_STEP)
    v_hat = v / (1.0 - ADAM_B2 ** ADAM_STEP)
    delta = -ADAM_LR * (m_hat / (jnp.sqrt(v_hat) + ADAM_EPS) + ADAM_WD * w)
    return delta, m, v


def _sum_devices(r_ref):
    acc = r_ref[0].astype(F32)
    for p in range(1, r_ref.shape[0]):
        acc = acc + r_ref[p].astype(F32)
    return acc


def _row_tile(rows, cols, n_bufs):
    budget = 24 * 1024 * 1024 // (n_bufs * 2 * cols * 4)
    t = rows
    while t > budget and t % 2 == 0 and (t // 2) % SUBLANES == 0:
        t //= 2
    return t


def _reduce_adam(recvs, w, m, v, name):
    nl, r, c = w.shape
    n_part = recvs[0].shape[0]
    tr = _row_tile(r, c, n_part * nl + 7)
    nt = r // tr

    def body(*refs):
        r_refs = refs[:nl]
        w_ref, m_ref, v_ref, g_ref, d_ref, mo_ref, vo_ref = refs[nl:]
        layer = pl.program_id(0)
        g = _sum_devices(r_refs[0])
        for ll in range(1, nl):
            g = jnp.where(layer == ll, _sum_devices(r_refs[ll]), g)
        delta, m2, v2 = _adam_math(w_ref[0], g, m_ref[0], v_ref[0])
        g_ref[0] = g
        d_ref[0] = delta
        mo_ref[0] = m2
        vo_ref[0] = v2

    def rspec(ll):
        return pl.BlockSpec((n_part, tr, c), lambda l, i: (0, jnp.where(l == ll, i, jnp.where(l < ll, 0, nt - 1)), 0))

    blk = pl.BlockSpec((1, tr, c), lambda l, i: (l, i, 0))
    return pl.pallas_call(
        body, name=name, grid=(nl, nt),
        in_specs=[rspec(ll) for ll in range(nl)] + [blk, blk, blk],
        out_specs=[blk] * 4,
        out_shape=[jax.ShapeDtypeStruct((nl, r, c), F32)] * 4,
        compiler_params=_params(2),
    )(*recvs, w, m, v)


def _sum8(recv, name):
    _, r, c = recv.shape

    def body(r_ref, o_ref):
        o_ref[...] = _sum_devices(r_ref)

    return pl.pallas_call(
        body, name=name, grid=(1,),
        in_specs=[_full(recv.shape)], out_specs=_full((r, c)),
        out_shape=jax.ShapeDtypeStruct((r, c), F32), compiler_params=_params(1),
    )(recv)


def _adam_call(w, g, m, v, name):
    r, c = w.shape

    def body(w_ref, g_ref, m_ref, v_ref, d_ref, mo_ref, vo_ref):
        delta, m2, v2 = _adam_math(w_ref[...], g_ref[...], m_ref[...], v_ref[...])
        d_ref[...] = delta
        mo_ref[...] = m2
        vo_ref[...] = v2

    return pl.pallas_call(
        body, name=name, grid=(1,),
        in_specs=[_full((r, c))] * 4, out_specs=[_full((r, c))] * 3,
        out_shape=[jax.ShapeDtypeStruct((r, c), F32)] * 3, compiler_params=_params(1),
    )(w, g, m, v)


def _tile_for(s, want):
    return min(want, s)


def _local_step_whole(x, c, target, wts):
    s, d = x.shape
    nl = wts["w_in_g"].shape[1]
    nd = wts["w_in_g"].shape[0]
    nh_ml = wts["w_qkv"].shape[2]
    t_big = _tile_for(s, 512)
    t_mid = _tile_for(s, 256)

    mod, cact = _mod_call(c, wts["w_ada_g"], wts["b_ada"])
    row = lambda a: a.reshape(1, -1)
    saved = []
    xl = x
    for l in range(nl):
        shift, scale, gate = (row(mod[l, kk * d:(kk + 1) * d]) for kk in range(3))
        u, hbf = _in_fwd(xl, row(wts["norm_g"][l]), scale, shift, wts["w_in_g"], l, t_mid)
        h_rg, y_rg = _rg_fwd(u, d, wts["rg_conv_w"][l], row(wts["rg_conv_b"][l]), wts["rg_w_a_bf"][l],
                             row(wts["rg_b_a"][l]), wts["rg_w_x_bf"][l], row(wts["rg_b_x"][l]),
                             row(wts["rg_lambda"][l]), t_mid)
        q, k, v, gcol = _ml_pre(u, d, wts["ml_conv_w"][l], row(wts["ml_conv_b"][l]), wts["w_qkv"][l, 0],
                                wts["w_qkv"][l, 1], wts["w_qkv"][l, 2], wts["wif_pad"][l], wts["bif_pad"][l], t_mid)
        grow = gcol[:, 0:16].T
        cell, y_ml, cs, ns, ms, mt = _ml_cell_fwd(q, k, v, gcol, grow, u, row(wts["ml_norm_g"][l]), nh_ml)
        x_new, y = _out_fwd(xl, y_rg, y_ml, gate, wts["w_out_g"], l, t_big)
        saved.append(dict(x=xl, u=u, hbf=hbf, h_rg=h_rg, y_rg=y_rg, q=q, k=k, v=v, gcol=gcol, grow=grow, cell=cell,
                          y_ml=y_ml, cs=cs, ns=ns, ms=ms, mt=mt, y=y, scale=scale, gate=gate))
        xl = x_new

    dx, loss_p, g_final = _loss_call(xl, row(wts["final_g"]), target, t_big)
    grads = [None] * nl
    cact_col = cact[0].reshape(d, 1)
    for l in reversed(range(nl)):
        sv = saved[l]
        dy_rg, dy_ml, gw_out, dgate = _out_bwd(dx, sv["gate"], sv["y"], sv["y_rg"], sv["y_ml"], wts["w_out_g"], l, t_big)
        dq, dk, dv, dgates, d_mlo, d_mlz, g_mlng = _ml_cell_bwd(
            dy_ml, sv["u"], sv["cell"], sv["q"], sv["k"], sv["v"], sv["gcol"], sv["grow"], sv["mt"], sv["cs"],
            sv["ns"], sv["ms"], row(wts["ml_norm_g"][l]), nh_ml)
        d_mlx, g_wq, g_wk, g_wv, g_wift, g_bif, g_mlcw, g_mlcb = _ml_pre_bwd(
            dq, dk, dv, dgates, sv["gcol"], sv["u"], sv["q"], sv["k"], sv["v"], wts["ml_conv_w"][l],
            row(wts["ml_conv_b"][l]), wts["w_qkv"][l, 0], wts["w_qkv"][l, 1], wts["w_qkv"][l, 2], wts["wift_pad"][l], t_mid)
        d_rgx, d_rgz, g_wa, g_wx, g_ba, g_bx, g_lam, g_rgcw, g_rgcb = _rg_bwd(
            dy_rg, sv["u"], sv["h_rg"], wts["rg_conv_w"][l], row(wts["rg_conv_b"][l]), wts["rg_w_a_bf"][l],
            row(wts["rg_b_a"][l]), wts["rg_w_x_bf"][l], row(wts["rg_b_x"][l]), row(wts["rg_lambda"][l]), t_mid)
        pieces = [d_rgx, d_rgz, d_mlx, d_mlo, d_mlz]
        dx, dscale, dshift, g_ng = _in_bwd(pieces, sv["x"], dx, row(wts["norm_g"][l]), sv["scale"], wts["w_in_g"], l, t_mid)
        half = nd // 2
        w_cols = wts["w_in_g"].shape[3]
        gw_in = jnp.concatenate([_in_bwd_w(pieces, sv["hbf"], w_cols, tuple(range(0, half)), t_big),
                                 _in_bwd_w(pieces, sv["hbf"], w_cols, tuple(range(half, nd)), t_big)], axis=0)
        dmod = jnp.concatenate([dshift[0:1], dscale[0:1], dgate[0:1]], axis=1)
        gw_ada = _ada_bwd_w(cact_col, dmod, nd)
        grads[l] = dict(w_ada=gw_ada, w_in=gw_in, w_out=gw_out, w_qkv=jnp.stack([g_wq, g_wk, g_wv]),
                        rg_conv_w=g_rgcw[0:CONV_WIDTH], ml_conv_w=g_mlcw[0:CONV_WIDTH], wif_t=g_wift[0:8],
                        norm_g=g_ng[0], b_ada=dmod[0], rg_conv_b=g_rgcb[0], rg_w_a=g_wa, rg_b_a=g_ba[0], rg_w_x=g_wx,
                        rg_b_x=g_bx[0], rg_lambda=g_lam[0], ml_conv_b=g_mlcb[0], ml_b_if=g_bif[0, 0:8],
                        ml_norm_g=g_mlng[0])
    return loss_p[0, 0], dx, grads, g_final[0]


REPLICATED = ("norm_g", "b_ada", "rg_conv_b", "rg_w_a", "rg_b_a", "rg_w_x", "rg_b_x", "rg_lambda", "ml_conv_b",
              "ml_b_if", "ml_norm_g", "final_g")
ROW_ALIGN = N_DEV * SUBLANES


def _to_rows(a):
    flat = a.reshape(-1)
    pad = (-flat.shape[0]) % LANES
    return jnp.pad(flat, (0, pad)).reshape(-1, LANES)


def _pack(arrays):
    rows = jnp.concatenate([_to_rows(a) for a in arrays], axis=0)
    return jnp.pad(rows, ((0, (-rows.shape[0]) % ROW_ALIGN), (0, 0)))


def _unpack(rows, like):
    out, at = [], 0
    for a in like:
        n = -(-a.size // LANES)
        out.append(rows[at:at + n].reshape(-1)[:a.size].reshape(a.shape))
        at += n
    return out


def _small_pack(rg_conv_w, ml_conv_w, ml_w_if):
    nl = rg_conv_w.shape[0]
    wif_t = jnp.swapaxes(ml_w_if, 1, 2).reshape(nl, -1, LANES)
    return jnp.concatenate([rg_conv_w, ml_conv_w, wif_t], axis=1)


def _small_unpack(p, if_rows):
    nl = p.shape[0]
    rg_cw = p[:, 0:CONV_WIDTH]
    ml_cw = p[:, CONV_WIDTH:2 * CONV_WIDTH]
    wif = jnp.swapaxes(p[:, 2 * CONV_WIDTH:].reshape(nl, 8, if_rows), 1, 2)
    return rg_cw, ml_cw, wif


def _assemble_weights(big, small, rep):
    w_ada_g, w_in_g, w_out_g, qkv_g = big
    nd, nl = small.shape[0], small.shape[1]
    d = w_in_g.shape[2]
    dh = qkv_g.shape[3]
    nh = d // dh
    rsh = qkv_g.shape[2] // (3 * nh)
    w_qkv = qkv_g.reshape(nd, nl, 3, nh, rsh, dh).transpose(1, 2, 3, 0, 4, 5).reshape(nl, 3, nh, nd * rsh, dh)
    cw = small[:, :, 0:2 * CONV_WIDTH].reshape(nd, nl, 2, CONV_WIDTH, LANES).transpose(1, 2, 3, 0, 4)
    cw = cw.reshape(nl, 2, CONV_WIDTH, nd * LANES)
    if_rows = (small.shape[2] - 2 * CONV_WIDTH) * LANES // 8
    wif_t = small[:, :, 2 * CONV_WIDTH:].reshape(nd, nl, 8, if_rows).transpose(1, 2, 0, 3).reshape(nl, 8, nd * if_rows)
    wift_pad = jnp.pad(wif_t, ((0, 0), (0, LANES - 8), (0, 0))).astype(BF16)
    wif_pad = jnp.swapaxes(wift_pad, 1, 2)
    bif_pad = jnp.pad(rep["ml_b_if"], ((0, 0), (0, LANES - 8))).reshape(nl, 1, LANES)
    wts = dict(rep)
    wts.update(w_ada_g=w_ada_g, w_in_g=w_in_g, w_out_g=w_out_g, w_qkv=w_qkv, rg_conv_w=cw[:, 0], ml_conv_w=cw[:, 1],
               wif_pad=wif_pad, wift_pad=wift_pad, bif_pad=bif_pad, rg_w_a_bf=rep["rg_w_a"].astype(BF16),
               rg_w_x_bf=rep["rg_w_x"].astype(BF16))
    return wts


def _qkv_slots(g_qkv, nd):
    three, nh, dh, _ = g_qkv.shape
    return g_qkv.reshape(three, nh, nd, dh // nd, dh).transpose(2, 0, 1, 3, 4).reshape(nd, three * nh * (dh // nd), dh)


def _small_slots(g):
    nd = N_DEV
    cw = jnp.stack([g["rg_conv_w"], g["ml_conv_w"]]).reshape(2, CONV_WIDTH, nd, LANES).transpose(2, 0, 1, 3)
    cw = cw.reshape(nd, 2 * CONV_WIDTH, LANES)
    wif = g["wif_t"].reshape(8, nd, -1).transpose(1, 0, 2).reshape(nd, -1, LANES)
    return jnp.concatenate([cw, wif], axis=1)


def _kernel_unhosted(x, c, norm_g, w_ada, b_ada, w_in, rg_conv_w, rg_conv_b, rg_w_a, rg_b_a, rg_w_x, rg_b_x, rg_lambda, ml_conv_w, ml_conv_b, ml_w_q, ml_w_k, ml_w_v, ml_w_if, ml_b_if, ml_norm_g, w_out, final_g, loss_target, m_norm_g, m_w_ada, m_b_ada, m_w_in, m_rg_conv_w, m_rg_conv_b, m_rg_w_a, m_rg_b_a, m_rg_w_x, m_rg_b_x, m_rg_lambda, m_ml_conv_w, m_ml_conv_b, m_ml_w_q, m_ml_w_k, m_ml_w_v, m_ml_w_if, m_ml_b_if, m_ml_norm_g, m_w_out, m_final_g, v_norm_g, v_w_ada, v_b_ada, v_w_in, v_rg_conv_w, v_rg_conv_b, v_rg_w_a, v_rg_b_a, v_rg_w_x, v_rg_b_x, v_rg_lambda, v_ml_conv_w, v_ml_conv_b, v_ml_w_q, v_ml_w_k, v_ml_w_v, v_ml_w_if, v_ml_b_if, v_ml_norm_g, v_w_out, v_final_g):
    given = dict(locals())
    nl = w_in.shape[0]
    rep = {n: given[n] for n in REPLICATED}

    def qkv_shard(prefix):
        return jnp.stack([given[prefix + "ml_w_q"], given[prefix + "ml_w_k"], given[prefix + "ml_w_v"]], axis=1).reshape(
            nl, -1, ml_w_q.shape[-1])

    *big, small = _gather_two_level(
        [w_ada.astype(BF16), w_in.astype(BF16), w_out.astype(BF16), qkv_shard("").astype(BF16),
         _small_pack(rg_conv_w, ml_conv_w, ml_w_if)], "gather_weights")
    wts = _assemble_weights(big, small, rep)

    loss_p, grad_x, grads, g_final = _local_step(x[0], c, loss_target[0], wts)
    loss = lax.psum(loss_p, MESH_AXES)

    keys = ("w_ada", "w_in", "w_out", "w_qkv", "small")
    parity = lax.axis_index("c").astype(jnp.int32).reshape(1)
    recv = []
    for l in range(nl):
        g = grads[l]
        parts = [g["w_ada"], g["w_in"], g["w_out"], _qkv_slots(g["w_qkv"], N_DEV), _small_slots(g)]
        other = _core_swap(parts, "core_swap_layer%d" % l)
        sums = [_pair_sum(a, o, parity, "pair_sum_%s_layer%d" % (key, l)) for key, a, o in zip(keys, parts, other)]
        recv.append(_chip_swap(sums, "chip_swap_layer%d" % l))
    shard = {"": dict(w_ada=w_ada, w_in=w_in, w_out=w_out, w_qkv=qkv_shard(""),
                      small=_small_pack(rg_conv_w, ml_conv_w, ml_w_if))}
    for p in ("m_", "v_"):
        shard[p] = dict(w_ada=given[p + "w_ada"], w_in=given[p + "w_in"], w_out=given[p + "w_out"], w_qkv=qkv_shard(p),
                        small=_small_pack(given[p + "rg_conv_w"], given[p + "ml_conv_w"], given[p + "ml_w_if"]))
    res = {}
    for ki, key in enumerate(keys):
        res[key] = _reduce_adam([recv[l][ki] for l in range(nl)], shard[""][key], shard["m_"][key], shard["v_"][key],
                                "reduce_adam_" + key)

    rep_g = dict(final_g=g_final)
    for n in REPLICATED[:-1]:
        rep_g[n] = jnp.stack([grads[l][n] for l in range(nl)])
    pack_g = _pack([rep_g[n] for n in REPLICATED])
    rows = pack_g.shape[0] // N_DEV
    mine = _sum8(_exchange([pack_g.reshape(N_DEV, rows, LANES)], False, "reduce_scatter_replicated")[0], "sum_replicated")
    g_rep = _exchange([mine], True, "gather_replicated")[0].reshape(N_DEV * rows, LANES)
    rep_like = [rep[n] for n in REPLICATED]
    d_rep, m_rep, v_rep = _adam_call(_pack(rep_like), g_rep, _pack([given["m_" + n] for n in REPLICATED]),
                                     _pack([given["v_" + n] for n in REPLICATED]), "adam_replicated")
    rep_out = [dict(zip(REPLICATED, _unpack(a, rep_like))) for a in (g_rep, d_rep, m_rep, v_rep)]

    if_rows = ml_w_if.shape[1]
    order = ("norm_g", "w_ada", "b_ada", "w_in", "rg_conv_w", "rg_conv_b", "rg_w_a", "rg_b_a", "rg_w_x", "rg_b_x",
             "rg_lambda", "ml_conv_w", "ml_conv_b", "ml_w_q", "ml_w_k", "ml_w_v", "ml_w_if", "ml_b_if", "ml_norm_g",
             "w_out", "final_g")
    outs = [loss, grad_x[None]]
    for kind in range(4):
        qkv = res["w_qkv"][kind].reshape((nl, 3) + ml_w_q.shape[1:])
        rg_cw, ml_cw, wif = _small_unpack(res["small"][kind], if_rows)
        sharded = dict(w_ada=res["w_ada"][kind], w_in=res["w_in"][kind], w_out=res["w_out"][kind], ml_w_q=qkv[:, 0],
                       ml_w_k=qkv[:, 1], ml_w_v=qkv[:, 2], rg_conv_w=rg_cw, ml_conv_w=ml_cw, ml_w_if=wif)
        for n in order:
            outs.append(sharded[n] if n in sharded else rep_out[kind][n])
    return tuple(outs)


def _slot(block):
    return 4 * block[0] + 2 * block[1] + block[2]


def _dma_sems(*counts):
    return [pltpu.SemaphoreType.DMA((n,)) for n in counts]


def _start_all(copies):
    for cp in copies:
        cp.start()


def _gather_ici_comm(arrs):
    n = len(arrs)

    def copies(ins, outs, sems):
        send_sems, recv_sems, local_sems = sems
        x, y, c, sibling, chips = _mesh_place()
        me = (x, y, c)
        peers = [(*chip, c) for chip in chips] + [sibling]
        local = [pltpu.make_async_copy(ins[kk], outs[kk].at[_slot(me)], local_sems.at[kk]) for kk in range(n)]
        sends = [_remote(ins[kk], outs[kk].at[_slot(me)], send_sems, recv_sems, kk * 4 + j, peer)
                 for j, peer in enumerate(peers) for kk in range(n)]
        recvs = [_remote(ins[kk], outs[kk].at[_slot(peer)], send_sems, recv_sems, kk * 4 + j, peer)
                 for j, peer in enumerate(peers) for kk in range(n)]
        return local, sends, recvs

    def start(ins, outs, sems):
        local, sends, _ = copies(ins, outs, sems)
        _start_all(sends + local)

    def finish(ins, outs, sems):
        local, sends, recvs = copies(ins, outs, sems)
        for cp in recvs:
            cp.wait_recv()
        for cp in sends:
            cp.wait_send()
        for cp in local:
            cp.wait()

    return _Comm(arrs, [jax.ShapeDtypeStruct((N_DEV,) + a.shape, a.dtype) for a in arrs], _dma_sems(4 * n, 4 * n, n),
                 start, finish)


def _gather_fwd_comm(bufs):
    n = len(bufs)

    def copies(ins, outs, sems):
        send_sems, recv_sems = sems
        _, _, c, sibling, chips = _mesh_place()
        sends = [_remote(ins[kk].at[_slot((*chip, c))], outs[kk].at[_slot((*chip, c))], send_sems, recv_sems, kk * 3 + j, sibling)
                 for j, chip in enumerate(chips) for kk in range(n)]
        recvs = [_remote(ins[kk].at[_slot((*chip, c))], outs[kk].at[_slot((*chip, 1 - c))], send_sems, recv_sems, kk * 3 + j, sibling)
                 for j, chip in enumerate(chips) for kk in range(n)]
        return sends, recvs

    def start(ins, outs, sems):
        _start_all(copies(ins, outs, sems)[0])

    def finish(ins, outs, sems):
        sends, recvs = copies(ins, outs, sems)
        for cp in recvs:
            cp.wait_recv()
        for cp in sends:
            cp.wait_send()

    return _Comm(bufs, [jax.ShapeDtypeStruct(a.shape, a.dtype) for a in bufs], _dma_sems(3 * n, 3 * n), start, finish,
                 aliases=[(i, i) for i in range(n)])


def _core_swap_comm(arrs):
    n = len(arrs)

    def copies(ins, outs, sems):
        send_sems, recv_sems = sems
        _, _, c, sibling, _ = _mesh_place()
        return [_remote(ins[kk].at[2 * q + (1 - c)], outs[kk].at[q], send_sems, recv_sems, kk * N_CHIPS + q, sibling)
                for q in range(N_CHIPS) for kk in range(n)]

    def start(ins, outs, sems):
        _start_all(copies(ins, outs, sems))

    def finish(ins, outs, sems):
        cps = copies(ins, outs, sems)
        for cp in cps:
            cp.wait_recv()
        for cp in cps:
            cp.wait_send()

    return _Comm(arrs, [jax.ShapeDtypeStruct((N_CHIPS,) + a.shape[1:], a.dtype) for a in arrs],
                 _dma_sems(N_CHIPS * n, N_CHIPS * n), start, finish)


def _chip_swap_comm(arrs):
    n = len(arrs)
    per = N_CHIPS - 1

    def copies(ins, outs, sems):
        send_sems, recv_sems, local_sems = sems
        x, y, c, _, chips = _mesh_place()
        mine = 2 * x + y
        sends = [_remote(ins[kk].at[2 * chip[0] + chip[1]], outs[kk].at[mine], send_sems, recv_sems, kk * per + j, (*chip, c))
                 for j, chip in enumerate(chips) for kk in range(n)]
        recvs = [_remote(ins[kk].at[mine], outs[kk].at[2 * chip[0] + chip[1]], send_sems, recv_sems, kk * per + j, (*chip, c))
                 for j, chip in enumerate(chips) for kk in range(n)]
        local = [pltpu.make_async_copy(ins[kk].at[mine], outs[kk].at[mine], local_sems.at[kk]) for kk in range(n)]
        return local, sends, recvs

    def start(ins, outs, sems):
        local, sends, _ = copies(ins, outs, sems)
        _start_all(sends + local)

    def finish(ins, outs, sems):
        local, sends, recvs = copies(ins, outs, sems)
        for cp in recvs:
            cp.wait_recv()
        for cp in sends:
            cp.wait_send()
        for cp in local:
            cp.wait()

    return _Comm(arrs, [jax.ShapeDtypeStruct(a.shape, a.dtype) for a in arrs], _dma_sems(per * n, per * n, n), start, finish)


def _ada_mod(c_all, w_ada, b_cols):
    nl, d, w = w_ada.shape

    def body(c_ref, w_ref, b_ref, m_ref, ca_ref):
        sub = _iota((SUBLANES, d), 0)
        cv = jnp.zeros((SUBLANES, d), F32)
        for b in range(N_DEV):
            cv = jnp.where(sub == b, c_ref[b], cv)
        ca = cv * _sigmoid(cv)
        ca_ref[...] = ca
        m_ref[...] = jnp.zeros_like(m_ref)
        for l in range(nl):
            ml = _mm_hi(ca, w_ref[l]) + b_ref[l:l + 1, :]
            for b in range(N_DEV):
                m_ref[b, l:l + 1, :] = _row(ml, b)

    return pl.pallas_call(
        body, name="adaln_mod_columns", grid=(1,),
        in_specs=[_full(c_all.shape), _full(w_ada.shape), _full(b_cols.shape)],
        out_specs=[_full((N_DEV, SUBLANES, w)), _full((SUBLANES, d))],
        out_shape=[jax.ShapeDtypeStruct((N_DEV, SUBLANES, w), F32), jax.ShapeDtypeStruct((SUBLANES, d), F32)],
        compiler_params=_params(1),
    )(c_all, w_ada, b_cols)


def _ada_grad_adam(cact_t, dmods, w, m, v):
    nl, d, wd = w.shape
    tr = _row_tile(d, wd, 8)

    def body(c_ref, dm_ref, w_ref, m_ref, v_ref, g_ref, d_ref, mo_ref, vo_ref):
        cv = c_ref[...]
        dm = dm_ref[0]
        g = _col(cv, 0) * _row(dm, 0)
        for b in range(1, N_DEV):
            g = g + _col(cv, b) * _row(dm, b)
        delta, m2, v2 = _adam_math(w_ref[0], g, m_ref[0], v_ref[0])
        g_ref[0] = g
        d_ref[0] = delta
        mo_ref[0] = m2
        vo_ref[0] = v2

    blk = pl.BlockSpec((1, tr, wd), lambda l, i: (l, i, 0))
    return pl.pallas_call(
        body, name="adaln_grad_adam", grid=(nl, d // tr),
        in_specs=[pl.BlockSpec((tr, N_DEV), lambda l, i: (i, 0)), pl.BlockSpec((1, N_DEV, wd), lambda l, i: (l, 0, 0)),
                  blk, blk, blk],
        out_specs=[blk] * 4, out_shape=[jax.ShapeDtypeStruct((nl, d, wd), F32)] * 4,
        compiler_params=_params(2),
    )(cact_t, dmods, w, m, v)


class _Plan:
    def __init__(self):
        self.hosted, self.after = {}, {}

    def host(self, key, comm, then=None):
        self.hosted.setdefault(key, []).append(comm)
        if then is not None:
            self.after.setdefault(key, []).append(then)

    def comms(self, key):
        return self.hosted.pop(key, None)

    def done(self, key):
        for fn in self.after.pop(key, []):
            fn()

    def flush(self):
        while self.hosted:
            key = next(iter(self.hosted))
            _call(lambda: None, self.comms(key), name="exchange_after_%s_%d" % key, grid=(1,), in_specs=[], out_specs=[],
                  out_shape=[], args=())
            self.done(key)


def _layer_fwd(l, xl, mod_l, wl, rep, plan):
    s, d = xl.shape
    t_big, t_mid = _tile_for(s, 512), _tile_for(s, 256)
    nh_ml = rep["ml_b_if"].shape[1] // 2
    row = lambda a: a.reshape(1, -1)
    hosted = lambda name: plan.comms((name, l)) if plan else None
    done = lambda name: plan.done((name, l)) if plan else None
    shift, scale, gate = (row(mod_l[kk * d:(kk + 1) * d]) for kk in range(3))
    u, hbf = _in_fwd(xl, row(rep["norm_g"][l]), scale, shift, wl["w_in_g"], 0, t_mid, hosted("in_proj_fwd"))
    done("in_proj_fwd")
    h_rg, y_rg = _rg_fwd(u, d, wl["rg_conv_w"], row(rep["rg_conv_b"][l]), rep["rg_w_a_bf"][l], row(rep["rg_b_a"][l]),
                         rep["rg_w_x_bf"][l], row(rep["rg_b_x"][l]), row(rep["rg_lambda"][l]), t_mid, hosted("rglru_fwd"))
    done("rglru_fwd")
    q, k, v, gcol = _ml_pre(u, d, wl["ml_conv_w"], row(rep["ml_conv_b"][l]), wl["w_qkv"][0], wl["w_qkv"][1],
                            wl["w_qkv"][2], wl["wif_pad"], wl["bif_pad"], t_mid, hosted("mlstm_proj_fwd"))
    done("mlstm_proj_fwd")
    grow = gcol[:, 0:16].T
    cell, y_ml, cs, ns, ms, mt = _ml_cell_fwd(q, k, v, gcol, grow, u, row(rep["ml_norm_g"][l]), nh_ml,
                                              hosted("mlstm_cell_fwd"))
    done("mlstm_cell_fwd")
    x_new, y = _out_fwd(xl, y_rg, y_ml, gate, wl["w_out_g"], 0, t_big, hosted("out_proj_fwd"))
    done("out_proj_fwd")
    saved = dict(x=xl, u=u, hbf=hbf, h_rg=h_rg, y_rg=y_rg, q=q, k=k, v=v, gcol=gcol, grow=grow, cell=cell, y_ml=y_ml,
                 cs=cs, ns=ns, ms=ms, mt=mt, y=y, scale=scale, gate=gate)
    return x_new, saved


def _layer_bwd(l, dx, sv, wl, rep, plan, grads=None, split_last=False):
    s, d = dx.shape
    t_big, t_mid = _tile_for(s, 512), _tile_for(s, 256)
    nh_ml = rep["ml_b_if"].shape[1] // 2
    nd, _, _, w_cols = wl["w_in_g"].shape
    grads = {} if grads is None else grads
    row = lambda a: a.reshape(1, -1)
    hosted = lambda name: plan.comms((name, l)) if plan else None
    done = lambda name: plan.done((name, l)) if plan else None
    dy_rg, dy_ml, gw_out, dgate = _out_bwd(dx, sv["gate"], sv["y"], sv["y_rg"], sv["y_ml"], wl["w_out_g"], 0, t_big,
                                           hosted("out_proj_bwd"))
    grads.update(w_out=gw_out)
    done("out_proj_bwd")
    dq, dk, dv, dgates, d_mlo, d_mlz, g_mlng = _ml_cell_bwd(
        dy_ml, sv["u"], sv["cell"], sv["q"], sv["k"], sv["v"], sv["gcol"], sv["grow"], sv["mt"], sv["cs"], sv["ns"],
        sv["ms"], row(rep["ml_norm_g"][l]), nh_ml, hosted("mlstm_cell_bwd"))
    done("mlstm_cell_bwd")
    d_mlx, g_wq, g_wk, g_wv, g_wift, g_bif, g_mlcw, g_mlcb = _ml_pre_bwd(
        dq, dk, dv, dgates, sv["gcol"], sv["u"], sv["q"], sv["k"], sv["v"], wl["ml_conv_w"], row(rep["ml_conv_b"][l]),
        wl["w_qkv"][0], wl["w_qkv"][1], wl["w_qkv"][2], wl["wift_pad"], t_mid, hosted("mlstm_proj_bwd"))
    done("mlstm_proj_bwd")
    d_rgx, d_rgz, g_wa, g_wx, g_ba, g_bx, g_lam, g_rgcw, g_rgcb = _rg_bwd(
        dy_rg, sv["u"], sv["h_rg"], wl["rg_conv_w"], row(rep["rg_conv_b"][l]), rep["rg_w_a_bf"][l], row(rep["rg_b_a"][l]),
        rep["rg_w_x_bf"][l], row(rep["rg_b_x"][l]), row(rep["rg_lambda"][l]), t_mid, hosted("rglru_bwd"))
    grads.update(w_qkv=jnp.stack([g_wq, g_wk, g_wv]), rg_conv_w=g_rgcw[0:CONV_WIDTH], ml_conv_w=g_mlcw[0:CONV_WIDTH],
                 wif_t=g_wift[0:8], rg_conv_b=g_rgcb[0], rg_w_a=g_wa, rg_b_a=g_ba[0], rg_w_x=g_wx, rg_b_x=g_bx[0],
                 rg_lambda=g_lam[0], ml_conv_b=g_mlcb[0], ml_b_if=g_bif[0, 0:8], ml_norm_g=g_mlng[0])
    done("rglru_bwd")
    pieces = [d_rgx, d_rgz, d_mlx, d_mlo, d_mlz]
    grads.update(w_in=_in_bwd_w(pieces, sv["hbf"], w_cols, tuple(range(nd)), t_big, hosted("in_proj_bwd_w")))
    done("in_proj_bwd_w")
    n_tiles = s // t_mid
    head = n_tiles // 2 if split_last and n_tiles > 1 else n_tiles
    in_args = (pieces, sv["x"], dx, row(rep["norm_g"][l]), sv["scale"], wl["w_in_g"], 0, t_mid)
    res = _in_bwd(*in_args, hosted("in_proj_bwd_x"), (0, head))
    done("in_proj_bwd_x")
    if head < n_tiles:
        res = _in_bwd(*in_args, hosted("in_proj_bwd_x_rest"), (head, n_tiles - head), res)
        done("in_proj_bwd_x_rest")
    dx, dscale, dshift, g_ng = res
    dmod = jnp.concatenate([dshift[0:1], dscale[0:1], dgate[0:1]], axis=1)
    grads.update(dmod=dmod, norm_g=g_ng[0], b_ada=dmod[0])
    return dx, grads


def _local_step(x, c, target, wts):
    d = x.shape[1]
    nl = wts["w_in_g"].shape[1]
    mod, cact = _mod_call(c, wts["w_ada_g"], wts["b_ada"])
    wl = [dict(w_in_g=wts["w_in_g"][:, l:l + 1], w_out_g=wts["w_out_g"][:, l:l + 1], w_qkv=wts["w_qkv"][l],
               rg_conv_w=wts["rg_conv_w"][l], ml_conv_w=wts["ml_conv_w"][l], wif_pad=wts["wif_pad"][l],
               wift_pad=wts["wift_pad"][l], bif_pad=wts["bif_pad"][l]) for l in range(nl)]
    saved, xl = [], x
    for l in range(nl):
        xl, sv = _layer_fwd(l, xl, mod[l], wl[l], wts, None)
        saved.append(sv)
    dx, loss_p, g_final = _loss_call(xl, wts["final_g"].reshape(1, -1), target, _tile_for(x.shape[0], 512))
    grads = [None] * nl
    for l in reversed(range(nl)):
        dx, grads[l] = _layer_bwd(l, dx, saved[l], wl[l], wts, None)
        grads[l]["w_ada"] = _ada_bwd_w(cact[0].reshape(d, 1), grads[l]["dmod"], wts["w_ada_g"].shape[0])
        grads[l]["b_ada"] = grads[l]["dmod"][0]
    return loss_p[0, 0], dx, grads, g_final[0]


def _full_qkv(qkv_g, d):
    nd, _, rows3, dh = qkv_g.shape
    nh = d // dh
    rsh = rows3 // (3 * nh)
    return qkv_g.reshape(nd, 3, nh, rsh, dh).transpose(1, 2, 0, 3, 4).reshape(3, nh, nd * rsh, dh)


def _small_weights(small, l, ml_b_if):
    nd = small.shape[0]
    sm = small[:, l]
    cw = sm[:, 0:2 * CONV_WIDTH].reshape(nd, 2, CONV_WIDTH, LANES).transpose(1, 2, 0, 3).reshape(2, CONV_WIDTH, nd * LANES)
    if_rows = (sm.shape[1] - 2 * CONV_WIDTH) * LANES // 8
    wif_t = sm[:, 2 * CONV_WIDTH:].reshape(nd, 8, if_rows).transpose(1, 0, 2).reshape(8, nd * if_rows)
    wift_pad = jnp.pad(wif_t, ((0, LANES - 8), (0, 0))).astype(BF16)
    return dict(rg_conv_w=cw[0], ml_conv_w=cw[1], wift_pad=wift_pad, wif_pad=wift_pad.T,
                bif_pad=jnp.pad(ml_b_if[l], (0, LANES - 8)).reshape(1, LANES))


def kernel(x, c, norm_g, w_ada, b_ada, w_in, rg_conv_w, rg_conv_b, rg_w_a, rg_b_a, rg_w_x, rg_b_x, rg_lambda, ml_conv_w, ml_conv_b, ml_w_q, ml_w_k, ml_w_v, ml_w_if, ml_b_if, ml_norm_g, w_out, final_g, loss_target, m_norm_g, m_w_ada, m_b_ada, m_w_in, m_rg_conv_w, m_rg_conv_b, m_rg_w_a, m_rg_b_a, m_rg_w_x, m_rg_b_x, m_rg_lambda, m_ml_conv_w, m_ml_conv_b, m_ml_w_q, m_ml_w_k, m_ml_w_v, m_ml_w_if, m_ml_b_if, m_ml_norm_g, m_w_out, m_final_g, v_norm_g, v_w_ada, v_b_ada, v_w_in, v_rg_conv_w, v_rg_conv_b, v_rg_w_a, v_rg_b_a, v_rg_w_x, v_rg_b_x, v_rg_lambda, v_ml_conv_w, v_ml_conv_b, v_ml_w_q, v_ml_w_k, v_ml_w_v, v_ml_w_if, v_ml_b_if, v_ml_norm_g, v_w_out, v_final_g):
    given = dict(locals())
    nl = w_in.shape[0]
    d = x.shape[2]
    rep = {n: given[n] for n in REPLICATED}
    rep.update(rg_w_a_bf=rg_w_a.astype(BF16), rg_w_x_bf=rg_w_x.astype(BF16))
    bf = lambda a: a.astype(BF16)

    def qkv_shard(prefix):
        return jnp.stack([given[prefix + "ml_w_q"], given[prefix + "ml_w_k"], given[prefix + "ml_w_v"]], axis=1).reshape(
            nl, -1, ml_w_q.shape[-1])

    def small_shard(prefix):
        return _small_pack(given[prefix + "rg_conv_w"], given[prefix + "ml_conv_w"], given[prefix + "ml_w_if"])

    plan = _Plan()
    qkv = qkv_shard("")
    w_in_first, small = _gather_two_level([bf(w_in[0:1]), small_shard("")], "gather_first")
    wl = [_small_weights(small, l, ml_b_if) for l in range(nl)]
    wl[0]["w_in_g"] = w_in_first

    def gather_behind(arrs, ici_host, fwd_host, then):
        ici = _gather_ici_comm(arrs)

        def pass_on():
            fwd = _gather_fwd_comm(ici.results)
            plan.host(fwd_host, fwd, lambda: then(fwd.results))

        plan.host(ici_host, ici, pass_on)

    def got_out(l):
        return lambda r: wl[l].update(w_out_g=r[0], w_qkv=_full_qkv(r[1], d))

    gather_behind([bf(w_out[0:1]), bf(qkv[0:1])], ("in_proj_fwd", 0), ("rglru_fwd", 0), got_out(0))
    for l in range(1, nl):
        gather_behind([bf(w_in[l:l + 1])], ("rglru_fwd", l - 1), ("mlstm_proj_fwd", l - 1),
                      lambda r, l=l: wl[l].update(w_in_g=r[0]))
        gather_behind([bf(w_out[l:l + 1]), bf(qkv[l:l + 1])], ("mlstm_cell_fwd", l - 1), ("out_proj_fwd", l - 1), got_out(l))

    wcols = w_ada.shape[2]
    c_all = _exchange([jnp.broadcast_to(c, (SUBLANES, d))], True, "gather_condition")[0]
    me = 4 * lax.axis_index("x") + 2 * lax.axis_index("y") + lax.axis_index("c")
    b_cols = jnp.pad(lax.dynamic_slice_in_dim(b_ada, me * wcols, wcols, axis=1), ((0, SUBLANES - nl), (0, 0)))
    mod_cols, cact_all = _ada_mod(c_all, w_ada, b_cols)
    mod_blocks = _exchange([mod_cols], False, "scatter_modulation")[0]
    mod = mod_blocks[:, 0:nl].transpose(1, 0, 2).reshape(nl, N_DEV * wcols)
    saved, xl = [], x[0]
    for l in range(nl):
        xl, sv = _layer_fwd(l, xl, mod[l], wl[l], rep, plan)
        saved.append(sv)
    grad_x, loss_p, g_final = _loss_call(xl, final_g.reshape(1, -1), loss_target[0], _tile_for(xl.shape[0], 512))

    keys = ("w_in", "w_out", "w_qkv", "small")
    parity = lax.axis_index("c").astype(jnp.int32).reshape(1)
    grads, recv = [None] * nl, [None] * nl

    def parts_of(g):
        return [g["w_in"], g["w_out"], _qkv_slots(g["w_qkv"], N_DEV), _small_slots(g)]

    def pair_sums(l, parts, other):
        return [_pair_sum(a, o, parity, "pair_sum_%s_layer%d" % (key, l)) for key, a, o in zip(keys, parts, other)]

    def reduce_behind(l, host_layer):
        parts = parts_of(grads[l])
        swap = _core_swap_comm(parts)

        def summed():
            sums = pair_sums(l, parts, swap.results)
            big = _chip_swap_comm([sums[0]])
            rest = _chip_swap_comm(sums[1:])
            plan.host(("mlstm_cell_bwd", host_layer), big)
            plan.host(("rglru_bwd", host_layer), rest, lambda: recv.__setitem__(l, big.results + rest.results))

        plan.host(("out_proj_bwd", host_layer), swap, summed)

    first, own = {}, {}

    def reduce_own(names, parts_fn, ready_key, swap_key, chip_key):
        def go():
            parts = parts_fn()
            swap = _core_swap_comm(parts)

            def summed():
                sums = [_pair_sum(a, o, parity, "pair_sum_%s_layer0" % n) for n, a, o in zip(names, parts, swap.results)]
                chip = _chip_swap_comm(sums)
                plan.host(chip_key, chip, lambda: own.update(zip(names, chip.results)))

            plan.host(swap_key, swap, summed)

        plan.after.setdefault(ready_key, []).append(go)

    reduce_own(["w_out"], lambda: [first["w_out"]], ("out_proj_bwd", 0), ("mlstm_cell_bwd", 0), ("mlstm_proj_bwd", 0))
    reduce_own(["w_qkv", "small"], lambda: [_qkv_slots(first["w_qkv"], N_DEV), _small_slots(first)],
               ("rglru_bwd", 0), ("in_proj_bwd_w", 0), ("in_proj_bwd_x", 0))
    reduce_own(["w_in"], lambda: [first["w_in"]], ("in_proj_bwd_w", 0), ("in_proj_bwd_x", 0), ("in_proj_bwd_x_rest", 0))

    for l in reversed(range(nl)):
        if l > 0:
            grad_x, grads[l] = _layer_bwd(l, grad_x, saved[l], wl[l], rep, plan)
            reduce_behind(l, l - 1)
        else:
            grad_x, grads[l] = _layer_bwd(l, grad_x, saved[l], wl[l], rep, plan, first, True)
    plan.flush()
    recv[0] = [own[key] for key in keys]

    shard = {p: dict(w_in=given[p + "w_in"], w_out=given[p + "w_out"], w_qkv=qkv_shard(p), small=small_shard(p))
             for p in ("", "m_", "v_")}
    res = {}
    for ki, key in enumerate(keys):
        res[key] = _reduce_adam([recv[l][ki] for l in range(nl)], shard[""][key], shard["m_"][key], shard["v_"][key],
                                "reduce_adam_" + key)

    dmods = jnp.concatenate([grads[l]["dmod"] for l in range(nl)], axis=0)
    dmod_blocks = jnp.pad(dmods.reshape(nl, N_DEV, wcols).transpose(1, 0, 2), ((0, 0), (0, SUBLANES - nl), (0, 0)))
    dmod_all = _exchange([dmod_blocks], False, "scatter_dmod")[0][:, 0:nl].transpose(1, 0, 2)
    res["w_ada"] = _ada_grad_adam(cact_all.T, dmod_all, w_ada, m_w_ada, v_w_ada)

    rep_g = dict(final_g=g_final[0])
    for n in REPLICATED[:-1]:
        rep_g[n] = jnp.stack([grads[l][n] for l in range(nl)])
    blank = [jnp.zeros((1,), F32)]
    pack_g = _pack([rep_g[n] for n in REPLICATED] + [loss_p[0, 0:1]])
    rows = pack_g.shape[0] // N_DEV
    mine = _sum8(_exchange([pack_g.reshape(N_DEV, rows, LANES)], False, "reduce_scatter_replicated")[0], "sum_replicated")
    g_rep = _exchange([mine], True, "gather_replicated")[0].reshape(N_DEV * rows, LANES)
    rep_like = [given[n] for n in REPLICATED]
    d_rep, m_rep, v_rep = _adam_call(_pack(rep_like + blank), g_rep, _pack([given["m_" + n] for n in REPLICATED] + blank),
                                     _pack([given["v_" + n] for n in REPLICATED] + blank), "adam_replicated")
    rep_out = [dict(zip(REPLICATED, _unpack(a, rep_like))) for a in (g_rep, d_rep, m_rep, v_rep)]
    loss = _unpack(g_rep, rep_like + blank)[-1].reshape(())

    if_rows = ml_w_if.shape[1]
    order = ("norm_g", "w_ada", "b_ada", "w_in", "rg_conv_w", "rg_conv_b", "rg_w_a", "rg_b_a", "rg_w_x", "rg_b_x",
             "rg_lambda", "ml_conv_w", "ml_conv_b", "ml_w_q", "ml_w_k", "ml_w_v", "ml_w_if", "ml_b_if", "ml_norm_g",
             "w_out", "final_g")
    outs = [loss, grad_x[None]]
    for kind in range(4):
        qkv_k = res["w_qkv"][kind].reshape((nl, 3) + ml_w_q.shape[1:])
        rg_cw, ml_cw, wif = _small_unpack(res["small"][kind], if_rows)
        sharded = dict(w_ada=res["w_ada"][kind], w_in=res["w_in"][kind], w_out=res["w_out"][kind], ml_w_q=qkv_k[:, 0],
                       ml_w_k=qkv_k[:, 1], ml_w_v=qkv_k[:, 2], rg_conv_w=rg_cw, ml_conv_w=ml_cw, ml_w_if=wif)
        for n in order:
            outs.append(sharded[n] if n in sharded else rep_out[kind][n])
    return tuple(outs)
```

```python
import functools

import jax
import jax.numpy as jnp
from jax import lax
from jax.experimental import pallas as pl
from jax.experimental.pallas import tpu as pltpu

F32 = jnp.float32
BF16 = jnp.bfloat16
MESH_AXES = ("x", "y", "c")
N_DEV = 8
EPS = 1e-6
RG_C = 8.0
ML_CHUNK = 128
CONV_WIDTH = 4
ADAM_LR = 0.001
ADAM_B1 = 0.9
ADAM_B2 = 0.999
ADAM_EPS = 1e-08
ADAM_WD = 0.01
ADAM_STEP = 10
NEG_BIG = -1e30
LANES = 128
SUBLANES = 8
VMEM_LIMIT = 56 * 1024 * 1024
HI = lax.Precision.HIGHEST


def _params(n_grid):
    return pltpu.CompilerParams(dimension_semantics=("arbitrary",) * n_grid, vmem_limit_bytes=VMEM_LIMIT)


def _mm(a, b):
    return jnp.dot(a.astype(BF16), b.astype(BF16), preferred_element_type=F32)


def _mm_nt(a, b):
    return lax.dot_general(a.astype(BF16), b.astype(BF16), (((1,), (1,)), ((), ())), preferred_element_type=F32)


def _mm_tn(a, b):
    return lax.dot_general(a.astype(BF16), b.astype(BF16), (((0,), (0,)), ((), ())), preferred_element_type=F32)


def _mm_hi(a, b):
    return jnp.dot(a, b, precision=HI, preferred_element_type=F32)


def _sigmoid(x):
    return 1.0 / (1.0 + jnp.exp(-x))


def _softplus(x):
    return jnp.maximum(x, 0.0) + jnp.log(1.0 + jnp.exp(-jnp.abs(x)))


def _neg_expm1(x):
    poly = -x * (1.0 + x * (0.5 + x * (1.0 / 6.0 + x * (1.0 / 24.0 + x * (1.0 / 120.0)))))
    return jnp.where(jnp.abs(x) < 0.05, poly, 1.0 - jnp.exp(x))


def _iota(shape, dim):
    return lax.broadcasted_iota(jnp.int32, shape, dim)


def _colsum(x):
    return jnp.sum(x, axis=0, keepdims=True)


def _rowsum(x):
    return jnp.sum(x, axis=1, keepdims=True)


def _col(x, j):
    return _rowsum(jnp.where(_iota(x.shape, 1) == j, x, 0.0))


def _row(x, j):
    return _colsum(jnp.where(_iota(x.shape, 0) == j, x, 0.0))


def _shift_down(x, j, prev8):
    if j == 0:
        return x
    t = x.shape[0]
    main = jnp.where(_iota(x.shape, 0) >= j, pltpu.roll(x, j, 0), 0.0)
    fix = jnp.where(_iota(prev8.shape, 0) < j, pltpu.roll(prev8, j, 0), 0.0)
    return jnp.concatenate([main[0:SUBLANES] + fix, main[SUBLANES:t]], axis=0)


def _shift_up(x, j, next8):
    if j == 0:
        return x
    t = x.shape[0]
    main = jnp.where(_iota(x.shape, 0) < t - j, pltpu.roll(x, t - j, 0), 0.0)
    fix = jnp.where(_iota(next8.shape, 0) >= SUBLANES - j, pltpu.roll(next8, SUBLANES - j, 0), 0.0)
    return jnp.concatenate([main[0:t - SUBLANES], main[t - SUBLANES:t] + fix], axis=0)


def _conv(x, prev8, w_ref):
    y = w_ref[CONV_WIDTH - 1:CONV_WIDTH, :] * x
    for j in range(1, CONV_WIDTH):
        y = y + w_ref[CONV_WIDTH - 1 - j:CONV_WIDTH - j, :] * _shift_down(x, j, prev8)
    return y


def _conv_bwd_x(dy, next8, w_ref):
    dx = w_ref[CONV_WIDTH - 1:CONV_WIDTH, :] * dy
    for j in range(1, CONV_WIDTH):
        dx = dx + w_ref[CONV_WIDTH - 1 - j:CONV_WIDTH - j, :] * _shift_up(dy, j, next8)
    return dx


def _scan_into(a, b, carry, out_ref, reverse):
    t, c = a.shape
    groups = t // SUBLANES
    a3 = a.reshape(groups, SUBLANES, c)
    b3 = b.reshape(groups, SUBLANES, c)
    sub = _iota(a3.shape, 1)
    for step in (1, 2, 4):
        keep = sub < SUBLANES - step if reverse else sub >= step
        shift = SUBLANES - step if reverse else step
        a_s = jnp.where(keep, pltpu.roll(a3, shift, 1), 1.0)
        b_s = jnp.where(keep, pltpu.roll(b3, shift, 1), 0.0)
        b3 = a3 * b_s + b3
        a3 = a3 * a_s
    for g in (reversed(range(groups)) if reverse else range(groups)):
        rows = slice(g * SUBLANES, (g + 1) * SUBLANES)
        out_ref[rows, :] = b3[g] + a3[g] * carry
        edge = g * SUBLANES if reverse else (g + 1) * SUBLANES - 1
        carry = out_ref[edge:edge + 1, :]


def _blockdiag(x, w_ref, transpose_w=False):
    nh, dh, _ = w_ref.shape
    outs = []
    for h in range(nh):
        xs = x[:, h * dh:(h + 1) * dh]
        outs.append(_mm_nt(xs, w_ref[h]) if transpose_w else _mm(xs, w_ref[h]))
    return jnp.concatenate(outs, axis=1)


def _rg_gates(xc, wa_ref, ba_ref, wx_ref, bx_ref, lam_ref):
    r = _sigmoid(_blockdiag(xc, wa_ref) + ba_ref[...])
    ig = _sigmoid(_blockdiag(xc, wx_ref) + bx_ref[...])
    sp = _softplus(-lam_ref[...])
    log_a = -RG_C * r * sp
    a = jnp.exp(log_a)
    beta = jnp.sqrt(_neg_expm1(2.0 * log_a))
    return r, ig, sp, a, beta


def _bcast8(row):
    return jnp.broadcast_to(row, (SUBLANES, row.shape[1]))


def _full(shape):
    nd = len(shape)
    return pl.BlockSpec(shape, lambda *_: (0,) * nd)


class _Comm:
    def __init__(self, arrays, out_shapes, sems, start, finish, aliases=()):
        self.arrays, self.out_shapes, self.sems = list(arrays), list(out_shapes), list(sems)
        self.start, self.finish, self.aliases = start, finish, tuple(aliases)
        self.results = None


def _call(body, comms, *, name, grid, in_specs, out_specs, out_shape, args, scratch_shapes=(), aliases=None):
    comms = [cm for cm in (comms or []) if cm is not None]
    n_in, n_out, n_sc = len(args), len(out_shape), len(scratch_shapes)
    c_arrays = [a for cm in comms for a in cm.arrays]
    c_outs = [o for cm in comms for o in cm.out_shapes]
    c_sems = [sm for cm in comms for sm in cm.sems]
    aliases, a_at, o_at = dict(aliases or {}), n_in, n_out
    for cm in comms:
        for (i, j) in cm.aliases:
            aliases[a_at + i] = o_at + j
        a_at += len(cm.arrays)
        o_at += len(cm.out_shapes)

    def wrapped(*refs):
        ins, c_in = refs[:n_in], refs[n_in:n_in + len(c_arrays)]
        at = n_in + len(c_arrays)
        outs, c_out = refs[at:at + n_out], refs[at + n_out:at + n_out + len(c_outs)]
        at += n_out + len(c_outs)
        scr, sems = refs[at:at + n_sc], refs[at + n_sc:]
        views, ia, io, isem = [], 0, 0, 0
        for cm in comms:
            views.append((c_in[ia:ia + len(cm.arrays)], c_out[io:io + len(cm.out_shapes)], sems[isem:isem + len(cm.sems)]))
            ia, io, isem = ia + len(cm.arrays), io + len(cm.out_shapes), isem + len(cm.sems)
        if comms:
            @pl.when(pl.program_id(0) == 0)
            def _():
                for cm, view in zip(comms, views):
                    cm.start(*view)

        body(*ins, *outs, *scr)
        if comms:
            @pl.when(pl.program_id(0) == grid[0] - 1)
            def _():
                for cm, view in zip(comms, views):
                    cm.finish(*view)

    hbm = pl.BlockSpec(memory_space=pl.ANY)
    res = pl.pallas_call(
        wrapped, name=name, grid=grid,
        in_specs=list(in_specs) + [hbm] * len(c_arrays), out_specs=list(out_specs) + [hbm] * len(c_outs),
        out_shape=list(out_shape) + c_outs, scratch_shapes=list(scratch_shapes) + c_sems,
        input_output_aliases=aliases, compiler_params=_params(len(grid)),
    )(*args, *c_arrays)
    at = n_out
    for cm in comms:
        cm.results = list(res[at:at + len(cm.out_shapes)])
        at += len(cm.out_shapes)
    return list(res[:n_out])


def _mod_call(c, w_ada_g, b_ada):
    nd, nl, d, w = w_ada_g.shape

    def body(c_ref, w_ref, b_ref, mod_ref, cact_ref):
        cv = c_ref[...]
        ca = _bcast8(cv * _sigmoid(cv))
        cact_ref[...] = ca
        mod_ref[0, 0] = _mm(ca, w_ref[0, 0]) + b_ref[0, 0]

    mod, cact = pl.pallas_call(
        body, name="adaln_mod", grid=(nl, nd),
        in_specs=[_full((1, d)),
                  pl.BlockSpec((1, 1, d, w), lambda l, j: (j, l, 0, 0)),
                  pl.BlockSpec((1, 1, 1, w), lambda l, j: (l, j, 0, 0))],
        out_specs=[pl.BlockSpec((1, 1, SUBLANES, w), lambda l, j: (l, j, 0, 0)), _full((SUBLANES, d))],
        out_shape=[jax.ShapeDtypeStruct((nl, nd, SUBLANES, w), F32), jax.ShapeDtypeStruct((SUBLANES, d), F32)],
        compiler_params=_params(2),
    )(c, w_ada_g, b_ada.reshape(nl, nd, 1, w))
    return mod[:, :, 0, :].reshape(nl, nd * w), cact


def _join_columns(w_ref, wcat):
    nd, _, _, w = w_ref.shape

    @pl.when(pl.program_id(0) == 0)
    def _():
        for j in range(nd):
            wcat[:, j * w:(j + 1) * w] = w_ref[j, 0]


def _in_fwd(x, ng, scale, shift, w_in_g, layer, tile, comms=None):
    s, d = x.shape
    nd, _, _, w = w_in_g.shape

    def body(x_ref, ng_ref, sc_ref, sh_ref, w_ref, u_ref, h_ref, wcat):
        _join_columns(w_ref, wcat)
        xv = x_ref[...]
        rs = lax.rsqrt(jnp.mean(xv * xv, axis=1, keepdims=True) + EPS)
        hb = (xv * rs * ng_ref[...] * (1.0 + sc_ref[...]) + sh_ref[...]).astype(BF16)
        h_ref[...] = hb
        u_ref[...] = jnp.dot(hb, wcat[...], preferred_element_type=F32)

    return _call(
        body, comms, name="in_proj_fwd", grid=(s // tile,),
        in_specs=[pl.BlockSpec((tile, d), lambda i: (i, 0)), _full((1, d)), _full((1, d)), _full((1, d)),
                  pl.BlockSpec((nd, 1, d, w), lambda i: (0, layer, 0, 0), pipeline_mode=pl.Buffered(1))],
        out_specs=[pl.BlockSpec((tile, nd * w), lambda i: (i, 0)), pl.BlockSpec((tile, d), lambda i: (i, 0))],
        out_shape=[jax.ShapeDtypeStruct((s, nd * w), F32), jax.ShapeDtypeStruct((s, d), BF16)],
        scratch_shapes=[pltpu.VMEM((d, nd * w), BF16)],
        args=(x, ng, scale, shift, w_in_g))


def _rg_fwd(u, d, conv_w, conv_b, w_a, b_a, w_x, b_x, lam, tile, comms=None):
    s = u.shape[0]

    def body(x_ref, z_ref, cw_ref, cb_ref, wa_ref, ba_ref, wx_ref, bx_ref, lam_ref, h_ref, y_ref, prev8, hcar):
        @pl.when(pl.program_id(0) == 0)
        def _():
            prev8[...] = jnp.zeros_like(prev8)
            hcar[...] = jnp.zeros_like(hcar)

        x = x_ref[...]
        xc = _conv(x, prev8[...], cw_ref) + cb_ref[...]
        prev8[...] = x[tile - SUBLANES:tile, :]
        _, ig, _, a, beta = _rg_gates(xc, wa_ref, ba_ref, wx_ref, bx_ref, lam_ref)
        _scan_into(a, beta * ig * xc, hcar[SUBLANES - 1:SUBLANES, :], h_ref, False)
        h = h_ref[...]
        hcar[...] = h[tile - SUBLANES:tile, :]
        z = z_ref[...]
        y_ref[...] = h * z * _sigmoid(z)

    vec = _full((1, d))
    return _call(
        body, comms, name="rglru_fwd", grid=(s // tile,),
        in_specs=[pl.BlockSpec((tile, d), lambda i: (i, 0)), pl.BlockSpec((tile, d), lambda i: (i, 1)),
                  _full(conv_w.shape), vec, _full(w_a.shape), vec, _full(w_x.shape), vec, vec],
        out_specs=[pl.BlockSpec((tile, d), lambda i: (i, 0))] * 2,
        out_shape=[jax.ShapeDtypeStruct((s, d), F32)] * 2,
        scratch_shapes=[pltpu.VMEM((SUBLANES, d), F32), pltpu.VMEM((SUBLANES, d), F32)],
        args=(u, u, conv_w, conv_b, w_a, b_a, w_x, b_x, lam))


def _ml_pre(u, d, conv_w, conv_b, w_q, w_k, w_v, wif, bif, tile, comms=None):
    s = u.shape[0]
    nh = w_q.shape[0]

    def body(x_ref, cw_ref, cb_ref, wq_ref, wk_ref, wv_ref, wif_ref, bif_ref, q_ref, k_ref, v_ref, g_ref, prev8):
        @pl.when(pl.program_id(0) == 0)
        def _():
            prev8[...] = jnp.zeros_like(prev8)

        x = x_ref[...]
        pre = _conv(x, prev8[...], cw_ref) + cb_ref[...]
        prev8[...] = x[tile - SUBLANES:tile, :]
        xc = pre * _sigmoid(pre)
        q = _blockdiag(xc, wq_ref)
        k = _blockdiag(xc, wk_ref)
        v = _blockdiag(x, wv_ref)
        q_ref[...] = q
        k_ref[...] = k
        v_ref[...] = v
        g = _mm(q, wif_ref[0:d, :]) + _mm(k, wif_ref[d:2 * d, :]) + _mm(v, wif_ref[2 * d:3 * d, :]) + bif_ref[...]
        lane = _iota(g.shape, 1)
        gl = jnp.where(lane < 4, g, jnp.where(lane < 8, -_softplus(-g), 0.0))
        tri = jnp.where(_iota((ML_CHUNK, ML_CHUNK), 1) <= _iota((ML_CHUNK, ML_CHUNK), 0), 1.0, 0.0)
        cums = [_mm_hi(tri, gl[c * ML_CHUNK:(c + 1) * ML_CHUNK, :]) for c in range(tile // ML_CHUNK)]
        cum = cums[0] if len(cums) == 1 else jnp.concatenate(cums, axis=0)
        g_ref[...] = gl + jnp.where((lane >= 8) & (lane < 12), pltpu.roll(cum, 4, 1), 0.0)

    vec = _full((1, d))
    return _call(
        body, comms, name="mlstm_proj_fwd", grid=(s // tile,),
        in_specs=[pl.BlockSpec((tile, d), lambda i: (i, 2)), _full(conv_w.shape), vec,
                  _full(w_q.shape), _full(w_k.shape), _full(w_v.shape), _full(wif.shape), _full((1, LANES))],
        out_specs=[pl.BlockSpec((tile, d), lambda i: (i, 0))] * 3 + [pl.BlockSpec((tile, LANES), lambda i: (i, 0))],
        out_shape=[jax.ShapeDtypeStruct((s, d), F32)] * 3 + [jax.ShapeDtypeStruct((s, LANES), F32)],
        scratch_shapes=[pltpu.VMEM((SUBLANES, d), F32)],
        args=(u, conv_w, conv_b, w_q, w_k, w_v, wif, bif))


def _cell_chunk(h, nh, q_ref, k_ref, v_ref, gc, gr, m_prev, c_h, n_h, m_t=None):
    lc = ML_CHUNK
    dh = q_ref.shape[1] // nh
    sl = slice(h * dh, (h + 1) * dh)
    qh = q_ref[:, sl]
    kh = k_ref[:, sl] * (dh ** -0.5)
    vh = v_ref[:, sl]
    li_c = _col(gc, h)
    b_c = _col(gc, 8 + h)
    lib_r = _row(gr, h) - _row(gr, 8 + h)
    b_last = _colsum(jnp.where(_iota((lc, 1), 0) == lc - 1, b_c, 0.0))
    causal = _iota((lc, lc), 1) <= _iota((lc, lc), 0)
    dmat = jnp.where(causal, b_c + lib_r, NEG_BIG)
    m_inter = b_c + m_prev
    if m_t is None:
        m_t = jnp.maximum(m_inter, jnp.max(dmat, axis=1, keepdims=True))
    w_intra = jnp.exp(dmat - m_t)
    w_inter = jnp.exp(m_inter - m_t)
    amat = _mm_nt(qh, kh)
    smat = amat * w_intra
    qc = _mm(qh, c_h)
    qn = _rowsum(qh * n_h)
    den = _rowsum(smat) + w_inter * qn
    gst = b_last - b_c + li_c
    m_new = jnp.maximum(b_last + m_prev, jnp.max(gst, axis=0, keepdims=True))
    w_state = jnp.exp(gst - m_new)
    decay = jnp.exp(b_last + m_prev - m_new)
    return dict(sl=sl, qh=qh, kh=kh, vh=vh, m_t=m_t, w_intra=w_intra, w_inter=w_inter, smat=smat, qc=qc, qn=qn,
                den=den, m_new=m_new, w_state=w_state, decay=decay)


def _ml_cell_fwd(q, k, v, gcol, grow, u, ng, nh, comms=None):
    s, d = q.shape
    lc = ML_CHUNK
    nc = s // lc
    dh = d // nh

    def body(q_ref, k_ref, v_ref, gc_ref, gr_ref, o_ref, z_ref, ng_ref,
             cell_ref, y_ref, cs_ref, ns_ref, ms_ref, mt_ref, c_sc, n_sc, m_sc):
        @pl.when(pl.program_id(0) == 0)
        def _():
            c_sc[...] = jnp.zeros_like(c_sc)
            n_sc[...] = jnp.zeros_like(n_sc)
            m_sc[...] = jnp.zeros_like(m_sc)

        gc = gc_ref[...]
        gr = gr_ref[...]
        lane = _iota((lc, LANES), 1)
        mt_acc = jnp.zeros((lc, LANES), F32)
        for h in range(nh):
            c_h = c_sc[h]
            n_h = n_sc[h, 0:1, :]
            m_prev = jnp.max(m_sc[h, 0:1, :], axis=1, keepdims=True)
            cs_ref[0, h] = c_h
            ns_ref[0, h] = n_sc[h]
            ms_ref[0, h] = m_sc[h]
            t = _cell_chunk(h, nh, q_ref, k_ref, v_ref, gc, gr, m_prev, c_h, n_h)
            sl = t["sl"]
            num = _mm(t["smat"], t["vh"]) + t["w_inter"] * t["qc"]
            cell_h = num / jnp.maximum(jnp.abs(t["den"]), jnp.exp(-t["m_t"]))
            mt_acc = jnp.where(lane == h, t["m_t"], mt_acc)
            kw = t["kh"] * t["w_state"]
            c_sc[h] = t["decay"] * c_h + _mm_tn(kw, t["vh"])
            n_sc[h] = _bcast8(t["decay"] * n_h + _colsum(kw))
            m_sc[h] = jnp.broadcast_to(t["m_new"], (SUBLANES, LANES))
            hg = _sigmoid(o_ref[:, sl]) * cell_h
            hn = hg * lax.rsqrt(jnp.mean(hg * hg, axis=1, keepdims=True) + EPS)
            z = z_ref[:, sl]
            cell_ref[:, sl] = cell_h
            y_ref[:, sl] = hn * ng_ref[:, sl] * z * _sigmoid(z)
        mt_ref[...] = mt_acc

    tok = pl.BlockSpec((lc, d), lambda c: (c, 0))
    return _call(
        body, comms, name="mlstm_cell_fwd", grid=(nc,),
        in_specs=[tok, tok, tok, pl.BlockSpec((lc, LANES), lambda c: (c, 0)), pl.BlockSpec((16, lc), lambda c: (0, c)),
                  pl.BlockSpec((lc, d), lambda c: (c, 3)), pl.BlockSpec((lc, d), lambda c: (c, 4)), _full((1, d))],
        out_specs=[tok, tok, pl.BlockSpec((1, nh, dh, dh), lambda c: (c, 0, 0, 0)),
                   pl.BlockSpec((1, nh, SUBLANES, dh), lambda c: (c, 0, 0, 0)),
                   pl.BlockSpec((1, nh, SUBLANES, LANES), lambda c: (c, 0, 0, 0)),
                   pl.BlockSpec((lc, LANES), lambda c: (c, 0))],
        out_shape=[jax.ShapeDtypeStruct((s, d), F32), jax.ShapeDtypeStruct((s, d), F32),
                   jax.ShapeDtypeStruct((nc, nh, dh, dh), F32), jax.ShapeDtypeStruct((nc, nh, SUBLANES, dh), F32),
                   jax.ShapeDtypeStruct((nc, nh, SUBLANES, LANES), F32), jax.ShapeDtypeStruct((s, LANES), F32)],
        scratch_shapes=[pltpu.VMEM((nh, dh, dh), F32), pltpu.VMEM((nh, SUBLANES, dh), F32),
                        pltpu.VMEM((nh, SUBLANES, LANES), F32)],
        args=(q, k, v, gcol, grow, u, u, ng))


def _out_fwd(x, y_rg, y_ml, gate, w_out_g, layer, tile, comms=None):
    s, d = x.shape
    nd, _, r, _ = w_out_g.shape

    def body(x_ref, yr_ref, ym_ref, g_ref, w_ref, xn_ref, y_ref):
        ycat = jnp.concatenate([yr_ref[...].astype(BF16), ym_ref[...].astype(BF16)], axis=1)
        acc = jnp.dot(ycat, w_ref[...].reshape(nd * r, d), preferred_element_type=F32)
        y_ref[...] = acc
        xn_ref[...] = x_ref[...] + g_ref[...] * acc

    tok = pl.BlockSpec((tile, d), lambda i: (i, 0))
    return _call(
        body, comms, name="out_proj_fwd", grid=(s // tile,),
        in_specs=[tok, tok, tok, _full((1, d)), pl.BlockSpec((nd, 1, r, d), lambda i: (0, layer, 0, 0))],
        out_specs=[tok, tok],
        out_shape=[jax.ShapeDtypeStruct((s, d), F32)] * 2,
        args=(x, y_rg, y_ml, gate, w_out_g))


def _loss_call(x, fg, target, tile):
    s, d = x.shape

    def body(x_ref, g_ref, t_ref, dx_ref, loss_ref, gg_ref):
        @pl.when(pl.program_id(0) == 0)
        def _():
            loss_ref[...] = jnp.zeros_like(loss_ref)
            gg_ref[...] = jnp.zeros_like(gg_ref)

        xv = x_ref[...]
        g = g_ref[...]
        rs = lax.rsqrt(jnp.mean(xv * xv, axis=1, keepdims=True) + EPS)
        xh = xv * rs
        e = xh * g - t_ref[...]
        loss_ref[...] += jnp.broadcast_to(_colsum(_rowsum(e * e)) * (0.5 / d), loss_ref.shape)
        dy = e * (1.0 / d)
        gg_ref[...] += _bcast8(_colsum(dy * xh))
        dxh = dy * g
        dx_ref[...] = rs * (dxh - xh * jnp.mean(dxh * xh, axis=1, keepdims=True))

    tok = pl.BlockSpec((tile, d), lambda i: (i, 0))
    return pl.pallas_call(
        body, name="final_norm_loss", grid=(s // tile,),
        in_specs=[tok, _full((1, d)), tok],
        out_specs=[tok, _full((SUBLANES, LANES)), _full((SUBLANES, d))],
        out_shape=[jax.ShapeDtypeStruct((s, d), F32), jax.ShapeDtypeStruct((SUBLANES, LANES), F32),
                   jax.ShapeDtypeStruct((SUBLANES, d), F32)],
        compiler_params=_params(1),
    )(x, fg, target)


def _out_bwd(dxo, gate, y, y_rg, y_ml, w_out_g, layer, tile, comms=None):
    s, d = dxo.shape
    nd, _, r, _ = w_out_g.shape

    def body(dx_ref, g_ref, y_ref, yr_ref, ym_ref, w_ref, dyr_ref, dym_ref, gw_ref, dg_ref):
        @pl.when(pl.program_id(0) == 0)
        def _():
            gw_ref[...] = jnp.zeros_like(gw_ref)
            dg_ref[...] = jnp.zeros_like(dg_ref)

        dxv = dx_ref[...]
        dg_ref[...] += _bcast8(_colsum(dxv * y_ref[...]))
        dyb = (dxv * g_ref[...]).astype(BF16)
        dycat = lax.dot_general(dyb, w_ref[...].reshape(nd * r, d), (((1,), (1,)), ((), ())), preferred_element_type=F32)
        dyr_ref[...] = dycat[:, 0:d]
        dym_ref[...] = dycat[:, d:2 * d]
        ycat = jnp.concatenate([yr_ref[...].astype(BF16), ym_ref[...].astype(BF16)], axis=1)
        gw_ref[...] += lax.dot_general(ycat, dyb, (((0,), (0,)), ((), ())), preferred_element_type=F32).reshape(nd, r, d)

    tok = pl.BlockSpec((tile, d), lambda i: (i, 0))
    return _call(
        body, comms, name="out_proj_bwd", grid=(s // tile,),
        in_specs=[tok, _full((1, d)), tok, tok, tok, pl.BlockSpec((nd, 1, r, d), lambda i: (0, layer, 0, 0))],
        out_specs=[tok, tok, _full((nd, r, d)), _full((SUBLANES, d))],
        out_shape=[jax.ShapeDtypeStruct((s, d), F32)] * 2 + [jax.ShapeDtypeStruct((nd, r, d), F32),
                                                             jax.ShapeDtypeStruct((SUBLANES, d), F32)],
        args=(dxo, gate, y, y_rg, y_ml, w_out_g))


def _ml_cell_bwd(dy_ml, u, cell, q, k, v, gcol, grow, mt, cs, ns, ms, ng, nh, comms=None):
    s, d = q.shape
    lc = ML_CHUNK
    nc = s // lc
    dh = d // nh

    def body(dy_ref, o_ref, z_ref, cell_ref, q_ref, k_ref, v_ref, gc_ref, gr_ref, mt_ref, cs_ref, ns_ref, ms_ref,
             ng_ref, dq_ref, dk_ref, dv_ref, dg_ref, do_ref, dz_ref, gng_ref, dc_sc, dn_sc):
        @pl.when(pl.program_id(0) == 0)
        def _():
            dc_sc[...] = jnp.zeros_like(dc_sc)
            dn_sc[...] = jnp.zeros_like(dn_sc)
            gng_ref[...] = jnp.zeros_like(gng_ref)

        gc = gc_ref[...]
        gr = gr_ref[...]
        mtv = mt_ref[...]
        lane = _iota((lc, LANES), 1)
        rowv = _iota((lc, 1), 0)
        dg_acc = jnp.zeros((lc, LANES), F32)
        for h in range(nh):
            c_h = cs_ref[0, h]
            n_h = ns_ref[0, h, 0:1, :]
            m_prev = jnp.max(ms_ref[0, h, 0:1, :], axis=1, keepdims=True)
            t = _cell_chunk(h, nh, q_ref, k_ref, v_ref, gc, gr, m_prev, c_h, n_h, m_t=_col(mtv, h))
            sl, qh, kh, vh = t["sl"], t["qh"], t["kh"], t["vh"]
            w_intra, w_inter, smat, w_state, decay = t["w_intra"], t["w_inter"], t["smat"], t["w_state"], t["decay"]
            cell_h = cell_ref[:, sl]
            o = o_ref[:, sl]
            z = z_ref[:, sl]
            dyv = dy_ref[:, sl]
            ngh = ng_ref[:, sl]
            so = _sigmoid(o)
            hg = so * cell_h
            rinv = lax.rsqrt(jnp.mean(hg * hg, axis=1, keepdims=True) + EPS)
            hn = hg * rinv
            sz = _sigmoid(z)
            dz_ref[:, sl] = (dyv * hn * ngh * (sz + z * sz * (1.0 - sz))).astype(BF16)
            dymid = dyv * z * sz
            gng_ref[:, sl] += _bcast8(_colsum(dymid * hn))
            dhn = dymid * ngh
            dhg = rinv * (dhn - hn * jnp.mean(dhn * hn, axis=1, keepdims=True))
            do_ref[:, sl] = (dhg * cell_h * so * (1.0 - so)).astype(BF16)
            dcell = dhg * so
            eneg = jnp.exp(-t["m_t"])
            aden = jnp.abs(t["den"])
            nst = jnp.maximum(aden, eneg)
            dnum = dcell / nst
            dden = jnp.where(aden > eneg, -_rowsum(cell_h * dcell) / nst * jnp.sign(t["den"]), 0.0)
            pmat = _mm_nt(dnum, vh) + dden
            damat = pmat * w_intra
            gmat = pmat * smat
            wdn = w_inter * dnum
            wdd = w_inter * dden
            dqh = _mm(damat, kh) + _mm_nt(wdn, c_h) + wdd * n_h
            dkh = _mm_tn(damat, qh)
            dvh = _mm_tn(smat, dnum)
            dw_inter = _rowsum(dnum * t["qc"]) + dden * t["qn"]
            dcn = dc_sc[h]
            dnn = dn_sc[h, 0:1, :]
            kw = kh * w_state
            dkw = _mm_nt(vh, dcn) + dnn
            dvh = dvh + _mm(kw, dcn)
            dkh = dkh + dkw * w_state
            dgst = _rowsum(dkw * kh) * w_state
            ddecay = _colsum(_rowsum(dcn * c_h)) + _rowsum(dnn * n_h)
            db_last = _colsum(dgst) + ddecay * decay
            rs_g = _rowsum(gmat)
            cs_g = _rowsum(gmat.T)
            db = rs_g - cs_g + dw_inter * w_inter - dgst + jnp.where(rowv == lc - 1, db_last, 0.0)
            dli = cs_g + dgst
            dc_sc[h] = decay * dcn + _mm_tn(qh, wdn)
            dn_sc[h] = _bcast8(decay * dnn + _colsum(qh * wdd))
            dq_ref[:, sl] = dqh
            dk_ref[:, sl] = dkh * (dh ** -0.5)
            dv_ref[:, sl] = dvh
            dg_acc = jnp.where(lane == h, dli, jnp.where(lane == 4 + h, db, dg_acc))
        dg_ref[...] = dg_acc

    rev = lambda c: nc - 1 - c
    tok = pl.BlockSpec((lc, d), lambda c: (rev(c), 0))
    g128 = pl.BlockSpec((lc, LANES), lambda c: (rev(c), 0))
    return _call(
        body, comms, name="mlstm_cell_bwd", grid=(nc,),
        in_specs=[tok, pl.BlockSpec((lc, d), lambda c: (rev(c), 3)), pl.BlockSpec((lc, d), lambda c: (rev(c), 4)),
                  tok, tok, tok, tok, g128, pl.BlockSpec((16, lc), lambda c: (0, rev(c))), g128,
                  pl.BlockSpec((1, nh, dh, dh), lambda c: (rev(c), 0, 0, 0)),
                  pl.BlockSpec((1, nh, SUBLANES, dh), lambda c: (rev(c), 0, 0, 0)),
                  pl.BlockSpec((1, nh, SUBLANES, LANES), lambda c: (rev(c), 0, 0, 0)), _full((1, d))],
        out_specs=[tok, tok, tok, g128, tok, tok, _full((SUBLANES, d))],
        out_shape=[jax.ShapeDtypeStruct((s, d), F32)] * 3 + [jax.ShapeDtypeStruct((s, LANES), F32)]
        + [jax.ShapeDtypeStruct((s, d), BF16)] * 2 + [jax.ShapeDtypeStruct((SUBLANES, d), F32)],
        scratch_shapes=[pltpu.VMEM((nh, dh, dh), F32), pltpu.VMEM((nh, SUBLANES, dh), F32)],
        args=(dy_ml, u, u, cell, q, k, v, gcol, grow, mt, cs, ns, ms, ng))


def _halo_spec(d, tile, nt, col):
    per = tile // SUBLANES
    return pl.BlockSpec((SUBLANES, d), lambda i: (jnp.maximum((nt - 1 - i) * per - 1, 0), col))


def _ml_pre_bwd(dq, dk, dv, dgates, gcol, u, q, k, v, conv_w, conv_b, w_q, w_k, w_v, wif_t, tile, comms=None):
    s, d = dq.shape
    nt = s // tile
    nh, dh, _ = w_q.shape

    def body(dq_ref, dk_ref, dv_ref, dg_ref, gc_ref, x_ref, halo_ref, q_ref, k_ref, v_ref, cw_ref, cb_ref,
             wq_ref, wk_ref, wv_ref, wift_ref,
             dx_ref, gwq_ref, gwk_ref, gwv_ref, gwif_ref, gbif_ref, gcw_ref, gcb_ref, next8):
        i = pl.program_id(0)

        @pl.when(i == 0)
        def _():
            next8[...] = jnp.zeros_like(next8)
            for ref in (gwq_ref, gwk_ref, gwv_ref, gwif_ref, gbif_ref, gcw_ref, gcb_ref):
                ref[...] = jnp.zeros_like(ref)

        x = x_ref[...]
        halo = halo_ref[...] * jnp.where(i < nt - 1, 1.0, 0.0)
        pre = _conv(x, halo, cw_ref) + cb_ref[...]
        sg = _sigmoid(pre)
        xc = pre * sg
        dgc = dg_ref[...]
        lane = _iota(dgc.shape, 1)
        utri = jnp.where(_iota((ML_CHUNK, ML_CHUNK), 0) <= _iota((ML_CHUNK, ML_CHUNK), 1), 1.0, 0.0)
        rcs = [_mm_hi(utri, dgc[c * ML_CHUNK:(c + 1) * ML_CHUNK, :]) for c in range(tile // ML_CHUNK)]
        rc = rcs[0] if len(rcs) == 1 else jnp.concatenate(rcs, axis=0)
        dgates_v = jnp.where(lane < 4, dgc, jnp.where(lane < 8, rc * (1.0 - jnp.exp(gc_ref[...])), 0.0))
        dgb = dgates_v.astype(BF16)
        gbif_ref[...] += jnp.broadcast_to(_colsum(dgates_v), gbif_ref.shape)
        ext = jnp.dot(dgb, wift_ref[...], preferred_element_type=F32)
        dqt = dq_ref[...] + ext[:, 0:d]
        dkt = dk_ref[...] + ext[:, d:2 * d]
        dvt = dv_ref[...] + ext[:, 2 * d:3 * d]
        gwif_ref[:, 0:d] += _mm_tn(dgb, q_ref[...])
        gwif_ref[:, d:2 * d] += _mm_tn(dgb, k_ref[...])
        gwif_ref[:, 2 * d:3 * d] += _mm_tn(dgb, v_ref[...])
        dxc_parts, dxv_parts = [], []
        for h in range(nh):
            sl = slice(h * dh, (h + 1) * dh)
            gwq_ref[h] += _mm_tn(xc[:, sl], dqt[:, sl])
            gwk_ref[h] += _mm_tn(xc[:, sl], dkt[:, sl])
            gwv_ref[h] += _mm_tn(x[:, sl], dvt[:, sl])
            dxc_parts.append(_mm_nt(dqt[:, sl], wq_ref[h]) + _mm_nt(dkt[:, sl], wk_ref[h]))
            dxv_parts.append(_mm_nt(dvt[:, sl], wv_ref[h]))
        dxc = jnp.concatenate(dxc_parts, axis=1)
        dxv = jnp.concatenate(dxv_parts, axis=1)
        dpre = dxc * (sg + pre * sg * (1.0 - sg))
        gcb_ref[...] += _bcast8(_colsum(dpre))
        for kk in range(CONV_WIDTH):
            gcw_ref[kk:kk + 1, :] += _colsum(dpre * _shift_down(x, CONV_WIDTH - 1 - kk, halo))
        dx_ref[...] = (dxv + _conv_bwd_x(dpre, next8[...], cw_ref)).astype(BF16)
        next8[...] = dpre[0:SUBLANES, :]

    rev = lambda i: nt - 1 - i
    tok = pl.BlockSpec((tile, d), lambda i: (rev(i), 0))
    g128 = pl.BlockSpec((tile, LANES), lambda i: (rev(i), 0))
    wsh = (nh, dh, dh)
    return _call(
        body, comms, name="mlstm_proj_bwd", grid=(nt,),
        in_specs=[tok, tok, tok, g128, g128, pl.BlockSpec((tile, d), lambda i: (rev(i), 2)), _halo_spec(d, tile, nt, 2),
                  tok, tok, tok, _full(conv_w.shape), _full((1, d)), _full(wsh), _full(wsh), _full(wsh),
                  _full(wif_t.shape)],
        out_specs=[tok, _full(wsh), _full(wsh), _full(wsh), _full((LANES, 3 * d)), _full((SUBLANES, LANES)),
                   _full((SUBLANES, d)), _full((SUBLANES, d))],
        out_shape=[jax.ShapeDtypeStruct((s, d), BF16)] + [jax.ShapeDtypeStruct(wsh, F32)] * 3
        + [jax.ShapeDtypeStruct((LANES, 3 * d), F32), jax.ShapeDtypeStruct((SUBLANES, LANES), F32),
           jax.ShapeDtypeStruct((SUBLANES, d), F32), jax.ShapeDtypeStruct((SUBLANES, d), F32)],
        scratch_shapes=[pltpu.VMEM((SUBLANES, d), F32)],
        args=(dq, dk, dv, dgates, gcol, u, u, q, k, v, conv_w, conv_b, w_q, w_k, w_v, wif_t))


def _rg_bwd(dy_rg, u, h_rg, conv_w, conv_b, w_a, b_a, w_x, b_x, lam, tile, comms=None):
    s, d = dy_rg.shape
    nt = s // tile
    nh, dh, _ = w_a.shape

    def body(dy_ref, x_ref, xhalo_ref, z_ref, h_ref, hhalo_ref, cw_ref, cb_ref, wa_ref, ba_ref, wx_ref, bx_ref, lam_ref,
             dx_ref, dz_ref, gwa_ref, gwx_ref, gba_ref, gbx_ref, glam_ref, gcw_ref, gcb_ref, next8, anext, dnext, dbuf):
        i = pl.program_id(0)

        @pl.when(i == 0)
        def _():
            for ref in (next8, anext, dnext, gwa_ref, gwx_ref, gba_ref, gbx_ref, glam_ref, gcw_ref, gcb_ref):
                ref[...] = jnp.zeros_like(ref)

        inner = jnp.where(i < nt - 1, 1.0, 0.0)
        x = x_ref[...]
        halo = xhalo_ref[...] * inner
        xc = _conv(x, halo, cw_ref) + cb_ref[...]
        r, ig, sp, a, beta = _rg_gates(xc, wa_ref, ba_ref, wx_ref, bx_ref, lam_ref)
        h = h_ref[...]
        row = _iota(h.shape, 0)
        hprev = jnp.where(row >= 1, pltpu.roll(h, 1, 0), hhalo_ref[SUBLANES - 1:SUBLANES, :] * inner)
        z = z_ref[...]
        sz = _sigmoid(z)
        dyv = dy_ref[...]
        dz_ref[...] = (dyv * h * (sz + z * sz * (1.0 - sz))).astype(BF16)
        a_up = jnp.where(row < tile - 1, pltpu.roll(a, tile - 1, 0), anext[0:1, :])
        _scan_into(a_up, dyv * z * sz, dnext[0:1, :], dbuf, True)
        delta = dbuf[...]
        anext[...] = a[0:SUBLANES, :]
        dnext[...] = delta[0:SUBLANES, :]
        dla = delta * hprev * a - delta * ig * xc * (a * a / beta)
        glam_ref[...] += _bcast8(_colsum(dla * r) * (RG_C * _sigmoid(-lam_ref[...])))
        dpa = dla * (-RG_C * sp) * r * (1.0 - r)
        dpx = delta * beta * xc * ig * (1.0 - ig)
        gba_ref[...] += _bcast8(_colsum(dpa))
        gbx_ref[...] += _bcast8(_colsum(dpx))
        parts = []
        for hh in range(nh):
            sl = slice(hh * dh, (hh + 1) * dh)
            gwa_ref[hh] += _mm_tn(xc[:, sl], dpa[:, sl])
            gwx_ref[hh] += _mm_tn(xc[:, sl], dpx[:, sl])
            parts.append(_mm_nt(dpa[:, sl], wa_ref[hh]) + _mm_nt(dpx[:, sl], wx_ref[hh]))
        dxc = delta * beta * ig + jnp.concatenate(parts, axis=1)
        gcb_ref[...] += _bcast8(_colsum(dxc))
        for kk in range(CONV_WIDTH):
            gcw_ref[kk:kk + 1, :] += _colsum(dxc * _shift_down(x, CONV_WIDTH - 1 - kk, halo))
        dx_ref[...] = _conv_bwd_x(dxc, next8[...], cw_ref).astype(BF16)
        next8[...] = dxc[0:SUBLANES, :]

    rev = lambda i: nt - 1 - i
    tok = pl.BlockSpec((tile, d), lambda i: (rev(i), 0))
    vec = _full((1, d))
    acc = _full((SUBLANES, d))
    wsh = (nh, dh, dh)
    return _call(
        body, comms, name="rglru_bwd", grid=(nt,),
        in_specs=[tok, tok, _halo_spec(d, tile, nt, 0), pl.BlockSpec((tile, d), lambda i: (rev(i), 1)), tok,
                  _halo_spec(d, tile, nt, 0), _full(conv_w.shape), vec, _full(wsh), vec, _full(wsh), vec, vec],
        out_specs=[tok, tok, _full(wsh), _full(wsh), acc, acc, acc, acc, acc],
        out_shape=[jax.ShapeDtypeStruct((s, d), BF16)] * 2 + [jax.ShapeDtypeStruct(wsh, F32)] * 2
        + [jax.ShapeDtypeStruct((SUBLANES, d), F32)] * 5,
        scratch_shapes=[pltpu.VMEM((SUBLANES, d), F32)] * 3 + [pltpu.VMEM((tile, d), F32)],
        args=(dy_rg, u, u, u, h_rg, h_rg, conv_w, conv_b, w_a, b_a, w_x, b_x, lam))


def _segments(d, w, n_pieces, n_slots):
    bounds = sorted({k * d for k in range(n_pieces + 1)} | {j * w for j in range(n_slots + 1)})
    return [(lo // d, lo % d, lo // w, lo % w, hi - lo) for lo, hi in zip(bounds[:-1], bounds[1:])]


def _in_bwd(pieces, x, dxo, ng, scale, w_in_g, layer, tile, comms=None, tiles=None, prev=None):
    s, d = x.shape
    nd, _, _, w = w_in_g.shape
    segs = _segments(d, w, len(pieces), nd)
    first, count = tiles or (0, s // tile)
    n_p = len(pieces)

    def body(*refs):
        p_refs = refs[:n_p]
        x_ref, dxo_ref, ng_ref, sc_ref, w_ref = refs[n_p:n_p + 5]
        dx_ref, dsc_ref, dsh_ref, gng_ref, wcat = refs[-5:]
        _join_columns(w_ref, wcat)

        @pl.when(pl.program_id(0) == 0)
        def _():
            for k, ref in enumerate((dsc_ref, dsh_ref, gng_ref)):
                ref[...] = jnp.zeros_like(ref) if prev is None else refs[n_p + 6 + k][...]

        du = jnp.concatenate([p[...] for p in p_refs], axis=1)
        dh = lax.dot_general(du, wcat[...], (((1,), (1,)), ((), ())), preferred_element_type=F32)
        xv = x_ref[...]
        g = ng_ref[...]
        rs = lax.rsqrt(jnp.mean(xv * xv, axis=1, keepdims=True) + EPS)
        xh = xv * rs
        dsh_ref[...] += _bcast8(_colsum(dh))
        dsc_ref[...] += _bcast8(_colsum(dh * xh * g))
        dhn = dh * (1.0 + sc_ref[...])
        gng_ref[...] += _bcast8(_colsum(dhn * xh))
        dxh = dhn * g
        dx_ref[...] = dxo_ref[...] + rs * (dxh - xh * jnp.mean(dxh * xh, axis=1, keepdims=True))

    tok = pl.BlockSpec((tile, d), lambda i: (i + first, 0))
    vec = _full((1, d))
    acc = _full((SUBLANES, d))
    more_specs = [] if prev is None else [pl.BlockSpec(memory_space=pl.ANY), acc, acc, acc]
    return _call(
        body, comms, name="in_proj_bwd_x", grid=(count,),
        in_specs=[tok] * n_p + [tok, tok, vec, vec, pl.BlockSpec((nd, 1, d, w), lambda i: (0, layer, 0, 0),
                                                               pipeline_mode=pl.Buffered(1))] + more_specs,
        out_specs=[tok, acc, acc, acc],
        out_shape=[jax.ShapeDtypeStruct((s, d), F32)] + [jax.ShapeDtypeStruct((SUBLANES, d), F32)] * 3,
        scratch_shapes=[pltpu.VMEM((d, nd * w), BF16)],
        args=(*pieces, x, dxo, ng, scale, w_in_g) + (() if prev is None else tuple(prev)),
        aliases={} if prev is None else {n_p + 5: 0})


def _in_bwd_w(pieces, hbf, w, slots, tile, comms=None):
    s, d = hbf.shape
    nd_all = len(pieces) * d // w
    segs = [sg for sg in _segments(d, w, len(pieces), nd_all) if sg[2] in slots]

    def body(*refs):
        p_refs = refs[:len(pieces)]
        h_ref, gw_ref = refs[len(pieces):]

        @pl.when(pl.program_id(0) == 0)
        def _():
            gw_ref[...] = jnp.zeros_like(gw_ref)

        hv = h_ref[...]
        for (kk, a, j, b, width) in segs:
            gw_ref[j - slots[0], :, b:b + width] += _mm_tn(hv, p_refs[kk][:, a:a + width])

    tok = pl.BlockSpec((tile, d), lambda i: (i, 0))
    return _call(
        body, comms, name="in_proj_bwd_w", grid=(s // tile,),
        in_specs=[tok] * len(pieces) + [tok],
        out_specs=[pl.BlockSpec((len(slots), d, w), lambda i: (0, 0, 0), pipeline_mode=pl.Buffered(1))],
        out_shape=[jax.ShapeDtypeStruct((len(slots), d, w), F32)],
        args=(*pieces, hbf))[0]


def _ada_bwd_w(cact_col, dmod, nd):
    d = cact_col.shape[0]
    w = dmod.shape[1] // nd

    def body(c_ref, m_ref, o_ref):
        o_ref[0] = c_ref[...] * m_ref[...]

    return pl.pallas_call(
        body, name="adaln_bwd_w", grid=(nd,),
        in_specs=[_full((d, 1)), pl.BlockSpec((1, w), lambda j: (0, j))],
        out_specs=pl.BlockSpec((1, d, w), lambda j: (j, 0, 0)),
        out_shape=jax.ShapeDtypeStruct((nd, d, w), F32),
        compiler_params=_params(1),
    )(cact_col, dmod)


def _exchange(arrs, gather, name):
    n = len(arrs)
    outs_shape = [jax.ShapeDtypeStruct((N_DEV,) + a.shape if gather else a.shape, a.dtype) for a in arrs]

    def body(*refs):
        ins, outs = refs[:n], refs[n:2 * n]
        send_sems, recv_sems, local_sems = refs[2 * n:]
        x, y, c = (lax.axis_index(ax) for ax in MESH_AXES)
        me = 4 * x + 2 * y + c
        sends, recvs = [], []
        for flip in range(1, N_DEV):
            px = x ^ ((flip >> 2) & 1)
            py = y ^ ((flip >> 1) & 1)
            pc = c ^ (flip & 1)
            peer = 4 * px + 2 * py + pc
            for kk in range(n):
                sem = kk * (N_DEV - 1) + flip - 1
                src = ins[kk] if gather else ins[kk].at[peer]
                sends.append(pltpu.make_async_remote_copy(
                    src_ref=src, dst_ref=outs[kk].at[me], send_sem=send_sems.at[sem], recv_sem=recv_sems.at[sem],
                    device_id=(px, py, pc), device_id_type=pl.DeviceIdType.MESH))
                recvs.append(pltpu.make_async_remote_copy(
                    src_ref=src, dst_ref=outs[kk].at[peer], send_sem=send_sems.at[sem], recv_sem=recv_sems.at[sem],
                    device_id=(px, py, pc), device_id_type=pl.DeviceIdType.MESH))
        for cp in sends:
            cp.start()
        local = [pltpu.make_async_copy(ins[kk] if gather else ins[kk].at[me], outs[kk].at[me], local_sems.at[kk])
                 for kk in range(n)]
        for cp in local:
            cp.start()
        for cp in recvs:
            cp.wait_recv()
        for cp in sends:
            cp.wait_send()
        for cp in local:
            cp.wait()

    return pl.pallas_call(
        body, name=name,
        in_specs=[pl.BlockSpec(memory_space=pl.ANY)] * n,
        out_specs=[pl.BlockSpec(memory_space=pl.ANY)] * n,
        out_shape=outs_shape,
        scratch_shapes=[pltpu.SemaphoreType.DMA((n * (N_DEV - 1),)), pltpu.SemaphoreType.DMA((n * (N_DEV - 1),)),
                        pltpu.SemaphoreType.DMA((n,))],
    )(*arrs)


def _mesh_place():
    x, y, c = (lax.axis_index(ax) for ax in MESH_AXES)
    return x, y, c, (x, y, 1 - c), [(1 - x, y), (x, 1 - y), (1 - x, 1 - y)]


def _remote(src, dst, send_sems, recv_sems, sem, to):
    return pltpu.make_async_remote_copy(src_ref=src, dst_ref=dst, send_sem=send_sems.at[sem], recv_sem=recv_sems.at[sem],
                                        device_id=to, device_id_type=pl.DeviceIdType.MESH)


def _gather_two_level(arrs, name):
    n = len(arrs)
    per = N_DEV - 1

    def body(*refs):
        ins, outs = refs[:n], refs[n:2 * n]
        send_sems, recv_sems, local_sems = refs[2 * n:]
        x, y, c, sibling, chips = _mesh_place()

        def copy(kk, j, block, to, src=None):
            dst = outs[kk].at[4 * block[0] + 2 * block[1] + block[2]]
            return _remote(dst if src is None else src, dst, send_sems, recv_sems, kk * per + j, to)

        me = (x, y, c)
        local = [pltpu.make_async_copy(ins[kk], outs[kk].at[4 * x + 2 * y + c], local_sems.at[kk]) for kk in range(n)]
        first = []
        for j, chip in enumerate(chips):
            first += [copy(kk, 1 + j, me, (*chip, c), src=ins[kk]) for kk in range(n)]
        first += [copy(kk, 0, me, sibling, src=ins[kk]) for kk in range(n)]
        for cp in first + local:
            cp.start()
        passed = []
        for j, chip in enumerate(chips):
            for kk in range(n):
                copy(kk, 1 + j, (*chip, c), me).wait_recv()
                passed.append(copy(kk, 4 + j, (*chip, c), sibling))
                passed[-1].start()
        for kk in range(n):
            copy(kk, 0, sibling, me).wait_recv()
        for j, chip in enumerate(chips):
            for kk in range(n):
                copy(kk, 4 + j, (*chip, 1 - c), me).wait_recv()
        for cp in first + passed:
            cp.wait_send()
        for cp in local:
            cp.wait()

    return pl.pallas_call(
        body, name=name,
        in_specs=[pl.BlockSpec(memory_space=pl.ANY)] * n, out_specs=[pl.BlockSpec(memory_space=pl.ANY)] * n,
        out_shape=[jax.ShapeDtypeStruct((N_DEV,) + a.shape, a.dtype) for a in arrs],
        scratch_shapes=[pltpu.SemaphoreType.DMA((n * per,)), pltpu.SemaphoreType.DMA((n * per,)),
                        pltpu.SemaphoreType.DMA((n,))],
    )(*arrs)


N_CHIPS = N_DEV // 2


def _core_swap(arrs, name):
    n = len(arrs)

    def body(*refs):
        ins, outs = refs[:n], refs[n:2 * n]
        send_sems, recv_sems = refs[2 * n:]
        _, _, c, sibling, _ = _mesh_place()
        copies = [_remote(ins[kk].at[2 * q + (1 - c)], outs[kk].at[q], send_sems, recv_sems, kk * N_CHIPS + q, sibling)
                  for q in range(N_CHIPS) for kk in range(n)]
        for cp in copies:
            cp.start()
        for cp in copies:
            cp.wait_recv()
        for cp in copies:
            cp.wait_send()

    return pl.pallas_call(
        body, name=name,
        in_specs=[pl.BlockSpec(memory_space=pl.ANY)] * n, out_specs=[pl.BlockSpec(memory_space=pl.ANY)] * n,
        out_shape=[jax.ShapeDtypeStruct((N_CHIPS,) + a.shape[1:], a.dtype) for a in arrs],
        scratch_shapes=[pltpu.SemaphoreType.DMA((n * N_CHIPS,)), pltpu.SemaphoreType.DMA((n * N_CHIPS,))],
    )(*arrs)


def _pair_sum(a, other, parity, name):
    _, r, c = a.shape
    tr = _row_tile(r, c, 3)

    def body(p_ref, a_ref, o_ref, s_ref):
        s_ref[...] = (a_ref[...] + o_ref[...]).astype(BF16)

    return pl.pallas_call(
        body, name=name,
        grid_spec=pltpu.PrefetchScalarGridSpec(
            num_scalar_prefetch=1, grid=(N_CHIPS, r // tr),
            in_specs=[pl.BlockSpec((1, tr, c), lambda q, i, p: (2 * q + p[0], i, 0)),
                      pl.BlockSpec((1, tr, c), lambda q, i, p: (q, i, 0))],
            out_specs=pl.BlockSpec((1, tr, c), lambda q, i, p: (q, i, 0))),
        out_shape=jax.ShapeDtypeStruct((N_CHIPS, r, c), BF16),
        compiler_params=_params(2),
    )(parity, a, other)


def _chip_swap(arrs, name):
    n = len(arrs)
    per = N_CHIPS - 1

    def body(*refs):
        ins, outs = refs[:n], refs[n:2 * n]
        send_sems, recv_sems, local_sems = refs[2 * n:]
        x, y, c, _, chips = _mesh_place()
        mine = 2 * x + y
        sends = [_remote(ins[kk].at[2 * chip[0] + chip[1]], outs[kk].at[mine], send_sems, recv_sems, kk * per + j, (*chip, c))
                 for j, chip in enumerate(chips) for kk in range(n)]
        recvs = [_remote(ins[kk].at[mine], outs[kk].at[2 * chip[0] + chip[1]], send_sems, recv_sems, kk * per + j, (*chip, c))
                 for j, chip in enumerate(chips) for kk in range(n)]
        local = [pltpu.make_async_copy(ins[kk].at[mine], outs[kk].at[mine], local_sems.at[kk]) for kk in range(n)]
        for cp in sends + local:
            cp.start()
        for cp in recvs:
            cp.wait_recv()
        for cp in sends:
            cp.wait_send()
        for cp in local:
            cp.wait()

    return pl.pallas_call(
        body, name=name,
        in_specs=[pl.BlockSpec(memory_space=pl.ANY)] * n, out_specs=[pl.BlockSpec(memory_space=pl.ANY)] * n,
        out_shape=[jax.ShapeDtypeStruct(a.shape, a.dtype) for a in arrs],
        scratch_shapes=[pltpu.SemaphoreType.DMA((n * per,)), pltpu.SemaphoreType.DMA((n * per,)),
                        pltpu.SemaphoreType.DMA((n,))],
    )(*arrs)


def _adam_math(w, g, m, v):
    m = ADAM_B1 * m + (1.0 - ADAM_B1) * g
    v = ADAM_B2 * v + (1.0 - ADAM_B2) * (g * g)
    m_hat = m / (1.0 - ADAM_B1 ** ADAM_STEP)
    v_hat = v / (1.0 - ADAM_B2 ** ADAM_STEP)
    delta = -ADAM_LR * (m_hat / (jnp.sqrt(v_hat) + ADAM_EPS) + ADAM_WD * w)
    return delta, m, v


def _sum_devices(r_ref):
    acc = r_ref[0].astype(F32)
    for p in range(1, r_ref.shape[0]):
        acc = acc + r_ref[p].astype(F32)
    return acc


def _row_tile(rows, cols, n_bufs):
    budget = 24 * 1024 * 1024 // (n_bufs * 2 * cols * 4)
    t = rows
    while t > budget and t % 2 == 0 and (t // 2) % SUBLANES == 0:
        t //= 2
    return t


def _reduce_adam(recvs, w, m, v, name):
    nl, r, c = w.shape
    n_part = recvs[0].shape[0]
    tr = _row_tile(r, c, n_part * nl + 7)
    nt = r // tr

    def body(*refs):
        r_refs = refs[:nl]
        w_ref, m_ref, v_ref, g_ref, d_ref, mo_ref, vo_ref = refs[nl:]
        layer = pl.program_id(0)
        g = _sum_devices(r_refs[0])
        for ll in range(1, nl):
            g = jnp.where(layer == ll, _sum_devices(r_refs[ll]), g)
        delta, m2, v2 = _adam_math(w_ref[0], g, m_ref[0], v_ref[0])
        g_ref[0] = g
        d_ref[0] = delta
        mo_ref[0] = m2
        vo_ref[0] = v2

    def rspec(ll):
        return pl.BlockSpec((n_part, tr, c), lambda l, i: (0, jnp.where(l == ll, i, jnp.where(l < ll, 0, nt - 1)), 0))

    blk = pl.BlockSpec((1, tr, c), lambda l, i: (l, i, 0))
    return pl.pallas_call(
        body, name=name, grid=(nl, nt),
        in_specs=[rspec(ll) for ll in range(nl)] + [blk, blk, blk],
        out_specs=[blk] * 4,
        out_shape=[jax.ShapeDtypeStruct((nl, r, c), F32)] * 4,
        compiler_params=_params(2),
    )(*recvs, w, m, v)


def _sum8(recv, name):
    _, r, c = recv.shape

    def body(r_ref, o_ref):
        o_ref[...] = _sum_devices(r_ref)

    return pl.pallas_call(
        body, name=name, grid=(1,),
        in_specs=[_full(recv.shape)], out_specs=_full((r, c)),
        out_shape=jax.ShapeDtypeStruct((r, c), F32), compiler_params=_params(1),
    )(recv)


def _adam_call(w, g, m, v, name):
    r, c = w.shape

    def body(w_ref, g_ref, m_ref, v_ref, d_ref, mo_ref, vo_ref):
        delta, m2, v2 = _adam_math(w_ref[...], g_ref[...], m_ref[...], v_ref[...])
        d_ref[...] = delta
        mo_ref[...] = m2
        vo_ref[...] = v2

    return pl.pallas_call(
        body, name=name, grid=(1,),
        in_specs=[_full((r, c))] * 4, out_specs=[_full((r, c))] * 3,
        out_shape=[jax.ShapeDtypeStruct((r, c), F32)] * 3, compiler_params=_params(1),
    )(w, g, m, v)


def _tile_for(s, want):
    return min(want, s)


def _local_step_whole(x, c, target, wts):
    s, d = x.shape
    nl = wts["w_in_g"].shape[1]
    nd = wts["w_in_g"].shape[0]
    nh_ml = wts["w_qkv"].shape[2]
    t_big = _tile_for(s, 512)
    t_mid = _tile_for(s, 256)

    mod, cact = _mod_call(c, wts["w_ada_g"], wts["b_ada"])
    row = lambda a: a.reshape(1, -1)
    saved = []
    xl = x
    for l in range(nl):
        shift, scale, gate = (row(mod[l, kk * d:(kk + 1) * d]) for kk in range(3))
        u, hbf = _in_fwd(xl, row(wts["norm_g"][l]), scale, shift, wts["w_in_g"], l, t_mid)
        h_rg, y_rg = _rg_fwd(u, d, wts["rg_conv_w"][l], row(wts["rg_conv_b"][l]), wts["rg_w_a_bf"][l],
                             row(wts["rg_b_a"][l]), wts["rg_w_x_bf"][l], row(wts["rg_b_x"][l]),
                             row(wts["rg_lambda"][l]), t_mid)
        q, k, v, gcol = _ml_pre(u, d, wts["ml_conv_w"][l], row(wts["ml_conv_b"][l]), wts["w_qkv"][l, 0],
                                wts["w_qkv"][l, 1], wts["w_qkv"][l, 2], wts["wif_pad"][l], wts["bif_pad"][l], t_mid)
        grow = gcol[:, 0:16].T
        cell, y_ml, cs, ns, ms, mt = _ml_cell_fwd(q, k, v, gcol, grow, u, row(wts["ml_norm_g"][l]), nh_ml)
        x_new, y = _out_fwd(xl, y_rg, y_ml, gate, wts["w_out_g"], l, t_big)
        saved.append(dict(x=xl, u=u, hbf=hbf, h_rg=h_rg, y_rg=y_rg, q=q, k=k, v=v, gcol=gcol, grow=grow, cell=cell,
                          y_ml=y_ml, cs=cs, ns=ns, ms=ms, mt=mt, y=y, scale=scale, gate=gate))
        xl = x_new

    dx, loss_p, g_final = _loss_call(xl, row(wts["final_g"]), target, t_big)
    grads = [None] * nl
    cact_col = cact[0].reshape(d, 1)
    for l in reversed(range(nl)):
        sv = saved[l]
        dy_rg, dy_ml, gw_out, dgate = _out_bwd(dx, sv["gate"], sv["y"], sv["y_rg"], sv["y_ml"], wts["w_out_g"], l, t_big)
        dq, dk, dv, dgates, d_mlo, d_mlz, g_mlng = _ml_cell_bwd(
            dy_ml, sv["u"], sv["cell"], sv["q"], sv["k"], sv["v"], sv["gcol"], sv["grow"], sv["mt"], sv["cs"],
            sv["ns"], sv["ms"], row(wts["ml_norm_g"][l]), nh_ml)
        d_mlx, g_wq, g_wk, g_wv, g_wift, g_bif, g_mlcw, g_mlcb = _ml_pre_bwd(
            dq, dk, dv, dgates, sv["gcol"], sv["u"], sv["q"], sv["k"], sv["v"], wts["ml_conv_w"][l],
            row(wts["ml_conv_b"][l]), wts["w_qkv"][l, 0], wts["w_qkv"][l, 1], wts["w_qkv"][l, 2], wts["wift_pad"][l], t_mid)
        d_rgx, d_rgz, g_wa, g_wx, g_ba, g_bx, g_lam, g_rgcw, g_rgcb = _rg_bwd(
            dy_rg, sv["u"], sv["h_rg"], wts["rg_conv_w"][l], row(wts["rg_conv_b"][l]), wts["rg_w_a_bf"][l],
            row(wts["rg_b_a"][l]), wts["rg_w_x_bf"][l], row(wts["rg_b_x"][l]), row(wts["rg_lambda"][l]), t_mid)
        pieces = [d_rgx, d_rgz, d_mlx, d_mlo, d_mlz]
        dx, dscale, dshift, g_ng = _in_bwd(pieces, sv["x"], dx, row(wts["norm_g"][l]), sv["scale"], wts["w_in_g"], l, t_mid)
        half = nd // 2
        w_cols = wts["w_in_g"].shape[3]
        gw_in = jnp.concatenate([_in_bwd_w(pieces, sv["hbf"], w_cols, tuple(range(0, half)), t_big),
                                 _in_bwd_w(pieces, sv["hbf"], w_cols, tuple(range(half, nd)), t_big)], axis=0)
        dmod = jnp.concatenate([dshift[0:1], dscale[0:1], dgate[0:1]], axis=1)
        gw_ada = _ada_bwd_w(cact_col, dmod, nd)
        grads[l] = dict(w_ada=gw_ada, w_in=gw_in, w_out=gw_out, w_qkv=jnp.stack([g_wq, g_wk, g_wv]),
                        rg_conv_w=g_rgcw[0:CONV_WIDTH], ml_conv_w=g_mlcw[0:CONV_WIDTH], wif_t=g_wift[0:8],
                        norm_g=g_ng[0], b_ada=dmod[0], rg_conv_b=g_rgcb[0], rg_w_a=g_wa, rg_b_a=g_ba[0], rg_w_x=g_wx,
                        rg_b_x=g_bx[0], rg_lambda=g_lam[0], ml_conv_b=g_mlcb[0], ml_b_if=g_bif[0, 0:8],
                        ml_norm_g=g_mlng[0])
    return loss_p[0, 0], dx, grads, g_final[0]


REPLICATED = ("norm_g", "b_ada", "rg_conv_b", "rg_w_a", "rg_b_a", "rg_w_x", "rg_b_x", "rg_lambda", "ml_conv_b",
              "ml_b_if", "ml_norm_g", "final_g")
ROW_ALIGN = N_DEV * SUBLANES


def _to_rows(a):
    flat = a.reshape(-1)
    pad = (-flat.shape[0]) % LANES
    return jnp.pad(flat, (0, pad)).reshape(-1, LANES)


def _pack(arrays):
    rows = jnp.concatenate([_to_rows(a) for a in arrays], axis=0)
    return jnp.pad(rows, ((0, (-rows.shape[0]) % ROW_ALIGN), (0, 0)))


def _unpack(rows, like):
    out, at = [], 0
    for a in like:
        n = -(-a.size // LANES)
        out.append(rows[at:at + n].reshape(-1)[:a.size].reshape(a.shape))
        at += n
    return out


def _small_pack(rg_conv_w, ml_conv_w, ml_w_if):
    nl = rg_conv_w.shape[0]
    wif_t = jnp.swapaxes(ml_w_if, 1, 2).reshape(nl, -1, LANES)
    return jnp.concatenate([rg_conv_w, ml_conv_w, wif_t], axis=1)


def _small_unpack(p, if_rows):
    nl = p.shape[0]
    rg_cw = p[:, 0:CONV_WIDTH]
    ml_cw = p[:, CONV_WIDTH:2 * CONV_WIDTH]
    wif = jnp.swapaxes(p[:, 2 * CONV_WIDTH:].reshape(nl, 8, if_rows), 1, 2)
    return rg_cw, ml_cw, wif


def _assemble_weights(big, small, rep):
    w_ada_g, w_in_g, w_out_g, qkv_g = big
    nd, nl = small.shape[0], small.shape[1]
    d = w_in_g.shape[2]
    dh = qkv_g.shape[3]
    nh = d // dh
    rsh = qkv_g.shape[2] // (3 * nh)
    w_qkv = qkv_g.reshape(nd, nl, 3, nh, rsh, dh).transpose(1, 2, 3, 0, 4, 5).reshape(nl, 3, nh, nd * rsh, dh)
    cw = small[:, :, 0:2 * CONV_WIDTH].reshape(nd, nl, 2, CONV_WIDTH, LANES).transpose(1, 2, 3, 0, 4)
    cw = cw.reshape(nl, 2, CONV_WIDTH, nd * LANES)
    if_rows = (small.shape[2] - 2 * CONV_WIDTH) * LANES // 8
    wif_t = small[:, :, 2 * CONV_WIDTH:].reshape(nd, nl, 8, if_rows).transpose(1, 2, 0, 3).reshape(nl, 8, nd * if_rows)
    wift_pad = jnp.pad(wif_t, ((0, 0), (0, LANES - 8), (0, 0))).astype(BF16)
    wif_pad = jnp.swapaxes(wift_pad, 1, 2)
    bif_pad = jnp.pad(rep["ml_b_if"], ((0, 0), (0, LANES - 8))).reshape(nl, 1, LANES)
    wts = dict(rep)
    wts.update(w_ada_g=w_ada_g, w_in_g=w_in_g, w_out_g=w_out_g, w_qkv=w_qkv, rg_conv_w=cw[:, 0], ml_conv_w=cw[:, 1],
               wif_pad=wif_pad, wift_pad=wift_pad, bif_pad=bif_pad, rg_w_a_bf=rep["rg_w_a"].astype(BF16),
               rg_w_x_bf=rep["rg_w_x"].astype(BF16))
    return wts


def _qkv_slots(g_qkv, nd):
    three, nh, dh, _ = g_qkv.shape
    return g_qkv.reshape(three, nh, nd, dh // nd, dh).transpose(2, 0, 1, 3, 4).reshape(nd, three * nh * (dh // nd), dh)


def _small_slots(g):
    nd = N_DEV
    cw = jnp.stack([g["rg_conv_w"], g["ml_conv_w"]]).reshape(2, CONV_WIDTH, nd, LANES).transpose(2, 0, 1, 3)
    cw = cw.reshape(nd, 2 * CONV_WIDTH, LANES)
    wif = g["wif_t"].reshape(8, nd, -1).transpose(1, 0, 2).reshape(nd, -1, LANES)
    return jnp.concatenate([cw, wif], axis=1)


def _kernel_unhosted(x, c, norm_g, w_ada, b_ada, w_in, rg_conv_w, rg_conv_b, rg_w_a, rg_b_a, rg_w_x, rg_b_x, rg_lambda, ml_conv_w, ml_conv_b, ml_w_q, ml_w_k, ml_w_v, ml_w_if, ml_b_if, ml_norm_g, w_out, final_g, loss_target, m_norm_g, m_w_ada, m_b_ada, m_w_in, m_rg_conv_w, m_rg_conv_b, m_rg_w_a, m_rg_b_a, m_rg_w_x, m_rg_b_x, m_rg_lambda, m_ml_conv_w, m_ml_conv_b, m_ml_w_q, m_ml_w_k, m_ml_w_v, m_ml_w_if, m_ml_b_if, m_ml_norm_g, m_w_out, m_final_g, v_norm_g, v_w_ada, v_b_ada, v_w_in, v_rg_conv_w, v_rg_conv_b, v_rg_w_a, v_rg_b_a, v_rg_w_x, v_rg_b_x, v_rg_lambda, v_ml_conv_w, v_ml_conv_b, v_ml_w_q, v_ml_w_k, v_ml_w_v, v_ml_w_if, v_ml_b_if, v_ml_norm_g, v_w_out, v_final_g):
    given = dict(locals())
    nl = w_in.shape[0]
    rep = {n: given[n] for n in REPLICATED}

    def qkv_shard(prefix):
        return jnp.stack([given[prefix + "ml_w_q"], given[prefix + "ml_w_k"], given[prefix + "ml_w_v"]], axis=1).reshape(
            nl, -1, ml_w_q.shape[-1])

    *big, small = _gather_two_level(
        [w_ada.astype(BF16), w_in.astype(BF16), w_out.astype(BF16), qkv_shard("").astype(BF16),
         _small_pack(rg_conv_w, ml_conv_w, ml_w_if)], "gather_weights")
    wts = _assemble_weights(big, small, rep)

    loss_p, grad_x, grads, g_final = _local_step(x[0], c, loss_target[0], wts)
    loss = lax.psum(loss_p, MESH_AXES)

    keys = ("w_ada", "w_in", "w_out", "w_qkv", "small")
    parity = lax.axis_index("c").astype(jnp.int32).reshape(1)
    recv = []
    for l in range(nl):
        g = grads[l]
        parts = [g["w_ada"], g["w_in"], g["w_out"], _qkv_slots(g["w_qkv"], N_DEV), _small_slots(g)]
        other = _core_swap(parts, "core_swap_layer%d" % l)
        sums = [_pair_sum(a, o, parity, "pair_sum_%s_layer%d" % (key, l)) for key, a, o in zip(keys, parts, other)]
        recv.append(_chip_swap(sums, "chip_swap_layer%d" % l))
    shard = {"": dict(w_ada=w_ada, w_in=w_in, w_out=w_out, w_qkv=qkv_shard(""),
                      small=_small_pack(rg_conv_w, ml_conv_w, ml_w_if))}
    for p in ("m_", "v_"):
        shard[p] = dict(w_ada=given[p + "w_ada"], w_in=given[p + "w_in"], w_out=given[p + "w_out"], w_qkv=qkv_shard(p),
                        small=_small_pack(given[p + "rg_conv_w"], given[p + "ml_conv_w"], given[p + "ml_w_if"]))
    res = {}
    for ki, key in enumerate(keys):
        res[key] = _reduce_adam([recv[l][ki] for l in range(nl)], shard[""][key], shard["m_"][key], shard["v_"][key],
                                "reduce_adam_" + key)

    rep_g = dict(final_g=g_final)
    for n in REPLICATED[:-1]:
        rep_g[n] = jnp.stack([grads[l][n] for l in range(nl)])
    pack_g = _pack([rep_g[n] for n in REPLICATED])
    rows = pack_g.shape[0] // N_DEV
    mine = _sum8(_exchange([pack_g.reshape(N_DEV, rows, LANES)], False, "reduce_scatter_replicated")[0], "sum_replicated")
    g_rep = _exchange([mine], True, "gather_replicated")[0].reshape(N_DEV * rows, LANES)
    rep_like = [rep[n] for n in REPLICATED]
    d_rep, m_rep, v_rep = _adam_call(_pack(rep_like), g_rep, _pack([given["m_" + n] for n in REPLICATED]),
                                     _pack([given["v_" + n] for n in REPLICATED]), "adam_replicated")
    rep_out = [dict(zip(REPLICATED, _unpack(a, rep_like))) for a in (g_rep, d_rep, m_rep, v_rep)]

    if_rows = ml_w_if.shape[1]
    order = ("norm_g", "w_ada", "b_ada", "w_in", "rg_conv_w", "rg_conv_b", "rg_w_a", "rg_b_a", "rg_w_x", "rg_b_x",
             "rg_lambda", "ml_conv_w", "ml_conv_b", "ml_w_q", "ml_w_k", "ml_w_v", "ml_w_if", "ml_b_if", "ml_norm_g",
             "w_out", "final_g")
    outs = [loss, grad_x[None]]
    for kind in range(4):
        qkv = res["w_qkv"][kind].reshape((nl, 3) + ml_w_q.shape[1:])
        rg_cw, ml_cw, wif = _small_unpack(res["small"][kind], if_rows)
        sharded = dict(w_ada=res["w_ada"][kind], w_in=res["w_in"][kind], w_out=res["w_out"][kind], ml_w_q=qkv[:, 0],
                       ml_w_k=qkv[:, 1], ml_w_v=qkv[:, 2], rg_conv_w=rg_cw, ml_conv_w=ml_cw, ml_w_if=wif)
        for n in order:
            outs.append(sharded[n] if n in sharded else rep_out[kind][n])
    return tuple(outs)


def _slot(block):
    return 4 * block[0] + 2 * block[1] + block[2]


def _dma_sems(*counts):
    return [pltpu.SemaphoreType.DMA((n,)) for n in counts]


def _start_all(copies):
    for cp in copies:
        cp.start()


def _gather_ici_comm(arrs):
    n = len(arrs)

    def copies(ins, outs, sems):
        send_sems, recv_sems, local_sems = sems
        x, y, c, sibling, chips = _mesh_place()
        me = (x, y, c)
        peers = [(*chip, c) for chip in chips] + [sibling]
        local = [pltpu.make_async_copy(ins[kk], outs[kk].at[_slot(me)], local_sems.at[kk]) for kk in range(n)]
        sends = [_remote(ins[kk], outs[kk].at[_slot(me)], send_sems, recv_sems, kk * 4 + j, peer)
                 for j, peer in enumerate(peers) for kk in range(n)]
        recvs = [_remote(ins[kk], outs[kk].at[_slot(peer)], send_sems, recv_sems, kk * 4 + j, peer)
                 for j, peer in enumerate(peers) for kk in range(n)]
        return local, sends, recvs

    def start(ins, outs, sems):
        local, sends, _ = copies(ins, outs, sems)
        _start_all(sends + local)

    def finish(ins, outs, sems):
        local, sends, recvs = copies(ins, outs, sems)
        for cp in recvs:
            cp.wait_recv()
        for cp in sends:
            cp.wait_send()
        for cp in local:
            cp.wait()

    return _Comm(arrs, [jax.ShapeDtypeStruct((N_DEV,) + a.shape, a.dtype) for a in arrs], _dma_sems(4 * n, 4 * n, n),
                 start, finish)


def _gather_fwd_comm(bufs):
    n = len(bufs)

    def copies(ins, outs, sems):
        send_sems, recv_sems = sems
        _, _, c, sibling, chips = _mesh_place()
        sends = [_remote(ins[kk].at[_slot((*chip, c))], outs[kk].at[_slot((*chip, c))], send_sems, recv_sems, kk * 3 + j, sibling)
                 for j, chip in enumerate(chips) for kk in range(n)]
        recvs = [_remote(ins[kk].at[_slot((*chip, c))], outs[kk].at[_slot((*chip, 1 - c))], send_sems, recv_sems, kk * 3 + j, sibling)
                 for j, chip in enumerate(chips) for kk in range(n)]
        return sends, recvs

    def start(ins, outs, sems):
        _start_all(copies(ins, outs, sems)[0])

    def finish(ins, outs, sems):
        sends, recvs = copies(ins, outs, sems)
        for cp in recvs:
            cp.wait_recv()
        for cp in sends:
            cp.wait_send()

    return _Comm(bufs, [jax.ShapeDtypeStruct(a.shape, a.dtype) for a in bufs], _dma_sems(3 * n, 3 * n), start, finish,
                 aliases=[(i, i) for i in range(n)])


def _core_swap_comm(arrs):
    n = len(arrs)

    def copies(ins, outs, sems):
        send_sems, recv_sems = sems
        _, _, c, sibling, _ = _mesh_place()
        return [_remote(ins[kk].at[2 * q + (1 - c)], outs[kk].at[q], send_sems, recv_sems, kk * N_CHIPS + q, sibling)
                for q in range(N_CHIPS) for kk in range(n)]

    def start(ins, outs, sems):
        _start_all(copies(ins, outs, sems))

    def finish(ins, outs, sems):
        cps = copies(ins, outs, sems)
        for cp in cps:
            cp.wait_recv()
        for cp in cps:
            cp.wait_send()

    return _Comm(arrs, [jax.ShapeDtypeStruct((N_CHIPS,) + a.shape[1:], a.dtype) for a in arrs],
                 _dma_sems(N_CHIPS * n, N_CHIPS * n), start, finish)


def _chip_swap_comm(arrs):
    n = len(arrs)
    per = N_CHIPS - 1

    def copies(ins, outs, sems):
        send_sems, recv_sems, local_sems = sems
        x, y, c, _, chips = _mesh_place()
        mine = 2 * x + y
        sends = [_remote(ins[kk].at[2 * chip[0] + chip[1]], outs[kk].at[mine], send_sems, recv_sems, kk * per + j, (*chip, c))
                 for j, chip in enumerate(chips) for kk in range(n)]
        recvs = [_remote(ins[kk].at[mine], outs[kk].at[2 * chip[0] + chip[1]], send_sems, recv_sems, kk * per + j, (*chip, c))
                 for j, chip in enumerate(chips) for kk in range(n)]
        local = [pltpu.make_async_copy(ins[kk].at[mine], outs[kk].at[mine], local_sems.at[kk]) for kk in range(n)]
        return local, sends, recvs

    def start(ins, outs, sems):
        local, sends, _ = copies(ins, outs, sems)
        _start_all(sends + local)

    def finish(ins, outs, sems):
        local, sends, recvs = copies(ins, outs, sems)
        for cp in recvs:
            cp.wait_recv()
        for cp in sends:
            cp.wait_send()
        for cp in local:
            cp.wait()

    return _Comm(arrs, [jax.ShapeDtypeStruct(a.shape, a.dtype) for a in arrs], _dma_sems(per * n, per * n, n), start, finish)


def _ada_mod(c_all, w_ada, b_cols):
    nl, d, w = w_ada.shape

    def body(c_ref, w_ref, b_ref, m_ref, ca_ref):
        sub = _iota((SUBLANES, d), 0)
        cv = jnp.zeros((SUBLANES, d), F32)
        for b in range(N_DEV):
            cv = jnp.where(sub == b, c_ref[b], cv)
        ca = cv * _sigmoid(cv)
        ca_ref[...] = ca
        m_ref[...] = jnp.zeros_like(m_ref)
        for l in range(nl):
            ml = _mm_hi(ca, w_ref[l]) + b_ref[l:l + 1, :]
            for b in range(N_DEV):
                m_ref[b, l:l + 1, :] = _row(ml, b)

    return pl.pallas_call(
        body, name="adaln_mod_columns", grid=(1,),
        in_specs=[_full(c_all.shape), _full(w_ada.shape), _full(b_cols.shape)],
        out_specs=[_full((N_DEV, SUBLANES, w)), _full((SUBLANES, d))],
        out_shape=[jax.ShapeDtypeStruct((N_DEV, SUBLANES, w), F32), jax.ShapeDtypeStruct((SUBLANES, d), F32)],
        compiler_params=_params(1),
    )(c_all, w_ada, b_cols)


def _ada_grad_adam(cact_t, dmods, w, m, v):
    nl, d, wd = w.shape
    tr = _row_tile(d, wd, 8)

    def body(c_ref, dm_ref, w_ref, m_ref, v_ref, g_ref, d_ref, mo_ref, vo_ref):
        cv = c_ref[...]
        dm = dm_ref[0]
        g = _col(cv, 0) * _row(dm, 0)
        for b in range(1, N_DEV):
            g = g + _col(cv, b) * _row(dm, b)
        delta, m2, v2 = _adam_math(w_ref[0], g, m_ref[0], v_ref[0])
        g_ref[0] = g
        d_ref[0] = delta
        mo_ref[0] = m2
        vo_ref[0] = v2

    blk = pl.BlockSpec((1, tr, wd), lambda l, i: (l, i, 0))
    return pl.pallas_call(
        body, name="adaln_grad_adam", grid=(nl, d // tr),
        in_specs=[pl.BlockSpec((tr, N_DEV), lambda l, i: (i, 0)), pl.BlockSpec((1, N_DEV, wd), lambda l, i: (l, 0, 0)),
                  blk, blk, blk],
        out_specs=[blk] * 4, out_shape=[jax.ShapeDtypeStruct((nl, d, wd), F32)] * 4,
        compiler_params=_params(2),
    )(cact_t, dmods, w, m, v)


class _Plan:
    def __init__(self):
        self.hosted, self.after = {}, {}

    def host(self, key, comm, then=None):
        self.hosted.setdefault(key, []).append(comm)
        if then is not None:
            self.after.setdefault(key, []).append(then)

    def comms(self, key):
        return self.hosted.pop(key, None)

    def done(self, key):
        for fn in self.after.pop(key, []):
            fn()

    def flush(self):
        while self.hosted:
            key = next(iter(self.hosted))
            _call(lambda: None, self.comms(key), name="exchange_after_%s_%d" % key, grid=(1,), in_specs=[], out_specs=[],
                  out_shape=[], args=())
            self.done(key)


def _layer_fwd(l, xl, mod_l, wl, rep, plan):
    s, d = xl.shape
    t_big, t_mid = _tile_for(s, 512), _tile_for(s, 256)
    nh_ml = rep["ml_b_if"].shape[1] // 2
    row = lambda a: a.reshape(1, -1)
    hosted = lambda name: plan.comms((name, l)) if plan else None
    done = lambda name: plan.done((name, l)) if plan else None
    shift, scale, gate = (row(mod_l[kk * d:(kk + 1) * d]) for kk in range(3))
    u, hbf = _in_fwd(xl, row(rep["norm_g"][l]), scale, shift, wl["w_in_g"], 0, t_mid, hosted("in_proj_fwd"))
    done("in_proj_fwd")
    h_rg, y_rg = _rg_fwd(u, d, wl["rg_conv_w"], row(rep["rg_conv_b"][l]), rep["rg_w_a_bf"][l], row(rep["rg_b_a"][l]),
                         rep["rg_w_x_bf"][l], row(rep["rg_b_x"][l]), row(rep["rg_lambda"][l]), t_mid, hosted("rglru_fwd"))
    done("rglru_fwd")
    q, k, v, gcol = _ml_pre(u, d, wl["ml_conv_w"], row(rep["ml_conv_b"][l]), wl["w_qkv"][0], wl["w_qkv"][1],
                            wl["w_qkv"][2], wl["wif_pad"], wl["bif_pad"], t_mid, hosted("mlstm_proj_fwd"))
    done("mlstm_proj_fwd")
    grow = gcol[:, 0:16].T
    cell, y_ml, cs, ns, ms, mt = _ml_cell_fwd(q, k, v, gcol, grow, u, row(rep["ml_norm_g"][l]), nh_ml,
                                              hosted("mlstm_cell_fwd"))
    done("mlstm_cell_fwd")
    x_new, y = _out_fwd(xl, y_rg, y_ml, gate, wl["w_out_g"], 0, t_big, hosted("out_proj_fwd"))
    done("out_proj_fwd")
    saved = dict(x=xl, u=u, hbf=hbf, h_rg=h_rg, y_rg=y_rg, q=q, k=k, v=v, gcol=gcol, grow=grow, cell=cell, y_ml=y_ml,
                 cs=cs, ns=ns, ms=ms, mt=mt, y=y, scale=scale, gate=gate)
    return x_new, saved


def _layer_bwd(l, dx, sv, wl, rep, plan, grads=None, split_last=False):
    s, d = dx.shape
    t_big, t_mid = _tile_for(s, 512), _tile_for(s, 256)
    nh_ml = rep["ml_b_if"].shape[1] // 2
    nd, _, _, w_cols = wl["w_in_g"].shape
    grads = {} if grads is None else grads
    row = lambda a: a.reshape(1, -1)
    hosted = lambda name: plan.comms((name, l)) if plan else None
    done = lambda name: plan.done((name, l)) if plan else None
    dy_rg, dy_ml, gw_out, dgate = _out_bwd(dx, sv["gate"], sv["y"], sv["y_rg"], sv["y_ml"], wl["w_out_g"], 0, t_big,
                                           hosted("out_proj_bwd"))
    grads.update(w_out=gw_out)
    done("out_proj_bwd")
    dq, dk, dv, dgates, d_mlo, d_mlz, g_mlng = _ml_cell_bwd(
        dy_ml, sv["u"], sv["cell"], sv["q"], sv["k"], sv["v"], sv["gcol"], sv["grow"], sv["mt"], sv["cs"], sv["ns"],
        sv["ms"], row(rep["ml_norm_g"][l]), nh_ml, hosted("mlstm_cell_bwd"))
    done("mlstm_cell_bwd")
    d_mlx, g_wq, g_wk, g_wv, g_wift, g_bif, g_mlcw, g_mlcb = _ml_pre_bwd(
        dq, dk, dv, dgates, sv["gcol"], sv["u"], sv["q"], sv["k"], sv["v"], wl["ml_conv_w"], row(rep["ml_conv_b"][l]),
        wl["w_qkv"][0], wl["w_qkv"][1], wl["w_qkv"][2], wl["wift_pad"], t_mid, hosted("mlstm_proj_bwd"))
    done("mlstm_proj_bwd")
    d_rgx, d_rgz, g_wa, g_wx, g_ba, g_bx, g_lam, g_rgcw, g_rgcb = _rg_bwd(
        dy_rg, sv["u"], sv["h_rg"], wl["rg_conv_w"], row(rep["rg_conv_b"][l]), rep["rg_w_a_bf"][l], row(rep["rg_b_a"][l]),
        rep["rg_w_x_bf"][l], row(rep["rg_b_x"][l]), row(rep["rg_lambda"][l]), t_mid, hosted("rglru_bwd"))
    grads.update(w_qkv=jnp.stack([g_wq, g_wk, g_wv]), rg_conv_w=g_rgcw[0:CONV_WIDTH], ml_conv_w=g_mlcw[0:CONV_WIDTH],
                 wif_t=g_wift[0:8], rg_conv_b=g_rgcb[0], rg_w_a=g_wa, rg_b_a=g_ba[0], rg_w_x=g_wx, rg_b_x=g_bx[0],
                 rg_lambda=g_lam[0], ml_conv_b=g_mlcb[0], ml_b_if=g_bif[0, 0:8], ml_norm_g=g_mlng[0])
    done("rglru_bwd")
    pieces = [d_rgx, d_rgz, d_mlx, d_mlo, d_mlz]
    grads.update(w_in=_in_bwd_w(pieces, sv["hbf"], w_cols, tuple(range(nd)), t_big, hosted("in_proj_bwd_w")))
    done("in_proj_bwd_w")
    n_tiles = s // t_mid
    counts = [n_tiles // 4, n_tiles - n_tiles // 4 - 1, 1] if split_last and n_tiles >= 4 else [n_tiles]
    in_args = (pieces, sv["x"], dx, row(rep["norm_g"][l]), sv["scale"], wl["w_in_g"], 0, t_mid)
    res, at = None, 0
    for key, count in zip(("in_proj_bwd_x", "in_proj_bwd_x_rest", "in_proj_bwd_x_end"), counts):
        res = _in_bwd(*in_args, hosted(key), (at, count), res)
        done(key)
        at += count
    dx, dscale, dshift, g_ng = res
    dmod = jnp.concatenate([dshift[0:1], dscale[0:1], dgate[0:1]], axis=1)
    grads.update(dmod=dmod, norm_g=g_ng[0], b_ada=dmod[0])
    return dx, grads


def _local_step(x, c, target, wts):
    d = x.shape[1]
    nl = wts["w_in_g"].shape[1]
    mod, cact = _mod_call(c, wts["w_ada_g"], wts["b_ada"])
    wl = [dict(w_in_g=wts["w_in_g"][:, l:l + 1], w_out_g=wts["w_out_g"][:, l:l + 1], w_qkv=wts["w_qkv"][l],
               rg_conv_w=wts["rg_conv_w"][l], ml_conv_w=wts["ml_conv_w"][l], wif_pad=wts["wif_pad"][l],
               wift_pad=wts["wift_pad"][l], bif_pad=wts["bif_pad"][l]) for l in range(nl)]
    saved, xl = [], x
    for l in range(nl):
        xl, sv = _layer_fwd(l, xl, mod[l], wl[l], wts, None)
        saved.append(sv)
    dx, loss_p, g_final = _loss_call(xl, wts["final_g"].reshape(1, -1), target, _tile_for(x.shape[0], 512))
    grads = [None] * nl
    for l in reversed(range(nl)):
        dx, grads[l] = _layer_bwd(l, dx, saved[l], wl[l], wts, None)
        grads[l]["w_ada"] = _ada_bwd_w(cact[0].reshape(d, 1), grads[l]["dmod"], wts["w_ada_g"].shape[0])
        grads[l]["b_ada"] = grads[l]["dmod"][0]
    return loss_p[0, 0], dx, grads, g_final[0]


def _full_qkv(qkv_g, d):
    nd, _, rows3, dh = qkv_g.shape
    nh = d // dh
    rsh = rows3 // (3 * nh)
    return qkv_g.reshape(nd, 3, nh, rsh, dh).transpose(1, 2, 0, 3, 4).reshape(3, nh, nd * rsh, dh)


def _small_weights(small, l, ml_b_if):
    nd = small.shape[0]
    sm = small[:, l]
    cw = sm[:, 0:2 * CONV_WIDTH].reshape(nd, 2, CONV_WIDTH, LANES).transpose(1, 2, 0, 3).reshape(2, CONV_WIDTH, nd * LANES)
    if_rows = (sm.shape[1] - 2 * CONV_WIDTH) * LANES // 8
    wif_t = sm[:, 2 * CONV_WIDTH:].reshape(nd, 8, if_rows).transpose(1, 0, 2).reshape(8, nd * if_rows)
    wift_pad = jnp.pad(wif_t, ((0, LANES - 8), (0, 0))).astype(BF16)
    return dict(rg_conv_w=cw[0], ml_conv_w=cw[1], wift_pad=wift_pad, wif_pad=wift_pad.T,
                bif_pad=jnp.pad(ml_b_if[l], (0, LANES - 8)).reshape(1, LANES))


def kernel(x, c, norm_g, w_ada, b_ada, w_in, rg_conv_w, rg_conv_b, rg_w_a, rg_b_a, rg_w_x, rg_b_x, rg_lambda, ml_conv_w, ml_conv_b, ml_w_q, ml_w_k, ml_w_v, ml_w_if, ml_b_if, ml_norm_g, w_out, final_g, loss_target, m_norm_g, m_w_ada, m_b_ada, m_w_in, m_rg_conv_w, m_rg_conv_b, m_rg_w_a, m_rg_b_a, m_rg_w_x, m_rg_b_x, m_rg_lambda, m_ml_conv_w, m_ml_conv_b, m_ml_w_q, m_ml_w_k, m_ml_w_v, m_ml_w_if, m_ml_b_if, m_ml_norm_g, m_w_out, m_final_g, v_norm_g, v_w_ada, v_b_ada, v_w_in, v_rg_conv_w, v_rg_conv_b, v_rg_w_a, v_rg_b_a, v_rg_w_x, v_rg_b_x, v_rg_lambda, v_ml_conv_w, v_ml_conv_b, v_ml_w_q, v_ml_w_k, v_ml_w_v, v_ml_w_if, v_ml_b_if, v_ml_norm_g, v_w_out, v_final_g):
    given = dict(locals())
    nl = w_in.shape[0]
    d = x.shape[2]
    rep = {n: given[n] for n in REPLICATED}
    rep.update(rg_w_a_bf=rg_w_a.astype(BF16), rg_w_x_bf=rg_w_x.astype(BF16))
    bf = lambda a: a.astype(BF16)

    def qkv_shard(prefix):
        return jnp.stack([given[prefix + "ml_w_q"], given[prefix + "ml_w_k"], given[prefix + "ml_w_v"]], axis=1).reshape(
            nl, -1, ml_w_q.shape[-1])

    def small_shard(prefix):
        return _small_pack(given[prefix + "rg_conv_w"], given[prefix + "ml_conv_w"], given[prefix + "ml_w_if"])

    plan = _Plan()
    qkv = qkv_shard("")
    w_in_first, small = _gather_two_level([bf(w_in[0:1]), small_shard("")], "gather_first")
    wl = [_small_weights(small, l, ml_b_if) for l in range(nl)]
    wl[0]["w_in_g"] = w_in_first

    def gather_behind(arrs, ici_host, fwd_host, then):
        ici = _gather_ici_comm(arrs)

        def pass_on():
            fwd = _gather_fwd_comm(ici.results)
            plan.host(fwd_host, fwd, lambda: then(fwd.results))

        plan.host(ici_host, ici, pass_on)

    def got_out(l):
        return lambda r: wl[l].update(w_out_g=r[0], w_qkv=_full_qkv(r[1], d))

    gather_behind([bf(w_out[0:1]), bf(qkv[0:1])], ("in_proj_fwd", 0), ("rglru_fwd", 0), got_out(0))
    for l in range(1, nl):
        gather_behind([bf(w_in[l:l + 1])], ("rglru_fwd", l - 1), ("mlstm_proj_fwd", l - 1),
                      lambda r, l=l: wl[l].update(w_in_g=r[0]))
        gather_behind([bf(w_out[l:l + 1]), bf(qkv[l:l + 1])], ("mlstm_cell_fwd", l - 1), ("out_proj_fwd", l - 1), got_out(l))

    wcols = w_ada.shape[2]
    c_all = _exchange([jnp.broadcast_to(c, (SUBLANES, d))], True, "gather_condition")[0]
    me = 4 * lax.axis_index("x") + 2 * lax.axis_index("y") + lax.axis_index("c")
    b_cols = jnp.pad(lax.dynamic_slice_in_dim(b_ada, me * wcols, wcols, axis=1), ((0, SUBLANES - nl), (0, 0)))
    mod_cols, cact_all = _ada_mod(c_all, w_ada, b_cols)
    mod_blocks = _exchange([mod_cols], False, "scatter_modulation")[0]
    mod = mod_blocks[:, 0:nl].transpose(1, 0, 2).reshape(nl, N_DEV * wcols)
    saved, xl = [], x[0]
    for l in range(nl):
        xl, sv = _layer_fwd(l, xl, mod[l], wl[l], rep, plan)
        saved.append(sv)
    grad_x, loss_p, g_final = _loss_call(xl, final_g.reshape(1, -1), loss_target[0], _tile_for(xl.shape[0], 512))

    keys = ("w_in", "w_out", "w_qkv", "small")
    parity = lax.axis_index("c").astype(jnp.int32).reshape(1)
    grads, recv = [None] * nl, [None] * nl

    def parts_of(g):
        return [g["w_in"], g["w_out"], _qkv_slots(g["w_qkv"], N_DEV), _small_slots(g)]

    def pair_sums(l, parts, other):
        return [_pair_sum(a, o, parity, "pair_sum_%s_layer%d" % (key, l)) for key, a, o in zip(keys, parts, other)]

    def reduce_behind(l, host_layer):
        parts = parts_of(grads[l])
        swap = _core_swap_comm(parts)

        def summed():
            sums = pair_sums(l, parts, swap.results)
            big = _chip_swap_comm([sums[0]])
            rest = _chip_swap_comm(sums[1:])
            plan.host(("mlstm_cell_bwd", host_layer), big)
            plan.host(("rglru_bwd", host_layer), rest, lambda: recv.__setitem__(l, big.results + rest.results))

        plan.host(("out_proj_bwd", host_layer), swap, summed)

    first, own = {}, {}

    def reduce_own(names, parts_fn, ready_key, swap_key, chip_key):
        def go():
            parts = parts_fn()
            swap = _core_swap_comm(parts)

            def summed():
                sums = [_pair_sum(a, o, parity, "pair_sum_%s_layer0" % n) for n, a, o in zip(names, parts, swap.results)]
                chip = _chip_swap_comm(sums)
                plan.host(chip_key, chip, lambda: own.update(zip(names, chip.results)))

            plan.host(swap_key, swap, summed)

        plan.after.setdefault(ready_key, []).append(go)

    reduce_own(["w_out"], lambda: [first["w_out"]], ("out_proj_bwd", 0), ("mlstm_cell_bwd", 0), ("mlstm_proj_bwd", 0))
    reduce_own(["w_qkv", "small"], lambda: [_qkv_slots(first["w_qkv"], N_DEV), _small_slots(first)],
               ("rglru_bwd", 0), ("in_proj_bwd_w", 0), ("in_proj_bwd_x", 0))
    reduce_own(["w_in"], lambda: [first["w_in"]], ("in_proj_bwd_w", 0), ("in_proj_bwd_x", 0), ("in_proj_bwd_x_rest", 0))

    for l in reversed(range(nl)):
        if l > 0:
            grad_x, grads[l] = _layer_bwd(l, grad_x, saved[l], wl[l], rep, plan)
            reduce_behind(l, l - 1)
        else:
            grad_x, grads[l] = _layer_bwd(l, grad_x, saved[l], wl[l], rep, plan, first, True)
    plan.flush()
    recv[0] = [own[key] for key in keys]

    shard = {p: dict(w_in=given[p + "w_in"], w_out=given[p + "w_out"], w_qkv=qkv_shard(p), small=small_shard(p))
             for p in ("", "m_", "v_")}
    res = {}
    for ki, key in enumerate(keys):
        res[key] = _reduce_adam([recv[l][ki] for l in range(nl)], shard[""][key], shard["m_"][key], shard["v_"][key],
                                "reduce_adam_" + key)

    dmods = jnp.concatenate([grads[l]["dmod"] for l in range(nl)], axis=0)
    dmod_blocks = jnp.pad(dmods.reshape(nl, N_DEV, wcols).transpose(1, 0, 2), ((0, 0), (0, SUBLANES - nl), (0, 0)))
    dmod_all = _exchange([dmod_blocks], False, "scatter_dmod")[0][:, 0:nl].transpose(1, 0, 2)
    res["w_ada"] = _ada_grad_adam(cact_all.T, dmod_all, w_ada, m_w_ada, v_w_ada)

    rep_g = dict(final_g=g_final[0])
    for n in REPLICATED[:-1]:
        rep_g[n] = jnp.stack([grads[l][n] for l in range(nl)])
    blank = [jnp.zeros((1,), F32)]
    pack_g = _pack([rep_g[n] for n in REPLICATED] + [loss_p[0, 0:1]])
    rows = pack_g.shape[0] // N_DEV
    mine = _sum8(_exchange([pack_g.reshape(N_DEV, rows, LANES)], False, "reduce_scatter_replicated")[0], "sum_replicated")
    g_rep = _exchange([mine], True, "gather_replicated")[0].reshape(N_DEV * rows, LANES)
    rep_like = [given[n] for n in REPLICATED]
    d_rep, m_rep, v_rep = _adam_call(_pack(rep_like + blank), g_rep, _pack([given["m_" + n] for n in REPLICATED] + blank),
                                     _pack([given["v_" + n] for n in REPLICATED] + blank), "adam_replicated")
    rep_out = [dict(zip(REPLICATED, _unpack(a, rep_like))) for a in (g_rep, d_rep, m_rep, v_rep)]
    loss = _unpack(g_rep, rep_like + blank)[-1].reshape(())

    if_rows = ml_w_if.shape[1]
    order = ("norm_g", "w_ada", "b_ada", "w_in", "rg_conv_w", "rg_conv_b", "rg_w_a", "rg_b_a", "rg_w_x", "rg_b_x",
             "rg_lambda", "ml_conv_w", "ml_conv_b", "ml_w_q", "ml_w_k", "ml_w_v", "ml_w_if", "ml_b_if", "ml_norm_g",
             "w_out", "final_g")
    outs = [loss, grad_x[None]]
    for kind in range(4):
        qkv_k = res["w_qkv"][kind].reshape((nl, 3) + ml_w_q.shape[1:])
        rg_cw, ml_cw, wif = _small_unpack(res["small"][kind], if_rows)
        sharded = dict(w_ada=res["w_ada"][kind], w_in=res["w_in"][kind], w_out=res["w_out"][kind], ml_w_q=qkv_k[:, 0],
                       ml_w_k=qkv_k[:, 1], ml_w_v=qkv_k[:, 2], rg_conv_w=rg_cw, ml_conv_w=ml_cw, ml_w_if=wif)
        for n in order:
            outs.append(sharded[n] if n in sharded else rep_out[kind][n])
    return tuple(outs)
```

```python
import functools

import jax
import jax.numpy as jnp
from jax import lax
from jax.experimental import pallas as pl
from jax.experimental.pallas import tpu as pltpu

F32 = jnp.float32
BF16 = jnp.bfloat16
MESH_AXES = ("x", "y", "c")
N_DEV = 8
EPS = 1e-6
RG_C = 8.0
ML_CHUNK = 128
CONV_WIDTH = 4
ADAM_LR = 0.001
ADAM_B1 = 0.9
ADAM_B2 = 0.999
ADAM_EPS = 1e-08
ADAM_WD = 0.01
ADAM_STEP = 10
NEG_BIG = -1e30
LANES = 128
SUBLANES = 8
VMEM_LIMIT = 56 * 1024 * 1024
HI = lax.Precision.HIGHEST


def _params(n_grid):
    return pltpu.CompilerParams(dimension_semantics=("arbitrary",) * n_grid, vmem_limit_bytes=VMEM_LIMIT)


def _mm(a, b):
    return jnp.dot(a.astype(BF16), b.astype(BF16), preferred_element_type=F32)


def _mm_nt(a, b):
    return lax.dot_general(a.astype(BF16), b.astype(BF16), (((1,), (1,)), ((), ())), preferred_element_type=F32)


def _mm_tn(a, b):
    return lax.dot_general(a.astype(BF16), b.astype(BF16), (((0,), (0,)), ((), ())), preferred_element_type=F32)


def _mm_hi(a, b):
    return jnp.dot(a, b, precision=HI, preferred_element_type=F32)


def _sigmoid(x):
    return 1.0 / (1.0 + jnp.exp(-x))


def _softplus(x):
    return jnp.maximum(x, 0.0) + jnp.log(1.0 + jnp.exp(-jnp.abs(x)))


def _neg_expm1(x):
    poly = -x * (1.0 + x * (0.5 + x * (1.0 / 6.0 + x * (1.0 / 24.0 + x * (1.0 / 120.0)))))
    return jnp.where(jnp.abs(x) < 0.05, poly, 1.0 - jnp.exp(x))


def _iota(shape, dim):
    return lax.broadcasted_iota(jnp.int32, shape, dim)


def _colsum(x):
    return jnp.sum(x, axis=0, keepdims=True)


def _rowsum(x):
    return jnp.sum(x, axis=1, keepdims=True)


def _col(x, j):
    return _rowsum(jnp.where(_iota(x.shape, 1) == j, x, 0.0))


def _row(x, j):
    return _colsum(jnp.where(_iota(x.shape, 0) == j, x, 0.0))


def _shift_down(x, j, prev8):
    if j == 0:
        return x
    t = x.shape[0]
    main = jnp.where(_iota(x.shape, 0) >= j, pltpu.roll(x, j, 0), 0.0)
    fix = jnp.where(_iota(prev8.shape, 0) < j, pltpu.roll(prev8, j, 0), 0.0)
    return jnp.concatenate([main[0:SUBLANES] + fix, main[SUBLANES:t]], axis=0)


def _shift_up(x, j, next8):
    if j == 0:
        return x
    t = x.shape[0]
    main = jnp.where(_iota(x.shape, 0) < t - j, pltpu.roll(x, t - j, 0), 0.0)
    fix = jnp.where(_iota(next8.shape, 0) >= SUBLANES - j, pltpu.roll(next8, SUBLANES - j, 0), 0.0)
    return jnp.concatenate([main[0:t - SUBLANES], main[t - SUBLANES:t] + fix], axis=0)


def _conv(x, prev8, w_ref):
    y = w_ref[CONV_WIDTH - 1:CONV_WIDTH, :] * x
    for j in range(1, CONV_WIDTH):
        y = y + w_ref[CONV_WIDTH - 1 - j:CONV_WIDTH - j, :] * _shift_down(x, j, prev8)
    return y


def _conv_bwd_x(dy, next8, w_ref):
    dx = w_ref[CONV_WIDTH - 1:CONV_WIDTH, :] * dy
    for j in range(1, CONV_WIDTH):
        dx = dx + w_ref[CONV_WIDTH - 1 - j:CONV_WIDTH - j, :] * _shift_up(dy, j, next8)
    return dx


def _scan_into(a, b, carry, out_ref, reverse):
    t, c = a.shape
    groups = t // SUBLANES
    a3 = a.reshape(groups, SUBLANES, c)
    b3 = b.reshape(groups, SUBLANES, c)
    sub = _iota(a3.shape, 1)
    for step in (1, 2, 4):
        keep = sub < SUBLANES - step if reverse else sub >= step
        shift = SUBLANES - step if reverse else step
        a_s = jnp.where(keep, pltpu.roll(a3, shift, 1), 1.0)
        b_s = jnp.where(keep, pltpu.roll(b3, shift, 1), 0.0)
        b3 = a3 * b_s + b3
        a3 = a3 * a_s
    for g in (reversed(range(groups)) if reverse else range(groups)):
        rows = slice(g * SUBLANES, (g + 1) * SUBLANES)
        out_ref[rows, :] = b3[g] + a3[g] * carry
        edge = g * SUBLANES if reverse else (g + 1) * SUBLANES - 1
        carry = out_ref[edge:edge + 1, :]


def _blockdiag(x, w_ref, transpose_w=False):
    nh, dh, _ = w_ref.shape
    outs = []
    for h in range(nh):
        xs = x[:, h * dh:(h + 1) * dh]
        outs.append(_mm_nt(xs, w_ref[h]) if transpose_w else _mm(xs, w_ref[h]))
    return jnp.concatenate(outs, axis=1)


def _rg_gates(xc, wa_ref, ba_ref, wx_ref, bx_ref, lam_ref):
    r = _sigmoid(_blockdiag(xc, wa_ref) + ba_ref[...])
    ig = _sigmoid(_blockdiag(xc, wx_ref) + bx_ref[...])
    sp = _softplus(-lam_ref[...])
    log_a = -RG_C * r * sp
    a = jnp.exp(log_a)
    beta = jnp.sqrt(_neg_expm1(2.0 * log_a))
    return r, ig, sp, a, beta


def _bcast8(row):
    return jnp.broadcast_to(row, (SUBLANES, row.shape[1]))


def _full(shape):
    nd = len(shape)
    return pl.BlockSpec(shape, lambda *_: (0,) * nd)


class _Comm:
    def __init__(self, arrays, out_shapes, sems, start, finish, aliases=()):
        self.arrays, self.out_shapes, self.sems = list(arrays), list(out_shapes), list(sems)
        self.start, self.finish, self.aliases = start, finish, tuple(aliases)
        self.results = None


class _RowOf:
    def __init__(self, ref, k):
        self.ref, self.k = ref, k

    def __getitem__(self, idx):
        cols = slice(None) if idx is Ellipsis else idx[1]
        return self.ref[0, self.k:self.k + 1, cols]


def _vec(table, layer, k):
    return ("row", table, layer, k)


def _is_row(arg):
    return isinstance(arg, tuple) and len(arg) == 4 and arg[0] == "row"


def _call(body, comms, *, name, grid, in_specs, out_specs, out_shape, args, scratch_shapes=(), aliases=None):
    comms = [cm for cm in (comms or []) if cm is not None]
    rows = {i: a[3] for i, a in enumerate(args) if _is_row(a)}
    in_specs = [pl.BlockSpec((1,) + a[1].shape[1:], functools.partial(lambda layer, *_: (layer, 0, 0), a[2]))
                if _is_row(a) else sp for a, sp in zip(args, in_specs)]
    args = tuple(a[1] if _is_row(a) else a for a in args)
    n_in, n_out, n_sc = len(args), len(out_shape), len(scratch_shapes)
    c_arrays = [a for cm in comms for a in cm.arrays]
    c_outs = [o for cm in comms for o in cm.out_shapes]
    c_sems = [sm for cm in comms for sm in cm.sems]
    aliases, a_at, o_at = dict(aliases or {}), n_in, n_out
    for cm in comms:
        for (i, j) in cm.aliases:
            aliases[a_at + i] = o_at + j
        a_at += len(cm.arrays)
        o_at += len(cm.out_shapes)

    def wrapped(*refs):
        ins, c_in = refs[:n_in], refs[n_in:n_in + len(c_arrays)]
        ins = [_RowOf(r, rows[i]) if i in rows else r for i, r in enumerate(ins)]
        at = n_in + len(c_arrays)
        outs, c_out = refs[at:at + n_out], refs[at + n_out:at + n_out + len(c_outs)]
        at += n_out + len(c_outs)
        scr, sems = refs[at:at + n_sc], refs[at + n_sc:]
        views, ia, io, isem = [], 0, 0, 0
        for cm in comms:
            views.append((c_in[ia:ia + len(cm.arrays)], c_out[io:io + len(cm.out_shapes)], sems[isem:isem + len(cm.sems)]))
            ia, io, isem = ia + len(cm.arrays), io + len(cm.out_shapes), isem + len(cm.sems)
        if comms:
            @pl.when(pl.program_id(0) == 0)
            def _():
                for cm, view in zip(comms, views):
                    cm.start(*view)

        body(*ins, *outs, *scr)
        if comms:
            @pl.when(pl.program_id(0) == grid[0] - 1)
            def _():
                for cm, view in zip(comms, views):
                    cm.finish(*view)

    hbm = pl.BlockSpec(memory_space=pl.ANY)
    res = pl.pallas_call(
        wrapped, name=name, grid=grid,
        in_specs=list(in_specs) + [hbm] * len(c_arrays), out_specs=list(out_specs) + [hbm] * len(c_outs),
        out_shape=list(out_shape) + c_outs, scratch_shapes=list(scratch_shapes) + c_sems,
        input_output_aliases=aliases, compiler_params=_params(len(grid)),
    )(*args, *c_arrays)
    at = n_out
    for cm in comms:
        cm.results = list(res[at:at + len(cm.out_shapes)])
        at += len(cm.out_shapes)
    return list(res[:n_out])


def _mod_call(c, w_ada_g, b_ada):
    nd, nl, d, w = w_ada_g.shape

    def body(c_ref, w_ref, b_ref, mod_ref, cact_ref):
        cv = c_ref[...]
        ca = _bcast8(cv * _sigmoid(cv))
        cact_ref[...] = ca
        mod_ref[0, 0] = _mm(ca, w_ref[0, 0]) + b_ref[0, 0]

    mod, cact = pl.pallas_call(
        body, name="adaln_mod", grid=(nl, nd),
        in_specs=[_full((1, d)),
                  pl.BlockSpec((1, 1, d, w), lambda l, j: (j, l, 0, 0)),
                  pl.BlockSpec((1, 1, 1, w), lambda l, j: (l, j, 0, 0))],
        out_specs=[pl.BlockSpec((1, 1, SUBLANES, w), lambda l, j: (l, j, 0, 0)), _full((SUBLANES, d))],
        out_shape=[jax.ShapeDtypeStruct((nl, nd, SUBLANES, w), F32), jax.ShapeDtypeStruct((SUBLANES, d), F32)],
        compiler_params=_params(2),
    )(c, w_ada_g, b_ada.reshape(nl, nd, 1, w))
    return mod[:, :, 0, :].reshape(nl, nd * w), cact


def _join_columns(w_ref, wcat):
    nd, _, _, w = w_ref.shape

    @pl.when(pl.program_id(0) == 0)
    def _():
        for j in range(nd):
            wcat[:, j * w:(j + 1) * w] = w_ref[j, 0]


def _in_fwd(x, ng, scale, shift, w_in_g, layer, tile, comms=None):
    s, d = x.shape
    nd, _, _, w = w_in_g.shape

    def body(x_ref, ng_ref, sc_ref, sh_ref, w_ref, u_ref, h_ref, wcat):
        _join_columns(w_ref, wcat)
        xv = x_ref[...]
        rs = lax.rsqrt(jnp.mean(xv * xv, axis=1, keepdims=True) + EPS)
        hb = (xv * rs * ng_ref[...] * (1.0 + sc_ref[...]) + sh_ref[...]).astype(BF16)
        h_ref[...] = hb
        u_ref[...] = jnp.dot(hb, wcat[...], preferred_element_type=F32)

    return _call(
        body, comms, name="in_proj_fwd", grid=(s // tile,),
        in_specs=[pl.BlockSpec((tile, d), lambda i: (i, 0)), _full((1, d)), _full((1, d)), _full((1, d)),
                  pl.BlockSpec((nd, 1, d, w), lambda i: (0, layer, 0, 0), pipeline_mode=pl.Buffered(1))],
        out_specs=[pl.BlockSpec((tile, nd * w), lambda i: (i, 0)), pl.BlockSpec((tile, d), lambda i: (i, 0))],
        out_shape=[jax.ShapeDtypeStruct((s, nd * w), F32), jax.ShapeDtypeStruct((s, d), BF16)],
        scratch_shapes=[pltpu.VMEM((d, nd * w), BF16)],
        args=(x, ng, scale, shift, w_in_g))


def _rg_fwd(u, d, conv_w, conv_b, w_a, b_a, w_x, b_x, lam, tile, comms=None):
    s = u.shape[0]

    def body(x_ref, z_ref, cw_ref, cb_ref, wa_ref, ba_ref, wx_ref, bx_ref, lam_ref, h_ref, y_ref, prev8, hcar):
        @pl.when(pl.program_id(0) == 0)
        def _():
            prev8[...] = jnp.zeros_like(prev8)
            hcar[...] = jnp.zeros_like(hcar)

        x = x_ref[...]
        xc = _conv(x, prev8[...], cw_ref) + cb_ref[...]
        prev8[...] = x[tile - SUBLANES:tile, :]
        _, ig, _, a, beta = _rg_gates(xc, wa_ref, ba_ref, wx_ref, bx_ref, lam_ref)
        _scan_into(a, beta * ig * xc, hcar[SUBLANES - 1:SUBLANES, :], h_ref, False)
        h = h_ref[...]
        hcar[...] = h[tile - SUBLANES:tile, :]
        z = z_ref[...]
        y_ref[...] = h * z * _sigmoid(z)

    vec = _full((1, d))
    return _call(
        body, comms, name="rglru_fwd", grid=(s // tile,),
        in_specs=[pl.BlockSpec((tile, d), lambda i: (i, 0)), pl.BlockSpec((tile, d), lambda i: (i, 1)),
                  _full(conv_w.shape), vec, _full(w_a.shape), vec, _full(w_x.shape), vec, vec],
        out_specs=[pl.BlockSpec((tile, d), lambda i: (i, 0))] * 2,
        out_shape=[jax.ShapeDtypeStruct((s, d), F32)] * 2,
        scratch_shapes=[pltpu.VMEM((SUBLANES, d), F32), pltpu.VMEM((SUBLANES, d), F32)],
        args=(u, u, conv_w, conv_b, w_a, b_a, w_x, b_x, lam))


def _ml_pre(u, d, conv_w, conv_b, w_q, w_k, w_v, wif, bif, tile, comms=None):
    s = u.shape[0]
    nh = w_q.shape[0]

    def body(x_ref, cw_ref, cb_ref, wq_ref, wk_ref, wv_ref, wif_ref, bif_ref, q_ref, k_ref, v_ref, g_ref, prev8):
        @pl.when(pl.program_id(0) == 0)
        def _():
            prev8[...] = jnp.zeros_like(prev8)

        x = x_ref[...]
        pre = _conv(x, prev8[...], cw_ref) + cb_ref[...]
        prev8[...] = x[tile - SUBLANES:tile, :]
        xc = pre * _sigmoid(pre)
        q = _blockdiag(xc, wq_ref)
        k = _blockdiag(xc, wk_ref)
        v = _blockdiag(x, wv_ref)
        q_ref[...] = q
        k_ref[...] = k
        v_ref[...] = v
        g = _mm(q, wif_ref[0:d, :]) + _mm(k, wif_ref[d:2 * d, :]) + _mm(v, wif_ref[2 * d:3 * d, :]) + bif_ref[...]
        lane = _iota(g.shape, 1)
        gl = jnp.where(lane < 4, g, jnp.where(lane < 8, -_softplus(-g), 0.0))
        tri = jnp.where(_iota((ML_CHUNK, ML_CHUNK), 1) <= _iota((ML_CHUNK, ML_CHUNK), 0), 1.0, 0.0)
        cums = [_mm_hi(tri, gl[c * ML_CHUNK:(c + 1) * ML_CHUNK, :]) for c in range(tile // ML_CHUNK)]
        cum = cums[0] if len(cums) == 1 else jnp.concatenate(cums, axis=0)
        g_ref[...] = gl + jnp.where((lane >= 8) & (lane < 12), pltpu.roll(cum, 4, 1), 0.0)

    vec = _full((1, d))
    return _call(
        body, comms, name="mlstm_proj_fwd", grid=(s // tile,),
        in_specs=[pl.BlockSpec((tile, d), lambda i: (i, 2)), _full(conv_w.shape), vec,
                  _full(w_q.shape), _full(w_k.shape), _full(w_v.shape), _full(wif.shape), _full((1, LANES))],
        out_specs=[pl.BlockSpec((tile, d), lambda i: (i, 0))] * 3 + [pl.BlockSpec((tile, LANES), lambda i: (i, 0))],
        out_shape=[jax.ShapeDtypeStruct((s, d), F32)] * 3 + [jax.ShapeDtypeStruct((s, LANES), F32)],
        scratch_shapes=[pltpu.VMEM((SUBLANES, d), F32)],
        args=(u, conv_w, conv_b, w_q, w_k, w_v, wif, bif))


def _cell_chunk(h, nh, q_ref, k_ref, v_ref, gc, gr, m_prev, c_h, n_h, m_t=None):
    lc = ML_CHUNK
    dh = q_ref.shape[1] // nh
    sl = slice(h * dh, (h + 1) * dh)
    qh = q_ref[:, sl]
    kh = k_ref[:, sl] * (dh ** -0.5)
    vh = v_ref[:, sl]
    li_c = _col(gc, h)
    b_c = _col(gc, 8 + h)
    lib_r = _row(gr, h) - _row(gr, 8 + h)
    b_last = _colsum(jnp.where(_iota((lc, 1), 0) == lc - 1, b_c, 0.0))
    causal = _iota((lc, lc), 1) <= _iota((lc, lc), 0)
    dmat = jnp.where(causal, b_c + lib_r, NEG_BIG)
    m_inter = b_c + m_prev
    if m_t is None:
        m_t = jnp.maximum(m_inter, jnp.max(dmat, axis=1, keepdims=True))
    w_intra = jnp.exp(dmat - m_t)
    w_inter = jnp.exp(m_inter - m_t)
    amat = _mm_nt(qh, kh)
    smat = amat * w_intra
    qc = _mm(qh, c_h)
    qn = _rowsum(qh * n_h)
    den = _rowsum(smat) + w_inter * qn
    gst = b_last - b_c + li_c
    m_new = jnp.maximum(b_last + m_prev, jnp.max(gst, axis=0, keepdims=True))
    w_state = jnp.exp(gst - m_new)
    decay = jnp.exp(b_last + m_prev - m_new)
    return dict(sl=sl, qh=qh, kh=kh, vh=vh, m_t=m_t, w_intra=w_intra, w_inter=w_inter, smat=smat, qc=qc, qn=qn,
                den=den, m_new=m_new, w_state=w_state, decay=decay)


def _ml_cell_fwd(q, k, v, gcol, grow, u, ng, nh, comms=None):
    s, d = q.shape
    lc = ML_CHUNK
    nc = s // lc
    dh = d // nh

    def body(q_ref, k_ref, v_ref, gc_ref, gr_ref, o_ref, z_ref, ng_ref,
             cell_ref, y_ref, cs_ref, ns_ref, ms_ref, mt_ref, c_sc, n_sc, m_sc):
        @pl.when(pl.program_id(0) == 0)
        def _():
            c_sc[...] = jnp.zeros_like(c_sc)
            n_sc[...] = jnp.zeros_like(n_sc)
            m_sc[...] = jnp.zeros_like(m_sc)

        gc = gc_ref[...]
        gr = gr_ref[...]
        lane = _iota((lc, LANES), 1)
        mt_acc = jnp.zeros((lc, LANES), F32)
        for h in range(nh):
            c_h = c_sc[h]
            n_h = n_sc[h, 0:1, :]
            m_prev = jnp.max(m_sc[h, 0:1, :], axis=1, keepdims=True)
            cs_ref[0, h] = c_h
            ns_ref[0, h] = n_sc[h]
            ms_ref[0, h] = m_sc[h]
            t = _cell_chunk(h, nh, q_ref, k_ref, v_ref, gc, gr, m_prev, c_h, n_h)
            sl = t["sl"]
            num = _mm(t["smat"], t["vh"]) + t["w_inter"] * t["qc"]
            cell_h = num / jnp.maximum(jnp.abs(t["den"]), jnp.exp(-t["m_t"]))
            mt_acc = jnp.where(lane == h, t["m_t"], mt_acc)
            kw = t["kh"] * t["w_state"]
            c_sc[h] = t["decay"] * c_h + _mm_tn(kw, t["vh"])
            n_sc[h] = _bcast8(t["decay"] * n_h + _colsum(kw))
            m_sc[h] = jnp.broadcast_to(t["m_new"], (SUBLANES, LANES))
            hg = _sigmoid(o_ref[:, sl]) * cell_h
            hn = hg * lax.rsqrt(jnp.mean(hg * hg, axis=1, keepdims=True) + EPS)
            z = z_ref[:, sl]
            cell_ref[:, sl] = cell_h
            y_ref[:, sl] = hn * ng_ref[:, sl] * z * _sigmoid(z)
        mt_ref[...] = mt_acc

    tok = pl.BlockSpec((lc, d), lambda c: (c, 0))
    return _call(
        body, comms, name="mlstm_cell_fwd", grid=(nc,),
        in_specs=[tok, tok, tok, pl.BlockSpec((lc, LANES), lambda c: (c, 0)), pl.BlockSpec((16, lc), lambda c: (0, c)),
                  pl.BlockSpec((lc, d), lambda c: (c, 3)), pl.BlockSpec((lc, d), lambda c: (c, 4)), _full((1, d))],
        out_specs=[tok, tok, pl.BlockSpec((1, nh, dh, dh), lambda c: (c, 0, 0, 0)),
                   pl.BlockSpec((1, nh, SUBLANES, dh), lambda c: (c, 0, 0, 0)),
                   pl.BlockSpec((1, nh, SUBLANES, LANES), lambda c: (c, 0, 0, 0)),
                   pl.BlockSpec((lc, LANES), lambda c: (c, 0))],
        out_shape=[jax.ShapeDtypeStruct((s, d), F32), jax.ShapeDtypeStruct((s, d), F32),
                   jax.ShapeDtypeStruct((nc, nh, dh, dh), F32), jax.ShapeDtypeStruct((nc, nh, SUBLANES, dh), F32),
                   jax.ShapeDtypeStruct((nc, nh, SUBLANES, LANES), F32), jax.ShapeDtypeStruct((s, LANES), F32)],
        scratch_shapes=[pltpu.VMEM((nh, dh, dh), F32), pltpu.VMEM((nh, SUBLANES, dh), F32),
                        pltpu.VMEM((nh, SUBLANES, LANES), F32)],
        args=(q, k, v, gcol, grow, u, u, ng))


def _out_fwd(x, y_rg, y_ml, gate, w_out_g, layer, tile, comms=None):
    s, d = x.shape
    nd, _, r, _ = w_out_g.shape

    def body(x_ref, yr_ref, ym_ref, g_ref, w_ref, xn_ref, y_ref):
        ycat = jnp.concatenate([yr_ref[...].astype(BF16), ym_ref[...].astype(BF16)], axis=1)
        acc = jnp.dot(ycat, w_ref[...].reshape(nd * r, d), preferred_element_type=F32)
        y_ref[...] = acc
        xn_ref[...] = x_ref[...] + g_ref[...] * acc

    tok = pl.BlockSpec((tile, d), lambda i: (i, 0))
    return _call(
        body, comms, name="out_proj_fwd", grid=(s // tile,),
        in_specs=[tok, tok, tok, _full((1, d)), pl.BlockSpec((nd, 1, r, d), lambda i: (0, layer, 0, 0))],
        out_specs=[tok, tok],
        out_shape=[jax.ShapeDtypeStruct((s, d), F32)] * 2,
        args=(x, y_rg, y_ml, gate, w_out_g))


def _loss_call(x, fg, target, tile):
    s, d = x.shape

    def body(x_ref, g_ref, t_ref, dx_ref, loss_ref, gg_ref):
        @pl.when(pl.program_id(0) == 0)
        def _():
            loss_ref[...] = jnp.zeros_like(loss_ref)
            gg_ref[...] = jnp.zeros_like(gg_ref)

        xv = x_ref[...]
        g = g_ref[...]
        rs = lax.rsqrt(jnp.mean(xv * xv, axis=1, keepdims=True) + EPS)
        xh = xv * rs
        e = xh * g - t_ref[...]
        loss_ref[...] += jnp.broadcast_to(_colsum(_rowsum(e * e)) * (0.5 / d), loss_ref.shape)
        dy = e * (1.0 / d)
        gg_ref[...] += _bcast8(_colsum(dy * xh))
        dxh = dy * g
        dx_ref[...] = rs * (dxh - xh * jnp.mean(dxh * xh, axis=1, keepdims=True))

    tok = pl.BlockSpec((tile, d), lambda i: (i, 0))
    return pl.pallas_call(
        body, name="final_norm_loss", grid=(s // tile,),
        in_specs=[tok, _full((1, d)), tok],
        out_specs=[tok, _full((SUBLANES, LANES)), _full((SUBLANES, d))],
        out_shape=[jax.ShapeDtypeStruct((s, d), F32), jax.ShapeDtypeStruct((SUBLANES, LANES), F32),
                   jax.ShapeDtypeStruct((SUBLANES, d), F32)],
        compiler_params=_params(1),
    )(x, fg, target)


def _out_bwd(dxo, gate, y, y_rg, y_ml, w_out_g, layer, tile, comms=None):
    s, d = dxo.shape
    nd, _, r, _ = w_out_g.shape

    def body(dx_ref, g_ref, y_ref, yr_ref, ym_ref, w_ref, dyr_ref, dym_ref, gw_ref, dg_ref):
        @pl.when(pl.program_id(0) == 0)
        def _():
            gw_ref[...] = jnp.zeros_like(gw_ref)
            dg_ref[...] = jnp.zeros_like(dg_ref)

        dxv = dx_ref[...]
        dg_ref[...] += _bcast8(_colsum(dxv * y_ref[...]))
        dyb = (dxv * g_ref[...]).astype(BF16)
        dycat = lax.dot_general(dyb, w_ref[...].reshape(nd * r, d), (((1,), (1,)), ((), ())), preferred_element_type=F32)
        dyr_ref[...] = dycat[:, 0:d]
        dym_ref[...] = dycat[:, d:2 * d]
        ycat = jnp.concatenate([yr_ref[...].astype(BF16), ym_ref[...].astype(BF16)], axis=1)
        gw_ref[...] += lax.dot_general(ycat, dyb, (((0,), (0,)), ((), ())), preferred_element_type=F32).reshape(nd, r, d)

    tok = pl.BlockSpec((tile, d), lambda i: (i, 0))
    return _call(
        body, comms, name="out_proj_bwd", grid=(s // tile,),
        in_specs=[tok, _full((1, d)), tok, tok, tok, pl.BlockSpec((nd, 1, r, d), lambda i: (0, layer, 0, 0))],
        out_specs=[tok, tok, _full((nd, r, d)), _full((SUBLANES, d))],
        out_shape=[jax.ShapeDtypeStruct((s, d), F32)] * 2 + [jax.ShapeDtypeStruct((nd, r, d), F32),
                                                             jax.ShapeDtypeStruct((SUBLANES, d), F32)],
        args=(dxo, gate, y, y_rg, y_ml, w_out_g))


def _ml_cell_bwd(dy_ml, u, cell, q, k, v, gcol, grow, mt, cs, ns, ms, ng, nh, comms=None):
    s, d = q.shape
    lc = ML_CHUNK
    nc = s // lc
    dh = d // nh

    def body(dy_ref, o_ref, z_ref, cell_ref, q_ref, k_ref, v_ref, gc_ref, gr_ref, mt_ref, cs_ref, ns_ref, ms_ref,
             ng_ref, dq_ref, dk_ref, dv_ref, dg_ref, do_ref, dz_ref, gng_ref, dc_sc, dn_sc):
        @pl.when(pl.program_id(0) == 0)
        def _():
            dc_sc[...] = jnp.zeros_like(dc_sc)
            dn_sc[...] = jnp.zeros_like(dn_sc)
            gng_ref[...] = jnp.zeros_like(gng_ref)

        gc = gc_ref[...]
        gr = gr_ref[...]
        mtv = mt_ref[...]
        lane = _iota((lc, LANES), 1)
        rowv = _iota((lc, 1), 0)
        dg_acc = jnp.zeros((lc, LANES), F32)
        for h in range(nh):
            c_h = cs_ref[0, h]
            n_h = ns_ref[0, h, 0:1, :]
            m_prev = jnp.max(ms_ref[0, h, 0:1, :], axis=1, keepdims=True)
            t = _cell_chunk(h, nh, q_ref, k_ref, v_ref, gc, gr, m_prev, c_h, n_h, m_t=_col(mtv, h))
            sl, qh, kh, vh = t["sl"], t["qh"], t["kh"], t["vh"]
            w_intra, w_inter, smat, w_state, decay = t["w_intra"], t["w_inter"], t["smat"], t["w_state"], t["decay"]
            cell_h = cell_ref[:, sl]
            o = o_ref[:, sl]
            z = z_ref[:, sl]
            dyv = dy_ref[:, sl]
            ngh = ng_ref[:, sl]
            so = _sigmoid(o)
            hg = so * cell_h
            rinv = lax.rsqrt(jnp.mean(hg * hg, axis=1, keepdims=True) + EPS)
            hn = hg * rinv
            sz = _sigmoid(z)
            dz_ref[:, sl] = (dyv * hn * ngh * (sz + z * sz * (1.0 - sz))).astype(BF16)
            dymid = dyv * z * sz
            gng_ref[:, sl] += _bcast8(_colsum(dymid * hn))
            dhn = dymid * ngh
            dhg = rinv * (dhn - hn * jnp.mean(dhn * hn, axis=1, keepdims=True))
            do_ref[:, sl] = (dhg * cell_h * so * (1.0 - so)).astype(BF16)
            dcell = dhg * so
            eneg = jnp.exp(-t["m_t"])
            aden = jnp.abs(t["den"])
            nst = jnp.maximum(aden, eneg)
            dnum = dcell / nst
            dden = jnp.where(aden > eneg, -_rowsum(cell_h * dcell) / nst * jnp.sign(t["den"]), 0.0)
            pmat = _mm_nt(dnum, vh) + dden
            damat = pmat * w_intra
            gmat = pmat * smat
            wdn = w_inter * dnum
            wdd = w_inter * dden
            dqh = _mm(damat, kh) + _mm_nt(wdn, c_h) + wdd * n_h
            dkh = _mm_tn(damat, qh)
            dvh = _mm_tn(smat, dnum)
            dw_inter = _rowsum(dnum * t["qc"]) + dden * t["qn"]
            dcn = dc_sc[h]
            dnn = dn_sc[h, 0:1, :]
            kw = kh * w_state
            dkw = _mm_nt(vh, dcn) + dnn
            dvh = dvh + _mm(kw, dcn)
            dkh = dkh + dkw * w_state
            dgst = _rowsum(dkw * kh) * w_state
            ddecay = _colsum(_rowsum(dcn * c_h)) + _rowsum(dnn * n_h)
            db_last = _colsum(dgst) + ddecay * decay
            rs_g = _rowsum(gmat)
            cs_g = _rowsum(gmat.T)
            db = rs_g - cs_g + dw_inter * w_inter - dgst + jnp.where(rowv == lc - 1, db_last, 0.0)
            dli = cs_g + dgst
            dc_sc[h] = decay * dcn + _mm_tn(qh, wdn)
            dn_sc[h] = _bcast8(decay * dnn + _colsum(qh * wdd))
            dq_ref[:, sl] = dqh
            dk_ref[:, sl] = dkh * (dh ** -0.5)
            dv_ref[:, sl] = dvh
            dg_acc = jnp.where(lane == h, dli, jnp.where(lane == 4 + h, db, dg_acc))
        dg_ref[...] = dg_acc

    rev = lambda c: nc - 1 - c
    tok = pl.BlockSpec((lc, d), lambda c: (rev(c), 0))
    g128 = pl.BlockSpec((lc, LANES), lambda c: (rev(c), 0))
    return _call(
        body, comms, name="mlstm_cell_bwd", grid=(nc,),
        in_specs=[tok, pl.BlockSpec((lc, d), lambda c: (rev(c), 3)), pl.BlockSpec((lc, d), lambda c: (rev(c), 4)),
                  tok, tok, tok, tok, g128, pl.BlockSpec((16, lc), lambda c: (0, rev(c))), g128,
                  pl.BlockSpec((1, nh, dh, dh), lambda c: (rev(c), 0, 0, 0)),
                  pl.BlockSpec((1, nh, SUBLANES, dh), lambda c: (rev(c), 0, 0, 0)),
                  pl.BlockSpec((1, nh, SUBLANES, LANES), lambda c: (rev(c), 0, 0, 0)), _full((1, d))],
        out_specs=[tok, tok, tok, g128, tok, tok, _full((SUBLANES, d))],
        out_shape=[jax.ShapeDtypeStruct((s, d), F32)] * 3 + [jax.ShapeDtypeStruct((s, LANES), F32)]
        + [jax.ShapeDtypeStruct((s, d), BF16)] * 2 + [jax.ShapeDtypeStruct((SUBLANES, d), F32)],
        scratch_shapes=[pltpu.VMEM((nh, dh, dh), F32), pltpu.VMEM((nh, SUBLANES, dh), F32)],
        args=(dy_ml, u, u, cell, q, k, v, gcol, grow, mt, cs, ns, ms, ng))


def _halo_spec(d, tile, nt, col):
    per = tile // SUBLANES
    return pl.BlockSpec((SUBLANES, d), lambda i: (jnp.maximum((nt - 1 - i) * per - 1, 0), col))


def _ml_pre_bwd(dq, dk, dv, dgates, gcol, u, q, k, v, conv_w, conv_b, w_q, w_k, w_v, wif_t, tile, comms=None):
    s, d = dq.shape
    nt = s // tile
    nh, dh, _ = w_q.shape

    def body(dq_ref, dk_ref, dv_ref, dg_ref, gc_ref, x_ref, halo_ref, q_ref, k_ref, v_ref, cw_ref, cb_ref,
             wq_ref, wk_ref, wv_ref, wift_ref,
             dx_ref, gwq_ref, gwk_ref, gwv_ref, gwif_ref, gbif_ref, gcw_ref, gcb_ref, next8):
        i = pl.program_id(0)

        @pl.when(i == 0)
        def _():
            next8[...] = jnp.zeros_like(next8)
            for ref in (gwq_ref, gwk_ref, gwv_ref, gwif_ref, gbif_ref, gcw_ref, gcb_ref):
                ref[...] = jnp.zeros_like(ref)

        x = x_ref[...]
        halo = halo_ref[...] * jnp.where(i < nt - 1, 1.0, 0.0)
        pre = _conv(x, halo, cw_ref) + cb_ref[...]
        sg = _sigmoid(pre)
        xc = pre * sg
        dgc = dg_ref[...]
        lane = _iota(dgc.shape, 1)
        utri = jnp.where(_iota((ML_CHUNK, ML_CHUNK), 0) <= _iota((ML_CHUNK, ML_CHUNK), 1), 1.0, 0.0)
        rcs = [_mm_hi(utri, dgc[c * ML_CHUNK:(c + 1) * ML_CHUNK, :]) for c in range(tile // ML_CHUNK)]
        rc = rcs[0] if len(rcs) == 1 else jnp.concatenate(rcs, axis=0)
        dgates_v = jnp.where(lane < 4, dgc, jnp.where(lane < 8, rc * (1.0 - jnp.exp(gc_ref[...])), 0.0))
        dgb = dgates_v.astype(BF16)
        gbif_ref[...] += jnp.broadcast_to(_colsum(dgates_v), gbif_ref.shape)
        ext = jnp.dot(dgb, wift_ref[...], preferred_element_type=F32)
        dqt = dq_ref[...] + ext[:, 0:d]
        dkt = dk_ref[...] + ext[:, d:2 * d]
        dvt = dv_ref[...] + ext[:, 2 * d:3 * d]
        gwif_ref[:, 0:d] += _mm_tn(dgb, q_ref[...])
        gwif_ref[:, d:2 * d] += _mm_tn(dgb, k_ref[...])
        gwif_ref[:, 2 * d:3 * d] += _mm_tn(dgb, v_ref[...])
        dxc_parts, dxv_parts = [], []
        for h in range(nh):
            sl = slice(h * dh, (h + 1) * dh)
            gwq_ref[h] += _mm_tn(xc[:, sl], dqt[:, sl])
            gwk_ref[h] += _mm_tn(xc[:, sl], dkt[:, sl])
            gwv_ref[h] += _mm_tn(x[:, sl], dvt[:, sl])
            dxc_parts.append(_mm_nt(dqt[:, sl], wq_ref[h]) + _mm_nt(dkt[:, sl], wk_ref[h]))
            dxv_parts.append(_mm_nt(dvt[:, sl], wv_ref[h]))
        dxc = jnp.concatenate(dxc_parts, axis=1)
        dxv = jnp.concatenate(dxv_parts, axis=1)
        dpre = dxc * (sg + pre * sg * (1.0 - sg))
        gcb_ref[...] += _bcast8(_colsum(dpre))
        for kk in range(CONV_WIDTH):
            gcw_ref[kk:kk + 1, :] += _colsum(dpre * _shift_down(x, CONV_WIDTH - 1 - kk, halo))
        dx_ref[...] = (dxv + _conv_bwd_x(dpre, next8[...], cw_ref)).astype(BF16)
        next8[...] = dpre[0:SUBLANES, :]

    rev = lambda i: nt - 1 - i
    tok = pl.BlockSpec((tile, d), lambda i: (rev(i), 0))
    g128 = pl.BlockSpec((tile, LANES), lambda i: (rev(i), 0))
    wsh = (nh, dh, dh)
    return _call(
        body, comms, name="mlstm_proj_bwd", grid=(nt,),
        in_specs=[tok, tok, tok, g128, g128, pl.BlockSpec((tile, d), lambda i: (rev(i), 2)), _halo_spec(d, tile, nt, 2),
                  tok, tok, tok, _full(conv_w.shape), _full((1, d)), _full(wsh), _full(wsh), _full(wsh),
                  _full(wif_t.shape)],
        out_specs=[tok, _full(wsh), _full(wsh), _full(wsh), _full((LANES, 3 * d)), _full((SUBLANES, LANES)),
                   _full((SUBLANES, d)), _full((SUBLANES, d))],
        out_shape=[jax.ShapeDtypeStruct((s, d), BF16)] + [jax.ShapeDtypeStruct(wsh, F32)] * 3
        + [jax.ShapeDtypeStruct((LANES, 3 * d), F32), jax.ShapeDtypeStruct((SUBLANES, LANES), F32),
           jax.ShapeDtypeStruct((SUBLANES, d), F32), jax.ShapeDtypeStruct((SUBLANES, d), F32)],
        scratch_shapes=[pltpu.VMEM((SUBLANES, d), F32)],
        args=(dq, dk, dv, dgates, gcol, u, u, q, k, v, conv_w, conv_b, w_q, w_k, w_v, wif_t))


def _rg_bwd(dy_rg, u, h_rg, conv_w, conv_b, w_a, b_a, w_x, b_x, lam, tile, comms=None):
    s, d = dy_rg.shape
    nt = s // tile
    nh, dh, _ = w_a.shape

    def body(dy_ref, x_ref, xhalo_ref, z_ref, h_ref, hhalo_ref, cw_ref, cb_ref, wa_ref, ba_ref, wx_ref, bx_ref, lam_ref,
             dx_ref, dz_ref, gwa_ref, gwx_ref, gba_ref, gbx_ref, glam_ref, gcw_ref, gcb_ref, next8, anext, dnext, dbuf):
        i = pl.program_id(0)

        @pl.when(i == 0)
        def _():
            for ref in (next8, anext, dnext, gwa_ref, gwx_ref, gba_ref, gbx_ref, glam_ref, gcw_ref, gcb_ref):
                ref[...] = jnp.zeros_like(ref)

        inner = jnp.where(i < nt - 1, 1.0, 0.0)
        x = x_ref[...]
        halo = xhalo_ref[...] * inner
        xc = _conv(x, halo, cw_ref) + cb_ref[...]
        r, ig, sp, a, beta = _rg_gates(xc, wa_ref, ba_ref, wx_ref, bx_ref, lam_ref)
        h = h_ref[...]
        row = _iota(h.shape, 0)
        hprev = jnp.where(row >= 1, pltpu.roll(h, 1, 0), hhalo_ref[SUBLANES - 1:SUBLANES, :] * inner)
        z = z_ref[...]
        sz = _sigmoid(z)
        dyv = dy_ref[...]
        dz_ref[...] = (dyv * h * (sz + z * sz * (1.0 - sz))).astype(BF16)
        a_up = jnp.where(row < tile - 1, pltpu.roll(a, tile - 1, 0), anext[0:1, :])
        _scan_into(a_up, dyv * z * sz, dnext[0:1, :], dbuf, True)
        delta = dbuf[...]
        anext[...] = a[0:SUBLANES, :]
        dnext[...] = delta[0:SUBLANES, :]
        dla = delta * hprev * a - delta * ig * xc * (a * a / beta)
        glam_ref[...] += _bcast8(_colsum(dla * r) * (RG_C * _sigmoid(-lam_ref[...])))
        dpa = dla * (-RG_C * sp) * r * (1.0 - r)
        dpx = delta * beta * xc * ig * (1.0 - ig)
        gba_ref[...] += _bcast8(_colsum(dpa))
        gbx_ref[...] += _bcast8(_colsum(dpx))
        parts = []
        for hh in range(nh):
            sl = slice(hh * dh, (hh + 1) * dh)
            gwa_ref[hh] += _mm_tn(xc[:, sl], dpa[:, sl])
            gwx_ref[hh] += _mm_tn(xc[:, sl], dpx[:, sl])
            parts.append(_mm_nt(dpa[:, sl], wa_ref[hh]) + _mm_nt(dpx[:, sl], wx_ref[hh]))
        dxc = delta * beta * ig + jnp.concatenate(parts, axis=1)
        gcb_ref[...] += _bcast8(_colsum(dxc))
        for kk in range(CONV_WIDTH):
            gcw_ref[kk:kk + 1, :] += _colsum(dxc * _shift_down(x, CONV_WIDTH - 1 - kk, halo))
        dx_ref[...] = _conv_bwd_x(dxc, next8[...], cw_ref).astype(BF16)
        next8[...] = dxc[0:SUBLANES, :]

    rev = lambda i: nt - 1 - i
    tok = pl.BlockSpec((tile, d), lambda i: (rev(i), 0))
    vec = _full((1, d))
    acc = _full((SUBLANES, d))
    wsh = (nh, dh, dh)
    return _call(
        body, comms, name="rglru_bwd", grid=(nt,),
        in_specs=[tok, tok, _halo_spec(d, tile, nt, 0), pl.BlockSpec((tile, d), lambda i: (rev(i), 1)), tok,
                  _halo_spec(d, tile, nt, 0), _full(conv_w.shape), vec, _full(wsh), vec, _full(wsh), vec, vec],
        out_specs=[tok, tok, _full(wsh), _full(wsh), acc, acc, acc, acc, acc],
        out_shape=[jax.ShapeDtypeStruct((s, d), BF16)] * 2 + [jax.ShapeDtypeStruct(wsh, F32)] * 2
        + [jax.ShapeDtypeStruct((SUBLANES, d), F32)] * 5,
        scratch_shapes=[pltpu.VMEM((SUBLANES, d), F32)] * 3 + [pltpu.VMEM((tile, d), F32)],
        args=(dy_rg, u, u, u, h_rg, h_rg, conv_w, conv_b, w_a, b_a, w_x, b_x, lam))


def _segments(d, w, n_pieces, n_slots):
    bounds = sorted({k * d for k in range(n_pieces + 1)} | {j * w for j in range(n_slots + 1)})
    return [(lo // d, lo % d, lo // w, lo % w, hi - lo) for lo, hi in zip(bounds[:-1], bounds[1:])]


def _in_bwd(pieces, x, dxo, ng, scale, w_in_g, layer, tile, comms=None, tiles=None, prev=None):
    s, d = x.shape
    nd, _, _, w = w_in_g.shape
    segs = _segments(d, w, len(pieces), nd)
    first, count = tiles or (0, s // tile)
    n_p = len(pieces)

    def body(*refs):
        p_refs = refs[:n_p]
        x_ref, dxo_ref, ng_ref, sc_ref, w_ref = refs[n_p:n_p + 5]
        dx_ref, dsc_ref, dsh_ref, gng_ref, wcat = refs[-5:]
        _join_columns(w_ref, wcat)

        @pl.when(pl.program_id(0) == 0)
        def _():
            for k, ref in enumerate((dsc_ref, dsh_ref, gng_ref)):
                ref[...] = jnp.zeros_like(ref) if prev is None else refs[n_p + 6 + k][...]

        du = jnp.concatenate([p[...] for p in p_refs], axis=1)
        dh = lax.dot_general(du, wcat[...], (((1,), (1,)), ((), ())), preferred_element_type=F32)
        xv = x_ref[...]
        g = ng_ref[...]
        rs = lax.rsqrt(jnp.mean(xv * xv, axis=1, keepdims=True) + EPS)
        xh = xv * rs
        dsh_ref[...] += _bcast8(_colsum(dh))
        dsc_ref[...] += _bcast8(_colsum(dh * xh * g))
        dhn = dh * (1.0 + sc_ref[...])
        gng_ref[...] += _bcast8(_colsum(dhn * xh))
        dxh = dhn * g
        dx_ref[...] = dxo_ref[...] + rs * (dxh - xh * jnp.mean(dxh * xh, axis=1, keepdims=True))

    tok = pl.BlockSpec((tile, d), lambda i: (i + first, 0))
    vec = _full((1, d))
    acc = _full((SUBLANES, d))
    more_specs = [] if prev is None else [pl.BlockSpec(memory_space=pl.ANY), acc, acc, acc]
    return _call(
        body, comms, name="in_proj_bwd_x", grid=(count,),
        in_specs=[tok] * n_p + [tok, tok, vec, vec, pl.BlockSpec((nd, 1, d, w), lambda i: (0, layer, 0, 0),
                                                               pipeline_mode=pl.Buffered(1))] + more_specs,
        out_specs=[tok, acc, acc, acc],
        out_shape=[jax.ShapeDtypeStruct((s, d), F32)] + [jax.ShapeDtypeStruct((SUBLANES, d), F32)] * 3,
        scratch_shapes=[pltpu.VMEM((d, nd * w), BF16)],
        args=(*pieces, x, dxo, ng, scale, w_in_g) + (() if prev is None else tuple(prev)),
        aliases={} if prev is None else {n_p + 5: 0})


def _in_bwd_w(pieces, hbf, w, slots, tile, comms=None):
    s, d = hbf.shape
    nd_all = len(pieces) * d // w
    segs = [sg for sg in _segments(d, w, len(pieces), nd_all) if sg[2] in slots]

    def body(*refs):
        p_refs = refs[:len(pieces)]
        h_ref, gw_ref = refs[len(pieces):]

        @pl.when(pl.program_id(0) == 0)
        def _():
            gw_ref[...] = jnp.zeros_like(gw_ref)

        hv = h_ref[...]
        for (kk, a, j, b, width) in segs:
            gw_ref[j - slots[0], :, b:b + width] += _mm_tn(hv, p_refs[kk][:, a:a + width])

    tok = pl.BlockSpec((tile, d), lambda i: (i, 0))
    return _call(
        body, comms, name="in_proj_bwd_w", grid=(s // tile,),
        in_specs=[tok] * len(pieces) + [tok],
        out_specs=[pl.BlockSpec((len(slots), d, w), lambda i: (0, 0, 0), pipeline_mode=pl.Buffered(1))],
        out_shape=[jax.ShapeDtypeStruct((len(slots), d, w), F32)],
        args=(*pieces, hbf))[0]


def _ada_bwd_w(cact_col, dmod, nd):
    d = cact_col.shape[0]
    w = dmod.shape[1] // nd

    def body(c_ref, m_ref, o_ref):
        o_ref[0] = c_ref[...] * m_ref[...]

    return pl.pallas_call(
        body, name="adaln_bwd_w", grid=(nd,),
        in_specs=[_full((d, 1)), pl.BlockSpec((1, w), lambda j: (0, j))],
        out_specs=pl.BlockSpec((1, d, w), lambda j: (j, 0, 0)),
        out_shape=jax.ShapeDtypeStruct((nd, d, w), F32),
        compiler_params=_params(1),
    )(cact_col, dmod)


def _exchange(arrs, gather, name):
    n = len(arrs)
    outs_shape = [jax.ShapeDtypeStruct((N_DEV,) + a.shape if gather else a.shape, a.dtype) for a in arrs]

    def body(*refs):
        ins, outs = refs[:n], refs[n:2 * n]
        send_sems, recv_sems, local_sems = refs[2 * n:]
        x, y, c = (lax.axis_index(ax) for ax in MESH_AXES)
        me = 4 * x + 2 * y + c
        sends, recvs = [], []
        for flip in range(1, N_DEV):
            px = x ^ ((flip >> 2) & 1)
            py = y ^ ((flip >> 1) & 1)
            pc = c ^ (flip & 1)
            peer = 4 * px + 2 * py + pc
            for kk in range(n):
                sem = kk * (N_DEV - 1) + flip - 1
                src = ins[kk] if gather else ins[kk].at[peer]
                sends.append(pltpu.make_async_remote_copy(
                    src_ref=src, dst_ref=outs[kk].at[me], send_sem=send_sems.at[sem], recv_sem=recv_sems.at[sem],
                    device_id=(px, py, pc), device_id_type=pl.DeviceIdType.MESH))
                recvs.append(pltpu.make_async_remote_copy(
                    src_ref=src, dst_ref=outs[kk].at[peer], send_sem=send_sems.at[sem], recv_sem=recv_sems.at[sem],
                    device_id=(px, py, pc), device_id_type=pl.DeviceIdType.MESH))
        for cp in sends:
            cp.start()
        local = [pltpu.make_async_copy(ins[kk] if gather else ins[kk].at[me], outs[kk].at[me], local_sems.at[kk])
                 for kk in range(n)]
        for cp in local:
            cp.start()
        for cp in recvs:
            cp.wait_recv()
        for cp in sends:
            cp.wait_send()
        for cp in local:
            cp.wait()

    return pl.pallas_call(
        body, name=name,
        in_specs=[pl.BlockSpec(memory_space=pl.ANY)] * n,
        out_specs=[pl.BlockSpec(memory_space=pl.ANY)] * n,
        out_shape=outs_shape,
        scratch_shapes=[pltpu.SemaphoreType.DMA((n * (N_DEV - 1),)), pltpu.SemaphoreType.DMA((n * (N_DEV - 1),)),
                        pltpu.SemaphoreType.DMA((n,))],
    )(*arrs)


def _mesh_place():
    x, y, c = (lax.axis_index(ax) for ax in MESH_AXES)
    return x, y, c, (x, y, 1 - c), [(1 - x, y), (x, 1 - y), (1 - x, 1 - y)]


def _remote(src, dst, send_sems, recv_sems, sem, to):
    return pltpu.make_async_remote_copy(src_ref=src, dst_ref=dst, send_sem=send_sems.at[sem], recv_sem=recv_sems.at[sem],
                                        device_id=to, device_id_type=pl.DeviceIdType.MESH)


def _gather_two_level(arrs, name):
    n = len(arrs)
    per = N_DEV - 1

    def body(*refs):
        ins, outs = refs[:n], refs[n:2 * n]
        send_sems, recv_sems, local_sems = refs[2 * n:]
        x, y, c, sibling, chips = _mesh_place()

        def copy(kk, j, block, to, src=None):
            dst = outs[kk].at[4 * block[0] + 2 * block[1] + block[2]]
            return _remote(dst if src is None else src, dst, send_sems, recv_sems, kk * per + j, to)

        me = (x, y, c)
        local = [pltpu.make_async_copy(ins[kk], outs[kk].at[4 * x + 2 * y + c], local_sems.at[kk]) for kk in range(n)]
        first = []
        for j, chip in enumerate(chips):
            first += [copy(kk, 1 + j, me, (*chip, c), src=ins[kk]) for kk in range(n)]
        first += [copy(kk, 0, me, sibling, src=ins[kk]) for kk in range(n)]
        for cp in first + local:
            cp.start()
        passed = []
        for j, chip in enumerate(chips):
            for kk in range(n):
                copy(kk, 1 + j, (*chip, c), me).wait_recv()
                passed.append(copy(kk, 4 + j, (*chip, c), sibling))
                passed[-1].start()
        for kk in range(n):
            copy(kk, 0, sibling, me).wait_recv()
        for j, chip in enumerate(chips):
            for kk in range(n):
                copy(kk, 4 + j, (*chip, 1 - c), me).wait_recv()
        for cp in first + passed:
            cp.wait_send()
        for cp in local:
            cp.wait()

    return pl.pallas_call(
        body, name=name,
        in_specs=[pl.BlockSpec(memory_space=pl.ANY)] * n, out_specs=[pl.BlockSpec(memory_space=pl.ANY)] * n,
        out_shape=[jax.ShapeDtypeStruct((N_DEV,) + a.shape, a.dtype) for a in arrs],
        scratch_shapes=[pltpu.SemaphoreType.DMA((n * per,)), pltpu.SemaphoreType.DMA((n * per,)),
                        pltpu.SemaphoreType.DMA((n,))],
    )(*arrs)


N_CHIPS = N_DEV // 2


def _core_swap(arrs, name):
    n = len(arrs)

    def body(*refs):
        ins, outs = refs[:n], refs[n:2 * n]
        send_sems, recv_sems = refs[2 * n:]
        _, _, c, sibling, _ = _mesh_place()
        copies = [_remote(ins[kk].at[2 * q + (1 - c)], outs[kk].at[q], send_sems, recv_sems, kk * N_CHIPS + q, sibling)
                  for q in range(N_CHIPS) for kk in range(n)]
        for cp in copies:
            cp.start()
        for cp in copies:
            cp.wait_recv()
        for cp in copies:
            cp.wait_send()

    return pl.pallas_call(
        body, name=name,
        in_specs=[pl.BlockSpec(memory_space=pl.ANY)] * n, out_specs=[pl.BlockSpec(memory_space=pl.ANY)] * n,
        out_shape=[jax.ShapeDtypeStruct((N_CHIPS,) + a.shape[1:], a.dtype) for a in arrs],
        scratch_shapes=[pltpu.SemaphoreType.DMA((n * N_CHIPS,)), pltpu.SemaphoreType.DMA((n * N_CHIPS,))],
    )(*arrs)


def _pair_sum(a, other, parity, name):
    _, r, c = a.shape
    tr = _row_tile(r, c, 3)

    def body(p_ref, a_ref, o_ref, s_ref):
        s_ref[...] = (a_ref[...] + o_ref[...]).astype(BF16)

    return pl.pallas_call(
        body, name=name,
        grid_spec=pltpu.PrefetchScalarGridSpec(
            num_scalar_prefetch=1, grid=(N_CHIPS, r // tr),
            in_specs=[pl.BlockSpec((1, tr, c), lambda q, i, p: (2 * q + p[0], i, 0)),
                      pl.BlockSpec((1, tr, c), lambda q, i, p: (q, i, 0))],
            out_specs=pl.BlockSpec((1, tr, c), lambda q, i, p: (q, i, 0))),
        out_shape=jax.ShapeDtypeStruct((N_CHIPS, r, c), BF16),
        compiler_params=_params(2),
    )(parity, a, other)


def _chip_swap(arrs, name):
    n = len(arrs)
    per = N_CHIPS - 1

    def body(*refs):
        ins, outs = refs[:n], refs[n:2 * n]
        send_sems, recv_sems, local_sems = refs[2 * n:]
        x, y, c, _, chips = _mesh_place()
        mine = 2 * x + y
        sends = [_remote(ins[kk].at[2 * chip[0] + chip[1]], outs[kk].at[mine], send_sems, recv_sems, kk * per + j, (*chip, c))
                 for j, chip in enumerate(chips) for kk in range(n)]
        recvs = [_remote(ins[kk].at[mine], outs[kk].at[2 * chip[0] + chip[1]], send_sems, recv_sems, kk * per + j, (*chip, c))
                 for j, chip in enumerate(chips) for kk in range(n)]
        local = [pltpu.make_async_copy(ins[kk].at[mine], outs[kk].at[mine], local_sems.at[kk]) for kk in range(n)]
        for cp in sends + local:
            cp.start()
        for cp in recvs:
            cp.wait_recv()
        for cp in sends:
            cp.wait_send()
        for cp in local:
            cp.wait()

    return pl.pallas_call(
        body, name=name,
        in_specs=[pl.BlockSpec(memory_space=pl.ANY)] * n, out_specs=[pl.BlockSpec(memory_space=pl.ANY)] * n,
        out_shape=[jax.ShapeDtypeStruct(a.shape, a.dtype) for a in arrs],
        scratch_shapes=[pltpu.SemaphoreType.DMA((n * per,)), pltpu.SemaphoreType.DMA((n * per,)),
                        pltpu.SemaphoreType.DMA((n,))],
    )(*arrs)


def _adam_math(w, g, m, v):
    m = ADAM_B1 * m + (1.0 - ADAM_B1) * g
    v = ADAM_B2 * v + (1.0 - ADAM_B2) * (g * g)
    m_hat = m / (1.0 - ADAM_B1 ** ADAM_STEP)
    v_hat = v / (1.0 - ADAM_B2 ** ADAM_STEP)
    delta = -ADAM_LR * (m_hat / (jnp.sqrt(v_hat) + ADAM_EPS) + ADAM_WD * w)
    return delta, m, v


def _sum_devices(r_ref):
    acc = r_ref[0].astype(F32)
    for p in range(1, r_ref.shape[0]):
        acc = acc + r_ref[p].astype(F32)
    return acc


def _row_tile(rows, cols, n_bufs):
    budget = 24 * 1024 * 1024 // (n_bufs * 2 * cols * 4)
    t = rows
    while t > budget and t % 2 == 0 and (t // 2) % SUBLANES == 0:
        t //= 2
    return t


def _reduce_adam(recvs, w, m, v, name):
    nl, r, c = w.shape
    n_part = recvs[0].shape[0]
    tr = _row_tile(r, c, n_part * nl + 7)
    nt = r // tr

    def body(*refs):
        r_refs = refs[:nl]
        w_ref, m_ref, v_ref, g_ref, d_ref, mo_ref, vo_ref = refs[nl:]
        layer = pl.program_id(0)
        g = _sum_devices(r_refs[0])
        for ll in range(1, nl):
            g = jnp.where(layer == ll, _sum_devices(r_refs[ll]), g)
        delta, m2, v2 = _adam_math(w_ref[0], g, m_ref[0], v_ref[0])
        g_ref[0] = g
        d_ref[0] = delta
        mo_ref[0] = m2
        vo_ref[0] = v2

    def rspec(ll):
        return pl.BlockSpec((n_part, tr, c), lambda l, i: (0, jnp.where(l == ll, i, jnp.where(l < ll, 0, nt - 1)), 0))

    blk = pl.BlockSpec((1, tr, c), lambda l, i: (l, i, 0))
    return pl.pallas_call(
        body, name=name, grid=(nl, nt),
        in_specs=[rspec(ll) for ll in range(nl)] + [blk, blk, blk],
        out_specs=[blk] * 4,
        out_shape=[jax.ShapeDtypeStruct((nl, r, c), F32)] * 4,
        compiler_params=_params(2),
    )(*recvs, w, m, v)


def _sum8(recv, name):
    _, r, c = recv.shape

    def body(r_ref, o_ref):
        o_ref[...] = _sum_devices(r_ref)

    return pl.pallas_call(
        body, name=name, grid=(1,),
        in_specs=[_full(recv.shape)], out_specs=_full((r, c)),
        out_shape=jax.ShapeDtypeStruct((r, c), F32), compiler_params=_params(1),
    )(recv)


def _adam_call(w, g, m, v, name):
    r, c = w.shape

    def body(w_ref, g_ref, m_ref, v_ref, d_ref, mo_ref, vo_ref):
        delta, m2, v2 = _adam_math(w_ref[...], g_ref[...], m_ref[...], v_ref[...])
        d_ref[...] = delta
        mo_ref[...] = m2
        vo_ref[...] = v2

    return pl.pallas_call(
        body, name=name, grid=(1,),
        in_specs=[_full((r, c))] * 4, out_specs=[_full((r, c))] * 3,
        out_shape=[jax.ShapeDtypeStruct((r, c), F32)] * 3, compiler_params=_params(1),
    )(w, g, m, v)


def _tile_for(s, want):
    return min(want, s)


def _local_step_whole(x, c, target, wts):
    s, d = x.shape
    nl = wts["w_in_g"].shape[1]
    nd = wts["w_in_g"].shape[0]
    nh_ml = wts["w_qkv"].shape[2]
    t_big = _tile_for(s, 512)
    t_mid = _tile_for(s, 256)

    mod, cact = _mod_call(c, wts["w_ada_g"], wts["b_ada"])
    row = lambda a: a.reshape(1, -1)
    saved = []
    xl = x
    for l in range(nl):
        shift, scale, gate = (row(mod[l, kk * d:(kk + 1) * d]) for kk in range(3))
        u, hbf = _in_fwd(xl, row(wts["norm_g"][l]), scale, shift, wts["w_in_g"], l, t_mid)
        h_rg, y_rg = _rg_fwd(u, d, wts["rg_conv_w"][l], row(wts["rg_conv_b"][l]), wts["rg_w_a_bf"][l],
                             row(wts["rg_b_a"][l]), wts["rg_w_x_bf"][l], row(wts["rg_b_x"][l]),
                             row(wts["rg_lambda"][l]), t_mid)
        q, k, v, gcol = _ml_pre(u, d, wts["ml_conv_w"][l], row(wts["ml_conv_b"][l]), wts["w_qkv"][l, 0],
                                wts["w_qkv"][l, 1], wts["w_qkv"][l, 2], wts["wif_pad"][l], wts["bif_pad"][l], t_mid)
        grow = gcol[:, 0:16].T
        cell, y_ml, cs, ns, ms, mt = _ml_cell_fwd(q, k, v, gcol, grow, u, row(wts["ml_norm_g"][l]), nh_ml)
        x_new, y = _out_fwd(xl, y_rg, y_ml, gate, wts["w_out_g"], l, t_big)
        saved.append(dict(x=xl, u=u, hbf=hbf, h_rg=h_rg, y_rg=y_rg, q=q, k=k, v=v, gcol=gcol, grow=grow, cell=cell,
                          y_ml=y_ml, cs=cs, ns=ns, ms=ms, mt=mt, y=y, scale=scale, gate=gate))
        xl = x_new

    dx, loss_p, g_final = _loss_call(xl, row(wts["final_g"]), target, t_big)
    grads = [None] * nl
    cact_col = cact[0].reshape(d, 1)
    for l in reversed(range(nl)):
        sv = saved[l]
        dy_rg, dy_ml, gw_out, dgate = _out_bwd(dx, sv["gate"], sv["y"], sv["y_rg"], sv["y_ml"], wts["w_out_g"], l, t_big)
        dq, dk, dv, dgates, d_mlo, d_mlz, g_mlng = _ml_cell_bwd(
            dy_ml, sv["u"], sv["cell"], sv["q"], sv["k"], sv["v"], sv["gcol"], sv["grow"], sv["mt"], sv["cs"],
            sv["ns"], sv["ms"], row(wts["ml_norm_g"][l]), nh_ml)
        d_mlx, g_wq, g_wk, g_wv, g_wift, g_bif, g_mlcw, g_mlcb = _ml_pre_bwd(
            dq, dk, dv, dgates, sv["gcol"], sv["u"], sv["q"], sv["k"], sv["v"], wts["ml_conv_w"][l],
            row(wts["ml_conv_b"][l]), wts["w_qkv"][l, 0], wts["w_qkv"][l, 1], wts["w_qkv"][l, 2], wts["wift_pad"][l], t_mid)
        d_rgx, d_rgz, g_wa, g_wx, g_ba, g_bx, g_lam, g_rgcw, g_rgcb = _rg_bwd(
            dy_rg, sv["u"], sv["h_rg"], wts["rg_conv_w"][l], row(wts["rg_conv_b"][l]), wts["rg_w_a_bf"][l],
            row(wts["rg_b_a"][l]), wts["rg_w_x_bf"][l], row(wts["rg_b_x"][l]), row(wts["rg_lambda"][l]), t_mid)
        pieces = [d_rgx, d_rgz, d_mlx, d_mlo, d_mlz]
        dx, dscale, dshift, g_ng = _in_bwd(pieces, sv["x"], dx, row(wts["norm_g"][l]), sv["scale"], wts["w_in_g"], l, t_mid)
        half = nd // 2
        w_cols = wts["w_in_g"].shape[3]
        gw_in = jnp.concatenate([_in_bwd_w(pieces, sv["hbf"], w_cols, tuple(range(0, half)), t_big),
                                 _in_bwd_w(pieces, sv["hbf"], w_cols, tuple(range(half, nd)), t_big)], axis=0)
        dmod = jnp.concatenate([dshift[0:1], dscale[0:1], dgate[0:1]], axis=1)
        gw_ada = _ada_bwd_w(cact_col, dmod, nd)
        grads[l] = dict(w_ada=gw_ada, w_in=gw_in, w_out=gw_out, w_qkv=jnp.stack([g_wq, g_wk, g_wv]),
                        rg_conv_w=g_rgcw[0:CONV_WIDTH], ml_conv_w=g_mlcw[0:CONV_WIDTH], wif_t=g_wift[0:8],
                        norm_g=g_ng[0], b_ada=dmod[0], rg_conv_b=g_rgcb[0], rg_w_a=g_wa, rg_b_a=g_ba[0], rg_w_x=g_wx,
                        rg_b_x=g_bx[0], rg_lambda=g_lam[0], ml_conv_b=g_mlcb[0], ml_b_if=g_bif[0, 0:8],
                        ml_norm_g=g_mlng[0])
    return loss_p[0, 0], dx, grads, g_final[0]


REPLICATED = ("norm_g", "b_ada", "rg_conv_b", "rg_w_a", "rg_b_a", "rg_w_x", "rg_b_x", "rg_lambda", "ml_conv_b",
              "ml_b_if", "ml_norm_g", "final_g")
ROW_ALIGN = N_DEV * SUBLANES


def _to_rows(a):
    flat = a.reshape(-1)
    pad = (-flat.shape[0]) % LANES
    return jnp.pad(flat, (0, pad)).reshape(-1, LANES)


def _pack(arrays):
    rows = jnp.concatenate([_to_rows(a) for a in arrays], axis=0)
    return jnp.pad(rows, ((0, (-rows.shape[0]) % ROW_ALIGN), (0, 0)))


def _unpack(rows, like):
    out, at = [], 0
    for a in like:
        n = -(-a.size // LANES)
        out.append(rows[at:at + n].reshape(-1)[:a.size].reshape(a.shape))
        at += n
    return out


def _small_pack(rg_conv_w, ml_conv_w, ml_w_if):
    nl = rg_conv_w.shape[0]
    wif_t = jnp.swapaxes(ml_w_if, 1, 2).reshape(nl, -1, LANES)
    return jnp.concatenate([rg_conv_w, ml_conv_w, wif_t], axis=1)


def _small_unpack(p, if_rows):
    nl = p.shape[0]
    rg_cw = p[:, 0:CONV_WIDTH]
    ml_cw = p[:, CONV_WIDTH:2 * CONV_WIDTH]
    wif = jnp.swapaxes(p[:, 2 * CONV_WIDTH:].reshape(nl, 8, if_rows), 1, 2)
    return rg_cw, ml_cw, wif


def _assemble_weights(big, small, rep):
    w_ada_g, w_in_g, w_out_g, qkv_g = big
    nd, nl = small.shape[0], small.shape[1]
    d = w_in_g.shape[2]
    dh = qkv_g.shape[3]
    nh = d // dh
    rsh = qkv_g.shape[2] // (3 * nh)
    w_qkv = qkv_g.reshape(nd, nl, 3, nh, rsh, dh).transpose(1, 2, 3, 0, 4, 5).reshape(nl, 3, nh, nd * rsh, dh)
    cw = small[:, :, 0:2 * CONV_WIDTH].reshape(nd, nl, 2, CONV_WIDTH, LANES).transpose(1, 2, 3, 0, 4)
    cw = cw.reshape(nl, 2, CONV_WIDTH, nd * LANES)
    if_rows = (small.shape[2] - 2 * CONV_WIDTH) * LANES // 8
    wif_t = small[:, :, 2 * CONV_WIDTH:].reshape(nd, nl, 8, if_rows).transpose(1, 2, 0, 3).reshape(nl, 8, nd * if_rows)
    wift_pad = jnp.pad(wif_t, ((0, 0), (0, LANES - 8), (0, 0))).astype(BF16)
    wif_pad = jnp.swapaxes(wift_pad, 1, 2)
    bif_pad = jnp.pad(rep["ml_b_if"], ((0, 0), (0, LANES - 8))).reshape(nl, 1, LANES)
    wts = dict(rep)
    wts.update(w_ada_g=w_ada_g, w_in_g=w_in_g, w_out_g=w_out_g, w_qkv=w_qkv, rg_conv_w=cw[:, 0], ml_conv_w=cw[:, 1],
               wif_pad=wif_pad, wift_pad=wift_pad, bif_pad=bif_pad, rg_w_a_bf=rep["rg_w_a"].astype(BF16),
               rg_w_x_bf=rep["rg_w_x"].astype(BF16))
    return wts


def _qkv_slots(g_qkv, nd):
    three, nh, dh, _ = g_qkv.shape
    return g_qkv.reshape(three, nh, nd, dh // nd, dh).transpose(2, 0, 1, 3, 4).reshape(nd, three * nh * (dh // nd), dh)


def _small_slots(g):
    nd = N_DEV
    cw = jnp.stack([g["rg_conv_w"], g["ml_conv_w"]]).reshape(2, CONV_WIDTH, nd, LANES).transpose(2, 0, 1, 3)
    cw = cw.reshape(nd, 2 * CONV_WIDTH, LANES)
    wif = g["wif_t"].reshape(8, nd, -1).transpose(1, 0, 2).reshape(nd, -1, LANES)
    return jnp.concatenate([cw, wif], axis=1)


def _kernel_unhosted(x, c, norm_g, w_ada, b_ada, w_in, rg_conv_w, rg_conv_b, rg_w_a, rg_b_a, rg_w_x, rg_b_x, rg_lambda, ml_conv_w, ml_conv_b, ml_w_q, ml_w_k, ml_w_v, ml_w_if, ml_b_if, ml_norm_g, w_out, final_g, loss_target, m_norm_g, m_w_ada, m_b_ada, m_w_in, m_rg_conv_w, m_rg_conv_b, m_rg_w_a, m_rg_b_a, m_rg_w_x, m_rg_b_x, m_rg_lambda, m_ml_conv_w, m_ml_conv_b, m_ml_w_q, m_ml_w_k, m_ml_w_v, m_ml_w_if, m_ml_b_if, m_ml_norm_g, m_w_out, m_final_g, v_norm_g, v_w_ada, v_b_ada, v_w_in, v_rg_conv_w, v_rg_conv_b, v_rg_w_a, v_rg_b_a, v_rg_w_x, v_rg_b_x, v_rg_lambda, v_ml_conv_w, v_ml_conv_b, v_ml_w_q, v_ml_w_k, v_ml_w_v, v_ml_w_if, v_ml_b_if, v_ml_norm_g, v_w_out, v_final_g):
    given = dict(locals())
    nl = w_in.shape[0]
    rep = {n: given[n] for n in REPLICATED}

    def qkv_shard(prefix):
        return jnp.stack([given[prefix + "ml_w_q"], given[prefix + "ml_w_k"], given[prefix + "ml_w_v"]], axis=1).reshape(
            nl, -1, ml_w_q.shape[-1])

    *big, small = _gather_two_level(
        [w_ada.astype(BF16), w_in.astype(BF16), w_out.astype(BF16), qkv_shard("").astype(BF16),
         _small_pack(rg_conv_w, ml_conv_w, ml_w_if)], "gather_weights")
    wts = _assemble_weights(big, small, rep)

    loss_p, grad_x, grads, g_final = _local_step(x[0], c, loss_target[0], wts)
    loss = lax.psum(loss_p, MESH_AXES)

    keys = ("w_ada", "w_in", "w_out", "w_qkv", "small")
    parity = lax.axis_index("c").astype(jnp.int32).reshape(1)
    recv = []
    for l in range(nl):
        g = grads[l]
        parts = [g["w_ada"], g["w_in"], g["w_out"], _qkv_slots(g["w_qkv"], N_DEV), _small_slots(g)]
        other = _core_swap(parts, "core_swap_layer%d" % l)
        sums = [_pair_sum(a, o, parity, "pair_sum_%s_layer%d" % (key, l)) for key, a, o in zip(keys, parts, other)]
        recv.append(_chip_swap(sums, "chip_swap_layer%d" % l))
    shard = {"": dict(w_ada=w_ada, w_in=w_in, w_out=w_out, w_qkv=qkv_shard(""),
                      small=_small_pack(rg_conv_w, ml_conv_w, ml_w_if))}
    for p in ("m_", "v_"):
        shard[p] = dict(w_ada=given[p + "w_ada"], w_in=given[p + "w_in"], w_out=given[p + "w_out"], w_qkv=qkv_shard(p),
                        small=_small_pack(given[p + "rg_conv_w"], given[p + "ml_conv_w"], given[p + "ml_w_if"]))
    res = {}
    for ki, key in enumerate(keys):
        res[key] = _reduce_adam([recv[l][ki] for l in range(nl)], shard[""][key], shard["m_"][key], shard["v_"][key],
                                "reduce_adam_" + key)

    rep_g = dict(final_g=g_final)
    for n in REPLICATED[:-1]:
        rep_g[n] = jnp.stack([grads[l][n] for l in range(nl)])
    pack_g = _pack([rep_g[n] for n in REPLICATED])
    rows = pack_g.shape[0] // N_DEV
    mine = _sum8(_exchange([pack_g.reshape(N_DEV, rows, LANES)], False, "reduce_scatter_replicated")[0], "sum_replicated")
    g_rep = _exchange([mine], True, "gather_replicated")[0].reshape(N_DEV * rows, LANES)
    rep_like = [rep[n] for n in REPLICATED]
    d_rep, m_rep, v_rep = _adam_call(_pack(rep_like), g_rep, _pack([given["m_" + n] for n in REPLICATED]),
                                     _pack([given["v_" + n] for n in REPLICATED]), "adam_replicated")
    rep_out = [dict(zip(REPLICATED, _unpack(a, rep_like))) for a in (g_rep, d_rep, m_rep, v_rep)]

    if_rows = ml_w_if.shape[1]
    order = ("norm_g", "w_ada", "b_ada", "w_in", "rg_conv_w", "rg_conv_b", "rg_w_a", "rg_b_a", "rg_w_x", "rg_b_x",
             "rg_lambda", "ml_conv_w", "ml_conv_b", "ml_w_q", "ml_w_k", "ml_w_v", "ml_w_if", "ml_b_if", "ml_norm_g",
             "w_out", "final_g")
    outs = [loss, grad_x[None]]
    for kind in range(4):
        qkv = res["w_qkv"][kind].reshape((nl, 3) + ml_w_q.shape[1:])
        rg_cw, ml_cw, wif = _small_unpack(res["small"][kind], if_rows)
        sharded = dict(w_ada=res["w_ada"][kind], w_in=res["w_in"][kind], w_out=res["w_out"][kind], ml_w_q=qkv[:, 0],
                       ml_w_k=qkv[:, 1], ml_w_v=qkv[:, 2], rg_conv_w=rg_cw, ml_conv_w=ml_cw, ml_w_if=wif)
        for n in order:
            outs.append(sharded[n] if n in sharded else rep_out[kind][n])
    return tuple(outs)


def _slot(block):
    return 4 * block[0] + 2 * block[1] + block[2]


def _dma_sems(*counts):
    return [pltpu.SemaphoreType.DMA((n,)) for n in counts]


def _start_all(copies):
    for cp in copies:
        cp.start()


def _gather_ici_comm(arrs):
    n = len(arrs)

    def copies(ins, outs, sems):
        send_sems, recv_sems, local_sems = sems
        x, y, c, sibling, chips = _mesh_place()
        me = (x, y, c)
        peers = [(*chip, c) for chip in chips] + [sibling]
        local = [pltpu.make_async_copy(ins[kk], outs[kk].at[_slot(me)], local_sems.at[kk]) for kk in range(n)]
        sends = [_remote(ins[kk], outs[kk].at[_slot(me)], send_sems, recv_sems, kk * 4 + j, peer)
                 for j, peer in enumerate(peers) for kk in range(n)]
        recvs = [_remote(ins[kk], outs[kk].at[_slot(peer)], send_sems, recv_sems, kk * 4 + j, peer)
                 for j, peer in enumerate(peers) for kk in range(n)]
        return local, sends, recvs

    def start(ins, outs, sems):
        local, sends, _ = copies(ins, outs, sems)
        _start_all(sends + local)

    def finish(ins, outs, sems):
        local, sends, recvs = copies(ins, outs, sems)
        for cp in recvs:
            cp.wait_recv()
        for cp in sends:
            cp.wait_send()
        for cp in local:
            cp.wait()

    return _Comm(arrs, [jax.ShapeDtypeStruct((N_DEV,) + a.shape, a.dtype) for a in arrs], _dma_sems(4 * n, 4 * n, n),
                 start, finish)


def _gather_fwd_comm(bufs):
    n = len(bufs)

    def copies(ins, outs, sems):
        send_sems, recv_sems = sems
        _, _, c, sibling, chips = _mesh_place()
        sends = [_remote(ins[kk].at[_slot((*chip, c))], outs[kk].at[_slot((*chip, c))], send_sems, recv_sems, kk * 3 + j, sibling)
                 for j, chip in enumerate(chips) for kk in range(n)]
        recvs = [_remote(ins[kk].at[_slot((*chip, c))], outs[kk].at[_slot((*chip, 1 - c))], send_sems, recv_sems, kk * 3 + j, sibling)
                 for j, chip in enumerate(chips) for kk in range(n)]
        return sends, recvs

    def start(ins, outs, sems):
        _start_all(copies(ins, outs, sems)[0])

    def finish(ins, outs, sems):
        sends, recvs = copies(ins, outs, sems)
        for cp in recvs:
            cp.wait_recv()
        for cp in sends:
            cp.wait_send()

    return _Comm(bufs, [jax.ShapeDtypeStruct(a.shape, a.dtype) for a in bufs], _dma_sems(3 * n, 3 * n), start, finish,
                 aliases=[(i, i) for i in range(n)])


def _core_swap_comm(arrs):
    n = len(arrs)

    def copies(ins, outs, sems):
        send_sems, recv_sems = sems
        _, _, c, sibling, _ = _mesh_place()
        return [_remote(ins[kk].at[2 * q + (1 - c)], outs[kk].at[q], send_sems, recv_sems, kk * N_CHIPS + q, sibling)
                for q in range(N_CHIPS) for kk in range(n)]

    def start(ins, outs, sems):
        _start_all(copies(ins, outs, sems))

    def finish(ins, outs, sems):
        cps = copies(ins, outs, sems)
        for cp in cps:
            cp.wait_recv()
        for cp in cps:
            cp.wait_send()

    return _Comm(arrs, [jax.ShapeDtypeStruct((N_CHIPS,) + a.shape[1:], a.dtype) for a in arrs],
                 _dma_sems(N_CHIPS * n, N_CHIPS * n), start, finish)


def _chip_swap_comm(arrs):
    n = len(arrs)
    per = N_CHIPS - 1

    def copies(ins, outs, sems):
        send_sems, recv_sems, local_sems = sems
        x, y, c, _, chips = _mesh_place()
        mine = 2 * x + y
        sends = [_remote(ins[kk].at[2 * chip[0] + chip[1]], outs[kk].at[mine], send_sems, recv_sems, kk * per + j, (*chip, c))
                 for j, chip in enumerate(chips) for kk in range(n)]
        recvs = [_remote(ins[kk].at[mine], outs[kk].at[2 * chip[0] + chip[1]], send_sems, recv_sems, kk * per + j, (*chip, c))
                 for j, chip in enumerate(chips) for kk in range(n)]
        local = [pltpu.make_async_copy(ins[kk].at[mine], outs[kk].at[mine], local_sems.at[kk]) for kk in range(n)]
        return local, sends, recvs

    def start(ins, outs, sems):
        local, sends, _ = copies(ins, outs, sems)
        _start_all(sends + local)

    def finish(ins, outs, sems):
        local, sends, recvs = copies(ins, outs, sems)
        for cp in recvs:
            cp.wait_recv()
        for cp in sends:
            cp.wait_send()
        for cp in local:
            cp.wait()

    return _Comm(arrs, [jax.ShapeDtypeStruct(a.shape, a.dtype) for a in arrs], _dma_sems(per * n, per * n, n), start, finish)


def _ada_mod(c_all, w_ada, b_cols):
    nl, d, w = w_ada.shape

    def body(c_ref, w_ref, b_ref, m_ref, ca_ref):
        sub = _iota((SUBLANES, d), 0)
        cv = jnp.zeros((SUBLANES, d), F32)
        for b in range(N_DEV):
            cv = jnp.where(sub == b, c_ref[b], cv)
        ca = cv * _sigmoid(cv)
        ca_ref[...] = ca
        m_ref[...] = jnp.zeros_like(m_ref)
        for l in range(nl):
            ml = _mm_hi(ca, w_ref[l]) + b_ref[l:l + 1, :]
            for b in range(N_DEV):
                m_ref[b, l:l + 1, :] = _row(ml, b)

    return pl.pallas_call(
        body, name="adaln_mod_columns", grid=(1,),
        in_specs=[_full(c_all.shape), _full(w_ada.shape), _full(b_cols.shape)],
        out_specs=[_full((N_DEV, SUBLANES, w)), _full((SUBLANES, d))],
        out_shape=[jax.ShapeDtypeStruct((N_DEV, SUBLANES, w), F32), jax.ShapeDtypeStruct((SUBLANES, d), F32)],
        compiler_params=_params(1),
    )(c_all, w_ada, b_cols)


def _ada_grad_adam(cact_t, dmods, w, m, v):
    nl, d, wd = w.shape
    tr = _row_tile(d, wd, 8)

    def body(c_ref, dm_ref, w_ref, m_ref, v_ref, g_ref, d_ref, mo_ref, vo_ref):
        cv = c_ref[...]
        dm = dm_ref[0]
        g = _col(cv, 0) * _row(dm, 0)
        for b in range(1, N_DEV):
            g = g + _col(cv, b) * _row(dm, b)
        delta, m2, v2 = _adam_math(w_ref[0], g, m_ref[0], v_ref[0])
        g_ref[0] = g
        d_ref[0] = delta
        mo_ref[0] = m2
        vo_ref[0] = v2

    blk = pl.BlockSpec((1, tr, wd), lambda l, i: (l, i, 0))
    return pl.pallas_call(
        body, name="adaln_grad_adam", grid=(nl, d // tr),
        in_specs=[pl.BlockSpec((tr, N_DEV), lambda l, i: (i, 0)), pl.BlockSpec((1, N_DEV, wd), lambda l, i: (l, 0, 0)),
                  blk, blk, blk],
        out_specs=[blk] * 4, out_shape=[jax.ShapeDtypeStruct((nl, d, wd), F32)] * 4,
        compiler_params=_params(2),
    )(cact_t, dmods, w, m, v)


REP_ROWS = ("norm_g", "dshift", "dscale", "dgate", "rg_conv_b", "rg_b_a", "rg_b_x", "rg_lambda", "ml_conv_b", "ml_norm_g",
            "ml_b_if")


def _sum_parts(recvs, name):
    def body(*refs):
        for r_ref, o_ref in zip(refs[:len(recvs)], refs[len(recvs):]):
            o_ref[...] = _sum_devices(r_ref)

    return pl.pallas_call(
        body, name=name, grid=(1,),
        in_specs=[_full(r.shape) for r in recvs], out_specs=[_full(r.shape[1:]) for r in recvs],
        out_shape=[jax.ShapeDtypeStruct(r.shape[1:], F32) for r in recvs], compiler_params=_params(1),
    )(*recvs)


def _adam_replicated(vp, mp, params, nl):
    d = vp.shape[1]
    nr = len(REP_ROWS)
    names = list(params)
    mat_shape = params["rg_w_a"][0].shape[1:]
    mat_rows = mp.shape[0] // (2 * nl)

    def pieces(name):
        if name == "final_g":
            return [(lambda vp_ref, mp_ref: vp_ref[nl * nr:nl * nr + 1, :], (slice(0, 1), slice(None)))]
        out = []
        for l in range(nl):
            if name in ("rg_w_a", "rg_w_x"):
                at = (2 * l + (name == "rg_w_x")) * mat_rows
                out.append((lambda vp_ref, mp_ref, at=at: mp_ref[at:at + mat_rows, :].reshape(mat_shape), l))
            elif name == "b_ada":
                for j in range(3):
                    r = l * nr + 1 + j
                    out.append((lambda vp_ref, mp_ref, r=r: vp_ref[r:r + 1, :], (slice(l, l + 1), slice(j * d, (j + 1) * d))))
            else:
                r = l * nr + REP_ROWS.index(name)
                cols = slice(0, LANES) if name == "ml_b_if" else slice(None)
                out.append((lambda vp_ref, mp_ref, r=r, cols=cols: vp_ref[r:r + 1, cols], (slice(l, l + 1), slice(None))))
        return out

    def body(*refs):
        vp_ref, mp_ref = refs[:2]
        ins, outs = refs[2:2 + 3 * len(names)], refs[2 + 3 * len(names):]
        for pi, name in enumerate(names):
            w_ref, m_ref, v_ref = ins[3 * pi:3 * pi + 3]
            g_ref, d_ref, mo_ref, vo_ref = outs[4 * pi:4 * pi + 4]
            for get, idx in pieces(name):
                g = get(vp_ref, mp_ref)
                delta, m2, v2 = _adam_math(w_ref[idx], g, m_ref[idx], v_ref[idx])
                g_ref[idx] = g
                d_ref[idx] = delta
                mo_ref[idx] = m2
                vo_ref[idx] = v2

    flat = [a for name in names for a in params[name]]
    out_shape = [jax.ShapeDtypeStruct(params[name][0].shape, F32) for name in names for _ in range(4)]
    res = pl.pallas_call(
        body, name="adam_replicated", grid=(1,),
        in_specs=[_full(vp.shape), _full(mp.shape)] + [_full(a.shape) for a in flat],
        out_specs=[_full(o.shape) for o in out_shape], out_shape=out_shape, compiler_params=_params(1),
    )(vp, mp, *flat)
    return {name: res[4 * pi:4 * pi + 4] for pi, name in enumerate(names)}


class _Plan:
    def __init__(self):
        self.hosted, self.after = {}, {}

    def host(self, key, comm, then=None):
        self.hosted.setdefault(key, []).append(comm)
        if then is not None:
            self.after.setdefault(key, []).append(then)

    def comms(self, key):
        return self.hosted.pop(key, None)

    def done(self, key):
        for fn in self.after.pop(key, []):
            fn()

    def flush(self):
        while self.hosted:
            key = next(iter(self.hosted))
            _call(lambda: None, self.comms(key), name="exchange_after_%s_%d" % key, grid=(1,), in_specs=[], out_specs=[],
                  out_shape=[], args=())
            self.done(key)


VEC_TABLE = ("norm_g", "rg_conv_b", "rg_b_a", "rg_b_x", "rg_lambda", "ml_conv_b", "ml_norm_g")


def _vec_table(rep):
    rows = [rep[n] for n in VEC_TABLE]
    return jnp.stack(rows + [jnp.zeros_like(rows[0])] * (SUBLANES - len(rows)), axis=1)


def _layer_fwd(l, xl, mod3, wl, rep, plan):
    s, d = xl.shape
    t_big, t_mid = _tile_for(s, 512), _tile_for(s, 256)
    nh_ml = rep["ml_b_if"].shape[1] // 2
    vec = lambda name: _vec(rep["vecs"], l, VEC_TABLE.index(name))
    shift, scale, gate = (_vec(mod3, l, kk) for kk in range(3))
    hosted = lambda name: plan.comms((name, l)) if plan else None
    done = lambda name: plan.done((name, l)) if plan else None
    u, hbf = _in_fwd(xl, vec("norm_g"), scale, shift, wl["w_in_g"], 0, t_mid, hosted("in_proj_fwd"))
    done("in_proj_fwd")
    h_rg, y_rg = _rg_fwd(u, d, wl["rg_conv_w"], vec("rg_conv_b"), rep["rg_w_a_bf"][l], vec("rg_b_a"),
                         rep["rg_w_x_bf"][l], vec("rg_b_x"), vec("rg_lambda"), t_mid, hosted("rglru_fwd"))
    done("rglru_fwd")
    q, k, v, gcol = _ml_pre(u, d, wl["ml_conv_w"], vec("ml_conv_b"), wl["w_qkv"][0], wl["w_qkv"][1],
                            wl["w_qkv"][2], wl["wif_pad"], wl["bif_pad"], t_mid, hosted("mlstm_proj_fwd"))
    done("mlstm_proj_fwd")
    grow = gcol[:, 0:16].T
    cell, y_ml, cs, ns, ms, mt = _ml_cell_fwd(q, k, v, gcol, grow, u, vec("ml_norm_g"), nh_ml, hosted("mlstm_cell_fwd"))
    done("mlstm_cell_fwd")
    x_new, y = _out_fwd(xl, y_rg, y_ml, gate, wl["w_out_g"], 0, t_big, hosted("out_proj_fwd"))
    done("out_proj_fwd")
    saved = dict(x=xl, u=u, hbf=hbf, h_rg=h_rg, y_rg=y_rg, q=q, k=k, v=v, gcol=gcol, grow=grow, cell=cell, y_ml=y_ml,
                 cs=cs, ns=ns, ms=ms, mt=mt, y=y, scale=scale, gate=gate)
    return x_new, saved


def _layer_bwd(l, dx, sv, wl, rep, plan, grads=None, split_last=False):
    s, d = dx.shape
    t_big, t_mid = _tile_for(s, 512), _tile_for(s, 256)
    nh_ml = rep["ml_b_if"].shape[1] // 2
    nd, _, _, w_cols = wl["w_in_g"].shape
    grads = {} if grads is None else grads
    vec = lambda name: _vec(rep["vecs"], l, VEC_TABLE.index(name))
    hosted = lambda name: plan.comms((name, l)) if plan else None
    done = lambda name: plan.done((name, l)) if plan else None
    dy_rg, dy_ml, gw_out, dgate = _out_bwd(dx, sv["gate"], sv["y"], sv["y_rg"], sv["y_ml"], wl["w_out_g"], 0, t_big,
                                           hosted("out_proj_bwd"))
    grads.update(w_out=gw_out)
    done("out_proj_bwd")
    dq, dk, dv, dgates, d_mlo, d_mlz, g_mlng = _ml_cell_bwd(
        dy_ml, sv["u"], sv["cell"], sv["q"], sv["k"], sv["v"], sv["gcol"], sv["grow"], sv["mt"], sv["cs"], sv["ns"],
        sv["ms"], vec("ml_norm_g"), nh_ml, hosted("mlstm_cell_bwd"))
    done("mlstm_cell_bwd")
    d_mlx, g_wq, g_wk, g_wv, g_wift, g_bif, g_mlcw, g_mlcb = _ml_pre_bwd(
        dq, dk, dv, dgates, sv["gcol"], sv["u"], sv["q"], sv["k"], sv["v"], wl["ml_conv_w"], vec("ml_conv_b"),
        wl["w_qkv"][0], wl["w_qkv"][1], wl["w_qkv"][2], wl["wift_pad"], t_mid, hosted("mlstm_proj_bwd"))
    done("mlstm_proj_bwd")
    d_rgx, d_rgz, g_wa, g_wx, g_ba, g_bx, g_lam, g_rgcw, g_rgcb = _rg_bwd(
        dy_rg, sv["u"], sv["h_rg"], wl["rg_conv_w"], vec("rg_conv_b"), rep["rg_w_a_bf"][l], vec("rg_b_a"),
        rep["rg_w_x_bf"][l], vec("rg_b_x"), vec("rg_lambda"), t_mid, hosted("rglru_bwd"))
    grads.update(w_qkv=jnp.stack([g_wq, g_wk, g_wv]), rg_conv_w=g_rgcw[0:CONV_WIDTH], ml_conv_w=g_mlcw[0:CONV_WIDTH],
                 wif_t=g_wift[0:8], rg_w_a=g_wa, rg_w_x=g_wx)
    acc = dict(dgate=dgate, rg_conv_b=g_rgcb, rg_b_a=g_ba, rg_b_x=g_bx, rg_lambda=g_lam, ml_conv_b=g_mlcb,
               ml_b_if=g_bif, ml_norm_g=g_mlng)
    done("rglru_bwd")
    pieces = [d_rgx, d_rgz, d_mlx, d_mlo, d_mlz]
    grads.update(w_in=_in_bwd_w(pieces, sv["hbf"], w_cols, tuple(range(nd)), t_big, hosted("in_proj_bwd_w")))
    done("in_proj_bwd_w")
    n_tiles = s // t_mid
    counts = [n_tiles // 4, n_tiles - n_tiles // 4 - 1, 1] if split_last and n_tiles >= 4 else [n_tiles]
    in_args = (pieces, sv["x"], dx, vec("norm_g"), sv["scale"], wl["w_in_g"], 0, t_mid)
    res, at = None, 0
    for key, count in zip(("in_proj_bwd_x", "in_proj_bwd_x_rest", "in_proj_bwd_x_end"), counts):
        res = _in_bwd(*in_args, hosted(key), (at, count), res)
        done(key)
        at += count
    dx, dscale, dshift, g_ng = res
    acc.update(norm_g=g_ng, dshift=dshift, dscale=dscale)
    grads.update(acc=acc, dmod=jnp.concatenate([dshift[0:1], dscale[0:1], dgate[0:1]], axis=1))
    return dx, grads


def _local_step(x, c, target, wts):
    d = x.shape[1]
    nl = wts["w_in_g"].shape[1]
    mod, cact = _mod_call(c, wts["w_ada_g"], wts["b_ada"])
    wl = [dict(w_in_g=wts["w_in_g"][:, l:l + 1], w_out_g=wts["w_out_g"][:, l:l + 1], w_qkv=wts["w_qkv"][l],
               rg_conv_w=wts["rg_conv_w"][l], ml_conv_w=wts["ml_conv_w"][l], wif_pad=wts["wif_pad"][l],
               wift_pad=wts["wift_pad"][l], bif_pad=wts["bif_pad"][l]) for l in range(nl)]
    rep = dict(wts, vecs=_vec_table(wts))
    mod3 = mod.reshape(nl, 3, d)
    saved, xl = [], x
    for l in range(nl):
        xl, sv = _layer_fwd(l, xl, mod3, wl[l], rep, None)
        saved.append(sv)
    dx, loss_p, g_final = _loss_call(xl, wts["final_g"].reshape(1, -1), target, _tile_for(x.shape[0], 512))
    grads = [None] * nl
    for l in reversed(range(nl)):
        dx, grads[l] = _layer_bwd(l, dx, saved[l], wl[l], rep, None)
        grads[l]["w_ada"] = _ada_bwd_w(cact[0].reshape(d, 1), grads[l]["dmod"], wts["w_ada_g"].shape[0])
        grads[l]["b_ada"] = grads[l]["dmod"][0]
        grads[l].update({n: a[0] for n, a in grads[l]["acc"].items()})
        grads[l]["ml_b_if"] = grads[l]["ml_b_if"][0:8]
    return loss_p[0, 0], dx, grads, g_final[0]


def _full_qkv(qkv_g, d):
    nd, _, rows3, dh = qkv_g.shape
    nh = d // dh
    rsh = rows3 // (3 * nh)
    return qkv_g.reshape(nd, 3, nh, rsh, dh).transpose(1, 2, 0, 3, 4).reshape(3, nh, nd * rsh, dh)


def _small_weights(small, l, ml_b_if):
    nd = small.shape[0]
    sm = small[:, l]
    cw = sm[:, 0:2 * CONV_WIDTH].reshape(nd, 2, CONV_WIDTH, LANES).transpose(1, 2, 0, 3).reshape(2, CONV_WIDTH, nd * LANES)
    if_rows = (sm.shape[1] - 2 * CONV_WIDTH) * LANES // 8
    wif_t = sm[:, 2 * CONV_WIDTH:].reshape(nd, 8, if_rows).transpose(1, 0, 2).reshape(8, nd * if_rows)
    wift_pad = jnp.pad(wif_t, ((0, LANES - 8), (0, 0))).astype(BF16)
    return dict(rg_conv_w=cw[0], ml_conv_w=cw[1], wift_pad=wift_pad, wif_pad=wift_pad.T,
                bif_pad=jnp.pad(ml_b_if[l], (0, LANES - 8)).reshape(1, LANES))


def kernel(x, c, norm_g, w_ada, b_ada, w_in, rg_conv_w, rg_conv_b, rg_w_a, rg_b_a, rg_w_x, rg_b_x, rg_lambda, ml_conv_w, ml_conv_b, ml_w_q, ml_w_k, ml_w_v, ml_w_if, ml_b_if, ml_norm_g, w_out, final_g, loss_target, m_norm_g, m_w_ada, m_b_ada, m_w_in, m_rg_conv_w, m_rg_conv_b, m_rg_w_a, m_rg_b_a, m_rg_w_x, m_rg_b_x, m_rg_lambda, m_ml_conv_w, m_ml_conv_b, m_ml_w_q, m_ml_w_k, m_ml_w_v, m_ml_w_if, m_ml_b_if, m_ml_norm_g, m_w_out, m_final_g, v_norm_g, v_w_ada, v_b_ada, v_w_in, v_rg_conv_w, v_rg_conv_b, v_rg_w_a, v_rg_b_a, v_rg_w_x, v_rg_b_x, v_rg_lambda, v_ml_conv_w, v_ml_conv_b, v_ml_w_q, v_ml_w_k, v_ml_w_v, v_ml_w_if, v_ml_b_if, v_ml_norm_g, v_w_out, v_final_g):
    given = dict(locals())
    nl = w_in.shape[0]
    d = x.shape[2]
    rep = {n: given[n] for n in REPLICATED}
    rep.update(rg_w_a_bf=rg_w_a.astype(BF16), rg_w_x_bf=rg_w_x.astype(BF16))
    bf = lambda a: a.astype(BF16)

    def qkv_shard(prefix):
        return jnp.stack([given[prefix + "ml_w_q"], given[prefix + "ml_w_k"], given[prefix + "ml_w_v"]], axis=1).reshape(
            nl, -1, ml_w_q.shape[-1])

    def small_shard(prefix):
        return _small_pack(given[prefix + "rg_conv_w"], given[prefix + "ml_conv_w"], given[prefix + "ml_w_if"])

    plan = _Plan()
    qkv = qkv_shard("")
    w_in_first, small = _gather_two_level([bf(w_in[0:1]), small_shard("")], "gather_first")
    wl = [_small_weights(small, l, ml_b_if) for l in range(nl)]
    wl[0]["w_in_g"] = w_in_first

    def gather_behind(arrs, ici_host, fwd_host, then):
        ici = _gather_ici_comm(arrs)

        def pass_on():
            fwd = _gather_fwd_comm(ici.results)
            plan.host(fwd_host, fwd, lambda: then(fwd.results))

        plan.host(ici_host, ici, pass_on)

    def got_out(l):
        return lambda r: wl[l].update(w_out_g=r[0], w_qkv=_full_qkv(r[1], d))

    gather_behind([bf(w_out[0:1]), bf(qkv[0:1])], ("in_proj_fwd", 0), ("rglru_fwd", 0), got_out(0))
    for l in range(1, nl):
        gather_behind([bf(w_in[l:l + 1])], ("rglru_fwd", l - 1), ("mlstm_proj_fwd", l - 1),
                      lambda r, l=l: wl[l].update(w_in_g=r[0]))
        gather_behind([bf(w_out[l:l + 1]), bf(qkv[l:l + 1])], ("mlstm_cell_fwd", l - 1), ("out_proj_fwd", l - 1), got_out(l))

    wcols = w_ada.shape[2]
    c_all = _exchange([jnp.broadcast_to(c, (SUBLANES, d))], True, "gather_condition")[0]
    me = 4 * lax.axis_index("x") + 2 * lax.axis_index("y") + lax.axis_index("c")
    b_cols = jnp.pad(lax.dynamic_slice_in_dim(b_ada, me * wcols, wcols, axis=1), ((0, SUBLANES - nl), (0, 0)))
    mod_cols, cact_all = _ada_mod(c_all, w_ada, b_cols)
    mod_blocks = _exchange([mod_cols], False, "scatter_modulation")[0]
    mod3 = mod_blocks[:, 0:nl].transpose(1, 0, 2).reshape(nl, 3, d)
    rep["vecs"] = _vec_table(rep)
    saved, xl = [], x[0]
    for l in range(nl):
        xl, sv = _layer_fwd(l, xl, mod3, wl[l], rep, plan)
        saved.append(sv)
    grad_x, loss_p, g_final = _loss_call(xl, final_g.reshape(1, -1), loss_target[0], _tile_for(xl.shape[0], 512))

    keys = ("w_in", "w_out", "w_qkv", "small")
    parity = lax.axis_index("c").astype(jnp.int32).reshape(1)
    grads, recv = [None] * nl, [None] * nl

    def parts_of(g):
        return [g["w_in"], g["w_out"], _qkv_slots(g["w_qkv"], N_DEV), _small_slots(g)]

    def pair_sums(l, parts, other):
        return [_pair_sum(a, o, parity, "pair_sum_%s_layer%d" % (key, l)) for key, a, o in zip(keys, parts, other)]

    def reduce_behind(l, host_layer):
        parts = parts_of(grads[l])
        swap = _core_swap_comm(parts)

        def summed():
            sums = pair_sums(l, parts, swap.results)
            big = _chip_swap_comm([sums[0]])
            rest = _chip_swap_comm(sums[1:])
            plan.host(("mlstm_cell_bwd", host_layer), big)
            plan.host(("rglru_bwd", host_layer), rest, lambda: recv.__setitem__(l, big.results + rest.results))

        plan.host(("out_proj_bwd", host_layer), swap, summed)

    first, own = {}, {}

    def reduce_own(names, parts_fn, ready_key, swap_key, chip_key):
        def go():
            parts = parts_fn()
            swap = _core_swap_comm(parts)

            def summed():
                sums = [_pair_sum(a, o, parity, "pair_sum_%s_layer0" % n) for n, a, o in zip(names, parts, swap.results)]
                chip = _chip_swap_comm(sums)
                plan.host(chip_key, chip, lambda: own.update(zip(names, chip.results)))

            plan.host(swap_key, swap, summed)

        plan.after.setdefault(ready_key, []).append(go)

    reduce_own(["w_out"], lambda: [first["w_out"]], ("out_proj_bwd", 0), ("mlstm_cell_bwd", 0), ("mlstm_proj_bwd", 0))
    reduce_own(["w_qkv", "small"], lambda: [_qkv_slots(first["w_qkv"], N_DEV), _small_slots(first)],
               ("rglru_bwd", 0), ("in_proj_bwd_w", 0), ("in_proj_bwd_x", 0))
    reduce_own(["w_in"], lambda: [first["w_in"]], ("in_proj_bwd_w", 0), ("in_proj_bwd_x", 0), ("in_proj_bwd_x_rest", 0))

    for l in reversed(range(nl)):
        if l > 0:
            grad_x, grads[l] = _layer_bwd(l, grad_x, saved[l], wl[l], rep, plan)
            reduce_behind(l, l - 1)
        else:
            grad_x, grads[l] = _layer_bwd(l, grad_x, saved[l], wl[l], rep, plan, first, True)
    plan.flush()
    recv[0] = [own[key] for key in keys]

    shard = {p: dict(w_in=given[p + "w_in"], w_out=given[p + "w_out"], w_qkv=qkv_shard(p), small=small_shard(p))
             for p in ("", "m_", "v_")}
    res = {}
    for ki, key in enumerate(keys):
        res[key] = _reduce_adam([recv[l][ki] for l in range(nl)], shard[""][key], shard["m_"][key], shard["v_"][key],
                                "reduce_adam_" + key)

    dmods = jnp.concatenate([grads[l]["dmod"] for l in range(nl)], axis=0)
    dmod_blocks = jnp.pad(dmods.reshape(nl, N_DEV, wcols).transpose(1, 0, 2), ((0, 0), (0, SUBLANES - nl), (0, 0)))
    dmod_all = _exchange([dmod_blocks], False, "scatter_dmod")[0][:, 0:nl].transpose(1, 0, 2)
    res["w_ada"] = _ada_grad_adam(cact_all.T, dmod_all, w_ada, m_w_ada, v_w_ada)

    widen = lambda a: jnp.pad(a, ((0, 0), (0, d - a.shape[1])))
    rows = [widen(grads[l]["acc"][n][0:1]) for l in range(nl) for n in REP_ROWS] + [g_final[0:1], widen(loss_p[0:1])]
    vp = jnp.concatenate(rows + [jnp.zeros(((-len(rows)) % ROW_ALIGN, d), F32)], axis=0)
    mp = jnp.stack([jnp.stack([grads[l]["rg_w_a"], grads[l]["rg_w_x"]]) for l in range(nl)]).reshape(-1, LANES)
    got = _exchange([vp.reshape(N_DEV, -1, d), mp.reshape(N_DEV, -1, LANES)], False, "reduce_scatter_replicated")
    vp_r, mp_r = _exchange(_sum_parts(got, "sum_replicated"), True, "gather_replicated")
    vp_r, mp_r = vp_r.reshape(-1, d), mp_r.reshape(-1, LANES)
    lanes = lambda a: jnp.pad(a, ((0, 0), (0, LANES - a.shape[1])))
    shaped = dict(ml_b_if=lanes, final_g=lambda a: a.reshape(1, d))
    names = [n for n in REPLICATED if n != "b_ada"] + ["b_ada"]
    rep_res = _adam_replicated(vp_r, mp_r, {n: tuple(shaped.get(n, lambda a: a)(given[p + n]) for p in ("", "m_", "v_"))
                                            for n in names}, nl)
    unshaped = dict(ml_b_if=lambda a: a[:, 0:ml_b_if.shape[1]], final_g=lambda a: a.reshape(d))
    rep_out = [{n: unshaped.get(n, lambda a: a)(rep_res[n][kind]) for n in names} for kind in range(4)]
    loss = vp_r[nl * len(REP_ROWS) + 1, 0]

    if_rows = ml_w_if.shape[1]
    order = ("norm_g", "w_ada", "b_ada", "w_in", "rg_conv_w", "rg_conv_b", "rg_w_a", "rg_b_a", "rg_w_x", "rg_b_x",
             "rg_lambda", "ml_conv_w", "ml_conv_b", "ml_w_q", "ml_w_k", "ml_w_v", "ml_w_if", "ml_b_if", "ml_norm_g",
             "w_out", "final_g")
    outs = [loss, grad_x[None]]
    for kind in range(4):
        qkv_k = res["w_qkv"][kind].reshape((nl, 3) + ml_w_q.shape[1:])
        rg_cw, ml_cw, wif = _small_unpack(res["small"][kind], if_rows)
        sharded = dict(w_ada=res["w_ada"][kind], w_in=res["w_in"][kind], w_out=res["w_out"][kind], ml_w_q=qkv_k[:, 0],
                       ml_w_k=qkv_k[:, 1], ml_w_v=qkv_k[:, 2], rg_conv_w=rg_cw, ml_conv_w=ml_cw, ml_w_if=wif)
        for n in order:
            outs.append(sharded[n] if n in sharded else rep_out[kind][n])
    return tuple(outs)
```

```python
import functools

import jax
import jax.numpy as jnp
from jax import lax
from jax.experimental import pallas as pl
from jax.experimental.pallas import tpu as pltpu

F32 = jnp.float32
BF16 = jnp.bfloat16
MESH_AXES = ("x", "y", "c")
N_DEV = 8
EPS = 1e-6
RG_C = 8.0
ML_CHUNK = 128
CONV_WIDTH = 4
ADAM_LR = 0.001
ADAM_B1 = 0.9
ADAM_B2 = 0.999
ADAM_EPS = 1e-08
ADAM_WD = 0.01
ADAM_STEP = 10
NEG_BIG = -1e30
LANES = 128
SUBLANES = 8
VMEM_LIMIT = 56 * 1024 * 1024
HI = lax.Precision.HIGHEST


def _params(n_grid):
    return pltpu.CompilerParams(dimension_semantics=("arbitrary",) * n_grid, vmem_limit_bytes=VMEM_LIMIT)


def _mm(a, b):
    return jnp.dot(a.astype(BF16), b.astype(BF16), preferred_element_type=F32)


def _mm_nt(a, b):
    return lax.dot_general(a.astype(BF16), b.astype(BF16), (((1,), (1,)), ((), ())), preferred_element_type=F32)


def _mm_tn(a, b):
    return lax.dot_general(a.astype(BF16), b.astype(BF16), (((0,), (0,)), ((), ())), preferred_element_type=F32)


def _mm_hi(a, b):
    return jnp.dot(a, b, precision=HI, preferred_element_type=F32)


def _sigmoid(x):
    return 1.0 / (1.0 + jnp.exp(-x))


def _softplus(x):
    return jnp.maximum(x, 0.0) + jnp.log(1.0 + jnp.exp(-jnp.abs(x)))


def _neg_expm1(x):
    poly = -x * (1.0 + x * (0.5 + x * (1.0 / 6.0 + x * (1.0 / 24.0 + x * (1.0 / 120.0)))))
    return jnp.where(jnp.abs(x) < 0.05, poly, 1.0 - jnp.exp(x))


def _iota(shape, dim):
    return lax.broadcasted_iota(jnp.int32, shape, dim)


def _colsum(x):
    return jnp.sum(x, axis=0, keepdims=True)


def _rowsum(x):
    return jnp.sum(x, axis=1, keepdims=True)


def _col(x, j):
    return _rowsum(jnp.where(_iota(x.shape, 1) == j, x, 0.0))


def _row(x, j):
    return _colsum(jnp.where(_iota(x.shape, 0) == j, x, 0.0))


def _shift_down(x, j, prev8):
    if j == 0:
        return x
    t = x.shape[0]
    main = jnp.where(_iota(x.shape, 0) >= j, pltpu.roll(x, j, 0), 0.0)
    fix = jnp.where(_iota(prev8.shape, 0) < j, pltpu.roll(prev8, j, 0), 0.0)
    return jnp.concatenate([main[0:SUBLANES] + fix, main[SUBLANES:t]], axis=0)


def _shift_up(x, j, next8):
    if j == 0:
        return x
    t = x.shape[0]
    main = jnp.where(_iota(x.shape, 0) < t - j, pltpu.roll(x, t - j, 0), 0.0)
    fix = jnp.where(_iota(next8.shape, 0) >= SUBLANES - j, pltpu.roll(next8, SUBLANES - j, 0), 0.0)
    return jnp.concatenate([main[0:t - SUBLANES], main[t - SUBLANES:t] + fix], axis=0)


def _conv(x, prev8, w_ref):
    y = w_ref[CONV_WIDTH - 1:CONV_WIDTH, :] * x
    for j in range(1, CONV_WIDTH):
        y = y + w_ref[CONV_WIDTH - 1 - j:CONV_WIDTH - j, :] * _shift_down(x, j, prev8)
    return y


def _conv_bwd_x(dy, next8, w_ref):
    dx = w_ref[CONV_WIDTH - 1:CONV_WIDTH, :] * dy
    for j in range(1, CONV_WIDTH):
        dx = dx + w_ref[CONV_WIDTH - 1 - j:CONV_WIDTH - j, :] * _shift_up(dy, j, next8)
    return dx


def _scan_into(a, b, carry, out_ref, reverse):
    t, c = a.shape
    groups = t // SUBLANES
    a3 = a.reshape(groups, SUBLANES, c)
    b3 = b.reshape(groups, SUBLANES, c)
    sub = _iota(a3.shape, 1)
    for step in (1, 2, 4):
        keep = sub < SUBLANES - step if reverse else sub >= step
        shift = SUBLANES - step if reverse else step
        a_s = jnp.where(keep, pltpu.roll(a3, shift, 1), 1.0)
        b_s = jnp.where(keep, pltpu.roll(b3, shift, 1), 0.0)
        b3 = a3 * b_s + b3
        a3 = a3 * a_s
    for g in (reversed(range(groups)) if reverse else range(groups)):
        rows = slice(g * SUBLANES, (g + 1) * SUBLANES)
        out_ref[rows, :] = b3[g] + a3[g] * carry
        edge = g * SUBLANES if reverse else (g + 1) * SUBLANES - 1
        carry = out_ref[edge:edge + 1, :]


def _blockdiag(x, w_ref, transpose_w=False):
    nh, dh, _ = w_ref.shape
    outs = []
    for h in range(nh):
        xs = x[:, h * dh:(h + 1) * dh]
        outs.append(_mm_nt(xs, w_ref[h]) if transpose_w else _mm(xs, w_ref[h]))
    return jnp.concatenate(outs, axis=1)


def _rg_gates(xc, wa_ref, ba_ref, wx_ref, bx_ref, lam_ref):
    r = _sigmoid(_blockdiag(xc, wa_ref) + ba_ref[...])
    ig = _sigmoid(_blockdiag(xc, wx_ref) + bx_ref[...])
    sp = _softplus(-lam_ref[...])
    log_a = -RG_C * r * sp
    a = jnp.exp(log_a)
    beta = jnp.sqrt(_neg_expm1(2.0 * log_a))
    return r, ig, sp, a, beta


def _bcast8(row):
    return jnp.broadcast_to(row, (SUBLANES, row.shape[1]))


def _full(shape):
    nd = len(shape)
    return pl.BlockSpec(shape, lambda *_: (0,) * nd)


class _Comm:
    def __init__(self, arrays, out_shapes, sems, start, finish, aliases=()):
        self.arrays, self.out_shapes, self.sems = list(arrays), list(out_shapes), list(sems)
        self.start, self.finish, self.aliases = start, finish, tuple(aliases)
        self.results = None


class _RowOf:
    def __init__(self, ref, k):
        self.ref, self.k = ref, k

    def __getitem__(self, idx):
        cols = slice(None) if idx is Ellipsis else idx[1]
        return self.ref[0, self.k:self.k + 1, cols]


def _vec(table, layer, k):
    return ("row", table, layer, k)


def _is_row(arg):
    return isinstance(arg, tuple) and len(arg) == 4 and arg[0] == "row"


def _call(body, comms, *, name, grid, in_specs, out_specs, out_shape, args, scratch_shapes=(), aliases=None):
    comms = [cm for cm in (comms or []) if cm is not None]
    rows = {i: a[3] for i, a in enumerate(args) if _is_row(a)}
    in_specs = [pl.BlockSpec((1,) + a[1].shape[1:], functools.partial(lambda layer, *_: (layer, 0, 0), a[2]))
                if _is_row(a) else sp for a, sp in zip(args, in_specs)]
    args = tuple(a[1] if _is_row(a) else a for a in args)
    n_in, n_out, n_sc = len(args), len(out_shape), len(scratch_shapes)
    c_arrays = [a for cm in comms for a in cm.arrays]
    c_outs = [o for cm in comms for o in cm.out_shapes]
    c_sems = [sm for cm in comms for sm in cm.sems]
    aliases, a_at, o_at = dict(aliases or {}), n_in, n_out
    for cm in comms:
        for (i, j) in cm.aliases:
            aliases[a_at + i] = o_at + j
        a_at += len(cm.arrays)
        o_at += len(cm.out_shapes)

    def wrapped(*refs):
        ins, c_in = refs[:n_in], refs[n_in:n_in + len(c_arrays)]
        ins = [_RowOf(r, rows[i]) if i in rows else r for i, r in enumerate(ins)]
        at = n_in + len(c_arrays)
        outs, c_out = refs[at:at + n_out], refs[at + n_out:at + n_out + len(c_outs)]
        at += n_out + len(c_outs)
        scr, sems = refs[at:at + n_sc], refs[at + n_sc:]
        views, ia, io, isem = [], 0, 0, 0
        for cm in comms:
            views.append((c_in[ia:ia + len(cm.arrays)], c_out[io:io + len(cm.out_shapes)], sems[isem:isem + len(cm.sems)]))
            ia, io, isem = ia + len(cm.arrays), io + len(cm.out_shapes), isem + len(cm.sems)
        if comms:
            @pl.when(pl.program_id(0) == 0)
            def _():
                for cm, view in zip(comms, views):
                    cm.start(*view)

        body(*ins, *outs, *scr)
        if comms:
            @pl.when(pl.program_id(0) == grid[0] - 1)
            def _():
                for cm, view in zip(comms, views):
                    cm.finish(*view)

    hbm = pl.BlockSpec(memory_space=pl.ANY)
    res = pl.pallas_call(
        wrapped, name=name, grid=grid,
        in_specs=list(in_specs) + [hbm] * len(c_arrays), out_specs=list(out_specs) + [hbm] * len(c_outs),
        out_shape=list(out_shape) + c_outs, scratch_shapes=list(scratch_shapes) + c_sems,
        input_output_aliases=aliases, compiler_params=_params(len(grid)),
    )(*args, *c_arrays)
    at = n_out
    for cm in comms:
        cm.results = list(res[at:at + len(cm.out_shapes)])
        at += len(cm.out_shapes)
    return list(res[:n_out])


def _mod_call(c, w_ada_g, b_ada):
    nd, nl, d, w = w_ada_g.shape

    def body(c_ref, w_ref, b_ref, mod_ref, cact_ref):
        cv = c_ref[...]
        ca = _bcast8(cv * _sigmoid(cv))
        cact_ref[...] = ca
        mod_ref[0, 0] = _mm(ca, w_ref[0, 0]) + b_ref[0, 0]

    mod, cact = pl.pallas_call(
        body, name="adaln_mod", grid=(nl, nd),
        in_specs=[_full((1, d)),
                  pl.BlockSpec((1, 1, d, w), lambda l, j: (j, l, 0, 0)),
                  pl.BlockSpec((1, 1, 1, w), lambda l, j: (l, j, 0, 0))],
        out_specs=[pl.BlockSpec((1, 1, SUBLANES, w), lambda l, j: (l, j, 0, 0)), _full((SUBLANES, d))],
        out_shape=[jax.ShapeDtypeStruct((nl, nd, SUBLANES, w), F32), jax.ShapeDtypeStruct((SUBLANES, d), F32)],
        compiler_params=_params(2),
    )(c, w_ada_g, b_ada.reshape(nl, nd, 1, w))
    return mod[:, :, 0, :].reshape(nl, nd * w), cact


def _join_columns(w_ref, wcat):
    nd, _, _, w = w_ref.shape

    @pl.when(pl.program_id(0) == 0)
    def _():
        for j in range(nd):
            wcat[:, j * w:(j + 1) * w] = w_ref[j, 0]


def _in_fwd(x, ng, scale, shift, w_in_g, layer, tile, comms=None):
    s, d = x.shape
    nd, _, _, w = w_in_g.shape

    def body(x_ref, ng_ref, sc_ref, sh_ref, w_ref, u_ref, h_ref, wcat):
        _join_columns(w_ref, wcat)
        xv = x_ref[...]
        rs = lax.rsqrt(jnp.mean(xv * xv, axis=1, keepdims=True) + EPS)
        hb = (xv * rs * ng_ref[...] * (1.0 + sc_ref[...]) + sh_ref[...]).astype(BF16)
        h_ref[...] = hb
        u_ref[...] = jnp.dot(hb, wcat[...], preferred_element_type=F32)

    return _call(
        body, comms, name="in_proj_fwd", grid=(s // tile,),
        in_specs=[pl.BlockSpec((tile, d), lambda i: (i, 0)), _full((1, d)), _full((1, d)), _full((1, d)),
                  pl.BlockSpec((nd, 1, d, w), lambda i: (0, layer, 0, 0), pipeline_mode=pl.Buffered(1))],
        out_specs=[pl.BlockSpec((tile, nd * w), lambda i: (i, 0)), pl.BlockSpec((tile, d), lambda i: (i, 0))],
        out_shape=[jax.ShapeDtypeStruct((s, nd * w), F32), jax.ShapeDtypeStruct((s, d), BF16)],
        scratch_shapes=[pltpu.VMEM((d, nd * w), BF16)],
        args=(x, ng, scale, shift, w_in_g))


def _rg_fwd(u, d, conv_w, conv_b, w_a, b_a, w_x, b_x, lam, tile, comms=None):
    s = u.shape[0]

    def body(x_ref, z_ref, cw_ref, cb_ref, wa_ref, ba_ref, wx_ref, bx_ref, lam_ref,
             h_ref, y_ref, xc_ref, r_ref, i_ref, a_ref, beta_ref, prev8, hcar):
        @pl.when(pl.program_id(0) == 0)
        def _():
            prev8[...] = jnp.zeros_like(prev8)
            hcar[...] = jnp.zeros_like(hcar)

        x = x_ref[...]
        xc = _conv(x, prev8[...], cw_ref) + cb_ref[...]
        prev8[...] = x[tile - SUBLANES:tile, :]
        r, ig, _, a, beta = _rg_gates(xc, wa_ref, ba_ref, wx_ref, bx_ref, lam_ref)
        xc_ref[...] = xc
        r_ref[...] = r
        i_ref[...] = ig
        a_ref[...] = a
        beta_ref[...] = beta
        _scan_into(a, beta * ig * xc, hcar[SUBLANES - 1:SUBLANES, :], h_ref, False)
        h = h_ref[...]
        hcar[...] = h[tile - SUBLANES:tile, :]
        z = z_ref[...]
        y_ref[...] = (h * z * _sigmoid(z)).astype(BF16)

    vec = _full((1, d))
    return _call(
        body, comms, name="rglru_fwd", grid=(s // tile,),
        in_specs=[pl.BlockSpec((tile, d), lambda i: (i, 0)), pl.BlockSpec((tile, d), lambda i: (i, 1)),
                  _full(conv_w.shape), vec, _full(w_a.shape), vec, _full(w_x.shape), vec, vec],
        out_specs=[pl.BlockSpec((tile, d), lambda i: (i, 0))] * 7,
        out_shape=[jax.ShapeDtypeStruct((s, d), F32), jax.ShapeDtypeStruct((s, d), BF16)] + [jax.ShapeDtypeStruct((s, d), F32)] * 5,
        scratch_shapes=[pltpu.VMEM((SUBLANES, d), F32), pltpu.VMEM((SUBLANES, d), F32)],
        args=(u, u, conv_w, conv_b, w_a, b_a, w_x, b_x, lam))


def _ml_pre(u, d, conv_w, conv_b, w_q, w_k, w_v, wif, bif, tile, comms=None):
    s = u.shape[0]
    nh = w_q.shape[0]

    def body(x_ref, cw_ref, cb_ref, wq_ref, wk_ref, wv_ref, wif_ref, bif_ref, q_ref, k_ref, v_ref, g_ref, prev8):
        @pl.when(pl.program_id(0) == 0)
        def _():
            prev8[...] = jnp.zeros_like(prev8)

        x = x_ref[...]
        pre = _conv(x, prev8[...], cw_ref) + cb_ref[...]
        prev8[...] = x[tile - SUBLANES:tile, :]
        xc = pre * _sigmoid(pre)
        q = _blockdiag(xc, wq_ref)
        k = _blockdiag(xc, wk_ref)
        v = _blockdiag(x, wv_ref)
        q_ref[...] = q
        k_ref[...] = k
        v_ref[...] = v
        g = _mm(q, wif_ref[0:d, :]) + _mm(k, wif_ref[d:2 * d, :]) + _mm(v, wif_ref[2 * d:3 * d, :]) + bif_ref[...]
        lane = _iota(g.shape, 1)
        gl = jnp.where(lane < 4, g, jnp.where(lane < 8, -_softplus(-g), 0.0))
        tri = jnp.where(_iota((ML_CHUNK, ML_CHUNK), 1) <= _iota((ML_CHUNK, ML_CHUNK), 0), 1.0, 0.0)
        cums = [_mm_hi(tri, gl[c * ML_CHUNK:(c + 1) * ML_CHUNK, :]) for c in range(tile // ML_CHUNK)]
        cum = cums[0] if len(cums) == 1 else jnp.concatenate(cums, axis=0)
        g_ref[...] = gl + jnp.where((lane >= 8) & (lane < 12), pltpu.roll(cum, 4, 1), 0.0)

    vec = _full((1, d))
    return _call(
        body, comms, name="mlstm_proj_fwd", grid=(s // tile,),
        in_specs=[pl.BlockSpec((tile, d), lambda i: (i, 2)), _full(conv_w.shape), vec,
                  _full(w_q.shape), _full(w_k.shape), _full(w_v.shape), _full(wif.shape), _full((1, LANES))],
        out_specs=[pl.BlockSpec((tile, d), lambda i: (i, 0))] * 3 + [pl.BlockSpec((tile, LANES), lambda i: (i, 0))],
        out_shape=[jax.ShapeDtypeStruct((s, d), F32)] * 3 + [jax.ShapeDtypeStruct((s, LANES), F32)],
        scratch_shapes=[pltpu.VMEM((SUBLANES, d), F32)],
        args=(u, conv_w, conv_b, w_q, w_k, w_v, wif, bif))


def _cell_chunk(h, nh, q_ref, k_ref, v_ref, gc, gr, m_prev, c_h, n_h, m_t=None):
    lc = ML_CHUNK
    dh = q_ref.shape[1] // nh
    sl = slice(h * dh, (h + 1) * dh)
    qh = q_ref[:, sl]
    kh = k_ref[:, sl] * (dh ** -0.5)
    vh = v_ref[:, sl]
    li_c = _col(gc, h)
    b_c = _col(gc, 8 + h)
    lib_r = _row(gr, h) - _row(gr, 8 + h)
    b_last = _colsum(jnp.where(_iota((lc, 1), 0) == lc - 1, b_c, 0.0))
    causal = _iota((lc, lc), 1) <= _iota((lc, lc), 0)
    dmat = jnp.where(causal, b_c + lib_r, NEG_BIG)
    m_inter = b_c + m_prev
    if m_t is None:
        m_t = jnp.maximum(m_inter, jnp.max(dmat, axis=1, keepdims=True))
    w_intra = jnp.exp(dmat - m_t)
    w_inter = jnp.exp(m_inter - m_t)
    amat = _mm_nt(qh, kh)
    smat = amat * w_intra
    qc = _mm(qh, c_h)
    qn = _rowsum(qh * n_h)
    den = _rowsum(smat) + w_inter * qn
    gst = b_last - b_c + li_c
    m_new = jnp.maximum(b_last + m_prev, jnp.max(gst, axis=0, keepdims=True))
    w_state = jnp.exp(gst - m_new)
    decay = jnp.exp(b_last + m_prev - m_new)
    return dict(sl=sl, qh=qh, kh=kh, vh=vh, m_t=m_t, w_intra=w_intra, w_inter=w_inter, smat=smat, qc=qc, qn=qn,
                den=den, m_new=m_new, w_state=w_state, decay=decay)


def _ml_cell_fwd(q, k, v, gcol, grow, u, ng, nh, comms=None):
    s, d = q.shape
    lc = ML_CHUNK
    nc = s // lc
    dh = d // nh

    def body(q_ref, k_ref, v_ref, gc_ref, gr_ref, o_ref, z_ref, ng_ref,
             cell_ref, y_ref, cs_ref, ns_ref, ms_ref, mt_ref, c_sc, n_sc, m_sc):
        @pl.when(pl.program_id(0) == 0)
        def _():
            c_sc[...] = jnp.zeros_like(c_sc)
            n_sc[...] = jnp.zeros_like(n_sc)
            m_sc[...] = jnp.zeros_like(m_sc)

        gc = gc_ref[...]
        gr = gr_ref[...]
        lane = _iota((lc, LANES), 1)
        mt_acc = jnp.zeros((lc, LANES), F32)
        for h in range(nh):
            c_h = c_sc[h]
            n_h = n_sc[h, 0:1, :]
            m_prev = jnp.max(m_sc[h, 0:1, :], axis=1, keepdims=True)
            cs_ref[0, h] = c_h
            ns_ref[0, h] = n_sc[h]
            ms_ref[0, h] = m_sc[h]
            t = _cell_chunk(h, nh, q_ref, k_ref, v_ref, gc, gr, m_prev, c_h, n_h)
            sl = t["sl"]
            num = _mm(t["smat"], t["vh"]) + t["w_inter"] * t["qc"]
            cell_h = num / jnp.maximum(jnp.abs(t["den"]), jnp.exp(-t["m_t"]))
            mt_acc = jnp.where(lane == h, t["m_t"], mt_acc)
            kw = t["kh"] * t["w_state"]
            c_sc[h] = t["decay"] * c_h + _mm_tn(kw, t["vh"])
            n_sc[h] = _bcast8(t["decay"] * n_h + _colsum(kw))
            m_sc[h] = jnp.broadcast_to(t["m_new"], (SUBLANES, LANES))
            hg = _sigmoid(o_ref[:, sl]) * cell_h
            hn = hg * lax.rsqrt(jnp.mean(hg * hg, axis=1, keepdims=True) + EPS)
            z = z_ref[:, sl]
            cell_ref[:, sl] = cell_h
            y_ref[:, sl] = (hn * ng_ref[:, sl] * z * _sigmoid(z)).astype(BF16)
        mt_ref[...] = mt_acc

    tok = pl.BlockSpec((lc, d), lambda c: (c, 0))
    return _call(
        body, comms, name="mlstm_cell_fwd", grid=(nc,),
        in_specs=[tok, tok, tok, pl.BlockSpec((lc, LANES), lambda c: (c, 0)), pl.BlockSpec((16, lc), lambda c: (0, c)),
                  pl.BlockSpec((lc, d), lambda c: (c, 3)), pl.BlockSpec((lc, d), lambda c: (c, 4)), _full((1, d))],
        out_specs=[tok, tok, pl.BlockSpec((1, nh, dh, dh), lambda c: (c, 0, 0, 0)),
                   pl.BlockSpec((1, nh, SUBLANES, dh), lambda c: (c, 0, 0, 0)),
                   pl.BlockSpec((1, nh, SUBLANES, LANES), lambda c: (c, 0, 0, 0)),
                   pl.BlockSpec((lc, LANES), lambda c: (c, 0))],
        out_shape=[jax.ShapeDtypeStruct((s, d), F32), jax.ShapeDtypeStruct((s, d), BF16),
                   jax.ShapeDtypeStruct((nc, nh, dh, dh), F32), jax.ShapeDtypeStruct((nc, nh, SUBLANES, dh), F32),
                   jax.ShapeDtypeStruct((nc, nh, SUBLANES, LANES), F32), jax.ShapeDtypeStruct((s, LANES), F32)],
        scratch_shapes=[pltpu.VMEM((nh, dh, dh), F32), pltpu.VMEM((nh, SUBLANES, dh), F32),
                        pltpu.VMEM((nh, SUBLANES, LANES), F32)],
        args=(q, k, v, gcol, grow, u, u, ng))


def _out_fwd(x, y_rg, y_ml, gate, w_out_g, layer, tile, comms=None):
    s, d = x.shape
    nd, _, r, _ = w_out_g.shape

    def body(x_ref, yr_ref, ym_ref, g_ref, w_ref, xn_ref, y_ref):
        ycat = jnp.concatenate([yr_ref[...].astype(BF16), ym_ref[...].astype(BF16)], axis=1)
        acc = jnp.dot(ycat, w_ref[...].reshape(nd * r, d), preferred_element_type=F32)
        y_ref[...] = acc
        xn_ref[...] = x_ref[...] + g_ref[...] * acc

    tok = pl.BlockSpec((tile, d), lambda i: (i, 0))
    return _call(
        body, comms, name="out_proj_fwd", grid=(s // tile,),
        in_specs=[tok, tok, tok, _full((1, d)), pl.BlockSpec((nd, 1, r, d), lambda i: (0, layer, 0, 0))],
        out_specs=[tok, tok],
        out_shape=[jax.ShapeDtypeStruct((s, d), F32)] * 2,
        args=(x, y_rg, y_ml, gate, w_out_g))


def _loss_call(x, fg, target, tile):
    s, d = x.shape

    def body(x_ref, g_ref, t_ref, dx_ref, loss_ref, gg_ref):
        @pl.when(pl.program_id(0) == 0)
        def _():
            loss_ref[...] = jnp.zeros_like(loss_ref)
            gg_ref[...] = jnp.zeros_like(gg_ref)

        xv = x_ref[...]
        g = g_ref[...]
        rs = lax.rsqrt(jnp.mean(xv * xv, axis=1, keepdims=True) + EPS)
        xh = xv * rs
        e = xh * g - t_ref[...]
        loss_ref[...] += jnp.broadcast_to(_colsum(_rowsum(e * e)) * (0.5 / d), loss_ref.shape)
        dy = e * (1.0 / d)
        gg_ref[...] += _bcast8(_colsum(dy * xh))
        dxh = dy * g
        dx_ref[...] = rs * (dxh - xh * jnp.mean(dxh * xh, axis=1, keepdims=True))

    tok = pl.BlockSpec((tile, d), lambda i: (i, 0))
    return pl.pallas_call(
        body, name="final_norm_loss", grid=(s // tile,),
        in_specs=[tok, _full((1, d)), tok],
        out_specs=[tok, _full((SUBLANES, LANES)), _full((SUBLANES, d))],
        out_shape=[jax.ShapeDtypeStruct((s, d), F32), jax.ShapeDtypeStruct((SUBLANES, LANES), F32),
                   jax.ShapeDtypeStruct((SUBLANES, d), F32)],
        compiler_params=_params(1),
    )(x, fg, target)


def _out_bwd(dxo, gate, y, y_rg, y_ml, w_out_g, layer, tile, comms=None):
    s, d = dxo.shape
    nd, _, r, _ = w_out_g.shape

    def body(dx_ref, g_ref, y_ref, yr_ref, ym_ref, w_ref, dyr_ref, dym_ref, gw_ref, dg_ref):
        @pl.when(pl.program_id(0) == 0)
        def _():
            gw_ref[...] = jnp.zeros_like(gw_ref)
            dg_ref[...] = jnp.zeros_like(dg_ref)

        dxv = dx_ref[...]
        dg_ref[...] += _bcast8(_colsum(dxv * y_ref[...]))
        dyb = (dxv * g_ref[...]).astype(BF16)
        dycat = lax.dot_general(dyb, w_ref[...].reshape(nd * r, d), (((1,), (1,)), ((), ())), preferred_element_type=F32)
        dyr_ref[...] = dycat[:, 0:d]
        dym_ref[...] = dycat[:, d:2 * d]
        ycat = jnp.concatenate([yr_ref[...].astype(BF16), ym_ref[...].astype(BF16)], axis=1)
        gw_ref[...] += lax.dot_general(ycat, dyb, (((0,), (0,)), ((), ())), preferred_element_type=F32).reshape(nd, r, d)

    tok = pl.BlockSpec((tile, d), lambda i: (i, 0))
    return _call(
        body, comms, name="out_proj_bwd", grid=(s // tile,),
        in_specs=[tok, _full((1, d)), tok, tok, tok, pl.BlockSpec((nd, 1, r, d), lambda i: (0, layer, 0, 0))],
        out_specs=[tok, tok, _full((nd, r, d)), _full((SUBLANES, d))],
        out_shape=[jax.ShapeDtypeStruct((s, d), F32)] * 2 + [jax.ShapeDtypeStruct((nd, r, d), F32),
                                                             jax.ShapeDtypeStruct((SUBLANES, d), F32)],
        args=(dxo, gate, y, y_rg, y_ml, w_out_g))


def _ml_cell_bwd(dy_ml, u, cell, q, k, v, gcol, grow, mt, cs, ns, ms, ng, nh, comms=None):
    s, d = q.shape
    lc = ML_CHUNK
    nc = s // lc
    dh = d // nh

    def body(dy_ref, o_ref, z_ref, cell_ref, q_ref, k_ref, v_ref, gc_ref, gr_ref, mt_ref, cs_ref, ns_ref, ms_ref,
             ng_ref, dq_ref, dk_ref, dv_ref, dg_ref, do_ref, dz_ref, gng_ref, dc_sc, dn_sc):
        @pl.when(pl.program_id(0) == 0)
        def _():
            dc_sc[...] = jnp.zeros_like(dc_sc)
            dn_sc[...] = jnp.zeros_like(dn_sc)
            gng_ref[...] = jnp.zeros_like(gng_ref)

        gc = gc_ref[...]
        gr = gr_ref[...]
        mtv = mt_ref[...]
        lane = _iota((lc, LANES), 1)
        rowv = _iota((lc, 1), 0)
        dg_acc = jnp.zeros((lc, LANES), F32)
        for h in range(nh):
            c_h = cs_ref[0, h]
            n_h = ns_ref[0, h, 0:1, :]
            m_prev = jnp.max(ms_ref[0, h, 0:1, :], axis=1, keepdims=True)
            t = _cell_chunk(h, nh, q_ref, k_ref, v_ref, gc, gr, m_prev, c_h, n_h, m_t=_col(mtv, h))
            sl, qh, kh, vh = t["sl"], t["qh"], t["kh"], t["vh"]
            w_intra, w_inter, smat, w_state, decay = t["w_intra"], t["w_inter"], t["smat"], t["w_state"], t["decay"]
            cell_h = cell_ref[:, sl]
            o = o_ref[:, sl]
            z = z_ref[:, sl]
            dyv = dy_ref[:, sl]
            ngh = ng_ref[:, sl]
            so = _sigmoid(o)
            hg = so * cell_h
            rinv = lax.rsqrt(jnp.mean(hg * hg, axis=1, keepdims=True) + EPS)
            hn = hg * rinv
            sz = _sigmoid(z)
            dz_ref[:, sl] = (dyv * hn * ngh * (sz + z * sz * (1.0 - sz))).astype(BF16)
            dymid = dyv * z * sz
            gng_ref[:, sl] += _bcast8(_colsum(dymid * hn))
            dhn = dymid * ngh
            dhg = rinv * (dhn - hn * jnp.mean(dhn * hn, axis=1, keepdims=True))
            do_ref[:, sl] = (dhg * cell_h * so * (1.0 - so)).astype(BF16)
            dcell = dhg * so
            eneg = jnp.exp(-t["m_t"])
            aden = jnp.abs(t["den"])
            nst = jnp.maximum(aden, eneg)
            dnum = dcell / nst
            dden = jnp.where(aden > eneg, -_rowsum(cell_h * dcell) / nst * jnp.sign(t["den"]), 0.0)
            pmat = _mm_nt(dnum, vh) + dden
            damat = pmat * w_intra
            gmat = pmat * smat
            wdn = w_inter * dnum
            wdd = w_inter * dden
            dqh = _mm(damat, kh) + _mm_nt(wdn, c_h) + wdd * n_h
            dkh = _mm_tn(damat, qh)
            dvh = _mm_tn(smat, dnum)
            dw_inter = _rowsum(dnum * t["qc"]) + dden * t["qn"]
            dcn = dc_sc[h]
            dnn = dn_sc[h, 0:1, :]
            kw = kh * w_state
            dkw = _mm_nt(vh, dcn) + dnn
            dvh = dvh + _mm(kw, dcn)
            dkh = dkh + dkw * w_state
            dgst = _rowsum(dkw * kh) * w_state
            ddecay = _colsum(_rowsum(dcn * c_h)) + _rowsum(dnn * n_h)
            db_last = _colsum(dgst) + ddecay * decay
            rs_g = _rowsum(gmat)
            cs_g = _rowsum(gmat.T)
            db = rs_g - cs_g + dw_inter * w_inter - dgst + jnp.where(rowv == lc - 1, db_last, 0.0)
            dli = cs_g + dgst
            dc_sc[h] = decay * dcn + _mm_tn(qh, wdn)
            dn_sc[h] = _bcast8(decay * dnn + _colsum(qh * wdd))
            dq_ref[:, sl] = dqh
            dk_ref[:, sl] = dkh * (dh ** -0.5)
            dv_ref[:, sl] = dvh
            dg_acc = jnp.where(lane == h, dli, jnp.where(lane == 4 + h, db, dg_acc))
        dg_ref[...] = dg_acc

    rev = lambda c: nc - 1 - c
    tok = pl.BlockSpec((lc, d), lambda c: (rev(c), 0))
    g128 = pl.BlockSpec((lc, LANES), lambda c: (rev(c), 0))
    return _call(
        body, comms, name="mlstm_cell_bwd", grid=(nc,),
        in_specs=[tok, pl.BlockSpec((lc, d), lambda c: (rev(c), 3)), pl.BlockSpec((lc, d), lambda c: (rev(c), 4)),
                  tok, tok, tok, tok, g128, pl.BlockSpec((16, lc), lambda c: (0, rev(c))), g128,
                  pl.BlockSpec((1, nh, dh, dh), lambda c: (rev(c), 0, 0, 0)),
                  pl.BlockSpec((1, nh, SUBLANES, dh), lambda c: (rev(c), 0, 0, 0)),
                  pl.BlockSpec((1, nh, SUBLANES, LANES), lambda c: (rev(c), 0, 0, 0)), _full((1, d))],
        out_specs=[tok, tok, tok, g128, tok, tok, _full((SUBLANES, d))],
        out_shape=[jax.ShapeDtypeStruct((s, d), F32)] * 3 + [jax.ShapeDtypeStruct((s, LANES), F32)]
        + [jax.ShapeDtypeStruct((s, d), BF16)] * 2 + [jax.ShapeDtypeStruct((SUBLANES, d), F32)],
        scratch_shapes=[pltpu.VMEM((nh, dh, dh), F32), pltpu.VMEM((nh, SUBLANES, dh), F32)],
        args=(dy_ml, u, u, cell, q, k, v, gcol, grow, mt, cs, ns, ms, ng))


def _halo_spec(d, tile, nt, col):
    per = tile // SUBLANES
    return pl.BlockSpec((SUBLANES, d), lambda i: (jnp.maximum((nt - 1 - i) * per - 1, 0), col))


def _ml_pre_bwd(dq, dk, dv, dgates, gcol, u, q, k, v, conv_w, conv_b, w_q, w_k, w_v, wif_t, tile, comms=None):
    s, d = dq.shape
    nt = s // tile
    nh, dh, _ = w_q.shape

    def body(dq_ref, dk_ref, dv_ref, dg_ref, gc_ref, x_ref, halo_ref, q_ref, k_ref, v_ref, cw_ref, cb_ref,
             wq_ref, wk_ref, wv_ref, wift_ref,
             dx_ref, gwq_ref, gwk_ref, gwv_ref, gwif_ref, gbif_ref, gcw_ref, gcb_ref, next8):
        i = pl.program_id(0)

        @pl.when(i == 0)
        def _():
            next8[...] = jnp.zeros_like(next8)
            for ref in (gwq_ref, gwk_ref, gwv_ref, gwif_ref, gbif_ref, gcw_ref, gcb_ref):
                ref[...] = jnp.zeros_like(ref)

        x = x_ref[...]
        halo = halo_ref[...] * jnp.where(i < nt - 1, 1.0, 0.0)
        pre = _conv(x, halo, cw_ref) + cb_ref[...]
        sg = _sigmoid(pre)
        xc = pre * sg
        dgc = dg_ref[...]
        lane = _iota(dgc.shape, 1)
        utri = jnp.where(_iota((ML_CHUNK, ML_CHUNK), 0) <= _iota((ML_CHUNK, ML_CHUNK), 1), 1.0, 0.0)
        rcs = [_mm_hi(utri, dgc[c * ML_CHUNK:(c + 1) * ML_CHUNK, :]) for c in range(tile // ML_CHUNK)]
        rc = rcs[0] if len(rcs) == 1 else jnp.concatenate(rcs, axis=0)
        dgates_v = jnp.where(lane < 4, dgc, jnp.where(lane < 8, rc * (1.0 - jnp.exp(gc_ref[...])), 0.0))
        dgb = dgates_v.astype(BF16)
        gbif_ref[...] += jnp.broadcast_to(_colsum(dgates_v), gbif_ref.shape)
        ext = jnp.dot(dgb, wift_ref[...], preferred_element_type=F32)
        dqt = dq_ref[...] + ext[:, 0:d]
        dkt = dk_ref[...] + ext[:, d:2 * d]
        dvt = dv_ref[...] + ext[:, 2 * d:3 * d]
        gwif_ref[:, 0:d] += _mm_tn(dgb, q_ref[...])
        gwif_ref[:, d:2 * d] += _mm_tn(dgb, k_ref[...])
        gwif_ref[:, 2 * d:3 * d] += _mm_tn(dgb, v_ref[...])
        dxc_parts, dxv_parts = [], []
        for h in range(nh):
            sl = slice(h * dh, (h + 1) * dh)
            gwq_ref[h] += _mm_tn(xc[:, sl], dqt[:, sl])
            gwk_ref[h] += _mm_tn(xc[:, sl], dkt[:, sl])
            gwv_ref[h] += _mm_tn(x[:, sl], dvt[:, sl])
            dxc_parts.append(_mm_nt(dqt[:, sl], wq_ref[h]) + _mm_nt(dkt[:, sl], wk_ref[h]))
            dxv_parts.append(_mm_nt(dvt[:, sl], wv_ref[h]))
        dxc = jnp.concatenate(dxc_parts, axis=1)
        dxv = jnp.concatenate(dxv_parts, axis=1)
        dpre = dxc * (sg + pre * sg * (1.0 - sg))
        gcb_ref[...] += _bcast8(_colsum(dpre))
        for kk in range(CONV_WIDTH):
            gcw_ref[kk:kk + 1, :] += _colsum(dpre * _shift_down(x, CONV_WIDTH - 1 - kk, halo))
        dx_ref[...] = (dxv + _conv_bwd_x(dpre, next8[...], cw_ref)).astype(BF16)
        next8[...] = dpre[0:SUBLANES, :]

    rev = lambda i: nt - 1 - i
    tok = pl.BlockSpec((tile, d), lambda i: (rev(i), 0))
    g128 = pl.BlockSpec((tile, LANES), lambda i: (rev(i), 0))
    wsh = (nh, dh, dh)
    return _call(
        body, comms, name="mlstm_proj_bwd", grid=(nt,),
        in_specs=[tok, tok, tok, g128, g128, pl.BlockSpec((tile, d), lambda i: (rev(i), 2)), _halo_spec(d, tile, nt, 2),
                  tok, tok, tok, _full(conv_w.shape), _full((1, d)), _full(wsh), _full(wsh), _full(wsh),
                  _full(wif_t.shape)],
        out_specs=[tok, _full(wsh), _full(wsh), _full(wsh), _full((LANES, 3 * d)), _full((SUBLANES, LANES)),
                   _full((SUBLANES, d)), _full((SUBLANES, d))],
        out_shape=[jax.ShapeDtypeStruct((s, d), BF16)] + [jax.ShapeDtypeStruct(wsh, F32)] * 3
        + [jax.ShapeDtypeStruct((LANES, 3 * d), F32), jax.ShapeDtypeStruct((SUBLANES, LANES), F32),
           jax.ShapeDtypeStruct((SUBLANES, d), F32), jax.ShapeDtypeStruct((SUBLANES, d), F32)],
        scratch_shapes=[pltpu.VMEM((SUBLANES, d), F32)],
        args=(dq, dk, dv, dgates, gcol, u, u, q, k, v, conv_w, conv_b, w_q, w_k, w_v, wif_t))


def _rg_bwd(dy_rg, u, h_rg, gates, conv_w, w_a, w_x, lam, tile, comms=None):
    s, d = dy_rg.shape
    nt = s // tile
    nh, dh, _ = w_a.shape

    def body(dy_ref, x_ref, xhalo_ref, z_ref, h_ref, hhalo_ref, xc_ref, r_ref, i_ref, a_ref, beta_ref, cw_ref, wa_ref,
             wx_ref, lam_ref,
             dx_ref, dz_ref, gwa_ref, gwx_ref, gba_ref, gbx_ref, glam_ref, gcw_ref, gcb_ref, next8, anext, dnext, dbuf):
        i = pl.program_id(0)

        @pl.when(i == 0)
        def _():
            for ref in (next8, anext, dnext, gwa_ref, gwx_ref, gba_ref, gbx_ref, glam_ref, gcw_ref, gcb_ref):
                ref[...] = jnp.zeros_like(ref)

        inner = jnp.where(i < nt - 1, 1.0, 0.0)
        x = x_ref[...]
        halo = xhalo_ref[...] * inner
        xc, r, ig, a, beta = xc_ref[...], r_ref[...], i_ref[...], a_ref[...], beta_ref[...]
        sp = _softplus(-lam_ref[...])
        h = h_ref[...]
        row = _iota(h.shape, 0)
        hprev = jnp.where(row >= 1, pltpu.roll(h, 1, 0), hhalo_ref[SUBLANES - 1:SUBLANES, :] * inner)
        z = z_ref[...]
        sz = _sigmoid(z)
        dyv = dy_ref[...]
        dz_ref[...] = (dyv * h * (sz + z * sz * (1.0 - sz))).astype(BF16)
        a_up = jnp.where(row < tile - 1, pltpu.roll(a, tile - 1, 0), anext[0:1, :])
        _scan_into(a_up, dyv * z * sz, dnext[0:1, :], dbuf, True)
        delta = dbuf[...]
        anext[...] = a[0:SUBLANES, :]
        dnext[...] = delta[0:SUBLANES, :]
        dla = delta * hprev * a - delta * ig * xc * (a * a / beta)
        glam_ref[...] += _bcast8(_colsum(dla * r) * (RG_C * _sigmoid(-lam_ref[...])))
        dpa = dla * (-RG_C * sp) * r * (1.0 - r)
        dpx = delta * beta * xc * ig * (1.0 - ig)
        gba_ref[...] += _bcast8(_colsum(dpa))
        gbx_ref[...] += _bcast8(_colsum(dpx))
        parts = []
        for hh in range(nh):
            sl = slice(hh * dh, (hh + 1) * dh)
            gwa_ref[hh] += _mm_tn(xc[:, sl], dpa[:, sl])
            gwx_ref[hh] += _mm_tn(xc[:, sl], dpx[:, sl])
            parts.append(_mm_nt(dpa[:, sl], wa_ref[hh]) + _mm_nt(dpx[:, sl], wx_ref[hh]))
        dxc = delta * beta * ig + jnp.concatenate(parts, axis=1)
        gcb_ref[...] += _bcast8(_colsum(dxc))
        for kk in range(CONV_WIDTH):
            gcw_ref[kk:kk + 1, :] += _colsum(dxc * _shift_down(x, CONV_WIDTH - 1 - kk, halo))
        dx_ref[...] = _conv_bwd_x(dxc, next8[...], cw_ref).astype(BF16)
        next8[...] = dxc[0:SUBLANES, :]

    rev = lambda i: nt - 1 - i
    tok = pl.BlockSpec((tile, d), lambda i: (rev(i), 0))
    vec = _full((1, d))
    acc = _full((SUBLANES, d))
    wsh = (nh, dh, dh)
    return _call(
        body, comms, name="rglru_bwd", grid=(nt,),
        in_specs=[tok, tok, _halo_spec(d, tile, nt, 0), pl.BlockSpec((tile, d), lambda i: (rev(i), 1)), tok,
                  _halo_spec(d, tile, nt, 0)] + [tok] * 5 + [_full(conv_w.shape), _full(wsh), _full(wsh), vec],
        out_specs=[tok, tok, _full(wsh), _full(wsh), acc, acc, acc, acc, acc],
        out_shape=[jax.ShapeDtypeStruct((s, d), BF16)] * 2 + [jax.ShapeDtypeStruct(wsh, F32)] * 2
        + [jax.ShapeDtypeStruct((SUBLANES, d), F32)] * 5,
        scratch_shapes=[pltpu.VMEM((SUBLANES, d), F32)] * 3 + [pltpu.VMEM((tile, d), F32)],
        args=(dy_rg, u, u, u, h_rg, h_rg, *gates, conv_w, w_a, w_x, lam))


def _segments(d, w, n_pieces, n_slots):
    bounds = sorted({k * d for k in range(n_pieces + 1)} | {j * w for j in range(n_slots + 1)})
    return [(lo // d, lo % d, lo // w, lo % w, hi - lo) for lo, hi in zip(bounds[:-1], bounds[1:])]


def _in_bwd(pieces, x, dxo, ng, scale, w_in_g, layer, tile, comms=None, tiles=None, prev=None):
    s, d = x.shape
    nd, _, _, w = w_in_g.shape
    segs = _segments(d, w, len(pieces), nd)
    first, count = tiles or (0, s // tile)
    n_p = len(pieces)

    def body(*refs):
        p_refs = refs[:n_p]
        x_ref, dxo_ref, ng_ref, sc_ref, w_ref = refs[n_p:n_p + 5]
        dx_ref, dsc_ref, dsh_ref, gng_ref, wcat = refs[-5:]
        _join_columns(w_ref, wcat)

        @pl.when(pl.program_id(0) == 0)
        def _():
            for k, ref in enumerate((dsc_ref, dsh_ref, gng_ref)):
                ref[...] = jnp.zeros_like(ref) if prev is None else refs[n_p + 6 + k][...]

        du = jnp.concatenate([p[...] for p in p_refs], axis=1)
        dh = lax.dot_general(du, wcat[...], (((1,), (1,)), ((), ())), preferred_element_type=F32)
        xv = x_ref[...]
        g = ng_ref[...]
        rs = lax.rsqrt(jnp.mean(xv * xv, axis=1, keepdims=True) + EPS)
        xh = xv * rs
        dsh_ref[...] += _bcast8(_colsum(dh))
        dsc_ref[...] += _bcast8(_colsum(dh * xh * g))
        dhn = dh * (1.0 + sc_ref[...])
        gng_ref[...] += _bcast8(_colsum(dhn * xh))
        dxh = dhn * g
        dx_ref[...] = dxo_ref[...] + rs * (dxh - xh * jnp.mean(dxh * xh, axis=1, keepdims=True))

    tok = pl.BlockSpec((tile, d), lambda i: (i + first, 0))
    vec = _full((1, d))
    acc = _full((SUBLANES, d))
    more_specs = [] if prev is None else [pl.BlockSpec(memory_space=pl.ANY), acc, acc, acc]
    return _call(
        body, comms, name="in_proj_bwd_x", grid=(count,),
        in_specs=[tok] * n_p + [tok, tok, vec, vec, pl.BlockSpec((nd, 1, d, w), lambda i: (0, layer, 0, 0),
                                                               pipeline_mode=pl.Buffered(1))] + more_specs,
        out_specs=[tok, acc, acc, acc],
        out_shape=[jax.ShapeDtypeStruct((s, d), F32)] + [jax.ShapeDtypeStruct((SUBLANES, d), F32)] * 3,
        scratch_shapes=[pltpu.VMEM((d, nd * w), BF16)],
        args=(*pieces, x, dxo, ng, scale, w_in_g) + (() if prev is None else tuple(prev)),
        aliases={} if prev is None else {n_p + 5: 0})


def _in_bwd_w(pieces, hbf, w, slots, tile, comms=None):
    s, d = hbf.shape
    nd_all = len(pieces) * d // w
    segs = [sg for sg in _segments(d, w, len(pieces), nd_all) if sg[2] in slots]

    def body(*refs):
        p_refs = refs[:len(pieces)]
        h_ref, gw_ref = refs[len(pieces):]

        @pl.when(pl.program_id(0) == 0)
        def _():
            gw_ref[...] = jnp.zeros_like(gw_ref)

        hv = h_ref[...]
        for (kk, a, j, b, width) in segs:
            gw_ref[j - slots[0], :, b:b + width] += _mm_tn(hv, p_refs[kk][:, a:a + width])

    tok = pl.BlockSpec((tile, d), lambda i: (i, 0))
    return _call(
        body, comms, name="in_proj_bwd_w", grid=(s // tile,),
        in_specs=[tok] * len(pieces) + [tok],
        out_specs=[pl.BlockSpec((len(slots), d, w), lambda i: (0, 0, 0), pipeline_mode=pl.Buffered(1))],
        out_shape=[jax.ShapeDtypeStruct((len(slots), d, w), F32)],
        args=(*pieces, hbf))[0]


def _ada_bwd_w(cact_col, dmod, nd):
    d = cact_col.shape[0]
    w = dmod.shape[1] // nd

    def body(c_ref, m_ref, o_ref):
        o_ref[0] = c_ref[...] * m_ref[...]

    return pl.pallas_call(
        body, name="adaln_bwd_w", grid=(nd,),
        in_specs=[_full((d, 1)), pl.BlockSpec((1, w), lambda j: (0, j))],
        out_specs=pl.BlockSpec((1, d, w), lambda j: (j, 0, 0)),
        out_shape=jax.ShapeDtypeStruct((nd, d, w), F32),
        compiler_params=_params(1),
    )(cact_col, dmod)


def _exchange(arrs, gather, name):
    return _run_comms([_exchange_comm(arrs, gather)], name)[0]


def _run_comms(comms, name):
    _call(lambda: None, comms, name=name, grid=(1,), in_specs=[], out_specs=[], out_shape=[], args=())
    return [cm.results for cm in comms]


def _exchange_comm(arrs, gather):
    n = len(arrs)
    per = N_DEV - 1

    def copies(ins, outs, sems):
        send_sems, recv_sems, local_sems = sems
        x, y, c = (lax.axis_index(ax) for ax in MESH_AXES)
        me = 4 * x + 2 * y + c
        sends, recvs = [], []
        for flip in range(1, N_DEV):
            px = x ^ ((flip >> 2) & 1)
            py = y ^ ((flip >> 1) & 1)
            pc = c ^ (flip & 1)
            peer = 4 * px + 2 * py + pc
            for kk in range(n):
                src = ins[kk] if gather else ins[kk].at[peer]
                sends.append(_remote(src, outs[kk].at[me], send_sems, recv_sems, kk * per + flip - 1, (px, py, pc)))
                recvs.append(_remote(src, outs[kk].at[peer], send_sems, recv_sems, kk * per + flip - 1, (px, py, pc)))
        local = [pltpu.make_async_copy(ins[kk] if gather else ins[kk].at[me], outs[kk].at[me], local_sems.at[kk])
                 for kk in range(n)]
        return local, sends, recvs

    def start(ins, outs, sems):
        local, sends, _ = copies(ins, outs, sems)
        for cp in sends + local:
            cp.start()

    def finish(ins, outs, sems):
        local, sends, recvs = copies(ins, outs, sems)
        for cp in recvs:
            cp.wait_recv()
        for cp in sends:
            cp.wait_send()
        for cp in local:
            cp.wait()

    return _Comm(arrs, [jax.ShapeDtypeStruct((N_DEV,) + a.shape if gather else a.shape, a.dtype) for a in arrs],
                 [pltpu.SemaphoreType.DMA((n * per,)), pltpu.SemaphoreType.DMA((n * per,)), pltpu.SemaphoreType.DMA((n,))],
                 start, finish)


def _mesh_place():
    x, y, c = (lax.axis_index(ax) for ax in MESH_AXES)
    return x, y, c, (x, y, 1 - c), [(1 - x, y), (x, 1 - y), (1 - x, 1 - y)]


def _remote(src, dst, send_sems, recv_sems, sem, to):
    return pltpu.make_async_remote_copy(src_ref=src, dst_ref=dst, send_sem=send_sems.at[sem], recv_sem=recv_sems.at[sem],
                                        device_id=to, device_id_type=pl.DeviceIdType.MESH)


def _gather_two_level(arrs, name):
    n = len(arrs)
    per = N_DEV - 1

    def body(*refs):
        ins, outs = refs[:n], refs[n:2 * n]
        send_sems, recv_sems, local_sems = refs[2 * n:]
        x, y, c, sibling, chips = _mesh_place()

        def copy(kk, j, block, to, src=None):
            dst = outs[kk].at[4 * block[0] + 2 * block[1] + block[2]]
            return _remote(dst if src is None else src, dst, send_sems, recv_sems, kk * per + j, to)

        me = (x, y, c)
        local = [pltpu.make_async_copy(ins[kk], outs[kk].at[4 * x + 2 * y + c], local_sems.at[kk]) for kk in range(n)]
        first = []
        for j, chip in enumerate(chips):
            first += [copy(kk, 1 + j, me, (*chip, c), src=ins[kk]) for kk in range(n)]
        first += [copy(kk, 0, me, sibling, src=ins[kk]) for kk in range(n)]
        for cp in first + local:
            cp.start()
        passed = []
        for j, chip in enumerate(chips):
            for kk in range(n):
                copy(kk, 1 + j, (*chip, c), me).wait_recv()
                passed.append(copy(kk, 4 + j, (*chip, c), sibling))
                passed[-1].start()
        for kk in range(n):
            copy(kk, 0, sibling, me).wait_recv()
        for j, chip in enumerate(chips):
            for kk in range(n):
                copy(kk, 4 + j, (*chip, 1 - c), me).wait_recv()
        for cp in first + passed:
            cp.wait_send()
        for cp in local:
            cp.wait()

    return pl.pallas_call(
        body, name=name,
        in_specs=[pl.BlockSpec(memory_space=pl.ANY)] * n, out_specs=[pl.BlockSpec(memory_space=pl.ANY)] * n,
        out_shape=[jax.ShapeDtypeStruct((N_DEV,) + a.shape, a.dtype) for a in arrs],
        scratch_shapes=[pltpu.SemaphoreType.DMA((n * per,)), pltpu.SemaphoreType.DMA((n * per,)),
                        pltpu.SemaphoreType.DMA((n,))],
    )(*arrs)


N_CHIPS = N_DEV // 2


def _core_swap(arrs, name):
    n = len(arrs)

    def body(*refs):
        ins, outs = refs[:n], refs[n:2 * n]
        send_sems, recv_sems = refs[2 * n:]
        _, _, c, sibling, _ = _mesh_place()
        copies = [_remote(ins[kk].at[2 * q + (1 - c)], outs[kk].at[q], send_sems, recv_sems, kk * N_CHIPS + q, sibling)
                  for q in range(N_CHIPS) for kk in range(n)]
        for cp in copies:
            cp.start()
        for cp in copies:
            cp.wait_recv()
        for cp in copies:
            cp.wait_send()

    return pl.pallas_call(
        body, name=name,
        in_specs=[pl.BlockSpec(memory_space=pl.ANY)] * n, out_specs=[pl.BlockSpec(memory_space=pl.ANY)] * n,
        out_shape=[jax.ShapeDtypeStruct((N_CHIPS,) + a.shape[1:], a.dtype) for a in arrs],
        scratch_shapes=[pltpu.SemaphoreType.DMA((n * N_CHIPS,)), pltpu.SemaphoreType.DMA((n * N_CHIPS,))],
    )(*arrs)


def _pair_sum(a, other, parity, name):
    _, r, c = a.shape
    tr = _row_tile(r, c, 3)

    def body(p_ref, a_ref, o_ref, s_ref):
        s_ref[...] = (a_ref[...] + o_ref[...]).astype(BF16)

    return pl.pallas_call(
        body, name=name,
        grid_spec=pltpu.PrefetchScalarGridSpec(
            num_scalar_prefetch=1, grid=(N_CHIPS, r // tr),
            in_specs=[pl.BlockSpec((1, tr, c), lambda q, i, p: (2 * q + p[0], i, 0)),
                      pl.BlockSpec((1, tr, c), lambda q, i, p: (q, i, 0))],
            out_specs=pl.BlockSpec((1, tr, c), lambda q, i, p: (q, i, 0))),
        out_shape=jax.ShapeDtypeStruct((N_CHIPS, r, c), BF16),
        compiler_params=_params(2),
    )(parity, a, other)


def _chip_swap(arrs, name):
    n = len(arrs)
    per = N_CHIPS - 1

    def body(*refs):
        ins, outs = refs[:n], refs[n:2 * n]
        send_sems, recv_sems, local_sems = refs[2 * n:]
        x, y, c, _, chips = _mesh_place()
        mine = 2 * x + y
        sends = [_remote(ins[kk].at[2 * chip[0] + chip[1]], outs[kk].at[mine], send_sems, recv_sems, kk * per + j, (*chip, c))
                 for j, chip in enumerate(chips) for kk in range(n)]
        recvs = [_remote(ins[kk].at[mine], outs[kk].at[2 * chip[0] + chip[1]], send_sems, recv_sems, kk * per + j, (*chip, c))
                 for j, chip in enumerate(chips) for kk in range(n)]
        local = [pltpu.make_async_copy(ins[kk].at[mine], outs[kk].at[mine], local_sems.at[kk]) for kk in range(n)]
        for cp in sends + local:
            cp.start()
        for cp in recvs:
            cp.wait_recv()
        for cp in sends:
            cp.wait_send()
        for cp in local:
            cp.wait()

    return pl.pallas_call(
        body, name=name,
        in_specs=[pl.BlockSpec(memory_space=pl.ANY)] * n, out_specs=[pl.BlockSpec(memory_space=pl.ANY)] * n,
        out_shape=[jax.ShapeDtypeStruct(a.shape, a.dtype) for a in arrs],
        scratch_shapes=[pltpu.SemaphoreType.DMA((n * per,)), pltpu.SemaphoreType.DMA((n * per,)),
                        pltpu.SemaphoreType.DMA((n,))],
    )(*arrs)


def _adam_math(w, g, m, v):
    m = ADAM_B1 * m + (1.0 - ADAM_B1) * g
    v = ADAM_B2 * v + (1.0 - ADAM_B2) * (g * g)
    m_hat = m / (1.0 - ADAM_B1 ** ADAM_STEP)
    v_hat = v / (1.0 - ADAM_B2 ** ADAM_STEP)
    delta = -ADAM_LR * (m_hat / (jnp.sqrt(v_hat) + ADAM_EPS) + ADAM_WD * w)
    return delta, m, v


def _sum_devices(r_ref):
    acc = r_ref[0].astype(F32)
    for p in range(1, r_ref.shape[0]):
        acc = acc + r_ref[p].astype(F32)
    return acc


def _row_tile(rows, cols, n_bufs):
    budget = 24 * 1024 * 1024 // (n_bufs * 2 * cols * 4)
    t = rows
    while t > budget and t % 2 == 0 and (t // 2) % SUBLANES == 0:
        t //= 2
    return t


def _reduce_adam(recvs, w, m, v, name):
    nl, r, c = w.shape
    n_part = recvs[0].shape[0]
    tr = _row_tile(r, c, n_part * nl + 7)
    nt = r // tr

    def body(*refs):
        r_refs = refs[:nl]
        w_ref, m_ref, v_ref, g_ref, d_ref, mo_ref, vo_ref = refs[nl:]
        layer = pl.program_id(0)
        g = _sum_devices(r_refs[0])
        for ll in range(1, nl):
            g = jnp.where(layer == ll, _sum_devices(r_refs[ll]), g)
        delta, m2, v2 = _adam_math(w_ref[0], g, m_ref[0], v_ref[0])
        g_ref[0] = g
        d_ref[0] = delta
        mo_ref[0] = m2
        vo_ref[0] = v2

    def rspec(ll):
        return pl.BlockSpec((n_part, tr, c), lambda l, i: (0, jnp.where(l == ll, i, jnp.where(l < ll, 0, nt - 1)), 0))

    blk = pl.BlockSpec((1, tr, c), lambda l, i: (l, i, 0))
    return pl.pallas_call(
        body, name=name, grid=(nl, nt),
        in_specs=[rspec(ll) for ll in range(nl)] + [blk, blk, blk],
        out_specs=[blk] * 4,
        out_shape=[jax.ShapeDtypeStruct((nl, r, c), F32)] * 4,
        compiler_params=_params(2),
    )(*recvs, w, m, v)


def _sum8(recv, name):
    _, r, c = recv.shape

    def body(r_ref, o_ref):
        o_ref[...] = _sum_devices(r_ref)

    return pl.pallas_call(
        body, name=name, grid=(1,),
        in_specs=[_full(recv.shape)], out_specs=_full((r, c)),
        out_shape=jax.ShapeDtypeStruct((r, c), F32), compiler_params=_params(1),
    )(recv)


def _adam_call(w, g, m, v, name):
    r, c = w.shape

    def body(w_ref, g_ref, m_ref, v_ref, d_ref, mo_ref, vo_ref):
        delta, m2, v2 = _adam_math(w_ref[...], g_ref[...], m_ref[...], v_ref[...])
        d_ref[...] = delta
        mo_ref[...] = m2
        vo_ref[...] = v2

    return pl.pallas_call(
        body, name=name, grid=(1,),
        in_specs=[_full((r, c))] * 4, out_specs=[_full((r, c))] * 3,
        out_shape=[jax.ShapeDtypeStruct((r, c), F32)] * 3, compiler_params=_params(1),
    )(w, g, m, v)


def _tile_for(s, want):
    return min(want, s)


def _local_step_whole(x, c, target, wts):
    s, d = x.shape
    nl = wts["w_in_g"].shape[1]
    nd = wts["w_in_g"].shape[0]
    nh_ml = wts["w_qkv"].shape[2]
    t_big = _tile_for(s, 512)
    t_mid = _tile_for(s, 256)

    mod, cact = _mod_call(c, wts["w_ada_g"], wts["b_ada"])
    row = lambda a: a.reshape(1, -1)
    saved = []
    xl = x
    for l in range(nl):
        shift, scale, gate = (row(mod[l, kk * d:(kk + 1) * d]) for kk in range(3))
        u, hbf = _in_fwd(xl, row(wts["norm_g"][l]), scale, shift, wts["w_in_g"], l, t_mid)
        h_rg, y_rg = _rg_fwd(u, d, wts["rg_conv_w"][l], row(wts["rg_conv_b"][l]), wts["rg_w_a_bf"][l],
                             row(wts["rg_b_a"][l]), wts["rg_w_x_bf"][l], row(wts["rg_b_x"][l]),
                             row(wts["rg_lambda"][l]), t_mid)
        q, k, v, gcol = _ml_pre(u, d, wts["ml_conv_w"][l], row(wts["ml_conv_b"][l]), wts["w_qkv"][l, 0],
                                wts["w_qkv"][l, 1], wts["w_qkv"][l, 2], wts["wif_pad"][l], wts["bif_pad"][l], t_mid)
        grow = gcol[:, 0:16].T
        cell, y_ml, cs, ns, ms, mt = _ml_cell_fwd(q, k, v, gcol, grow, u, row(wts["ml_norm_g"][l]), nh_ml)
        x_new, y = _out_fwd(xl, y_rg, y_ml, gate, wts["w_out_g"], l, t_big)
        saved.append(dict(x=xl, u=u, hbf=hbf, h_rg=h_rg, y_rg=y_rg, q=q, k=k, v=v, gcol=gcol, grow=grow, cell=cell,
                          y_ml=y_ml, cs=cs, ns=ns, ms=ms, mt=mt, y=y, scale=scale, gate=gate))
        xl = x_new

    dx, loss_p, g_final = _loss_call(xl, row(wts["final_g"]), target, t_big)
    grads = [None] * nl
    cact_col = cact[0].reshape(d, 1)
    for l in reversed(range(nl)):
        sv = saved[l]
        dy_rg, dy_ml, gw_out, dgate = _out_bwd(dx, sv["gate"], sv["y"], sv["y_rg"], sv["y_ml"], wts["w_out_g"], l, t_big)
        dq, dk, dv, dgates, d_mlo, d_mlz, g_mlng = _ml_cell_bwd(
            dy_ml, sv["u"], sv["cell"], sv["q"], sv["k"], sv["v"], sv["gcol"], sv["grow"], sv["mt"], sv["cs"],
            sv["ns"], sv["ms"], row(wts["ml_norm_g"][l]), nh_ml)
        d_mlx, g_wq, g_wk, g_wv, g_wift, g_bif, g_mlcw, g_mlcb = _ml_pre_bwd(
            dq, dk, dv, dgates, sv["gcol"], sv["u"], sv["q"], sv["k"], sv["v"], wts["ml_conv_w"][l],
            row(wts["ml_conv_b"][l]), wts["w_qkv"][l, 0], wts["w_qkv"][l, 1], wts["w_qkv"][l, 2], wts["wift_pad"][l], t_mid)
        d_rgx, d_rgz, g_wa, g_wx, g_ba, g_bx, g_lam, g_rgcw, g_rgcb = _rg_bwd(
            dy_rg, sv["u"], sv["h_rg"], wts["rg_conv_w"][l], row(wts["rg_conv_b"][l]), wts["rg_w_a_bf"][l],
            row(wts["rg_b_a"][l]), wts["rg_w_x_bf"][l], row(wts["rg_b_x"][l]), row(wts["rg_lambda"][l]), t_mid)
        pieces = [d_rgx, d_rgz, d_mlx, d_mlo, d_mlz]
        dx, dscale, dshift, g_ng = _in_bwd(pieces, sv["x"], dx, row(wts["norm_g"][l]), sv["scale"], wts["w_in_g"], l, t_mid)
        half = nd // 2
        w_cols = wts["w_in_g"].shape[3]
        gw_in = jnp.concatenate([_in_bwd_w(pieces, sv["hbf"], w_cols, tuple(range(0, half)), t_big),
                                 _in_bwd_w(pieces, sv["hbf"], w_cols, tuple(range(half, nd)), t_big)], axis=0)
        dmod = jnp.concatenate([dshift[0:1], dscale[0:1], dgate[0:1]], axis=1)
        gw_ada = _ada_bwd_w(cact_col, dmod, nd)
        grads[l] = dict(w_ada=gw_ada, w_in=gw_in, w_out=gw_out, w_qkv=jnp.stack([g_wq, g_wk, g_wv]),
                        rg_conv_w=g_rgcw[0:CONV_WIDTH], ml_conv_w=g_mlcw[0:CONV_WIDTH], wif_t=g_wift[0:8],
                        norm_g=g_ng[0], b_ada=dmod[0], rg_conv_b=g_rgcb[0], rg_w_a=g_wa, rg_b_a=g_ba[0], rg_w_x=g_wx,
                        rg_b_x=g_bx[0], rg_lambda=g_lam[0], ml_conv_b=g_mlcb[0], ml_b_if=g_bif[0, 0:8],
                        ml_norm_g=g_mlng[0])
    return loss_p[0, 0], dx, grads, g_final[0]


REPLICATED = ("norm_g", "b_ada", "rg_conv_b", "rg_w_a", "rg_b_a", "rg_w_x", "rg_b_x", "rg_lambda", "ml_conv_b",
              "ml_b_if", "ml_norm_g", "final_g")
ROW_ALIGN = N_DEV * SUBLANES


def _to_rows(a):
    flat = a.reshape(-1)
    pad = (-flat.shape[0]) % LANES
    return jnp.pad(flat, (0, pad)).reshape(-1, LANES)


def _pack(arrays):
    rows = jnp.concatenate([_to_rows(a) for a in arrays], axis=0)
    return jnp.pad(rows, ((0, (-rows.shape[0]) % ROW_ALIGN), (0, 0)))


def _unpack(rows, like):
    out, at = [], 0
    for a in like:
        n = -(-a.size // LANES)
        out.append(rows[at:at + n].reshape(-1)[:a.size].reshape(a.shape))
        at += n
    return out


def _small_pack(rg_conv_w, ml_conv_w, ml_w_if):
    nl = rg_conv_w.shape[0]
    wif_t = jnp.swapaxes(ml_w_if, 1, 2).reshape(nl, -1, LANES)
    return jnp.concatenate([rg_conv_w, ml_conv_w, wif_t], axis=1)


def _small_unpack(p, if_rows):
    nl = p.shape[0]
    rg_cw = p[:, 0:CONV_WIDTH]
    ml_cw = p[:, CONV_WIDTH:2 * CONV_WIDTH]
    wif = jnp.swapaxes(p[:, 2 * CONV_WIDTH:].reshape(nl, 8, if_rows), 1, 2)
    return rg_cw, ml_cw, wif


def _assemble_weights(big, small, rep):
    w_ada_g, w_in_g, w_out_g, qkv_g = big
    nd, nl = small.shape[0], small.shape[1]
    d = w_in_g.shape[2]
    dh = qkv_g.shape[3]
    nh = d // dh
    rsh = qkv_g.shape[2] // (3 * nh)
    w_qkv = qkv_g.reshape(nd, nl, 3, nh, rsh, dh).transpose(1, 2, 3, 0, 4, 5).reshape(nl, 3, nh, nd * rsh, dh)
    cw = small[:, :, 0:2 * CONV_WIDTH].reshape(nd, nl, 2, CONV_WIDTH, LANES).transpose(1, 2, 3, 0, 4)
    cw = cw.reshape(nl, 2, CONV_WIDTH, nd * LANES)
    if_rows = (small.shape[2] - 2 * CONV_WIDTH) * LANES // 8
    wif_t = small[:, :, 2 * CONV_WIDTH:].reshape(nd, nl, 8, if_rows).transpose(1, 2, 0, 3).reshape(nl, 8, nd * if_rows)
    wift_pad = jnp.pad(wif_t, ((0, 0), (0, LANES - 8), (0, 0))).astype(BF16)
    wif_pad = jnp.swapaxes(wift_pad, 1, 2)
    bif_pad = jnp.pad(rep["ml_b_if"], ((0, 0), (0, LANES - 8))).reshape(nl, 1, LANES)
    wts = dict(rep)
    wts.update(w_ada_g=w_ada_g, w_in_g=w_in_g, w_out_g=w_out_g, w_qkv=w_qkv, rg_conv_w=cw[:, 0], ml_conv_w=cw[:, 1],
               wif_pad=wif_pad, wift_pad=wift_pad, bif_pad=bif_pad, rg_w_a_bf=rep["rg_w_a"].astype(BF16),
               rg_w_x_bf=rep["rg_w_x"].astype(BF16))
    return wts


def _qkv_slots(g_qkv, nd):
    three, nh, dh, _ = g_qkv.shape
    return g_qkv.reshape(three, nh, nd, dh // nd, dh).transpose(2, 0, 1, 3, 4).reshape(nd, three * nh * (dh // nd), dh)


def _small_slots(g):
    nd = N_DEV
    cw = jnp.stack([g["rg_conv_w"], g["ml_conv_w"]]).reshape(2, CONV_WIDTH, nd, LANES).transpose(2, 0, 1, 3)
    cw = cw.reshape(nd, 2 * CONV_WIDTH, LANES)
    wif = g["wif_t"].reshape(8, nd, -1).transpose(1, 0, 2).reshape(nd, -1, LANES)
    return jnp.concatenate([cw, wif], axis=1)


def _kernel_unhosted(x, c, norm_g, w_ada, b_ada, w_in, rg_conv_w, rg_conv_b, rg_w_a, rg_b_a, rg_w_x, rg_b_x, rg_lambda, ml_conv_w, ml_conv_b, ml_w_q, ml_w_k, ml_w_v, ml_w_if, ml_b_if, ml_norm_g, w_out, final_g, loss_target, m_norm_g, m_w_ada, m_b_ada, m_w_in, m_rg_conv_w, m_rg_conv_b, m_rg_w_a, m_rg_b_a, m_rg_w_x, m_rg_b_x, m_rg_lambda, m_ml_conv_w, m_ml_conv_b, m_ml_w_q, m_ml_w_k, m_ml_w_v, m_ml_w_if, m_ml_b_if, m_ml_norm_g, m_w_out, m_final_g, v_norm_g, v_w_ada, v_b_ada, v_w_in, v_rg_conv_w, v_rg_conv_b, v_rg_w_a, v_rg_b_a, v_rg_w_x, v_rg_b_x, v_rg_lambda, v_ml_conv_w, v_ml_conv_b, v_ml_w_q, v_ml_w_k, v_ml_w_v, v_ml_w_if, v_ml_b_if, v_ml_norm_g, v_w_out, v_final_g):
    given = dict(locals())
    nl = w_in.shape[0]
    rep = {n: given[n] for n in REPLICATED}

    def qkv_shard(prefix):
        return jnp.stack([given[prefix + "ml_w_q"], given[prefix + "ml_w_k"], given[prefix + "ml_w_v"]], axis=1).reshape(
            nl, -1, ml_w_q.shape[-1])

    *big, small = _gather_two_level(
        [w_ada.astype(BF16), w_in.astype(BF16), w_out.astype(BF16), qkv_shard("").astype(BF16),
         _small_pack(rg_conv_w, ml_conv_w, ml_w_if)], "gather_weights")
    wts = _assemble_weights(big, small, rep)

    loss_p, grad_x, grads, g_final = _local_step(x[0], c, loss_target[0], wts)
    loss = lax.psum(loss_p, MESH_AXES)

    keys = ("w_ada", "w_in", "w_out", "w_qkv", "small")
    parity = lax.axis_index("c").astype(jnp.int32).reshape(1)
    recv = []
    for l in range(nl):
        g = grads[l]
        parts = [g["w_ada"], g["w_in"], g["w_out"], _qkv_slots(g["w_qkv"], N_DEV), _small_slots(g)]
        other = _core_swap(parts, "core_swap_layer%d" % l)
        sums = [_pair_sum(a, o, parity, "pair_sum_%s_layer%d" % (key, l)) for key, a, o in zip(keys, parts, other)]
        recv.append(_chip_swap(sums, "chip_swap_layer%d" % l))
    shard = {"": dict(w_ada=w_ada, w_in=w_in, w_out=w_out, w_qkv=qkv_shard(""),
                      small=_small_pack(rg_conv_w, ml_conv_w, ml_w_if))}
    for p in ("m_", "v_"):
        shard[p] = dict(w_ada=given[p + "w_ada"], w_in=given[p + "w_in"], w_out=given[p + "w_out"], w_qkv=qkv_shard(p),
                        small=_small_pack(given[p + "rg_conv_w"], given[p + "ml_conv_w"], given[p + "ml_w_if"]))
    res = {}
    for ki, key in enumerate(keys):
        res[key] = _reduce_adam([recv[l][ki] for l in range(nl)], shard[""][key], shard["m_"][key], shard["v_"][key],
                                "reduce_adam_" + key)

    rep_g = dict(final_g=g_final)
    for n in REPLICATED[:-1]:
        rep_g[n] = jnp.stack([grads[l][n] for l in range(nl)])
    pack_g = _pack([rep_g[n] for n in REPLICATED])
    rows = pack_g.shape[0] // N_DEV
    mine = _sum8(_exchange([pack_g.reshape(N_DEV, rows, LANES)], False, "reduce_scatter_replicated")[0], "sum_replicated")
    g_rep = _exchange([mine], True, "gather_replicated")[0].reshape(N_DEV * rows, LANES)
    rep_like = [rep[n] for n in REPLICATED]
    d_rep, m_rep, v_rep = _adam_call(_pack(rep_like), g_rep, _pack([given["m_" + n] for n in REPLICATED]),
                                     _pack([given["v_" + n] for n in REPLICATED]), "adam_replicated")
    rep_out = [dict(zip(REPLICATED, _unpack(a, rep_like))) for a in (g_rep, d_rep, m_rep, v_rep)]

    if_rows = ml_w_if.shape[1]
    order = ("norm_g", "w_ada", "b_ada", "w_in", "rg_conv_w", "rg_conv_b", "rg_w_a", "rg_b_a", "rg_w_x", "rg_b_x",
             "rg_lambda", "ml_conv_w", "ml_conv_b", "ml_w_q", "ml_w_k", "ml_w_v", "ml_w_if", "ml_b_if", "ml_norm_g",
             "w_out", "final_g")
    outs = [loss, grad_x[None]]
    for kind in range(4):
        qkv = res["w_qkv"][kind].reshape((nl, 3) + ml_w_q.shape[1:])
        rg_cw, ml_cw, wif = _small_unpack(res["small"][kind], if_rows)
        sharded = dict(w_ada=res["w_ada"][kind], w_in=res["w_in"][kind], w_out=res["w_out"][kind], ml_w_q=qkv[:, 0],
                       ml_w_k=qkv[:, 1], ml_w_v=qkv[:, 2], rg_conv_w=rg_cw, ml_conv_w=ml_cw, ml_w_if=wif)
        for n in order:
            outs.append(sharded[n] if n in sharded else rep_out[kind][n])
    return tuple(outs)


def _slot(block):
    return 4 * block[0] + 2 * block[1] + block[2]


def _dma_sems(*counts):
    return [pltpu.SemaphoreType.DMA((n,)) for n in counts]


def _start_all(copies):
    for cp in copies:
        cp.start()


def _gather_ici_comm(arrs):
    n = len(arrs)

    def copies(ins, outs, sems):
        send_sems, recv_sems, local_sems = sems
        x, y, c, sibling, chips = _mesh_place()
        me = (x, y, c)
        peers = [(*chip, c) for chip in chips] + [sibling]
        local = [pltpu.make_async_copy(ins[kk], outs[kk].at[_slot(me)], local_sems.at[kk]) for kk in range(n)]
        sends = [_remote(ins[kk], outs[kk].at[_slot(me)], send_sems, recv_sems, kk * 4 + j, peer)
                 for j, peer in enumerate(peers) for kk in range(n)]
        recvs = [_remote(ins[kk], outs[kk].at[_slot(peer)], send_sems, recv_sems, kk * 4 + j, peer)
                 for j, peer in enumerate(peers) for kk in range(n)]
        return local, sends, recvs

    def start(ins, outs, sems):
        local, sends, _ = copies(ins, outs, sems)
        _start_all(sends + local)

    def finish(ins, outs, sems):
        local, sends, recvs = copies(ins, outs, sems)
        for cp in recvs:
            cp.wait_recv()
        for cp in sends:
            cp.wait_send()
        for cp in local:
            cp.wait()

    return _Comm(arrs, [jax.ShapeDtypeStruct((N_DEV,) + a.shape, a.dtype) for a in arrs], _dma_sems(4 * n, 4 * n, n),
                 start, finish)


def _gather_fwd_comm(bufs):
    n = len(bufs)

    def copies(ins, outs, sems):
        send_sems, recv_sems = sems
        _, _, c, sibling, chips = _mesh_place()
        sends = [_remote(ins[kk].at[_slot((*chip, c))], outs[kk].at[_slot((*chip, c))], send_sems, recv_sems, kk * 3 + j, sibling)
                 for j, chip in enumerate(chips) for kk in range(n)]
        recvs = [_remote(ins[kk].at[_slot((*chip, c))], outs[kk].at[_slot((*chip, 1 - c))], send_sems, recv_sems, kk * 3 + j, sibling)
                 for j, chip in enumerate(chips) for kk in range(n)]
        return sends, recvs

    def start(ins, outs, sems):
        _start_all(copies(ins, outs, sems)[0])

    def finish(ins, outs, sems):
        sends, recvs = copies(ins, outs, sems)
        for cp in recvs:
            cp.wait_recv()
        for cp in sends:
            cp.wait_send()

    return _Comm(bufs, [jax.ShapeDtypeStruct(a.shape, a.dtype) for a in bufs], _dma_sems(3 * n, 3 * n), start, finish,
                 aliases=[(i, i) for i in range(n)])


def _core_swap_comm(arrs):
    n = len(arrs)

    def copies(ins, outs, sems):
        send_sems, recv_sems = sems
        _, _, c, sibling, _ = _mesh_place()
        return [_remote(ins[kk].at[2 * q + (1 - c)], outs[kk].at[q], send_sems, recv_sems, kk * N_CHIPS + q, sibling)
                for q in range(N_CHIPS) for kk in range(n)]

    def start(ins, outs, sems):
        _start_all(copies(ins, outs, sems))

    def finish(ins, outs, sems):
        cps = copies(ins, outs, sems)
        for cp in cps:
            cp.wait_recv()
        for cp in cps:
            cp.wait_send()

    return _Comm(arrs, [jax.ShapeDtypeStruct((N_CHIPS,) + a.shape[1:], a.dtype) for a in arrs],
                 _dma_sems(N_CHIPS * n, N_CHIPS * n), start, finish)


def _chip_swap_comm(arrs):
    n = len(arrs)
    per = N_CHIPS - 1

    def copies(ins, outs, sems):
        send_sems, recv_sems, local_sems = sems
        x, y, c, _, chips = _mesh_place()
        mine = 2 * x + y
        sends = [_remote(ins[kk].at[2 * chip[0] + chip[1]], outs[kk].at[mine], send_sems, recv_sems, kk * per + j, (*chip, c))
                 for j, chip in enumerate(chips) for kk in range(n)]
        recvs = [_remote(ins[kk].at[mine], outs[kk].at[2 * chip[0] + chip[1]], send_sems, recv_sems, kk * per + j, (*chip, c))
                 for j, chip in enumerate(chips) for kk in range(n)]
        local = [pltpu.make_async_copy(ins[kk].at[mine], outs[kk].at[mine], local_sems.at[kk]) for kk in range(n)]
        return local, sends, recvs

    def start(ins, outs, sems):
        local, sends, _ = copies(ins, outs, sems)
        _start_all(sends + local)

    def finish(ins, outs, sems):
        local, sends, recvs = copies(ins, outs, sems)
        for cp in recvs:
            cp.wait_recv()
        for cp in sends:
            cp.wait_send()
        for cp in local:
            cp.wait()

    return _Comm(arrs, [jax.ShapeDtypeStruct(a.shape, a.dtype) for a in arrs], _dma_sems(per * n, per * n, n), start, finish)


def _ada_mod(c_all, w_ada, b_cols, comms=None):
    nl, d, w = w_ada.shape

    def body(c_ref, w_ref, b_ref, m_ref, ca_ref):
        sub = _iota((SUBLANES, d), 0)
        cv = jnp.zeros((SUBLANES, d), F32)
        for b in range(N_DEV):
            cv = jnp.where(sub == b, c_ref[b], cv)
        ca = cv * _sigmoid(cv)
        ca_ref[...] = ca
        m_ref[...] = jnp.zeros_like(m_ref)
        for l in range(nl):
            ml = _mm_hi(ca, w_ref[l]) + b_ref[l:l + 1, :]
            for b in range(N_DEV):
                m_ref[b, l:l + 1, :] = _row(ml, b)

    return _call(
        body, comms, name="adaln_mod_columns", grid=(1,),
        in_specs=[_full(c_all.shape), _full(w_ada.shape), _full(b_cols.shape)],
        out_specs=[_full((N_DEV, SUBLANES, w)), _full((SUBLANES, d))],
        out_shape=[jax.ShapeDtypeStruct((N_DEV, SUBLANES, w), F32), jax.ShapeDtypeStruct((SUBLANES, d), F32)],
        args=(c_all, w_ada, b_cols))


def _ada_grad_adam(cact_t, dmods, w, m, v):
    nl, d, wd = w.shape
    tr = _row_tile(d, wd, 8)

    def body(c_ref, dm_ref, w_ref, m_ref, v_ref, g_ref, d_ref, mo_ref, vo_ref):
        cv = c_ref[...]
        dm = dm_ref[0]
        g = _col(cv, 0) * _row(dm, 0)
        for b in range(1, N_DEV):
            g = g + _col(cv, b) * _row(dm, b)
        delta, m2, v2 = _adam_math(w_ref[0], g, m_ref[0], v_ref[0])
        g_ref[0] = g
        d_ref[0] = delta
        mo_ref[0] = m2
        vo_ref[0] = v2

    blk = pl.BlockSpec((1, tr, wd), lambda l, i: (l, i, 0))
    return pl.pallas_call(
        body, name="adaln_grad_adam", grid=(nl, d // tr),
        in_specs=[pl.BlockSpec((tr, N_DEV), lambda l, i: (i, 0)), pl.BlockSpec((1, N_DEV, wd), lambda l, i: (l, 0, 0)),
                  blk, blk, blk],
        out_specs=[blk] * 4, out_shape=[jax.ShapeDtypeStruct((nl, d, wd), F32)] * 4,
        compiler_params=_params(2),
    )(cact_t, dmods, w, m, v)


REP_ROWS = ("norm_g", "dshift", "dscale", "dgate", "rg_conv_b", "rg_b_a", "rg_b_x", "rg_lambda", "ml_conv_b", "ml_norm_g",
            "ml_b_if")


def _sum_parts(recvs, name):
    def body(*refs):
        for r_ref, o_ref in zip(refs[:len(recvs)], refs[len(recvs):]):
            o_ref[...] = _sum_devices(r_ref)

    return pl.pallas_call(
        body, name=name, grid=(1,),
        in_specs=[_full(r.shape) for r in recvs], out_specs=[_full(r.shape[1:]) for r in recvs],
        out_shape=[jax.ShapeDtypeStruct(r.shape[1:], F32) for r in recvs], compiler_params=_params(1),
    )(*recvs)


def _adam_replicated(vp, mp, params, nl):
    d = vp.shape[1]
    nr = len(REP_ROWS)
    names = list(params)
    mat_shape = params["rg_w_a"][0].shape[1:]
    mat_rows = mp.shape[0] // (2 * nl)

    def pieces(name):
        if name == "final_g":
            return [(lambda vp_ref, mp_ref: vp_ref[nl * nr:nl * nr + 1, :], (slice(0, 1), slice(None)))]
        out = []
        for l in range(nl):
            if name in ("rg_w_a", "rg_w_x"):
                at = (2 * l + (name == "rg_w_x")) * mat_rows
                out.append((lambda vp_ref, mp_ref, at=at: mp_ref[at:at + mat_rows, :].reshape(mat_shape), l))
            elif name == "b_ada":
                for j in range(3):
                    r = l * nr + 1 + j
                    out.append((lambda vp_ref, mp_ref, r=r: vp_ref[r:r + 1, :], (slice(l, l + 1), slice(j * d, (j + 1) * d))))
            else:
                r = l * nr + REP_ROWS.index(name)
                cols = slice(0, LANES) if name == "ml_b_if" else slice(None)
                out.append((lambda vp_ref, mp_ref, r=r, cols=cols: vp_ref[r:r + 1, cols], (slice(l, l + 1), slice(None))))
        return out

    def body(*refs):
        vp_ref, mp_ref = refs[:2]
        ins, outs = refs[2:2 + 3 * len(names)], refs[2 + 3 * len(names):]
        for pi, name in enumerate(names):
            w_ref, m_ref, v_ref = ins[3 * pi:3 * pi + 3]
            g_ref, d_ref, mo_ref, vo_ref = outs[4 * pi:4 * pi + 4]
            for get, idx in pieces(name):
                g = get(vp_ref, mp_ref)
                delta, m2, v2 = _adam_math(w_ref[idx], g, m_ref[idx], v_ref[idx])
                g_ref[idx] = g
                d_ref[idx] = delta
                mo_ref[idx] = m2
                vo_ref[idx] = v2

    flat = [a for name in names for a in params[name]]
    out_shape = [jax.ShapeDtypeStruct(params[name][0].shape, F32) for name in names for _ in range(4)]
    res = pl.pallas_call(
        body, name="adam_replicated", grid=(1,),
        in_specs=[_full(vp.shape), _full(mp.shape)] + [_full(a.shape) for a in flat],
        out_specs=[_full(o.shape) for o in out_shape], out_shape=out_shape, compiler_params=_params(1),
    )(vp, mp, *flat)
    return {name: res[4 * pi:4 * pi + 4] for pi, name in enumerate(names)}


class _Plan:
    def __init__(self):
        self.hosted, self.after = {}, {}

    def host(self, key, comm, then=None):
        self.hosted.setdefault(key, []).append(comm)
        if then is not None:
            self.after.setdefault(key, []).append(then)

    def comms(self, key):
        return self.hosted.pop(key, None)

    def done(self, key):
        for fn in self.after.pop(key, []):
            fn()

    def flush(self):
        while self.hosted:
            key = next(iter(self.hosted))
            _call(lambda: None, self.comms(key), name="exchange_after_%s_%d" % key, grid=(1,), in_specs=[], out_specs=[],
                  out_shape=[], args=())
            self.done(key)


VEC_TABLE = ("norm_g", "rg_conv_b", "rg_b_a", "rg_b_x", "rg_lambda", "ml_conv_b", "ml_norm_g")


def _vec_table(rep):
    rows = [rep[n] for n in VEC_TABLE]
    return jnp.stack(rows + [jnp.zeros_like(rows[0])] * (SUBLANES - len(rows)), axis=1)


def _layer_fwd(l, xl, mod3, wl, rep, plan):
    s, d = xl.shape
    t_big, t_mid = _tile_for(s, 512), _tile_for(s, 256)
    nh_ml = rep["ml_b_if"].shape[1] // 2
    vec = lambda name: _vec(rep["vecs"], l, VEC_TABLE.index(name))
    shift, scale, gate = (_vec(mod3, l, kk) for kk in range(3))
    hosted = lambda name: plan.comms((name, l)) if plan else None
    done = lambda name: plan.done((name, l)) if plan else None
    u, hbf = _in_fwd(xl, vec("norm_g"), scale, shift, wl["w_in_g"], 0, t_mid, hosted("in_proj_fwd"))
    done("in_proj_fwd")
    h_rg, y_rg, *rg_gates = _rg_fwd(u, d, wl["rg_conv_w"], vec("rg_conv_b"), rep["rg_w_a_bf"][l], vec("rg_b_a"),
                                    rep["rg_w_x_bf"][l], vec("rg_b_x"), vec("rg_lambda"), t_mid, hosted("rglru_fwd"))
    done("rglru_fwd")
    q, k, v, gcol = _ml_pre(u, d, wl["ml_conv_w"], vec("ml_conv_b"), wl["w_qkv"][0], wl["w_qkv"][1],
                            wl["w_qkv"][2], wl["wif_pad"], wl["bif_pad"], t_mid, hosted("mlstm_proj_fwd"))
    done("mlstm_proj_fwd")
    grow = gcol[:, 0:16].T
    cell, y_ml, cs, ns, ms, mt = _ml_cell_fwd(q, k, v, gcol, grow, u, vec("ml_norm_g"), nh_ml, hosted("mlstm_cell_fwd"))
    done("mlstm_cell_fwd")
    x_new, y = _out_fwd(xl, y_rg, y_ml, gate, wl["w_out_g"], 0, t_big, hosted("out_proj_fwd"))
    done("out_proj_fwd")
    saved = dict(x=xl, u=u, hbf=hbf, h_rg=h_rg, y_rg=y_rg, q=q, k=k, v=v, gcol=gcol, grow=grow, cell=cell, y_ml=y_ml,
                 cs=cs, ns=ns, ms=ms, mt=mt, y=y, scale=scale, gate=gate, rg_gates=rg_gates)
    return x_new, saved


def _layer_bwd(l, dx, sv, wl, rep, plan, grads=None, split_last=False):
    s, d = dx.shape
    t_big, t_mid = _tile_for(s, 512), _tile_for(s, 256)
    nh_ml = rep["ml_b_if"].shape[1] // 2
    nd, _, _, w_cols = wl["w_in_g"].shape
    grads = {} if grads is None else grads
    vec = lambda name: _vec(rep["vecs"], l, VEC_TABLE.index(name))
    hosted = lambda name: plan.comms((name, l)) if plan else None
    done = lambda name: plan.done((name, l)) if plan else None
    dy_rg, dy_ml, gw_out, dgate = _out_bwd(dx, sv["gate"], sv["y"], sv["y_rg"], sv["y_ml"], wl["w_out_g"], 0, t_big,
                                           hosted("out_proj_bwd"))
    grads.update(w_out=gw_out)
    done("out_proj_bwd")
    dq, dk, dv, dgates, d_mlo, d_mlz, g_mlng = _ml_cell_bwd(
        dy_ml, sv["u"], sv["cell"], sv["q"], sv["k"], sv["v"], sv["gcol"], sv["grow"], sv["mt"], sv["cs"], sv["ns"],
        sv["ms"], vec("ml_norm_g"), nh_ml, hosted("mlstm_cell_bwd"))
    done("mlstm_cell_bwd")
    d_mlx, g_wq, g_wk, g_wv, g_wift, g_bif, g_mlcw, g_mlcb = _ml_pre_bwd(
        dq, dk, dv, dgates, sv["gcol"], sv["u"], sv["q"], sv["k"], sv["v"], wl["ml_conv_w"], vec("ml_conv_b"),
        wl["w_qkv"][0], wl["w_qkv"][1], wl["w_qkv"][2], wl["wift_pad"], t_mid, hosted("mlstm_proj_bwd"))
    done("mlstm_proj_bwd")
    d_rgx, d_rgz, g_wa, g_wx, g_ba, g_bx, g_lam, g_rgcw, g_rgcb = _rg_bwd(
        dy_rg, sv["u"], sv["h_rg"], sv["rg_gates"], wl["rg_conv_w"], rep["rg_w_a_bf"][l], rep["rg_w_x_bf"][l],
        vec("rg_lambda"), t_mid, hosted("rglru_bwd"))
    grads.update(w_qkv=jnp.stack([g_wq, g_wk, g_wv]), rg_conv_w=g_rgcw[0:CONV_WIDTH], ml_conv_w=g_mlcw[0:CONV_WIDTH],
                 wif_t=g_wift[0:8], rg_w_a=g_wa, rg_w_x=g_wx)
    acc = dict(dgate=dgate, rg_conv_b=g_rgcb, rg_b_a=g_ba, rg_b_x=g_bx, rg_lambda=g_lam, ml_conv_b=g_mlcb,
               ml_b_if=g_bif, ml_norm_g=g_mlng)
    done("rglru_bwd")
    pieces = [d_rgx, d_rgz, d_mlx, d_mlo, d_mlz]
    grads.update(w_in=_in_bwd_w(pieces, sv["hbf"], w_cols, tuple(range(nd)), t_big, hosted("in_proj_bwd_w")))
    done("in_proj_bwd_w")
    n_tiles = s // t_mid
    counts = [n_tiles // 4, n_tiles - n_tiles // 4 - 1, 1] if split_last and n_tiles >= 4 else [n_tiles]
    in_args = (pieces, sv["x"], dx, vec("norm_g"), sv["scale"], wl["w_in_g"], 0, t_mid)
    res, at = None, 0
    for key, count in zip(("in_proj_bwd_x", "in_proj_bwd_x_rest", "in_proj_bwd_x_end"), counts):
        res = _in_bwd(*in_args, hosted(key), (at, count), res)
        done(key)
        at += count
    dx, dscale, dshift, g_ng = res
    acc.update(norm_g=g_ng, dshift=dshift, dscale=dscale)
    grads.update(acc=acc, dmod=jnp.concatenate([dshift[0:1], dscale[0:1], dgate[0:1]], axis=1))
    return dx, grads


def _local_step(x, c, target, wts):
    d = x.shape[1]
    nl = wts["w_in_g"].shape[1]
    mod, cact = _mod_call(c, wts["w_ada_g"], wts["b_ada"])
    wl = [dict(w_in_g=wts["w_in_g"][:, l:l + 1], w_out_g=wts["w_out_g"][:, l:l + 1], w_qkv=wts["w_qkv"][l],
               rg_conv_w=wts["rg_conv_w"][l], ml_conv_w=wts["ml_conv_w"][l], wif_pad=wts["wif_pad"][l],
               wift_pad=wts["wift_pad"][l], bif_pad=wts["bif_pad"][l]) for l in range(nl)]
    rep = dict(wts, vecs=_vec_table(wts))
    mod3 = mod.reshape(nl, 3, d)
    saved, xl = [], x
    for l in range(nl):
        xl, sv = _layer_fwd(l, xl, mod3, wl[l], rep, None)
        saved.append(sv)
    dx, loss_p, g_final = _loss_call(xl, wts["final_g"].reshape(1, -1), target, _tile_for(x.shape[0], 512))
    grads = [None] * nl
    for l in reversed(range(nl)):
        dx, grads[l] = _layer_bwd(l, dx, saved[l], wl[l], rep, None)
        grads[l]["w_ada"] = _ada_bwd_w(cact[0].reshape(d, 1), grads[l]["dmod"], wts["w_ada_g"].shape[0])
        grads[l]["b_ada"] = grads[l]["dmod"][0]
        grads[l].update({n: a[0] for n, a in grads[l]["acc"].items()})
        grads[l]["ml_b_if"] = grads[l]["ml_b_if"][0:8]
    return loss_p[0, 0], dx, grads, g_final[0]


def _full_qkv(qkv_g, d):
    nd, _, rows3, dh = qkv_g.shape
    nh = d // dh
    rsh = rows3 // (3 * nh)
    return qkv_g.reshape(nd, 3, nh, rsh, dh).transpose(1, 2, 0, 3, 4).reshape(3, nh, nd * rsh, dh)


def _small_weights(small, l, ml_b_if):
    nd = small.shape[0]
    sm = small[:, l]
    cw = sm[:, 0:2 * CONV_WIDTH].reshape(nd, 2, CONV_WIDTH, LANES).transpose(1, 2, 0, 3).reshape(2, CONV_WIDTH, nd * LANES)
    if_rows = (sm.shape[1] - 2 * CONV_WIDTH) * LANES // 8
    wif_t = sm[:, 2 * CONV_WIDTH:].reshape(nd, 8, if_rows).transpose(1, 0, 2).reshape(8, nd * if_rows)
    wift_pad = jnp.pad(wif_t, ((0, LANES - 8), (0, 0))).astype(BF16)
    return dict(rg_conv_w=cw[0], ml_conv_w=cw[1], wift_pad=wift_pad, wif_pad=wift_pad.T,
                bif_pad=jnp.pad(ml_b_if[l], (0, LANES - 8)).reshape(1, LANES))


def kernel(x, c, norm_g, w_ada, b_ada, w_in, rg_conv_w, rg_conv_b, rg_w_a, rg_b_a, rg_w_x, rg_b_x, rg_lambda, ml_conv_w, ml_conv_b, ml_w_q, ml_w_k, ml_w_v, ml_w_if, ml_b_if, ml_norm_g, w_out, final_g, loss_target, m_norm_g, m_w_ada, m_b_ada, m_w_in, m_rg_conv_w, m_rg_conv_b, m_rg_w_a, m_rg_b_a, m_rg_w_x, m_rg_b_x, m_rg_lambda, m_ml_conv_w, m_ml_conv_b, m_ml_w_q, m_ml_w_k, m_ml_w_v, m_ml_w_if, m_ml_b_if, m_ml_norm_g, m_w_out, m_final_g, v_norm_g, v_w_ada, v_b_ada, v_w_in, v_rg_conv_w, v_rg_conv_b, v_rg_w_a, v_rg_b_a, v_rg_w_x, v_rg_b_x, v_rg_lambda, v_ml_conv_w, v_ml_conv_b, v_ml_w_q, v_ml_w_k, v_ml_w_v, v_ml_w_if, v_ml_b_if, v_ml_norm_g, v_w_out, v_final_g):
    given = dict(locals())
    nl = w_in.shape[0]
    d = x.shape[2]
    rep = {n: given[n] for n in REPLICATED}
    rep.update(rg_w_a_bf=rg_w_a.astype(BF16), rg_w_x_bf=rg_w_x.astype(BF16))
    bf = lambda a: a.astype(BF16)

    def qkv_shard(prefix):
        return jnp.stack([given[prefix + "ml_w_q"], given[prefix + "ml_w_k"], given[prefix + "ml_w_v"]], axis=1).reshape(
            nl, -1, ml_w_q.shape[-1])

    def small_shard(prefix):
        return _small_pack(given[prefix + "rg_conv_w"], given[prefix + "ml_conv_w"], given[prefix + "ml_w_if"])

    plan = _Plan()
    qkv = qkv_shard("")
    first_ici = _gather_ici_comm([bf(w_in[0:1]), small_shard("")])
    condition = _exchange_comm([jnp.broadcast_to(c, (SUBLANES, d))], True)
    _run_comms([first_ici, condition], "gather_first")
    first_fwd = _gather_fwd_comm(first_ici.results)
    wcols = w_ada.shape[2]
    me = 4 * lax.axis_index("x") + 2 * lax.axis_index("y") + lax.axis_index("c")
    b_cols = jnp.pad(lax.dynamic_slice_in_dim(b_ada, me * wcols, wcols, axis=1), ((0, SUBLANES - nl), (0, 0)))
    mod_cols, cact_all = _ada_mod(condition.results[0], w_ada, b_cols, [first_fwd])
    w_in_first, small = first_fwd.results
    wl = [_small_weights(small, l, ml_b_if) for l in range(nl)]
    wl[0]["w_in_g"] = w_in_first

    def gather_behind(arrs, ici_host, fwd_host, then):
        ici = _gather_ici_comm(arrs)

        def pass_on():
            fwd = _gather_fwd_comm(ici.results)
            plan.host(fwd_host, fwd, lambda: then(fwd.results))

        plan.host(ici_host, ici, pass_on)

    def got_out(l):
        return lambda r: wl[l].update(w_out_g=r[0], w_qkv=_full_qkv(r[1], d))

    gather_behind([bf(w_out[0:1]), bf(qkv[0:1])], ("in_proj_fwd", 0), ("rglru_fwd", 0), got_out(0))
    for l in range(1, nl):
        gather_behind([bf(w_in[l:l + 1])], ("rglru_fwd", l - 1), ("mlstm_proj_fwd", l - 1),
                      lambda r, l=l: wl[l].update(w_in_g=r[0]))
        gather_behind([bf(w_out[l:l + 1]), bf(qkv[l:l + 1])], ("mlstm_cell_fwd", l - 1), ("out_proj_fwd", l - 1), got_out(l))

    mod_blocks = _exchange([mod_cols], False, "scatter_modulation")[0]
    mod3 = mod_blocks[:, 0:nl].transpose(1, 0, 2).reshape(nl, 3, d)
    rep["vecs"] = _vec_table(rep)
    saved, xl = [], x[0]
    for l in range(nl):
        xl, sv = _layer_fwd(l, xl, mod3, wl[l], rep, plan)
        saved.append(sv)
    grad_x, loss_p, g_final = _loss_call(xl, final_g.reshape(1, -1), loss_target[0], _tile_for(xl.shape[0], 512))

    keys = ("w_in", "w_out", "w_qkv", "small")
    parity = lax.axis_index("c").astype(jnp.int32).reshape(1)
    grads, recv = [None] * nl, [None] * nl

    def parts_of(g):
        return [g["w_in"], g["w_out"], _qkv_slots(g["w_qkv"], N_DEV), _small_slots(g)]

    def pair_sums(l, parts, other):
        return [_pair_sum(a, o, parity, "pair_sum_%s_layer%d" % (key, l)) for key, a, o in zip(keys, parts, other)]

    def reduce_behind(l, host_layer):
        parts = parts_of(grads[l])
        swap = _core_swap_comm(parts)

        def summed():
            sums = pair_sums(l, parts, swap.results)
            big = _chip_swap_comm([sums[0]])
            rest = _chip_swap_comm(sums[1:])
            plan.host(("mlstm_cell_bwd", host_layer), big)
            plan.host(("rglru_bwd", host_layer), rest, lambda: recv.__setitem__(l, big.results + rest.results))

        plan.host(("out_proj_bwd", host_layer), swap, summed)

    first, own = {}, {}

    def reduce_own(names, parts_fn, ready_key, swap_key, chip_key):
        def go():
            parts = parts_fn()
            swap = _core_swap_comm(parts)

            def summed():
                sums = [_pair_sum(a, o, parity, "pair_sum_%s_layer0" % n) for n, a, o in zip(names, parts, swap.results)]
                chip = _chip_swap_comm(sums)
                plan.host(chip_key, chip, lambda: own.update(zip(names, chip.results)))

            plan.host(swap_key, swap, summed)

        plan.after.setdefault(ready_key, []).append(go)

    reduce_own(["w_out"], lambda: [first["w_out"]], ("out_proj_bwd", 0), ("mlstm_cell_bwd", 0), ("mlstm_proj_bwd", 0))
    reduce_own(["w_qkv", "small"], lambda: [_qkv_slots(first["w_qkv"], N_DEV), _small_slots(first)],
               ("rglru_bwd", 0), ("in_proj_bwd_w", 0), ("in_proj_bwd_x", 0))
    reduce_own(["w_in"], lambda: [first["w_in"]], ("in_proj_bwd_w", 0), ("in_proj_bwd_x", 0), ("in_proj_bwd_x_rest", 0))

    for l in reversed(range(nl)):
        if l > 0:
            grad_x, grads[l] = _layer_bwd(l, grad_x, saved[l], wl[l], rep, plan)
            reduce_behind(l, l - 1)
        else:
            grad_x, grads[l] = _layer_bwd(l, grad_x, saved[l], wl[l], rep, plan, first, True)
    plan.flush()
    recv[0] = [own[key] for key in keys]

    shard = {p: dict(w_in=given[p + "w_in"], w_out=given[p + "w_out"], w_qkv=qkv_shard(p), small=small_shard(p))
             for p in ("", "m_", "v_")}
    res = {}
    for ki, key in enumerate(keys):
        res[key] = _reduce_adam([recv[l][ki] for l in range(nl)], shard[""][key], shard["m_"][key], shard["v_"][key],
                                "reduce_adam_" + key)

    dmods = jnp.concatenate([grads[l]["dmod"] for l in range(nl)], axis=0)
    dmod_blocks = jnp.pad(dmods.reshape(nl, N_DEV, wcols).transpose(1, 0, 2), ((0, 0), (0, SUBLANES - nl), (0, 0)))
    dmod_all = _exchange([dmod_blocks], False, "scatter_dmod")[0][:, 0:nl].transpose(1, 0, 2)
    res["w_ada"] = _ada_grad_adam(cact_all.T, dmod_all, w_ada, m_w_ada, v_w_ada)

    widen = lambda a: jnp.pad(a, ((0, 0), (0, d - a.shape[1])))
    rows = [widen(grads[l]["acc"][n][0:1]) for l in range(nl) for n in REP_ROWS] + [g_final[0:1], widen(loss_p[0:1])]
    vp = jnp.concatenate(rows + [jnp.zeros(((-len(rows)) % ROW_ALIGN, d), F32)], axis=0)
    mp = jnp.stack([jnp.stack([grads[l]["rg_w_a"], grads[l]["rg_w_x"]]) for l in range(nl)]).reshape(-1, LANES)
    got = _exchange([vp.reshape(N_DEV, -1, d), mp.reshape(N_DEV, -1, LANES)], False, "reduce_scatter_replicated")
    vp_r, mp_r = _exchange(_sum_parts(got, "sum_replicated"), True, "gather_replicated")
    vp_r, mp_r = vp_r.reshape(-1, d), mp_r.reshape(-1, LANES)
    lanes = lambda a: jnp.pad(a, ((0, 0), (0, LANES - a.shape[1])))
    shaped = dict(ml_b_if=lanes, final_g=lambda a: a.reshape(1, d))
    names = [n for n in REPLICATED if n != "b_ada"] + ["b_ada"]
    rep_res = _adam_replicated(vp_r, mp_r, {n: tuple(shaped.get(n, lambda a: a)(given[p + n]) for p in ("", "m_", "v_"))
                                            for n in names}, nl)
    unshaped = dict(ml_b_if=lambda a: a[:, 0:ml_b_if.shape[1]], final_g=lambda a: a.reshape(d))
    rep_out = [{n: unshaped.get(n, lambda a: a)(rep_res[n][kind]) for n in names} for kind in range(4)]
    loss = vp_r[nl * len(REP_ROWS) + 1, 0]

    if_rows = ml_w_if.shape[1]
    order = ("norm_g", "w_ada", "b_ada", "w_in", "rg_conv_w", "rg_conv_b", "rg_w_a", "rg_b_a", "rg_w_x", "rg_b_x",
             "rg_lambda", "ml_conv_w", "ml_conv_b", "ml_w_q", "ml_w_k", "ml_w_v", "ml_w_if", "ml_b_if", "ml_norm_g",
             "w_out", "final_g")
    outs = [loss, grad_x[None]]
    for kind in range(4):
        qkv_k = res["w_qkv"][kind].reshape((nl, 3) + ml_w_q.shape[1:])
        rg_cw, ml_cw, wif = _small_unpack(res["small"][kind], if_rows)
        sharded = dict(w_ada=res["w_ada"][kind], w_in=res["w_in"][kind], w_out=res["w_out"][kind], ml_w_q=qkv_k[:, 0],
                       ml_w_k=qkv_k[:, 1], ml_w_v=qkv_k[:, 2], rg_conv_w=rg_cw, ml_conv_w=ml_cw, ml_w_if=wif)
        for n in order:
            outs.append(sharded[n] if n in sharded else rep_out[kind][n])
    return tuple(outs)
```

```python
import functools

import jax
import jax.numpy as jnp
from jax import lax
from jax.experimental import pallas as pl
from jax.experimental.pallas import tpu as pltpu

F32 = jnp.float32
BF16 = jnp.bfloat16
MESH_AXES = ("x", "y", "c")
N_DEV = 8
EPS = 1e-6
RG_C = 8.0
ML_CHUNK = 128
CONV_WIDTH = 4
ADAM_LR = 0.001
ADAM_B1 = 0.9
ADAM_B2 = 0.999
ADAM_EPS = 1e-08
ADAM_WD = 0.01
ADAM_STEP = 10
NEG_BIG = -1e30
LANES = 128
SUBLANES = 8
VMEM_LIMIT = 56 * 1024 * 1024
HI = lax.Precision.HIGHEST


def _params(n_grid):
    return pltpu.CompilerParams(dimension_semantics=("arbitrary",) * n_grid, vmem_limit_bytes=VMEM_LIMIT)


def _mm(a, b):
    return jnp.dot(a.astype(BF16), b.astype(BF16), preferred_element_type=F32)


def _mm_nt(a, b):
    return lax.dot_general(a.astype(BF16), b.astype(BF16), (((1,), (1,)), ((), ())), preferred_element_type=F32)


def _mm_tn(a, b):
    return lax.dot_general(a.astype(BF16), b.astype(BF16), (((0,), (0,)), ((), ())), preferred_element_type=F32)


def _mm_hi(a, b):
    return jnp.dot(a, b, precision=HI, preferred_element_type=F32)


def _sigmoid(x):
    return 1.0 / (1.0 + jnp.exp(-x))


def _softplus(x):
    return jnp.maximum(x, 0.0) + jnp.log(1.0 + jnp.exp(-jnp.abs(x)))


def _neg_expm1(x):
    poly = -x * (1.0 + x * (0.5 + x * (1.0 / 6.0 + x * (1.0 / 24.0 + x * (1.0 / 120.0)))))
    return jnp.where(jnp.abs(x) < 0.05, poly, 1.0 - jnp.exp(x))


def _iota(shape, dim):
    return lax.broadcasted_iota(jnp.int32, shape, dim)


def _colsum(x):
    return jnp.sum(x, axis=0, keepdims=True)


def _rowsum(x):
    return jnp.sum(x, axis=1, keepdims=True)


def _col(x, j):
    return _rowsum(jnp.where(_iota(x.shape, 1) == j, x, 0.0))


def _row(x, j):
    return _colsum(jnp.where(_iota(x.shape, 0) == j, x, 0.0))


def _shift_down(x, j, prev8):
    if j == 0:
        return x
    t = x.shape[0]
    main = jnp.where(_iota(x.shape, 0) >= j, pltpu.roll(x, j, 0), 0.0)
    fix = jnp.where(_iota(prev8.shape, 0) < j, pltpu.roll(prev8, j, 0), 0.0)
    return jnp.concatenate([main[0:SUBLANES] + fix, main[SUBLANES:t]], axis=0)


def _shift_up(x, j, next8):
    if j == 0:
        return x
    t = x.shape[0]
    main = jnp.where(_iota(x.shape, 0) < t - j, pltpu.roll(x, t - j, 0), 0.0)
    fix = jnp.where(_iota(next8.shape, 0) >= SUBLANES - j, pltpu.roll(next8, SUBLANES - j, 0), 0.0)
    return jnp.concatenate([main[0:t - SUBLANES], main[t - SUBLANES:t] + fix], axis=0)


def _conv(x, prev8, w_ref):
    y = w_ref[CONV_WIDTH - 1:CONV_WIDTH, :] * x
    for j in range(1, CONV_WIDTH):
        y = y + w_ref[CONV_WIDTH - 1 - j:CONV_WIDTH - j, :] * _shift_down(x, j, prev8)
    return y


def _conv_bwd(dy, x, next8, w_ref, gw_ref):
    dx = None
    for j in range(CONV_WIDTH):
        k = CONV_WIDTH - 1 - j
        up = _shift_up(dy, j, next8)
        gw_ref[k:k + 1, :] += _colsum(up * x)
        term = w_ref[k:k + 1, :] * up
        dx = term if dx is None else dx + term
    return dx


def _scan_into(a, b, carry, out_ref, reverse):
    t, c = a.shape
    groups = t // SUBLANES
    a3 = a.reshape(groups, SUBLANES, c)
    b3 = b.reshape(groups, SUBLANES, c)
    sub = _iota(a3.shape, 1)
    for step in (1, 2, 4):
        keep = sub < SUBLANES - step if reverse else sub >= step
        shift = SUBLANES - step if reverse else step
        a_s = jnp.where(keep, pltpu.roll(a3, shift, 1), 1.0)
        b_s = jnp.where(keep, pltpu.roll(b3, shift, 1), 0.0)
        b3 = a3 * b_s + b3
        a3 = a3 * a_s
    for g in (reversed(range(groups)) if reverse else range(groups)):
        rows = slice(g * SUBLANES, (g + 1) * SUBLANES)
        out_ref[rows, :] = b3[g] + a3[g] * carry
        edge = g * SUBLANES if reverse else (g + 1) * SUBLANES - 1
        carry = out_ref[edge:edge + 1, :]


def _blockdiag(x, w_ref, transpose_w=False):
    nh, dh, _ = w_ref.shape
    outs = []
    for h in range(nh):
        xs = x[:, h * dh:(h + 1) * dh]
        outs.append(_mm_nt(xs, w_ref[h]) if transpose_w else _mm(xs, w_ref[h]))
    return jnp.concatenate(outs, axis=1)


def _rg_gates(xc, wa_ref, ba_ref, wx_ref, bx_ref, lam_ref):
    r = _sigmoid(_blockdiag(xc, wa_ref) + ba_ref[...])
    ig = _sigmoid(_blockdiag(xc, wx_ref) + bx_ref[...])
    sp = _softplus(-lam_ref[...])
    log_a = -RG_C * r * sp
    a = jnp.exp(log_a)
    beta = jnp.sqrt(_neg_expm1(2.0 * log_a))
    return r, ig, sp, a, beta


def _bcast8(row):
    return jnp.broadcast_to(row, (SUBLANES, row.shape[1]))


def _full(shape):
    nd = len(shape)
    return pl.BlockSpec(shape, lambda *_: (0,) * nd)


class _Comm:
    def __init__(self, arrays, out_shapes, sems, start, finish, aliases=()):
        self.arrays, self.out_shapes, self.sems = list(arrays), list(out_shapes), list(sems)
        self.start, self.finish, self.aliases = start, finish, tuple(aliases)
        self.results = None


class _RowOf:
    def __init__(self, ref, k):
        self.ref, self.k = ref, k

    def __getitem__(self, idx):
        cols = slice(None) if idx is Ellipsis else idx[1]
        return self.ref[0, self.k:self.k + 1, cols]


def _vec(table, layer, k):
    return ("row", table, layer, k)


def _is_row(arg):
    return isinstance(arg, tuple) and len(arg) == 4 and arg[0] == "row"


def _call(body, comms, *, name, grid, in_specs, out_specs, out_shape, args, scratch_shapes=(), aliases=None):
    comms = [cm for cm in (comms or []) if cm is not None]
    rows = {i: a[3] for i, a in enumerate(args) if _is_row(a)}
    in_specs = [pl.BlockSpec((1,) + a[1].shape[1:], functools.partial(lambda layer, *_: (layer, 0, 0), a[2]))
                if _is_row(a) else sp for a, sp in zip(args, in_specs)]
    args = tuple(a[1] if _is_row(a) else a for a in args)
    n_in, n_out, n_sc = len(args), len(out_shape), len(scratch_shapes)
    c_arrays = [a for cm in comms for a in cm.arrays]
    c_outs = [o for cm in comms for o in cm.out_shapes]
    c_sems = [sm for cm in comms for sm in cm.sems]
    aliases, a_at, o_at = dict(aliases or {}), n_in, n_out
    for cm in comms:
        for (i, j) in cm.aliases:
            aliases[a_at + i] = o_at + j
        a_at += len(cm.arrays)
        o_at += len(cm.out_shapes)

    def wrapped(*refs):
        ins, c_in = refs[:n_in], refs[n_in:n_in + len(c_arrays)]
        ins = [_RowOf(r, rows[i]) if i in rows else r for i, r in enumerate(ins)]
        at = n_in + len(c_arrays)
        outs, c_out = refs[at:at + n_out], refs[at + n_out:at + n_out + len(c_outs)]
        at += n_out + len(c_outs)
        scr, sems = refs[at:at + n_sc], refs[at + n_sc:]
        views, ia, io, isem = [], 0, 0, 0
        for cm in comms:
            views.append((c_in[ia:ia + len(cm.arrays)], c_out[io:io + len(cm.out_shapes)], sems[isem:isem + len(cm.sems)]))
            ia, io, isem = ia + len(cm.arrays), io + len(cm.out_shapes), isem + len(cm.sems)
        if comms:
            @pl.when(pl.program_id(0) == 0)
            def _():
                for cm, view in zip(comms, views):
                    cm.start(*view)

        body(*ins, *outs, *scr)
        if comms:
            @pl.when(pl.program_id(0) == grid[0] - 1)
            def _():
                for cm, view in zip(comms, views):
                    cm.finish(*view)

    hbm = pl.BlockSpec(memory_space=pl.ANY)
    res = pl.pallas_call(
        wrapped, name=name, grid=grid,
        in_specs=list(in_specs) + [hbm] * len(c_arrays), out_specs=list(out_specs) + [hbm] * len(c_outs),
        out_shape=list(out_shape) + c_outs, scratch_shapes=list(scratch_shapes) + c_sems,
        input_output_aliases=aliases, compiler_params=_params(len(grid)),
    )(*args, *c_arrays)
    at = n_out
    for cm in comms:
        cm.results = list(res[at:at + len(cm.out_shapes)])
        at += len(cm.out_shapes)
    return list(res[:n_out])


def _mod_call(c, w_ada_g, b_ada):
    nd, nl, d, w = w_ada_g.shape

    def body(c_ref, w_ref, b_ref, mod_ref, cact_ref):
        cv = c_ref[...]
        ca = _bcast8(cv * _sigmoid(cv))
        cact_ref[...] = ca
        mod_ref[0, 0] = _mm(ca, w_ref[0, 0]) + b_ref[0, 0]

    mod, cact = pl.pallas_call(
        body, name="adaln_mod", grid=(nl, nd),
        in_specs=[_full((1, d)),
                  pl.BlockSpec((1, 1, d, w), lambda l, j: (j, l, 0, 0)),
                  pl.BlockSpec((1, 1, 1, w), lambda l, j: (l, j, 0, 0))],
        out_specs=[pl.BlockSpec((1, 1, SUBLANES, w), lambda l, j: (l, j, 0, 0)), _full((SUBLANES, d))],
        out_shape=[jax.ShapeDtypeStruct((nl, nd, SUBLANES, w), F32), jax.ShapeDtypeStruct((SUBLANES, d), F32)],
        compiler_params=_params(2),
    )(c, w_ada_g, b_ada.reshape(nl, nd, 1, w))
    return mod[:, :, 0, :].reshape(nl, nd * w), cact


def _join_columns(w_ref, wcat):
    nd, _, _, w = w_ref.shape

    @pl.when(pl.program_id(0) == 0)
    def _():
        for j in range(nd):
            wcat[:, j * w:(j + 1) * w] = w_ref[j, 0]


def _in_fwd(x, ng, scale, shift, w_in_g, layer, tile, comms=None):
    s, d = x.shape
    nd, _, _, w = w_in_g.shape

    def body(x_ref, ng_ref, sc_ref, sh_ref, w_ref, u_ref, h_ref, wcat):
        _join_columns(w_ref, wcat)
        xv = x_ref[...]
        rs = lax.rsqrt(jnp.mean(xv * xv, axis=1, keepdims=True) + EPS)
        hb = (xv * rs * ng_ref[...] * (1.0 + sc_ref[...]) + sh_ref[...]).astype(BF16)
        h_ref[...] = hb
        u_ref[...] = jnp.dot(hb, wcat[...], preferred_element_type=F32)

    return _call(
        body, comms, name="in_proj_fwd", grid=(s // tile,),
        in_specs=[pl.BlockSpec((tile, d), lambda i: (i, 0)), _full((1, d)), _full((1, d)), _full((1, d)),
                  pl.BlockSpec((nd, 1, d, w), lambda i: (0, layer, 0, 0), pipeline_mode=pl.Buffered(1))],
        out_specs=[pl.BlockSpec((tile, nd * w), lambda i: (i, 0)), pl.BlockSpec((tile, d), lambda i: (i, 0))],
        out_shape=[jax.ShapeDtypeStruct((s, nd * w), F32), jax.ShapeDtypeStruct((s, d), BF16)],
        scratch_shapes=[pltpu.VMEM((d, nd * w), BF16)],
        args=(x, ng, scale, shift, w_in_g))


def _rg_fwd(u, d, conv_w, conv_b, w_a, b_a, w_x, b_x, lam, tile, comms=None):
    s = u.shape[0]

    def body(x_ref, z_ref, cw_ref, cb_ref, wa_ref, ba_ref, wx_ref, bx_ref, lam_ref,
             h_ref, y_ref, xc_ref, r_ref, i_ref, a_ref, beta_ref, prev8, hcar):
        @pl.when(pl.program_id(0) == 0)
        def _():
            prev8[...] = jnp.zeros_like(prev8)
            hcar[...] = jnp.zeros_like(hcar)

        x = x_ref[...]
        xc = _conv(x, prev8[...], cw_ref) + cb_ref[...]
        prev8[...] = x[tile - SUBLANES:tile, :]
        r, ig, _, a, beta = _rg_gates(xc, wa_ref, ba_ref, wx_ref, bx_ref, lam_ref)
        xc_ref[...] = xc
        r_ref[...] = r
        i_ref[...] = ig
        a_ref[...] = a
        beta_ref[...] = beta
        _scan_into(a, beta * ig * xc, hcar[SUBLANES - 1:SUBLANES, :], h_ref, False)
        h = h_ref[...]
        hcar[...] = h[tile - SUBLANES:tile, :]
        z = z_ref[...]
        y_ref[...] = (h * z * _sigmoid(z)).astype(BF16)

    vec = _full((1, d))
    return _call(
        body, comms, name="rglru_fwd", grid=(s // tile,),
        in_specs=[pl.BlockSpec((tile, d), lambda i: (i, 0)), pl.BlockSpec((tile, d), lambda i: (i, 1)),
                  _full(conv_w.shape), vec, _full(w_a.shape), vec, _full(w_x.shape), vec, vec],
        out_specs=[pl.BlockSpec((tile, d), lambda i: (i, 0))] * 7,
        out_shape=[jax.ShapeDtypeStruct((s, d), F32), jax.ShapeDtypeStruct((s, d), BF16)] + [jax.ShapeDtypeStruct((s, d), F32)] * 5,
        scratch_shapes=[pltpu.VMEM((SUBLANES, d), F32), pltpu.VMEM((SUBLANES, d), F32)],
        args=(u, u, conv_w, conv_b, w_a, b_a, w_x, b_x, lam))


def _ml_pre(u, d, conv_w, conv_b, w_q, w_k, w_v, wif, bif, tile, comms=None):
    s = u.shape[0]
    nh = w_q.shape[0]

    def body(x_ref, cw_ref, cb_ref, wq_ref, wk_ref, wv_ref, wif_ref, bif_ref, q_ref, k_ref, v_ref, g_ref, pre_ref, prev8):
        @pl.when(pl.program_id(0) == 0)
        def _():
            prev8[...] = jnp.zeros_like(prev8)

        x = x_ref[...]
        pre = _conv(x, prev8[...], cw_ref) + cb_ref[...]
        prev8[...] = x[tile - SUBLANES:tile, :]
        xc = pre * _sigmoid(pre)
        q = _blockdiag(xc, wq_ref)
        k = _blockdiag(xc, wk_ref)
        v = _blockdiag(x, wv_ref)
        pre_ref[...] = pre
        q_ref[...] = q
        k_ref[...] = k
        v_ref[...] = v
        g = _mm(q, wif_ref[0:d, :]) + _mm(k, wif_ref[d:2 * d, :]) + _mm(v, wif_ref[2 * d:3 * d, :]) + bif_ref[...]
        lane = _iota(g.shape, 1)
        gl = jnp.where(lane < 4, g, jnp.where(lane < 8, -_softplus(-g), 0.0))
        tri = jnp.where(_iota((ML_CHUNK, ML_CHUNK), 1) <= _iota((ML_CHUNK, ML_CHUNK), 0), 1.0, 0.0)
        cums = [_mm_hi(tri, gl[c * ML_CHUNK:(c + 1) * ML_CHUNK, :]) for c in range(tile // ML_CHUNK)]
        cum = cums[0] if len(cums) == 1 else jnp.concatenate(cums, axis=0)
        g_ref[...] = gl + jnp.where((lane >= 8) & (lane < 12), pltpu.roll(cum, 4, 1), 0.0)

    vec = _full((1, d))
    return _call(
        body, comms, name="mlstm_proj_fwd", grid=(s // tile,),
        in_specs=[pl.BlockSpec((tile, d), lambda i: (i, 2)), _full(conv_w.shape), vec,
                  _full(w_q.shape), _full(w_k.shape), _full(w_v.shape), _full(wif.shape), _full((1, LANES))],
        out_specs=[pl.BlockSpec((tile, d), lambda i: (i, 0))] * 3 + [pl.BlockSpec((tile, LANES), lambda i: (i, 0)),
                                                                     pl.BlockSpec((tile, d), lambda i: (i, 0))],
        out_shape=[jax.ShapeDtypeStruct((s, d), F32)] * 3 + [jax.ShapeDtypeStruct((s, LANES), F32),
                                                             jax.ShapeDtypeStruct((s, d), F32)],
        scratch_shapes=[pltpu.VMEM((SUBLANES, d), F32)],
        args=(u, conv_w, conv_b, w_q, w_k, w_v, wif, bif))


def _cell_chunk(h, nh, q_ref, k_ref, v_ref, gc, gr, m_prev, c_h, n_h, m_t=None):
    lc = ML_CHUNK
    dh = q_ref.shape[1] // nh
    sl = slice(h * dh, (h + 1) * dh)
    qh = q_ref[:, sl]
    kh = k_ref[:, sl] * (dh ** -0.5)
    vh = v_ref[:, sl]
    li_c = _col(gc, h)
    b_c = _col(gc, 8 + h)
    lib_r = _row(gr, h) - _row(gr, 8 + h)
    b_last = _colsum(jnp.where(_iota((lc, 1), 0) == lc - 1, b_c, 0.0))
    causal = _iota((lc, lc), 1) <= _iota((lc, lc), 0)
    dmat = jnp.where(causal, b_c + lib_r, NEG_BIG)
    m_inter = b_c + m_prev
    if m_t is None:
        m_t = jnp.maximum(m_inter, jnp.max(dmat, axis=1, keepdims=True))
    w_intra = jnp.exp(dmat - m_t)
    w_inter = jnp.exp(m_inter - m_t)
    amat = _mm_nt(qh, kh)
    smat = amat * w_intra
    qc = _mm(qh, c_h)
    qn = _rowsum(qh * n_h)
    den = _rowsum(smat) + w_inter * qn
    gst = b_last - b_c + li_c
    m_new = jnp.maximum(b_last + m_prev, jnp.max(gst, axis=0, keepdims=True))
    w_state = jnp.exp(gst - m_new)
    decay = jnp.exp(b_last + m_prev - m_new)
    return dict(sl=sl, qh=qh, kh=kh, vh=vh, m_t=m_t, w_intra=w_intra, w_inter=w_inter, smat=smat, qc=qc, qn=qn,
                den=den, m_new=m_new, w_state=w_state, decay=decay)


def _ml_cell_fwd(q, k, v, gcol, grow, u, ng, nh, comms=None):
    s, d = q.shape
    lc = ML_CHUNK
    nc = s // lc
    dh = d // nh

    def body(q_ref, k_ref, v_ref, gc_ref, gr_ref, o_ref, z_ref, ng_ref,
             cell_ref, y_ref, cs_ref, ns_ref, ms_ref, mt_ref, c_sc, n_sc, m_sc):
        @pl.when(pl.program_id(0) == 0)
        def _():
            c_sc[...] = jnp.zeros_like(c_sc)
            n_sc[...] = jnp.zeros_like(n_sc)
            m_sc[...] = jnp.zeros_like(m_sc)

        gc = gc_ref[...]
        gr = gr_ref[...]
        lane = _iota((lc, LANES), 1)
        mt_acc = jnp.zeros((lc, LANES), F32)
        for h in range(nh):
            c_h = c_sc[h]
            n_h = n_sc[h, 0:1, :]
            m_prev = jnp.max(m_sc[h, 0:1, :], axis=1, keepdims=True)
            cs_ref[0, h] = c_h
            ns_ref[0, h] = n_sc[h]
            ms_ref[0, h] = m_sc[h]
            t = _cell_chunk(h, nh, q_ref, k_ref, v_ref, gc, gr, m_prev, c_h, n_h)
            sl = t["sl"]
            num = _mm(t["smat"], t["vh"]) + t["w_inter"] * t["qc"]
            cell_h = num / jnp.maximum(jnp.abs(t["den"]), jnp.exp(-t["m_t"]))
            mt_acc = jnp.where(lane == h, t["m_t"], mt_acc)
            kw = t["kh"] * t["w_state"]
            c_sc[h] = t["decay"] * c_h + _mm_tn(kw, t["vh"])
            n_sc[h] = _bcast8(t["decay"] * n_h + _colsum(kw))
            m_sc[h] = jnp.broadcast_to(t["m_new"], (SUBLANES, LANES))
            hg = _sigmoid(o_ref[:, sl]) * cell_h
            hn = hg * lax.rsqrt(jnp.mean(hg * hg, axis=1, keepdims=True) + EPS)
            z = z_ref[:, sl]
            cell_ref[:, sl] = cell_h
            y_ref[:, sl] = (hn * ng_ref[:, sl] * z * _sigmoid(z)).astype(BF16)
        mt_ref[...] = mt_acc

    tok = pl.BlockSpec((lc, d), lambda c: (c, 0))
    return _call(
        body, comms, name="mlstm_cell_fwd", grid=(nc,),
        in_specs=[tok, tok, tok, pl.BlockSpec((lc, LANES), lambda c: (c, 0)), pl.BlockSpec((16, lc), lambda c: (0, c)),
                  pl.BlockSpec((lc, d), lambda c: (c, 3)), pl.BlockSpec((lc, d), lambda c: (c, 4)), _full((1, d))],
        out_specs=[tok, tok, pl.BlockSpec((1, nh, dh, dh), lambda c: (c, 0, 0, 0)),
                   pl.BlockSpec((1, nh, SUBLANES, dh), lambda c: (c, 0, 0, 0)),
                   pl.BlockSpec((1, nh, SUBLANES, LANES), lambda c: (c, 0, 0, 0)),
                   pl.BlockSpec((lc, LANES), lambda c: (c, 0))],
        out_shape=[jax.ShapeDtypeStruct((s, d), F32), jax.ShapeDtypeStruct((s, d), BF16),
                   jax.ShapeDtypeStruct((nc, nh, dh, dh), F32), jax.ShapeDtypeStruct((nc, nh, SUBLANES, dh), F32),
                   jax.ShapeDtypeStruct((nc, nh, SUBLANES, LANES), F32), jax.ShapeDtypeStruct((s, LANES), F32)],
        scratch_shapes=[pltpu.VMEM((nh, dh, dh), F32), pltpu.VMEM((nh, SUBLANES, dh), F32),
                        pltpu.VMEM((nh, SUBLANES, LANES), F32)],
        args=(q, k, v, gcol, grow, u, u, ng))


def _out_fwd(x, y_rg, y_ml, gate, w_out_g, layer, tile, comms=None):
    s, d = x.shape
    nd, _, r, _ = w_out_g.shape

    def body(x_ref, yr_ref, ym_ref, g_ref, w_ref, xn_ref, y_ref):
        ycat = jnp.concatenate([yr_ref[...].astype(BF16), ym_ref[...].astype(BF16)], axis=1)
        acc = jnp.dot(ycat, w_ref[...].reshape(nd * r, d), preferred_element_type=F32)
        y_ref[...] = acc
        xn_ref[...] = x_ref[...] + g_ref[...] * acc

    tok = pl.BlockSpec((tile, d), lambda i: (i, 0))
    return _call(
        body, comms, name="out_proj_fwd", grid=(s // tile,),
        in_specs=[tok, tok, tok, _full((1, d)), pl.BlockSpec((nd, 1, r, d), lambda i: (0, layer, 0, 0))],
        out_specs=[tok, tok],
        out_shape=[jax.ShapeDtypeStruct((s, d), F32)] * 2,
        args=(x, y_rg, y_ml, gate, w_out_g))


def _loss_call(x, fg, target, tile):
    s, d = x.shape

    def body(x_ref, g_ref, t_ref, dx_ref, loss_ref, gg_ref):
        @pl.when(pl.program_id(0) == 0)
        def _():
            loss_ref[...] = jnp.zeros_like(loss_ref)
            gg_ref[...] = jnp.zeros_like(gg_ref)

        xv = x_ref[...]
        g = g_ref[...]
        rs = lax.rsqrt(jnp.mean(xv * xv, axis=1, keepdims=True) + EPS)
        xh = xv * rs
        e = xh * g - t_ref[...]
        loss_ref[...] += jnp.broadcast_to(_colsum(_rowsum(e * e)) * (0.5 / d), loss_ref.shape)
        dy = e * (1.0 / d)
        gg_ref[...] += _bcast8(_colsum(dy * xh))
        dxh = dy * g
        dx_ref[...] = rs * (dxh - xh * jnp.mean(dxh * xh, axis=1, keepdims=True))

    tok = pl.BlockSpec((tile, d), lambda i: (i, 0))
    return pl.pallas_call(
        body, name="final_norm_loss", grid=(s // tile,),
        in_specs=[tok, _full((1, d)), tok],
        out_specs=[tok, _full((SUBLANES, LANES)), _full((SUBLANES, d))],
        out_shape=[jax.ShapeDtypeStruct((s, d), F32), jax.ShapeDtypeStruct((SUBLANES, LANES), F32),
                   jax.ShapeDtypeStruct((SUBLANES, d), F32)],
        compiler_params=_params(1),
    )(x, fg, target)


def _out_bwd(dxo, gate, y, y_rg, y_ml, w_out_g, layer, tile, comms=None):
    s, d = dxo.shape
    nd, _, r, _ = w_out_g.shape

    def body(dx_ref, g_ref, y_ref, yr_ref, ym_ref, w_ref, dyr_ref, dym_ref, gw_ref, dg_ref):
        @pl.when(pl.program_id(0) == 0)
        def _():
            gw_ref[...] = jnp.zeros_like(gw_ref)
            dg_ref[...] = jnp.zeros_like(dg_ref)

        dxv = dx_ref[...]
        dg_ref[...] += _bcast8(_colsum(dxv * y_ref[...]))
        dyb = (dxv * g_ref[...]).astype(BF16)
        dycat = lax.dot_general(dyb, w_ref[...].reshape(nd * r, d), (((1,), (1,)), ((), ())), preferred_element_type=F32)
        dyr_ref[...] = dycat[:, 0:d]
        dym_ref[...] = dycat[:, d:2 * d]
        ycat = jnp.concatenate([yr_ref[...].astype(BF16), ym_ref[...].astype(BF16)], axis=1)
        gw_ref[...] += lax.dot_general(ycat, dyb, (((0,), (0,)), ((), ())), preferred_element_type=F32).reshape(nd, r, d)

    tok = pl.BlockSpec((tile, d), lambda i: (i, 0))
    return _call(
        body, comms, name="out_proj_bwd", grid=(s // tile,),
        in_specs=[tok, _full((1, d)), tok, tok, tok, pl.BlockSpec((nd, 1, r, d), lambda i: (0, layer, 0, 0))],
        out_specs=[tok, tok, _full((nd, r, d)), _full((SUBLANES, d))],
        out_shape=[jax.ShapeDtypeStruct((s, d), F32)] * 2 + [jax.ShapeDtypeStruct((nd, r, d), F32),
                                                             jax.ShapeDtypeStruct((SUBLANES, d), F32)],
        args=(dxo, gate, y, y_rg, y_ml, w_out_g))


def _ml_cell_bwd(dy_ml, u, cell, q, k, v, gcol, grow, mt, cs, ns, ms, ng, nh, comms=None):
    s, d = q.shape
    lc = ML_CHUNK
    nc = s // lc
    dh = d // nh

    def body(dy_ref, o_ref, z_ref, cell_ref, q_ref, k_ref, v_ref, gc_ref, gr_ref, mt_ref, cs_ref, ns_ref, ms_ref,
             ng_ref, dq_ref, dk_ref, dv_ref, dg_ref, do_ref, dz_ref, gng_ref, dc_sc, dn_sc):
        @pl.when(pl.program_id(0) == 0)
        def _():
            dc_sc[...] = jnp.zeros_like(dc_sc)
            dn_sc[...] = jnp.zeros_like(dn_sc)
            gng_ref[...] = jnp.zeros_like(gng_ref)

        gc = gc_ref[...]
        gr = gr_ref[...]
        mtv = mt_ref[...]
        lane = _iota((lc, LANES), 1)
        rowv = _iota((lc, 1), 0)
        dg_acc = jnp.zeros((lc, LANES), F32)
        for h in range(nh):
            c_h = cs_ref[0, h]
            n_h = ns_ref[0, h, 0:1, :]
            m_prev = jnp.max(ms_ref[0, h, 0:1, :], axis=1, keepdims=True)
            t = _cell_chunk(h, nh, q_ref, k_ref, v_ref, gc, gr, m_prev, c_h, n_h, m_t=_col(mtv, h))
            sl, qh, kh, vh = t["sl"], t["qh"], t["kh"], t["vh"]
            w_intra, w_inter, smat, w_state, decay = t["w_intra"], t["w_inter"], t["smat"], t["w_state"], t["decay"]
            cell_h = cell_ref[:, sl]
            o = o_ref[:, sl]
            z = z_ref[:, sl]
            dyv = dy_ref[:, sl]
            ngh = ng_ref[:, sl]
            so = _sigmoid(o)
            hg = so * cell_h
            rinv = lax.rsqrt(jnp.mean(hg * hg, axis=1, keepdims=True) + EPS)
            hn = hg * rinv
            sz = _sigmoid(z)
            dz_ref[:, sl] = (dyv * hn * ngh * (sz + z * sz * (1.0 - sz))).astype(BF16)
            dymid = dyv * z * sz
            gng_ref[:, sl] += _bcast8(_colsum(dymid * hn))
            dhn = dymid * ngh
            dhg = rinv * (dhn - hn * jnp.mean(dhn * hn, axis=1, keepdims=True))
            do_ref[:, sl] = (dhg * cell_h * so * (1.0 - so)).astype(BF16)
            dcell = dhg * so
            eneg = jnp.exp(-t["m_t"])
            aden = jnp.abs(t["den"])
            nst = jnp.maximum(aden, eneg)
            dnum = dcell / nst
            dden = jnp.where(aden > eneg, -_rowsum(cell_h * dcell) / nst * jnp.sign(t["den"]), 0.0)
            pmat = _mm_nt(dnum, vh) + dden
            damat = pmat * w_intra
            gmat = pmat * smat
            wdn = w_inter * dnum
            wdd = w_inter * dden
            dqh = _mm(damat, kh) + _mm_nt(wdn, c_h) + wdd * n_h
            dkh = _mm_tn(damat, qh)
            dvh = _mm_tn(smat, dnum)
            dw_inter = _rowsum(dnum * t["qc"]) + dden * t["qn"]
            dcn = dc_sc[h]
            dnn = dn_sc[h, 0:1, :]
            kw = kh * w_state
            dkw = _mm_nt(vh, dcn) + dnn
            dvh = dvh + _mm(kw, dcn)
            dkh = dkh + dkw * w_state
            dgst = _rowsum(dkw * kh) * w_state
            ddecay = _colsum(_rowsum(dcn * c_h)) + _rowsum(dnn * n_h)
            db_last = _colsum(dgst) + ddecay * decay
            rs_g = _rowsum(gmat)
            cs_g = _rowsum(gmat.T)
            db = rs_g - cs_g + dw_inter * w_inter - dgst + jnp.where(rowv == lc - 1, db_last, 0.0)
            dli = cs_g + dgst
            dc_sc[h] = decay * dcn + _mm_tn(qh, wdn)
            dn_sc[h] = _bcast8(decay * dnn + _colsum(qh * wdd))
            dq_ref[:, sl] = dqh
            dk_ref[:, sl] = dkh * (dh ** -0.5)
            dv_ref[:, sl] = dvh
            dg_acc = jnp.where(lane == h, dli, jnp.where(lane == 4 + h, db, dg_acc))
        dg_ref[...] = dg_acc

    rev = lambda c: nc - 1 - c
    tok = pl.BlockSpec((lc, d), lambda c: (rev(c), 0))
    g128 = pl.BlockSpec((lc, LANES), lambda c: (rev(c), 0))
    return _call(
        body, comms, name="mlstm_cell_bwd", grid=(nc,),
        in_specs=[tok, pl.BlockSpec((lc, d), lambda c: (rev(c), 3)), pl.BlockSpec((lc, d), lambda c: (rev(c), 4)),
                  tok, tok, tok, tok, g128, pl.BlockSpec((16, lc), lambda c: (0, rev(c))), g128,
                  pl.BlockSpec((1, nh, dh, dh), lambda c: (rev(c), 0, 0, 0)),
                  pl.BlockSpec((1, nh, SUBLANES, dh), lambda c: (rev(c), 0, 0, 0)),
                  pl.BlockSpec((1, nh, SUBLANES, LANES), lambda c: (rev(c), 0, 0, 0)), _full((1, d))],
        out_specs=[tok, tok, tok, g128, tok, tok, _full((SUBLANES, d))],
        out_shape=[jax.ShapeDtypeStruct((s, d), F32)] * 3 + [jax.ShapeDtypeStruct((s, LANES), F32)]
        + [jax.ShapeDtypeStruct((s, d), BF16)] * 2 + [jax.ShapeDtypeStruct((SUBLANES, d), F32)],
        scratch_shapes=[pltpu.VMEM((nh, dh, dh), F32), pltpu.VMEM((nh, SUBLANES, dh), F32)],
        args=(dy_ml, u, u, cell, q, k, v, gcol, grow, mt, cs, ns, ms, ng))


def _halo_spec(d, tile, nt, col):
    per = tile // SUBLANES
    return pl.BlockSpec((SUBLANES, d), lambda i: (jnp.maximum((nt - 1 - i) * per - 1, 0), col))


def _ml_pre_bwd(dq, dk, dv, dgates, gcol, u, pre, q, k, v, conv_w, w_q, w_k, w_v, wif_t, tile, comms=None):
    s, d = dq.shape
    nt = s // tile
    nh, dh, _ = w_q.shape

    def body(dq_ref, dk_ref, dv_ref, dg_ref, gc_ref, x_ref, pre_ref, q_ref, k_ref, v_ref, cw_ref,
             wq_ref, wk_ref, wv_ref, wift_ref,
             dx_ref, gwq_ref, gwk_ref, gwv_ref, gwif_ref, gbif_ref, gcw_ref, gcb_ref, next8):
        @pl.when(pl.program_id(0) == 0)
        def _():
            next8[...] = jnp.zeros_like(next8)
            for ref in (gwq_ref, gwk_ref, gwv_ref, gwif_ref, gbif_ref, gcw_ref, gcb_ref):
                ref[...] = jnp.zeros_like(ref)

        x = x_ref[...]
        pre = pre_ref[...]
        sg = _sigmoid(pre)
        xc = pre * sg
        dgc = dg_ref[...]
        lane = _iota(dgc.shape, 1)
        utri = jnp.where(_iota((ML_CHUNK, ML_CHUNK), 0) <= _iota((ML_CHUNK, ML_CHUNK), 1), 1.0, 0.0)
        rcs = [_mm_hi(utri, dgc[c * ML_CHUNK:(c + 1) * ML_CHUNK, :]) for c in range(tile // ML_CHUNK)]
        rc = rcs[0] if len(rcs) == 1 else jnp.concatenate(rcs, axis=0)
        dgates_v = jnp.where(lane < 4, dgc, jnp.where(lane < 8, rc * (1.0 - jnp.exp(gc_ref[...])), 0.0))
        dgb = dgates_v.astype(BF16)
        gbif_ref[...] += jnp.broadcast_to(_colsum(dgates_v), gbif_ref.shape)
        ext = jnp.dot(dgb, wift_ref[...], preferred_element_type=F32)
        dqt = dq_ref[...] + ext[:, 0:d]
        dkt = dk_ref[...] + ext[:, d:2 * d]
        dvt = dv_ref[...] + ext[:, 2 * d:3 * d]
        gwif_ref[:, 0:d] += _mm_tn(dgb, q_ref[...])
        gwif_ref[:, d:2 * d] += _mm_tn(dgb, k_ref[...])
        gwif_ref[:, 2 * d:3 * d] += _mm_tn(dgb, v_ref[...])
        dxc_parts, dxv_parts = [], []
        for h in range(nh):
            sl = slice(h * dh, (h + 1) * dh)
            gwq_ref[h] += _mm_tn(xc[:, sl], dqt[:, sl])
            gwk_ref[h] += _mm_tn(xc[:, sl], dkt[:, sl])
            gwv_ref[h] += _mm_tn(x[:, sl], dvt[:, sl])
            dxc_parts.append(_mm_nt(dqt[:, sl], wq_ref[h]) + _mm_nt(dkt[:, sl], wk_ref[h]))
            dxv_parts.append(_mm_nt(dvt[:, sl], wv_ref[h]))
        dxc = jnp.concatenate(dxc_parts, axis=1)
        dxv = jnp.concatenate(dxv_parts, axis=1)
        dpre = dxc * (sg + pre * sg * (1.0 - sg))
        gcb_ref[...] += _bcast8(_colsum(dpre))
        dx_ref[...] = (dxv + _conv_bwd(dpre, x, next8[...], cw_ref, gcw_ref)).astype(BF16)
        next8[...] = dpre[0:SUBLANES, :]

    rev = lambda i: nt - 1 - i
    tok = pl.BlockSpec((tile, d), lambda i: (rev(i), 0))
    g128 = pl.BlockSpec((tile, LANES), lambda i: (rev(i), 0))
    wsh = (nh, dh, dh)
    return _call(
        body, comms, name="mlstm_proj_bwd", grid=(nt,),
        in_specs=[tok, tok, tok, g128, g128, pl.BlockSpec((tile, d), lambda i: (rev(i), 2)), tok,
                  tok, tok, tok, _full(conv_w.shape), _full(wsh), _full(wsh), _full(wsh), _full(wif_t.shape)],
        out_specs=[tok, _full(wsh), _full(wsh), _full(wsh), _full((LANES, 3 * d)), _full((SUBLANES, LANES)),
                   _full((SUBLANES, d)), _full((SUBLANES, d))],
        out_shape=[jax.ShapeDtypeStruct((s, d), BF16)] + [jax.ShapeDtypeStruct(wsh, F32)] * 3
        + [jax.ShapeDtypeStruct((LANES, 3 * d), F32), jax.ShapeDtypeStruct((SUBLANES, LANES), F32),
           jax.ShapeDtypeStruct((SUBLANES, d), F32), jax.ShapeDtypeStruct((SUBLANES, d), F32)],
        scratch_shapes=[pltpu.VMEM((SUBLANES, d), F32)],
        args=(dq, dk, dv, dgates, gcol, u, pre, q, k, v, conv_w, w_q, w_k, w_v, wif_t))


def _rg_bwd(dy_rg, u, h_rg, gates, conv_w, w_a, w_x, lam, tile, comms=None):
    s, d = dy_rg.shape
    nt = s // tile
    nh, dh, _ = w_a.shape

    def body(dy_ref, x_ref, z_ref, h_ref, hhalo_ref, xc_ref, r_ref, i_ref, a_ref, beta_ref, cw_ref, wa_ref,
             wx_ref, lam_ref,
             dx_ref, dz_ref, gwa_ref, gwx_ref, gba_ref, gbx_ref, glam_ref, gcw_ref, gcb_ref, next8, anext, dnext, dbuf):
        i = pl.program_id(0)

        @pl.when(i == 0)
        def _():
            for ref in (next8, anext, dnext, gwa_ref, gwx_ref, gba_ref, gbx_ref, glam_ref, gcw_ref, gcb_ref):
                ref[...] = jnp.zeros_like(ref)

        inner = jnp.where(i < nt - 1, 1.0, 0.0)
        xc, r, ig, a, beta = xc_ref[...], r_ref[...], i_ref[...], a_ref[...], beta_ref[...]
        sp = _softplus(-lam_ref[...])
        h = h_ref[...]
        row = _iota(h.shape, 0)
        hprev = jnp.where(row >= 1, pltpu.roll(h, 1, 0), hhalo_ref[SUBLANES - 1:SUBLANES, :] * inner)
        z = z_ref[...]
        sz = _sigmoid(z)
        dyv = dy_ref[...]
        dz_ref[...] = (dyv * h * (sz + z * sz * (1.0 - sz))).astype(BF16)
        a_up = jnp.where(row < tile - 1, pltpu.roll(a, tile - 1, 0), anext[0:1, :])
        _scan_into(a_up, dyv * z * sz, dnext[0:1, :], dbuf, True)
        delta = dbuf[...]
        anext[...] = a[0:SUBLANES, :]
        dnext[...] = delta[0:SUBLANES, :]
        dla = delta * hprev * a - delta * ig * xc * (a * a / beta)
        glam_ref[...] += _bcast8(_colsum(dla * r) * (RG_C * _sigmoid(-lam_ref[...])))
        dpa = dla * (-RG_C * sp) * r * (1.0 - r)
        dpx = delta * beta * xc * ig * (1.0 - ig)
        gba_ref[...] += _bcast8(_colsum(dpa))
        gbx_ref[...] += _bcast8(_colsum(dpx))
        parts = []
        for hh in range(nh):
            sl = slice(hh * dh, (hh + 1) * dh)
            gwa_ref[hh] += _mm_tn(xc[:, sl], dpa[:, sl])
            gwx_ref[hh] += _mm_tn(xc[:, sl], dpx[:, sl])
            parts.append(_mm_nt(dpa[:, sl], wa_ref[hh]) + _mm_nt(dpx[:, sl], wx_ref[hh]))
        dxc = delta * beta * ig + jnp.concatenate(parts, axis=1)
        gcb_ref[...] += _bcast8(_colsum(dxc))
        dx_ref[...] = _conv_bwd(dxc, x_ref[...], next8[...], cw_ref, gcw_ref).astype(BF16)
        next8[...] = dxc[0:SUBLANES, :]

    rev = lambda i: nt - 1 - i
    tok = pl.BlockSpec((tile, d), lambda i: (rev(i), 0))
    vec = _full((1, d))
    acc = _full((SUBLANES, d))
    wsh = (nh, dh, dh)
    return _call(
        body, comms, name="rglru_bwd", grid=(nt,),
        in_specs=[tok, tok, pl.BlockSpec((tile, d), lambda i: (rev(i), 1)), tok,
                  _halo_spec(d, tile, nt, 0)] + [tok] * 5 + [_full(conv_w.shape), _full(wsh), _full(wsh), vec],
        out_specs=[tok, tok, _full(wsh), _full(wsh), acc, acc, acc, acc, acc],
        out_shape=[jax.ShapeDtypeStruct((s, d), BF16)] * 2 + [jax.ShapeDtypeStruct(wsh, F32)] * 2
        + [jax.ShapeDtypeStruct((SUBLANES, d), F32)] * 5,
        scratch_shapes=[pltpu.VMEM((SUBLANES, d), F32)] * 3 + [pltpu.VMEM((tile, d), F32)],
        args=(dy_rg, u, u, h_rg, h_rg, *gates, conv_w, w_a, w_x, lam))


def _segments(d, w, n_pieces, n_slots):
    bounds = sorted({k * d for k in range(n_pieces + 1)} | {j * w for j in range(n_slots + 1)})
    return [(lo // d, lo % d, lo // w, lo % w, hi - lo) for lo, hi in zip(bounds[:-1], bounds[1:])]


def _in_bwd(pieces, x, dxo, ng, scale, w_in_g, layer, tile, comms=None, tiles=None, prev=None):
    s, d = x.shape
    nd, _, _, w = w_in_g.shape
    segs = _segments(d, w, len(pieces), nd)
    first, count = tiles or (0, s // tile)
    n_p = len(pieces)

    def body(*refs):
        p_refs = refs[:n_p]
        x_ref, dxo_ref, ng_ref, sc_ref, w_ref = refs[n_p:n_p + 5]
        dx_ref, dsc_ref, dsh_ref, gng_ref, wcat = refs[-5:]
        _join_columns(w_ref, wcat)

        @pl.when(pl.program_id(0) == 0)
        def _():
            for k, ref in enumerate((dsc_ref, dsh_ref, gng_ref)):
                ref[...] = jnp.zeros_like(ref) if prev is None else refs[n_p + 6 + k][...]

        du = jnp.concatenate([p[...] for p in p_refs], axis=1)
        dh = lax.dot_general(du, wcat[...], (((1,), (1,)), ((), ())), preferred_element_type=F32)
        xv = x_ref[...]
        g = ng_ref[...]
        rs = lax.rsqrt(jnp.mean(xv * xv, axis=1, keepdims=True) + EPS)
        xh = xv * rs
        dsh_ref[...] += _bcast8(_colsum(dh))
        dsc_ref[...] += _bcast8(_colsum(dh * xh * g))
        dhn = dh * (1.0 + sc_ref[...])
        gng_ref[...] += _bcast8(_colsum(dhn * xh))
        dxh = dhn * g
        dx_ref[...] = dxo_ref[...] + rs * (dxh - xh * jnp.mean(dxh * xh, axis=1, keepdims=True))

    tok = pl.BlockSpec((tile, d), lambda i: (i + first, 0))
    vec = _full((1, d))
    acc = _full((SUBLANES, d))
    more_specs = [] if prev is None else [pl.BlockSpec(memory_space=pl.ANY), acc, acc, acc]
    return _call(
        body, comms, name="in_proj_bwd_x", grid=(count,),
        in_specs=[tok] * n_p + [tok, tok, vec, vec, pl.BlockSpec((nd, 1, d, w), lambda i: (0, layer, 0, 0),
                                                               pipeline_mode=pl.Buffered(1))] + more_specs,
        out_specs=[tok, acc, acc, acc],
        out_shape=[jax.ShapeDtypeStruct((s, d), F32)] + [jax.ShapeDtypeStruct((SUBLANES, d), F32)] * 3,
        scratch_shapes=[pltpu.VMEM((d, nd * w), BF16)],
        args=(*pieces, x, dxo, ng, scale, w_in_g) + (() if prev is None else tuple(prev)),
        aliases={} if prev is None else {n_p + 5: 0})


def _in_bwd_w(pieces, hbf, w, slots, tile, comms=None):
    s, d = hbf.shape
    nd_all = len(pieces) * d // w
    segs = [sg for sg in _segments(d, w, len(pieces), nd_all) if sg[2] in slots]

    def body(*refs):
        p_refs = refs[:len(pieces)]
        h_ref, gw_ref = refs[len(pieces):]

        @pl.when(pl.program_id(0) == 0)
        def _():
            gw_ref[...] = jnp.zeros_like(gw_ref)

        hv = h_ref[...]
        for (kk, a, j, b, width) in segs:
            gw_ref[j - slots[0], :, b:b + width] += _mm_tn(hv, p_refs[kk][:, a:a + width])

    tok = pl.BlockSpec((tile, d), lambda i: (i, 0))
    return _call(
        body, comms, name="in_proj_bwd_w", grid=(s // tile,),
        in_specs=[tok] * len(pieces) + [tok],
        out_specs=[pl.BlockSpec((len(slots), d, w), lambda i: (0, 0, 0), pipeline_mode=pl.Buffered(1))],
        out_shape=[jax.ShapeDtypeStruct((len(slots), d, w), F32)],
        args=(*pieces, hbf))[0]


def _ada_bwd_w(cact_col, dmod, nd):
    d = cact_col.shape[0]
    w = dmod.shape[1] // nd

    def body(c_ref, m_ref, o_ref):
        o_ref[0] = c_ref[...] * m_ref[...]

    return pl.pallas_call(
        body, name="adaln_bwd_w", grid=(nd,),
        in_specs=[_full((d, 1)), pl.BlockSpec((1, w), lambda j: (0, j))],
        out_specs=pl.BlockSpec((1, d, w), lambda j: (j, 0, 0)),
        out_shape=jax.ShapeDtypeStruct((nd, d, w), F32),
        compiler_params=_params(1),
    )(cact_col, dmod)


def _exchange(arrs, gather, name):
    return _run_comms([_exchange_comm(arrs, gather)], name)[0]


def _run_comms(comms, name):
    _call(lambda: None, comms, name=name, grid=(1,), in_specs=[], out_specs=[], out_shape=[], args=())
    return [cm.results for cm in comms]


def _exchange_comm(arrs, gather):
    n = len(arrs)
    per = N_DEV - 1

    def copies(ins, outs, sems):
        send_sems, recv_sems, local_sems = sems
        x, y, c = (lax.axis_index(ax) for ax in MESH_AXES)
        me = 4 * x + 2 * y + c
        sends, recvs = [], []
        for flip in range(1, N_DEV):
            px = x ^ ((flip >> 2) & 1)
            py = y ^ ((flip >> 1) & 1)
            pc = c ^ (flip & 1)
            peer = 4 * px + 2 * py + pc
            for kk in range(n):
                src = ins[kk] if gather else ins[kk].at[peer]
                sends.append(_remote(src, outs[kk].at[me], send_sems, recv_sems, kk * per + flip - 1, (px, py, pc)))
                recvs.append(_remote(src, outs[kk].at[peer], send_sems, recv_sems, kk * per + flip - 1, (px, py, pc)))
        local = [pltpu.make_async_copy(ins[kk] if gather else ins[kk].at[me], outs[kk].at[me], local_sems.at[kk])
                 for kk in range(n)]
        return local, sends, recvs

    def start(ins, outs, sems):
        local, sends, _ = copies(ins, outs, sems)
        for cp in sends + local:
            cp.start()

    def finish(ins, outs, sems):
        local, sends, recvs = copies(ins, outs, sems)
        for cp in recvs:
            cp.wait_recv()
        for cp in sends:
            cp.wait_send()
        for cp in local:
            cp.wait()

    return _Comm(arrs, [jax.ShapeDtypeStruct((N_DEV,) + a.shape if gather else a.shape, a.dtype) for a in arrs],
                 [pltpu.SemaphoreType.DMA((n * per,)), pltpu.SemaphoreType.DMA((n * per,)), pltpu.SemaphoreType.DMA((n,))],
                 start, finish)


def _mesh_place():
    x, y, c = (lax.axis_index(ax) for ax in MESH_AXES)
    return x, y, c, (x, y, 1 - c), [(1 - x, y), (x, 1 - y), (1 - x, 1 - y)]


def _remote(src, dst, send_sems, recv_sems, sem, to):
    return pltpu.make_async_remote_copy(src_ref=src, dst_ref=dst, send_sem=send_sems.at[sem], recv_sem=recv_sems.at[sem],
                                        device_id=to, device_id_type=pl.DeviceIdType.MESH)


def _gather_two_level(arrs, name):
    n = len(arrs)
    per = N_DEV - 1

    def body(*refs):
        ins, outs = refs[:n], refs[n:2 * n]
        send_sems, recv_sems, local_sems = refs[2 * n:]
        x, y, c, sibling, chips = _mesh_place()

        def copy(kk, j, block, to, src=None):
            dst = outs[kk].at[4 * block[0] + 2 * block[1] + block[2]]
            return _remote(dst if src is None else src, dst, send_sems, recv_sems, kk * per + j, to)

        me = (x, y, c)
        local = [pltpu.make_async_copy(ins[kk], outs[kk].at[4 * x + 2 * y + c], local_sems.at[kk]) for kk in range(n)]
        first = []
        for j, chip in enumerate(chips):
            first += [copy(kk, 1 + j, me, (*chip, c), src=ins[kk]) for kk in range(n)]
        first += [copy(kk, 0, me, sibling, src=ins[kk]) for kk in range(n)]
        for cp in first + local:
            cp.start()
        passed = []
        for j, chip in enumerate(chips):
            for kk in range(n):
                copy(kk, 1 + j, (*chip, c), me).wait_recv()
                passed.append(copy(kk, 4 + j, (*chip, c), sibling))
                passed[-1].start()
        for kk in range(n):
            copy(kk, 0, sibling, me).wait_recv()
        for j, chip in enumerate(chips):
            for kk in range(n):
                copy(kk, 4 + j, (*chip, 1 - c), me).wait_recv()
        for cp in first + passed:
            cp.wait_send()
        for cp in local:
            cp.wait()

    return pl.pallas_call(
        body, name=name,
        in_specs=[pl.BlockSpec(memory_space=pl.ANY)] * n, out_specs=[pl.BlockSpec(memory_space=pl.ANY)] * n,
        out_shape=[jax.ShapeDtypeStruct((N_DEV,) + a.shape, a.dtype) for a in arrs],
        scratch_shapes=[pltpu.SemaphoreType.DMA((n * per,)), pltpu.SemaphoreType.DMA((n * per,)),
                        pltpu.SemaphoreType.DMA((n,))],
    )(*arrs)


N_CHIPS = N_DEV // 2


def _core_swap(arrs, name):
    n = len(arrs)

    def body(*refs):
        ins, outs = refs[:n], refs[n:2 * n]
        send_sems, recv_sems = refs[2 * n:]
        _, _, c, sibling, _ = _mesh_place()
        copies = [_remote(ins[kk].at[2 * q + (1 - c)], outs[kk].at[q], send_sems, recv_sems, kk * N_CHIPS + q, sibling)
                  for q in range(N_CHIPS) for kk in range(n)]
        for cp in copies:
            cp.start()
        for cp in copies:
            cp.wait_recv()
        for cp in copies:
            cp.wait_send()

    return pl.pallas_call(
        body, name=name,
        in_specs=[pl.BlockSpec(memory_space=pl.ANY)] * n, out_specs=[pl.BlockSpec(memory_space=pl.ANY)] * n,
        out_shape=[jax.ShapeDtypeStruct((N_CHIPS,) + a.shape[1:], a.dtype) for a in arrs],
        scratch_shapes=[pltpu.SemaphoreType.DMA((n * N_CHIPS,)), pltpu.SemaphoreType.DMA((n * N_CHIPS,))],
    )(*arrs)


def _pair_sum(a, other, parity, name):
    _, r, c = a.shape
    tr = _row_tile(r, c, 3)

    def body(p_ref, a_ref, o_ref, s_ref):
        s_ref[...] = (a_ref[...] + o_ref[...]).astype(BF16)

    return pl.pallas_call(
        body, name=name,
        grid_spec=pltpu.PrefetchScalarGridSpec(
            num_scalar_prefetch=1, grid=(N_CHIPS, r // tr),
            in_specs=[pl.BlockSpec((1, tr, c), lambda q, i, p: (2 * q + p[0], i, 0)),
                      pl.BlockSpec((1, tr, c), lambda q, i, p: (q, i, 0))],
            out_specs=pl.BlockSpec((1, tr, c), lambda q, i, p: (q, i, 0))),
        out_shape=jax.ShapeDtypeStruct((N_CHIPS, r, c), BF16),
        compiler_params=_params(2),
    )(parity, a, other)


def _chip_swap(arrs, name):
    n = len(arrs)
    per = N_CHIPS - 1

    def body(*refs):
        ins, outs = refs[:n], refs[n:2 * n]
        send_sems, recv_sems, local_sems = refs[2 * n:]
        x, y, c, _, chips = _mesh_place()
        mine = 2 * x + y
        sends = [_remote(ins[kk].at[2 * chip[0] + chip[1]], outs[kk].at[mine], send_sems, recv_sems, kk * per + j, (*chip, c))
                 for j, chip in enumerate(chips) for kk in range(n)]
        recvs = [_remote(ins[kk].at[mine], outs[kk].at[2 * chip[0] + chip[1]], send_sems, recv_sems, kk * per + j, (*chip, c))
                 for j, chip in enumerate(chips) for kk in range(n)]
        local = [pltpu.make_async_copy(ins[kk].at[mine], outs[kk].at[mine], local_sems.at[kk]) for kk in range(n)]
        for cp in sends + local:
            cp.start()
        for cp in recvs:
            cp.wait_recv()
        for cp in sends:
            cp.wait_send()
        for cp in local:
            cp.wait()

    return pl.pallas_call(
        body, name=name,
        in_specs=[pl.BlockSpec(memory_space=pl.ANY)] * n, out_specs=[pl.BlockSpec(memory_space=pl.ANY)] * n,
        out_shape=[jax.ShapeDtypeStruct(a.shape, a.dtype) for a in arrs],
        scratch_shapes=[pltpu.SemaphoreType.DMA((n * per,)), pltpu.SemaphoreType.DMA((n * per,)),
                        pltpu.SemaphoreType.DMA((n,))],
    )(*arrs)


def _adam_math(w, g, m, v):
    m = ADAM_B1 * m + (1.0 - ADAM_B1) * g
    v = ADAM_B2 * v + (1.0 - ADAM_B2) * (g * g)
    m_hat = m / (1.0 - ADAM_B1 ** ADAM_STEP)
    v_hat = v / (1.0 - ADAM_B2 ** ADAM_STEP)
    delta = -ADAM_LR * (m_hat / (jnp.sqrt(v_hat) + ADAM_EPS) + ADAM_WD * w)
    return delta, m, v


def _sum_devices(r_ref):
    acc = r_ref[0].astype(F32)
    for p in range(1, r_ref.shape[0]):
        acc = acc + r_ref[p].astype(F32)
    return acc


def _row_tile(rows, cols, n_bufs):
    budget = 24 * 1024 * 1024 // (n_bufs * 2 * cols * 4)
    t = rows
    while t > budget and t % 2 == 0 and (t // 2) % SUBLANES == 0:
        t //= 2
    return t


def _reduce_adam(recvs, w, m, v, name):
    nl, r, c = w.shape
    n_part = recvs[0].shape[0]
    tr = _row_tile(r, c, n_part * nl + 7)
    nt = r // tr

    def body(*refs):
        r_refs = refs[:nl]
        w_ref, m_ref, v_ref, g_ref, d_ref, mo_ref, vo_ref = refs[nl:]
        layer = pl.program_id(0)
        g = _sum_devices(r_refs[0])
        for ll in range(1, nl):
            g = jnp.where(layer == ll, _sum_devices(r_refs[ll]), g)
        delta, m2, v2 = _adam_math(w_ref[0], g, m_ref[0], v_ref[0])
        g_ref[0] = g
        d_ref[0] = delta
        mo_ref[0] = m2
        vo_ref[0] = v2

    def rspec(ll):
        return pl.BlockSpec((n_part, tr, c), lambda l, i: (0, jnp.where(l == ll, i, jnp.where(l < ll, 0, nt - 1)), 0))

    blk = pl.BlockSpec((1, tr, c), lambda l, i: (l, i, 0))
    return pl.pallas_call(
        body, name=name, grid=(nl, nt),
        in_specs=[rspec(ll) for ll in range(nl)] + [blk, blk, blk],
        out_specs=[blk] * 4,
        out_shape=[jax.ShapeDtypeStruct((nl, r, c), F32)] * 4,
        compiler_params=_params(2),
    )(*recvs, w, m, v)


def _sum8(recv, name):
    _, r, c = recv.shape

    def body(r_ref, o_ref):
        o_ref[...] = _sum_devices(r_ref)

    return pl.pallas_call(
        body, name=name, grid=(1,),
        in_specs=[_full(recv.shape)], out_specs=_full((r, c)),
        out_shape=jax.ShapeDtypeStruct((r, c), F32), compiler_params=_params(1),
    )(recv)


def _adam_call(w, g, m, v, name):
    r, c = w.shape

    def body(w_ref, g_ref, m_ref, v_ref, d_ref, mo_ref, vo_ref):
        delta, m2, v2 = _adam_math(w_ref[...], g_ref[...], m_ref[...], v_ref[...])
        d_ref[...] = delta
        mo_ref[...] = m2
        vo_ref[...] = v2

    return pl.pallas_call(
        body, name=name, grid=(1,),
        in_specs=[_full((r, c))] * 4, out_specs=[_full((r, c))] * 3,
        out_shape=[jax.ShapeDtypeStruct((r, c), F32)] * 3, compiler_params=_params(1),
    )(w, g, m, v)


def _tile_for(s, want):
    return min(want, s)


def _local_step_whole(x, c, target, wts):
    s, d = x.shape
    nl = wts["w_in_g"].shape[1]
    nd = wts["w_in_g"].shape[0]
    nh_ml = wts["w_qkv"].shape[2]
    t_big = _tile_for(s, 512)
    t_mid = _tile_for(s, 256)

    mod, cact = _mod_call(c, wts["w_ada_g"], wts["b_ada"])
    row = lambda a: a.reshape(1, -1)
    saved = []
    xl = x
    for l in range(nl):
        shift, scale, gate = (row(mod[l, kk * d:(kk + 1) * d]) for kk in range(3))
        u, hbf = _in_fwd(xl, row(wts["norm_g"][l]), scale, shift, wts["w_in_g"], l, t_mid)
        h_rg, y_rg = _rg_fwd(u, d, wts["rg_conv_w"][l], row(wts["rg_conv_b"][l]), wts["rg_w_a_bf"][l],
                             row(wts["rg_b_a"][l]), wts["rg_w_x_bf"][l], row(wts["rg_b_x"][l]),
                             row(wts["rg_lambda"][l]), t_mid)
        q, k, v, gcol = _ml_pre(u, d, wts["ml_conv_w"][l], row(wts["ml_conv_b"][l]), wts["w_qkv"][l, 0],
                                wts["w_qkv"][l, 1], wts["w_qkv"][l, 2], wts["wif_pad"][l], wts["bif_pad"][l], t_mid)
        grow = gcol[:, 0:16].T
        cell, y_ml, cs, ns, ms, mt = _ml_cell_fwd(q, k, v, gcol, grow, u, row(wts["ml_norm_g"][l]), nh_ml)
        x_new, y = _out_fwd(xl, y_rg, y_ml, gate, wts["w_out_g"], l, t_big)
        saved.append(dict(x=xl, u=u, hbf=hbf, h_rg=h_rg, y_rg=y_rg, q=q, k=k, v=v, gcol=gcol, grow=grow, cell=cell,
                          y_ml=y_ml, cs=cs, ns=ns, ms=ms, mt=mt, y=y, scale=scale, gate=gate))
        xl = x_new

    dx, loss_p, g_final = _loss_call(xl, row(wts["final_g"]), target, t_big)
    grads = [None] * nl
    cact_col = cact[0].reshape(d, 1)
    for l in reversed(range(nl)):
        sv = saved[l]
        dy_rg, dy_ml, gw_out, dgate = _out_bwd(dx, sv["gate"], sv["y"], sv["y_rg"], sv["y_ml"], wts["w_out_g"], l, t_big)
        dq, dk, dv, dgates, d_mlo, d_mlz, g_mlng = _ml_cell_bwd(
            dy_ml, sv["u"], sv["cell"], sv["q"], sv["k"], sv["v"], sv["gcol"], sv["grow"], sv["mt"], sv["cs"],
            sv["ns"], sv["ms"], row(wts["ml_norm_g"][l]), nh_ml)
        d_mlx, g_wq, g_wk, g_wv, g_wift, g_bif, g_mlcw, g_mlcb = _ml_pre_bwd(
            dq, dk, dv, dgates, sv["gcol"], sv["u"], sv["q"], sv["k"], sv["v"], wts["ml_conv_w"][l],
            row(wts["ml_conv_b"][l]), wts["w_qkv"][l, 0], wts["w_qkv"][l, 1], wts["w_qkv"][l, 2], wts["wift_pad"][l], t_mid)
        d_rgx, d_rgz, g_wa, g_wx, g_ba, g_bx, g_lam, g_rgcw, g_rgcb = _rg_bwd(
            dy_rg, sv["u"], sv["h_rg"], wts["rg_conv_w"][l], row(wts["rg_conv_b"][l]), wts["rg_w_a_bf"][l],
            row(wts["rg_b_a"][l]), wts["rg_w_x_bf"][l], row(wts["rg_b_x"][l]), row(wts["rg_lambda"][l]), t_mid)
        pieces = [d_rgx, d_rgz, d_mlx, d_mlo, d_mlz]
        dx, dscale, dshift, g_ng = _in_bwd(pieces, sv["x"], dx, row(wts["norm_g"][l]), sv["scale"], wts["w_in_g"], l, t_mid)
        half = nd // 2
        w_cols = wts["w_in_g"].shape[3]
        gw_in = jnp.concatenate([_in_bwd_w(pieces, sv["hbf"], w_cols, tuple(range(0, half)), t_big),
                                 _in_bwd_w(pieces, sv["hbf"], w_cols, tuple(range(half, nd)), t_big)], axis=0)
        dmod = jnp.concatenate([dshift[0:1], dscale[0:1], dgate[0:1]], axis=1)
        gw_ada = _ada_bwd_w(cact_col, dmod, nd)
        grads[l] = dict(w_ada=gw_ada, w_in=gw_in, w_out=gw_out, w_qkv=jnp.stack([g_wq, g_wk, g_wv]),
                        rg_conv_w=g_rgcw[0:CONV_WIDTH], ml_conv_w=g_mlcw[0:CONV_WIDTH], wif_t=g_wift[0:8],
                        norm_g=g_ng[0], b_ada=dmod[0], rg_conv_b=g_rgcb[0], rg_w_a=g_wa, rg_b_a=g_ba[0], rg_w_x=g_wx,
                        rg_b_x=g_bx[0], rg_lambda=g_lam[0], ml_conv_b=g_mlcb[0], ml_b_if=g_bif[0, 0:8],
                        ml_norm_g=g_mlng[0])
    return loss_p[0, 0], dx, grads, g_final[0]


REPLICATED = ("norm_g", "b_ada", "rg_conv_b", "rg_w_a", "rg_b_a", "rg_w_x", "rg_b_x", "rg_lambda", "ml_conv_b",
              "ml_b_if", "ml_norm_g", "final_g")
ROW_ALIGN = N_DEV * SUBLANES


def _to_rows(a):
    flat = a.reshape(-1)
    pad = (-flat.shape[0]) % LANES
    return jnp.pad(flat, (0, pad)).reshape(-1, LANES)


def _pack(arrays):
    rows = jnp.concatenate([_to_rows(a) for a in arrays], axis=0)
    return jnp.pad(rows, ((0, (-rows.shape[0]) % ROW_ALIGN), (0, 0)))


def _unpack(rows, like):
    out, at = [], 0
    for a in like:
        n = -(-a.size // LANES)
        out.append(rows[at:at + n].reshape(-1)[:a.size].reshape(a.shape))
        at += n
    return out


def _small_pack(rg_conv_w, ml_conv_w, ml_w_if):
    nl = rg_conv_w.shape[0]
    wif_t = jnp.swapaxes(ml_w_if, 1, 2).reshape(nl, -1, LANES)
    return jnp.concatenate([rg_conv_w, ml_conv_w, wif_t], axis=1)


def _small_unpack(p, if_rows):
    nl = p.shape[0]
    rg_cw = p[:, 0:CONV_WIDTH]
    ml_cw = p[:, CONV_WIDTH:2 * CONV_WIDTH]
    wif = jnp.swapaxes(p[:, 2 * CONV_WIDTH:].reshape(nl, 8, if_rows), 1, 2)
    return rg_cw, ml_cw, wif


def _assemble_weights(big, small, rep):
    w_ada_g, w_in_g, w_out_g, qkv_g = big
    nd, nl = small.shape[0], small.shape[1]
    d = w_in_g.shape[2]
    dh = qkv_g.shape[3]
    nh = d // dh
    rsh = qkv_g.shape[2] // (3 * nh)
    w_qkv = qkv_g.reshape(nd, nl, 3, nh, rsh, dh).transpose(1, 2, 3, 0, 4, 5).reshape(nl, 3, nh, nd * rsh, dh)
    cw = small[:, :, 0:2 * CONV_WIDTH].reshape(nd, nl, 2, CONV_WIDTH, LANES).transpose(1, 2, 3, 0, 4)
    cw = cw.reshape(nl, 2, CONV_WIDTH, nd * LANES)
    if_rows = (small.shape[2] - 2 * CONV_WIDTH) * LANES // 8
    wif_t = small[:, :, 2 * CONV_WIDTH:].reshape(nd, nl, 8, if_rows).transpose(1, 2, 0, 3).reshape(nl, 8, nd * if_rows)
    wift_pad = jnp.pad(wif_t, ((0, 0), (0, LANES - 8), (0, 0))).astype(BF16)
    wif_pad = jnp.swapaxes(wift_pad, 1, 2)
    bif_pad = jnp.pad(rep["ml_b_if"], ((0, 0), (0, LANES - 8))).reshape(nl, 1, LANES)
    wts = dict(rep)
    wts.update(w_ada_g=w_ada_g, w_in_g=w_in_g, w_out_g=w_out_g, w_qkv=w_qkv, rg_conv_w=cw[:, 0], ml_conv_w=cw[:, 1],
               wif_pad=wif_pad, wift_pad=wift_pad, bif_pad=bif_pad, rg_w_a_bf=rep["rg_w_a"].astype(BF16),
               rg_w_x_bf=rep["rg_w_x"].astype(BF16))
    return wts


def _qkv_slots(g_qkv, nd):
    three, nh, dh, _ = g_qkv.shape
    return g_qkv.reshape(three, nh, nd, dh // nd, dh).transpose(2, 0, 1, 3, 4).reshape(nd, three * nh * (dh // nd), dh)


def _small_slots(g):
    nd = N_DEV
    cw = jnp.stack([g["rg_conv_w"], g["ml_conv_w"]]).reshape(2, CONV_WIDTH, nd, LANES).transpose(2, 0, 1, 3)
    cw = cw.reshape(nd, 2 * CONV_WIDTH, LANES)
    wif = g["wif_t"].reshape(8, nd, -1).transpose(1, 0, 2).reshape(nd, -1, LANES)
    return jnp.concatenate([cw, wif], axis=1)


def _kernel_unhosted(x, c, norm_g, w_ada, b_ada, w_in, rg_conv_w, rg_conv_b, rg_w_a, rg_b_a, rg_w_x, rg_b_x, rg_lambda, ml_conv_w, ml_conv_b, ml_w_q, ml_w_k, ml_w_v, ml_w_if, ml_b_if, ml_norm_g, w_out, final_g, loss_target, m_norm_g, m_w_ada, m_b_ada, m_w_in, m_rg_conv_w, m_rg_conv_b, m_rg_w_a, m_rg_b_a, m_rg_w_x, m_rg_b_x, m_rg_lambda, m_ml_conv_w, m_ml_conv_b, m_ml_w_q, m_ml_w_k, m_ml_w_v, m_ml_w_if, m_ml_b_if, m_ml_norm_g, m_w_out, m_final_g, v_norm_g, v_w_ada, v_b_ada, v_w_in, v_rg_conv_w, v_rg_conv_b, v_rg_w_a, v_rg_b_a, v_rg_w_x, v_rg_b_x, v_rg_lambda, v_ml_conv_w, v_ml_conv_b, v_ml_w_q, v_ml_w_k, v_ml_w_v, v_ml_w_if, v_ml_b_if, v_ml_norm_g, v_w_out, v_final_g):
    given = dict(locals())
    nl = w_in.shape[0]
    rep = {n: given[n] for n in REPLICATED}

    def qkv_shard(prefix):
        return jnp.stack([given[prefix + "ml_w_q"], given[prefix + "ml_w_k"], given[prefix + "ml_w_v"]], axis=1).reshape(
            nl, -1, ml_w_q.shape[-1])

    *big, small = _gather_two_level(
        [w_ada.astype(BF16), w_in.astype(BF16), w_out.astype(BF16), qkv_shard("").astype(BF16),
         _small_pack(rg_conv_w, ml_conv_w, ml_w_if)], "gather_weights")
    wts = _assemble_weights(big, small, rep)

    loss_p, grad_x, grads, g_final = _local_step(x[0], c, loss_target[0], wts)
    loss = lax.psum(loss_p, MESH_AXES)

    keys = ("w_ada", "w_in", "w_out", "w_qkv", "small")
    parity = lax.axis_index("c").astype(jnp.int32).reshape(1)
    recv = []
    for l in range(nl):
        g = grads[l]
        parts = [g["w_ada"], g["w_in"], g["w_out"], _qkv_slots(g["w_qkv"], N_DEV), _small_slots(g)]
        other = _core_swap(parts, "core_swap_layer%d" % l)
        sums = [_pair_sum(a, o, parity, "pair_sum_%s_layer%d" % (key, l)) for key, a, o in zip(keys, parts, other)]
        recv.append(_chip_swap(sums, "chip_swap_layer%d" % l))
    shard = {"": dict(w_ada=w_ada, w_in=w_in, w_out=w_out, w_qkv=qkv_shard(""),
                      small=_small_pack(rg_conv_w, ml_conv_w, ml_w_if))}
    for p in ("m_", "v_"):
        shard[p] = dict(w_ada=given[p + "w_ada"], w_in=given[p + "w_in"], w_out=given[p + "w_out"], w_qkv=qkv_shard(p),
                        small=_small_pack(given[p + "rg_conv_w"], given[p + "ml_conv_w"], given[p + "ml_w_if"]))
    res = {}
    for ki, key in enumerate(keys):
        res[key] = _reduce_adam([recv[l][ki] for l in range(nl)], shard[""][key], shard["m_"][key], shard["v_"][key],
                                "reduce_adam_" + key)

    rep_g = dict(final_g=g_final)
    for n in REPLICATED[:-1]:
        rep_g[n] = jnp.stack([grads[l][n] for l in range(nl)])
    pack_g = _pack([rep_g[n] for n in REPLICATED])
    rows = pack_g.shape[0] // N_DEV
    mine = _sum8(_exchange([pack_g.reshape(N_DEV, rows, LANES)], False, "reduce_scatter_replicated")[0], "sum_replicated")
    g_rep = _exchange([mine], True, "gather_replicated")[0].reshape(N_DEV * rows, LANES)
    rep_like = [rep[n] for n in REPLICATED]
    d_rep, m_rep, v_rep = _adam_call(_pack(rep_like), g_rep, _pack([given["m_" + n] for n in REPLICATED]),
                                     _pack([given["v_" + n] for n in REPLICATED]), "adam_replicated")
    rep_out = [dict(zip(REPLICATED, _unpack(a, rep_like))) for a in (g_rep, d_rep, m_rep, v_rep)]

    if_rows = ml_w_if.shape[1]
    order = ("norm_g", "w_ada", "b_ada", "w_in", "rg_conv_w", "rg_conv_b", "rg_w_a", "rg_b_a", "rg_w_x", "rg_b_x",
             "rg_lambda", "ml_conv_w", "ml_conv_b", "ml_w_q", "ml_w_k", "ml_w_v", "ml_w_if", "ml_b_if", "ml_norm_g",
             "w_out", "final_g")
    outs = [loss, grad_x[None]]
    for kind in range(4):
        qkv = res["w_qkv"][kind].reshape((nl, 3) + ml_w_q.shape[1:])
        rg_cw, ml_cw, wif = _small_unpack(res["small"][kind], if_rows)
        sharded = dict(w_ada=res["w_ada"][kind], w_in=res["w_in"][kind], w_out=res["w_out"][kind], ml_w_q=qkv[:, 0],
                       ml_w_k=qkv[:, 1], ml_w_v=qkv[:, 2], rg_conv_w=rg_cw, ml_conv_w=ml_cw, ml_w_if=wif)
        for n in order:
            outs.append(sharded[n] if n in sharded else rep_out[kind][n])
    return tuple(outs)


def _slot(block):
    return 4 * block[0] + 2 * block[1] + block[2]


def _dma_sems(*counts):
    return [pltpu.SemaphoreType.DMA((n,)) for n in counts]


def _start_all(copies):
    for cp in copies:
        cp.start()


def _gather_ici_comm(arrs):
    n = len(arrs)

    def copies(ins, outs, sems):
        send_sems, recv_sems, local_sems = sems
        x, y, c, sibling, chips = _mesh_place()
        me = (x, y, c)
        peers = [(*chip, c) for chip in chips] + [sibling]
        local = [pltpu.make_async_copy(ins[kk], outs[kk].at[_slot(me)], local_sems.at[kk]) for kk in range(n)]
        sends = [_remote(ins[kk], outs[kk].at[_slot(me)], send_sems, recv_sems, kk * 4 + j, peer)
                 for j, peer in enumerate(peers) for kk in range(n)]
        recvs = [_remote(ins[kk], outs[kk].at[_slot(peer)], send_sems, recv_sems, kk * 4 + j, peer)
                 for j, peer in enumerate(peers) for kk in range(n)]
        return local, sends, recvs

    def start(ins, outs, sems):
        local, sends, _ = copies(ins, outs, sems)
        _start_all(sends + local)

    def finish(ins, outs, sems):
        local, sends, recvs = copies(ins, outs, sems)
        for cp in recvs:
            cp.wait_recv()
        for cp in sends:
            cp.wait_send()
        for cp in local:
            cp.wait()

    return _Comm(arrs, [jax.ShapeDtypeStruct((N_DEV,) + a.shape, a.dtype) for a in arrs], _dma_sems(4 * n, 4 * n, n),
                 start, finish)


def _gather_fwd_comm(bufs):
    n = len(bufs)

    def copies(ins, outs, sems):
        send_sems, recv_sems = sems
        _, _, c, sibling, chips = _mesh_place()
        sends = [_remote(ins[kk].at[_slot((*chip, c))], outs[kk].at[_slot((*chip, c))], send_sems, recv_sems, kk * 3 + j, sibling)
                 for j, chip in enumerate(chips) for kk in range(n)]
        recvs = [_remote(ins[kk].at[_slot((*chip, c))], outs[kk].at[_slot((*chip, 1 - c))], send_sems, recv_sems, kk * 3 + j, sibling)
                 for j, chip in enumerate(chips) for kk in range(n)]
        return sends, recvs

    def start(ins, outs, sems):
        _start_all(copies(ins, outs, sems)[0])

    def finish(ins, outs, sems):
        sends, recvs = copies(ins, outs, sems)
        for cp in recvs:
            cp.wait_recv()
        for cp in sends:
            cp.wait_send()

    return _Comm(bufs, [jax.ShapeDtypeStruct(a.shape, a.dtype) for a in bufs], _dma_sems(3 * n, 3 * n), start, finish,
                 aliases=[(i, i) for i in range(n)])


def _core_swap_comm(arrs):
    n = len(arrs)

    def copies(ins, outs, sems):
        send_sems, recv_sems = sems
        _, _, c, sibling, _ = _mesh_place()
        return [_remote(ins[kk].at[2 * q + (1 - c)], outs[kk].at[q], send_sems, recv_sems, kk * N_CHIPS + q, sibling)
                for q in range(N_CHIPS) for kk in range(n)]

    def start(ins, outs, sems):
        _start_all(copies(ins, outs, sems))

    def finish(ins, outs, sems):
        cps = copies(ins, outs, sems)
        for cp in cps:
            cp.wait_recv()
        for cp in cps:
            cp.wait_send()

    return _Comm(arrs, [jax.ShapeDtypeStruct((N_CHIPS,) + a.shape[1:], a.dtype) for a in arrs],
                 _dma_sems(N_CHIPS * n, N_CHIPS * n), start, finish)


def _chip_swap_comm(arrs):
    n = len(arrs)
    per = N_CHIPS - 1

    def copies(ins, outs, sems):
        send_sems, recv_sems, local_sems = sems
        x, y, c, _, chips = _mesh_place()
        mine = 2 * x + y
        sends = [_remote(ins[kk].at[2 * chip[0] + chip[1]], outs[kk].at[mine], send_sems, recv_sems, kk * per + j, (*chip, c))
                 for j, chip in enumerate(chips) for kk in range(n)]
        recvs = [_remote(ins[kk].at[mine], outs[kk].at[2 * chip[0] + chip[1]], send_sems, recv_sems, kk * per + j, (*chip, c))
                 for j, chip in enumerate(chips) for kk in range(n)]
        local = [pltpu.make_async_copy(ins[kk].at[mine], outs[kk].at[mine], local_sems.at[kk]) for kk in range(n)]
        return local, sends, recvs

    def start(ins, outs, sems):
        local, sends, _ = copies(ins, outs, sems)
        _start_all(sends + local)

    def finish(ins, outs, sems):
        local, sends, recvs = copies(ins, outs, sems)
        for cp in recvs:
            cp.wait_recv()
        for cp in sends:
            cp.wait_send()
        for cp in local:
            cp.wait()

    return _Comm(arrs, [jax.ShapeDtypeStruct(a.shape, a.dtype) for a in arrs], _dma_sems(per * n, per * n, n), start, finish)


def _ada_mod(c_all, w_ada, b_cols, comms=None):
    nl, d, w = w_ada.shape

    def body(c_ref, w_ref, b_ref, m_ref, ca_ref):
        sub = _iota((SUBLANES, d), 0)
        cv = jnp.zeros((SUBLANES, d), F32)
        for b in range(N_DEV):
            cv = jnp.where(sub == b, c_ref[b], cv)
        ca = cv * _sigmoid(cv)
        ca_ref[...] = ca
        m_ref[...] = jnp.zeros_like(m_ref)
        for l in range(nl):
            ml = _mm_hi(ca, w_ref[l]) + b_ref[l:l + 1, :]
            for b in range(N_DEV):
                m_ref[b, l:l + 1, :] = _row(ml, b)

    return _call(
        body, comms, name="adaln_mod_columns", grid=(1,),
        in_specs=[_full(c_all.shape), _full(w_ada.shape), _full(b_cols.shape)],
        out_specs=[_full((N_DEV, SUBLANES, w)), _full((SUBLANES, d))],
        out_shape=[jax.ShapeDtypeStruct((N_DEV, SUBLANES, w), F32), jax.ShapeDtypeStruct((SUBLANES, d), F32)],
        args=(c_all, w_ada, b_cols))


def _ada_grad_adam(cact_t, dmods, w, m, v):
    nl, d, wd = w.shape
    tr = _row_tile(d, wd, 8)

    def body(c_ref, dm_ref, w_ref, m_ref, v_ref, g_ref, d_ref, mo_ref, vo_ref):
        cv = c_ref[...]
        dm = dm_ref[0]
        g = _col(cv, 0) * _row(dm, 0)
        for b in range(1, N_DEV):
            g = g + _col(cv, b) * _row(dm, b)
        delta, m2, v2 = _adam_math(w_ref[0], g, m_ref[0], v_ref[0])
        g_ref[0] = g
        d_ref[0] = delta
        mo_ref[0] = m2
        vo_ref[0] = v2

    blk = pl.BlockSpec((1, tr, wd), lambda l, i: (l, i, 0))
    return pl.pallas_call(
        body, name="adaln_grad_adam", grid=(nl, d // tr),
        in_specs=[pl.BlockSpec((tr, N_DEV), lambda l, i: (i, 0)), pl.BlockSpec((1, N_DEV, wd), lambda l, i: (l, 0, 0)),
                  blk, blk, blk],
        out_specs=[blk] * 4, out_shape=[jax.ShapeDtypeStruct((nl, d, wd), F32)] * 4,
        compiler_params=_params(2),
    )(cact_t, dmods, w, m, v)


REP_ROWS = ("norm_g", "dshift", "dscale", "dgate", "rg_conv_b", "rg_b_a", "rg_b_x", "rg_lambda", "ml_conv_b", "ml_norm_g",
            "ml_b_if")


def _sum_parts(recvs, name):
    def body(*refs):
        for r_ref, o_ref in zip(refs[:len(recvs)], refs[len(recvs):]):
            o_ref[...] = _sum_devices(r_ref)

    return pl.pallas_call(
        body, name=name, grid=(1,),
        in_specs=[_full(r.shape) for r in recvs], out_specs=[_full(r.shape[1:]) for r in recvs],
        out_shape=[jax.ShapeDtypeStruct(r.shape[1:], F32) for r in recvs], compiler_params=_params(1),
    )(*recvs)


def _adam_replicated(vp, mp, params, nl):
    d = vp.shape[1]
    nr = len(REP_ROWS)
    names = list(params)
    mat_shape = params["rg_w_a"][0].shape[1:]
    mat_rows = mp.shape[0] // (2 * nl)

    def pieces(name):
        if name == "final_g":
            return [(lambda vp_ref, mp_ref: vp_ref[nl * nr:nl * nr + 1, :], (slice(0, 1), slice(None)))]
        out = []
        for l in range(nl):
            if name in ("rg_w_a", "rg_w_x"):
                at = (2 * l + (name == "rg_w_x")) * mat_rows
                out.append((lambda vp_ref, mp_ref, at=at: mp_ref[at:at + mat_rows, :].reshape(mat_shape), l))
            elif name == "b_ada":
                for j in range(3):
                    r = l * nr + 1 + j
                    out.append((lambda vp_ref, mp_ref, r=r: vp_ref[r:r + 1, :], (slice(l, l + 1), slice(j * d, (j + 1) * d))))
            else:
                r = l * nr + REP_ROWS.index(name)
                cols = slice(0, LANES) if name == "ml_b_if" else slice(None)
                out.append((lambda vp_ref, mp_ref, r=r, cols=cols: vp_ref[r:r + 1, cols], (slice(l, l + 1), slice(None))))
        return out

    def body(*refs):
        vp_ref, mp_ref = refs[:2]
        ins, outs = refs[2:2 + 3 * len(names)], refs[2 + 3 * len(names):]
        for pi, name in enumerate(names):
            w_ref, m_ref, v_ref = ins[3 * pi:3 * pi + 3]
            g_ref, d_ref, mo_ref, vo_ref = outs[4 * pi:4 * pi + 4]
            for get, idx in pieces(name):
                g = get(vp_ref, mp_ref)
                delta, m2, v2 = _adam_math(w_ref[idx], g, m_ref[idx], v_ref[idx])
                g_ref[idx] = g
                d_ref[idx] = delta
                mo_ref[idx] = m2
                vo_ref[idx] = v2

    flat = [a for name in names for a in params[name]]
    out_shape = [jax.ShapeDtypeStruct(params[name][0].shape, F32) for name in names for _ in range(4)]
    res = pl.pallas_call(
        body, name="adam_replicated", grid=(1,),
        in_specs=[_full(vp.shape), _full(mp.shape)] + [_full(a.shape) for a in flat],
        out_specs=[_full(o.shape) for o in out_shape], out_shape=out_shape, compiler_params=_params(1),
    )(vp, mp, *flat)
    return {name: res[4 * pi:4 * pi + 4] for pi, name in enumerate(names)}


class _Plan:
    def __init__(self):
        self.hosted, self.after = {}, {}

    def host(self, key, comm, then=None):
        self.hosted.setdefault(key, []).append(comm)
        if then is not None:
            self.after.setdefault(key, []).append(then)

    def comms(self, key):
        return self.hosted.pop(key, None)

    def done(self, key):
        for fn in self.after.pop(key, []):
            fn()

    def flush(self):
        while self.hosted:
            key = next(iter(self.hosted))
            _call(lambda: None, self.comms(key), name="exchange_after_%s_%d" % key, grid=(1,), in_specs=[], out_specs=[],
                  out_shape=[], args=())
            self.done(key)


VEC_TABLE = ("norm_g", "rg_conv_b", "rg_b_a", "rg_b_x", "rg_lambda", "ml_conv_b", "ml_norm_g")


def _vec_table(rep):
    rows = [rep[n] for n in VEC_TABLE]
    return jnp.stack(rows + [jnp.zeros_like(rows[0])] * (SUBLANES - len(rows)), axis=1)


def _layer_fwd(l, xl, mod3, wl, rep, plan):
    s, d = xl.shape
    t_big, t_mid = _tile_for(s, 512), _tile_for(s, 256)
    nh_ml = rep["ml_b_if"].shape[1] // 2
    vec = lambda name: _vec(rep["vecs"], l, VEC_TABLE.index(name))
    shift, scale, gate = (_vec(mod3, l, kk) for kk in range(3))
    hosted = lambda name: plan.comms((name, l)) if plan else None
    done = lambda name: plan.done((name, l)) if plan else None
    u, hbf = _in_fwd(xl, vec("norm_g"), scale, shift, wl["w_in_g"], 0, t_mid, hosted("in_proj_fwd"))
    done("in_proj_fwd")
    h_rg, y_rg, *rg_gates = _rg_fwd(u, d, wl["rg_conv_w"], vec("rg_conv_b"), rep["rg_w_a_bf"][l], vec("rg_b_a"),
                                    rep["rg_w_x_bf"][l], vec("rg_b_x"), vec("rg_lambda"), t_mid, hosted("rglru_fwd"))
    done("rglru_fwd")
    q, k, v, gcol, pre = _ml_pre(u, d, wl["ml_conv_w"], vec("ml_conv_b"), wl["w_qkv"][0], wl["w_qkv"][1],
                                 wl["w_qkv"][2], wl["wif_pad"], wl["bif_pad"], t_mid, hosted("mlstm_proj_fwd"))
    done("mlstm_proj_fwd")
    grow = gcol[:, 0:16].T
    cell, y_ml, cs, ns, ms, mt = _ml_cell_fwd(q, k, v, gcol, grow, u, vec("ml_norm_g"), nh_ml, hosted("mlstm_cell_fwd"))
    done("mlstm_cell_fwd")
    x_new, y = _out_fwd(xl, y_rg, y_ml, gate, wl["w_out_g"], 0, t_big, hosted("out_proj_fwd"))
    done("out_proj_fwd")
    saved = dict(x=xl, u=u, hbf=hbf, h_rg=h_rg, y_rg=y_rg, q=q, k=k, v=v, gcol=gcol, grow=grow, cell=cell, y_ml=y_ml,
                 cs=cs, ns=ns, ms=ms, mt=mt, y=y, scale=scale, gate=gate, rg_gates=rg_gates, pre=pre)
    return x_new, saved


def _layer_bwd(l, dx, sv, wl, rep, plan, grads=None, split_last=False):
    s, d = dx.shape
    t_big, t_mid = _tile_for(s, 512), _tile_for(s, 256)
    nh_ml = rep["ml_b_if"].shape[1] // 2
    nd, _, _, w_cols = wl["w_in_g"].shape
    grads = {} if grads is None else grads
    vec = lambda name: _vec(rep["vecs"], l, VEC_TABLE.index(name))
    hosted = lambda name: plan.comms((name, l)) if plan else None
    done = lambda name: plan.done((name, l)) if plan else None
    dy_rg, dy_ml, gw_out, dgate = _out_bwd(dx, sv["gate"], sv["y"], sv["y_rg"], sv["y_ml"], wl["w_out_g"], 0, t_big,
                                           hosted("out_proj_bwd"))
    grads.update(w_out=gw_out)
    done("out_proj_bwd")
    dq, dk, dv, dgates, d_mlo, d_mlz, g_mlng = _ml_cell_bwd(
        dy_ml, sv["u"], sv["cell"], sv["q"], sv["k"], sv["v"], sv["gcol"], sv["grow"], sv["mt"], sv["cs"], sv["ns"],
        sv["ms"], vec("ml_norm_g"), nh_ml, hosted("mlstm_cell_bwd"))
    done("mlstm_cell_bwd")
    d_mlx, g_wq, g_wk, g_wv, g_wift, g_bif, g_mlcw, g_mlcb = _ml_pre_bwd(
        dq, dk, dv, dgates, sv["gcol"], sv["u"], sv["pre"], sv["q"], sv["k"], sv["v"], wl["ml_conv_w"],
        wl["w_qkv"][0], wl["w_qkv"][1], wl["w_qkv"][2], wl["wift_pad"], t_mid, hosted("mlstm_proj_bwd"))
    done("mlstm_proj_bwd")
    d_rgx, d_rgz, g_wa, g_wx, g_ba, g_bx, g_lam, g_rgcw, g_rgcb = _rg_bwd(
        dy_rg, sv["u"], sv["h_rg"], sv["rg_gates"], wl["rg_conv_w"], rep["rg_w_a_bf"][l], rep["rg_w_x_bf"][l],
        vec("rg_lambda"), t_mid, hosted("rglru_bwd"))
    grads.update(w_qkv=jnp.stack([g_wq, g_wk, g_wv]), rg_conv_w=g_rgcw[0:CONV_WIDTH], ml_conv_w=g_mlcw[0:CONV_WIDTH],
                 wif_t=g_wift[0:8], rg_w_a=g_wa, rg_w_x=g_wx)
    acc = dict(dgate=dgate, rg_conv_b=g_rgcb, rg_b_a=g_ba, rg_b_x=g_bx, rg_lambda=g_lam, ml_conv_b=g_mlcb,
               ml_b_if=g_bif, ml_norm_g=g_mlng)
    done("rglru_bwd")
    pieces = [d_rgx, d_rgz, d_mlx, d_mlo, d_mlz]
    grads.update(w_in=_in_bwd_w(pieces, sv["hbf"], w_cols, tuple(range(nd)), t_big, hosted("in_proj_bwd_w")))
    done("in_proj_bwd_w")
    n_tiles = s // t_mid
    counts = [n_tiles // 4, n_tiles - n_tiles // 4 - 1, 1] if split_last and n_tiles >= 4 else [n_tiles]
    in_args = (pieces, sv["x"], dx, vec("norm_g"), sv["scale"], wl["w_in_g"], 0, t_mid)
    res, at = None, 0
    for key, count in zip(("in_proj_bwd_x", "in_proj_bwd_x_rest", "in_proj_bwd_x_end"), counts):
        res = _in_bwd(*in_args, hosted(key), (at, count), res)
        done(key)
        at += count
    dx, dscale, dshift, g_ng = res
    acc.update(norm_g=g_ng, dshift=dshift, dscale=dscale)
    grads.update(acc=acc, dmod=jnp.concatenate([dshift[0:1], dscale[0:1], dgate[0:1]], axis=1))
    return dx, grads


def _local_step(x, c, target, wts):
    d = x.shape[1]
    nl = wts["w_in_g"].shape[1]
    mod, cact = _mod_call(c, wts["w_ada_g"], wts["b_ada"])
    wl = [dict(w_in_g=wts["w_in_g"][:, l:l + 1], w_out_g=wts["w_out_g"][:, l:l + 1], w_qkv=wts["w_qkv"][l],
               rg_conv_w=wts["rg_conv_w"][l], ml_conv_w=wts["ml_conv_w"][l], wif_pad=wts["wif_pad"][l],
               wift_pad=wts["wift_pad"][l], bif_pad=wts["bif_pad"][l]) for l in range(nl)]
    rep = dict(wts, vecs=_vec_table(wts))
    mod3 = mod.reshape(nl, 3, d)
    saved, xl = [], x
    for l in range(nl):
        xl, sv = _layer_fwd(l, xl, mod3, wl[l], rep, None)
        saved.append(sv)
    dx, loss_p, g_final = _loss_call(xl, wts["final_g"].reshape(1, -1), target, _tile_for(x.shape[0], 512))
    grads = [None] * nl
    for l in reversed(range(nl)):
        dx, grads[l] = _layer_bwd(l, dx, saved[l], wl[l], rep, None)
        grads[l]["w_ada"] = _ada_bwd_w(cact[0].reshape(d, 1), grads[l]["dmod"], wts["w_ada_g"].shape[0])
        grads[l]["b_ada"] = grads[l]["dmod"][0]
        grads[l].update({n: a[0] for n, a in grads[l]["acc"].items()})
        grads[l]["ml_b_if"] = grads[l]["ml_b_if"][0:8]
    return loss_p[0, 0], dx, grads, g_final[0]


def _full_qkv(qkv_g, d):
    nd, _, rows3, dh = qkv_g.shape
    nh = d // dh
    rsh = rows3 // (3 * nh)
    return qkv_g.reshape(nd, 3, nh, rsh, dh).transpose(1, 2, 0, 3, 4).reshape(3, nh, nd * rsh, dh)


def _small_weights(small, l, ml_b_if):
    nd = small.shape[0]
    sm = small[:, l]
    cw = sm[:, 0:2 * CONV_WIDTH].reshape(nd, 2, CONV_WIDTH, LANES).transpose(1, 2, 0, 3).reshape(2, CONV_WIDTH, nd * LANES)
    if_rows = (sm.shape[1] - 2 * CONV_WIDTH) * LANES // 8
    wif_t = sm[:, 2 * CONV_WIDTH:].reshape(nd, 8, if_rows).transpose(1, 0, 2).reshape(8, nd * if_rows)
    wift_pad = jnp.pad(wif_t, ((0, LANES - 8), (0, 0))).astype(BF16)
    return dict(rg_conv_w=cw[0], ml_conv_w=cw[1], wift_pad=wift_pad, wif_pad=wift_pad.T,
                bif_pad=jnp.pad(ml_b_if[l], (0, LANES - 8)).reshape(1, LANES))


def kernel(x, c, norm_g, w_ada, b_ada, w_in, rg_conv_w, rg_conv_b, rg_w_a, rg_b_a, rg_w_x, rg_b_x, rg_lambda, ml_conv_w, ml_conv_b, ml_w_q, ml_w_k, ml_w_v, ml_w_if, ml_b_if, ml_norm_g, w_out, final_g, loss_target, m_norm_g, m_w_ada, m_b_ada, m_w_in, m_rg_conv_w, m_rg_conv_b, m_rg_w_a, m_rg_b_a, m_rg_w_x, m_rg_b_x, m_rg_lambda, m_ml_conv_w, m_ml_conv_b, m_ml_w_q, m_ml_w_k, m_ml_w_v, m_ml_w_if, m_ml_b_if, m_ml_norm_g, m_w_out, m_final_g, v_norm_g, v_w_ada, v_b_ada, v_w_in, v_rg_conv_w, v_rg_conv_b, v_rg_w_a, v_rg_b_a, v_rg_w_x, v_rg_b_x, v_rg_lambda, v_ml_conv_w, v_ml_conv_b, v_ml_w_q, v_ml_w_k, v_ml_w_v, v_ml_w_if, v_ml_b_if, v_ml_norm_g, v_w_out, v_final_g):
    given = dict(locals())
    nl = w_in.shape[0]
    d = x.shape[2]
    rep = {n: given[n] for n in REPLICATED}
    rep.update(rg_w_a_bf=rg_w_a.astype(BF16), rg_w_x_bf=rg_w_x.astype(BF16))
    bf = lambda a: a.astype(BF16)

    def qkv_shard(prefix):
        return jnp.stack([given[prefix + "ml_w_q"], given[prefix + "ml_w_k"], given[prefix + "ml_w_v"]], axis=1).reshape(
            nl, -1, ml_w_q.shape[-1])

    def small_shard(prefix):
        return _small_pack(given[prefix + "rg_conv_w"], given[prefix + "ml_conv_w"], given[prefix + "ml_w_if"])

    plan = _Plan()
    qkv = qkv_shard("")
    first_ici = _gather_ici_comm([bf(w_in[0:1]), small_shard("")])
    condition = _exchange_comm([jnp.broadcast_to(c, (SUBLANES, d))], True)
    _run_comms([first_ici, condition], "gather_first")
    first_fwd = _gather_fwd_comm(first_ici.results)
    wcols = w_ada.shape[2]
    me = 4 * lax.axis_index("x") + 2 * lax.axis_index("y") + lax.axis_index("c")
    b_cols = jnp.pad(lax.dynamic_slice_in_dim(b_ada, me * wcols, wcols, axis=1), ((0, SUBLANES - nl), (0, 0)))
    mod_cols, cact_all = _ada_mod(condition.results[0], w_ada, b_cols, [first_fwd])
    w_in_first, small = first_fwd.results
    wl = [_small_weights(small, l, ml_b_if) for l in range(nl)]
    wl[0]["w_in_g"] = w_in_first

    def gather_behind(arrs, ici_host, fwd_host, then):
        ici = _gather_ici_comm(arrs)

        def pass_on():
            fwd = _gather_fwd_comm(ici.results)
            plan.host(fwd_host, fwd, lambda: then(fwd.results))

        plan.host(ici_host, ici, pass_on)

    def got_out(l):
        return lambda r: wl[l].update(w_out_g=r[0], w_qkv=_full_qkv(r[1], d))

    gather_behind([bf(w_out[0:1]), bf(qkv[0:1])], ("in_proj_fwd", 0), ("rglru_fwd", 0), got_out(0))
    for l in range(1, nl):
        gather_behind([bf(w_in[l:l + 1])], ("rglru_fwd", l - 1), ("mlstm_proj_fwd", l - 1),
                      lambda r, l=l: wl[l].update(w_in_g=r[0]))
        gather_behind([bf(w_out[l:l + 1]), bf(qkv[l:l + 1])], ("mlstm_cell_fwd", l - 1), ("out_proj_fwd", l - 1), got_out(l))

    mod_blocks = _exchange([mod_cols], False, "scatter_modulation")[0]
    mod3 = mod_blocks[:, 0:nl].transpose(1, 0, 2).reshape(nl, 3, d)
    rep["vecs"] = _vec_table(rep)
    saved, xl = [], x[0]
    for l in range(nl):
        xl, sv = _layer_fwd(l, xl, mod3, wl[l], rep, plan)
        saved.append(sv)
    grad_x, loss_p, g_final = _loss_call(xl, final_g.reshape(1, -1), loss_target[0], _tile_for(xl.shape[0], 512))

    keys = ("w_in", "w_out", "w_qkv", "small")
    parity = lax.axis_index("c").astype(jnp.int32).reshape(1)
    grads, recv = [None] * nl, [None] * nl

    def parts_of(g):
        return [g["w_in"], g["w_out"], _qkv_slots(g["w_qkv"], N_DEV), _small_slots(g)]

    def pair_sums(l, parts, other):
        return [_pair_sum(a, o, parity, "pair_sum_%s_layer%d" % (key, l)) for key, a, o in zip(keys, parts, other)]

    def reduce_behind(l, host_layer):
        parts = parts_of(grads[l])
        swap = _core_swap_comm(parts)

        def summed():
            sums = pair_sums(l, parts, swap.results)
            big = _chip_swap_comm([sums[0]])
            rest = _chip_swap_comm(sums[1:])
            plan.host(("mlstm_cell_bwd", host_layer), big)
            plan.host(("rglru_bwd", host_layer), rest, lambda: recv.__setitem__(l, big.results + rest.results))

        plan.host(("out_proj_bwd", host_layer), swap, summed)

    first, own = {}, {}

    def reduce_own(names, parts_fn, ready_key, swap_key, chip_key):
        def go():
            parts = parts_fn()
            swap = _core_swap_comm(parts)

            def summed():
                sums = [_pair_sum(a, o, parity, "pair_sum_%s_layer0" % n) for n, a, o in zip(names, parts, swap.results)]
                chip = _chip_swap_comm(sums)
                plan.host(chip_key, chip, lambda: own.update(zip(names, chip.results)))

            plan.host(swap_key, swap, summed)

        plan.after.setdefault(ready_key, []).append(go)

    reduce_own(["w_out"], lambda: [first["w_out"]], ("out_proj_bwd", 0), ("mlstm_cell_bwd", 0), ("mlstm_proj_bwd", 0))
    reduce_own(["w_qkv", "small"], lambda: [_qkv_slots(first["w_qkv"], N_DEV), _small_slots(first)],
               ("rglru_bwd", 0), ("in_proj_bwd_w", 0), ("in_proj_bwd_x", 0))
    reduce_own(["w_in"], lambda: [first["w_in"]], ("in_proj_bwd_w", 0), ("in_proj_bwd_x", 0), ("in_proj_bwd_x_rest", 0))

    for l in reversed(range(nl)):
        if l > 0:
            grad_x, grads[l] = _layer_bwd(l, grad_x, saved[l], wl[l], rep, plan)
            reduce_behind(l, l - 1)
        else:
            grad_x, grads[l] = _layer_bwd(l, grad_x, saved[l], wl[l], rep, plan, first, True)
    plan.flush()
    recv[0] = [own[key] for key in keys]

    shard = {p: dict(w_in=given[p + "w_in"], w_out=given[p + "w_out"], w_qkv=qkv_shard(p), small=small_shard(p))
             for p in ("", "m_", "v_")}
    res = {}
    for ki, key in enumerate(keys):
        res[key] = _reduce_adam([recv[l][ki] for l in range(nl)], shard[""][key], shard["m_"][key], shard["v_"][key],
                                "reduce_adam_" + key)

    dmods = jnp.concatenate([grads[l]["dmod"] for l in range(nl)], axis=0)
    dmod_blocks = jnp.pad(dmods.reshape(nl, N_DEV, wcols).transpose(1, 0, 2), ((0, 0), (0, SUBLANES - nl), (0, 0)))
    dmod_all = _exchange([dmod_blocks], False, "scatter_dmod")[0][:, 0:nl].transpose(1, 0, 2)
    res["w_ada"] = _ada_grad_adam(cact_all.T, dmod_all, w_ada, m_w_ada, v_w_ada)

    widen = lambda a: jnp.pad(a, ((0, 0), (0, d - a.shape[1])))
    rows = [widen(grads[l]["acc"][n][0:1]) for l in range(nl) for n in REP_ROWS] + [g_final[0:1], widen(loss_p[0:1])]
    vp = jnp.concatenate(rows + [jnp.zeros(((-len(rows)) % ROW_ALIGN, d), F32)], axis=0)
    mp = jnp.stack([jnp.stack([grads[l]["rg_w_a"], grads[l]["rg_w_x"]]) for l in range(nl)]).reshape(-1, LANES)
    got = _exchange([vp.reshape(N_DEV, -1, d), mp.reshape(N_DEV, -1, LANES)], False, "reduce_scatter_replicated")
    vp_r, mp_r = _exchange(_sum_parts(got, "sum_replicated"), True, "gather_replicated")
    vp_r, mp_r = vp_r.reshape(-1, d), mp_r.reshape(-1, LANES)
    lanes = lambda a: jnp.pad(a, ((0, 0), (0, LANES - a.shape[1])))
    shaped = dict(ml_b_if=lanes, final_g=lambda a: a.reshape(1, d))
    names = [n for n in REPLICATED if n != "b_ada"] + ["b_ada"]
    rep_res = _adam_replicated(vp_r, mp_r, {n: tuple(shaped.get(n, lambda a: a)(given[p + n]) for p in ("", "m_", "v_"))
                                            for n in names}, nl)
    unshaped = dict(ml_b_if=lambda a: a[:, 0:ml_b_if.shape[1]], final_g=lambda a: a.reshape(d))
    rep_out = [{n: unshaped.get(n, lambda a: a)(rep_res[n][kind]) for n in names} for kind in range(4)]
    loss = vp_r[nl * len(REP_ROWS) + 1, 0]

    if_rows = ml_w_if.shape[1]
    order = ("norm_g", "w_ada", "b_ada", "w_in", "rg_conv_w", "rg_conv_b", "rg_w_a", "rg_b_a", "rg_w_x", "rg_b_x",
             "rg_lambda", "ml_conv_w", "ml_conv_b", "ml_w_q", "ml_w_k", "ml_w_v", "ml_w_if", "ml_b_if", "ml_norm_g",
             "w_out", "final_g")
    outs = [loss, grad_x[None]]
    for kind in range(4):
        qkv_k = res["w_qkv"][kind].reshape((nl, 3) + ml_w_q.shape[1:])
        rg_cw, ml_cw, wif = _small_unpack(res["small"][kind], if_rows)
        sharded = dict(w_ada=res["w_ada"][kind], w_in=res["w_in"][kind], w_out=res["w_out"][kind], ml_w_q=qkv_k[:, 0],
                       ml_w_k=qkv_k[:, 1], ml_w_v=qkv_k[:, 2], rg_conv_w=rg_cw, ml_conv_w=ml_cw, ml_w_if=wif)
        for n in order:
            outs.append(sharded[n] if n in sharded else rep_out[kind][n])
    return tuple(outs)
```

```python
import functools

import jax
import jax.numpy as jnp
from jax import lax
from jax.experimental import pallas as pl
from jax.experimental.pallas import tpu as pltpu

F32 = jnp.float32
BF16 = jnp.bfloat16
MESH_AXES = ("x", "y", "c")
N_DEV = 8
EPS = 1e-6
RG_C = 8.0
ML_CHUNK = 128
CONV_WIDTH = 4
ADAM_LR = 0.001
ADAM_B1 = 0.9
ADAM_B2 = 0.999
ADAM_EPS = 1e-08
ADAM_WD = 0.01
ADAM_STEP = 10
NEG_BIG = -1e30
LANES = 128
SUBLANES = 8
VMEM_LIMIT = 56 * 1024 * 1024
HI = lax.Precision.HIGHEST


def _params(n_grid):
    return pltpu.CompilerParams(dimension_semantics=("arbitrary",) * n_grid, vmem_limit_bytes=VMEM_LIMIT)


def _mm(a, b):
    return jnp.dot(a.astype(BF16), b.astype(BF16), preferred_element_type=F32)


def _mm_nt(a, b):
    return lax.dot_general(a.astype(BF16), b.astype(BF16), (((1,), (1,)), ((), ())), preferred_element_type=F32)


def _mm_tn(a, b):
    return lax.dot_general(a.astype(BF16), b.astype(BF16), (((0,), (0,)), ((), ())), preferred_element_type=F32)


def _mm_hi(a, b):
    return jnp.dot(a, b, precision=HI, preferred_element_type=F32)


def _sigmoid(x):
    return 1.0 / (1.0 + jnp.exp(-x))


def _softplus(x):
    return jnp.maximum(x, 0.0) + jnp.log(1.0 + jnp.exp(-jnp.abs(x)))


def _neg_expm1(x):
    poly = -x * (1.0 + x * (0.5 + x * (1.0 / 6.0 + x * (1.0 / 24.0 + x * (1.0 / 120.0)))))
    return jnp.where(jnp.abs(x) < 0.05, poly, 1.0 - jnp.exp(x))


def _iota(shape, dim):
    return lax.broadcasted_iota(jnp.int32, shape, dim)


def _colsum(x):
    return jnp.sum(x, axis=0, keepdims=True)


def _rowsum(x):
    return jnp.sum(x, axis=1, keepdims=True)


def _col(x, j):
    return _rowsum(jnp.where(_iota(x.shape, 1) == j, x, 0.0))


def _row(x, j):
    return _colsum(jnp.where(_iota(x.shape, 0) == j, x, 0.0))


def _shift_down(x, j, prev8):
    if j == 0:
        return x
    t = x.shape[0]
    main = jnp.where(_iota(x.shape, 0) >= j, pltpu.roll(x, j, 0), 0.0)
    fix = jnp.where(_iota(prev8.shape, 0) < j, pltpu.roll(prev8, j, 0), 0.0)
    return jnp.concatenate([main[0:SUBLANES] + fix, main[SUBLANES:t]], axis=0)


def _shift_up(x, j, next8):
    if j == 0:
        return x
    t = x.shape[0]
    main = jnp.where(_iota(x.shape, 0) < t - j, pltpu.roll(x, t - j, 0), 0.0)
    fix = jnp.where(_iota(next8.shape, 0) >= SUBLANES - j, pltpu.roll(next8, SUBLANES - j, 0), 0.0)
    return jnp.concatenate([main[0:t - SUBLANES], main[t - SUBLANES:t] + fix], axis=0)


def _conv(x, prev8, w_ref):
    y = w_ref[CONV_WIDTH - 1:CONV_WIDTH, :] * x
    for j in range(1, CONV_WIDTH):
        y = y + w_ref[CONV_WIDTH - 1 - j:CONV_WIDTH - j, :] * _shift_down(x, j, prev8)
    return y


def _conv_bwd(dy, x, next8, w_ref, gw_ref):
    dx = None
    for j in range(CONV_WIDTH):
        k = CONV_WIDTH - 1 - j
        up = _shift_up(dy, j, next8)
        gw_ref[k:k + 1, :] += _colsum(up * x)
        term = w_ref[k:k + 1, :] * up
        dx = term if dx is None else dx + term
    return dx


def _scan_into(a, b, carry, out_ref, reverse):
    t, c = a.shape
    groups = t // SUBLANES
    a3 = a.reshape(groups, SUBLANES, c)
    b3 = b.reshape(groups, SUBLANES, c)
    sub = _iota(a3.shape, 1)
    for step in (1, 2, 4):
        keep = sub < SUBLANES - step if reverse else sub >= step
        shift = SUBLANES - step if reverse else step
        a_s = jnp.where(keep, pltpu.roll(a3, shift, 1), 1.0)
        b_s = jnp.where(keep, pltpu.roll(b3, shift, 1), 0.0)
        b3 = a3 * b_s + b3
        a3 = a3 * a_s
    for g in (reversed(range(groups)) if reverse else range(groups)):
        rows = slice(g * SUBLANES, (g + 1) * SUBLANES)
        out_ref[rows, :] = b3[g] + a3[g] * carry
        edge = g * SUBLANES if reverse else (g + 1) * SUBLANES - 1
        carry = out_ref[edge:edge + 1, :]


def _blockdiag(x, w_ref, transpose_w=False):
    nh, dh, _ = w_ref.shape
    outs = []
    for h in range(nh):
        xs = x[:, h * dh:(h + 1) * dh]
        outs.append(_mm_nt(xs, w_ref[h]) if transpose_w else _mm(xs, w_ref[h]))
    return jnp.concatenate(outs, axis=1)


def _rg_gates(xc, wa_ref, ba_ref, wx_ref, bx_ref, lam_ref):
    r = _sigmoid(_blockdiag(xc, wa_ref) + ba_ref[...])
    ig = _sigmoid(_blockdiag(xc, wx_ref) + bx_ref[...])
    sp = _softplus(-lam_ref[...])
    log_a = -RG_C * r * sp
    a = jnp.exp(log_a)
    beta = jnp.sqrt(_neg_expm1(2.0 * log_a))
    return r, ig, sp, a, beta


def _bcast8(row):
    return jnp.broadcast_to(row, (SUBLANES, row.shape[1]))


def _full(shape):
    nd = len(shape)
    return pl.BlockSpec(shape, lambda *_: (0,) * nd)


class _Comm:
    def __init__(self, arrays, out_shapes, sems, start, finish, aliases=()):
        self.arrays, self.out_shapes, self.sems = list(arrays), list(out_shapes), list(sems)
        self.start, self.finish, self.aliases = start, finish, tuple(aliases)
        self.results = None


class _RowOf:
    def __init__(self, ref, k):
        self.ref, self.k = ref, k

    def __getitem__(self, idx):
        cols = slice(None) if idx is Ellipsis else idx[1]
        return self.ref[0, self.k:self.k + 1, cols]


def _vec(table, layer, k):
    return ("row", table, layer, k)


def _is_row(arg):
    return isinstance(arg, tuple) and len(arg) == 4 and arg[0] == "row"


def _call(body, comms, *, name, grid, in_specs, out_specs, out_shape, args, scratch_shapes=(), aliases=None):
    comms = [cm for cm in (comms or []) if cm is not None]
    rows = {i: a[3] for i, a in enumerate(args) if _is_row(a)}
    in_specs = [pl.BlockSpec((1,) + a[1].shape[1:], functools.partial(lambda layer, *_: (layer, 0, 0), a[2]))
                if _is_row(a) else sp for a, sp in zip(args, in_specs)]
    args = tuple(a[1] if _is_row(a) else a for a in args)
    n_in, n_out, n_sc = len(args), len(out_shape), len(scratch_shapes)
    c_arrays = [a for cm in comms for a in cm.arrays]
    c_outs = [o for cm in comms for o in cm.out_shapes]
    c_sems = [sm for cm in comms for sm in cm.sems]
    aliases, a_at, o_at = dict(aliases or {}), n_in, n_out
    for cm in comms:
        for (i, j) in cm.aliases:
            aliases[a_at + i] = o_at + j
        a_at += len(cm.arrays)
        o_at += len(cm.out_shapes)

    def wrapped(*refs):
        ins, c_in = refs[:n_in], refs[n_in:n_in + len(c_arrays)]
        ins = [_RowOf(r, rows[i]) if i in rows else r for i, r in enumerate(ins)]
        at = n_in + len(c_arrays)
        outs, c_out = refs[at:at + n_out], refs[at + n_out:at + n_out + len(c_outs)]
        at += n_out + len(c_outs)
        scr, sems = refs[at:at + n_sc], refs[at + n_sc:]
        views, ia, io, isem = [], 0, 0, 0
        for cm in comms:
            views.append((c_in[ia:ia + len(cm.arrays)], c_out[io:io + len(cm.out_shapes)], sems[isem:isem + len(cm.sems)]))
            ia, io, isem = ia + len(cm.arrays), io + len(cm.out_shapes), isem + len(cm.sems)
        if comms:
            @pl.when(pl.program_id(0) == 0)
            def _():
                for cm, view in zip(comms, views):
                    cm.start(*view)

        body(*ins, *outs, *scr)
        if comms:
            @pl.when(pl.program_id(0) == grid[0] - 1)
            def _():
                for cm, view in zip(comms, views):
                    cm.finish(*view)

    hbm = pl.BlockSpec(memory_space=pl.ANY)
    res = pl.pallas_call(
        wrapped, name=name, grid=grid,
        in_specs=list(in_specs) + [hbm] * len(c_arrays), out_specs=list(out_specs) + [hbm] * len(c_outs),
        out_shape=list(out_shape) + c_outs, scratch_shapes=list(scratch_shapes) + c_sems,
        input_output_aliases=aliases, compiler_params=_params(len(grid)),
    )(*args, *c_arrays)
    at = n_out
    for cm in comms:
        cm.results = list(res[at:at + len(cm.out_shapes)])
        at += len(cm.out_shapes)
    return list(res[:n_out])


def _mod_call(c, w_ada_g, b_ada):
    nd, nl, d, w = w_ada_g.shape

    def body(c_ref, w_ref, b_ref, mod_ref, cact_ref):
        cv = c_ref[...]
        ca = _bcast8(cv * _sigmoid(cv))
        cact_ref[...] = ca
        mod_ref[0, 0] = _mm(ca, w_ref[0, 0]) + b_ref[0, 0]

    mod, cact = pl.pallas_call(
        body, name="adaln_mod", grid=(nl, nd),
        in_specs=[_full((1, d)),
                  pl.BlockSpec((1, 1, d, w), lambda l, j: (j, l, 0, 0)),
                  pl.BlockSpec((1, 1, 1, w), lambda l, j: (l, j, 0, 0))],
        out_specs=[pl.BlockSpec((1, 1, SUBLANES, w), lambda l, j: (l, j, 0, 0)), _full((SUBLANES, d))],
        out_shape=[jax.ShapeDtypeStruct((nl, nd, SUBLANES, w), F32), jax.ShapeDtypeStruct((SUBLANES, d), F32)],
        compiler_params=_params(2),
    )(c, w_ada_g, b_ada.reshape(nl, nd, 1, w))
    return mod[:, :, 0, :].reshape(nl, nd * w), cact


def _join_columns(w_ref, wcat):
    nd, _, _, w = w_ref.shape

    @pl.when(pl.program_id(0) == 0)
    def _():
        for j in range(nd):
            wcat[:, j * w:(j + 1) * w] = w_ref[j, 0]


def _in_fwd(x, ng, scale, shift, w_in_g, layer, tile, comms=None):
    s, d = x.shape
    nd, _, _, w = w_in_g.shape

    def body(x_ref, ng_ref, sc_ref, sh_ref, w_ref, u_ref, h_ref, wcat):
        _join_columns(w_ref, wcat)
        xv = x_ref[...]
        rs = lax.rsqrt(jnp.mean(xv * xv, axis=1, keepdims=True) + EPS)
        hb = (xv * rs * ng_ref[...] * (1.0 + sc_ref[...]) + sh_ref[...]).astype(BF16)
        h_ref[...] = hb
        u_ref[...] = jnp.dot(hb, wcat[...], preferred_element_type=F32)

    return _call(
        body, comms, name="in_proj_fwd", grid=(s // tile,),
        in_specs=[pl.BlockSpec((tile, d), lambda i: (i, 0)), _full((1, d)), _full((1, d)), _full((1, d)),
                  pl.BlockSpec((nd, 1, d, w), lambda i: (0, layer, 0, 0), pipeline_mode=pl.Buffered(1))],
        out_specs=[pl.BlockSpec((tile, nd * w), lambda i: (i, 0)), pl.BlockSpec((tile, d), lambda i: (i, 0))],
        out_shape=[jax.ShapeDtypeStruct((s, nd * w), F32), jax.ShapeDtypeStruct((s, d), BF16)],
        scratch_shapes=[pltpu.VMEM((d, nd * w), BF16)],
        args=(x, ng, scale, shift, w_in_g))


def _rg_fwd(u, d, conv_w, conv_b, w_a, b_a, w_x, b_x, lam, tile, comms=None):
    s = u.shape[0]

    def body(x_ref, z_ref, cw_ref, cb_ref, wa_ref, ba_ref, wx_ref, bx_ref, lam_ref,
             h_ref, y_ref, xc_ref, r_ref, i_ref, a_ref, beta_ref, prev8, hcar):
        @pl.when(pl.program_id(0) == 0)
        def _():
            prev8[...] = jnp.zeros_like(prev8)
            hcar[...] = jnp.zeros_like(hcar)

        x = x_ref[...]
        xc = _conv(x, prev8[...], cw_ref) + cb_ref[...]
        prev8[...] = x[tile - SUBLANES:tile, :]
        r, ig, _, a, beta = _rg_gates(xc, wa_ref, ba_ref, wx_ref, bx_ref, lam_ref)
        xc_ref[...] = xc
        r_ref[...] = r
        i_ref[...] = ig
        a_ref[...] = a
        beta_ref[...] = beta
        _scan_into(a, beta * ig * xc, hcar[SUBLANES - 1:SUBLANES, :], h_ref, False)
        h = h_ref[...]
        hcar[...] = h[tile - SUBLANES:tile, :]
        z = z_ref[...]
        y_ref[...] = (h * z * _sigmoid(z)).astype(BF16)

    vec = _full((1, d))
    return _call(
        body, comms, name="rglru_fwd", grid=(s // tile,),
        in_specs=[pl.BlockSpec((tile, d), lambda i: (i, 0)), pl.BlockSpec((tile, d), lambda i: (i, 1)),
                  _full(conv_w.shape), vec, _full(w_a.shape), vec, _full(w_x.shape), vec, vec],
        out_specs=[pl.BlockSpec((tile, d), lambda i: (i, 0))] * 7,
        out_shape=[jax.ShapeDtypeStruct((s, d), F32), jax.ShapeDtypeStruct((s, d), BF16)] + [jax.ShapeDtypeStruct((s, d), F32)] * 5,
        scratch_shapes=[pltpu.VMEM((SUBLANES, d), F32), pltpu.VMEM((SUBLANES, d), F32)],
        args=(u, u, conv_w, conv_b, w_a, b_a, w_x, b_x, lam))


def _ml_pre(u, d, conv_w, conv_b, w_q, w_k, w_v, wif, bif, tile, comms=None):
    s = u.shape[0]
    nh = w_q.shape[0]

    def body(x_ref, cw_ref, cb_ref, wq_ref, wk_ref, wv_ref, wif_ref, bif_ref, q_ref, k_ref, v_ref, g_ref, pre_ref, prev8):
        @pl.when(pl.program_id(0) == 0)
        def _():
            prev8[...] = jnp.zeros_like(prev8)

        x = x_ref[...]
        pre = _conv(x, prev8[...], cw_ref) + cb_ref[...]
        prev8[...] = x[tile - SUBLANES:tile, :]
        xc = pre * _sigmoid(pre)
        q = _blockdiag(xc, wq_ref)
        k = _blockdiag(xc, wk_ref)
        v = _blockdiag(x, wv_ref)
        pre_ref[...] = pre
        q_ref[...] = q
        k_ref[...] = k
        v_ref[...] = v
        g = _mm(q, wif_ref[0:d, :]) + _mm(k, wif_ref[d:2 * d, :]) + _mm(v, wif_ref[2 * d:3 * d, :]) + bif_ref[...]
        lane = _iota(g.shape, 1)
        gl = jnp.where(lane < 4, g, jnp.where(lane < 8, -_softplus(-g), 0.0))
        tri = jnp.where(_iota((ML_CHUNK, ML_CHUNK), 1) <= _iota((ML_CHUNK, ML_CHUNK), 0), 1.0, 0.0)
        cums = [_mm_hi(tri, gl[c * ML_CHUNK:(c + 1) * ML_CHUNK, :]) for c in range(tile // ML_CHUNK)]
        cum = cums[0] if len(cums) == 1 else jnp.concatenate(cums, axis=0)
        g_ref[...] = gl + jnp.where((lane >= 8) & (lane < 12), pltpu.roll(cum, 4, 1), 0.0)

    vec = _full((1, d))
    return _call(
        body, comms, name="mlstm_proj_fwd", grid=(s // tile,),
        in_specs=[pl.BlockSpec((tile, d), lambda i: (i, 2)), _full(conv_w.shape), vec,
                  _full(w_q.shape), _full(w_k.shape), _full(w_v.shape), _full(wif.shape), _full((1, LANES))],
        out_specs=[pl.BlockSpec((tile, d), lambda i: (i, 0))] * 3 + [pl.BlockSpec((tile, LANES), lambda i: (i, 0)),
                                                                     pl.BlockSpec((tile, d), lambda i: (i, 0))],
        out_shape=[jax.ShapeDtypeStruct((s, d), F32)] * 3 + [jax.ShapeDtypeStruct((s, LANES), F32),
                                                             jax.ShapeDtypeStruct((s, d), F32)],
        scratch_shapes=[pltpu.VMEM((SUBLANES, d), F32)],
        args=(u, conv_w, conv_b, w_q, w_k, w_v, wif, bif))


def _cell_chunk(h, nh, q_ref, k_ref, v_ref, gc, gr, m_prev, c_h, n_h, m_t=None):
    lc = ML_CHUNK
    dh = q_ref.shape[1] // nh
    sl = slice(h * dh, (h + 1) * dh)
    qh = q_ref[:, sl]
    kh = k_ref[:, sl] * (dh ** -0.5)
    vh = v_ref[:, sl]
    li_c = _col(gc, h)
    b_c = _col(gc, 8 + h)
    lib_r = _row(gr, h) - _row(gr, 8 + h)
    b_last = _colsum(jnp.where(_iota((lc, 1), 0) == lc - 1, b_c, 0.0))
    causal = _iota((lc, lc), 1) <= _iota((lc, lc), 0)
    dmat = jnp.where(causal, b_c + lib_r, NEG_BIG)
    m_inter = b_c + m_prev
    if m_t is None:
        m_t = jnp.maximum(m_inter, jnp.max(dmat, axis=1, keepdims=True))
    w_intra = jnp.exp(dmat - m_t)
    w_inter = jnp.exp(m_inter - m_t)
    amat = _mm_nt(qh, kh)
    smat = amat * w_intra
    qc = _mm(qh, c_h)
    qn = _rowsum(qh * n_h)
    den = _rowsum(smat) + w_inter * qn
    gst = b_last - b_c + li_c
    m_new = jnp.maximum(b_last + m_prev, jnp.max(gst, axis=0, keepdims=True))
    w_state = jnp.exp(gst - m_new)
    decay = jnp.exp(b_last + m_prev - m_new)
    return dict(sl=sl, qh=qh, kh=kh, vh=vh, m_t=m_t, w_intra=w_intra, w_inter=w_inter, smat=smat, qc=qc, qn=qn,
                den=den, m_new=m_new, w_state=w_state, decay=decay)


def _ml_cell_fwd(q, k, v, gcol, grow, u, ng, nh, comms=None):
    s, d = q.shape
    lc = ML_CHUNK
    nc = s // lc
    dh = d // nh

    def body(q_ref, k_ref, v_ref, gc_ref, gr_ref, o_ref, z_ref, ng_ref,
             cell_ref, y_ref, cs_ref, ns_ref, ms_ref, mt_ref, c_sc, n_sc, m_sc):
        @pl.when(pl.program_id(0) == 0)
        def _():
            c_sc[...] = jnp.zeros_like(c_sc)
            n_sc[...] = jnp.zeros_like(n_sc)
            m_sc[...] = jnp.zeros_like(m_sc)

        gc = gc_ref[...]
        gr = gr_ref[...]
        lane = _iota((lc, LANES), 1)
        mt_acc = jnp.zeros((lc, LANES), F32)
        for h in range(nh):
            c_h = c_sc[h]
            n_h = n_sc[h, 0:1, :]
            m_prev = jnp.max(m_sc[h, 0:1, :], axis=1, keepdims=True)
            cs_ref[0, h] = c_h
            ns_ref[0, h] = n_sc[h]
            ms_ref[0, h] = m_sc[h]
            t = _cell_chunk(h, nh, q_ref, k_ref, v_ref, gc, gr, m_prev, c_h, n_h)
            sl = t["sl"]
            num = _mm(t["smat"], t["vh"]) + t["w_inter"] * t["qc"]
            cell_h = num / jnp.maximum(jnp.abs(t["den"]), jnp.exp(-t["m_t"]))
            mt_acc = jnp.where(lane == h, t["m_t"], mt_acc)
            kw = t["kh"] * t["w_state"]
            c_sc[h] = t["decay"] * c_h + _mm_tn(kw, t["vh"])
            n_sc[h] = _bcast8(t["decay"] * n_h + _colsum(kw))
            m_sc[h] = jnp.broadcast_to(t["m_new"], (SUBLANES, LANES))
            hg = _sigmoid(o_ref[:, sl]) * cell_h
            hn = hg * lax.rsqrt(jnp.mean(hg * hg, axis=1, keepdims=True) + EPS)
            z = z_ref[:, sl]
            cell_ref[:, sl] = cell_h
            y_ref[:, sl] = (hn * ng_ref[:, sl] * z * _sigmoid(z)).astype(BF16)
        mt_ref[...] = mt_acc

    tok = pl.BlockSpec((lc, d), lambda c: (c, 0))
    return _call(
        body, comms, name="mlstm_cell_fwd", grid=(nc,),
        in_specs=[tok, tok, tok, pl.BlockSpec((lc, LANES), lambda c: (c, 0)), pl.BlockSpec((16, lc), lambda c: (0, c)),
                  pl.BlockSpec((lc, d), lambda c: (c, 3)), pl.BlockSpec((lc, d), lambda c: (c, 4)), _full((1, d))],
        out_specs=[tok, tok, pl.BlockSpec((1, nh, dh, dh), lambda c: (c, 0, 0, 0)),
                   pl.BlockSpec((1, nh, SUBLANES, dh), lambda c: (c, 0, 0, 0)),
                   pl.BlockSpec((1, nh, SUBLANES, LANES), lambda c: (c, 0, 0, 0)),
                   pl.BlockSpec((lc, LANES), lambda c: (c, 0))],
        out_shape=[jax.ShapeDtypeStruct((s, d), F32), jax.ShapeDtypeStruct((s, d), BF16),
                   jax.ShapeDtypeStruct((nc, nh, dh, dh), F32), jax.ShapeDtypeStruct((nc, nh, SUBLANES, dh), F32),
                   jax.ShapeDtypeStruct((nc, nh, SUBLANES, LANES), F32), jax.ShapeDtypeStruct((s, LANES), F32)],
        scratch_shapes=[pltpu.VMEM((nh, dh, dh), F32), pltpu.VMEM((nh, SUBLANES, dh), F32),
                        pltpu.VMEM((nh, SUBLANES, LANES), F32)],
        args=(q, k, v, gcol, grow, u, u, ng))


def _out_fwd(x, y_rg, y_ml, gate, w_out_g, layer, tile, comms=None):
    s, d = x.shape
    nd, _, r, _ = w_out_g.shape

    def body(x_ref, yr_ref, ym_ref, g_ref, w_ref, xn_ref, y_ref):
        ycat = jnp.concatenate([yr_ref[...].astype(BF16), ym_ref[...].astype(BF16)], axis=1)
        acc = jnp.dot(ycat, w_ref[...].reshape(nd * r, d), preferred_element_type=F32)
        y_ref[...] = acc
        xn_ref[...] = x_ref[...] + g_ref[...] * acc

    tok = pl.BlockSpec((tile, d), lambda i: (i, 0))
    return _call(
        body, comms, name="out_proj_fwd", grid=(s // tile,),
        in_specs=[tok, tok, tok, _full((1, d)), pl.BlockSpec((nd, 1, r, d), lambda i: (0, layer, 0, 0))],
        out_specs=[tok, tok],
        out_shape=[jax.ShapeDtypeStruct((s, d), F32)] * 2,
        args=(x, y_rg, y_ml, gate, w_out_g))


def _loss_call(x, fg, target, tile):
    s, d = x.shape

    def body(x_ref, g_ref, t_ref, dx_ref, loss_ref, gg_ref):
        @pl.when(pl.program_id(0) == 0)
        def _():
            loss_ref[...] = jnp.zeros_like(loss_ref)
            gg_ref[...] = jnp.zeros_like(gg_ref)

        xv = x_ref[...]
        g = g_ref[...]
        rs = lax.rsqrt(jnp.mean(xv * xv, axis=1, keepdims=True) + EPS)
        xh = xv * rs
        e = xh * g - t_ref[...]
        loss_ref[...] += jnp.broadcast_to(_colsum(_rowsum(e * e)) * (0.5 / d), loss_ref.shape)
        dy = e * (1.0 / d)
        gg_ref[...] += _bcast8(_colsum(dy * xh))
        dxh = dy * g
        dx_ref[...] = rs * (dxh - xh * jnp.mean(dxh * xh, axis=1, keepdims=True))

    tok = pl.BlockSpec((tile, d), lambda i: (i, 0))
    return pl.pallas_call(
        body, name="final_norm_loss", grid=(s // tile,),
        in_specs=[tok, _full((1, d)), tok],
        out_specs=[tok, _full((SUBLANES, LANES)), _full((SUBLANES, d))],
        out_shape=[jax.ShapeDtypeStruct((s, d), F32), jax.ShapeDtypeStruct((SUBLANES, LANES), F32),
                   jax.ShapeDtypeStruct((SUBLANES, d), F32)],
        compiler_params=_params(1),
    )(x, fg, target)


def _ml_out_stage_bwd(dy, cell, o, z, ng):
    so = _sigmoid(o)
    hg = so * cell
    rinv = lax.rsqrt(jnp.mean(hg * hg, axis=1, keepdims=True) + EPS)
    hn = hg * rinv
    sz = _sigmoid(z)
    dz = dy * hn * ng * (sz + z * sz * (1.0 - sz))
    dymid = dy * z * sz
    dhn = dymid * ng
    dhg = rinv * (dhn - hn * jnp.mean(dhn * hn, axis=1, keepdims=True))
    return dz, dhg * cell * so * (1.0 - so), dhg * so, _colsum(dymid * hn)


def _out_bwd(dxo, gate, y, y_rg, y_ml, cell, u, ng, w_out_g, layer, tile, nh, comms=None):
    s, d = dxo.shape
    nd, _, r, _ = w_out_g.shape
    dh = d // nh

    def body(dx_ref, g_ref, y_ref, yr_ref, ym_ref, cell_ref, o_ref, z_ref, ng_ref, w_ref,
             dyr_ref, dcell_ref, do_ref, dz_ref, gw_ref, dg_ref, gng_ref):
        @pl.when(pl.program_id(0) == 0)
        def _():
            for ref in (gw_ref, dg_ref, gng_ref):
                ref[...] = jnp.zeros_like(ref)

        dxv = dx_ref[...]
        dg_ref[...] += _bcast8(_colsum(dxv * y_ref[...]))
        dyb = (dxv * g_ref[...]).astype(BF16)
        dycat = lax.dot_general(dyb, w_ref[...].reshape(nd * r, d), (((1,), (1,)), ((), ())), preferred_element_type=F32)
        dyr_ref[...] = dycat[:, 0:d]
        ycat = jnp.concatenate([yr_ref[...].astype(BF16), ym_ref[...].astype(BF16)], axis=1)
        gw_ref[...] += lax.dot_general(ycat, dyb, (((0,), (0,)), ((), ())), preferred_element_type=F32).reshape(nd, r, d)
        for h in range(nh):
            sl = slice(h * dh, (h + 1) * dh)
            dz, do, dcell, gng = _ml_out_stage_bwd(dycat[:, d + h * dh:d + (h + 1) * dh], cell_ref[:, sl], o_ref[:, sl],
                                                   z_ref[:, sl], ng_ref[:, sl])
            dz_ref[:, sl] = dz.astype(BF16)
            do_ref[:, sl] = do.astype(BF16)
            dcell_ref[:, sl] = dcell
            gng_ref[:, sl] += _bcast8(gng)

    tok = pl.BlockSpec((tile, d), lambda i: (i, 0))
    acc = _full((SUBLANES, d))
    return _call(
        body, comms, name="out_proj_bwd", grid=(s // tile,),
        in_specs=[tok, _full((1, d)), tok, tok, tok, tok, pl.BlockSpec((tile, d), lambda i: (i, 3)),
                  pl.BlockSpec((tile, d), lambda i: (i, 4)), _full((1, d)),
                  pl.BlockSpec((nd, 1, r, d), lambda i: (0, layer, 0, 0), pipeline_mode=pl.Buffered(1))],
        out_specs=[tok, tok, tok, tok, pl.BlockSpec((nd, r, d), lambda i: (0, 0, 0), pipeline_mode=pl.Buffered(1)), acc, acc],
        out_shape=[jax.ShapeDtypeStruct((s, d), F32)] * 2 + [jax.ShapeDtypeStruct((s, d), BF16)] * 2
        + [jax.ShapeDtypeStruct((nd, r, d), F32)] + [jax.ShapeDtypeStruct((SUBLANES, d), F32)] * 2,
        args=(dxo, gate, y, y_rg, y_ml, cell, u, u, ng, w_out_g))


def _ml_cell_bwd(dcell, cell, q, k, v, gcol, grow, mt, cs, ns, ms, nh, comms=None):
    s, d = q.shape
    lc = ML_CHUNK
    nc = s // lc
    dh = d // nh

    def body(dcell_ref, cell_ref, q_ref, k_ref, v_ref, gc_ref, gr_ref, mt_ref, cs_ref, ns_ref, ms_ref,
             dq_ref, dk_ref, dv_ref, dg_ref, dc_sc, dn_sc):
        @pl.when(pl.program_id(0) == 0)
        def _():
            dc_sc[...] = jnp.zeros_like(dc_sc)
            dn_sc[...] = jnp.zeros_like(dn_sc)

        gc = gc_ref[...]
        gr = gr_ref[...]
        mtv = mt_ref[...]
        lane = _iota((lc, LANES), 1)
        rowv = _iota((lc, 1), 0)
        dg_acc = jnp.zeros((lc, LANES), F32)
        for h in range(nh):
            c_h = cs_ref[0, h]
            n_h = ns_ref[0, h, 0:1, :]
            m_prev = jnp.max(ms_ref[0, h, 0:1, :], axis=1, keepdims=True)
            t = _cell_chunk(h, nh, q_ref, k_ref, v_ref, gc, gr, m_prev, c_h, n_h, m_t=_col(mtv, h))
            sl, qh, kh, vh = t["sl"], t["qh"], t["kh"], t["vh"]
            w_intra, w_inter, smat, w_state, decay = t["w_intra"], t["w_inter"], t["smat"], t["w_state"], t["decay"]
            cell_h = cell_ref[:, sl]
            dcell = dcell_ref[:, sl]
            eneg = jnp.exp(-t["m_t"])
            aden = jnp.abs(t["den"])
            nst = jnp.maximum(aden, eneg)
            dnum = dcell / nst
            dden = jnp.where(aden > eneg, -_rowsum(cell_h * dcell) / nst * jnp.sign(t["den"]), 0.0)
            pmat = _mm_nt(dnum, vh) + dden
            damat = pmat * w_intra
            gmat = pmat * smat
            wdn = w_inter * dnum
            wdd = w_inter * dden
            dqh = _mm(damat, kh) + _mm_nt(wdn, c_h) + wdd * n_h
            dkh = _mm_tn(damat, qh)
            dvh = _mm_tn(smat, dnum)
            dw_inter = _rowsum(dnum * t["qc"]) + dden * t["qn"]
            dcn = dc_sc[h]
            dnn = dn_sc[h, 0:1, :]
            kw = kh * w_state
            dkw = _mm_nt(vh, dcn) + dnn
            dvh = dvh + _mm(kw, dcn)
            dkh = dkh + dkw * w_state
            dgst = _rowsum(dkw * kh) * w_state
            ddecay = _colsum(_rowsum(dcn * c_h)) + _rowsum(dnn * n_h)
            db_last = _colsum(dgst) + ddecay * decay
            rs_g = _rowsum(gmat)
            cs_g = _rowsum(gmat.T)
            db = rs_g - cs_g + dw_inter * w_inter - dgst + jnp.where(rowv == lc - 1, db_last, 0.0)
            dli = cs_g + dgst
            dc_sc[h] = decay * dcn + _mm_tn(qh, wdn)
            dn_sc[h] = _bcast8(decay * dnn + _colsum(qh * wdd))
            dq_ref[:, sl] = dqh
            dk_ref[:, sl] = dkh * (dh ** -0.5)
            dv_ref[:, sl] = dvh
            dg_acc = jnp.where(lane == h, dli, jnp.where(lane == 4 + h, db, dg_acc))
        dg_ref[...] = dg_acc

    rev = lambda c: nc - 1 - c
    tok = pl.BlockSpec((lc, d), lambda c: (rev(c), 0))
    g128 = pl.BlockSpec((lc, LANES), lambda c: (rev(c), 0))
    return _call(
        body, comms, name="mlstm_cell_bwd", grid=(nc,),
        in_specs=[tok, tok, tok, tok, tok, g128, pl.BlockSpec((16, lc), lambda c: (0, rev(c))), g128,
                  pl.BlockSpec((1, nh, dh, dh), lambda c: (rev(c), 0, 0, 0)),
                  pl.BlockSpec((1, nh, SUBLANES, dh), lambda c: (rev(c), 0, 0, 0)),
                  pl.BlockSpec((1, nh, SUBLANES, LANES), lambda c: (rev(c), 0, 0, 0))],
        out_specs=[tok, tok, tok, g128],
        out_shape=[jax.ShapeDtypeStruct((s, d), F32)] * 3 + [jax.ShapeDtypeStruct((s, LANES), F32)],
        scratch_shapes=[pltpu.VMEM((nh, dh, dh), F32), pltpu.VMEM((nh, SUBLANES, dh), F32)],
        args=(dcell, cell, q, k, v, gcol, grow, mt, cs, ns, ms))


def _halo_spec(d, tile, nt, col):
    per = tile // SUBLANES
    return pl.BlockSpec((SUBLANES, d), lambda i: (jnp.maximum((nt - 1 - i) * per - 1, 0), col))


def _ml_pre_bwd(dq, dk, dv, dgates, gcol, u, pre, q, k, v, conv_w, w_q, w_k, w_v, wif_t, tile, comms=None):
    s, d = dq.shape
    nt = s // tile
    nh, dh, _ = w_q.shape

    def body(dq_ref, dk_ref, dv_ref, dg_ref, gc_ref, x_ref, pre_ref, q_ref, k_ref, v_ref, cw_ref,
             wq_ref, wk_ref, wv_ref, wift_ref,
             dx_ref, gwq_ref, gwk_ref, gwv_ref, gwif_ref, gbif_ref, gcw_ref, gcb_ref, next8):
        @pl.when(pl.program_id(0) == 0)
        def _():
            next8[...] = jnp.zeros_like(next8)
            for ref in (gwq_ref, gwk_ref, gwv_ref, gwif_ref, gbif_ref, gcw_ref, gcb_ref):
                ref[...] = jnp.zeros_like(ref)

        x = x_ref[...]
        pre = pre_ref[...]
        sg = _sigmoid(pre)
        xc = pre * sg
        dgc = dg_ref[...]
        lane = _iota(dgc.shape, 1)
        utri = jnp.where(_iota((ML_CHUNK, ML_CHUNK), 0) <= _iota((ML_CHUNK, ML_CHUNK), 1), 1.0, 0.0)
        rcs = [_mm_hi(utri, dgc[c * ML_CHUNK:(c + 1) * ML_CHUNK, :]) for c in range(tile // ML_CHUNK)]
        rc = rcs[0] if len(rcs) == 1 else jnp.concatenate(rcs, axis=0)
        dgates_v = jnp.where(lane < 4, dgc, jnp.where(lane < 8, rc * (1.0 - jnp.exp(gc_ref[...])), 0.0))
        dgb = dgates_v.astype(BF16)
        gbif_ref[...] += jnp.broadcast_to(_colsum(dgates_v), gbif_ref.shape)
        ext = jnp.dot(dgb, wift_ref[...], preferred_element_type=F32)
        dqt = dq_ref[...] + ext[:, 0:d]
        dkt = dk_ref[...] + ext[:, d:2 * d]
        dvt = dv_ref[...] + ext[:, 2 * d:3 * d]
        gwif_ref[:, 0:d] += _mm_tn(dgb, q_ref[...])
        gwif_ref[:, d:2 * d] += _mm_tn(dgb, k_ref[...])
        gwif_ref[:, 2 * d:3 * d] += _mm_tn(dgb, v_ref[...])
        dxc_parts, dxv_parts = [], []
        for h in range(nh):
            sl = slice(h * dh, (h + 1) * dh)
            gwq_ref[h] += _mm_tn(xc[:, sl], dqt[:, sl])
            gwk_ref[h] += _mm_tn(xc[:, sl], dkt[:, sl])
            gwv_ref[h] += _mm_tn(x[:, sl], dvt[:, sl])
            dxc_parts.append(_mm_nt(dqt[:, sl], wq_ref[h]) + _mm_nt(dkt[:, sl], wk_ref[h]))
            dxv_parts.append(_mm_nt(dvt[:, sl], wv_ref[h]))
        dxc = jnp.concatenate(dxc_parts, axis=1)
        dxv = jnp.concatenate(dxv_parts, axis=1)
        dpre = dxc * (sg + pre * sg * (1.0 - sg))
        gcb_ref[...] += _bcast8(_colsum(dpre))
        dx_ref[...] = (dxv + _conv_bwd(dpre, x, next8[...], cw_ref, gcw_ref)).astype(BF16)
        next8[...] = dpre[0:SUBLANES, :]

    rev = lambda i: nt - 1 - i
    tok = pl.BlockSpec((tile, d), lambda i: (rev(i), 0))
    g128 = pl.BlockSpec((tile, LANES), lambda i: (rev(i), 0))
    wsh = (nh, dh, dh)
    return _call(
        body, comms, name="mlstm_proj_bwd", grid=(nt,),
        in_specs=[tok, tok, tok, g128, g128, pl.BlockSpec((tile, d), lambda i: (rev(i), 2)), tok,
                  tok, tok, tok, _full(conv_w.shape), _full(wsh), _full(wsh), _full(wsh), _full(wif_t.shape)],
        out_specs=[tok, _full(wsh), _full(wsh), _full(wsh), _full((LANES, 3 * d)), _full((SUBLANES, LANES)),
                   _full((SUBLANES, d)), _full((SUBLANES, d))],
        out_shape=[jax.ShapeDtypeStruct((s, d), BF16)] + [jax.ShapeDtypeStruct(wsh, F32)] * 3
        + [jax.ShapeDtypeStruct((LANES, 3 * d), F32), jax.ShapeDtypeStruct((SUBLANES, LANES), F32),
           jax.ShapeDtypeStruct((SUBLANES, d), F32), jax.ShapeDtypeStruct((SUBLANES, d), F32)],
        scratch_shapes=[pltpu.VMEM((SUBLANES, d), F32)],
        args=(dq, dk, dv, dgates, gcol, u, pre, q, k, v, conv_w, w_q, w_k, w_v, wif_t))


def _rg_bwd(dy_rg, u, h_rg, gates, conv_w, w_a, w_x, lam, tile, comms=None):
    s, d = dy_rg.shape
    nt = s // tile
    nh, dh, _ = w_a.shape

    def body(dy_ref, x_ref, z_ref, h_ref, hhalo_ref, xc_ref, r_ref, i_ref, a_ref, beta_ref, cw_ref, wa_ref,
             wx_ref, lam_ref,
             dx_ref, dz_ref, gwa_ref, gwx_ref, gba_ref, gbx_ref, glam_ref, gcw_ref, gcb_ref, next8, anext, dnext, dbuf):
        i = pl.program_id(0)

        @pl.when(i == 0)
        def _():
            for ref in (next8, anext, dnext, gwa_ref, gwx_ref, gba_ref, gbx_ref, glam_ref, gcw_ref, gcb_ref):
                ref[...] = jnp.zeros_like(ref)

        inner = jnp.where(i < nt - 1, 1.0, 0.0)
        xc, r, ig, a, beta = xc_ref[...], r_ref[...], i_ref[...], a_ref[...], beta_ref[...]
        sp = _softplus(-lam_ref[...])
        h = h_ref[...]
        row = _iota(h.shape, 0)
        hprev = jnp.where(row >= 1, pltpu.roll(h, 1, 0), hhalo_ref[SUBLANES - 1:SUBLANES, :] * inner)
        z = z_ref[...]
        sz = _sigmoid(z)
        dyv = dy_ref[...]
        dz_ref[...] = (dyv * h * (sz + z * sz * (1.0 - sz))).astype(BF16)
        a_up = jnp.where(row < tile - 1, pltpu.roll(a, tile - 1, 0), anext[0:1, :])
        _scan_into(a_up, dyv * z * sz, dnext[0:1, :], dbuf, True)
        delta = dbuf[...]
        anext[...] = a[0:SUBLANES, :]
        dnext[...] = delta[0:SUBLANES, :]
        dla = delta * hprev * a - delta * ig * xc * (a * a / beta)
        glam_ref[...] += _bcast8(_colsum(dla * r) * (RG_C * _sigmoid(-lam_ref[...])))
        dpa = dla * (-RG_C * sp) * r * (1.0 - r)
        dpx = delta * beta * xc * ig * (1.0 - ig)
        gba_ref[...] += _bcast8(_colsum(dpa))
        gbx_ref[...] += _bcast8(_colsum(dpx))
        parts = []
        for hh in range(nh):
            sl = slice(hh * dh, (hh + 1) * dh)
            gwa_ref[hh] += _mm_tn(xc[:, sl], dpa[:, sl])
            gwx_ref[hh] += _mm_tn(xc[:, sl], dpx[:, sl])
            parts.append(_mm_nt(dpa[:, sl], wa_ref[hh]) + _mm_nt(dpx[:, sl], wx_ref[hh]))
        dxc = delta * beta * ig + jnp.concatenate(parts, axis=1)
        gcb_ref[...] += _bcast8(_colsum(dxc))
        dx_ref[...] = _conv_bwd(dxc, x_ref[...], next8[...], cw_ref, gcw_ref).astype(BF16)
        next8[...] = dxc[0:SUBLANES, :]

    rev = lambda i: nt - 1 - i
    tok = pl.BlockSpec((tile, d), lambda i: (rev(i), 0))
    vec = _full((1, d))
    acc = _full((SUBLANES, d))
    wsh = (nh, dh, dh)
    return _call(
        body, comms, name="rglru_bwd", grid=(nt,),
        in_specs=[tok, tok, pl.BlockSpec((tile, d), lambda i: (rev(i), 1)), tok,
                  _halo_spec(d, tile, nt, 0)] + [tok] * 5 + [_full(conv_w.shape), _full(wsh), _full(wsh), vec],
        out_specs=[tok, tok, _full(wsh), _full(wsh), acc, acc, acc, acc, acc],
        out_shape=[jax.ShapeDtypeStruct((s, d), BF16)] * 2 + [jax.ShapeDtypeStruct(wsh, F32)] * 2
        + [jax.ShapeDtypeStruct((SUBLANES, d), F32)] * 5,
        scratch_shapes=[pltpu.VMEM((SUBLANES, d), F32)] * 3 + [pltpu.VMEM((tile, d), F32)],
        args=(dy_rg, u, u, h_rg, h_rg, *gates, conv_w, w_a, w_x, lam))


def _segments(d, w, n_pieces, n_slots):
    bounds = sorted({k * d for k in range(n_pieces + 1)} | {j * w for j in range(n_slots + 1)})
    return [(lo // d, lo % d, lo // w, lo % w, hi - lo) for lo, hi in zip(bounds[:-1], bounds[1:])]


def _in_bwd(pieces, x, dxo, ng, scale, w_in_g, layer, tile, comms=None, tiles=None, prev=None):
    s, d = x.shape
    nd, _, _, w = w_in_g.shape
    segs = _segments(d, w, len(pieces), nd)
    first, count = tiles or (0, s // tile)
    n_p = len(pieces)

    def body(*refs):
        p_refs = refs[:n_p]
        x_ref, dxo_ref, ng_ref, sc_ref, w_ref = refs[n_p:n_p + 5]
        dx_ref, dsc_ref, dsh_ref, gng_ref, wcat = refs[-5:]
        _join_columns(w_ref, wcat)

        @pl.when(pl.program_id(0) == 0)
        def _():
            for k, ref in enumerate((dsc_ref, dsh_ref, gng_ref)):
                ref[...] = jnp.zeros_like(ref) if prev is None else refs[n_p + 6 + k][...]

        du = jnp.concatenate([p[...] for p in p_refs], axis=1)
        dh = lax.dot_general(du, wcat[...], (((1,), (1,)), ((), ())), preferred_element_type=F32)
        xv = x_ref[...]
        g = ng_ref[...]
        rs = lax.rsqrt(jnp.mean(xv * xv, axis=1, keepdims=True) + EPS)
        xh = xv * rs
        dsh_ref[...] += _bcast8(_colsum(dh))
        dsc_ref[...] += _bcast8(_colsum(dh * xh * g))
        dhn = dh * (1.0 + sc_ref[...])
        gng_ref[...] += _bcast8(_colsum(dhn * xh))
        dxh = dhn * g
        dx_ref[...] = dxo_ref[...] + rs * (dxh - xh * jnp.mean(dxh * xh, axis=1, keepdims=True))

    tok = pl.BlockSpec((tile, d), lambda i: (i + first, 0))
    vec = _full((1, d))
    acc = _full((SUBLANES, d))
    more_specs = [] if prev is None else [pl.BlockSpec(memory_space=pl.ANY), acc, acc, acc]
    return _call(
        body, comms, name="in_proj_bwd_x", grid=(count,),
        in_specs=[tok] * n_p + [tok, tok, vec, vec, pl.BlockSpec((nd, 1, d, w), lambda i: (0, layer, 0, 0),
                                                               pipeline_mode=pl.Buffered(1))] + more_specs,
        out_specs=[tok, acc, acc, acc],
        out_shape=[jax.ShapeDtypeStruct((s, d), F32)] + [jax.ShapeDtypeStruct((SUBLANES, d), F32)] * 3,
        scratch_shapes=[pltpu.VMEM((d, nd * w), BF16)],
        args=(*pieces, x, dxo, ng, scale, w_in_g) + (() if prev is None else tuple(prev)),
        aliases={} if prev is None else {n_p + 5: 0})


def _in_bwd_w(pieces, hbf, w, slots, tile, comms=None):
    s, d = hbf.shape
    nd_all = len(pieces) * d // w
    segs = [sg for sg in _segments(d, w, len(pieces), nd_all) if sg[2] in slots]

    def body(*refs):
        p_refs = refs[:len(pieces)]
        h_ref, gw_ref = refs[len(pieces):]

        @pl.when(pl.program_id(0) == 0)
        def _():
            gw_ref[...] = jnp.zeros_like(gw_ref)

        hv = h_ref[...]
        for (kk, a, j, b, width) in segs:
            gw_ref[j - slots[0], :, b:b + width] += _mm_tn(hv, p_refs[kk][:, a:a + width])

    tok = pl.BlockSpec((tile, d), lambda i: (i, 0))
    return _call(
        body, comms, name="in_proj_bwd_w", grid=(s // tile,),
        in_specs=[tok] * len(pieces) + [tok],
        out_specs=[pl.BlockSpec((len(slots), d, w), lambda i: (0, 0, 0), pipeline_mode=pl.Buffered(1))],
        out_shape=[jax.ShapeDtypeStruct((len(slots), d, w), F32)],
        args=(*pieces, hbf))[0]


def _ada_bwd_w(cact_col, dmod, nd):
    d = cact_col.shape[0]
    w = dmod.shape[1] // nd

    def body(c_ref, m_ref, o_ref):
        o_ref[0] = c_ref[...] * m_ref[...]

    return pl.pallas_call(
        body, name="adaln_bwd_w", grid=(nd,),
        in_specs=[_full((d, 1)), pl.BlockSpec((1, w), lambda j: (0, j))],
        out_specs=pl.BlockSpec((1, d, w), lambda j: (j, 0, 0)),
        out_shape=jax.ShapeDtypeStruct((nd, d, w), F32),
        compiler_params=_params(1),
    )(cact_col, dmod)


def _exchange(arrs, gather, name):
    return _run_comms([_exchange_comm(arrs, gather)], name)[0]


def _run_comms(comms, name):
    _call(lambda: None, comms, name=name, grid=(1,), in_specs=[], out_specs=[], out_shape=[], args=())
    return [cm.results for cm in comms]


def _exchange_comm(arrs, gather):
    n = len(arrs)
    per = N_DEV - 1

    def copies(ins, outs, sems):
        send_sems, recv_sems, local_sems = sems
        x, y, c = (lax.axis_index(ax) for ax in MESH_AXES)
        me = 4 * x + 2 * y + c
        sends, recvs = [], []
        for flip in range(1, N_DEV):
            px = x ^ ((flip >> 2) & 1)
            py = y ^ ((flip >> 1) & 1)
            pc = c ^ (flip & 1)
            peer = 4 * px + 2 * py + pc
            for kk in range(n):
                src = ins[kk] if gather else ins[kk].at[peer]
                sends.append(_remote(src, outs[kk].at[me], send_sems, recv_sems, kk * per + flip - 1, (px, py, pc)))
                recvs.append(_remote(src, outs[kk].at[peer], send_sems, recv_sems, kk * per + flip - 1, (px, py, pc)))
        local = [pltpu.make_async_copy(ins[kk] if gather else ins[kk].at[me], outs[kk].at[me], local_sems.at[kk])
                 for kk in range(n)]
        return local, sends, recvs

    def start(ins, outs, sems):
        local, sends, _ = copies(ins, outs, sems)
        for cp in sends + local:
            cp.start()

    def finish(ins, outs, sems):
        local, sends, recvs = copies(ins, outs, sems)
        for cp in recvs:
            cp.wait_recv()
        for cp in sends:
            cp.wait_send()
        for cp in local:
            cp.wait()

    return _Comm(arrs, [jax.ShapeDtypeStruct((N_DEV,) + a.shape if gather else a.shape, a.dtype) for a in arrs],
                 [pltpu.SemaphoreType.DMA((n * per,)), pltpu.SemaphoreType.DMA((n * per,)), pltpu.SemaphoreType.DMA((n,))],
                 start, finish)


def _mesh_place():
    x, y, c = (lax.axis_index(ax) for ax in MESH_AXES)
    return x, y, c, (x, y, 1 - c), [(1 - x, y), (x, 1 - y), (1 - x, 1 - y)]


def _remote(src, dst, send_sems, recv_sems, sem, to):
    return pltpu.make_async_remote_copy(src_ref=src, dst_ref=dst, send_sem=send_sems.at[sem], recv_sem=recv_sems.at[sem],
                                        device_id=to, device_id_type=pl.DeviceIdType.MESH)


def _gather_two_level(arrs, name):
    n = len(arrs)
    per = N_DEV - 1

    def body(*refs):
        ins, outs = refs[:n], refs[n:2 * n]
        send_sems, recv_sems, local_sems = refs[2 * n:]
        x, y, c, sibling, chips = _mesh_place()

        def copy(kk, j, block, to, src=None):
            dst = outs[kk].at[4 * block[0] + 2 * block[1] + block[2]]
            return _remote(dst if src is None else src, dst, send_sems, recv_sems, kk * per + j, to)

        me = (x, y, c)
        local = [pltpu.make_async_copy(ins[kk], outs[kk].at[4 * x + 2 * y + c], local_sems.at[kk]) for kk in range(n)]
        first = []
        for j, chip in enumerate(chips):
            first += [copy(kk, 1 + j, me, (*chip, c), src=ins[kk]) for kk in range(n)]
        first += [copy(kk, 0, me, sibling, src=ins[kk]) for kk in range(n)]
        for cp in first + local:
            cp.start()
        passed = []
        for j, chip in enumerate(chips):
            for kk in range(n):
                copy(kk, 1 + j, (*chip, c), me).wait_recv()
                passed.append(copy(kk, 4 + j, (*chip, c), sibling))
                passed[-1].start()
        for kk in range(n):
            copy(kk, 0, sibling, me).wait_recv()
        for j, chip in enumerate(chips):
            for kk in range(n):
                copy(kk, 4 + j, (*chip, 1 - c), me).wait_recv()
        for cp in first + passed:
            cp.wait_send()
        for cp in local:
            cp.wait()

    return pl.pallas_call(
        body, name=name,
        in_specs=[pl.BlockSpec(memory_space=pl.ANY)] * n, out_specs=[pl.BlockSpec(memory_space=pl.ANY)] * n,
        out_shape=[jax.ShapeDtypeStruct((N_DEV,) + a.shape, a.dtype) for a in arrs],
        scratch_shapes=[pltpu.SemaphoreType.DMA((n * per,)), pltpu.SemaphoreType.DMA((n * per,)),
                        pltpu.SemaphoreType.DMA((n,))],
    )(*arrs)


N_CHIPS = N_DEV // 2


def _core_swap(arrs, name):
    n = len(arrs)

    def body(*refs):
        ins, outs = refs[:n], refs[n:2 * n]
        send_sems, recv_sems = refs[2 * n:]
        _, _, c, sibling, _ = _mesh_place()
        copies = [_remote(ins[kk].at[2 * q + (1 - c)], outs[kk].at[q], send_sems, recv_sems, kk * N_CHIPS + q, sibling)
                  for q in range(N_CHIPS) for kk in range(n)]
        for cp in copies:
            cp.start()
        for cp in copies:
            cp.wait_recv()
        for cp in copies:
            cp.wait_send()

    return pl.pallas_call(
        body, name=name,
        in_specs=[pl.BlockSpec(memory_space=pl.ANY)] * n, out_specs=[pl.BlockSpec(memory_space=pl.ANY)] * n,
        out_shape=[jax.ShapeDtypeStruct((N_CHIPS,) + a.shape[1:], a.dtype) for a in arrs],
        scratch_shapes=[pltpu.SemaphoreType.DMA((n * N_CHIPS,)), pltpu.SemaphoreType.DMA((n * N_CHIPS,))],
    )(*arrs)


def _pair_sum(a, other, parity, name):
    _, r, c = a.shape
    tr = _row_tile(r, c, 3)

    def body(p_ref, a_ref, o_ref, s_ref):
        s_ref[...] = (a_ref[...] + o_ref[...]).astype(BF16)

    return pl.pallas_call(
        body, name=name,
        grid_spec=pltpu.PrefetchScalarGridSpec(
            num_scalar_prefetch=1, grid=(N_CHIPS, r // tr),
            in_specs=[pl.BlockSpec((1, tr, c), lambda q, i, p: (2 * q + p[0], i, 0)),
                      pl.BlockSpec((1, tr, c), lambda q, i, p: (q, i, 0))],
            out_specs=pl.BlockSpec((1, tr, c), lambda q, i, p: (q, i, 0))),
        out_shape=jax.ShapeDtypeStruct((N_CHIPS, r, c), BF16),
        compiler_params=_params(2),
    )(parity, a, other)


def _chip_swap(arrs, name):
    n = len(arrs)
    per = N_CHIPS - 1

    def body(*refs):
        ins, outs = refs[:n], refs[n:2 * n]
        send_sems, recv_sems, local_sems = refs[2 * n:]
        x, y, c, _, chips = _mesh_place()
        mine = 2 * x + y
        sends = [_remote(ins[kk].at[2 * chip[0] + chip[1]], outs[kk].at[mine], send_sems, recv_sems, kk * per + j, (*chip, c))
                 for j, chip in enumerate(chips) for kk in range(n)]
        recvs = [_remote(ins[kk].at[mine], outs[kk].at[2 * chip[0] + chip[1]], send_sems, recv_sems, kk * per + j, (*chip, c))
                 for j, chip in enumerate(chips) for kk in range(n)]
        local = [pltpu.make_async_copy(ins[kk].at[mine], outs[kk].at[mine], local_sems.at[kk]) for kk in range(n)]
        for cp in sends + local:
            cp.start()
        for cp in recvs:
            cp.wait_recv()
        for cp in sends:
            cp.wait_send()
        for cp in local:
            cp.wait()

    return pl.pallas_call(
        body, name=name,
        in_specs=[pl.BlockSpec(memory_space=pl.ANY)] * n, out_specs=[pl.BlockSpec(memory_space=pl.ANY)] * n,
        out_shape=[jax.ShapeDtypeStruct(a.shape, a.dtype) for a in arrs],
        scratch_shapes=[pltpu.SemaphoreType.DMA((n * per,)), pltpu.SemaphoreType.DMA((n * per,)),
                        pltpu.SemaphoreType.DMA((n,))],
    )(*arrs)


def _adam_math(w, g, m, v):
    m = ADAM_B1 * m + (1.0 - ADAM_B1) * g
    v = ADAM_B2 * v + (1.0 - ADAM_B2) * (g * g)
    m_hat = m / (1.0 - ADAM_B1 ** ADAM_STEP)
    v_hat = v / (1.0 - ADAM_B2 ** ADAM_STEP)
    delta = -ADAM_LR * (m_hat / (jnp.sqrt(v_hat) + ADAM_EPS) + ADAM_WD * w)
    return delta, m, v


def _sum_devices(r_ref):
    acc = r_ref[0].astype(F32)
    for p in range(1, r_ref.shape[0]):
        acc = acc + r_ref[p].astype(F32)
    return acc


def _row_tile(rows, cols, n_bufs):
    budget = 24 * 1024 * 1024 // (n_bufs * 2 * cols * 4)
    t = rows
    while t > budget and t % 2 == 0 and (t // 2) % SUBLANES == 0:
        t //= 2
    return t


def _reduce_adam(recvs, w, m, v, name):
    nl, r, c = w.shape
    n_part = recvs[0].shape[0]
    tr = _row_tile(r, c, n_part * nl + 7)
    nt = r // tr

    def body(*refs):
        r_refs = refs[:nl]
        w_ref, m_ref, v_ref, g_ref, d_ref, mo_ref, vo_ref = refs[nl:]
        layer = pl.program_id(0)
        g = _sum_devices(r_refs[0])
        for ll in range(1, nl):
            g = jnp.where(layer == ll, _sum_devices(r_refs[ll]), g)
        delta, m2, v2 = _adam_math(w_ref[0], g, m_ref[0], v_ref[0])
        g_ref[0] = g
        d_ref[0] = delta
        mo_ref[0] = m2
        vo_ref[0] = v2

    def rspec(ll):
        return pl.BlockSpec((n_part, tr, c), lambda l, i: (0, jnp.where(l == ll, i, jnp.where(l < ll, 0, nt - 1)), 0))

    blk = pl.BlockSpec((1, tr, c), lambda l, i: (l, i, 0))
    return pl.pallas_call(
        body, name=name, grid=(nl, nt),
        in_specs=[rspec(ll) for ll in range(nl)] + [blk, blk, blk],
        out_specs=[blk] * 4,
        out_shape=[jax.ShapeDtypeStruct((nl, r, c), F32)] * 4,
        compiler_params=_params(2),
    )(*recvs, w, m, v)


def _sum8(recv, name):
    _, r, c = recv.shape

    def body(r_ref, o_ref):
        o_ref[...] = _sum_devices(r_ref)

    return pl.pallas_call(
        body, name=name, grid=(1,),
        in_specs=[_full(recv.shape)], out_specs=_full((r, c)),
        out_shape=jax.ShapeDtypeStruct((r, c), F32), compiler_params=_params(1),
    )(recv)


def _adam_call(w, g, m, v, name):
    r, c = w.shape

    def body(w_ref, g_ref, m_ref, v_ref, d_ref, mo_ref, vo_ref):
        delta, m2, v2 = _adam_math(w_ref[...], g_ref[...], m_ref[...], v_ref[...])
        d_ref[...] = delta
        mo_ref[...] = m2
        vo_ref[...] = v2

    return pl.pallas_call(
        body, name=name, grid=(1,),
        in_specs=[_full((r, c))] * 4, out_specs=[_full((r, c))] * 3,
        out_shape=[jax.ShapeDtypeStruct((r, c), F32)] * 3, compiler_params=_params(1),
    )(w, g, m, v)


def _tile_for(s, want):
    return min(want, s)


def _local_step_whole(x, c, target, wts):
    s, d = x.shape
    nl = wts["w_in_g"].shape[1]
    nd = wts["w_in_g"].shape[0]
    nh_ml = wts["w_qkv"].shape[2]
    t_big = _tile_for(s, 512)
    t_mid = _tile_for(s, 256)

    mod, cact = _mod_call(c, wts["w_ada_g"], wts["b_ada"])
    row = lambda a: a.reshape(1, -1)
    saved = []
    xl = x
    for l in range(nl):
        shift, scale, gate = (row(mod[l, kk * d:(kk + 1) * d]) for kk in range(3))
        u, hbf = _in_fwd(xl, row(wts["norm_g"][l]), scale, shift, wts["w_in_g"], l, t_mid)
        h_rg, y_rg = _rg_fwd(u, d, wts["rg_conv_w"][l], row(wts["rg_conv_b"][l]), wts["rg_w_a_bf"][l],
                             row(wts["rg_b_a"][l]), wts["rg_w_x_bf"][l], row(wts["rg_b_x"][l]),
                             row(wts["rg_lambda"][l]), t_mid)
        q, k, v, gcol = _ml_pre(u, d, wts["ml_conv_w"][l], row(wts["ml_conv_b"][l]), wts["w_qkv"][l, 0],
                                wts["w_qkv"][l, 1], wts["w_qkv"][l, 2], wts["wif_pad"][l], wts["bif_pad"][l], t_mid)
        grow = gcol[:, 0:16].T
        cell, y_ml, cs, ns, ms, mt = _ml_cell_fwd(q, k, v, gcol, grow, u, row(wts["ml_norm_g"][l]), nh_ml)
        x_new, y = _out_fwd(xl, y_rg, y_ml, gate, wts["w_out_g"], l, t_big)
        saved.append(dict(x=xl, u=u, hbf=hbf, h_rg=h_rg, y_rg=y_rg, q=q, k=k, v=v, gcol=gcol, grow=grow, cell=cell,
                          y_ml=y_ml, cs=cs, ns=ns, ms=ms, mt=mt, y=y, scale=scale, gate=gate))
        xl = x_new

    dx, loss_p, g_final = _loss_call(xl, row(wts["final_g"]), target, t_big)
    grads = [None] * nl
    cact_col = cact[0].reshape(d, 1)
    for l in reversed(range(nl)):
        sv = saved[l]
        dy_rg, dy_ml, gw_out, dgate = _out_bwd(dx, sv["gate"], sv["y"], sv["y_rg"], sv["y_ml"], wts["w_out_g"], l, t_big)
        dq, dk, dv, dgates, d_mlo, d_mlz, g_mlng = _ml_cell_bwd(
            dy_ml, sv["u"], sv["cell"], sv["q"], sv["k"], sv["v"], sv["gcol"], sv["grow"], sv["mt"], sv["cs"],
            sv["ns"], sv["ms"], row(wts["ml_norm_g"][l]), nh_ml)
        d_mlx, g_wq, g_wk, g_wv, g_wift, g_bif, g_mlcw, g_mlcb = _ml_pre_bwd(
            dq, dk, dv, dgates, sv["gcol"], sv["u"], sv["q"], sv["k"], sv["v"], wts["ml_conv_w"][l],
            row(wts["ml_conv_b"][l]), wts["w_qkv"][l, 0], wts["w_qkv"][l, 1], wts["w_qkv"][l, 2], wts["wift_pad"][l], t_mid)
        d_rgx, d_rgz, g_wa, g_wx, g_ba, g_bx, g_lam, g_rgcw, g_rgcb = _rg_bwd(
            dy_rg, sv["u"], sv["h_rg"], wts["rg_conv_w"][l], row(wts["rg_conv_b"][l]), wts["rg_w_a_bf"][l],
            row(wts["rg_b_a"][l]), wts["rg_w_x_bf"][l], row(wts["rg_b_x"][l]), row(wts["rg_lambda"][l]), t_mid)
        pieces = [d_rgx, d_rgz, d_mlx, d_mlo, d_mlz]
        dx, dscale, dshift, g_ng = _in_bwd(pieces, sv["x"], dx, row(wts["norm_g"][l]), sv["scale"], wts["w_in_g"], l, t_mid)
        half = nd // 2
        w_cols = wts["w_in_g"].shape[3]
        gw_in = jnp.concatenate([_in_bwd_w(pieces, sv["hbf"], w_cols, tuple(range(0, half)), t_big),
                                 _in_bwd_w(pieces, sv["hbf"], w_cols, tuple(range(half, nd)), t_big)], axis=0)
        dmod = jnp.concatenate([dshift[0:1], dscale[0:1], dgate[0:1]], axis=1)
        gw_ada = _ada_bwd_w(cact_col, dmod, nd)
        grads[l] = dict(w_ada=gw_ada, w_in=gw_in, w_out=gw_out, w_qkv=jnp.stack([g_wq, g_wk, g_wv]),
                        rg_conv_w=g_rgcw[0:CONV_WIDTH], ml_conv_w=g_mlcw[0:CONV_WIDTH], wif_t=g_wift[0:8],
                        norm_g=g_ng[0], b_ada=dmod[0], rg_conv_b=g_rgcb[0], rg_w_a=g_wa, rg_b_a=g_ba[0], rg_w_x=g_wx,
                        rg_b_x=g_bx[0], rg_lambda=g_lam[0], ml_conv_b=g_mlcb[0], ml_b_if=g_bif[0, 0:8],
                        ml_norm_g=g_mlng[0])
    return loss_p[0, 0], dx, grads, g_final[0]


REPLICATED = ("norm_g", "b_ada", "rg_conv_b", "rg_w_a", "rg_b_a", "rg_w_x", "rg_b_x", "rg_lambda", "ml_conv_b",
              "ml_b_if", "ml_norm_g", "final_g")
ROW_ALIGN = N_DEV * SUBLANES


def _to_rows(a):
    flat = a.reshape(-1)
    pad = (-flat.shape[0]) % LANES
    return jnp.pad(flat, (0, pad)).reshape(-1, LANES)


def _pack(arrays):
    rows = jnp.concatenate([_to_rows(a) for a in arrays], axis=0)
    return jnp.pad(rows, ((0, (-rows.shape[0]) % ROW_ALIGN), (0, 0)))


def _unpack(rows, like):
    out, at = [], 0
    for a in like:
        n = -(-a.size // LANES)
        out.append(rows[at:at + n].reshape(-1)[:a.size].reshape(a.shape))
        at += n
    return out


def _small_pack(rg_conv_w, ml_conv_w, ml_w_if):
    nl = rg_conv_w.shape[0]
    wif_t = jnp.swapaxes(ml_w_if, 1, 2).reshape(nl, -1, LANES)
    return jnp.concatenate([rg_conv_w, ml_conv_w, wif_t], axis=1)


def _small_unpack(p, if_rows):
    nl = p.shape[0]
    rg_cw = p[:, 0:CONV_WIDTH]
    ml_cw = p[:, CONV_WIDTH:2 * CONV_WIDTH]
    wif = jnp.swapaxes(p[:, 2 * CONV_WIDTH:].reshape(nl, 8, if_rows), 1, 2)
    return rg_cw, ml_cw, wif


def _assemble_weights(big, small, rep):
    w_ada_g, w_in_g, w_out_g, qkv_g = big
    nd, nl = small.shape[0], small.shape[1]
    d = w_in_g.shape[2]
    dh = qkv_g.shape[3]
    nh = d // dh
    rsh = qkv_g.shape[2] // (3 * nh)
    w_qkv = qkv_g.reshape(nd, nl, 3, nh, rsh, dh).transpose(1, 2, 3, 0, 4, 5).reshape(nl, 3, nh, nd * rsh, dh)
    cw = small[:, :, 0:2 * CONV_WIDTH].reshape(nd, nl, 2, CONV_WIDTH, LANES).transpose(1, 2, 3, 0, 4)
    cw = cw.reshape(nl, 2, CONV_WIDTH, nd * LANES)
    if_rows = (small.shape[2] - 2 * CONV_WIDTH) * LANES // 8
    wif_t = small[:, :, 2 * CONV_WIDTH:].reshape(nd, nl, 8, if_rows).transpose(1, 2, 0, 3).reshape(nl, 8, nd * if_rows)
    wift_pad = jnp.pad(wif_t, ((0, 0), (0, LANES - 8), (0, 0))).astype(BF16)
    wif_pad = jnp.swapaxes(wift_pad, 1, 2)
    bif_pad = jnp.pad(rep["ml_b_if"], ((0, 0), (0, LANES - 8))).reshape(nl, 1, LANES)
    wts = dict(rep)
    wts.update(w_ada_g=w_ada_g, w_in_g=w_in_g, w_out_g=w_out_g, w_qkv=w_qkv, rg_conv_w=cw[:, 0], ml_conv_w=cw[:, 1],
               wif_pad=wif_pad, wift_pad=wift_pad, bif_pad=bif_pad, rg_w_a_bf=rep["rg_w_a"].astype(BF16),
               rg_w_x_bf=rep["rg_w_x"].astype(BF16))
    return wts


def _qkv_slots(g_qkv, nd):
    three, nh, dh, _ = g_qkv.shape
    return g_qkv.reshape(three, nh, nd, dh // nd, dh).transpose(2, 0, 1, 3, 4).reshape(nd, three * nh * (dh // nd), dh)


def _small_slots(g):
    nd = N_DEV
    cw = jnp.stack([g["rg_conv_w"], g["ml_conv_w"]]).reshape(2, CONV_WIDTH, nd, LANES).transpose(2, 0, 1, 3)
    cw = cw.reshape(nd, 2 * CONV_WIDTH, LANES)
    wif = g["wif_t"].reshape(8, nd, -1).transpose(1, 0, 2).reshape(nd, -1, LANES)
    return jnp.concatenate([cw, wif], axis=1)


def _kernel_unhosted(x, c, norm_g, w_ada, b_ada, w_in, rg_conv_w, rg_conv_b, rg_w_a, rg_b_a, rg_w_x, rg_b_x, rg_lambda, ml_conv_w, ml_conv_b, ml_w_q, ml_w_k, ml_w_v, ml_w_if, ml_b_if, ml_norm_g, w_out, final_g, loss_target, m_norm_g, m_w_ada, m_b_ada, m_w_in, m_rg_conv_w, m_rg_conv_b, m_rg_w_a, m_rg_b_a, m_rg_w_x, m_rg_b_x, m_rg_lambda, m_ml_conv_w, m_ml_conv_b, m_ml_w_q, m_ml_w_k, m_ml_w_v, m_ml_w_if, m_ml_b_if, m_ml_norm_g, m_w_out, m_final_g, v_norm_g, v_w_ada, v_b_ada, v_w_in, v_rg_conv_w, v_rg_conv_b, v_rg_w_a, v_rg_b_a, v_rg_w_x, v_rg_b_x, v_rg_lambda, v_ml_conv_w, v_ml_conv_b, v_ml_w_q, v_ml_w_k, v_ml_w_v, v_ml_w_if, v_ml_b_if, v_ml_norm_g, v_w_out, v_final_g):
    given = dict(locals())
    nl = w_in.shape[0]
    rep = {n: given[n] for n in REPLICATED}

    def qkv_shard(prefix):
        return jnp.stack([given[prefix + "ml_w_q"], given[prefix + "ml_w_k"], given[prefix + "ml_w_v"]], axis=1).reshape(
            nl, -1, ml_w_q.shape[-1])

    *big, small = _gather_two_level(
        [w_ada.astype(BF16), w_in.astype(BF16), w_out.astype(BF16), qkv_shard("").astype(BF16),
         _small_pack(rg_conv_w, ml_conv_w, ml_w_if)], "gather_weights")
    wts = _assemble_weights(big, small, rep)

    loss_p, grad_x, grads, g_final = _local_step(x[0], c, loss_target[0], wts)
    loss = lax.psum(loss_p, MESH_AXES)

    keys = ("w_ada", "w_in", "w_out", "w_qkv", "small")
    parity = lax.axis_index("c").astype(jnp.int32).reshape(1)
    recv = []
    for l in range(nl):
        g = grads[l]
        parts = [g["w_ada"], g["w_in"], g["w_out"], _qkv_slots(g["w_qkv"], N_DEV), _small_slots(g)]
        other = _core_swap(parts, "core_swap_layer%d" % l)
        sums = [_pair_sum(a, o, parity, "pair_sum_%s_layer%d" % (key, l)) for key, a, o in zip(keys, parts, other)]
        recv.append(_chip_swap(sums, "chip_swap_layer%d" % l))
    shard = {"": dict(w_ada=w_ada, w_in=w_in, w_out=w_out, w_qkv=qkv_shard(""),
                      small=_small_pack(rg_conv_w, ml_conv_w, ml_w_if))}
    for p in ("m_", "v_"):
        shard[p] = dict(w_ada=given[p + "w_ada"], w_in=given[p + "w_in"], w_out=given[p + "w_out"], w_qkv=qkv_shard(p),
                        small=_small_pack(given[p + "rg_conv_w"], given[p + "ml_conv_w"], given[p + "ml_w_if"]))
    res = {}
    for ki, key in enumerate(keys):
        res[key] = _reduce_adam([recv[l][ki] for l in range(nl)], shard[""][key], shard["m_"][key], shard["v_"][key],
                                "reduce_adam_" + key)

    rep_g = dict(final_g=g_final)
    for n in REPLICATED[:-1]:
        rep_g[n] = jnp.stack([grads[l][n] for l in range(nl)])
    pack_g = _pack([rep_g[n] for n in REPLICATED])
    rows = pack_g.shape[0] // N_DEV
    mine = _sum8(_exchange([pack_g.reshape(N_DEV, rows, LANES)], False, "reduce_scatter_replicated")[0], "sum_replicated")
    g_rep = _exchange([mine], True, "gather_replicated")[0].reshape(N_DEV * rows, LANES)
    rep_like = [rep[n] for n in REPLICATED]
    d_rep, m_rep, v_rep = _adam_call(_pack(rep_like), g_rep, _pack([given["m_" + n] for n in REPLICATED]),
                                     _pack([given["v_" + n] for n in REPLICATED]), "adam_replicated")
    rep_out = [dict(zip(REPLICATED, _unpack(a, rep_like))) for a in (g_rep, d_rep, m_rep, v_rep)]

    if_rows = ml_w_if.shape[1]
    order = ("norm_g", "w_ada", "b_ada", "w_in", "rg_conv_w", "rg_conv_b", "rg_w_a", "rg_b_a", "rg_w_x", "rg_b_x",
             "rg_lambda", "ml_conv_w", "ml_conv_b", "ml_w_q", "ml_w_k", "ml_w_v", "ml_w_if", "ml_b_if", "ml_norm_g",
             "w_out", "final_g")
    outs = [loss, grad_x[None]]
    for kind in range(4):
        qkv = res["w_qkv"][kind].reshape((nl, 3) + ml_w_q.shape[1:])
        rg_cw, ml_cw, wif = _small_unpack(res["small"][kind], if_rows)
        sharded = dict(w_ada=res["w_ada"][kind], w_in=res["w_in"][kind], w_out=res["w_out"][kind], ml_w_q=qkv[:, 0],
                       ml_w_k=qkv[:, 1], ml_w_v=qkv[:, 2], rg_conv_w=rg_cw, ml_conv_w=ml_cw, ml_w_if=wif)
        for n in order:
            outs.append(sharded[n] if n in sharded else rep_out[kind][n])
    return tuple(outs)


def _slot(block):
    return 4 * block[0] + 2 * block[1] + block[2]


def _dma_sems(*counts):
    return [pltpu.SemaphoreType.DMA((n,)) for n in counts]


def _start_all(copies):
    for cp in copies:
        cp.start()


def _gather_ici_comm(arrs):
    n = len(arrs)

    def copies(ins, outs, sems):
        send_sems, recv_sems, local_sems = sems
        x, y, c, sibling, chips = _mesh_place()
        me = (x, y, c)
        peers = [(*chip, c) for chip in chips] + [sibling]
        local = [pltpu.make_async_copy(ins[kk], outs[kk].at[_slot(me)], local_sems.at[kk]) for kk in range(n)]
        sends = [_remote(ins[kk], outs[kk].at[_slot(me)], send_sems, recv_sems, kk * 4 + j, peer)
                 for j, peer in enumerate(peers) for kk in range(n)]
        recvs = [_remote(ins[kk], outs[kk].at[_slot(peer)], send_sems, recv_sems, kk * 4 + j, peer)
                 for j, peer in enumerate(peers) for kk in range(n)]
        return local, sends, recvs

    def start(ins, outs, sems):
        local, sends, _ = copies(ins, outs, sems)
        _start_all(sends + local)

    def finish(ins, outs, sems):
        local, sends, recvs = copies(ins, outs, sems)
        for cp in recvs:
            cp.wait_recv()
        for cp in sends:
            cp.wait_send()
        for cp in local:
            cp.wait()

    return _Comm(arrs, [jax.ShapeDtypeStruct((N_DEV,) + a.shape, a.dtype) for a in arrs], _dma_sems(4 * n, 4 * n, n),
                 start, finish)


def _gather_fwd_comm(bufs):
    n = len(bufs)

    def copies(ins, outs, sems):
        send_sems, recv_sems = sems
        _, _, c, sibling, chips = _mesh_place()
        sends = [_remote(ins[kk].at[_slot((*chip, c))], outs[kk].at[_slot((*chip, c))], send_sems, recv_sems, kk * 3 + j, sibling)
                 for j, chip in enumerate(chips) for kk in range(n)]
        recvs = [_remote(ins[kk].at[_slot((*chip, c))], outs[kk].at[_slot((*chip, 1 - c))], send_sems, recv_sems, kk * 3 + j, sibling)
                 for j, chip in enumerate(chips) for kk in range(n)]
        return sends, recvs

    def start(ins, outs, sems):
        _start_all(copies(ins, outs, sems)[0])

    def finish(ins, outs, sems):
        sends, recvs = copies(ins, outs, sems)
        for cp in recvs:
            cp.wait_recv()
        for cp in sends:
            cp.wait_send()

    return _Comm(bufs, [jax.ShapeDtypeStruct(a.shape, a.dtype) for a in bufs], _dma_sems(3 * n, 3 * n), start, finish,
                 aliases=[(i, i) for i in range(n)])


def _core_swap_comm(arrs):
    n = len(arrs)

    def copies(ins, outs, sems):
        send_sems, recv_sems = sems
        _, _, c, sibling, _ = _mesh_place()
        return [_remote(ins[kk].at[2 * q + (1 - c)], outs[kk].at[q], send_sems, recv_sems, kk * N_CHIPS + q, sibling)
                for q in range(N_CHIPS) for kk in range(n)]

    def start(ins, outs, sems):
        _start_all(copies(ins, outs, sems))

    def finish(ins, outs, sems):
        cps = copies(ins, outs, sems)
        for cp in cps:
            cp.wait_recv()
        for cp in cps:
            cp.wait_send()

    return _Comm(arrs, [jax.ShapeDtypeStruct((N_CHIPS,) + a.shape[1:], a.dtype) for a in arrs],
                 _dma_sems(N_CHIPS * n, N_CHIPS * n), start, finish)


def _chip_swap_comm(arrs):
    n = len(arrs)
    per = N_CHIPS - 1

    def copies(ins, outs, sems):
        send_sems, recv_sems, local_sems = sems
        x, y, c, _, chips = _mesh_place()
        mine = 2 * x + y
        sends = [_remote(ins[kk].at[2 * chip[0] + chip[1]], outs[kk].at[mine], send_sems, recv_sems, kk * per + j, (*chip, c))
                 for j, chip in enumerate(chips) for kk in range(n)]
        recvs = [_remote(ins[kk].at[mine], outs[kk].at[2 * chip[0] + chip[1]], send_sems, recv_sems, kk * per + j, (*chip, c))
                 for j, chip in enumerate(chips) for kk in range(n)]
        local = [pltpu.make_async_copy(ins[kk].at[mine], outs[kk].at[mine], local_sems.at[kk]) for kk in range(n)]
        return local, sends, recvs

    def start(ins, outs, sems):
        local, sends, _ = copies(ins, outs, sems)
        _start_all(sends + local)

    def finish(ins, outs, sems):
        local, sends, recvs = copies(ins, outs, sems)
        for cp in recvs:
            cp.wait_recv()
        for cp in sends:
            cp.wait_send()
        for cp in local:
            cp.wait()

    return _Comm(arrs, [jax.ShapeDtypeStruct(a.shape, a.dtype) for a in arrs], _dma_sems(per * n, per * n, n), start, finish)


def _ada_mod(c_all, w_ada, b_cols, comms=None):
    nl, d, w = w_ada.shape

    def body(c_ref, w_ref, b_ref, m_ref, ca_ref):
        sub = _iota((SUBLANES, d), 0)
        cv = jnp.zeros((SUBLANES, d), F32)
        for b in range(N_DEV):
            cv = jnp.where(sub == b, c_ref[b], cv)
        ca = cv * _sigmoid(cv)
        ca_ref[...] = ca
        m_ref[...] = jnp.zeros_like(m_ref)
        for l in range(nl):
            ml = _mm_hi(ca, w_ref[l]) + b_ref[l:l + 1, :]
            for b in range(N_DEV):
                m_ref[b, l:l + 1, :] = _row(ml, b)

    return _call(
        body, comms, name="adaln_mod_columns", grid=(1,),
        in_specs=[_full(c_all.shape), _full(w_ada.shape), _full(b_cols.shape)],
        out_specs=[_full((N_DEV, SUBLANES, w)), _full((SUBLANES, d))],
        out_shape=[jax.ShapeDtypeStruct((N_DEV, SUBLANES, w), F32), jax.ShapeDtypeStruct((SUBLANES, d), F32)],
        args=(c_all, w_ada, b_cols))


def _ada_grad_adam(cact_t, dmods, w, m, v):
    nl, d, wd = w.shape
    tr = _row_tile(d, wd, 8)

    def body(c_ref, dm_ref, w_ref, m_ref, v_ref, g_ref, d_ref, mo_ref, vo_ref):
        cv = c_ref[...]
        dm = dm_ref[0]
        g = _col(cv, 0) * _row(dm, 0)
        for b in range(1, N_DEV):
            g = g + _col(cv, b) * _row(dm, b)
        delta, m2, v2 = _adam_math(w_ref[0], g, m_ref[0], v_ref[0])
        g_ref[0] = g
        d_ref[0] = delta
        mo_ref[0] = m2
        vo_ref[0] = v2

    blk = pl.BlockSpec((1, tr, wd), lambda l, i: (l, i, 0))
    return pl.pallas_call(
        body, name="adaln_grad_adam", grid=(nl, d // tr),
        in_specs=[pl.BlockSpec((tr, N_DEV), lambda l, i: (i, 0)), pl.BlockSpec((1, N_DEV, wd), lambda l, i: (l, 0, 0)),
                  blk, blk, blk],
        out_specs=[blk] * 4, out_shape=[jax.ShapeDtypeStruct((nl, d, wd), F32)] * 4,
        compiler_params=_params(2),
    )(cact_t, dmods, w, m, v)


REP_ROWS = ("norm_g", "dshift", "dscale", "dgate", "rg_conv_b", "rg_b_a", "rg_b_x", "rg_lambda", "ml_conv_b", "ml_norm_g",
            "ml_b_if")


def _sum_parts(recvs, name):
    def body(*refs):
        for r_ref, o_ref in zip(refs[:len(recvs)], refs[len(recvs):]):
            o_ref[...] = _sum_devices(r_ref).astype(o_ref.dtype)

    return pl.pallas_call(
        body, name=name, grid=(1,),
        in_specs=[_full(r.shape) for r in recvs], out_specs=[_full(r.shape[1:]) for r in recvs],
        out_shape=[jax.ShapeDtypeStruct(r.shape[1:], r.dtype) for r in recvs], compiler_params=_params(1),
    )(*recvs)


def _adam_replicated(vp, mp, params, nl):
    d = vp.shape[1]
    nr = len(REP_ROWS)
    names = list(params)
    mat_shape = params["rg_w_a"][0].shape[1:]
    mat_rows = mp.shape[0] // (2 * nl)

    def pieces(name):
        if name == "final_g":
            return [(lambda vp_ref, mp_ref: vp_ref[nl * nr:nl * nr + 1, :], (slice(0, 1), slice(None)))]
        out = []
        for l in range(nl):
            if name in ("rg_w_a", "rg_w_x"):
                at = (2 * l + (name == "rg_w_x")) * mat_rows
                out.append((lambda vp_ref, mp_ref, at=at: mp_ref[at:at + mat_rows, :].astype(F32).reshape(mat_shape), l))
            elif name == "b_ada":
                for j in range(3):
                    r = l * nr + 1 + j
                    out.append((lambda vp_ref, mp_ref, r=r: vp_ref[r:r + 1, :], (slice(l, l + 1), slice(j * d, (j + 1) * d))))
            else:
                r = l * nr + REP_ROWS.index(name)
                cols = slice(0, LANES) if name == "ml_b_if" else slice(None)
                out.append((lambda vp_ref, mp_ref, r=r, cols=cols: vp_ref[r:r + 1, cols], (slice(l, l + 1), slice(None))))
        return out

    def body(*refs):
        vp_ref, mp_ref = refs[:2]
        ins, outs = refs[2:2 + 3 * len(names)], refs[2 + 3 * len(names):]
        for pi, name in enumerate(names):
            w_ref, m_ref, v_ref = ins[3 * pi:3 * pi + 3]
            g_ref, d_ref, mo_ref, vo_ref = outs[4 * pi:4 * pi + 4]
            for get, idx in pieces(name):
                g = get(vp_ref, mp_ref)
                delta, m2, v2 = _adam_math(w_ref[idx], g, m_ref[idx], v_ref[idx])
                g_ref[idx] = g
                d_ref[idx] = delta
                mo_ref[idx] = m2
                vo_ref[idx] = v2

    flat = [a for name in names for a in params[name]]
    out_shape = [jax.ShapeDtypeStruct(params[name][0].shape, F32) for name in names for _ in range(4)]
    res = pl.pallas_call(
        body, name="adam_replicated", grid=(1,),
        in_specs=[_full(vp.shape), _full(mp.shape)] + [_full(a.shape) for a in flat],
        out_specs=[_full(o.shape) for o in out_shape], out_shape=out_shape, compiler_params=_params(1),
    )(vp, mp, *flat)
    return {name: res[4 * pi:4 * pi + 4] for pi, name in enumerate(names)}


class _Plan:
    def __init__(self):
        self.hosted, self.after = {}, {}

    def host(self, key, comm, then=None):
        self.hosted.setdefault(key, []).append(comm)
        if then is not None:
            self.after.setdefault(key, []).append(then)

    def comms(self, key):
        return self.hosted.pop(key, None)

    def done(self, key):
        for fn in self.after.pop(key, []):
            fn()

    def flush(self):
        while self.hosted:
            key = next(iter(self.hosted))
            _call(lambda: None, self.comms(key), name="exchange_after_%s_%d" % key, grid=(1,), in_specs=[], out_specs=[],
                  out_shape=[], args=())
            self.done(key)


VEC_TABLE = ("norm_g", "rg_conv_b", "rg_b_a", "rg_b_x", "rg_lambda", "ml_conv_b", "ml_norm_g")


def _vec_table(rep):
    rows = [rep[n] for n in VEC_TABLE]
    return jnp.stack(rows + [jnp.zeros_like(rows[0])] * (SUBLANES - len(rows)), axis=1)


def _layer_fwd(l, xl, mod3, wl, rep, plan):
    s, d = xl.shape
    t_big, t_mid = _tile_for(s, 512), _tile_for(s, 256)
    nh_ml = rep["ml_b_if"].shape[1] // 2
    vec = lambda name: _vec(rep["vecs"], l, VEC_TABLE.index(name))
    shift, scale, gate = (_vec(mod3, l, kk) for kk in range(3))
    hosted = lambda name: plan.comms((name, l)) if plan else None
    done = lambda name: plan.done((name, l)) if plan else None
    u, hbf = _in_fwd(xl, vec("norm_g"), scale, shift, wl["w_in_g"], 0, t_mid, hosted("in_proj_fwd"))
    done("in_proj_fwd")
    h_rg, y_rg, *rg_gates = _rg_fwd(u, d, wl["rg_conv_w"], vec("rg_conv_b"), rep["rg_w_a_bf"][l], vec("rg_b_a"),
                                    rep["rg_w_x_bf"][l], vec("rg_b_x"), vec("rg_lambda"), t_mid, hosted("rglru_fwd"))
    done("rglru_fwd")
    q, k, v, gcol, pre = _ml_pre(u, d, wl["ml_conv_w"], vec("ml_conv_b"), wl["w_qkv"][0], wl["w_qkv"][1],
                                 wl["w_qkv"][2], wl["wif_pad"], wl["bif_pad"], t_mid, hosted("mlstm_proj_fwd"))
    done("mlstm_proj_fwd")
    grow = gcol[:, 0:16].T
    cell, y_ml, cs, ns, ms, mt = _ml_cell_fwd(q, k, v, gcol, grow, u, vec("ml_norm_g"), nh_ml, hosted("mlstm_cell_fwd"))
    done("mlstm_cell_fwd")
    x_new, y = _out_fwd(xl, y_rg, y_ml, gate, wl["w_out_g"], 0, t_big, hosted("out_proj_fwd"))
    done("out_proj_fwd")
    saved = dict(x=xl, u=u, hbf=hbf, h_rg=h_rg, y_rg=y_rg, q=q, k=k, v=v, gcol=gcol, grow=grow, cell=cell, y_ml=y_ml,
                 cs=cs, ns=ns, ms=ms, mt=mt, y=y, scale=scale, gate=gate, rg_gates=rg_gates, pre=pre)
    return x_new, saved


def _layer_bwd(l, dx, sv, wl, rep, plan, grads=None, split_last=False):
    s, d = dx.shape
    t_big, t_mid = _tile_for(s, 512), _tile_for(s, 256)
    nh_ml = rep["ml_b_if"].shape[1] // 2
    nd, _, _, w_cols = wl["w_in_g"].shape
    grads = {} if grads is None else grads
    vec = lambda name: _vec(rep["vecs"], l, VEC_TABLE.index(name))
    hosted = lambda name: plan.comms((name, l)) if plan else None
    done = lambda name: plan.done((name, l)) if plan else None
    dy_rg, dcell, d_mlo, d_mlz, gw_out, dgate, g_mlng = _out_bwd(
        dx, sv["gate"], sv["y"], sv["y_rg"], sv["y_ml"], sv["cell"], sv["u"], vec("ml_norm_g"), wl["w_out_g"], 0, t_mid,
        nh_ml, hosted("out_proj_bwd"))
    grads.update(w_out=gw_out)
    done("out_proj_bwd")
    dq, dk, dv, dgates = _ml_cell_bwd(dcell, sv["cell"], sv["q"], sv["k"], sv["v"], sv["gcol"], sv["grow"], sv["mt"],
                                      sv["cs"], sv["ns"], sv["ms"], nh_ml, hosted("mlstm_cell_bwd"))
    done("mlstm_cell_bwd")
    d_mlx, g_wq, g_wk, g_wv, g_wift, g_bif, g_mlcw, g_mlcb = _ml_pre_bwd(
        dq, dk, dv, dgates, sv["gcol"], sv["u"], sv["pre"], sv["q"], sv["k"], sv["v"], wl["ml_conv_w"],
        wl["w_qkv"][0], wl["w_qkv"][1], wl["w_qkv"][2], wl["wift_pad"], t_mid, hosted("mlstm_proj_bwd"))
    done("mlstm_proj_bwd")
    d_rgx, d_rgz, g_wa, g_wx, g_ba, g_bx, g_lam, g_rgcw, g_rgcb = _rg_bwd(
        dy_rg, sv["u"], sv["h_rg"], sv["rg_gates"], wl["rg_conv_w"], rep["rg_w_a_bf"][l], rep["rg_w_x_bf"][l],
        vec("rg_lambda"), t_mid, hosted("rglru_bwd"))
    grads.update(w_qkv=jnp.stack([g_wq, g_wk, g_wv]), rg_conv_w=g_rgcw[0:CONV_WIDTH], ml_conv_w=g_mlcw[0:CONV_WIDTH],
                 wif_t=g_wift[0:8], rg_w_a=g_wa, rg_w_x=g_wx)
    acc = dict(dgate=dgate, rg_conv_b=g_rgcb, rg_b_a=g_ba, rg_b_x=g_bx, rg_lambda=g_lam, ml_conv_b=g_mlcb,
               ml_b_if=g_bif, ml_norm_g=g_mlng)
    done("rglru_bwd")
    pieces = [d_rgx, d_rgz, d_mlx, d_mlo, d_mlz]
    grads.update(w_in=_in_bwd_w(pieces, sv["hbf"], w_cols, tuple(range(nd)), t_big, hosted("in_proj_bwd_w")))
    done("in_proj_bwd_w")
    n_tiles = s // t_mid
    counts = [n_tiles // 4, n_tiles - n_tiles // 4 - 1, 1] if split_last and n_tiles >= 4 else [n_tiles]
    in_args = (pieces, sv["x"], dx, vec("norm_g"), sv["scale"], wl["w_in_g"], 0, t_mid)
    res, at = None, 0
    for key, count in zip(("in_proj_bwd_x", "in_proj_bwd_x_rest", "in_proj_bwd_x_end"), counts):
        res = _in_bwd(*in_args, hosted(key), (at, count), res)
        done(key)
        at += count
    dx, dscale, dshift, g_ng = res
    acc.update(norm_g=g_ng, dshift=dshift, dscale=dscale)
    grads.update(acc=acc, dmod=jnp.concatenate([dshift[0:1], dscale[0:1], dgate[0:1]], axis=1))
    return dx, grads


def _local_step(x, c, target, wts):
    d = x.shape[1]
    nl = wts["w_in_g"].shape[1]
    mod, cact = _mod_call(c, wts["w_ada_g"], wts["b_ada"])
    wl = [dict(w_in_g=wts["w_in_g"][:, l:l + 1], w_out_g=wts["w_out_g"][:, l:l + 1], w_qkv=wts["w_qkv"][l],
               rg_conv_w=wts["rg_conv_w"][l], ml_conv_w=wts["ml_conv_w"][l], wif_pad=wts["wif_pad"][l],
               wift_pad=wts["wift_pad"][l], bif_pad=wts["bif_pad"][l]) for l in range(nl)]
    rep = dict(wts, vecs=_vec_table(wts))
    mod3 = mod.reshape(nl, 3, d)
    saved, xl = [], x
    for l in range(nl):
        xl, sv = _layer_fwd(l, xl, mod3, wl[l], rep, None)
        saved.append(sv)
    dx, loss_p, g_final = _loss_call(xl, wts["final_g"].reshape(1, -1), target, _tile_for(x.shape[0], 512))
    grads = [None] * nl
    for l in reversed(range(nl)):
        dx, grads[l] = _layer_bwd(l, dx, saved[l], wl[l], rep, None)
        grads[l]["w_ada"] = _ada_bwd_w(cact[0].reshape(d, 1), grads[l]["dmod"], wts["w_ada_g"].shape[0])
        grads[l]["b_ada"] = grads[l]["dmod"][0]
        grads[l].update({n: a[0] for n, a in grads[l]["acc"].items()})
        grads[l]["ml_b_if"] = grads[l]["ml_b_if"][0:8]
    return loss_p[0, 0], dx, grads, g_final[0]


def _full_qkv(qkv_g, d):
    nd, _, rows3, dh = qkv_g.shape
    nh = d // dh
    rsh = rows3 // (3 * nh)
    return qkv_g.reshape(nd, 3, nh, rsh, dh).transpose(1, 2, 0, 3, 4).reshape(3, nh, nd * rsh, dh)


def _small_weights(small, l, ml_b_if):
    nd = small.shape[0]
    sm = small[:, l]
    cw = sm[:, 0:2 * CONV_WIDTH].reshape(nd, 2, CONV_WIDTH, LANES).transpose(1, 2, 0, 3).reshape(2, CONV_WIDTH, nd * LANES)
    if_rows = (sm.shape[1] - 2 * CONV_WIDTH) * LANES // 8
    wif_t = sm[:, 2 * CONV_WIDTH:].reshape(nd, 8, if_rows).transpose(1, 0, 2).reshape(8, nd * if_rows)
    wift_pad = jnp.pad(wif_t, ((0, LANES - 8), (0, 0))).astype(BF16)
    return dict(rg_conv_w=cw[0], ml_conv_w=cw[1], wift_pad=wift_pad, wif_pad=wift_pad.T,
                bif_pad=jnp.pad(ml_b_if[l], (0, LANES - 8)).reshape(1, LANES))


def kernel(x, c, norm_g, w_ada, b_ada, w_in, rg_conv_w, rg_conv_b, rg_w_a, rg_b_a, rg_w_x, rg_b_x, rg_lambda, ml_conv_w, ml_conv_b, ml_w_q, ml_w_k, ml_w_v, ml_w_if, ml_b_if, ml_norm_g, w_out, final_g, loss_target, m_norm_g, m_w_ada, m_b_ada, m_w_in, m_rg_conv_w, m_rg_conv_b, m_rg_w_a, m_rg_b_a, m_rg_w_x, m_rg_b_x, m_rg_lambda, m_ml_conv_w, m_ml_conv_b, m_ml_w_q, m_ml_w_k, m_ml_w_v, m_ml_w_if, m_ml_b_if, m_ml_norm_g, m_w_out, m_final_g, v_norm_g, v_w_ada, v_b_ada, v_w_in, v_rg_conv_w, v_rg_conv_b, v_rg_w_a, v_rg_b_a, v_rg_w_x, v_rg_b_x, v_rg_lambda, v_ml_conv_w, v_ml_conv_b, v_ml_w_q, v_ml_w_k, v_ml_w_v, v_ml_w_if, v_ml_b_if, v_ml_norm_g, v_w_out, v_final_g):
    given = dict(locals())
    nl = w_in.shape[0]
    d = x.shape[2]
    rep = {n: given[n] for n in REPLICATED}
    rep.update(rg_w_a_bf=rg_w_a.astype(BF16), rg_w_x_bf=rg_w_x.astype(BF16))
    bf = lambda a: a.astype(BF16)

    def qkv_shard(prefix):
        return jnp.stack([given[prefix + "ml_w_q"], given[prefix + "ml_w_k"], given[prefix + "ml_w_v"]], axis=1).reshape(
            nl, -1, ml_w_q.shape[-1])

    def small_shard(prefix):
        return _small_pack(given[prefix + "rg_conv_w"], given[prefix + "ml_conv_w"], given[prefix + "ml_w_if"])

    plan = _Plan()
    qkv = qkv_shard("")
    first_ici = _gather_ici_comm([bf(w_in[0:1]), small_shard("")])
    condition = _exchange_comm([jnp.broadcast_to(c, (SUBLANES, d))], True)
    _run_comms([first_ici, condition], "gather_first")
    first_fwd = _gather_fwd_comm(first_ici.results)
    wcols = w_ada.shape[2]
    me = 4 * lax.axis_index("x") + 2 * lax.axis_index("y") + lax.axis_index("c")
    b_cols = jnp.pad(lax.dynamic_slice_in_dim(b_ada, me * wcols, wcols, axis=1), ((0, SUBLANES - nl), (0, 0)))
    mod_cols, cact_all = _ada_mod(condition.results[0], w_ada, b_cols, [first_fwd])
    w_in_first, small = first_fwd.results
    wl = [_small_weights(small, l, ml_b_if) for l in range(nl)]
    wl[0]["w_in_g"] = w_in_first

    def gather_behind(arrs, ici_host, fwd_host, then):
        ici = _gather_ici_comm(arrs)

        def pass_on():
            fwd = _gather_fwd_comm(ici.results)
            plan.host(fwd_host, fwd, lambda: then(fwd.results))

        plan.host(ici_host, ici, pass_on)

    def got_out(l):
        return lambda r: wl[l].update(w_out_g=r[0], w_qkv=_full_qkv(r[1], d))

    gather_behind([bf(w_out[0:1]), bf(qkv[0:1])], ("in_proj_fwd", 0), ("rglru_fwd", 0), got_out(0))
    for l in range(1, nl):
        gather_behind([bf(w_in[l:l + 1])], ("rglru_fwd", l - 1), ("mlstm_proj_fwd", l - 1),
                      lambda r, l=l: wl[l].update(w_in_g=r[0]))
        gather_behind([bf(w_out[l:l + 1]), bf(qkv[l:l + 1])], ("mlstm_cell_fwd", l - 1), ("out_proj_fwd", l - 1), got_out(l))

    mod_blocks = _exchange([mod_cols], False, "scatter_modulation")[0]
    mod3 = mod_blocks[:, 0:nl].transpose(1, 0, 2).reshape(nl, 3, d)
    rep["vecs"] = _vec_table(rep)
    saved, xl = [], x[0]
    for l in range(nl):
        xl, sv = _layer_fwd(l, xl, mod3, wl[l], rep, plan)
        saved.append(sv)
    grad_x, loss_p, g_final = _loss_call(xl, final_g.reshape(1, -1), loss_target[0], _tile_for(xl.shape[0], 512))

    keys = ("w_in", "w_out", "w_qkv", "small")
    parity = lax.axis_index("c").astype(jnp.int32).reshape(1)
    grads, recv = [None] * nl, [None] * nl

    def parts_of(g):
        return [g["w_in"], g["w_out"], _qkv_slots(g["w_qkv"], N_DEV), _small_slots(g)]

    def pair_sums(l, parts, other):
        return [_pair_sum(a, o, parity, "pair_sum_%s_layer%d" % (key, l)) for key, a, o in zip(keys, parts, other)]

    def reduce_behind(l, host_layer):
        parts = parts_of(grads[l])
        swap = _core_swap_comm(parts)

        def summed():
            sums = pair_sums(l, parts, swap.results)
            big = _chip_swap_comm([sums[0]])
            rest = _chip_swap_comm(sums[1:])
            plan.host(("mlstm_cell_bwd", host_layer), big)
            plan.host(("rglru_bwd", host_layer), rest, lambda: recv.__setitem__(l, big.results + rest.results))

        plan.host(("out_proj_bwd", host_layer), swap, summed)

    first, own = {}, {}

    def reduce_own(names, parts_fn, ready_key, swap_key, chip_key):
        def go():
            parts = parts_fn()
            swap = _core_swap_comm(parts)

            def summed():
                sums = [_pair_sum(a, o, parity, "pair_sum_%s_layer0" % n) for n, a, o in zip(names, parts, swap.results)]
                chip = _chip_swap_comm(sums)
                plan.host(chip_key, chip, lambda: own.update(zip(names, chip.results)))

            plan.host(swap_key, swap, summed)

        plan.after.setdefault(ready_key, []).append(go)

    reduce_own(["w_out"], lambda: [first["w_out"]], ("out_proj_bwd", 0), ("mlstm_cell_bwd", 0), ("mlstm_proj_bwd", 0))
    reduce_own(["w_qkv", "small"], lambda: [_qkv_slots(first["w_qkv"], N_DEV), _small_slots(first)],
               ("rglru_bwd", 0), ("in_proj_bwd_w", 0), ("in_proj_bwd_x", 0))
    reduce_own(["w_in"], lambda: [first["w_in"]], ("in_proj_bwd_w", 0), ("in_proj_bwd_x", 0), ("in_proj_bwd_x_rest", 0))

    for l in reversed(range(nl)):
        if l > 0:
            grad_x, grads[l] = _layer_bwd(l, grad_x, saved[l], wl[l], rep, plan)
            reduce_behind(l, l - 1)
        else:
            grad_x, grads[l] = _layer_bwd(l, grad_x, saved[l], wl[l], rep, plan, first, True)
    plan.flush()
    recv[0] = [own[key] for key in keys]

    shard = {p: dict(w_in=given[p + "w_in"], w_out=given[p + "w_out"], w_qkv=qkv_shard(p), small=small_shard(p))
             for p in ("", "m_", "v_")}
    res = {}
    for ki, key in enumerate(keys):
        res[key] = _reduce_adam([recv[l][ki] for l in range(nl)], shard[""][key], shard["m_"][key], shard["v_"][key],
                                "reduce_adam_" + key)

    dmods = jnp.concatenate([grads[l]["dmod"] for l in range(nl)], axis=0)
    dmod_blocks = jnp.pad(dmods.reshape(nl, N_DEV, wcols).transpose(1, 0, 2), ((0, 0), (0, SUBLANES - nl), (0, 0)))
    dmod_all = _exchange([dmod_blocks], False, "scatter_dmod")[0][:, 0:nl].transpose(1, 0, 2)
    res["w_ada"] = _ada_grad_adam(cact_all.T, dmod_all, w_ada, m_w_ada, v_w_ada)

    widen = lambda a: jnp.pad(a, ((0, 0), (0, d - a.shape[1])))
    rows = [widen(grads[l]["acc"][n][0:1]) for l in range(nl) for n in REP_ROWS] + [g_final[0:1], widen(loss_p[0:1])]
    vp = jnp.concatenate(rows + [jnp.zeros(((-len(rows)) % ROW_ALIGN, d), F32)], axis=0)
    mp = jnp.stack([jnp.stack([grads[l]["rg_w_a"], grads[l]["rg_w_x"]]) for l in range(nl)]).reshape(-1, LANES).astype(BF16)
    got = _exchange([vp.reshape(N_DEV, -1, d), mp.reshape(N_DEV, -1, LANES)], False, "reduce_scatter_replicated")
    vp_r, mp_r = _exchange(_sum_parts(got, "sum_replicated"), True, "gather_replicated")
    vp_r, mp_r = vp_r.reshape(-1, d), mp_r.reshape(-1, LANES)
    lanes = lambda a: jnp.pad(a, ((0, 0), (0, LANES - a.shape[1])))
    shaped = dict(ml_b_if=lanes, final_g=lambda a: a.reshape(1, d))
    names = [n for n in REPLICATED if n != "b_ada"] + ["b_ada"]
    rep_res = _adam_replicated(vp_r, mp_r, {n: tuple(shaped.get(n, lambda a: a)(given[p + n]) for p in ("", "m_", "v_"))
                                            for n in names}, nl)
    unshaped = dict(ml_b_if=lambda a: a[:, 0:ml_b_if.shape[1]], final_g=lambda a: a.reshape(d))
    rep_out = [{n: unshaped.get(n, lambda a: a)(rep_res[n][kind]) for n in names} for kind in range(4)]
    loss = vp_r[nl * len(REP_ROWS) + 1, 0]

    if_rows = ml_w_if.shape[1]
    order = ("norm_g", "w_ada", "b_ada", "w_in", "rg_conv_w", "rg_conv_b", "rg_w_a", "rg_b_a", "rg_w_x", "rg_b_x",
             "rg_lambda", "ml_conv_w", "ml_conv_b", "ml_w_q", "ml_w_k", "ml_w_v", "ml_w_if", "ml_b_if", "ml_norm_g",
             "w_out", "final_g")
    outs = [loss, grad_x[None]]
    for kind in range(4):
        qkv_k = res["w_qkv"][kind].reshape((nl, 3) + ml_w_q.shape[1:])
        rg_cw, ml_cw, wif = _small_unpack(res["small"][kind], if_rows)
        sharded = dict(w_ada=res["w_ada"][kind], w_in=res["w_in"][kind], w_out=res["w_out"][kind], ml_w_q=qkv_k[:, 0],
                       ml_w_k=qkv_k[:, 1], ml_w_v=qkv_k[:, 2], rg_conv_w=rg_cw, ml_conv_w=ml_cw, ml_w_if=wif)
        for n in order:
            outs.append(sharded[n] if n in sharded else rep_out[kind][n])
    return tuple(outs)
```

```python
import functools

import jax
import jax.numpy as jnp
from jax import lax
from jax.experimental import pallas as pl
from jax.experimental.pallas import tpu as pltpu

F32 = jnp.float32
BF16 = jnp.bfloat16
MESH_AXES = ("x", "y", "c")
N_DEV = 8
EPS = 1e-6
RG_C = 8.0
ML_CHUNK = 128
CONV_WIDTH = 4
ADAM_LR = 0.001
ADAM_B1 = 0.9
ADAM_B2 = 0.999
ADAM_EPS = 1e-08
ADAM_WD = 0.01
ADAM_STEP = 10
NEG_BIG = -1e30
LANES = 128
SUBLANES = 8
VMEM_LIMIT = 56 * 1024 * 1024
HI = lax.Precision.HIGHEST


def _params(n_grid):
    return pltpu.CompilerParams(dimension_semantics=("arbitrary",) * n_grid, vmem_limit_bytes=VMEM_LIMIT)


def _mm(a, b):
    return jnp.dot(a.astype(BF16), b.astype(BF16), preferred_element_type=F32)


def _mm_nt(a, b):
    return lax.dot_general(a.astype(BF16), b.astype(BF16), (((1,), (1,)), ((), ())), preferred_element_type=F32)


def _mm_tn(a, b):
    return lax.dot_general(a.astype(BF16), b.astype(BF16), (((0,), (0,)), ((), ())), preferred_element_type=F32)


def _mm_hi(a, b):
    return jnp.dot(a, b, precision=HI, preferred_element_type=F32)


def _sigmoid(x):
    return 1.0 / (1.0 + jnp.exp(-x))


def _softplus(x):
    return jnp.maximum(x, 0.0) + jnp.log(1.0 + jnp.exp(-jnp.abs(x)))


def _neg_expm1(x):
    poly = -x * (1.0 + x * (0.5 + x * (1.0 / 6.0 + x * (1.0 / 24.0 + x * (1.0 / 120.0)))))
    return jnp.where(jnp.abs(x) < 0.05, poly, 1.0 - jnp.exp(x))


def _iota(shape, dim):
    return lax.broadcasted_iota(jnp.int32, shape, dim)


def _colsum(x):
    return jnp.sum(x, axis=0, keepdims=True)


def _rowsum(x):
    return jnp.sum(x, axis=1, keepdims=True)


def _col(x, j):
    return _rowsum(jnp.where(_iota(x.shape, 1) == j, x, 0.0))


def _row(x, j):
    return _colsum(jnp.where(_iota(x.shape, 0) == j, x, 0.0))


def _shift_down(x, j, prev8):
    if j == 0:
        return x
    t = x.shape[0]
    main = jnp.where(_iota(x.shape, 0) >= j, pltpu.roll(x, j, 0), 0.0)
    fix = jnp.where(_iota(prev8.shape, 0) < j, pltpu.roll(prev8, j, 0), 0.0)
    return jnp.concatenate([main[0:SUBLANES] + fix, main[SUBLANES:t]], axis=0)


def _shift_up(x, j, next8):
    if j == 0:
        return x
    t = x.shape[0]
    main = jnp.where(_iota(x.shape, 0) < t - j, pltpu.roll(x, t - j, 0), 0.0)
    fix = jnp.where(_iota(next8.shape, 0) >= SUBLANES - j, pltpu.roll(next8, SUBLANES - j, 0), 0.0)
    return jnp.concatenate([main[0:t - SUBLANES], main[t - SUBLANES:t] + fix], axis=0)


def _conv(x, prev8, w_ref):
    y = w_ref[CONV_WIDTH - 1:CONV_WIDTH, :] * x
    for j in range(1, CONV_WIDTH):
        y = y + w_ref[CONV_WIDTH - 1 - j:CONV_WIDTH - j, :] * _shift_down(x, j, prev8)
    return y


def _conv_bwd(dy, x, next8, w_ref, gw_ref):
    dx = None
    for j in range(CONV_WIDTH):
        k = CONV_WIDTH - 1 - j
        up = _shift_up(dy, j, next8)
        gw_ref[k:k + 1, :] += _colsum(up * x)
        term = w_ref[k:k + 1, :] * up
        dx = term if dx is None else dx + term
    return dx


def _scan_into(a, b, carry, out_ref, reverse):
    t, c = a.shape
    groups = t // SUBLANES
    a3 = a.reshape(groups, SUBLANES, c)
    b3 = b.reshape(groups, SUBLANES, c)
    sub = _iota(a3.shape, 1)
    for step in (1, 2, 4):
        keep = sub < SUBLANES - step if reverse else sub >= step
        shift = SUBLANES - step if reverse else step
        a_s = jnp.where(keep, pltpu.roll(a3, shift, 1), 1.0)
        b_s = jnp.where(keep, pltpu.roll(b3, shift, 1), 0.0)
        b3 = a3 * b_s + b3
        a3 = a3 * a_s
    for g in (reversed(range(groups)) if reverse else range(groups)):
        rows = slice(g * SUBLANES, (g + 1) * SUBLANES)
        out_ref[rows, :] = b3[g] + a3[g] * carry
        edge = g * SUBLANES if reverse else (g + 1) * SUBLANES - 1
        carry = out_ref[edge:edge + 1, :]


def _blockdiag(x, w_ref, transpose_w=False):
    nh, dh, _ = w_ref.shape
    outs = []
    for h in range(nh):
        xs = x[:, h * dh:(h + 1) * dh]
        outs.append(_mm_nt(xs, w_ref[h]) if transpose_w else _mm(xs, w_ref[h]))
    return jnp.concatenate(outs, axis=1)


def _rg_gates(xc, wa_ref, ba_ref, wx_ref, bx_ref, lam_ref):
    r = _sigmoid(_blockdiag(xc, wa_ref) + ba_ref[...])
    ig = _sigmoid(_blockdiag(xc, wx_ref) + bx_ref[...])
    sp = _softplus(-lam_ref[...])
    log_a = -RG_C * r * sp
    a = jnp.exp(log_a)
    beta = jnp.sqrt(_neg_expm1(2.0 * log_a))
    return r, ig, sp, a, beta


def _bcast8(row):
    return jnp.broadcast_to(row, (SUBLANES, row.shape[1]))


def _full(shape):
    nd = len(shape)
    return pl.BlockSpec(shape, lambda *_: (0,) * nd)


class _Comm:
    def __init__(self, arrays, out_shapes, sems, start, finish, aliases=()):
        self.arrays, self.out_shapes, self.sems = list(arrays), list(out_shapes), list(sems)
        self.start, self.finish, self.aliases = start, finish, tuple(aliases)
        self.results = None


class _RowOf:
    def __init__(self, ref, k):
        self.ref, self.k = ref, k

    def __getitem__(self, idx):
        cols = slice(None) if idx is Ellipsis else idx[1]
        return self.ref[0, self.k:self.k + 1, cols]


def _vec(table, layer, k):
    return ("row", table, layer, k)


def _is_row(arg):
    return isinstance(arg, tuple) and len(arg) == 4 and arg[0] == "row"


def _call(body, comms, *, name, grid, in_specs, out_specs, out_shape, args, scratch_shapes=(), aliases=None):
    comms = [cm for cm in (comms or []) if cm is not None]
    rows = {i: a[3] for i, a in enumerate(args) if _is_row(a)}
    in_specs = [pl.BlockSpec((1,) + a[1].shape[1:], functools.partial(lambda layer, *_: (layer, 0, 0), a[2]))
                if _is_row(a) else sp for a, sp in zip(args, in_specs)]
    args = tuple(a[1] if _is_row(a) else a for a in args)
    n_in, n_out, n_sc = len(args), len(out_shape), len(scratch_shapes)
    c_arrays = [a for cm in comms for a in cm.arrays]
    c_outs = [o for cm in comms for o in cm.out_shapes]
    c_sems = [sm for cm in comms for sm in cm.sems]
    aliases, a_at, o_at = dict(aliases or {}), n_in, n_out
    for cm in comms:
        for (i, j) in cm.aliases:
            aliases[a_at + i] = o_at + j
        a_at += len(cm.arrays)
        o_at += len(cm.out_shapes)

    def wrapped(*refs):
        ins, c_in = refs[:n_in], refs[n_in:n_in + len(c_arrays)]
        ins = [_RowOf(r, rows[i]) if i in rows else r for i, r in enumerate(ins)]
        at = n_in + len(c_arrays)
        outs, c_out = refs[at:at + n_out], refs[at + n_out:at + n_out + len(c_outs)]
        at += n_out + len(c_outs)
        scr, sems = refs[at:at + n_sc], refs[at + n_sc:]
        views, ia, io, isem = [], 0, 0, 0
        for cm in comms:
            views.append((c_in[ia:ia + len(cm.arrays)], c_out[io:io + len(cm.out_shapes)], sems[isem:isem + len(cm.sems)]))
            ia, io, isem = ia + len(cm.arrays), io + len(cm.out_shapes), isem + len(cm.sems)
        if comms:
            @pl.when(pl.program_id(0) == 0)
            def _():
                for cm, view in zip(comms, views):
                    cm.start(*view)

        body(*ins, *outs, *scr)
        if comms:
            @pl.when(pl.program_id(0) == grid[0] - 1)
            def _():
                for cm, view in zip(comms, views):
                    cm.finish(*view)

    hbm = pl.BlockSpec(memory_space=pl.ANY)
    res = pl.pallas_call(
        wrapped, name=name, grid=grid,
        in_specs=list(in_specs) + [hbm] * len(c_arrays), out_specs=list(out_specs) + [hbm] * len(c_outs),
        out_shape=list(out_shape) + c_outs, scratch_shapes=list(scratch_shapes) + c_sems,
        input_output_aliases=aliases, compiler_params=_params(len(grid)),
    )(*args, *c_arrays)
    at = n_out
    for cm in comms:
        cm.results = list(res[at:at + len(cm.out_shapes)])
        at += len(cm.out_shapes)
    return list(res[:n_out])


def _mod_call(c, w_ada_g, b_ada):
    nd, nl, d, w = w_ada_g.shape

    def body(c_ref, w_ref, b_ref, mod_ref, cact_ref):
        cv = c_ref[...]
        ca = _bcast8(cv * _sigmoid(cv))
        cact_ref[...] = ca
        mod_ref[0, 0] = _mm(ca, w_ref[0, 0]) + b_ref[0, 0]

    mod, cact = pl.pallas_call(
        body, name="adaln_mod", grid=(nl, nd),
        in_specs=[_full((1, d)),
                  pl.BlockSpec((1, 1, d, w), lambda l, j: (j, l, 0, 0)),
                  pl.BlockSpec((1, 1, 1, w), lambda l, j: (l, j, 0, 0))],
        out_specs=[pl.BlockSpec((1, 1, SUBLANES, w), lambda l, j: (l, j, 0, 0)), _full((SUBLANES, d))],
        out_shape=[jax.ShapeDtypeStruct((nl, nd, SUBLANES, w), F32), jax.ShapeDtypeStruct((SUBLANES, d), F32)],
        compiler_params=_params(2),
    )(c, w_ada_g, b_ada.reshape(nl, nd, 1, w))
    return mod[:, :, 0, :].reshape(nl, nd * w), cact


def _join_columns(w_ref, wcat):
    nd, _, _, w = w_ref.shape

    @pl.when(pl.program_id(0) == 0)
    def _():
        for j in range(nd):
            wcat[:, j * w:(j + 1) * w] = w_ref[j, 0]


def _in_fwd(x, ng, scale, shift, w_in_g, layer, tile, comms=None):
    s, d = x.shape
    nd, _, _, w = w_in_g.shape

    def body(x_ref, ng_ref, sc_ref, sh_ref, w_ref, u_ref, h_ref, wcat):
        _join_columns(w_ref, wcat)
        xv = x_ref[...]
        rs = lax.rsqrt(jnp.mean(xv * xv, axis=1, keepdims=True) + EPS)
        hb = (xv * rs * ng_ref[...] * (1.0 + sc_ref[...]) + sh_ref[...]).astype(BF16)
        h_ref[...] = hb
        u_ref[...] = jnp.dot(hb, wcat[...], preferred_element_type=F32)

    return _call(
        body, comms, name="in_proj_fwd", grid=(s // tile,),
        in_specs=[pl.BlockSpec((tile, d), lambda i: (i, 0)), _full((1, d)), _full((1, d)), _full((1, d)),
                  pl.BlockSpec((nd, 1, d, w), lambda i: (0, layer, 0, 0), pipeline_mode=pl.Buffered(1))],
        out_specs=[pl.BlockSpec((tile, nd * w), lambda i: (i, 0)), pl.BlockSpec((tile, d), lambda i: (i, 0))],
        out_shape=[jax.ShapeDtypeStruct((s, nd * w), F32), jax.ShapeDtypeStruct((s, d), BF16)],
        scratch_shapes=[pltpu.VMEM((d, nd * w), BF16)],
        args=(x, ng, scale, shift, w_in_g))


def _rg_fwd(u, d, conv_w, conv_b, w_a, b_a, w_x, b_x, lam, tile, comms=None):
    s = u.shape[0]

    def body(x_ref, z_ref, cw_ref, cb_ref, wa_ref, ba_ref, wx_ref, bx_ref, lam_ref,
             h_ref, y_ref, xc_ref, r_ref, i_ref, a_ref, beta_ref, prev8, hcar):
        @pl.when(pl.program_id(0) == 0)
        def _():
            prev8[...] = jnp.zeros_like(prev8)
            hcar[...] = jnp.zeros_like(hcar)

        x = x_ref[...]
        xc = _conv(x, prev8[...], cw_ref) + cb_ref[...]
        prev8[...] = x[tile - SUBLANES:tile, :]
        r, ig, _, a, beta = _rg_gates(xc, wa_ref, ba_ref, wx_ref, bx_ref, lam_ref)
        xc_ref[...] = xc
        r_ref[...] = r
        i_ref[...] = ig
        a_ref[...] = a
        beta_ref[...] = beta
        _scan_into(a, beta * ig * xc, hcar[SUBLANES - 1:SUBLANES, :], h_ref, False)
        h = h_ref[...]
        hcar[...] = h[tile - SUBLANES:tile, :]
        z = z_ref[...]
        y_ref[...] = (h * z * _sigmoid(z)).astype(BF16)

    vec = _full((1, d))
    return _call(
        body, comms, name="rglru_fwd", grid=(s // tile,),
        in_specs=[pl.BlockSpec((tile, d), lambda i: (i, 0)), pl.BlockSpec((tile, d), lambda i: (i, 1)),
                  _full(conv_w.shape), vec, _full(w_a.shape), vec, _full(w_x.shape), vec, vec],
        out_specs=[pl.BlockSpec((tile, d), lambda i: (i, 0))] * 7,
        out_shape=[jax.ShapeDtypeStruct((s, d), F32), jax.ShapeDtypeStruct((s, d), BF16)] + [jax.ShapeDtypeStruct((s, d), F32)] * 5,
        scratch_shapes=[pltpu.VMEM((SUBLANES, d), F32), pltpu.VMEM((SUBLANES, d), F32)],
        args=(u, u, conv_w, conv_b, w_a, b_a, w_x, b_x, lam))


def _ml_pre(u, d, conv_w, conv_b, w_q, w_k, w_v, wif, bif, tile, comms=None):
    s = u.shape[0]
    nh = w_q.shape[0]

    def body(x_ref, cw_ref, cb_ref, wq_ref, wk_ref, wv_ref, wif_ref, bif_ref, q_ref, k_ref, v_ref, g_ref, pre_ref, prev8):
        @pl.when(pl.program_id(0) == 0)
        def _():
            prev8[...] = jnp.zeros_like(prev8)

        x = x_ref[...]
        pre = _conv(x, prev8[...], cw_ref) + cb_ref[...]
        prev8[...] = x[tile - SUBLANES:tile, :]
        xc = pre * _sigmoid(pre)
        q = _blockdiag(xc, wq_ref)
        k = _blockdiag(xc, wk_ref)
        v = _blockdiag(x, wv_ref)
        pre_ref[...] = pre
        q_ref[...] = q
        k_ref[...] = k
        v_ref[...] = v
        g = _mm(q, wif_ref[0:d, :]) + _mm(k, wif_ref[d:2 * d, :]) + _mm(v, wif_ref[2 * d:3 * d, :]) + bif_ref[...]
        lane = _iota(g.shape, 1)
        gl = jnp.where(lane < 4, g, jnp.where(lane < 8, -_softplus(-g), 0.0))
        tri = jnp.where(_iota((ML_CHUNK, ML_CHUNK), 1) <= _iota((ML_CHUNK, ML_CHUNK), 0), 1.0, 0.0)
        cums = [_mm_hi(tri, gl[c * ML_CHUNK:(c + 1) * ML_CHUNK, :]) for c in range(tile // ML_CHUNK)]
        cum = cums[0] if len(cums) == 1 else jnp.concatenate(cums, axis=0)
        g_ref[...] = gl + jnp.where((lane >= 8) & (lane < 12), pltpu.roll(cum, 4, 1), 0.0)

    vec = _full((1, d))
    return _call(
        body, comms, name="mlstm_proj_fwd", grid=(s // tile,),
        in_specs=[pl.BlockSpec((tile, d), lambda i: (i, 2)), _full(conv_w.shape), vec,
                  _full(w_q.shape), _full(w_k.shape), _full(w_v.shape), _full(wif.shape), _full((1, LANES))],
        out_specs=[pl.BlockSpec((tile, d), lambda i: (i, 0))] * 3 + [pl.BlockSpec((tile, LANES), lambda i: (i, 0)),
                                                                     pl.BlockSpec((tile, d), lambda i: (i, 0))],
        out_shape=[jax.ShapeDtypeStruct((s, d), F32)] * 3 + [jax.ShapeDtypeStruct((s, LANES), F32),
                                                             jax.ShapeDtypeStruct((s, d), F32)],
        scratch_shapes=[pltpu.VMEM((SUBLANES, d), F32)],
        args=(u, conv_w, conv_b, w_q, w_k, w_v, wif, bif))


def _cell_chunk(h, nh, q_ref, k_ref, v_ref, gc, gr, m_prev, c_h, n_h, m_t=None):
    lc = ML_CHUNK
    dh = q_ref.shape[1] // nh
    sl = slice(h * dh, (h + 1) * dh)
    qh = q_ref[:, sl]
    kh = k_ref[:, sl] * (dh ** -0.5)
    vh = v_ref[:, sl]
    li_c = _col(gc, h)
    b_c = _col(gc, 8 + h)
    lib_r = _row(gr, h) - _row(gr, 8 + h)
    b_last = _colsum(jnp.where(_iota((lc, 1), 0) == lc - 1, b_c, 0.0))
    causal = _iota((lc, lc), 1) <= _iota((lc, lc), 0)
    dmat = jnp.where(causal, b_c + lib_r, NEG_BIG)
    m_inter = b_c + m_prev
    if m_t is None:
        m_t = jnp.maximum(m_inter, jnp.max(dmat, axis=1, keepdims=True))
    w_intra = jnp.exp(dmat - m_t)
    w_inter = jnp.exp(m_inter - m_t)
    amat = _mm_nt(qh, kh)
    smat = amat * w_intra
    qc = _mm(qh, c_h)
    qn = _rowsum(qh * n_h)
    den = _rowsum(smat) + w_inter * qn
    gst = b_last - b_c + li_c
    m_new = jnp.maximum(b_last + m_prev, jnp.max(gst, axis=0, keepdims=True))
    w_state = jnp.exp(gst - m_new)
    decay = jnp.exp(b_last + m_prev - m_new)
    return dict(sl=sl, qh=qh, kh=kh, vh=vh, m_t=m_t, w_intra=w_intra, w_inter=w_inter, smat=smat, qc=qc, qn=qn,
                den=den, m_new=m_new, w_state=w_state, decay=decay)


def _ml_cell_fwd(q, k, v, gcol, grow, u, ng, nh, comms=None):
    s, d = q.shape
    lc = ML_CHUNK
    nc = s // lc
    dh = d // nh

    def body(q_ref, k_ref, v_ref, gc_ref, gr_ref, o_ref, z_ref, ng_ref,
             cell_ref, y_ref, cs_ref, ns_ref, ms_ref, mt_ref, c_sc, n_sc, m_sc):
        @pl.when(pl.program_id(0) == 0)
        def _():
            c_sc[...] = jnp.zeros_like(c_sc)
            n_sc[...] = jnp.zeros_like(n_sc)
            m_sc[...] = jnp.zeros_like(m_sc)

        gc = gc_ref[...]
        gr = gr_ref[...]
        lane = _iota((lc, LANES), 1)
        mt_acc = jnp.zeros((lc, LANES), F32)
        for h in range(nh):
            c_h = c_sc[h]
            n_h = n_sc[h, 0:1, :]
            m_prev = jnp.max(m_sc[h, 0:1, :], axis=1, keepdims=True)
            cs_ref[0, h] = c_h
            ns_ref[0, h] = n_sc[h]
            ms_ref[0, h] = m_sc[h]
            t = _cell_chunk(h, nh, q_ref, k_ref, v_ref, gc, gr, m_prev, c_h, n_h)
            sl = t["sl"]
            num = _mm(t["smat"], t["vh"]) + t["w_inter"] * t["qc"]
            cell_h = num / jnp.maximum(jnp.abs(t["den"]), jnp.exp(-t["m_t"]))
            mt_acc = jnp.where(lane == h, t["m_t"], mt_acc)
            kw = t["kh"] * t["w_state"]
            c_sc[h] = t["decay"] * c_h + _mm_tn(kw, t["vh"])
            n_sc[h] = _bcast8(t["decay"] * n_h + _colsum(kw))
            m_sc[h] = jnp.broadcast_to(t["m_new"], (SUBLANES, LANES))
            hg = _sigmoid(o_ref[:, sl]) * cell_h
            hn = hg * lax.rsqrt(jnp.mean(hg * hg, axis=1, keepdims=True) + EPS)
            z = z_ref[:, sl]
            cell_ref[:, sl] = cell_h
            y_ref[:, sl] = (hn * ng_ref[:, sl] * z * _sigmoid(z)).astype(BF16)
        mt_ref[...] = mt_acc

    tok = pl.BlockSpec((lc, d), lambda c: (c, 0))
    return _call(
        body, comms, name="mlstm_cell_fwd", grid=(nc,),
        in_specs=[tok, tok, tok, pl.BlockSpec((lc, LANES), lambda c: (c, 0)), pl.BlockSpec((16, lc), lambda c: (0, c)),
                  pl.BlockSpec((lc, d), lambda c: (c, 3)), pl.BlockSpec((lc, d), lambda c: (c, 4)), _full((1, d))],
        out_specs=[tok, tok, pl.BlockSpec((1, nh, dh, dh), lambda c: (c, 0, 0, 0)),
                   pl.BlockSpec((1, nh, SUBLANES, dh), lambda c: (c, 0, 0, 0)),
                   pl.BlockSpec((1, nh, SUBLANES, LANES), lambda c: (c, 0, 0, 0)),
                   pl.BlockSpec((lc, LANES), lambda c: (c, 0))],
        out_shape=[jax.ShapeDtypeStruct((s, d), F32), jax.ShapeDtypeStruct((s, d), BF16),
                   jax.ShapeDtypeStruct((nc, nh, dh, dh), F32), jax.ShapeDtypeStruct((nc, nh, SUBLANES, dh), F32),
                   jax.ShapeDtypeStruct((nc, nh, SUBLANES, LANES), F32), jax.ShapeDtypeStruct((s, LANES), F32)],
        scratch_shapes=[pltpu.VMEM((nh, dh, dh), F32), pltpu.VMEM((nh, SUBLANES, dh), F32),
                        pltpu.VMEM((nh, SUBLANES, LANES), F32)],
        args=(q, k, v, gcol, grow, u, u, ng))


def _out_fwd(x, y_rg, y_ml, gate, w_out_g, layer, tile, comms=None):
    s, d = x.shape
    nd, _, r, _ = w_out_g.shape

    def body(x_ref, yr_ref, ym_ref, g_ref, w_ref, xn_ref, y_ref):
        ycat = jnp.concatenate([yr_ref[...].astype(BF16), ym_ref[...].astype(BF16)], axis=1)
        acc = jnp.dot(ycat, w_ref[...].reshape(nd * r, d), preferred_element_type=F32)
        y_ref[...] = acc
        xn_ref[...] = x_ref[...] + g_ref[...] * acc

    tok = pl.BlockSpec((tile, d), lambda i: (i, 0))
    return _call(
        body, comms, name="out_proj_fwd", grid=(s // tile,),
        in_specs=[tok, tok, tok, _full((1, d)), pl.BlockSpec((nd, 1, r, d), lambda i: (0, layer, 0, 0))],
        out_specs=[tok, tok],
        out_shape=[jax.ShapeDtypeStruct((s, d), F32)] * 2,
        args=(x, y_rg, y_ml, gate, w_out_g))


def _loss_call(x, fg, target, tile):
    s, d = x.shape

    def body(x_ref, g_ref, t_ref, dx_ref, loss_ref, gg_ref):
        @pl.when(pl.program_id(0) == 0)
        def _():
            loss_ref[...] = jnp.zeros_like(loss_ref)
            gg_ref[...] = jnp.zeros_like(gg_ref)

        xv = x_ref[...]
        g = g_ref[...]
        rs = lax.rsqrt(jnp.mean(xv * xv, axis=1, keepdims=True) + EPS)
        xh = xv * rs
        e = xh * g - t_ref[...]
        loss_ref[...] += jnp.broadcast_to(_colsum(_rowsum(e * e)) * (0.5 / d), loss_ref.shape)
        dy = e * (1.0 / d)
        gg_ref[...] += _bcast8(_colsum(dy * xh))
        dxh = dy * g
        dx_ref[...] = rs * (dxh - xh * jnp.mean(dxh * xh, axis=1, keepdims=True))

    tok = pl.BlockSpec((tile, d), lambda i: (i, 0))
    return pl.pallas_call(
        body, name="final_norm_loss", grid=(s // tile,),
        in_specs=[tok, _full((1, d)), tok],
        out_specs=[tok, _full((SUBLANES, LANES)), _full((SUBLANES, d))],
        out_shape=[jax.ShapeDtypeStruct((s, d), F32), jax.ShapeDtypeStruct((SUBLANES, LANES), F32),
                   jax.ShapeDtypeStruct((SUBLANES, d), F32)],
        compiler_params=_params(1),
    )(x, fg, target)


def _ml_out_stage_bwd(dy, cell, o, z, ng):
    so = _sigmoid(o)
    hg = so * cell
    rinv = lax.rsqrt(jnp.mean(hg * hg, axis=1, keepdims=True) + EPS)
    hn = hg * rinv
    sz = _sigmoid(z)
    dz = dy * hn * ng * (sz + z * sz * (1.0 - sz))
    dymid = dy * z * sz
    dhn = dymid * ng
    dhg = rinv * (dhn - hn * jnp.mean(dhn * hn, axis=1, keepdims=True))
    return dz, dhg * cell * so * (1.0 - so), dhg * so, _colsum(dymid * hn)


def _out_bwd_with_output_stage(dxo, gate, y, y_rg, y_ml, cell, u, ng, w_out_g, layer, tile, nh, comms=None):
    s, d = dxo.shape
    nd, _, r, _ = w_out_g.shape
    dh = d // nh

    def body(dx_ref, g_ref, y_ref, yr_ref, ym_ref, cell_ref, o_ref, z_ref, ng_ref, w_ref,
             dyr_ref, dcell_ref, do_ref, dz_ref, gw_ref, dg_ref, gng_ref):
        @pl.when(pl.program_id(0) == 0)
        def _():
            for ref in (gw_ref, dg_ref, gng_ref):
                ref[...] = jnp.zeros_like(ref)

        dxv = dx_ref[...]
        dg_ref[...] += _bcast8(_colsum(dxv * y_ref[...]))
        dyb = (dxv * g_ref[...]).astype(BF16)
        dycat = lax.dot_general(dyb, w_ref[...].reshape(nd * r, d), (((1,), (1,)), ((), ())), preferred_element_type=F32)
        dyr_ref[...] = dycat[:, 0:d]
        ycat = jnp.concatenate([yr_ref[...].astype(BF16), ym_ref[...].astype(BF16)], axis=1)
        gw_ref[...] += lax.dot_general(ycat, dyb, (((0,), (0,)), ((), ())), preferred_element_type=F32).reshape(nd, r, d)
        for h in range(nh):
            sl = slice(h * dh, (h + 1) * dh)
            dz, do, dcell, gng = _ml_out_stage_bwd(dycat[:, d + h * dh:d + (h + 1) * dh], cell_ref[:, sl], o_ref[:, sl],
                                                   z_ref[:, sl], ng_ref[:, sl])
            dz_ref[:, sl] = dz.astype(BF16)
            do_ref[:, sl] = do.astype(BF16)
            dcell_ref[:, sl] = dcell
            gng_ref[:, sl] += _bcast8(gng)

    tok = pl.BlockSpec((tile, d), lambda i: (i, 0))
    acc = _full((SUBLANES, d))
    return _call(
        body, comms, name="out_proj_bwd", grid=(s // tile,),
        in_specs=[tok, _full((1, d)), tok, tok, tok, tok, pl.BlockSpec((tile, d), lambda i: (i, 3)),
                  pl.BlockSpec((tile, d), lambda i: (i, 4)), _full((1, d)),
                  pl.BlockSpec((nd, 1, r, d), lambda i: (0, layer, 0, 0), pipeline_mode=pl.Buffered(1))],
        out_specs=[tok, tok, tok, tok, pl.BlockSpec((nd, r, d), lambda i: (0, 0, 0), pipeline_mode=pl.Buffered(1)), acc, acc],
        out_shape=[jax.ShapeDtypeStruct((s, d), F32)] * 2 + [jax.ShapeDtypeStruct((s, d), BF16)] * 2
        + [jax.ShapeDtypeStruct((nd, r, d), F32)] + [jax.ShapeDtypeStruct((SUBLANES, d), F32)] * 2,
        args=(dxo, gate, y, y_rg, y_ml, cell, u, u, ng, w_out_g))


def _ml_cell_only_bwd(dcell, cell, q, k, v, gcol, grow, mt, cs, ns, ms, nh, comms=None):
    s, d = q.shape
    lc = ML_CHUNK
    nc = s // lc
    dh = d // nh

    def body(dcell_ref, cell_ref, q_ref, k_ref, v_ref, gc_ref, gr_ref, mt_ref, cs_ref, ns_ref, ms_ref,
             dq_ref, dk_ref, dv_ref, dg_ref, dc_sc, dn_sc):
        @pl.when(pl.program_id(0) == 0)
        def _():
            dc_sc[...] = jnp.zeros_like(dc_sc)
            dn_sc[...] = jnp.zeros_like(dn_sc)

        gc = gc_ref[...]
        gr = gr_ref[...]
        mtv = mt_ref[...]
        lane = _iota((lc, LANES), 1)
        rowv = _iota((lc, 1), 0)
        dg_acc = jnp.zeros((lc, LANES), F32)
        for h in range(nh):
            c_h = cs_ref[0, h]
            n_h = ns_ref[0, h, 0:1, :]
            m_prev = jnp.max(ms_ref[0, h, 0:1, :], axis=1, keepdims=True)
            t = _cell_chunk(h, nh, q_ref, k_ref, v_ref, gc, gr, m_prev, c_h, n_h, m_t=_col(mtv, h))
            sl, qh, kh, vh = t["sl"], t["qh"], t["kh"], t["vh"]
            w_intra, w_inter, smat, w_state, decay = t["w_intra"], t["w_inter"], t["smat"], t["w_state"], t["decay"]
            cell_h = cell_ref[:, sl]
            dcell = dcell_ref[:, sl]
            eneg = jnp.exp(-t["m_t"])
            aden = jnp.abs(t["den"])
            nst = jnp.maximum(aden, eneg)
            dnum = dcell / nst
            dden = jnp.where(aden > eneg, -_rowsum(cell_h * dcell) / nst * jnp.sign(t["den"]), 0.0)
            pmat = _mm_nt(dnum, vh) + dden
            damat = pmat * w_intra
            gmat = pmat * smat
            wdn = w_inter * dnum
            wdd = w_inter * dden
            dqh = _mm(damat, kh) + _mm_nt(wdn, c_h) + wdd * n_h
            dkh = _mm_tn(damat, qh)
            dvh = _mm_tn(smat, dnum)
            dw_inter = _rowsum(dnum * t["qc"]) + dden * t["qn"]
            dcn = dc_sc[h]
            dnn = dn_sc[h, 0:1, :]
            kw = kh * w_state
            dkw = _mm_nt(vh, dcn) + dnn
            dvh = dvh + _mm(kw, dcn)
            dkh = dkh + dkw * w_state
            dgst = _rowsum(dkw * kh) * w_state
            ddecay = _colsum(_rowsum(dcn * c_h)) + _rowsum(dnn * n_h)
            db_last = _colsum(dgst) + ddecay * decay
            rs_g = _rowsum(gmat)
            cs_g = _rowsum(gmat.T)
            db = rs_g - cs_g + dw_inter * w_inter - dgst + jnp.where(rowv == lc - 1, db_last, 0.0)
            dli = cs_g + dgst
            dc_sc[h] = decay * dcn + _mm_tn(qh, wdn)
            dn_sc[h] = _bcast8(decay * dnn + _colsum(qh * wdd))
            dq_ref[:, sl] = dqh
            dk_ref[:, sl] = dkh * (dh ** -0.5)
            dv_ref[:, sl] = dvh
            dg_acc = jnp.where(lane == h, dli, jnp.where(lane == 4 + h, db, dg_acc))
        dg_ref[...] = dg_acc

    rev = lambda c: nc - 1 - c
    tok = pl.BlockSpec((lc, d), lambda c: (rev(c), 0))
    g128 = pl.BlockSpec((lc, LANES), lambda c: (rev(c), 0))
    return _call(
        body, comms, name="mlstm_cell_bwd", grid=(nc,),
        in_specs=[tok, tok, tok, tok, tok, g128, pl.BlockSpec((16, lc), lambda c: (0, rev(c))), g128,
                  pl.BlockSpec((1, nh, dh, dh), lambda c: (rev(c), 0, 0, 0)),
                  pl.BlockSpec((1, nh, SUBLANES, dh), lambda c: (rev(c), 0, 0, 0)),
                  pl.BlockSpec((1, nh, SUBLANES, LANES), lambda c: (rev(c), 0, 0, 0))],
        out_specs=[tok, tok, tok, g128],
        out_shape=[jax.ShapeDtypeStruct((s, d), F32)] * 3 + [jax.ShapeDtypeStruct((s, LANES), F32)],
        scratch_shapes=[pltpu.VMEM((nh, dh, dh), F32), pltpu.VMEM((nh, SUBLANES, dh), F32)],
        args=(dcell, cell, q, k, v, gcol, grow, mt, cs, ns, ms))


def _out_bwd(dxo, gate, y, y_rg, y_ml, w_out_g, layer, tile, comms=None):
    s, d = dxo.shape
    nd, _, r, _ = w_out_g.shape

    def body(dx_ref, g_ref, y_ref, yr_ref, ym_ref, w_ref, dyr_ref, dym_ref, gw_ref, dg_ref):
        @pl.when(pl.program_id(0) == 0)
        def _():
            gw_ref[...] = jnp.zeros_like(gw_ref)
            dg_ref[...] = jnp.zeros_like(dg_ref)

        dxv = dx_ref[...]
        dg_ref[...] += _bcast8(_colsum(dxv * y_ref[...]))
        dyb = (dxv * g_ref[...]).astype(BF16)
        dycat = lax.dot_general(dyb, w_ref[...].reshape(nd * r, d), (((1,), (1,)), ((), ())), preferred_element_type=F32)
        dyr_ref[...] = dycat[:, 0:d]
        dym_ref[...] = dycat[:, d:2 * d]
        ycat = jnp.concatenate([yr_ref[...].astype(BF16), ym_ref[...].astype(BF16)], axis=1)
        gw_ref[...] += lax.dot_general(ycat, dyb, (((0,), (0,)), ((), ())), preferred_element_type=F32).reshape(nd, r, d)

    tok = pl.BlockSpec((tile, d), lambda i: (i, 0))
    return _call(
        body, comms, name="out_proj_bwd", grid=(s // tile,),
        in_specs=[tok, _full((1, d)), tok, tok, tok, pl.BlockSpec((nd, 1, r, d), lambda i: (0, layer, 0, 0))],
        out_specs=[tok, tok, _full((nd, r, d)), _full((SUBLANES, d))],
        out_shape=[jax.ShapeDtypeStruct((s, d), F32)] * 2 + [jax.ShapeDtypeStruct((nd, r, d), F32),
                                                             jax.ShapeDtypeStruct((SUBLANES, d), F32)],
        args=(dxo, gate, y, y_rg, y_ml, w_out_g))


def _ml_cell_bwd(dy_ml, u, cell, q, k, v, gcol, grow, mt, cs, ns, ms, ng, nh, comms=None):
    s, d = q.shape
    lc = ML_CHUNK
    nc = s // lc
    dh = d // nh

    def body(dy_ref, o_ref, z_ref, cell_ref, q_ref, k_ref, v_ref, gc_ref, gr_ref, mt_ref, cs_ref, ns_ref, ms_ref,
             ng_ref, dq_ref, dk_ref, dv_ref, dg_ref, do_ref, dz_ref, gng_ref, dc_sc, dn_sc):
        @pl.when(pl.program_id(0) == 0)
        def _():
            dc_sc[...] = jnp.zeros_like(dc_sc)
            dn_sc[...] = jnp.zeros_like(dn_sc)
            gng_ref[...] = jnp.zeros_like(gng_ref)

        gc = gc_ref[...]
        gr = gr_ref[...]
        mtv = mt_ref[...]
        lane = _iota((lc, LANES), 1)
        rowv = _iota((lc, 1), 0)
        dg_acc = jnp.zeros((lc, LANES), F32)
        for h in range(nh):
            c_h = cs_ref[0, h]
            n_h = ns_ref[0, h, 0:1, :]
            m_prev = jnp.max(ms_ref[0, h, 0:1, :], axis=1, keepdims=True)
            t = _cell_chunk(h, nh, q_ref, k_ref, v_ref, gc, gr, m_prev, c_h, n_h, m_t=_col(mtv, h))
            sl, qh, kh, vh = t["sl"], t["qh"], t["kh"], t["vh"]
            w_intra, w_inter, smat, w_state, decay = t["w_intra"], t["w_inter"], t["smat"], t["w_state"], t["decay"]
            cell_h = cell_ref[:, sl]
            dz, do, dcell, gng = _ml_out_stage_bwd(dy_ref[:, sl], cell_h, o_ref[:, sl], z_ref[:, sl], ng_ref[:, sl])
            dz_ref[:, sl] = dz.astype(BF16)
            do_ref[:, sl] = do.astype(BF16)
            gng_ref[:, sl] += _bcast8(gng)
            eneg = jnp.exp(-t["m_t"])
            aden = jnp.abs(t["den"])
            nst = jnp.maximum(aden, eneg)
            dnum = dcell / nst
            dden = jnp.where(aden > eneg, -_rowsum(cell_h * dcell) / nst * jnp.sign(t["den"]), 0.0)
            pmat = _mm_nt(dnum, vh) + dden
            damat = pmat * w_intra
            gmat = pmat * smat
            wdn = w_inter * dnum
            wdd = w_inter * dden
            dqh = _mm(damat, kh) + _mm_nt(wdn, c_h) + wdd * n_h
            dkh = _mm_tn(damat, qh)
            dvh = _mm_tn(smat, dnum)
            dw_inter = _rowsum(dnum * t["qc"]) + dden * t["qn"]
            dcn = dc_sc[h]
            dnn = dn_sc[h, 0:1, :]
            kw = kh * w_state
            dkw = _mm_nt(vh, dcn) + dnn
            dvh = dvh + _mm(kw, dcn)
            dkh = dkh + dkw * w_state
            dgst = _rowsum(dkw * kh) * w_state
            ddecay = _colsum(_rowsum(dcn * c_h)) + _rowsum(dnn * n_h)
            db_last = _colsum(dgst) + ddecay * decay
            rs_g = _rowsum(gmat)
            cs_g = _rowsum(gmat.T)
            db = rs_g - cs_g + dw_inter * w_inter - dgst + jnp.where(rowv == lc - 1, db_last, 0.0)
            dli = cs_g + dgst
            dc_sc[h] = decay * dcn + _mm_tn(qh, wdn)
            dn_sc[h] = _bcast8(decay * dnn + _colsum(qh * wdd))
            dq_ref[:, sl] = dqh
            dk_ref[:, sl] = dkh * (dh ** -0.5)
            dv_ref[:, sl] = dvh
            dg_acc = jnp.where(lane == h, dli, jnp.where(lane == 4 + h, db, dg_acc))
        dg_ref[...] = dg_acc

    rev = lambda c: nc - 1 - c
    tok = pl.BlockSpec((lc, d), lambda c: (rev(c), 0))
    g128 = pl.BlockSpec((lc, LANES), lambda c: (rev(c), 0))
    return _call(
        body, comms, name="mlstm_cell_bwd", grid=(nc,),
        in_specs=[tok, pl.BlockSpec((lc, d), lambda c: (rev(c), 3)), pl.BlockSpec((lc, d), lambda c: (rev(c), 4)),
                  tok, tok, tok, tok, g128, pl.BlockSpec((16, lc), lambda c: (0, rev(c))), g128,
                  pl.BlockSpec((1, nh, dh, dh), lambda c: (rev(c), 0, 0, 0)),
                  pl.BlockSpec((1, nh, SUBLANES, dh), lambda c: (rev(c), 0, 0, 0)),
                  pl.BlockSpec((1, nh, SUBLANES, LANES), lambda c: (rev(c), 0, 0, 0)), _full((1, d))],
        out_specs=[tok, tok, tok, g128, tok, tok, _full((SUBLANES, d))],
        out_shape=[jax.ShapeDtypeStruct((s, d), F32)] * 3 + [jax.ShapeDtypeStruct((s, LANES), F32)]
        + [jax.ShapeDtypeStruct((s, d), BF16)] * 2 + [jax.ShapeDtypeStruct((SUBLANES, d), F32)],
        scratch_shapes=[pltpu.VMEM((nh, dh, dh), F32), pltpu.VMEM((nh, SUBLANES, dh), F32)],
        args=(dy_ml, u, u, cell, q, k, v, gcol, grow, mt, cs, ns, ms, ng))


def _halo_spec(d, tile, nt, col):
    per = tile // SUBLANES
    return pl.BlockSpec((SUBLANES, d), lambda i: (jnp.maximum((nt - 1 - i) * per - 1, 0), col))


def _ml_pre_bwd(dq, dk, dv, dgates, gcol, u, pre, q, k, v, conv_w, w_q, w_k, w_v, wif_t, tile, comms=None):
    s, d = dq.shape
    nt = s // tile
    nh, dh, _ = w_q.shape

    def body(dq_ref, dk_ref, dv_ref, dg_ref, gc_ref, x_ref, pre_ref, q_ref, k_ref, v_ref, cw_ref,
             wq_ref, wk_ref, wv_ref, wift_ref,
             dx_ref, gwq_ref, gwk_ref, gwv_ref, gwif_ref, gbif_ref, gcw_ref, gcb_ref, next8):
        @pl.when(pl.program_id(0) == 0)
        def _():
            next8[...] = jnp.zeros_like(next8)
            for ref in (gwq_ref, gwk_ref, gwv_ref, gwif_ref, gbif_ref, gcw_ref, gcb_ref):
                ref[...] = jnp.zeros_like(ref)

        x = x_ref[...]
        pre = pre_ref[...]
        sg = _sigmoid(pre)
        xc = pre * sg
        dgc = dg_ref[...]
        lane = _iota(dgc.shape, 1)
        utri = jnp.where(_iota((ML_CHUNK, ML_CHUNK), 0) <= _iota((ML_CHUNK, ML_CHUNK), 1), 1.0, 0.0)
        rcs = [_mm_hi(utri, dgc[c * ML_CHUNK:(c + 1) * ML_CHUNK, :]) for c in range(tile // ML_CHUNK)]
        rc = rcs[0] if len(rcs) == 1 else jnp.concatenate(rcs, axis=0)
        dgates_v = jnp.where(lane < 4, dgc, jnp.where(lane < 8, rc * (1.0 - jnp.exp(gc_ref[...])), 0.0))
        dgb = dgates_v.astype(BF16)
        gbif_ref[...] += jnp.broadcast_to(_colsum(dgates_v), gbif_ref.shape)
        ext = jnp.dot(dgb, wift_ref[...], preferred_element_type=F32)
        dqt = dq_ref[...] + ext[:, 0:d]
        dkt = dk_ref[...] + ext[:, d:2 * d]
        dvt = dv_ref[...] + ext[:, 2 * d:3 * d]
        gwif_ref[:, 0:d] += _mm_tn(dgb, q_ref[...])
        gwif_ref[:, d:2 * d] += _mm_tn(dgb, k_ref[...])
        gwif_ref[:, 2 * d:3 * d] += _mm_tn(dgb, v_ref[...])
        dxc_parts, dxv_parts = [], []
        for h in range(nh):
            sl = slice(h * dh, (h + 1) * dh)
            gwq_ref[h] += _mm_tn(xc[:, sl], dqt[:, sl])
            gwk_ref[h] += _mm_tn(xc[:, sl], dkt[:, sl])
            gwv_ref[h] += _mm_tn(x[:, sl], dvt[:, sl])
            dxc_parts.append(_mm_nt(dqt[:, sl], wq_ref[h]) + _mm_nt(dkt[:, sl], wk_ref[h]))
            dxv_parts.append(_mm_nt(dvt[:, sl], wv_ref[h]))
        dxc = jnp.concatenate(dxc_parts, axis=1)
        dxv = jnp.concatenate(dxv_parts, axis=1)
        dpre = dxc * (sg + pre * sg * (1.0 - sg))
        gcb_ref[...] += _bcast8(_colsum(dpre))
        dx_ref[...] = (dxv + _conv_bwd(dpre, x, next8[...], cw_ref, gcw_ref)).astype(BF16)
        next8[...] = dpre[0:SUBLANES, :]

    rev = lambda i: nt - 1 - i
    tok = pl.BlockSpec((tile, d), lambda i: (rev(i), 0))
    g128 = pl.BlockSpec((tile, LANES), lambda i: (rev(i), 0))
    wsh = (nh, dh, dh)
    return _call(
        body, comms, name="mlstm_proj_bwd", grid=(nt,),
        in_specs=[tok, tok, tok, g128, g128, pl.BlockSpec((tile, d), lambda i: (rev(i), 2)), tok,
                  tok, tok, tok, _full(conv_w.shape), _full(wsh), _full(wsh), _full(wsh), _full(wif_t.shape)],
        out_specs=[tok, _full(wsh), _full(wsh), _full(wsh), _full((LANES, 3 * d)), _full((SUBLANES, LANES)),
                   _full((SUBLANES, d)), _full((SUBLANES, d))],
        out_shape=[jax.ShapeDtypeStruct((s, d), BF16)] + [jax.ShapeDtypeStruct(wsh, F32)] * 3
        + [jax.ShapeDtypeStruct((LANES, 3 * d), F32), jax.ShapeDtypeStruct((SUBLANES, LANES), F32),
           jax.ShapeDtypeStruct((SUBLANES, d), F32), jax.ShapeDtypeStruct((SUBLANES, d), F32)],
        scratch_shapes=[pltpu.VMEM((SUBLANES, d), F32)],
        args=(dq, dk, dv, dgates, gcol, u, pre, q, k, v, conv_w, w_q, w_k, w_v, wif_t))


def _rg_bwd(dy_rg, u, h_rg, gates, conv_w, w_a, w_x, lam, tile, comms=None):
    s, d = dy_rg.shape
    nt = s // tile
    nh, dh, _ = w_a.shape

    def body(dy_ref, x_ref, z_ref, h_ref, hhalo_ref, xc_ref, r_ref, i_ref, a_ref, beta_ref, cw_ref, wa_ref,
             wx_ref, lam_ref,
             dx_ref, dz_ref, gwa_ref, gwx_ref, gba_ref, gbx_ref, glam_ref, gcw_ref, gcb_ref, next8, anext, dnext, dbuf):
        i = pl.program_id(0)

        @pl.when(i == 0)
        def _():
            for ref in (next8, anext, dnext, gwa_ref, gwx_ref, gba_ref, gbx_ref, glam_ref, gcw_ref, gcb_ref):
                ref[...] = jnp.zeros_like(ref)

        inner = jnp.where(i < nt - 1, 1.0, 0.0)
        xc, r, ig, a, beta = xc_ref[...], r_ref[...], i_ref[...], a_ref[...], beta_ref[...]
        sp = _softplus(-lam_ref[...])
        h = h_ref[...]
        row = _iota(h.shape, 0)
        hprev = jnp.where(row >= 1, pltpu.roll(h, 1, 0), hhalo_ref[SUBLANES - 1:SUBLANES, :] * inner)
        z = z_ref[...]
        sz = _sigmoid(z)
        dyv = dy_ref[...]
        dz_ref[...] = (dyv * h * (sz + z * sz * (1.0 - sz))).astype(BF16)
        a_up = jnp.where(row < tile - 1, pltpu.roll(a, tile - 1, 0), anext[0:1, :])
        _scan_into(a_up, dyv * z * sz, dnext[0:1, :], dbuf, True)
        delta = dbuf[...]
        anext[...] = a[0:SUBLANES, :]
        dnext[...] = delta[0:SUBLANES, :]
        dla = delta * hprev * a - delta * ig * xc * (a * a / beta)
        glam_ref[...] += _bcast8(_colsum(dla * r) * (RG_C * _sigmoid(-lam_ref[...])))
        dpa = dla * (-RG_C * sp) * r * (1.0 - r)
        dpx = delta * beta * xc * ig * (1.0 - ig)
        gba_ref[...] += _bcast8(_colsum(dpa))
        gbx_ref[...] += _bcast8(_colsum(dpx))
        parts = []
        for hh in range(nh):
            sl = slice(hh * dh, (hh + 1) * dh)
            gwa_ref[hh] += _mm_tn(xc[:, sl], dpa[:, sl])
            gwx_ref[hh] += _mm_tn(xc[:, sl], dpx[:, sl])
            parts.append(_mm_nt(dpa[:, sl], wa_ref[hh]) + _mm_nt(dpx[:, sl], wx_ref[hh]))
        dxc = delta * beta * ig + jnp.concatenate(parts, axis=1)
        gcb_ref[...] += _bcast8(_colsum(dxc))
        dx_ref[...] = _conv_bwd(dxc, x_ref[...], next8[...], cw_ref, gcw_ref).astype(BF16)
        next8[...] = dxc[0:SUBLANES, :]

    rev = lambda i: nt - 1 - i
    tok = pl.BlockSpec((tile, d), lambda i: (rev(i), 0))
    vec = _full((1, d))
    acc = _full((SUBLANES, d))
    wsh = (nh, dh, dh)
    return _call(
        body, comms, name="rglru_bwd", grid=(nt,),
        in_specs=[tok, tok, pl.BlockSpec((tile, d), lambda i: (rev(i), 1)), tok,
                  _halo_spec(d, tile, nt, 0)] + [tok] * 5 + [_full(conv_w.shape), _full(wsh), _full(wsh), vec],
        out_specs=[tok, tok, _full(wsh), _full(wsh), acc, acc, acc, acc, acc],
        out_shape=[jax.ShapeDtypeStruct((s, d), BF16)] * 2 + [jax.ShapeDtypeStruct(wsh, F32)] * 2
        + [jax.ShapeDtypeStruct((SUBLANES, d), F32)] * 5,
        scratch_shapes=[pltpu.VMEM((SUBLANES, d), F32)] * 3 + [pltpu.VMEM((tile, d), F32)],
        args=(dy_rg, u, u, h_rg, h_rg, *gates, conv_w, w_a, w_x, lam))


def _segments(d, w, n_pieces, n_slots):
    bounds = sorted({k * d for k in range(n_pieces + 1)} | {j * w for j in range(n_slots + 1)})
    return [(lo // d, lo % d, lo // w, lo % w, hi - lo) for lo, hi in zip(bounds[:-1], bounds[1:])]


def _in_bwd(pieces, x, dxo, ng, scale, w_in_g, layer, tile, comms=None, tiles=None, prev=None):
    s, d = x.shape
    nd, _, _, w = w_in_g.shape
    segs = _segments(d, w, len(pieces), nd)
    first, count = tiles or (0, s // tile)
    n_p = len(pieces)

    def body(*refs):
        p_refs = refs[:n_p]
        x_ref, dxo_ref, ng_ref, sc_ref, w_ref = refs[n_p:n_p + 5]
        dx_ref, dsc_ref, dsh_ref, gng_ref, wcat = refs[-5:]
        _join_columns(w_ref, wcat)

        @pl.when(pl.program_id(0) == 0)
        def _():
            for k, ref in enumerate((dsc_ref, dsh_ref, gng_ref)):
                ref[...] = jnp.zeros_like(ref) if prev is None else refs[n_p + 6 + k][...]

        du = jnp.concatenate([p[...] for p in p_refs], axis=1)
        dh = lax.dot_general(du, wcat[...], (((1,), (1,)), ((), ())), preferred_element_type=F32)
        xv = x_ref[...]
        g = ng_ref[...]
        rs = lax.rsqrt(jnp.mean(xv * xv, axis=1, keepdims=True) + EPS)
        xh = xv * rs
        dsh_ref[...] += _bcast8(_colsum(dh))
        dsc_ref[...] += _bcast8(_colsum(dh * xh * g))
        dhn = dh * (1.0 + sc_ref[...])
        gng_ref[...] += _bcast8(_colsum(dhn * xh))
        dxh = dhn * g
        dx_ref[...] = dxo_ref[...] + rs * (dxh - xh * jnp.mean(dxh * xh, axis=1, keepdims=True))

    tok = pl.BlockSpec((tile, d), lambda i: (i + first, 0))
    vec = _full((1, d))
    acc = _full((SUBLANES, d))
    more_specs = [] if prev is None else [pl.BlockSpec(memory_space=pl.ANY), acc, acc, acc]
    return _call(
        body, comms, name="in_proj_bwd_x", grid=(count,),
        in_specs=[tok] * n_p + [tok, tok, vec, vec, pl.BlockSpec((nd, 1, d, w), lambda i: (0, layer, 0, 0),
                                                               pipeline_mode=pl.Buffered(1))] + more_specs,
        out_specs=[tok, acc, acc, acc],
        out_shape=[jax.ShapeDtypeStruct((s, d), F32)] + [jax.ShapeDtypeStruct((SUBLANES, d), F32)] * 3,
        scratch_shapes=[pltpu.VMEM((d, nd * w), BF16)],
        args=(*pieces, x, dxo, ng, scale, w_in_g) + (() if prev is None else tuple(prev)),
        aliases={} if prev is None else {n_p + 5: 0})


def _in_bwd_w(pieces, hbf, w, slots, tile, comms=None):
    s, d = hbf.shape
    nd_all = len(pieces) * d // w
    segs = [sg for sg in _segments(d, w, len(pieces), nd_all) if sg[2] in slots]

    def body(*refs):
        p_refs = refs[:len(pieces)]
        h_ref, gw_ref = refs[len(pieces):]

        @pl.when(pl.program_id(0) == 0)
        def _():
            gw_ref[...] = jnp.zeros_like(gw_ref)

        hv = h_ref[...]
        for (kk, a, j, b, width) in segs:
            gw_ref[j - slots[0], :, b:b + width] += _mm_tn(hv, p_refs[kk][:, a:a + width])

    tok = pl.BlockSpec((tile, d), lambda i: (i, 0))
    return _call(
        body, comms, name="in_proj_bwd_w", grid=(s // tile,),
        in_specs=[tok] * len(pieces) + [tok],
        out_specs=[pl.BlockSpec((len(slots), d, w), lambda i: (0, 0, 0), pipeline_mode=pl.Buffered(1))],
        out_shape=[jax.ShapeDtypeStruct((len(slots), d, w), F32)],
        args=(*pieces, hbf))[0]


def _ada_bwd_w(cact_col, dmod, nd):
    d = cact_col.shape[0]
    w = dmod.shape[1] // nd

    def body(c_ref, m_ref, o_ref):
        o_ref[0] = c_ref[...] * m_ref[...]

    return pl.pallas_call(
        body, name="adaln_bwd_w", grid=(nd,),
        in_specs=[_full((d, 1)), pl.BlockSpec((1, w), lambda j: (0, j))],
        out_specs=pl.BlockSpec((1, d, w), lambda j: (j, 0, 0)),
        out_shape=jax.ShapeDtypeStruct((nd, d, w), F32),
        compiler_params=_params(1),
    )(cact_col, dmod)


def _exchange(arrs, gather, name):
    return _run_comms([_exchange_comm(arrs, gather)], name)[0]


def _run_comms(comms, name):
    _call(lambda: None, comms, name=name, grid=(1,), in_specs=[], out_specs=[], out_shape=[], args=())
    return [cm.results for cm in comms]


def _exchange_comm(arrs, gather):
    n = len(arrs)
    per = N_DEV - 1

    def copies(ins, outs, sems):
        send_sems, recv_sems, local_sems = sems
        x, y, c = (lax.axis_index(ax) for ax in MESH_AXES)
        me = 4 * x + 2 * y + c
        sends, recvs = [], []
        for flip in range(1, N_DEV):
            px = x ^ ((flip >> 2) & 1)
            py = y ^ ((flip >> 1) & 1)
            pc = c ^ (flip & 1)
            peer = 4 * px + 2 * py + pc
            for kk in range(n):
                src = ins[kk] if gather else ins[kk].at[peer]
                sends.append(_remote(src, outs[kk].at[me], send_sems, recv_sems, kk * per + flip - 1, (px, py, pc)))
                recvs.append(_remote(src, outs[kk].at[peer], send_sems, recv_sems, kk * per + flip - 1, (px, py, pc)))
        local = [pltpu.make_async_copy(ins[kk] if gather else ins[kk].at[me], outs[kk].at[me], local_sems.at[kk])
                 for kk in range(n)]
        return local, sends, recvs

    def start(ins, outs, sems):
        local, sends, _ = copies(ins, outs, sems)
        for cp in sends + local:
            cp.start()

    def finish(ins, outs, sems):
        local, sends, recvs = copies(ins, outs, sems)
        for cp in recvs:
            cp.wait_recv()
        for cp in sends:
            cp.wait_send()
        for cp in local:
            cp.wait()

    return _Comm(arrs, [jax.ShapeDtypeStruct((N_DEV,) + a.shape if gather else a.shape, a.dtype) for a in arrs],
                 [pltpu.SemaphoreType.DMA((n * per,)), pltpu.SemaphoreType.DMA((n * per,)), pltpu.SemaphoreType.DMA((n,))],
                 start, finish)


def _mesh_place():
    x, y, c = (lax.axis_index(ax) for ax in MESH_AXES)
    return x, y, c, (x, y, 1 - c), [(1 - x, y), (x, 1 - y), (1 - x, 1 - y)]


def _remote(src, dst, send_sems, recv_sems, sem, to):
    return pltpu.make_async_remote_copy(src_ref=src, dst_ref=dst, send_sem=send_sems.at[sem], recv_sem=recv_sems.at[sem],
                                        device_id=to, device_id_type=pl.DeviceIdType.MESH)


def _gather_two_level(arrs, name):
    n = len(arrs)
    per = N_DEV - 1

    def body(*refs):
        ins, outs = refs[:n], refs[n:2 * n]
        send_sems, recv_sems, local_sems = refs[2 * n:]
        x, y, c, sibling, chips = _mesh_place()

        def copy(kk, j, block, to, src=None):
            dst = outs[kk].at[4 * block[0] + 2 * block[1] + block[2]]
            return _remote(dst if src is None else src, dst, send_sems, recv_sems, kk * per + j, to)

        me = (x, y, c)
        local = [pltpu.make_async_copy(ins[kk], outs[kk].at[4 * x + 2 * y + c], local_sems.at[kk]) for kk in range(n)]
        first = []
        for j, chip in enumerate(chips):
            first += [copy(kk, 1 + j, me, (*chip, c), src=ins[kk]) for kk in range(n)]
        first += [copy(kk, 0, me, sibling, src=ins[kk]) for kk in range(n)]
        for cp in first + local:
            cp.start()
        passed = []
        for j, chip in enumerate(chips):
            for kk in range(n):
                copy(kk, 1 + j, (*chip, c), me).wait_recv()
                passed.append(copy(kk, 4 + j, (*chip, c), sibling))
                passed[-1].start()
        for kk in range(n):
            copy(kk, 0, sibling, me).wait_recv()
        for j, chip in enumerate(chips):
            for kk in range(n):
                copy(kk, 4 + j, (*chip, 1 - c), me).wait_recv()
        for cp in first + passed:
            cp.wait_send()
        for cp in local:
            cp.wait()

    return pl.pallas_call(
        body, name=name,
        in_specs=[pl.BlockSpec(memory_space=pl.ANY)] * n, out_specs=[pl.BlockSpec(memory_space=pl.ANY)] * n,
        out_shape=[jax.ShapeDtypeStruct((N_DEV,) + a.shape, a.dtype) for a in arrs],
        scratch_shapes=[pltpu.SemaphoreType.DMA((n * per,)), pltpu.SemaphoreType.DMA((n * per,)),
                        pltpu.SemaphoreType.DMA((n,))],
    )(*arrs)


N_CHIPS = N_DEV // 2


def _core_swap(arrs, name):
    n = len(arrs)

    def body(*refs):
        ins, outs = refs[:n], refs[n:2 * n]
        send_sems, recv_sems = refs[2 * n:]
        _, _, c, sibling, _ = _mesh_place()
        copies = [_remote(ins[kk].at[2 * q + (1 - c)], outs[kk].at[q], send_sems, recv_sems, kk * N_CHIPS + q, sibling)
                  for q in range(N_CHIPS) for kk in range(n)]
        for cp in copies:
            cp.start()
        for cp in copies:
            cp.wait_recv()
        for cp in copies:
            cp.wait_send()

    return pl.pallas_call(
        body, name=name,
        in_specs=[pl.BlockSpec(memory_space=pl.ANY)] * n, out_specs=[pl.BlockSpec(memory_space=pl.ANY)] * n,
        out_shape=[jax.ShapeDtypeStruct((N_CHIPS,) + a.shape[1:], a.dtype) for a in arrs],
        scratch_shapes=[pltpu.SemaphoreType.DMA((n * N_CHIPS,)), pltpu.SemaphoreType.DMA((n * N_CHIPS,))],
    )(*arrs)


def _pair_sum(a, other, parity, name):
    _, r, c = a.shape
    tr = _row_tile(r, c, 3)

    def body(p_ref, a_ref, o_ref, s_ref):
        s_ref[...] = (a_ref[...] + o_ref[...]).astype(BF16)

    return pl.pallas_call(
        body, name=name,
        grid_spec=pltpu.PrefetchScalarGridSpec(
            num_scalar_prefetch=1, grid=(N_CHIPS, r // tr),
            in_specs=[pl.BlockSpec((1, tr, c), lambda q, i, p: (2 * q + p[0], i, 0)),
                      pl.BlockSpec((1, tr, c), lambda q, i, p: (q, i, 0))],
            out_specs=pl.BlockSpec((1, tr, c), lambda q, i, p: (q, i, 0))),
        out_shape=jax.ShapeDtypeStruct((N_CHIPS, r, c), BF16),
        compiler_params=_params(2),
    )(parity, a, other)


def _chip_swap(arrs, name):
    n = len(arrs)
    per = N_CHIPS - 1

    def body(*refs):
        ins, outs = refs[:n], refs[n:2 * n]
        send_sems, recv_sems, local_sems = refs[2 * n:]
        x, y, c, _, chips = _mesh_place()
        mine = 2 * x + y
        sends = [_remote(ins[kk].at[2 * chip[0] + chip[1]], outs[kk].at[mine], send_sems, recv_sems, kk * per + j, (*chip, c))
                 for j, chip in enumerate(chips) for kk in range(n)]
        recvs = [_remote(ins[kk].at[mine], outs[kk].at[2 * chip[0] + chip[1]], send_sems, recv_sems, kk * per + j, (*chip, c))
                 for j, chip in enumerate(chips) for kk in range(n)]
        local = [pltpu.make_async_copy(ins[kk].at[mine], outs[kk].at[mine], local_sems.at[kk]) for kk in range(n)]
        for cp in sends + local:
            cp.start()
        for cp in recvs:
            cp.wait_recv()
        for cp in sends:
            cp.wait_send()
        for cp in local:
            cp.wait()

    return pl.pallas_call(
        body, name=name,
        in_specs=[pl.BlockSpec(memory_space=pl.ANY)] * n, out_specs=[pl.BlockSpec(memory_space=pl.ANY)] * n,
        out_shape=[jax.ShapeDtypeStruct(a.shape, a.dtype) for a in arrs],
        scratch_shapes=[pltpu.SemaphoreType.DMA((n * per,)), pltpu.SemaphoreType.DMA((n * per,)),
                        pltpu.SemaphoreType.DMA((n,))],
    )(*arrs)


def _adam_math(w, g, m, v):
    m = ADAM_B1 * m + (1.0 - ADAM_B1) * g
    v = ADAM_B2 * v + (1.0 - ADAM_B2) * (g * g)
    m_hat = m / (1.0 - ADAM_B1 ** ADAM_STEP)
    v_hat = v / (1.0 - ADAM_B2 ** ADAM_STEP)
    delta = -ADAM_LR * (m_hat / (jnp.sqrt(v_hat) + ADAM_EPS) + ADAM_WD * w)
    return delta, m, v


def _sum_devices(r_ref):
    acc = r_ref[0].astype(F32)
    for p in range(1, r_ref.shape[0]):
        acc = acc + r_ref[p].astype(F32)
    return acc


def _row_tile(rows, cols, n_bufs):
    budget = 24 * 1024 * 1024 // (n_bufs * 2 * cols * 4)
    t = rows
    while t > budget and t % 2 == 0 and (t // 2) % SUBLANES == 0:
        t //= 2
    return t


def _reduce_adam(recvs, w, m, v, name):
    nl, r, c = w.shape
    n_part = recvs[0].shape[0]
    tr = _row_tile(r, c, n_part * nl + 7)
    nt = r // tr

    def body(*refs):
        r_refs = refs[:nl]
        w_ref, m_ref, v_ref, g_ref, d_ref, mo_ref, vo_ref = refs[nl:]
        layer = pl.program_id(0)
        g = _sum_devices(r_refs[0])
        for ll in range(1, nl):
            g = jnp.where(layer == ll, _sum_devices(r_refs[ll]), g)
        delta, m2, v2 = _adam_math(w_ref[0], g, m_ref[0], v_ref[0])
        g_ref[0] = g
        d_ref[0] = delta
        mo_ref[0] = m2
        vo_ref[0] = v2

    def rspec(ll):
        return pl.BlockSpec((n_part, tr, c), lambda l, i: (0, jnp.where(l == ll, i, jnp.where(l < ll, 0, nt - 1)), 0))

    blk = pl.BlockSpec((1, tr, c), lambda l, i: (l, i, 0))
    return pl.pallas_call(
        body, name=name, grid=(nl, nt),
        in_specs=[rspec(ll) for ll in range(nl)] + [blk, blk, blk],
        out_specs=[blk] * 4,
        out_shape=[jax.ShapeDtypeStruct((nl, r, c), F32)] * 4,
        compiler_params=_params(2),
    )(*recvs, w, m, v)


def _sum8(recv, name):
    _, r, c = recv.shape

    def body(r_ref, o_ref):
        o_ref[...] = _sum_devices(r_ref)

    return pl.pallas_call(
        body, name=name, grid=(1,),
        in_specs=[_full(recv.shape)], out_specs=_full((r, c)),
        out_shape=jax.ShapeDtypeStruct((r, c), F32), compiler_params=_params(1),
    )(recv)


def _adam_call(w, g, m, v, name):
    r, c = w.shape

    def body(w_ref, g_ref, m_ref, v_ref, d_ref, mo_ref, vo_ref):
        delta, m2, v2 = _adam_math(w_ref[...], g_ref[...], m_ref[...], v_ref[...])
        d_ref[...] = delta
        mo_ref[...] = m2
        vo_ref[...] = v2

    return pl.pallas_call(
        body, name=name, grid=(1,),
        in_specs=[_full((r, c))] * 4, out_specs=[_full((r, c))] * 3,
        out_shape=[jax.ShapeDtypeStruct((r, c), F32)] * 3, compiler_params=_params(1),
    )(w, g, m, v)


def _tile_for(s, want):
    return min(want, s)


def _local_step_whole(x, c, target, wts):
    s, d = x.shape
    nl = wts["w_in_g"].shape[1]
    nd = wts["w_in_g"].shape[0]
    nh_ml = wts["w_qkv"].shape[2]
    t_big = _tile_for(s, 512)
    t_mid = _tile_for(s, 256)

    mod, cact = _mod_call(c, wts["w_ada_g"], wts["b_ada"])
    row = lambda a: a.reshape(1, -1)
    saved = []
    xl = x
    for l in range(nl):
        shift, scale, gate = (row(mod[l, kk * d:(kk + 1) * d]) for kk in range(3))
        u, hbf = _in_fwd(xl, row(wts["norm_g"][l]), scale, shift, wts["w_in_g"], l, t_mid)
        h_rg, y_rg = _rg_fwd(u, d, wts["rg_conv_w"][l], row(wts["rg_conv_b"][l]), wts["rg_w_a_bf"][l],
                             row(wts["rg_b_a"][l]), wts["rg_w_x_bf"][l], row(wts["rg_b_x"][l]),
                             row(wts["rg_lambda"][l]), t_mid)
        q, k, v, gcol = _ml_pre(u, d, wts["ml_conv_w"][l], row(wts["ml_conv_b"][l]), wts["w_qkv"][l, 0],
                                wts["w_qkv"][l, 1], wts["w_qkv"][l, 2], wts["wif_pad"][l], wts["bif_pad"][l], t_mid)
        grow = gcol[:, 0:16].T
        cell, y_ml, cs, ns, ms, mt = _ml_cell_fwd(q, k, v, gcol, grow, u, row(wts["ml_norm_g"][l]), nh_ml)
        x_new, y = _out_fwd(xl, y_rg, y_ml, gate, wts["w_out_g"], l, t_big)
        saved.append(dict(x=xl, u=u, hbf=hbf, h_rg=h_rg, y_rg=y_rg, q=q, k=k, v=v, gcol=gcol, grow=grow, cell=cell,
                          y_ml=y_ml, cs=cs, ns=ns, ms=ms, mt=mt, y=y, scale=scale, gate=gate))
        xl = x_new

    dx, loss_p, g_final = _loss_call(xl, row(wts["final_g"]), target, t_big)
    grads = [None] * nl
    cact_col = cact[0].reshape(d, 1)
    for l in reversed(range(nl)):
        sv = saved[l]
        dy_rg, dy_ml, gw_out, dgate = _out_bwd(dx, sv["gate"], sv["y"], sv["y_rg"], sv["y_ml"], wts["w_out_g"], l, t_big)
        dq, dk, dv, dgates, d_mlo, d_mlz, g_mlng = _ml_cell_bwd(
            dy_ml, sv["u"], sv["cell"], sv["q"], sv["k"], sv["v"], sv["gcol"], sv["grow"], sv["mt"], sv["cs"],
            sv["ns"], sv["ms"], row(wts["ml_norm_g"][l]), nh_ml)
        d_mlx, g_wq, g_wk, g_wv, g_wift, g_bif, g_mlcw, g_mlcb = _ml_pre_bwd(
            dq, dk, dv, dgates, sv["gcol"], sv["u"], sv["q"], sv["k"], sv["v"], wts["ml_conv_w"][l],
            row(wts["ml_conv_b"][l]), wts["w_qkv"][l, 0], wts["w_qkv"][l, 1], wts["w_qkv"][l, 2], wts["wift_pad"][l], t_mid)
        d_rgx, d_rgz, g_wa, g_wx, g_ba, g_bx, g_lam, g_rgcw, g_rgcb = _rg_bwd(
            dy_rg, sv["u"], sv["h_rg"], wts["rg_conv_w"][l], row(wts["rg_conv_b"][l]), wts["rg_w_a_bf"][l],
            row(wts["rg_b_a"][l]), wts["rg_w_x_bf"][l], row(wts["rg_b_x"][l]), row(wts["rg_lambda"][l]), t_mid)
        pieces = [d_rgx, d_rgz, d_mlx, d_mlo, d_mlz]
        dx, dscale, dshift, g_ng = _in_bwd(pieces, sv["x"], dx, row(wts["norm_g"][l]), sv["scale"], wts["w_in_g"], l, t_mid)
        half = nd // 2
        w_cols = wts["w_in_g"].shape[3]
        gw_in = jnp.concatenate([_in_bwd_w(pieces, sv["hbf"], w_cols, tuple(range(0, half)), t_big),
                                 _in_bwd_w(pieces, sv["hbf"], w_cols, tuple(range(half, nd)), t_big)], axis=0)
        dmod = jnp.concatenate([dshift[0:1], dscale[0:1], dgate[0:1]], axis=1)
        gw_ada = _ada_bwd_w(cact_col, dmod, nd)
        grads[l] = dict(w_ada=gw_ada, w_in=gw_in, w_out=gw_out, w_qkv=jnp.stack([g_wq, g_wk, g_wv]),
                        rg_conv_w=g_rgcw[0:CONV_WIDTH], ml_conv_w=g_mlcw[0:CONV_WIDTH], wif_t=g_wift[0:8],
                        norm_g=g_ng[0], b_ada=dmod[0], rg_conv_b=g_rgcb[0], rg_w_a=g_wa, rg_b_a=g_ba[0], rg_w_x=g_wx,
                        rg_b_x=g_bx[0], rg_lambda=g_lam[0], ml_conv_b=g_mlcb[0], ml_b_if=g_bif[0, 0:8],
                        ml_norm_g=g_mlng[0])
    return loss_p[0, 0], dx, grads, g_final[0]


REPLICATED = ("norm_g", "b_ada", "rg_conv_b", "rg_w_a", "rg_b_a", "rg_w_x", "rg_b_x", "rg_lambda", "ml_conv_b",
              "ml_b_if", "ml_norm_g", "final_g")
ROW_ALIGN = N_DEV * SUBLANES


def _to_rows(a):
    flat = a.reshape(-1)
    pad = (-flat.shape[0]) % LANES
    return jnp.pad(flat, (0, pad)).reshape(-1, LANES)


def _pack(arrays):
    rows = jnp.concatenate([_to_rows(a) for a in arrays], axis=0)
    return jnp.pad(rows, ((0, (-rows.shape[0]) % ROW_ALIGN), (0, 0)))


def _unpack(rows, like):
    out, at = [], 0
    for a in like:
        n = -(-a.size // LANES)
        out.append(rows[at:at + n].reshape(-1)[:a.size].reshape(a.shape))
        at += n
    return out


def _small_pack(rg_conv_w, ml_conv_w, ml_w_if):
    nl = rg_conv_w.shape[0]
    wif_t = jnp.swapaxes(ml_w_if, 1, 2).reshape(nl, -1, LANES)
    return jnp.concatenate([rg_conv_w, ml_conv_w, wif_t], axis=1)


def _small_unpack(p, if_rows):
    nl = p.shape[0]
    rg_cw = p[:, 0:CONV_WIDTH]
    ml_cw = p[:, CONV_WIDTH:2 * CONV_WIDTH]
    wif = jnp.swapaxes(p[:, 2 * CONV_WIDTH:].reshape(nl, 8, if_rows), 1, 2)
    return rg_cw, ml_cw, wif


def _assemble_weights(big, small, rep):
    w_ada_g, w_in_g, w_out_g, qkv_g = big
    nd, nl = small.shape[0], small.shape[1]
    d = w_in_g.shape[2]
    dh = qkv_g.shape[3]
    nh = d // dh
    rsh = qkv_g.shape[2] // (3 * nh)
    w_qkv = qkv_g.reshape(nd, nl, 3, nh, rsh, dh).transpose(1, 2, 3, 0, 4, 5).reshape(nl, 3, nh, nd * rsh, dh)
    cw = small[:, :, 0:2 * CONV_WIDTH].reshape(nd, nl, 2, CONV_WIDTH, LANES).transpose(1, 2, 3, 0, 4)
    cw = cw.reshape(nl, 2, CONV_WIDTH, nd * LANES)
    if_rows = (small.shape[2] - 2 * CONV_WIDTH) * LANES // 8
    wif_t = small[:, :, 2 * CONV_WIDTH:].reshape(nd, nl, 8, if_rows).transpose(1, 2, 0, 3).reshape(nl, 8, nd * if_rows)
    wift_pad = jnp.pad(wif_t, ((0, 0), (0, LANES - 8), (0, 0))).astype(BF16)
    wif_pad = jnp.swapaxes(wift_pad, 1, 2)
    bif_pad = jnp.pad(rep["ml_b_if"], ((0, 0), (0, LANES - 8))).reshape(nl, 1, LANES)
    wts = dict(rep)
    wts.update(w_ada_g=w_ada_g, w_in_g=w_in_g, w_out_g=w_out_g, w_qkv=w_qkv, rg_conv_w=cw[:, 0], ml_conv_w=cw[:, 1],
               wif_pad=wif_pad, wift_pad=wift_pad, bif_pad=bif_pad, rg_w_a_bf=rep["rg_w_a"].astype(BF16),
               rg_w_x_bf=rep["rg_w_x"].astype(BF16))
    return wts


def _qkv_slots(g_qkv, nd):
    three, nh, dh, _ = g_qkv.shape
    return g_qkv.reshape(three, nh, nd, dh // nd, dh).transpose(2, 0, 1, 3, 4).reshape(nd, three * nh * (dh // nd), dh)


def _small_slots(g):
    nd = N_DEV
    cw = jnp.stack([g["rg_conv_w"], g["ml_conv_w"]]).reshape(2, CONV_WIDTH, nd, LANES).transpose(2, 0, 1, 3)
    cw = cw.reshape(nd, 2 * CONV_WIDTH, LANES)
    wif = g["wif_t"].reshape(8, nd, -1).transpose(1, 0, 2).reshape(nd, -1, LANES)
    return jnp.concatenate([cw, wif], axis=1)


def _kernel_unhosted(x, c, norm_g, w_ada, b_ada, w_in, rg_conv_w, rg_conv_b, rg_w_a, rg_b_a, rg_w_x, rg_b_x, rg_lambda, ml_conv_w, ml_conv_b, ml_w_q, ml_w_k, ml_w_v, ml_w_if, ml_b_if, ml_norm_g, w_out, final_g, loss_target, m_norm_g, m_w_ada, m_b_ada, m_w_in, m_rg_conv_w, m_rg_conv_b, m_rg_w_a, m_rg_b_a, m_rg_w_x, m_rg_b_x, m_rg_lambda, m_ml_conv_w, m_ml_conv_b, m_ml_w_q, m_ml_w_k, m_ml_w_v, m_ml_w_if, m_ml_b_if, m_ml_norm_g, m_w_out, m_final_g, v_norm_g, v_w_ada, v_b_ada, v_w_in, v_rg_conv_w, v_rg_conv_b, v_rg_w_a, v_rg_b_a, v_rg_w_x, v_rg_b_x, v_rg_lambda, v_ml_conv_w, v_ml_conv_b, v_ml_w_q, v_ml_w_k, v_ml_w_v, v_ml_w_if, v_ml_b_if, v_ml_norm_g, v_w_out, v_final_g):
    given = dict(locals())
    nl = w_in.shape[0]
    rep = {n: given[n] for n in REPLICATED}

    def qkv_shard(prefix):
        return jnp.stack([given[prefix + "ml_w_q"], given[prefix + "ml_w_k"], given[prefix + "ml_w_v"]], axis=1).reshape(
            nl, -1, ml_w_q.shape[-1])

    *big, small = _gather_two_level(
        [w_ada.astype(BF16), w_in.astype(BF16), w_out.astype(BF16), qkv_shard("").astype(BF16),
         _small_pack(rg_conv_w, ml_conv_w, ml_w_if)], "gather_weights")
    wts = _assemble_weights(big, small, rep)

    loss_p, grad_x, grads, g_final = _local_step(x[0], c, loss_target[0], wts)
    loss = lax.psum(loss_p, MESH_AXES)

    keys = ("w_ada", "w_in", "w_out", "w_qkv", "small")
    parity = lax.axis_index("c").astype(jnp.int32).reshape(1)
    recv = []
    for l in range(nl):
        g = grads[l]
        parts = [g["w_ada"], g["w_in"], g["w_out"], _qkv_slots(g["w_qkv"], N_DEV), _small_slots(g)]
        other = _core_swap(parts, "core_swap_layer%d" % l)
        sums = [_pair_sum(a, o, parity, "pair_sum_%s_layer%d" % (key, l)) for key, a, o in zip(keys, parts, other)]
        recv.append(_chip_swap(sums, "chip_swap_layer%d" % l))
    shard = {"": dict(w_ada=w_ada, w_in=w_in, w_out=w_out, w_qkv=qkv_shard(""),
                      small=_small_pack(rg_conv_w, ml_conv_w, ml_w_if))}
    for p in ("m_", "v_"):
        shard[p] = dict(w_ada=given[p + "w_ada"], w_in=given[p + "w_in"], w_out=given[p + "w_out"], w_qkv=qkv_shard(p),
                        small=_small_pack(given[p + "rg_conv_w"], given[p + "ml_conv_w"], given[p + "ml_w_if"]))
    res = {}
    for ki, key in enumerate(keys):
        res[key] = _reduce_adam([recv[l][ki] for l in range(nl)], shard[""][key], shard["m_"][key], shard["v_"][key],
                                "reduce_adam_" + key)

    rep_g = dict(final_g=g_final)
    for n in REPLICATED[:-1]:
        rep_g[n] = jnp.stack([grads[l][n] for l in range(nl)])
    pack_g = _pack([rep_g[n] for n in REPLICATED])
    rows = pack_g.shape[0] // N_DEV
    mine = _sum8(_exchange([pack_g.reshape(N_DEV, rows, LANES)], False, "reduce_scatter_replicated")[0], "sum_replicated")
    g_rep = _exchange([mine], True, "gather_replicated")[0].reshape(N_DEV * rows, LANES)
    rep_like = [rep[n] for n in REPLICATED]
    d_rep, m_rep, v_rep = _adam_call(_pack(rep_like), g_rep, _pack([given["m_" + n] for n in REPLICATED]),
                                     _pack([given["v_" + n] for n in REPLICATED]), "adam_replicated")
    rep_out = [dict(zip(REPLICATED, _unpack(a, rep_like))) for a in (g_rep, d_rep, m_rep, v_rep)]

    if_rows = ml_w_if.shape[1]
    order = ("norm_g", "w_ada", "b_ada", "w_in", "rg_conv_w", "rg_conv_b", "rg_w_a", "rg_b_a", "rg_w_x", "rg_b_x",
             "rg_lambda", "ml_conv_w", "ml_conv_b", "ml_w_q", "ml_w_k", "ml_w_v", "ml_w_if", "ml_b_if", "ml_norm_g",
             "w_out", "final_g")
    outs = [loss, grad_x[None]]
    for kind in range(4):
        qkv = res["w_qkv"][kind].reshape((nl, 3) + ml_w_q.shape[1:])
        rg_cw, ml_cw, wif = _small_unpack(res["small"][kind], if_rows)
        sharded = dict(w_ada=res["w_ada"][kind], w_in=res["w_in"][kind], w_out=res["w_out"][kind], ml_w_q=qkv[:, 0],
                       ml_w_k=qkv[:, 1], ml_w_v=qkv[:, 2], rg_conv_w=rg_cw, ml_conv_w=ml_cw, ml_w_if=wif)
        for n in order:
            outs.append(sharded[n] if n in sharded else rep_out[kind][n])
    return tuple(outs)


def _slot(block):
    return 4 * block[0] + 2 * block[1] + block[2]


def _dma_sems(*counts):
    return [pltpu.SemaphoreType.DMA((n,)) for n in counts]


def _start_all(copies):
    for cp in copies:
        cp.start()


def _gather_ici_comm(arrs):
    n = len(arrs)

    def copies(ins, outs, sems):
        send_sems, recv_sems, local_sems = sems
        x, y, c, sibling, chips = _mesh_place()
        me = (x, y, c)
        peers = [(*chip, c) for chip in chips] + [sibling]
        local = [pltpu.make_async_copy(ins[kk], outs[kk].at[_slot(me)], local_sems.at[kk]) for kk in range(n)]
        sends = [_remote(ins[kk], outs[kk].at[_slot(me)], send_sems, recv_sems, kk * 4 + j, peer)
                 for j, peer in enumerate(peers) for kk in range(n)]
        recvs = [_remote(ins[kk], outs[kk].at[_slot(peer)], send_sems, recv_sems, kk * 4 + j, peer)
                 for j, peer in enumerate(peers) for kk in range(n)]
        return local, sends, recvs

    def start(ins, outs, sems):
        local, sends, _ = copies(ins, outs, sems)
        _start_all(sends + local)

    def finish(ins, outs, sems):
        local, sends, recvs = copies(ins, outs, sems)
        for cp in recvs:
            cp.wait_recv()
        for cp in sends:
            cp.wait_send()
        for cp in local:
            cp.wait()

    return _Comm(arrs, [jax.ShapeDtypeStruct((N_DEV,) + a.shape, a.dtype) for a in arrs], _dma_sems(4 * n, 4 * n, n),
                 start, finish)


def _gather_fwd_comm(bufs):
    n = len(bufs)

    def copies(ins, outs, sems):
        send_sems, recv_sems = sems
        _, _, c, sibling, chips = _mesh_place()
        sends = [_remote(ins[kk].at[_slot((*chip, c))], outs[kk].at[_slot((*chip, c))], send_sems, recv_sems, kk * 3 + j, sibling)
                 for j, chip in enumerate(chips) for kk in range(n)]
        recvs = [_remote(ins[kk].at[_slot((*chip, c))], outs[kk].at[_slot((*chip, 1 - c))], send_sems, recv_sems, kk * 3 + j, sibling)
                 for j, chip in enumerate(chips) for kk in range(n)]
        return sends, recvs

    def start(ins, outs, sems):
        _start_all(copies(ins, outs, sems)[0])

    def finish(ins, outs, sems):
        sends, recvs = copies(ins, outs, sems)
        for cp in recvs:
            cp.wait_recv()
        for cp in sends:
            cp.wait_send()

    return _Comm(bufs, [jax.ShapeDtypeStruct(a.shape, a.dtype) for a in bufs], _dma_sems(3 * n, 3 * n), start, finish,
                 aliases=[(i, i) for i in range(n)])


def _core_swap_comm(arrs):
    n = len(arrs)

    def copies(ins, outs, sems):
        send_sems, recv_sems = sems
        _, _, c, sibling, _ = _mesh_place()
        return [_remote(ins[kk].at[2 * q + (1 - c)], outs[kk].at[q], send_sems, recv_sems, kk * N_CHIPS + q, sibling)
                for q in range(N_CHIPS) for kk in range(n)]

    def start(ins, outs, sems):
        _start_all(copies(ins, outs, sems))

    def finish(ins, outs, sems):
        cps = copies(ins, outs, sems)
        for cp in cps:
            cp.wait_recv()
        for cp in cps:
            cp.wait_send()

    return _Comm(arrs, [jax.ShapeDtypeStruct((N_CHIPS,) + a.shape[1:], a.dtype) for a in arrs],
                 _dma_sems(N_CHIPS * n, N_CHIPS * n), start, finish)


def _chip_swap_comm(arrs):
    n = len(arrs)
    per = N_CHIPS - 1

    def copies(ins, outs, sems):
        send_sems, recv_sems, local_sems = sems
        x, y, c, _, chips = _mesh_place()
        mine = 2 * x + y
        sends = [_remote(ins[kk].at[2 * chip[0] + chip[1]], outs[kk].at[mine], send_sems, recv_sems, kk * per + j, (*chip, c))
                 for j, chip in enumerate(chips) for kk in range(n)]
        recvs = [_remote(ins[kk].at[mine], outs[kk].at[2 * chip[0] + chip[1]], send_sems, recv_sems, kk * per + j, (*chip, c))
                 for j, chip in enumerate(chips) for kk in range(n)]
        local = [pltpu.make_async_copy(ins[kk].at[mine], outs[kk].at[mine], local_sems.at[kk]) for kk in range(n)]
        return local, sends, recvs

    def start(ins, outs, sems):
        local, sends, _ = copies(ins, outs, sems)
        _start_all(sends + local)

    def finish(ins, outs, sems):
        local, sends, recvs = copies(ins, outs, sems)
        for cp in recvs:
            cp.wait_recv()
        for cp in sends:
            cp.wait_send()
        for cp in local:
            cp.wait()

    return _Comm(arrs, [jax.ShapeDtypeStruct(a.shape, a.dtype) for a in arrs], _dma_sems(per * n, per * n, n), start, finish)


def _ada_mod(c_all, w_ada, b_cols, comms=None):
    nl, d, w = w_ada.shape

    def body(c_ref, w_ref, b_ref, m_ref, ca_ref):
        sub = _iota((SUBLANES, d), 0)
        cv = jnp.zeros((SUBLANES, d), F32)
        for b in range(N_DEV):
            cv = jnp.where(sub == b, c_ref[b], cv)
        ca = cv * _sigmoid(cv)
        ca_ref[...] = ca
        m_ref[...] = jnp.zeros_like(m_ref)
        for l in range(nl):
            ml = _mm_hi(ca, w_ref[l]) + b_ref[l:l + 1, :]
            for b in range(N_DEV):
                m_ref[b, l:l + 1, :] = _row(ml, b)

    return _call(
        body, comms, name="adaln_mod_columns", grid=(1,),
        in_specs=[_full(c_all.shape), _full(w_ada.shape), _full(b_cols.shape)],
        out_specs=[_full((N_DEV, SUBLANES, w)), _full((SUBLANES, d))],
        out_shape=[jax.ShapeDtypeStruct((N_DEV, SUBLANES, w), F32), jax.ShapeDtypeStruct((SUBLANES, d), F32)],
        args=(c_all, w_ada, b_cols))


def _ada_grad_adam(cact_t, dmods, w, m, v):
    nl, d, wd = w.shape
    tr = _row_tile(d, wd, 8)

    def body(c_ref, dm_ref, w_ref, m_ref, v_ref, g_ref, d_ref, mo_ref, vo_ref):
        cv = c_ref[...]
        dm = dm_ref[0]
        g = _col(cv, 0) * _row(dm, 0)
        for b in range(1, N_DEV):
            g = g + _col(cv, b) * _row(dm, b)
        delta, m2, v2 = _adam_math(w_ref[0], g, m_ref[0], v_ref[0])
        g_ref[0] = g
        d_ref[0] = delta
        mo_ref[0] = m2
        vo_ref[0] = v2

    blk = pl.BlockSpec((1, tr, wd), lambda l, i: (l, i, 0))
    return pl.pallas_call(
        body, name="adaln_grad_adam", grid=(nl, d // tr),
        in_specs=[pl.BlockSpec((tr, N_DEV), lambda l, i: (i, 0)), pl.BlockSpec((1, N_DEV, wd), lambda l, i: (l, 0, 0)),
                  blk, blk, blk],
        out_specs=[blk] * 4, out_shape=[jax.ShapeDtypeStruct((nl, d, wd), F32)] * 4,
        compiler_params=_params(2),
    )(cact_t, dmods, w, m, v)


REP_ROWS = ("norm_g", "dshift", "dscale", "dgate", "rg_conv_b", "rg_b_a", "rg_b_x", "rg_lambda", "ml_conv_b", "ml_norm_g",
            "ml_b_if")


def _sum_parts(recvs, name):
    def body(*refs):
        for r_ref, o_ref in zip(refs[:len(recvs)], refs[len(recvs):]):
            o_ref[...] = _sum_devices(r_ref).astype(o_ref.dtype)

    return pl.pallas_call(
        body, name=name, grid=(1,),
        in_specs=[_full(r.shape) for r in recvs], out_specs=[_full(r.shape[1:]) for r in recvs],
        out_shape=[jax.ShapeDtypeStruct(r.shape[1:], r.dtype) for r in recvs], compiler_params=_params(1),
    )(*recvs)


def _adam_replicated(vp, mp, params, nl):
    d = vp.shape[1]
    nr = len(REP_ROWS)
    names = list(params)
    mat_shape = params["rg_w_a"][0].shape[1:]
    mat_rows = mp.shape[0] // (2 * nl)

    def pieces(name):
        if name == "final_g":
            return [(lambda vp_ref, mp_ref: vp_ref[nl * nr:nl * nr + 1, :], (slice(0, 1), slice(None)))]
        out = []
        for l in range(nl):
            if name in ("rg_w_a", "rg_w_x"):
                at = (2 * l + (name == "rg_w_x")) * mat_rows
                out.append((lambda vp_ref, mp_ref, at=at: mp_ref[at:at + mat_rows, :].astype(F32).reshape(mat_shape), l))
            elif name == "b_ada":
                for j in range(3):
                    r = l * nr + 1 + j
                    out.append((lambda vp_ref, mp_ref, r=r: vp_ref[r:r + 1, :], (slice(l, l + 1), slice(j * d, (j + 1) * d))))
            else:
                r = l * nr + REP_ROWS.index(name)
                cols = slice(0, LANES) if name == "ml_b_if" else slice(None)
                out.append((lambda vp_ref, mp_ref, r=r, cols=cols: vp_ref[r:r + 1, cols], (slice(l, l + 1), slice(None))))
        return out

    def body(*refs):
        vp_ref, mp_ref = refs[:2]
        ins, outs = refs[2:2 + 3 * len(names)], refs[2 + 3 * len(names):]
        for pi, name in enumerate(names):
            w_ref, m_ref, v_ref = ins[3 * pi:3 * pi + 3]
            g_ref, d_ref, mo_ref, vo_ref = outs[4 * pi:4 * pi + 4]
            for get, idx in pieces(name):
                g = get(vp_ref, mp_ref)
                delta, m2, v2 = _adam_math(w_ref[idx], g, m_ref[idx], v_ref[idx])
                g_ref[idx] = g
                d_ref[idx] = delta
                mo_ref[idx] = m2
                vo_ref[idx] = v2

    flat = [a for name in names for a in params[name]]
    out_shape = [jax.ShapeDtypeStruct(params[name][0].shape, F32) for name in names for _ in range(4)]
    res = pl.pallas_call(
        body, name="adam_replicated", grid=(1,),
        in_specs=[_full(vp.shape), _full(mp.shape)] + [_full(a.shape) for a in flat],
        out_specs=[_full(o.shape) for o in out_shape], out_shape=out_shape, compiler_params=_params(1),
    )(vp, mp, *flat)
    return {name: res[4 * pi:4 * pi + 4] for pi, name in enumerate(names)}


class _Plan:
    def __init__(self):
        self.hosted, self.after = {}, {}

    def host(self, key, comm, then=None):
        self.hosted.setdefault(key, []).append(comm)
        if then is not None:
            self.after.setdefault(key, []).append(then)

    def comms(self, key):
        return self.hosted.pop(key, None)

    def done(self, key):
        for fn in self.after.pop(key, []):
            fn()

    def flush(self):
        while self.hosted:
            key = next(iter(self.hosted))
            _call(lambda: None, self.comms(key), name="exchange_after_%s_%d" % key, grid=(1,), in_specs=[], out_specs=[],
                  out_shape=[], args=())
            self.done(key)


VEC_TABLE = ("norm_g", "rg_conv_b", "rg_b_a", "rg_b_x", "rg_lambda", "ml_conv_b", "ml_norm_g")


def _vec_table(rep):
    rows = [rep[n] for n in VEC_TABLE]
    return jnp.stack(rows + [jnp.zeros_like(rows[0])] * (SUBLANES - len(rows)), axis=1)


def _layer_fwd(l, xl, mod3, wl, rep, plan):
    s, d = xl.shape
    t_big, t_mid = _tile_for(s, 512), _tile_for(s, 256)
    nh_ml = rep["ml_b_if"].shape[1] // 2
    vec = lambda name: _vec(rep["vecs"], l, VEC_TABLE.index(name))
    shift, scale, gate = (_vec(mod3, l, kk) for kk in range(3))
    hosted = lambda name: plan.comms((name, l)) if plan else None
    done = lambda name: plan.done((name, l)) if plan else None
    u, hbf = _in_fwd(xl, vec("norm_g"), scale, shift, wl["w_in_g"], 0, t_mid, hosted("in_proj_fwd"))
    done("in_proj_fwd")
    h_rg, y_rg, *rg_gates = _rg_fwd(u, d, wl["rg_conv_w"], vec("rg_conv_b"), rep["rg_w_a_bf"][l], vec("rg_b_a"),
                                    rep["rg_w_x_bf"][l], vec("rg_b_x"), vec("rg_lambda"), t_mid, hosted("rglru_fwd"))
    done("rglru_fwd")
    q, k, v, gcol, pre = _ml_pre(u, d, wl["ml_conv_w"], vec("ml_conv_b"), wl["w_qkv"][0], wl["w_qkv"][1],
                                 wl["w_qkv"][2], wl["wif_pad"], wl["bif_pad"], t_mid, hosted("mlstm_proj_fwd"))
    done("mlstm_proj_fwd")
    grow = gcol[:, 0:16].T
    cell, y_ml, cs, ns, ms, mt = _ml_cell_fwd(q, k, v, gcol, grow, u, vec("ml_norm_g"), nh_ml, hosted("mlstm_cell_fwd"))
    done("mlstm_cell_fwd")
    x_new, y = _out_fwd(xl, y_rg, y_ml, gate, wl["w_out_g"], 0, t_big, hosted("out_proj_fwd"))
    done("out_proj_fwd")
    saved = dict(x=xl, u=u, hbf=hbf, h_rg=h_rg, y_rg=y_rg, q=q, k=k, v=v, gcol=gcol, grow=grow, cell=cell, y_ml=y_ml,
                 cs=cs, ns=ns, ms=ms, mt=mt, y=y, scale=scale, gate=gate, rg_gates=rg_gates, pre=pre)
    return x_new, saved


def _layer_bwd(l, dx, sv, wl, rep, plan, grads=None, split_last=False):
    s, d = dx.shape
    t_big, t_mid = _tile_for(s, 512), _tile_for(s, 256)
    nh_ml = rep["ml_b_if"].shape[1] // 2
    nd, _, _, w_cols = wl["w_in_g"].shape
    grads = {} if grads is None else grads
    vec = lambda name: _vec(rep["vecs"], l, VEC_TABLE.index(name))
    hosted = lambda name: plan.comms((name, l)) if plan else None
    done = lambda name: plan.done((name, l)) if plan else None
    dy_rg, dy_ml, gw_out, dgate = _out_bwd(dx, sv["gate"], sv["y"], sv["y_rg"], sv["y_ml"], wl["w_out_g"], 0, t_big,
                                           hosted("out_proj_bwd"))
    grads.update(w_out=gw_out)
    done("out_proj_bwd")
    dq, dk, dv, dgates, d_mlo, d_mlz, g_mlng = _ml_cell_bwd(
        dy_ml, sv["u"], sv["cell"], sv["q"], sv["k"], sv["v"], sv["gcol"], sv["grow"], sv["mt"], sv["cs"], sv["ns"],
        sv["ms"], vec("ml_norm_g"), nh_ml, hosted("mlstm_cell_bwd"))
    done("mlstm_cell_bwd")
    d_mlx, g_wq, g_wk, g_wv, g_wift, g_bif, g_mlcw, g_mlcb = _ml_pre_bwd(
        dq, dk, dv, dgates, sv["gcol"], sv["u"], sv["pre"], sv["q"], sv["k"], sv["v"], wl["ml_conv_w"],
        wl["w_qkv"][0], wl["w_qkv"][1], wl["w_qkv"][2], wl["wift_pad"], t_mid, hosted("mlstm_proj_bwd"))
    done("mlstm_proj_bwd")
    d_rgx, d_rgz, g_wa, g_wx, g_ba, g_bx, g_lam, g_rgcw, g_rgcb = _rg_bwd(
        dy_rg, sv["u"], sv["h_rg"], sv["rg_gates"], wl["rg_conv_w"], rep["rg_w_a_bf"][l], rep["rg_w_x_bf"][l],
        vec("rg_lambda"), t_mid, hosted("rglru_bwd"))
    grads.update(w_qkv=jnp.stack([g_wq, g_wk, g_wv]), rg_conv_w=g_rgcw[0:CONV_WIDTH], ml_conv_w=g_mlcw[0:CONV_WIDTH],
                 wif_t=g_wift[0:8], rg_w_a=g_wa, rg_w_x=g_wx)
    acc = dict(dgate=dgate, rg_conv_b=g_rgcb, rg_b_a=g_ba, rg_b_x=g_bx, rg_lambda=g_lam, ml_conv_b=g_mlcb,
               ml_b_if=g_bif, ml_norm_g=g_mlng)
    done("rglru_bwd")
    pieces = [d_rgx, d_rgz, d_mlx, d_mlo, d_mlz]
    grads.update(w_in=_in_bwd_w(pieces, sv["hbf"], w_cols, tuple(range(nd)), t_big, hosted("in_proj_bwd_w")))
    done("in_proj_bwd_w")
    n_tiles = s // t_mid
    counts = [n_tiles // 4, n_tiles - n_tiles // 4 - 1, 1] if split_last and n_tiles >= 4 else [n_tiles]
    in_args = (pieces, sv["x"], dx, vec("norm_g"), sv["scale"], wl["w_in_g"], 0, t_mid)
    res, at = None, 0
    for key, count in zip(("in_proj_bwd_x", "in_proj_bwd_x_rest", "in_proj_bwd_x_end"), counts):
        res = _in_bwd(*in_args, hosted(key), (at, count), res)
        done(key)
        at += count
    dx, dscale, dshift, g_ng = res
    acc.update(norm_g=g_ng, dshift=dshift, dscale=dscale)
    grads.update(acc=acc, dmod=jnp.concatenate([dshift[0:1], dscale[0:1], dgate[0:1]], axis=1))
    return dx, grads


def _local_step(x, c, target, wts):
    d = x.shape[1]
    nl = wts["w_in_g"].shape[1]
    mod, cact = _mod_call(c, wts["w_ada_g"], wts["b_ada"])
    wl = [dict(w_in_g=wts["w_in_g"][:, l:l + 1], w_out_g=wts["w_out_g"][:, l:l + 1], w_qkv=wts["w_qkv"][l],
               rg_conv_w=wts["rg_conv_w"][l], ml_conv_w=wts["ml_conv_w"][l], wif_pad=wts["wif_pad"][l],
               wift_pad=wts["wift_pad"][l], bif_pad=wts["bif_pad"][l]) for l in range(nl)]
    rep = dict(wts, vecs=_vec_table(wts))
    mod3 = mod.reshape(nl, 3, d)
    saved, xl = [], x
    for l in range(nl):
        xl, sv = _layer_fwd(l, xl, mod3, wl[l], rep, None)
        saved.append(sv)
    dx, loss_p, g_final = _loss_call(xl, wts["final_g"].reshape(1, -1), target, _tile_for(x.shape[0], 512))
    grads = [None] * nl
    for l in reversed(range(nl)):
        dx, grads[l] = _layer_bwd(l, dx, saved[l], wl[l], rep, None)
        grads[l]["w_ada"] = _ada_bwd_w(cact[0].reshape(d, 1), grads[l]["dmod"], wts["w_ada_g"].shape[0])
        grads[l]["b_ada"] = grads[l]["dmod"][0]
        grads[l].update({n: a[0] for n, a in grads[l]["acc"].items()})
        grads[l]["ml_b_if"] = grads[l]["ml_b_if"][0:8]
    return loss_p[0, 0], dx, grads, g_final[0]


def _full_qkv(qkv_g, d):
    nd, _, rows3, dh = qkv_g.shape
    nh = d // dh
    rsh = rows3 // (3 * nh)
    return qkv_g.reshape(nd, 3, nh, rsh, dh).transpose(1, 2, 0, 3, 4).reshape(3, nh, nd * rsh, dh)


def _small_weights(small, l, ml_b_if):
    nd = small.shape[0]
    sm = small[:, l]
    cw = sm[:, 0:2 * CONV_WIDTH].reshape(nd, 2, CONV_WIDTH, LANES).transpose(1, 2, 0, 3).reshape(2, CONV_WIDTH, nd * LANES)
    if_rows = (sm.shape[1] - 2 * CONV_WIDTH) * LANES // 8
    wif_t = sm[:, 2 * CONV_WIDTH:].reshape(nd, 8, if_rows).transpose(1, 0, 2).reshape(8, nd * if_rows)
    wift_pad = jnp.pad(wif_t, ((0, LANES - 8), (0, 0))).astype(BF16)
    return dict(rg_conv_w=cw[0], ml_conv_w=cw[1], wift_pad=wift_pad, wif_pad=wift_pad.T,
                bif_pad=jnp.pad(ml_b_if[l], (0, LANES - 8)).reshape(1, LANES))


def kernel(x, c, norm_g, w_ada, b_ada, w_in, rg_conv_w, rg_conv_b, rg_w_a, rg_b_a, rg_w_x, rg_b_x, rg_lambda, ml_conv_w, ml_conv_b, ml_w_q, ml_w_k, ml_w_v, ml_w_if, ml_b_if, ml_norm_g, w_out, final_g, loss_target, m_norm_g, m_w_ada, m_b_ada, m_w_in, m_rg_conv_w, m_rg_conv_b, m_rg_w_a, m_rg_b_a, m_rg_w_x, m_rg_b_x, m_rg_lambda, m_ml_conv_w, m_ml_conv_b, m_ml_w_q, m_ml_w_k, m_ml_w_v, m_ml_w_if, m_ml_b_if, m_ml_norm_g, m_w_out, m_final_g, v_norm_g, v_w_ada, v_b_ada, v_w_in, v_rg_conv_w, v_rg_conv_b, v_rg_w_a, v_rg_b_a, v_rg_w_x, v_rg_b_x, v_rg_lambda, v_ml_conv_w, v_ml_conv_b, v_ml_w_q, v_ml_w_k, v_ml_w_v, v_ml_w_if, v_ml_b_if, v_ml_norm_g, v_w_out, v_final_g):
    given = dict(locals())
    nl = w_in.shape[0]
    d = x.shape[2]
    rep = {n: given[n] for n in REPLICATED}
    rep.update(rg_w_a_bf=rg_w_a.astype(BF16), rg_w_x_bf=rg_w_x.astype(BF16))
    bf = lambda a: a.astype(BF16)

    def qkv_shard(prefix):
        return jnp.stack([given[prefix + "ml_w_q"], given[prefix + "ml_w_k"], given[prefix + "ml_w_v"]], axis=1).reshape(
            nl, -1, ml_w_q.shape[-1])

    def small_shard(prefix):
        return _small_pack(given[prefix + "rg_conv_w"], given[prefix + "ml_conv_w"], given[prefix + "ml_w_if"])

    plan = _Plan()
    qkv = qkv_shard("")
    first_ici = _gather_ici_comm([bf(w_in[0:1]), small_shard("")])
    condition = _exchange_comm([jnp.broadcast_to(c, (SUBLANES, d))], True)
    _run_comms([first_ici, condition], "gather_first")
    first_fwd = _gather_fwd_comm(first_ici.results)
    wcols = w_ada.shape[2]
    me = 4 * lax.axis_index("x") + 2 * lax.axis_index("y") + lax.axis_index("c")
    b_cols = jnp.pad(lax.dynamic_slice_in_dim(b_ada, me * wcols, wcols, axis=1), ((0, SUBLANES - nl), (0, 0)))
    mod_cols, cact_all = _ada_mod(condition.results[0], w_ada, b_cols, [first_fwd])
    w_in_first, small = first_fwd.results
    wl = [_small_weights(small, l, ml_b_if) for l in range(nl)]
    wl[0]["w_in_g"] = w_in_first

    def gather_behind(arrs, ici_host, fwd_host, then):
        ici = _gather_ici_comm(arrs)

        def pass_on():
            fwd = _gather_fwd_comm(ici.results)
            plan.host(fwd_host, fwd, lambda: then(fwd.results))

        plan.host(ici_host, ici, pass_on)

    def got_out(l):
        return lambda r: wl[l].update(w_out_g=r[0], w_qkv=_full_qkv(r[1], d))

    gather_behind([bf(w_out[0:1]), bf(qkv[0:1])], ("in_proj_fwd", 0), ("rglru_fwd", 0), got_out(0))
    for l in range(1, nl):
        gather_behind([bf(w_in[l:l + 1])], ("rglru_fwd", l - 1), ("mlstm_proj_fwd", l - 1),
                      lambda r, l=l: wl[l].update(w_in_g=r[0]))
        gather_behind([bf(w_out[l:l + 1]), bf(qkv[l:l + 1])], ("mlstm_cell_fwd", l - 1), ("out_proj_fwd", l - 1), got_out(l))

    mod_blocks = _exchange([mod_cols], False, "scatter_modulation")[0]
    mod3 = mod_blocks[:, 0:nl].transpose(1, 0, 2).reshape(nl, 3, d)
    rep["vecs"] = _vec_table(rep)
    saved, xl = [], x[0]
    for l in range(nl):
        xl, sv = _layer_fwd(l, xl, mod3, wl[l], rep, plan)
        saved.append(sv)
    grad_x, loss_p, g_final = _loss_call(xl, final_g.reshape(1, -1), loss_target[0], _tile_for(xl.shape[0], 512))

    keys = ("w_in", "w_out", "w_qkv", "small")
    parity = lax.axis_index("c").astype(jnp.int32).reshape(1)
    grads, recv = [None] * nl, [None] * nl

    def parts_of(g):
        return [g["w_in"], g["w_out"], _qkv_slots(g["w_qkv"], N_DEV), _small_slots(g)]

    def pair_sums(l, parts, other):
        return [_pair_sum(a, o, parity, "pair_sum_%s_layer%d" % (key, l)) for key, a, o in zip(keys, parts, other)]

    def reduce_behind(l, host_layer):
        parts = parts_of(grads[l])
        swap = _core_swap_comm(parts)

        def summed():
            sums = pair_sums(l, parts, swap.results)
            big = _chip_swap_comm([sums[0]])
            rest = _chip_swap_comm(sums[1:])
            plan.host(("mlstm_cell_bwd", host_layer), big)
            plan.host(("rglru_bwd", host_layer), rest, lambda: recv.__setitem__(l, big.results + rest.results))

        plan.host(("out_proj_bwd", host_layer), swap, summed)

    first, own = {}, {}

    def reduce_own(names, parts_fn, ready_key, swap_key, chip_key):
        def go():
            parts = parts_fn()
            swap = _core_swap_comm(parts)

            def summed():
                sums = [_pair_sum(a, o, parity, "pair_sum_%s_layer0" % n) for n, a, o in zip(names, parts, swap.results)]
                chip = _chip_swap_comm(sums)
                plan.host(chip_key, chip, lambda: own.update(zip(names, chip.results)))

            plan.host(swap_key, swap, summed)

        plan.after.setdefault(ready_key, []).append(go)

    reduce_own(["w_out"], lambda: [first["w_out"]], ("out_proj_bwd", 0), ("mlstm_cell_bwd", 0), ("mlstm_proj_bwd", 0))
    reduce_own(["w_qkv", "small"], lambda: [_qkv_slots(first["w_qkv"], N_DEV), _small_slots(first)],
               ("rglru_bwd", 0), ("in_proj_bwd_w", 0), ("in_proj_bwd_x", 0))
    reduce_own(["w_in"], lambda: [first["w_in"]], ("in_proj_bwd_w", 0), ("in_proj_bwd_x", 0), ("in_proj_bwd_x_rest", 0))

    for l in reversed(range(nl)):
        if l > 0:
            grad_x, grads[l] = _layer_bwd(l, grad_x, saved[l], wl[l], rep, plan)
            reduce_behind(l, l - 1)
        else:
            grad_x, grads[l] = _layer_bwd(l, grad_x, saved[l], wl[l], rep, plan, first, True)
    plan.flush()
    recv[0] = [own[key] for key in keys]

    shard = {p: dict(w_in=given[p + "w_in"], w_out=given[p + "w_out"], w_qkv=qkv_shard(p), small=small_shard(p))
             for p in ("", "m_", "v_")}
    res = {}
    for ki, key in enumerate(keys):
        res[key] = _reduce_adam([recv[l][ki] for l in range(nl)], shard[""][key], shard["m_"][key], shard["v_"][key],
                                "reduce_adam_" + key)

    dmods = jnp.concatenate([grads[l]["dmod"] for l in range(nl)], axis=0)
    dmod_blocks = jnp.pad(dmods.reshape(nl, N_DEV, wcols).transpose(1, 0, 2), ((0, 0), (0, SUBLANES - nl), (0, 0)))
    dmod_all = _exchange([dmod_blocks], False, "scatter_dmod")[0][:, 0:nl].transpose(1, 0, 2)
    res["w_ada"] = _ada_grad_adam(cact_all.T, dmod_all, w_ada, m_w_ada, v_w_ada)

    widen = lambda a: jnp.pad(a, ((0, 0), (0, d - a.shape[1])))
    rows = [widen(grads[l]["acc"][n][0:1]) for l in range(nl) for n in REP_ROWS] + [g_final[0:1], widen(loss_p[0:1])]
    vp = jnp.concatenate(rows + [jnp.zeros(((-len(rows)) % ROW_ALIGN, d), F32)], axis=0)
    mp = jnp.stack([jnp.stack([grads[l]["rg_w_a"], grads[l]["rg_w_x"]]) for l in range(nl)]).reshape(-1, LANES).astype(BF16)
    got = _exchange([vp.reshape(N_DEV, -1, d), mp.reshape(N_DEV, -1, LANES)], False, "reduce_scatter_replicated")
    vp_r, mp_r = _exchange(_sum_parts(got, "sum_replicated"), True, "gather_replicated")
    vp_r, mp_r = vp_r.reshape(-1, d), mp_r.reshape(-1, LANES)
    lanes = lambda a: jnp.pad(a, ((0, 0), (0, LANES - a.shape[1])))
    shaped = dict(ml_b_if=lanes, final_g=lambda a: a.reshape(1, d))
    names = [n for n in REPLICATED if n != "b_ada"] + ["b_ada"]
    rep_res = _adam_replicated(vp_r, mp_r, {n: tuple(shaped.get(n, lambda a: a)(given[p + n]) for p in ("", "m_", "v_"))
                                            for n in names}, nl)
    unshaped = dict(ml_b_if=lambda a: a[:, 0:ml_b_if.shape[1]], final_g=lambda a: a.reshape(d))
    rep_out = [{n: unshaped.get(n, lambda a: a)(rep_res[n][kind]) for n in names} for kind in range(4)]
    loss = vp_r[nl * len(REP_ROWS) + 1, 0]

    if_rows = ml_w_if.shape[1]
    order = ("norm_g", "w_ada", "b_ada", "w_in", "rg_conv_w", "rg_conv_b", "rg_w_a", "rg_b_a", "rg_w_x", "rg_b_x",
             "rg_lambda", "ml_conv_w", "ml_conv_b", "ml_w_q", "ml_w_k", "ml_w_v", "ml_w_if", "ml_b_if", "ml_norm_g",
             "w_out", "final_g")
    outs = [loss, grad_x[None]]
    for kind in range(4):
        qkv_k = res["w_qkv"][kind].reshape((nl, 3) + ml_w_q.shape[1:])
        rg_cw, ml_cw, wif = _small_unpack(res["small"][kind], if_rows)
        sharded = dict(w_ada=res["w_ada"][kind], w_in=res["w_in"][kind], w_out=res["w_out"][kind], ml_w_q=qkv_k[:, 0],
                       ml_w_k=qkv_k[:, 1], ml_w_v=qkv_k[:, 2], rg_conv_w=rg_cw, ml_conv_w=ml_cw, ml_w_if=wif)
        for n in order:
            outs.append(sharded[n] if n in sharded else rep_out[kind][n])
    return tuple(outs)
```

```python
import functools

import jax
import jax.numpy as jnp
from jax import lax
from jax.experimental import pallas as pl
from jax.experimental.pallas import tpu as pltpu

F32 = jnp.float32
BF16 = jnp.bfloat16
MESH_AXES = ("x", "y", "c")
N_DEV = 8
EPS = 1e-6
RG_C = 8.0
ML_CHUNK = 128
CONV_WIDTH = 4
ADAM_LR = 0.001
ADAM_B1 = 0.9
ADAM_B2 = 0.999
ADAM_EPS = 1e-08
ADAM_WD = 0.01
ADAM_STEP = 10
NEG_BIG = -1e30
LANES = 128
SUBLANES = 8
VMEM_LIMIT = 56 * 1024 * 1024
HI = lax.Precision.HIGHEST


def _params(n_grid):
    return pltpu.CompilerParams(dimension_semantics=("arbitrary",) * n_grid, vmem_limit_bytes=VMEM_LIMIT)


def _mm(a, b):
    return jnp.dot(a.astype(BF16), b.astype(BF16), preferred_element_type=F32)


def _mm_nt(a, b):
    return lax.dot_general(a.astype(BF16), b.astype(BF16), (((1,), (1,)), ((), ())), preferred_element_type=F32)


def _mm_tn(a, b):
    return lax.dot_general(a.astype(BF16), b.astype(BF16), (((0,), (0,)), ((), ())), preferred_element_type=F32)


def _mm_hi(a, b):
    return jnp.dot(a, b, precision=HI, preferred_element_type=F32)


def _sigmoid(x):
    return 1.0 / (1.0 + jnp.exp(-x))


def _softplus(x):
    return jnp.maximum(x, 0.0) + jnp.log(1.0 + jnp.exp(-jnp.abs(x)))


def _neg_expm1(x):
    poly = -x * (1.0 + x * (0.5 + x * (1.0 / 6.0 + x * (1.0 / 24.0 + x * (1.0 / 120.0)))))
    return jnp.where(jnp.abs(x) < 0.05, poly, 1.0 - jnp.exp(x))


def _iota(shape, dim):
    return lax.broadcasted_iota(jnp.int32, shape, dim)


def _colsum(x):
    return jnp.sum(x, axis=0, keepdims=True)


def _rowsum(x):
    return jnp.sum(x, axis=1, keepdims=True)


def _col(x, j):
    return _rowsum(jnp.where(_iota(x.shape, 1) == j, x, 0.0))


def _row(x, j):
    return _colsum(jnp.where(_iota(x.shape, 0) == j, x, 0.0))


def _shift_down(x, j, prev8):
    if j == 0:
        return x
    t = x.shape[0]
    main = jnp.where(_iota(x.shape, 0) >= j, pltpu.roll(x, j, 0), 0.0)
    fix = jnp.where(_iota(prev8.shape, 0) < j, pltpu.roll(prev8, j, 0), 0.0)
    return jnp.concatenate([main[0:SUBLANES] + fix, main[SUBLANES:t]], axis=0)


def _shift_up(x, j, next8):
    if j == 0:
        return x
    t = x.shape[0]
    main = jnp.where(_iota(x.shape, 0) < t - j, pltpu.roll(x, t - j, 0), 0.0)
    fix = jnp.where(_iota(next8.shape, 0) >= SUBLANES - j, pltpu.roll(next8, SUBLANES - j, 0), 0.0)
    return jnp.concatenate([main[0:t - SUBLANES], main[t - SUBLANES:t] + fix], axis=0)


def _conv(x, prev8, w_ref):
    y = w_ref[CONV_WIDTH - 1:CONV_WIDTH, :] * x
    for j in range(1, CONV_WIDTH):
        y = y + w_ref[CONV_WIDTH - 1 - j:CONV_WIDTH - j, :] * _shift_down(x, j, prev8)
    return y


def _conv_bwd(dy, x, next8, w_ref, gw_ref):
    dx = None
    for j in range(CONV_WIDTH):
        k = CONV_WIDTH - 1 - j
        up = _shift_up(dy, j, next8)
        gw_ref[k:k + 1, :] += _colsum(up * x)
        term = w_ref[k:k + 1, :] * up
        dx = term if dx is None else dx + term
    return dx


def _scan_into(a, b, carry, out_ref, reverse):
    t, c = a.shape
    groups = t // SUBLANES
    a3 = a.reshape(groups, SUBLANES, c)
    b3 = b.reshape(groups, SUBLANES, c)
    sub = _iota(a3.shape, 1)
    for step in (1, 2, 4):
        keep = sub < SUBLANES - step if reverse else sub >= step
        shift = SUBLANES - step if reverse else step
        a_s = jnp.where(keep, pltpu.roll(a3, shift, 1), 1.0)
        b_s = jnp.where(keep, pltpu.roll(b3, shift, 1), 0.0)
        b3 = a3 * b_s + b3
        a3 = a3 * a_s
    for g in (reversed(range(groups)) if reverse else range(groups)):
        rows = slice(g * SUBLANES, (g + 1) * SUBLANES)
        out_ref[rows, :] = b3[g] + a3[g] * carry
        edge = g * SUBLANES if reverse else (g + 1) * SUBLANES - 1
        carry = out_ref[edge:edge + 1, :]


def _blockdiag(x, w_ref, transpose_w=False):
    nh, dh, _ = w_ref.shape
    outs = []
    for h in range(nh):
        xs = x[:, h * dh:(h + 1) * dh]
        outs.append(_mm_nt(xs, w_ref[h]) if transpose_w else _mm(xs, w_ref[h]))
    return jnp.concatenate(outs, axis=1)


def _rg_gates(xc, wa_ref, ba_ref, wx_ref, bx_ref, lam_ref):
    r = _sigmoid(_blockdiag(xc, wa_ref) + ba_ref[...])
    ig = _sigmoid(_blockdiag(xc, wx_ref) + bx_ref[...])
    sp = _softplus(-lam_ref[...])
    log_a = -RG_C * r * sp
    a = jnp.exp(log_a)
    beta = jnp.sqrt(_neg_expm1(2.0 * log_a))
    return r, ig, sp, a, beta


def _bcast8(row):
    return jnp.broadcast_to(row, (SUBLANES, row.shape[1]))


def _full(shape):
    nd = len(shape)
    return pl.BlockSpec(shape, lambda *_: (0,) * nd)


class _Comm:
    def __init__(self, arrays, out_shapes, sems, start, finish, aliases=()):
        self.arrays, self.out_shapes, self.sems = list(arrays), list(out_shapes), list(sems)
        self.start, self.finish, self.aliases = start, finish, tuple(aliases)
        self.results = None


class _RowOf:
    def __init__(self, ref, k):
        self.ref, self.k = ref, k

    def __getitem__(self, idx):
        cols = slice(None) if idx is Ellipsis else idx[1]
        return self.ref[0, self.k:self.k + 1, cols]


def _vec(table, layer, k):
    return ("row", table, layer, k)


def _is_row(arg):
    return isinstance(arg, tuple) and len(arg) == 4 and arg[0] == "row"


def _call(body, comms, *, name, grid, in_specs, out_specs, out_shape, args, scratch_shapes=(), aliases=None, prefetch=()):
    comms = [cm for cm in (comms or []) if cm is not None]
    rows = {i: a[3] for i, a in enumerate(args) if _is_row(a)}
    in_specs = [pl.BlockSpec((1,) + a[1].shape[1:], functools.partial(lambda layer, *_: (layer, 0, 0), a[2]))
                if _is_row(a) else sp for a, sp in zip(args, in_specs)]
    args = tuple(a[1] if _is_row(a) else a for a in args)
    n_in, n_out, n_sc = len(args), len(out_shape), len(scratch_shapes)
    c_arrays = [a for cm in comms for a in cm.arrays]
    c_outs = [o for cm in comms for o in cm.out_shapes]
    c_sems = [sm for cm in comms for sm in cm.sems]
    aliases, a_at, o_at = dict(aliases or {}), n_in, n_out
    for cm in comms:
        for (i, j) in cm.aliases:
            aliases[a_at + i] = o_at + j
        a_at += len(cm.arrays)
        o_at += len(cm.out_shapes)

    def wrapped(*refs):
        pre, refs = refs[:len(prefetch)], refs[len(prefetch):]
        ins, c_in = refs[:n_in], refs[n_in:n_in + len(c_arrays)]
        ins = [_RowOf(r, rows[i]) if i in rows else r for i, r in enumerate(ins)]
        at = n_in + len(c_arrays)
        outs, c_out = refs[at:at + n_out], refs[at + n_out:at + n_out + len(c_outs)]
        at += n_out + len(c_outs)
        scr, sems = refs[at:at + n_sc], refs[at + n_sc:]
        views, ia, io, isem = [], 0, 0, 0
        for cm in comms:
            views.append((c_in[ia:ia + len(cm.arrays)], c_out[io:io + len(cm.out_shapes)], sems[isem:isem + len(cm.sems)]))
            ia, io, isem = ia + len(cm.arrays), io + len(cm.out_shapes), isem + len(cm.sems)
        if comms:
            @pl.when(pl.program_id(0) == 0)
            def _():
                for cm, view in zip(comms, views):
                    cm.start(*view)

        body(*pre, *ins, *outs, *scr)
        if comms:
            @pl.when(pl.program_id(0) == grid[0] - 1)
            def _():
                for cm, view in zip(comms, views):
                    cm.finish(*view)

    hbm = pl.BlockSpec(memory_space=pl.ANY)
    specs = dict(grid=grid, in_specs=list(in_specs) + [hbm] * len(c_arrays), out_specs=list(out_specs) + [hbm] * len(c_outs),
                 scratch_shapes=list(scratch_shapes) + c_sems)
    if prefetch:
        specs = dict(grid_spec=pltpu.PrefetchScalarGridSpec(num_scalar_prefetch=len(prefetch), **specs))
        aliases = {i + len(prefetch): o for i, o in aliases.items()}
    res = pl.pallas_call(
        wrapped, name=name, out_shape=list(out_shape) + c_outs, input_output_aliases=aliases,
        compiler_params=_params(len(grid)), **specs,
    )(*prefetch, *args, *c_arrays)
    at = n_out
    for cm in comms:
        cm.results = list(res[at:at + len(cm.out_shapes)])
        at += len(cm.out_shapes)
    return list(res[:n_out])


def _mod_call(c, w_ada_g, b_ada):
    nd, nl, d, w = w_ada_g.shape

    def body(c_ref, w_ref, b_ref, mod_ref, cact_ref):
        cv = c_ref[...]
        ca = _bcast8(cv * _sigmoid(cv))
        cact_ref[...] = ca
        mod_ref[0, 0] = _mm(ca, w_ref[0, 0]) + b_ref[0, 0]

    mod, cact = pl.pallas_call(
        body, name="adaln_mod", grid=(nl, nd),
        in_specs=[_full((1, d)),
                  pl.BlockSpec((1, 1, d, w), lambda l, j: (j, l, 0, 0)),
                  pl.BlockSpec((1, 1, 1, w), lambda l, j: (l, j, 0, 0))],
        out_specs=[pl.BlockSpec((1, 1, SUBLANES, w), lambda l, j: (l, j, 0, 0)), _full((SUBLANES, d))],
        out_shape=[jax.ShapeDtypeStruct((nl, nd, SUBLANES, w), F32), jax.ShapeDtypeStruct((SUBLANES, d), F32)],
        compiler_params=_params(2),
    )(c, w_ada_g, b_ada.reshape(nl, nd, 1, w))
    return mod[:, :, 0, :].reshape(nl, nd * w), cact


def _join_columns(w_ref, wcat):
    nd, _, _, w = w_ref.shape

    @pl.when(pl.program_id(0) == 0)
    def _():
        for j in range(nd):
            wcat[:, j * w:(j + 1) * w] = w_ref[j, 0]


def _in_fwd(x, ng, scale, shift, w_in_g, layer, tile, comms=None):
    s, d = x.shape
    nd, _, _, w = w_in_g.shape

    def body(x_ref, ng_ref, sc_ref, sh_ref, w_ref, u_ref, h_ref, wcat):
        _join_columns(w_ref, wcat)
        xv = x_ref[...]
        rs = lax.rsqrt(jnp.mean(xv * xv, axis=1, keepdims=True) + EPS)
        hb = (xv * rs * ng_ref[...] * (1.0 + sc_ref[...]) + sh_ref[...]).astype(BF16)
        h_ref[...] = hb
        u_ref[...] = jnp.dot(hb, wcat[...], preferred_element_type=F32)

    return _call(
        body, comms, name="in_proj_fwd", grid=(s // tile,),
        in_specs=[pl.BlockSpec((tile, d), lambda i: (i, 0)), _full((1, d)), _full((1, d)), _full((1, d)),
                  pl.BlockSpec((nd, 1, d, w), lambda i: (0, layer, 0, 0), pipeline_mode=pl.Buffered(1))],
        out_specs=[pl.BlockSpec((tile, nd * w), lambda i: (i, 0)), pl.BlockSpec((tile, d), lambda i: (i, 0))],
        out_shape=[jax.ShapeDtypeStruct((s, nd * w), F32), jax.ShapeDtypeStruct((s, d), BF16)],
        scratch_shapes=[pltpu.VMEM((d, nd * w), BF16)],
        args=(x, ng, scale, shift, w_in_g))


def _in_fwd_gathering(x, ng, scale, shift, w_shard, order, comms=None):
    s, d = x.shape
    w = w_shard.shape[1]
    rows = min(512, s)

    def body(order_ref, x_ref, ng_ref, sc_ref, sh_ref, wsh_ref, u_ref, h_ref, land_ref, wbuf, xbuf, send_sems, recv_sems,
             dma_sems):
        k = pl.program_id(0)
        mx, my, mc, sibling, chips = _mesh_place()
        me = (mx, my, mc)
        peers = [sibling] + [(*chip, mc) for chip in chips]
        blocks = peers + [(*chip, 1 - mc) for chip in chips]

        def own_send(j):
            return _remote(wsh_ref, land_ref.at[_slot(me)], send_sems, recv_sems, j, peers[j])

        def landing(j):
            return _remote(wsh_ref, land_ref.at[_slot(blocks[j])], send_sems, recv_sems, j, sibling)

        def pass_on(j):
            at = land_ref.at[_slot((*chips[j - 4], mc))]
            return _remote(at, at, send_sems, recv_sems, j, sibling)

        def load(src, slot):
            cp = pltpu.make_async_copy(src, wbuf.at[slot], dma_sems.at[slot])
            cp.start()
            cp.wait()

        @pl.when(k == 0)
        def _():
            for j in range(4):
                own_send(j).start()
            keep = pltpu.make_async_copy(wsh_ref, land_ref.at[_slot(me)], dma_sems.at[2])
            keep.start()
            fetch = lambda i: pltpu.make_async_copy(x_ref.at[i * rows:(i + 1) * rows, :], xbuf.at[i % 2], dma_sems.at[3 + i % 2])
            fetch(0).start()
            for i in range(s // rows):
                if (i + 1) * rows < s:
                    fetch(i + 1).start()
                fetch(i).wait()
                xv = xbuf[i % 2]
                rs = lax.rsqrt(jnp.mean(xv * xv, axis=1, keepdims=True) + EPS)
                h_ref[i * rows:(i + 1) * rows, :] = (xv * rs * ng_ref[...] * (1.0 + sc_ref[...]) + sh_ref[...]).astype(BF16)
            load(wsh_ref, 0)
            keep.wait()

        for step in range(1, N_DEV):
            @pl.when(k == step)
            def _(step=step):
                j = step - 1
                landing(j).wait_recv()
                if 1 <= j <= 3:
                    pass_on(j + 3).start()
                load(land_ref.at[_slot(blocks[j])], step % 2)

        u_ref[...] = jnp.dot(h_ref[...], wbuf[k % 2], preferred_element_type=F32)

        @pl.when(k == N_DEV - 1)
        def _():
            for j in range(4):
                own_send(j).wait_send()
            for j in range(4, N_DEV - 1):
                pass_on(j).wait_send()

    one = pltpu.SemaphoreType.DMA
    u, hbf, land = _call(
        body, comms, name="in_proj_fwd", grid=(N_DEV,), prefetch=(order,),
        in_specs=[pl.BlockSpec(memory_space=pl.ANY), _full((1, d)), _full((1, d)), _full((1, d)),
                  pl.BlockSpec(memory_space=pl.ANY)],
        out_specs=[pl.BlockSpec((s, w), lambda k, o: (0, o[k])),
                   pl.BlockSpec((s, d), lambda k, o: (0, 0), pipeline_mode=pl.Buffered(1)), pl.BlockSpec(memory_space=pl.ANY)],
        out_shape=[jax.ShapeDtypeStruct((s, N_DEV * w), F32), jax.ShapeDtypeStruct((s, d), BF16),
                   jax.ShapeDtypeStruct((N_DEV, d, w), BF16)],
        scratch_shapes=[pltpu.VMEM((2, d, w), BF16), pltpu.VMEM((2, rows, d), F32), one((N_DEV - 1,)), one((N_DEV - 1,)),
                        one((5,))],
        args=(x, ng, scale, shift, w_shard))
    return u, hbf, land


def _rg_fwd(u, d, conv_w, conv_b, w_a, b_a, w_x, b_x, lam, tile, comms=None):
    s = u.shape[0]

    def body(x_ref, z_ref, cw_ref, cb_ref, wa_ref, ba_ref, wx_ref, bx_ref, lam_ref,
             h_ref, y_ref, xc_ref, r_ref, i_ref, a_ref, beta_ref, prev8, hcar):
        @pl.when(pl.program_id(0) == 0)
        def _():
            prev8[...] = jnp.zeros_like(prev8)
            hcar[...] = jnp.zeros_like(hcar)

        x = x_ref[...]
        xc = _conv(x, prev8[...], cw_ref) + cb_ref[...]
        prev8[...] = x[tile - SUBLANES:tile, :]
        r, ig, _, a, beta = _rg_gates(xc, wa_ref, ba_ref, wx_ref, bx_ref, lam_ref)
        xc_ref[...] = xc
        r_ref[...] = r
        i_ref[...] = ig
        a_ref[...] = a
        beta_ref[...] = beta
        _scan_into(a, beta * ig * xc, hcar[SUBLANES - 1:SUBLANES, :], h_ref, False)
        h = h_ref[...]
        hcar[...] = h[tile - SUBLANES:tile, :]
        z = z_ref[...]
        y_ref[...] = (h * z * _sigmoid(z)).astype(BF16)

    vec = _full((1, d))
    return _call(
        body, comms, name="rglru_fwd", grid=(s // tile,),
        in_specs=[pl.BlockSpec((tile, d), lambda i: (i, 0)), pl.BlockSpec((tile, d), lambda i: (i, 1)),
                  _full(conv_w.shape), vec, _full(w_a.shape), vec, _full(w_x.shape), vec, vec],
        out_specs=[pl.BlockSpec((tile, d), lambda i: (i, 0))] * 7,
        out_shape=[jax.ShapeDtypeStruct((s, d), F32), jax.ShapeDtypeStruct((s, d), BF16)] + [jax.ShapeDtypeStruct((s, d), F32)] * 5,
        scratch_shapes=[pltpu.VMEM((SUBLANES, d), F32), pltpu.VMEM((SUBLANES, d), F32)],
        args=(u, u, conv_w, conv_b, w_a, b_a, w_x, b_x, lam))


def _ml_pre(u, d, conv_w, conv_b, w_q, w_k, w_v, wif, bif, tile, comms=None):
    s = u.shape[0]
    nh = w_q.shape[0]

    def body(x_ref, cw_ref, cb_ref, wq_ref, wk_ref, wv_ref, wif_ref, bif_ref, q_ref, k_ref, v_ref, g_ref, pre_ref, prev8):
        @pl.when(pl.program_id(0) == 0)
        def _():
            prev8[...] = jnp.zeros_like(prev8)

        x = x_ref[...]
        pre = _conv(x, prev8[...], cw_ref) + cb_ref[...]
        prev8[...] = x[tile - SUBLANES:tile, :]
        xc = pre * _sigmoid(pre)
        q = _blockdiag(xc, wq_ref)
        k = _blockdiag(xc, wk_ref)
        v = _blockdiag(x, wv_ref)
        pre_ref[...] = pre
        q_ref[...] = q
        k_ref[...] = k
        v_ref[...] = v
        g = _mm(q, wif_ref[0:d, :]) + _mm(k, wif_ref[d:2 * d, :]) + _mm(v, wif_ref[2 * d:3 * d, :]) + bif_ref[...]
        lane = _iota(g.shape, 1)
        gl = jnp.where(lane < 4, g, jnp.where(lane < 8, -_softplus(-g), 0.0))
        tri = jnp.where(_iota((ML_CHUNK, ML_CHUNK), 1) <= _iota((ML_CHUNK, ML_CHUNK), 0), 1.0, 0.0)
        cums = [_mm_hi(tri, gl[c * ML_CHUNK:(c + 1) * ML_CHUNK, :]) for c in range(tile // ML_CHUNK)]
        cum = cums[0] if len(cums) == 1 else jnp.concatenate(cums, axis=0)
        g_ref[...] = gl + jnp.where((lane >= 8) & (lane < 12), pltpu.roll(cum, 4, 1), 0.0)

    vec = _full((1, d))
    return _call(
        body, comms, name="mlstm_proj_fwd", grid=(s // tile,),
        in_specs=[pl.BlockSpec((tile, d), lambda i: (i, 2)), _full(conv_w.shape), vec,
                  _full(w_q.shape), _full(w_k.shape), _full(w_v.shape), _full(wif.shape), _full((1, LANES))],
        out_specs=[pl.BlockSpec((tile, d), lambda i: (i, 0))] * 3 + [pl.BlockSpec((tile, LANES), lambda i: (i, 0)),
                                                                     pl.BlockSpec((tile, d), lambda i: (i, 0))],
        out_shape=[jax.ShapeDtypeStruct((s, d), F32)] * 3 + [jax.ShapeDtypeStruct((s, LANES), F32),
                                                             jax.ShapeDtypeStruct((s, d), F32)],
        scratch_shapes=[pltpu.VMEM((SUBLANES, d), F32)],
        args=(u, conv_w, conv_b, w_q, w_k, w_v, wif, bif))


def _cell_chunk(h, nh, q_ref, k_ref, v_ref, gc, gr, m_prev, c_h, n_h, m_t=None):
    lc = ML_CHUNK
    dh = q_ref.shape[1] // nh
    sl = slice(h * dh, (h + 1) * dh)
    qh = q_ref[:, sl]
    kh = k_ref[:, sl] * (dh ** -0.5)
    vh = v_ref[:, sl]
    li_c = _col(gc, h)
    b_c = _col(gc, 8 + h)
    lib_r = _row(gr, h) - _row(gr, 8 + h)
    b_last = _colsum(jnp.where(_iota((lc, 1), 0) == lc - 1, b_c, 0.0))
    causal = _iota((lc, lc), 1) <= _iota((lc, lc), 0)
    dmat = jnp.where(causal, b_c + lib_r, NEG_BIG)
    m_inter = b_c + m_prev
    if m_t is None:
        m_t = jnp.maximum(m_inter, jnp.max(dmat, axis=1, keepdims=True))
    w_intra = jnp.exp(dmat - m_t)
    w_inter = jnp.exp(m_inter - m_t)
    amat = _mm_nt(qh, kh)
    smat = amat * w_intra
    qc = _mm(qh, c_h)
    qn = _rowsum(qh * n_h)
    den = _rowsum(smat) + w_inter * qn
    gst = b_last - b_c + li_c
    m_new = jnp.maximum(b_last + m_prev, jnp.max(gst, axis=0, keepdims=True))
    w_state = jnp.exp(gst - m_new)
    decay = jnp.exp(b_last + m_prev - m_new)
    return dict(sl=sl, qh=qh, kh=kh, vh=vh, m_t=m_t, w_intra=w_intra, w_inter=w_inter, smat=smat, qc=qc, qn=qn,
                den=den, m_new=m_new, w_state=w_state, decay=decay)


def _ml_cell_fwd(q, k, v, gcol, grow, u, ng, nh, comms=None):
    s, d = q.shape
    lc = ML_CHUNK
    nc = s // lc
    dh = d // nh

    def body(q_ref, k_ref, v_ref, gc_ref, gr_ref, o_ref, z_ref, ng_ref,
             cell_ref, y_ref, cs_ref, ns_ref, ms_ref, mt_ref, c_sc, n_sc, m_sc):
        @pl.when(pl.program_id(0) == 0)
        def _():
            c_sc[...] = jnp.zeros_like(c_sc)
            n_sc[...] = jnp.zeros_like(n_sc)
            m_sc[...] = jnp.zeros_like(m_sc)

        gc = gc_ref[...]
        gr = gr_ref[...]
        lane = _iota((lc, LANES), 1)
        mt_acc = jnp.zeros((lc, LANES), F32)
        for h in range(nh):
            c_h = c_sc[h]
            n_h = n_sc[h, 0:1, :]
            m_prev = jnp.max(m_sc[h, 0:1, :], axis=1, keepdims=True)
            cs_ref[0, h] = c_h
            ns_ref[0, h] = n_sc[h]
            ms_ref[0, h] = m_sc[h]
            t = _cell_chunk(h, nh, q_ref, k_ref, v_ref, gc, gr, m_prev, c_h, n_h)
            sl = t["sl"]
            num = _mm(t["smat"], t["vh"]) + t["w_inter"] * t["qc"]
            cell_h = num / jnp.maximum(jnp.abs(t["den"]), jnp.exp(-t["m_t"]))
            mt_acc = jnp.where(lane == h, t["m_t"], mt_acc)
            kw = t["kh"] * t["w_state"]
            c_sc[h] = t["decay"] * c_h + _mm_tn(kw, t["vh"])
            n_sc[h] = _bcast8(t["decay"] * n_h + _colsum(kw))
            m_sc[h] = jnp.broadcast_to(t["m_new"], (SUBLANES, LANES))
            hg = _sigmoid(o_ref[:, sl]) * cell_h
            hn = hg * lax.rsqrt(jnp.mean(hg * hg, axis=1, keepdims=True) + EPS)
            z = z_ref[:, sl]
            cell_ref[:, sl] = cell_h
            y_ref[:, sl] = (hn * ng_ref[:, sl] * z * _sigmoid(z)).astype(BF16)
        mt_ref[...] = mt_acc

    tok = pl.BlockSpec((lc, d), lambda c: (c, 0))
    return _call(
        body, comms, name="mlstm_cell_fwd", grid=(nc,),
        in_specs=[tok, tok, tok, pl.BlockSpec((lc, LANES), lambda c: (c, 0)), pl.BlockSpec((16, lc), lambda c: (0, c)),
                  pl.BlockSpec((lc, d), lambda c: (c, 3)), pl.BlockSpec((lc, d), lambda c: (c, 4)), _full((1, d))],
        out_specs=[tok, tok, pl.BlockSpec((1, nh, dh, dh), lambda c: (c, 0, 0, 0)),
                   pl.BlockSpec((1, nh, SUBLANES, dh), lambda c: (c, 0, 0, 0)),
                   pl.BlockSpec((1, nh, SUBLANES, LANES), lambda c: (c, 0, 0, 0)),
                   pl.BlockSpec((lc, LANES), lambda c: (c, 0))],
        out_shape=[jax.ShapeDtypeStruct((s, d), F32), jax.ShapeDtypeStruct((s, d), BF16),
                   jax.ShapeDtypeStruct((nc, nh, dh, dh), F32), jax.ShapeDtypeStruct((nc, nh, SUBLANES, dh), F32),
                   jax.ShapeDtypeStruct((nc, nh, SUBLANES, LANES), F32), jax.ShapeDtypeStruct((s, LANES), F32)],
        scratch_shapes=[pltpu.VMEM((nh, dh, dh), F32), pltpu.VMEM((nh, SUBLANES, dh), F32),
                        pltpu.VMEM((nh, SUBLANES, LANES), F32)],
        args=(q, k, v, gcol, grow, u, u, ng))


def _out_fwd(x, y_rg, y_ml, gate, w_out_g, layer, tile, comms=None):
    s, d = x.shape
    nd, _, r, _ = w_out_g.shape

    def body(x_ref, yr_ref, ym_ref, g_ref, w_ref, xn_ref, y_ref):
        ycat = jnp.concatenate([yr_ref[...].astype(BF16), ym_ref[...].astype(BF16)], axis=1)
        acc = jnp.dot(ycat, w_ref[...].reshape(nd * r, d), preferred_element_type=F32)
        y_ref[...] = acc
        xn_ref[...] = x_ref[...] + g_ref[...] * acc

    tok = pl.BlockSpec((tile, d), lambda i: (i, 0))
    return _call(
        body, comms, name="out_proj_fwd", grid=(s // tile,),
        in_specs=[tok, tok, tok, _full((1, d)), pl.BlockSpec((nd, 1, r, d), lambda i: (0, layer, 0, 0))],
        out_specs=[tok, tok],
        out_shape=[jax.ShapeDtypeStruct((s, d), F32)] * 2,
        args=(x, y_rg, y_ml, gate, w_out_g))


def _loss_call(x, fg, target, tile):
    s, d = x.shape

    def body(x_ref, g_ref, t_ref, dx_ref, loss_ref, gg_ref):
        @pl.when(pl.program_id(0) == 0)
        def _():
            loss_ref[...] = jnp.zeros_like(loss_ref)
            gg_ref[...] = jnp.zeros_like(gg_ref)

        xv = x_ref[...]
        g = g_ref[...]
        rs = lax.rsqrt(jnp.mean(xv * xv, axis=1, keepdims=True) + EPS)
        xh = xv * rs
        e = xh * g - t_ref[...]
        loss_ref[...] += jnp.broadcast_to(_colsum(_rowsum(e * e)) * (0.5 / d), loss_ref.shape)
        dy = e * (1.0 / d)
        gg_ref[...] += _bcast8(_colsum(dy * xh))
        dxh = dy * g
        dx_ref[...] = rs * (dxh - xh * jnp.mean(dxh * xh, axis=1, keepdims=True))

    tok = pl.BlockSpec((tile, d), lambda i: (i, 0))
    return pl.pallas_call(
        body, name="final_norm_loss", grid=(s // tile,),
        in_specs=[tok, _full((1, d)), tok],
        out_specs=[tok, _full((SUBLANES, LANES)), _full((SUBLANES, d))],
        out_shape=[jax.ShapeDtypeStruct((s, d), F32), jax.ShapeDtypeStruct((SUBLANES, LANES), F32),
                   jax.ShapeDtypeStruct((SUBLANES, d), F32)],
        compiler_params=_params(1),
    )(x, fg, target)


def _ml_out_stage_bwd(dy, cell, o, z, ng):
    so = _sigmoid(o)
    hg = so * cell
    rinv = lax.rsqrt(jnp.mean(hg * hg, axis=1, keepdims=True) + EPS)
    hn = hg * rinv
    sz = _sigmoid(z)
    dz = dy * hn * ng * (sz + z * sz * (1.0 - sz))
    dymid = dy * z * sz
    dhn = dymid * ng
    dhg = rinv * (dhn - hn * jnp.mean(dhn * hn, axis=1, keepdims=True))
    return dz, dhg * cell * so * (1.0 - so), dhg * so, _colsum(dymid * hn)


def _out_bwd_with_output_stage(dxo, gate, y, y_rg, y_ml, cell, u, ng, w_out_g, layer, tile, nh, comms=None):
    s, d = dxo.shape
    nd, _, r, _ = w_out_g.shape
    dh = d // nh

    def body(dx_ref, g_ref, y_ref, yr_ref, ym_ref, cell_ref, o_ref, z_ref, ng_ref, w_ref,
             dyr_ref, dcell_ref, do_ref, dz_ref, gw_ref, dg_ref, gng_ref):
        @pl.when(pl.program_id(0) == 0)
        def _():
            for ref in (gw_ref, dg_ref, gng_ref):
                ref[...] = jnp.zeros_like(ref)

        dxv = dx_ref[...]
        dg_ref[...] += _bcast8(_colsum(dxv * y_ref[...]))
        dyb = (dxv * g_ref[...]).astype(BF16)
        dycat = lax.dot_general(dyb, w_ref[...].reshape(nd * r, d), (((1,), (1,)), ((), ())), preferred_element_type=F32)
        dyr_ref[...] = dycat[:, 0:d]
        ycat = jnp.concatenate([yr_ref[...].astype(BF16), ym_ref[...].astype(BF16)], axis=1)
        gw_ref[...] += lax.dot_general(ycat, dyb, (((0,), (0,)), ((), ())), preferred_element_type=F32).reshape(nd, r, d)
        for h in range(nh):
            sl = slice(h * dh, (h + 1) * dh)
            dz, do, dcell, gng = _ml_out_stage_bwd(dycat[:, d + h * dh:d + (h + 1) * dh], cell_ref[:, sl], o_ref[:, sl],
                                                   z_ref[:, sl], ng_ref[:, sl])
            dz_ref[:, sl] = dz.astype(BF16)
            do_ref[:, sl] = do.astype(BF16)
            dcell_ref[:, sl] = dcell
            gng_ref[:, sl] += _bcast8(gng)

    tok = pl.BlockSpec((tile, d), lambda i: (i, 0))
    acc = _full((SUBLANES, d))
    return _call(
        body, comms, name="out_proj_bwd", grid=(s // tile,),
        in_specs=[tok, _full((1, d)), tok, tok, tok, tok, pl.BlockSpec((tile, d), lambda i: (i, 3)),
                  pl.BlockSpec((tile, d), lambda i: (i, 4)), _full((1, d)),
                  pl.BlockSpec((nd, 1, r, d), lambda i: (0, layer, 0, 0), pipeline_mode=pl.Buffered(1))],
        out_specs=[tok, tok, tok, tok, pl.BlockSpec((nd, r, d), lambda i: (0, 0, 0), pipeline_mode=pl.Buffered(1)), acc, acc],
        out_shape=[jax.ShapeDtypeStruct((s, d), F32)] * 2 + [jax.ShapeDtypeStruct((s, d), BF16)] * 2
        + [jax.ShapeDtypeStruct((nd, r, d), F32)] + [jax.ShapeDtypeStruct((SUBLANES, d), F32)] * 2,
        args=(dxo, gate, y, y_rg, y_ml, cell, u, u, ng, w_out_g))


def _ml_cell_only_bwd(dcell, cell, q, k, v, gcol, grow, mt, cs, ns, ms, nh, comms=None):
    s, d = q.shape
    lc = ML_CHUNK
    nc = s // lc
    dh = d // nh

    def body(dcell_ref, cell_ref, q_ref, k_ref, v_ref, gc_ref, gr_ref, mt_ref, cs_ref, ns_ref, ms_ref,
             dq_ref, dk_ref, dv_ref, dg_ref, dc_sc, dn_sc):
        @pl.when(pl.program_id(0) == 0)
        def _():
            dc_sc[...] = jnp.zeros_like(dc_sc)
            dn_sc[...] = jnp.zeros_like(dn_sc)

        gc = gc_ref[...]
        gr = gr_ref[...]
        mtv = mt_ref[...]
        lane = _iota((lc, LANES), 1)
        rowv = _iota((lc, 1), 0)
        dg_acc = jnp.zeros((lc, LANES), F32)
        for h in range(nh):
            c_h = cs_ref[0, h]
            n_h = ns_ref[0, h, 0:1, :]
            m_prev = jnp.max(ms_ref[0, h, 0:1, :], axis=1, keepdims=True)
            t = _cell_chunk(h, nh, q_ref, k_ref, v_ref, gc, gr, m_prev, c_h, n_h, m_t=_col(mtv, h))
            sl, qh, kh, vh = t["sl"], t["qh"], t["kh"], t["vh"]
            w_intra, w_inter, smat, w_state, decay = t["w_intra"], t["w_inter"], t["smat"], t["w_state"], t["decay"]
            cell_h = cell_ref[:, sl]
            dcell = dcell_ref[:, sl]
            eneg = jnp.exp(-t["m_t"])
            aden = jnp.abs(t["den"])
            nst = jnp.maximum(aden, eneg)
            dnum = dcell / nst
            dden = jnp.where(aden > eneg, -_rowsum(cell_h * dcell) / nst * jnp.sign(t["den"]), 0.0)
            pmat = _mm_nt(dnum, vh) + dden
            damat = pmat * w_intra
            gmat = pmat * smat
            wdn = w_inter * dnum
            wdd = w_inter * dden
            dqh = _mm(damat, kh) + _mm_nt(wdn, c_h) + wdd * n_h
            dkh = _mm_tn(damat, qh)
            dvh = _mm_tn(smat, dnum)
            dw_inter = _rowsum(dnum * t["qc"]) + dden * t["qn"]
            dcn = dc_sc[h]
            dnn = dn_sc[h, 0:1, :]
            kw = kh * w_state
            dkw = _mm_nt(vh, dcn) + dnn
            dvh = dvh + _mm(kw, dcn)
            dkh = dkh + dkw * w_state
            dgst = _rowsum(dkw * kh) * w_state
            ddecay = _colsum(_rowsum(dcn * c_h)) + _rowsum(dnn * n_h)
            db_last = _colsum(dgst) + ddecay * decay
            rs_g = _rowsum(gmat)
            cs_g = _rowsum(gmat.T)
            db = rs_g - cs_g + dw_inter * w_inter - dgst + jnp.where(rowv == lc - 1, db_last, 0.0)
            dli = cs_g + dgst
            dc_sc[h] = decay * dcn + _mm_tn(qh, wdn)
            dn_sc[h] = _bcast8(decay * dnn + _colsum(qh * wdd))
            dq_ref[:, sl] = dqh
            dk_ref[:, sl] = dkh * (dh ** -0.5)
            dv_ref[:, sl] = dvh
            dg_acc = jnp.where(lane == h, dli, jnp.where(lane == 4 + h, db, dg_acc))
        dg_ref[...] = dg_acc

    rev = lambda c: nc - 1 - c
    tok = pl.BlockSpec((lc, d), lambda c: (rev(c), 0))
    g128 = pl.BlockSpec((lc, LANES), lambda c: (rev(c), 0))
    return _call(
        body, comms, name="mlstm_cell_bwd", grid=(nc,),
        in_specs=[tok, tok, tok, tok, tok, g128, pl.BlockSpec((16, lc), lambda c: (0, rev(c))), g128,
                  pl.BlockSpec((1, nh, dh, dh), lambda c: (rev(c), 0, 0, 0)),
                  pl.BlockSpec((1, nh, SUBLANES, dh), lambda c: (rev(c), 0, 0, 0)),
                  pl.BlockSpec((1, nh, SUBLANES, LANES), lambda c: (rev(c), 0, 0, 0))],
        out_specs=[tok, tok, tok, g128],
        out_shape=[jax.ShapeDtypeStruct((s, d), F32)] * 3 + [jax.ShapeDtypeStruct((s, LANES), F32)],
        scratch_shapes=[pltpu.VMEM((nh, dh, dh), F32), pltpu.VMEM((nh, SUBLANES, dh), F32)],
        args=(dcell, cell, q, k, v, gcol, grow, mt, cs, ns, ms))


def _out_bwd(dxo, gate, y, y_rg, y_ml, w_out_g, layer, tile, comms=None):
    s, d = dxo.shape
    nd, _, r, _ = w_out_g.shape

    def body(dx_ref, g_ref, y_ref, yr_ref, ym_ref, w_ref, dyr_ref, dym_ref, gw_ref, dg_ref):
        @pl.when(pl.program_id(0) == 0)
        def _():
            gw_ref[...] = jnp.zeros_like(gw_ref)
            dg_ref[...] = jnp.zeros_like(dg_ref)

        dxv = dx_ref[...]
        dg_ref[...] += _bcast8(_colsum(dxv * y_ref[...]))
        dyb = (dxv * g_ref[...]).astype(BF16)
        dycat = lax.dot_general(dyb, w_ref[...].reshape(nd * r, d), (((1,), (1,)), ((), ())), preferred_element_type=F32)
        dyr_ref[...] = dycat[:, 0:d]
        dym_ref[...] = dycat[:, d:2 * d]
        ycat = jnp.concatenate([yr_ref[...].astype(BF16), ym_ref[...].astype(BF16)], axis=1)
        gw_ref[...] += lax.dot_general(ycat, dyb, (((0,), (0,)), ((), ())), preferred_element_type=F32).reshape(nd, r, d)

    tok = pl.BlockSpec((tile, d), lambda i: (i, 0))
    return _call(
        body, comms, name="out_proj_bwd", grid=(s // tile,),
        in_specs=[tok, _full((1, d)), tok, tok, tok, pl.BlockSpec((nd, 1, r, d), lambda i: (0, layer, 0, 0))],
        out_specs=[tok, tok, _full((nd, r, d)), _full((SUBLANES, d))],
        out_shape=[jax.ShapeDtypeStruct((s, d), F32)] * 2 + [jax.ShapeDtypeStruct((nd, r, d), F32),
                                                             jax.ShapeDtypeStruct((SUBLANES, d), F32)],
        args=(dxo, gate, y, y_rg, y_ml, w_out_g))


def _ml_cell_bwd(dy_ml, u, cell, q, k, v, gcol, grow, mt, cs, ns, ms, ng, nh, comms=None):
    s, d = q.shape
    lc = ML_CHUNK
    nc = s // lc
    dh = d // nh

    def body(dy_ref, o_ref, z_ref, cell_ref, q_ref, k_ref, v_ref, gc_ref, gr_ref, mt_ref, cs_ref, ns_ref, ms_ref,
             ng_ref, dq_ref, dk_ref, dv_ref, dg_ref, do_ref, dz_ref, gng_ref, dc_sc, dn_sc):
        @pl.when(pl.program_id(0) == 0)
        def _():
            dc_sc[...] = jnp.zeros_like(dc_sc)
            dn_sc[...] = jnp.zeros_like(dn_sc)
            gng_ref[...] = jnp.zeros_like(gng_ref)

        gc = gc_ref[...]
        gr = gr_ref[...]
        mtv = mt_ref[...]
        lane = _iota((lc, LANES), 1)
        rowv = _iota((lc, 1), 0)
        dg_acc = jnp.zeros((lc, LANES), F32)
        for h in range(nh):
            c_h = cs_ref[0, h]
            n_h = ns_ref[0, h, 0:1, :]
            m_prev = jnp.max(ms_ref[0, h, 0:1, :], axis=1, keepdims=True)
            t = _cell_chunk(h, nh, q_ref, k_ref, v_ref, gc, gr, m_prev, c_h, n_h, m_t=_col(mtv, h))
            sl, qh, kh, vh = t["sl"], t["qh"], t["kh"], t["vh"]
            w_intra, w_inter, smat, w_state, decay = t["w_intra"], t["w_inter"], t["smat"], t["w_state"], t["decay"]
            cell_h = cell_ref[:, sl]
            dz, do, dcell, gng = _ml_out_stage_bwd(dy_ref[:, sl], cell_h, o_ref[:, sl], z_ref[:, sl], ng_ref[:, sl])
            dz_ref[:, sl] = dz.astype(BF16)
            do_ref[:, sl] = do.astype(BF16)
            gng_ref[:, sl] += _bcast8(gng)
            eneg = jnp.exp(-t["m_t"])
            aden = jnp.abs(t["den"])
            nst = jnp.maximum(aden, eneg)
            dnum = dcell / nst
            dden = jnp.where(aden > eneg, -_rowsum(cell_h * dcell) / nst * jnp.sign(t["den"]), 0.0)
            pmat = _mm_nt(dnum, vh) + dden
            damat = pmat * w_intra
            gmat = pmat * smat
            wdn = w_inter * dnum
            wdd = w_inter * dden
            dqh = _mm(damat, kh) + _mm_nt(wdn, c_h) + wdd * n_h
            dkh = _mm_tn(damat, qh)
            dvh = _mm_tn(smat, dnum)
            dw_inter = _rowsum(dnum * t["qc"]) + dden * t["qn"]
            dcn = dc_sc[h]
            dnn = dn_sc[h, 0:1, :]
            kw = kh * w_state
            dkw = _mm_nt(vh, dcn) + dnn
            dvh = dvh + _mm(kw, dcn)
            dkh = dkh + dkw * w_state
            dgst = _rowsum(dkw * kh) * w_state
            ddecay = _colsum(_rowsum(dcn * c_h)) + _rowsum(dnn * n_h)
            db_last = _colsum(dgst) + ddecay * decay
            rs_g = _rowsum(gmat)
            cs_g = _rowsum(gmat.T)
            db = rs_g - cs_g + dw_inter * w_inter - dgst + jnp.where(rowv == lc - 1, db_last, 0.0)
            dli = cs_g + dgst
            dc_sc[h] = decay * dcn + _mm_tn(qh, wdn)
            dn_sc[h] = _bcast8(decay * dnn + _colsum(qh * wdd))
            dq_ref[:, sl] = dqh
            dk_ref[:, sl] = dkh * (dh ** -0.5)
            dv_ref[:, sl] = dvh
            dg_acc = jnp.where(lane == h, dli, jnp.where(lane == 4 + h, db, dg_acc))
        dg_ref[...] = dg_acc

    rev = lambda c: nc - 1 - c
    tok = pl.BlockSpec((lc, d), lambda c: (rev(c), 0))
    g128 = pl.BlockSpec((lc, LANES), lambda c: (rev(c), 0))
    return _call(
        body, comms, name="mlstm_cell_bwd", grid=(nc,),
        in_specs=[tok, pl.BlockSpec((lc, d), lambda c: (rev(c), 3)), pl.BlockSpec((lc, d), lambda c: (rev(c), 4)),
                  tok, tok, tok, tok, g128, pl.BlockSpec((16, lc), lambda c: (0, rev(c))), g128,
                  pl.BlockSpec((1, nh, dh, dh), lambda c: (rev(c), 0, 0, 0)),
                  pl.BlockSpec((1, nh, SUBLANES, dh), lambda c: (rev(c), 0, 0, 0)),
                  pl.BlockSpec((1, nh, SUBLANES, LANES), lambda c: (rev(c), 0, 0, 0)), _full((1, d))],
        out_specs=[tok, tok, tok, g128, tok, tok, _full((SUBLANES, d))],
        out_shape=[jax.ShapeDtypeStruct((s, d), F32)] * 3 + [jax.ShapeDtypeStruct((s, LANES), F32)]
        + [jax.ShapeDtypeStruct((s, d), BF16)] * 2 + [jax.ShapeDtypeStruct((SUBLANES, d), F32)],
        scratch_shapes=[pltpu.VMEM((nh, dh, dh), F32), pltpu.VMEM((nh, SUBLANES, dh), F32)],
        args=(dy_ml, u, u, cell, q, k, v, gcol, grow, mt, cs, ns, ms, ng))


def _halo_spec(d, tile, nt, col):
    per = tile // SUBLANES
    return pl.BlockSpec((SUBLANES, d), lambda i: (jnp.maximum((nt - 1 - i) * per - 1, 0), col))


def _ml_pre_bwd(dq, dk, dv, dgates, gcol, u, pre, q, k, v, conv_w, w_q, w_k, w_v, wif_t, tile, comms=None):
    s, d = dq.shape
    nt = s // tile
    nh, dh, _ = w_q.shape

    def body(dq_ref, dk_ref, dv_ref, dg_ref, gc_ref, x_ref, pre_ref, q_ref, k_ref, v_ref, cw_ref,
             wq_ref, wk_ref, wv_ref, wift_ref,
             dx_ref, gwq_ref, gwk_ref, gwv_ref, gwif_ref, gbif_ref, gcw_ref, gcb_ref, next8):
        @pl.when(pl.program_id(0) == 0)
        def _():
            next8[...] = jnp.zeros_like(next8)
            for ref in (gwq_ref, gwk_ref, gwv_ref, gwif_ref, gbif_ref, gcw_ref, gcb_ref):
                ref[...] = jnp.zeros_like(ref)

        x = x_ref[...]
        pre = pre_ref[...]
        sg = _sigmoid(pre)
        xc = pre * sg
        dgc = dg_ref[...]
        lane = _iota(dgc.shape, 1)
        utri = jnp.where(_iota((ML_CHUNK, ML_CHUNK), 0) <= _iota((ML_CHUNK, ML_CHUNK), 1), 1.0, 0.0)
        rcs = [_mm_hi(utri, dgc[c * ML_CHUNK:(c + 1) * ML_CHUNK, :]) for c in range(tile // ML_CHUNK)]
        rc = rcs[0] if len(rcs) == 1 else jnp.concatenate(rcs, axis=0)
        dgates_v = jnp.where(lane < 4, dgc, jnp.where(lane < 8, rc * (1.0 - jnp.exp(gc_ref[...])), 0.0))
        dgb = dgates_v.astype(BF16)
        gbif_ref[...] += jnp.broadcast_to(_colsum(dgates_v), gbif_ref.shape)
        ext = jnp.dot(dgb, wift_ref[...], preferred_element_type=F32)
        dqt = dq_ref[...] + ext[:, 0:d]
        dkt = dk_ref[...] + ext[:, d:2 * d]
        dvt = dv_ref[...] + ext[:, 2 * d:3 * d]
        gwif_ref[:, 0:d] += _mm_tn(dgb, q_ref[...])
        gwif_ref[:, d:2 * d] += _mm_tn(dgb, k_ref[...])
        gwif_ref[:, 2 * d:3 * d] += _mm_tn(dgb, v_ref[...])
        dxc_parts, dxv_parts = [], []
        for h in range(nh):
            sl = slice(h * dh, (h + 1) * dh)
            gwq_ref[h] += _mm_tn(xc[:, sl], dqt[:, sl])
            gwk_ref[h] += _mm_tn(xc[:, sl], dkt[:, sl])
            gwv_ref[h] += _mm_tn(x[:, sl], dvt[:, sl])
            dxc_parts.append(_mm_nt(dqt[:, sl], wq_ref[h]) + _mm_nt(dkt[:, sl], wk_ref[h]))
            dxv_parts.append(_mm_nt(dvt[:, sl], wv_ref[h]))
        dxc = jnp.concatenate(dxc_parts, axis=1)
        dxv = jnp.concatenate(dxv_parts, axis=1)
        dpre = dxc * (sg + pre * sg * (1.0 - sg))
        gcb_ref[...] += _bcast8(_colsum(dpre))
        dx_ref[...] = (dxv + _conv_bwd(dpre, x, next8[...], cw_ref, gcw_ref)).astype(BF16)
        next8[...] = dpre[0:SUBLANES, :]

    rev = lambda i: nt - 1 - i
    tok = pl.BlockSpec((tile, d), lambda i: (rev(i), 0))
    g128 = pl.BlockSpec((tile, LANES), lambda i: (rev(i), 0))
    wsh = (nh, dh, dh)
    return _call(
        body, comms, name="mlstm_proj_bwd", grid=(nt,),
        in_specs=[tok, tok, tok, g128, g128, pl.BlockSpec((tile, d), lambda i: (rev(i), 2)), tok,
                  tok, tok, tok, _full(conv_w.shape), _full(wsh), _full(wsh), _full(wsh), _full(wif_t.shape)],
        out_specs=[tok, _full(wsh), _full(wsh), _full(wsh), _full((LANES, 3 * d)), _full((SUBLANES, LANES)),
                   _full((SUBLANES, d)), _full((SUBLANES, d))],
        out_shape=[jax.ShapeDtypeStruct((s, d), BF16)] + [jax.ShapeDtypeStruct(wsh, F32)] * 3
        + [jax.ShapeDtypeStruct((LANES, 3 * d), F32), jax.ShapeDtypeStruct((SUBLANES, LANES), F32),
           jax.ShapeDtypeStruct((SUBLANES, d), F32), jax.ShapeDtypeStruct((SUBLANES, d), F32)],
        scratch_shapes=[pltpu.VMEM((SUBLANES, d), F32)],
        args=(dq, dk, dv, dgates, gcol, u, pre, q, k, v, conv_w, w_q, w_k, w_v, wif_t))


def _rg_bwd(dy_rg, u, h_rg, gates, conv_w, w_a, w_x, lam, tile, comms=None):
    s, d = dy_rg.shape
    nt = s // tile
    nh, dh, _ = w_a.shape

    def body(dy_ref, x_ref, z_ref, h_ref, hhalo_ref, xc_ref, r_ref, i_ref, a_ref, beta_ref, cw_ref, wa_ref,
             wx_ref, lam_ref,
             dx_ref, dz_ref, gwa_ref, gwx_ref, gba_ref, gbx_ref, glam_ref, gcw_ref, gcb_ref, next8, anext, dnext, dbuf):
        i = pl.program_id(0)

        @pl.when(i == 0)
        def _():
            for ref in (next8, anext, dnext, gwa_ref, gwx_ref, gba_ref, gbx_ref, glam_ref, gcw_ref, gcb_ref):
                ref[...] = jnp.zeros_like(ref)

        inner = jnp.where(i < nt - 1, 1.0, 0.0)
        xc, r, ig, a, beta = xc_ref[...], r_ref[...], i_ref[...], a_ref[...], beta_ref[...]
        sp = _softplus(-lam_ref[...])
        h = h_ref[...]
        row = _iota(h.shape, 0)
        hprev = jnp.where(row >= 1, pltpu.roll(h, 1, 0), hhalo_ref[SUBLANES - 1:SUBLANES, :] * inner)
        z = z_ref[...]
        sz = _sigmoid(z)
        dyv = dy_ref[...]
        dz_ref[...] = (dyv * h * (sz + z * sz * (1.0 - sz))).astype(BF16)
        a_up = jnp.where(row < tile - 1, pltpu.roll(a, tile - 1, 0), anext[0:1, :])
        _scan_into(a_up, dyv * z * sz, dnext[0:1, :], dbuf, True)
        delta = dbuf[...]
        anext[...] = a[0:SUBLANES, :]
        dnext[...] = delta[0:SUBLANES, :]
        dla = delta * hprev * a - delta * ig * xc * (a * a / beta)
        glam_ref[...] += _bcast8(_colsum(dla * r) * (RG_C * _sigmoid(-lam_ref[...])))
        dpa = dla * (-RG_C * sp) * r * (1.0 - r)
        dpx = delta * beta * xc * ig * (1.0 - ig)
        gba_ref[...] += _bcast8(_colsum(dpa))
        gbx_ref[...] += _bcast8(_colsum(dpx))
        parts = []
        for hh in range(nh):
            sl = slice(hh * dh, (hh + 1) * dh)
            gwa_ref[hh] += _mm_tn(xc[:, sl], dpa[:, sl])
            gwx_ref[hh] += _mm_tn(xc[:, sl], dpx[:, sl])
            parts.append(_mm_nt(dpa[:, sl], wa_ref[hh]) + _mm_nt(dpx[:, sl], wx_ref[hh]))
        dxc = delta * beta * ig + jnp.concatenate(parts, axis=1)
        gcb_ref[...] += _bcast8(_colsum(dxc))
        dx_ref[...] = _conv_bwd(dxc, x_ref[...], next8[...], cw_ref, gcw_ref).astype(BF16)
        next8[...] = dxc[0:SUBLANES, :]

    rev = lambda i: nt - 1 - i
    tok = pl.BlockSpec((tile, d), lambda i: (rev(i), 0))
    vec = _full((1, d))
    acc = _full((SUBLANES, d))
    wsh = (nh, dh, dh)
    return _call(
        body, comms, name="rglru_bwd", grid=(nt,),
        in_specs=[tok, tok, pl.BlockSpec((tile, d), lambda i: (rev(i), 1)), tok,
                  _halo_spec(d, tile, nt, 0)] + [tok] * 5 + [_full(conv_w.shape), _full(wsh), _full(wsh), vec],
        out_specs=[tok, tok, _full(wsh), _full(wsh), acc, acc, acc, acc, acc],
        out_shape=[jax.ShapeDtypeStruct((s, d), BF16)] * 2 + [jax.ShapeDtypeStruct(wsh, F32)] * 2
        + [jax.ShapeDtypeStruct((SUBLANES, d), F32)] * 5,
        scratch_shapes=[pltpu.VMEM((SUBLANES, d), F32)] * 3 + [pltpu.VMEM((tile, d), F32)],
        args=(dy_rg, u, u, h_rg, h_rg, *gates, conv_w, w_a, w_x, lam))


def _segments(d, w, n_pieces, n_slots):
    bounds = sorted({k * d for k in range(n_pieces + 1)} | {j * w for j in range(n_slots + 1)})
    return [(lo // d, lo % d, lo // w, lo % w, hi - lo) for lo, hi in zip(bounds[:-1], bounds[1:])]


def _in_bwd(pieces, x, dxo, ng, scale, w_in_g, layer, tile, comms=None, tiles=None, prev=None):
    s, d = x.shape
    nd, _, _, w = w_in_g.shape
    segs = _segments(d, w, len(pieces), nd)
    first, count = tiles or (0, s // tile)
    n_p = len(pieces)

    def body(*refs):
        p_refs = refs[:n_p]
        x_ref, dxo_ref, ng_ref, sc_ref, w_ref = refs[n_p:n_p + 5]
        dx_ref, dsc_ref, dsh_ref, gng_ref, wcat = refs[-5:]
        _join_columns(w_ref, wcat)

        @pl.when(pl.program_id(0) == 0)
        def _():
            for k, ref in enumerate((dsc_ref, dsh_ref, gng_ref)):
                ref[...] = jnp.zeros_like(ref) if prev is None else refs[n_p + 6 + k][...]

        du = jnp.concatenate([p[...] for p in p_refs], axis=1)
        dh = lax.dot_general(du, wcat[...], (((1,), (1,)), ((), ())), preferred_element_type=F32)
        xv = x_ref[...]
        g = ng_ref[...]
        rs = lax.rsqrt(jnp.mean(xv * xv, axis=1, keepdims=True) + EPS)
        xh = xv * rs
        dsh_ref[...] += _bcast8(_colsum(dh))
        dsc_ref[...] += _bcast8(_colsum(dh * xh * g))
        dhn = dh * (1.0 + sc_ref[...])
        gng_ref[...] += _bcast8(_colsum(dhn * xh))
        dxh = dhn * g
        dx_ref[...] = dxo_ref[...] + rs * (dxh - xh * jnp.mean(dxh * xh, axis=1, keepdims=True))

    tok = pl.BlockSpec((tile, d), lambda i: (i + first, 0))
    vec = _full((1, d))
    acc = _full((SUBLANES, d))
    more_specs = [] if prev is None else [pl.BlockSpec(memory_space=pl.ANY), acc, acc, acc]
    return _call(
        body, comms, name="in_proj_bwd_x", grid=(count,),
        in_specs=[tok] * n_p + [tok, tok, vec, vec, pl.BlockSpec((nd, 1, d, w), lambda i: (0, layer, 0, 0),
                                                               pipeline_mode=pl.Buffered(1))] + more_specs,
        out_specs=[tok, acc, acc, acc],
        out_shape=[jax.ShapeDtypeStruct((s, d), F32)] + [jax.ShapeDtypeStruct((SUBLANES, d), F32)] * 3,
        scratch_shapes=[pltpu.VMEM((d, nd * w), BF16)],
        args=(*pieces, x, dxo, ng, scale, w_in_g) + (() if prev is None else tuple(prev)),
        aliases={} if prev is None else {n_p + 5: 0})


def _in_bwd_w(pieces, hbf, w, slots, tile, comms=None):
    s, d = hbf.shape
    nd_all = len(pieces) * d // w
    segs = [sg for sg in _segments(d, w, len(pieces), nd_all) if sg[2] in slots]

    def body(*refs):
        p_refs = refs[:len(pieces)]
        h_ref, gw_ref = refs[len(pieces):]

        @pl.when(pl.program_id(0) == 0)
        def _():
            gw_ref[...] = jnp.zeros_like(gw_ref)

        hv = h_ref[...]
        for (kk, a, j, b, width) in segs:
            gw_ref[j - slots[0], :, b:b + width] += _mm_tn(hv, p_refs[kk][:, a:a + width])

    tok = pl.BlockSpec((tile, d), lambda i: (i, 0))
    return _call(
        body, comms, name="in_proj_bwd_w", grid=(s // tile,),
        in_specs=[tok] * len(pieces) + [tok],
        out_specs=[pl.BlockSpec((len(slots), d, w), lambda i: (0, 0, 0), pipeline_mode=pl.Buffered(1))],
        out_shape=[jax.ShapeDtypeStruct((len(slots), d, w), F32)],
        args=(*pieces, hbf))[0]


def _ada_bwd_w(cact_col, dmod, nd):
    d = cact_col.shape[0]
    w = dmod.shape[1] // nd

    def body(c_ref, m_ref, o_ref):
        o_ref[0] = c_ref[...] * m_ref[...]

    return pl.pallas_call(
        body, name="adaln_bwd_w", grid=(nd,),
        in_specs=[_full((d, 1)), pl.BlockSpec((1, w), lambda j: (0, j))],
        out_specs=pl.BlockSpec((1, d, w), lambda j: (j, 0, 0)),
        out_shape=jax.ShapeDtypeStruct((nd, d, w), F32),
        compiler_params=_params(1),
    )(cact_col, dmod)


def _exchange(arrs, gather, name):
    return _run_comms([_exchange_comm(arrs, gather)], name)[0]


def _run_comms(comms, name):
    _call(lambda: None, comms, name=name, grid=(1,), in_specs=[], out_specs=[], out_shape=[], args=())
    return [cm.results for cm in comms]


def _exchange_comm(arrs, gather):
    n = len(arrs)
    per = N_DEV - 1

    def copies(ins, outs, sems):
        send_sems, recv_sems, local_sems = sems
        x, y, c = (lax.axis_index(ax) for ax in MESH_AXES)
        me = 4 * x + 2 * y + c
        sends, recvs = [], []
        for flip in range(1, N_DEV):
            px = x ^ ((flip >> 2) & 1)
            py = y ^ ((flip >> 1) & 1)
            pc = c ^ (flip & 1)
            peer = 4 * px + 2 * py + pc
            for kk in range(n):
                src = ins[kk] if gather else ins[kk].at[peer]
                sends.append(_remote(src, outs[kk].at[me], send_sems, recv_sems, kk * per + flip - 1, (px, py, pc)))
                recvs.append(_remote(src, outs[kk].at[peer], send_sems, recv_sems, kk * per + flip - 1, (px, py, pc)))
        local = [pltpu.make_async_copy(ins[kk] if gather else ins[kk].at[me], outs[kk].at[me], local_sems.at[kk])
                 for kk in range(n)]
        return local, sends, recvs

    def start(ins, outs, sems):
        local, sends, _ = copies(ins, outs, sems)
        for cp in sends + local:
            cp.start()

    def finish(ins, outs, sems):
        local, sends, recvs = copies(ins, outs, sems)
        for cp in recvs:
            cp.wait_recv()
        for cp in sends:
            cp.wait_send()
        for cp in local:
            cp.wait()

    return _Comm(arrs, [jax.ShapeDtypeStruct((N_DEV,) + a.shape if gather else a.shape, a.dtype) for a in arrs],
                 [pltpu.SemaphoreType.DMA((n * per,)), pltpu.SemaphoreType.DMA((n * per,)), pltpu.SemaphoreType.DMA((n,))],
                 start, finish)


def _mesh_place():
    x, y, c = (lax.axis_index(ax) for ax in MESH_AXES)
    return x, y, c, (x, y, 1 - c), [(1 - x, y), (x, 1 - y), (1 - x, 1 - y)]


def _remote(src, dst, send_sems, recv_sems, sem, to):
    return pltpu.make_async_remote_copy(src_ref=src, dst_ref=dst, send_sem=send_sems.at[sem], recv_sem=recv_sems.at[sem],
                                        device_id=to, device_id_type=pl.DeviceIdType.MESH)


def _gather_two_level(arrs, name):
    n = len(arrs)
    per = N_DEV - 1

    def body(*refs):
        ins, outs = refs[:n], refs[n:2 * n]
        send_sems, recv_sems, local_sems = refs[2 * n:]
        x, y, c, sibling, chips = _mesh_place()

        def copy(kk, j, block, to, src=None):
            dst = outs[kk].at[4 * block[0] + 2 * block[1] + block[2]]
            return _remote(dst if src is None else src, dst, send_sems, recv_sems, kk * per + j, to)

        me = (x, y, c)
        local = [pltpu.make_async_copy(ins[kk], outs[kk].at[4 * x + 2 * y + c], local_sems.at[kk]) for kk in range(n)]
        first = []
        for j, chip in enumerate(chips):
            first += [copy(kk, 1 + j, me, (*chip, c), src=ins[kk]) for kk in range(n)]
        first += [copy(kk, 0, me, sibling, src=ins[kk]) for kk in range(n)]
        for cp in first + local:
            cp.start()
        passed = []
        for j, chip in enumerate(chips):
            for kk in range(n):
                copy(kk, 1 + j, (*chip, c), me).wait_recv()
                passed.append(copy(kk, 4 + j, (*chip, c), sibling))
                passed[-1].start()
        for kk in range(n):
            copy(kk, 0, sibling, me).wait_recv()
        for j, chip in enumerate(chips):
            for kk in range(n):
                copy(kk, 4 + j, (*chip, 1 - c), me).wait_recv()
        for cp in first + passed:
            cp.wait_send()
        for cp in local:
            cp.wait()

    return pl.pallas_call(
        body, name=name,
        in_specs=[pl.BlockSpec(memory_space=pl.ANY)] * n, out_specs=[pl.BlockSpec(memory_space=pl.ANY)] * n,
        out_shape=[jax.ShapeDtypeStruct((N_DEV,) + a.shape, a.dtype) for a in arrs],
        scratch_shapes=[pltpu.SemaphoreType.DMA((n * per,)), pltpu.SemaphoreType.DMA((n * per,)),
                        pltpu.SemaphoreType.DMA((n,))],
    )(*arrs)


N_CHIPS = N_DEV // 2


def _core_swap(arrs, name):
    n = len(arrs)

    def body(*refs):
        ins, outs = refs[:n], refs[n:2 * n]
        send_sems, recv_sems = refs[2 * n:]
        _, _, c, sibling, _ = _mesh_place()
        copies = [_remote(ins[kk].at[2 * q + (1 - c)], outs[kk].at[q], send_sems, recv_sems, kk * N_CHIPS + q, sibling)
                  for q in range(N_CHIPS) for kk in range(n)]
        for cp in copies:
            cp.start()
        for cp in copies:
            cp.wait_recv()
        for cp in copies:
            cp.wait_send()

    return pl.pallas_call(
        body, name=name,
        in_specs=[pl.BlockSpec(memory_space=pl.ANY)] * n, out_specs=[pl.BlockSpec(memory_space=pl.ANY)] * n,
        out_shape=[jax.ShapeDtypeStruct((N_CHIPS,) + a.shape[1:], a.dtype) for a in arrs],
        scratch_shapes=[pltpu.SemaphoreType.DMA((n * N_CHIPS,)), pltpu.SemaphoreType.DMA((n * N_CHIPS,))],
    )(*arrs)


def _pair_sum(a, other, parity, name):
    _, r, c = a.shape
    tr = _row_tile(r, c, 3)

    def body(p_ref, a_ref, o_ref, s_ref):
        s_ref[...] = (a_ref[...] + o_ref[...]).astype(BF16)

    return pl.pallas_call(
        body, name=name,
        grid_spec=pltpu.PrefetchScalarGridSpec(
            num_scalar_prefetch=1, grid=(N_CHIPS, r // tr),
            in_specs=[pl.BlockSpec((1, tr, c), lambda q, i, p: (2 * q + p[0], i, 0)),
                      pl.BlockSpec((1, tr, c), lambda q, i, p: (q, i, 0))],
            out_specs=pl.BlockSpec((1, tr, c), lambda q, i, p: (q, i, 0))),
        out_shape=jax.ShapeDtypeStruct((N_CHIPS, r, c), BF16),
        compiler_params=_params(2),
    )(parity, a, other)


def _chip_swap(arrs, name):
    n = len(arrs)
    per = N_CHIPS - 1

    def body(*refs):
        ins, outs = refs[:n], refs[n:2 * n]
        send_sems, recv_sems, local_sems = refs[2 * n:]
        x, y, c, _, chips = _mesh_place()
        mine = 2 * x + y
        sends = [_remote(ins[kk].at[2 * chip[0] + chip[1]], outs[kk].at[mine], send_sems, recv_sems, kk * per + j, (*chip, c))
                 for j, chip in enumerate(chips) for kk in range(n)]
        recvs = [_remote(ins[kk].at[mine], outs[kk].at[2 * chip[0] + chip[1]], send_sems, recv_sems, kk * per + j, (*chip, c))
                 for j, chip in enumerate(chips) for kk in range(n)]
        local = [pltpu.make_async_copy(ins[kk].at[mine], outs[kk].at[mine], local_sems.at[kk]) for kk in range(n)]
        for cp in sends + local:
            cp.start()
        for cp in recvs:
            cp.wait_recv()
        for cp in sends:
            cp.wait_send()
        for cp in local:
            cp.wait()

    return pl.pallas_call(
        body, name=name,
        in_specs=[pl.BlockSpec(memory_space=pl.ANY)] * n, out_specs=[pl.BlockSpec(memory_space=pl.ANY)] * n,
        out_shape=[jax.ShapeDtypeStruct(a.shape, a.dtype) for a in arrs],
        scratch_shapes=[pltpu.SemaphoreType.DMA((n * per,)), pltpu.SemaphoreType.DMA((n * per,)),
                        pltpu.SemaphoreType.DMA((n,))],
    )(*arrs)


def _adam_math(w, g, m, v):
    m = ADAM_B1 * m + (1.0 - ADAM_B1) * g
    v = ADAM_B2 * v + (1.0 - ADAM_B2) * (g * g)
    m_hat = m / (1.0 - ADAM_B1 ** ADAM_STEP)
    v_hat = v / (1.0 - ADAM_B2 ** ADAM_STEP)
    delta = -ADAM_LR * (m_hat / (jnp.sqrt(v_hat) + ADAM_EPS) + ADAM_WD * w)
    return delta, m, v


def _sum_devices(r_ref):
    acc = r_ref[0].astype(F32)
    for p in range(1, r_ref.shape[0]):
        acc = acc + r_ref[p].astype(F32)
    return acc


def _row_tile(rows, cols, n_bufs):
    budget = 24 * 1024 * 1024 // (n_bufs * 2 * cols * 4)
    t = rows
    while t > budget and t % 2 == 0 and (t // 2) % SUBLANES == 0:
        t //= 2
    return t


def _reduce_adam(recvs, w, m, v, name):
    nl, r, c = w.shape
    n_part = recvs[0].shape[0]
    tr = _row_tile(r, c, n_part * nl + 7)
    nt = r // tr

    def body(*refs):
        r_refs = refs[:nl]
        w_ref, m_ref, v_ref, g_ref, d_ref, mo_ref, vo_ref = refs[nl:]
        layer = pl.program_id(0)
        g = _sum_devices(r_refs[0])
        for ll in range(1, nl):
            g = jnp.where(layer == ll, _sum_devices(r_refs[ll]), g)
        delta, m2, v2 = _adam_math(w_ref[0], g, m_ref[0], v_ref[0])
        g_ref[0] = g
        d_ref[0] = delta
        mo_ref[0] = m2
        vo_ref[0] = v2

    def rspec(ll):
        return pl.BlockSpec((n_part, tr, c), lambda l, i: (0, jnp.where(l == ll, i, jnp.where(l < ll, 0, nt - 1)), 0))

    blk = pl.BlockSpec((1, tr, c), lambda l, i: (l, i, 0))
    return pl.pallas_call(
        body, name=name, grid=(nl, nt),
        in_specs=[rspec(ll) for ll in range(nl)] + [blk, blk, blk],
        out_specs=[blk] * 4,
        out_shape=[jax.ShapeDtypeStruct((nl, r, c), F32)] * 4,
        compiler_params=_params(2),
    )(*recvs, w, m, v)


def _sum8(recv, name):
    _, r, c = recv.shape

    def body(r_ref, o_ref):
        o_ref[...] = _sum_devices(r_ref)

    return pl.pallas_call(
        body, name=name, grid=(1,),
        in_specs=[_full(recv.shape)], out_specs=_full((r, c)),
        out_shape=jax.ShapeDtypeStruct((r, c), F32), compiler_params=_params(1),
    )(recv)


def _adam_call(w, g, m, v, name):
    r, c = w.shape

    def body(w_ref, g_ref, m_ref, v_ref, d_ref, mo_ref, vo_ref):
        delta, m2, v2 = _adam_math(w_ref[...], g_ref[...], m_ref[...], v_ref[...])
        d_ref[...] = delta
        mo_ref[...] = m2
        vo_ref[...] = v2

    return pl.pallas_call(
        body, name=name, grid=(1,),
        in_specs=[_full((r, c))] * 4, out_specs=[_full((r, c))] * 3,
        out_shape=[jax.ShapeDtypeStruct((r, c), F32)] * 3, compiler_params=_params(1),
    )(w, g, m, v)


def _tile_for(s, want):
    return min(want, s)


def _local_step_whole(x, c, target, wts):
    s, d = x.shape
    nl = wts["w_in_g"].shape[1]
    nd = wts["w_in_g"].shape[0]
    nh_ml = wts["w_qkv"].shape[2]
    t_big = _tile_for(s, 512)
    t_mid = _tile_for(s, 256)

    mod, cact = _mod_call(c, wts["w_ada_g"], wts["b_ada"])
    row = lambda a: a.reshape(1, -1)
    saved = []
    xl = x
    for l in range(nl):
        shift, scale, gate = (row(mod[l, kk * d:(kk + 1) * d]) for kk in range(3))
        u, hbf = _in_fwd(xl, row(wts["norm_g"][l]), scale, shift, wts["w_in_g"], l, t_mid)
        h_rg, y_rg = _rg_fwd(u, d, wts["rg_conv_w"][l], row(wts["rg_conv_b"][l]), wts["rg_w_a_bf"][l],
                             row(wts["rg_b_a"][l]), wts["rg_w_x_bf"][l], row(wts["rg_b_x"][l]),
                             row(wts["rg_lambda"][l]), t_mid)
        q, k, v, gcol = _ml_pre(u, d, wts["ml_conv_w"][l], row(wts["ml_conv_b"][l]), wts["w_qkv"][l, 0],
                                wts["w_qkv"][l, 1], wts["w_qkv"][l, 2], wts["wif_pad"][l], wts["bif_pad"][l], t_mid)
        grow = gcol[:, 0:16].T
        cell, y_ml, cs, ns, ms, mt = _ml_cell_fwd(q, k, v, gcol, grow, u, row(wts["ml_norm_g"][l]), nh_ml)
        x_new, y = _out_fwd(xl, y_rg, y_ml, gate, wts["w_out_g"], l, t_big)
        saved.append(dict(x=xl, u=u, hbf=hbf, h_rg=h_rg, y_rg=y_rg, q=q, k=k, v=v, gcol=gcol, grow=grow, cell=cell,
                          y_ml=y_ml, cs=cs, ns=ns, ms=ms, mt=mt, y=y, scale=scale, gate=gate))
        xl = x_new

    dx, loss_p, g_final = _loss_call(xl, row(wts["final_g"]), target, t_big)
    grads = [None] * nl
    cact_col = cact[0].reshape(d, 1)
    for l in reversed(range(nl)):
        sv = saved[l]
        dy_rg, dy_ml, gw_out, dgate = _out_bwd(dx, sv["gate"], sv["y"], sv["y_rg"], sv["y_ml"], wts["w_out_g"], l, t_big)
        dq, dk, dv, dgates, d_mlo, d_mlz, g_mlng = _ml_cell_bwd(
            dy_ml, sv["u"], sv["cell"], sv["q"], sv["k"], sv["v"], sv["gcol"], sv["grow"], sv["mt"], sv["cs"],
            sv["ns"], sv["ms"], row(wts["ml_norm_g"][l]), nh_ml)
        d_mlx, g_wq, g_wk, g_wv, g_wift, g_bif, g_mlcw, g_mlcb = _ml_pre_bwd(
            dq, dk, dv, dgates, sv["gcol"], sv["u"], sv["q"], sv["k"], sv["v"], wts["ml_conv_w"][l],
            row(wts["ml_conv_b"][l]), wts["w_qkv"][l, 0], wts["w_qkv"][l, 1], wts["w_qkv"][l, 2], wts["wift_pad"][l], t_mid)
        d_rgx, d_rgz, g_wa, g_wx, g_ba, g_bx, g_lam, g_rgcw, g_rgcb = _rg_bwd(
            dy_rg, sv["u"], sv["h_rg"], wts["rg_conv_w"][l], row(wts["rg_conv_b"][l]), wts["rg_w_a_bf"][l],
            row(wts["rg_b_a"][l]), wts["rg_w_x_bf"][l], row(wts["rg_b_x"][l]), row(wts["rg_lambda"][l]), t_mid)
        pieces = [d_rgx, d_rgz, d_mlx, d_mlo, d_mlz]
        dx, dscale, dshift, g_ng = _in_bwd(pieces, sv["x"], dx, row(wts["norm_g"][l]), sv["scale"], wts["w_in_g"], l, t_mid)
        half = nd // 2
        w_cols = wts["w_in_g"].shape[3]
        gw_in = jnp.concatenate([_in_bwd_w(pieces, sv["hbf"], w_cols, tuple(range(0, half)), t_big),
                                 _in_bwd_w(pieces, sv["hbf"], w_cols, tuple(range(half, nd)), t_big)], axis=0)
        dmod = jnp.concatenate([dshift[0:1], dscale[0:1], dgate[0:1]], axis=1)
        gw_ada = _ada_bwd_w(cact_col, dmod, nd)
        grads[l] = dict(w_ada=gw_ada, w_in=gw_in, w_out=gw_out, w_qkv=jnp.stack([g_wq, g_wk, g_wv]),
                        rg_conv_w=g_rgcw[0:CONV_WIDTH], ml_conv_w=g_mlcw[0:CONV_WIDTH], wif_t=g_wift[0:8],
                        norm_g=g_ng[0], b_ada=dmod[0], rg_conv_b=g_rgcb[0], rg_w_a=g_wa, rg_b_a=g_ba[0], rg_w_x=g_wx,
                        rg_b_x=g_bx[0], rg_lambda=g_lam[0], ml_conv_b=g_mlcb[0], ml_b_if=g_bif[0, 0:8],
                        ml_norm_g=g_mlng[0])
    return loss_p[0, 0], dx, grads, g_final[0]


REPLICATED = ("norm_g", "b_ada", "rg_conv_b", "rg_w_a", "rg_b_a", "rg_w_x", "rg_b_x", "rg_lambda", "ml_conv_b",
              "ml_b_if", "ml_norm_g", "final_g")
ROW_ALIGN = N_DEV * SUBLANES


def _to_rows(a):
    flat = a.reshape(-1)
    pad = (-flat.shape[0]) % LANES
    return jnp.pad(flat, (0, pad)).reshape(-1, LANES)


def _pack(arrays):
    rows = jnp.concatenate([_to_rows(a) for a in arrays], axis=0)
    return jnp.pad(rows, ((0, (-rows.shape[0]) % ROW_ALIGN), (0, 0)))


def _unpack(rows, like):
    out, at = [], 0
    for a in like:
        n = -(-a.size // LANES)
        out.append(rows[at:at + n].reshape(-1)[:a.size].reshape(a.shape))
        at += n
    return out


def _small_pack(rg_conv_w, ml_conv_w, ml_w_if):
    nl = rg_conv_w.shape[0]
    wif_t = jnp.swapaxes(ml_w_if, 1, 2).reshape(nl, -1, LANES)
    return jnp.concatenate([rg_conv_w, ml_conv_w, wif_t], axis=1)


def _small_unpack(p, if_rows):
    nl = p.shape[0]
    rg_cw = p[:, 0:CONV_WIDTH]
    ml_cw = p[:, CONV_WIDTH:2 * CONV_WIDTH]
    wif = jnp.swapaxes(p[:, 2 * CONV_WIDTH:].reshape(nl, 8, if_rows), 1, 2)
    return rg_cw, ml_cw, wif


def _assemble_weights(big, small, rep):
    w_ada_g, w_in_g, w_out_g, qkv_g = big
    nd, nl = small.shape[0], small.shape[1]
    d = w_in_g.shape[2]
    dh = qkv_g.shape[3]
    nh = d // dh
    rsh = qkv_g.shape[2] // (3 * nh)
    w_qkv = qkv_g.reshape(nd, nl, 3, nh, rsh, dh).transpose(1, 2, 3, 0, 4, 5).reshape(nl, 3, nh, nd * rsh, dh)
    cw = small[:, :, 0:2 * CONV_WIDTH].reshape(nd, nl, 2, CONV_WIDTH, LANES).transpose(1, 2, 3, 0, 4)
    cw = cw.reshape(nl, 2, CONV_WIDTH, nd * LANES)
    if_rows = (small.shape[2] - 2 * CONV_WIDTH) * LANES // 8
    wif_t = small[:, :, 2 * CONV_WIDTH:].reshape(nd, nl, 8, if_rows).transpose(1, 2, 0, 3).reshape(nl, 8, nd * if_rows)
    wift_pad = jnp.pad(wif_t, ((0, 0), (0, LANES - 8), (0, 0))).astype(BF16)
    wif_pad = jnp.swapaxes(wift_pad, 1, 2)
    bif_pad = jnp.pad(rep["ml_b_if"], ((0, 0), (0, LANES - 8))).reshape(nl, 1, LANES)
    wts = dict(rep)
    wts.update(w_ada_g=w_ada_g, w_in_g=w_in_g, w_out_g=w_out_g, w_qkv=w_qkv, rg_conv_w=cw[:, 0], ml_conv_w=cw[:, 1],
               wif_pad=wif_pad, wift_pad=wift_pad, bif_pad=bif_pad, rg_w_a_bf=rep["rg_w_a"].astype(BF16),
               rg_w_x_bf=rep["rg_w_x"].astype(BF16))
    return wts


def _qkv_slots(g_qkv, nd):
    three, nh, dh, _ = g_qkv.shape
    return g_qkv.reshape(three, nh, nd, dh // nd, dh).transpose(2, 0, 1, 3, 4).reshape(nd, three * nh * (dh // nd), dh)


def _small_slots(g):
    nd = N_DEV
    cw = jnp.stack([g["rg_conv_w"], g["ml_conv_w"]]).reshape(2, CONV_WIDTH, nd, LANES).transpose(2, 0, 1, 3)
    cw = cw.reshape(nd, 2 * CONV_WIDTH, LANES)
    wif = g["wif_t"].reshape(8, nd, -1).transpose(1, 0, 2).reshape(nd, -1, LANES)
    return jnp.concatenate([cw, wif], axis=1)


def _kernel_unhosted(x, c, norm_g, w_ada, b_ada, w_in, rg_conv_w, rg_conv_b, rg_w_a, rg_b_a, rg_w_x, rg_b_x, rg_lambda, ml_conv_w, ml_conv_b, ml_w_q, ml_w_k, ml_w_v, ml_w_if, ml_b_if, ml_norm_g, w_out, final_g, loss_target, m_norm_g, m_w_ada, m_b_ada, m_w_in, m_rg_conv_w, m_rg_conv_b, m_rg_w_a, m_rg_b_a, m_rg_w_x, m_rg_b_x, m_rg_lambda, m_ml_conv_w, m_ml_conv_b, m_ml_w_q, m_ml_w_k, m_ml_w_v, m_ml_w_if, m_ml_b_if, m_ml_norm_g, m_w_out, m_final_g, v_norm_g, v_w_ada, v_b_ada, v_w_in, v_rg_conv_w, v_rg_conv_b, v_rg_w_a, v_rg_b_a, v_rg_w_x, v_rg_b_x, v_rg_lambda, v_ml_conv_w, v_ml_conv_b, v_ml_w_q, v_ml_w_k, v_ml_w_v, v_ml_w_if, v_ml_b_if, v_ml_norm_g, v_w_out, v_final_g):
    given = dict(locals())
    nl = w_in.shape[0]
    rep = {n: given[n] for n in REPLICATED}

    def qkv_shard(prefix):
        return jnp.stack([given[prefix + "ml_w_q"], given[prefix + "ml_w_k"], given[prefix + "ml_w_v"]], axis=1).reshape(
            nl, -1, ml_w_q.shape[-1])

    *big, small = _gather_two_level(
        [w_ada.astype(BF16), w_in.astype(BF16), w_out.astype(BF16), qkv_shard("").astype(BF16),
         _small_pack(rg_conv_w, ml_conv_w, ml_w_if)], "gather_weights")
    wts = _assemble_weights(big, small, rep)

    loss_p, grad_x, grads, g_final = _local_step(x[0], c, loss_target[0], wts)
    loss = lax.psum(loss_p, MESH_AXES)

    keys = ("w_ada", "w_in", "w_out", "w_qkv", "small")
    parity = lax.axis_index("c").astype(jnp.int32).reshape(1)
    recv = []
    for l in range(nl):
        g = grads[l]
        parts = [g["w_ada"], g["w_in"], g["w_out"], _qkv_slots(g["w_qkv"], N_DEV), _small_slots(g)]
        other = _core_swap(parts, "core_swap_layer%d" % l)
        sums = [_pair_sum(a, o, parity, "pair_sum_%s_layer%d" % (key, l)) for key, a, o in zip(keys, parts, other)]
        recv.append(_chip_swap(sums, "chip_swap_layer%d" % l))
    shard = {"": dict(w_ada=w_ada, w_in=w_in, w_out=w_out, w_qkv=qkv_shard(""),
                      small=_small_pack(rg_conv_w, ml_conv_w, ml_w_if))}
    for p in ("m_", "v_"):
        shard[p] = dict(w_ada=given[p + "w_ada"], w_in=given[p + "w_in"], w_out=given[p + "w_out"], w_qkv=qkv_shard(p),
                        small=_small_pack(given[p + "rg_conv_w"], given[p + "ml_conv_w"], given[p + "ml_w_if"]))
    res = {}
    for ki, key in enumerate(keys):
        res[key] = _reduce_adam([recv[l][ki] for l in range(nl)], shard[""][key], shard["m_"][key], shard["v_"][key],
                                "reduce_adam_" + key)

    rep_g = dict(final_g=g_final)
    for n in REPLICATED[:-1]:
        rep_g[n] = jnp.stack([grads[l][n] for l in range(nl)])
    pack_g = _pack([rep_g[n] for n in REPLICATED])
    rows = pack_g.shape[0] // N_DEV
    mine = _sum8(_exchange([pack_g.reshape(N_DEV, rows, LANES)], False, "reduce_scatter_replicated")[0], "sum_replicated")
    g_rep = _exchange([mine], True, "gather_replicated")[0].reshape(N_DEV * rows, LANES)
    rep_like = [rep[n] for n in REPLICATED]
    d_rep, m_rep, v_rep = _adam_call(_pack(rep_like), g_rep, _pack([given["m_" + n] for n in REPLICATED]),
                                     _pack([given["v_" + n] for n in REPLICATED]), "adam_replicated")
    rep_out = [dict(zip(REPLICATED, _unpack(a, rep_like))) for a in (g_rep, d_rep, m_rep, v_rep)]

    if_rows = ml_w_if.shape[1]
    order = ("norm_g", "w_ada", "b_ada", "w_in", "rg_conv_w", "rg_conv_b", "rg_w_a", "rg_b_a", "rg_w_x", "rg_b_x",
             "rg_lambda", "ml_conv_w", "ml_conv_b", "ml_w_q", "ml_w_k", "ml_w_v", "ml_w_if", "ml_b_if", "ml_norm_g",
             "w_out", "final_g")
    outs = [loss, grad_x[None]]
    for kind in range(4):
        qkv = res["w_qkv"][kind].reshape((nl, 3) + ml_w_q.shape[1:])
        rg_cw, ml_cw, wif = _small_unpack(res["small"][kind], if_rows)
        sharded = dict(w_ada=res["w_ada"][kind], w_in=res["w_in"][kind], w_out=res["w_out"][kind], ml_w_q=qkv[:, 0],
                       ml_w_k=qkv[:, 1], ml_w_v=qkv[:, 2], rg_conv_w=rg_cw, ml_conv_w=ml_cw, ml_w_if=wif)
        for n in order:
            outs.append(sharded[n] if n in sharded else rep_out[kind][n])
    return tuple(outs)


def _slot(block):
    return 4 * block[0] + 2 * block[1] + block[2]


def _dma_sems(*counts):
    return [pltpu.SemaphoreType.DMA((n,)) for n in counts]


def _start_all(copies):
    for cp in copies:
        cp.start()


def _gather_ici_comm(arrs):
    n = len(arrs)

    def copies(ins, outs, sems):
        send_sems, recv_sems, local_sems = sems
        x, y, c, sibling, chips = _mesh_place()
        me = (x, y, c)
        peers = [(*chip, c) for chip in chips] + [sibling]
        local = [pltpu.make_async_copy(ins[kk], outs[kk].at[_slot(me)], local_sems.at[kk]) for kk in range(n)]
        sends = [_remote(ins[kk], outs[kk].at[_slot(me)], send_sems, recv_sems, kk * 4 + j, peer)
                 for j, peer in enumerate(peers) for kk in range(n)]
        recvs = [_remote(ins[kk], outs[kk].at[_slot(peer)], send_sems, recv_sems, kk * 4 + j, peer)
                 for j, peer in enumerate(peers) for kk in range(n)]
        return local, sends, recvs

    def start(ins, outs, sems):
        local, sends, _ = copies(ins, outs, sems)
        _start_all(sends + local)

    def finish(ins, outs, sems):
        local, sends, recvs = copies(ins, outs, sems)
        for cp in recvs:
            cp.wait_recv()
        for cp in sends:
            cp.wait_send()
        for cp in local:
            cp.wait()

    return _Comm(arrs, [jax.ShapeDtypeStruct((N_DEV,) + a.shape, a.dtype) for a in arrs], _dma_sems(4 * n, 4 * n, n),
                 start, finish)


def _gather_fwd_comm(bufs):
    n = len(bufs)

    def copies(ins, outs, sems):
        send_sems, recv_sems = sems
        _, _, c, sibling, chips = _mesh_place()
        sends = [_remote(ins[kk].at[_slot((*chip, c))], outs[kk].at[_slot((*chip, c))], send_sems, recv_sems, kk * 3 + j, sibling)
                 for j, chip in enumerate(chips) for kk in range(n)]
        recvs = [_remote(ins[kk].at[_slot((*chip, c))], outs[kk].at[_slot((*chip, 1 - c))], send_sems, recv_sems, kk * 3 + j, sibling)
                 for j, chip in enumerate(chips) for kk in range(n)]
        return sends, recvs

    def start(ins, outs, sems):
        _start_all(copies(ins, outs, sems)[0])

    def finish(ins, outs, sems):
        sends, recvs = copies(ins, outs, sems)
        for cp in recvs:
            cp.wait_recv()
        for cp in sends:
            cp.wait_send()

    return _Comm(bufs, [jax.ShapeDtypeStruct(a.shape, a.dtype) for a in bufs], _dma_sems(3 * n, 3 * n), start, finish,
                 aliases=[(i, i) for i in range(n)])


def _core_swap_comm(arrs):
    n = len(arrs)

    def copies(ins, outs, sems):
        send_sems, recv_sems = sems
        _, _, c, sibling, _ = _mesh_place()
        return [_remote(ins[kk].at[2 * q + (1 - c)], outs[kk].at[q], send_sems, recv_sems, kk * N_CHIPS + q, sibling)
                for q in range(N_CHIPS) for kk in range(n)]

    def start(ins, outs, sems):
        _start_all(copies(ins, outs, sems))

    def finish(ins, outs, sems):
        cps = copies(ins, outs, sems)
        for cp in cps:
            cp.wait_recv()
        for cp in cps:
            cp.wait_send()

    return _Comm(arrs, [jax.ShapeDtypeStruct((N_CHIPS,) + a.shape[1:], a.dtype) for a in arrs],
                 _dma_sems(N_CHIPS * n, N_CHIPS * n), start, finish)


def _chip_swap_comm(arrs):
    n = len(arrs)
    per = N_CHIPS - 1

    def copies(ins, outs, sems):
        send_sems, recv_sems, local_sems = sems
        x, y, c, _, chips = _mesh_place()
        mine = 2 * x + y
        sends = [_remote(ins[kk].at[2 * chip[0] + chip[1]], outs[kk].at[mine], send_sems, recv_sems, kk * per + j, (*chip, c))
                 for j, chip in enumerate(chips) for kk in range(n)]
        recvs = [_remote(ins[kk].at[mine], outs[kk].at[2 * chip[0] + chip[1]], send_sems, recv_sems, kk * per + j, (*chip, c))
                 for j, chip in enumerate(chips) for kk in range(n)]
        local = [pltpu.make_async_copy(ins[kk].at[mine], outs[kk].at[mine], local_sems.at[kk]) for kk in range(n)]
        return local, sends, recvs

    def start(ins, outs, sems):
        local, sends, _ = copies(ins, outs, sems)
        _start_all(sends + local)

    def finish(ins, outs, sems):
        local, sends, recvs = copies(ins, outs, sems)
        for cp in recvs:
            cp.wait_recv()
        for cp in sends:
            cp.wait_send()
        for cp in local:
            cp.wait()

    return _Comm(arrs, [jax.ShapeDtypeStruct(a.shape, a.dtype) for a in arrs], _dma_sems(per * n, per * n, n), start, finish)


def _ada_mod(c_all, w_ada, b_cols, comms=None):
    nl, d, w = w_ada.shape

    def body(c_ref, w_ref, b_ref, m_ref, ca_ref):
        sub = _iota((SUBLANES, d), 0)
        cv = jnp.zeros((SUBLANES, d), F32)
        for b in range(N_DEV):
            cv = jnp.where(sub == b, c_ref[b], cv)
        ca = cv * _sigmoid(cv)
        ca_ref[...] = ca
        m_ref[...] = jnp.zeros_like(m_ref)
        for l in range(nl):
            ml = _mm_hi(ca, w_ref[l]) + b_ref[l:l + 1, :]
            for b in range(N_DEV):
                m_ref[b, l:l + 1, :] = _row(ml, b)

    return _call(
        body, comms, name="adaln_mod_columns", grid=(1,),
        in_specs=[_full(c_all.shape), _full(w_ada.shape), _full(b_cols.shape)],
        out_specs=[_full((N_DEV, SUBLANES, w)), _full((SUBLANES, d))],
        out_shape=[jax.ShapeDtypeStruct((N_DEV, SUBLANES, w), F32), jax.ShapeDtypeStruct((SUBLANES, d), F32)],
        args=(c_all, w_ada, b_cols))


def _ada_grad_adam(cact_t, dmods, w, m, v):
    nl, d, wd = w.shape
    tr = _row_tile(d, wd, 8)

    def body(c_ref, dm_ref, w_ref, m_ref, v_ref, g_ref, d_ref, mo_ref, vo_ref):
        cv = c_ref[...]
        dm = dm_ref[0]
        g = _col(cv, 0) * _row(dm, 0)
        for b in range(1, N_DEV):
            g = g + _col(cv, b) * _row(dm, b)
        delta, m2, v2 = _adam_math(w_ref[0], g, m_ref[0], v_ref[0])
        g_ref[0] = g
        d_ref[0] = delta
        mo_ref[0] = m2
        vo_ref[0] = v2

    blk = pl.BlockSpec((1, tr, wd), lambda l, i: (l, i, 0))
    return pl.pallas_call(
        body, name="adaln_grad_adam", grid=(nl, d // tr),
        in_specs=[pl.BlockSpec((tr, N_DEV), lambda l, i: (i, 0)), pl.BlockSpec((1, N_DEV, wd), lambda l, i: (l, 0, 0)),
                  blk, blk, blk],
        out_specs=[blk] * 4, out_shape=[jax.ShapeDtypeStruct((nl, d, wd), F32)] * 4,
        compiler_params=_params(2),
    )(cact_t, dmods, w, m, v)


REP_ROWS = ("norm_g", "dshift", "dscale", "dgate", "rg_conv_b", "rg_b_a", "rg_b_x", "rg_lambda", "ml_conv_b", "ml_norm_g",
            "ml_b_if")


def _sum_parts(recvs, name):
    def body(*refs):
        for r_ref, o_ref in zip(refs[:len(recvs)], refs[len(recvs):]):
            o_ref[...] = _sum_devices(r_ref).astype(o_ref.dtype)

    return pl.pallas_call(
        body, name=name, grid=(1,),
        in_specs=[_full(r.shape) for r in recvs], out_specs=[_full(r.shape[1:]) for r in recvs],
        out_shape=[jax.ShapeDtypeStruct(r.shape[1:], r.dtype) for r in recvs], compiler_params=_params(1),
    )(*recvs)


def _adam_replicated(vp, mp, params, nl):
    d = vp.shape[1]
    nr = len(REP_ROWS)
    names = list(params)
    mat_shape = params["rg_w_a"][0].shape[1:]
    mat_rows = mp.shape[0] // (2 * nl)

    def pieces(name):
        if name == "final_g":
            return [(lambda vp_ref, mp_ref: vp_ref[nl * nr:nl * nr + 1, :], (slice(0, 1), slice(None)))]
        out = []
        for l in range(nl):
            if name in ("rg_w_a", "rg_w_x"):
                at = (2 * l + (name == "rg_w_x")) * mat_rows
                out.append((lambda vp_ref, mp_ref, at=at: mp_ref[at:at + mat_rows, :].astype(F32).reshape(mat_shape), l))
            elif name == "b_ada":
                for j in range(3):
                    r = l * nr + 1 + j
                    out.append((lambda vp_ref, mp_ref, r=r: vp_ref[r:r + 1, :], (slice(l, l + 1), slice(j * d, (j + 1) * d))))
            else:
                r = l * nr + REP_ROWS.index(name)
                cols = slice(0, LANES) if name == "ml_b_if" else slice(None)
                out.append((lambda vp_ref, mp_ref, r=r, cols=cols: vp_ref[r:r + 1, cols], (slice(l, l + 1), slice(None))))
        return out

    def body(*refs):
        vp_ref, mp_ref = refs[:2]
        ins, outs = refs[2:2 + 3 * len(names)], refs[2 + 3 * len(names):]
        for pi, name in enumerate(names):
            w_ref, m_ref, v_ref = ins[3 * pi:3 * pi + 3]
            g_ref, d_ref, mo_ref, vo_ref = outs[4 * pi:4 * pi + 4]
            for get, idx in pieces(name):
                g = get(vp_ref, mp_ref)
                delta, m2, v2 = _adam_math(w_ref[idx], g, m_ref[idx], v_ref[idx])
                g_ref[idx] = g
                d_ref[idx] = delta
                mo_ref[idx] = m2
                vo_ref[idx] = v2

    flat = [a for name in names for a in params[name]]
    out_shape = [jax.ShapeDtypeStruct(params[name][0].shape, F32) for name in names for _ in range(4)]
    res = pl.pallas_call(
        body, name="adam_replicated", grid=(1,),
        in_specs=[_full(vp.shape), _full(mp.shape)] + [_full(a.shape) for a in flat],
        out_specs=[_full(o.shape) for o in out_shape], out_shape=out_shape, compiler_params=_params(1),
    )(vp, mp, *flat)
    return {name: res[4 * pi:4 * pi + 4] for pi, name in enumerate(names)}


class _Plan:
    def __init__(self):
        self.hosted, self.after = {}, {}

    def host(self, key, comm, then=None):
        self.hosted.setdefault(key, []).append(comm)
        if then is not None:
            self.after.setdefault(key, []).append(then)

    def comms(self, key):
        return self.hosted.pop(key, None)

    def done(self, key):
        for fn in self.after.pop(key, []):
            fn()

    def flush(self):
        while self.hosted:
            key = next(iter(self.hosted))
            _call(lambda: None, self.comms(key), name="exchange_after_%s_%d" % key, grid=(1,), in_specs=[], out_specs=[],
                  out_shape=[], args=())
            self.done(key)


VEC_TABLE = ("norm_g", "rg_conv_b", "rg_b_a", "rg_b_x", "rg_lambda", "ml_conv_b", "ml_norm_g")


def _vec_table(rep):
    rows = [rep[n] for n in VEC_TABLE]
    return jnp.stack(rows + [jnp.zeros_like(rows[0])] * (SUBLANES - len(rows)), axis=1)


def _layer_fwd(l, xl, mod3, wl, rep, plan):
    s, d = xl.shape
    t_big, t_mid = _tile_for(s, 512), _tile_for(s, 256)
    nh_ml = rep["ml_b_if"].shape[1] // 2
    vec = lambda name: _vec(rep["vecs"], l, VEC_TABLE.index(name))
    shift, scale, gate = (_vec(mod3, l, kk) for kk in range(3))
    hosted = lambda name: plan.comms((name, l)) if plan else None
    done = lambda name: plan.done((name, l)) if plan else None
    if "w_in_shard" in wl:
        u, hbf, land = _in_fwd_gathering(xl, vec("norm_g"), scale, shift, wl["w_in_shard"], wl["order"], hosted("in_proj_fwd"))
        wl["w_in_g"] = land.reshape(land.shape[0], 1, *land.shape[1:])
    else:
        u, hbf = _in_fwd(xl, vec("norm_g"), scale, shift, wl["w_in_g"], 0, t_mid, hosted("in_proj_fwd"))
    done("in_proj_fwd")
    h_rg, y_rg, *rg_gates = _rg_fwd(u, d, wl["rg_conv_w"], vec("rg_conv_b"), rep["rg_w_a_bf"][l], vec("rg_b_a"),
                                    rep["rg_w_x_bf"][l], vec("rg_b_x"), vec("rg_lambda"), t_mid, hosted("rglru_fwd"))
    done("rglru_fwd")
    q, k, v, gcol, pre = _ml_pre(u, d, wl["ml_conv_w"], vec("ml_conv_b"), wl["w_qkv"][0], wl["w_qkv"][1],
                                 wl["w_qkv"][2], wl["wif_pad"], wl["bif_pad"], t_mid, hosted("mlstm_proj_fwd"))
    done("mlstm_proj_fwd")
    grow = gcol[:, 0:16].T
    cell, y_ml, cs, ns, ms, mt = _ml_cell_fwd(q, k, v, gcol, grow, u, vec("ml_norm_g"), nh_ml, hosted("mlstm_cell_fwd"))
    done("mlstm_cell_fwd")
    x_new, y = _out_fwd(xl, y_rg, y_ml, gate, wl["w_out_g"], 0, t_big, hosted("out_proj_fwd"))
    done("out_proj_fwd")
    saved = dict(x=xl, u=u, hbf=hbf, h_rg=h_rg, y_rg=y_rg, q=q, k=k, v=v, gcol=gcol, grow=grow, cell=cell, y_ml=y_ml,
                 cs=cs, ns=ns, ms=ms, mt=mt, y=y, scale=scale, gate=gate, rg_gates=rg_gates, pre=pre)
    return x_new, saved


def _layer_bwd(l, dx, sv, wl, rep, plan, grads=None, split_last=False):
    s, d = dx.shape
    t_big, t_mid = _tile_for(s, 512), _tile_for(s, 256)
    nh_ml = rep["ml_b_if"].shape[1] // 2
    nd, _, _, w_cols = wl["w_in_g"].shape
    grads = {} if grads is None else grads
    vec = lambda name: _vec(rep["vecs"], l, VEC_TABLE.index(name))
    hosted = lambda name: plan.comms((name, l)) if plan else None
    done = lambda name: plan.done((name, l)) if plan else None
    dy_rg, dy_ml, gw_out, dgate = _out_bwd(dx, sv["gate"], sv["y"], sv["y_rg"], sv["y_ml"], wl["w_out_g"], 0, t_big,
                                           hosted("out_proj_bwd"))
    grads.update(w_out=gw_out)
    done("out_proj_bwd")
    dq, dk, dv, dgates, d_mlo, d_mlz, g_mlng = _ml_cell_bwd(
        dy_ml, sv["u"], sv["cell"], sv["q"], sv["k"], sv["v"], sv["gcol"], sv["grow"], sv["mt"], sv["cs"], sv["ns"],
        sv["ms"], vec("ml_norm_g"), nh_ml, hosted("mlstm_cell_bwd"))
    done("mlstm_cell_bwd")
    d_mlx, g_wq, g_wk, g_wv, g_wift, g_bif, g_mlcw, g_mlcb = _ml_pre_bwd(
        dq, dk, dv, dgates, sv["gcol"], sv["u"], sv["pre"], sv["q"], sv["k"], sv["v"], wl["ml_conv_w"],
        wl["w_qkv"][0], wl["w_qkv"][1], wl["w_qkv"][2], wl["wift_pad"], t_mid, hosted("mlstm_proj_bwd"))
    done("mlstm_proj_bwd")
    d_rgx, d_rgz, g_wa, g_wx, g_ba, g_bx, g_lam, g_rgcw, g_rgcb = _rg_bwd(
        dy_rg, sv["u"], sv["h_rg"], sv["rg_gates"], wl["rg_conv_w"], rep["rg_w_a_bf"][l], rep["rg_w_x_bf"][l],
        vec("rg_lambda"), t_mid, hosted("rglru_bwd"))
    grads.update(w_qkv=jnp.stack([g_wq, g_wk, g_wv]), rg_conv_w=g_rgcw[0:CONV_WIDTH], ml_conv_w=g_mlcw[0:CONV_WIDTH],
                 wif_t=g_wift[0:8], rg_w_a=g_wa, rg_w_x=g_wx)
    acc = dict(dgate=dgate, rg_conv_b=g_rgcb, rg_b_a=g_ba, rg_b_x=g_bx, rg_lambda=g_lam, ml_conv_b=g_mlcb,
               ml_b_if=g_bif, ml_norm_g=g_mlng)
    done("rglru_bwd")
    pieces = [d_rgx, d_rgz, d_mlx, d_mlo, d_mlz]
    grads.update(w_in=_in_bwd_w(pieces, sv["hbf"], w_cols, tuple(range(nd)), t_big, hosted("in_proj_bwd_w")))
    done("in_proj_bwd_w")
    n_tiles = s // t_mid
    counts = [n_tiles // 4, n_tiles - n_tiles // 4 - 1, 1] if split_last and n_tiles >= 4 else [n_tiles]
    in_args = (pieces, sv["x"], dx, vec("norm_g"), sv["scale"], wl["w_in_g"], 0, t_mid)
    res, at = None, 0
    for key, count in zip(("in_proj_bwd_x", "in_proj_bwd_x_rest", "in_proj_bwd_x_end"), counts):
        res = _in_bwd(*in_args, hosted(key), (at, count), res)
        done(key)
        at += count
    dx, dscale, dshift, g_ng = res
    acc.update(norm_g=g_ng, dshift=dshift, dscale=dscale)
    grads.update(acc=acc, dmod=jnp.concatenate([dshift[0:1], dscale[0:1], dgate[0:1]], axis=1))
    return dx, grads


def _local_step(x, c, target, wts):
    d = x.shape[1]
    nl = wts["w_in_g"].shape[1]
    mod, cact = _mod_call(c, wts["w_ada_g"], wts["b_ada"])
    wl = [dict(w_in_g=wts["w_in_g"][:, l:l + 1], w_out_g=wts["w_out_g"][:, l:l + 1], w_qkv=wts["w_qkv"][l],
               rg_conv_w=wts["rg_conv_w"][l], ml_conv_w=wts["ml_conv_w"][l], wif_pad=wts["wif_pad"][l],
               wift_pad=wts["wift_pad"][l], bif_pad=wts["bif_pad"][l]) for l in range(nl)]
    rep = dict(wts, vecs=_vec_table(wts))
    mod3 = mod.reshape(nl, 3, d)
    saved, xl = [], x
    for l in range(nl):
        xl, sv = _layer_fwd(l, xl, mod3, wl[l], rep, None)
        saved.append(sv)
    dx, loss_p, g_final = _loss_call(xl, wts["final_g"].reshape(1, -1), target, _tile_for(x.shape[0], 512))
    grads = [None] * nl
    for l in reversed(range(nl)):
        dx, grads[l] = _layer_bwd(l, dx, saved[l], wl[l], rep, None)
        grads[l]["w_ada"] = _ada_bwd_w(cact[0].reshape(d, 1), grads[l]["dmod"], wts["w_ada_g"].shape[0])
        grads[l]["b_ada"] = grads[l]["dmod"][0]
        grads[l].update({n: a[0] for n, a in grads[l]["acc"].items()})
        grads[l]["ml_b_if"] = grads[l]["ml_b_if"][0:8]
    return loss_p[0, 0], dx, grads, g_final[0]


def _full_qkv(qkv_g, d):
    nd, _, rows3, dh = qkv_g.shape
    nh = d // dh
    rsh = rows3 // (3 * nh)
    return qkv_g.reshape(nd, 3, nh, rsh, dh).transpose(1, 2, 0, 3, 4).reshape(3, nh, nd * rsh, dh)


def _small_weights(small, l, ml_b_if):
    nd = small.shape[0]
    sm = small[:, l]
    cw = sm[:, 0:2 * CONV_WIDTH].reshape(nd, 2, CONV_WIDTH, LANES).transpose(1, 2, 0, 3).reshape(2, CONV_WIDTH, nd * LANES)
    if_rows = (sm.shape[1] - 2 * CONV_WIDTH) * LANES // 8
    wif_t = sm[:, 2 * CONV_WIDTH:].reshape(nd, 8, if_rows).transpose(1, 0, 2).reshape(8, nd * if_rows)
    wift_pad = jnp.pad(wif_t, ((0, LANES - 8), (0, 0))).astype(BF16)
    return dict(rg_conv_w=cw[0], ml_conv_w=cw[1], wift_pad=wift_pad, wif_pad=wift_pad.T,
                bif_pad=jnp.pad(ml_b_if[l], (0, LANES - 8)).reshape(1, LANES))


def kernel(x, c, norm_g, w_ada, b_ada, w_in, rg_conv_w, rg_conv_b, rg_w_a, rg_b_a, rg_w_x, rg_b_x, rg_lambda, ml_conv_w, ml_conv_b, ml_w_q, ml_w_k, ml_w_v, ml_w_if, ml_b_if, ml_norm_g, w_out, final_g, loss_target, m_norm_g, m_w_ada, m_b_ada, m_w_in, m_rg_conv_w, m_rg_conv_b, m_rg_w_a, m_rg_b_a, m_rg_w_x, m_rg_b_x, m_rg_lambda, m_ml_conv_w, m_ml_conv_b, m_ml_w_q, m_ml_w_k, m_ml_w_v, m_ml_w_if, m_ml_b_if, m_ml_norm_g, m_w_out, m_final_g, v_norm_g, v_w_ada, v_b_ada, v_w_in, v_rg_conv_w, v_rg_conv_b, v_rg_w_a, v_rg_b_a, v_rg_w_x, v_rg_b_x, v_rg_lambda, v_ml_conv_w, v_ml_conv_b, v_ml_w_q, v_ml_w_k, v_ml_w_v, v_ml_w_if, v_ml_b_if, v_ml_norm_g, v_w_out, v_final_g):
    given = dict(locals())
    nl = w_in.shape[0]
    d = x.shape[2]
    rep = {n: given[n] for n in REPLICATED}
    rep.update(rg_w_a_bf=rg_w_a.astype(BF16), rg_w_x_bf=rg_w_x.astype(BF16))
    bf = lambda a: a.astype(BF16)

    def qkv_shard(prefix):
        return jnp.stack([given[prefix + "ml_w_q"], given[prefix + "ml_w_k"], given[prefix + "ml_w_v"]], axis=1).reshape(
            nl, -1, ml_w_q.shape[-1])

    def small_shard(prefix):
        return _small_pack(given[prefix + "rg_conv_w"], given[prefix + "ml_conv_w"], given[prefix + "ml_w_if"])

    plan = _Plan()
    qkv = qkv_shard("")
    first_ici = _gather_ici_comm([small_shard("")])
    condition = _exchange_comm([jnp.broadcast_to(c, (SUBLANES, d))], True)
    _run_comms([first_ici, condition], "gather_first")
    first_fwd = _gather_fwd_comm(first_ici.results)
    wcols = w_ada.shape[2]
    mx, my, mc = (lax.axis_index(ax) for ax in MESH_AXES)
    me = 4 * mx + 2 * my + mc
    b_cols = jnp.pad(lax.dynamic_slice_in_dim(b_ada, me * wcols, wcols, axis=1), ((0, SUBLANES - nl), (0, 0)))
    mod_cols, cact_all = _ada_mod(condition.results[0], w_ada, b_cols, [first_fwd])
    small = first_fwd.results[0]
    wl = [_small_weights(small, l, ml_b_if) for l in range(nl)]
    flips = [(0, 0, 0), (0, 0, 1), (1, 0, 0), (0, 1, 0), (1, 1, 0), (1, 0, 1), (0, 1, 1), (1, 1, 1)]
    wl[0]["order"] = jnp.stack([4 * (mx ^ fx) + 2 * (my ^ fy) + (mc ^ fc) for fx, fy, fc in flips]).astype(jnp.int32)
    wl[0]["w_in_shard"] = bf(w_in[0])

    def gather_behind(arrs, ici_host, fwd_host, then):
        ici = _gather_ici_comm(arrs)

        def pass_on():
            fwd = _gather_fwd_comm(ici.results)
            plan.host(fwd_host, fwd, lambda: then(fwd.results))

        plan.host(ici_host, ici, pass_on)

    def got_out(l):
        return lambda r: wl[l].update(w_out_g=r[0], w_qkv=_full_qkv(r[1], d))

    gather_behind([bf(w_out[0:1]), bf(qkv[0:1])], ("in_proj_fwd", 0), ("rglru_fwd", 0), got_out(0))
    for l in range(1, nl):
        gather_behind([bf(w_in[l:l + 1])], ("rglru_fwd", l - 1), ("mlstm_proj_fwd", l - 1),
                      lambda r, l=l: wl[l].update(w_in_g=r[0]))
        gather_behind([bf(w_out[l:l + 1]), bf(qkv[l:l + 1])], ("mlstm_cell_fwd", l - 1), ("out_proj_fwd", l - 1), got_out(l))

    mod_blocks = _exchange([mod_cols], False, "scatter_modulation")[0]
    mod3 = mod_blocks[:, 0:nl].transpose(1, 0, 2).reshape(nl, 3, d)
    rep["vecs"] = _vec_table(rep)
    saved, xl = [], x[0]
    for l in range(nl):
        xl, sv = _layer_fwd(l, xl, mod3, wl[l], rep, plan)
        saved.append(sv)
    grad_x, loss_p, g_final = _loss_call(xl, final_g.reshape(1, -1), loss_target[0], _tile_for(xl.shape[0], 512))

    keys = ("w_in", "w_out", "w_qkv", "small")
    parity = lax.axis_index("c").astype(jnp.int32).reshape(1)
    grads, recv = [None] * nl, [None] * nl

    def parts_of(g):
        return [g["w_in"], g["w_out"], _qkv_slots(g["w_qkv"], N_DEV), _small_slots(g)]

    def pair_sums(l, parts, other):
        return [_pair_sum(a, o, parity, "pair_sum_%s_layer%d" % (key, l)) for key, a, o in zip(keys, parts, other)]

    def reduce_behind(l, host_layer):
        parts = parts_of(grads[l])
        swap = _core_swap_comm(parts)

        def summed():
            sums = pair_sums(l, parts, swap.results)
            big = _chip_swap_comm([sums[0]])
            rest = _chip_swap_comm(sums[1:])
            plan.host(("mlstm_cell_bwd", host_layer), big)
            plan.host(("rglru_bwd", host_layer), rest, lambda: recv.__setitem__(l, big.results + rest.results))

        plan.host(("out_proj_bwd", host_layer), swap, summed)

    first, own = {}, {}

    def reduce_own(names, parts_fn, ready_key, swap_key, chip_key):
        def go():
            parts = parts_fn()
            swap = _core_swap_comm(parts)

            def summed():
                sums = [_pair_sum(a, o, parity, "pair_sum_%s_layer0" % n) for n, a, o in zip(names, parts, swap.results)]
                chip = _chip_swap_comm(sums)
                plan.host(chip_key, chip, lambda: own.update(zip(names, chip.results)))

            plan.host(swap_key, swap, summed)

        plan.after.setdefault(ready_key, []).append(go)

    reduce_own(["w_out"], lambda: [first["w_out"]], ("out_proj_bwd", 0), ("mlstm_cell_bwd", 0), ("mlstm_proj_bwd", 0))
    reduce_own(["w_qkv", "small"], lambda: [_qkv_slots(first["w_qkv"], N_DEV), _small_slots(first)],
               ("rglru_bwd", 0), ("in_proj_bwd_w", 0), ("in_proj_bwd_x", 0))
    reduce_own(["w_in"], lambda: [first["w_in"]], ("in_proj_bwd_w", 0), ("in_proj_bwd_x", 0), ("in_proj_bwd_x_rest", 0))

    for l in reversed(range(nl)):
        if l > 0:
            grad_x, grads[l] = _layer_bwd(l, grad_x, saved[l], wl[l], rep, plan)
            reduce_behind(l, l - 1)
        else:
            grad_x, grads[l] = _layer_bwd(l, grad_x, saved[l], wl[l], rep, plan, first, True)
    plan.flush()
    recv[0] = [own[key] for key in keys]

    shard = {p: dict(w_in=given[p + "w_in"], w_out=given[p + "w_out"], w_qkv=qkv_shard(p), small=small_shard(p))
             for p in ("", "m_", "v_")}
    res = {}
    for ki, key in enumerate(keys):
        res[key] = _reduce_adam([recv[l][ki] for l in range(nl)], shard[""][key], shard["m_"][key], shard["v_"][key],
                                "reduce_adam_" + key)

    dmods = jnp.concatenate([grads[l]["dmod"] for l in range(nl)], axis=0)
    dmod_blocks = jnp.pad(dmods.reshape(nl, N_DEV, wcols).transpose(1, 0, 2), ((0, 0), (0, SUBLANES - nl), (0, 0)))
    dmod_all = _exchange([dmod_blocks], False, "scatter_dmod")[0][:, 0:nl].transpose(1, 0, 2)
    res["w_ada"] = _ada_grad_adam(cact_all.T, dmod_all, w_ada, m_w_ada, v_w_ada)

    widen = lambda a: jnp.pad(a, ((0, 0), (0, d - a.shape[1])))
    rows = [widen(grads[l]["acc"][n][0:1]) for l in range(nl) for n in REP_ROWS] + [g_final[0:1], widen(loss_p[0:1])]
    vp = jnp.concatenate(rows + [jnp.zeros(((-len(rows)) % ROW_ALIGN, d), F32)], axis=0)
    mp = jnp.stack([jnp.stack([grads[l]["rg_w_a"], grads[l]["rg_w_x"]]) for l in range(nl)]).reshape(-1, LANES).astype(BF16)
    got = _exchange([vp.reshape(N_DEV, -1, d), mp.reshape(N_DEV, -1, LANES)], False, "reduce_scatter_replicated")
    vp_r, mp_r = _exchange(_sum_parts(got, "sum_replicated"), True, "gather_replicated")
    vp_r, mp_r = vp_r.reshape(-1, d), mp_r.reshape(-1, LANES)
    lanes = lambda a: jnp.pad(a, ((0, 0), (0, LANES - a.shape[1])))
    shaped = dict(ml_b_if=lanes, final_g=lambda a: a.reshape(1, d))
    names = [n for n in REPLICATED if n != "b_ada"] + ["b_ada"]
    rep_res = _adam_replicated(vp_r, mp_r, {n: tuple(shaped.get(n, lambda a: a)(given[p + n]) for p in ("", "m_", "v_"))
                                            for n in names}, nl)
    unshaped = dict(ml_b_if=lambda a: a[:, 0:ml_b_if.shape[1]], final_g=lambda a: a.reshape(d))
    rep_out = [{n: unshaped.get(n, lambda a: a)(rep_res[n][kind]) for n in names} for kind in range(4)]
    loss = vp_r[nl * len(REP_ROWS) + 1, 0]

    if_rows = ml_w_if.shape[1]
    order = ("norm_g", "w_ada", "b_ada", "w_in", "rg_conv_w", "rg_conv_b", "rg_w_a", "rg_b_a", "rg_w_x", "rg_b_x",
             "rg_lambda", "ml_conv_w", "ml_conv_b", "ml_w_q", "ml_w_k", "ml_w_v", "ml_w_if", "ml_b_if", "ml_norm_g",
             "w_out", "final_g")
    outs = [loss, grad_x[None]]
    for kind in range(4):
        qkv_k = res["w_qkv"][kind].reshape((nl, 3) + ml_w_q.shape[1:])
        rg_cw, ml_cw, wif = _small_unpack(res["small"][kind], if_rows)
        sharded = dict(w_ada=res["w_ada"][kind], w_in=res["w_in"][kind], w_out=res["w_out"][kind], ml_w_q=qkv_k[:, 0],
                       ml_w_k=qkv_k[:, 1], ml_w_v=qkv_k[:, 2], rg_conv_w=rg_cw, ml_conv_w=ml_cw, ml_w_if=wif)
        for n in order:
            outs.append(sharded[n] if n in sharded else rep_out[kind][n])
    return tuple(outs)
```

```python
import functools

import jax
import jax.numpy as jnp
from jax import lax
from jax.experimental import pallas as pl
from jax.experimental.pallas import tpu as pltpu

F32 = jnp.float32
BF16 = jnp.bfloat16
MESH_AXES = ("x", "y", "c")
N_DEV = 8
EPS = 1e-6
RG_C = 8.0
ML_CHUNK = 128
CONV_WIDTH = 4
ADAM_LR = 0.001
ADAM_B1 = 0.9
ADAM_B2 = 0.999
ADAM_EPS = 1e-08
ADAM_WD = 0.01
ADAM_STEP = 10
NEG_BIG = -1e30
LANES = 128
SUBLANES = 8
VMEM_LIMIT = 56 * 1024 * 1024
HI = lax.Precision.HIGHEST


def _params(n_grid):
    return pltpu.CompilerParams(dimension_semantics=("arbitrary",) * n_grid, vmem_limit_bytes=VMEM_LIMIT)


def _mm(a, b):
    return jnp.dot(a.astype(BF16), b.astype(BF16), preferred_element_type=F32)


def _mm_nt(a, b):
    return lax.dot_general(a.astype(BF16), b.astype(BF16), (((1,), (1,)), ((), ())), preferred_element_type=F32)


def _mm_tn(a, b):
    return lax.dot_general(a.astype(BF16), b.astype(BF16), (((0,), (0,)), ((), ())), preferred_element_type=F32)


def _mm_hi(a, b):
    return jnp.dot(a, b, precision=HI, preferred_element_type=F32)


def _sigmoid(x):
    return 1.0 / (1.0 + jnp.exp(-x))


def _softplus(x):
    return jnp.maximum(x, 0.0) + jnp.log(1.0 + jnp.exp(-jnp.abs(x)))


def _neg_expm1(x):
    poly = -x * (1.0 + x * (0.5 + x * (1.0 / 6.0 + x * (1.0 / 24.0 + x * (1.0 / 120.0)))))
    return jnp.where(jnp.abs(x) < 0.05, poly, 1.0 - jnp.exp(x))


def _iota(shape, dim):
    return lax.broadcasted_iota(jnp.int32, shape, dim)


def _colsum(x):
    return jnp.sum(x, axis=0, keepdims=True)


def _rowsum(x):
    return jnp.sum(x, axis=1, keepdims=True)


def _col(x, j):
    return _rowsum(jnp.where(_iota(x.shape, 1) == j, x, 0.0))


def _row(x, j):
    return _colsum(jnp.where(_iota(x.shape, 0) == j, x, 0.0))


def _shift_down(x, j, prev8):
    if j == 0:
        return x
    t = x.shape[0]
    main = jnp.where(_iota(x.shape, 0) >= j, pltpu.roll(x, j, 0), 0.0)
    fix = jnp.where(_iota(prev8.shape, 0) < j, pltpu.roll(prev8, j, 0), 0.0)
    return jnp.concatenate([main[0:SUBLANES] + fix, main[SUBLANES:t]], axis=0)


def _shift_up(x, j, next8):
    if j == 0:
        return x
    t = x.shape[0]
    main = jnp.where(_iota(x.shape, 0) < t - j, pltpu.roll(x, t - j, 0), 0.0)
    fix = jnp.where(_iota(next8.shape, 0) >= SUBLANES - j, pltpu.roll(next8, SUBLANES - j, 0), 0.0)
    return jnp.concatenate([main[0:t - SUBLANES], main[t - SUBLANES:t] + fix], axis=0)


def _conv(x, prev8, w_ref):
    y = w_ref[CONV_WIDTH - 1:CONV_WIDTH, :] * x
    for j in range(1, CONV_WIDTH):
        y = y + w_ref[CONV_WIDTH - 1 - j:CONV_WIDTH - j, :] * _shift_down(x, j, prev8)
    return y


def _conv_bwd(dy, x, next8, w_ref, gw_ref):
    dx = None
    for j in range(CONV_WIDTH):
        k = CONV_WIDTH - 1 - j
        up = _shift_up(dy, j, next8)
        gw_ref[k:k + 1, :] += _colsum(up * x)
        term = w_ref[k:k + 1, :] * up
        dx = term if dx is None else dx + term
    return dx


def _scan_into(a, b, carry, out_ref, reverse):
    t, c = a.shape
    groups = t // SUBLANES
    a3 = a.reshape(groups, SUBLANES, c)
    b3 = b.reshape(groups, SUBLANES, c)
    sub = _iota(a3.shape, 1)
    for step in (1, 2, 4):
        keep = sub < SUBLANES - step if reverse else sub >= step
        shift = SUBLANES - step if reverse else step
        a_s = jnp.where(keep, pltpu.roll(a3, shift, 1), 1.0)
        b_s = jnp.where(keep, pltpu.roll(b3, shift, 1), 0.0)
        b3 = a3 * b_s + b3
        a3 = a3 * a_s
    for g in (reversed(range(groups)) if reverse else range(groups)):
        rows = slice(g * SUBLANES, (g + 1) * SUBLANES)
        out_ref[rows, :] = b3[g] + a3[g] * carry
        edge = g * SUBLANES if reverse else (g + 1) * SUBLANES - 1
        carry = out_ref[edge:edge + 1, :]


def _blockdiag(x, w_ref, transpose_w=False):
    nh, dh, _ = w_ref.shape
    outs = []
    for h in range(nh):
        xs = x[:, h * dh:(h + 1) * dh]
        outs.append(_mm_nt(xs, w_ref[h]) if transpose_w else _mm(xs, w_ref[h]))
    return jnp.concatenate(outs, axis=1)


def _rg_gates(xc, wa_ref, ba_ref, wx_ref, bx_ref, lam_ref):
    r = _sigmoid(_blockdiag(xc, wa_ref) + ba_ref[...])
    ig = _sigmoid(_blockdiag(xc, wx_ref) + bx_ref[...])
    sp = _softplus(-lam_ref[...])
    log_a = -RG_C * r * sp
    a = jnp.exp(log_a)
    beta = jnp.sqrt(_neg_expm1(2.0 * log_a))
    return r, ig, sp, a, beta


def _bcast8(row):
    return jnp.broadcast_to(row, (SUBLANES, row.shape[1]))


def _full(shape):
    nd = len(shape)
    return pl.BlockSpec(shape, lambda *_: (0,) * nd)


class _Comm:
    def __init__(self, arrays, out_shapes, sems, start, finish, aliases=()):
        self.arrays, self.out_shapes, self.sems = list(arrays), list(out_shapes), list(sems)
        self.start, self.finish, self.aliases = start, finish, tuple(aliases)
        self.results = None


class _RowOf:
    def __init__(self, ref, k):
        self.ref, self.k = ref, k

    def __getitem__(self, idx):
        cols = slice(None) if idx is Ellipsis else idx[1]
        return self.ref[0, self.k:self.k + 1, cols]


def _vec(table, layer, k):
    return ("row", table, layer, k)


def _is_row(arg):
    return isinstance(arg, tuple) and len(arg) == 4 and arg[0] == "row"


def _call(body, comms, *, name, grid, in_specs, out_specs, out_shape, args, scratch_shapes=(), aliases=None, prefetch=()):
    comms = [cm for cm in (comms or []) if cm is not None]
    rows = {i: a[3] for i, a in enumerate(args) if _is_row(a)}
    in_specs = [pl.BlockSpec((1,) + a[1].shape[1:], functools.partial(lambda layer, *_: (layer, 0, 0), a[2]))
                if _is_row(a) else sp for a, sp in zip(args, in_specs)]
    args = tuple(a[1] if _is_row(a) else a for a in args)
    n_in, n_out, n_sc = len(args), len(out_shape), len(scratch_shapes)
    c_arrays = [a for cm in comms for a in cm.arrays]
    c_outs = [o for cm in comms for o in cm.out_shapes]
    c_sems = [sm for cm in comms for sm in cm.sems]
    aliases, a_at, o_at = dict(aliases or {}), n_in, n_out
    for cm in comms:
        for (i, j) in cm.aliases:
            aliases[a_at + i] = o_at + j
        a_at += len(cm.arrays)
        o_at += len(cm.out_shapes)

    def wrapped(*refs):
        pre, refs = refs[:len(prefetch)], refs[len(prefetch):]
        ins, c_in = refs[:n_in], refs[n_in:n_in + len(c_arrays)]
        ins = [_RowOf(r, rows[i]) if i in rows else r for i, r in enumerate(ins)]
        at = n_in + len(c_arrays)
        outs, c_out = refs[at:at + n_out], refs[at + n_out:at + n_out + len(c_outs)]
        at += n_out + len(c_outs)
        scr, sems = refs[at:at + n_sc], refs[at + n_sc:]
        views, ia, io, isem = [], 0, 0, 0
        for cm in comms:
            views.append((c_in[ia:ia + len(cm.arrays)], c_out[io:io + len(cm.out_shapes)], sems[isem:isem + len(cm.sems)]))
            ia, io, isem = ia + len(cm.arrays), io + len(cm.out_shapes), isem + len(cm.sems)
        if comms:
            @pl.when(pl.program_id(0) == 0)
            def _():
                for cm, view in zip(comms, views):
                    cm.start(*view)

        body(*pre, *ins, *outs, *scr)
        if comms:
            @pl.when(pl.program_id(0) == grid[0] - 1)
            def _():
                for cm, view in zip(comms, views):
                    cm.finish(*view)

    hbm = pl.BlockSpec(memory_space=pl.ANY)
    specs = dict(grid=grid, in_specs=list(in_specs) + [hbm] * len(c_arrays), out_specs=list(out_specs) + [hbm] * len(c_outs),
                 scratch_shapes=list(scratch_shapes) + c_sems)
    if prefetch:
        specs = dict(grid_spec=pltpu.PrefetchScalarGridSpec(num_scalar_prefetch=len(prefetch), **specs))
        aliases = {i + len(prefetch): o for i, o in aliases.items()}
    res = pl.pallas_call(
        wrapped, name=name, out_shape=list(out_shape) + c_outs, input_output_aliases=aliases,
        compiler_params=_params(len(grid)), **specs,
    )(*prefetch, *args, *c_arrays)
    at = n_out
    for cm in comms:
        cm.results = list(res[at:at + len(cm.out_shapes)])
        at += len(cm.out_shapes)
    return list(res[:n_out])


def _mod_call(c, w_ada_g, b_ada):
    nd, nl, d, w = w_ada_g.shape

    def body(c_ref, w_ref, b_ref, mod_ref, cact_ref):
        cv = c_ref[...]
        ca = _bcast8(cv * _sigmoid(cv))
        cact_ref[...] = ca
        mod_ref[0, 0] = _mm(ca, w_ref[0, 0]) + b_ref[0, 0]

    mod, cact = pl.pallas_call(
        body, name="adaln_mod", grid=(nl, nd),
        in_specs=[_full((1, d)),
                  pl.BlockSpec((1, 1, d, w), lambda l, j: (j, l, 0, 0)),
                  pl.BlockSpec((1, 1, 1, w), lambda l, j: (l, j, 0, 0))],
        out_specs=[pl.BlockSpec((1, 1, SUBLANES, w), lambda l, j: (l, j, 0, 0)), _full((SUBLANES, d))],
        out_shape=[jax.ShapeDtypeStruct((nl, nd, SUBLANES, w), F32), jax.ShapeDtypeStruct((SUBLANES, d), F32)],
        compiler_params=_params(2),
    )(c, w_ada_g, b_ada.reshape(nl, nd, 1, w))
    return mod[:, :, 0, :].reshape(nl, nd * w), cact


def _join_columns(w_ref, wcat):
    nd, _, _, w = w_ref.shape

    @pl.when(pl.program_id(0) == 0)
    def _():
        for j in range(nd):
            wcat[:, j * w:(j + 1) * w] = w_ref[j, 0]


def _in_fwd(x, ng, scale, shift, w_in_g, layer, tile, comms=None):
    s, d = x.shape
    nd, _, _, w = w_in_g.shape

    def body(x_ref, ng_ref, sc_ref, sh_ref, w_ref, u_ref, h_ref, wcat):
        _join_columns(w_ref, wcat)
        xv = x_ref[...]
        rs = lax.rsqrt(jnp.mean(xv * xv, axis=1, keepdims=True) + EPS)
        hb = (xv * rs * ng_ref[...] * (1.0 + sc_ref[...]) + sh_ref[...]).astype(BF16)
        h_ref[...] = hb
        u_ref[...] = jnp.dot(hb, wcat[...], preferred_element_type=F32)

    return _call(
        body, comms, name="in_proj_fwd", grid=(s // tile,),
        in_specs=[pl.BlockSpec((tile, d), lambda i: (i, 0)), _full((1, d)), _full((1, d)), _full((1, d)),
                  pl.BlockSpec((nd, 1, d, w), lambda i: (0, layer, 0, 0), pipeline_mode=pl.Buffered(1))],
        out_specs=[pl.BlockSpec((tile, nd * w), lambda i: (i, 0)), pl.BlockSpec((tile, d), lambda i: (i, 0))],
        out_shape=[jax.ShapeDtypeStruct((s, nd * w), F32), jax.ShapeDtypeStruct((s, d), BF16)],
        scratch_shapes=[pltpu.VMEM((d, nd * w), BF16)],
        args=(x, ng, scale, shift, w_in_g))


def _in_fwd_gathering(x, ng, scale, shift, w_shard, order, comms=None):
    s, d = x.shape
    w = w_shard.shape[1]
    rows = min(512, s)

    def body(order_ref, x_ref, ng_ref, sc_ref, sh_ref, wsh_ref, u_ref, h_ref, land_ref, wbuf, xbuf, send_sems, recv_sems,
             dma_sems):
        k = pl.program_id(0)
        mx, my, mc, sibling, chips = _mesh_place()
        me = (mx, my, mc)
        peers = [sibling] + [(*chip, mc) for chip in chips]
        blocks = peers + [(*chip, 1 - mc) for chip in chips]

        def own_send(j):
            return _remote(wsh_ref, land_ref.at[_slot(me)], send_sems, recv_sems, j, peers[j])

        def landing(j):
            return _remote(wsh_ref, land_ref.at[_slot(blocks[j])], send_sems, recv_sems, j, sibling)

        def pass_on(j):
            at = land_ref.at[_slot((*chips[j - 4], mc))]
            return _remote(at, at, send_sems, recv_sems, j, sibling)

        def load(src, slot):
            cp = pltpu.make_async_copy(src, wbuf.at[slot], dma_sems.at[slot])
            cp.start()
            cp.wait()

        @pl.when(k == 0)
        def _():
            for j in range(4):
                own_send(j).start()
            keep = pltpu.make_async_copy(wsh_ref, land_ref.at[_slot(me)], dma_sems.at[2])
            keep.start()
            fetch = lambda i: pltpu.make_async_copy(x_ref.at[i * rows:(i + 1) * rows, :], xbuf.at[i % 2], dma_sems.at[3 + i % 2])
            fetch(0).start()
            for i in range(s // rows):
                if (i + 1) * rows < s:
                    fetch(i + 1).start()
                fetch(i).wait()
                xv = xbuf[i % 2]
                rs = lax.rsqrt(jnp.mean(xv * xv, axis=1, keepdims=True) + EPS)
                h_ref[i * rows:(i + 1) * rows, :] = (xv * rs * ng_ref[...] * (1.0 + sc_ref[...]) + sh_ref[...]).astype(BF16)
            load(wsh_ref, 0)
            keep.wait()

        for step in range(1, N_DEV):
            @pl.when(k == step)
            def _(step=step):
                j = step - 1
                landing(j).wait_recv()
                if 1 <= j <= 3:
                    pass_on(j + 3).start()
                load(land_ref.at[_slot(blocks[j])], step % 2)

        u_ref[...] = jnp.dot(h_ref[...], wbuf[k % 2], preferred_element_type=F32)

        @pl.when(k == N_DEV - 1)
        def _():
            for j in range(4):
                own_send(j).wait_send()
            for j in range(4, N_DEV - 1):
                pass_on(j).wait_send()

    one = pltpu.SemaphoreType.DMA
    u, hbf, land = _call(
        body, comms, name="in_proj_fwd", grid=(N_DEV,), prefetch=(order,),
        in_specs=[pl.BlockSpec(memory_space=pl.ANY), _full((1, d)), _full((1, d)), _full((1, d)),
                  pl.BlockSpec(memory_space=pl.ANY)],
        out_specs=[pl.BlockSpec((s, w), lambda k, o: (0, o[k])),
                   pl.BlockSpec((s, d), lambda k, o: (0, 0), pipeline_mode=pl.Buffered(1)), pl.BlockSpec(memory_space=pl.ANY)],
        out_shape=[jax.ShapeDtypeStruct((s, N_DEV * w), F32), jax.ShapeDtypeStruct((s, d), BF16),
                   jax.ShapeDtypeStruct((N_DEV, d, w), BF16)],
        scratch_shapes=[pltpu.VMEM((2, d, w), BF16), pltpu.VMEM((2, rows, d), F32), one((N_DEV - 1,)), one((N_DEV - 1,)),
                        one((5,))],
        args=(x, ng, scale, shift, w_shard))
    return u, hbf, land


def _rg_fwd(u, d, conv_w, conv_b, w_a, b_a, w_x, b_x, lam, tile, comms=None):
    s = u.shape[0]

    def body(x_ref, z_ref, cw_ref, cb_ref, wa_ref, ba_ref, wx_ref, bx_ref, lam_ref,
             h_ref, y_ref, xc_ref, r_ref, i_ref, a_ref, beta_ref, prev8, hcar):
        @pl.when(pl.program_id(0) == 0)
        def _():
            prev8[...] = jnp.zeros_like(prev8)
            hcar[...] = jnp.zeros_like(hcar)

        x = x_ref[...]
        xc = _conv(x, prev8[...], cw_ref) + cb_ref[...]
        prev8[...] = x[tile - SUBLANES:tile, :]
        r, ig, _, a, beta = _rg_gates(xc, wa_ref, ba_ref, wx_ref, bx_ref, lam_ref)
        xc_ref[...] = xc
        r_ref[...] = r
        i_ref[...] = ig
        a_ref[...] = a
        beta_ref[...] = beta
        _scan_into(a, beta * ig * xc, hcar[SUBLANES - 1:SUBLANES, :], h_ref, False)
        h = h_ref[...]
        hcar[...] = h[tile - SUBLANES:tile, :]
        z = z_ref[...]
        y_ref[...] = (h * z * _sigmoid(z)).astype(BF16)

    vec = _full((1, d))
    return _call(
        body, comms, name="rglru_fwd", grid=(s // tile,),
        in_specs=[pl.BlockSpec((tile, d), lambda i: (i, 0)), pl.BlockSpec((tile, d), lambda i: (i, 1)),
                  _full(conv_w.shape), vec, _full(w_a.shape), vec, _full(w_x.shape), vec, vec],
        out_specs=[pl.BlockSpec((tile, d), lambda i: (i, 0))] * 7,
        out_shape=[jax.ShapeDtypeStruct((s, d), F32), jax.ShapeDtypeStruct((s, d), BF16)] + [jax.ShapeDtypeStruct((s, d), F32)] * 5,
        scratch_shapes=[pltpu.VMEM((SUBLANES, d), F32), pltpu.VMEM((SUBLANES, d), F32)],
        args=(u, u, conv_w, conv_b, w_a, b_a, w_x, b_x, lam))


def _ml_pre(u, d, conv_w, conv_b, w_q, w_k, w_v, wif, bif, tile, comms=None):
    s = u.shape[0]
    nh = w_q.shape[0]

    def body(x_ref, cw_ref, cb_ref, wq_ref, wk_ref, wv_ref, wif_ref, bif_ref, q_ref, k_ref, v_ref, g_ref, pre_ref, prev8):
        @pl.when(pl.program_id(0) == 0)
        def _():
            prev8[...] = jnp.zeros_like(prev8)

        x = x_ref[...]
        pre = _conv(x, prev8[...], cw_ref) + cb_ref[...]
        prev8[...] = x[tile - SUBLANES:tile, :]
        xc = pre * _sigmoid(pre)
        q = _blockdiag(xc, wq_ref)
        k = _blockdiag(xc, wk_ref)
        v = _blockdiag(x, wv_ref)
        pre_ref[...] = pre
        q_ref[...] = q
        k_ref[...] = k
        v_ref[...] = v
        g = _mm(q, wif_ref[0:d, :]) + _mm(k, wif_ref[d:2 * d, :]) + _mm(v, wif_ref[2 * d:3 * d, :]) + bif_ref[...]
        lane = _iota(g.shape, 1)
        gl = jnp.where(lane < 4, g, jnp.where(lane < 8, -_softplus(-g), 0.0))
        tri = jnp.where(_iota((ML_CHUNK, ML_CHUNK), 1) <= _iota((ML_CHUNK, ML_CHUNK), 0), 1.0, 0.0)
        cums = [_mm_hi(tri, gl[c * ML_CHUNK:(c + 1) * ML_CHUNK, :]) for c in range(tile // ML_CHUNK)]
        cum = cums[0] if len(cums) == 1 else jnp.concatenate(cums, axis=0)
        g_ref[...] = gl + jnp.where((lane >= 8) & (lane < 12), pltpu.roll(cum, 4, 1), 0.0)

    vec = _full((1, d))
    return _call(
        body, comms, name="mlstm_proj_fwd", grid=(s // tile,),
        in_specs=[pl.BlockSpec((tile, d), lambda i: (i, 2)), _full(conv_w.shape), vec,
                  _full(w_q.shape), _full(w_k.shape), _full(w_v.shape), _full(wif.shape), _full((1, LANES))],
        out_specs=[pl.BlockSpec((tile, d), lambda i: (i, 0))] * 3 + [pl.BlockSpec((tile, LANES), lambda i: (i, 0)),
                                                                     pl.BlockSpec((tile, d), lambda i: (i, 0))],
        out_shape=[jax.ShapeDtypeStruct((s, d), F32)] * 3 + [jax.ShapeDtypeStruct((s, LANES), F32),
                                                             jax.ShapeDtypeStruct((s, d), F32)],
        scratch_shapes=[pltpu.VMEM((SUBLANES, d), F32)],
        args=(u, conv_w, conv_b, w_q, w_k, w_v, wif, bif))


def _cell_chunk(h, nh, q_ref, k_ref, v_ref, gc, gr, m_prev, c_h, n_h, m_t=None):
    lc = ML_CHUNK
    dh = q_ref.shape[1] // nh
    sl = slice(h * dh, (h + 1) * dh)
    qh = q_ref[:, sl]
    kh = k_ref[:, sl] * (dh ** -0.5)
    vh = v_ref[:, sl]
    li_c = _col(gc, h)
    b_c = _col(gc, 8 + h)
    lib_r = _row(gr, h) - _row(gr, 8 + h)
    b_last = _colsum(jnp.where(_iota((lc, 1), 0) == lc - 1, b_c, 0.0))
    causal = _iota((lc, lc), 1) <= _iota((lc, lc), 0)
    dmat = jnp.where(causal, b_c + lib_r, NEG_BIG)
    m_inter = b_c + m_prev
    if m_t is None:
        m_t = jnp.maximum(m_inter, jnp.max(dmat, axis=1, keepdims=True))
    w_intra = jnp.exp(dmat - m_t)
    w_inter = jnp.exp(m_inter - m_t)
    amat = _mm_nt(qh, kh)
    smat = amat * w_intra
    qc = _mm(qh, c_h)
    qn = _rowsum(qh * n_h)
    den = _rowsum(smat) + w_inter * qn
    gst = b_last - b_c + li_c
    m_new = jnp.maximum(b_last + m_prev, jnp.max(gst, axis=0, keepdims=True))
    w_state = jnp.exp(gst - m_new)
    decay = jnp.exp(b_last + m_prev - m_new)
    return dict(sl=sl, qh=qh, kh=kh, vh=vh, m_t=m_t, w_intra=w_intra, w_inter=w_inter, smat=smat, qc=qc, qn=qn,
                den=den, m_new=m_new, w_state=w_state, decay=decay)


def _ml_cell_fwd(q, k, v, gcol, grow, u, ng, nh, comms=None):
    s, d = q.shape
    lc = ML_CHUNK
    nc = s // lc
    dh = d // nh

    def body(q_ref, k_ref, v_ref, gc_ref, gr_ref, o_ref, z_ref, ng_ref,
             cell_ref, y_ref, cs_ref, ns_ref, ms_ref, mt_ref, c_sc, n_sc, m_sc):
        @pl.when(pl.program_id(0) == 0)
        def _():
            c_sc[...] = jnp.zeros_like(c_sc)
            n_sc[...] = jnp.zeros_like(n_sc)
            m_sc[...] = jnp.zeros_like(m_sc)

        gc = gc_ref[...]
        gr = gr_ref[...]
        lane = _iota((lc, LANES), 1)
        mt_acc = jnp.zeros((lc, LANES), F32)
        for h in range(nh):
            c_h = c_sc[h]
            n_h = n_sc[h, 0:1, :]
            m_prev = jnp.max(m_sc[h, 0:1, :], axis=1, keepdims=True)
            cs_ref[0, h] = c_h
            ns_ref[0, h] = n_sc[h]
            ms_ref[0, h] = m_sc[h]
            t = _cell_chunk(h, nh, q_ref, k_ref, v_ref, gc, gr, m_prev, c_h, n_h)
            sl = t["sl"]
            num = _mm(t["smat"], t["vh"]) + t["w_inter"] * t["qc"]
            cell_h = num / jnp.maximum(jnp.abs(t["den"]), jnp.exp(-t["m_t"]))
            mt_acc = jnp.where(lane == h, t["m_t"], mt_acc)
            kw = t["kh"] * t["w_state"]
            c_sc[h] = t["decay"] * c_h + _mm_tn(kw, t["vh"])
            n_sc[h] = _bcast8(t["decay"] * n_h + _colsum(kw))
            m_sc[h] = jnp.broadcast_to(t["m_new"], (SUBLANES, LANES))
            hg = _sigmoid(o_ref[:, sl]) * cell_h
            hn = hg * lax.rsqrt(jnp.mean(hg * hg, axis=1, keepdims=True) + EPS)
            z = z_ref[:, sl]
            cell_ref[:, sl] = cell_h
            y_ref[:, sl] = (hn * ng_ref[:, sl] * z * _sigmoid(z)).astype(BF16)
        mt_ref[...] = mt_acc

    tok = pl.BlockSpec((lc, d), lambda c: (c, 0))
    return _call(
        body, comms, name="mlstm_cell_fwd", grid=(nc,),
        in_specs=[tok, tok, tok, pl.BlockSpec((lc, LANES), lambda c: (c, 0)), pl.BlockSpec((16, lc), lambda c: (0, c)),
                  pl.BlockSpec((lc, d), lambda c: (c, 3)), pl.BlockSpec((lc, d), lambda c: (c, 4)), _full((1, d))],
        out_specs=[tok, tok, pl.BlockSpec((1, nh, dh, dh), lambda c: (c, 0, 0, 0)),
                   pl.BlockSpec((1, nh, SUBLANES, dh), lambda c: (c, 0, 0, 0)),
                   pl.BlockSpec((1, nh, SUBLANES, LANES), lambda c: (c, 0, 0, 0)),
                   pl.BlockSpec((lc, LANES), lambda c: (c, 0))],
        out_shape=[jax.ShapeDtypeStruct((s, d), F32), jax.ShapeDtypeStruct((s, d), BF16),
                   jax.ShapeDtypeStruct((nc, nh, dh, dh), F32), jax.ShapeDtypeStruct((nc, nh, SUBLANES, dh), F32),
                   jax.ShapeDtypeStruct((nc, nh, SUBLANES, LANES), F32), jax.ShapeDtypeStruct((s, LANES), F32)],
        scratch_shapes=[pltpu.VMEM((nh, dh, dh), F32), pltpu.VMEM((nh, SUBLANES, dh), F32),
                        pltpu.VMEM((nh, SUBLANES, LANES), F32)],
        args=(q, k, v, gcol, grow, u, u, ng))


def _out_fwd(x, y_rg, y_ml, gate, w_out_g, layer, tile, comms=None):
    s, d = x.shape
    nd, _, r, _ = w_out_g.shape

    def body(x_ref, yr_ref, ym_ref, g_ref, w_ref, xn_ref, y_ref):
        ycat = jnp.concatenate([yr_ref[...].astype(BF16), ym_ref[...].astype(BF16)], axis=1)
        acc = jnp.dot(ycat, w_ref[...].reshape(nd * r, d), preferred_element_type=F32)
        y_ref[...] = acc
        xn_ref[...] = x_ref[...] + g_ref[...] * acc

    tok = pl.BlockSpec((tile, d), lambda i: (i, 0))
    return _call(
        body, comms, name="out_proj_fwd", grid=(s // tile,),
        in_specs=[tok, tok, tok, _full((1, d)), pl.BlockSpec((nd, 1, r, d), lambda i: (0, layer, 0, 0))],
        out_specs=[tok, tok],
        out_shape=[jax.ShapeDtypeStruct((s, d), F32)] * 2,
        args=(x, y_rg, y_ml, gate, w_out_g))


def _loss_call(x, fg, target, tile):
    s, d = x.shape

    def body(x_ref, g_ref, t_ref, dx_ref, loss_ref, gg_ref):
        @pl.when(pl.program_id(0) == 0)
        def _():
            loss_ref[...] = jnp.zeros_like(loss_ref)
            gg_ref[...] = jnp.zeros_like(gg_ref)

        xv = x_ref[...]
        g = g_ref[...]
        rs = lax.rsqrt(jnp.mean(xv * xv, axis=1, keepdims=True) + EPS)
        xh = xv * rs
        e = xh * g - t_ref[...]
        loss_ref[...] += jnp.broadcast_to(_colsum(_rowsum(e * e)) * (0.5 / d), loss_ref.shape)
        dy = e * (1.0 / d)
        gg_ref[...] += _bcast8(_colsum(dy * xh))
        dxh = dy * g
        dx_ref[...] = rs * (dxh - xh * jnp.mean(dxh * xh, axis=1, keepdims=True))

    tok = pl.BlockSpec((tile, d), lambda i: (i, 0))
    return pl.pallas_call(
        body, name="final_norm_loss", grid=(s // tile,),
        in_specs=[tok, _full((1, d)), tok],
        out_specs=[tok, _full((SUBLANES, LANES)), _full((SUBLANES, d))],
        out_shape=[jax.ShapeDtypeStruct((s, d), F32), jax.ShapeDtypeStruct((SUBLANES, LANES), F32),
                   jax.ShapeDtypeStruct((SUBLANES, d), F32)],
        compiler_params=_params(1),
    )(x, fg, target)


def _ml_out_stage_bwd(dy, cell, o, z, ng):
    so = _sigmoid(o)
    hg = so * cell
    rinv = lax.rsqrt(jnp.mean(hg * hg, axis=1, keepdims=True) + EPS)
    hn = hg * rinv
    sz = _sigmoid(z)
    dz = dy * hn * ng * (sz + z * sz * (1.0 - sz))
    dymid = dy * z * sz
    dhn = dymid * ng
    dhg = rinv * (dhn - hn * jnp.mean(dhn * hn, axis=1, keepdims=True))
    return dz, dhg * cell * so * (1.0 - so), dhg * so, _colsum(dymid * hn)


def _out_bwd_with_output_stage(dxo, gate, y, y_rg, y_ml, cell, u, ng, w_out_g, layer, tile, nh, comms=None):
    s, d = dxo.shape
    nd, _, r, _ = w_out_g.shape
    dh = d // nh

    def body(dx_ref, g_ref, y_ref, yr_ref, ym_ref, cell_ref, o_ref, z_ref, ng_ref, w_ref,
             dyr_ref, dcell_ref, do_ref, dz_ref, gw_ref, dg_ref, gng_ref):
        @pl.when(pl.program_id(0) == 0)
        def _():
            for ref in (gw_ref, dg_ref, gng_ref):
                ref[...] = jnp.zeros_like(ref)

        dxv = dx_ref[...]
        dg_ref[...] += _bcast8(_colsum(dxv * y_ref[...]))
        dyb = (dxv * g_ref[...]).astype(BF16)
        dycat = lax.dot_general(dyb, w_ref[...].reshape(nd * r, d), (((1,), (1,)), ((), ())), preferred_element_type=F32)
        dyr_ref[...] = dycat[:, 0:d]
        ycat = jnp.concatenate([yr_ref[...].astype(BF16), ym_ref[...].astype(BF16)], axis=1)
        gw_ref[...] += lax.dot_general(ycat, dyb, (((0,), (0,)), ((), ())), preferred_element_type=F32).reshape(nd, r, d)
        for h in range(nh):
            sl = slice(h * dh, (h + 1) * dh)
            dz, do, dcell, gng = _ml_out_stage_bwd(dycat[:, d + h * dh:d + (h + 1) * dh], cell_ref[:, sl], o_ref[:, sl],
                                                   z_ref[:, sl], ng_ref[:, sl])
            dz_ref[:, sl] = dz.astype(BF16)
            do_ref[:, sl] = do.astype(BF16)
            dcell_ref[:, sl] = dcell
            gng_ref[:, sl] += _bcast8(gng)

    tok = pl.BlockSpec((tile, d), lambda i: (i, 0))
    acc = _full((SUBLANES, d))
    return _call(
        body, comms, name="out_proj_bwd", grid=(s // tile,),
        in_specs=[tok, _full((1, d)), tok, tok, tok, tok, pl.BlockSpec((tile, d), lambda i: (i, 3)),
                  pl.BlockSpec((tile, d), lambda i: (i, 4)), _full((1, d)),
                  pl.BlockSpec((nd, 1, r, d), lambda i: (0, layer, 0, 0), pipeline_mode=pl.Buffered(1))],
        out_specs=[tok, tok, tok, tok, pl.BlockSpec((nd, r, d), lambda i: (0, 0, 0), pipeline_mode=pl.Buffered(1)), acc, acc],
        out_shape=[jax.ShapeDtypeStruct((s, d), F32)] * 2 + [jax.ShapeDtypeStruct((s, d), BF16)] * 2
        + [jax.ShapeDtypeStruct((nd, r, d), F32)] + [jax.ShapeDtypeStruct((SUBLANES, d), F32)] * 2,
        args=(dxo, gate, y, y_rg, y_ml, cell, u, u, ng, w_out_g))


def _ml_cell_only_bwd(dcell, cell, q, k, v, gcol, grow, mt, cs, ns, ms, nh, comms=None):
    s, d = q.shape
    lc = ML_CHUNK
    nc = s // lc
    dh = d // nh

    def body(dcell_ref, cell_ref, q_ref, k_ref, v_ref, gc_ref, gr_ref, mt_ref, cs_ref, ns_ref, ms_ref,
             dq_ref, dk_ref, dv_ref, dg_ref, dc_sc, dn_sc):
        @pl.when(pl.program_id(0) == 0)
        def _():
            dc_sc[...] = jnp.zeros_like(dc_sc)
            dn_sc[...] = jnp.zeros_like(dn_sc)

        gc = gc_ref[...]
        gr = gr_ref[...]
        mtv = mt_ref[...]
        lane = _iota((lc, LANES), 1)
        rowv = _iota((lc, 1), 0)
        dg_acc = jnp.zeros((lc, LANES), F32)
        for h in range(nh):
            c_h = cs_ref[0, h]
            n_h = ns_ref[0, h, 0:1, :]
            m_prev = jnp.max(ms_ref[0, h, 0:1, :], axis=1, keepdims=True)
            t = _cell_chunk(h, nh, q_ref, k_ref, v_ref, gc, gr, m_prev, c_h, n_h, m_t=_col(mtv, h))
            sl, qh, kh, vh = t["sl"], t["qh"], t["kh"], t["vh"]
            w_intra, w_inter, smat, w_state, decay = t["w_intra"], t["w_inter"], t["smat"], t["w_state"], t["decay"]
            cell_h = cell_ref[:, sl]
            dcell = dcell_ref[:, sl]
            eneg = jnp.exp(-t["m_t"])
            aden = jnp.abs(t["den"])
            nst = jnp.maximum(aden, eneg)
            dnum = dcell / nst
            dden = jnp.where(aden > eneg, -_rowsum(cell_h * dcell) / nst * jnp.sign(t["den"]), 0.0)
            pmat = _mm_nt(dnum, vh) + dden
            damat = pmat * w_intra
            gmat = pmat * smat
            wdn = w_inter * dnum
            wdd = w_inter * dden
            dqh = _mm(damat, kh) + _mm_nt(wdn, c_h) + wdd * n_h
            dkh = _mm_tn(damat, qh)
            dvh = _mm_tn(smat, dnum)
            dw_inter = _rowsum(dnum * t["qc"]) + dden * t["qn"]
            dcn = dc_sc[h]
            dnn = dn_sc[h, 0:1, :]
            kw = kh * w_state
            dkw = _mm_nt(vh, dcn) + dnn
            dvh = dvh + _mm(kw, dcn)
            dkh = dkh + dkw * w_state
            dgst = _rowsum(dkw * kh) * w_state
            ddecay = _colsum(_rowsum(dcn * c_h)) + _rowsum(dnn * n_h)
            db_last = _colsum(dgst) + ddecay * decay
            rs_g = _rowsum(gmat)
            cs_g = _rowsum(gmat.T)
            db = rs_g - cs_g + dw_inter * w_inter - dgst + jnp.where(rowv == lc - 1, db_last, 0.0)
            dli = cs_g + dgst
            dc_sc[h] = decay * dcn + _mm_tn(qh, wdn)
            dn_sc[h] = _bcast8(decay * dnn + _colsum(qh * wdd))
            dq_ref[:, sl] = dqh
            dk_ref[:, sl] = dkh * (dh ** -0.5)
            dv_ref[:, sl] = dvh
            dg_acc = jnp.where(lane == h, dli, jnp.where(lane == 4 + h, db, dg_acc))
        dg_ref[...] = dg_acc

    rev = lambda c: nc - 1 - c
    tok = pl.BlockSpec((lc, d), lambda c: (rev(c), 0))
    g128 = pl.BlockSpec((lc, LANES), lambda c: (rev(c), 0))
    return _call(
        body, comms, name="mlstm_cell_bwd", grid=(nc,),
        in_specs=[tok, tok, tok, tok, tok, g128, pl.BlockSpec((16, lc), lambda c: (0, rev(c))), g128,
                  pl.BlockSpec((1, nh, dh, dh), lambda c: (rev(c), 0, 0, 0)),
                  pl.BlockSpec((1, nh, SUBLANES, dh), lambda c: (rev(c), 0, 0, 0)),
                  pl.BlockSpec((1, nh, SUBLANES, LANES), lambda c: (rev(c), 0, 0, 0))],
        out_specs=[tok, tok, tok, g128],
        out_shape=[jax.ShapeDtypeStruct((s, d), F32)] * 3 + [jax.ShapeDtypeStruct((s, LANES), F32)],
        scratch_shapes=[pltpu.VMEM((nh, dh, dh), F32), pltpu.VMEM((nh, SUBLANES, dh), F32)],
        args=(dcell, cell, q, k, v, gcol, grow, mt, cs, ns, ms))


def _out_bwd(dxo, gate, y, y_rg, y_ml, w_out_g, layer, tile, comms=None):
    s, d = dxo.shape
    nd, _, r, _ = w_out_g.shape

    def body(dx_ref, g_ref, y_ref, yr_ref, ym_ref, w_ref, dyr_ref, dym_ref, gw_ref, dg_ref):
        @pl.when(pl.program_id(0) == 0)
        def _():
            gw_ref[...] = jnp.zeros_like(gw_ref)
            dg_ref[...] = jnp.zeros_like(dg_ref)

        dxv = dx_ref[...]
        dg_ref[...] += _bcast8(_colsum(dxv * y_ref[...]))
        dyb = (dxv * g_ref[...]).astype(BF16)
        dycat = lax.dot_general(dyb, w_ref[...].reshape(nd * r, d), (((1,), (1,)), ((), ())), preferred_element_type=F32)
        dyr_ref[...] = dycat[:, 0:d]
        dym_ref[...] = dycat[:, d:2 * d]
        ycat = jnp.concatenate([yr_ref[...].astype(BF16), ym_ref[...].astype(BF16)], axis=1)
        gw_ref[...] += lax.dot_general(ycat, dyb, (((0,), (0,)), ((), ())), preferred_element_type=F32).reshape(nd, r, d)

    tok = pl.BlockSpec((tile, d), lambda i: (i, 0))
    return _call(
        body, comms, name="out_proj_bwd", grid=(s // tile,),
        in_specs=[tok, _full((1, d)), tok, tok, tok, pl.BlockSpec((nd, 1, r, d), lambda i: (0, layer, 0, 0))],
        out_specs=[tok, tok, _full((nd, r, d)), _full((SUBLANES, d))],
        out_shape=[jax.ShapeDtypeStruct((s, d), F32)] * 2 + [jax.ShapeDtypeStruct((nd, r, d), F32),
                                                             jax.ShapeDtypeStruct((SUBLANES, d), F32)],
        args=(dxo, gate, y, y_rg, y_ml, w_out_g))


def _ml_cell_bwd(dy_ml, u, cell, q, k, v, gcol, grow, mt, cs, ns, ms, ng, nh, comms=None):
    s, d = q.shape
    lc = ML_CHUNK
    nc = s // lc
    dh = d // nh

    def body(dy_ref, o_ref, z_ref, cell_ref, q_ref, k_ref, v_ref, gc_ref, gr_ref, mt_ref, cs_ref, ns_ref, ms_ref,
             ng_ref, dq_ref, dk_ref, dv_ref, dg_ref, do_ref, dz_ref, gng_ref, dc_sc, dn_sc):
        @pl.when(pl.program_id(0) == 0)
        def _():
            dc_sc[...] = jnp.zeros_like(dc_sc)
            dn_sc[...] = jnp.zeros_like(dn_sc)
            gng_ref[...] = jnp.zeros_like(gng_ref)

        gc = gc_ref[...]
        gr = gr_ref[...]
        mtv = mt_ref[...]
        lane = _iota((lc, LANES), 1)
        rowv = _iota((lc, 1), 0)
        dg_acc = jnp.zeros((lc, LANES), F32)
        for h in range(nh):
            c_h = cs_ref[0, h]
            n_h = ns_ref[0, h, 0:1, :]
            m_prev = jnp.max(ms_ref[0, h, 0:1, :], axis=1, keepdims=True)
            t = _cell_chunk(h, nh, q_ref, k_ref, v_ref, gc, gr, m_prev, c_h, n_h, m_t=_col(mtv, h))
            sl, qh, kh, vh = t["sl"], t["qh"], t["kh"], t["vh"]
            w_intra, w_inter, smat, w_state, decay = t["w_intra"], t["w_inter"], t["smat"], t["w_state"], t["decay"]
            cell_h = cell_ref[:, sl]
            dz, do, dcell, gng = _ml_out_stage_bwd(dy_ref[:, sl], cell_h, o_ref[:, sl], z_ref[:, sl], ng_ref[:, sl])
            dz_ref[:, sl] = dz.astype(BF16)
            do_ref[:, sl] = do.astype(BF16)
            gng_ref[:, sl] += _bcast8(gng)
            eneg = jnp.exp(-t["m_t"])
            aden = jnp.abs(t["den"])
            nst = jnp.maximum(aden, eneg)
            dnum = dcell / nst
            dden = jnp.where(aden > eneg, -_rowsum(cell_h * dcell) / nst * jnp.sign(t["den"]), 0.0)
            pmat = _mm_nt(dnum, vh) + dden
            damat = pmat * w_intra
            gmat = pmat * smat
            wdn = w_inter * dnum
            wdd = w_inter * dden
            dqh = _mm(damat, kh) + _mm_nt(wdn, c_h) + wdd * n_h
            dkh = _mm_tn(damat, qh)
            dvh = _mm_tn(smat, dnum)
            dw_inter = _rowsum(dnum * t["qc"]) + dden * t["qn"]
            dcn = dc_sc[h]
            dnn = dn_sc[h, 0:1, :]
            kw = kh * w_state
            dkw = _mm_nt(vh, dcn) + dnn
            dvh = dvh + _mm(kw, dcn)
            dkh = dkh + dkw * w_state
            dgst = _rowsum(dkw * kh) * w_state
            ddecay = _colsum(_rowsum(dcn * c_h)) + _rowsum(dnn * n_h)
            db_last = _colsum(dgst) + ddecay * decay
            rs_g = _rowsum(gmat)
            cs_g = _rowsum(gmat.T)
            db = rs_g - cs_g + dw_inter * w_inter - dgst + jnp.where(rowv == lc - 1, db_last, 0.0)
            dli = cs_g + dgst
            dc_sc[h] = decay * dcn + _mm_tn(qh, wdn)
            dn_sc[h] = _bcast8(decay * dnn + _colsum(qh * wdd))
            dq_ref[:, sl] = dqh
            dk_ref[:, sl] = dkh * (dh ** -0.5)
            dv_ref[:, sl] = dvh
            dg_acc = jnp.where(lane == h, dli, jnp.where(lane == 4 + h, db, dg_acc))
        dg_ref[...] = dg_acc

    rev = lambda c: nc - 1 - c
    tok = pl.BlockSpec((lc, d), lambda c: (rev(c), 0))
    g128 = pl.BlockSpec((lc, LANES), lambda c: (rev(c), 0))
    return _call(
        body, comms, name="mlstm_cell_bwd", grid=(nc,),
        in_specs=[tok, pl.BlockSpec((lc, d), lambda c: (rev(c), 3)), pl.BlockSpec((lc, d), lambda c: (rev(c), 4)),
                  tok, tok, tok, tok, g128, pl.BlockSpec((16, lc), lambda c: (0, rev(c))), g128,
                  pl.BlockSpec((1, nh, dh, dh), lambda c: (rev(c), 0, 0, 0)),
                  pl.BlockSpec((1, nh, SUBLANES, dh), lambda c: (rev(c), 0, 0, 0)),
                  pl.BlockSpec((1, nh, SUBLANES, LANES), lambda c: (rev(c), 0, 0, 0)), _full((1, d))],
        out_specs=[tok, tok, tok, g128, tok, tok, _full((SUBLANES, d))],
        out_shape=[jax.ShapeDtypeStruct((s, d), F32)] * 3 + [jax.ShapeDtypeStruct((s, LANES), F32)]
        + [jax.ShapeDtypeStruct((s, d), BF16)] * 2 + [jax.ShapeDtypeStruct((SUBLANES, d), F32)],
        scratch_shapes=[pltpu.VMEM((nh, dh, dh), F32), pltpu.VMEM((nh, SUBLANES, dh), F32)],
        args=(dy_ml, u, u, cell, q, k, v, gcol, grow, mt, cs, ns, ms, ng))


def _halo_spec(d, tile, nt, col):
    per = tile // SUBLANES
    return pl.BlockSpec((SUBLANES, d), lambda i: (jnp.maximum((nt - 1 - i) * per - 1, 0), col))


def _ml_pre_bwd(dq, dk, dv, dgates, gcol, u, pre, q, k, v, conv_w, w_q, w_k, w_v, wif_t, tile, comms=None):
    s, d = dq.shape
    nt = s // tile
    nh, dh, _ = w_q.shape

    def body(dq_ref, dk_ref, dv_ref, dg_ref, gc_ref, x_ref, pre_ref, q_ref, k_ref, v_ref, cw_ref,
             wq_ref, wk_ref, wv_ref, wift_ref,
             dx_ref, gwq_ref, gwk_ref, gwv_ref, gwif_ref, gbif_ref, gcw_ref, gcb_ref, next8):
        @pl.when(pl.program_id(0) == 0)
        def _():
            next8[...] = jnp.zeros_like(next8)
            for ref in (gwq_ref, gwk_ref, gwv_ref, gwif_ref, gbif_ref, gcw_ref, gcb_ref):
                ref[...] = jnp.zeros_like(ref)

        x = x_ref[...]
        pre = pre_ref[...]
        sg = _sigmoid(pre)
        xc = pre * sg
        dgc = dg_ref[...]
        lane = _iota(dgc.shape, 1)
        utri = jnp.where(_iota((ML_CHUNK, ML_CHUNK), 0) <= _iota((ML_CHUNK, ML_CHUNK), 1), 1.0, 0.0)
        rcs = [_mm_hi(utri, dgc[c * ML_CHUNK:(c + 1) * ML_CHUNK, :]) for c in range(tile // ML_CHUNK)]
        rc = rcs[0] if len(rcs) == 1 else jnp.concatenate(rcs, axis=0)
        dgates_v = jnp.where(lane < 4, dgc, jnp.where(lane < 8, rc * (1.0 - jnp.exp(gc_ref[...])), 0.0))
        dgb = dgates_v.astype(BF16)
        gbif_ref[...] += jnp.broadcast_to(_colsum(dgates_v), gbif_ref.shape)
        ext = jnp.dot(dgb, wift_ref[...], preferred_element_type=F32)
        dqt = dq_ref[...] + ext[:, 0:d]
        dkt = dk_ref[...] + ext[:, d:2 * d]
        dvt = dv_ref[...] + ext[:, 2 * d:3 * d]
        gwif_ref[:, 0:d] += _mm_tn(dgb, q_ref[...])
        gwif_ref[:, d:2 * d] += _mm_tn(dgb, k_ref[...])
        gwif_ref[:, 2 * d:3 * d] += _mm_tn(dgb, v_ref[...])
        dxc_parts, dxv_parts = [], []
        for h in range(nh):
            sl = slice(h * dh, (h + 1) * dh)
            gwq_ref[h] += _mm_tn(xc[:, sl], dqt[:, sl])
            gwk_ref[h] += _mm_tn(xc[:, sl], dkt[:, sl])
            gwv_ref[h] += _mm_tn(x[:, sl], dvt[:, sl])
            dxc_parts.append(_mm_nt(dqt[:, sl], wq_ref[h]) + _mm_nt(dkt[:, sl], wk_ref[h]))
            dxv_parts.append(_mm_nt(dvt[:, sl], wv_ref[h]))
        dxc = jnp.concatenate(dxc_parts, axis=1)
        dxv = jnp.concatenate(dxv_parts, axis=1)
        dpre = dxc * (sg + pre * sg * (1.0 - sg))
        gcb_ref[...] += _bcast8(_colsum(dpre))
        dx_ref[...] = (dxv + _conv_bwd(dpre, x, next8[...], cw_ref, gcw_ref)).astype(BF16)
        next8[...] = dpre[0:SUBLANES, :]

    rev = lambda i: nt - 1 - i
    tok = pl.BlockSpec((tile, d), lambda i: (rev(i), 0))
    g128 = pl.BlockSpec((tile, LANES), lambda i: (rev(i), 0))
    wsh = (nh, dh, dh)
    return _call(
        body, comms, name="mlstm_proj_bwd", grid=(nt,),
        in_specs=[tok, tok, tok, g128, g128, pl.BlockSpec((tile, d), lambda i: (rev(i), 2)), tok,
                  tok, tok, tok, _full(conv_w.shape), _full(wsh), _full(wsh), _full(wsh), _full(wif_t.shape)],
        out_specs=[tok, _full(wsh), _full(wsh), _full(wsh), _full((LANES, 3 * d)), _full((SUBLANES, LANES)),
                   _full((SUBLANES, d)), _full((SUBLANES, d))],
        out_shape=[jax.ShapeDtypeStruct((s, d), BF16)] + [jax.ShapeDtypeStruct(wsh, F32)] * 3
        + [jax.ShapeDtypeStruct((LANES, 3 * d), F32), jax.ShapeDtypeStruct((SUBLANES, LANES), F32),
           jax.ShapeDtypeStruct((SUBLANES, d), F32), jax.ShapeDtypeStruct((SUBLANES, d), F32)],
        scratch_shapes=[pltpu.VMEM((SUBLANES, d), F32)],
        args=(dq, dk, dv, dgates, gcol, u, pre, q, k, v, conv_w, w_q, w_k, w_v, wif_t))


def _rg_bwd(dy_rg, u, h_rg, gates, conv_w, w_a, w_x, lam, tile, comms=None):
    s, d = dy_rg.shape
    nt = s // tile
    nh, dh, _ = w_a.shape

    def body(dy_ref, x_ref, z_ref, h_ref, hhalo_ref, xc_ref, r_ref, i_ref, a_ref, beta_ref, cw_ref, wa_ref,
             wx_ref, lam_ref,
             dx_ref, dz_ref, gwa_ref, gwx_ref, gba_ref, gbx_ref, glam_ref, gcw_ref, gcb_ref, next8, anext, dnext, dbuf):
        i = pl.program_id(0)

        @pl.when(i == 0)
        def _():
            for ref in (next8, anext, dnext, gwa_ref, gwx_ref, gba_ref, gbx_ref, glam_ref, gcw_ref, gcb_ref):
                ref[...] = jnp.zeros_like(ref)

        inner = jnp.where(i < nt - 1, 1.0, 0.0)
        xc, r, ig, a, beta = xc_ref[...], r_ref[...], i_ref[...], a_ref[...], beta_ref[...]
        sp = _softplus(-lam_ref[...])
        h = h_ref[...]
        row = _iota(h.shape, 0)
        hprev = jnp.where(row >= 1, pltpu.roll(h, 1, 0), hhalo_ref[SUBLANES - 1:SUBLANES, :] * inner)
        z = z_ref[...]
        sz = _sigmoid(z)
        dyv = dy_ref[...]
        dz_ref[...] = (dyv * h * (sz + z * sz * (1.0 - sz))).astype(BF16)
        a_up = jnp.where(row < tile - 1, pltpu.roll(a, tile - 1, 0), anext[0:1, :])
        _scan_into(a_up, dyv * z * sz, dnext[0:1, :], dbuf, True)
        delta = dbuf[...]
        anext[...] = a[0:SUBLANES, :]
        dnext[...] = delta[0:SUBLANES, :]
        dla = delta * hprev * a - delta * ig * xc * (a * a / beta)
        glam_ref[...] += _bcast8(_colsum(dla * r) * (RG_C * _sigmoid(-lam_ref[...])))
        dpa = dla * (-RG_C * sp) * r * (1.0 - r)
        dpx = delta * beta * xc * ig * (1.0 - ig)
        gba_ref[...] += _bcast8(_colsum(dpa))
        gbx_ref[...] += _bcast8(_colsum(dpx))
        parts = []
        for hh in range(nh):
            sl = slice(hh * dh, (hh + 1) * dh)
            gwa_ref[hh] += _mm_tn(xc[:, sl], dpa[:, sl])
            gwx_ref[hh] += _mm_tn(xc[:, sl], dpx[:, sl])
            parts.append(_mm_nt(dpa[:, sl], wa_ref[hh]) + _mm_nt(dpx[:, sl], wx_ref[hh]))
        dxc = delta * beta * ig + jnp.concatenate(parts, axis=1)
        gcb_ref[...] += _bcast8(_colsum(dxc))
        dx_ref[...] = _conv_bwd(dxc, x_ref[...], next8[...], cw_ref, gcw_ref).astype(BF16)
        next8[...] = dxc[0:SUBLANES, :]

    rev = lambda i: nt - 1 - i
    tok = pl.BlockSpec((tile, d), lambda i: (rev(i), 0))
    vec = _full((1, d))
    acc = _full((SUBLANES, d))
    wsh = (nh, dh, dh)
    return _call(
        body, comms, name="rglru_bwd", grid=(nt,),
        in_specs=[tok, tok, pl.BlockSpec((tile, d), lambda i: (rev(i), 1)), tok,
                  _halo_spec(d, tile, nt, 0)] + [tok] * 5 + [_full(conv_w.shape), _full(wsh), _full(wsh), vec],
        out_specs=[tok, tok, _full(wsh), _full(wsh), acc, acc, acc, acc, acc],
        out_shape=[jax.ShapeDtypeStruct((s, d), BF16)] * 2 + [jax.ShapeDtypeStruct(wsh, F32)] * 2
        + [jax.ShapeDtypeStruct((SUBLANES, d), F32)] * 5,
        scratch_shapes=[pltpu.VMEM((SUBLANES, d), F32)] * 3 + [pltpu.VMEM((tile, d), F32)],
        args=(dy_rg, u, u, h_rg, h_rg, *gates, conv_w, w_a, w_x, lam))


def _segments(d, w, n_pieces, n_slots):
    bounds = sorted({k * d for k in range(n_pieces + 1)} | {j * w for j in range(n_slots + 1)})
    return [(lo // d, lo % d, lo // w, lo % w, hi - lo) for lo, hi in zip(bounds[:-1], bounds[1:])]


def _in_bwd(pieces, x, dxo, ng, scale, w_in_g, layer, tile, comms=None, tiles=None, prev=None):
    s, d = x.shape
    nd, _, _, w = w_in_g.shape
    segs = _segments(d, w, len(pieces), nd)
    first, count = tiles or (0, s // tile)
    n_p = len(pieces)

    def body(*refs):
        p_refs = refs[:n_p]
        x_ref, dxo_ref, ng_ref, sc_ref, w_ref = refs[n_p:n_p + 5]
        dx_ref, dsc_ref, dsh_ref, gng_ref, wcat = refs[-5:]
        _join_columns(w_ref, wcat)

        @pl.when(pl.program_id(0) == 0)
        def _():
            for k, ref in enumerate((dsc_ref, dsh_ref, gng_ref)):
                ref[...] = jnp.zeros_like(ref) if prev is None else refs[n_p + 6 + k][...]

        du = jnp.concatenate([p[...] for p in p_refs], axis=1)
        dh = lax.dot_general(du, wcat[...], (((1,), (1,)), ((), ())), preferred_element_type=F32)
        xv = x_ref[...]
        g = ng_ref[...]
        rs = lax.rsqrt(jnp.mean(xv * xv, axis=1, keepdims=True) + EPS)
        xh = xv * rs
        dsh_ref[...] += _bcast8(_colsum(dh))
        dsc_ref[...] += _bcast8(_colsum(dh * xh * g))
        dhn = dh * (1.0 + sc_ref[...])
        gng_ref[...] += _bcast8(_colsum(dhn * xh))
        dxh = dhn * g
        dx_ref[...] = dxo_ref[...] + rs * (dxh - xh * jnp.mean(dxh * xh, axis=1, keepdims=True))

    tok = pl.BlockSpec((tile, d), lambda i: (i + first, 0))
    vec = _full((1, d))
    acc = _full((SUBLANES, d))
    more_specs = [] if prev is None else [pl.BlockSpec(memory_space=pl.ANY), acc, acc, acc]
    return _call(
        body, comms, name="in_proj_bwd_x", grid=(count,),
        in_specs=[tok] * n_p + [tok, tok, vec, vec, pl.BlockSpec((nd, 1, d, w), lambda i: (0, layer, 0, 0),
                                                               pipeline_mode=pl.Buffered(1))] + more_specs,
        out_specs=[tok, acc, acc, acc],
        out_shape=[jax.ShapeDtypeStruct((s, d), F32)] + [jax.ShapeDtypeStruct((SUBLANES, d), F32)] * 3,
        scratch_shapes=[pltpu.VMEM((d, nd * w), BF16)],
        args=(*pieces, x, dxo, ng, scale, w_in_g) + (() if prev is None else tuple(prev)),
        aliases={} if prev is None else {n_p + 5: 0})


def _in_bwd_w(pieces, hbf, w, slots, tile, comms=None):
    s, d = hbf.shape
    nd_all = len(pieces) * d // w
    segs = [sg for sg in _segments(d, w, len(pieces), nd_all) if sg[2] in slots]

    def body(*refs):
        p_refs = refs[:len(pieces)]
        h_ref, gw_ref = refs[len(pieces):]

        @pl.when(pl.program_id(0) == 0)
        def _():
            gw_ref[...] = jnp.zeros_like(gw_ref)

        hv = h_ref[...]
        for (kk, a, j, b, width) in segs:
            gw_ref[j - slots[0], :, b:b + width] += _mm_tn(hv, p_refs[kk][:, a:a + width])

    tok = pl.BlockSpec((tile, d), lambda i: (i, 0))
    return _call(
        body, comms, name="in_proj_bwd_w", grid=(s // tile,),
        in_specs=[tok] * len(pieces) + [tok],
        out_specs=[pl.BlockSpec((len(slots), d, w), lambda i: (0, 0, 0), pipeline_mode=pl.Buffered(1))],
        out_shape=[jax.ShapeDtypeStruct((len(slots), d, w), F32)],
        args=(*pieces, hbf))[0]


def _ada_bwd_w(cact_col, dmod, nd):
    d = cact_col.shape[0]
    w = dmod.shape[1] // nd

    def body(c_ref, m_ref, o_ref):
        o_ref[0] = c_ref[...] * m_ref[...]

    return pl.pallas_call(
        body, name="adaln_bwd_w", grid=(nd,),
        in_specs=[_full((d, 1)), pl.BlockSpec((1, w), lambda j: (0, j))],
        out_specs=pl.BlockSpec((1, d, w), lambda j: (j, 0, 0)),
        out_shape=jax.ShapeDtypeStruct((nd, d, w), F32),
        compiler_params=_params(1),
    )(cact_col, dmod)


def _exchange(arrs, gather, name):
    return _run_comms([_exchange_comm(arrs, gather)], name)[0]


def _run_comms(comms, name):
    _call(lambda: None, comms, name=name, grid=(1,), in_specs=[], out_specs=[], out_shape=[], args=())
    return [cm.results for cm in comms]


def _exchange_comm(arrs, gather):
    n = len(arrs)
    per = N_DEV - 1

    def copies(ins, outs, sems):
        send_sems, recv_sems, local_sems = sems
        x, y, c = (lax.axis_index(ax) for ax in MESH_AXES)
        me = 4 * x + 2 * y + c
        sends, recvs = [], []
        for flip in range(1, N_DEV):
            px = x ^ ((flip >> 2) & 1)
            py = y ^ ((flip >> 1) & 1)
            pc = c ^ (flip & 1)
            peer = 4 * px + 2 * py + pc
            for kk in range(n):
                src = ins[kk] if gather else ins[kk].at[peer]
                sends.append(_remote(src, outs[kk].at[me], send_sems, recv_sems, kk * per + flip - 1, (px, py, pc)))
                recvs.append(_remote(src, outs[kk].at[peer], send_sems, recv_sems, kk * per + flip - 1, (px, py, pc)))
        local = [pltpu.make_async_copy(ins[kk] if gather else ins[kk].at[me], outs[kk].at[me], local_sems.at[kk])
                 for kk in range(n)]
        return local, sends, recvs

    def start(ins, outs, sems):
        local, sends, _ = copies(ins, outs, sems)
        for cp in sends + local:
            cp.start()

    def finish(ins, outs, sems):
        local, sends, recvs = copies(ins, outs, sems)
        for cp in recvs:
            cp.wait_recv()
        for cp in sends:
            cp.wait_send()
        for cp in local:
            cp.wait()

    return _Comm(arrs, [jax.ShapeDtypeStruct((N_DEV,) + a.shape if gather else a.shape, a.dtype) for a in arrs],
                 [pltpu.SemaphoreType.DMA((n * per,)), pltpu.SemaphoreType.DMA((n * per,)), pltpu.SemaphoreType.DMA((n,))],
                 start, finish)


def _mesh_place():
    x, y, c = (lax.axis_index(ax) for ax in MESH_AXES)
    return x, y, c, (x, y, 1 - c), [(1 - x, y), (x, 1 - y), (1 - x, 1 - y)]


def _remote(src, dst, send_sems, recv_sems, sem, to):
    return pltpu.make_async_remote_copy(src_ref=src, dst_ref=dst, send_sem=send_sems.at[sem], recv_sem=recv_sems.at[sem],
                                        device_id=to, device_id_type=pl.DeviceIdType.MESH)


def _gather_two_level(arrs, name):
    n = len(arrs)
    per = N_DEV - 1

    def body(*refs):
        ins, outs = refs[:n], refs[n:2 * n]
        send_sems, recv_sems, local_sems = refs[2 * n:]
        x, y, c, sibling, chips = _mesh_place()

        def copy(kk, j, block, to, src=None):
            dst = outs[kk].at[4 * block[0] + 2 * block[1] + block[2]]
            return _remote(dst if src is None else src, dst, send_sems, recv_sems, kk * per + j, to)

        me = (x, y, c)
        local = [pltpu.make_async_copy(ins[kk], outs[kk].at[4 * x + 2 * y + c], local_sems.at[kk]) for kk in range(n)]
        first = []
        for j, chip in enumerate(chips):
            first += [copy(kk, 1 + j, me, (*chip, c), src=ins[kk]) for kk in range(n)]
        first += [copy(kk, 0, me, sibling, src=ins[kk]) for kk in range(n)]
        for cp in first + local:
            cp.start()
        passed = []
        for j, chip in enumerate(chips):
            for kk in range(n):
                copy(kk, 1 + j, (*chip, c), me).wait_recv()
                passed.append(copy(kk, 4 + j, (*chip, c), sibling))
                passed[-1].start()
        for kk in range(n):
            copy(kk, 0, sibling, me).wait_recv()
        for j, chip in enumerate(chips):
            for kk in range(n):
                copy(kk, 4 + j, (*chip, 1 - c), me).wait_recv()
        for cp in first + passed:
            cp.wait_send()
        for cp in local:
            cp.wait()

    return pl.pallas_call(
        body, name=name,
        in_specs=[pl.BlockSpec(memory_space=pl.ANY)] * n, out_specs=[pl.BlockSpec(memory_space=pl.ANY)] * n,
        out_shape=[jax.ShapeDtypeStruct((N_DEV,) + a.shape, a.dtype) for a in arrs],
        scratch_shapes=[pltpu.SemaphoreType.DMA((n * per,)), pltpu.SemaphoreType.DMA((n * per,)),
                        pltpu.SemaphoreType.DMA((n,))],
    )(*arrs)


N_CHIPS = N_DEV // 2


def _core_swap(arrs, name):
    n = len(arrs)

    def body(*refs):
        ins, outs = refs[:n], refs[n:2 * n]
        send_sems, recv_sems = refs[2 * n:]
        _, _, c, sibling, _ = _mesh_place()
        copies = [_remote(ins[kk].at[2 * q + (1 - c)], outs[kk].at[q], send_sems, recv_sems, kk * N_CHIPS + q, sibling)
                  for q in range(N_CHIPS) for kk in range(n)]
        for cp in copies:
            cp.start()
        for cp in copies:
            cp.wait_recv()
        for cp in copies:
            cp.wait_send()

    return pl.pallas_call(
        body, name=name,
        in_specs=[pl.BlockSpec(memory_space=pl.ANY)] * n, out_specs=[pl.BlockSpec(memory_space=pl.ANY)] * n,
        out_shape=[jax.ShapeDtypeStruct((N_CHIPS,) + a.shape[1:], a.dtype) for a in arrs],
        scratch_shapes=[pltpu.SemaphoreType.DMA((n * N_CHIPS,)), pltpu.SemaphoreType.DMA((n * N_CHIPS,))],
    )(*arrs)


def _pair_sum(a, other, parity, name):
    _, r, c = a.shape
    tr = _row_tile(r, c, 3)

    def body(p_ref, a_ref, o_ref, s_ref):
        s_ref[...] = (a_ref[...] + o_ref[...]).astype(BF16)

    return pl.pallas_call(
        body, name=name,
        grid_spec=pltpu.PrefetchScalarGridSpec(
            num_scalar_prefetch=1, grid=(N_CHIPS, r // tr),
            in_specs=[pl.BlockSpec((1, tr, c), lambda q, i, p: (2 * q + p[0], i, 0)),
                      pl.BlockSpec((1, tr, c), lambda q, i, p: (q, i, 0))],
            out_specs=pl.BlockSpec((1, tr, c), lambda q, i, p: (q, i, 0))),
        out_shape=jax.ShapeDtypeStruct((N_CHIPS, r, c), BF16),
        compiler_params=_params(2),
    )(parity, a, other)


def _chip_swap(arrs, name):
    n = len(arrs)
    per = N_CHIPS - 1

    def body(*refs):
        ins, outs = refs[:n], refs[n:2 * n]
        send_sems, recv_sems, local_sems = refs[2 * n:]
        x, y, c, _, chips = _mesh_place()
        mine = 2 * x + y
        sends = [_remote(ins[kk].at[2 * chip[0] + chip[1]], outs[kk].at[mine], send_sems, recv_sems, kk * per + j, (*chip, c))
                 for j, chip in enumerate(chips) for kk in range(n)]
        recvs = [_remote(ins[kk].at[mine], outs[kk].at[2 * chip[0] + chip[1]], send_sems, recv_sems, kk * per + j, (*chip, c))
                 for j, chip in enumerate(chips) for kk in range(n)]
        local = [pltpu.make_async_copy(ins[kk].at[mine], outs[kk].at[mine], local_sems.at[kk]) for kk in range(n)]
        for cp in sends + local:
            cp.start()
        for cp in recvs:
            cp.wait_recv()
        for cp in sends:
            cp.wait_send()
        for cp in local:
            cp.wait()

    return pl.pallas_call(
        body, name=name,
        in_specs=[pl.BlockSpec(memory_space=pl.ANY)] * n, out_specs=[pl.BlockSpec(memory_space=pl.ANY)] * n,
        out_shape=[jax.ShapeDtypeStruct(a.shape, a.dtype) for a in arrs],
        scratch_shapes=[pltpu.SemaphoreType.DMA((n * per,)), pltpu.SemaphoreType.DMA((n * per,)),
                        pltpu.SemaphoreType.DMA((n,))],
    )(*arrs)


def _adam_math(w, g, m, v):
    m = ADAM_B1 * m + (1.0 - ADAM_B1) * g
    v = ADAM_B2 * v + (1.0 - ADAM_B2) * (g * g)
    m_hat = m / (1.0 - ADAM_B1 ** ADAM_STEP)
    v_hat = v / (1.0 - ADAM_B2 ** ADAM_STEP)
    delta = -ADAM_LR * (m_hat / (jnp.sqrt(v_hat) + ADAM_EPS) + ADAM_WD * w)
    return delta, m, v


def _sum_devices(r_ref):
    acc = r_ref[0].astype(F32)
    for p in range(1, r_ref.shape[0]):
        acc = acc + r_ref[p].astype(F32)
    return acc


def _row_tile(rows, cols, n_bufs):
    budget = 24 * 1024 * 1024 // (n_bufs * 2 * cols * 4)
    t = rows
    while t > budget and t % 2 == 0 and (t // 2) % SUBLANES == 0:
        t //= 2
    return t


def _reduce_adam(recvs, w, m, v, name, comms=None):
    nl, r, c = w.shape
    n_part = recvs[0].shape[0]
    tr = _row_tile(r, c, n_part * nl + 7)
    nt = r // tr

    def body(*refs):
        r_refs = refs[:nl]
        w_ref, m_ref, v_ref, g_ref, d_ref, mo_ref, vo_ref = refs[nl:]
        layer = pl.program_id(0) // nt
        g = _sum_devices(r_refs[0])
        for ll in range(1, nl):
            g = jnp.where(layer == ll, _sum_devices(r_refs[ll]), g)
        delta, m2, v2 = _adam_math(w_ref[0], g, m_ref[0], v_ref[0])
        g_ref[0] = g
        d_ref[0] = delta
        mo_ref[0] = m2
        vo_ref[0] = v2

    def rspec(ll):
        return pl.BlockSpec((n_part, tr, c),
                            lambda i: (0, jnp.where(i // nt == ll, i % nt, jnp.where(i // nt < ll, 0, nt - 1)), 0))

    blk = pl.BlockSpec((1, tr, c), lambda i: (i // nt, i % nt, 0))
    return _call(
        body, comms, name=name, grid=(nl * nt,),
        in_specs=[rspec(ll) for ll in range(nl)] + [blk, blk, blk],
        out_specs=[blk] * 4,
        out_shape=[jax.ShapeDtypeStruct((nl, r, c), F32)] * 4,
        args=(*recvs, w, m, v))


def _sum8(recv, name):
    _, r, c = recv.shape

    def body(r_ref, o_ref):
        o_ref[...] = _sum_devices(r_ref)

    return pl.pallas_call(
        body, name=name, grid=(1,),
        in_specs=[_full(recv.shape)], out_specs=_full((r, c)),
        out_shape=jax.ShapeDtypeStruct((r, c), F32), compiler_params=_params(1),
    )(recv)


def _adam_call(w, g, m, v, name):
    r, c = w.shape

    def body(w_ref, g_ref, m_ref, v_ref, d_ref, mo_ref, vo_ref):
        delta, m2, v2 = _adam_math(w_ref[...], g_ref[...], m_ref[...], v_ref[...])
        d_ref[...] = delta
        mo_ref[...] = m2
        vo_ref[...] = v2

    return pl.pallas_call(
        body, name=name, grid=(1,),
        in_specs=[_full((r, c))] * 4, out_specs=[_full((r, c))] * 3,
        out_shape=[jax.ShapeDtypeStruct((r, c), F32)] * 3, compiler_params=_params(1),
    )(w, g, m, v)


def _tile_for(s, want):
    return min(want, s)


def _local_step_whole(x, c, target, wts):
    s, d = x.shape
    nl = wts["w_in_g"].shape[1]
    nd = wts["w_in_g"].shape[0]
    nh_ml = wts["w_qkv"].shape[2]
    t_big = _tile_for(s, 512)
    t_mid = _tile_for(s, 256)

    mod, cact = _mod_call(c, wts["w_ada_g"], wts["b_ada"])
    row = lambda a: a.reshape(1, -1)
    saved = []
    xl = x
    for l in range(nl):
        shift, scale, gate = (row(mod[l, kk * d:(kk + 1) * d]) for kk in range(3))
        u, hbf = _in_fwd(xl, row(wts["norm_g"][l]), scale, shift, wts["w_in_g"], l, t_mid)
        h_rg, y_rg = _rg_fwd(u, d, wts["rg_conv_w"][l], row(wts["rg_conv_b"][l]), wts["rg_w_a_bf"][l],
                             row(wts["rg_b_a"][l]), wts["rg_w_x_bf"][l], row(wts["rg_b_x"][l]),
                             row(wts["rg_lambda"][l]), t_mid)
        q, k, v, gcol = _ml_pre(u, d, wts["ml_conv_w"][l], row(wts["ml_conv_b"][l]), wts["w_qkv"][l, 0],
                                wts["w_qkv"][l, 1], wts["w_qkv"][l, 2], wts["wif_pad"][l], wts["bif_pad"][l], t_mid)
        grow = gcol[:, 0:16].T
        cell, y_ml, cs, ns, ms, mt = _ml_cell_fwd(q, k, v, gcol, grow, u, row(wts["ml_norm_g"][l]), nh_ml)
        x_new, y = _out_fwd(xl, y_rg, y_ml, gate, wts["w_out_g"], l, t_big)
        saved.append(dict(x=xl, u=u, hbf=hbf, h_rg=h_rg, y_rg=y_rg, q=q, k=k, v=v, gcol=gcol, grow=grow, cell=cell,
                          y_ml=y_ml, cs=cs, ns=ns, ms=ms, mt=mt, y=y, scale=scale, gate=gate))
        xl = x_new

    dx, loss_p, g_final = _loss_call(xl, row(wts["final_g"]), target, t_big)
    grads = [None] * nl
    cact_col = cact[0].reshape(d, 1)
    for l in reversed(range(nl)):
        sv = saved[l]
        dy_rg, dy_ml, gw_out, dgate = _out_bwd(dx, sv["gate"], sv["y"], sv["y_rg"], sv["y_ml"], wts["w_out_g"], l, t_big)
        dq, dk, dv, dgates, d_mlo, d_mlz, g_mlng = _ml_cell_bwd(
            dy_ml, sv["u"], sv["cell"], sv["q"], sv["k"], sv["v"], sv["gcol"], sv["grow"], sv["mt"], sv["cs"],
            sv["ns"], sv["ms"], row(wts["ml_norm_g"][l]), nh_ml)
        d_mlx, g_wq, g_wk, g_wv, g_wift, g_bif, g_mlcw, g_mlcb = _ml_pre_bwd(
            dq, dk, dv, dgates, sv["gcol"], sv["u"], sv["q"], sv["k"], sv["v"], wts["ml_conv_w"][l],
            row(wts["ml_conv_b"][l]), wts["w_qkv"][l, 0], wts["w_qkv"][l, 1], wts["w_qkv"][l, 2], wts["wift_pad"][l], t_mid)
        d_rgx, d_rgz, g_wa, g_wx, g_ba, g_bx, g_lam, g_rgcw, g_rgcb = _rg_bwd(
            dy_rg, sv["u"], sv["h_rg"], wts["rg_conv_w"][l], row(wts["rg_conv_b"][l]), wts["rg_w_a_bf"][l],
            row(wts["rg_b_a"][l]), wts["rg_w_x_bf"][l], row(wts["rg_b_x"][l]), row(wts["rg_lambda"][l]), t_mid)
        pieces = [d_rgx, d_rgz, d_mlx, d_mlo, d_mlz]
        dx, dscale, dshift, g_ng = _in_bwd(pieces, sv["x"], dx, row(wts["norm_g"][l]), sv["scale"], wts["w_in_g"], l, t_mid)
        half = nd // 2
        w_cols = wts["w_in_g"].shape[3]
        gw_in = jnp.concatenate([_in_bwd_w(pieces, sv["hbf"], w_cols, tuple(range(0, half)), t_big),
                                 _in_bwd_w(pieces, sv["hbf"], w_cols, tuple(range(half, nd)), t_big)], axis=0)
        dmod = jnp.concatenate([dshift[0:1], dscale[0:1], dgate[0:1]], axis=1)
        gw_ada = _ada_bwd_w(cact_col, dmod, nd)
        grads[l] = dict(w_ada=gw_ada, w_in=gw_in, w_out=gw_out, w_qkv=jnp.stack([g_wq, g_wk, g_wv]),
                        rg_conv_w=g_rgcw[0:CONV_WIDTH], ml_conv_w=g_mlcw[0:CONV_WIDTH], wif_t=g_wift[0:8],
                        norm_g=g_ng[0], b_ada=dmod[0], rg_conv_b=g_rgcb[0], rg_w_a=g_wa, rg_b_a=g_ba[0], rg_w_x=g_wx,
                        rg_b_x=g_bx[0], rg_lambda=g_lam[0], ml_conv_b=g_mlcb[0], ml_b_if=g_bif[0, 0:8],
                        ml_norm_g=g_mlng[0])
    return loss_p[0, 0], dx, grads, g_final[0]


REPLICATED = ("norm_g", "b_ada", "rg_conv_b", "rg_w_a", "rg_b_a", "rg_w_x", "rg_b_x", "rg_lambda", "ml_conv_b",
              "ml_b_if", "ml_norm_g", "final_g")
ROW_ALIGN = N_DEV * SUBLANES


def _to_rows(a):
    flat = a.reshape(-1)
    pad = (-flat.shape[0]) % LANES
    return jnp.pad(flat, (0, pad)).reshape(-1, LANES)


def _pack(arrays):
    rows = jnp.concatenate([_to_rows(a) for a in arrays], axis=0)
    return jnp.pad(rows, ((0, (-rows.shape[0]) % ROW_ALIGN), (0, 0)))


def _unpack(rows, like):
    out, at = [], 0
    for a in like:
        n = -(-a.size // LANES)
        out.append(rows[at:at + n].reshape(-1)[:a.size].reshape(a.shape))
        at += n
    return out


def _small_pack(rg_conv_w, ml_conv_w, ml_w_if):
    nl = rg_conv_w.shape[0]
    wif_t = jnp.swapaxes(ml_w_if, 1, 2).reshape(nl, -1, LANES)
    return jnp.concatenate([rg_conv_w, ml_conv_w, wif_t], axis=1)


def _small_unpack(p, if_rows):
    nl = p.shape[0]
    rg_cw = p[:, 0:CONV_WIDTH]
    ml_cw = p[:, CONV_WIDTH:2 * CONV_WIDTH]
    wif = jnp.swapaxes(p[:, 2 * CONV_WIDTH:].reshape(nl, 8, if_rows), 1, 2)
    return rg_cw, ml_cw, wif


def _assemble_weights(big, small, rep):
    w_ada_g, w_in_g, w_out_g, qkv_g = big
    nd, nl = small.shape[0], small.shape[1]
    d = w_in_g.shape[2]
    dh = qkv_g.shape[3]
    nh = d // dh
    rsh = qkv_g.shape[2] // (3 * nh)
    w_qkv = qkv_g.reshape(nd, nl, 3, nh, rsh, dh).transpose(1, 2, 3, 0, 4, 5).reshape(nl, 3, nh, nd * rsh, dh)
    cw = small[:, :, 0:2 * CONV_WIDTH].reshape(nd, nl, 2, CONV_WIDTH, LANES).transpose(1, 2, 3, 0, 4)
    cw = cw.reshape(nl, 2, CONV_WIDTH, nd * LANES)
    if_rows = (small.shape[2] - 2 * CONV_WIDTH) * LANES // 8
    wif_t = small[:, :, 2 * CONV_WIDTH:].reshape(nd, nl, 8, if_rows).transpose(1, 2, 0, 3).reshape(nl, 8, nd * if_rows)
    wift_pad = jnp.pad(wif_t, ((0, 0), (0, LANES - 8), (0, 0))).astype(BF16)
    wif_pad = jnp.swapaxes(wift_pad, 1, 2)
    bif_pad = jnp.pad(rep["ml_b_if"], ((0, 0), (0, LANES - 8))).reshape(nl, 1, LANES)
    wts = dict(rep)
    wts.update(w_ada_g=w_ada_g, w_in_g=w_in_g, w_out_g=w_out_g, w_qkv=w_qkv, rg_conv_w=cw[:, 0], ml_conv_w=cw[:, 1],
               wif_pad=wif_pad, wift_pad=wift_pad, bif_pad=bif_pad, rg_w_a_bf=rep["rg_w_a"].astype(BF16),
               rg_w_x_bf=rep["rg_w_x"].astype(BF16))
    return wts


def _qkv_slots(g_qkv, nd):
    three, nh, dh, _ = g_qkv.shape
    return g_qkv.reshape(three, nh, nd, dh // nd, dh).transpose(2, 0, 1, 3, 4).reshape(nd, three * nh * (dh // nd), dh)


def _small_slots(g):
    nd = N_DEV
    cw = jnp.stack([g["rg_conv_w"], g["ml_conv_w"]]).reshape(2, CONV_WIDTH, nd, LANES).transpose(2, 0, 1, 3)
    cw = cw.reshape(nd, 2 * CONV_WIDTH, LANES)
    wif = g["wif_t"].reshape(8, nd, -1).transpose(1, 0, 2).reshape(nd, -1, LANES)
    return jnp.concatenate([cw, wif], axis=1)


def _kernel_unhosted(x, c, norm_g, w_ada, b_ada, w_in, rg_conv_w, rg_conv_b, rg_w_a, rg_b_a, rg_w_x, rg_b_x, rg_lambda, ml_conv_w, ml_conv_b, ml_w_q, ml_w_k, ml_w_v, ml_w_if, ml_b_if, ml_norm_g, w_out, final_g, loss_target, m_norm_g, m_w_ada, m_b_ada, m_w_in, m_rg_conv_w, m_rg_conv_b, m_rg_w_a, m_rg_b_a, m_rg_w_x, m_rg_b_x, m_rg_lambda, m_ml_conv_w, m_ml_conv_b, m_ml_w_q, m_ml_w_k, m_ml_w_v, m_ml_w_if, m_ml_b_if, m_ml_norm_g, m_w_out, m_final_g, v_norm_g, v_w_ada, v_b_ada, v_w_in, v_rg_conv_w, v_rg_conv_b, v_rg_w_a, v_rg_b_a, v_rg_w_x, v_rg_b_x, v_rg_lambda, v_ml_conv_w, v_ml_conv_b, v_ml_w_q, v_ml_w_k, v_ml_w_v, v_ml_w_if, v_ml_b_if, v_ml_norm_g, v_w_out, v_final_g):
    given = dict(locals())
    nl = w_in.shape[0]
    rep = {n: given[n] for n in REPLICATED}

    def qkv_shard(prefix):
        return jnp.stack([given[prefix + "ml_w_q"], given[prefix + "ml_w_k"], given[prefix + "ml_w_v"]], axis=1).reshape(
            nl, -1, ml_w_q.shape[-1])

    *big, small = _gather_two_level(
        [w_ada.astype(BF16), w_in.astype(BF16), w_out.astype(BF16), qkv_shard("").astype(BF16),
         _small_pack(rg_conv_w, ml_conv_w, ml_w_if)], "gather_weights")
    wts = _assemble_weights(big, small, rep)

    loss_p, grad_x, grads, g_final = _local_step(x[0], c, loss_target[0], wts)
    loss = lax.psum(loss_p, MESH_AXES)

    keys = ("w_ada", "w_in", "w_out", "w_qkv", "small")
    parity = lax.axis_index("c").astype(jnp.int32).reshape(1)
    recv = []
    for l in range(nl):
        g = grads[l]
        parts = [g["w_ada"], g["w_in"], g["w_out"], _qkv_slots(g["w_qkv"], N_DEV), _small_slots(g)]
        other = _core_swap(parts, "core_swap_layer%d" % l)
        sums = [_pair_sum(a, o, parity, "pair_sum_%s_layer%d" % (key, l)) for key, a, o in zip(keys, parts, other)]
        recv.append(_chip_swap(sums, "chip_swap_layer%d" % l))
    shard = {"": dict(w_ada=w_ada, w_in=w_in, w_out=w_out, w_qkv=qkv_shard(""),
                      small=_small_pack(rg_conv_w, ml_conv_w, ml_w_if))}
    for p in ("m_", "v_"):
        shard[p] = dict(w_ada=given[p + "w_ada"], w_in=given[p + "w_in"], w_out=given[p + "w_out"], w_qkv=qkv_shard(p),
                        small=_small_pack(given[p + "rg_conv_w"], given[p + "ml_conv_w"], given[p + "ml_w_if"]))
    res = {}
    for ki, key in enumerate(keys):
        res[key] = _reduce_adam([recv[l][ki] for l in range(nl)], shard[""][key], shard["m_"][key], shard["v_"][key],
                                "reduce_adam_" + key)

    rep_g = dict(final_g=g_final)
    for n in REPLICATED[:-1]:
        rep_g[n] = jnp.stack([grads[l][n] for l in range(nl)])
    pack_g = _pack([rep_g[n] for n in REPLICATED])
    rows = pack_g.shape[0] // N_DEV
    mine = _sum8(_exchange([pack_g.reshape(N_DEV, rows, LANES)], False, "reduce_scatter_replicated")[0], "sum_replicated")
    g_rep = _exchange([mine], True, "gather_replicated")[0].reshape(N_DEV * rows, LANES)
    rep_like = [rep[n] for n in REPLICATED]
    d_rep, m_rep, v_rep = _adam_call(_pack(rep_like), g_rep, _pack([given["m_" + n] for n in REPLICATED]),
                                     _pack([given["v_" + n] for n in REPLICATED]), "adam_replicated")
    rep_out = [dict(zip(REPLICATED, _unpack(a, rep_like))) for a in (g_rep, d_rep, m_rep, v_rep)]

    if_rows = ml_w_if.shape[1]
    order = ("norm_g", "w_ada", "b_ada", "w_in", "rg_conv_w", "rg_conv_b", "rg_w_a", "rg_b_a", "rg_w_x", "rg_b_x",
             "rg_lambda", "ml_conv_w", "ml_conv_b", "ml_w_q", "ml_w_k", "ml_w_v", "ml_w_if", "ml_b_if", "ml_norm_g",
             "w_out", "final_g")
    outs = [loss, grad_x[None]]
    for kind in range(4):
        qkv = res["w_qkv"][kind].reshape((nl, 3) + ml_w_q.shape[1:])
        rg_cw, ml_cw, wif = _small_unpack(res["small"][kind], if_rows)
        sharded = dict(w_ada=res["w_ada"][kind], w_in=res["w_in"][kind], w_out=res["w_out"][kind], ml_w_q=qkv[:, 0],
                       ml_w_k=qkv[:, 1], ml_w_v=qkv[:, 2], rg_conv_w=rg_cw, ml_conv_w=ml_cw, ml_w_if=wif)
        for n in order:
            outs.append(sharded[n] if n in sharded else rep_out[kind][n])
    return tuple(outs)


def _slot(block):
    return 4 * block[0] + 2 * block[1] + block[2]


def _dma_sems(*counts):
    return [pltpu.SemaphoreType.DMA((n,)) for n in counts]


def _start_all(copies):
    for cp in copies:
        cp.start()


def _gather_ici_comm(arrs):
    n = len(arrs)

    def copies(ins, outs, sems):
        send_sems, recv_sems, local_sems = sems
        x, y, c, sibling, chips = _mesh_place()
        me = (x, y, c)
        peers = [(*chip, c) for chip in chips] + [sibling]
        local = [pltpu.make_async_copy(ins[kk], outs[kk].at[_slot(me)], local_sems.at[kk]) for kk in range(n)]
        sends = [_remote(ins[kk], outs[kk].at[_slot(me)], send_sems, recv_sems, kk * 4 + j, peer)
                 for j, peer in enumerate(peers) for kk in range(n)]
        recvs = [_remote(ins[kk], outs[kk].at[_slot(peer)], send_sems, recv_sems, kk * 4 + j, peer)
                 for j, peer in enumerate(peers) for kk in range(n)]
        return local, sends, recvs

    def start(ins, outs, sems):
        local, sends, _ = copies(ins, outs, sems)
        _start_all(sends + local)

    def finish(ins, outs, sems):
        local, sends, recvs = copies(ins, outs, sems)
        for cp in recvs:
            cp.wait_recv()
        for cp in sends:
            cp.wait_send()
        for cp in local:
            cp.wait()

    return _Comm(arrs, [jax.ShapeDtypeStruct((N_DEV,) + a.shape, a.dtype) for a in arrs], _dma_sems(4 * n, 4 * n, n),
                 start, finish)


def _gather_fwd_comm(bufs):
    n = len(bufs)

    def copies(ins, outs, sems):
        send_sems, recv_sems = sems
        _, _, c, sibling, chips = _mesh_place()
        sends = [_remote(ins[kk].at[_slot((*chip, c))], outs[kk].at[_slot((*chip, c))], send_sems, recv_sems, kk * 3 + j, sibling)
                 for j, chip in enumerate(chips) for kk in range(n)]
        recvs = [_remote(ins[kk].at[_slot((*chip, c))], outs[kk].at[_slot((*chip, 1 - c))], send_sems, recv_sems, kk * 3 + j, sibling)
                 for j, chip in enumerate(chips) for kk in range(n)]
        return sends, recvs

    def start(ins, outs, sems):
        _start_all(copies(ins, outs, sems)[0])

    def finish(ins, outs, sems):
        sends, recvs = copies(ins, outs, sems)
        for cp in recvs:
            cp.wait_recv()
        for cp in sends:
            cp.wait_send()

    return _Comm(bufs, [jax.ShapeDtypeStruct(a.shape, a.dtype) for a in bufs], _dma_sems(3 * n, 3 * n), start, finish,
                 aliases=[(i, i) for i in range(n)])


def _core_swap_comm(arrs):
    n = len(arrs)

    def copies(ins, outs, sems):
        send_sems, recv_sems = sems
        _, _, c, sibling, _ = _mesh_place()
        return [_remote(ins[kk].at[2 * q + (1 - c)], outs[kk].at[q], send_sems, recv_sems, kk * N_CHIPS + q, sibling)
                for q in range(N_CHIPS) for kk in range(n)]

    def start(ins, outs, sems):
        _start_all(copies(ins, outs, sems))

    def finish(ins, outs, sems):
        cps = copies(ins, outs, sems)
        for cp in cps:
            cp.wait_recv()
        for cp in cps:
            cp.wait_send()

    return _Comm(arrs, [jax.ShapeDtypeStruct((N_CHIPS,) + a.shape[1:], a.dtype) for a in arrs],
                 _dma_sems(N_CHIPS * n, N_CHIPS * n), start, finish)


def _chip_swap_comm(arrs):
    n = len(arrs)
    per = N_CHIPS - 1

    def copies(ins, outs, sems):
        send_sems, recv_sems, local_sems = sems
        x, y, c, _, chips = _mesh_place()
        mine = 2 * x + y
        sends = [_remote(ins[kk].at[2 * chip[0] + chip[1]], outs[kk].at[mine], send_sems, recv_sems, kk * per + j, (*chip, c))
                 for j, chip in enumerate(chips) for kk in range(n)]
        recvs = [_remote(ins[kk].at[mine], outs[kk].at[2 * chip[0] + chip[1]], send_sems, recv_sems, kk * per + j, (*chip, c))
                 for j, chip in enumerate(chips) for kk in range(n)]
        local = [pltpu.make_async_copy(ins[kk].at[mine], outs[kk].at[mine], local_sems.at[kk]) for kk in range(n)]
        return local, sends, recvs

    def start(ins, outs, sems):
        local, sends, _ = copies(ins, outs, sems)
        _start_all(sends + local)

    def finish(ins, outs, sems):
        local, sends, recvs = copies(ins, outs, sems)
        for cp in recvs:
            cp.wait_recv()
        for cp in sends:
            cp.wait_send()
        for cp in local:
            cp.wait()

    return _Comm(arrs, [jax.ShapeDtypeStruct(a.shape, a.dtype) for a in arrs], _dma_sems(per * n, per * n, n), start, finish)


def _ada_mod(c_all, w_ada, b_cols, comms=None):
    nl, d, w = w_ada.shape

    def body(c_ref, w_ref, b_ref, m_ref, ca_ref):
        sub = _iota((SUBLANES, d), 0)
        cv = jnp.zeros((SUBLANES, d), F32)
        for b in range(N_DEV):
            cv = jnp.where(sub == b, c_ref[b], cv)
        ca = cv * _sigmoid(cv)
        ca_ref[...] = ca
        m_ref[...] = jnp.zeros_like(m_ref)
        for l in range(nl):
            ml = _mm_hi(ca, w_ref[l]) + b_ref[l:l + 1, :]
            for b in range(N_DEV):
                m_ref[b, l:l + 1, :] = _row(ml, b)

    return _call(
        body, comms, name="adaln_mod_columns", grid=(1,),
        in_specs=[_full(c_all.shape), _full(w_ada.shape), _full(b_cols.shape)],
        out_specs=[_full((N_DEV, SUBLANES, w)), _full((SUBLANES, d))],
        out_shape=[jax.ShapeDtypeStruct((N_DEV, SUBLANES, w), F32), jax.ShapeDtypeStruct((SUBLANES, d), F32)],
        args=(c_all, w_ada, b_cols))


def _ada_grad_adam(cact_t, dmods, w, m, v, comms=None):
    nl, d, wd = w.shape
    tr = _row_tile(d, wd, 8)
    nt = d // tr

    def body(c_ref, dm_ref, w_ref, m_ref, v_ref, g_ref, d_ref, mo_ref, vo_ref):
        cv = c_ref[...]
        dm = dm_ref[0]
        g = _col(cv, 0) * _row(dm, 0)
        for b in range(1, N_DEV):
            g = g + _col(cv, b) * _row(dm, b)
        delta, m2, v2 = _adam_math(w_ref[0], g, m_ref[0], v_ref[0])
        g_ref[0] = g
        d_ref[0] = delta
        mo_ref[0] = m2
        vo_ref[0] = v2

    blk = pl.BlockSpec((1, tr, wd), lambda i: (i // nt, i % nt, 0))
    return _call(
        body, comms, name="adaln_grad_adam", grid=(nl * nt,),
        in_specs=[pl.BlockSpec((tr, N_DEV), lambda i: (i % nt, 0)), pl.BlockSpec((1, N_DEV, wd), lambda i: (i // nt, 0, 0)),
                  blk, blk, blk],
        out_specs=[blk] * 4, out_shape=[jax.ShapeDtypeStruct((nl, d, wd), F32)] * 4,
        args=(cact_t, dmods, w, m, v))


REP_ROWS = ("norm_g", "dshift", "dscale", "dgate", "rg_conv_b", "rg_b_a", "rg_b_x", "rg_lambda", "ml_conv_b", "ml_norm_g",
            "ml_b_if")


def _sum_parts(recvs, name):
    def body(*refs):
        for r_ref, o_ref in zip(refs[:len(recvs)], refs[len(recvs):]):
            o_ref[...] = _sum_devices(r_ref).astype(o_ref.dtype)

    return pl.pallas_call(
        body, name=name, grid=(1,),
        in_specs=[_full(r.shape) for r in recvs], out_specs=[_full(r.shape[1:]) for r in recvs],
        out_shape=[jax.ShapeDtypeStruct(r.shape[1:], r.dtype) for r in recvs], compiler_params=_params(1),
    )(*recvs)


def _adam_replicated(vp, mp, params, nl):
    d = vp.shape[1]
    nr = len(REP_ROWS)
    names = list(params)
    mat_shape = params["rg_w_a"][0].shape[1:]
    mat_rows = mp.shape[0] // (2 * nl)

    def pieces(name):
        if name == "final_g":
            return [(lambda vp_ref, mp_ref: vp_ref[nl * nr:nl * nr + 1, :], (slice(0, 1), slice(None)))]
        out = []
        for l in range(nl):
            if name in ("rg_w_a", "rg_w_x"):
                at = (2 * l + (name == "rg_w_x")) * mat_rows
                out.append((lambda vp_ref, mp_ref, at=at: mp_ref[at:at + mat_rows, :].astype(F32).reshape(mat_shape), l))
            elif name == "b_ada":
                for j in range(3):
                    r = l * nr + 1 + j
                    out.append((lambda vp_ref, mp_ref, r=r: vp_ref[r:r + 1, :], (slice(l, l + 1), slice(j * d, (j + 1) * d))))
            else:
                r = l * nr + REP_ROWS.index(name)
                cols = slice(0, LANES) if name == "ml_b_if" else slice(None)
                out.append((lambda vp_ref, mp_ref, r=r, cols=cols: vp_ref[r:r + 1, cols], (slice(l, l + 1), slice(None))))
        return out

    def body(*refs):
        vp_ref, mp_ref = refs[:2]
        ins, outs = refs[2:2 + 3 * len(names)], refs[2 + 3 * len(names):]
        for pi, name in enumerate(names):
            w_ref, m_ref, v_ref = ins[3 * pi:3 * pi + 3]
            g_ref, d_ref, mo_ref, vo_ref = outs[4 * pi:4 * pi + 4]
            for get, idx in pieces(name):
                g = get(vp_ref, mp_ref)
                delta, m2, v2 = _adam_math(w_ref[idx], g, m_ref[idx], v_ref[idx])
                g_ref[idx] = g
                d_ref[idx] = delta
                mo_ref[idx] = m2
                vo_ref[idx] = v2

    flat = [a for name in names for a in params[name]]
    out_shape = [jax.ShapeDtypeStruct(params[name][0].shape, F32) for name in names for _ in range(4)]
    res = pl.pallas_call(
        body, name="adam_replicated", grid=(1,),
        in_specs=[_full(vp.shape), _full(mp.shape)] + [_full(a.shape) for a in flat],
        out_specs=[_full(o.shape) for o in out_shape], out_shape=out_shape, compiler_params=_params(1),
    )(vp, mp, *flat)
    return {name: res[4 * pi:4 * pi + 4] for pi, name in enumerate(names)}


class _Plan:
    def __init__(self):
        self.hosted, self.after = {}, {}

    def host(self, key, comm, then=None):
        self.hosted.setdefault(key, []).append(comm)
        if then is not None:
            self.after.setdefault(key, []).append(then)

    def comms(self, key):
        return self.hosted.pop(key, None)

    def done(self, key):
        for fn in self.after.pop(key, []):
            fn()

    def flush(self):
        while self.hosted:
            key = next(iter(self.hosted))
            _call(lambda: None, self.comms(key), name="exchange_after_%s_%d" % key, grid=(1,), in_specs=[], out_specs=[],
                  out_shape=[], args=())
            self.done(key)


VEC_TABLE = ("norm_g", "rg_conv_b", "rg_b_a", "rg_b_x", "rg_lambda", "ml_conv_b", "ml_norm_g")


def _vec_table(rep):
    rows = [rep[n] for n in VEC_TABLE]
    return jnp.stack(rows + [jnp.zeros_like(rows[0])] * (SUBLANES - len(rows)), axis=1)


def _layer_fwd(l, xl, mod3, wl, rep, plan):
    s, d = xl.shape
    t_big, t_mid = _tile_for(s, 512), _tile_for(s, 256)
    nh_ml = rep["ml_b_if"].shape[1] // 2
    vec = lambda name: _vec(rep["vecs"], l, VEC_TABLE.index(name))
    shift, scale, gate = (_vec(mod3, l, kk) for kk in range(3))
    hosted = lambda name: plan.comms((name, l)) if plan else None
    done = lambda name: plan.done((name, l)) if plan else None
    if "w_in_shard" in wl:
        u, hbf, land = _in_fwd_gathering(xl, vec("norm_g"), scale, shift, wl["w_in_shard"], wl["order"], hosted("in_proj_fwd"))
        wl["w_in_g"] = land.reshape(land.shape[0], 1, *land.shape[1:])
    else:
        u, hbf = _in_fwd(xl, vec("norm_g"), scale, shift, wl["w_in_g"], 0, t_mid, hosted("in_proj_fwd"))
    done("in_proj_fwd")
    h_rg, y_rg, *rg_gates = _rg_fwd(u, d, wl["rg_conv_w"], vec("rg_conv_b"), rep["rg_w_a_bf"][l], vec("rg_b_a"),
                                    rep["rg_w_x_bf"][l], vec("rg_b_x"), vec("rg_lambda"), t_mid, hosted("rglru_fwd"))
    done("rglru_fwd")
    q, k, v, gcol, pre = _ml_pre(u, d, wl["ml_conv_w"], vec("ml_conv_b"), wl["w_qkv"][0], wl["w_qkv"][1],
                                 wl["w_qkv"][2], wl["wif_pad"], wl["bif_pad"], t_mid, hosted("mlstm_proj_fwd"))
    done("mlstm_proj_fwd")
    grow = gcol[:, 0:16].T
    cell, y_ml, cs, ns, ms, mt = _ml_cell_fwd(q, k, v, gcol, grow, u, vec("ml_norm_g"), nh_ml, hosted("mlstm_cell_fwd"))
    done("mlstm_cell_fwd")
    x_new, y = _out_fwd(xl, y_rg, y_ml, gate, wl["w_out_g"], 0, t_big, hosted("out_proj_fwd"))
    done("out_proj_fwd")
    saved = dict(x=xl, u=u, hbf=hbf, h_rg=h_rg, y_rg=y_rg, q=q, k=k, v=v, gcol=gcol, grow=grow, cell=cell, y_ml=y_ml,
                 cs=cs, ns=ns, ms=ms, mt=mt, y=y, scale=scale, gate=gate, rg_gates=rg_gates, pre=pre)
    return x_new, saved


def _layer_bwd(l, dx, sv, wl, rep, plan, grads=None, split_last=False):
    s, d = dx.shape
    t_big, t_mid = _tile_for(s, 512), _tile_for(s, 256)
    nh_ml = rep["ml_b_if"].shape[1] // 2
    nd, _, _, w_cols = wl["w_in_g"].shape
    grads = {} if grads is None else grads
    vec = lambda name: _vec(rep["vecs"], l, VEC_TABLE.index(name))
    hosted = lambda name: plan.comms((name, l)) if plan else None
    done = lambda name: plan.done((name, l)) if plan else None
    dy_rg, dy_ml, gw_out, dgate = _out_bwd(dx, sv["gate"], sv["y"], sv["y_rg"], sv["y_ml"], wl["w_out_g"], 0, t_big,
                                           hosted("out_proj_bwd"))
    grads.update(w_out=gw_out)
    done("out_proj_bwd")
    dq, dk, dv, dgates, d_mlo, d_mlz, g_mlng = _ml_cell_bwd(
        dy_ml, sv["u"], sv["cell"], sv["q"], sv["k"], sv["v"], sv["gcol"], sv["grow"], sv["mt"], sv["cs"], sv["ns"],
        sv["ms"], vec("ml_norm_g"), nh_ml, hosted("mlstm_cell_bwd"))
    done("mlstm_cell_bwd")
    d_mlx, g_wq, g_wk, g_wv, g_wift, g_bif, g_mlcw, g_mlcb = _ml_pre_bwd(
        dq, dk, dv, dgates, sv["gcol"], sv["u"], sv["pre"], sv["q"], sv["k"], sv["v"], wl["ml_conv_w"],
        wl["w_qkv"][0], wl["w_qkv"][1], wl["w_qkv"][2], wl["wift_pad"], t_mid, hosted("mlstm_proj_bwd"))
    done("mlstm_proj_bwd")
    d_rgx, d_rgz, g_wa, g_wx, g_ba, g_bx, g_lam, g_rgcw, g_rgcb = _rg_bwd(
        dy_rg, sv["u"], sv["h_rg"], sv["rg_gates"], wl["rg_conv_w"], rep["rg_w_a_bf"][l], rep["rg_w_x_bf"][l],
        vec("rg_lambda"), t_mid, hosted("rglru_bwd"))
    grads.update(w_qkv=jnp.stack([g_wq, g_wk, g_wv]), rg_conv_w=g_rgcw[0:CONV_WIDTH], ml_conv_w=g_mlcw[0:CONV_WIDTH],
                 wif_t=g_wift[0:8], rg_w_a=g_wa, rg_w_x=g_wx)
    acc = dict(dgate=dgate, rg_conv_b=g_rgcb, rg_b_a=g_ba, rg_b_x=g_bx, rg_lambda=g_lam, ml_conv_b=g_mlcb,
               ml_b_if=g_bif, ml_norm_g=g_mlng)
    done("rglru_bwd")
    pieces = [d_rgx, d_rgz, d_mlx, d_mlo, d_mlz]
    grads.update(w_in=_in_bwd_w(pieces, sv["hbf"], w_cols, tuple(range(nd)), t_big, hosted("in_proj_bwd_w")))
    done("in_proj_bwd_w")
    n_tiles = s // t_mid
    counts = [n_tiles // 4, n_tiles - n_tiles // 4 - 1, 1] if split_last and n_tiles >= 4 else [n_tiles]
    in_args = (pieces, sv["x"], dx, vec("norm_g"), sv["scale"], wl["w_in_g"], 0, t_mid)
    res, at = None, 0
    for key, count in zip(("in_proj_bwd_x", "in_proj_bwd_x_rest", "in_proj_bwd_x_end"), counts):
        res = _in_bwd(*in_args, hosted(key), (at, count), res)
        done(key)
        at += count
    dx, dscale, dshift, g_ng = res
    acc.update(norm_g=g_ng, dshift=dshift, dscale=dscale)
    grads.update(acc=acc, dmod=jnp.concatenate([dshift[0:1], dscale[0:1], dgate[0:1]], axis=1))
    return dx, grads


def _local_step(x, c, target, wts):
    d = x.shape[1]
    nl = wts["w_in_g"].shape[1]
    mod, cact = _mod_call(c, wts["w_ada_g"], wts["b_ada"])
    wl = [dict(w_in_g=wts["w_in_g"][:, l:l + 1], w_out_g=wts["w_out_g"][:, l:l + 1], w_qkv=wts["w_qkv"][l],
               rg_conv_w=wts["rg_conv_w"][l], ml_conv_w=wts["ml_conv_w"][l], wif_pad=wts["wif_pad"][l],
               wift_pad=wts["wift_pad"][l], bif_pad=wts["bif_pad"][l]) for l in range(nl)]
    rep = dict(wts, vecs=_vec_table(wts))
    mod3 = mod.reshape(nl, 3, d)
    saved, xl = [], x
    for l in range(nl):
        xl, sv = _layer_fwd(l, xl, mod3, wl[l], rep, None)
        saved.append(sv)
    dx, loss_p, g_final = _loss_call(xl, wts["final_g"].reshape(1, -1), target, _tile_for(x.shape[0], 512))
    grads = [None] * nl
    for l in reversed(range(nl)):
        dx, grads[l] = _layer_bwd(l, dx, saved[l], wl[l], rep, None)
        grads[l]["w_ada"] = _ada_bwd_w(cact[0].reshape(d, 1), grads[l]["dmod"], wts["w_ada_g"].shape[0])
        grads[l]["b_ada"] = grads[l]["dmod"][0]
        grads[l].update({n: a[0] for n, a in grads[l]["acc"].items()})
        grads[l]["ml_b_if"] = grads[l]["ml_b_if"][0:8]
    return loss_p[0, 0], dx, grads, g_final[0]


def _full_qkv(qkv_g, d):
    nd, _, rows3, dh = qkv_g.shape
    nh = d // dh
    rsh = rows3 // (3 * nh)
    return qkv_g.reshape(nd, 3, nh, rsh, dh).transpose(1, 2, 0, 3, 4).reshape(3, nh, nd * rsh, dh)


def _small_weights(small, l, ml_b_if):
    nd = small.shape[0]
    sm = small[:, l]
    cw = sm[:, 0:2 * CONV_WIDTH].reshape(nd, 2, CONV_WIDTH, LANES).transpose(1, 2, 0, 3).reshape(2, CONV_WIDTH, nd * LANES)
    if_rows = (sm.shape[1] - 2 * CONV_WIDTH) * LANES // 8
    wif_t = sm[:, 2 * CONV_WIDTH:].reshape(nd, 8, if_rows).transpose(1, 0, 2).reshape(8, nd * if_rows)
    wift_pad = jnp.pad(wif_t, ((0, LANES - 8), (0, 0))).astype(BF16)
    return dict(rg_conv_w=cw[0], ml_conv_w=cw[1], wift_pad=wift_pad, wif_pad=wift_pad.T,
                bif_pad=jnp.pad(ml_b_if[l], (0, LANES - 8)).reshape(1, LANES))


def kernel(x, c, norm_g, w_ada, b_ada, w_in, rg_conv_w, rg_conv_b, rg_w_a, rg_b_a, rg_w_x, rg_b_x, rg_lambda, ml_conv_w, ml_conv_b, ml_w_q, ml_w_k, ml_w_v, ml_w_if, ml_b_if, ml_norm_g, w_out, final_g, loss_target, m_norm_g, m_w_ada, m_b_ada, m_w_in, m_rg_conv_w, m_rg_conv_b, m_rg_w_a, m_rg_b_a, m_rg_w_x, m_rg_b_x, m_rg_lambda, m_ml_conv_w, m_ml_conv_b, m_ml_w_q, m_ml_w_k, m_ml_w_v, m_ml_w_if, m_ml_b_if, m_ml_norm_g, m_w_out, m_final_g, v_norm_g, v_w_ada, v_b_ada, v_w_in, v_rg_conv_w, v_rg_conv_b, v_rg_w_a, v_rg_b_a, v_rg_w_x, v_rg_b_x, v_rg_lambda, v_ml_conv_w, v_ml_conv_b, v_ml_w_q, v_ml_w_k, v_ml_w_v, v_ml_w_if, v_ml_b_if, v_ml_norm_g, v_w_out, v_final_g):
    given = dict(locals())
    nl = w_in.shape[0]
    d = x.shape[2]
    rep = {n: given[n] for n in REPLICATED}
    rep.update(rg_w_a_bf=rg_w_a.astype(BF16), rg_w_x_bf=rg_w_x.astype(BF16))
    bf = lambda a: a.astype(BF16)

    def qkv_shard(prefix):
        return jnp.stack([given[prefix + "ml_w_q"], given[prefix + "ml_w_k"], given[prefix + "ml_w_v"]], axis=1).reshape(
            nl, -1, ml_w_q.shape[-1])

    def small_shard(prefix):
        return _small_pack(given[prefix + "rg_conv_w"], given[prefix + "ml_conv_w"], given[prefix + "ml_w_if"])

    plan = _Plan()
    qkv = qkv_shard("")
    first_ici = _gather_ici_comm([bf(w_in[0:1]), small_shard("")])
    condition = _exchange_comm([jnp.broadcast_to(c, (SUBLANES, d))], True)
    _run_comms([first_ici, condition], "gather_first")
    first_fwd = _gather_fwd_comm(first_ici.results)
    wcols = w_ada.shape[2]
    me = 4 * lax.axis_index("x") + 2 * lax.axis_index("y") + lax.axis_index("c")
    b_cols = jnp.pad(lax.dynamic_slice_in_dim(b_ada, me * wcols, wcols, axis=1), ((0, SUBLANES - nl), (0, 0)))
    mod_cols, cact_all = _ada_mod(condition.results[0], w_ada, b_cols, [first_fwd])
    w_in_first, small = first_fwd.results
    wl = [_small_weights(small, l, ml_b_if) for l in range(nl)]
    wl[0]["w_in_g"] = w_in_first

    def gather_behind(arrs, ici_host, fwd_host, then):
        ici = _gather_ici_comm(arrs)

        def pass_on():
            fwd = _gather_fwd_comm(ici.results)
            plan.host(fwd_host, fwd, lambda: then(fwd.results))

        plan.host(ici_host, ici, pass_on)

    def got_out(l):
        return lambda r: wl[l].update(w_out_g=r[0], w_qkv=_full_qkv(r[1], d))

    gather_behind([bf(w_out[0:1]), bf(qkv[0:1])], ("in_proj_fwd", 0), ("rglru_fwd", 0), got_out(0))
    for l in range(1, nl):
        gather_behind([bf(w_in[l:l + 1])], ("rglru_fwd", l - 1), ("mlstm_proj_fwd", l - 1),
                      lambda r, l=l: wl[l].update(w_in_g=r[0]))
        gather_behind([bf(w_out[l:l + 1]), bf(qkv[l:l + 1])], ("mlstm_cell_fwd", l - 1), ("out_proj_fwd", l - 1), got_out(l))

    mod_blocks = _exchange([mod_cols], False, "scatter_modulation")[0]
    mod3 = mod_blocks[:, 0:nl].transpose(1, 0, 2).reshape(nl, 3, d)
    rep["vecs"] = _vec_table(rep)
    saved, xl = [], x[0]
    for l in range(nl):
        xl, sv = _layer_fwd(l, xl, mod3, wl[l], rep, plan)
        saved.append(sv)
    grad_x, loss_p, g_final = _loss_call(xl, final_g.reshape(1, -1), loss_target[0], _tile_for(xl.shape[0], 512))

    keys = ("w_in", "w_out", "w_qkv", "small")
    parity = lax.axis_index("c").astype(jnp.int32).reshape(1)
    grads, recv = [None] * nl, [None] * nl

    def parts_of(g):
        return [g["w_in"], g["w_out"], _qkv_slots(g["w_qkv"], N_DEV), _small_slots(g)]

    def pair_sums(l, parts, other):
        return [_pair_sum(a, o, parity, "pair_sum_%s_layer%d" % (key, l)) for key, a, o in zip(keys, parts, other)]

    def reduce_behind(l, host_layer):
        parts = parts_of(grads[l])
        swap = _core_swap_comm(parts)

        def summed():
            sums = pair_sums(l, parts, swap.results)
            big = _chip_swap_comm([sums[0]])
            rest = _chip_swap_comm(sums[1:])
            plan.host(("mlstm_cell_bwd", host_layer), big)
            plan.host(("rglru_bwd", host_layer), rest, lambda: recv.__setitem__(l, big.results + rest.results))

        plan.host(("out_proj_bwd", host_layer), swap, summed)

    first, own = {}, {}

    def reduce_own(names, parts_fn, ready_key, swap_key, chip_key):
        def go():
            parts = parts_fn()
            swap = _core_swap_comm(parts)

            def summed():
                sums = [_pair_sum(a, o, parity, "pair_sum_%s_layer0" % n) for n, a, o in zip(names, parts, swap.results)]
                chip = _chip_swap_comm(sums)
                plan.host(chip_key, chip, lambda: own.update(zip(names, chip.results)))

            plan.host(swap_key, swap, summed)

        plan.after.setdefault(ready_key, []).append(go)

    reduce_own(["w_out"], lambda: [first["w_out"]], ("out_proj_bwd", 0), ("mlstm_cell_bwd", 0), ("mlstm_proj_bwd", 0))
    reduce_own(["w_qkv", "small"], lambda: [_qkv_slots(first["w_qkv"], N_DEV), _small_slots(first)],
               ("rglru_bwd", 0), ("in_proj_bwd_w", 0), ("in_proj_bwd_x", 0))
    reduce_own(["w_in"], lambda: [first["w_in"]], ("in_proj_bwd_w", 0), ("in_proj_bwd_x", 0), ("in_proj_bwd_x_rest", 0))

    matrices = {}

    def reduce_matrices():
        layers = [grads[l] if l > 0 else first for l in range(nl)]
        mp = jnp.stack([jnp.stack([g["rg_w_a"], g["rg_w_x"]]) for g in layers]).reshape(N_DEV, -1, LANES).astype(BF16)
        scatter = _exchange_comm([mp], False)

        def summed():
            gather = _exchange_comm(_sum_parts(scatter.results, "sum_replicated_matrices"), True)
            plan.host(("in_proj_bwd_x", 0), gather, lambda: matrices.update(mp=gather.results[0].reshape(-1, LANES)))

        plan.host(("in_proj_bwd_w", 0), scatter, summed)

    plan.after.setdefault(("rglru_bwd", 0), []).append(reduce_matrices)

    for l in reversed(range(nl)):
        if l > 0:
            grad_x, grads[l] = _layer_bwd(l, grad_x, saved[l], wl[l], rep, plan)
            reduce_behind(l, l - 1)
        else:
            grad_x, grads[l] = _layer_bwd(l, grad_x, saved[l], wl[l], rep, plan, first, True)
    plan.flush()
    recv[0] = [own[key] for key in keys]

    shard = {p: dict(w_in=given[p + "w_in"], w_out=given[p + "w_out"], w_qkv=qkv_shard(p), small=small_shard(p))
             for p in ("", "m_", "v_")}
    res = {}
    for ki, key in enumerate(keys):
        res[key] = _reduce_adam([recv[l][ki] for l in range(nl)], shard[""][key], shard["m_"][key], shard["v_"][key],
                                "reduce_adam_" + key)

    dmods = jnp.concatenate([grads[l]["dmod"] for l in range(nl)], axis=0)
    dmod_blocks = jnp.pad(dmods.reshape(nl, N_DEV, wcols).transpose(1, 0, 2), ((0, 0), (0, SUBLANES - nl), (0, 0)))
    dmod_all = _exchange([dmod_blocks], False, "scatter_dmod")[0][:, 0:nl].transpose(1, 0, 2)
    res["w_ada"] = _ada_grad_adam(cact_all.T, dmod_all, w_ada, m_w_ada, v_w_ada)

    widen = lambda a: jnp.pad(a, ((0, 0), (0, d - a.shape[1])))
    rows = [widen(grads[l]["acc"][n][0:1]) for l in range(nl) for n in REP_ROWS] + [g_final[0:1], widen(loss_p[0:1])]
    vp = jnp.concatenate(rows + [jnp.zeros(((-len(rows)) % ROW_ALIGN, d), F32)], axis=0)
    got = _exchange([vp.reshape(N_DEV, -1, d)], False, "reduce_scatter_replicated")
    vp_r = _exchange(_sum_parts(got, "sum_replicated"), True, "gather_replicated")[0].reshape(-1, d)
    mp_r = matrices["mp"]
    lanes = lambda a: jnp.pad(a, ((0, 0), (0, LANES - a.shape[1])))
    shaped = dict(ml_b_if=lanes, final_g=lambda a: a.reshape(1, d))
    names = [n for n in REPLICATED if n != "b_ada"] + ["b_ada"]
    rep_res = _adam_replicated(vp_r, mp_r, {n: tuple(shaped.get(n, lambda a: a)(given[p + n]) for p in ("", "m_", "v_"))
                                            for n in names}, nl)
    unshaped = dict(ml_b_if=lambda a: a[:, 0:ml_b_if.shape[1]], final_g=lambda a: a.reshape(d))
    rep_out = [{n: unshaped.get(n, lambda a: a)(rep_res[n][kind]) for n in names} for kind in range(4)]
    loss = vp_r[nl * len(REP_ROWS) + 1, 0]

    if_rows = ml_w_if.shape[1]
    order = ("norm_g", "w_ada", "b_ada", "w_in", "rg_conv_w", "rg_conv_b", "rg_w_a", "rg_b_a", "rg_w_x", "rg_b_x",
             "rg_lambda", "ml_conv_w", "ml_conv_b", "ml_w_q", "ml_w_k", "ml_w_v", "ml_w_if", "ml_b_if", "ml_norm_g",
             "w_out", "final_g")
    outs = [loss, grad_x[None]]
    for kind in range(4):
        qkv_k = res["w_qkv"][kind].reshape((nl, 3) + ml_w_q.shape[1:])
        rg_cw, ml_cw, wif = _small_unpack(res["small"][kind], if_rows)
        sharded = dict(w_ada=res["w_ada"][kind], w_in=res["w_in"][kind], w_out=res["w_out"][kind], ml_w_q=qkv_k[:, 0],
                       ml_w_k=qkv_k[:, 1], ml_w_v=qkv_k[:, 2], rg_conv_w=rg_cw, ml_conv_w=ml_cw, ml_w_if=wif)
        for n in order:
            outs.append(sharded[n] if n in sharded else rep_out[kind][n])
    return tuple(outs)
```

```python
import functools

import jax
import jax.numpy as jnp
from jax import lax
from jax.experimental import pallas as pl
from jax.experimental.pallas import tpu as pltpu

F32 = jnp.float32
BF16 = jnp.bfloat16
MESH_AXES = ("x", "y", "c")
N_DEV = 8
EPS = 1e-6
RG_C = 8.0
ML_CHUNK = 128
CONV_WIDTH = 4
ADAM_LR = 0.001
ADAM_B1 = 0.9
ADAM_B2 = 0.999
ADAM_EPS = 1e-08
ADAM_WD = 0.01
ADAM_STEP = 10
NEG_BIG = -1e30
LANES = 128
SUBLANES = 8
VMEM_LIMIT = 56 * 1024 * 1024
HI = lax.Precision.HIGHEST


def _params(n_grid):
    return pltpu.CompilerParams(dimension_semantics=("arbitrary",) * n_grid, vmem_limit_bytes=VMEM_LIMIT)


def _mm(a, b):
    return jnp.dot(a.astype(BF16), b.astype(BF16), preferred_element_type=F32)


def _mm_nt(a, b):
    return lax.dot_general(a.astype(BF16), b.astype(BF16), (((1,), (1,)), ((), ())), preferred_element_type=F32)


def _mm_tn(a, b):
    return lax.dot_general(a.astype(BF16), b.astype(BF16), (((0,), (0,)), ((), ())), preferred_element_type=F32)


def _mm_hi(a, b):
    return jnp.dot(a, b, precision=HI, preferred_element_type=F32)


def _sigmoid(x):
    return 1.0 / (1.0 + jnp.exp(-x))


def _softplus(x):
    return jnp.maximum(x, 0.0) + jnp.log(1.0 + jnp.exp(-jnp.abs(x)))


def _neg_expm1(x):
    poly = -x * (1.0 + x * (0.5 + x * (1.0 / 6.0 + x * (1.0 / 24.0 + x * (1.0 / 120.0)))))
    return jnp.where(jnp.abs(x) < 0.05, poly, 1.0 - jnp.exp(x))


def _iota(shape, dim):
    return lax.broadcasted_iota(jnp.int32, shape, dim)


def _colsum(x):
    return jnp.sum(x, axis=0, keepdims=True)


def _rowsum(x):
    return jnp.sum(x, axis=1, keepdims=True)


def _col(x, j):
    return _rowsum(jnp.where(_iota(x.shape, 1) == j, x, 0.0))


def _row(x, j):
    return _colsum(jnp.where(_iota(x.shape, 0) == j, x, 0.0))


def _shift_down(x, j, prev8):
    if j == 0:
        return x
    t = x.shape[0]
    main = jnp.where(_iota(x.shape, 0) >= j, pltpu.roll(x, j, 0), 0.0)
    fix = jnp.where(_iota(prev8.shape, 0) < j, pltpu.roll(prev8, j, 0), 0.0)
    return jnp.concatenate([main[0:SUBLANES] + fix, main[SUBLANES:t]], axis=0)


def _shift_up(x, j, next8):
    if j == 0:
        return x
    t = x.shape[0]
    main = jnp.where(_iota(x.shape, 0) < t - j, pltpu.roll(x, t - j, 0), 0.0)
    fix = jnp.where(_iota(next8.shape, 0) >= SUBLANES - j, pltpu.roll(next8, SUBLANES - j, 0), 0.0)
    return jnp.concatenate([main[0:t - SUBLANES], main[t - SUBLANES:t] + fix], axis=0)


def _conv(x, prev8, w_ref):
    y = w_ref[CONV_WIDTH - 1:CONV_WIDTH, :] * x
    for j in range(1, CONV_WIDTH):
        y = y + w_ref[CONV_WIDTH - 1 - j:CONV_WIDTH - j, :] * _shift_down(x, j, prev8)
    return y


def _conv_bwd(dy, x, next8, w_ref, gw_ref):
    dx = None
    for j in range(CONV_WIDTH):
        k = CONV_WIDTH - 1 - j
        up = _shift_up(dy, j, next8)
        gw_ref[k:k + 1, :] += _colsum(up * x)
        term = w_ref[k:k + 1, :] * up
        dx = term if dx is None else dx + term
    return dx


def _scan_into(a, b, carry, out_ref, reverse):
    t, c = a.shape
    groups = t // SUBLANES
    a3 = a.reshape(groups, SUBLANES, c)
    b3 = b.reshape(groups, SUBLANES, c)
    sub = _iota(a3.shape, 1)
    for step in (1, 2, 4):
        keep = sub < SUBLANES - step if reverse else sub >= step
        shift = SUBLANES - step if reverse else step
        a_s = jnp.where(keep, pltpu.roll(a3, shift, 1), 1.0)
        b_s = jnp.where(keep, pltpu.roll(b3, shift, 1), 0.0)
        b3 = a3 * b_s + b3
        a3 = a3 * a_s
    for g in (reversed(range(groups)) if reverse else range(groups)):
        rows = slice(g * SUBLANES, (g + 1) * SUBLANES)
        out_ref[rows, :] = b3[g] + a3[g] * carry
        edge = g * SUBLANES if reverse else (g + 1) * SUBLANES - 1
        carry = out_ref[edge:edge + 1, :]


def _blockdiag(x, w_ref, transpose_w=False):
    nh, dh, _ = w_ref.shape
    outs = []
    for h in range(nh):
        xs = x[:, h * dh:(h + 1) * dh]
        outs.append(_mm_nt(xs, w_ref[h]) if transpose_w else _mm(xs, w_ref[h]))
    return jnp.concatenate(outs, axis=1)


def _rg_gates(xc, wa_ref, ba_ref, wx_ref, bx_ref, lam_ref):
    r = _sigmoid(_blockdiag(xc, wa_ref) + ba_ref[...])
    ig = _sigmoid(_blockdiag(xc, wx_ref) + bx_ref[...])
    sp = _softplus(-lam_ref[...])
    log_a = -RG_C * r * sp
    a = jnp.exp(log_a)
    beta = jnp.sqrt(_neg_expm1(2.0 * log_a))
    return r, ig, sp, a, beta


def _bcast8(row):
    return jnp.broadcast_to(row, (SUBLANES, row.shape[1]))


def _full(shape):
    nd = len(shape)
    return pl.BlockSpec(shape, lambda *_: (0,) * nd)


class _Comm:
    def __init__(self, arrays, out_shapes, sems, start, finish, aliases=()):
        self.arrays, self.out_shapes, self.sems = list(arrays), list(out_shapes), list(sems)
        self.start, self.finish, self.aliases = start, finish, tuple(aliases)
        self.results = None


class _RowOf:
    def __init__(self, ref, k):
        self.ref, self.k = ref, k

    def __getitem__(self, idx):
        cols = slice(None) if idx is Ellipsis else idx[1]
        return self.ref[0, self.k:self.k + 1, cols]


class _PartOf:
    def __init__(self, ref, rows=None, cols=None, lead=None):
        self.ref, self.rows, self.cols, self.lead = ref, rows, cols, lead
        if rows is not None:
            self.shape = (rows.stop - rows.start,) + tuple(ref.shape[1:])

    def _at(self, idx):
        if self.lead is not None:
            return (self.lead,) + tuple(idx[1:])
        if self.cols is not None:
            return (slice(None), self.cols)
        return (self.rows, slice(None) if idx is Ellipsis else idx[1])

    def __getitem__(self, idx):
        return self.ref[self._at(idx)]

    def __setitem__(self, idx, value):
        self.ref[self._at(idx)] = value


def _vec(table, layer, k):
    return ("row", table, layer, k)


def _is_row(arg):
    return isinstance(arg, tuple) and len(arg) == 4 and arg[0] == "row"


def _call(body, comms, *, name, grid, in_specs, out_specs, out_shape, args, scratch_shapes=(), aliases=None, prefetch=()):
    comms = [cm for cm in (comms or []) if cm is not None]
    rows = {i: a[3] for i, a in enumerate(args) if _is_row(a)}
    in_specs = [pl.BlockSpec((1,) + a[1].shape[1:], functools.partial(lambda layer, *_: (layer, 0, 0), a[2]))
                if _is_row(a) else sp for a, sp in zip(args, in_specs)]
    args = tuple(a[1] if _is_row(a) else a for a in args)
    n_in, n_out, n_sc = len(args), len(out_shape), len(scratch_shapes)
    c_arrays = [a for cm in comms for a in cm.arrays]
    c_outs = [o for cm in comms for o in cm.out_shapes]
    c_sems = [sm for cm in comms for sm in cm.sems]
    aliases, a_at, o_at = dict(aliases or {}), n_in, n_out
    for cm in comms:
        for (i, j) in cm.aliases:
            aliases[a_at + i] = o_at + j
        a_at += len(cm.arrays)
        o_at += len(cm.out_shapes)

    def wrapped(*refs):
        pre, refs = refs[:len(prefetch)], refs[len(prefetch):]
        ins, c_in = refs[:n_in], refs[n_in:n_in + len(c_arrays)]
        ins = [_RowOf(r, rows[i]) if i in rows else r for i, r in enumerate(ins)]
        at = n_in + len(c_arrays)
        outs, c_out = refs[at:at + n_out], refs[at + n_out:at + n_out + len(c_outs)]
        at += n_out + len(c_outs)
        scr, sems = refs[at:at + n_sc], refs[at + n_sc:]
        views, ia, io, isem = [], 0, 0, 0
        for cm in comms:
            views.append((c_in[ia:ia + len(cm.arrays)], c_out[io:io + len(cm.out_shapes)], sems[isem:isem + len(cm.sems)]))
            ia, io, isem = ia + len(cm.arrays), io + len(cm.out_shapes), isem + len(cm.sems)
        if comms:
            @pl.when(pl.program_id(0) == 0)
            def _():
                for cm, view in zip(comms, views):
                    cm.start(*view)

        body(*pre, *ins, *outs, *scr)
        if comms:
            @pl.when(pl.program_id(0) == grid[0] - 1)
            def _():
                for cm, view in zip(comms, views):
                    cm.finish(*view)

    hbm = pl.BlockSpec(memory_space=pl.ANY)
    specs = dict(grid=grid, in_specs=list(in_specs) + [hbm] * len(c_arrays), out_specs=list(out_specs) + [hbm] * len(c_outs),
                 scratch_shapes=list(scratch_shapes) + c_sems)
    if prefetch:
        specs = dict(grid_spec=pltpu.PrefetchScalarGridSpec(num_scalar_prefetch=len(prefetch), **specs))
        aliases = {i + len(prefetch): o for i, o in aliases.items()}
    res = pl.pallas_call(
        wrapped, name=name, out_shape=list(out_shape) + c_outs, input_output_aliases=aliases,
        compiler_params=_params(len(grid)), **specs,
    )(*prefetch, *args, *c_arrays)
    at = n_out
    for cm in comms:
        cm.results = list(res[at:at + len(cm.out_shapes)])
        at += len(cm.out_shapes)
    return list(res[:n_out])


def _mod_call(c, w_ada_g, b_ada):
    nd, nl, d, w = w_ada_g.shape

    def body(c_ref, w_ref, b_ref, mod_ref, cact_ref):
        cv = c_ref[...]
        ca = _bcast8(cv * _sigmoid(cv))
        cact_ref[...] = ca
        mod_ref[0, 0] = _mm(ca, w_ref[0, 0]) + b_ref[0, 0]

    mod, cact = pl.pallas_call(
        body, name="adaln_mod", grid=(nl, nd),
        in_specs=[_full((1, d)),
                  pl.BlockSpec((1, 1, d, w), lambda l, j: (j, l, 0, 0)),
                  pl.BlockSpec((1, 1, 1, w), lambda l, j: (l, j, 0, 0))],
        out_specs=[pl.BlockSpec((1, 1, SUBLANES, w), lambda l, j: (l, j, 0, 0)), _full((SUBLANES, d))],
        out_shape=[jax.ShapeDtypeStruct((nl, nd, SUBLANES, w), F32), jax.ShapeDtypeStruct((SUBLANES, d), F32)],
        compiler_params=_params(2),
    )(c, w_ada_g, b_ada.reshape(nl, nd, 1, w))
    return mod[:, :, 0, :].reshape(nl, nd * w), cact


def _join_columns(w_ref, wcat):
    nd, _, _, w = w_ref.shape

    @pl.when(pl.program_id(0) == 0)
    def _():
        for j in range(nd):
            wcat[:, j * w:(j + 1) * w] = w_ref[j, 0]


def _in_fwd(x, ng, scale, shift, w_in_g, layer, tile, comms=None):
    s, d = x.shape
    nd, _, _, w = w_in_g.shape

    def body(x_ref, ng_ref, sc_ref, sh_ref, w_ref, u_ref, h_ref, wcat):
        _join_columns(w_ref, wcat)
        xv = x_ref[...]
        rs = lax.rsqrt(jnp.mean(xv * xv, axis=1, keepdims=True) + EPS)
        hb = (xv * rs * ng_ref[...] * (1.0 + sc_ref[...]) + sh_ref[...]).astype(BF16)
        h_ref[...] = hb
        u_ref[...] = jnp.dot(hb, wcat[...], preferred_element_type=F32)

    return _call(
        body, comms, name="in_proj_fwd", grid=(s // tile,),
        in_specs=[pl.BlockSpec((tile, d), lambda i: (i, 0)), _full((1, d)), _full((1, d)), _full((1, d)),
                  pl.BlockSpec((nd, 1, d, w), lambda i: (0, layer, 0, 0), pipeline_mode=pl.Buffered(1))],
        out_specs=[pl.BlockSpec((tile, nd * w), lambda i: (i, 0)), pl.BlockSpec((tile, d), lambda i: (i, 0))],
        out_shape=[jax.ShapeDtypeStruct((s, nd * w), F32), jax.ShapeDtypeStruct((s, d), BF16)],
        scratch_shapes=[pltpu.VMEM((d, nd * w), BF16)],
        args=(x, ng, scale, shift, w_in_g))


def _in_fwd_gathering(x, ng, scale, shift, w_shard, order, comms=None):
    s, d = x.shape
    w = w_shard.shape[1]
    rows = min(512, s)

    def body(order_ref, x_ref, ng_ref, sc_ref, sh_ref, wsh_ref, u_ref, h_ref, land_ref, wbuf, xbuf, send_sems, recv_sems,
             dma_sems):
        k = pl.program_id(0)
        mx, my, mc, sibling, chips = _mesh_place()
        me = (mx, my, mc)
        peers = [sibling] + [(*chip, mc) for chip in chips]
        blocks = peers + [(*chip, 1 - mc) for chip in chips]

        def own_send(j):
            return _remote(wsh_ref, land_ref.at[_slot(me)], send_sems, recv_sems, j, peers[j])

        def landing(j):
            return _remote(wsh_ref, land_ref.at[_slot(blocks[j])], send_sems, recv_sems, j, sibling)

        def pass_on(j):
            at = land_ref.at[_slot((*chips[j - 4], mc))]
            return _remote(at, at, send_sems, recv_sems, j, sibling)

        def load(src, slot):
            cp = pltpu.make_async_copy(src, wbuf.at[slot], dma_sems.at[slot])
            cp.start()
            cp.wait()

        @pl.when(k == 0)
        def _():
            for j in range(4):
                own_send(j).start()
            keep = pltpu.make_async_copy(wsh_ref, land_ref.at[_slot(me)], dma_sems.at[2])
            keep.start()
            fetch = lambda i: pltpu.make_async_copy(x_ref.at[i * rows:(i + 1) * rows, :], xbuf.at[i % 2], dma_sems.at[3 + i % 2])
            fetch(0).start()
            for i in range(s // rows):
                if (i + 1) * rows < s:
                    fetch(i + 1).start()
                fetch(i).wait()
                xv = xbuf[i % 2]
                rs = lax.rsqrt(jnp.mean(xv * xv, axis=1, keepdims=True) + EPS)
                h_ref[i * rows:(i + 1) * rows, :] = (xv * rs * ng_ref[...] * (1.0 + sc_ref[...]) + sh_ref[...]).astype(BF16)
            load(wsh_ref, 0)
            keep.wait()

        for step in range(1, N_DEV):
            @pl.when(k == step)
            def _(step=step):
                j = step - 1
                landing(j).wait_recv()
                if 1 <= j <= 3:
                    pass_on(j + 3).start()
                load(land_ref.at[_slot(blocks[j])], step % 2)

        u_ref[...] = jnp.dot(h_ref[...], wbuf[k % 2], preferred_element_type=F32)

        @pl.when(k == N_DEV - 1)
        def _():
            for j in range(4):
                own_send(j).wait_send()
            for j in range(4, N_DEV - 1):
                pass_on(j).wait_send()

    one = pltpu.SemaphoreType.DMA
    u, hbf, land = _call(
        body, comms, name="in_proj_fwd", grid=(N_DEV,), prefetch=(order,),
        in_specs=[pl.BlockSpec(memory_space=pl.ANY), _full((1, d)), _full((1, d)), _full((1, d)),
                  pl.BlockSpec(memory_space=pl.ANY)],
        out_specs=[pl.BlockSpec((s, w), lambda k, o: (0, o[k])),
                   pl.BlockSpec((s, d), lambda k, o: (0, 0), pipeline_mode=pl.Buffered(1)), pl.BlockSpec(memory_space=pl.ANY)],
        out_shape=[jax.ShapeDtypeStruct((s, N_DEV * w), F32), jax.ShapeDtypeStruct((s, d), BF16),
                   jax.ShapeDtypeStruct((N_DEV, d, w), BF16)],
        scratch_shapes=[pltpu.VMEM((2, d, w), BF16), pltpu.VMEM((2, rows, d), F32), one((N_DEV - 1,)), one((N_DEV - 1,)),
                        one((5,))],
        args=(x, ng, scale, shift, w_shard))
    return u, hbf, land


def _rg_fwd(u, d, conv_w, conv_b, w_a, b_a, w_x, b_x, lam, tile, comms=None):
    s = u.shape[0]

    def body(x_ref, z_ref, cw_ref, cb_ref, wa_ref, ba_ref, wx_ref, bx_ref, lam_ref,
             h_ref, y_ref, xc_ref, r_ref, i_ref, a_ref, beta_ref, prev8, hcar):
        @pl.when(pl.program_id(0) == 0)
        def _():
            prev8[...] = jnp.zeros_like(prev8)
            hcar[...] = jnp.zeros_like(hcar)

        x = x_ref[...]
        xc = _conv(x, prev8[...], cw_ref) + cb_ref[...]
        prev8[...] = x[tile - SUBLANES:tile, :]
        r, ig, _, a, beta = _rg_gates(xc, wa_ref, ba_ref, wx_ref, bx_ref, lam_ref)
        xc_ref[...] = xc
        r_ref[...] = r
        i_ref[...] = ig
        a_ref[...] = a
        beta_ref[...] = beta
        _scan_into(a, beta * ig * xc, hcar[SUBLANES - 1:SUBLANES, :], h_ref, False)
        h = h_ref[...]
        hcar[...] = h[tile - SUBLANES:tile, :]
        z = z_ref[...]
        y_ref[...] = (h * z * _sigmoid(z)).astype(BF16)

    vec = _full((1, d))
    return _call(
        body, comms, name="rglru_fwd", grid=(s // tile,),
        in_specs=[pl.BlockSpec((tile, d), lambda i: (i, 0)), pl.BlockSpec((tile, d), lambda i: (i, 1)),
                  _full(conv_w.shape), vec, _full(w_a.shape), vec, _full(w_x.shape), vec, vec],
        out_specs=[pl.BlockSpec((tile, d), lambda i: (i, 0))] * 7,
        out_shape=[jax.ShapeDtypeStruct((s, d), F32), jax.ShapeDtypeStruct((s, d), BF16)] + [jax.ShapeDtypeStruct((s, d), F32)] * 5,
        scratch_shapes=[pltpu.VMEM((SUBLANES, d), F32), pltpu.VMEM((SUBLANES, d), F32)],
        args=(u, u, conv_w, conv_b, w_a, b_a, w_x, b_x, lam))


def _ml_pre(u, d, conv_w, conv_b, w_q, w_k, w_v, wif, bif, tile, comms=None):
    s = u.shape[0]
    nh = w_q.shape[0]

    def body(x_ref, cw_ref, cb_ref, wq_ref, wk_ref, wv_ref, wif_ref, bif_ref, q_ref, k_ref, v_ref, g_ref, pre_ref, prev8):
        @pl.when(pl.program_id(0) == 0)
        def _():
            prev8[...] = jnp.zeros_like(prev8)

        x = x_ref[...]
        pre = _conv(x, prev8[...], cw_ref) + cb_ref[...]
        prev8[...] = x[tile - SUBLANES:tile, :]
        xc = pre * _sigmoid(pre)
        q = _blockdiag(xc, wq_ref)
        k = _blockdiag(xc, wk_ref)
        v = _blockdiag(x, wv_ref)
        pre_ref[...] = pre
        q_ref[...] = q
        k_ref[...] = k
        v_ref[...] = v
        g = _mm(q, wif_ref[0:d, :]) + _mm(k, wif_ref[d:2 * d, :]) + _mm(v, wif_ref[2 * d:3 * d, :]) + bif_ref[...]
        lane = _iota(g.shape, 1)
        gl = jnp.where(lane < 4, g, jnp.where(lane < 8, -_softplus(-g), 0.0))
        tri = jnp.where(_iota((ML_CHUNK, ML_CHUNK), 1) <= _iota((ML_CHUNK, ML_CHUNK), 0), 1.0, 0.0)
        cums = [_mm_hi(tri, gl[c * ML_CHUNK:(c + 1) * ML_CHUNK, :]) for c in range(tile // ML_CHUNK)]
        cum = cums[0] if len(cums) == 1 else jnp.concatenate(cums, axis=0)
        g_ref[...] = gl + jnp.where((lane >= 8) & (lane < 12), pltpu.roll(cum, 4, 1), 0.0)

    vec = _full((1, d))
    return _call(
        body, comms, name="mlstm_proj_fwd", grid=(s // tile,),
        in_specs=[pl.BlockSpec((tile, d), lambda i: (i, 2)), _full(conv_w.shape), vec,
                  _full(w_q.shape), _full(w_k.shape), _full(w_v.shape), _full(wif.shape), _full((1, LANES))],
        out_specs=[pl.BlockSpec((tile, d), lambda i: (i, 0))] * 3 + [pl.BlockSpec((tile, LANES), lambda i: (i, 0)),
                                                                     pl.BlockSpec((tile, d), lambda i: (i, 0))],
        out_shape=[jax.ShapeDtypeStruct((s, d), F32)] * 3 + [jax.ShapeDtypeStruct((s, LANES), F32),
                                                             jax.ShapeDtypeStruct((s, d), F32)],
        scratch_shapes=[pltpu.VMEM((SUBLANES, d), F32)],
        args=(u, conv_w, conv_b, w_q, w_k, w_v, wif, bif))


CELL_CHUNKS_PER_STEP = 4
CELL_BWD_CHUNKS_PER_STEP = 2


def _cell_chunk(h, nh, q_ref, k_ref, v_ref, gc, gr, m_prev, c_h, n_h, m_t=None, r0=0):
    lc = ML_CHUNK
    dh = q_ref.shape[1] // nh
    sl = slice(h * dh, (h + 1) * dh)
    qh = q_ref[r0:r0 + lc, sl]
    kh = k_ref[r0:r0 + lc, sl] * (dh ** -0.5)
    vh = v_ref[r0:r0 + lc, sl]
    li_c = _col(gc, h)
    b_c = _col(gc, 8 + h)
    lib_r = _row(gr, h) - _row(gr, 8 + h)
    b_last = _colsum(jnp.where(_iota((lc, 1), 0) == lc - 1, b_c, 0.0))
    causal = _iota((lc, lc), 1) <= _iota((lc, lc), 0)
    dmat = jnp.where(causal, b_c + lib_r, NEG_BIG)
    m_inter = b_c + m_prev
    if m_t is None:
        m_t = jnp.maximum(m_inter, jnp.max(dmat, axis=1, keepdims=True))
    w_intra = jnp.exp(dmat - m_t)
    w_inter = jnp.exp(m_inter - m_t)
    amat = _mm_nt(qh, kh)
    smat = amat * w_intra
    qc = _mm(qh, c_h)
    qn = _rowsum(qh * n_h)
    den = _rowsum(smat) + w_inter * qn
    gst = b_last - b_c + li_c
    m_new = jnp.maximum(b_last + m_prev, jnp.max(gst, axis=0, keepdims=True))
    w_state = jnp.exp(gst - m_new)
    decay = jnp.exp(b_last + m_prev - m_new)
    return dict(sl=sl, qh=qh, kh=kh, vh=vh, m_t=m_t, w_intra=w_intra, w_inter=w_inter, smat=smat, qc=qc, qn=qn,
                den=den, m_new=m_new, w_state=w_state, decay=decay)


def _ml_cell_fwd(q, k, v, gcol, grow, u, ng, nh, comms=None):
    s, d = q.shape
    lc = ML_CHUNK
    nc = s // lc
    dh = d // nh

    per = CELL_CHUNKS_PER_STEP if nc % CELL_CHUNKS_PER_STEP == 0 else 1

    def body(q_ref, k_ref, v_ref, gc_ref, gr_ref, o_ref, z_ref, ng_ref,
             cell_ref, y_ref, cs_ref, ns_ref, ms_ref, mt_ref, c_sc, n_sc, m_sc):
        @pl.when(pl.program_id(0) == 0)
        def _():
            c_sc[...] = jnp.zeros_like(c_sc)
            n_sc[...] = jnp.zeros_like(n_sc)
            m_sc[...] = jnp.zeros_like(m_sc)

        lane = _iota((lc, LANES), 1)
        for cc in range(per):
            rows = slice(cc * lc, (cc + 1) * lc)
            gc = gc_ref[rows, :]
            gr = gr_ref[:, rows]
            mt_acc = jnp.zeros((lc, LANES), F32)
            for h in range(nh):
                c_h = c_sc[h]
                n_h = n_sc[h, 0:1, :]
                m_prev = jnp.max(m_sc[h, 0:1, :], axis=1, keepdims=True)
                cs_ref[cc, h] = c_h
                ns_ref[cc, h] = n_sc[h]
                ms_ref[cc, h] = m_sc[h]
                t = _cell_chunk(h, nh, q_ref, k_ref, v_ref, gc, gr, m_prev, c_h, n_h, r0=cc * lc)
                sl = t["sl"]
                num = _mm(t["smat"], t["vh"]) + t["w_inter"] * t["qc"]
                cell_h = num / jnp.maximum(jnp.abs(t["den"]), jnp.exp(-t["m_t"]))
                mt_acc = jnp.where(lane == h, t["m_t"], mt_acc)
                kw = t["kh"] * t["w_state"]
                c_sc[h] = t["decay"] * c_h + _mm_tn(kw, t["vh"])
                n_sc[h] = _bcast8(t["decay"] * n_h + _colsum(kw))
                m_sc[h] = jnp.broadcast_to(t["m_new"], (SUBLANES, LANES))
                hg = _sigmoid(o_ref[rows, sl]) * cell_h
                hn = hg * lax.rsqrt(jnp.mean(hg * hg, axis=1, keepdims=True) + EPS)
                z = z_ref[rows, sl]
                cell_ref[rows, sl] = cell_h
                y_ref[rows, sl] = (hn * ng_ref[:, sl] * z * _sigmoid(z)).astype(BF16)
            mt_ref[rows, :] = mt_acc

    tok = pl.BlockSpec((per * lc, d), lambda c: (c, 0))
    return _call(
        body, comms, name="mlstm_cell_fwd", grid=(nc // per,),
        in_specs=[tok, tok, tok, pl.BlockSpec((per * lc, LANES), lambda c: (c, 0)),
                  pl.BlockSpec((16, per * lc), lambda c: (0, c)),
                  pl.BlockSpec((per * lc, d), lambda c: (c, 3)), pl.BlockSpec((per * lc, d), lambda c: (c, 4)), _full((1, d))],
        out_specs=[tok, tok, pl.BlockSpec((per, nh, dh, dh), lambda c: (c, 0, 0, 0)),
                   pl.BlockSpec((per, nh, SUBLANES, dh), lambda c: (c, 0, 0, 0)),
                   pl.BlockSpec((per, nh, SUBLANES, LANES), lambda c: (c, 0, 0, 0)),
                   pl.BlockSpec((per * lc, LANES), lambda c: (c, 0))],
        out_shape=[jax.ShapeDtypeStruct((s, d), F32), jax.ShapeDtypeStruct((s, d), BF16),
                   jax.ShapeDtypeStruct((nc, nh, dh, dh), F32), jax.ShapeDtypeStruct((nc, nh, SUBLANES, dh), F32),
                   jax.ShapeDtypeStruct((nc, nh, SUBLANES, LANES), F32), jax.ShapeDtypeStruct((s, LANES), F32)],
        scratch_shapes=[pltpu.VMEM((nh, dh, dh), F32), pltpu.VMEM((nh, SUBLANES, dh), F32),
                        pltpu.VMEM((nh, SUBLANES, LANES), F32)],
        args=(q, k, v, gcol, grow, u, u, ng))


def _out_fwd(x, y_rg, y_ml, gate, w_out_g, layer, tile, comms=None):
    s, d = x.shape
    nd, _, r, _ = w_out_g.shape

    def body(x_ref, yr_ref, ym_ref, g_ref, w_ref, xn_ref, y_ref):
        ycat = jnp.concatenate([yr_ref[...].astype(BF16), ym_ref[...].astype(BF16)], axis=1)
        acc = jnp.dot(ycat, w_ref[...].reshape(nd * r, d), preferred_element_type=F32)
        y_ref[...] = acc
        xn_ref[...] = x_ref[...] + g_ref[...] * acc

    tok = pl.BlockSpec((tile, d), lambda i: (i, 0))
    return _call(
        body, comms, name="out_proj_fwd", grid=(s // tile,),
        in_specs=[tok, tok, tok, _full((1, d)), pl.BlockSpec((nd, 1, r, d), lambda i: (0, layer, 0, 0))],
        out_specs=[tok, tok],
        out_shape=[jax.ShapeDtypeStruct((s, d), F32)] * 2,
        args=(x, y_rg, y_ml, gate, w_out_g))


def _loss_call(x, fg, target, tile):
    s, d = x.shape

    def body(x_ref, g_ref, t_ref, dx_ref, loss_ref, gg_ref):
        @pl.when(pl.program_id(0) == 0)
        def _():
            loss_ref[...] = jnp.zeros_like(loss_ref)
            gg_ref[...] = jnp.zeros_like(gg_ref)

        xv = x_ref[...]
        g = g_ref[...]
        rs = lax.rsqrt(jnp.mean(xv * xv, axis=1, keepdims=True) + EPS)
        xh = xv * rs
        e = xh * g - t_ref[...]
        loss_ref[...] += jnp.broadcast_to(_colsum(_rowsum(e * e)) * (0.5 / d), loss_ref.shape)
        dy = e * (1.0 / d)
        gg_ref[...] += _bcast8(_colsum(dy * xh))
        dxh = dy * g
        dx_ref[...] = rs * (dxh - xh * jnp.mean(dxh * xh, axis=1, keepdims=True))

    tok = pl.BlockSpec((tile, d), lambda i: (i, 0))
    return pl.pallas_call(
        body, name="final_norm_loss", grid=(s // tile,),
        in_specs=[tok, _full((1, d)), tok],
        out_specs=[tok, _full((SUBLANES, LANES)), _full((SUBLANES, d))],
        out_shape=[jax.ShapeDtypeStruct((s, d), F32), jax.ShapeDtypeStruct((SUBLANES, LANES), F32),
                   jax.ShapeDtypeStruct((SUBLANES, d), F32)],
        compiler_params=_params(1),
    )(x, fg, target)


def _ml_out_stage_bwd(dy, cell, o, z, ng):
    so = _sigmoid(o)
    hg = so * cell
    rinv = lax.rsqrt(jnp.mean(hg * hg, axis=1, keepdims=True) + EPS)
    hn = hg * rinv
    sz = _sigmoid(z)
    dz = dy * hn * ng * (sz + z * sz * (1.0 - sz))
    dymid = dy * z * sz
    dhn = dymid * ng
    dhg = rinv * (dhn - hn * jnp.mean(dhn * hn, axis=1, keepdims=True))
    return dz, dhg * cell * so * (1.0 - so), dhg * so, _colsum(dymid * hn)


def _out_bwd_with_output_stage(dxo, gate, y, y_rg, y_ml, cell, u, ng, w_out_g, layer, tile, nh, comms=None):
    s, d = dxo.shape
    nd, _, r, _ = w_out_g.shape
    dh = d // nh

    def body(dx_ref, g_ref, y_ref, yr_ref, ym_ref, cell_ref, o_ref, z_ref, ng_ref, w_ref,
             dyr_ref, dcell_ref, do_ref, dz_ref, gw_ref, dg_ref, gng_ref):
        @pl.when(pl.program_id(0) == 0)
        def _():
            for ref in (gw_ref, dg_ref, gng_ref):
                ref[...] = jnp.zeros_like(ref)

        dxv = dx_ref[...]
        dg_ref[...] += _bcast8(_colsum(dxv * y_ref[...]))
        dyb = (dxv * g_ref[...]).astype(BF16)
        dycat = lax.dot_general(dyb, w_ref[...].reshape(nd * r, d), (((1,), (1,)), ((), ())), preferred_element_type=F32)
        dyr_ref[...] = dycat[:, 0:d]
        ycat = jnp.concatenate([yr_ref[...].astype(BF16), ym_ref[...].astype(BF16)], axis=1)
        gw_ref[...] += lax.dot_general(ycat, dyb, (((0,), (0,)), ((), ())), preferred_element_type=F32).reshape(nd, r, d)
        for h in range(nh):
            sl = slice(h * dh, (h + 1) * dh)
            dz, do, dcell, gng = _ml_out_stage_bwd(dycat[:, d + h * dh:d + (h + 1) * dh], cell_ref[:, sl], o_ref[:, sl],
                                                   z_ref[:, sl], ng_ref[:, sl])
            dz_ref[:, sl] = dz.astype(BF16)
            do_ref[:, sl] = do.astype(BF16)
            dcell_ref[:, sl] = dcell
            gng_ref[:, sl] += _bcast8(gng)

    tok = pl.BlockSpec((tile, d), lambda i: (i, 0))
    acc = _full((SUBLANES, d))
    return _call(
        body, comms, name="out_proj_bwd", grid=(s // tile,),
        in_specs=[tok, _full((1, d)), tok, tok, tok, tok, pl.BlockSpec((tile, d), lambda i: (i, 3)),
                  pl.BlockSpec((tile, d), lambda i: (i, 4)), _full((1, d)),
                  pl.BlockSpec((nd, 1, r, d), lambda i: (0, layer, 0, 0), pipeline_mode=pl.Buffered(1))],
        out_specs=[tok, tok, tok, tok, pl.BlockSpec((nd, r, d), lambda i: (0, 0, 0), pipeline_mode=pl.Buffered(1)), acc, acc],
        out_shape=[jax.ShapeDtypeStruct((s, d), F32)] * 2 + [jax.ShapeDtypeStruct((s, d), BF16)] * 2
        + [jax.ShapeDtypeStruct((nd, r, d), F32)] + [jax.ShapeDtypeStruct((SUBLANES, d), F32)] * 2,
        args=(dxo, gate, y, y_rg, y_ml, cell, u, u, ng, w_out_g))


def _ml_cell_only_bwd(dcell, cell, q, k, v, gcol, grow, mt, cs, ns, ms, nh, comms=None):
    s, d = q.shape
    lc = ML_CHUNK
    nc = s // lc
    dh = d // nh

    def body(dcell_ref, cell_ref, q_ref, k_ref, v_ref, gc_ref, gr_ref, mt_ref, cs_ref, ns_ref, ms_ref,
             dq_ref, dk_ref, dv_ref, dg_ref, dc_sc, dn_sc):
        @pl.when(pl.program_id(0) == 0)
        def _():
            dc_sc[...] = jnp.zeros_like(dc_sc)
            dn_sc[...] = jnp.zeros_like(dn_sc)

        gc = gc_ref[...]
        gr = gr_ref[...]
        mtv = mt_ref[...]
        lane = _iota((lc, LANES), 1)
        rowv = _iota((lc, 1), 0)
        dg_acc = jnp.zeros((lc, LANES), F32)
        for h in range(nh):
            c_h = cs_ref[0, h]
            n_h = ns_ref[0, h, 0:1, :]
            m_prev = jnp.max(ms_ref[0, h, 0:1, :], axis=1, keepdims=True)
            t = _cell_chunk(h, nh, q_ref, k_ref, v_ref, gc, gr, m_prev, c_h, n_h, m_t=_col(mtv, h))
            sl, qh, kh, vh = t["sl"], t["qh"], t["kh"], t["vh"]
            w_intra, w_inter, smat, w_state, decay = t["w_intra"], t["w_inter"], t["smat"], t["w_state"], t["decay"]
            cell_h = cell_ref[:, sl]
            dcell = dcell_ref[:, sl]
            eneg = jnp.exp(-t["m_t"])
            aden = jnp.abs(t["den"])
            nst = jnp.maximum(aden, eneg)
            dnum = dcell / nst
            dden = jnp.where(aden > eneg, -_rowsum(cell_h * dcell) / nst * jnp.sign(t["den"]), 0.0)
            pmat = _mm_nt(dnum, vh) + dden
            damat = pmat * w_intra
            gmat = pmat * smat
            wdn = w_inter * dnum
            wdd = w_inter * dden
            dqh = _mm(damat, kh) + _mm_nt(wdn, c_h) + wdd * n_h
            dkh = _mm_tn(damat, qh)
            dvh = _mm_tn(smat, dnum)
            dw_inter = _rowsum(dnum * t["qc"]) + dden * t["qn"]
            dcn = dc_sc[h]
            dnn = dn_sc[h, 0:1, :]
            kw = kh * w_state
            dkw = _mm_nt(vh, dcn) + dnn
            dvh = dvh + _mm(kw, dcn)
            dkh = dkh + dkw * w_state
            dgst = _rowsum(dkw * kh) * w_state
            ddecay = _colsum(_rowsum(dcn * c_h)) + _rowsum(dnn * n_h)
            db_last = _colsum(dgst) + ddecay * decay
            rs_g = _rowsum(gmat)
            cs_g = _rowsum(gmat.T)
            db = rs_g - cs_g + dw_inter * w_inter - dgst + jnp.where(rowv == lc - 1, db_last, 0.0)
            dli = cs_g + dgst
            dc_sc[h] = decay * dcn + _mm_tn(qh, wdn)
            dn_sc[h] = _bcast8(decay * dnn + _colsum(qh * wdd))
            dq_ref[:, sl] = dqh
            dk_ref[:, sl] = dkh * (dh ** -0.5)
            dv_ref[:, sl] = dvh
            dg_acc = jnp.where(lane == h, dli, jnp.where(lane == 4 + h, db, dg_acc))
        dg_ref[...] = dg_acc

    rev = lambda c: nc - 1 - c
    tok = pl.BlockSpec((lc, d), lambda c: (rev(c), 0))
    g128 = pl.BlockSpec((lc, LANES), lambda c: (rev(c), 0))
    return _call(
        body, comms, name="mlstm_cell_bwd", grid=(nc,),
        in_specs=[tok, tok, tok, tok, tok, g128, pl.BlockSpec((16, lc), lambda c: (0, rev(c))), g128,
                  pl.BlockSpec((1, nh, dh, dh), lambda c: (rev(c), 0, 0, 0)),
                  pl.BlockSpec((1, nh, SUBLANES, dh), lambda c: (rev(c), 0, 0, 0)),
                  pl.BlockSpec((1, nh, SUBLANES, LANES), lambda c: (rev(c), 0, 0, 0))],
        out_specs=[tok, tok, tok, g128],
        out_shape=[jax.ShapeDtypeStruct((s, d), F32)] * 3 + [jax.ShapeDtypeStruct((s, LANES), F32)],
        scratch_shapes=[pltpu.VMEM((nh, dh, dh), F32), pltpu.VMEM((nh, SUBLANES, dh), F32)],
        args=(dcell, cell, q, k, v, gcol, grow, mt, cs, ns, ms))


def _out_bwd(dxo, gate, y, y_rg, y_ml, w_out_g, layer, tile, comms=None):
    s, d = dxo.shape
    nd, _, r, _ = w_out_g.shape

    def body(dx_ref, g_ref, y_ref, yr_ref, ym_ref, w_ref, dyr_ref, dym_ref, gw_ref, dg_ref):
        @pl.when(pl.program_id(0) == 0)
        def _():
            gw_ref[...] = jnp.zeros_like(gw_ref)
            dg_ref[...] = jnp.zeros_like(dg_ref)

        dxv = dx_ref[...]
        dg_ref[...] += _bcast8(_colsum(dxv * y_ref[...]))
        dyb = (dxv * g_ref[...]).astype(BF16)
        dycat = lax.dot_general(dyb, w_ref[...].reshape(nd * r, d), (((1,), (1,)), ((), ())), preferred_element_type=F32)
        dyr_ref[...] = dycat[:, 0:d]
        dym_ref[...] = dycat[:, d:2 * d]
        ycat = jnp.concatenate([yr_ref[...].astype(BF16), ym_ref[...].astype(BF16)], axis=1)
        gw_ref[...] += lax.dot_general(ycat, dyb, (((0,), (0,)), ((), ())), preferred_element_type=F32).reshape(nd, r, d)

    tok = pl.BlockSpec((tile, d), lambda i: (i, 0))
    return _call(
        body, comms, name="out_proj_bwd", grid=(s // tile,),
        in_specs=[tok, _full((1, d)), tok, tok, tok, pl.BlockSpec((nd, 1, r, d), lambda i: (0, layer, 0, 0))],
        out_specs=[tok, tok, _full((nd, r, d)), _full((SUBLANES, d))],
        out_shape=[jax.ShapeDtypeStruct((s, d), F32)] * 2 + [jax.ShapeDtypeStruct((nd, r, d), F32),
                                                             jax.ShapeDtypeStruct((SUBLANES, d), F32)],
        args=(dxo, gate, y, y_rg, y_ml, w_out_g))


def _ml_cell_bwd(dy_ml, u, cell, q, k, v, gcol, grow, mt, cs, ns, ms, ng, nh, comms=None):
    s, d = q.shape
    lc = ML_CHUNK
    nc = s // lc
    dh = d // nh

    per = CELL_BWD_CHUNKS_PER_STEP if nc % CELL_BWD_CHUNKS_PER_STEP == 0 else 1

    def body(*refs):
        gng_ref, dc_sc, dn_sc = refs[20], refs[21], refs[22]

        @pl.when(pl.program_id(0) == 0)
        def _():
            dc_sc[...] = jnp.zeros_like(dc_sc)
            dn_sc[...] = jnp.zeros_like(dn_sc)
            gng_ref[...] = jnp.zeros_like(gng_ref)

        for cc in reversed(range(per)):
            rows = slice(cc * lc, (cc + 1) * lc)
            views = [refs[at] if at == 13 else _PartOf(refs[at], cols=rows) if at == 8 else
                     _PartOf(refs[at], lead=cc) if at in (10, 11, 12) else _PartOf(refs[at], rows=rows) for at in range(20)]
            chunk(*views, gng_ref, dc_sc, dn_sc)

    def chunk(dy_ref, o_ref, z_ref, cell_ref, q_ref, k_ref, v_ref, gc_ref, gr_ref, mt_ref, cs_ref, ns_ref, ms_ref,
              ng_ref, dq_ref, dk_ref, dv_ref, dg_ref, do_ref, dz_ref, gng_ref, dc_sc, dn_sc):
        gc = gc_ref[...]
        gr = gr_ref[...]
        mtv = mt_ref[...]
        lane = _iota((lc, LANES), 1)
        rowv = _iota((lc, 1), 0)
        dg_acc = jnp.zeros((lc, LANES), F32)
        for h in range(nh):
            c_h = cs_ref[0, h]
            n_h = ns_ref[0, h, 0:1, :]
            m_prev = jnp.max(ms_ref[0, h, 0:1, :], axis=1, keepdims=True)
            t = _cell_chunk(h, nh, q_ref, k_ref, v_ref, gc, gr, m_prev, c_h, n_h, m_t=_col(mtv, h))
            sl, qh, kh, vh = t["sl"], t["qh"], t["kh"], t["vh"]
            w_intra, w_inter, smat, w_state, decay = t["w_intra"], t["w_inter"], t["smat"], t["w_state"], t["decay"]
            cell_h = cell_ref[:, sl]
            dz, do, dcell, gng = _ml_out_stage_bwd(dy_ref[:, sl], cell_h, o_ref[:, sl], z_ref[:, sl], ng_ref[:, sl])
            dz_ref[:, sl] = dz.astype(BF16)
            do_ref[:, sl] = do.astype(BF16)
            gng_ref[:, sl] += _bcast8(gng)
            eneg = jnp.exp(-t["m_t"])
            aden = jnp.abs(t["den"])
            nst = jnp.maximum(aden, eneg)
            dnum = dcell / nst
            dden = jnp.where(aden > eneg, -_rowsum(cell_h * dcell) / nst * jnp.sign(t["den"]), 0.0)
            pmat = _mm_nt(dnum, vh) + dden
            damat = pmat * w_intra
            gmat = pmat * smat
            wdn = w_inter * dnum
            wdd = w_inter * dden
            dqh = _mm(damat, kh) + _mm_nt(wdn, c_h) + wdd * n_h
            dkh = _mm_tn(damat, qh)
            dvh = _mm_tn(smat, dnum)
            dw_inter = _rowsum(dnum * t["qc"]) + dden * t["qn"]
            dcn = dc_sc[h]
            dnn = dn_sc[h, 0:1, :]
            kw = kh * w_state
            dkw = _mm_nt(vh, dcn) + dnn
            dvh = dvh + _mm(kw, dcn)
            dkh = dkh + dkw * w_state
            dgst = _rowsum(dkw * kh) * w_state
            ddecay = _colsum(_rowsum(dcn * c_h)) + _rowsum(dnn * n_h)
            db_last = _colsum(dgst) + ddecay * decay
            rs_g = _rowsum(gmat)
            cs_g = _rowsum(gmat.T)
            db = rs_g - cs_g + dw_inter * w_inter - dgst + jnp.where(rowv == lc - 1, db_last, 0.0)
            dli = cs_g + dgst
            dc_sc[h] = decay * dcn + _mm_tn(qh, wdn)
            dn_sc[h] = _bcast8(decay * dnn + _colsum(qh * wdd))
            dq_ref[:, sl] = dqh
            dk_ref[:, sl] = dkh * (dh ** -0.5)
            dv_ref[:, sl] = dvh
            dg_acc = jnp.where(lane == h, dli, jnp.where(lane == 4 + h, db, dg_acc))
        dg_ref[...] = dg_acc

    rev = lambda c: nc // per - 1 - c
    tok = pl.BlockSpec((per * lc, d), lambda c: (rev(c), 0))
    g128 = pl.BlockSpec((per * lc, LANES), lambda c: (rev(c), 0))
    return _call(
        body, comms, name="mlstm_cell_bwd", grid=(nc // per,),
        in_specs=[tok, pl.BlockSpec((per * lc, d), lambda c: (rev(c), 3)), pl.BlockSpec((per * lc, d), lambda c: (rev(c), 4)),
                  tok, tok, tok, tok, g128, pl.BlockSpec((16, per * lc), lambda c: (0, rev(c))), g128,
                  pl.BlockSpec((per, nh, dh, dh), lambda c: (rev(c), 0, 0, 0)),
                  pl.BlockSpec((per, nh, SUBLANES, dh), lambda c: (rev(c), 0, 0, 0)),
                  pl.BlockSpec((per, nh, SUBLANES, LANES), lambda c: (rev(c), 0, 0, 0)), _full((1, d))],
        out_specs=[tok, tok, tok, g128, tok, tok, _full((SUBLANES, d))],
        out_shape=[jax.ShapeDtypeStruct((s, d), F32)] * 3 + [jax.ShapeDtypeStruct((s, LANES), F32)]
        + [jax.ShapeDtypeStruct((s, d), BF16)] * 2 + [jax.ShapeDtypeStruct((SUBLANES, d), F32)],
        scratch_shapes=[pltpu.VMEM((nh, dh, dh), F32), pltpu.VMEM((nh, SUBLANES, dh), F32)],
        args=(dy_ml, u, u, cell, q, k, v, gcol, grow, mt, cs, ns, ms, ng))


def _halo_spec(d, tile, nt, col):
    per = tile // SUBLANES
    return pl.BlockSpec((SUBLANES, d), lambda i: (jnp.maximum((nt - 1 - i) * per - 1, 0), col))


def _ml_pre_bwd(dq, dk, dv, dgates, gcol, u, pre, q, k, v, conv_w, w_q, w_k, w_v, wif_t, tile, comms=None):
    s, d = dq.shape
    nt = s // tile
    nh, dh, _ = w_q.shape

    def body(dq_ref, dk_ref, dv_ref, dg_ref, gc_ref, x_ref, pre_ref, q_ref, k_ref, v_ref, cw_ref,
             wq_ref, wk_ref, wv_ref, wift_ref,
             dx_ref, gwq_ref, gwk_ref, gwv_ref, gwif_ref, gbif_ref, gcw_ref, gcb_ref, next8):
        @pl.when(pl.program_id(0) == 0)
        def _():
            next8[...] = jnp.zeros_like(next8)
            for ref in (gwq_ref, gwk_ref, gwv_ref, gwif_ref, gbif_ref, gcw_ref, gcb_ref):
                ref[...] = jnp.zeros_like(ref)

        x = x_ref[...]
        pre = pre_ref[...]
        sg = _sigmoid(pre)
        xc = pre * sg
        dgc = dg_ref[...]
        lane = _iota(dgc.shape, 1)
        utri = jnp.where(_iota((ML_CHUNK, ML_CHUNK), 0) <= _iota((ML_CHUNK, ML_CHUNK), 1), 1.0, 0.0)
        rcs = [_mm_hi(utri, dgc[c * ML_CHUNK:(c + 1) * ML_CHUNK, :]) for c in range(tile // ML_CHUNK)]
        rc = rcs[0] if len(rcs) == 1 else jnp.concatenate(rcs, axis=0)
        dgates_v = jnp.where(lane < 4, dgc, jnp.where(lane < 8, rc * (1.0 - jnp.exp(gc_ref[...])), 0.0))
        dgb = dgates_v.astype(BF16)
        gbif_ref[...] += jnp.broadcast_to(_colsum(dgates_v), gbif_ref.shape)
        ext = jnp.dot(dgb, wift_ref[...], preferred_element_type=F32)
        dqt = dq_ref[...] + ext[:, 0:d]
        dkt = dk_ref[...] + ext[:, d:2 * d]
        dvt = dv_ref[...] + ext[:, 2 * d:3 * d]
        gwif_ref[:, 0:d] += _mm_tn(dgb, q_ref[...])
        gwif_ref[:, d:2 * d] += _mm_tn(dgb, k_ref[...])
        gwif_ref[:, 2 * d:3 * d] += _mm_tn(dgb, v_ref[...])
        dxc_parts, dxv_parts = [], []
        for h in range(nh):
            sl = slice(h * dh, (h + 1) * dh)
            gwq_ref[h] += _mm_tn(xc[:, sl], dqt[:, sl])
            gwk_ref[h] += _mm_tn(xc[:, sl], dkt[:, sl])
            gwv_ref[h] += _mm_tn(x[:, sl], dvt[:, sl])
            dxc_parts.append(_mm_nt(dqt[:, sl], wq_ref[h]) + _mm_nt(dkt[:, sl], wk_ref[h]))
            dxv_parts.append(_mm_nt(dvt[:, sl], wv_ref[h]))
        dxc = jnp.concatenate(dxc_parts, axis=1)
        dxv = jnp.concatenate(dxv_parts, axis=1)
        dpre = dxc * (sg + pre * sg * (1.0 - sg))
        gcb_ref[...] += _bcast8(_colsum(dpre))
        dx_ref[...] = (dxv + _conv_bwd(dpre, x, next8[...], cw_ref, gcw_ref)).astype(BF16)
        next8[...] = dpre[0:SUBLANES, :]

    rev = lambda i: nt - 1 - i
    tok = pl.BlockSpec((tile, d), lambda i: (rev(i), 0))
    g128 = pl.BlockSpec((tile, LANES), lambda i: (rev(i), 0))
    wsh = (nh, dh, dh)
    return _call(
        body, comms, name="mlstm_proj_bwd", grid=(nt,),
        in_specs=[tok, tok, tok, g128, g128, pl.BlockSpec((tile, d), lambda i: (rev(i), 2)), tok,
                  tok, tok, tok, _full(conv_w.shape), _full(wsh), _full(wsh), _full(wsh), _full(wif_t.shape)],
        out_specs=[tok, _full(wsh), _full(wsh), _full(wsh), _full((LANES, 3 * d)), _full((SUBLANES, LANES)),
                   _full((SUBLANES, d)), _full((SUBLANES, d))],
        out_shape=[jax.ShapeDtypeStruct((s, d), BF16)] + [jax.ShapeDtypeStruct(wsh, F32)] * 3
        + [jax.ShapeDtypeStruct((LANES, 3 * d), F32), jax.ShapeDtypeStruct((SUBLANES, LANES), F32),
           jax.ShapeDtypeStruct((SUBLANES, d), F32), jax.ShapeDtypeStruct((SUBLANES, d), F32)],
        scratch_shapes=[pltpu.VMEM((SUBLANES, d), F32)],
        args=(dq, dk, dv, dgates, gcol, u, pre, q, k, v, conv_w, w_q, w_k, w_v, wif_t))


def _rg_bwd(dy_rg, u, h_rg, gates, conv_w, w_a, w_x, lam, tile, comms=None):
    s, d = dy_rg.shape
    nt = s // tile
    nh, dh, _ = w_a.shape

    def body(dy_ref, x_ref, z_ref, h_ref, hhalo_ref, xc_ref, r_ref, i_ref, a_ref, beta_ref, cw_ref, wa_ref,
             wx_ref, lam_ref,
             dx_ref, dz_ref, gwa_ref, gwx_ref, gba_ref, gbx_ref, glam_ref, gcw_ref, gcb_ref, next8, anext, dnext, dbuf):
        i = pl.program_id(0)

        @pl.when(i == 0)
        def _():
            for ref in (next8, anext, dnext, gwa_ref, gwx_ref, gba_ref, gbx_ref, glam_ref, gcw_ref, gcb_ref):
                ref[...] = jnp.zeros_like(ref)

        inner = jnp.where(i < nt - 1, 1.0, 0.0)
        xc, r, ig, a, beta = xc_ref[...], r_ref[...], i_ref[...], a_ref[...], beta_ref[...]
        sp = _softplus(-lam_ref[...])
        h = h_ref[...]
        row = _iota(h.shape, 0)
        hprev = jnp.where(row >= 1, pltpu.roll(h, 1, 0), hhalo_ref[SUBLANES - 1:SUBLANES, :] * inner)
        z = z_ref[...]
        sz = _sigmoid(z)
        dyv = dy_ref[...]
        dz_ref[...] = (dyv * h * (sz + z * sz * (1.0 - sz))).astype(BF16)
        a_up = jnp.where(row < tile - 1, pltpu.roll(a, tile - 1, 0), anext[0:1, :])
        _scan_into(a_up, dyv * z * sz, dnext[0:1, :], dbuf, True)
        delta = dbuf[...]
        anext[...] = a[0:SUBLANES, :]
        dnext[...] = delta[0:SUBLANES, :]
        dla = delta * hprev * a - delta * ig * xc * (a * a / beta)
        glam_ref[...] += _bcast8(_colsum(dla * r) * (RG_C * _sigmoid(-lam_ref[...])))
        dpa = dla * (-RG_C * sp) * r * (1.0 - r)
        dpx = delta * beta * xc * ig * (1.0 - ig)
        gba_ref[...] += _bcast8(_colsum(dpa))
        gbx_ref[...] += _bcast8(_colsum(dpx))
        parts = []
        for hh in range(nh):
            sl = slice(hh * dh, (hh + 1) * dh)
            gwa_ref[hh] += _mm_tn(xc[:, sl], dpa[:, sl])
            gwx_ref[hh] += _mm_tn(xc[:, sl], dpx[:, sl])
            parts.append(_mm_nt(dpa[:, sl], wa_ref[hh]) + _mm_nt(dpx[:, sl], wx_ref[hh]))
        dxc = delta * beta * ig + jnp.concatenate(parts, axis=1)
        gcb_ref[...] += _bcast8(_colsum(dxc))
        dx_ref[...] = _conv_bwd(dxc, x_ref[...], next8[...], cw_ref, gcw_ref).astype(BF16)
        next8[...] = dxc[0:SUBLANES, :]

    rev = lambda i: nt - 1 - i
    tok = pl.BlockSpec((tile, d), lambda i: (rev(i), 0))
    vec = _full((1, d))
    acc = _full((SUBLANES, d))
    wsh = (nh, dh, dh)
    return _call(
        body, comms, name="rglru_bwd", grid=(nt,),
        in_specs=[tok, tok, pl.BlockSpec((tile, d), lambda i: (rev(i), 1)), tok,
                  _halo_spec(d, tile, nt, 0)] + [tok] * 5 + [_full(conv_w.shape), _full(wsh), _full(wsh), vec],
        out_specs=[tok, tok, _full(wsh), _full(wsh), acc, acc, acc, acc, acc],
        out_shape=[jax.ShapeDtypeStruct((s, d), BF16)] * 2 + [jax.ShapeDtypeStruct(wsh, F32)] * 2
        + [jax.ShapeDtypeStruct((SUBLANES, d), F32)] * 5,
        scratch_shapes=[pltpu.VMEM((SUBLANES, d), F32)] * 3 + [pltpu.VMEM((tile, d), F32)],
        args=(dy_rg, u, u, h_rg, h_rg, *gates, conv_w, w_a, w_x, lam))


def _segments(d, w, n_pieces, n_slots):
    bounds = sorted({k * d for k in range(n_pieces + 1)} | {j * w for j in range(n_slots + 1)})
    return [(lo // d, lo % d, lo // w, lo % w, hi - lo) for lo, hi in zip(bounds[:-1], bounds[1:])]


def _in_bwd(pieces, x, dxo, ng, scale, w_in_g, layer, tile, comms=None, tiles=None, prev=None):
    s, d = x.shape
    nd, _, _, w = w_in_g.shape
    segs = _segments(d, w, len(pieces), nd)
    first, count = tiles or (0, s // tile)
    n_p = len(pieces)

    def body(*refs):
        p_refs = refs[:n_p]
        x_ref, dxo_ref, ng_ref, sc_ref, w_ref = refs[n_p:n_p + 5]
        dx_ref, dsc_ref, dsh_ref, gng_ref, wcat = refs[-5:]
        _join_columns(w_ref, wcat)

        @pl.when(pl.program_id(0) == 0)
        def _():
            for k, ref in enumerate((dsc_ref, dsh_ref, gng_ref)):
                ref[...] = jnp.zeros_like(ref) if prev is None else refs[n_p + 6 + k][...]

        du = jnp.concatenate([p[...] for p in p_refs], axis=1)
        dh = lax.dot_general(du, wcat[...], (((1,), (1,)), ((), ())), preferred_element_type=F32)
        xv = x_ref[...]
        g = ng_ref[...]
        rs = lax.rsqrt(jnp.mean(xv * xv, axis=1, keepdims=True) + EPS)
        xh = xv * rs
        dsh_ref[...] += _bcast8(_colsum(dh))
        dsc_ref[...] += _bcast8(_colsum(dh * xh * g))
        dhn = dh * (1.0 + sc_ref[...])
        gng_ref[...] += _bcast8(_colsum(dhn * xh))
        dxh = dhn * g
        dx_ref[...] = dxo_ref[...] + rs * (dxh - xh * jnp.mean(dxh * xh, axis=1, keepdims=True))

    tok = pl.BlockSpec((tile, d), lambda i: (i + first, 0))
    vec = _full((1, d))
    acc = _full((SUBLANES, d))
    more_specs = [] if prev is None else [pl.BlockSpec(memory_space=pl.ANY), acc, acc, acc]
    return _call(
        body, comms, name="in_proj_bwd_x", grid=(count,),
        in_specs=[tok] * n_p + [tok, tok, vec, vec, pl.BlockSpec((nd, 1, d, w), lambda i: (0, layer, 0, 0),
                                                               pipeline_mode=pl.Buffered(1))] + more_specs,
        out_specs=[tok, acc, acc, acc],
        out_shape=[jax.ShapeDtypeStruct((s, d), F32)] + [jax.ShapeDtypeStruct((SUBLANES, d), F32)] * 3,
        scratch_shapes=[pltpu.VMEM((d, nd * w), BF16)],
        args=(*pieces, x, dxo, ng, scale, w_in_g) + (() if prev is None else tuple(prev)),
        aliases={} if prev is None else {n_p + 5: 0})


def _in_bwd_w(pieces, hbf, w, slots, tile, comms=None):
    s, d = hbf.shape
    nd_all = len(pieces) * d // w
    segs = [sg for sg in _segments(d, w, len(pieces), nd_all) if sg[2] in slots]

    def body(*refs):
        p_refs = refs[:len(pieces)]
        h_ref, gw_ref = refs[len(pieces):]

        @pl.when(pl.program_id(0) == 0)
        def _():
            gw_ref[...] = jnp.zeros_like(gw_ref)

        hv = h_ref[...]
        for (kk, a, j, b, width) in segs:
            gw_ref[j - slots[0], :, b:b + width] += _mm_tn(hv, p_refs[kk][:, a:a + width])

    tok = pl.BlockSpec((tile, d), lambda i: (i, 0))
    return _call(
        body, comms, name="in_proj_bwd_w", grid=(s // tile,),
        in_specs=[tok] * len(pieces) + [tok],
        out_specs=[pl.BlockSpec((len(slots), d, w), lambda i: (0, 0, 0), pipeline_mode=pl.Buffered(1))],
        out_shape=[jax.ShapeDtypeStruct((len(slots), d, w), F32)],
        args=(*pieces, hbf))[0]


def _ada_bwd_w(cact_col, dmod, nd):
    d = cact_col.shape[0]
    w = dmod.shape[1] // nd

    def body(c_ref, m_ref, o_ref):
        o_ref[0] = c_ref[...] * m_ref[...]

    return pl.pallas_call(
        body, name="adaln_bwd_w", grid=(nd,),
        in_specs=[_full((d, 1)), pl.BlockSpec((1, w), lambda j: (0, j))],
        out_specs=pl.BlockSpec((1, d, w), lambda j: (j, 0, 0)),
        out_shape=jax.ShapeDtypeStruct((nd, d, w), F32),
        compiler_params=_params(1),
    )(cact_col, dmod)


def _exchange(arrs, gather, name):
    return _run_comms([_exchange_comm(arrs, gather)], name)[0]


def _run_comms(comms, name):
    _call(lambda: None, comms, name=name, grid=(1,), in_specs=[], out_specs=[], out_shape=[], args=())
    return [cm.results for cm in comms]


def _exchange_comm(arrs, gather):
    n = len(arrs)
    per = N_DEV - 1

    def copies(ins, outs, sems):
        send_sems, recv_sems, local_sems = sems
        x, y, c = (lax.axis_index(ax) for ax in MESH_AXES)
        me = 4 * x + 2 * y + c
        sends, recvs = [], []
        for flip in range(1, N_DEV):
            px = x ^ ((flip >> 2) & 1)
            py = y ^ ((flip >> 1) & 1)
            pc = c ^ (flip & 1)
            peer = 4 * px + 2 * py + pc
            for kk in range(n):
                src = ins[kk] if gather else ins[kk].at[peer]
                sends.append(_remote(src, outs[kk].at[me], send_sems, recv_sems, kk * per + flip - 1, (px, py, pc)))
                recvs.append(_remote(src, outs[kk].at[peer], send_sems, recv_sems, kk * per + flip - 1, (px, py, pc)))
        local = [pltpu.make_async_copy(ins[kk] if gather else ins[kk].at[me], outs[kk].at[me], local_sems.at[kk])
                 for kk in range(n)]
        return local, sends, recvs

    def start(ins, outs, sems):
        local, sends, _ = copies(ins, outs, sems)
        for cp in sends + local:
            cp.start()

    def finish(ins, outs, sems):
        local, sends, recvs = copies(ins, outs, sems)
        for cp in recvs:
            cp.wait_recv()
        for cp in sends:
            cp.wait_send()
        for cp in local:
            cp.wait()

    return _Comm(arrs, [jax.ShapeDtypeStruct((N_DEV,) + a.shape if gather else a.shape, a.dtype) for a in arrs],
                 [pltpu.SemaphoreType.DMA((n * per,)), pltpu.SemaphoreType.DMA((n * per,)), pltpu.SemaphoreType.DMA((n,))],
                 start, finish)


def _mesh_place():
    x, y, c = (lax.axis_index(ax) for ax in MESH_AXES)
    return x, y, c, (x, y, 1 - c), [(1 - x, y), (x, 1 - y), (1 - x, 1 - y)]


def _remote(src, dst, send_sems, recv_sems, sem, to):
    return pltpu.make_async_remote_copy(src_ref=src, dst_ref=dst, send_sem=send_sems.at[sem], recv_sem=recv_sems.at[sem],
                                        device_id=to, device_id_type=pl.DeviceIdType.MESH)


def _gather_two_level(arrs, name):
    n = len(arrs)
    per = N_DEV - 1

    def body(*refs):
        ins, outs = refs[:n], refs[n:2 * n]
        send_sems, recv_sems, local_sems = refs[2 * n:]
        x, y, c, sibling, chips = _mesh_place()

        def copy(kk, j, block, to, src=None):
            dst = outs[kk].at[4 * block[0] + 2 * block[1] + block[2]]
            return _remote(dst if src is None else src, dst, send_sems, recv_sems, kk * per + j, to)

        me = (x, y, c)
        local = [pltpu.make_async_copy(ins[kk], outs[kk].at[4 * x + 2 * y + c], local_sems.at[kk]) for kk in range(n)]
        first = []
        for j, chip in enumerate(chips):
            first += [copy(kk, 1 + j, me, (*chip, c), src=ins[kk]) for kk in range(n)]
        first += [copy(kk, 0, me, sibling, src=ins[kk]) for kk in range(n)]
        for cp in first + local:
            cp.start()
        passed = []
        for j, chip in enumerate(chips):
            for kk in range(n):
                copy(kk, 1 + j, (*chip, c), me).wait_recv()
                passed.append(copy(kk, 4 + j, (*chip, c), sibling))
                passed[-1].start()
        for kk in range(n):
            copy(kk, 0, sibling, me).wait_recv()
        for j, chip in enumerate(chips):
            for kk in range(n):
                copy(kk, 4 + j, (*chip, 1 - c), me).wait_recv()
        for cp in first + passed:
            cp.wait_send()
        for cp in local:
            cp.wait()

    return pl.pallas_call(
        body, name=name,
        in_specs=[pl.BlockSpec(memory_space=pl.ANY)] * n, out_specs=[pl.BlockSpec(memory_space=pl.ANY)] * n,
        out_shape=[jax.ShapeDtypeStruct((N_DEV,) + a.shape, a.dtype) for a in arrs],
        scratch_shapes=[pltpu.SemaphoreType.DMA((n * per,)), pltpu.SemaphoreType.DMA((n * per,)),
                        pltpu.SemaphoreType.DMA((n,))],
    )(*arrs)


N_CHIPS = N_DEV // 2


def _core_swap(arrs, name):
    n = len(arrs)

    def body(*refs):
        ins, outs = refs[:n], refs[n:2 * n]
        send_sems, recv_sems = refs[2 * n:]
        _, _, c, sibling, _ = _mesh_place()
        copies = [_remote(ins[kk].at[2 * q + (1 - c)], outs[kk].at[q], send_sems, recv_sems, kk * N_CHIPS + q, sibling)
                  for q in range(N_CHIPS) for kk in range(n)]
        for cp in copies:
            cp.start()
        for cp in copies:
            cp.wait_recv()
        for cp in copies:
            cp.wait_send()

    return pl.pallas_call(
        body, name=name,
        in_specs=[pl.BlockSpec(memory_space=pl.ANY)] * n, out_specs=[pl.BlockSpec(memory_space=pl.ANY)] * n,
        out_shape=[jax.ShapeDtypeStruct((N_CHIPS,) + a.shape[1:], a.dtype) for a in arrs],
        scratch_shapes=[pltpu.SemaphoreType.DMA((n * N_CHIPS,)), pltpu.SemaphoreType.DMA((n * N_CHIPS,))],
    )(*arrs)


def _pair_sum(a, other, parity, name):
    _, r, c = a.shape
    tr = _row_tile(r, c, 3)

    def body(p_ref, a_ref, o_ref, s_ref):
        s_ref[...] = (a_ref[...] + o_ref[...]).astype(BF16)

    return pl.pallas_call(
        body, name=name,
        grid_spec=pltpu.PrefetchScalarGridSpec(
            num_scalar_prefetch=1, grid=(N_CHIPS, r // tr),
            in_specs=[pl.BlockSpec((1, tr, c), lambda q, i, p: (2 * q + p[0], i, 0)),
                      pl.BlockSpec((1, tr, c), lambda q, i, p: (q, i, 0))],
            out_specs=pl.BlockSpec((1, tr, c), lambda q, i, p: (q, i, 0))),
        out_shape=jax.ShapeDtypeStruct((N_CHIPS, r, c), BF16),
        compiler_params=_params(2),
    )(parity, a, other)


def _chip_swap(arrs, name):
    n = len(arrs)
    per = N_CHIPS - 1

    def body(*refs):
        ins, outs = refs[:n], refs[n:2 * n]
        send_sems, recv_sems, local_sems = refs[2 * n:]
        x, y, c, _, chips = _mesh_place()
        mine = 2 * x + y
        sends = [_remote(ins[kk].at[2 * chip[0] + chip[1]], outs[kk].at[mine], send_sems, recv_sems, kk * per + j, (*chip, c))
                 for j, chip in enumerate(chips) for kk in range(n)]
        recvs = [_remote(ins[kk].at[mine], outs[kk].at[2 * chip[0] + chip[1]], send_sems, recv_sems, kk * per + j, (*chip, c))
                 for j, chip in enumerate(chips) for kk in range(n)]
        local = [pltpu.make_async_copy(ins[kk].at[mine], outs[kk].at[mine], local_sems.at[kk]) for kk in range(n)]
        for cp in sends + local:
            cp.start()
        for cp in recvs:
            cp.wait_recv()
        for cp in sends:
            cp.wait_send()
        for cp in local:
            cp.wait()

    return pl.pallas_call(
        body, name=name,
        in_specs=[pl.BlockSpec(memory_space=pl.ANY)] * n, out_specs=[pl.BlockSpec(memory_space=pl.ANY)] * n,
        out_shape=[jax.ShapeDtypeStruct(a.shape, a.dtype) for a in arrs],
        scratch_shapes=[pltpu.SemaphoreType.DMA((n * per,)), pltpu.SemaphoreType.DMA((n * per,)),
                        pltpu.SemaphoreType.DMA((n,))],
    )(*arrs)


def _adam_math(w, g, m, v):
    m = ADAM_B1 * m + (1.0 - ADAM_B1) * g
    v = ADAM_B2 * v + (1.0 - ADAM_B2) * (g * g)
    m_hat = m / (1.0 - ADAM_B1 ** ADAM_STEP)
    v_hat = v / (1.0 - ADAM_B2 ** ADAM_STEP)
    delta = -ADAM_LR * (m_hat / (jnp.sqrt(v_hat) + ADAM_EPS) + ADAM_WD * w)
    return delta, m, v


def _sum_devices(r_ref):
    acc = r_ref[0].astype(F32)
    for p in range(1, r_ref.shape[0]):
        acc = acc + r_ref[p].astype(F32)
    return acc


def _row_tile(rows, cols, n_bufs):
    budget = 24 * 1024 * 1024 // (n_bufs * 2 * cols * 4)
    t = rows
    while t > budget and t % 2 == 0 and (t // 2) % SUBLANES == 0:
        t //= 2
    return t


def _reduce_adam(recvs, w, m, v, name, comms=None):
    nl, r, c = w.shape
    n_part = recvs[0].shape[0]
    tr = _row_tile(r, c, n_part * nl + 7)
    nt = r // tr

    def body(*refs):
        r_refs = refs[:nl]
        w_ref, m_ref, v_ref, g_ref, d_ref, mo_ref, vo_ref = refs[nl:]
        layer = pl.program_id(0) // nt
        g = _sum_devices(r_refs[0])
        for ll in range(1, nl):
            g = jnp.where(layer == ll, _sum_devices(r_refs[ll]), g)
        delta, m2, v2 = _adam_math(w_ref[0], g, m_ref[0], v_ref[0])
        g_ref[0] = g
        d_ref[0] = delta
        mo_ref[0] = m2
        vo_ref[0] = v2

    def rspec(ll):
        return pl.BlockSpec((n_part, tr, c),
                            lambda i: (0, jnp.where(i // nt == ll, i % nt, jnp.where(i // nt < ll, 0, nt - 1)), 0))

    blk = pl.BlockSpec((1, tr, c), lambda i: (i // nt, i % nt, 0))
    return _call(
        body, comms, name=name, grid=(nl * nt,),
        in_specs=[rspec(ll) for ll in range(nl)] + [blk, blk, blk],
        out_specs=[blk] * 4,
        out_shape=[jax.ShapeDtypeStruct((nl, r, c), F32)] * 4,
        args=(*recvs, w, m, v))


def _sum8(recv, name):
    _, r, c = recv.shape

    def body(r_ref, o_ref):
        o_ref[...] = _sum_devices(r_ref)

    return pl.pallas_call(
        body, name=name, grid=(1,),
        in_specs=[_full(recv.shape)], out_specs=_full((r, c)),
        out_shape=jax.ShapeDtypeStruct((r, c), F32), compiler_params=_params(1),
    )(recv)


def _adam_call(w, g, m, v, name):
    r, c = w.shape

    def body(w_ref, g_ref, m_ref, v_ref, d_ref, mo_ref, vo_ref):
        delta, m2, v2 = _adam_math(w_ref[...], g_ref[...], m_ref[...], v_ref[...])
        d_ref[...] = delta
        mo_ref[...] = m2
        vo_ref[...] = v2

    return pl.pallas_call(
        body, name=name, grid=(1,),
        in_specs=[_full((r, c))] * 4, out_specs=[_full((r, c))] * 3,
        out_shape=[jax.ShapeDtypeStruct((r, c), F32)] * 3, compiler_params=_params(1),
    )(w, g, m, v)


def _tile_for(s, want):
    return min(want, s)


def _local_step_whole(x, c, target, wts):
    s, d = x.shape
    nl = wts["w_in_g"].shape[1]
    nd = wts["w_in_g"].shape[0]
    nh_ml = wts["w_qkv"].shape[2]
    t_big = _tile_for(s, 512)
    t_mid = _tile_for(s, 256)

    mod, cact = _mod_call(c, wts["w_ada_g"], wts["b_ada"])
    row = lambda a: a.reshape(1, -1)
    saved = []
    xl = x
    for l in range(nl):
        shift, scale, gate = (row(mod[l, kk * d:(kk + 1) * d]) for kk in range(3))
        u, hbf = _in_fwd(xl, row(wts["norm_g"][l]), scale, shift, wts["w_in_g"], l, t_mid)
        h_rg, y_rg = _rg_fwd(u, d, wts["rg_conv_w"][l], row(wts["rg_conv_b"][l]), wts["rg_w_a_bf"][l],
                             row(wts["rg_b_a"][l]), wts["rg_w_x_bf"][l], row(wts["rg_b_x"][l]),
                             row(wts["rg_lambda"][l]), t_mid)
        q, k, v, gcol = _ml_pre(u, d, wts["ml_conv_w"][l], row(wts["ml_conv_b"][l]), wts["w_qkv"][l, 0],
                                wts["w_qkv"][l, 1], wts["w_qkv"][l, 2], wts["wif_pad"][l], wts["bif_pad"][l], t_mid)
        grow = gcol[:, 0:16].T
        cell, y_ml, cs, ns, ms, mt = _ml_cell_fwd(q, k, v, gcol, grow, u, row(wts["ml_norm_g"][l]), nh_ml)
        x_new, y = _out_fwd(xl, y_rg, y_ml, gate, wts["w_out_g"], l, t_big)
        saved.append(dict(x=xl, u=u, hbf=hbf, h_rg=h_rg, y_rg=y_rg, q=q, k=k, v=v, gcol=gcol, grow=grow, cell=cell,
                          y_ml=y_ml, cs=cs, ns=ns, ms=ms, mt=mt, y=y, scale=scale, gate=gate))
        xl = x_new

    dx, loss_p, g_final = _loss_call(xl, row(wts["final_g"]), target, t_big)
    grads = [None] * nl
    cact_col = cact[0].reshape(d, 1)
    for l in reversed(range(nl)):
        sv = saved[l]
        dy_rg, dy_ml, gw_out, dgate = _out_bwd(dx, sv["gate"], sv["y"], sv["y_rg"], sv["y_ml"], wts["w_out_g"], l, t_big)
        dq, dk, dv, dgates, d_mlo, d_mlz, g_mlng = _ml_cell_bwd(
            dy_ml, sv["u"], sv["cell"], sv["q"], sv["k"], sv["v"], sv["gcol"], sv["grow"], sv["mt"], sv["cs"],
            sv["ns"], sv["ms"], row(wts["ml_norm_g"][l]), nh_ml)
        d_mlx, g_wq, g_wk, g_wv, g_wift, g_bif, g_mlcw, g_mlcb = _ml_pre_bwd(
            dq, dk, dv, dgates, sv["gcol"], sv["u"], sv["q"], sv["k"], sv["v"], wts["ml_conv_w"][l],
            row(wts["ml_conv_b"][l]), wts["w_qkv"][l, 0], wts["w_qkv"][l, 1], wts["w_qkv"][l, 2], wts["wift_pad"][l], t_mid)
        d_rgx, d_rgz, g_wa, g_wx, g_ba, g_bx, g_lam, g_rgcw, g_rgcb = _rg_bwd(
            dy_rg, sv["u"], sv["h_rg"], wts["rg_conv_w"][l], row(wts["rg_conv_b"][l]), wts["rg_w_a_bf"][l],
            row(wts["rg_b_a"][l]), wts["rg_w_x_bf"][l], row(wts["rg_b_x"][l]), row(wts["rg_lambda"][l]), t_mid)
        pieces = [d_rgx, d_rgz, d_mlx, d_mlo, d_mlz]
        dx, dscale, dshift, g_ng = _in_bwd(pieces, sv["x"], dx, row(wts["norm_g"][l]), sv["scale"], wts["w_in_g"], l, t_mid)
        half = nd // 2
        w_cols = wts["w_in_g"].shape[3]
        gw_in = jnp.concatenate([_in_bwd_w(pieces, sv["hbf"], w_cols, tuple(range(0, half)), t_big),
                                 _in_bwd_w(pieces, sv["hbf"], w_cols, tuple(range(half, nd)), t_big)], axis=0)
        dmod = jnp.concatenate([dshift[0:1], dscale[0:1], dgate[0:1]], axis=1)
        gw_ada = _ada_bwd_w(cact_col, dmod, nd)
        grads[l] = dict(w_ada=gw_ada, w_in=gw_in, w_out=gw_out, w_qkv=jnp.stack([g_wq, g_wk, g_wv]),
                        rg_conv_w=g_rgcw[0:CONV_WIDTH], ml_conv_w=g_mlcw[0:CONV_WIDTH], wif_t=g_wift[0:8],
                        norm_g=g_ng[0], b_ada=dmod[0], rg_conv_b=g_rgcb[0], rg_w_a=g_wa, rg_b_a=g_ba[0], rg_w_x=g_wx,
                        rg_b_x=g_bx[0], rg_lambda=g_lam[0], ml_conv_b=g_mlcb[0], ml_b_if=g_bif[0, 0:8],
                        ml_norm_g=g_mlng[0])
    return loss_p[0, 0], dx, grads, g_final[0]


REPLICATED = ("norm_g", "b_ada", "rg_conv_b", "rg_w_a", "rg_b_a", "rg_w_x", "rg_b_x", "rg_lambda", "ml_conv_b",
              "ml_b_if", "ml_norm_g", "final_g")
ROW_ALIGN = N_DEV * SUBLANES


def _to_rows(a):
    flat = a.reshape(-1)
    pad = (-flat.shape[0]) % LANES
    return jnp.pad(flat, (0, pad)).reshape(-1, LANES)


def _pack(arrays):
    rows = jnp.concatenate([_to_rows(a) for a in arrays], axis=0)
    return jnp.pad(rows, ((0, (-rows.shape[0]) % ROW_ALIGN), (0, 0)))


def _unpack(rows, like):
    out, at = [], 0
    for a in like:
        n = -(-a.size // LANES)
        out.append(rows[at:at + n].reshape(-1)[:a.size].reshape(a.shape))
        at += n
    return out


def _small_pack(rg_conv_w, ml_conv_w, ml_w_if):
    nl = rg_conv_w.shape[0]
    wif_t = jnp.swapaxes(ml_w_if, 1, 2).reshape(nl, -1, LANES)
    return jnp.concatenate([rg_conv_w, ml_conv_w, wif_t], axis=1)


def _small_unpack(p, if_rows):
    nl = p.shape[0]
    rg_cw = p[:, 0:CONV_WIDTH]
    ml_cw = p[:, CONV_WIDTH:2 * CONV_WIDTH]
    wif = jnp.swapaxes(p[:, 2 * CONV_WIDTH:].reshape(nl, 8, if_rows), 1, 2)
    return rg_cw, ml_cw, wif


def _assemble_weights(big, small, rep):
    w_ada_g, w_in_g, w_out_g, qkv_g = big
    nd, nl = small.shape[0], small.shape[1]
    d = w_in_g.shape[2]
    dh = qkv_g.shape[3]
    nh = d // dh
    rsh = qkv_g.shape[2] // (3 * nh)
    w_qkv = qkv_g.reshape(nd, nl, 3, nh, rsh, dh).transpose(1, 2, 3, 0, 4, 5).reshape(nl, 3, nh, nd * rsh, dh)
    cw = small[:, :, 0:2 * CONV_WIDTH].reshape(nd, nl, 2, CONV_WIDTH, LANES).transpose(1, 2, 3, 0, 4)
    cw = cw.reshape(nl, 2, CONV_WIDTH, nd * LANES)
    if_rows = (small.shape[2] - 2 * CONV_WIDTH) * LANES // 8
    wif_t = small[:, :, 2 * CONV_WIDTH:].reshape(nd, nl, 8, if_rows).transpose(1, 2, 0, 3).reshape(nl, 8, nd * if_rows)
    wift_pad = jnp.pad(wif_t, ((0, 0), (0, LANES - 8), (0, 0))).astype(BF16)
    wif_pad = jnp.swapaxes(wift_pad, 1, 2)
    bif_pad = jnp.pad(rep["ml_b_if"], ((0, 0), (0, LANES - 8))).reshape(nl, 1, LANES)
    wts = dict(rep)
    wts.update(w_ada_g=w_ada_g, w_in_g=w_in_g, w_out_g=w_out_g, w_qkv=w_qkv, rg_conv_w=cw[:, 0], ml_conv_w=cw[:, 1],
               wif_pad=wif_pad, wift_pad=wift_pad, bif_pad=bif_pad, rg_w_a_bf=rep["rg_w_a"].astype(BF16),
               rg_w_x_bf=rep["rg_w_x"].astype(BF16))
    return wts


def _qkv_slots(g_qkv, nd):
    three, nh, dh, _ = g_qkv.shape
    return g_qkv.reshape(three, nh, nd, dh // nd, dh).transpose(2, 0, 1, 3, 4).reshape(nd, three * nh * (dh // nd), dh)


def _small_slots(g):
    nd = N_DEV
    cw = jnp.stack([g["rg_conv_w"], g["ml_conv_w"]]).reshape(2, CONV_WIDTH, nd, LANES).transpose(2, 0, 1, 3)
    cw = cw.reshape(nd, 2 * CONV_WIDTH, LANES)
    wif = g["wif_t"].reshape(8, nd, -1).transpose(1, 0, 2).reshape(nd, -1, LANES)
    return jnp.concatenate([cw, wif], axis=1)


def _kernel_unhosted(x, c, norm_g, w_ada, b_ada, w_in, rg_conv_w, rg_conv_b, rg_w_a, rg_b_a, rg_w_x, rg_b_x, rg_lambda, ml_conv_w, ml_conv_b, ml_w_q, ml_w_k, ml_w_v, ml_w_if, ml_b_if, ml_norm_g, w_out, final_g, loss_target, m_norm_g, m_w_ada, m_b_ada, m_w_in, m_rg_conv_w, m_rg_conv_b, m_rg_w_a, m_rg_b_a, m_rg_w_x, m_rg_b_x, m_rg_lambda, m_ml_conv_w, m_ml_conv_b, m_ml_w_q, m_ml_w_k, m_ml_w_v, m_ml_w_if, m_ml_b_if, m_ml_norm_g, m_w_out, m_final_g, v_norm_g, v_w_ada, v_b_ada, v_w_in, v_rg_conv_w, v_rg_conv_b, v_rg_w_a, v_rg_b_a, v_rg_w_x, v_rg_b_x, v_rg_lambda, v_ml_conv_w, v_ml_conv_b, v_ml_w_q, v_ml_w_k, v_ml_w_v, v_ml_w_if, v_ml_b_if, v_ml_norm_g, v_w_out, v_final_g):
    given = dict(locals())
    nl = w_in.shape[0]
    rep = {n: given[n] for n in REPLICATED}

    def qkv_shard(prefix):
        return jnp.stack([given[prefix + "ml_w_q"], given[prefix + "ml_w_k"], given[prefix + "ml_w_v"]], axis=1).reshape(
            nl, -1, ml_w_q.shape[-1])

    *big, small = _gather_two_level(
        [w_ada.astype(BF16), w_in.astype(BF16), w_out.astype(BF16), qkv_shard("").astype(BF16),
         _small_pack(rg_conv_w, ml_conv_w, ml_w_if)], "gather_weights")
    wts = _assemble_weights(big, small, rep)

    loss_p, grad_x, grads, g_final = _local_step(x[0], c, loss_target[0], wts)
    loss = lax.psum(loss_p, MESH_AXES)

    keys = ("w_ada", "w_in", "w_out", "w_qkv", "small")
    parity = lax.axis_index("c").astype(jnp.int32).reshape(1)
    recv = []
    for l in range(nl):
        g = grads[l]
        parts = [g["w_ada"], g["w_in"], g["w_out"], _qkv_slots(g["w_qkv"], N_DEV), _small_slots(g)]
        other = _core_swap(parts, "core_swap_layer%d" % l)
        sums = [_pair_sum(a, o, parity, "pair_sum_%s_layer%d" % (key, l)) for key, a, o in zip(keys, parts, other)]
        recv.append(_chip_swap(sums, "chip_swap_layer%d" % l))
    shard = {"": dict(w_ada=w_ada, w_in=w_in, w_out=w_out, w_qkv=qkv_shard(""),
                      small=_small_pack(rg_conv_w, ml_conv_w, ml_w_if))}
    for p in ("m_", "v_"):
        shard[p] = dict(w_ada=given[p + "w_ada"], w_in=given[p + "w_in"], w_out=given[p + "w_out"], w_qkv=qkv_shard(p),
                        small=_small_pack(given[p + "rg_conv_w"], given[p + "ml_conv_w"], given[p + "ml_w_if"]))
    res = {}
    for ki, key in enumerate(keys):
        res[key] = _reduce_adam([recv[l][ki] for l in range(nl)], shard[""][key], shard["m_"][key], shard["v_"][key],
                                "reduce_adam_" + key)

    rep_g = dict(final_g=g_final)
    for n in REPLICATED[:-1]:
        rep_g[n] = jnp.stack([grads[l][n] for l in range(nl)])
    pack_g = _pack([rep_g[n] for n in REPLICATED])
    rows = pack_g.shape[0] // N_DEV
    mine = _sum8(_exchange([pack_g.reshape(N_DEV, rows, LANES)], False, "reduce_scatter_replicated")[0], "sum_replicated")
    g_rep = _exchange([mine], True, "gather_replicated")[0].reshape(N_DEV * rows, LANES)
    rep_like = [rep[n] for n in REPLICATED]
    d_rep, m_rep, v_rep = _adam_call(_pack(rep_like), g_rep, _pack([given["m_" + n] for n in REPLICATED]),
                                     _pack([given["v_" + n] for n in REPLICATED]), "adam_replicated")
    rep_out = [dict(zip(REPLICATED, _unpack(a, rep_like))) for a in (g_rep, d_rep, m_rep, v_rep)]

    if_rows = ml_w_if.shape[1]
    order = ("norm_g", "w_ada", "b_ada", "w_in", "rg_conv_w", "rg_conv_b", "rg_w_a", "rg_b_a", "rg_w_x", "rg_b_x",
             "rg_lambda", "ml_conv_w", "ml_conv_b", "ml_w_q", "ml_w_k", "ml_w_v", "ml_w_if", "ml_b_if", "ml_norm_g",
             "w_out", "final_g")
    outs = [loss, grad_x[None]]
    for kind in range(4):
        qkv = res["w_qkv"][kind].reshape((nl, 3) + ml_w_q.shape[1:])
        rg_cw, ml_cw, wif = _small_unpack(res["small"][kind], if_rows)
        sharded = dict(w_ada=res["w_ada"][kind], w_in=res["w_in"][kind], w_out=res["w_out"][kind], ml_w_q=qkv[:, 0],
                       ml_w_k=qkv[:, 1], ml_w_v=qkv[:, 2], rg_conv_w=rg_cw, ml_conv_w=ml_cw, ml_w_if=wif)
        for n in order:
            outs.append(sharded[n] if n in sharded else rep_out[kind][n])
    return tuple(outs)


def _slot(block):
    return 4 * block[0] + 2 * block[1] + block[2]


def _dma_sems(*counts):
    return [pltpu.SemaphoreType.DMA((n,)) for n in counts]


def _start_all(copies):
    for cp in copies:
        cp.start()


def _gather_ici_comm(arrs):
    n = len(arrs)

    def copies(ins, outs, sems):
        send_sems, recv_sems, local_sems = sems
        x, y, c, sibling, chips = _mesh_place()
        me = (x, y, c)
        peers = [(*chip, c) for chip in chips] + [sibling]
        local = [pltpu.make_async_copy(ins[kk], outs[kk].at[_slot(me)], local_sems.at[kk]) for kk in range(n)]
        sends = [_remote(ins[kk], outs[kk].at[_slot(me)], send_sems, recv_sems, kk * 4 + j, peer)
                 for j, peer in enumerate(peers) for kk in range(n)]
        recvs = [_remote(ins[kk], outs[kk].at[_slot(peer)], send_sems, recv_sems, kk * 4 + j, peer)
                 for j, peer in enumerate(peers) for kk in range(n)]
        return local, sends, recvs

    def start(ins, outs, sems):
        local, sends, _ = copies(ins, outs, sems)
        _start_all(sends + local)

    def finish(ins, outs, sems):
        local, sends, recvs = copies(ins, outs, sems)
        for cp in recvs:
            cp.wait_recv()
        for cp in sends:
            cp.wait_send()
        for cp in local:
            cp.wait()

    return _Comm(arrs, [jax.ShapeDtypeStruct((N_DEV,) + a.shape, a.dtype) for a in arrs], _dma_sems(4 * n, 4 * n, n),
                 start, finish)


def _gather_fwd_comm(bufs):
    n = len(bufs)

    def copies(ins, outs, sems):
        send_sems, recv_sems = sems
        _, _, c, sibling, chips = _mesh_place()
        sends = [_remote(ins[kk].at[_slot((*chip, c))], outs[kk].at[_slot((*chip, c))], send_sems, recv_sems, kk * 3 + j, sibling)
                 for j, chip in enumerate(chips) for kk in range(n)]
        recvs = [_remote(ins[kk].at[_slot((*chip, c))], outs[kk].at[_slot((*chip, 1 - c))], send_sems, recv_sems, kk * 3 + j, sibling)
                 for j, chip in enumerate(chips) for kk in range(n)]
        return sends, recvs

    def start(ins, outs, sems):
        _start_all(copies(ins, outs, sems)[0])

    def finish(ins, outs, sems):
        sends, recvs = copies(ins, outs, sems)
        for cp in recvs:
            cp.wait_recv()
        for cp in sends:
            cp.wait_send()

    return _Comm(bufs, [jax.ShapeDtypeStruct(a.shape, a.dtype) for a in bufs], _dma_sems(3 * n, 3 * n), start, finish,
                 aliases=[(i, i) for i in range(n)])


def _core_swap_comm(arrs):
    n = len(arrs)

    def copies(ins, outs, sems):
        send_sems, recv_sems = sems
        _, _, c, sibling, _ = _mesh_place()
        return [_remote(ins[kk].at[2 * q + (1 - c)], outs[kk].at[q], send_sems, recv_sems, kk * N_CHIPS + q, sibling)
                for q in range(N_CHIPS) for kk in range(n)]

    def start(ins, outs, sems):
        _start_all(copies(ins, outs, sems))

    def finish(ins, outs, sems):
        cps = copies(ins, outs, sems)
        for cp in cps:
            cp.wait_recv()
        for cp in cps:
            cp.wait_send()

    return _Comm(arrs, [jax.ShapeDtypeStruct((N_CHIPS,) + a.shape[1:], a.dtype) for a in arrs],
                 _dma_sems(N_CHIPS * n, N_CHIPS * n), start, finish)


def _chip_swap_comm(arrs):
    n = len(arrs)
    per = N_CHIPS - 1

    def copies(ins, outs, sems):
        send_sems, recv_sems, local_sems = sems
        x, y, c, _, chips = _mesh_place()
        mine = 2 * x + y
        sends = [_remote(ins[kk].at[2 * chip[0] + chip[1]], outs[kk].at[mine], send_sems, recv_sems, kk * per + j, (*chip, c))
                 for j, chip in enumerate(chips) for kk in range(n)]
        recvs = [_remote(ins[kk].at[mine], outs[kk].at[2 * chip[0] + chip[1]], send_sems, recv_sems, kk * per + j, (*chip, c))
                 for j, chip in enumerate(chips) for kk in range(n)]
        local = [pltpu.make_async_copy(ins[kk].at[mine], outs[kk].at[mine], local_sems.at[kk]) for kk in range(n)]
        return local, sends, recvs

    def start(ins, outs, sems):
        local, sends, _ = copies(ins, outs, sems)
        _start_all(sends + local)

    def finish(ins, outs, sems):
        local, sends, recvs = copies(ins, outs, sems)
        for cp in recvs:
            cp.wait_recv()
        for cp in sends:
            cp.wait_send()
        for cp in local:
            cp.wait()

    return _Comm(arrs, [jax.ShapeDtypeStruct(a.shape, a.dtype) for a in arrs], _dma_sems(per * n, per * n, n), start, finish)


def _ada_mod(c_all, w_ada, b_cols, comms=None):
    nl, d, w = w_ada.shape

    def body(c_ref, w_ref, b_ref, m_ref, ca_ref):
        sub = _iota((SUBLANES, d), 0)
        cv = jnp.zeros((SUBLANES, d), F32)
        for b in range(N_DEV):
            cv = jnp.where(sub == b, c_ref[b], cv)
        ca = cv * _sigmoid(cv)
        ca_ref[...] = ca
        m_ref[...] = jnp.zeros_like(m_ref)
        for l in range(nl):
            ml = _mm_hi(ca, w_ref[l]) + b_ref[l:l + 1, :]
            for b in range(N_DEV):
                m_ref[b, l:l + 1, :] = _row(ml, b)

    return _call(
        body, comms, name="adaln_mod_columns", grid=(1,),
        in_specs=[_full(c_all.shape), _full(w_ada.shape), _full(b_cols.shape)],
        out_specs=[_full((N_DEV, SUBLANES, w)), _full((SUBLANES, d))],
        out_shape=[jax.ShapeDtypeStruct((N_DEV, SUBLANES, w), F32), jax.ShapeDtypeStruct((SUBLANES, d), F32)],
        args=(c_all, w_ada, b_cols))


def _ada_grad_adam(cact_t, dmods, w, m, v, comms=None):
    nl, d, wd = w.shape
    tr = _row_tile(d, wd, 8)
    nt = d // tr

    def body(c_ref, dm_ref, w_ref, m_ref, v_ref, g_ref, d_ref, mo_ref, vo_ref):
        cv = c_ref[...]
        dm = dm_ref[0]
        g = _col(cv, 0) * _row(dm, 0)
        for b in range(1, N_DEV):
            g = g + _col(cv, b) * _row(dm, b)
        delta, m2, v2 = _adam_math(w_ref[0], g, m_ref[0], v_ref[0])
        g_ref[0] = g
        d_ref[0] = delta
        mo_ref[0] = m2
        vo_ref[0] = v2

    blk = pl.BlockSpec((1, tr, wd), lambda i: (i // nt, i % nt, 0))
    return _call(
        body, comms, name="adaln_grad_adam", grid=(nl * nt,),
        in_specs=[pl.BlockSpec((tr, N_DEV), lambda i: (i % nt, 0)), pl.BlockSpec((1, N_DEV, wd), lambda i: (i // nt, 0, 0)),
                  blk, blk, blk],
        out_specs=[blk] * 4, out_shape=[jax.ShapeDtypeStruct((nl, d, wd), F32)] * 4,
        args=(cact_t, dmods, w, m, v))


REP_ROWS = ("norm_g", "dshift", "dscale", "dgate", "rg_conv_b", "rg_b_a", "rg_b_x", "rg_lambda", "ml_conv_b", "ml_norm_g",
            "ml_b_if")


def _sum_parts(recvs, name):
    def body(*refs):
        for r_ref, o_ref in zip(refs[:len(recvs)], refs[len(recvs):]):
            o_ref[...] = _sum_devices(r_ref).astype(o_ref.dtype)

    return pl.pallas_call(
        body, name=name, grid=(1,),
        in_specs=[_full(r.shape) for r in recvs], out_specs=[_full(r.shape[1:]) for r in recvs],
        out_shape=[jax.ShapeDtypeStruct(r.shape[1:], r.dtype) for r in recvs], compiler_params=_params(1),
    )(*recvs)


def _adam_replicated(vp, mp, params, nl):
    d = vp.shape[1]
    nr = len(REP_ROWS)
    names = list(params)
    mat_shape = params["rg_w_a"][0].shape[1:]
    mat_rows = mp.shape[0] // (2 * nl)

    def pieces(name):
        if name == "final_g":
            return [(lambda vp_ref, mp_ref: vp_ref[nl * nr:nl * nr + 1, :], (slice(0, 1), slice(None)))]
        out = []
        for l in range(nl):
            if name in ("rg_w_a", "rg_w_x"):
                at = (2 * l + (name == "rg_w_x")) * mat_rows
                out.append((lambda vp_ref, mp_ref, at=at: mp_ref[at:at + mat_rows, :].astype(F32).reshape(mat_shape), l))
            elif name == "b_ada":
                for j in range(3):
                    r = l * nr + 1 + j
                    out.append((lambda vp_ref, mp_ref, r=r: vp_ref[r:r + 1, :], (slice(l, l + 1), slice(j * d, (j + 1) * d))))
            else:
                r = l * nr + REP_ROWS.index(name)
                cols = slice(0, LANES) if name == "ml_b_if" else slice(None)
                out.append((lambda vp_ref, mp_ref, r=r, cols=cols: vp_ref[r:r + 1, cols], (slice(l, l + 1), slice(None))))
        return out

    def body(*refs):
        vp_ref, mp_ref = refs[:2]
        ins, outs = refs[2:2 + 3 * len(names)], refs[2 + 3 * len(names):]
        for pi, name in enumerate(names):
            w_ref, m_ref, v_ref = ins[3 * pi:3 * pi + 3]
            g_ref, d_ref, mo_ref, vo_ref = outs[4 * pi:4 * pi + 4]
            for get, idx in pieces(name):
                g = get(vp_ref, mp_ref)
                delta, m2, v2 = _adam_math(w_ref[idx], g, m_ref[idx], v_ref[idx])
                g_ref[idx] = g
                d_ref[idx] = delta
                mo_ref[idx] = m2
                vo_ref[idx] = v2

    flat = [a for name in names for a in params[name]]
    out_shape = [jax.ShapeDtypeStruct(params[name][0].shape, F32) for name in names for _ in range(4)]
    res = pl.pallas_call(
        body, name="adam_replicated", grid=(1,),
        in_specs=[_full(vp.shape), _full(mp.shape)] + [_full(a.shape) for a in flat],
        out_specs=[_full(o.shape) for o in out_shape], out_shape=out_shape, compiler_params=_params(1),
    )(vp, mp, *flat)
    return {name: res[4 * pi:4 * pi + 4] for pi, name in enumerate(names)}


class _Plan:
    def __init__(self):
        self.hosted, self.after = {}, {}

    def host(self, key, comm, then=None):
        self.hosted.setdefault(key, []).append(comm)
        if then is not None:
            self.after.setdefault(key, []).append(then)

    def comms(self, key):
        return self.hosted.pop(key, None)

    def done(self, key):
        for fn in self.after.pop(key, []):
            fn()

    def flush(self):
        while self.hosted:
            key = next(iter(self.hosted))
            _call(lambda: None, self.comms(key), name="exchange_after_%s_%d" % key, grid=(1,), in_specs=[], out_specs=[],
                  out_shape=[], args=())
            self.done(key)


VEC_TABLE = ("norm_g", "rg_conv_b", "rg_b_a", "rg_b_x", "rg_lambda", "ml_conv_b", "ml_norm_g")


def _vec_table(rep):
    rows = [rep[n] for n in VEC_TABLE]
    return jnp.stack(rows + [jnp.zeros_like(rows[0])] * (SUBLANES - len(rows)), axis=1)


def _layer_fwd(l, xl, mod3, wl, rep, plan):
    s, d = xl.shape
    t_big, t_mid = _tile_for(s, 512), _tile_for(s, 256)
    nh_ml = rep["ml_b_if"].shape[1] // 2
    vec = lambda name: _vec(rep["vecs"], l, VEC_TABLE.index(name))
    shift, scale, gate = (_vec(mod3, l, kk) for kk in range(3))
    hosted = lambda name: plan.comms((name, l)) if plan else None
    done = lambda name: plan.done((name, l)) if plan else None
    if "w_in_shard" in wl:
        u, hbf, land = _in_fwd_gathering(xl, vec("norm_g"), scale, shift, wl["w_in_shard"], wl["order"], hosted("in_proj_fwd"))
        wl["w_in_g"] = land.reshape(land.shape[0], 1, *land.shape[1:])
    else:
        u, hbf = _in_fwd(xl, vec("norm_g"), scale, shift, wl["w_in_g"], 0, t_mid, hosted("in_proj_fwd"))
    done("in_proj_fwd")
    h_rg, y_rg, *rg_gates = _rg_fwd(u, d, wl["rg_conv_w"], vec("rg_conv_b"), rep["rg_w_a_bf"][l], vec("rg_b_a"),
                                    rep["rg_w_x_bf"][l], vec("rg_b_x"), vec("rg_lambda"), t_mid, hosted("rglru_fwd"))
    done("rglru_fwd")
    q, k, v, gcol, pre = _ml_pre(u, d, wl["ml_conv_w"], vec("ml_conv_b"), wl["w_qkv"][0], wl["w_qkv"][1],
                                 wl["w_qkv"][2], wl["wif_pad"], wl["bif_pad"], t_mid, hosted("mlstm_proj_fwd"))
    done("mlstm_proj_fwd")
    grow = gcol[:, 0:16].T
    cell, y_ml, cs, ns, ms, mt = _ml_cell_fwd(q, k, v, gcol, grow, u, vec("ml_norm_g"), nh_ml, hosted("mlstm_cell_fwd"))
    done("mlstm_cell_fwd")
    x_new, y = _out_fwd(xl, y_rg, y_ml, gate, wl["w_out_g"], 0, t_big, hosted("out_proj_fwd"))
    done("out_proj_fwd")
    saved = dict(x=xl, u=u, hbf=hbf, h_rg=h_rg, y_rg=y_rg, q=q, k=k, v=v, gcol=gcol, grow=grow, cell=cell, y_ml=y_ml,
                 cs=cs, ns=ns, ms=ms, mt=mt, y=y, scale=scale, gate=gate, rg_gates=rg_gates, pre=pre)
    return x_new, saved


def _layer_bwd(l, dx, sv, wl, rep, plan, grads=None, split_last=False):
    s, d = dx.shape
    t_big, t_mid = _tile_for(s, 512), _tile_for(s, 256)
    nh_ml = rep["ml_b_if"].shape[1] // 2
    nd, _, _, w_cols = wl["w_in_g"].shape
    grads = {} if grads is None else grads
    vec = lambda name: _vec(rep["vecs"], l, VEC_TABLE.index(name))
    hosted = lambda name: plan.comms((name, l)) if plan else None
    done = lambda name: plan.done((name, l)) if plan else None
    dy_rg, dy_ml, gw_out, dgate = _out_bwd(dx, sv["gate"], sv["y"], sv["y_rg"], sv["y_ml"], wl["w_out_g"], 0, t_big,
                                           hosted("out_proj_bwd"))
    grads.update(w_out=gw_out)
    done("out_proj_bwd")
    dq, dk, dv, dgates, d_mlo, d_mlz, g_mlng = _ml_cell_bwd(
        dy_ml, sv["u"], sv["cell"], sv["q"], sv["k"], sv["v"], sv["gcol"], sv["grow"], sv["mt"], sv["cs"], sv["ns"],
        sv["ms"], vec("ml_norm_g"), nh_ml, hosted("mlstm_cell_bwd"))
    done("mlstm_cell_bwd")
    d_mlx, g_wq, g_wk, g_wv, g_wift, g_bif, g_mlcw, g_mlcb = _ml_pre_bwd(
        dq, dk, dv, dgates, sv["gcol"], sv["u"], sv["pre"], sv["q"], sv["k"], sv["v"], wl["ml_conv_w"],
        wl["w_qkv"][0], wl["w_qkv"][1], wl["w_qkv"][2], wl["wift_pad"], t_mid, hosted("mlstm_proj_bwd"))
    done("mlstm_proj_bwd")
    d_rgx, d_rgz, g_wa, g_wx, g_ba, g_bx, g_lam, g_rgcw, g_rgcb = _rg_bwd(
        dy_rg, sv["u"], sv["h_rg"], sv["rg_gates"], wl["rg_conv_w"], rep["rg_w_a_bf"][l], rep["rg_w_x_bf"][l],
        vec("rg_lambda"), t_mid, hosted("rglru_bwd"))
    grads.update(w_qkv=jnp.stack([g_wq, g_wk, g_wv]), rg_conv_w=g_rgcw[0:CONV_WIDTH], ml_conv_w=g_mlcw[0:CONV_WIDTH],
                 wif_t=g_wift[0:8], rg_w_a=g_wa, rg_w_x=g_wx)
    acc = dict(dgate=dgate, rg_conv_b=g_rgcb, rg_b_a=g_ba, rg_b_x=g_bx, rg_lambda=g_lam, ml_conv_b=g_mlcb,
               ml_b_if=g_bif, ml_norm_g=g_mlng)
    done("rglru_bwd")
    pieces = [d_rgx, d_rgz, d_mlx, d_mlo, d_mlz]
    grads.update(w_in=_in_bwd_w(pieces, sv["hbf"], w_cols, tuple(range(nd)), t_big, hosted("in_proj_bwd_w")))
    done("in_proj_bwd_w")
    n_tiles = s // t_mid
    counts = [n_tiles // 4, n_tiles - n_tiles // 4 - 1, 1] if split_last and n_tiles >= 4 else [n_tiles]
    in_args = (pieces, sv["x"], dx, vec("norm_g"), sv["scale"], wl["w_in_g"], 0, t_mid)
    res, at = None, 0
    for key, count in zip(("in_proj_bwd_x", "in_proj_bwd_x_rest", "in_proj_bwd_x_end"), counts):
        res = _in_bwd(*in_args, hosted(key), (at, count), res)
        done(key)
        at += count
    dx, dscale, dshift, g_ng = res
    acc.update(norm_g=g_ng, dshift=dshift, dscale=dscale)
    grads.update(acc=acc, dmod=jnp.concatenate([dshift[0:1], dscale[0:1], dgate[0:1]], axis=1))
    return dx, grads


def _local_step(x, c, target, wts):
    d = x.shape[1]
    nl = wts["w_in_g"].shape[1]
    mod, cact = _mod_call(c, wts["w_ada_g"], wts["b_ada"])
    wl = [dict(w_in_g=wts["w_in_g"][:, l:l + 1], w_out_g=wts["w_out_g"][:, l:l + 1], w_qkv=wts["w_qkv"][l],
               rg_conv_w=wts["rg_conv_w"][l], ml_conv_w=wts["ml_conv_w"][l], wif_pad=wts["wif_pad"][l],
               wift_pad=wts["wift_pad"][l], bif_pad=wts["bif_pad"][l]) for l in range(nl)]
    rep = dict(wts, vecs=_vec_table(wts))
    mod3 = mod.reshape(nl, 3, d)
    saved, xl = [], x
    for l in range(nl):
        xl, sv = _layer_fwd(l, xl, mod3, wl[l], rep, None)
        saved.append(sv)
    dx, loss_p, g_final = _loss_call(xl, wts["final_g"].reshape(1, -1), target, _tile_for(x.shape[0], 512))
    grads = [None] * nl
    for l in reversed(range(nl)):
        dx, grads[l] = _layer_bwd(l, dx, saved[l], wl[l], rep, None)
        grads[l]["w_ada"] = _ada_bwd_w(cact[0].reshape(d, 1), grads[l]["dmod"], wts["w_ada_g"].shape[0])
        grads[l]["b_ada"] = grads[l]["dmod"][0]
        grads[l].update({n: a[0] for n, a in grads[l]["acc"].items()})
        grads[l]["ml_b_if"] = grads[l]["ml_b_if"][0:8]
    return loss_p[0, 0], dx, grads, g_final[0]


def _full_qkv(qkv_g, d):
    nd, _, rows3, dh = qkv_g.shape
    nh = d // dh
    rsh = rows3 // (3 * nh)
    return qkv_g.reshape(nd, 3, nh, rsh, dh).transpose(1, 2, 0, 3, 4).reshape(3, nh, nd * rsh, dh)


def _small_weights(small, l, ml_b_if):
    nd = small.shape[0]
    sm = small[:, l]
    cw = sm[:, 0:2 * CONV_WIDTH].reshape(nd, 2, CONV_WIDTH, LANES).transpose(1, 2, 0, 3).reshape(2, CONV_WIDTH, nd * LANES)
    if_rows = (sm.shape[1] - 2 * CONV_WIDTH) * LANES // 8
    wif_t = sm[:, 2 * CONV_WIDTH:].reshape(nd, 8, if_rows).transpose(1, 0, 2).reshape(8, nd * if_rows)
    wift_pad = jnp.pad(wif_t, ((0, LANES - 8), (0, 0))).astype(BF16)
    return dict(rg_conv_w=cw[0], ml_conv_w=cw[1], wift_pad=wift_pad, wif_pad=wift_pad.T,
                bif_pad=jnp.pad(ml_b_if[l], (0, LANES - 8)).reshape(1, LANES))


def kernel(x, c, norm_g, w_ada, b_ada, w_in, rg_conv_w, rg_conv_b, rg_w_a, rg_b_a, rg_w_x, rg_b_x, rg_lambda, ml_conv_w, ml_conv_b, ml_w_q, ml_w_k, ml_w_v, ml_w_if, ml_b_if, ml_norm_g, w_out, final_g, loss_target, m_norm_g, m_w_ada, m_b_ada, m_w_in, m_rg_conv_w, m_rg_conv_b, m_rg_w_a, m_rg_b_a, m_rg_w_x, m_rg_b_x, m_rg_lambda, m_ml_conv_w, m_ml_conv_b, m_ml_w_q, m_ml_w_k, m_ml_w_v, m_ml_w_if, m_ml_b_if, m_ml_norm_g, m_w_out, m_final_g, v_norm_g, v_w_ada, v_b_ada, v_w_in, v_rg_conv_w, v_rg_conv_b, v_rg_w_a, v_rg_b_a, v_rg_w_x, v_rg_b_x, v_rg_lambda, v_ml_conv_w, v_ml_conv_b, v_ml_w_q, v_ml_w_k, v_ml_w_v, v_ml_w_if, v_ml_b_if, v_ml_norm_g, v_w_out, v_final_g):
    given = dict(locals())
    nl = w_in.shape[0]
    d = x.shape[2]
    rep = {n: given[n] for n in REPLICATED}
    rep.update(rg_w_a_bf=rg_w_a.astype(BF16), rg_w_x_bf=rg_w_x.astype(BF16))
    bf = lambda a: a.astype(BF16)

    def qkv_shard(prefix):
        return jnp.stack([given[prefix + "ml_w_q"], given[prefix + "ml_w_k"], given[prefix + "ml_w_v"]], axis=1).reshape(
            nl, -1, ml_w_q.shape[-1])

    def small_shard(prefix):
        return _small_pack(given[prefix + "rg_conv_w"], given[prefix + "ml_conv_w"], given[prefix + "ml_w_if"])

    plan = _Plan()
    qkv = qkv_shard("")
    first_ici = _gather_ici_comm([bf(w_in[0:1]), small_shard("")])
    condition = _exchange_comm([jnp.broadcast_to(c, (SUBLANES, d))], True)
    _run_comms([first_ici, condition], "gather_first")
    first_fwd = _gather_fwd_comm(first_ici.results)
    wcols = w_ada.shape[2]
    me = 4 * lax.axis_index("x") + 2 * lax.axis_index("y") + lax.axis_index("c")
    b_cols = jnp.pad(lax.dynamic_slice_in_dim(b_ada, me * wcols, wcols, axis=1), ((0, SUBLANES - nl), (0, 0)))
    mod_cols, cact_all = _ada_mod(condition.results[0], w_ada, b_cols, [first_fwd])
    w_in_first, small = first_fwd.results
    wl = [_small_weights(small, l, ml_b_if) for l in range(nl)]
    wl[0]["w_in_g"] = w_in_first

    def gather_behind(arrs, ici_host, fwd_host, then):
        ici = _gather_ici_comm(arrs)

        def pass_on():
            fwd = _gather_fwd_comm(ici.results)
            plan.host(fwd_host, fwd, lambda: then(fwd.results))

        plan.host(ici_host, ici, pass_on)

    def got_out(l):
        return lambda r: wl[l].update(w_out_g=r[0], w_qkv=_full_qkv(r[1], d))

    gather_behind([bf(w_out[0:1]), bf(qkv[0:1])], ("in_proj_fwd", 0), ("rglru_fwd", 0), got_out(0))
    for l in range(1, nl):
        gather_behind([bf(w_in[l:l + 1])], ("rglru_fwd", l - 1), ("mlstm_proj_fwd", l - 1),
                      lambda r, l=l: wl[l].update(w_in_g=r[0]))
        gather_behind([bf(w_out[l:l + 1]), bf(qkv[l:l + 1])], ("mlstm_cell_fwd", l - 1), ("out_proj_fwd", l - 1), got_out(l))

    mod_blocks = _exchange([mod_cols], False, "scatter_modulation")[0]
    mod3 = mod_blocks[:, 0:nl].transpose(1, 0, 2).reshape(nl, 3, d)
    rep["vecs"] = _vec_table(rep)
    saved, xl = [], x[0]
    for l in range(nl):
        xl, sv = _layer_fwd(l, xl, mod3, wl[l], rep, plan)
        saved.append(sv)
    grad_x, loss_p, g_final = _loss_call(xl, final_g.reshape(1, -1), loss_target[0], _tile_for(xl.shape[0], 512))

    keys = ("w_in", "w_out", "w_qkv", "small")
    parity = lax.axis_index("c").astype(jnp.int32).reshape(1)
    grads, recv = [None] * nl, [None] * nl

    def parts_of(g):
        return [g["w_in"], g["w_out"], _qkv_slots(g["w_qkv"], N_DEV), _small_slots(g)]

    def pair_sums(l, parts, other):
        return [_pair_sum(a, o, parity, "pair_sum_%s_layer%d" % (key, l)) for key, a, o in zip(keys, parts, other)]

    def reduce_behind(l, host_layer):
        parts = parts_of(grads[l])
        swap = _core_swap_comm(parts)

        def summed():
            sums = pair_sums(l, parts, swap.results)
            big = _chip_swap_comm([sums[0]])
            rest = _chip_swap_comm(sums[1:])
            plan.host(("mlstm_cell_bwd", host_layer), big)
            plan.host(("rglru_bwd", host_layer), rest, lambda: recv.__setitem__(l, big.results + rest.results))

        plan.host(("out_proj_bwd", host_layer), swap, summed)

    first, own = {}, {}

    def reduce_own(names, parts_fn, ready_key, swap_key, chip_key):
        def go():
            parts = parts_fn()
            swap = _core_swap_comm(parts)

            def summed():
                sums = [_pair_sum(a, o, parity, "pair_sum_%s_layer0" % n) for n, a, o in zip(names, parts, swap.results)]
                chip = _chip_swap_comm(sums)
                plan.host(chip_key, chip, lambda: own.update(zip(names, chip.results)))

            plan.host(swap_key, swap, summed)

        plan.after.setdefault(ready_key, []).append(go)

    reduce_own(["w_out"], lambda: [first["w_out"]], ("out_proj_bwd", 0), ("mlstm_cell_bwd", 0), ("mlstm_proj_bwd", 0))
    reduce_own(["w_qkv", "small"], lambda: [_qkv_slots(first["w_qkv"], N_DEV), _small_slots(first)],
               ("rglru_bwd", 0), ("in_proj_bwd_w", 0), ("in_proj_bwd_x", 0))
    reduce_own(["w_in"], lambda: [first["w_in"]], ("in_proj_bwd_w", 0), ("in_proj_bwd_x", 0), ("in_proj_bwd_x_rest", 0))

    matrices = {}

    def reduce_matrices():
        layers = [grads[l] if l > 0 else first for l in range(nl)]
        mp = jnp.stack([jnp.stack([g["rg_w_a"], g["rg_w_x"]]) for g in layers]).reshape(N_DEV, -1, LANES).astype(BF16)
        scatter = _exchange_comm([mp], False)

        def summed():
            gather = _exchange_comm(_sum_parts(scatter.results, "sum_replicated_matrices"), True)
            plan.host(("in_proj_bwd_x", 0), gather, lambda: matrices.update(mp=gather.results[0].reshape(-1, LANES)))

        plan.host(("in_proj_bwd_w", 0), scatter, summed)

    plan.after.setdefault(("rglru_bwd", 0), []).append(reduce_matrices)

    for l in reversed(range(nl)):
        if l > 0:
            grad_x, grads[l] = _layer_bwd(l, grad_x, saved[l], wl[l], rep, plan)
            reduce_behind(l, l - 1)
        else:
            grad_x, grads[l] = _layer_bwd(l, grad_x, saved[l], wl[l], rep, plan, first, True)
    plan.flush()
    recv[0] = [own[key] for key in keys]

    shard = {p: dict(w_in=given[p + "w_in"], w_out=given[p + "w_out"], w_qkv=qkv_shard(p), small=small_shard(p))
             for p in ("", "m_", "v_")}
    res = {}
    for ki, key in enumerate(keys):
        res[key] = _reduce_adam([recv[l][ki] for l in range(nl)], shard[""][key], shard["m_"][key], shard["v_"][key],
                                "reduce_adam_" + key)

    dmods = jnp.concatenate([grads[l]["dmod"] for l in range(nl)], axis=0)
    dmod_blocks = jnp.pad(dmods.reshape(nl, N_DEV, wcols).transpose(1, 0, 2), ((0, 0), (0, SUBLANES - nl), (0, 0)))
    dmod_all = _exchange([dmod_blocks], False, "scatter_dmod")[0][:, 0:nl].transpose(1, 0, 2)
    res["w_ada"] = _ada_grad_adam(cact_all.T, dmod_all, w_ada, m_w_ada, v_w_ada)

    widen = lambda a: jnp.pad(a, ((0, 0), (0, d - a.shape[1])))
    rows = [widen(grads[l]["acc"][n][0:1]) for l in range(nl) for n in REP_ROWS] + [g_final[0:1], widen(loss_p[0:1])]
    vp = jnp.concatenate(rows + [jnp.zeros(((-len(rows)) % ROW_ALIGN, d), F32)], axis=0)
    got = _exchange([vp.reshape(N_DEV, -1, d)], False, "reduce_scatter_replicated")
    vp_r = _exchange(_sum_parts(got, "sum_replicated"), True, "gather_replicated")[0].reshape(-1, d)
    mp_r = matrices["mp"]
    lanes = lambda a: jnp.pad(a, ((0, 0), (0, LANES - a.shape[1])))
    shaped = dict(ml_b_if=lanes, final_g=lambda a: a.reshape(1, d))
    names = [n for n in REPLICATED if n != "b_ada"] + ["b_ada"]
    rep_res = _adam_replicated(vp_r, mp_r, {n: tuple(shaped.get(n, lambda a: a)(given[p + n]) for p in ("", "m_", "v_"))
                                            for n in names}, nl)
    unshaped = dict(ml_b_if=lambda a: a[:, 0:ml_b_if.shape[1]], final_g=lambda a: a.reshape(d))
    rep_out = [{n: unshaped.get(n, lambda a: a)(rep_res[n][kind]) for n in names} for kind in range(4)]
    loss = vp_r[nl * len(REP_ROWS) + 1, 0]

    if_rows = ml_w_if.shape[1]
    order = ("norm_g", "w_ada", "b_ada", "w_in", "rg_conv_w", "rg_conv_b", "rg_w_a", "rg_b_a", "rg_w_x", "rg_b_x",
             "rg_lambda", "ml_conv_w", "ml_conv_b", "ml_w_q", "ml_w_k", "ml_w_v", "ml_w_if", "ml_b_if", "ml_norm_g",
             "w_out", "final_g")
    outs = [loss, grad_x[None]]
    for kind in range(4):
        qkv_k = res["w_qkv"][kind].reshape((nl, 3) + ml_w_q.shape[1:])
        rg_cw, ml_cw, wif = _small_unpack(res["small"][kind], if_rows)
        sharded = dict(w_ada=res["w_ada"][kind], w_in=res["w_in"][kind], w_out=res["w_out"][kind], ml_w_q=qkv_k[:, 0],
                       ml_w_k=qkv_k[:, 1], ml_w_v=qkv_k[:, 2], rg_conv_w=rg_cw, ml_conv_w=ml_cw, ml_w_if=wif)
        for n in order:
            outs.append(sharded[n] if n in sharded else rep_out[kind][n])
    return tuple(outs)
```

```python
import functools

import jax
import jax.numpy as jnp
from jax import lax
from jax.experimental import pallas as pl
from jax.experimental.pallas import tpu as pltpu

F32 = jnp.float32
BF16 = jnp.bfloat16
MESH_AXES = ("x", "y", "c")
N_DEV = 8
EPS = 1e-6
RG_C = 8.0
ML_CHUNK = 128
CONV_WIDTH = 4
ADAM_LR = 0.001
ADAM_B1 = 0.9
ADAM_B2 = 0.999
ADAM_EPS = 1e-08
ADAM_WD = 0.01
ADAM_STEP = 10
NEG_BIG = -1e30
LANES = 128
SUBLANES = 8
VMEM_LIMIT = 56 * 1024 * 1024
HI = lax.Precision.HIGHEST


def _params(n_grid):
    return pltpu.CompilerParams(dimension_semantics=("arbitrary",) * n_grid, vmem_limit_bytes=VMEM_LIMIT)


def _mm(a, b):
    return jnp.dot(a.astype(BF16), b.astype(BF16), preferred_element_type=F32)


def _mm_nt(a, b):
    return lax.dot_general(a.astype(BF16), b.astype(BF16), (((1,), (1,)), ((), ())), preferred_element_type=F32)


def _mm_tn(a, b):
    return lax.dot_general(a.astype(BF16), b.astype(BF16), (((0,), (0,)), ((), ())), preferred_element_type=F32)


def _mm_hi(a, b):
    return jnp.dot(a, b, precision=HI, preferred_element_type=F32)


def _sigmoid(x):
    return 1.0 / (1.0 + jnp.exp(-x))


def _softplus(x):
    return jnp.maximum(x, 0.0) + jnp.log(1.0 + jnp.exp(-jnp.abs(x)))


def _neg_expm1(x):
    poly = -x * (1.0 + x * (0.5 + x * (1.0 / 6.0 + x * (1.0 / 24.0 + x * (1.0 / 120.0)))))
    return jnp.where(jnp.abs(x) < 0.05, poly, 1.0 - jnp.exp(x))


def _iota(shape, dim):
    return lax.broadcasted_iota(jnp.int32, shape, dim)


def _colsum(x):
    return jnp.sum(x, axis=0, keepdims=True)


def _rowsum(x):
    return jnp.sum(x, axis=1, keepdims=True)


def _col(x, j):
    return _rowsum(jnp.where(_iota(x.shape, 1) == j, x, 0.0))


def _row(x, j):
    return _colsum(jnp.where(_iota(x.shape, 0) == j, x, 0.0))


def _shift_down(x, j, prev8):
    if j == 0:
        return x
    t = x.shape[0]
    main = jnp.where(_iota(x.shape, 0) >= j, pltpu.roll(x, j, 0), 0.0)
    fix = jnp.where(_iota(prev8.shape, 0) < j, pltpu.roll(prev8, j, 0), 0.0)
    return jnp.concatenate([main[0:SUBLANES] + fix, main[SUBLANES:t]], axis=0)


def _shift_up(x, j, next8):
    if j == 0:
        return x
    t = x.shape[0]
    main = jnp.where(_iota(x.shape, 0) < t - j, pltpu.roll(x, t - j, 0), 0.0)
    fix = jnp.where(_iota(next8.shape, 0) >= SUBLANES - j, pltpu.roll(next8, SUBLANES - j, 0), 0.0)
    return jnp.concatenate([main[0:t - SUBLANES], main[t - SUBLANES:t] + fix], axis=0)


def _conv(x, prev8, w_ref):
    y = w_ref[CONV_WIDTH - 1:CONV_WIDTH, :] * x
    for j in range(1, CONV_WIDTH):
        y = y + w_ref[CONV_WIDTH - 1 - j:CONV_WIDTH - j, :] * _shift_down(x, j, prev8)
    return y


def _conv_bwd(dy, x, next8, w_ref, gw_ref):
    dx = None
    for j in range(CONV_WIDTH):
        k = CONV_WIDTH - 1 - j
        up = _shift_up(dy, j, next8)
        gw_ref[k:k + 1, :] += _colsum(up * x)
        term = w_ref[k:k + 1, :] * up
        dx = term if dx is None else dx + term
    return dx


def _scan_into(a, b, carry, out_ref, reverse):
    t, c = a.shape
    groups = t // SUBLANES
    a3 = a.reshape(groups, SUBLANES, c)
    b3 = b.reshape(groups, SUBLANES, c)
    sub = _iota(a3.shape, 1)
    for step in (1, 2, 4):
        keep = sub < SUBLANES - step if reverse else sub >= step
        shift = SUBLANES - step if reverse else step
        a_s = jnp.where(keep, pltpu.roll(a3, shift, 1), 1.0)
        b_s = jnp.where(keep, pltpu.roll(b3, shift, 1), 0.0)
        b3 = a3 * b_s + b3
        a3 = a3 * a_s
    for g in (reversed(range(groups)) if reverse else range(groups)):
        rows = slice(g * SUBLANES, (g + 1) * SUBLANES)
        out_ref[rows, :] = b3[g] + a3[g] * carry
        edge = g * SUBLANES if reverse else (g + 1) * SUBLANES - 1
        carry = out_ref[edge:edge + 1, :]


def _blockdiag(x, w_ref, transpose_w=False):
    nh, dh, _ = w_ref.shape
    outs = []
    for h in range(nh):
        xs = x[:, h * dh:(h + 1) * dh]
        outs.append(_mm_nt(xs, w_ref[h]) if transpose_w else _mm(xs, w_ref[h]))
    return jnp.concatenate(outs, axis=1)


def _rg_gates(xc, wa_ref, ba_ref, wx_ref, bx_ref, lam_ref):
    r = _sigmoid(_blockdiag(xc, wa_ref) + ba_ref[...])
    ig = _sigmoid(_blockdiag(xc, wx_ref) + bx_ref[...])
    sp = _softplus(-lam_ref[...])
    log_a = -RG_C * r * sp
    a = jnp.exp(log_a)
    beta = jnp.sqrt(_neg_expm1(2.0 * log_a))
    return r, ig, sp, a, beta


def _bcast8(row):
    return jnp.broadcast_to(row, (SUBLANES, row.shape[1]))


def _full(shape):
    nd = len(shape)
    return pl.BlockSpec(shape, lambda *_: (0,) * nd)


class _Comm:
    def __init__(self, arrays, out_shapes, sems, start, finish, aliases=()):
        self.arrays, self.out_shapes, self.sems = list(arrays), list(out_shapes), list(sems)
        self.start, self.finish, self.aliases = start, finish, tuple(aliases)
        self.results = None


class _RowOf:
    def __init__(self, ref, k):
        self.ref, self.k = ref, k

    def __getitem__(self, idx):
        cols = slice(None) if idx is Ellipsis else idx[1]
        return self.ref[0, self.k:self.k + 1, cols]


class _PartOf:
    def __init__(self, ref, rows=None, cols=None, lead=None):
        self.ref, self.rows, self.cols, self.lead = ref, rows, cols, lead
        if rows is not None:
            self.shape = (rows.stop - rows.start,) + tuple(ref.shape[1:])

    def _at(self, idx):
        if self.lead is not None:
            return (self.lead,) + tuple(idx[1:])
        if self.cols is not None:
            return (slice(None), self.cols)
        return (self.rows, slice(None) if idx is Ellipsis else idx[1])

    def __getitem__(self, idx):
        return self.ref[self._at(idx)]

    def __setitem__(self, idx, value):
        self.ref[self._at(idx)] = value


def _vec(table, layer, k):
    return ("row", table, layer, k)


def _is_row(arg):
    return isinstance(arg, tuple) and len(arg) == 4 and arg[0] == "row"


def _call(body, comms, *, name, grid, in_specs, out_specs, out_shape, args, scratch_shapes=(), aliases=None):
    comms = [cm for cm in (comms or []) if cm is not None]
    rows = {i: a[3] for i, a in enumerate(args) if _is_row(a)}
    in_specs = [pl.BlockSpec((1,) + a[1].shape[1:], functools.partial(lambda layer, *_: (layer, 0, 0), a[2]))
                if _is_row(a) else sp for a, sp in zip(args, in_specs)]
    args = tuple(a[1] if _is_row(a) else a for a in args)
    n_in, n_out, n_sc = len(args), len(out_shape), len(scratch_shapes)
    c_arrays = [a for cm in comms for a in cm.arrays]
    c_outs = [o for cm in comms for o in cm.out_shapes]
    c_sems = [sm for cm in comms for sm in cm.sems]
    aliases, a_at, o_at = dict(aliases or {}), n_in, n_out
    for cm in comms:
        for (i, j) in cm.aliases:
            aliases[a_at + i] = o_at + j
        a_at += len(cm.arrays)
        o_at += len(cm.out_shapes)

    def wrapped(*refs):
        ins, c_in = refs[:n_in], refs[n_in:n_in + len(c_arrays)]
        ins = [_RowOf(r, rows[i]) if i in rows else r for i, r in enumerate(ins)]
        at = n_in + len(c_arrays)
        outs, c_out = refs[at:at + n_out], refs[at + n_out:at + n_out + len(c_outs)]
        at += n_out + len(c_outs)
        scr, sems = refs[at:at + n_sc], refs[at + n_sc:]
        views, ia, io, isem = [], 0, 0, 0
        for cm in comms:
            views.append((c_in[ia:ia + len(cm.arrays)], c_out[io:io + len(cm.out_shapes)], sems[isem:isem + len(cm.sems)]))
            ia, io, isem = ia + len(cm.arrays), io + len(cm.out_shapes), isem + len(cm.sems)
        if comms:
            @pl.when(pl.program_id(0) == 0)
            def _():
                for cm, view in zip(comms, views):
                    cm.start(*view)

        body(*ins, *outs, *scr)
        if comms:
            @pl.when(pl.program_id(0) == grid[0] - 1)
            def _():
                for cm, view in zip(comms, views):
                    cm.finish(*view)

    hbm = pl.BlockSpec(memory_space=pl.ANY)
    res = pl.pallas_call(
        wrapped, name=name, grid=grid,
        in_specs=list(in_specs) + [hbm] * len(c_arrays), out_specs=list(out_specs) + [hbm] * len(c_outs),
        out_shape=list(out_shape) + c_outs, scratch_shapes=list(scratch_shapes) + c_sems,
        input_output_aliases=aliases, compiler_params=_params(len(grid)),
    )(*args, *c_arrays)
    at = n_out
    for cm in comms:
        cm.results = list(res[at:at + len(cm.out_shapes)])
        at += len(cm.out_shapes)
    return list(res[:n_out])


def _join_columns(w_ref, wcat):
    nd, _, _, w = w_ref.shape

    @pl.when(pl.program_id(0) == 0)
    def _():
        for j in range(nd):
            wcat[:, j * w:(j + 1) * w] = w_ref[j, 0]


def _in_fwd(x, ng, scale, shift, w_in_g, layer, tile, comms=None):
    s, d = x.shape
    nd, _, _, w = w_in_g.shape

    def body(x_ref, ng_ref, sc_ref, sh_ref, w_ref, u_ref, h_ref, wcat):
        _join_columns(w_ref, wcat)
        xv = x_ref[...]
        rs = lax.rsqrt(jnp.mean(xv * xv, axis=1, keepdims=True) + EPS)
        hb = (xv * rs * ng_ref[...] * (1.0 + sc_ref[...]) + sh_ref[...]).astype(BF16)
        h_ref[...] = hb
        u_ref[...] = jnp.dot(hb, wcat[...], preferred_element_type=F32)

    return _call(
        body, comms, name="in_proj_fwd", grid=(s // tile,),
        in_specs=[pl.BlockSpec((tile, d), lambda i: (i, 0)), _full((1, d)), _full((1, d)), _full((1, d)),
                  pl.BlockSpec((nd, 1, d, w), lambda i: (0, layer, 0, 0), pipeline_mode=pl.Buffered(1))],
        out_specs=[pl.BlockSpec((tile, nd * w), lambda i: (i, 0)), pl.BlockSpec((tile, d), lambda i: (i, 0))],
        out_shape=[jax.ShapeDtypeStruct((s, nd * w), F32), jax.ShapeDtypeStruct((s, d), BF16)],
        scratch_shapes=[pltpu.VMEM((d, nd * w), BF16)],
        args=(x, ng, scale, shift, w_in_g))


def _rg_fwd(u, d, conv_w, conv_b, w_a, b_a, w_x, b_x, lam, tile, comms=None):
    s = u.shape[0]

    def body(x_ref, z_ref, cw_ref, cb_ref, wa_ref, ba_ref, wx_ref, bx_ref, lam_ref,
             h_ref, y_ref, xc_ref, r_ref, i_ref, a_ref, beta_ref, prev8, hcar):
        @pl.when(pl.program_id(0) == 0)
        def _():
            prev8[...] = jnp.zeros_like(prev8)
            hcar[...] = jnp.zeros_like(hcar)

        x = x_ref[...]
        xc = _conv(x, prev8[...], cw_ref) + cb_ref[...]
        prev8[...] = x[tile - SUBLANES:tile, :]
        r, ig, _, a, beta = _rg_gates(xc, wa_ref, ba_ref, wx_ref, bx_ref, lam_ref)
        xc_ref[...] = xc
        r_ref[...] = r
        i_ref[...] = ig
        a_ref[...] = a
        beta_ref[...] = beta
        _scan_into(a, beta * ig * xc, hcar[SUBLANES - 1:SUBLANES, :], h_ref, False)
        h = h_ref[...]
        hcar[...] = h[tile - SUBLANES:tile, :]
        z = z_ref[...]
        y_ref[...] = (h * z * _sigmoid(z)).astype(BF16)

    vec = _full((1, d))
    return _call(
        body, comms, name="rglru_fwd", grid=(s // tile,),
        in_specs=[pl.BlockSpec((tile, d), lambda i: (i, 0)), pl.BlockSpec((tile, d), lambda i: (i, 1)),
                  _full(conv_w.shape), vec, _full(w_a.shape), vec, _full(w_x.shape), vec, vec],
        out_specs=[pl.BlockSpec((tile, d), lambda i: (i, 0))] * 7,
        out_shape=[jax.ShapeDtypeStruct((s, d), F32), jax.ShapeDtypeStruct((s, d), BF16)] + [jax.ShapeDtypeStruct((s, d), F32)] * 5,
        scratch_shapes=[pltpu.VMEM((SUBLANES, d), F32), pltpu.VMEM((SUBLANES, d), F32)],
        args=(u, u, conv_w, conv_b, w_a, b_a, w_x, b_x, lam))


def _ml_pre(u, d, conv_w, conv_b, w_q, w_k, w_v, wif, bif, tile, comms=None):
    s = u.shape[0]
    nh = w_q.shape[0]

    def body(x_ref, cw_ref, cb_ref, wq_ref, wk_ref, wv_ref, wif_ref, bif_ref, q_ref, k_ref, v_ref, g_ref, pre_ref, prev8):
        @pl.when(pl.program_id(0) == 0)
        def _():
            prev8[...] = jnp.zeros_like(prev8)

        x = x_ref[...]
        pre = _conv(x, prev8[...], cw_ref) + cb_ref[...]
        prev8[...] = x[tile - SUBLANES:tile, :]
        xc = pre * _sigmoid(pre)
        q = _blockdiag(xc, wq_ref)
        k = _blockdiag(xc, wk_ref)
        v = _blockdiag(x, wv_ref)
        pre_ref[...] = pre
        q_ref[...] = q
        k_ref[...] = k
        v_ref[...] = v
        g = _mm(q, wif_ref[0:d, :]) + _mm(k, wif_ref[d:2 * d, :]) + _mm(v, wif_ref[2 * d:3 * d, :]) + bif_ref[...]
        lane = _iota(g.shape, 1)
        gl = jnp.where(lane < 4, g, jnp.where(lane < 8, -_softplus(-g), 0.0))
        tri = jnp.where(_iota((ML_CHUNK, ML_CHUNK), 1) <= _iota((ML_CHUNK, ML_CHUNK), 0), 1.0, 0.0)
        cums = [_mm_hi(tri, gl[c * ML_CHUNK:(c + 1) * ML_CHUNK, :]) for c in range(tile // ML_CHUNK)]
        cum = cums[0] if len(cums) == 1 else jnp.concatenate(cums, axis=0)
        g_ref[...] = gl + jnp.where((lane >= 8) & (lane < 12), pltpu.roll(cum, 4, 1), 0.0)

    vec = _full((1, d))
    return _call(
        body, comms, name="mlstm_proj_fwd", grid=(s // tile,),
        in_specs=[pl.BlockSpec((tile, d), lambda i: (i, 2)), _full(conv_w.shape), vec,
                  _full(w_q.shape), _full(w_k.shape), _full(w_v.shape), _full(wif.shape), _full((1, LANES))],
        out_specs=[pl.BlockSpec((tile, d), lambda i: (i, 0))] * 3 + [pl.BlockSpec((tile, LANES), lambda i: (i, 0)),
                                                                     pl.BlockSpec((tile, d), lambda i: (i, 0))],
        out_shape=[jax.ShapeDtypeStruct((s, d), F32)] * 3 + [jax.ShapeDtypeStruct((s, LANES), F32),
                                                             jax.ShapeDtypeStruct((s, d), F32)],
        scratch_shapes=[pltpu.VMEM((SUBLANES, d), F32)],
        args=(u, conv_w, conv_b, w_q, w_k, w_v, wif, bif))


CELL_CHUNKS_PER_STEP = 4
CELL_BWD_CHUNKS_PER_STEP = 2


def _cell_chunk(h, nh, q_ref, k_ref, v_ref, gc, gr, m_prev, c_h, n_h, m_t=None, r0=0):
    lc = ML_CHUNK
    dh = q_ref.shape[1] // nh
    sl = slice(h * dh, (h + 1) * dh)
    qh = q_ref[r0:r0 + lc, sl]
    kh = k_ref[r0:r0 + lc, sl] * (dh ** -0.5)
    vh = v_ref[r0:r0 + lc, sl]
    li_c = _col(gc, h)
    b_c = _col(gc, 8 + h)
    lib_r = _row(gr, h) - _row(gr, 8 + h)
    b_last = _colsum(jnp.where(_iota((lc, 1), 0) == lc - 1, b_c, 0.0))
    causal = _iota((lc, lc), 1) <= _iota((lc, lc), 0)
    dmat = jnp.where(causal, b_c + lib_r, NEG_BIG)
    m_inter = b_c + m_prev
    if m_t is None:
        m_t = jnp.maximum(m_inter, jnp.max(dmat, axis=1, keepdims=True))
    w_intra = jnp.exp(dmat - m_t)
    w_inter = jnp.exp(m_inter - m_t)
    amat = _mm_nt(qh, kh)
    smat = amat * w_intra
    qc = _mm(qh, c_h)
    qn = _rowsum(qh * n_h)
    den = _rowsum(smat) + w_inter * qn
    gst = b_last - b_c + li_c
    m_new = jnp.maximum(b_last + m_prev, jnp.max(gst, axis=0, keepdims=True))
    w_state = jnp.exp(gst - m_new)
    decay = jnp.exp(b_last + m_prev - m_new)
    return dict(sl=sl, qh=qh, kh=kh, vh=vh, m_t=m_t, w_intra=w_intra, w_inter=w_inter, smat=smat, qc=qc, qn=qn,
                den=den, m_new=m_new, w_state=w_state, decay=decay)


def _ml_cell_fwd(q, k, v, gcol, grow, u, ng, nh, comms=None):
    s, d = q.shape
    lc = ML_CHUNK
    nc = s // lc
    dh = d // nh

    per = CELL_CHUNKS_PER_STEP if nc % CELL_CHUNKS_PER_STEP == 0 else 1

    def body(q_ref, k_ref, v_ref, gc_ref, gr_ref, o_ref, z_ref, ng_ref,
             cell_ref, y_ref, cs_ref, ns_ref, ms_ref, mt_ref, c_sc, n_sc, m_sc):
        @pl.when(pl.program_id(0) == 0)
        def _():
            c_sc[...] = jnp.zeros_like(c_sc)
            n_sc[...] = jnp.zeros_like(n_sc)
            m_sc[...] = jnp.zeros_like(m_sc)

        lane = _iota((lc, LANES), 1)
        for cc in range(per):
            rows = slice(cc * lc, (cc + 1) * lc)
            gc = gc_ref[rows, :]
            gr = gr_ref[:, rows]
            mt_acc = jnp.zeros((lc, LANES), F32)
            for h in range(nh):
                c_h = c_sc[h]
                n_h = n_sc[h, 0:1, :]
                m_prev = jnp.max(m_sc[h, 0:1, :], axis=1, keepdims=True)
                cs_ref[cc, h] = c_h
                ns_ref[cc, h] = n_sc[h]
                ms_ref[cc, h] = m_sc[h]
                t = _cell_chunk(h, nh, q_ref, k_ref, v_ref, gc, gr, m_prev, c_h, n_h, r0=cc * lc)
                sl = t["sl"]
                num = _mm(t["smat"], t["vh"]) + t["w_inter"] * t["qc"]
                cell_h = num / jnp.maximum(jnp.abs(t["den"]), jnp.exp(-t["m_t"]))
                mt_acc = jnp.where(lane == h, t["m_t"], mt_acc)
                kw = t["kh"] * t["w_state"]
                c_sc[h] = t["decay"] * c_h + _mm_tn(kw, t["vh"])
                n_sc[h] = _bcast8(t["decay"] * n_h + _colsum(kw))
                m_sc[h] = jnp.broadcast_to(t["m_new"], (SUBLANES, LANES))
                hg = _sigmoid(o_ref[rows, sl]) * cell_h
                hn = hg * lax.rsqrt(jnp.mean(hg * hg, axis=1, keepdims=True) + EPS)
                z = z_ref[rows, sl]
                cell_ref[rows, sl] = cell_h
                y_ref[rows, sl] = (hn * ng_ref[:, sl] * z * _sigmoid(z)).astype(BF16)
            mt_ref[rows, :] = mt_acc

    tok = pl.BlockSpec((per * lc, d), lambda c: (c, 0))
    return _call(
        body, comms, name="mlstm_cell_fwd", grid=(nc // per,),
        in_specs=[tok, tok, tok, pl.BlockSpec((per * lc, LANES), lambda c: (c, 0)),
                  pl.BlockSpec((16, per * lc), lambda c: (0, c)),
                  pl.BlockSpec((per * lc, d), lambda c: (c, 3)), pl.BlockSpec((per * lc, d), lambda c: (c, 4)), _full((1, d))],
        out_specs=[tok, tok, pl.BlockSpec((per, nh, dh, dh), lambda c: (c, 0, 0, 0)),
                   pl.BlockSpec((per, nh, SUBLANES, dh), lambda c: (c, 0, 0, 0)),
                   pl.BlockSpec((per, nh, SUBLANES, LANES), lambda c: (c, 0, 0, 0)),
                   pl.BlockSpec((per * lc, LANES), lambda c: (c, 0))],
        out_shape=[jax.ShapeDtypeStruct((s, d), F32), jax.ShapeDtypeStruct((s, d), BF16),
                   jax.ShapeDtypeStruct((nc, nh, dh, dh), F32), jax.ShapeDtypeStruct((nc, nh, SUBLANES, dh), F32),
                   jax.ShapeDtypeStruct((nc, nh, SUBLANES, LANES), F32), jax.ShapeDtypeStruct((s, LANES), F32)],
        scratch_shapes=[pltpu.VMEM((nh, dh, dh), F32), pltpu.VMEM((nh, SUBLANES, dh), F32),
                        pltpu.VMEM((nh, SUBLANES, LANES), F32)],
        args=(q, k, v, gcol, grow, u, u, ng))


def _out_fwd(x, y_rg, y_ml, gate, w_out_g, layer, tile, comms=None):
    s, d = x.shape
    nd, _, r, _ = w_out_g.shape

    def body(x_ref, yr_ref, ym_ref, g_ref, w_ref, xn_ref, y_ref):
        ycat = jnp.concatenate([yr_ref[...].astype(BF16), ym_ref[...].astype(BF16)], axis=1)
        acc = jnp.dot(ycat, w_ref[...].reshape(nd * r, d), preferred_element_type=F32)
        y_ref[...] = acc
        xn_ref[...] = x_ref[...] + g_ref[...] * acc

    tok = pl.BlockSpec((tile, d), lambda i: (i, 0))
    return _call(
        body, comms, name="out_proj_fwd", grid=(s // tile,),
        in_specs=[tok, tok, tok, _full((1, d)), pl.BlockSpec((nd, 1, r, d), lambda i: (0, layer, 0, 0))],
        out_specs=[tok, tok],
        out_shape=[jax.ShapeDtypeStruct((s, d), F32)] * 2,
        args=(x, y_rg, y_ml, gate, w_out_g))


def _loss_call(x, fg, target, tile):
    s, d = x.shape

    def body(x_ref, g_ref, t_ref, dx_ref, loss_ref, gg_ref):
        @pl.when(pl.program_id(0) == 0)
        def _():
            loss_ref[...] = jnp.zeros_like(loss_ref)
            gg_ref[...] = jnp.zeros_like(gg_ref)

        xv = x_ref[...]
        g = g_ref[...]
        rs = lax.rsqrt(jnp.mean(xv * xv, axis=1, keepdims=True) + EPS)
        xh = xv * rs
        e = xh * g - t_ref[...]
        loss_ref[...] += jnp.broadcast_to(_colsum(_rowsum(e * e)) * (0.5 / d), loss_ref.shape)
        dy = e * (1.0 / d)
        gg_ref[...] += _bcast8(_colsum(dy * xh))
        dxh = dy * g
        dx_ref[...] = rs * (dxh - xh * jnp.mean(dxh * xh, axis=1, keepdims=True))

    tok = pl.BlockSpec((tile, d), lambda i: (i, 0))
    return pl.pallas_call(
        body, name="final_norm_loss", grid=(s // tile,),
        in_specs=[tok, _full((1, d)), tok],
        out_specs=[tok, _full((SUBLANES, LANES)), _full((SUBLANES, d))],
        out_shape=[jax.ShapeDtypeStruct((s, d), F32), jax.ShapeDtypeStruct((SUBLANES, LANES), F32),
                   jax.ShapeDtypeStruct((SUBLANES, d), F32)],
        compiler_params=_params(1),
    )(x, fg, target)


def _ml_out_stage_bwd(dy, cell, o, z, ng):
    so = _sigmoid(o)
    hg = so * cell
    rinv = lax.rsqrt(jnp.mean(hg * hg, axis=1, keepdims=True) + EPS)
    hn = hg * rinv
    sz = _sigmoid(z)
    dz = dy * hn * ng * (sz + z * sz * (1.0 - sz))
    dymid = dy * z * sz
    dhn = dymid * ng
    dhg = rinv * (dhn - hn * jnp.mean(dhn * hn, axis=1, keepdims=True))
    return dz, dhg * cell * so * (1.0 - so), dhg * so, _colsum(dymid * hn)


def _out_bwd(dxo, gate, y, y_rg, y_ml, w_out_g, layer, tile, comms=None):
    s, d = dxo.shape
    nd, _, r, _ = w_out_g.shape

    def body(dx_ref, g_ref, y_ref, yr_ref, ym_ref, w_ref, dyr_ref, dym_ref, gw_ref, dg_ref):
        @pl.when(pl.program_id(0) == 0)
        def _():
            gw_ref[...] = jnp.zeros_like(gw_ref)
            dg_ref[...] = jnp.zeros_like(dg_ref)

        dxv = dx_ref[...]
        dg_ref[...] += _bcast8(_colsum(dxv * y_ref[...]))
        dyb = (dxv * g_ref[...]).astype(BF16)
        dycat = lax.dot_general(dyb, w_ref[...].reshape(nd * r, d), (((1,), (1,)), ((), ())), preferred_element_type=F32)
        dyr_ref[...] = dycat[:, 0:d]
        dym_ref[...] = dycat[:, d:2 * d]
        ycat = jnp.concatenate([yr_ref[...].astype(BF16), ym_ref[...].astype(BF16)], axis=1)
        gw_ref[...] += lax.dot_general(ycat, dyb, (((0,), (0,)), ((), ())), preferred_element_type=F32).reshape(nd, r, d)

    tok = pl.BlockSpec((tile, d), lambda i: (i, 0))
    return _call(
        body, comms, name="out_proj_bwd", grid=(s // tile,),
        in_specs=[tok, _full((1, d)), tok, tok, tok, pl.BlockSpec((nd, 1, r, d), lambda i: (0, layer, 0, 0))],
        out_specs=[tok, tok, _full((nd, r, d)), _full((SUBLANES, d))],
        out_shape=[jax.ShapeDtypeStruct((s, d), F32)] * 2 + [jax.ShapeDtypeStruct((nd, r, d), F32),
                                                             jax.ShapeDtypeStruct((SUBLANES, d), F32)],
        args=(dxo, gate, y, y_rg, y_ml, w_out_g))


def _ml_cell_bwd(dy_ml, u, cell, q, k, v, gcol, grow, mt, cs, ns, ms, ng, nh, comms=None):
    s, d = q.shape
    lc = ML_CHUNK
    nc = s // lc
    dh = d // nh

    per = CELL_BWD_CHUNKS_PER_STEP if nc % CELL_BWD_CHUNKS_PER_STEP == 0 else 1

    def body(*refs):
        gng_ref, dc_sc, dn_sc = refs[20], refs[21], refs[22]

        @pl.when(pl.program_id(0) == 0)
        def _():
            dc_sc[...] = jnp.zeros_like(dc_sc)
            dn_sc[...] = jnp.zeros_like(dn_sc)
            gng_ref[...] = jnp.zeros_like(gng_ref)

        for cc in reversed(range(per)):
            rows = slice(cc * lc, (cc + 1) * lc)
            views = [refs[at] if at == 13 else _PartOf(refs[at], cols=rows) if at == 8 else
                     _PartOf(refs[at], lead=cc) if at in (10, 11, 12) else _PartOf(refs[at], rows=rows) for at in range(20)]
            chunk(*views, gng_ref, dc_sc, dn_sc)

    def chunk(dy_ref, o_ref, z_ref, cell_ref, q_ref, k_ref, v_ref, gc_ref, gr_ref, mt_ref, cs_ref, ns_ref, ms_ref,
              ng_ref, dq_ref, dk_ref, dv_ref, dg_ref, do_ref, dz_ref, gng_ref, dc_sc, dn_sc):
        gc = gc_ref[...]
        gr = gr_ref[...]
        mtv = mt_ref[...]
        lane = _iota((lc, LANES), 1)
        rowv = _iota((lc, 1), 0)
        dg_acc = jnp.zeros((lc, LANES), F32)
        for h in range(nh):
            c_h = cs_ref[0, h]
            n_h = ns_ref[0, h, 0:1, :]
            m_prev = jnp.max(ms_ref[0, h, 0:1, :], axis=1, keepdims=True)
            t = _cell_chunk(h, nh, q_ref, k_ref, v_ref, gc, gr, m_prev, c_h, n_h, m_t=_col(mtv, h))
            sl, qh, kh, vh = t["sl"], t["qh"], t["kh"], t["vh"]
            w_intra, w_inter, smat, w_state, decay = t["w_intra"], t["w_inter"], t["smat"], t["w_state"], t["decay"]
            cell_h = cell_ref[:, sl]
            dz, do, dcell, gng = _ml_out_stage_bwd(dy_ref[:, sl], cell_h, o_ref[:, sl], z_ref[:, sl], ng_ref[:, sl])
            dz_ref[:, sl] = dz.astype(BF16)
            do_ref[:, sl] = do.astype(BF16)
            gng_ref[:, sl] += _bcast8(gng)
            eneg = jnp.exp(-t["m_t"])
            aden = jnp.abs(t["den"])
            nst = jnp.maximum(aden, eneg)
            dnum = dcell / nst
            dden = jnp.where(aden > eneg, -_rowsum(cell_h * dcell) / nst * jnp.sign(t["den"]), 0.0)
            pmat = _mm_nt(dnum, vh) + dden
            damat = pmat * w_intra
            gmat = pmat * smat
            wdn = w_inter * dnum
            wdd = w_inter * dden
            dqh = _mm(damat, kh) + _mm_nt(wdn, c_h) + wdd * n_h
            dkh = _mm_tn(damat, qh)
            dvh = _mm_tn(smat, dnum)
            dw_inter = _rowsum(dnum * t["qc"]) + dden * t["qn"]
            dcn = dc_sc[h]
            dnn = dn_sc[h, 0:1, :]
            kw = kh * w_state
            dkw = _mm_nt(vh, dcn) + dnn
            dvh = dvh + _mm(kw, dcn)
            dkh = dkh + dkw * w_state
            dgst = _rowsum(dkw * kh) * w_state
            ddecay = _colsum(_rowsum(dcn * c_h)) + _rowsum(dnn * n_h)
            db_last = _colsum(dgst) + ddecay * decay
            rs_g = _rowsum(gmat)
            cs_g = _rowsum(gmat.T)
            db = rs_g - cs_g + dw_inter * w_inter - dgst + jnp.where(rowv == lc - 1, db_last, 0.0)
            dli = cs_g + dgst
            dc_sc[h] = decay * dcn + _mm_tn(qh, wdn)
            dn_sc[h] = _bcast8(decay * dnn + _colsum(qh * wdd))
            dq_ref[:, sl] = dqh
            dk_ref[:, sl] = dkh * (dh ** -0.5)
            dv_ref[:, sl] = dvh
            dg_acc = jnp.where(lane == h, dli, jnp.where(lane == 4 + h, db, dg_acc))
        dg_ref[...] = dg_acc

    rev = lambda c: nc // per - 1 - c
    tok = pl.BlockSpec((per * lc, d), lambda c: (rev(c), 0))
    g128 = pl.BlockSpec((per * lc, LANES), lambda c: (rev(c), 0))
    return _call(
        body, comms, name="mlstm_cell_bwd", grid=(nc // per,),
        in_specs=[tok, pl.BlockSpec((per * lc, d), lambda c: (rev(c), 3)), pl.BlockSpec((per * lc, d), lambda c: (rev(c), 4)),
                  tok, tok, tok, tok, g128, pl.BlockSpec((16, per * lc), lambda c: (0, rev(c))), g128,
                  pl.BlockSpec((per, nh, dh, dh), lambda c: (rev(c), 0, 0, 0)),
                  pl.BlockSpec((per, nh, SUBLANES, dh), lambda c: (rev(c), 0, 0, 0)),
                  pl.BlockSpec((per, nh, SUBLANES, LANES), lambda c: (rev(c), 0, 0, 0)), _full((1, d))],
        out_specs=[tok, tok, tok, g128, tok, tok, _full((SUBLANES, d))],
        out_shape=[jax.ShapeDtypeStruct((s, d), F32)] * 3 + [jax.ShapeDtypeStruct((s, LANES), F32)]
        + [jax.ShapeDtypeStruct((s, d), BF16)] * 2 + [jax.ShapeDtypeStruct((SUBLANES, d), F32)],
        scratch_shapes=[pltpu.VMEM((nh, dh, dh), F32), pltpu.VMEM((nh, SUBLANES, dh), F32)],
        args=(dy_ml, u, u, cell, q, k, v, gcol, grow, mt, cs, ns, ms, ng))


def _halo_spec(d, tile, nt, col):
    per = tile // SUBLANES
    return pl.BlockSpec((SUBLANES, d), lambda i: (jnp.maximum((nt - 1 - i) * per - 1, 0), col))


def _ml_pre_bwd(dq, dk, dv, dgates, gcol, u, pre, q, k, v, conv_w, w_q, w_k, w_v, wif_t, tile, comms=None):
    s, d = dq.shape
    nt = s // tile
    nh, dh, _ = w_q.shape

    def body(dq_ref, dk_ref, dv_ref, dg_ref, gc_ref, x_ref, pre_ref, q_ref, k_ref, v_ref, cw_ref,
             wq_ref, wk_ref, wv_ref, wift_ref,
             dx_ref, gwq_ref, gwk_ref, gwv_ref, gwif_ref, gbif_ref, gcw_ref, gcb_ref, next8):
        @pl.when(pl.program_id(0) == 0)
        def _():
            next8[...] = jnp.zeros_like(next8)
            for ref in (gwq_ref, gwk_ref, gwv_ref, gwif_ref, gbif_ref, gcw_ref, gcb_ref):
                ref[...] = jnp.zeros_like(ref)

        x = x_ref[...]
        pre = pre_ref[...]
        sg = _sigmoid(pre)
        xc = pre * sg
        dgc = dg_ref[...]
        lane = _iota(dgc.shape, 1)
        utri = jnp.where(_iota((ML_CHUNK, ML_CHUNK), 0) <= _iota((ML_CHUNK, ML_CHUNK), 1), 1.0, 0.0)
        rcs = [_mm_hi(utri, dgc[c * ML_CHUNK:(c + 1) * ML_CHUNK, :]) for c in range(tile // ML_CHUNK)]
        rc = rcs[0] if len(rcs) == 1 else jnp.concatenate(rcs, axis=0)
        dgates_v = jnp.where(lane < 4, dgc, jnp.where(lane < 8, rc * (1.0 - jnp.exp(gc_ref[...])), 0.0))
        dgb = dgates_v.astype(BF16)
        gbif_ref[...] += jnp.broadcast_to(_colsum(dgates_v), gbif_ref.shape)
        ext = jnp.dot(dgb, wift_ref[...], preferred_element_type=F32)
        dqt = dq_ref[...] + ext[:, 0:d]
        dkt = dk_ref[...] + ext[:, d:2 * d]
        dvt = dv_ref[...] + ext[:, 2 * d:3 * d]
        gwif_ref[:, 0:d] += _mm_tn(dgb, q_ref[...])
        gwif_ref[:, d:2 * d] += _mm_tn(dgb, k_ref[...])
        gwif_ref[:, 2 * d:3 * d] += _mm_tn(dgb, v_ref[...])
        dxc_parts, dxv_parts = [], []
        for h in range(nh):
            sl = slice(h * dh, (h + 1) * dh)
            gwq_ref[h] += _mm_tn(xc[:, sl], dqt[:, sl])
            gwk_ref[h] += _mm_tn(xc[:, sl], dkt[:, sl])
            gwv_ref[h] += _mm_tn(x[:, sl], dvt[:, sl])
            dxc_parts.append(_mm_nt(dqt[:, sl], wq_ref[h]) + _mm_nt(dkt[:, sl], wk_ref[h]))
            dxv_parts.append(_mm_nt(dvt[:, sl], wv_ref[h]))
        dxc = jnp.concatenate(dxc_parts, axis=1)
        dxv = jnp.concatenate(dxv_parts, axis=1)
        dpre = dxc * (sg + pre * sg * (1.0 - sg))
        gcb_ref[...] += _bcast8(_colsum(dpre))
        dx_ref[...] = (dxv + _conv_bwd(dpre, x, next8[...], cw_ref, gcw_ref)).astype(BF16)
        next8[...] = dpre[0:SUBLANES, :]

    rev = lambda i: nt - 1 - i
    tok = pl.BlockSpec((tile, d), lambda i: (rev(i), 0))
    g128 = pl.BlockSpec((tile, LANES), lambda i: (rev(i), 0))
    wsh = (nh, dh, dh)
    return _call(
        body, comms, name="mlstm_proj_bwd", grid=(nt,),
        in_specs=[tok, tok, tok, g128, g128, pl.BlockSpec((tile, d), lambda i: (rev(i), 2)), tok,
                  tok, tok, tok, _full(conv_w.shape), _full(wsh), _full(wsh), _full(wsh), _full(wif_t.shape)],
        out_specs=[tok, _full(wsh), _full(wsh), _full(wsh), _full((LANES, 3 * d)), _full((SUBLANES, LANES)),
                   _full((SUBLANES, d)), _full((SUBLANES, d))],
        out_shape=[jax.ShapeDtypeStruct((s, d), BF16)] + [jax.ShapeDtypeStruct(wsh, F32)] * 3
        + [jax.ShapeDtypeStruct((LANES, 3 * d), F32), jax.ShapeDtypeStruct((SUBLANES, LANES), F32),
           jax.ShapeDtypeStruct((SUBLANES, d), F32), jax.ShapeDtypeStruct((SUBLANES, d), F32)],
        scratch_shapes=[pltpu.VMEM((SUBLANES, d), F32)],
        args=(dq, dk, dv, dgates, gcol, u, pre, q, k, v, conv_w, w_q, w_k, w_v, wif_t))


def _rg_bwd(dy_rg, u, h_rg, gates, conv_w, w_a, w_x, lam, tile, comms=None):
    s, d = dy_rg.shape
    nt = s // tile
    nh, dh, _ = w_a.shape

    def body(dy_ref, x_ref, z_ref, h_ref, hhalo_ref, xc_ref, r_ref, i_ref, a_ref, beta_ref, cw_ref, wa_ref,
             wx_ref, lam_ref,
             dx_ref, dz_ref, gwa_ref, gwx_ref, gba_ref, gbx_ref, glam_ref, gcw_ref, gcb_ref, next8, anext, dnext, dbuf):
        i = pl.program_id(0)

        @pl.when(i == 0)
        def _():
            for ref in (next8, anext, dnext, gwa_ref, gwx_ref, gba_ref, gbx_ref, glam_ref, gcw_ref, gcb_ref):
                ref[...] = jnp.zeros_like(ref)

        inner = jnp.where(i < nt - 1, 1.0, 0.0)
        xc, r, ig, a, beta = xc_ref[...], r_ref[...], i_ref[...], a_ref[...], beta_ref[...]
        sp = _softplus(-lam_ref[...])
        h = h_ref[...]
        row = _iota(h.shape, 0)
        hprev = jnp.where(row >= 1, pltpu.roll(h, 1, 0), hhalo_ref[SUBLANES - 1:SUBLANES, :] * inner)
        z = z_ref[...]
        sz = _sigmoid(z)
        dyv = dy_ref[...]
        dz_ref[...] = (dyv * h * (sz + z * sz * (1.0 - sz))).astype(BF16)
        a_up = jnp.where(row < tile - 1, pltpu.roll(a, tile - 1, 0), anext[0:1, :])
        _scan_into(a_up, dyv * z * sz, dnext[0:1, :], dbuf, True)
        delta = dbuf[...]
        anext[...] = a[0:SUBLANES, :]
        dnext[...] = delta[0:SUBLANES, :]
        dla = delta * hprev * a - delta * ig * xc * (a * a / beta)
        glam_ref[...] += _bcast8(_colsum(dla * r) * (RG_C * _sigmoid(-lam_ref[...])))
        dpa = dla * (-RG_C * sp) * r * (1.0 - r)
        dpx = delta * beta * xc * ig * (1.0 - ig)
        gba_ref[...] += _bcast8(_colsum(dpa))
        gbx_ref[...] += _bcast8(_colsum(dpx))
        parts = []
        for hh in range(nh):
            sl = slice(hh * dh, (hh + 1) * dh)
            gwa_ref[hh] += _mm_tn(xc[:, sl], dpa[:, sl])
            gwx_ref[hh] += _mm_tn(xc[:, sl], dpx[:, sl])
            parts.append(_mm_nt(dpa[:, sl], wa_ref[hh]) + _mm_nt(dpx[:, sl], wx_ref[hh]))
        dxc = delta * beta * ig + jnp.concatenate(parts, axis=1)
        gcb_ref[...] += _bcast8(_colsum(dxc))
        dx_ref[...] = _conv_bwd(dxc, x_ref[...], next8[...], cw_ref, gcw_ref).astype(BF16)
        next8[...] = dxc[0:SUBLANES, :]

    rev = lambda i: nt - 1 - i
    tok = pl.BlockSpec((tile, d), lambda i: (rev(i), 0))
    vec = _full((1, d))
    acc = _full((SUBLANES, d))
    wsh = (nh, dh, dh)
    return _call(
        body, comms, name="rglru_bwd", grid=(nt,),
        in_specs=[tok, tok, pl.BlockSpec((tile, d), lambda i: (rev(i), 1)), tok,
                  _halo_spec(d, tile, nt, 0)] + [tok] * 5 + [_full(conv_w.shape), _full(wsh), _full(wsh), vec],
        out_specs=[tok, tok, _full(wsh), _full(wsh), acc, acc, acc, acc, acc],
        out_shape=[jax.ShapeDtypeStruct((s, d), BF16)] * 2 + [jax.ShapeDtypeStruct(wsh, F32)] * 2
        + [jax.ShapeDtypeStruct((SUBLANES, d), F32)] * 5,
        scratch_shapes=[pltpu.VMEM((SUBLANES, d), F32)] * 3 + [pltpu.VMEM((tile, d), F32)],
        args=(dy_rg, u, u, h_rg, h_rg, *gates, conv_w, w_a, w_x, lam))


def _segments(d, w, n_pieces, n_slots):
    bounds = sorted({k * d for k in range(n_pieces + 1)} | {j * w for j in range(n_slots + 1)})
    return [(lo // d, lo % d, lo // w, lo % w, hi - lo) for lo, hi in zip(bounds[:-1], bounds[1:])]


def _in_bwd(pieces, x, dxo, ng, scale, w_in_g, layer, tile, comms=None, tiles=None, prev=None):
    s, d = x.shape
    nd, _, _, w = w_in_g.shape
    first, count = tiles or (0, s // tile)
    n_p = len(pieces)

    def body(*refs):
        p_refs = refs[:n_p]
        x_ref, dxo_ref, ng_ref, sc_ref, w_ref = refs[n_p:n_p + 5]
        dx_ref, dsc_ref, dsh_ref, gng_ref, wcat = refs[-5:]
        _join_columns(w_ref, wcat)

        @pl.when(pl.program_id(0) == 0)
        def _():
            for k, ref in enumerate((dsc_ref, dsh_ref, gng_ref)):
                ref[...] = jnp.zeros_like(ref) if prev is None else refs[n_p + 6 + k][...]

        du = jnp.concatenate([p[...] for p in p_refs], axis=1)
        dh = lax.dot_general(du, wcat[...], (((1,), (1,)), ((), ())), preferred_element_type=F32)
        xv = x_ref[...]
        g = ng_ref[...]
        rs = lax.rsqrt(jnp.mean(xv * xv, axis=1, keepdims=True) + EPS)
        xh = xv * rs
        dsh_ref[...] += _bcast8(_colsum(dh))
        dsc_ref[...] += _bcast8(_colsum(dh * xh * g))
        dhn = dh * (1.0 + sc_ref[...])
        gng_ref[...] += _bcast8(_colsum(dhn * xh))
        dxh = dhn * g
        dx_ref[...] = dxo_ref[...] + rs * (dxh - xh * jnp.mean(dxh * xh, axis=1, keepdims=True))

    tok = pl.BlockSpec((tile, d), lambda i: (i + first, 0))
    vec = _full((1, d))
    acc = _full((SUBLANES, d))
    more_specs = [] if prev is None else [pl.BlockSpec(memory_space=pl.ANY), acc, acc, acc]
    return _call(
        body, comms, name="in_proj_bwd_x", grid=(count,),
        in_specs=[tok] * n_p + [tok, tok, vec, vec, pl.BlockSpec((nd, 1, d, w), lambda i: (0, layer, 0, 0),
                                                               pipeline_mode=pl.Buffered(1))] + more_specs,
        out_specs=[tok, acc, acc, acc],
        out_shape=[jax.ShapeDtypeStruct((s, d), F32)] + [jax.ShapeDtypeStruct((SUBLANES, d), F32)] * 3,
        scratch_shapes=[pltpu.VMEM((d, nd * w), BF16)],
        args=(*pieces, x, dxo, ng, scale, w_in_g) + (() if prev is None else tuple(prev)),
        aliases={} if prev is None else {n_p + 5: 0})


def _in_bwd_w(pieces, hbf, w, slots, tile, comms=None):
    s, d = hbf.shape
    nd_all = len(pieces) * d // w
    segs = [sg for sg in _segments(d, w, len(pieces), nd_all) if sg[2] in slots]

    def body(*refs):
        p_refs = refs[:len(pieces)]
        h_ref, gw_ref = refs[len(pieces):]

        @pl.when(pl.program_id(0) == 0)
        def _():
            gw_ref[...] = jnp.zeros_like(gw_ref)

        hv = h_ref[...]
        for (kk, a, j, b, width) in segs:
            gw_ref[j - slots[0], :, b:b + width] += _mm_tn(hv, p_refs[kk][:, a:a + width])

    tok = pl.BlockSpec((tile, d), lambda i: (i, 0))
    return _call(
        body, comms, name="in_proj_bwd_w", grid=(s // tile,),
        in_specs=[tok] * len(pieces) + [tok],
        out_specs=[pl.BlockSpec((len(slots), d, w), lambda i: (0, 0, 0), pipeline_mode=pl.Buffered(1))],
        out_shape=[jax.ShapeDtypeStruct((len(slots), d, w), F32)],
        args=(*pieces, hbf))[0]


def _exchange(arrs, gather, name):
    return _run_comms([_exchange_comm(arrs, gather)], name)[0]


def _run_comms(comms, name):
    _call(lambda: None, comms, name=name, grid=(1,), in_specs=[], out_specs=[], out_shape=[], args=())
    return [cm.results for cm in comms]


def _exchange_comm(arrs, gather):
    n = len(arrs)
    per = N_DEV - 1

    def copies(ins, outs, sems):
        send_sems, recv_sems, local_sems = sems
        x, y, c = (lax.axis_index(ax) for ax in MESH_AXES)
        me = 4 * x + 2 * y + c
        sends, recvs = [], []
        for flip in range(1, N_DEV):
            px = x ^ ((flip >> 2) & 1)
            py = y ^ ((flip >> 1) & 1)
            pc = c ^ (flip & 1)
            peer = 4 * px + 2 * py + pc
            for kk in range(n):
                src = ins[kk] if gather else ins[kk].at[peer]
                sends.append(_remote(src, outs[kk].at[me], send_sems, recv_sems, kk * per + flip - 1, (px, py, pc)))
                recvs.append(_remote(src, outs[kk].at[peer], send_sems, recv_sems, kk * per + flip - 1, (px, py, pc)))
        local = [pltpu.make_async_copy(ins[kk] if gather else ins[kk].at[me], outs[kk].at[me], local_sems.at[kk])
                 for kk in range(n)]
        return local, sends, recvs

    def start(ins, outs, sems):
        local, sends, _ = copies(ins, outs, sems)
        for cp in sends + local:
            cp.start()

    def finish(ins, outs, sems):
        local, sends, recvs = copies(ins, outs, sems)
        for cp in recvs:
            cp.wait_recv()
        for cp in sends:
            cp.wait_send()
        for cp in local:
            cp.wait()

    return _Comm(arrs, [jax.ShapeDtypeStruct((N_DEV,) + a.shape if gather else a.shape, a.dtype) for a in arrs],
                 [pltpu.SemaphoreType.DMA((n * per,)), pltpu.SemaphoreType.DMA((n * per,)), pltpu.SemaphoreType.DMA((n,))],
                 start, finish)


def _mesh_place():
    x, y, c = (lax.axis_index(ax) for ax in MESH_AXES)
    return x, y, c, (x, y, 1 - c), [(1 - x, y), (x, 1 - y), (1 - x, 1 - y)]


def _remote(src, dst, send_sems, recv_sems, sem, to):
    return pltpu.make_async_remote_copy(src_ref=src, dst_ref=dst, send_sem=send_sems.at[sem], recv_sem=recv_sems.at[sem],
                                        device_id=to, device_id_type=pl.DeviceIdType.MESH)


N_CHIPS = N_DEV // 2


def _pair_sum(a, other, parity, name):
    _, r, c = a.shape
    tr = _row_tile(r, c, 3)

    def body(p_ref, a_ref, o_ref, s_ref):
        s_ref[...] = (a_ref[...] + o_ref[...]).astype(BF16)

    return pl.pallas_call(
        body, name=name,
        grid_spec=pltpu.PrefetchScalarGridSpec(
            num_scalar_prefetch=1, grid=(N_CHIPS, r // tr),
            in_specs=[pl.BlockSpec((1, tr, c), lambda q, i, p: (2 * q + p[0], i, 0)),
                      pl.BlockSpec((1, tr, c), lambda q, i, p: (q, i, 0))],
            out_specs=pl.BlockSpec((1, tr, c), lambda q, i, p: (q, i, 0))),
        out_shape=jax.ShapeDtypeStruct((N_CHIPS, r, c), BF16),
        compiler_params=_params(2),
    )(parity, a, other)


def _adam_math(w, g, m, v):
    m = ADAM_B1 * m + (1.0 - ADAM_B1) * g
    v = ADAM_B2 * v + (1.0 - ADAM_B2) * (g * g)
    m_hat = m / (1.0 - ADAM_B1 ** ADAM_STEP)
    v_hat = v / (1.0 - ADAM_B2 ** ADAM_STEP)
    delta = -ADAM_LR * (m_hat / (jnp.sqrt(v_hat) + ADAM_EPS) + ADAM_WD * w)
    return delta, m, v


def _sum_devices(r_ref):
    acc = r_ref[0].astype(F32)
    for p in range(1, r_ref.shape[0]):
        acc = acc + r_ref[p].astype(F32)
    return acc


def _row_tile(rows, cols, n_bufs):
    budget = 24 * 1024 * 1024 // (n_bufs * 2 * cols * 4)
    t = rows
    while t > budget and t % 2 == 0 and (t // 2) % SUBLANES == 0:
        t //= 2
    return t


def _reduce_adam(recvs, w, m, v, name, comms=None):
    nl, r, c = w.shape
    n_part = recvs[0].shape[0]
    tr = _row_tile(r, c, n_part * nl + 7)
    nt = r // tr

    def body(*refs):
        r_refs = refs[:nl]
        w_ref, m_ref, v_ref, g_ref, d_ref, mo_ref, vo_ref = refs[nl:]
        layer = pl.program_id(0) // nt
        g = _sum_devices(r_refs[0])
        for ll in range(1, nl):
            g = jnp.where(layer == ll, _sum_devices(r_refs[ll]), g)
        delta, m2, v2 = _adam_math(w_ref[0], g, m_ref[0], v_ref[0])
        g_ref[0] = g
        d_ref[0] = delta
        mo_ref[0] = m2
        vo_ref[0] = v2

    def rspec(ll):
        return pl.BlockSpec((n_part, tr, c),
                            lambda i: (0, jnp.where(i // nt == ll, i % nt, jnp.where(i // nt < ll, 0, nt - 1)), 0))

    blk = pl.BlockSpec((1, tr, c), lambda i: (i // nt, i % nt, 0))
    return _call(
        body, comms, name=name, grid=(nl * nt,),
        in_specs=[rspec(ll) for ll in range(nl)] + [blk, blk, blk],
        out_specs=[blk] * 4,
        out_shape=[jax.ShapeDtypeStruct((nl, r, c), F32)] * 4,
        args=(*recvs, w, m, v))


def _tile_for(s, want):
    return min(want, s)


REPLICATED = ("norm_g", "b_ada", "rg_conv_b", "rg_w_a", "rg_b_a", "rg_w_x", "rg_b_x", "rg_lambda", "ml_conv_b",
              "ml_b_if", "ml_norm_g", "final_g")


def _small_pack(rg_conv_w, ml_conv_w, ml_w_if):
    nl = rg_conv_w.shape[0]
    wif_t = jnp.swapaxes(ml_w_if, 1, 2).reshape(nl, -1, LANES)
    return jnp.concatenate([rg_conv_w, ml_conv_w, wif_t], axis=1)


def _small_unpack(p, if_rows):
    nl = p.shape[0]
    rg_cw = p[:, 0:CONV_WIDTH]
    ml_cw = p[:, CONV_WIDTH:2 * CONV_WIDTH]
    wif = jnp.swapaxes(p[:, 2 * CONV_WIDTH:].reshape(nl, 8, if_rows), 1, 2)
    return rg_cw, ml_cw, wif


def _qkv_slots(g_qkv, nd):
    three, nh, dh, _ = g_qkv.shape
    return g_qkv.reshape(three, nh, nd, dh // nd, dh).transpose(2, 0, 1, 3, 4).reshape(nd, three * nh * (dh // nd), dh)


def _small_slots(g):
    nd = N_DEV
    cw = jnp.stack([g["rg_conv_w"], g["ml_conv_w"]]).reshape(2, CONV_WIDTH, nd, LANES).transpose(2, 0, 1, 3)
    cw = cw.reshape(nd, 2 * CONV_WIDTH, LANES)
    wif = g["wif_t"].reshape(8, nd, -1).transpose(1, 0, 2).reshape(nd, -1, LANES)
    return jnp.concatenate([cw, wif], axis=1)


def _slot(block):
    return 4 * block[0] + 2 * block[1] + block[2]


def _dma_sems(*counts):
    return [pltpu.SemaphoreType.DMA((n,)) for n in counts]


def _start_all(copies):
    for cp in copies:
        cp.start()


def _gather_ici_comm(arrs):
    n = len(arrs)

    def copies(ins, outs, sems):
        send_sems, recv_sems, local_sems = sems
        x, y, c, sibling, chips = _mesh_place()
        me = (x, y, c)
        peers = [(*chip, c) for chip in chips] + [sibling]
        local = [pltpu.make_async_copy(ins[kk], outs[kk].at[_slot(me)], local_sems.at[kk]) for kk in range(n)]
        sends = [_remote(ins[kk], outs[kk].at[_slot(me)], send_sems, recv_sems, kk * 4 + j, peer)
                 for j, peer in enumerate(peers) for kk in range(n)]
        recvs = [_remote(ins[kk], outs[kk].at[_slot(peer)], send_sems, recv_sems, kk * 4 + j, peer)
                 for j, peer in enumerate(peers) for kk in range(n)]
        return local, sends, recvs

    def start(ins, outs, sems):
        local, sends, _ = copies(ins, outs, sems)
        _start_all(sends + local)

    def finish(ins, outs, sems):
        local, sends, recvs = copies(ins, outs, sems)
        for cp in recvs:
            cp.wait_recv()
        for cp in sends:
            cp.wait_send()
        for cp in local:
            cp.wait()

    return _Comm(arrs, [jax.ShapeDtypeStruct((N_DEV,) + a.shape, a.dtype) for a in arrs], _dma_sems(4 * n, 4 * n, n),
                 start, finish)


def _gather_fwd_comm(bufs):
    n = len(bufs)

    def copies(ins, outs, sems):
        send_sems, recv_sems = sems
        _, _, c, sibling, chips = _mesh_place()
        sends = [_remote(ins[kk].at[_slot((*chip, c))], outs[kk].at[_slot((*chip, c))], send_sems, recv_sems, kk * 3 + j, sibling)
                 for j, chip in enumerate(chips) for kk in range(n)]
        recvs = [_remote(ins[kk].at[_slot((*chip, c))], outs[kk].at[_slot((*chip, 1 - c))], send_sems, recv_sems, kk * 3 + j, sibling)
                 for j, chip in enumerate(chips) for kk in range(n)]
        return sends, recvs

    def start(ins, outs, sems):
        _start_all(copies(ins, outs, sems)[0])

    def finish(ins, outs, sems):
        sends, recvs = copies(ins, outs, sems)
        for cp in recvs:
            cp.wait_recv()
        for cp in sends:
            cp.wait_send()

    return _Comm(bufs, [jax.ShapeDtypeStruct(a.shape, a.dtype) for a in bufs], _dma_sems(3 * n, 3 * n), start, finish,
                 aliases=[(i, i) for i in range(n)])


def _core_swap_comm(arrs):
    n = len(arrs)

    def copies(ins, outs, sems):
        send_sems, recv_sems = sems
        _, _, c, sibling, _ = _mesh_place()
        return [_remote(ins[kk].at[2 * q + (1 - c)], outs[kk].at[q], send_sems, recv_sems, kk * N_CHIPS + q, sibling)
                for q in range(N_CHIPS) for kk in range(n)]

    def start(ins, outs, sems):
        _start_all(copies(ins, outs, sems))

    def finish(ins, outs, sems):
        cps = copies(ins, outs, sems)
        for cp in cps:
            cp.wait_recv()
        for cp in cps:
            cp.wait_send()

    return _Comm(arrs, [jax.ShapeDtypeStruct((N_CHIPS,) + a.shape[1:], a.dtype) for a in arrs],
                 _dma_sems(N_CHIPS * n, N_CHIPS * n), start, finish)


def _chip_swap_comm(arrs):
    n = len(arrs)
    per = N_CHIPS - 1

    def copies(ins, outs, sems):
        send_sems, recv_sems, local_sems = sems
        x, y, c, _, chips = _mesh_place()
        mine = 2 * x + y
        sends = [_remote(ins[kk].at[2 * chip[0] + chip[1]], outs[kk].at[mine], send_sems, recv_sems, kk * per + j, (*chip, c))
                 for j, chip in enumerate(chips) for kk in range(n)]
        recvs = [_remote(ins[kk].at[mine], outs[kk].at[2 * chip[0] + chip[1]], send_sems, recv_sems, kk * per + j, (*chip, c))
                 for j, chip in enumerate(chips) for kk in range(n)]
        local = [pltpu.make_async_copy(ins[kk].at[mine], outs[kk].at[mine], local_sems.at[kk]) for kk in range(n)]
        return local, sends, recvs

    def start(ins, outs, sems):
        local, sends, _ = copies(ins, outs, sems)
        _start_all(sends + local)

    def finish(ins, outs, sems):
        local, sends, recvs = copies(ins, outs, sems)
        for cp in recvs:
            cp.wait_recv()
        for cp in sends:
            cp.wait_send()
        for cp in local:
            cp.wait()

    return _Comm(arrs, [jax.ShapeDtypeStruct(a.shape, a.dtype) for a in arrs], _dma_sems(per * n, per * n, n), start, finish)


def _ada_mod(c_all, w_ada, b_cols, comms=None):
    nl, d, w = w_ada.shape

    def body(c_ref, w_ref, b_ref, m_ref, ca_ref):
        sub = _iota((SUBLANES, d), 0)
        cv = jnp.zeros((SUBLANES, d), F32)
        for b in range(N_DEV):
            cv = jnp.where(sub == b, c_ref[b], cv)
        ca = cv * _sigmoid(cv)
        ca_ref[...] = ca
        m_ref[...] = jnp.zeros_like(m_ref)
        for l in range(nl):
            ml = _mm_hi(ca, w_ref[l]) + b_ref[l:l + 1, :]
            for b in range(N_DEV):
                m_ref[b, l:l + 1, :] = _row(ml, b)

    return _call(
        body, comms, name="adaln_mod_columns", grid=(1,),
        in_specs=[_full(c_all.shape), _full(w_ada.shape), _full(b_cols.shape)],
        out_specs=[_full((N_DEV, SUBLANES, w)), _full((SUBLANES, d))],
        out_shape=[jax.ShapeDtypeStruct((N_DEV, SUBLANES, w), F32), jax.ShapeDtypeStruct((SUBLANES, d), F32)],
        args=(c_all, w_ada, b_cols))


def _ada_grad_adam(cact_t, dmods, w, m, v, comms=None):
    nl, d, wd = w.shape
    tr = _row_tile(d, wd, 8)
    nt = d // tr

    def body(c_ref, dm_ref, w_ref, m_ref, v_ref, g_ref, d_ref, mo_ref, vo_ref):
        cv = c_ref[...]
        dm = dm_ref[0]
        g = _col(cv, 0) * _row(dm, 0)
        for b in range(1, N_DEV):
            g = g + _col(cv, b) * _row(dm, b)
        delta, m2, v2 = _adam_math(w_ref[0], g, m_ref[0], v_ref[0])
        g_ref[0] = g
        d_ref[0] = delta
        mo_ref[0] = m2
        vo_ref[0] = v2

    blk = pl.BlockSpec((1, tr, wd), lambda i: (i // nt, i % nt, 0))
    return _call(
        body, comms, name="adaln_grad_adam", grid=(nl * nt,),
        in_specs=[pl.BlockSpec((tr, N_DEV), lambda i: (i % nt, 0)), pl.BlockSpec((1, N_DEV, wd), lambda i: (i // nt, 0, 0)),
                  blk, blk, blk],
        out_specs=[blk] * 4, out_shape=[jax.ShapeDtypeStruct((nl, d, wd), F32)] * 4,
        args=(cact_t, dmods, w, m, v))


REP_ROWS = ("norm_g", "dshift", "dscale", "dgate", "rg_conv_b", "rg_b_a", "rg_b_x", "rg_lambda", "ml_conv_b", "ml_norm_g",
            "ml_b_if")


def _sum_parts(recvs, name):
    def body(*refs):
        for r_ref, o_ref in zip(refs[:len(recvs)], refs[len(recvs):]):
            o_ref[...] = _sum_devices(r_ref).astype(o_ref.dtype)

    return pl.pallas_call(
        body, name=name, grid=(1,),
        in_specs=[_full(r.shape) for r in recvs], out_specs=[_full(r.shape[1:]) for r in recvs],
        out_shape=[jax.ShapeDtypeStruct(r.shape[1:], r.dtype) for r in recvs], compiler_params=_params(1),
    )(*recvs)


def _adam_replicated(vp, mp, params, nl):
    d = vp.shape[2]
    nr = len(REP_ROWS)
    names = list(params)
    mat_shape = params["rg_w_a"][0].shape[1:]
    mat_rows = mp.shape[0] // (2 * nl)

    def pieces(name):
        if name == "final_g":
            return [(lambda vp_ref, mp_ref: vp_ref[nl * nr:nl * nr + 1, :], (slice(0, 1), slice(None)))]
        out = []
        for l in range(nl):
            if name in ("rg_w_a", "rg_w_x"):
                at = (2 * l + (name == "rg_w_x")) * mat_rows
                out.append((lambda vp_ref, mp_ref, at=at: mp_ref[at:at + mat_rows, :].astype(F32).reshape(mat_shape), l))
            elif name == "b_ada":
                for j in range(3):
                    r = l * nr + 1 + j
                    out.append((lambda vp_ref, mp_ref, r=r: vp_ref[r:r + 1, :], (slice(l, l + 1), slice(j * d, (j + 1) * d))))
            else:
                r = l * nr + REP_ROWS.index(name)
                cols = slice(0, LANES) if name == "ml_b_if" else slice(None)
                out.append((lambda vp_ref, mp_ref, r=r, cols=cols: vp_ref[r:r + 1, cols], (slice(l, l + 1), slice(None))))
        return out

    def body(*refs):
        parts_ref, mp_ref, vp_ref = refs[0], refs[1], refs[-1]
        ins, outs = refs[2:2 + 3 * len(names)], refs[2 + 3 * len(names):-1]
        vp_ref[...] = _sum_devices(parts_ref)
        for pi, name in enumerate(names):
            w_ref, m_ref, v_ref = ins[3 * pi:3 * pi + 3]
            g_ref, d_ref, mo_ref, vo_ref = outs[4 * pi:4 * pi + 4]
            for get, idx in pieces(name):
                g = get(vp_ref, mp_ref)
                delta, m2, v2 = _adam_math(w_ref[idx], g, m_ref[idx], v_ref[idx])
                g_ref[idx] = g
                d_ref[idx] = delta
                mo_ref[idx] = m2
                vo_ref[idx] = v2

    flat = [a for name in names for a in params[name]]
    out_shape = [jax.ShapeDtypeStruct(params[name][0].shape, F32) for name in names for _ in range(4)]
    out_shape.append(jax.ShapeDtypeStruct(vp.shape[1:], F32))
    res = pl.pallas_call(
        body, name="adam_replicated", grid=(1,),
        in_specs=[_full(vp.shape), _full(mp.shape)] + [_full(a.shape) for a in flat],
        out_specs=[_full(o.shape) for o in out_shape], out_shape=out_shape, compiler_params=_params(1),
    )(vp, mp, *flat)
    return {name: res[4 * pi:4 * pi + 4] for pi, name in enumerate(names)}, res[-1]


class _Plan:
    def __init__(self):
        self.hosted, self.after = {}, {}

    def host(self, key, comm, then=None):
        self.hosted.setdefault(key, []).append(comm)
        if then is not None:
            self.after.setdefault(key, []).append(then)

    def comms(self, key):
        return self.hosted.pop(key, None)

    def done(self, key):
        for fn in self.after.pop(key, []):
            fn()

    def flush(self):
        while self.hosted:
            key = next(iter(self.hosted))
            _call(lambda: None, self.comms(key), name="exchange_after_%s_%d" % key, grid=(1,), in_specs=[], out_specs=[],
                  out_shape=[], args=())
            self.done(key)


VEC_TABLE = ("norm_g", "rg_conv_b", "rg_b_a", "rg_b_x", "rg_lambda", "ml_conv_b", "ml_norm_g")


def _vec_table(rep):
    rows = [rep[n] for n in VEC_TABLE]
    return jnp.stack(rows + [jnp.zeros_like(rows[0])] * (SUBLANES - len(rows)), axis=1)


def _layer_fwd(l, xl, mod3, wl, rep, plan):
    s, d = xl.shape
    t_big, t_mid = _tile_for(s, 512), _tile_for(s, 256)
    nh_ml = rep["ml_b_if"].shape[1] // 2
    vec = lambda name: _vec(rep["vecs"], l, VEC_TABLE.index(name))
    shift, scale, gate = (_vec(mod3, l, kk) for kk in range(3))
    hosted = lambda name: plan.comms((name, l)) if plan else None
    done = lambda name: plan.done((name, l)) if plan else None
    u, hbf = _in_fwd(xl, vec("norm_g"), scale, shift, wl["w_in_g"], 0, t_mid, hosted("in_proj_fwd"))
    done("in_proj_fwd")
    h_rg, y_rg, *rg_gates = _rg_fwd(u, d, wl["rg_conv_w"], vec("rg_conv_b"), rep["rg_w_a_bf"][l], vec("rg_b_a"),
                                    rep["rg_w_x_bf"][l], vec("rg_b_x"), vec("rg_lambda"), t_mid, hosted("rglru_fwd"))
    done("rglru_fwd")
    q, k, v, gcol, pre = _ml_pre(u, d, wl["ml_conv_w"], vec("ml_conv_b"), wl["w_qkv"][0], wl["w_qkv"][1],
                                 wl["w_qkv"][2], wl["wif_pad"], wl["bif_pad"], t_mid, hosted("mlstm_proj_fwd"))
    done("mlstm_proj_fwd")
    grow = gcol[:, 0:16].T
    cell, y_ml, cs, ns, ms, mt = _ml_cell_fwd(q, k, v, gcol, grow, u, vec("ml_norm_g"), nh_ml, hosted("mlstm_cell_fwd"))
    done("mlstm_cell_fwd")
    x_new, y = _out_fwd(xl, y_rg, y_ml, gate, wl["w_out_g"], 0, t_big, hosted("out_proj_fwd"))
    done("out_proj_fwd")
    saved = dict(x=xl, u=u, hbf=hbf, h_rg=h_rg, y_rg=y_rg, q=q, k=k, v=v, gcol=gcol, grow=grow, cell=cell, y_ml=y_ml,
                 cs=cs, ns=ns, ms=ms, mt=mt, y=y, scale=scale, gate=gate, rg_gates=rg_gates, pre=pre)
    return x_new, saved


def _layer_bwd(l, dx, sv, wl, rep, plan, grads=None, split_last=False):
    s, d = dx.shape
    t_big, t_mid = _tile_for(s, 512), _tile_for(s, 256)
    nh_ml = rep["ml_b_if"].shape[1] // 2
    nd, _, _, w_cols = wl["w_in_g"].shape
    grads = {} if grads is None else grads
    vec = lambda name: _vec(rep["vecs"], l, VEC_TABLE.index(name))
    hosted = lambda name: plan.comms((name, l)) if plan else None
    done = lambda name: plan.done((name, l)) if plan else None
    dy_rg, dy_ml, gw_out, dgate = _out_bwd(dx, sv["gate"], sv["y"], sv["y_rg"], sv["y_ml"], wl["w_out_g"], 0, t_big,
                                           hosted("out_proj_bwd"))
    grads.update(w_out=gw_out)
    done("out_proj_bwd")
    dq, dk, dv, dgates, d_mlo, d_mlz, g_mlng = _ml_cell_bwd(
        dy_ml, sv["u"], sv["cell"], sv["q"], sv["k"], sv["v"], sv["gcol"], sv["grow"], sv["mt"], sv["cs"], sv["ns"],
        sv["ms"], vec("ml_norm_g"), nh_ml, hosted("mlstm_cell_bwd"))
    done("mlstm_cell_bwd")
    d_mlx, g_wq, g_wk, g_wv, g_wift, g_bif, g_mlcw, g_mlcb = _ml_pre_bwd(
        dq, dk, dv, dgates, sv["gcol"], sv["u"], sv["pre"], sv["q"], sv["k"], sv["v"], wl["ml_conv_w"],
        wl["w_qkv"][0], wl["w_qkv"][1], wl["w_qkv"][2], wl["wift_pad"], t_mid, hosted("mlstm_proj_bwd"))
    done("mlstm_proj_bwd")
    d_rgx, d_rgz, g_wa, g_wx, g_ba, g_bx, g_lam, g_rgcw, g_rgcb = _rg_bwd(
        dy_rg, sv["u"], sv["h_rg"], sv["rg_gates"], wl["rg_conv_w"], rep["rg_w_a_bf"][l], rep["rg_w_x_bf"][l],
        vec("rg_lambda"), t_mid, hosted("rglru_bwd"))
    grads.update(w_qkv=jnp.stack([g_wq, g_wk, g_wv]), rg_conv_w=g_rgcw[0:CONV_WIDTH], ml_conv_w=g_mlcw[0:CONV_WIDTH],
                 wif_t=g_wift[0:8], rg_w_a=g_wa, rg_w_x=g_wx)
    acc = dict(dgate=dgate, rg_conv_b=g_rgcb, rg_b_a=g_ba, rg_b_x=g_bx, rg_lambda=g_lam, ml_conv_b=g_mlcb,
               ml_b_if=g_bif, ml_norm_g=g_mlng)
    done("rglru_bwd")
    pieces = [d_rgx, d_rgz, d_mlx, d_mlo, d_mlz]
    grads.update(w_in=_in_bwd_w(pieces, sv["hbf"], w_cols, tuple(range(nd)), t_big, hosted("in_proj_bwd_w")))
    done("in_proj_bwd_w")
    n_tiles = s // t_mid
    counts = [n_tiles // 5, n_tiles - n_tiles // 5 - 1, 1] if split_last and n_tiles >= 5 else [n_tiles]
    in_args = (pieces, sv["x"], dx, vec("norm_g"), sv["scale"], wl["w_in_g"], 0, t_mid)
    res, at = None, 0
    for key, count in zip(("in_proj_bwd_x", "in_proj_bwd_x_rest", "in_proj_bwd_x_end"), counts):
        res = _in_bwd(*in_args, hosted(key), (at, count), res)
        done(key)
        at += count
    dx, dscale, dshift, g_ng = res
    acc.update(norm_g=g_ng, dshift=dshift, dscale=dscale)
    grads.update(acc=acc, dmod=jnp.concatenate([dshift[0:1], dscale[0:1], dgate[0:1]], axis=1))
    return dx, grads


def _full_qkv(qkv_g, d):
    nd, _, rows3, dh = qkv_g.shape
    nh = d // dh
    rsh = rows3 // (3 * nh)
    return qkv_g.reshape(nd, 3, nh, rsh, dh).transpose(1, 2, 0, 3, 4).reshape(3, nh, nd * rsh, dh)


def _small_weights(small, l, ml_b_if):
    nd = small.shape[0]
    sm = small[:, l]
    cw = sm[:, 0:2 * CONV_WIDTH].reshape(nd, 2, CONV_WIDTH, LANES).transpose(1, 2, 0, 3).reshape(2, CONV_WIDTH, nd * LANES)
    if_rows = (sm.shape[1] - 2 * CONV_WIDTH) * LANES // 8
    wif_t = sm[:, 2 * CONV_WIDTH:].reshape(nd, 8, if_rows).transpose(1, 0, 2).reshape(8, nd * if_rows)
    wift_pad = jnp.pad(wif_t, ((0, LANES - 8), (0, 0))).astype(BF16)
    return dict(rg_conv_w=cw[0], ml_conv_w=cw[1], wift_pad=wift_pad, wif_pad=wift_pad.T,
                bif_pad=jnp.pad(ml_b_if[l], (0, LANES - 8)).reshape(1, LANES))


def kernel(x, c, norm_g, w_ada, b_ada, w_in, rg_conv_w, rg_conv_b, rg_w_a, rg_b_a, rg_w_x, rg_b_x, rg_lambda, ml_conv_w, ml_conv_b, ml_w_q, ml_w_k, ml_w_v, ml_w_if, ml_b_if, ml_norm_g, w_out, final_g, loss_target, m_norm_g, m_w_ada, m_b_ada, m_w_in, m_rg_conv_w, m_rg_conv_b, m_rg_w_a, m_rg_b_a, m_rg_w_x, m_rg_b_x, m_rg_lambda, m_ml_conv_w, m_ml_conv_b, m_ml_w_q, m_ml_w_k, m_ml_w_v, m_ml_w_if, m_ml_b_if, m_ml_norm_g, m_w_out, m_final_g, v_norm_g, v_w_ada, v_b_ada, v_w_in, v_rg_conv_w, v_rg_conv_b, v_rg_w_a, v_rg_b_a, v_rg_w_x, v_rg_b_x, v_rg_lambda, v_ml_conv_w, v_ml_conv_b, v_ml_w_q, v_ml_w_k, v_ml_w_v, v_ml_w_if, v_ml_b_if, v_ml_norm_g, v_w_out, v_final_g):
    given = dict(locals())
    nl = w_in.shape[0]
    d = x.shape[2]
    rep = {n: given[n] for n in REPLICATED}
    rep.update(rg_w_a_bf=rg_w_a.astype(BF16), rg_w_x_bf=rg_w_x.astype(BF16))
    bf = lambda a: a.astype(BF16)

    def qkv_shard(prefix):
        return jnp.stack([given[prefix + "ml_w_q"], given[prefix + "ml_w_k"], given[prefix + "ml_w_v"]], axis=1).reshape(
            nl, -1, ml_w_q.shape[-1])

    def small_shard(prefix):
        return _small_pack(given[prefix + "rg_conv_w"], given[prefix + "ml_conv_w"], given[prefix + "ml_w_if"])

    plan = _Plan()
    qkv = qkv_shard("")
    first_ici = _gather_ici_comm([bf(w_in[0:1]), small_shard("")])
    condition = _exchange_comm([jnp.broadcast_to(c, (SUBLANES, d))], True)
    _run_comms([first_ici, condition], "gather_first")
    first_fwd = _gather_fwd_comm(first_ici.results)
    wcols = w_ada.shape[2]
    me = 4 * lax.axis_index("x") + 2 * lax.axis_index("y") + lax.axis_index("c")
    b_cols = jnp.pad(lax.dynamic_slice_in_dim(b_ada, me * wcols, wcols, axis=1), ((0, SUBLANES - nl), (0, 0)))
    mod_cols, cact_all = _ada_mod(condition.results[0], w_ada, b_cols, [first_fwd])
    w_in_first, small = first_fwd.results
    wl = [_small_weights(small, l, ml_b_if) for l in range(nl)]
    wl[0]["w_in_g"] = w_in_first

    def gather_behind(arrs, ici_host, fwd_host, then):
        ici = _gather_ici_comm(arrs)

        def pass_on():
            fwd = _gather_fwd_comm(ici.results)
            plan.host(fwd_host, fwd, lambda: then(fwd.results))

        plan.host(ici_host, ici, pass_on)

    def got_out(l):
        return lambda r: wl[l].update(w_out_g=r[0], w_qkv=_full_qkv(r[1], d))

    gather_behind([bf(w_out[0:1]), bf(qkv[0:1])], ("in_proj_fwd", 0), ("rglru_fwd", 0), got_out(0))
    for l in range(1, nl):
        gather_behind([bf(w_in[l:l + 1])], ("rglru_fwd", l - 1), ("mlstm_proj_fwd", l - 1),
                      lambda r, l=l: wl[l].update(w_in_g=r[0]))
        gather_behind([bf(w_out[l:l + 1]), bf(qkv[l:l + 1])], ("mlstm_cell_fwd", l - 1), ("out_proj_fwd", l - 1), got_out(l))

    mod_blocks = _exchange([mod_cols], False, "scatter_modulation")[0]
    mod3 = mod_blocks[:, 0:nl].transpose(1, 0, 2).reshape(nl, 3, d)
    rep["vecs"] = _vec_table(rep)
    saved, xl = [], x[0]
    for l in range(nl):
        xl, sv = _layer_fwd(l, xl, mod3, wl[l], rep, plan)
        saved.append(sv)
    grad_x, loss_p, g_final = _loss_call(xl, final_g.reshape(1, -1), loss_target[0], _tile_for(xl.shape[0], 512))

    keys = ("w_in", "w_out", "w_qkv", "small")
    parity = lax.axis_index("c").astype(jnp.int32).reshape(1)
    grads, recv = [None] * nl, [None] * nl

    def parts_of(g):
        return [g["w_in"], g["w_out"], _qkv_slots(g["w_qkv"], N_DEV), _small_slots(g)]

    def pair_sums(l, parts, other):
        return [_pair_sum(a, o, parity, "pair_sum_%s_layer%d" % (key, l)) for key, a, o in zip(keys, parts, other)]

    def reduce_behind(l, host_layer):
        parts = parts_of(grads[l])
        swap = _core_swap_comm(parts)

        def summed():
            sums = pair_sums(l, parts, swap.results)
            big = _chip_swap_comm([sums[0]])
            rest = _chip_swap_comm(sums[1:])
            plan.host(("mlstm_cell_bwd", host_layer), big)
            plan.host(("rglru_bwd", host_layer), rest, lambda: recv.__setitem__(l, big.results + rest.results))

        plan.host(("out_proj_bwd", host_layer), swap, summed)

    first, own = {}, {}

    def reduce_own(names, parts_fn, ready_key, swap_key, chip_key):
        def go():
            parts = parts_fn()
            swap = _core_swap_comm(parts)

            def summed():
                sums = [_pair_sum(a, o, parity, "pair_sum_%s_layer0" % n) for n, a, o in zip(names, parts, swap.results)]
                chip = _chip_swap_comm(sums)
                plan.host(chip_key, chip, lambda: own.update(zip(names, chip.results)))

            plan.host(swap_key, swap, summed)

        plan.after.setdefault(ready_key, []).append(go)

    reduce_own(["w_out"], lambda: [first["w_out"]], ("out_proj_bwd", 0), ("mlstm_cell_bwd", 0), ("mlstm_proj_bwd", 0))
    reduce_own(["w_qkv", "small"], lambda: [_qkv_slots(first["w_qkv"], N_DEV), _small_slots(first)],
               ("rglru_bwd", 0), ("in_proj_bwd_w", 0), ("in_proj_bwd_x", 0))
    reduce_own(["w_in"], lambda: [first["w_in"]], ("in_proj_bwd_w", 0), ("in_proj_bwd_x", 0), ("in_proj_bwd_x_rest", 0))

    matrices = {}

    def reduce_matrices():
        layers = [grads[l] if l > 0 else first for l in range(nl)]
        mp = jnp.stack([jnp.stack([g["rg_w_a"], g["rg_w_x"]]) for g in layers]).reshape(N_DEV, -1, LANES).astype(BF16)
        scatter = _exchange_comm([mp], False)

        def summed():
            gather = _exchange_comm(_sum_parts(scatter.results, "sum_replicated_matrices"), True)
            plan.host(("in_proj_bwd_x", 0), gather, lambda: matrices.update(mp=gather.results[0].reshape(-1, LANES)))

        plan.host(("in_proj_bwd_w", 0), scatter, summed)

    plan.after.setdefault(("rglru_bwd", 0), []).append(reduce_matrices)

    for l in reversed(range(nl)):
        if l > 0:
            grad_x, grads[l] = _layer_bwd(l, grad_x, saved[l], wl[l], rep, plan)
            reduce_behind(l, l - 1)
        else:
            grad_x, grads[l] = _layer_bwd(l, grad_x, saved[l], wl[l], rep, plan, first, True)
    plan.flush()
    recv[0] = [own[key] for key in keys]

    shard = {p: dict(w_in=given[p + "w_in"], w_out=given[p + "w_out"], w_qkv=qkv_shard(p), small=small_shard(p))
             for p in ("", "m_", "v_")}
    res = {}
    for ki, key in enumerate(keys):
        res[key] = _reduce_adam([recv[l][ki] for l in range(nl)], shard[""][key], shard["m_"][key], shard["v_"][key],
                                "reduce_adam_" + key)

    dmods = jnp.concatenate([grads[l]["dmod"] for l in range(nl)], axis=0)
    dmod_blocks = jnp.pad(dmods.reshape(nl, N_DEV, wcols).transpose(1, 0, 2), ((0, 0), (0, SUBLANES - nl), (0, 0)))
    dmod_all = _exchange([dmod_blocks], False, "scatter_dmod")[0][:, 0:nl].transpose(1, 0, 2)
    res["w_ada"] = _ada_grad_adam(cact_all.T, dmod_all, w_ada, m_w_ada, v_w_ada)

    widen = lambda a: jnp.pad(a, ((0, 0), (0, d - a.shape[1])))
    rows = [widen(grads[l]["acc"][n][0:1]) for l in range(nl) for n in REP_ROWS] + [g_final[0:1], widen(loss_p[0:1])]
    vp = jnp.concatenate(rows + [jnp.zeros(((-len(rows)) % SUBLANES, d), F32)], axis=0)
    vp_all = _exchange([vp], True, "gather_replicated")[0]
    mp_r = matrices["mp"]
    lanes = lambda a: jnp.pad(a, ((0, 0), (0, LANES - a.shape[1])))
    shaped = dict(ml_b_if=lanes, final_g=lambda a: a.reshape(1, d))
    names = [n for n in REPLICATED if n != "b_ada"] + ["b_ada"]
    rep_res, vp_r = _adam_replicated(vp_all, mp_r, {n: tuple(shaped.get(n, lambda a: a)(given[p + n]) for p in ("", "m_", "v_"))
                                                    for n in names}, nl)
    unshaped = dict(ml_b_if=lambda a: a[:, 0:ml_b_if.shape[1]], final_g=lambda a: a.reshape(d))
    rep_out = [{n: unshaped.get(n, lambda a: a)(rep_res[n][kind]) for n in names} for kind in range(4)]
    loss = vp_r[nl * len(REP_ROWS) + 1, 0]

    if_rows = ml_w_if.shape[1]
    order = ("norm_g", "w_ada", "b_ada", "w_in", "rg_conv_w", "rg_conv_b", "rg_w_a", "rg_b_a", "rg_w_x", "rg_b_x",
             "rg_lambda", "ml_conv_w", "ml_conv_b", "ml_w_q", "ml_w_k", "ml_w_v", "ml_w_if", "ml_b_if", "ml_norm_g",
             "w_out", "final_g")
    outs = [loss, grad_x[None]]
    for kind in range(4):
        qkv_k = res["w_qkv"][kind].reshape((nl, 3) + ml_w_q.shape[1:])
        rg_cw, ml_cw, wif = _small_unpack(res["small"][kind], if_rows)
        sharded = dict(w_ada=res["w_ada"][kind], w_in=res["w_in"][kind], w_out=res["w_out"][kind], ml_w_q=qkv_k[:, 0],
                       ml_w_k=qkv_k[:, 1], ml_w_v=qkv_k[:, 2], rg_conv_w=rg_cw, ml_conv_w=ml_cw, ml_w_if=wif)
        for n in order:
            outs.append(sharded[n] if n in sharded else rep_out[kind][n])
    return tuple(outs)
```

```python
import functools

import jax
import jax.numpy as jnp
from jax import lax
from jax.experimental import pallas as pl
from jax.experimental.pallas import tpu as pltpu

F32 = jnp.float32
BF16 = jnp.bfloat16
MESH_AXES = ("x", "y", "c")
N_DEV = 8
EPS = 1e-6
RG_C = 8.0
ML_CHUNK = 128
CONV_WIDTH = 4
ADAM_LR = 0.001
ADAM_B1 = 0.9
ADAM_B2 = 0.999
ADAM_EPS = 1e-08
ADAM_WD = 0.01
ADAM_STEP = 10
NEG_BIG = -1e30
LANES = 128
SUBLANES = 8
VMEM_LIMIT = 56 * 1024 * 1024
HI = lax.Precision.HIGHEST


def _params(n_grid):
    return pltpu.CompilerParams(dimension_semantics=("arbitrary",) * n_grid, vmem_limit_bytes=VMEM_LIMIT)


def _mm(a, b):
    return jnp.dot(a.astype(BF16), b.astype(BF16), preferred_element_type=F32)


def _mm_nt(a, b):
    return lax.dot_general(a.astype(BF16), b.astype(BF16), (((1,), (1,)), ((), ())), preferred_element_type=F32)


def _mm_tn(a, b):
    return lax.dot_general(a.astype(BF16), b.astype(BF16), (((0,), (0,)), ((), ())), preferred_element_type=F32)


def _mm_hi(a, b):
    return jnp.dot(a, b, precision=HI, preferred_element_type=F32)


def _sigmoid(x):
    return 1.0 / (1.0 + jnp.exp(-x))


def _softplus(x):
    return jnp.maximum(x, 0.0) + jnp.log(1.0 + jnp.exp(-jnp.abs(x)))


def _neg_expm1(x):
    poly = -x * (1.0 + x * (0.5 + x * (1.0 / 6.0 + x * (1.0 / 24.0 + x * (1.0 / 120.0)))))
    return jnp.where(jnp.abs(x) < 0.05, poly, 1.0 - jnp.exp(x))


def _iota(shape, dim):
    return lax.broadcasted_iota(jnp.int32, shape, dim)


def _colsum(x):
    return jnp.sum(x, axis=0, keepdims=True)


def _rowsum(x):
    return jnp.sum(x, axis=1, keepdims=True)


def _col(x, j):
    return _rowsum(jnp.where(_iota(x.shape, 1) == j, x, 0.0))


def _row(x, j):
    return _colsum(jnp.where(_iota(x.shape, 0) == j, x, 0.0))


def _shift_down(x, j, prev8):
    if j == 0:
        return x
    t = x.shape[0]
    main = jnp.where(_iota(x.shape, 0) >= j, pltpu.roll(x, j, 0), 0.0)
    fix = jnp.where(_iota(prev8.shape, 0) < j, pltpu.roll(prev8, j, 0), 0.0)
    return jnp.concatenate([main[0:SUBLANES] + fix, main[SUBLANES:t]], axis=0)


def _shift_up(x, j, next8):
    if j == 0:
        return x
    t = x.shape[0]
    main = jnp.where(_iota(x.shape, 0) < t - j, pltpu.roll(x, t - j, 0), 0.0)
    fix = jnp.where(_iota(next8.shape, 0) >= SUBLANES - j, pltpu.roll(next8, SUBLANES - j, 0), 0.0)
    return jnp.concatenate([main[0:t - SUBLANES], main[t - SUBLANES:t] + fix], axis=0)


def _conv(x, prev8, w_ref):
    y = w_ref[CONV_WIDTH - 1:CONV_WIDTH, :] * x
    for j in range(1, CONV_WIDTH):
        y = y + w_ref[CONV_WIDTH - 1 - j:CONV_WIDTH - j, :] * _shift_down(x, j, prev8)
    return y


def _conv_bwd(dy, x, next8, w_ref, gw_ref):
    dx = None
    for j in range(CONV_WIDTH):
        k = CONV_WIDTH - 1 - j
        up = _shift_up(dy, j, next8)
        gw_ref[k:k + 1, :] += _colsum(up * x)
        term = w_ref[k:k + 1, :] * up
        dx = term if dx is None else dx + term
    return dx


def _scan_into(a, b, carry, out_ref, reverse):
    t, c = a.shape
    groups = t // SUBLANES
    a3 = a.reshape(groups, SUBLANES, c)
    b3 = b.reshape(groups, SUBLANES, c)
    sub = _iota(a3.shape, 1)
    for step in (1, 2, 4):
        keep = sub < SUBLANES - step if reverse else sub >= step
        shift = SUBLANES - step if reverse else step
        a_s = jnp.where(keep, pltpu.roll(a3, shift, 1), 1.0)
        b_s = jnp.where(keep, pltpu.roll(b3, shift, 1), 0.0)
        b3 = a3 * b_s + b3
        a3 = a3 * a_s
    for g in (reversed(range(groups)) if reverse else range(groups)):
        rows = slice(g * SUBLANES, (g + 1) * SUBLANES)
        out_ref[rows, :] = b3[g] + a3[g] * carry
        edge = g * SUBLANES if reverse else (g + 1) * SUBLANES - 1
        carry = out_ref[edge:edge + 1, :]


def _blockdiag(x, w_ref, transpose_w=False):
    nh, dh, _ = w_ref.shape
    outs = []
    for h in range(nh):
        xs = x[:, h * dh:(h + 1) * dh]
        outs.append(_mm_nt(xs, w_ref[h]) if transpose_w else _mm(xs, w_ref[h]))
    return jnp.concatenate(outs, axis=1)


def _rg_gates(xc, wa_ref, ba_ref, wx_ref, bx_ref, lam_ref):
    r = _sigmoid(_blockdiag(xc, wa_ref) + ba_ref[...])
    ig = _sigmoid(_blockdiag(xc, wx_ref) + bx_ref[...])
    sp = _softplus(-lam_ref[...])
    log_a = -RG_C * r * sp
    a = jnp.exp(log_a)
    beta = jnp.sqrt(_neg_expm1(2.0 * log_a))
    return r, ig, sp, a, beta


def _bcast8(row):
    return jnp.broadcast_to(row, (SUBLANES, row.shape[1]))


def _full(shape):
    nd = len(shape)
    return pl.BlockSpec(shape, lambda *_: (0,) * nd)


class _Comm:
    def __init__(self, arrays, out_shapes, sems, start, finish, aliases=()):
        self.arrays, self.out_shapes, self.sems = list(arrays), list(out_shapes), list(sems)
        self.start, self.finish, self.aliases = start, finish, tuple(aliases)
        self.results = None


class _RowOf:
    def __init__(self, ref, k):
        self.ref, self.k = ref, k

    def __getitem__(self, idx):
        cols = slice(None) if idx is Ellipsis else idx[1]
        return self.ref[0, self.k:self.k + 1, cols]


class _PartOf:
    def __init__(self, ref, rows=None, cols=None, lead=None):
        self.ref, self.rows, self.cols, self.lead = ref, rows, cols, lead
        if rows is not None:
            self.shape = (rows.stop - rows.start,) + tuple(ref.shape[1:])

    def _at(self, idx):
        if self.lead is not None:
            return (self.lead,) + tuple(idx[1:])
        if self.cols is not None:
            return (slice(None), self.cols)
        return (self.rows, slice(None) if idx is Ellipsis else idx[1])

    def __getitem__(self, idx):
        return self.ref[self._at(idx)]

    def __setitem__(self, idx, value):
        self.ref[self._at(idx)] = value


def _vec(table, layer, k):
    return ("row", table, layer, k)


def _is_row(arg):
    return isinstance(arg, tuple) and len(arg) == 4 and arg[0] == "row"


def _call(body, comms, *, name, grid, in_specs, out_specs, out_shape, args, scratch_shapes=(), aliases=None):
    comms = [cm for cm in (comms or []) if cm is not None]
    rows = {i: a[3] for i, a in enumerate(args) if _is_row(a)}
    in_specs = [pl.BlockSpec((1,) + a[1].shape[1:], functools.partial(lambda layer, *_: (layer, 0, 0), a[2]))
                if _is_row(a) else sp for a, sp in zip(args, in_specs)]
    args = tuple(a[1] if _is_row(a) else a for a in args)
    n_in, n_out, n_sc = len(args), len(out_shape), len(scratch_shapes)
    c_arrays = [a for cm in comms for a in cm.arrays]
    c_outs = [o for cm in comms for o in cm.out_shapes]
    c_sems = [sm for cm in comms for sm in cm.sems]
    aliases, a_at, o_at = dict(aliases or {}), n_in, n_out
    for cm in comms:
        for (i, j) in cm.aliases:
            aliases[a_at + i] = o_at + j
        a_at += len(cm.arrays)
        o_at += len(cm.out_shapes)

    def wrapped(*refs):
        ins, c_in = refs[:n_in], refs[n_in:n_in + len(c_arrays)]
        ins = [_RowOf(r, rows[i]) if i in rows else r for i, r in enumerate(ins)]
        at = n_in + len(c_arrays)
        outs, c_out = refs[at:at + n_out], refs[at + n_out:at + n_out + len(c_outs)]
        at += n_out + len(c_outs)
        scr, sems = refs[at:at + n_sc], refs[at + n_sc:]
        views, ia, io, isem = [], 0, 0, 0
        for cm in comms:
            views.append((c_in[ia:ia + len(cm.arrays)], c_out[io:io + len(cm.out_shapes)], sems[isem:isem + len(cm.sems)]))
            ia, io, isem = ia + len(cm.arrays), io + len(cm.out_shapes), isem + len(cm.sems)
        if comms:
            @pl.when(pl.program_id(0) == 0)
            def _():
                for cm, view in zip(comms, views):
                    cm.start(*view)

        body(*ins, *outs, *scr)
        if comms:
            @pl.when(pl.program_id(0) == grid[0] - 1)
            def _():
                for cm, view in zip(comms, views):
                    cm.finish(*view)

    hbm = pl.BlockSpec(memory_space=pl.ANY)
    res = pl.pallas_call(
        wrapped, name=name, grid=grid,
        in_specs=list(in_specs) + [hbm] * len(c_arrays), out_specs=list(out_specs) + [hbm] * len(c_outs),
        out_shape=list(out_shape) + c_outs, scratch_shapes=list(scratch_shapes) + c_sems,
        input_output_aliases=aliases, compiler_params=_params(len(grid)),
    )(*args, *c_arrays)
    at = n_out
    for cm in comms:
        cm.results = list(res[at:at + len(cm.out_shapes)])
        at += len(cm.out_shapes)
    return list(res[:n_out])


def _join_columns(w_ref, wcat):
    nd, _, _, w = w_ref.shape

    @pl.when(pl.program_id(0) == 0)
    def _():
        for j in range(nd):
            wcat[:, j * w:(j + 1) * w] = w_ref[j, 0]


def _in_fwd(x, ng, scale, shift, w_in_g, layer, tile, comms=None):
    s, d = x.shape
    nd, _, _, w = w_in_g.shape

    def body(x_ref, ng_ref, sc_ref, sh_ref, w_ref, u_ref, h_ref, wcat):
        _join_columns(w_ref, wcat)
        xv = x_ref[...]
        rs = lax.rsqrt(jnp.mean(xv * xv, axis=1, keepdims=True) + EPS)
        hb = (xv * rs * ng_ref[...] * (1.0 + sc_ref[...]) + sh_ref[...]).astype(BF16)
        h_ref[...] = hb
        u_ref[...] = jnp.dot(hb, wcat[...], preferred_element_type=F32)

    return _call(
        body, comms, name="in_proj_fwd", grid=(s // tile,),
        in_specs=[pl.BlockSpec((tile, d), lambda i: (i, 0)), _full((1, d)), _full((1, d)), _full((1, d)),
                  pl.BlockSpec((nd, 1, d, w), lambda i: (0, layer, 0, 0), pipeline_mode=pl.Buffered(1))],
        out_specs=[pl.BlockSpec((tile, nd * w), lambda i: (i, 0)), pl.BlockSpec((tile, d), lambda i: (i, 0))],
        out_shape=[jax.ShapeDtypeStruct((s, nd * w), F32), jax.ShapeDtypeStruct((s, d), BF16)],
        scratch_shapes=[pltpu.VMEM((d, nd * w), BF16)],
        args=(x, ng, scale, shift, w_in_g))


def _rg_fwd(u, d, conv_w, conv_b, w_a, b_a, w_x, b_x, lam, tile, comms=None):
    s = u.shape[0]

    def body(x_ref, z_ref, cw_ref, cb_ref, wa_ref, ba_ref, wx_ref, bx_ref, lam_ref,
             h_ref, y_ref, xc_ref, r_ref, i_ref, a_ref, beta_ref, prev8, hcar):
        @pl.when(pl.program_id(0) == 0)
        def _():
            prev8[...] = jnp.zeros_like(prev8)
            hcar[...] = jnp.zeros_like(hcar)

        x = x_ref[...]
        xc = _conv(x, prev8[...], cw_ref) + cb_ref[...]
        prev8[...] = x[tile - SUBLANES:tile, :]
        r, ig, _, a, beta = _rg_gates(xc, wa_ref, ba_ref, wx_ref, bx_ref, lam_ref)
        xc_ref[...] = xc
        r_ref[...] = r
        i_ref[...] = ig
        a_ref[...] = a
        beta_ref[...] = beta
        _scan_into(a, beta * ig * xc, hcar[SUBLANES - 1:SUBLANES, :], h_ref, False)
        h = h_ref[...]
        hcar[...] = h[tile - SUBLANES:tile, :]
        z = z_ref[...]
        y_ref[...] = (h * z * _sigmoid(z)).astype(BF16)

    vec = _full((1, d))
    return _call(
        body, comms, name="rglru_fwd", grid=(s // tile,),
        in_specs=[pl.BlockSpec((tile, d), lambda i: (i, 0)), pl.BlockSpec((tile, d), lambda i: (i, 1)),
                  _full(conv_w.shape), vec, _full(w_a.shape), vec, _full(w_x.shape), vec, vec],
        out_specs=[pl.BlockSpec((tile, d), lambda i: (i, 0))] * 7,
        out_shape=[jax.ShapeDtypeStruct((s, d), F32), jax.ShapeDtypeStruct((s, d), BF16)] + [jax.ShapeDtypeStruct((s, d), F32)] * 5,
        scratch_shapes=[pltpu.VMEM((SUBLANES, d), F32), pltpu.VMEM((SUBLANES, d), F32)],
        args=(u, u, conv_w, conv_b, w_a, b_a, w_x, b_x, lam))


def _ml_pre(u, d, conv_w, conv_b, w_q, w_k, w_v, wif, bif, tile, comms=None):
    s = u.shape[0]
    nh = w_q.shape[0]

    def body(x_ref, cw_ref, cb_ref, wq_ref, wk_ref, wv_ref, wif_ref, bif_ref, q_ref, k_ref, v_ref, g_ref, pre_ref, prev8):
        @pl.when(pl.program_id(0) == 0)
        def _():
            prev8[...] = jnp.zeros_like(prev8)

        x = x_ref[...]
        pre = _conv(x, prev8[...], cw_ref) + cb_ref[...]
        prev8[...] = x[tile - SUBLANES:tile, :]
        xc = pre * _sigmoid(pre)
        q = _blockdiag(xc, wq_ref)
        k = _blockdiag(xc, wk_ref)
        v = _blockdiag(x, wv_ref)
        pre_ref[...] = pre
        q_ref[...] = q
        k_ref[...] = k
        v_ref[...] = v
        g = _mm(q, wif_ref[0:d, :]) + _mm(k, wif_ref[d:2 * d, :]) + _mm(v, wif_ref[2 * d:3 * d, :]) + bif_ref[...]
        lane = _iota(g.shape, 1)
        gl = jnp.where(lane < 4, g, jnp.where(lane < 8, -_softplus(-g), 0.0))
        tri = jnp.where(_iota((ML_CHUNK, ML_CHUNK), 1) <= _iota((ML_CHUNK, ML_CHUNK), 0), 1.0, 0.0)
        cums = [_mm_hi(tri, gl[c * ML_CHUNK:(c + 1) * ML_CHUNK, :]) for c in range(tile // ML_CHUNK)]
        cum = cums[0] if len(cums) == 1 else jnp.concatenate(cums, axis=0)
        g_ref[...] = gl + jnp.where((lane >= 8) & (lane < 12), pltpu.roll(cum, 4, 1), 0.0)

    vec = _full((1, d))
    return _call(
        body, comms, name="mlstm_proj_fwd", grid=(s // tile,),
        in_specs=[pl.BlockSpec((tile, d), lambda i: (i, 2)), _full(conv_w.shape), vec,
                  _full(w_q.shape), _full(w_k.shape), _full(w_v.shape), _full(wif.shape), _full((1, LANES))],
        out_specs=[pl.BlockSpec((tile, d), lambda i: (i, 0))] * 3 + [pl.BlockSpec((tile, LANES), lambda i: (i, 0)),
                                                                     pl.BlockSpec((tile, d), lambda i: (i, 0))],
        out_shape=[jax.ShapeDtypeStruct((s, d), F32)] * 3 + [jax.ShapeDtypeStruct((s, LANES), F32),
                                                             jax.ShapeDtypeStruct((s, d), F32)],
        scratch_shapes=[pltpu.VMEM((SUBLANES, d), F32)],
        args=(u, conv_w, conv_b, w_q, w_k, w_v, wif, bif))


CELL_CHUNKS_PER_STEP = 4
CELL_BWD_CHUNKS_PER_STEP = 2


def _cell_chunk(h, nh, q_ref, k_ref, v_ref, gc, gr, m_prev, c_h, n_h, m_t=None, r0=0):
    lc = ML_CHUNK
    dh = q_ref.shape[1] // nh
    sl = slice(h * dh, (h + 1) * dh)
    qh = q_ref[r0:r0 + lc, sl]
    kh = k_ref[r0:r0 + lc, sl] * (dh ** -0.5)
    vh = v_ref[r0:r0 + lc, sl]
    li_c = _col(gc, h)
    b_c = _col(gc, 8 + h)
    lib_r = _row(gr, h) - _row(gr, 8 + h)
    b_last = _colsum(jnp.where(_iota((lc, 1), 0) == lc - 1, b_c, 0.0))
    causal = _iota((lc, lc), 1) <= _iota((lc, lc), 0)
    dmat = jnp.where(causal, b_c + lib_r, NEG_BIG)
    m_inter = b_c + m_prev
    if m_t is None:
        m_t = jnp.maximum(m_inter, jnp.max(dmat, axis=1, keepdims=True))
    w_intra = jnp.exp(dmat - m_t)
    w_inter = jnp.exp(m_inter - m_t)
    amat = _mm_nt(qh, kh)
    smat = amat * w_intra
    qc = _mm(qh, c_h)
    qn = _rowsum(qh * n_h)
    den = _rowsum(smat) + w_inter * qn
    gst = b_last - b_c + li_c
    m_new = jnp.maximum(b_last + m_prev, jnp.max(gst, axis=0, keepdims=True))
    w_state = jnp.exp(gst - m_new)
    decay = jnp.exp(b_last + m_prev - m_new)
    return dict(sl=sl, qh=qh, kh=kh, vh=vh, m_t=m_t, w_intra=w_intra, w_inter=w_inter, smat=smat, qc=qc, qn=qn,
                den=den, m_new=m_new, w_state=w_state, decay=decay)


def _ml_cell_fwd(q, k, v, gcol, grow, u, ng, nh, comms=None):
    s, d = q.shape
    lc = ML_CHUNK
    nc = s // lc
    dh = d // nh

    per = CELL_CHUNKS_PER_STEP if nc % CELL_CHUNKS_PER_STEP == 0 else 1

    def body(q_ref, k_ref, v_ref, gc_ref, gr_ref, o_ref, z_ref, ng_ref,
             cell_ref, y_ref, cs_ref, ns_ref, ms_ref, mt_ref, c_sc, n_sc, m_sc):
        @pl.when(pl.program_id(0) == 0)
        def _():
            c_sc[...] = jnp.zeros_like(c_sc)
            n_sc[...] = jnp.zeros_like(n_sc)
            m_sc[...] = jnp.zeros_like(m_sc)

        lane = _iota((lc, LANES), 1)
        for cc in range(per):
            rows = slice(cc * lc, (cc + 1) * lc)
            gc = gc_ref[rows, :]
            gr = gr_ref[:, rows]
            mt_acc = jnp.zeros((lc, LANES), F32)
            for h in range(nh):
                c_h = c_sc[h]
                n_h = n_sc[h, 0:1, :]
                m_prev = jnp.max(m_sc[h, 0:1, :], axis=1, keepdims=True)
                cs_ref[cc, h] = c_h
                ns_ref[cc, h] = n_sc[h]
                ms_ref[cc, h] = m_sc[h]
                t = _cell_chunk(h, nh, q_ref, k_ref, v_ref, gc, gr, m_prev, c_h, n_h, r0=cc * lc)
                sl = t["sl"]
                num = _mm(t["smat"], t["vh"]) + t["w_inter"] * t["qc"]
                cell_h = num / jnp.maximum(jnp.abs(t["den"]), jnp.exp(-t["m_t"]))
                mt_acc = jnp.where(lane == h, t["m_t"], mt_acc)
                kw = t["kh"] * t["w_state"]
                c_sc[h] = t["decay"] * c_h + _mm_tn(kw, t["vh"])
                n_sc[h] = _bcast8(t["decay"] * n_h + _colsum(kw))
                m_sc[h] = jnp.broadcast_to(t["m_new"], (SUBLANES, LANES))
                hg = _sigmoid(o_ref[rows, sl]) * cell_h
                hn = hg * lax.rsqrt(jnp.mean(hg * hg, axis=1, keepdims=True) + EPS)
                z = z_ref[rows, sl]
                cell_ref[rows, sl] = cell_h
                y_ref[rows, sl] = (hn * ng_ref[:, sl] * z * _sigmoid(z)).astype(BF16)
            mt_ref[rows, :] = mt_acc

    tok = pl.BlockSpec((per * lc, d), lambda c: (c, 0))
    return _call(
        body, comms, name="mlstm_cell_fwd", grid=(nc // per,),
        in_specs=[tok, tok, tok, pl.BlockSpec((per * lc, LANES), lambda c: (c, 0)),
                  pl.BlockSpec((16, per * lc), lambda c: (0, c)),
                  pl.BlockSpec((per * lc, d), lambda c: (c, 3)), pl.BlockSpec((per * lc, d), lambda c: (c, 4)), _full((1, d))],
        out_specs=[tok, tok, pl.BlockSpec((per, nh, dh, dh), lambda c: (c, 0, 0, 0)),
                   pl.BlockSpec((per, nh, SUBLANES, dh), lambda c: (c, 0, 0, 0)),
                   pl.BlockSpec((per, nh, SUBLANES, LANES), lambda c: (c, 0, 0, 0)),
                   pl.BlockSpec((per * lc, LANES), lambda c: (c, 0))],
        out_shape=[jax.ShapeDtypeStruct((s, d), F32), jax.ShapeDtypeStruct((s, d), BF16),
                   jax.ShapeDtypeStruct((nc, nh, dh, dh), F32), jax.ShapeDtypeStruct((nc, nh, SUBLANES, dh), F32),
                   jax.ShapeDtypeStruct((nc, nh, SUBLANES, LANES), F32), jax.ShapeDtypeStruct((s, LANES), F32)],
        scratch_shapes=[pltpu.VMEM((nh, dh, dh), F32), pltpu.VMEM((nh, SUBLANES, dh), F32),
                        pltpu.VMEM((nh, SUBLANES, LANES), F32)],
        args=(q, k, v, gcol, grow, u, u, ng))


def _out_fwd(x, y_rg, y_ml, gate, w_out_g, layer, tile, comms=None, head=None):
    s, d = x.shape
    nd, _, r, _ = w_out_g.shape

    def body(x_ref, yr_ref, ym_ref, g_ref, w_ref, *rest):
        ycat = jnp.concatenate([yr_ref[...].astype(BF16), ym_ref[...].astype(BF16)], axis=1)
        acc = jnp.dot(ycat, w_ref[...].reshape(nd * r, d), preferred_element_type=F32)
        xn = x_ref[...] + g_ref[...] * acc
        if head is None:
            xn_ref, y_ref = rest
            y_ref[...] = acc
            xn_ref[...] = xn
            return
        fg_ref, t_ref, y_ref, dx_ref, loss_ref, gg_ref = rest
        y_ref[...] = acc

        @pl.when(pl.program_id(0) == 0)
        def _():
            loss_ref[...] = jnp.zeros_like(loss_ref)
            gg_ref[...] = jnp.zeros_like(gg_ref)

        fg = fg_ref[...]
        rs = lax.rsqrt(jnp.mean(xn * xn, axis=1, keepdims=True) + EPS)
        xh = xn * rs
        e = xh * fg - t_ref[...]
        loss_ref[...] += jnp.broadcast_to(_colsum(_rowsum(e * e)) * (0.5 / d), loss_ref.shape)
        dy = e * (1.0 / d)
        gg_ref[...] += _bcast8(_colsum(dy * xh))
        dxh = dy * fg
        dx_ref[...] = rs * (dxh - xh * jnp.mean(dxh * xh, axis=1, keepdims=True))

    tok = pl.BlockSpec((tile, d), lambda i: (i, 0))
    in_specs = [tok, tok, tok, _full((1, d)), pl.BlockSpec((nd, 1, r, d), lambda i: (0, layer, 0, 0))]
    if head is None:
        return _call(body, comms, name="out_proj_fwd", grid=(s // tile,), in_specs=in_specs, out_specs=[tok, tok],
                     out_shape=[jax.ShapeDtypeStruct((s, d), F32)] * 2, args=(x, y_rg, y_ml, gate, w_out_g))
    return _call(
        body, comms, name="out_proj_loss", grid=(s // tile,),
        in_specs=in_specs + [_full((1, d)), tok],
        out_specs=[tok, tok, _full((SUBLANES, LANES)), _full((SUBLANES, d))],
        out_shape=[jax.ShapeDtypeStruct((s, d), F32)] * 2 + [jax.ShapeDtypeStruct((SUBLANES, LANES), F32),
                                                             jax.ShapeDtypeStruct((SUBLANES, d), F32)],
        args=(x, y_rg, y_ml, gate, w_out_g, *head))


def _ml_out_stage_bwd(dy, cell, o, z, ng):
    so = _sigmoid(o)
    hg = so * cell
    rinv = lax.rsqrt(jnp.mean(hg * hg, axis=1, keepdims=True) + EPS)
    hn = hg * rinv
    sz = _sigmoid(z)
    dz = dy * hn * ng * (sz + z * sz * (1.0 - sz))
    dymid = dy * z * sz
    dhn = dymid * ng
    dhg = rinv * (dhn - hn * jnp.mean(dhn * hn, axis=1, keepdims=True))
    return dz, dhg * cell * so * (1.0 - so), dhg * so, _colsum(dymid * hn)


def _out_bwd(dxo, gate, y, y_rg, y_ml, w_out_g, layer, tile, comms=None):
    s, d = dxo.shape
    nd, _, r, _ = w_out_g.shape

    def body(dx_ref, g_ref, y_ref, yr_ref, ym_ref, w_ref, dyr_ref, dym_ref, gw_ref, dg_ref):
        @pl.when(pl.program_id(0) == 0)
        def _():
            gw_ref[...] = jnp.zeros_like(gw_ref)
            dg_ref[...] = jnp.zeros_like(dg_ref)

        dxv = dx_ref[...]
        dg_ref[...] += _bcast8(_colsum(dxv * y_ref[...]))
        dyb = (dxv * g_ref[...]).astype(BF16)
        dycat = lax.dot_general(dyb, w_ref[...].reshape(nd * r, d), (((1,), (1,)), ((), ())), preferred_element_type=F32)
        dyr_ref[...] = dycat[:, 0:d]
        dym_ref[...] = dycat[:, d:2 * d]
        ycat = jnp.concatenate([yr_ref[...].astype(BF16), ym_ref[...].astype(BF16)], axis=1)
        gw_ref[...] += lax.dot_general(ycat, dyb, (((0,), (0,)), ((), ())), preferred_element_type=F32).reshape(nd, r, d)

    tok = pl.BlockSpec((tile, d), lambda i: (i, 0))
    return _call(
        body, comms, name="out_proj_bwd", grid=(s // tile,),
        in_specs=[tok, _full((1, d)), tok, tok, tok, pl.BlockSpec((nd, 1, r, d), lambda i: (0, layer, 0, 0))],
        out_specs=[tok, tok, _full((nd, r, d)), _full((SUBLANES, d))],
        out_shape=[jax.ShapeDtypeStruct((s, d), F32)] * 2 + [jax.ShapeDtypeStruct((nd, r, d), F32),
                                                             jax.ShapeDtypeStruct((SUBLANES, d), F32)],
        args=(dxo, gate, y, y_rg, y_ml, w_out_g))


def _ml_cell_bwd(dy_ml, u, cell, q, k, v, gcol, grow, mt, cs, ns, ms, ng, nh, comms=None):
    s, d = q.shape
    lc = ML_CHUNK
    nc = s // lc
    dh = d // nh

    per = CELL_BWD_CHUNKS_PER_STEP if nc % CELL_BWD_CHUNKS_PER_STEP == 0 else 1

    def body(*refs):
        gng_ref, dc_sc, dn_sc = refs[20], refs[21], refs[22]

        @pl.when(pl.program_id(0) == 0)
        def _():
            dc_sc[...] = jnp.zeros_like(dc_sc)
            dn_sc[...] = jnp.zeros_like(dn_sc)
            gng_ref[...] = jnp.zeros_like(gng_ref)

        for cc in reversed(range(per)):
            rows = slice(cc * lc, (cc + 1) * lc)
            views = [refs[at] if at == 13 else _PartOf(refs[at], cols=rows) if at == 8 else
                     _PartOf(refs[at], lead=cc) if at in (10, 11, 12) else _PartOf(refs[at], rows=rows) for at in range(20)]
            chunk(*views, gng_ref, dc_sc, dn_sc)

    def chunk(dy_ref, o_ref, z_ref, cell_ref, q_ref, k_ref, v_ref, gc_ref, gr_ref, mt_ref, cs_ref, ns_ref, ms_ref,
              ng_ref, dq_ref, dk_ref, dv_ref, dg_ref, do_ref, dz_ref, gng_ref, dc_sc, dn_sc):
        gc = gc_ref[...]
        gr = gr_ref[...]
        mtv = mt_ref[...]
        lane = _iota((lc, LANES), 1)
        rowv = _iota((lc, 1), 0)
        dg_acc = jnp.zeros((lc, LANES), F32)
        for h in range(nh):
            c_h = cs_ref[0, h]
            n_h = ns_ref[0, h, 0:1, :]
            m_prev = jnp.max(ms_ref[0, h, 0:1, :], axis=1, keepdims=True)
            t = _cell_chunk(h, nh, q_ref, k_ref, v_ref, gc, gr, m_prev, c_h, n_h, m_t=_col(mtv, h))
            sl, qh, kh, vh = t["sl"], t["qh"], t["kh"], t["vh"]
            w_intra, w_inter, smat, w_state, decay = t["w_intra"], t["w_inter"], t["smat"], t["w_state"], t["decay"]
            cell_h = cell_ref[:, sl]
            dz, do, dcell, gng = _ml_out_stage_bwd(dy_ref[:, sl], cell_h, o_ref[:, sl], z_ref[:, sl], ng_ref[:, sl])
            dz_ref[:, sl] = dz.astype(BF16)
            do_ref[:, sl] = do.astype(BF16)
            gng_ref[:, sl] += _bcast8(gng)
            eneg = jnp.exp(-t["m_t"])
            aden = jnp.abs(t["den"])
            nst = jnp.maximum(aden, eneg)
            dnum = dcell / nst
            dden = jnp.where(aden > eneg, -_rowsum(cell_h * dcell) / nst * jnp.sign(t["den"]), 0.0)
            pmat = _mm_nt(dnum, vh) + dden
            damat = pmat * w_intra
            gmat = pmat * smat
            wdn = w_inter * dnum
            wdd = w_inter * dden
            dqh = _mm(damat, kh) + _mm_nt(wdn, c_h) + wdd * n_h
            dkh = _mm_tn(damat, qh)
            dvh = _mm_tn(smat, dnum)
            dw_inter = _rowsum(dnum * t["qc"]) + dden * t["qn"]
            dcn = dc_sc[h]
            dnn = dn_sc[h, 0:1, :]
            kw = kh * w_state
            dkw = _mm_nt(vh, dcn) + dnn
            dvh = dvh + _mm(kw, dcn)
            dkh = dkh + dkw * w_state
            dgst = _rowsum(dkw * kh) * w_state
            ddecay = _colsum(_rowsum(dcn * c_h)) + _rowsum(dnn * n_h)
            db_last = _colsum(dgst) + ddecay * decay
            rs_g = _rowsum(gmat)
            cs_g = _rowsum(gmat.T)
            db = rs_g - cs_g + dw_inter * w_inter - dgst + jnp.where(rowv == lc - 1, db_last, 0.0)
            dli = cs_g + dgst
            dc_sc[h] = decay * dcn + _mm_tn(qh, wdn)
            dn_sc[h] = _bcast8(decay * dnn + _colsum(qh * wdd))
            dq_ref[:, sl] = dqh
            dk_ref[:, sl] = dkh * (dh ** -0.5)
            dv_ref[:, sl] = dvh
            dg_acc = jnp.where(lane == h, dli, jnp.where(lane == 4 + h, db, dg_acc))
        dg_ref[...] = dg_acc

    rev = lambda c: nc // per - 1 - c
    tok = pl.BlockSpec((per * lc, d), lambda c: (rev(c), 0))
    g128 = pl.BlockSpec((per * lc, LANES), lambda c: (rev(c), 0))
    return _call(
        body, comms, name="mlstm_cell_bwd", grid=(nc // per,),
        in_specs=[tok, pl.BlockSpec((per * lc, d), lambda c: (rev(c), 3)), pl.BlockSpec((per * lc, d), lambda c: (rev(c), 4)),
                  tok, tok, tok, tok, g128, pl.BlockSpec((16, per * lc), lambda c: (0, rev(c))), g128,
                  pl.BlockSpec((per, nh, dh, dh), lambda c: (rev(c), 0, 0, 0)),
                  pl.BlockSpec((per, nh, SUBLANES, dh), lambda c: (rev(c), 0, 0, 0)),
                  pl.BlockSpec((per, nh, SUBLANES, LANES), lambda c: (rev(c), 0, 0, 0)), _full((1, d))],
        out_specs=[tok, tok, tok, g128, tok, tok, _full((SUBLANES, d))],
        out_shape=[jax.ShapeDtypeStruct((s, d), F32)] * 3 + [jax.ShapeDtypeStruct((s, LANES), F32)]
        + [jax.ShapeDtypeStruct((s, d), BF16)] * 2 + [jax.ShapeDtypeStruct((SUBLANES, d), F32)],
        scratch_shapes=[pltpu.VMEM((nh, dh, dh), F32), pltpu.VMEM((nh, SUBLANES, dh), F32)],
        args=(dy_ml, u, u, cell, q, k, v, gcol, grow, mt, cs, ns, ms, ng))


def _halo_spec(d, tile, nt, col):
    per = tile // SUBLANES
    return pl.BlockSpec((SUBLANES, d), lambda i: (jnp.maximum((nt - 1 - i) * per - 1, 0), col))


def _ml_pre_bwd(dq, dk, dv, dgates, gcol, u, pre, q, k, v, conv_w, w_q, w_k, w_v, wif_t, tile, comms=None):
    s, d = dq.shape
    nt = s // tile
    nh, dh, _ = w_q.shape

    def body(dq_ref, dk_ref, dv_ref, dg_ref, gc_ref, x_ref, pre_ref, q_ref, k_ref, v_ref, cw_ref,
             wq_ref, wk_ref, wv_ref, wift_ref,
             dx_ref, gwq_ref, gwk_ref, gwv_ref, gwif_ref, gbif_ref, gcw_ref, gcb_ref, next8):
        @pl.when(pl.program_id(0) == 0)
        def _():
            next8[...] = jnp.zeros_like(next8)
            for ref in (gwq_ref, gwk_ref, gwv_ref, gwif_ref, gbif_ref, gcw_ref, gcb_ref):
                ref[...] = jnp.zeros_like(ref)

        x = x_ref[...]
        pre = pre_ref[...]
        sg = _sigmoid(pre)
        xc = pre * sg
        dgc = dg_ref[...]
        lane = _iota(dgc.shape, 1)
        utri = jnp.where(_iota((ML_CHUNK, ML_CHUNK), 0) <= _iota((ML_CHUNK, ML_CHUNK), 1), 1.0, 0.0)
        rcs = [_mm_hi(utri, dgc[c * ML_CHUNK:(c + 1) * ML_CHUNK, :]) for c in range(tile // ML_CHUNK)]
        rc = rcs[0] if len(rcs) == 1 else jnp.concatenate(rcs, axis=0)
        dgates_v = jnp.where(lane < 4, dgc, jnp.where(lane < 8, rc * (1.0 - jnp.exp(gc_ref[...])), 0.0))
        dgb = dgates_v.astype(BF16)
        gbif_ref[...] += jnp.broadcast_to(_colsum(dgates_v), gbif_ref.shape)
        ext = jnp.dot(dgb, wift_ref[...], preferred_element_type=F32)
        dqt = dq_ref[...] + ext[:, 0:d]
        dkt = dk_ref[...] + ext[:, d:2 * d]
        dvt = dv_ref[...] + ext[:, 2 * d:3 * d]
        gwif_ref[:, 0:d] += _mm_tn(dgb, q_ref[...])
        gwif_ref[:, d:2 * d] += _mm_tn(dgb, k_ref[...])
        gwif_ref[:, 2 * d:3 * d] += _mm_tn(dgb, v_ref[...])
        dxc_parts, dxv_parts = [], []
        for h in range(nh):
            sl = slice(h * dh, (h + 1) * dh)
            gwq_ref[h] += _mm_tn(xc[:, sl], dqt[:, sl])
            gwk_ref[h] += _mm_tn(xc[:, sl], dkt[:, sl])
            gwv_ref[h] += _mm_tn(x[:, sl], dvt[:, sl])
            dxc_parts.append(_mm_nt(dqt[:, sl], wq_ref[h]) + _mm_nt(dkt[:, sl], wk_ref[h]))
            dxv_parts.append(_mm_nt(dvt[:, sl], wv_ref[h]))
        dxc = jnp.concatenate(dxc_parts, axis=1)
        dxv = jnp.concatenate(dxv_parts, axis=1)
        dpre = dxc * (sg + pre * sg * (1.0 - sg))
        gcb_ref[...] += _bcast8(_colsum(dpre))
        dx_ref[...] = (dxv + _conv_bwd(dpre, x, next8[...], cw_ref, gcw_ref)).astype(BF16)
        next8[...] = dpre[0:SUBLANES, :]

    rev = lambda i: nt - 1 - i
    tok = pl.BlockSpec((tile, d), lambda i: (rev(i), 0))
    g128 = pl.BlockSpec((tile, LANES), lambda i: (rev(i), 0))
    wsh = (nh, dh, dh)
    return _call(
        body, comms, name="mlstm_proj_bwd", grid=(nt,),
        in_specs=[tok, tok, tok, g128, g128, pl.BlockSpec((tile, d), lambda i: (rev(i), 2)), tok,
                  tok, tok, tok, _full(conv_w.shape), _full(wsh), _full(wsh), _full(wsh), _full(wif_t.shape)],
        out_specs=[tok, _full(wsh), _full(wsh), _full(wsh), _full((LANES, 3 * d)), _full((SUBLANES, LANES)),
                   _full((SUBLANES, d)), _full((SUBLANES, d))],
        out_shape=[jax.ShapeDtypeStruct((s, d), BF16)] + [jax.ShapeDtypeStruct(wsh, F32)] * 3
        + [jax.ShapeDtypeStruct((LANES, 3 * d), F32), jax.ShapeDtypeStruct((SUBLANES, LANES), F32),
           jax.ShapeDtypeStruct((SUBLANES, d), F32), jax.ShapeDtypeStruct((SUBLANES, d), F32)],
        scratch_shapes=[pltpu.VMEM((SUBLANES, d), F32)],
        args=(dq, dk, dv, dgates, gcol, u, pre, q, k, v, conv_w, w_q, w_k, w_v, wif_t))


def _rg_bwd(dy_rg, u, h_rg, gates, conv_w, w_a, w_x, lam, tile, comms=None):
    s, d = dy_rg.shape
    nt = s // tile
    nh, dh, _ = w_a.shape

    def body(dy_ref, x_ref, z_ref, h_ref, hhalo_ref, xc_ref, r_ref, i_ref, a_ref, beta_ref, cw_ref, wa_ref,
             wx_ref, lam_ref,
             dx_ref, dz_ref, gwa_ref, gwx_ref, gba_ref, gbx_ref, glam_ref, gcw_ref, gcb_ref, next8, anext, dnext, dbuf):
        i = pl.program_id(0)

        @pl.when(i == 0)
        def _():
            for ref in (next8, anext, dnext, gwa_ref, gwx_ref, gba_ref, gbx_ref, glam_ref, gcw_ref, gcb_ref):
                ref[...] = jnp.zeros_like(ref)

        inner = jnp.where(i < nt - 1, 1.0, 0.0)
        xc, r, ig, a, beta = xc_ref[...], r_ref[...], i_ref[...], a_ref[...], beta_ref[...]
        sp = _softplus(-lam_ref[...])
        h = h_ref[...]
        row = _iota(h.shape, 0)
        hprev = jnp.where(row >= 1, pltpu.roll(h, 1, 0), hhalo_ref[SUBLANES - 1:SUBLANES, :] * inner)
        z = z_ref[...]
        sz = _sigmoid(z)
        dyv = dy_ref[...]
        dz_ref[...] = (dyv * h * (sz + z * sz * (1.0 - sz))).astype(BF16)
        a_up = jnp.where(row < tile - 1, pltpu.roll(a, tile - 1, 0), anext[0:1, :])
        _scan_into(a_up, dyv * z * sz, dnext[0:1, :], dbuf, True)
        delta = dbuf[...]
        anext[...] = a[0:SUBLANES, :]
        dnext[...] = delta[0:SUBLANES, :]
        dla = delta * hprev * a - delta * ig * xc * (a * a / beta)
        glam_ref[...] += _bcast8(_colsum(dla * r) * (RG_C * _sigmoid(-lam_ref[...])))
        dpa = dla * (-RG_C * sp) * r * (1.0 - r)
        dpx = delta * beta * xc * ig * (1.0 - ig)
        gba_ref[...] += _bcast8(_colsum(dpa))
        gbx_ref[...] += _bcast8(_colsum(dpx))
        parts = []
        for hh in range(nh):
            sl = slice(hh * dh, (hh + 1) * dh)
            gwa_ref[hh] += _mm_tn(xc[:, sl], dpa[:, sl])
            gwx_ref[hh] += _mm_tn(xc[:, sl], dpx[:, sl])
            parts.append(_mm_nt(dpa[:, sl], wa_ref[hh]) + _mm_nt(dpx[:, sl], wx_ref[hh]))
        dxc = delta * beta * ig + jnp.concatenate(parts, axis=1)
        gcb_ref[...] += _bcast8(_colsum(dxc))
        dx_ref[...] = _conv_bwd(dxc, x_ref[...], next8[...], cw_ref, gcw_ref).astype(BF16)
        next8[...] = dxc[0:SUBLANES, :]

    rev = lambda i: nt - 1 - i
    tok = pl.BlockSpec((tile, d), lambda i: (rev(i), 0))
    vec = _full((1, d))
    acc = _full((SUBLANES, d))
    wsh = (nh, dh, dh)
    return _call(
        body, comms, name="rglru_bwd", grid=(nt,),
        in_specs=[tok, tok, pl.BlockSpec((tile, d), lambda i: (rev(i), 1)), tok,
                  _halo_spec(d, tile, nt, 0)] + [tok] * 5 + [_full(conv_w.shape), _full(wsh), _full(wsh), vec],
        out_specs=[tok, tok, _full(wsh), _full(wsh), acc, acc, acc, acc, acc],
        out_shape=[jax.ShapeDtypeStruct((s, d), BF16)] * 2 + [jax.ShapeDtypeStruct(wsh, F32)] * 2
        + [jax.ShapeDtypeStruct((SUBLANES, d), F32)] * 5,
        scratch_shapes=[pltpu.VMEM((SUBLANES, d), F32)] * 3 + [pltpu.VMEM((tile, d), F32)],
        args=(dy_rg, u, u, h_rg, h_rg, *gates, conv_w, w_a, w_x, lam))


def _segments(d, w, n_pieces, n_slots):
    bounds = sorted({k * d for k in range(n_pieces + 1)} | {j * w for j in range(n_slots + 1)})
    return [(lo // d, lo % d, lo // w, lo % w, hi - lo) for lo, hi in zip(bounds[:-1], bounds[1:])]


def _in_bwd(pieces, x, dxo, ng, scale, w_in_g, layer, tile, comms=None, tiles=None, prev=None):
    s, d = x.shape
    nd, _, _, w = w_in_g.shape
    first, count = tiles or (0, s // tile)
    n_p = len(pieces)

    def body(*refs):
        p_refs = refs[:n_p]
        x_ref, dxo_ref, ng_ref, sc_ref, w_ref = refs[n_p:n_p + 5]
        dx_ref, dsc_ref, dsh_ref, gng_ref, wcat = refs[-5:]
        _join_columns(w_ref, wcat)

        @pl.when(pl.program_id(0) == 0)
        def _():
            for k, ref in enumerate((dsc_ref, dsh_ref, gng_ref)):
                ref[...] = jnp.zeros_like(ref) if prev is None else refs[n_p + 6 + k][...]

        du = jnp.concatenate([p[...] for p in p_refs], axis=1)
        dh = lax.dot_general(du, wcat[...], (((1,), (1,)), ((), ())), preferred_element_type=F32)
        xv = x_ref[...]
        g = ng_ref[...]
        rs = lax.rsqrt(jnp.mean(xv * xv, axis=1, keepdims=True) + EPS)
        xh = xv * rs
        dsh_ref[...] += _bcast8(_colsum(dh))
        dsc_ref[...] += _bcast8(_colsum(dh * xh * g))
        dhn = dh * (1.0 + sc_ref[...])
        gng_ref[...] += _bcast8(_colsum(dhn * xh))
        dxh = dhn * g
        dx_ref[...] = dxo_ref[...] + rs * (dxh - xh * jnp.mean(dxh * xh, axis=1, keepdims=True))

    tok = pl.BlockSpec((tile, d), lambda i: (i + first, 0))
    vec = _full((1, d))
    acc = _full((SUBLANES, d))
    more_specs = [] if prev is None else [pl.BlockSpec(memory_space=pl.ANY), acc, acc, acc]
    return _call(
        body, comms, name="in_proj_bwd_x", grid=(count,),
        in_specs=[tok] * n_p + [tok, tok, vec, vec, pl.BlockSpec((nd, 1, d, w), lambda i: (0, layer, 0, 0),
                                                               pipeline_mode=pl.Buffered(1))] + more_specs,
        out_specs=[tok, acc, acc, acc],
        out_shape=[jax.ShapeDtypeStruct((s, d), F32)] + [jax.ShapeDtypeStruct((SUBLANES, d), F32)] * 3,
        scratch_shapes=[pltpu.VMEM((d, nd * w), BF16)],
        args=(*pieces, x, dxo, ng, scale, w_in_g) + (() if prev is None else tuple(prev)),
        aliases={} if prev is None else {n_p + 5: 0})


def _in_bwd_w(pieces, hbf, w, slots, tile, comms=None):
    s, d = hbf.shape
    nd_all = len(pieces) * d // w
    segs = [sg for sg in _segments(d, w, len(pieces), nd_all) if sg[2] in slots]

    def body(*refs):
        p_refs = refs[:len(pieces)]
        h_ref, gw_ref = refs[len(pieces):]

        @pl.when(pl.program_id(0) == 0)
        def _():
            gw_ref[...] = jnp.zeros_like(gw_ref)

        hv = h_ref[...]
        for (kk, a, j, b, width) in segs:
            gw_ref[j - slots[0], :, b:b + width] += _mm_tn(hv, p_refs[kk][:, a:a + width])

    tok = pl.BlockSpec((tile, d), lambda i: (i, 0))
    return _call(
        body, comms, name="in_proj_bwd_w", grid=(s // tile,),
        in_specs=[tok] * len(pieces) + [tok],
        out_specs=[pl.BlockSpec((len(slots), d, w), lambda i: (0, 0, 0), pipeline_mode=pl.Buffered(1))],
        out_shape=[jax.ShapeDtypeStruct((len(slots), d, w), F32)],
        args=(*pieces, hbf))[0]


def _exchange(arrs, gather, name):
    return _run_comms([_exchange_comm(arrs, gather)], name)[0]


def _run_comms(comms, name):
    _call(lambda: None, comms, name=name, grid=(1,), in_specs=[], out_specs=[], out_shape=[], args=())
    return [cm.results for cm in comms]


def _exchange_comm(arrs, gather):
    n = len(arrs)
    per = N_DEV - 1

    def copies(ins, outs, sems):
        send_sems, recv_sems, local_sems = sems
        x, y, c = (lax.axis_index(ax) for ax in MESH_AXES)
        me = 4 * x + 2 * y + c
        sends, recvs = [], []
        for flip in range(1, N_DEV):
            px = x ^ ((flip >> 2) & 1)
            py = y ^ ((flip >> 1) & 1)
            pc = c ^ (flip & 1)
            peer = 4 * px + 2 * py + pc
            for kk in range(n):
                src = ins[kk] if gather else ins[kk].at[peer]
                sends.append(_remote(src, outs[kk].at[me], send_sems, recv_sems, kk * per + flip - 1, (px, py, pc)))
                recvs.append(_remote(src, outs[kk].at[peer], send_sems, recv_sems, kk * per + flip - 1, (px, py, pc)))
        local = [pltpu.make_async_copy(ins[kk] if gather else ins[kk].at[me], outs[kk].at[me], local_sems.at[kk])
                 for kk in range(n)]
        return local, sends, recvs

    def start(ins, outs, sems):
        local, sends, _ = copies(ins, outs, sems)
        for cp in sends + local:
            cp.start()

    def finish(ins, outs, sems):
        local, sends, recvs = copies(ins, outs, sems)
        for cp in recvs:
            cp.wait_recv()
        for cp in sends:
            cp.wait_send()
        for cp in local:
            cp.wait()

    return _Comm(arrs, [jax.ShapeDtypeStruct((N_DEV,) + a.shape if gather else a.shape, a.dtype) for a in arrs],
                 [pltpu.SemaphoreType.DMA((n * per,)), pltpu.SemaphoreType.DMA((n * per,)), pltpu.SemaphoreType.DMA((n,))],
                 start, finish)


def _mesh_place():
    x, y, c = (lax.axis_index(ax) for ax in MESH_AXES)
    return x, y, c, (x, y, 1 - c), [(1 - x, y), (x, 1 - y), (1 - x, 1 - y)]


def _remote(src, dst, send_sems, recv_sems, sem, to):
    return pltpu.make_async_remote_copy(src_ref=src, dst_ref=dst, send_sem=send_sems.at[sem], recv_sem=recv_sems.at[sem],
                                        device_id=to, device_id_type=pl.DeviceIdType.MESH)


N_CHIPS = N_DEV // 2


def _pair_sum(a, other, parity, name):
    _, r, c = a.shape
    tr = _row_tile(r, c, 3)

    def body(p_ref, a_ref, o_ref, s_ref):
        s_ref[...] = (a_ref[...] + o_ref[...]).astype(BF16)

    return pl.pallas_call(
        body, name=name,
        grid_spec=pltpu.PrefetchScalarGridSpec(
            num_scalar_prefetch=1, grid=(N_CHIPS, r // tr),
            in_specs=[pl.BlockSpec((1, tr, c), lambda q, i, p: (2 * q + p[0], i, 0)),
                      pl.BlockSpec((1, tr, c), lambda q, i, p: (q, i, 0))],
            out_specs=pl.BlockSpec((1, tr, c), lambda q, i, p: (q, i, 0))),
        out_shape=jax.ShapeDtypeStruct((N_CHIPS, r, c), BF16),
        compiler_params=_params(2),
    )(parity, a, other)


def _adam_math(w, g, m, v):
    m = ADAM_B1 * m + (1.0 - ADAM_B1) * g
    v = ADAM_B2 * v + (1.0 - ADAM_B2) * (g * g)
    m_hat = m / (1.0 - ADAM_B1 ** ADAM_STEP)
    v_hat = v / (1.0 - ADAM_B2 ** ADAM_STEP)
    delta = -ADAM_LR * (m_hat / (jnp.sqrt(v_hat) + ADAM_EPS) + ADAM_WD * w)
    return delta, m, v


def _sum_devices(r_ref):
    acc = r_ref[0].astype(F32)
    for p in range(1, r_ref.shape[0]):
        acc = acc + r_ref[p].astype(F32)
    return acc


def _row_tile(rows, cols, n_bufs):
    budget = 24 * 1024 * 1024 // (n_bufs * 2 * cols * 4)
    t = rows
    while t > budget and t % 2 == 0 and (t // 2) % SUBLANES == 0:
        t //= 2
    return t


def _reduce_adam(recvs, w, m, v, name, comms=None):
    nl, r, c = w.shape
    n_part = recvs[0].shape[0]
    tr = _row_tile(r, c, n_part * nl + 7)
    nt = r // tr

    def body(*refs):
        r_refs = refs[:nl]
        w_ref, m_ref, v_ref, g_ref, d_ref, mo_ref, vo_ref = refs[nl:]
        layer = pl.program_id(0) // nt
        g = _sum_devices(r_refs[0])
        for ll in range(1, nl):
            g = jnp.where(layer == ll, _sum_devices(r_refs[ll]), g)
        delta, m2, v2 = _adam_math(w_ref[0], g, m_ref[0], v_ref[0])
        g_ref[0] = g
        d_ref[0] = delta
        mo_ref[0] = m2
        vo_ref[0] = v2

    def rspec(ll):
        return pl.BlockSpec((n_part, tr, c),
                            lambda i: (0, jnp.where(i // nt == ll, i % nt, jnp.where(i // nt < ll, 0, nt - 1)), 0))

    blk = pl.BlockSpec((1, tr, c), lambda i: (i // nt, i % nt, 0))
    return _call(
        body, comms, name=name, grid=(nl * nt,),
        in_specs=[rspec(ll) for ll in range(nl)] + [blk, blk, blk],
        out_specs=[blk] * 4,
        out_shape=[jax.ShapeDtypeStruct((nl, r, c), F32)] * 4,
        args=(*recvs, w, m, v))


def _tile_for(s, want):
    return min(want, s)


REPLICATED = ("norm_g", "b_ada", "rg_conv_b", "rg_w_a", "rg_b_a", "rg_w_x", "rg_b_x", "rg_lambda", "ml_conv_b",
              "ml_b_if", "ml_norm_g", "final_g")


def _small_pack(rg_conv_w, ml_conv_w, ml_w_if):
    nl = rg_conv_w.shape[0]
    wif_t = jnp.swapaxes(ml_w_if, 1, 2).reshape(nl, -1, LANES)
    return jnp.concatenate([rg_conv_w, ml_conv_w, wif_t], axis=1)


def _small_unpack(p, if_rows):
    nl = p.shape[0]
    rg_cw = p[:, 0:CONV_WIDTH]
    ml_cw = p[:, CONV_WIDTH:2 * CONV_WIDTH]
    wif = jnp.swapaxes(p[:, 2 * CONV_WIDTH:].reshape(nl, 8, if_rows), 1, 2)
    return rg_cw, ml_cw, wif


def _qkv_slots(g_qkv, nd):
    three, nh, dh, _ = g_qkv.shape
    return g_qkv.reshape(three, nh, nd, dh // nd, dh).transpose(2, 0, 1, 3, 4).reshape(nd, three * nh * (dh // nd), dh)


def _small_slots(g):
    nd = N_DEV
    cw = jnp.stack([g["rg_conv_w"], g["ml_conv_w"]]).reshape(2, CONV_WIDTH, nd, LANES).transpose(2, 0, 1, 3)
    cw = cw.reshape(nd, 2 * CONV_WIDTH, LANES)
    wif = g["wif_t"].reshape(8, nd, -1).transpose(1, 0, 2).reshape(nd, -1, LANES)
    return jnp.concatenate([cw, wif], axis=1)


def _slot(block):
    return 4 * block[0] + 2 * block[1] + block[2]


def _dma_sems(*counts):
    return [pltpu.SemaphoreType.DMA((n,)) for n in counts]


def _start_all(copies):
    for cp in copies:
        cp.start()


def _gather_ici_comm(arrs):
    n = len(arrs)

    def copies(ins, outs, sems):
        send_sems, recv_sems, local_sems = sems
        x, y, c, sibling, chips = _mesh_place()
        me = (x, y, c)
        peers = [(*chip, c) for chip in chips] + [sibling]
        local = [pltpu.make_async_copy(ins[kk], outs[kk].at[_slot(me)], local_sems.at[kk]) for kk in range(n)]
        sends = [_remote(ins[kk], outs[kk].at[_slot(me)], send_sems, recv_sems, kk * 4 + j, peer)
                 for j, peer in enumerate(peers) for kk in range(n)]
        recvs = [_remote(ins[kk], outs[kk].at[_slot(peer)], send_sems, recv_sems, kk * 4 + j, peer)
                 for j, peer in enumerate(peers) for kk in range(n)]
        return local, sends, recvs

    def start(ins, outs, sems):
        local, sends, _ = copies(ins, outs, sems)
        _start_all(sends + local)

    def finish(ins, outs, sems):
        local, sends, recvs = copies(ins, outs, sems)
        for cp in recvs:
            cp.wait_recv()
        for cp in sends:
            cp.wait_send()
        for cp in local:
            cp.wait()

    return _Comm(arrs, [jax.ShapeDtypeStruct((N_DEV,) + a.shape, a.dtype) for a in arrs], _dma_sems(4 * n, 4 * n, n),
                 start, finish)


def _gather_fwd_comm(bufs):
    n = len(bufs)

    def copies(ins, outs, sems):
        send_sems, recv_sems = sems
        _, _, c, sibling, chips = _mesh_place()
        sends = [_remote(ins[kk].at[_slot((*chip, c))], outs[kk].at[_slot((*chip, c))], send_sems, recv_sems, kk * 3 + j, sibling)
                 for j, chip in enumerate(chips) for kk in range(n)]
        recvs = [_remote(ins[kk].at[_slot((*chip, c))], outs[kk].at[_slot((*chip, 1 - c))], send_sems, recv_sems, kk * 3 + j, sibling)
                 for j, chip in enumerate(chips) for kk in range(n)]
        return sends, recvs

    def start(ins, outs, sems):
        _start_all(copies(ins, outs, sems)[0])

    def finish(ins, outs, sems):
        sends, recvs = copies(ins, outs, sems)
        for cp in recvs:
            cp.wait_recv()
        for cp in sends:
            cp.wait_send()

    return _Comm(bufs, [jax.ShapeDtypeStruct(a.shape, a.dtype) for a in bufs], _dma_sems(3 * n, 3 * n), start, finish,
                 aliases=[(i, i) for i in range(n)])


def _core_swap_comm(arrs):
    n = len(arrs)

    def copies(ins, outs, sems):
        send_sems, recv_sems = sems
        _, _, c, sibling, _ = _mesh_place()
        return [_remote(ins[kk].at[2 * q + (1 - c)], outs[kk].at[q], send_sems, recv_sems, kk * N_CHIPS + q, sibling)
                for q in range(N_CHIPS) for kk in range(n)]

    def start(ins, outs, sems):
        _start_all(copies(ins, outs, sems))

    def finish(ins, outs, sems):
        cps = copies(ins, outs, sems)
        for cp in cps:
            cp.wait_recv()
        for cp in cps:
            cp.wait_send()

    return _Comm(arrs, [jax.ShapeDtypeStruct((N_CHIPS,) + a.shape[1:], a.dtype) for a in arrs],
                 _dma_sems(N_CHIPS * n, N_CHIPS * n), start, finish)


def _chip_swap_comm(arrs):
    n = len(arrs)
    per = N_CHIPS - 1

    def copies(ins, outs, sems):
        send_sems, recv_sems, local_sems = sems
        x, y, c, _, chips = _mesh_place()
        mine = 2 * x + y
        sends = [_remote(ins[kk].at[2 * chip[0] + chip[1]], outs[kk].at[mine], send_sems, recv_sems, kk * per + j, (*chip, c))
                 for j, chip in enumerate(chips) for kk in range(n)]
        recvs = [_remote(ins[kk].at[mine], outs[kk].at[2 * chip[0] + chip[1]], send_sems, recv_sems, kk * per + j, (*chip, c))
                 for j, chip in enumerate(chips) for kk in range(n)]
        local = [pltpu.make_async_copy(ins[kk].at[mine], outs[kk].at[mine], local_sems.at[kk]) for kk in range(n)]
        return local, sends, recvs

    def start(ins, outs, sems):
        local, sends, _ = copies(ins, outs, sems)
        _start_all(sends + local)

    def finish(ins, outs, sems):
        local, sends, recvs = copies(ins, outs, sems)
        for cp in recvs:
            cp.wait_recv()
        for cp in sends:
            cp.wait_send()
        for cp in local:
            cp.wait()

    return _Comm(arrs, [jax.ShapeDtypeStruct(a.shape, a.dtype) for a in arrs], _dma_sems(per * n, per * n, n), start, finish)


def _ada_mod(c_all, w_ada, b_cols, comms=None):
    nl, d, w = w_ada.shape

    def body(c_ref, w_ref, b_ref, m_ref, ca_ref):
        sub = _iota((SUBLANES, d), 0)
        cv = jnp.zeros((SUBLANES, d), F32)
        for b in range(N_DEV):
            cv = jnp.where(sub == b, c_ref[b], cv)
        ca = cv * _sigmoid(cv)
        ca_ref[...] = ca
        m_ref[...] = jnp.zeros_like(m_ref)
        for l in range(nl):
            ml = _mm_hi(ca, w_ref[l]) + b_ref[l:l + 1, :]
            for b in range(N_DEV):
                m_ref[b, l:l + 1, :] = _row(ml, b)

    return _call(
        body, comms, name="adaln_mod_columns", grid=(1,),
        in_specs=[_full(c_all.shape), _full(w_ada.shape), _full(b_cols.shape)],
        out_specs=[_full((N_DEV, SUBLANES, w)), _full((SUBLANES, d))],
        out_shape=[jax.ShapeDtypeStruct((N_DEV, SUBLANES, w), F32), jax.ShapeDtypeStruct((SUBLANES, d), F32)],
        args=(c_all, w_ada, b_cols))


def _ada_grad_adam(cact_t, dmods, w, m, v, comms=None):
    nl, d, wd = w.shape
    tr = _row_tile(d, wd, 8)
    nt = d // tr

    def body(c_ref, dm_ref, w_ref, m_ref, v_ref, g_ref, d_ref, mo_ref, vo_ref):
        cv = c_ref[...]
        dm = dm_ref[0]
        g = _col(cv, 0) * _row(dm, 0)
        for b in range(1, N_DEV):
            g = g + _col(cv, b) * _row(dm, b)
        delta, m2, v2 = _adam_math(w_ref[0], g, m_ref[0], v_ref[0])
        g_ref[0] = g
        d_ref[0] = delta
        mo_ref[0] = m2
        vo_ref[0] = v2

    blk = pl.BlockSpec((1, tr, wd), lambda i: (i // nt, i % nt, 0))
    return _call(
        body, comms, name="adaln_grad_adam", grid=(nl * nt,),
        in_specs=[pl.BlockSpec((tr, N_DEV), lambda i: (i % nt, 0)), pl.BlockSpec((1, N_DEV, wd), lambda i: (i // nt, 0, 0)),
                  blk, blk, blk],
        out_specs=[blk] * 4, out_shape=[jax.ShapeDtypeStruct((nl, d, wd), F32)] * 4,
        args=(cact_t, dmods, w, m, v))


REP_ROWS = ("norm_g", "dshift", "dscale", "dgate", "rg_conv_b", "rg_b_a", "rg_b_x", "rg_lambda", "ml_conv_b", "ml_norm_g",
            "ml_b_if")


def _sum_parts(recvs, name):
    def body(*refs):
        for r_ref, o_ref in zip(refs[:len(recvs)], refs[len(recvs):]):
            o_ref[...] = _sum_devices(r_ref).astype(o_ref.dtype)

    return pl.pallas_call(
        body, name=name, grid=(1,),
        in_specs=[_full(r.shape) for r in recvs], out_specs=[_full(r.shape[1:]) for r in recvs],
        out_shape=[jax.ShapeDtypeStruct(r.shape[1:], r.dtype) for r in recvs], compiler_params=_params(1),
    )(*recvs)


def _adam_replicated(vp, mp, params, nl):
    d = vp.shape[2]
    nr = len(REP_ROWS)
    names = list(params)
    mat_shape = params["rg_w_a"][0].shape[1:]
    mat_rows = mp.shape[0] // (2 * nl)

    def pieces(name):
        if name == "final_g":
            return [(lambda vp_ref, mp_ref: vp_ref[nl * nr:nl * nr + 1, :], (slice(0, 1), slice(None)))]
        out = []
        for l in range(nl):
            if name in ("rg_w_a", "rg_w_x"):
                at = (2 * l + (name == "rg_w_x")) * mat_rows
                out.append((lambda vp_ref, mp_ref, at=at: mp_ref[at:at + mat_rows, :].astype(F32).reshape(mat_shape), l))
            elif name == "b_ada":
                for j in range(3):
                    r = l * nr + 1 + j
                    out.append((lambda vp_ref, mp_ref, r=r: vp_ref[r:r + 1, :], (slice(l, l + 1), slice(j * d, (j + 1) * d))))
            else:
                r = l * nr + REP_ROWS.index(name)
                cols = slice(0, LANES) if name == "ml_b_if" else slice(None)
                out.append((lambda vp_ref, mp_ref, r=r, cols=cols: vp_ref[r:r + 1, cols], (slice(l, l + 1), slice(None))))
        return out

    def body(*refs):
        parts_ref, mp_ref, vp_ref = refs[0], refs[1], refs[-1]
        ins, outs = refs[2:2 + 3 * len(names)], refs[2 + 3 * len(names):-1]
        vp_ref[...] = _sum_devices(parts_ref)
        for pi, name in enumerate(names):
            w_ref, m_ref, v_ref = ins[3 * pi:3 * pi + 3]
            g_ref, d_ref, mo_ref, vo_ref = outs[4 * pi:4 * pi + 4]
            for get, idx in pieces(name):
                g = get(vp_ref, mp_ref)
                delta, m2, v2 = _adam_math(w_ref[idx], g, m_ref[idx], v_ref[idx])
                g_ref[idx] = g
                d_ref[idx] = delta
                mo_ref[idx] = m2
                vo_ref[idx] = v2

    flat = [a for name in names for a in params[name]]
    out_shape = [jax.ShapeDtypeStruct(params[name][0].shape, F32) for name in names for _ in range(4)]
    out_shape.append(jax.ShapeDtypeStruct(vp.shape[1:], F32))
    res = pl.pallas_call(
        body, name="adam_replicated", grid=(1,),
        in_specs=[_full(vp.shape), _full(mp.shape)] + [_full(a.shape) for a in flat],
        out_specs=[_full(o.shape) for o in out_shape], out_shape=out_shape, compiler_params=_params(1),
    )(vp, mp, *flat)
    return {name: res[4 * pi:4 * pi + 4] for pi, name in enumerate(names)}, res[-1]


class _Plan:
    def __init__(self):
        self.hosted, self.after = {}, {}

    def host(self, key, comm, then=None):
        self.hosted.setdefault(key, []).append(comm)
        if then is not None:
            self.after.setdefault(key, []).append(then)

    def comms(self, key):
        return self.hosted.pop(key, None)

    def done(self, key):
        for fn in self.after.pop(key, []):
            fn()

    def flush(self):
        while self.hosted:
            key = next(iter(self.hosted))
            _call(lambda: None, self.comms(key), name="exchange_after_%s_%d" % key, grid=(1,), in_specs=[], out_specs=[],
                  out_shape=[], args=())
            self.done(key)


VEC_TABLE = ("norm_g", "rg_conv_b", "rg_b_a", "rg_b_x", "rg_lambda", "ml_conv_b", "ml_norm_g")


def _vec_table(rep):
    rows = [rep[n] for n in VEC_TABLE]
    return jnp.stack(rows + [jnp.zeros_like(rows[0])] * (SUBLANES - len(rows)), axis=1)


def _layer_fwd(l, xl, mod3, wl, rep, plan, head=None):
    s, d = xl.shape
    t_big, t_mid = _tile_for(s, 512), _tile_for(s, 256)
    nh_ml = rep["ml_b_if"].shape[1] // 2
    vec = lambda name: _vec(rep["vecs"], l, VEC_TABLE.index(name))
    shift, scale, gate = (_vec(mod3, l, kk) for kk in range(3))
    hosted = lambda name: plan.comms((name, l)) if plan else None
    done = lambda name: plan.done((name, l)) if plan else None
    u, hbf = _in_fwd(xl, vec("norm_g"), scale, shift, wl["w_in_g"], 0, t_big, hosted("in_proj_fwd"))
    done("in_proj_fwd")
    h_rg, y_rg, *rg_gates = _rg_fwd(u, d, wl["rg_conv_w"], vec("rg_conv_b"), rep["rg_w_a_bf"][l], vec("rg_b_a"),
                                    rep["rg_w_x_bf"][l], vec("rg_b_x"), vec("rg_lambda"), t_mid, hosted("rglru_fwd"))
    done("rglru_fwd")
    q, k, v, gcol, pre = _ml_pre(u, d, wl["ml_conv_w"], vec("ml_conv_b"), wl["w_qkv"][0], wl["w_qkv"][1],
                                 wl["w_qkv"][2], wl["wif_pad"], wl["bif_pad"], t_mid, hosted("mlstm_proj_fwd"))
    done("mlstm_proj_fwd")
    grow = gcol[:, 0:16].T
    cell, y_ml, cs, ns, ms, mt = _ml_cell_fwd(q, k, v, gcol, grow, u, vec("ml_norm_g"), nh_ml, hosted("mlstm_cell_fwd"))
    done("mlstm_cell_fwd")
    res = _out_fwd(xl, y_rg, y_ml, gate, wl["w_out_g"], 0, t_big, hosted("out_proj_fwd"), head)
    done("out_proj_fwd")
    x_new, y = (res[0], res[1]) if head is None else (tuple(res[1:]), res[0])
    saved = dict(x=xl, u=u, hbf=hbf, h_rg=h_rg, y_rg=y_rg, q=q, k=k, v=v, gcol=gcol, grow=grow, cell=cell, y_ml=y_ml,
                 cs=cs, ns=ns, ms=ms, mt=mt, y=y, scale=scale, gate=gate, rg_gates=rg_gates, pre=pre)
    return x_new, saved


def _layer_bwd(l, dx, sv, wl, rep, plan, grads=None, split_last=False):
    s, d = dx.shape
    t_big, t_mid = _tile_for(s, 512), _tile_for(s, 256)
    nh_ml = rep["ml_b_if"].shape[1] // 2
    nd, _, _, w_cols = wl["w_in_g"].shape
    grads = {} if grads is None else grads
    vec = lambda name: _vec(rep["vecs"], l, VEC_TABLE.index(name))
    hosted = lambda name: plan.comms((name, l)) if plan else None
    done = lambda name: plan.done((name, l)) if plan else None
    dy_rg, dy_ml, gw_out, dgate = _out_bwd(dx, sv["gate"], sv["y"], sv["y_rg"], sv["y_ml"], wl["w_out_g"], 0, t_big,
                                           hosted("out_proj_bwd"))
    grads.update(w_out=gw_out)
    done("out_proj_bwd")
    dq, dk, dv, dgates, d_mlo, d_mlz, g_mlng = _ml_cell_bwd(
        dy_ml, sv["u"], sv["cell"], sv["q"], sv["k"], sv["v"], sv["gcol"], sv["grow"], sv["mt"], sv["cs"], sv["ns"],
        sv["ms"], vec("ml_norm_g"), nh_ml, hosted("mlstm_cell_bwd"))
    done("mlstm_cell_bwd")
    d_mlx, g_wq, g_wk, g_wv, g_wift, g_bif, g_mlcw, g_mlcb = _ml_pre_bwd(
        dq, dk, dv, dgates, sv["gcol"], sv["u"], sv["pre"], sv["q"], sv["k"], sv["v"], wl["ml_conv_w"],
        wl["w_qkv"][0], wl["w_qkv"][1], wl["w_qkv"][2], wl["wift_pad"], t_mid, hosted("mlstm_proj_bwd"))
    done("mlstm_proj_bwd")
    d_rgx, d_rgz, g_wa, g_wx, g_ba, g_bx, g_lam, g_rgcw, g_rgcb = _rg_bwd(
        dy_rg, sv["u"], sv["h_rg"], sv["rg_gates"], wl["rg_conv_w"], rep["rg_w_a_bf"][l], rep["rg_w_x_bf"][l],
        vec("rg_lambda"), t_mid, hosted("rglru_bwd"))
    grads.update(w_qkv=jnp.stack([g_wq, g_wk, g_wv]), rg_conv_w=g_rgcw[0:CONV_WIDTH], ml_conv_w=g_mlcw[0:CONV_WIDTH],
                 wif_t=g_wift[0:8], rg_w_a=g_wa, rg_w_x=g_wx)
    acc = dict(dgate=dgate, rg_conv_b=g_rgcb, rg_b_a=g_ba, rg_b_x=g_bx, rg_lambda=g_lam, ml_conv_b=g_mlcb,
               ml_b_if=g_bif, ml_norm_g=g_mlng)
    done("rglru_bwd")
    pieces = [d_rgx, d_rgz, d_mlx, d_mlo, d_mlz]
    grads.update(w_in=_in_bwd_w(pieces, sv["hbf"], w_cols, tuple(range(nd)), _tile_for(s, 1024), hosted("in_proj_bwd_w")))
    done("in_proj_bwd_w")
    n_tiles = s // t_mid
    counts = [n_tiles // 5, n_tiles - n_tiles // 5 - 1, 1] if split_last and n_tiles >= 5 else [n_tiles]
    in_args = (pieces, sv["x"], dx, vec("norm_g"), sv["scale"], wl["w_in_g"], 0, t_mid)
    res, at = None, 0
    for key, count in zip(("in_proj_bwd_x", "in_proj_bwd_x_rest", "in_proj_bwd_x_end"), counts):
        res = _in_bwd(*in_args, hosted(key), (at, count), res)
        done(key)
        at += count
    dx, dscale, dshift, g_ng = res
    acc.update(norm_g=g_ng, dshift=dshift, dscale=dscale)
    grads.update(acc=acc, dmod=jnp.concatenate([dshift[0:1], dscale[0:1], dgate[0:1]], axis=1))
    return dx, grads


def _full_qkv(qkv_g, d):
    nd, _, rows3, dh = qkv_g.shape
    nh = d // dh
    rsh = rows3 // (3 * nh)
    return qkv_g.reshape(nd, 3, nh, rsh, dh).transpose(1, 2, 0, 3, 4).reshape(3, nh, nd * rsh, dh)


def _small_weights(small, l, ml_b_if):
    nd = small.shape[0]
    sm = small[:, l]
    cw = sm[:, 0:2 * CONV_WIDTH].reshape(nd, 2, CONV_WIDTH, LANES).transpose(1, 2, 0, 3).reshape(2, CONV_WIDTH, nd * LANES)
    if_rows = (sm.shape[1] - 2 * CONV_WIDTH) * LANES // 8
    wif_t = sm[:, 2 * CONV_WIDTH:].reshape(nd, 8, if_rows).transpose(1, 0, 2).reshape(8, nd * if_rows)
    wift_pad = jnp.pad(wif_t, ((0, LANES - 8), (0, 0))).astype(BF16)
    return dict(rg_conv_w=cw[0], ml_conv_w=cw[1], wift_pad=wift_pad, wif_pad=wift_pad.T,
                bif_pad=jnp.pad(ml_b_if[l], (0, LANES - 8)).reshape(1, LANES))


def kernel(x, c, norm_g, w_ada, b_ada, w_in, rg_conv_w, rg_conv_b, rg_w_a, rg_b_a, rg_w_x, rg_b_x, rg_lambda, ml_conv_w, ml_conv_b, ml_w_q, ml_w_k, ml_w_v, ml_w_if, ml_b_if, ml_norm_g, w_out, final_g, loss_target, m_norm_g, m_w_ada, m_b_ada, m_w_in, m_rg_conv_w, m_rg_conv_b, m_rg_w_a, m_rg_b_a, m_rg_w_x, m_rg_b_x, m_rg_lambda, m_ml_conv_w, m_ml_conv_b, m_ml_w_q, m_ml_w_k, m_ml_w_v, m_ml_w_if, m_ml_b_if, m_ml_norm_g, m_w_out, m_final_g, v_norm_g, v_w_ada, v_b_ada, v_w_in, v_rg_conv_w, v_rg_conv_b, v_rg_w_a, v_rg_b_a, v_rg_w_x, v_rg_b_x, v_rg_lambda, v_ml_conv_w, v_ml_conv_b, v_ml_w_q, v_ml_w_k, v_ml_w_v, v_ml_w_if, v_ml_b_if, v_ml_norm_g, v_w_out, v_final_g):
    given = dict(locals())
    nl = w_in.shape[0]
    d = x.shape[2]
    rep = {n: given[n] for n in REPLICATED}
    rep.update(rg_w_a_bf=rg_w_a.astype(BF16), rg_w_x_bf=rg_w_x.astype(BF16))
    bf = lambda a: a.astype(BF16)

    def qkv_shard(prefix):
        return jnp.stack([given[prefix + "ml_w_q"], given[prefix + "ml_w_k"], given[prefix + "ml_w_v"]], axis=1).reshape(
            nl, -1, ml_w_q.shape[-1])

    def small_shard(prefix):
        return _small_pack(given[prefix + "rg_conv_w"], given[prefix + "ml_conv_w"], given[prefix + "ml_w_if"])

    plan = _Plan()
    qkv = qkv_shard("")
    first_ici = _gather_ici_comm([bf(w_in[0:1]), small_shard("")])
    condition = _exchange_comm([jnp.broadcast_to(c, (SUBLANES, d))], True)
    _run_comms([first_ici, condition], "gather_first")
    first_fwd = _gather_fwd_comm(first_ici.results)
    wcols = w_ada.shape[2]
    me = 4 * lax.axis_index("x") + 2 * lax.axis_index("y") + lax.axis_index("c")
    b_cols = jnp.pad(lax.dynamic_slice_in_dim(b_ada, me * wcols, wcols, axis=1), ((0, SUBLANES - nl), (0, 0)))
    mod_cols, cact_all = _ada_mod(condition.results[0], w_ada, b_cols, [first_fwd])
    w_in_first, small = first_fwd.results
    wl = [_small_weights(small, l, ml_b_if) for l in range(nl)]
    wl[0]["w_in_g"] = w_in_first

    def gather_behind(arrs, ici_host, fwd_host, then):
        ici = _gather_ici_comm(arrs)

        def pass_on():
            fwd = _gather_fwd_comm(ici.results)
            plan.host(fwd_host, fwd, lambda: then(fwd.results))

        plan.host(ici_host, ici, pass_on)

    def got_out(l):
        return lambda r: wl[l].update(w_out_g=r[0], w_qkv=_full_qkv(r[1], d))

    gather_behind([bf(w_out[0:1]), bf(qkv[0:1])], ("in_proj_fwd", 0), ("rglru_fwd", 0), got_out(0))
    for l in range(1, nl):
        gather_behind([bf(w_in[l:l + 1])], ("rglru_fwd", l - 1), ("mlstm_proj_fwd", l - 1),
                      lambda r, l=l: wl[l].update(w_in_g=r[0]))
        gather_behind([bf(w_out[l:l + 1]), bf(qkv[l:l + 1])], ("mlstm_cell_fwd", l - 1), ("out_proj_fwd", l - 1), got_out(l))

    mod_blocks = _exchange([mod_cols], False, "scatter_modulation")[0]
    mod3 = mod_blocks[:, 0:nl].transpose(1, 0, 2).reshape(nl, 3, d)
    rep["vecs"] = _vec_table(rep)
    saved, xl = [], x[0]
    for l in range(nl):
        head = (final_g.reshape(1, -1), loss_target[0]) if l == nl - 1 else None
        xl, sv = _layer_fwd(l, xl, mod3, wl[l], rep, plan, head)
        saved.append(sv)
    grad_x, loss_p, g_final = xl

    keys = ("w_in", "w_out", "w_qkv", "small")
    parity = lax.axis_index("c").astype(jnp.int32).reshape(1)
    grads, recv = [None] * nl, [None] * nl

    def parts_of(g):
        return [g["w_in"], g["w_out"], _qkv_slots(g["w_qkv"], N_DEV), _small_slots(g)]

    def pair_sums(l, parts, other):
        return [_pair_sum(a, o, parity, "pair_sum_%s_layer%d" % (key, l)) for key, a, o in zip(keys, parts, other)]

    def reduce_behind(l, host_layer):
        parts = parts_of(grads[l])
        swap = _core_swap_comm(parts)

        def summed():
            sums = pair_sums(l, parts, swap.results)
            big = _chip_swap_comm([sums[0]])
            rest = _chip_swap_comm(sums[1:])
            plan.host(("mlstm_cell_bwd", host_layer), big)
            plan.host(("rglru_bwd", host_layer), rest, lambda: recv.__setitem__(l, big.results + rest.results))

        plan.host(("out_proj_bwd", host_layer), swap, summed)

    first, own = {}, {}

    def reduce_own(names, parts_fn, ready_key, swap_key, chip_key):
        def go():
            parts = parts_fn()
            swap = _core_swap_comm(parts)

            def summed():
                sums = [_pair_sum(a, o, parity, "pair_sum_%s_layer0" % n) for n, a, o in zip(names, parts, swap.results)]
                chip = _chip_swap_comm(sums)
                plan.host(chip_key, chip, lambda: own.update(zip(names, chip.results)))

            plan.host(swap_key, swap, summed)

        plan.after.setdefault(ready_key, []).append(go)

    reduce_own(["w_out"], lambda: [first["w_out"]], ("out_proj_bwd", 0), ("mlstm_cell_bwd", 0), ("mlstm_proj_bwd", 0))
    reduce_own(["w_qkv", "small"], lambda: [_qkv_slots(first["w_qkv"], N_DEV), _small_slots(first)],
               ("rglru_bwd", 0), ("in_proj_bwd_w", 0), ("in_proj_bwd_x", 0))
    reduce_own(["w_in"], lambda: [first["w_in"]], ("in_proj_bwd_w", 0), ("in_proj_bwd_x", 0), ("in_proj_bwd_x_rest", 0))

    matrices = {}

    def reduce_matrices():
        layers = [grads[l] if l > 0 else first for l in range(nl)]
        mp = jnp.stack([jnp.stack([g["rg_w_a"], g["rg_w_x"]]) for g in layers]).reshape(N_DEV, -1, LANES).astype(BF16)
        scatter = _exchange_comm([mp], False)

        def summed():
            gather = _exchange_comm(_sum_parts(scatter.results, "sum_replicated_matrices"), True)
            plan.host(("in_proj_bwd_x", 0), gather, lambda: matrices.update(mp=gather.results[0].reshape(-1, LANES)))

        plan.host(("in_proj_bwd_w", 0), scatter, summed)

    plan.after.setdefault(("rglru_bwd", 0), []).append(reduce_matrices)

    for l in reversed(range(nl)):
        if l > 0:
            grad_x, grads[l] = _layer_bwd(l, grad_x, saved[l], wl[l], rep, plan)
            reduce_behind(l, l - 1)
        else:
            grad_x, grads[l] = _layer_bwd(l, grad_x, saved[l], wl[l], rep, plan, first, True)
    plan.flush()
    recv[0] = [own[key] for key in keys]

    shard = {p: dict(w_in=given[p + "w_in"], w_out=given[p + "w_out"], w_qkv=qkv_shard(p), small=small_shard(p))
             for p in ("", "m_", "v_")}
    res = {}
    for ki, key in enumerate(keys):
        res[key] = _reduce_adam([recv[l][ki] for l in range(nl)], shard[""][key], shard["m_"][key], shard["v_"][key],
                                "reduce_adam_" + key)

    dmods = jnp.concatenate([grads[l]["dmod"] for l in range(nl)], axis=0)
    dmod_blocks = jnp.pad(dmods.reshape(nl, N_DEV, wcols).transpose(1, 0, 2), ((0, 0), (0, SUBLANES - nl), (0, 0)))
    dmod_all = _exchange([dmod_blocks], False, "scatter_dmod")[0][:, 0:nl].transpose(1, 0, 2)
    res["w_ada"] = _ada_grad_adam(cact_all.T, dmod_all, w_ada, m_w_ada, v_w_ada)

    widen = lambda a: jnp.pad(a, ((0, 0), (0, d - a.shape[1])))
    rows = [widen(grads[l]["acc"][n][0:1]) for l in range(nl) for n in REP_ROWS] + [g_final[0:1], widen(loss_p[0:1])]
    vp = jnp.concatenate(rows + [jnp.zeros(((-len(rows)) % SUBLANES, d), F32)], axis=0)
    vp_all = _exchange([vp], True, "gather_replicated")[0]
    mp_r = matrices["mp"]
    lanes = lambda a: jnp.pad(a, ((0, 0), (0, LANES - a.shape[1])))
    shaped = dict(ml_b_if=lanes, final_g=lambda a: a.reshape(1, d))
    names = [n for n in REPLICATED if n != "b_ada"] + ["b_ada"]
    rep_res, vp_r = _adam_replicated(vp_all, mp_r, {n: tuple(shaped.get(n, lambda a: a)(given[p + n]) for p in ("", "m_", "v_"))
                                                    for n in names}, nl)
    unshaped = dict(ml_b_if=lambda a: a[:, 0:ml_b_if.shape[1]], final_g=lambda a: a.reshape(d))
    rep_out = [{n: unshaped.get(n, lambda a: a)(rep_res[n][kind]) for n in names} for kind in range(4)]
    loss = vp_r[nl * len(REP_ROWS) + 1, 0]

    if_rows = ml_w_if.shape[1]
    order = ("norm_g", "w_ada", "b_ada", "w_in", "rg_conv_w", "rg_conv_b", "rg_w_a", "rg_b_a", "rg_w_x", "rg_b_x",
             "rg_lambda", "ml_conv_w", "ml_conv_b", "ml_w_q", "ml_w_k", "ml_w_v", "ml_w_if", "ml_b_if", "ml_norm_g",
             "w_out", "final_g")
    outs = [loss, grad_x[None]]
    for kind in range(4):
        qkv_k = res["w_qkv"][kind].reshape((nl, 3) + ml_w_q.shape[1:])
        rg_cw, ml_cw, wif = _small_unpack(res["small"][kind], if_rows)
        sharded = dict(w_ada=res["w_ada"][kind], w_in=res["w_in"][kind], w_out=res["w_out"][kind], ml_w_q=qkv_k[:, 0],
                       ml_w_k=qkv_k[:, 1], ml_w_v=qkv_k[:, 2], rg_conv_w=rg_cw, ml_conv_w=ml_cw, ml_w_if=wif)
        for n in order:
            outs.append(sharded[n] if n in sharded else rep_out[kind][n])
    return tuple(outs)
```

```python
import functools

import jax
import jax.numpy as jnp
from jax import lax
from jax.experimental import pallas as pl
from jax.experimental.pallas import tpu as pltpu

F32 = jnp.float32
BF16 = jnp.bfloat16
MESH_AXES = ("x", "y", "c")
N_DEV = 8
EPS = 1e-6
RG_C = 8.0
ML_CHUNK = 128
CONV_WIDTH = 4
ADAM_LR = 0.001
ADAM_B1 = 0.9
ADAM_B2 = 0.999
ADAM_EPS = 1e-08
ADAM_WD = 0.01
ADAM_STEP = 10
NEG_BIG = -1e30
LANES = 128
SUBLANES = 8
VMEM_LIMIT = 56 * 1024 * 1024
HI = lax.Precision.HIGHEST


def _params(n_grid):
    return pltpu.CompilerParams(dimension_semantics=("arbitrary",) * n_grid, vmem_limit_bytes=VMEM_LIMIT)


def _mm(a, b):
    return jnp.dot(a.astype(BF16), b.astype(BF16), preferred_element_type=F32)


def _mm_nt(a, b):
    return lax.dot_general(a.astype(BF16), b.astype(BF16), (((1,), (1,)), ((), ())), preferred_element_type=F32)


def _mm_tn(a, b):
    return lax.dot_general(a.astype(BF16), b.astype(BF16), (((0,), (0,)), ((), ())), preferred_element_type=F32)


def _mm_hi(a, b):
    return jnp.dot(a, b, precision=HI, preferred_element_type=F32)


def _sigmoid(x):
    return 1.0 / (1.0 + jnp.exp(-x))


def _softplus(x):
    return jnp.maximum(x, 0.0) + jnp.log(1.0 + jnp.exp(-jnp.abs(x)))


def _neg_expm1(x):
    poly = -x * (1.0 + x * (0.5 + x * (1.0 / 6.0 + x * (1.0 / 24.0 + x * (1.0 / 120.0)))))
    return jnp.where(jnp.abs(x) < 0.05, poly, 1.0 - jnp.exp(x))


def _iota(shape, dim):
    return lax.broadcasted_iota(jnp.int32, shape, dim)


def _colsum(x):
    return jnp.sum(x, axis=0, keepdims=True)


def _rowsum(x):
    return jnp.sum(x, axis=1, keepdims=True)


def _col(x, j):
    return _rowsum(jnp.where(_iota(x.shape, 1) == j, x, 0.0))


def _row(x, j):
    return _colsum(jnp.where(_iota(x.shape, 0) == j, x, 0.0))


def _shift_down(x, j, prev8):
    if j == 0:
        return x
    t = x.shape[0]
    main = jnp.where(_iota(x.shape, 0) >= j, pltpu.roll(x, j, 0), 0.0)
    fix = jnp.where(_iota(prev8.shape, 0) < j, pltpu.roll(prev8, j, 0), 0.0)
    return jnp.concatenate([main[0:SUBLANES] + fix, main[SUBLANES:t]], axis=0)


def _shift_up(x, j, next8):
    if j == 0:
        return x
    t = x.shape[0]
    main = jnp.where(_iota(x.shape, 0) < t - j, pltpu.roll(x, t - j, 0), 0.0)
    fix = jnp.where(_iota(next8.shape, 0) >= SUBLANES - j, pltpu.roll(next8, SUBLANES - j, 0), 0.0)
    return jnp.concatenate([main[0:t - SUBLANES], main[t - SUBLANES:t] + fix], axis=0)


def _conv(x, prev8, w_ref):
    y = w_ref[CONV_WIDTH - 1:CONV_WIDTH, :] * x
    for j in range(1, CONV_WIDTH):
        y = y + w_ref[CONV_WIDTH - 1 - j:CONV_WIDTH - j, :] * _shift_down(x, j, prev8)
    return y


def _conv_bwd(dy, x, next8, w_ref, gw_ref):
    dx = None
    for j in range(CONV_WIDTH):
        k = CONV_WIDTH - 1 - j
        up = _shift_up(dy, j, next8)
        gw_ref[k:k + 1, :] += _colsum(up * x)
        term = w_ref[k:k + 1, :] * up
        dx = term if dx is None else dx + term
    return dx


def _scan_into(a, b, carry, out_ref, reverse):
    t, c = a.shape
    groups = t // SUBLANES
    a3 = a.reshape(groups, SUBLANES, c)
    b3 = b.reshape(groups, SUBLANES, c)
    sub = _iota(a3.shape, 1)
    for step in (1, 2, 4):
        keep = sub < SUBLANES - step if reverse else sub >= step
        shift = SUBLANES - step if reverse else step
        a_s = jnp.where(keep, pltpu.roll(a3, shift, 1), 1.0)
        b_s = jnp.where(keep, pltpu.roll(b3, shift, 1), 0.0)
        b3 = a3 * b_s + b3
        a3 = a3 * a_s
    for g in (reversed(range(groups)) if reverse else range(groups)):
        rows = slice(g * SUBLANES, (g + 1) * SUBLANES)
        out_ref[rows, :] = b3[g] + a3[g] * carry
        edge = g * SUBLANES if reverse else (g + 1) * SUBLANES - 1
        carry = out_ref[edge:edge + 1, :]


def _blockdiag(x, w_ref, transpose_w=False):
    nh, dh, _ = w_ref.shape
    outs = []
    for h in range(nh):
        xs = x[:, h * dh:(h + 1) * dh]
        outs.append(_mm_nt(xs, w_ref[h]) if transpose_w else _mm(xs, w_ref[h]))
    return jnp.concatenate(outs, axis=1)


def _rg_gates(xc, wa_ref, ba_ref, wx_ref, bx_ref, lam_ref):
    r = _sigmoid(_blockdiag(xc, wa_ref) + ba_ref[...])
    ig = _sigmoid(_blockdiag(xc, wx_ref) + bx_ref[...])
    sp = _softplus(-lam_ref[...])
    log_a = -RG_C * r * sp
    a = jnp.exp(log_a)
    beta = jnp.sqrt(_neg_expm1(2.0 * log_a))
    return r, ig, sp, a, beta


def _bcast8(row):
    return jnp.broadcast_to(row, (SUBLANES, row.shape[1]))


def _full(shape):
    nd = len(shape)
    return pl.BlockSpec(shape, lambda *_: (0,) * nd)


class _Comm:
    def __init__(self, arrays, out_shapes, sems, start, finish, aliases=()):
        self.arrays, self.out_shapes, self.sems = list(arrays), list(out_shapes), list(sems)
        self.start, self.finish, self.aliases = start, finish, tuple(aliases)
        self.results = None


class _RowOf:
    def __init__(self, ref, k):
        self.ref, self.k = ref, k

    def __getitem__(self, idx):
        cols = slice(None) if idx is Ellipsis else idx[1]
        return self.ref[0, self.k:self.k + 1, cols]


class _PartOf:
    def __init__(self, ref, rows=None, cols=None, lead=None):
        self.ref, self.rows, self.cols, self.lead = ref, rows, cols, lead
        if rows is not None:
            self.shape = (rows.stop - rows.start,) + tuple(ref.shape[1:])

    def _at(self, idx):
        if self.lead is not None:
            return (self.lead,) + tuple(idx[1:])
        if self.cols is not None:
            return (slice(None), self.cols)
        return (self.rows, slice(None) if idx is Ellipsis else idx[1])

    def __getitem__(self, idx):
        return self.ref[self._at(idx)]

    def __setitem__(self, idx, value):
        self.ref[self._at(idx)] = value


def _vec(table, layer, k):
    return ("row", table, layer, k)


def _is_row(arg):
    return isinstance(arg, tuple) and len(arg) == 4 and arg[0] == "row"


def _call(body, comms, *, name, grid, in_specs, out_specs, out_shape, args, scratch_shapes=(), aliases=None):
    comms = [cm for cm in (comms or []) if cm is not None]
    rows = {i: a[3] for i, a in enumerate(args) if _is_row(a)}
    in_specs = [pl.BlockSpec((1,) + a[1].shape[1:], functools.partial(lambda layer, *_: (layer, 0, 0), a[2]))
                if _is_row(a) else sp for a, sp in zip(args, in_specs)]
    args = tuple(a[1] if _is_row(a) else a for a in args)
    n_in, n_out, n_sc = len(args), len(out_shape), len(scratch_shapes)
    c_arrays = [a for cm in comms for a in cm.arrays]
    c_outs = [o for cm in comms for o in cm.out_shapes]
    c_sems = [sm for cm in comms for sm in cm.sems]
    aliases, a_at, o_at = dict(aliases or {}), n_in, n_out
    for cm in comms:
        for (i, j) in cm.aliases:
            aliases[a_at + i] = o_at + j
        a_at += len(cm.arrays)
        o_at += len(cm.out_shapes)

    def wrapped(*refs):
        ins, c_in = refs[:n_in], refs[n_in:n_in + len(c_arrays)]
        ins = [_RowOf(r, rows[i]) if i in rows else r for i, r in enumerate(ins)]
        at = n_in + len(c_arrays)
        outs, c_out = refs[at:at + n_out], refs[at + n_out:at + n_out + len(c_outs)]
        at += n_out + len(c_outs)
        scr, sems = refs[at:at + n_sc], refs[at + n_sc:]
        views, ia, io, isem = [], 0, 0, 0
        for cm in comms:
            views.append((c_in[ia:ia + len(cm.arrays)], c_out[io:io + len(cm.out_shapes)], sems[isem:isem + len(cm.sems)]))
            ia, io, isem = ia + len(cm.arrays), io + len(cm.out_shapes), isem + len(cm.sems)
        if comms:
            @pl.when(pl.program_id(0) == 0)
            def _():
                for cm, view in zip(comms, views):
                    cm.start(*view)

        body(*ins, *outs, *scr)
        if comms:
            @pl.when(pl.program_id(0) == grid[0] - 1)
            def _():
                for cm, view in zip(comms, views):
                    cm.finish(*view)

    hbm = pl.BlockSpec(memory_space=pl.ANY)
    res = pl.pallas_call(
        wrapped, name=name, grid=grid,
        in_specs=list(in_specs) + [hbm] * len(c_arrays), out_specs=list(out_specs) + [hbm] * len(c_outs),
        out_shape=list(out_shape) + c_outs, scratch_shapes=list(scratch_shapes) + c_sems,
        input_output_aliases=aliases, compiler_params=_params(len(grid)),
    )(*args, *c_arrays)
    at = n_out
    for cm in comms:
        cm.results = list(res[at:at + len(cm.out_shapes)])
        at += len(cm.out_shapes)
    return list(res[:n_out])


def _join_columns(w_ref, wcat):
    nd, _, _, w = w_ref.shape

    @pl.when(pl.program_id(0) == 0)
    def _():
        for j in range(nd):
            wcat[:, j * w:(j + 1) * w] = w_ref[j, 0]


def _in_fwd(x, ng, scale, shift, w_in_g, layer, tile, comms=None):
    s, d = x.shape
    nd, _, _, w = w_in_g.shape

    def body(x_ref, ng_ref, sc_ref, sh_ref, w_ref, u_ref, h_ref, wcat):
        _join_columns(w_ref, wcat)
        xv = x_ref[...]
        rs = lax.rsqrt(jnp.mean(xv * xv, axis=1, keepdims=True) + EPS)
        hb = (xv * rs * ng_ref[...] * (1.0 + sc_ref[...]) + sh_ref[...]).astype(BF16)
        h_ref[...] = hb
        u_ref[...] = jnp.dot(hb, wcat[...], preferred_element_type=F32)

    return _call(
        body, comms, name="in_proj_fwd", grid=(s // tile,),
        in_specs=[pl.BlockSpec((tile, d), lambda i: (i, 0)), _full((1, d)), _full((1, d)), _full((1, d)),
                  pl.BlockSpec((nd, 1, d, w), lambda i: (0, layer, 0, 0), pipeline_mode=pl.Buffered(1))],
        out_specs=[pl.BlockSpec((tile, nd * w), lambda i: (i, 0)), pl.BlockSpec((tile, d), lambda i: (i, 0))],
        out_shape=[jax.ShapeDtypeStruct((s, nd * w), F32), jax.ShapeDtypeStruct((s, d), BF16)],
        scratch_shapes=[pltpu.VMEM((d, nd * w), BF16)],
        args=(x, ng, scale, shift, w_in_g))


def _rg_fwd(u, d, conv_w, conv_b, w_a, b_a, w_x, b_x, lam, tile, comms=None):
    s = u.shape[0]

    def body(x_ref, z_ref, cw_ref, cb_ref, wa_ref, ba_ref, wx_ref, bx_ref, lam_ref,
             h_ref, y_ref, xc_ref, r_ref, i_ref, a_ref, beta_ref, prev8, hcar):
        @pl.when(pl.program_id(0) == 0)
        def _():
            prev8[...] = jnp.zeros_like(prev8)
            hcar[...] = jnp.zeros_like(hcar)

        x = x_ref[...]
        xc = _conv(x, prev8[...], cw_ref) + cb_ref[...]
        prev8[...] = x[tile - SUBLANES:tile, :]
        r, ig, _, a, beta = _rg_gates(xc, wa_ref, ba_ref, wx_ref, bx_ref, lam_ref)
        xc_ref[...] = xc
        r_ref[...] = r
        i_ref[...] = ig
        a_ref[...] = a
        beta_ref[...] = beta
        _scan_into(a, beta * ig * xc, hcar[SUBLANES - 1:SUBLANES, :], h_ref, False)
        h = h_ref[...]
        hcar[...] = h[tile - SUBLANES:tile, :]
        z = z_ref[...]
        y_ref[...] = (h * z * _sigmoid(z)).astype(BF16)

    vec = _full((1, d))
    return _call(
        body, comms, name="rglru_fwd", grid=(s // tile,),
        in_specs=[pl.BlockSpec((tile, d), lambda i: (i, 0)), pl.BlockSpec((tile, d), lambda i: (i, 1)),
                  _full(conv_w.shape), vec, _full(w_a.shape), vec, _full(w_x.shape), vec, vec],
        out_specs=[pl.BlockSpec((tile, d), lambda i: (i, 0))] * 7,
        out_shape=[jax.ShapeDtypeStruct((s, d), F32), jax.ShapeDtypeStruct((s, d), BF16)] + [jax.ShapeDtypeStruct((s, d), F32)] * 5,
        scratch_shapes=[pltpu.VMEM((SUBLANES, d), F32), pltpu.VMEM((SUBLANES, d), F32)],
        args=(u, u, conv_w, conv_b, w_a, b_a, w_x, b_x, lam))


def _ml_pre(u, d, conv_w, conv_b, w_q, w_k, w_v, wif, bif, tile, comms=None):
    s = u.shape[0]
    nh = w_q.shape[0]

    def body(x_ref, cw_ref, cb_ref, wq_ref, wk_ref, wv_ref, wif_ref, bif_ref, q_ref, k_ref, v_ref, g_ref, pre_ref, prev8):
        @pl.when(pl.program_id(0) == 0)
        def _():
            prev8[...] = jnp.zeros_like(prev8)

        x = x_ref[...]
        pre = _conv(x, prev8[...], cw_ref) + cb_ref[...]
        prev8[...] = x[tile - SUBLANES:tile, :]
        xc = pre * _sigmoid(pre)
        q = _blockdiag(xc, wq_ref)
        k = _blockdiag(xc, wk_ref)
        v = _blockdiag(x, wv_ref)
        pre_ref[...] = pre
        q_ref[...] = q
        k_ref[...] = k
        v_ref[...] = v
        g = _mm(q, wif_ref[0:d, :]) + _mm(k, wif_ref[d:2 * d, :]) + _mm(v, wif_ref[2 * d:3 * d, :]) + bif_ref[...]
        lane = _iota(g.shape, 1)
        gl = jnp.where(lane < 4, g, jnp.where(lane < 8, -_softplus(-g), 0.0))
        tri = jnp.where(_iota((ML_CHUNK, ML_CHUNK), 1) <= _iota((ML_CHUNK, ML_CHUNK), 0), 1.0, 0.0)
        cums = [_mm_hi(tri, gl[c * ML_CHUNK:(c + 1) * ML_CHUNK, :]) for c in range(tile // ML_CHUNK)]
        cum = cums[0] if len(cums) == 1 else jnp.concatenate(cums, axis=0)
        g_ref[...] = gl + jnp.where((lane >= 8) & (lane < 12), pltpu.roll(cum, 4, 1), 0.0)

    vec = _full((1, d))
    return _call(
        body, comms, name="mlstm_proj_fwd", grid=(s // tile,),
        in_specs=[pl.BlockSpec((tile, d), lambda i: (i, 2)), _full(conv_w.shape), vec,
                  _full(w_q.shape), _full(w_k.shape), _full(w_v.shape), _full(wif.shape), _full((1, LANES))],
        out_specs=[pl.BlockSpec((tile, d), lambda i: (i, 0))] * 3 + [pl.BlockSpec((tile, LANES), lambda i: (i, 0)),
                                                                     pl.BlockSpec((tile, d), lambda i: (i, 0))],
        out_shape=[jax.ShapeDtypeStruct((s, d), F32)] * 3 + [jax.ShapeDtypeStruct((s, LANES), F32),
                                                             jax.ShapeDtypeStruct((s, d), F32)],
        scratch_shapes=[pltpu.VMEM((SUBLANES, d), F32)],
        args=(u, conv_w, conv_b, w_q, w_k, w_v, wif, bif))


CELL_CHUNKS_PER_STEP = 4
CELL_BWD_CHUNKS_PER_STEP = 2


def _cell_chunk(h, nh, q_ref, k_ref, v_ref, gc, gr, m_prev, c_h, n_h, m_t=None, r0=0):
    lc = ML_CHUNK
    dh = q_ref.shape[1] // nh
    sl = slice(h * dh, (h + 1) * dh)
    qh = q_ref[r0:r0 + lc, sl]
    kh = k_ref[r0:r0 + lc, sl] * (dh ** -0.5)
    vh = v_ref[r0:r0 + lc, sl]
    li_c = _col(gc, h)
    b_c = _col(gc, 8 + h)
    lib_r = _row(gr, h) - _row(gr, 8 + h)
    b_last = _colsum(jnp.where(_iota((lc, 1), 0) == lc - 1, b_c, 0.0))
    causal = _iota((lc, lc), 1) <= _iota((lc, lc), 0)
    dmat = jnp.where(causal, b_c + lib_r, NEG_BIG)
    m_inter = b_c + m_prev
    if m_t is None:
        m_t = jnp.maximum(m_inter, jnp.max(dmat, axis=1, keepdims=True))
    w_intra = jnp.exp(dmat - m_t)
    w_inter = jnp.exp(m_inter - m_t)
    amat = _mm_nt(qh, kh)
    smat = amat * w_intra
    qc = _mm(qh, c_h)
    qn = _rowsum(qh * n_h)
    den = _rowsum(smat) + w_inter * qn
    gst = b_last - b_c + li_c
    m_new = jnp.maximum(b_last + m_prev, jnp.max(gst, axis=0, keepdims=True))
    w_state = jnp.exp(gst - m_new)
    decay = jnp.exp(b_last + m_prev - m_new)
    return dict(sl=sl, qh=qh, kh=kh, vh=vh, m_t=m_t, w_intra=w_intra, w_inter=w_inter, smat=smat, qc=qc, qn=qn,
                den=den, m_new=m_new, w_state=w_state, decay=decay)


def _ml_cell_fwd(q, k, v, gcol, grow, u, ng, nh, comms=None):
    s, d = q.shape
    lc = ML_CHUNK
    nc = s // lc
    dh = d // nh

    per = CELL_CHUNKS_PER_STEP if nc % CELL_CHUNKS_PER_STEP == 0 else 1

    def body(q_ref, k_ref, v_ref, gc_ref, gr_ref, o_ref, z_ref, ng_ref,
             cell_ref, y_ref, cs_ref, ns_ref, ms_ref, mt_ref, c_sc, n_sc, m_sc):
        @pl.when(pl.program_id(0) == 0)
        def _():
            c_sc[...] = jnp.zeros_like(c_sc)
            n_sc[...] = jnp.zeros_like(n_sc)
            m_sc[...] = jnp.zeros_like(m_sc)

        lane = _iota((lc, LANES), 1)
        for cc in range(per):
            rows = slice(cc * lc, (cc + 1) * lc)
            gc = gc_ref[rows, :]
            gr = gr_ref[:, rows]
            mt_acc = jnp.zeros((lc, LANES), F32)
            for h in range(nh):
                c_h = c_sc[h]
                n_h = n_sc[h, 0:1, :]
                m_prev = jnp.max(m_sc[h, 0:1, :], axis=1, keepdims=True)
                cs_ref[cc, h] = c_h
                ns_ref[cc, h] = n_sc[h]
                ms_ref[cc, h] = m_sc[h]
                t = _cell_chunk(h, nh, q_ref, k_ref, v_ref, gc, gr, m_prev, c_h, n_h, r0=cc * lc)
                sl = t["sl"]
                num = _mm(t["smat"], t["vh"]) + t["w_inter"] * t["qc"]
                cell_h = num / jnp.maximum(jnp.abs(t["den"]), jnp.exp(-t["m_t"]))
                mt_acc = jnp.where(lane == h, t["m_t"], mt_acc)
                kw = t["kh"] * t["w_state"]
                c_sc[h] = t["decay"] * c_h + _mm_tn(kw, t["vh"])
                n_sc[h] = _bcast8(t["decay"] * n_h + _colsum(kw))
                m_sc[h] = jnp.broadcast_to(t["m_new"], (SUBLANES, LANES))
                hg = _sigmoid(o_ref[rows, sl]) * cell_h
                hn = hg * lax.rsqrt(jnp.mean(hg * hg, axis=1, keepdims=True) + EPS)
                z = z_ref[rows, sl]
                cell_ref[rows, sl] = cell_h
                y_ref[rows, sl] = (hn * ng_ref[:, sl] * z * _sigmoid(z)).astype(BF16)
            mt_ref[rows, :] = mt_acc

    tok = pl.BlockSpec((per * lc, d), lambda c: (c, 0))
    return _call(
        body, comms, name="mlstm_cell_fwd", grid=(nc // per,),
        in_specs=[tok, tok, tok, pl.BlockSpec((per * lc, LANES), lambda c: (c, 0)),
                  pl.BlockSpec((16, per * lc), lambda c: (0, c)),
                  pl.BlockSpec((per * lc, d), lambda c: (c, 3)), pl.BlockSpec((per * lc, d), lambda c: (c, 4)), _full((1, d))],
        out_specs=[tok, tok, pl.BlockSpec((per, nh, dh, dh), lambda c: (c, 0, 0, 0)),
                   pl.BlockSpec((per, nh, SUBLANES, dh), lambda c: (c, 0, 0, 0)),
                   pl.BlockSpec((per, nh, SUBLANES, LANES), lambda c: (c, 0, 0, 0)),
                   pl.BlockSpec((per * lc, LANES), lambda c: (c, 0))],
        out_shape=[jax.ShapeDtypeStruct((s, d), F32), jax.ShapeDtypeStruct((s, d), BF16),
                   jax.ShapeDtypeStruct((nc, nh, dh, dh), F32), jax.ShapeDtypeStruct((nc, nh, SUBLANES, dh), F32),
                   jax.ShapeDtypeStruct((nc, nh, SUBLANES, LANES), F32), jax.ShapeDtypeStruct((s, LANES), F32)],
        scratch_shapes=[pltpu.VMEM((nh, dh, dh), F32), pltpu.VMEM((nh, SUBLANES, dh), F32),
                        pltpu.VMEM((nh, SUBLANES, LANES), F32)],
        args=(q, k, v, gcol, grow, u, u, ng))


def _out_fwd(x, y_rg, y_ml, gate, w_out_g, layer, tile, comms=None, head=None):
    s, d = x.shape
    nd, _, r, _ = w_out_g.shape

    def body(x_ref, yr_ref, ym_ref, g_ref, w_ref, *rest):
        ycat = jnp.concatenate([yr_ref[...].astype(BF16), ym_ref[...].astype(BF16)], axis=1)
        acc = jnp.dot(ycat, w_ref[...].reshape(nd * r, d), preferred_element_type=F32)
        xn = x_ref[...] + g_ref[...] * acc
        if head is None:
            xn_ref, y_ref = rest
            y_ref[...] = acc
            xn_ref[...] = xn
            return
        fg_ref, t_ref, y_ref, dx_ref, loss_ref, gg_ref = rest
        y_ref[...] = acc

        @pl.when(pl.program_id(0) == 0)
        def _():
            loss_ref[...] = jnp.zeros_like(loss_ref)
            gg_ref[...] = jnp.zeros_like(gg_ref)

        fg = fg_ref[...]
        rs = lax.rsqrt(jnp.mean(xn * xn, axis=1, keepdims=True) + EPS)
        xh = xn * rs
        e = xh * fg - t_ref[...]
        loss_ref[...] += jnp.broadcast_to(_colsum(_rowsum(e * e)) * (0.5 / d), loss_ref.shape)
        dy = e * (1.0 / d)
        gg_ref[...] += _bcast8(_colsum(dy * xh))
        dxh = dy * fg
        dx_ref[...] = rs * (dxh - xh * jnp.mean(dxh * xh, axis=1, keepdims=True))

    tok = pl.BlockSpec((tile, d), lambda i: (i, 0))
    in_specs = [tok, tok, tok, _full((1, d)), pl.BlockSpec((nd, 1, r, d), lambda i: (0, layer, 0, 0))]
    if head is None:
        return _call(body, comms, name="out_proj_fwd", grid=(s // tile,), in_specs=in_specs, out_specs=[tok, tok],
                     out_shape=[jax.ShapeDtypeStruct((s, d), F32)] * 2, args=(x, y_rg, y_ml, gate, w_out_g))
    return _call(
        body, comms, name="out_proj_loss", grid=(s // tile,),
        in_specs=in_specs + [_full((1, d)), tok],
        out_specs=[tok, tok, _full((SUBLANES, LANES)), _full((SUBLANES, d))],
        out_shape=[jax.ShapeDtypeStruct((s, d), F32)] * 2 + [jax.ShapeDtypeStruct((SUBLANES, LANES), F32),
                                                             jax.ShapeDtypeStruct((SUBLANES, d), F32)],
        args=(x, y_rg, y_ml, gate, w_out_g, *head))


def _ml_out_stage_bwd(dy, cell, o, z, ng):
    so = _sigmoid(o)
    hg = so * cell
    rinv = lax.rsqrt(jnp.mean(hg * hg, axis=1, keepdims=True) + EPS)
    hn = hg * rinv
    sz = _sigmoid(z)
    dz = dy * hn * ng * (sz + z * sz * (1.0 - sz))
    dymid = dy * z * sz
    dhn = dymid * ng
    dhg = rinv * (dhn - hn * jnp.mean(dhn * hn, axis=1, keepdims=True))
    return dz, dhg * cell * so * (1.0 - so), dhg * so, _colsum(dymid * hn)


def _out_bwd(dxo, gate, y, y_rg, y_ml, w_out_g, layer, tile, comms=None):
    s, d = dxo.shape
    nd, _, r, _ = w_out_g.shape

    def body(dx_ref, g_ref, y_ref, yr_ref, ym_ref, w_ref, dyr_ref, dym_ref, gw_ref, dg_ref):
        @pl.when(pl.program_id(0) == 0)
        def _():
            gw_ref[...] = jnp.zeros_like(gw_ref)
            dg_ref[...] = jnp.zeros_like(dg_ref)

        dxv = dx_ref[...]
        dg_ref[...] += _bcast8(_colsum(dxv * y_ref[...]))
        dyb = (dxv * g_ref[...]).astype(BF16)
        dycat = lax.dot_general(dyb, w_ref[...].reshape(nd * r, d), (((1,), (1,)), ((), ())), preferred_element_type=F32)
        dyr_ref[...] = dycat[:, 0:d]
        dym_ref[...] = dycat[:, d:2 * d]
        ycat = jnp.concatenate([yr_ref[...].astype(BF16), ym_ref[...].astype(BF16)], axis=1)
        gw_ref[...] += lax.dot_general(ycat, dyb, (((0,), (0,)), ((), ())), preferred_element_type=F32).reshape(nd, r, d)

    tok = pl.BlockSpec((tile, d), lambda i: (i, 0))
    return _call(
        body, comms, name="out_proj_bwd", grid=(s // tile,),
        in_specs=[tok, _full((1, d)), tok, tok, tok, pl.BlockSpec((nd, 1, r, d), lambda i: (0, layer, 0, 0))],
        out_specs=[tok, tok, _full((nd, r, d)), _full((SUBLANES, d))],
        out_shape=[jax.ShapeDtypeStruct((s, d), F32)] * 2 + [jax.ShapeDtypeStruct((nd, r, d), F32),
                                                             jax.ShapeDtypeStruct((SUBLANES, d), F32)],
        args=(dxo, gate, y, y_rg, y_ml, w_out_g))


def _ml_cell_bwd(dy_ml, u, cell, q, k, v, gcol, grow, mt, cs, ns, ms, ng, nh, comms=None):
    s, d = q.shape
    lc = ML_CHUNK
    nc = s // lc
    dh = d // nh

    per = CELL_BWD_CHUNKS_PER_STEP if nc % CELL_BWD_CHUNKS_PER_STEP == 0 else 1

    def body(*refs):
        gng_ref, dc_sc, dn_sc = refs[20], refs[21], refs[22]

        @pl.when(pl.program_id(0) == 0)
        def _():
            dc_sc[...] = jnp.zeros_like(dc_sc)
            dn_sc[...] = jnp.zeros_like(dn_sc)
            gng_ref[...] = jnp.zeros_like(gng_ref)

        for cc in reversed(range(per)):
            rows = slice(cc * lc, (cc + 1) * lc)
            views = [refs[at] if at == 13 else _PartOf(refs[at], cols=rows) if at == 8 else
                     _PartOf(refs[at], lead=cc) if at in (10, 11, 12) else _PartOf(refs[at], rows=rows) for at in range(20)]
            chunk(*views, gng_ref, dc_sc, dn_sc)

    def chunk(dy_ref, o_ref, z_ref, cell_ref, q_ref, k_ref, v_ref, gc_ref, gr_ref, mt_ref, cs_ref, ns_ref, ms_ref,
              ng_ref, dq_ref, dk_ref, dv_ref, dg_ref, do_ref, dz_ref, gng_ref, dc_sc, dn_sc):
        gc = gc_ref[...]
        gr = gr_ref[...]
        mtv = mt_ref[...]
        lane = _iota((lc, LANES), 1)
        rowv = _iota((lc, 1), 0)
        dg_acc = jnp.zeros((lc, LANES), F32)
        for h in range(nh):
            c_h = cs_ref[0, h]
            n_h = ns_ref[0, h, 0:1, :]
            m_prev = jnp.max(ms_ref[0, h, 0:1, :], axis=1, keepdims=True)
            t = _cell_chunk(h, nh, q_ref, k_ref, v_ref, gc, gr, m_prev, c_h, n_h, m_t=_col(mtv, h))
            sl, qh, kh, vh = t["sl"], t["qh"], t["kh"], t["vh"]
            w_intra, w_inter, smat, w_state, decay = t["w_intra"], t["w_inter"], t["smat"], t["w_state"], t["decay"]
            cell_h = cell_ref[:, sl]
            dz, do, dcell, gng = _ml_out_stage_bwd(dy_ref[:, sl], cell_h, o_ref[:, sl], z_ref[:, sl], ng_ref[:, sl])
            dz_ref[:, sl] = dz.astype(BF16)
            do_ref[:, sl] = do.astype(BF16)
            gng_ref[:, sl] += _bcast8(gng)
            eneg = jnp.exp(-t["m_t"])
            aden = jnp.abs(t["den"])
            nst = jnp.maximum(aden, eneg)
            dnum = dcell / nst
            dden = jnp.where(aden > eneg, -_rowsum(cell_h * dcell) / nst * jnp.sign(t["den"]), 0.0)
            pmat = _mm_nt(dnum, vh) + dden
            damat = pmat * w_intra
            gmat = pmat * smat
            wdn = w_inter * dnum
            wdd = w_inter * dden
            dqh = _mm(damat, kh) + _mm_nt(wdn, c_h) + wdd * n_h
            dkh = _mm_tn(damat, qh)
            dvh = _mm_tn(smat, dnum)
            dw_inter = _rowsum(dnum * t["qc"]) + dden * t["qn"]
            dcn = dc_sc[h]
            dnn = dn_sc[h, 0:1, :]
            kw = kh * w_state
            dkw = _mm_nt(vh, dcn) + dnn
            dvh = dvh + _mm(kw, dcn)
            dkh = dkh + dkw * w_state
            dgst = _rowsum(dkw * kh) * w_state
            ddecay = _colsum(_rowsum(dcn * c_h)) + _rowsum(dnn * n_h)
            db_last = _colsum(dgst) + ddecay * decay
            rs_g = _rowsum(gmat)
            cs_g = _rowsum(gmat.T)
            db = rs_g - cs_g + dw_inter * w_inter - dgst + jnp.where(rowv == lc - 1, db_last, 0.0)
            dli = cs_g + dgst
            dc_sc[h] = decay * dcn + _mm_tn(qh, wdn)
            dn_sc[h] = _bcast8(decay * dnn + _colsum(qh * wdd))
            dq_ref[:, sl] = dqh
            dk_ref[:, sl] = dkh * (dh ** -0.5)
            dv_ref[:, sl] = dvh
            dg_acc = jnp.where(lane == h, dli, jnp.where(lane == 4 + h, db, dg_acc))
        dg_ref[...] = dg_acc

    rev = lambda c: nc // per - 1 - c
    tok = pl.BlockSpec((per * lc, d), lambda c: (rev(c), 0))
    g128 = pl.BlockSpec((per * lc, LANES), lambda c: (rev(c), 0))
    return _call(
        body, comms, name="mlstm_cell_bwd", grid=(nc // per,),
        in_specs=[tok, pl.BlockSpec((per * lc, d), lambda c: (rev(c), 3)), pl.BlockSpec((per * lc, d), lambda c: (rev(c), 4)),
                  tok, tok, tok, tok, g128, pl.BlockSpec((16, per * lc), lambda c: (0, rev(c))), g128,
                  pl.BlockSpec((per, nh, dh, dh), lambda c: (rev(c), 0, 0, 0)),
                  pl.BlockSpec((per, nh, SUBLANES, dh), lambda c: (rev(c), 0, 0, 0)),
                  pl.BlockSpec((per, nh, SUBLANES, LANES), lambda c: (rev(c), 0, 0, 0)), _full((1, d))],
        out_specs=[tok, tok, tok, g128, tok, tok, _full((SUBLANES, d))],
        out_shape=[jax.ShapeDtypeStruct((s, d), F32)] * 3 + [jax.ShapeDtypeStruct((s, LANES), F32)]
        + [jax.ShapeDtypeStruct((s, d), BF16)] * 2 + [jax.ShapeDtypeStruct((SUBLANES, d), F32)],
        scratch_shapes=[pltpu.VMEM((nh, dh, dh), F32), pltpu.VMEM((nh, SUBLANES, dh), F32)],
        args=(dy_ml, u, u, cell, q, k, v, gcol, grow, mt, cs, ns, ms, ng))


def _halo_spec(d, tile, nt, col):
    per = tile // SUBLANES
    return pl.BlockSpec((SUBLANES, d), lambda i: (jnp.maximum((nt - 1 - i) * per - 1, 0), col))


def _ml_pre_bwd(dq, dk, dv, dgates, gcol, u, pre, q, k, v, conv_w, w_q, w_k, w_v, wif_t, tile, comms=None):
    s, d = dq.shape
    nt = s // tile
    nh, dh, _ = w_q.shape

    def body(dq_ref, dk_ref, dv_ref, dg_ref, gc_ref, x_ref, pre_ref, q_ref, k_ref, v_ref, cw_ref,
             wq_ref, wk_ref, wv_ref, wift_ref,
             dx_ref, gwq_ref, gwk_ref, gwv_ref, gwif_ref, gbif_ref, gcw_ref, gcb_ref, next8):
        @pl.when(pl.program_id(0) == 0)
        def _():
            next8[...] = jnp.zeros_like(next8)
            for ref in (gwq_ref, gwk_ref, gwv_ref, gwif_ref, gbif_ref, gcw_ref, gcb_ref):
                ref[...] = jnp.zeros_like(ref)

        x = x_ref[...]
        pre = pre_ref[...]
        sg = _sigmoid(pre)
        xc = pre * sg
        dgc = dg_ref[...]
        lane = _iota(dgc.shape, 1)
        utri = jnp.where(_iota((ML_CHUNK, ML_CHUNK), 0) <= _iota((ML_CHUNK, ML_CHUNK), 1), 1.0, 0.0)
        rcs = [_mm_hi(utri, dgc[c * ML_CHUNK:(c + 1) * ML_CHUNK, :]) for c in range(tile // ML_CHUNK)]
        rc = rcs[0] if len(rcs) == 1 else jnp.concatenate(rcs, axis=0)
        dgates_v = jnp.where(lane < 4, dgc, jnp.where(lane < 8, rc * (1.0 - jnp.exp(gc_ref[...])), 0.0))
        dgb = dgates_v.astype(BF16)
        gbif_ref[...] += jnp.broadcast_to(_colsum(dgates_v), gbif_ref.shape)
        ext = jnp.dot(dgb, wift_ref[...], preferred_element_type=F32)
        dqt = dq_ref[...] + ext[:, 0:d]
        dkt = dk_ref[...] + ext[:, d:2 * d]
        dvt = dv_ref[...] + ext[:, 2 * d:3 * d]
        gwif_ref[:, 0:d] += _mm_tn(dgb, q_ref[...])
        gwif_ref[:, d:2 * d] += _mm_tn(dgb, k_ref[...])
        gwif_ref[:, 2 * d:3 * d] += _mm_tn(dgb, v_ref[...])
        dxc_parts, dxv_parts = [], []
        for h in range(nh):
            sl = slice(h * dh, (h + 1) * dh)
            gwq_ref[h] += _mm_tn(xc[:, sl], dqt[:, sl])
            gwk_ref[h] += _mm_tn(xc[:, sl], dkt[:, sl])
            gwv_ref[h] += _mm_tn(x[:, sl], dvt[:, sl])
            dxc_parts.append(_mm_nt(dqt[:, sl], wq_ref[h]) + _mm_nt(dkt[:, sl], wk_ref[h]))
            dxv_parts.append(_mm_nt(dvt[:, sl], wv_ref[h]))
        dxc = jnp.concatenate(dxc_parts, axis=1)
        dxv = jnp.concatenate(dxv_parts, axis=1)
        dpre = dxc * (sg + pre * sg * (1.0 - sg))
        gcb_ref[...] += _bcast8(_colsum(dpre))
        dx_ref[...] = (dxv + _conv_bwd(dpre, x, next8[...], cw_ref, gcw_ref)).astype(BF16)
        next8[...] = dpre[0:SUBLANES, :]

    rev = lambda i: nt - 1 - i
    tok = pl.BlockSpec((tile, d), lambda i: (rev(i), 0))
    g128 = pl.BlockSpec((tile, LANES), lambda i: (rev(i), 0))
    wsh = (nh, dh, dh)
    return _call(
        body, comms, name="mlstm_proj_bwd", grid=(nt,),
        in_specs=[tok, tok, tok, g128, g128, pl.BlockSpec((tile, d), lambda i: (rev(i), 2)), tok,
                  tok, tok, tok, _full(conv_w.shape), _full(wsh), _full(wsh), _full(wsh), _full(wif_t.shape)],
        out_specs=[tok, _full(wsh), _full(wsh), _full(wsh), _full((LANES, 3 * d)), _full((SUBLANES, LANES)),
                   _full((SUBLANES, d)), _full((SUBLANES, d))],
        out_shape=[jax.ShapeDtypeStruct((s, d), BF16)] + [jax.ShapeDtypeStruct(wsh, F32)] * 3
        + [jax.ShapeDtypeStruct((LANES, 3 * d), F32), jax.ShapeDtypeStruct((SUBLANES, LANES), F32),
           jax.ShapeDtypeStruct((SUBLANES, d), F32), jax.ShapeDtypeStruct((SUBLANES, d), F32)],
        scratch_shapes=[pltpu.VMEM((SUBLANES, d), F32)],
        args=(dq, dk, dv, dgates, gcol, u, pre, q, k, v, conv_w, w_q, w_k, w_v, wif_t))


def _rg_bwd(dy_rg, u, h_rg, gates, conv_w, w_a, w_x, lam, tile, comms=None):
    s, d = dy_rg.shape
    nt = s // tile
    nh, dh, _ = w_a.shape

    def body(dy_ref, x_ref, z_ref, h_ref, hhalo_ref, xc_ref, r_ref, i_ref, a_ref, beta_ref, cw_ref, wa_ref,
             wx_ref, lam_ref,
             dx_ref, dz_ref, gwa_ref, gwx_ref, gba_ref, gbx_ref, glam_ref, gcw_ref, gcb_ref, next8, anext, dnext, dbuf):
        i = pl.program_id(0)

        @pl.when(i == 0)
        def _():
            for ref in (next8, anext, dnext, gwa_ref, gwx_ref, gba_ref, gbx_ref, glam_ref, gcw_ref, gcb_ref):
                ref[...] = jnp.zeros_like(ref)

        inner = jnp.where(i < nt - 1, 1.0, 0.0)
        xc, r, ig, a, beta = xc_ref[...], r_ref[...], i_ref[...], a_ref[...], beta_ref[...]
        sp = _softplus(-lam_ref[...])
        h = h_ref[...]
        row = _iota(h.shape, 0)
        hprev = jnp.where(row >= 1, pltpu.roll(h, 1, 0), hhalo_ref[SUBLANES - 1:SUBLANES, :] * inner)
        z = z_ref[...]
        sz = _sigmoid(z)
        dyv = dy_ref[...]
        dz_ref[...] = (dyv * h * (sz + z * sz * (1.0 - sz))).astype(BF16)
        a_up = jnp.where(row < tile - 1, pltpu.roll(a, tile - 1, 0), anext[0:1, :])
        _scan_into(a_up, dyv * z * sz, dnext[0:1, :], dbuf, True)
        delta = dbuf[...]
        anext[...] = a[0:SUBLANES, :]
        dnext[...] = delta[0:SUBLANES, :]
        dla = delta * hprev * a - delta * ig * xc * (a * a / beta)
        glam_ref[...] += _bcast8(_colsum(dla * r) * (RG_C * _sigmoid(-lam_ref[...])))
        dpa = dla * (-RG_C * sp) * r * (1.0 - r)
        dpx = delta * beta * xc * ig * (1.0 - ig)
        gba_ref[...] += _bcast8(_colsum(dpa))
        gbx_ref[...] += _bcast8(_colsum(dpx))
        parts = []
        for hh in range(nh):
            sl = slice(hh * dh, (hh + 1) * dh)
            gwa_ref[hh] += _mm_tn(xc[:, sl], dpa[:, sl])
            gwx_ref[hh] += _mm_tn(xc[:, sl], dpx[:, sl])
            parts.append(_mm_nt(dpa[:, sl], wa_ref[hh]) + _mm_nt(dpx[:, sl], wx_ref[hh]))
        dxc = delta * beta * ig + jnp.concatenate(parts, axis=1)
        gcb_ref[...] += _bcast8(_colsum(dxc))
        dx_ref[...] = _conv_bwd(dxc, x_ref[...], next8[...], cw_ref, gcw_ref).astype(BF16)
        next8[...] = dxc[0:SUBLANES, :]

    rev = lambda i: nt - 1 - i
    tok = pl.BlockSpec((tile, d), lambda i: (rev(i), 0))
    vec = _full((1, d))
    acc = _full((SUBLANES, d))
    wsh = (nh, dh, dh)
    return _call(
        body, comms, name="rglru_bwd", grid=(nt,),
        in_specs=[tok, tok, pl.BlockSpec((tile, d), lambda i: (rev(i), 1)), tok,
                  _halo_spec(d, tile, nt, 0)] + [tok] * 5 + [_full(conv_w.shape), _full(wsh), _full(wsh), vec],
        out_specs=[tok, tok, _full(wsh), _full(wsh), acc, acc, acc, acc, acc],
        out_shape=[jax.ShapeDtypeStruct((s, d), BF16)] * 2 + [jax.ShapeDtypeStruct(wsh, F32)] * 2
        + [jax.ShapeDtypeStruct((SUBLANES, d), F32)] * 5,
        scratch_shapes=[pltpu.VMEM((SUBLANES, d), F32)] * 3 + [pltpu.VMEM((tile, d), F32)],
        args=(dy_rg, u, u, h_rg, h_rg, *gates, conv_w, w_a, w_x, lam))


def _segments(d, w, n_pieces, n_slots):
    bounds = sorted({k * d for k in range(n_pieces + 1)} | {j * w for j in range(n_slots + 1)})
    return [(lo // d, lo % d, lo // w, lo % w, hi - lo) for lo, hi in zip(bounds[:-1], bounds[1:])]


def _in_bwd(pieces, x, dxo, ng, scale, w_in_g, layer, tile, comms=None, tiles=None, prev=None):
    s, d = x.shape
    nd, _, _, w = w_in_g.shape
    first, count = tiles or (0, s // tile)
    n_p = len(pieces)

    def body(*refs):
        p_refs = refs[:n_p]
        x_ref, dxo_ref, ng_ref, sc_ref, w_ref = refs[n_p:n_p + 5]
        dx_ref, dsc_ref, dsh_ref, gng_ref, wcat = refs[-5:]
        _join_columns(w_ref, wcat)

        @pl.when(pl.program_id(0) == 0)
        def _():
            for k, ref in enumerate((dsc_ref, dsh_ref, gng_ref)):
                ref[...] = jnp.zeros_like(ref) if prev is None else refs[n_p + 6 + k][...]

        du = jnp.concatenate([p[...] for p in p_refs], axis=1)
        dh = lax.dot_general(du, wcat[...], (((1,), (1,)), ((), ())), preferred_element_type=F32)
        xv = x_ref[...]
        g = ng_ref[...]
        rs = lax.rsqrt(jnp.mean(xv * xv, axis=1, keepdims=True) + EPS)
        xh = xv * rs
        dsh_ref[...] += _bcast8(_colsum(dh))
        dsc_ref[...] += _bcast8(_colsum(dh * xh * g))
        dhn = dh * (1.0 + sc_ref[...])
        gng_ref[...] += _bcast8(_colsum(dhn * xh))
        dxh = dhn * g
        dx_ref[...] = dxo_ref[...] + rs * (dxh - xh * jnp.mean(dxh * xh, axis=1, keepdims=True))

    tok = pl.BlockSpec((tile, d), lambda i: (i + first, 0))
    vec = _full((1, d))
    acc = _full((SUBLANES, d))
    more_specs = [] if prev is None else [pl.BlockSpec(memory_space=pl.ANY), acc, acc, acc]
    return _call(
        body, comms, name="in_proj_bwd_x", grid=(count,),
        in_specs=[tok] * n_p + [tok, tok, vec, vec, pl.BlockSpec((nd, 1, d, w), lambda i: (0, layer, 0, 0),
                                                               pipeline_mode=pl.Buffered(1))] + more_specs,
        out_specs=[tok, acc, acc, acc],
        out_shape=[jax.ShapeDtypeStruct((s, d), F32)] + [jax.ShapeDtypeStruct((SUBLANES, d), F32)] * 3,
        scratch_shapes=[pltpu.VMEM((d, nd * w), BF16)],
        args=(*pieces, x, dxo, ng, scale, w_in_g) + (() if prev is None else tuple(prev)),
        aliases={} if prev is None else {n_p + 5: 0})


def _in_bwd_w(pieces, hbf, w, slots, tile, comms=None):
    s, d = hbf.shape
    nd_all = len(pieces) * d // w
    segs = [sg for sg in _segments(d, w, len(pieces), nd_all) if sg[2] in slots]

    def body(*refs):
        p_refs = refs[:len(pieces)]
        h_ref, gw_ref = refs[len(pieces):]

        @pl.when(pl.program_id(0) == 0)
        def _():
            gw_ref[...] = jnp.zeros_like(gw_ref)

        hv = h_ref[...]
        for (kk, a, j, b, width) in segs:
            gw_ref[j - slots[0], :, b:b + width] += _mm_tn(hv, p_refs[kk][:, a:a + width])

    tok = pl.BlockSpec((tile, d), lambda i: (i, 0))
    return _call(
        body, comms, name="in_proj_bwd_w", grid=(s // tile,),
        in_specs=[tok] * len(pieces) + [tok],
        out_specs=[pl.BlockSpec((len(slots), d, w), lambda i: (0, 0, 0), pipeline_mode=pl.Buffered(1))],
        out_shape=[jax.ShapeDtypeStruct((len(slots), d, w), F32)],
        args=(*pieces, hbf))[0]


def _exchange(arrs, gather, name):
    return _run_comms([_exchange_comm(arrs, gather)], name)[0]


def _run_comms(comms, name):
    _call(lambda: None, comms, name=name, grid=(1,), in_specs=[], out_specs=[], out_shape=[], args=())
    return [cm.results for cm in comms]


def _exchange_comm(arrs, gather):
    n = len(arrs)
    per = N_DEV - 1

    def copies(ins, outs, sems):
        send_sems, recv_sems, local_sems = sems
        x, y, c = (lax.axis_index(ax) for ax in MESH_AXES)
        me = 4 * x + 2 * y + c
        sends, recvs = [], []
        for flip in range(1, N_DEV):
            px = x ^ ((flip >> 2) & 1)
            py = y ^ ((flip >> 1) & 1)
            pc = c ^ (flip & 1)
            peer = 4 * px + 2 * py + pc
            for kk in range(n):
                src = ins[kk] if gather else ins[kk].at[peer]
                sends.append(_remote(src, outs[kk].at[me], send_sems, recv_sems, kk * per + flip - 1, (px, py, pc)))
                recvs.append(_remote(src, outs[kk].at[peer], send_sems, recv_sems, kk * per + flip - 1, (px, py, pc)))
        local = [pltpu.make_async_copy(ins[kk] if gather else ins[kk].at[me], outs[kk].at[me], local_sems.at[kk])
                 for kk in range(n)]
        return local, sends, recvs

    def start(ins, outs, sems):
        local, sends, _ = copies(ins, outs, sems)
        for cp in sends + local:
            cp.start()

    def finish(ins, outs, sems):
        local, sends, recvs = copies(ins, outs, sems)
        for cp in recvs:
            cp.wait_recv()
        for cp in sends:
            cp.wait_send()
        for cp in local:
            cp.wait()

    return _Comm(arrs, [jax.ShapeDtypeStruct((N_DEV,) + a.shape if gather else a.shape, a.dtype) for a in arrs],
                 [pltpu.SemaphoreType.DMA((n * per,)), pltpu.SemaphoreType.DMA((n * per,)), pltpu.SemaphoreType.DMA((n,))],
                 start, finish)


def _mesh_place():
    x, y, c = (lax.axis_index(ax) for ax in MESH_AXES)
    return x, y, c, (x, y, 1 - c), [(1 - x, y), (x, 1 - y), (1 - x, 1 - y)]


def _remote(src, dst, send_sems, recv_sems, sem, to):
    return pltpu.make_async_remote_copy(src_ref=src, dst_ref=dst, send_sem=send_sems.at[sem], recv_sem=recv_sems.at[sem],
                                        device_id=to, device_id_type=pl.DeviceIdType.MESH)


N_CHIPS = N_DEV // 2


def _pair_sum(a, other, parity, name):
    _, r, c = a.shape
    tr = _row_tile(r, c, 3)

    def body(p_ref, a_ref, o_ref, s_ref):
        s_ref[...] = (a_ref[...] + o_ref[...]).astype(BF16)

    return pl.pallas_call(
        body, name=name,
        grid_spec=pltpu.PrefetchScalarGridSpec(
            num_scalar_prefetch=1, grid=(N_CHIPS, r // tr),
            in_specs=[pl.BlockSpec((1, tr, c), lambda q, i, p: (2 * q + p[0], i, 0)),
                      pl.BlockSpec((1, tr, c), lambda q, i, p: (q, i, 0))],
            out_specs=pl.BlockSpec((1, tr, c), lambda q, i, p: (q, i, 0))),
        out_shape=jax.ShapeDtypeStruct((N_CHIPS, r, c), BF16),
        compiler_params=_params(2),
    )(parity, a, other)


def _adam_math(w, g, m, v):
    m = ADAM_B1 * m + (1.0 - ADAM_B1) * g
    v = ADAM_B2 * v + (1.0 - ADAM_B2) * (g * g)
    m_hat = m / (1.0 - ADAM_B1 ** ADAM_STEP)
    v_hat = v / (1.0 - ADAM_B2 ** ADAM_STEP)
    delta = -ADAM_LR * (m_hat / (jnp.sqrt(v_hat) + ADAM_EPS) + ADAM_WD * w)
    return delta, m, v


def _sum_devices(r_ref):
    acc = r_ref[0].astype(F32)
    for p in range(1, r_ref.shape[0]):
        acc = acc + r_ref[p].astype(F32)
    return acc


def _row_tile(rows, cols, n_bufs):
    budget = 24 * 1024 * 1024 // (n_bufs * 2 * cols * 4)
    t = rows
    while t > budget and t % 2 == 0 and (t // 2) % SUBLANES == 0:
        t //= 2
    return t


def _reduce_adam(recvs, w, m, v, name, comms=None):
    nl, r, c = w.shape
    n_part = recvs[0].shape[0]
    tr = _row_tile(r, c, n_part * nl + 7)
    nt = r // tr

    def body(*refs):
        r_refs = refs[:nl]
        w_ref, m_ref, v_ref, g_ref, d_ref, mo_ref, vo_ref = refs[nl:]
        layer = pl.program_id(0) // nt
        g = _sum_devices(r_refs[0])
        for ll in range(1, nl):
            g = jnp.where(layer == ll, _sum_devices(r_refs[ll]), g)
        delta, m2, v2 = _adam_math(w_ref[0], g, m_ref[0], v_ref[0])
        g_ref[0] = g
        d_ref[0] = delta
        mo_ref[0] = m2
        vo_ref[0] = v2

    def rspec(ll):
        return pl.BlockSpec((n_part, tr, c),
                            lambda i: (0, jnp.where(i // nt == ll, i % nt, jnp.where(i // nt < ll, 0, nt - 1)), 0))

    blk = pl.BlockSpec((1, tr, c), lambda i: (i // nt, i % nt, 0))
    return _call(
        body, comms, name=name, grid=(nl * nt,),
        in_specs=[rspec(ll) for ll in range(nl)] + [blk, blk, blk],
        out_specs=[blk] * 4,
        out_shape=[jax.ShapeDtypeStruct((nl, r, c), F32)] * 4,
        args=(*recvs, w, m, v))


def _tile_for(s, want):
    return min(want, s)


REPLICATED = ("norm_g", "b_ada", "rg_conv_b", "rg_w_a", "rg_b_a", "rg_w_x", "rg_b_x", "rg_lambda", "ml_conv_b",
              "ml_b_if", "ml_norm_g", "final_g")


def _small_pack(rg_conv_w, ml_conv_w, ml_w_if):
    nl = rg_conv_w.shape[0]
    wif_t = jnp.swapaxes(ml_w_if, 1, 2).reshape(nl, -1, LANES)
    return jnp.concatenate([rg_conv_w, ml_conv_w, wif_t], axis=1)


def _small_unpack(p, if_rows):
    nl = p.shape[0]
    rg_cw = p[:, 0:CONV_WIDTH]
    ml_cw = p[:, CONV_WIDTH:2 * CONV_WIDTH]
    wif = jnp.swapaxes(p[:, 2 * CONV_WIDTH:].reshape(nl, 8, if_rows), 1, 2)
    return rg_cw, ml_cw, wif


def _qkv_slots(g_qkv, nd):
    three, nh, dh, _ = g_qkv.shape
    return g_qkv.reshape(three, nh, nd, dh // nd, dh).transpose(2, 0, 1, 3, 4).reshape(nd, three * nh * (dh // nd), dh)


def _small_slots(g):
    nd = N_DEV
    cw = jnp.stack([g["rg_conv_w"], g["ml_conv_w"]]).reshape(2, CONV_WIDTH, nd, LANES).transpose(2, 0, 1, 3)
    cw = cw.reshape(nd, 2 * CONV_WIDTH, LANES)
    wif = g["wif_t"].reshape(8, nd, -1).transpose(1, 0, 2).reshape(nd, -1, LANES)
    return jnp.concatenate([cw, wif], axis=1)


def _slot(block):
    return 4 * block[0] + 2 * block[1] + block[2]


def _dma_sems(*counts):
    return [pltpu.SemaphoreType.DMA((n,)) for n in counts]


def _start_all(copies):
    for cp in copies:
        cp.start()


def _gather_ici_comm(arrs):
    n = len(arrs)

    def copies(ins, outs, sems):
        send_sems, recv_sems, local_sems = sems
        x, y, c, sibling, chips = _mesh_place()
        me = (x, y, c)
        peers = [(*chip, c) for chip in chips] + [sibling]
        local = [pltpu.make_async_copy(ins[kk], outs[kk].at[_slot(me)], local_sems.at[kk]) for kk in range(n)]
        sends = [_remote(ins[kk], outs[kk].at[_slot(me)], send_sems, recv_sems, kk * 4 + j, peer)
                 for j, peer in enumerate(peers) for kk in range(n)]
        recvs = [_remote(ins[kk], outs[kk].at[_slot(peer)], send_sems, recv_sems, kk * 4 + j, peer)
                 for j, peer in enumerate(peers) for kk in range(n)]
        return local, sends, recvs

    def start(ins, outs, sems):
        local, sends, _ = copies(ins, outs, sems)
        _start_all(sends + local)

    def finish(ins, outs, sems):
        local, sends, recvs = copies(ins, outs, sems)
        for cp in recvs:
            cp.wait_recv()
        for cp in sends:
            cp.wait_send()
        for cp in local:
            cp.wait()

    return _Comm(arrs, [jax.ShapeDtypeStruct((N_DEV,) + a.shape, a.dtype) for a in arrs], _dma_sems(4 * n, 4 * n, n),
                 start, finish)


def _gather_fwd_comm(bufs):
    n = len(bufs)

    def copies(ins, outs, sems):
        send_sems, recv_sems = sems
        _, _, c, sibling, chips = _mesh_place()
        sends = [_remote(ins[kk].at[_slot((*chip, c))], outs[kk].at[_slot((*chip, c))], send_sems, recv_sems, kk * 3 + j, sibling)
                 for j, chip in enumerate(chips) for kk in range(n)]
        recvs = [_remote(ins[kk].at[_slot((*chip, c))], outs[kk].at[_slot((*chip, 1 - c))], send_sems, recv_sems, kk * 3 + j, sibling)
                 for j, chip in enumerate(chips) for kk in range(n)]
        return sends, recvs

    def start(ins, outs, sems):
        _start_all(copies(ins, outs, sems)[0])

    def finish(ins, outs, sems):
        sends, recvs = copies(ins, outs, sems)
        for cp in recvs:
            cp.wait_recv()
        for cp in sends:
            cp.wait_send()

    return _Comm(bufs, [jax.ShapeDtypeStruct(a.shape, a.dtype) for a in bufs], _dma_sems(3 * n, 3 * n), start, finish,
                 aliases=[(i, i) for i in range(n)])


def _core_swap_comm(arrs):
    n = len(arrs)

    def copies(ins, outs, sems):
        send_sems, recv_sems = sems
        _, _, c, sibling, _ = _mesh_place()
        return [_remote(ins[kk].at[2 * q + (1 - c)], outs[kk].at[q], send_sems, recv_sems, kk * N_CHIPS + q, sibling)
                for q in range(N_CHIPS) for kk in range(n)]

    def start(ins, outs, sems):
        _start_all(copies(ins, outs, sems))

    def finish(ins, outs, sems):
        cps = copies(ins, outs, sems)
        for cp in cps:
            cp.wait_recv()
        for cp in cps:
            cp.wait_send()

    return _Comm(arrs, [jax.ShapeDtypeStruct((N_CHIPS,) + a.shape[1:], a.dtype) for a in arrs],
                 _dma_sems(N_CHIPS * n, N_CHIPS * n), start, finish)


def _chip_swap_comm(arrs):
    n = len(arrs)
    per = N_CHIPS - 1

    def copies(ins, outs, sems):
        send_sems, recv_sems, local_sems = sems
        x, y, c, _, chips = _mesh_place()
        mine = 2 * x + y
        sends = [_remote(ins[kk].at[2 * chip[0] + chip[1]], outs[kk].at[mine], send_sems, recv_sems, kk * per + j, (*chip, c))
                 for j, chip in enumerate(chips) for kk in range(n)]
        recvs = [_remote(ins[kk].at[mine], outs[kk].at[2 * chip[0] + chip[1]], send_sems, recv_sems, kk * per + j, (*chip, c))
                 for j, chip in enumerate(chips) for kk in range(n)]
        local = [pltpu.make_async_copy(ins[kk].at[mine], outs[kk].at[mine], local_sems.at[kk]) for kk in range(n)]
        return local, sends, recvs

    def start(ins, outs, sems):
        local, sends, _ = copies(ins, outs, sems)
        _start_all(sends + local)

    def finish(ins, outs, sems):
        local, sends, recvs = copies(ins, outs, sems)
        for cp in recvs:
            cp.wait_recv()
        for cp in sends:
            cp.wait_send()
        for cp in local:
            cp.wait()

    return _Comm(arrs, [jax.ShapeDtypeStruct(a.shape, a.dtype) for a in arrs], _dma_sems(per * n, per * n, n), start, finish)


def _ada_mod(c_all, w_ada, b_cols, comms=None):
    nl, d, w = w_ada.shape

    def body(c_ref, w_ref, b_ref, m_ref, ca_ref):
        sub = _iota((SUBLANES, d), 0)
        cv = jnp.zeros((SUBLANES, d), F32)
        for b in range(N_DEV):
            cv = jnp.where(sub == b, c_ref[b], cv)
        ca = cv * _sigmoid(cv)
        ca_ref[...] = ca
        m_ref[...] = jnp.zeros_like(m_ref)
        for l in range(nl):
            ml = _mm_hi(ca, w_ref[l]) + b_ref[l:l + 1, :]
            for b in range(N_DEV):
                m_ref[b, l:l + 1, :] = _row(ml, b)

    return _call(
        body, comms, name="adaln_mod_columns", grid=(1,),
        in_specs=[_full(c_all.shape), _full(w_ada.shape), _full(b_cols.shape)],
        out_specs=[_full((N_DEV, SUBLANES, w)), _full((SUBLANES, d))],
        out_shape=[jax.ShapeDtypeStruct((N_DEV, SUBLANES, w), F32), jax.ShapeDtypeStruct((SUBLANES, d), F32)],
        args=(c_all, w_ada, b_cols))


def _ada_grad_adam(cact_t, dmods, w, m, v, comms=None):
    nl, d, wd = w.shape
    tr = _row_tile(d, wd, 8)
    nt = d // tr

    def body(c_ref, dm_ref, w_ref, m_ref, v_ref, g_ref, d_ref, mo_ref, vo_ref):
        cv = c_ref[...]
        dm = dm_ref[0]
        g = _col(cv, 0) * _row(dm, 0)
        for b in range(1, N_DEV):
            g = g + _col(cv, b) * _row(dm, b)
        delta, m2, v2 = _adam_math(w_ref[0], g, m_ref[0], v_ref[0])
        g_ref[0] = g
        d_ref[0] = delta
        mo_ref[0] = m2
        vo_ref[0] = v2

    blk = pl.BlockSpec((1, tr, wd), lambda i: (i // nt, i % nt, 0))
    return _call(
        body, comms, name="adaln_grad_adam", grid=(nl * nt,),
        in_specs=[pl.BlockSpec((tr, N_DEV), lambda i: (i % nt, 0)), pl.BlockSpec((1, N_DEV, wd), lambda i: (i // nt, 0, 0)),
                  blk, blk, blk],
        out_specs=[blk] * 4, out_shape=[jax.ShapeDtypeStruct((nl, d, wd), F32)] * 4,
        args=(cact_t, dmods, w, m, v))


REP_ROWS = ("norm_g", "dshift", "dscale", "dgate", "rg_conv_b", "rg_b_a", "rg_b_x", "rg_lambda", "ml_conv_b", "ml_norm_g",
            "ml_b_if")


def _sum_parts(recvs, name):
    def body(*refs):
        for r_ref, o_ref in zip(refs[:len(recvs)], refs[len(recvs):]):
            o_ref[...] = _sum_devices(r_ref).astype(o_ref.dtype)

    return pl.pallas_call(
        body, name=name, grid=(1,),
        in_specs=[_full(r.shape) for r in recvs], out_specs=[_full(r.shape[1:]) for r in recvs],
        out_shape=[jax.ShapeDtypeStruct(r.shape[1:], r.dtype) for r in recvs], compiler_params=_params(1),
    )(*recvs)


def _adam_replicated(vp, mp, params, nl):
    d = vp.shape[2]
    nr = len(REP_ROWS)
    names = list(params)
    mat_shape = params["rg_w_a"][0].shape[1:]
    mat_rows = mp.shape[0] // (2 * nl)

    def pieces(name):
        if name == "final_g":
            return [(lambda vp_ref, mp_ref: vp_ref[nl * nr:nl * nr + 1, :], (slice(0, 1), slice(None)))]
        out = []
        for l in range(nl):
            if name in ("rg_w_a", "rg_w_x"):
                at = (2 * l + (name == "rg_w_x")) * mat_rows
                out.append((lambda vp_ref, mp_ref, at=at: mp_ref[at:at + mat_rows, :].astype(F32).reshape(mat_shape), l))
            elif name == "b_ada":
                for j in range(3):
                    r = l * nr + 1 + j
                    out.append((lambda vp_ref, mp_ref, r=r: vp_ref[r:r + 1, :], (slice(l, l + 1), slice(j * d, (j + 1) * d))))
            else:
                r = l * nr + REP_ROWS.index(name)
                cols = slice(0, LANES) if name == "ml_b_if" else slice(None)
                out.append((lambda vp_ref, mp_ref, r=r, cols=cols: vp_ref[r:r + 1, cols], (slice(l, l + 1), slice(None))))
        return out

    def body(*refs):
        parts_ref, mp_ref, vp_ref = refs[0], refs[1], refs[-1]
        ins, outs = refs[2:2 + 3 * len(names)], refs[2 + 3 * len(names):-1]
        vp_ref[...] = _sum_devices(parts_ref)
        for pi, name in enumerate(names):
            w_ref, m_ref, v_ref = ins[3 * pi:3 * pi + 3]
            g_ref, d_ref, mo_ref, vo_ref = outs[4 * pi:4 * pi + 4]
            for get, idx in pieces(name):
                g = get(vp_ref, mp_ref)
                delta, m2, v2 = _adam_math(w_ref[idx], g, m_ref[idx], v_ref[idx])
                g_ref[idx] = g
                d_ref[idx] = delta
                mo_ref[idx] = m2
                vo_ref[idx] = v2

    flat = [a for name in names for a in params[name]]
    out_shape = [jax.ShapeDtypeStruct(params[name][0].shape, F32) for name in names for _ in range(4)]
    out_shape.append(jax.ShapeDtypeStruct(vp.shape[1:], F32))
    res = pl.pallas_call(
        body, name="adam_replicated", grid=(1,),
        in_specs=[_full(vp.shape), _full(mp.shape)] + [_full(a.shape) for a in flat],
        out_specs=[_full(o.shape) for o in out_shape], out_shape=out_shape, compiler_params=_params(1),
    )(vp, mp, *flat)
    return {name: res[4 * pi:4 * pi + 4] for pi, name in enumerate(names)}, res[-1]


class _Plan:
    def __init__(self):
        self.hosted, self.after = {}, {}

    def host(self, key, comm, then=None):
        self.hosted.setdefault(key, []).append(comm)
        if then is not None:
            self.after.setdefault(key, []).append(then)

    def comms(self, key):
        return self.hosted.pop(key, None)

    def done(self, key):
        for fn in self.after.pop(key, []):
            fn()

    def flush(self):
        while self.hosted:
            key = next(iter(self.hosted))
            _call(lambda: None, self.comms(key), name="exchange_after_%s_%d" % key, grid=(1,), in_specs=[], out_specs=[],
                  out_shape=[], args=())
            self.done(key)


VEC_TABLE = ("norm_g", "rg_conv_b", "rg_b_a", "rg_b_x", "rg_lambda", "ml_conv_b", "ml_norm_g")


def _vec_table(rep):
    rows = [rep[n] for n in VEC_TABLE]
    return jnp.stack(rows + [jnp.zeros_like(rows[0])] * (SUBLANES - len(rows)), axis=1)


def _layer_fwd(l, xl, mod3, wl, rep, plan, head=None):
    s, d = xl.shape
    t_big, t_mid = _tile_for(s, 512), _tile_for(s, 256)
    nh_ml = rep["ml_b_if"].shape[1] // 2
    vec = lambda name: _vec(rep["vecs"], l, VEC_TABLE.index(name))
    shift, scale, gate = (_vec(mod3, l, kk) for kk in range(3))
    hosted = lambda name: plan.comms((name, l)) if plan else None
    done = lambda name: plan.done((name, l)) if plan else None
    u, hbf = _in_fwd(xl, vec("norm_g"), scale, shift, wl["w_in_g"], 0, t_big, hosted("in_proj_fwd"))
    done("in_proj_fwd")
    h_rg, y_rg, *rg_gates = _rg_fwd(u, d, wl["rg_conv_w"], vec("rg_conv_b"), rep["rg_w_a_bf"][l], vec("rg_b_a"),
                                    rep["rg_w_x_bf"][l], vec("rg_b_x"), vec("rg_lambda"), t_mid, hosted("rglru_fwd"))
    done("rglru_fwd")
    q, k, v, gcol, pre = _ml_pre(u, d, wl["ml_conv_w"], vec("ml_conv_b"), wl["w_qkv"][0], wl["w_qkv"][1],
                                 wl["w_qkv"][2], wl["wif_pad"], wl["bif_pad"], t_mid, hosted("mlstm_proj_fwd"))
    done("mlstm_proj_fwd")
    grow = gcol[:, 0:16].T
    cell, y_ml, cs, ns, ms, mt = _ml_cell_fwd(q, k, v, gcol, grow, u, vec("ml_norm_g"), nh_ml, hosted("mlstm_cell_fwd"))
    done("mlstm_cell_fwd")
    res = _out_fwd(xl, y_rg, y_ml, gate, wl["w_out_g"], 0, t_big, hosted("out_proj_fwd"), head)
    done("out_proj_fwd")
    x_new, y = (res[0], res[1]) if head is None else (tuple(res[1:]), res[0])
    saved = dict(x=xl, u=u, hbf=hbf, h_rg=h_rg, y_rg=y_rg, q=q, k=k, v=v, gcol=gcol, grow=grow, cell=cell, y_ml=y_ml,
                 cs=cs, ns=ns, ms=ms, mt=mt, y=y, scale=scale, gate=gate, rg_gates=rg_gates, pre=pre)
    return x_new, saved


def _layer_bwd(l, dx, sv, wl, rep, plan, grads=None, split_last=False):
    s, d = dx.shape
    t_big, t_mid = _tile_for(s, 512), _tile_for(s, 256)
    nh_ml = rep["ml_b_if"].shape[1] // 2
    nd, _, _, w_cols = wl["w_in_g"].shape
    grads = {} if grads is None else grads
    vec = lambda name: _vec(rep["vecs"], l, VEC_TABLE.index(name))
    hosted = lambda name: plan.comms((name, l)) if plan else None
    done = lambda name: plan.done((name, l)) if plan else None
    dy_rg, dy_ml, gw_out, dgate = _out_bwd(dx, sv["gate"], sv["y"], sv["y_rg"], sv["y_ml"], wl["w_out_g"], 0, t_big,
                                           hosted("out_proj_bwd"))
    grads.update(w_out=gw_out)
    done("out_proj_bwd")
    dq, dk, dv, dgates, d_mlo, d_mlz, g_mlng = _ml_cell_bwd(
        dy_ml, sv["u"], sv["cell"], sv["q"], sv["k"], sv["v"], sv["gcol"], sv["grow"], sv["mt"], sv["cs"], sv["ns"],
        sv["ms"], vec("ml_norm_g"), nh_ml, hosted("mlstm_cell_bwd"))
    done("mlstm_cell_bwd")
    d_mlx, g_wq, g_wk, g_wv, g_wift, g_bif, g_mlcw, g_mlcb = _ml_pre_bwd(
        dq, dk, dv, dgates, sv["gcol"], sv["u"], sv["pre"], sv["q"], sv["k"], sv["v"], wl["ml_conv_w"],
        wl["w_qkv"][0], wl["w_qkv"][1], wl["w_qkv"][2], wl["wift_pad"], t_mid, hosted("mlstm_proj_bwd"))
    done("mlstm_proj_bwd")
    d_rgx, d_rgz, g_wa, g_wx, g_ba, g_bx, g_lam, g_rgcw, g_rgcb = _rg_bwd(
        dy_rg, sv["u"], sv["h_rg"], sv["rg_gates"], wl["rg_conv_w"], rep["rg_w_a_bf"][l], rep["rg_w_x_bf"][l],
        vec("rg_lambda"), t_mid, hosted("rglru_bwd"))
    grads.update(w_qkv=jnp.stack([g_wq, g_wk, g_wv]), rg_conv_w=g_rgcw[0:CONV_WIDTH], ml_conv_w=g_mlcw[0:CONV_WIDTH],
                 wif_t=g_wift[0:8], rg_w_a=g_wa, rg_w_x=g_wx)
    acc = dict(dgate=dgate, rg_conv_b=g_rgcb, rg_b_a=g_ba, rg_b_x=g_bx, rg_lambda=g_lam, ml_conv_b=g_mlcb,
               ml_b_if=g_bif, ml_norm_g=g_mlng)
    done("rglru_bwd")
    pieces = [d_rgx, d_rgz, d_mlx, d_mlo, d_mlz]
    grads.update(w_in=_in_bwd_w(pieces, sv["hbf"], w_cols, tuple(range(nd)), _tile_for(s, 1024), hosted("in_proj_bwd_w")))
    done("in_proj_bwd_w")
    n_tiles = s // t_mid
    counts = [n_tiles // 5, n_tiles - n_tiles // 5 - 1, 1] if split_last and n_tiles >= 5 else [n_tiles]
    in_args = (pieces, sv["x"], dx, vec("norm_g"), sv["scale"], wl["w_in_g"], 0, t_mid)
    res, at = None, 0
    for key, count in zip(("in_proj_bwd_x", "in_proj_bwd_x_rest", "in_proj_bwd_x_end"), counts):
        res = _in_bwd(*in_args, hosted(key), (at, count), res)
        done(key)
        at += count
    dx, dscale, dshift, g_ng = res
    acc.update(norm_g=g_ng, dshift=dshift, dscale=dscale)
    grads.update(acc=acc, dmod=jnp.concatenate([dshift[0:1], dscale[0:1], dgate[0:1]], axis=1))
    return dx, grads


def _full_qkv(qkv_g, d):
    nd, _, rows3, dh = qkv_g.shape
    nh = d // dh
    rsh = rows3 // (3 * nh)
    return qkv_g.reshape(nd, 3, nh, rsh, dh).transpose(1, 2, 0, 3, 4).reshape(3, nh, nd * rsh, dh)


def _small_weights(small, l, ml_b_if):
    nd = small.shape[0]
    sm = small[:, l]
    cw = sm[:, 0:2 * CONV_WIDTH].reshape(nd, 2, CONV_WIDTH, LANES).transpose(1, 2, 0, 3).reshape(2, CONV_WIDTH, nd * LANES)
    if_rows = (sm.shape[1] - 2 * CONV_WIDTH) * LANES // 8
    wif_t = sm[:, 2 * CONV_WIDTH:].reshape(nd, 8, if_rows).transpose(1, 0, 2).reshape(8, nd * if_rows)
    wift_pad = jnp.pad(wif_t, ((0, LANES - 8), (0, 0))).astype(BF16)
    return dict(rg_conv_w=cw[0], ml_conv_w=cw[1], wift_pad=wift_pad, wif_pad=wift_pad.T,
                bif_pad=jnp.pad(ml_b_if[l], (0, LANES - 8)).reshape(1, LANES))


def kernel(x, c, norm_g, w_ada, b_ada, w_in, rg_conv_w, rg_conv_b, rg_w_a, rg_b_a, rg_w_x, rg_b_x, rg_lambda, ml_conv_w, ml_conv_b, ml_w_q, ml_w_k, ml_w_v, ml_w_if, ml_b_if, ml_norm_g, w_out, final_g, loss_target, m_norm_g, m_w_ada, m_b_ada, m_w_in, m_rg_conv_w, m_rg_conv_b, m_rg_w_a, m_rg_b_a, m_rg_w_x, m_rg_b_x, m_rg_lambda, m_ml_conv_w, m_ml_conv_b, m_ml_w_q, m_ml_w_k, m_ml_w_v, m_ml_w_if, m_ml_b_if, m_ml_norm_g, m_w_out, m_final_g, v_norm_g, v_w_ada, v_b_ada, v_w_in, v_rg_conv_w, v_rg_conv_b, v_rg_w_a, v_rg_b_a, v_rg_w_x, v_rg_b_x, v_rg_lambda, v_ml_conv_w, v_ml_conv_b, v_ml_w_q, v_ml_w_k, v_ml_w_v, v_ml_w_if, v_ml_b_if, v_ml_norm_g, v_w_out, v_final_g):
    given = dict(locals())
    nl = w_in.shape[0]
    d = x.shape[2]
    rep = {n: given[n] for n in REPLICATED}
    rep.update(rg_w_a_bf=rg_w_a.astype(BF16), rg_w_x_bf=rg_w_x.astype(BF16))
    bf = lambda a: a.astype(BF16)

    def qkv_shard(prefix):
        return jnp.stack([given[prefix + "ml_w_q"], given[prefix + "ml_w_k"], given[prefix + "ml_w_v"]], axis=1).reshape(
            nl, -1, ml_w_q.shape[-1])

    def small_shard(prefix):
        return _small_pack(given[prefix + "rg_conv_w"], given[prefix + "ml_conv_w"], given[prefix + "ml_w_if"])

    plan = _Plan()
    qkv = qkv_shard("")
    first_ici = _gather_ici_comm([bf(w_in[0:1]), small_shard("")])
    condition = _exchange_comm([jnp.broadcast_to(c, (SUBLANES, d))], True)
    _run_comms([first_ici, condition], "gather_first")
    first_fwd = _gather_fwd_comm(first_ici.results)
    wcols = w_ada.shape[2]
    me = 4 * lax.axis_index("x") + 2 * lax.axis_index("y") + lax.axis_index("c")
    b_cols = jnp.pad(lax.dynamic_slice_in_dim(b_ada, me * wcols, wcols, axis=1), ((0, SUBLANES - nl), (0, 0)))
    mod_cols, cact_all = _ada_mod(condition.results[0], w_ada, b_cols, [first_fwd])
    w_in_first, small = first_fwd.results
    wl = [_small_weights(small, l, ml_b_if) for l in range(nl)]
    wl[0]["w_in_g"] = w_in_first

    def gather_behind(arrs, ici_host, fwd_host, then):
        ici = _gather_ici_comm(arrs)

        def pass_on():
            fwd = _gather_fwd_comm(ici.results)
            plan.host(fwd_host, fwd, lambda: then(fwd.results))

        plan.host(ici_host, ici, pass_on)

    def got_out(l):
        return lambda r: wl[l].update(w_out_g=r[0], w_qkv=_full_qkv(r[1], d))

    gather_behind([bf(w_out[0:1]), bf(qkv[0:1])], ("in_proj_fwd", 0), ("rglru_fwd", 0), got_out(0))
    for l in range(1, nl):
        gather_behind([bf(w_in[l:l + 1])], ("rglru_fwd", l - 1), ("mlstm_cell_fwd", l - 1),
                      lambda r, l=l: wl[l].update(w_in_g=r[0]))
        gather_behind([bf(w_out[l:l + 1]), bf(qkv[l:l + 1])], ("mlstm_cell_fwd", l - 1), ("out_proj_fwd", l - 1), got_out(l))

    mod_blocks = _exchange([mod_cols], False, "scatter_modulation")[0]
    mod3 = mod_blocks[:, 0:nl].transpose(1, 0, 2).reshape(nl, 3, d)
    rep["vecs"] = _vec_table(rep)
    saved, xl = [], x[0]
    for l in range(nl):
        head = (final_g.reshape(1, -1), loss_target[0]) if l == nl - 1 else None
        xl, sv = _layer_fwd(l, xl, mod3, wl[l], rep, plan, head)
        saved.append(sv)
    grad_x, loss_p, g_final = xl

    keys = ("w_in", "w_out", "w_qkv", "small")
    parity = lax.axis_index("c").astype(jnp.int32).reshape(1)
    grads, recv = [None] * nl, [None] * nl

    def small_parts(g):
        return [bf(_qkv_slots(g["w_qkv"], N_DEV)), bf(_small_slots(g))]

    def reduce_behind(l, host_layer):
        parts = [grads[l]["w_in"], grads[l]["w_out"]]
        swap = _core_swap_comm(parts)
        direct = _exchange_comm(small_parts(grads[l]), False)

        def summed():
            sums = [_pair_sum(a, o, parity, "pair_sum_%s_layer%d" % (key, l)) for key, a, o in zip(keys, parts, swap.results)]
            big = _chip_swap_comm([sums[0]])
            rest = _chip_swap_comm([sums[1]])
            plan.host(("mlstm_cell_bwd", host_layer), big)
            plan.host(("rglru_bwd", host_layer), rest,
                      lambda: recv.__setitem__(l, big.results + rest.results + direct.results))

        plan.host(("out_proj_bwd", host_layer), swap, summed)
        plan.host(("out_proj_bwd", host_layer), direct)

    first, own = {}, {}

    def reduce_own(names, parts_fn, ready_key, swap_key, chip_key):
        def go():
            parts = parts_fn()
            swap = _core_swap_comm(parts)

            def summed():
                sums = [_pair_sum(a, o, parity, "pair_sum_%s_layer0" % n) for n, a, o in zip(names, parts, swap.results)]
                chip = _chip_swap_comm(sums)
                plan.host(chip_key, chip, lambda: own.update(zip(names, chip.results)))

            plan.host(swap_key, swap, summed)

        plan.after.setdefault(ready_key, []).append(go)

    reduce_own(["w_out"], lambda: [first["w_out"]], ("out_proj_bwd", 0), ("mlstm_cell_bwd", 0), ("mlstm_proj_bwd", 0))
    reduce_own(["w_in"], lambda: [first["w_in"]], ("in_proj_bwd_w", 0), ("in_proj_bwd_x", 0), ("in_proj_bwd_x_rest", 0))

    def small_own():
        direct = _exchange_comm(small_parts(first), False)
        plan.host(("in_proj_bwd_w", 0), direct, lambda: own.update(w_qkv=direct.results[0], small=direct.results[1]))

    plan.after.setdefault(("rglru_bwd", 0), []).append(small_own)

    matrices = {}

    def reduce_matrices():
        layers = [grads[l] if l > 0 else first for l in range(nl)]
        mp = jnp.stack([jnp.stack([g["rg_w_a"], g["rg_w_x"]]) for g in layers]).reshape(N_DEV, -1, LANES).astype(BF16)
        scatter = _exchange_comm([mp], False)

        def summed():
            gather = _exchange_comm(_sum_parts(scatter.results, "sum_replicated_matrices"), True)
            plan.host(("in_proj_bwd_x", 0), gather, lambda: matrices.update(mp=gather.results[0].reshape(-1, LANES)))

        plan.host(("in_proj_bwd_w", 0), scatter, summed)

    plan.after.setdefault(("rglru_bwd", 0), []).append(reduce_matrices)

    for l in reversed(range(nl)):
        if l > 0:
            grad_x, grads[l] = _layer_bwd(l, grad_x, saved[l], wl[l], rep, plan)
            reduce_behind(l, l - 1)
        else:
            grad_x, grads[l] = _layer_bwd(l, grad_x, saved[l], wl[l], rep, plan, first, True)
    plan.flush()
    recv[0] = [own[key] for key in keys]

    shard = {p: dict(w_in=given[p + "w_in"], w_out=given[p + "w_out"], w_qkv=qkv_shard(p), small=small_shard(p))
             for p in ("", "m_", "v_")}
    res = {}
    for ki, key in enumerate(keys):
        res[key] = _reduce_adam([recv[l][ki] for l in range(nl)], shard[""][key], shard["m_"][key], shard["v_"][key],
                                "reduce_adam_" + key)

    dmods = jnp.concatenate([grads[l]["dmod"] for l in range(nl)], axis=0)
    dmod_blocks = jnp.pad(dmods.reshape(nl, N_DEV, wcols).transpose(1, 0, 2), ((0, 0), (0, SUBLANES - nl), (0, 0)))
    dmod_all = _exchange([dmod_blocks], False, "scatter_dmod")[0][:, 0:nl].transpose(1, 0, 2)
    res["w_ada"] = _ada_grad_adam(cact_all.T, dmod_all, w_ada, m_w_ada, v_w_ada)

    widen = lambda a: jnp.pad(a, ((0, 0), (0, d - a.shape[1])))
    rows = [widen(grads[l]["acc"][n][0:1]) for l in range(nl) for n in REP_ROWS] + [g_final[0:1], widen(loss_p[0:1])]
    vp = jnp.concatenate(rows + [jnp.zeros(((-len(rows)) % SUBLANES, d), F32)], axis=0)
    vp_all = _exchange([vp], True, "gather_replicated")[0]
    mp_r = matrices["mp"]
    lanes = lambda a: jnp.pad(a, ((0, 0), (0, LANES - a.shape[1])))
    shaped = dict(ml_b_if=lanes, final_g=lambda a: a.reshape(1, d))
    names = [n for n in REPLICATED if n != "b_ada"] + ["b_ada"]
    rep_res, vp_r = _adam_replicated(vp_all, mp_r, {n: tuple(shaped.get(n, lambda a: a)(given[p + n]) for p in ("", "m_", "v_"))
                                                    for n in names}, nl)
    unshaped = dict(ml_b_if=lambda a: a[:, 0:ml_b_if.shape[1]], final_g=lambda a: a.reshape(d))
    rep_out = [{n: unshaped.get(n, lambda a: a)(rep_res[n][kind]) for n in names} for kind in range(4)]
    loss = vp_r[nl * len(REP_ROWS) + 1, 0]

    if_rows = ml_w_if.shape[1]
    order = ("norm_g", "w_ada", "b_ada", "w_in", "rg_conv_w", "rg_conv_b", "rg_w_a", "rg_b_a", "rg_w_x", "rg_b_x",
             "rg_lambda", "ml_conv_w", "ml_conv_b", "ml_w_q", "ml_w_k", "ml_w_v", "ml_w_if", "ml_b_if", "ml_norm_g",
             "w_out", "final_g")
    outs = [loss, grad_x[None]]
    for kind in range(4):
        qkv_k = res["w_qkv"][kind].reshape((nl, 3) + ml_w_q.shape[1:])
        rg_cw, ml_cw, wif = _small_unpack(res["small"][kind], if_rows)
        sharded = dict(w_ada=res["w_ada"][kind], w_in=res["w_in"][kind], w_out=res["w_out"][kind], ml_w_q=qkv_k[:, 0],
                       ml_w_k=qkv_k[:, 1], ml_w_v=qkv_k[:, 2], rg_conv_w=rg_cw, ml_conv_w=ml_cw, ml_w_if=wif)
        for n in order:
            outs.append(sharded[n] if n in sharded else rep_out[kind][n])
    return tuple(outs)
```

```python
import functools

import jax
import jax.numpy as jnp
from jax import lax
from jax.experimental import pallas as pl
from jax.experimental.pallas import tpu as pltpu

F32 = jnp.float32
BF16 = jnp.bfloat16
MESH_AXES = ("x", "y", "c")
N_DEV = 8
EPS = 1e-6
RG_C = 8.0
ML_CHUNK = 128
CONV_WIDTH = 4
ADAM_LR = 0.001
ADAM_B1 = 0.9
ADAM_B2 = 0.999
ADAM_EPS = 1e-08
ADAM_WD = 0.01
ADAM_STEP = 10
NEG_BIG = -1e30
LANES = 128
SUBLANES = 8
VMEM_LIMIT = 56 * 1024 * 1024
HI = lax.Precision.HIGHEST


def _params(n_grid):
    return pltpu.CompilerParams(dimension_semantics=("arbitrary",) * n_grid, vmem_limit_bytes=VMEM_LIMIT)


def _mm(a, b):
    return jnp.dot(a.astype(BF16), b.astype(BF16), preferred_element_type=F32)


def _mm_nt(a, b):
    return lax.dot_general(a.astype(BF16), b.astype(BF16), (((1,), (1,)), ((), ())), preferred_element_type=F32)


def _mm_tn(a, b):
    return lax.dot_general(a.astype(BF16), b.astype(BF16), (((0,), (0,)), ((), ())), preferred_element_type=F32)


def _mm_hi(a, b):
    return jnp.dot(a, b, precision=HI, preferred_element_type=F32)


def _sigmoid(x):
    return 1.0 / (1.0 + jnp.exp(-x))


def _softplus(x):
    return jnp.maximum(x, 0.0) + jnp.log(1.0 + jnp.exp(-jnp.abs(x)))


def _neg_expm1(x):
    poly = -x * (1.0 + x * (0.5 + x * (1.0 / 6.0 + x * (1.0 / 24.0 + x * (1.0 / 120.0)))))
    return jnp.where(jnp.abs(x) < 0.05, poly, 1.0 - jnp.exp(x))


def _iota(shape, dim):
    return lax.broadcasted_iota(jnp.int32, shape, dim)


def _colsum(x):
    return jnp.sum(x, axis=0, keepdims=True)


def _rowsum(x):
    return jnp.sum(x, axis=1, keepdims=True)


def _col(x, j):
    return _rowsum(jnp.where(_iota(x.shape, 1) == j, x, 0.0))


def _row(x, j):
    return _colsum(jnp.where(_iota(x.shape, 0) == j, x, 0.0))


def _shift_down(x, j, prev8):
    if j == 0:
        return x
    t = x.shape[0]
    main = jnp.where(_iota(x.shape, 0) >= j, pltpu.roll(x, j, 0), 0.0)
    fix = jnp.where(_iota(prev8.shape, 0) < j, pltpu.roll(prev8, j, 0), 0.0)
    return jnp.concatenate([main[0:SUBLANES] + fix, main[SUBLANES:t]], axis=0)


def _shift_up(x, j, next8):
    if j == 0:
        return x
    t = x.shape[0]
    main = jnp.where(_iota(x.shape, 0) < t - j, pltpu.roll(x, t - j, 0), 0.0)
    fix = jnp.where(_iota(next8.shape, 0) >= SUBLANES - j, pltpu.roll(next8, SUBLANES - j, 0), 0.0)
    return jnp.concatenate([main[0:t - SUBLANES], main[t - SUBLANES:t] + fix], axis=0)


def _conv(x, prev8, w_ref):
    y = w_ref[CONV_WIDTH - 1:CONV_WIDTH, :] * x
    for j in range(1, CONV_WIDTH):
        y = y + w_ref[CONV_WIDTH - 1 - j:CONV_WIDTH - j, :] * _shift_down(x, j, prev8)
    return y


def _conv_bwd(dy, x, next8, w_ref, gw_ref):
    dx = None
    for j in range(CONV_WIDTH):
        k = CONV_WIDTH - 1 - j
        up = _shift_up(dy, j, next8)
        gw_ref[k:k + 1, :] += _colsum(up * x)
        term = w_ref[k:k + 1, :] * up
        dx = term if dx is None else dx + term
    return dx


def _scan_into(a, b, carry, out_ref, reverse):
    t, c = a.shape
    groups = t // SUBLANES
    a3 = a.reshape(groups, SUBLANES, c)
    b3 = b.reshape(groups, SUBLANES, c)
    sub = _iota(a3.shape, 1)
    for step in (1, 2, 4):
        keep = sub < SUBLANES - step if reverse else sub >= step
        shift = SUBLANES - step if reverse else step
        a_s = jnp.where(keep, pltpu.roll(a3, shift, 1), 1.0)
        b_s = jnp.where(keep, pltpu.roll(b3, shift, 1), 0.0)
        b3 = a3 * b_s + b3
        a3 = a3 * a_s
    for g in (reversed(range(groups)) if reverse else range(groups)):
        rows = slice(g * SUBLANES, (g + 1) * SUBLANES)
        out_ref[rows, :] = b3[g] + a3[g] * carry
        edge = g * SUBLANES if reverse else (g + 1) * SUBLANES - 1
        carry = out_ref[edge:edge + 1, :]


def _blockdiag(x, w_ref, transpose_w=False):
    nh, dh, _ = w_ref.shape
    outs = []
    for h in range(nh):
        xs = x[:, h * dh:(h + 1) * dh]
        outs.append(_mm_nt(xs, w_ref[h]) if transpose_w else _mm(xs, w_ref[h]))
    return jnp.concatenate(outs, axis=1)


def _rg_gates(xc, wa_ref, ba_ref, wx_ref, bx_ref, lam_ref):
    r = _sigmoid(_blockdiag(xc, wa_ref) + ba_ref[...])
    ig = _sigmoid(_blockdiag(xc, wx_ref) + bx_ref[...])
    sp = _softplus(-lam_ref[...])
    log_a = -RG_C * r * sp
    a = jnp.exp(log_a)
    beta = jnp.sqrt(_neg_expm1(2.0 * log_a))
    return r, ig, sp, a, beta


def _bcast8(row):
    return jnp.broadcast_to(row, (SUBLANES, row.shape[1]))


def _full(shape):
    nd = len(shape)
    return pl.BlockSpec(shape, lambda *_: (0,) * nd)


class _Comm:
    def __init__(self, arrays, out_shapes, sems, start, finish, aliases=()):
        self.arrays, self.out_shapes, self.sems = list(arrays), list(out_shapes), list(sems)
        self.start, self.finish, self.aliases = start, finish, tuple(aliases)
        self.results = None


class _RowOf:
    def __init__(self, ref, k):
        self.ref, self.k = ref, k

    def __getitem__(self, idx):
        cols = slice(None) if idx is Ellipsis else idx[1]
        return self.ref[0, self.k:self.k + 1, cols]


class _PartOf:
    def __init__(self, ref, rows=None, cols=None, lead=None):
        self.ref, self.rows, self.cols, self.lead = ref, rows, cols, lead
        if rows is not None:
            self.shape = (rows.stop - rows.start,) + tuple(ref.shape[1:])

    def _at(self, idx):
        if self.lead is not None:
            return (self.lead,) + tuple(idx[1:])
        if self.cols is not None:
            return (slice(None), self.cols)
        return (self.rows, slice(None) if idx is Ellipsis else idx[1])

    def __getitem__(self, idx):
        return self.ref[self._at(idx)]

    def __setitem__(self, idx, value):
        self.ref[self._at(idx)] = value


def _vec(table, layer, k):
    return ("row", table, layer, k)


def _is_row(arg):
    return isinstance(arg, tuple) and len(arg) == 4 and arg[0] == "row"


def _call(body, comms, *, name, grid, in_specs, out_specs, out_shape, args, scratch_shapes=(), aliases=None):
    comms = [cm for cm in (comms or []) if cm is not None]
    rows = {i: a[3] for i, a in enumerate(args) if _is_row(a)}
    in_specs = [pl.BlockSpec((1,) + a[1].shape[1:], functools.partial(lambda layer, *_: (layer, 0, 0), a[2]))
                if _is_row(a) else sp for a, sp in zip(args, in_specs)]
    args = tuple(a[1] if _is_row(a) else a for a in args)
    n_in, n_out, n_sc = len(args), len(out_shape), len(scratch_shapes)
    c_arrays = [a for cm in comms for a in cm.arrays]
    c_outs = [o for cm in comms for o in cm.out_shapes]
    c_sems = [sm for cm in comms for sm in cm.sems]
    aliases, a_at, o_at = dict(aliases or {}), n_in, n_out
    for cm in comms:
        for (i, j) in cm.aliases:
            aliases[a_at + i] = o_at + j
        a_at += len(cm.arrays)
        o_at += len(cm.out_shapes)

    def wrapped(*refs):
        ins, c_in = refs[:n_in], refs[n_in:n_in + len(c_arrays)]
        ins = [_RowOf(r, rows[i]) if i in rows else r for i, r in enumerate(ins)]
        at = n_in + len(c_arrays)
        outs, c_out = refs[at:at + n_out], refs[at + n_out:at + n_out + len(c_outs)]
        at += n_out + len(c_outs)
        scr, sems = refs[at:at + n_sc], refs[at + n_sc:]
        views, ia, io, isem = [], 0, 0, 0
        for cm in comms:
            views.append((c_in[ia:ia + len(cm.arrays)], c_out[io:io + len(cm.out_shapes)], sems[isem:isem + len(cm.sems)]))
            ia, io, isem = ia + len(cm.arrays), io + len(cm.out_shapes), isem + len(cm.sems)
        if comms:
            @pl.when(pl.program_id(0) == 0)
            def _():
                for cm, view in zip(comms, views):
                    cm.start(*view)

        body(*ins, *outs, *scr)
        if comms:
            @pl.when(pl.program_id(0) == grid[0] - 1)
            def _():
                for cm, view in zip(comms, views):
                    cm.finish(*view)

    hbm = pl.BlockSpec(memory_space=pl.ANY)
    res = pl.pallas_call(
        wrapped, name=name, grid=grid,
        in_specs=list(in_specs) + [hbm] * len(c_arrays), out_specs=list(out_specs) + [hbm] * len(c_outs),
        out_shape=list(out_shape) + c_outs, scratch_shapes=list(scratch_shapes) + c_sems,
        input_output_aliases=aliases, compiler_params=_params(len(grid)),
    )(*args, *c_arrays)
    at = n_out
    for cm in comms:
        cm.results = list(res[at:at + len(cm.out_shapes)])
        at += len(cm.out_shapes)
    return list(res[:n_out])


def _join_columns(w_ref, wcat):
    nd, _, _, w = w_ref.shape

    @pl.when(pl.program_id(0) == 0)
    def _():
        for j in range(nd):
            wcat[:, j * w:(j + 1) * w] = w_ref[j, 0]


def _in_fwd(x, ng, scale, shift, w_in_g, layer, tile, comms=None):
    s, d = x.shape
    nd, _, _, w = w_in_g.shape

    def body(x_ref, ng_ref, sc_ref, sh_ref, w_ref, u_ref, h_ref, wcat):
        _join_columns(w_ref, wcat)
        xv = x_ref[...]
        rs = lax.rsqrt(jnp.mean(xv * xv, axis=1, keepdims=True) + EPS)
        hb = (xv * rs * ng_ref[...] * (1.0 + sc_ref[...]) + sh_ref[...]).astype(BF16)
        h_ref[...] = hb
        u_ref[...] = jnp.dot(hb, wcat[...], preferred_element_type=F32)

    return _call(
        body, comms, name="in_proj_fwd", grid=(s // tile,),
        in_specs=[pl.BlockSpec((tile, d), lambda i: (i, 0)), _full((1, d)), _full((1, d)), _full((1, d)),
                  pl.BlockSpec((nd, 1, d, w), lambda i: (0, layer, 0, 0), pipeline_mode=pl.Buffered(1))],
        out_specs=[pl.BlockSpec((tile, nd * w), lambda i: (i, 0)), pl.BlockSpec((tile, d), lambda i: (i, 0))],
        out_shape=[jax.ShapeDtypeStruct((s, nd * w), F32), jax.ShapeDtypeStruct((s, d), BF16)],
        scratch_shapes=[pltpu.VMEM((d, nd * w), BF16)],
        args=(x, ng, scale, shift, w_in_g))


def _rg_fwd(u, d, conv_w, conv_b, w_a, b_a, w_x, b_x, lam, tile, comms=None):
    s = u.shape[0]

    def body(x_ref, z_ref, cw_ref, cb_ref, wa_ref, ba_ref, wx_ref, bx_ref, lam_ref,
             h_ref, y_ref, xc_ref, r_ref, i_ref, a_ref, beta_ref, prev8, hcar):
        @pl.when(pl.program_id(0) == 0)
        def _():
            prev8[...] = jnp.zeros_like(prev8)
            hcar[...] = jnp.zeros_like(hcar)

        x = x_ref[...]
        xc = _conv(x, prev8[...], cw_ref) + cb_ref[...]
        prev8[...] = x[tile - SUBLANES:tile, :]
        r, ig, _, a, beta = _rg_gates(xc, wa_ref, ba_ref, wx_ref, bx_ref, lam_ref)
        xc_ref[...] = xc
        r_ref[...] = r
        i_ref[...] = ig
        a_ref[...] = a
        beta_ref[...] = beta
        _scan_into(a, beta * ig * xc, hcar[SUBLANES - 1:SUBLANES, :], h_ref, False)
        h = h_ref[...]
        hcar[...] = h[tile - SUBLANES:tile, :]
        z = z_ref[...]
        y_ref[...] = (h * z * _sigmoid(z)).astype(BF16)

    vec = _full((1, d))
    return _call(
        body, comms, name="rglru_fwd", grid=(s // tile,),
        in_specs=[pl.BlockSpec((tile, d), lambda i: (i, 0)), pl.BlockSpec((tile, d), lambda i: (i, 1)),
                  _full(conv_w.shape), vec, _full(w_a.shape), vec, _full(w_x.shape), vec, vec],
        out_specs=[pl.BlockSpec((tile, d), lambda i: (i, 0))] * 7,
        out_shape=[jax.ShapeDtypeStruct((s, d), F32), jax.ShapeDtypeStruct((s, d), BF16)] + [jax.ShapeDtypeStruct((s, d), F32)] * 5,
        scratch_shapes=[pltpu.VMEM((SUBLANES, d), F32), pltpu.VMEM((SUBLANES, d), F32)],
        args=(u, u, conv_w, conv_b, w_a, b_a, w_x, b_x, lam))


def _ml_pre(u, d, conv_w, conv_b, w_q, w_k, w_v, wif, bif, tile, comms=None):
    s = u.shape[0]
    nh = w_q.shape[0]

    def body(x_ref, cw_ref, cb_ref, wq_ref, wk_ref, wv_ref, wif_ref, bif_ref, q_ref, k_ref, v_ref, g_ref, pre_ref, prev8):
        @pl.when(pl.program_id(0) == 0)
        def _():
            prev8[...] = jnp.zeros_like(prev8)

        x = x_ref[...]
        pre = _conv(x, prev8[...], cw_ref) + cb_ref[...]
        prev8[...] = x[tile - SUBLANES:tile, :]
        xc = pre * _sigmoid(pre)
        q = _blockdiag(xc, wq_ref)
        k = _blockdiag(xc, wk_ref)
        v = _blockdiag(x, wv_ref)
        pre_ref[...] = pre
        q_ref[...] = q
        k_ref[...] = k
        v_ref[...] = v
        g = _mm(q, wif_ref[0:d, :]) + _mm(k, wif_ref[d:2 * d, :]) + _mm(v, wif_ref[2 * d:3 * d, :]) + bif_ref[...]
        lane = _iota(g.shape, 1)
        gl = jnp.where(lane < 4, g, jnp.where(lane < 8, -_softplus(-g), 0.0))
        tri = jnp.where(_iota((ML_CHUNK, ML_CHUNK), 1) <= _iota((ML_CHUNK, ML_CHUNK), 0), 1.0, 0.0)
        cums = [_mm_hi(tri, gl[c * ML_CHUNK:(c + 1) * ML_CHUNK, :]) for c in range(tile // ML_CHUNK)]
        cum = cums[0] if len(cums) == 1 else jnp.concatenate(cums, axis=0)
        g_ref[...] = gl + jnp.where((lane >= 8) & (lane < 12), pltpu.roll(cum, 4, 1), 0.0)

    vec = _full((1, d))
    return _call(
        body, comms, name="mlstm_proj_fwd", grid=(s // tile,),
        in_specs=[pl.BlockSpec((tile, d), lambda i: (i, 2)), _full(conv_w.shape), vec,
                  _full(w_q.shape), _full(w_k.shape), _full(w_v.shape), _full(wif.shape), _full((1, LANES))],
        out_specs=[pl.BlockSpec((tile, d), lambda i: (i, 0))] * 3 + [pl.BlockSpec((tile, LANES), lambda i: (i, 0)),
                                                                     pl.BlockSpec((tile, d), lambda i: (i, 0))],
        out_shape=[jax.ShapeDtypeStruct((s, d), F32)] * 3 + [jax.ShapeDtypeStruct((s, LANES), F32),
                                                             jax.ShapeDtypeStruct((s, d), F32)],
        scratch_shapes=[pltpu.VMEM((SUBLANES, d), F32)],
        args=(u, conv_w, conv_b, w_q, w_k, w_v, wif, bif))


CELL_CHUNKS_PER_STEP = 4
CELL_BWD_CHUNKS_PER_STEP = 2


def _cell_chunk(h, nh, q_ref, k_ref, v_ref, gc, gr, m_prev, c_h, n_h, m_t=None, r0=0):
    lc = ML_CHUNK
    dh = q_ref.shape[1] // nh
    sl = slice(h * dh, (h + 1) * dh)
    qh = q_ref[r0:r0 + lc, sl]
    kh = k_ref[r0:r0 + lc, sl] * (dh ** -0.5)
    vh = v_ref[r0:r0 + lc, sl]
    li_c = _col(gc, h)
    b_c = _col(gc, 8 + h)
    lib_r = _row(gr, h) - _row(gr, 8 + h)
    b_last = _colsum(jnp.where(_iota((lc, 1), 0) == lc - 1, b_c, 0.0))
    causal = _iota((lc, lc), 1) <= _iota((lc, lc), 0)
    dmat = jnp.where(causal, b_c + lib_r, NEG_BIG)
    m_inter = b_c + m_prev
    if m_t is None:
        m_t = jnp.maximum(m_inter, jnp.max(dmat, axis=1, keepdims=True))
    w_intra = jnp.exp(dmat - m_t)
    w_inter = jnp.exp(m_inter - m_t)
    amat = _mm_nt(qh, kh)
    smat = amat * w_intra
    qc = _mm(qh, c_h)
    qn = _rowsum(qh * n_h)
    den = _rowsum(smat) + w_inter * qn
    gst = b_last - b_c + li_c
    m_new = jnp.maximum(b_last + m_prev, jnp.max(gst, axis=0, keepdims=True))
    w_state = jnp.exp(gst - m_new)
    decay = jnp.exp(b_last + m_prev - m_new)
    return dict(sl=sl, qh=qh, kh=kh, vh=vh, m_t=m_t, w_intra=w_intra, w_inter=w_inter, smat=smat, qc=qc, qn=qn,
                den=den, m_new=m_new, w_state=w_state, decay=decay)


def _ml_cell_fwd(q, k, v, gcol, grow, u, ng, nh, comms=None):
    s, d = q.shape
    lc = ML_CHUNK
    nc = s // lc
    dh = d // nh

    per = CELL_CHUNKS_PER_STEP if nc % CELL_CHUNKS_PER_STEP == 0 else 1

    def body(q_ref, k_ref, v_ref, gc_ref, gr_ref, o_ref, z_ref, ng_ref,
             cell_ref, y_ref, cs_ref, ns_ref, ms_ref, mt_ref, c_sc, n_sc, m_sc):
        @pl.when(pl.program_id(0) == 0)
        def _():
            c_sc[...] = jnp.zeros_like(c_sc)
            n_sc[...] = jnp.zeros_like(n_sc)
            m_sc[...] = jnp.zeros_like(m_sc)

        lane = _iota((lc, LANES), 1)
        for cc in range(per):
            rows = slice(cc * lc, (cc + 1) * lc)
            gc = gc_ref[rows, :]
            gr = gr_ref[:, rows]
            mt_acc = jnp.zeros((lc, LANES), F32)
            for h in range(nh):
                c_h = c_sc[h]
                n_h = n_sc[h, 0:1, :]
                m_prev = jnp.max(m_sc[h, 0:1, :], axis=1, keepdims=True)
                cs_ref[cc, h] = c_h
                ns_ref[cc, h] = n_sc[h]
                ms_ref[cc, h] = m_sc[h]
                t = _cell_chunk(h, nh, q_ref, k_ref, v_ref, gc, gr, m_prev, c_h, n_h, r0=cc * lc)
                sl = t["sl"]
                num = _mm(t["smat"], t["vh"]) + t["w_inter"] * t["qc"]
                cell_h = num / jnp.maximum(jnp.abs(t["den"]), jnp.exp(-t["m_t"]))
                mt_acc = jnp.where(lane == h, t["m_t"], mt_acc)
                kw = t["kh"] * t["w_state"]
                c_sc[h] = t["decay"] * c_h + _mm_tn(kw, t["vh"])
                n_sc[h] = _bcast8(t["decay"] * n_h + _colsum(kw))
                m_sc[h] = jnp.broadcast_to(t["m_new"], (SUBLANES, LANES))
                hg = _sigmoid(o_ref[rows, sl]) * cell_h
                hn = hg * lax.rsqrt(jnp.mean(hg * hg, axis=1, keepdims=True) + EPS)
                z = z_ref[rows, sl]
                cell_ref[rows, sl] = cell_h
                y_ref[rows, sl] = (hn * ng_ref[:, sl] * z * _sigmoid(z)).astype(BF16)
            mt_ref[rows, :] = mt_acc

    tok = pl.BlockSpec((per * lc, d), lambda c: (c, 0))
    return _call(
        body, comms, name="mlstm_cell_fwd", grid=(nc // per,),
        in_specs=[tok, tok, tok, pl.BlockSpec((per * lc, LANES), lambda c: (c, 0)),
                  pl.BlockSpec((16, per * lc), lambda c: (0, c)),
                  pl.BlockSpec((per * lc, d), lambda c: (c, 3)), pl.BlockSpec((per * lc, d), lambda c: (c, 4)), _full((1, d))],
        out_specs=[tok, tok, pl.BlockSpec((per, nh, dh, dh), lambda c: (c, 0, 0, 0)),
                   pl.BlockSpec((per, nh, SUBLANES, dh), lambda c: (c, 0, 0, 0)),
                   pl.BlockSpec((per, nh, SUBLANES, LANES), lambda c: (c, 0, 0, 0)),
                   pl.BlockSpec((per * lc, LANES), lambda c: (c, 0))],
        out_shape=[jax.ShapeDtypeStruct((s, d), F32), jax.ShapeDtypeStruct((s, d), BF16),
                   jax.ShapeDtypeStruct((nc, nh, dh, dh), F32), jax.ShapeDtypeStruct((nc, nh, SUBLANES, dh), F32),
                   jax.ShapeDtypeStruct((nc, nh, SUBLANES, LANES), F32), jax.ShapeDtypeStruct((s, LANES), F32)],
        scratch_shapes=[pltpu.VMEM((nh, dh, dh), F32), pltpu.VMEM((nh, SUBLANES, dh), F32),
                        pltpu.VMEM((nh, SUBLANES, LANES), F32)],
        args=(q, k, v, gcol, grow, u, u, ng))


def _out_fwd(x, y_rg, y_ml, gate, w_out_g, layer, tile, comms=None, head=None):
    s, d = x.shape
    nd, _, r, _ = w_out_g.shape

    def body(x_ref, yr_ref, ym_ref, g_ref, w_ref, *rest):
        ycat = jnp.concatenate([yr_ref[...].astype(BF16), ym_ref[...].astype(BF16)], axis=1)
        acc = jnp.dot(ycat, w_ref[...].reshape(nd * r, d), preferred_element_type=F32)
        xn = x_ref[...] + g_ref[...] * acc
        if head is None:
            xn_ref, y_ref = rest
            y_ref[...] = acc
            xn_ref[...] = xn
            return
        fg_ref, t_ref, y_ref, dx_ref, loss_ref, gg_ref = rest
        y_ref[...] = acc

        @pl.when(pl.program_id(0) == 0)
        def _():
            loss_ref[...] = jnp.zeros_like(loss_ref)
            gg_ref[...] = jnp.zeros_like(gg_ref)

        fg = fg_ref[...]
        rs = lax.rsqrt(jnp.mean(xn * xn, axis=1, keepdims=True) + EPS)
        xh = xn * rs
        e = xh * fg - t_ref[...]
        loss_ref[...] += jnp.broadcast_to(_colsum(_rowsum(e * e)) * (0.5 / d), loss_ref.shape)
        dy = e * (1.0 / d)
        gg_ref[...] += _bcast8(_colsum(dy * xh))
        dxh = dy * fg
        dx_ref[...] = rs * (dxh - xh * jnp.mean(dxh * xh, axis=1, keepdims=True))

    tok = pl.BlockSpec((tile, d), lambda i: (i, 0))
    in_specs = [tok, tok, tok, _full((1, d)), pl.BlockSpec((nd, 1, r, d), lambda i: (0, layer, 0, 0))]
    if head is None:
        return _call(body, comms, name="out_proj_fwd", grid=(s // tile,), in_specs=in_specs, out_specs=[tok, tok],
                     out_shape=[jax.ShapeDtypeStruct((s, d), F32)] * 2, args=(x, y_rg, y_ml, gate, w_out_g))
    return _call(
        body, comms, name="out_proj_loss", grid=(s // tile,),
        in_specs=in_specs + [_full((1, d)), tok],
        out_specs=[tok, tok, _full((SUBLANES, LANES)), _full((SUBLANES, d))],
        out_shape=[jax.ShapeDtypeStruct((s, d), F32)] * 2 + [jax.ShapeDtypeStruct((SUBLANES, LANES), F32),
                                                             jax.ShapeDtypeStruct((SUBLANES, d), F32)],
        args=(x, y_rg, y_ml, gate, w_out_g, *head))


def _ml_out_stage_bwd(dy, cell, o, z, ng):
    so = _sigmoid(o)
    hg = so * cell
    rinv = lax.rsqrt(jnp.mean(hg * hg, axis=1, keepdims=True) + EPS)
    hn = hg * rinv
    sz = _sigmoid(z)
    dz = dy * hn * ng * (sz + z * sz * (1.0 - sz))
    dymid = dy * z * sz
    dhn = dymid * ng
    dhg = rinv * (dhn - hn * jnp.mean(dhn * hn, axis=1, keepdims=True))
    return dz, dhg * cell * so * (1.0 - so), dhg * so, _colsum(dymid * hn)


def _out_bwd(dxo, gate, y, y_rg, y_ml, w_out_g, layer, tile, comms=None):
    s, d = dxo.shape
    nd, _, r, _ = w_out_g.shape

    def body(dx_ref, g_ref, y_ref, yr_ref, ym_ref, w_ref, dyr_ref, dym_ref, gw_ref, dg_ref):
        @pl.when(pl.program_id(0) == 0)
        def _():
            gw_ref[...] = jnp.zeros_like(gw_ref)
            dg_ref[...] = jnp.zeros_like(dg_ref)

        dxv = dx_ref[...]
        dg_ref[...] += _bcast8(_colsum(dxv * y_ref[...]))
        dyb = (dxv * g_ref[...]).astype(BF16)
        dycat = lax.dot_general(dyb, w_ref[...].reshape(nd * r, d), (((1,), (1,)), ((), ())), preferred_element_type=F32)
        dyr_ref[...] = dycat[:, 0:d]
        dym_ref[...] = dycat[:, d:2 * d]
        ycat = jnp.concatenate([yr_ref[...].astype(BF16), ym_ref[...].astype(BF16)], axis=1)
        gw_ref[...] += lax.dot_general(ycat, dyb, (((0,), (0,)), ((), ())), preferred_element_type=F32).reshape(nd, r, d)

    tok = pl.BlockSpec((tile, d), lambda i: (i, 0))
    return _call(
        body, comms, name="out_proj_bwd", grid=(s // tile,),
        in_specs=[tok, _full((1, d)), tok, tok, tok, pl.BlockSpec((nd, 1, r, d), lambda i: (0, layer, 0, 0))],
        out_specs=[tok, tok, _full((nd, r, d)), _full((SUBLANES, d))],
        out_shape=[jax.ShapeDtypeStruct((s, d), F32)] * 2 + [jax.ShapeDtypeStruct((nd, r, d), F32),
                                                             jax.ShapeDtypeStruct((SUBLANES, d), F32)],
        args=(dxo, gate, y, y_rg, y_ml, w_out_g))


def _ml_cell_bwd(dy_ml, u, cell, q, k, v, gcol, grow, mt, cs, ns, ms, ng, nh, comms=None):
    s, d = q.shape
    lc = ML_CHUNK
    nc = s // lc
    dh = d // nh

    per = CELL_BWD_CHUNKS_PER_STEP if nc % CELL_BWD_CHUNKS_PER_STEP == 0 else 1

    def body(*refs):
        gng_ref, dc_sc, dn_sc = refs[20], refs[21], refs[22]

        @pl.when(pl.program_id(0) == 0)
        def _():
            dc_sc[...] = jnp.zeros_like(dc_sc)
            dn_sc[...] = jnp.zeros_like(dn_sc)
            gng_ref[...] = jnp.zeros_like(gng_ref)

        for cc in reversed(range(per)):
            rows = slice(cc * lc, (cc + 1) * lc)
            views = [refs[at] if at == 13 else _PartOf(refs[at], cols=rows) if at == 8 else
                     _PartOf(refs[at], lead=cc) if at in (10, 11, 12) else _PartOf(refs[at], rows=rows) for at in range(20)]
            chunk(*views, gng_ref, dc_sc, dn_sc)

    def chunk(dy_ref, o_ref, z_ref, cell_ref, q_ref, k_ref, v_ref, gc_ref, gr_ref, mt_ref, cs_ref, ns_ref, ms_ref,
              ng_ref, dq_ref, dk_ref, dv_ref, dg_ref, do_ref, dz_ref, gng_ref, dc_sc, dn_sc):
        gc = gc_ref[...]
        gr = gr_ref[...]
        mtv = mt_ref[...]
        lane = _iota((lc, LANES), 1)
        rowv = _iota((lc, 1), 0)
        dg_acc = jnp.zeros((lc, LANES), F32)
        for h in range(nh):
            c_h = cs_ref[0, h]
            n_h = ns_ref[0, h, 0:1, :]
            m_prev = jnp.max(ms_ref[0, h, 0:1, :], axis=1, keepdims=True)
            t = _cell_chunk(h, nh, q_ref, k_ref, v_ref, gc, gr, m_prev, c_h, n_h, m_t=_col(mtv, h))
            sl, qh, kh, vh = t["sl"], t["qh"], t["kh"], t["vh"]
            w_intra, w_inter, smat, w_state, decay = t["w_intra"], t["w_inter"], t["smat"], t["w_state"], t["decay"]
            cell_h = cell_ref[:, sl]
            dz, do, dcell, gng = _ml_out_stage_bwd(dy_ref[:, sl], cell_h, o_ref[:, sl], z_ref[:, sl], ng_ref[:, sl])
            dz_ref[:, sl] = dz.astype(BF16)
            do_ref[:, sl] = do.astype(BF16)
            gng_ref[:, sl] += _bcast8(gng)
            eneg = jnp.exp(-t["m_t"])
            aden = jnp.abs(t["den"])
            nst = jnp.maximum(aden, eneg)
            dnum = dcell / nst
            dden = jnp.where(aden > eneg, -_rowsum(cell_h * dcell) / nst * jnp.sign(t["den"]), 0.0)
            pmat = _mm_nt(dnum, vh) + dden
            damat = pmat * w_intra
            gmat = pmat * smat
            wdn = w_inter * dnum
            wdd = w_inter * dden
            dqh = _mm(damat, kh) + _mm_nt(wdn, c_h) + wdd * n_h
            dkh = _mm_tn(damat, qh)
            dvh = _mm_tn(smat, dnum)
            dw_inter = _rowsum(dnum * t["qc"]) + dden * t["qn"]
            dcn = dc_sc[h]
            dnn = dn_sc[h, 0:1, :]
            kw = kh * w_state
            dkw = _mm_nt(vh, dcn) + dnn
            dvh = dvh + _mm(kw, dcn)
            dkh = dkh + dkw * w_state
            dgst = _rowsum(dkw * kh) * w_state
            ddecay = _colsum(_rowsum(dcn * c_h)) + _rowsum(dnn * n_h)
            db_last = _colsum(dgst) + ddecay * decay
            rs_g = _rowsum(gmat)
            cs_g = _rowsum(gmat.T)
            db = rs_g - cs_g + dw_inter * w_inter - dgst + jnp.where(rowv == lc - 1, db_last, 0.0)
            dli = cs_g + dgst
            dc_sc[h] = decay * dcn + _mm_tn(qh, wdn)
            dn_sc[h] = _bcast8(decay * dnn + _colsum(qh * wdd))
            dq_ref[:, sl] = dqh
            dk_ref[:, sl] = dkh * (dh ** -0.5)
            dv_ref[:, sl] = dvh
            dg_acc = jnp.where(lane == h, dli, jnp.where(lane == 4 + h, db, dg_acc))
        dg_ref[...] = dg_acc

    rev = lambda c: nc // per - 1 - c
    tok = pl.BlockSpec((per * lc, d), lambda c: (rev(c), 0))
    g128 = pl.BlockSpec((per * lc, LANES), lambda c: (rev(c), 0))
    return _call(
        body, comms, name="mlstm_cell_bwd", grid=(nc // per,),
        in_specs=[tok, pl.BlockSpec((per * lc, d), lambda c: (rev(c), 3)), pl.BlockSpec((per * lc, d), lambda c: (rev(c), 4)),
                  tok, tok, tok, tok, g128, pl.BlockSpec((16, per * lc), lambda c: (0, rev(c))), g128,
                  pl.BlockSpec((per, nh, dh, dh), lambda c: (rev(c), 0, 0, 0)),
                  pl.BlockSpec((per, nh, SUBLANES, dh), lambda c: (rev(c), 0, 0, 0)),
                  pl.BlockSpec((per, nh, SUBLANES, LANES), lambda c: (rev(c), 0, 0, 0)), _full((1, d))],
        out_specs=[tok, tok, tok, g128, tok, tok, _full((SUBLANES, d))],
        out_shape=[jax.ShapeDtypeStruct((s, d), F32)] * 3 + [jax.ShapeDtypeStruct((s, LANES), F32)]
        + [jax.ShapeDtypeStruct((s, d), BF16)] * 2 + [jax.ShapeDtypeStruct((SUBLANES, d), F32)],
        scratch_shapes=[pltpu.VMEM((nh, dh, dh), F32), pltpu.VMEM((nh, SUBLANES, dh), F32)],
        args=(dy_ml, u, u, cell, q, k, v, gcol, grow, mt, cs, ns, ms, ng))


def _halo_spec(d, tile, nt, col):
    per = tile // SUBLANES
    return pl.BlockSpec((SUBLANES, d), lambda i: (jnp.maximum((nt - 1 - i) * per - 1, 0), col))


def _ml_pre_bwd(dq, dk, dv, dgates, gcol, u, pre, q, k, v, conv_w, w_q, w_k, w_v, wif_t, tile, comms=None):
    s, d = dq.shape
    nt = s // tile
    nh, dh, _ = w_q.shape

    def body(dq_ref, dk_ref, dv_ref, dg_ref, gc_ref, x_ref, pre_ref, q_ref, k_ref, v_ref, cw_ref,
             wq_ref, wk_ref, wv_ref, wift_ref,
             dx_ref, gwq_ref, gwk_ref, gwv_ref, gwif_ref, gbif_ref, gcw_ref, gcb_ref, next8):
        @pl.when(pl.program_id(0) == 0)
        def _():
            next8[...] = jnp.zeros_like(next8)
            for ref in (gwq_ref, gwk_ref, gwv_ref, gwif_ref, gbif_ref, gcw_ref, gcb_ref):
                ref[...] = jnp.zeros_like(ref)

        x = x_ref[...]
        pre = pre_ref[...]
        sg = _sigmoid(pre)
        xc = pre * sg
        dgc = dg_ref[...]
        lane = _iota(dgc.shape, 1)
        utri = jnp.where(_iota((ML_CHUNK, ML_CHUNK), 0) <= _iota((ML_CHUNK, ML_CHUNK), 1), 1.0, 0.0)
        rcs = [_mm_hi(utri, dgc[c * ML_CHUNK:(c + 1) * ML_CHUNK, :]) for c in range(tile // ML_CHUNK)]
        rc = rcs[0] if len(rcs) == 1 else jnp.concatenate(rcs, axis=0)
        dgates_v = jnp.where(lane < 4, dgc, jnp.where(lane < 8, rc * (1.0 - jnp.exp(gc_ref[...])), 0.0))
        dgb = dgates_v.astype(BF16)
        gbif_ref[...] += jnp.broadcast_to(_colsum(dgates_v), gbif_ref.shape)
        ext = jnp.dot(dgb, wift_ref[...], preferred_element_type=F32)
        dqt = dq_ref[...] + ext[:, 0:d]
        dkt = dk_ref[...] + ext[:, d:2 * d]
        dvt = dv_ref[...] + ext[:, 2 * d:3 * d]
        gwif_ref[:, 0:d] += _mm_tn(dgb, q_ref[...])
        gwif_ref[:, d:2 * d] += _mm_tn(dgb, k_ref[...])
        gwif_ref[:, 2 * d:3 * d] += _mm_tn(dgb, v_ref[...])
        dxc_parts, dxv_parts = [], []
        for h in range(nh):
            sl = slice(h * dh, (h + 1) * dh)
            gwq_ref[h] += _mm_tn(xc[:, sl], dqt[:, sl])
            gwk_ref[h] += _mm_tn(xc[:, sl], dkt[:, sl])
            gwv_ref[h] += _mm_tn(x[:, sl], dvt[:, sl])
            dxc_parts.append(_mm_nt(dqt[:, sl], wq_ref[h]) + _mm_nt(dkt[:, sl], wk_ref[h]))
            dxv_parts.append(_mm_nt(dvt[:, sl], wv_ref[h]))
        dxc = jnp.concatenate(dxc_parts, axis=1)
        dxv = jnp.concatenate(dxv_parts, axis=1)
        dpre = dxc * (sg + pre * sg * (1.0 - sg))
        gcb_ref[...] += _bcast8(_colsum(dpre))
        dx_ref[...] = (dxv + _conv_bwd(dpre, x, next8[...], cw_ref, gcw_ref)).astype(BF16)
        next8[...] = dpre[0:SUBLANES, :]

    rev = lambda i: nt - 1 - i
    tok = pl.BlockSpec((tile, d), lambda i: (rev(i), 0))
    g128 = pl.BlockSpec((tile, LANES), lambda i: (rev(i), 0))
    wsh = (nh, dh, dh)
    return _call(
        body, comms, name="mlstm_proj_bwd", grid=(nt,),
        in_specs=[tok, tok, tok, g128, g128, pl.BlockSpec((tile, d), lambda i: (rev(i), 2)), tok,
                  tok, tok, tok, _full(conv_w.shape), _full(wsh), _full(wsh), _full(wsh), _full(wif_t.shape)],
        out_specs=[tok, _full(wsh), _full(wsh), _full(wsh), _full((LANES, 3 * d)), _full((SUBLANES, LANES)),
                   _full((SUBLANES, d)), _full((SUBLANES, d))],
        out_shape=[jax.ShapeDtypeStruct((s, d), BF16)] + [jax.ShapeDtypeStruct(wsh, F32)] * 3
        + [jax.ShapeDtypeStruct((LANES, 3 * d), F32), jax.ShapeDtypeStruct((SUBLANES, LANES), F32),
           jax.ShapeDtypeStruct((SUBLANES, d), F32), jax.ShapeDtypeStruct((SUBLANES, d), F32)],
        scratch_shapes=[pltpu.VMEM((SUBLANES, d), F32)],
        args=(dq, dk, dv, dgates, gcol, u, pre, q, k, v, conv_w, w_q, w_k, w_v, wif_t))


def _rg_bwd(dy_rg, u, h_rg, gates, conv_w, w_a, w_x, lam, tile, comms=None):
    s, d = dy_rg.shape
    nt = s // tile
    nh, dh, _ = w_a.shape

    def body(dy_ref, x_ref, z_ref, h_ref, hhalo_ref, xc_ref, r_ref, i_ref, a_ref, beta_ref, cw_ref, wa_ref,
             wx_ref, lam_ref,
             dx_ref, dz_ref, gwa_ref, gwx_ref, gba_ref, gbx_ref, glam_ref, gcw_ref, gcb_ref, next8, anext, dnext, dbuf):
        i = pl.program_id(0)

        @pl.when(i == 0)
        def _():
            for ref in (next8, anext, dnext, gwa_ref, gwx_ref, gba_ref, gbx_ref, glam_ref, gcw_ref, gcb_ref):
                ref[...] = jnp.zeros_like(ref)

        inner = jnp.where(i < nt - 1, 1.0, 0.0)
        xc, r, ig, a, beta = xc_ref[...], r_ref[...], i_ref[...], a_ref[...], beta_ref[...]
        sp = _softplus(-lam_ref[...])
        h = h_ref[...]
        row = _iota(h.shape, 0)
        hprev = jnp.where(row >= 1, pltpu.roll(h, 1, 0), hhalo_ref[SUBLANES - 1:SUBLANES, :] * inner)
        z = z_ref[...]
        sz = _sigmoid(z)
        dyv = dy_ref[...]
        dz_ref[...] = (dyv * h * (sz + z * sz * (1.0 - sz))).astype(BF16)
        a_up = jnp.where(row < tile - 1, pltpu.roll(a, tile - 1, 0), anext[0:1, :])
        _scan_into(a_up, dyv * z * sz, dnext[0:1, :], dbuf, True)
        delta = dbuf[...]
        anext[...] = a[0:SUBLANES, :]
        dnext[...] = delta[0:SUBLANES, :]
        dla = delta * hprev * a - delta * ig * xc * (a * a / beta)
        glam_ref[...] += _bcast8(_colsum(dla * r) * (RG_C * _sigmoid(-lam_ref[...])))
        dpa = dla * (-RG_C * sp) * r * (1.0 - r)
        dpx = delta * beta * xc * ig * (1.0 - ig)
        gba_ref[...] += _bcast8(_colsum(dpa))
        gbx_ref[...] += _bcast8(_colsum(dpx))
        parts = []
        for hh in range(nh):
            sl = slice(hh * dh, (hh + 1) * dh)
            gwa_ref[hh] += _mm_tn(xc[:, sl], dpa[:, sl])
            gwx_ref[hh] += _mm_tn(xc[:, sl], dpx[:, sl])
            parts.append(_mm_nt(dpa[:, sl], wa_ref[hh]) + _mm_nt(dpx[:, sl], wx_ref[hh]))
        dxc = delta * beta * ig + jnp.concatenate(parts, axis=1)
        gcb_ref[...] += _bcast8(_colsum(dxc))
        dx_ref[...] = _conv_bwd(dxc, x_ref[...], next8[...], cw_ref, gcw_ref).astype(BF16)
        next8[...] = dxc[0:SUBLANES, :]

    rev = lambda i: nt - 1 - i
    tok = pl.BlockSpec((tile, d), lambda i: (rev(i), 0))
    vec = _full((1, d))
    acc = _full((SUBLANES, d))
    wsh = (nh, dh, dh)
    return _call(
        body, comms, name="rglru_bwd", grid=(nt,),
        in_specs=[tok, tok, pl.BlockSpec((tile, d), lambda i: (rev(i), 1)), tok,
                  _halo_spec(d, tile, nt, 0)] + [tok] * 5 + [_full(conv_w.shape), _full(wsh), _full(wsh), vec],
        out_specs=[tok, tok, _full(wsh), _full(wsh), acc, acc, acc, acc, acc],
        out_shape=[jax.ShapeDtypeStruct((s, d), BF16)] * 2 + [jax.ShapeDtypeStruct(wsh, F32)] * 2
        + [jax.ShapeDtypeStruct((SUBLANES, d), F32)] * 5,
        scratch_shapes=[pltpu.VMEM((SUBLANES, d), F32)] * 3 + [pltpu.VMEM((tile, d), F32)],
        args=(dy_rg, u, u, h_rg, h_rg, *gates, conv_w, w_a, w_x, lam))


def _segments(d, w, n_pieces, n_slots):
    bounds = sorted({k * d for k in range(n_pieces + 1)} | {j * w for j in range(n_slots + 1)})
    return [(lo // d, lo % d, lo // w, lo % w, hi - lo) for lo, hi in zip(bounds[:-1], bounds[1:])]


def _in_bwd(pieces, x, dxo, ng, scale, w_in_g, layer, tile, comms=None, tiles=None, prev=None):
    s, d = x.shape
    nd, _, _, w = w_in_g.shape
    first, count = tiles or (0, s // tile)
    n_p = len(pieces)

    def body(*refs):
        p_refs = refs[:n_p]
        x_ref, dxo_ref, ng_ref, sc_ref, w_ref = refs[n_p:n_p + 5]
        dx_ref, dsc_ref, dsh_ref, gng_ref, wcat = refs[-5:]
        _join_columns(w_ref, wcat)

        @pl.when(pl.program_id(0) == 0)
        def _():
            for k, ref in enumerate((dsc_ref, dsh_ref, gng_ref)):
                ref[...] = jnp.zeros_like(ref) if prev is None else refs[n_p + 6 + k][...]

        du = jnp.concatenate([p[...] for p in p_refs], axis=1)
        dh = lax.dot_general(du, wcat[...], (((1,), (1,)), ((), ())), preferred_element_type=F32)
        xv = x_ref[...]
        g = ng_ref[...]
        rs = lax.rsqrt(jnp.mean(xv * xv, axis=1, keepdims=True) + EPS)
        xh = xv * rs
        dsh_ref[...] += _bcast8(_colsum(dh))
        dsc_ref[...] += _bcast8(_colsum(dh * xh * g))
        dhn = dh * (1.0 + sc_ref[...])
        gng_ref[...] += _bcast8(_colsum(dhn * xh))
        dxh = dhn * g
        dx_ref[...] = dxo_ref[...] + rs * (dxh - xh * jnp.mean(dxh * xh, axis=1, keepdims=True))

    tok = pl.BlockSpec((tile, d), lambda i: (i + first, 0))
    vec = _full((1, d))
    acc = _full((SUBLANES, d))
    more_specs = [] if prev is None else [pl.BlockSpec(memory_space=pl.ANY), acc, acc, acc]
    return _call(
        body, comms, name="in_proj_bwd_x", grid=(count,),
        in_specs=[tok] * n_p + [tok, tok, vec, vec, pl.BlockSpec((nd, 1, d, w), lambda i: (0, layer, 0, 0),
                                                               pipeline_mode=pl.Buffered(1))] + more_specs,
        out_specs=[tok, acc, acc, acc],
        out_shape=[jax.ShapeDtypeStruct((s, d), F32)] + [jax.ShapeDtypeStruct((SUBLANES, d), F32)] * 3,
        scratch_shapes=[pltpu.VMEM((d, nd * w), BF16)],
        args=(*pieces, x, dxo, ng, scale, w_in_g) + (() if prev is None else tuple(prev)),
        aliases={} if prev is None else {n_p + 5: 0})


def _in_bwd_w(pieces, hbf, w, slots, tile, comms=None):
    s, d = hbf.shape
    nd_all = len(pieces) * d // w
    segs = [sg for sg in _segments(d, w, len(pieces), nd_all) if sg[2] in slots]

    def body(*refs):
        p_refs = refs[:len(pieces)]
        h_ref, gw_ref = refs[len(pieces):]

        @pl.when(pl.program_id(0) == 0)
        def _():
            gw_ref[...] = jnp.zeros_like(gw_ref)

        hv = h_ref[...]
        for (kk, a, j, b, width) in segs:
            gw_ref[j - slots[0], :, b:b + width] += _mm_tn(hv, p_refs[kk][:, a:a + width])

    tok = pl.BlockSpec((tile, d), lambda i: (i, 0))
    return _call(
        body, comms, name="in_proj_bwd_w", grid=(s // tile,),
        in_specs=[tok] * len(pieces) + [tok],
        out_specs=[pl.BlockSpec((len(slots), d, w), lambda i: (0, 0, 0), pipeline_mode=pl.Buffered(1))],
        out_shape=[jax.ShapeDtypeStruct((len(slots), d, w), F32)],
        args=(*pieces, hbf))[0]


def _exchange(arrs, gather, name):
    return _run_comms([_exchange_comm(arrs, gather)], name)[0]


def _run_comms(comms, name):
    _call(lambda: None, comms, name=name, grid=(1,), in_specs=[], out_specs=[], out_shape=[], args=())
    return [cm.results for cm in comms]


def _exchange_comm(arrs, gather):
    n = len(arrs)
    per = N_DEV - 1

    def copies(ins, outs, sems):
        send_sems, recv_sems, local_sems = sems
        x, y, c = (lax.axis_index(ax) for ax in MESH_AXES)
        me = 4 * x + 2 * y + c
        sends, recvs = [], []
        for flip in range(1, N_DEV):
            px = x ^ ((flip >> 2) & 1)
            py = y ^ ((flip >> 1) & 1)
            pc = c ^ (flip & 1)
            peer = 4 * px + 2 * py + pc
            for kk in range(n):
                src = ins[kk] if gather else ins[kk].at[peer]
                sends.append(_remote(src, outs[kk].at[me], send_sems, recv_sems, kk * per + flip - 1, (px, py, pc)))
                recvs.append(_remote(src, outs[kk].at[peer], send_sems, recv_sems, kk * per + flip - 1, (px, py, pc)))
        local = [pltpu.make_async_copy(ins[kk] if gather else ins[kk].at[me], outs[kk].at[me], local_sems.at[kk])
                 for kk in range(n)]
        return local, sends, recvs

    def start(ins, outs, sems):
        local, sends, _ = copies(ins, outs, sems)
        for cp in sends + local:
            cp.start()

    def finish(ins, outs, sems):
        local, sends, recvs = copies(ins, outs, sems)
        for cp in recvs:
            cp.wait_recv()
        for cp in sends:
            cp.wait_send()
        for cp in local:
            cp.wait()

    return _Comm(arrs, [jax.ShapeDtypeStruct((N_DEV,) + a.shape if gather else a.shape, a.dtype) for a in arrs],
                 [pltpu.SemaphoreType.DMA((n * per,)), pltpu.SemaphoreType.DMA((n * per,)), pltpu.SemaphoreType.DMA((n,))],
                 start, finish)


def _mesh_place():
    x, y, c = (lax.axis_index(ax) for ax in MESH_AXES)
    return x, y, c, (x, y, 1 - c), [(1 - x, y), (x, 1 - y), (1 - x, 1 - y)]


def _remote(src, dst, send_sems, recv_sems, sem, to):
    return pltpu.make_async_remote_copy(src_ref=src, dst_ref=dst, send_sem=send_sems.at[sem], recv_sem=recv_sems.at[sem],
                                        device_id=to, device_id_type=pl.DeviceIdType.MESH)


N_CHIPS = N_DEV // 2


def _pair_sum(a, other, parity, name):
    _, r, c = a.shape
    tr = _row_tile(r, c, 3)

    def body(p_ref, a_ref, o_ref, s_ref):
        s_ref[...] = (a_ref[...] + o_ref[...]).astype(BF16)

    return pl.pallas_call(
        body, name=name,
        grid_spec=pltpu.PrefetchScalarGridSpec(
            num_scalar_prefetch=1, grid=(N_CHIPS, r // tr),
            in_specs=[pl.BlockSpec((1, tr, c), lambda q, i, p: (2 * q + p[0], i, 0)),
                      pl.BlockSpec((1, tr, c), lambda q, i, p: (q, i, 0))],
            out_specs=pl.BlockSpec((1, tr, c), lambda q, i, p: (q, i, 0))),
        out_shape=jax.ShapeDtypeStruct((N_CHIPS, r, c), BF16),
        compiler_params=_params(2),
    )(parity, a, other)


def _adam_math(w, g, m, v):
    m = ADAM_B1 * m + (1.0 - ADAM_B1) * g
    v = ADAM_B2 * v + (1.0 - ADAM_B2) * (g * g)
    m_hat = m / (1.0 - ADAM_B1 ** ADAM_STEP)
    v_hat = v / (1.0 - ADAM_B2 ** ADAM_STEP)
    delta = -ADAM_LR * (m_hat / (jnp.sqrt(v_hat) + ADAM_EPS) + ADAM_WD * w)
    return delta, m, v


def _sum_devices(r_ref):
    acc = r_ref[0].astype(F32)
    for p in range(1, r_ref.shape[0]):
        acc = acc + r_ref[p].astype(F32)
    return acc


def _row_tile(rows, cols, n_bufs):
    budget = 24 * 1024 * 1024 // (n_bufs * 2 * cols * 4)
    t = rows
    while t > budget and t % 2 == 0 and (t // 2) % SUBLANES == 0:
        t //= 2
    return t


def _reduce_adam(recvs, w, m, v, name, comms=None):
    nl, r, c = w.shape
    n_part = recvs[0].shape[0]
    tr = _row_tile(r, c, n_part * nl + 7)
    nt = r // tr

    def body(*refs):
        r_refs = refs[:nl]
        w_ref, m_ref, v_ref, g_ref, d_ref, mo_ref, vo_ref = refs[nl:]
        layer = pl.program_id(0) // nt
        g = _sum_devices(r_refs[0])
        for ll in range(1, nl):
            g = jnp.where(layer == ll, _sum_devices(r_refs[ll]), g)
        delta, m2, v2 = _adam_math(w_ref[0], g, m_ref[0], v_ref[0])
        g_ref[0] = g
        d_ref[0] = delta
        mo_ref[0] = m2
        vo_ref[0] = v2

    def rspec(ll):
        return pl.BlockSpec((n_part, tr, c),
                            lambda i: (0, jnp.where(i // nt == ll, i % nt, jnp.where(i // nt < ll, 0, nt - 1)), 0))

    blk = pl.BlockSpec((1, tr, c), lambda i: (i // nt, i % nt, 0))
    return _call(
        body, comms, name=name, grid=(nl * nt,),
        in_specs=[rspec(ll) for ll in range(nl)] + [blk, blk, blk],
        out_specs=[blk] * 4,
        out_shape=[jax.ShapeDtypeStruct((nl, r, c), F32)] * 4,
        args=(*recvs, w, m, v))


def _tile_for(s, want):
    return min(want, s)


REPLICATED = ("norm_g", "b_ada", "rg_conv_b", "rg_w_a", "rg_b_a", "rg_w_x", "rg_b_x", "rg_lambda", "ml_conv_b",
              "ml_b_if", "ml_norm_g", "final_g")


def _small_pack(rg_conv_w, ml_conv_w, ml_w_if):
    nl = rg_conv_w.shape[0]
    wif_t = jnp.swapaxes(ml_w_if, 1, 2).reshape(nl, -1, LANES)
    return jnp.concatenate([rg_conv_w, ml_conv_w, wif_t], axis=1)


def _small_unpack(p, if_rows):
    nl = p.shape[0]
    rg_cw = p[:, 0:CONV_WIDTH]
    ml_cw = p[:, CONV_WIDTH:2 * CONV_WIDTH]
    wif = jnp.swapaxes(p[:, 2 * CONV_WIDTH:].reshape(nl, 8, if_rows), 1, 2)
    return rg_cw, ml_cw, wif


def _qkv_slots(g_qkv, nd):
    three, nh, dh, _ = g_qkv.shape
    return g_qkv.reshape(three, nh, nd, dh // nd, dh).transpose(2, 0, 1, 3, 4).reshape(nd, three * nh * (dh // nd), dh)


def _small_slots(g):
    nd = N_DEV
    cw = jnp.stack([g["rg_conv_w"], g["ml_conv_w"]]).reshape(2, CONV_WIDTH, nd, LANES).transpose(2, 0, 1, 3)
    cw = cw.reshape(nd, 2 * CONV_WIDTH, LANES)
    wif = g["wif_t"].reshape(8, nd, -1).transpose(1, 0, 2).reshape(nd, -1, LANES)
    return jnp.concatenate([cw, wif], axis=1)


def _slot(block):
    return 4 * block[0] + 2 * block[1] + block[2]


def _dma_sems(*counts):
    return [pltpu.SemaphoreType.DMA((n,)) for n in counts]


def _start_all(copies):
    for cp in copies:
        cp.start()


def _gather_ici_comm(arrs):
    n = len(arrs)

    def copies(ins, outs, sems):
        send_sems, recv_sems, local_sems = sems
        x, y, c, sibling, chips = _mesh_place()
        me = (x, y, c)
        peers = [(*chip, c) for chip in chips] + [sibling]
        local = [pltpu.make_async_copy(ins[kk], outs[kk].at[_slot(me)], local_sems.at[kk]) for kk in range(n)]
        sends = [_remote(ins[kk], outs[kk].at[_slot(me)], send_sems, recv_sems, kk * 4 + j, peer)
                 for j, peer in enumerate(peers) for kk in range(n)]
        recvs = [_remote(ins[kk], outs[kk].at[_slot(peer)], send_sems, recv_sems, kk * 4 + j, peer)
                 for j, peer in enumerate(peers) for kk in range(n)]
        return local, sends, recvs

    def start(ins, outs, sems):
        local, sends, _ = copies(ins, outs, sems)
        _start_all(sends + local)

    def finish(ins, outs, sems):
        local, sends, recvs = copies(ins, outs, sems)
        for cp in recvs:
            cp.wait_recv()
        for cp in sends:
            cp.wait_send()
        for cp in local:
            cp.wait()

    return _Comm(arrs, [jax.ShapeDtypeStruct((N_DEV,) + a.shape, a.dtype) for a in arrs], _dma_sems(4 * n, 4 * n, n),
                 start, finish)


def _gather_fwd_comm(bufs):
    n = len(bufs)

    def copies(ins, outs, sems):
        send_sems, recv_sems = sems
        _, _, c, sibling, chips = _mesh_place()
        sends = [_remote(ins[kk].at[_slot((*chip, c))], outs[kk].at[_slot((*chip, c))], send_sems, recv_sems, kk * 3 + j, sibling)
                 for j, chip in enumerate(chips) for kk in range(n)]
        recvs = [_remote(ins[kk].at[_slot((*chip, c))], outs[kk].at[_slot((*chip, 1 - c))], send_sems, recv_sems, kk * 3 + j, sibling)
                 for j, chip in enumerate(chips) for kk in range(n)]
        return sends, recvs

    def start(ins, outs, sems):
        _start_all(copies(ins, outs, sems)[0])

    def finish(ins, outs, sems):
        sends, recvs = copies(ins, outs, sems)
        for cp in recvs:
            cp.wait_recv()
        for cp in sends:
            cp.wait_send()

    return _Comm(bufs, [jax.ShapeDtypeStruct(a.shape, a.dtype) for a in bufs], _dma_sems(3 * n, 3 * n), start, finish,
                 aliases=[(i, i) for i in range(n)])


def _core_swap_comm(arrs):
    n = len(arrs)

    def copies(ins, outs, sems):
        send_sems, recv_sems = sems
        _, _, c, sibling, _ = _mesh_place()
        return [_remote(ins[kk].at[2 * q + (1 - c)], outs[kk].at[q], send_sems, recv_sems, kk * N_CHIPS + q, sibling)
                for q in range(N_CHIPS) for kk in range(n)]

    def start(ins, outs, sems):
        _start_all(copies(ins, outs, sems))

    def finish(ins, outs, sems):
        cps = copies(ins, outs, sems)
        for cp in cps:
            cp.wait_recv()
        for cp in cps:
            cp.wait_send()

    return _Comm(arrs, [jax.ShapeDtypeStruct((N_CHIPS,) + a.shape[1:], a.dtype) for a in arrs],
                 _dma_sems(N_CHIPS * n, N_CHIPS * n), start, finish)


def _chip_swap_comm(arrs):
    n = len(arrs)
    per = N_CHIPS - 1

    def copies(ins, outs, sems):
        send_sems, recv_sems, local_sems = sems
        x, y, c, _, chips = _mesh_place()
        mine = 2 * x + y
        sends = [_remote(ins[kk].at[2 * chip[0] + chip[1]], outs[kk].at[mine], send_sems, recv_sems, kk * per + j, (*chip, c))
                 for j, chip in enumerate(chips) for kk in range(n)]
        recvs = [_remote(ins[kk].at[mine], outs[kk].at[2 * chip[0] + chip[1]], send_sems, recv_sems, kk * per + j, (*chip, c))
                 for j, chip in enumerate(chips) for kk in range(n)]
        local = [pltpu.make_async_copy(ins[kk].at[mine], outs[kk].at[mine], local_sems.at[kk]) for kk in range(n)]
        return local, sends, recvs

    def start(ins, outs, sems):
        local, sends, _ = copies(ins, outs, sems)
        _start_all(sends + local)

    def finish(ins, outs, sems):
        local, sends, recvs = copies(ins, outs, sems)
        for cp in recvs:
            cp.wait_recv()
        for cp in sends:
            cp.wait_send()
        for cp in local:
            cp.wait()

    return _Comm(arrs, [jax.ShapeDtypeStruct(a.shape, a.dtype) for a in arrs], _dma_sems(per * n, per * n, n), start, finish)


def _ada_mod(c_all, w_ada, b_cols, comms=None):
    nl, d, w = w_ada.shape

    def body(c_ref, w_ref, b_ref, m_ref, ca_ref):
        sub = _iota((SUBLANES, d), 0)
        cv = jnp.zeros((SUBLANES, d), F32)
        for b in range(N_DEV):
            cv = jnp.where(sub == b, c_ref[b], cv)
        ca = cv * _sigmoid(cv)
        ca_ref[...] = ca
        m_ref[...] = jnp.zeros_like(m_ref)
        for l in range(nl):
            ml = _mm_hi(ca, w_ref[l]) + b_ref[l:l + 1, :]
            for b in range(N_DEV):
                m_ref[b, l:l + 1, :] = _row(ml, b)

    return _call(
        body, comms, name="adaln_mod_columns", grid=(1,),
        in_specs=[_full(c_all.shape), _full(w_ada.shape), _full(b_cols.shape)],
        out_specs=[_full((N_DEV, SUBLANES, w)), _full((SUBLANES, d))],
        out_shape=[jax.ShapeDtypeStruct((N_DEV, SUBLANES, w), F32), jax.ShapeDtypeStruct((SUBLANES, d), F32)],
        args=(c_all, w_ada, b_cols))


def _ada_grad_adam(cact_t, dmods, w, m, v, comms=None):
    nl, d, wd = w.shape
    tr = _row_tile(d, wd, 8)
    nt = d // tr

    def body(c_ref, dm_ref, w_ref, m_ref, v_ref, g_ref, d_ref, mo_ref, vo_ref):
        cv = c_ref[...]
        dm = dm_ref[0]
        g = _col(cv, 0) * _row(dm, 0)
        for b in range(1, N_DEV):
            g = g + _col(cv, b) * _row(dm, b)
        delta, m2, v2 = _adam_math(w_ref[0], g, m_ref[0], v_ref[0])
        g_ref[0] = g
        d_ref[0] = delta
        mo_ref[0] = m2
        vo_ref[0] = v2

    blk = pl.BlockSpec((1, tr, wd), lambda i: (i // nt, i % nt, 0))
    return _call(
        body, comms, name="adaln_grad_adam", grid=(nl * nt,),
        in_specs=[pl.BlockSpec((tr, N_DEV), lambda i: (i % nt, 0)), pl.BlockSpec((1, N_DEV, wd), lambda i: (i // nt, 0, 0)),
                  blk, blk, blk],
        out_specs=[blk] * 4, out_shape=[jax.ShapeDtypeStruct((nl, d, wd), F32)] * 4,
        args=(cact_t, dmods, w, m, v))


REP_ROWS = ("norm_g", "dshift", "dscale", "dgate", "rg_conv_b", "rg_b_a", "rg_b_x", "rg_lambda", "ml_conv_b", "ml_norm_g",
            "ml_b_if")


def _sum_parts(recvs, name):
    def body(*refs):
        for r_ref, o_ref in zip(refs[:len(recvs)], refs[len(recvs):]):
            o_ref[...] = _sum_devices(r_ref).astype(o_ref.dtype)

    return pl.pallas_call(
        body, name=name, grid=(1,),
        in_specs=[_full(r.shape) for r in recvs], out_specs=[_full(r.shape[1:]) for r in recvs],
        out_shape=[jax.ShapeDtypeStruct(r.shape[1:], r.dtype) for r in recvs], compiler_params=_params(1),
    )(*recvs)


def _adam_replicated(vp, mp, params, nl):
    d = vp.shape[2]
    nr = len(REP_ROWS)
    names = list(params)
    mat_shape = params["rg_w_a"][0].shape[1:]
    mat_rows = mp.shape[0] // (2 * nl)

    def pieces(name):
        if name == "final_g":
            return [(lambda vp_ref, mp_ref: vp_ref[nl * nr:nl * nr + 1, :], (slice(0, 1), slice(None)))]
        out = []
        for l in range(nl):
            if name in ("rg_w_a", "rg_w_x"):
                at = (2 * l + (name == "rg_w_x")) * mat_rows
                out.append((lambda vp_ref, mp_ref, at=at: mp_ref[at:at + mat_rows, :].astype(F32).reshape(mat_shape), l))
            elif name == "b_ada":
                for j in range(3):
                    r = l * nr + 1 + j
                    out.append((lambda vp_ref, mp_ref, r=r: vp_ref[r:r + 1, :], (slice(l, l + 1), slice(j * d, (j + 1) * d))))
            else:
                r = l * nr + REP_ROWS.index(name)
                cols = slice(0, LANES) if name == "ml_b_if" else slice(None)
                out.append((lambda vp_ref, mp_ref, r=r, cols=cols: vp_ref[r:r + 1, cols], (slice(l, l + 1), slice(None))))
        return out

    def body(*refs):
        parts_ref, mp_ref, vp_ref = refs[0], refs[1], refs[-1]
        ins, outs = refs[2:2 + 3 * len(names)], refs[2 + 3 * len(names):-1]
        vp_ref[...] = _sum_devices(parts_ref)
        for pi, name in enumerate(names):
            w_ref, m_ref, v_ref = ins[3 * pi:3 * pi + 3]
            g_ref, d_ref, mo_ref, vo_ref = outs[4 * pi:4 * pi + 4]
            for get, idx in pieces(name):
                g = get(vp_ref, mp_ref)
                delta, m2, v2 = _adam_math(w_ref[idx], g, m_ref[idx], v_ref[idx])
                g_ref[idx] = g
                d_ref[idx] = delta
                mo_ref[idx] = m2
                vo_ref[idx] = v2

    flat = [a for name in names for a in params[name]]
    out_shape = [jax.ShapeDtypeStruct(params[name][0].shape, F32) for name in names for _ in range(4)]
    out_shape.append(jax.ShapeDtypeStruct(vp.shape[1:], F32))
    res = pl.pallas_call(
        body, name="adam_replicated", grid=(1,),
        in_specs=[_full(vp.shape), _full(mp.shape)] + [_full(a.shape) for a in flat],
        out_specs=[_full(o.shape) for o in out_shape], out_shape=out_shape, compiler_params=_params(1),
    )(vp, mp, *flat)
    return {name: res[4 * pi:4 * pi + 4] for pi, name in enumerate(names)}, res[-1]


class _Plan:
    def __init__(self):
        self.hosted, self.after = {}, {}

    def host(self, key, comm, then=None):
        self.hosted.setdefault(key, []).append(comm)
        if then is not None:
            self.after.setdefault(key, []).append(then)

    def comms(self, key):
        return self.hosted.pop(key, None)

    def done(self, key):
        for fn in self.after.pop(key, []):
            fn()

    def flush(self):
        while self.hosted:
            key = next(iter(self.hosted))
            _call(lambda: None, self.comms(key), name="exchange_after_%s_%d" % key, grid=(1,), in_specs=[], out_specs=[],
                  out_shape=[], args=())
            self.done(key)


VEC_TABLE = ("norm_g", "rg_conv_b", "rg_b_a", "rg_b_x", "rg_lambda", "ml_conv_b", "ml_norm_g")


def _vec_table(rep):
    rows = [rep[n] for n in VEC_TABLE]
    return jnp.stack(rows + [jnp.zeros_like(rows[0])] * (SUBLANES - len(rows)), axis=1)


def _layer_fwd(l, xl, mod3, wl, rep, plan, head=None):
    s, d = xl.shape
    t_big, t_mid = _tile_for(s, 512), _tile_for(s, 256)
    nh_ml = rep["ml_b_if"].shape[1] // 2
    vec = lambda name: _vec(rep["vecs"], l, VEC_TABLE.index(name))
    shift, scale, gate = (_vec(mod3, l, kk) for kk in range(3))
    hosted = lambda name: plan.comms((name, l)) if plan else None
    done = lambda name: plan.done((name, l)) if plan else None
    u, hbf = _in_fwd(xl, vec("norm_g"), scale, shift, wl["w_in_g"], 0, t_big, hosted("in_proj_fwd"))
    done("in_proj_fwd")
    h_rg, y_rg, *rg_gates = _rg_fwd(u, d, wl["rg_conv_w"], vec("rg_conv_b"), rep["rg_w_a_bf"][l], vec("rg_b_a"),
                                    rep["rg_w_x_bf"][l], vec("rg_b_x"), vec("rg_lambda"), t_mid, hosted("rglru_fwd"))
    done("rglru_fwd")
    q, k, v, gcol, pre = _ml_pre(u, d, wl["ml_conv_w"], vec("ml_conv_b"), wl["w_qkv"][0], wl["w_qkv"][1],
                                 wl["w_qkv"][2], wl["wif_pad"], wl["bif_pad"], t_mid, hosted("mlstm_proj_fwd"))
    done("mlstm_proj_fwd")
    grow = gcol[:, 0:16].T
    cell, y_ml, cs, ns, ms, mt = _ml_cell_fwd(q, k, v, gcol, grow, u, vec("ml_norm_g"), nh_ml, hosted("mlstm_cell_fwd"))
    done("mlstm_cell_fwd")
    res = _out_fwd(xl, y_rg, y_ml, gate, wl["w_out_g"], 0, t_big, hosted("out_proj_fwd"), head)
    done("out_proj_fwd")
    x_new, y = (res[0], res[1]) if head is None else (tuple(res[1:]), res[0])
    saved = dict(x=xl, u=u, hbf=hbf, h_rg=h_rg, y_rg=y_rg, q=q, k=k, v=v, gcol=gcol, grow=grow, cell=cell, y_ml=y_ml,
                 cs=cs, ns=ns, ms=ms, mt=mt, y=y, scale=scale, gate=gate, rg_gates=rg_gates, pre=pre)
    return x_new, saved


def _layer_bwd(l, dx, sv, wl, rep, plan, grads=None, split_last=False):
    s, d = dx.shape
    t_big, t_mid = _tile_for(s, 512), _tile_for(s, 256)
    nh_ml = rep["ml_b_if"].shape[1] // 2
    nd, _, _, w_cols = wl["w_in_g"].shape
    grads = {} if grads is None else grads
    vec = lambda name: _vec(rep["vecs"], l, VEC_TABLE.index(name))
    hosted = lambda name: plan.comms((name, l)) if plan else None
    done = lambda name: plan.done((name, l)) if plan else None
    dy_rg, dy_ml, gw_out, dgate = _out_bwd(dx, sv["gate"], sv["y"], sv["y_rg"], sv["y_ml"], wl["w_out_g"], 0, t_big,
                                           hosted("out_proj_bwd"))
    grads.update(w_out=gw_out)
    done("out_proj_bwd")
    dq, dk, dv, dgates, d_mlo, d_mlz, g_mlng = _ml_cell_bwd(
        dy_ml, sv["u"], sv["cell"], sv["q"], sv["k"], sv["v"], sv["gcol"], sv["grow"], sv["mt"], sv["cs"], sv["ns"],
        sv["ms"], vec("ml_norm_g"), nh_ml, hosted("mlstm_cell_bwd"))
    done("mlstm_cell_bwd")
    d_mlx, g_wq, g_wk, g_wv, g_wift, g_bif, g_mlcw, g_mlcb = _ml_pre_bwd(
        dq, dk, dv, dgates, sv["gcol"], sv["u"], sv["pre"], sv["q"], sv["k"], sv["v"], wl["ml_conv_w"],
        wl["w_qkv"][0], wl["w_qkv"][1], wl["w_qkv"][2], wl["wift_pad"], t_mid, hosted("mlstm_proj_bwd"))
    done("mlstm_proj_bwd")
    d_rgx, d_rgz, g_wa, g_wx, g_ba, g_bx, g_lam, g_rgcw, g_rgcb = _rg_bwd(
        dy_rg, sv["u"], sv["h_rg"], sv["rg_gates"], wl["rg_conv_w"], rep["rg_w_a_bf"][l], rep["rg_w_x_bf"][l],
        vec("rg_lambda"), t_mid, hosted("rglru_bwd"))
    grads.update(w_qkv=jnp.stack([g_wq, g_wk, g_wv]), rg_conv_w=g_rgcw[0:CONV_WIDTH], ml_conv_w=g_mlcw[0:CONV_WIDTH],
                 wif_t=g_wift[0:8], rg_w_a=g_wa, rg_w_x=g_wx)
    acc = dict(dgate=dgate, rg_conv_b=g_rgcb, rg_b_a=g_ba, rg_b_x=g_bx, rg_lambda=g_lam, ml_conv_b=g_mlcb,
               ml_b_if=g_bif, ml_norm_g=g_mlng)
    done("rglru_bwd")
    pieces = [d_rgx, d_rgz, d_mlx, d_mlo, d_mlz]
    grads.update(w_in=_in_bwd_w(pieces, sv["hbf"], w_cols, tuple(range(nd)), _tile_for(s, 1024), hosted("in_proj_bwd_w")))
    done("in_proj_bwd_w")
    n_tiles = s // t_mid
    counts = [n_tiles // 5, n_tiles - n_tiles // 5 - 1, 1] if split_last and n_tiles >= 5 else [n_tiles]
    in_args = (pieces, sv["x"], dx, vec("norm_g"), sv["scale"], wl["w_in_g"], 0, t_mid)
    res, at = None, 0
    for key, count in zip(("in_proj_bwd_x", "in_proj_bwd_x_rest", "in_proj_bwd_x_end"), counts):
        res = _in_bwd(*in_args, hosted(key), (at, count), res)
        done(key)
        at += count
    dx, dscale, dshift, g_ng = res
    acc.update(norm_g=g_ng, dshift=dshift, dscale=dscale)
    grads.update(acc=acc, dmod=jnp.concatenate([dshift[0:1], dscale[0:1], dgate[0:1]], axis=1))
    return dx, grads


def _full_qkv(qkv_g, d):
    nd, _, rows3, dh = qkv_g.shape
    nh = d // dh
    rsh = rows3 // (3 * nh)
    return qkv_g.reshape(nd, 3, nh, rsh, dh).transpose(1, 2, 0, 3, 4).reshape(3, nh, nd * rsh, dh)


def _small_weights(small, l, ml_b_if):
    nd = small.shape[0]
    sm = small[:, l]
    cw = sm[:, 0:2 * CONV_WIDTH].reshape(nd, 2, CONV_WIDTH, LANES).transpose(1, 2, 0, 3).reshape(2, CONV_WIDTH, nd * LANES)
    if_rows = (sm.shape[1] - 2 * CONV_WIDTH) * LANES // 8
    wif_t = sm[:, 2 * CONV_WIDTH:].reshape(nd, 8, if_rows).transpose(1, 0, 2).reshape(8, nd * if_rows)
    wift_pad = jnp.pad(wif_t, ((0, LANES - 8), (0, 0))).astype(BF16)
    return dict(rg_conv_w=cw[0], ml_conv_w=cw[1], wift_pad=wift_pad, wif_pad=wift_pad.T,
                bif_pad=jnp.pad(ml_b_if[l], (0, LANES - 8)).reshape(1, LANES))


def kernel(x, c, norm_g, w_ada, b_ada, w_in, rg_conv_w, rg_conv_b, rg_w_a, rg_b_a, rg_w_x, rg_b_x, rg_lambda, ml_conv_w, ml_conv_b, ml_w_q, ml_w_k, ml_w_v, ml_w_if, ml_b_if, ml_norm_g, w_out, final_g, loss_target, m_norm_g, m_w_ada, m_b_ada, m_w_in, m_rg_conv_w, m_rg_conv_b, m_rg_w_a, m_rg_b_a, m_rg_w_x, m_rg_b_x, m_rg_lambda, m_ml_conv_w, m_ml_conv_b, m_ml_w_q, m_ml_w_k, m_ml_w_v, m_ml_w_if, m_ml_b_if, m_ml_norm_g, m_w_out, m_final_g, v_norm_g, v_w_ada, v_b_ada, v_w_in, v_rg_conv_w, v_rg_conv_b, v_rg_w_a, v_rg_b_a, v_rg_w_x, v_rg_b_x, v_rg_lambda, v_ml_conv_w, v_ml_conv_b, v_ml_w_q, v_ml_w_k, v_ml_w_v, v_ml_w_if, v_ml_b_if, v_ml_norm_g, v_w_out, v_final_g):
    given = dict(locals())
    nl = w_in.shape[0]
    d = x.shape[2]
    rep = {n: given[n] for n in REPLICATED}
    rep.update(rg_w_a_bf=rg_w_a.astype(BF16), rg_w_x_bf=rg_w_x.astype(BF16))
    bf = lambda a: a.astype(BF16)

    def qkv_shard(prefix):
        return jnp.stack([given[prefix + "ml_w_q"], given[prefix + "ml_w_k"], given[prefix + "ml_w_v"]], axis=1).reshape(
            nl, -1, ml_w_q.shape[-1])

    def small_shard(prefix):
        return _small_pack(given[prefix + "rg_conv_w"], given[prefix + "ml_conv_w"], given[prefix + "ml_w_if"])

    plan = _Plan()
    qkv = qkv_shard("")
    first_ici = _gather_ici_comm([bf(w_in[0:1]), small_shard("")])
    condition = _exchange_comm([jnp.broadcast_to(c, (SUBLANES, d))], True)
    _run_comms([first_ici, condition], "gather_first")
    first_fwd = _gather_fwd_comm(first_ici.results)
    wcols = w_ada.shape[2]
    me = 4 * lax.axis_index("x") + 2 * lax.axis_index("y") + lax.axis_index("c")
    b_cols = jnp.pad(lax.dynamic_slice_in_dim(b_ada, me * wcols, wcols, axis=1), ((0, SUBLANES - nl), (0, 0)))
    mod_cols, cact_all = _ada_mod(condition.results[0], w_ada, b_cols, [first_fwd])
    w_in_first, small = first_fwd.results
    wl = [_small_weights(small, l, ml_b_if) for l in range(nl)]
    wl[0]["w_in_g"] = w_in_first

    def gather_behind(arrs, ici_host, fwd_host, then):
        ici = _gather_ici_comm(arrs)

        def pass_on():
            fwd = _gather_fwd_comm(ici.results)
            plan.host(fwd_host, fwd, lambda: then(fwd.results))

        plan.host(ici_host, ici, pass_on)

    def got_out(l):
        return lambda r: wl[l].update(w_out_g=r[0], w_qkv=_full_qkv(r[1], d))

    gather_behind([bf(w_out[0:1]), bf(qkv[0:1])], ("in_proj_fwd", 0), ("rglru_fwd", 0), got_out(0))
    for l in range(1, nl):
        gather_behind([bf(w_in[l:l + 1])], ("rglru_fwd", l - 1), ("mlstm_cell_fwd", l - 1),
                      lambda r, l=l: wl[l].update(w_in_g=r[0]))
        gather_behind([bf(w_out[l:l + 1]), bf(qkv[l:l + 1])], ("mlstm_cell_fwd", l - 1), ("out_proj_fwd", l - 1), got_out(l))

    mod_blocks = _exchange([mod_cols], False, "scatter_modulation")[0]
    mod3 = mod_blocks[:, 0:nl].transpose(1, 0, 2).reshape(nl, 3, d)
    rep["vecs"] = _vec_table(rep)
    saved, xl = [], x[0]
    for l in range(nl):
        head = (final_g.reshape(1, -1), loss_target[0]) if l == nl - 1 else None
        xl, sv = _layer_fwd(l, xl, mod3, wl[l], rep, plan, head)
        saved.append(sv)
    grad_x, loss_p, g_final = xl

    keys = ("w_in", "w_out", "w_qkv", "small")
    parity = lax.axis_index("c").astype(jnp.int32).reshape(1)
    grads, recv = [None] * nl, [None] * nl

    def small_parts(g):
        return [bf(_qkv_slots(g["w_qkv"], N_DEV)), bf(_small_slots(g))]

    def reduce_behind(l, host_layer):
        parts = [grads[l]["w_in"], grads[l]["w_out"]]
        swap = _core_swap_comm(parts)
        direct = _exchange_comm(small_parts(grads[l]), False)

        def summed():
            sums = [_pair_sum(a, o, parity, "pair_sum_%s_layer%d" % (key, l)) for key, a, o in zip(keys, parts, swap.results)]
            big = _chip_swap_comm([sums[0]])
            rest = _chip_swap_comm([sums[1]])
            plan.host(("mlstm_cell_bwd", host_layer), big)
            plan.host(("rglru_bwd", host_layer), rest,
                      lambda: recv.__setitem__(l, big.results + rest.results + direct.results))

        plan.host(("out_proj_bwd", host_layer), swap, summed)
        plan.host(("out_proj_bwd", host_layer), direct)

    first, own = {}, {}

    def reduce_own(names, parts_fn, ready_key, swap_key, chip_key):
        def go():
            parts = parts_fn()
            swap = _core_swap_comm(parts)

            def summed():
                sums = [_pair_sum(a, o, parity, "pair_sum_%s_layer0" % n) for n, a, o in zip(names, parts, swap.results)]
                chip = _chip_swap_comm(sums)
                plan.host(chip_key, chip, lambda: own.update(zip(names, chip.results)))

            plan.host(swap_key, swap, summed)

        plan.after.setdefault(ready_key, []).append(go)

    reduce_own(["w_out"], lambda: [first["w_out"]], ("out_proj_bwd", 0), ("mlstm_cell_bwd", 0), ("mlstm_proj_bwd", 0))
    reduce_own(["w_in"], lambda: [first["w_in"]], ("in_proj_bwd_w", 0), ("in_proj_bwd_x", 0), ("in_proj_bwd_x_rest", 0))

    def small_own():
        direct = _exchange_comm(small_parts(first), False)
        plan.host(("in_proj_bwd_w", 0), direct, lambda: own.update(w_qkv=direct.results[0], small=direct.results[1]))

    plan.after.setdefault(("rglru_bwd", 0), []).append(small_own)

    matrices = {}

    def reduce_matrices():
        layers = [grads[l] if l > 0 else first for l in range(nl)]
        mp = jnp.stack([jnp.stack([g["rg_w_a"], g["rg_w_x"]]) for g in layers]).reshape(N_DEV, -1, LANES).astype(BF16)
        scatter = _exchange_comm([mp], False)

        def summed():
            gather = _exchange_comm(_sum_parts(scatter.results, "sum_replicated_matrices"), True)
            plan.host(("in_proj_bwd_x", 0), gather, lambda: matrices.update(mp=gather.results[0].reshape(-1, LANES)))

        plan.host(("in_proj_bwd_w", 0), scatter, summed)

    plan.after.setdefault(("rglru_bwd", 0), []).append(reduce_matrices)

    for l in reversed(range(nl)):
        if l > 0:
            grad_x, grads[l] = _layer_bwd(l, grad_x, saved[l], wl[l], rep, plan)
            reduce_behind(l, l - 1)
        else:
            grad_x, grads[l] = _layer_bwd(l, grad_x, saved[l], wl[l], rep, plan, first, True)
    plan.flush()
    recv[0] = [own[key] for key in keys]

    shard = {p: dict(w_in=given[p + "w_in"], w_out=given[p + "w_out"], w_qkv=qkv_shard(p), small=small_shard(p))
             for p in ("", "m_", "v_")}
    res = {}
    for ki, key in enumerate(keys):
        res[key] = _reduce_adam([recv[l][ki] for l in range(nl)], shard[""][key], shard["m_"][key], shard["v_"][key],
                                "reduce_adam_" + key)

    dmods = jnp.concatenate([grads[l]["dmod"] for l in range(nl)], axis=0)
    dmod_blocks = jnp.pad(dmods.reshape(nl, N_DEV, wcols).transpose(1, 0, 2), ((0, 0), (0, SUBLANES - nl), (0, 0)))
    widen = lambda a: jnp.pad(a, ((0, 0), (0, d - a.shape[1])))
    rows = [widen(grads[l]["acc"][n][0:1]) for l in range(nl) for n in REP_ROWS] + [g_final[0:1], widen(loss_p[0:1])]
    vp = jnp.concatenate(rows + [jnp.zeros(((-len(rows)) % SUBLANES, d), F32)], axis=0)
    (dmod_recv,), (vp_all,) = _run_comms([_exchange_comm([dmod_blocks], False), _exchange_comm([vp], True)], "tail_exchange")
    res["w_ada"] = _ada_grad_adam(cact_all.T, dmod_recv[:, 0:nl].transpose(1, 0, 2), w_ada, m_w_ada, v_w_ada)
    mp_r = matrices["mp"]
    lanes = lambda a: jnp.pad(a, ((0, 0), (0, LANES - a.shape[1])))
    shaped = dict(ml_b_if=lanes, final_g=lambda a: a.reshape(1, d))
    names = [n for n in REPLICATED if n != "b_ada"] + ["b_ada"]
    rep_res, vp_r = _adam_replicated(vp_all, mp_r, {n: tuple(shaped.get(n, lambda a: a)(given[p + n]) for p in ("", "m_", "v_"))
                                                    for n in names}, nl)
    unshaped = dict(ml_b_if=lambda a: a[:, 0:ml_b_if.shape[1]], final_g=lambda a: a.reshape(d))
    rep_out = [{n: unshaped.get(n, lambda a: a)(rep_res[n][kind]) for n in names} for kind in range(4)]
    loss = vp_r[nl * len(REP_ROWS) + 1, 0]

    if_rows = ml_w_if.shape[1]
    order = ("norm_g", "w_ada", "b_ada", "w_in", "rg_conv_w", "rg_conv_b", "rg_w_a", "rg_b_a", "rg_w_x", "rg_b_x",
             "rg_lambda", "ml_conv_w", "ml_conv_b", "ml_w_q", "ml_w_k", "ml_w_v", "ml_w_if", "ml_b_if", "ml_norm_g",
             "w_out", "final_g")
    outs = [loss, grad_x[None]]
    for kind in range(4):
        qkv_k = res["w_qkv"][kind].reshape((nl, 3) + ml_w_q.shape[1:])
        rg_cw, ml_cw, wif = _small_unpack(res["small"][kind], if_rows)
        sharded = dict(w_ada=res["w_ada"][kind], w_in=res["w_in"][kind], w_out=res["w_out"][kind], ml_w_q=qkv_k[:, 0],
                       ml_w_k=qkv_k[:, 1], ml_w_v=qkv_k[:, 2], rg_conv_w=rg_cw, ml_conv_w=ml_cw, ml_w_if=wif)
        for n in order:
            outs.append(sharded[n] if n in sharded else rep_out[kind][n])
    return tuple(outs)
```

```python
import functools

import jax
import jax.numpy as jnp
from jax import lax
from jax.experimental import pallas as pl
from jax.experimental.pallas import tpu as pltpu

F32 = jnp.float32
BF16 = jnp.bfloat16
MESH_AXES = ("x", "y", "c")
N_DEV = 8
EPS = 1e-6
RG_C = 8.0
ML_CHUNK = 128
CONV_WIDTH = 4
ADAM_LR = 0.001
ADAM_B1 = 0.9
ADAM_B2 = 0.999
ADAM_EPS = 1e-08
ADAM_WD = 0.01
ADAM_STEP = 10
NEG_BIG = -1e30
LANES = 128
SUBLANES = 8
VMEM_LIMIT = 56 * 1024 * 1024
HI = lax.Precision.HIGHEST


def _params(n_grid):
    return pltpu.CompilerParams(dimension_semantics=("arbitrary",) * n_grid, vmem_limit_bytes=VMEM_LIMIT)


def _mm(a, b):
    return jnp.dot(a.astype(BF16), b.astype(BF16), preferred_element_type=F32)


def _mm_nt(a, b):
    return lax.dot_general(a.astype(BF16), b.astype(BF16), (((1,), (1,)), ((), ())), preferred_element_type=F32)


def _mm_tn(a, b):
    return lax.dot_general(a.astype(BF16), b.astype(BF16), (((0,), (0,)), ((), ())), preferred_element_type=F32)


def _mm_hi(a, b):
    return jnp.dot(a, b, precision=HI, preferred_element_type=F32)


def _sigmoid(x):
    return 1.0 / (1.0 + jnp.exp(-x))


def _softplus(x):
    return jnp.maximum(x, 0.0) + jnp.log(1.0 + jnp.exp(-jnp.abs(x)))


def _neg_expm1(x):
    poly = -x * (1.0 + x * (0.5 + x * (1.0 / 6.0 + x * (1.0 / 24.0 + x * (1.0 / 120.0)))))
    return jnp.where(jnp.abs(x) < 0.05, poly, 1.0 - jnp.exp(x))


def _iota(shape, dim):
    return lax.broadcasted_iota(jnp.int32, shape, dim)


def _colsum(x):
    return jnp.sum(x, axis=0, keepdims=True)


def _rowsum(x):
    return jnp.sum(x, axis=1, keepdims=True)


def _col(x, j):
    return _rowsum(jnp.where(_iota(x.shape, 1) == j, x, 0.0))


def _row(x, j):
    return _colsum(jnp.where(_iota(x.shape, 0) == j, x, 0.0))


def _shift_down(x, j, prev8):
    if j == 0:
        return x
    t = x.shape[0]
    main = jnp.where(_iota(x.shape, 0) >= j, pltpu.roll(x, j, 0), 0.0)
    fix = jnp.where(_iota(prev8.shape, 0) < j, pltpu.roll(prev8, j, 0), 0.0)
    return jnp.concatenate([main[0:SUBLANES] + fix, main[SUBLANES:t]], axis=0)


def _shift_up(x, j, next8):
    if j == 0:
        return x
    t = x.shape[0]
    main = jnp.where(_iota(x.shape, 0) < t - j, pltpu.roll(x, t - j, 0), 0.0)
    fix = jnp.where(_iota(next8.shape, 0) >= SUBLANES - j, pltpu.roll(next8, SUBLANES - j, 0), 0.0)
    return jnp.concatenate([main[0:t - SUBLANES], main[t - SUBLANES:t] + fix], axis=0)


def _conv(x, prev8, w_ref):
    y = w_ref[CONV_WIDTH - 1:CONV_WIDTH, :] * x
    for j in range(1, CONV_WIDTH):
        y = y + w_ref[CONV_WIDTH - 1 - j:CONV_WIDTH - j, :] * _shift_down(x, j, prev8)
    return y


def _conv_bwd(dy, x, next8, w_ref, gw_ref):
    dx = None
    for j in range(CONV_WIDTH):
        k = CONV_WIDTH - 1 - j
        up = _shift_up(dy, j, next8)
        gw_ref[k:k + 1, :] += _colsum(up * x)
        term = w_ref[k:k + 1, :] * up
        dx = term if dx is None else dx + term
    return dx


def _scan_into(a, b, carry, out_ref, reverse):
    t, c = a.shape
    groups = t // SUBLANES
    a3 = a.reshape(groups, SUBLANES, c)
    b3 = b.reshape(groups, SUBLANES, c)
    sub = _iota(a3.shape, 1)
    for step in (1, 2, 4):
        keep = sub < SUBLANES - step if reverse else sub >= step
        shift = SUBLANES - step if reverse else step
        a_s = jnp.where(keep, pltpu.roll(a3, shift, 1), 1.0)
        b_s = jnp.where(keep, pltpu.roll(b3, shift, 1), 0.0)
        b3 = a3 * b_s + b3
        a3 = a3 * a_s
    for g in (reversed(range(groups)) if reverse else range(groups)):
        rows = slice(g * SUBLANES, (g + 1) * SUBLANES)
        out_ref[rows, :] = b3[g] + a3[g] * carry
        edge = g * SUBLANES if reverse else (g + 1) * SUBLANES - 1
        carry = out_ref[edge:edge + 1, :]


def _blockdiag(x, w_ref, transpose_w=False):
    nh, dh, _ = w_ref.shape
    outs = []
    for h in range(nh):
        xs = x[:, h * dh:(h + 1) * dh]
        outs.append(_mm_nt(xs, w_ref[h]) if transpose_w else _mm(xs, w_ref[h]))
    return jnp.concatenate(outs, axis=1)


def _rg_gates(xc, wa_ref, ba_ref, wx_ref, bx_ref, lam_ref):
    r = _sigmoid(_blockdiag(xc, wa_ref) + ba_ref[...])
    ig = _sigmoid(_blockdiag(xc, wx_ref) + bx_ref[...])
    sp = _softplus(-lam_ref[...])
    log_a = -RG_C * r * sp
    a = jnp.exp(log_a)
    beta = jnp.sqrt(_neg_expm1(2.0 * log_a))
    return r, ig, sp, a, beta


def _bcast8(row):
    return jnp.broadcast_to(row, (SUBLANES, row.shape[1]))


def _full(shape):
    nd = len(shape)
    return pl.BlockSpec(shape, lambda *_: (0,) * nd)


class _Comm:
    def __init__(self, arrays, out_shapes, sems, start, finish, aliases=()):
        self.arrays, self.out_shapes, self.sems = list(arrays), list(out_shapes), list(sems)
        self.start, self.finish, self.aliases = start, finish, tuple(aliases)
        self.results = None


class _RowOf:
    def __init__(self, ref, k):
        self.ref, self.k = ref, k

    def __getitem__(self, idx):
        cols = slice(None) if idx is Ellipsis else idx[1]
        return self.ref[0, self.k:self.k + 1, cols]


class _PartOf:
    def __init__(self, ref, rows=None, cols=None, lead=None):
        self.ref, self.rows, self.cols, self.lead = ref, rows, cols, lead
        if rows is not None:
            self.shape = (rows.stop - rows.start,) + tuple(ref.shape[1:])

    def _at(self, idx):
        if self.lead is not None:
            return (self.lead,) + tuple(idx[1:])
        if self.cols is not None:
            return (slice(None), self.cols)
        return (self.rows, slice(None) if idx is Ellipsis else idx[1])

    def __getitem__(self, idx):
        return self.ref[self._at(idx)]

    def __setitem__(self, idx, value):
        self.ref[self._at(idx)] = value


def _vec(table, layer, k):
    return ("row", table, layer, k)


def _is_row(arg):
    return isinstance(arg, tuple) and len(arg) == 4 and arg[0] == "row"


def _call(body, comms, *, name, grid, in_specs, out_specs, out_shape, args, scratch_shapes=(), aliases=None):
    comms = [cm for cm in (comms or []) if cm is not None]
    rows = {i: a[3] for i, a in enumerate(args) if _is_row(a)}
    in_specs = [pl.BlockSpec((1,) + a[1].shape[1:], functools.partial(lambda layer, *_: (layer, 0, 0), a[2]))
                if _is_row(a) else sp for a, sp in zip(args, in_specs)]
    args = tuple(a[1] if _is_row(a) else a for a in args)
    n_in, n_out, n_sc = len(args), len(out_shape), len(scratch_shapes)
    c_arrays = [a for cm in comms for a in cm.arrays]
    c_outs = [o for cm in comms for o in cm.out_shapes]
    c_sems = [sm for cm in comms for sm in cm.sems]
    aliases, a_at, o_at = dict(aliases or {}), n_in, n_out
    for cm in comms:
        for (i, j) in cm.aliases:
            aliases[a_at + i] = o_at + j
        a_at += len(cm.arrays)
        o_at += len(cm.out_shapes)

    def wrapped(*refs):
        ins, c_in = refs[:n_in], refs[n_in:n_in + len(c_arrays)]
        ins = [_RowOf(r, rows[i]) if i in rows else r for i, r in enumerate(ins)]
        at = n_in + len(c_arrays)
        outs, c_out = refs[at:at + n_out], refs[at + n_out:at + n_out + len(c_outs)]
        at += n_out + len(c_outs)
        scr, sems = refs[at:at + n_sc], refs[at + n_sc:]
        views, ia, io, isem = [], 0, 0, 0
        for cm in comms:
            views.append((c_in[ia:ia + len(cm.arrays)], c_out[io:io + len(cm.out_shapes)], sems[isem:isem + len(cm.sems)]))
            ia, io, isem = ia + len(cm.arrays), io + len(cm.out_shapes), isem + len(cm.sems)
        if comms:
            @pl.when(pl.program_id(0) == 0)
            def _():
                for cm, view in zip(comms, views):
                    cm.start(*view)

        body(*ins, *outs, *scr)
        if comms:
            @pl.when(pl.program_id(0) == grid[0] - 1)
            def _():
                for cm, view in zip(comms, views):
                    cm.finish(*view)

    hbm = pl.BlockSpec(memory_space=pl.ANY)
    res = pl.pallas_call(
        wrapped, name=name, grid=grid,
        in_specs=list(in_specs) + [hbm] * len(c_arrays), out_specs=list(out_specs) + [hbm] * len(c_outs),
        out_shape=list(out_shape) + c_outs, scratch_shapes=list(scratch_shapes) + c_sems,
        input_output_aliases=aliases, compiler_params=_params(len(grid)),
    )(*args, *c_arrays)
    at = n_out
    for cm in comms:
        cm.results = list(res[at:at + len(cm.out_shapes)])
        at += len(cm.out_shapes)
    return list(res[:n_out])


def _join_columns(w_ref, wcat):
    nd, _, _, w = w_ref.shape

    @pl.when(pl.program_id(0) == 0)
    def _():
        for j in range(nd):
            wcat[:, j * w:(j + 1) * w] = w_ref[j, 0]


def _in_fwd(x, ng, scale, shift, w_in_g, layer, tile, comms=None):
    s, d = x.shape
    nd, _, _, w = w_in_g.shape

    def body(x_ref, ng_ref, sc_ref, sh_ref, w_ref, u_ref, h_ref, wcat):
        _join_columns(w_ref, wcat)
        xv = x_ref[...]
        rs = lax.rsqrt(jnp.mean(xv * xv, axis=1, keepdims=True) + EPS)
        hb = (xv * rs * ng_ref[...] * (1.0 + sc_ref[...]) + sh_ref[...]).astype(BF16)
        h_ref[...] = hb
        u_ref[...] = jnp.dot(hb, wcat[...], preferred_element_type=F32)

    return _call(
        body, comms, name="in_proj_fwd", grid=(s // tile,),
        in_specs=[pl.BlockSpec((tile, d), lambda i: (i, 0)), _full((1, d)), _full((1, d)), _full((1, d)),
                  pl.BlockSpec((nd, 1, d, w), lambda i: (0, layer, 0, 0), pipeline_mode=pl.Buffered(1))],
        out_specs=[pl.BlockSpec((tile, nd * w), lambda i: (i, 0)), pl.BlockSpec((tile, d), lambda i: (i, 0))],
        out_shape=[jax.ShapeDtypeStruct((s, nd * w), F32), jax.ShapeDtypeStruct((s, d), BF16)],
        scratch_shapes=[pltpu.VMEM((d, nd * w), BF16)],
        args=(x, ng, scale, shift, w_in_g))


def _rg_fwd(u, d, conv_w, conv_b, w_a, b_a, w_x, b_x, lam, tile, comms=None):
    s = u.shape[0]

    def body(x_ref, z_ref, cw_ref, cb_ref, wa_ref, ba_ref, wx_ref, bx_ref, lam_ref,
             h_ref, y_ref, xc_ref, r_ref, i_ref, a_ref, beta_ref, prev8, hcar):
        @pl.when(pl.program_id(0) == 0)
        def _():
            prev8[...] = jnp.zeros_like(prev8)
            hcar[...] = jnp.zeros_like(hcar)

        x = x_ref[...]
        xc = _conv(x, prev8[...], cw_ref) + cb_ref[...]
        prev8[...] = x[tile - SUBLANES:tile, :]
        r, ig, _, a, beta = _rg_gates(xc, wa_ref, ba_ref, wx_ref, bx_ref, lam_ref)
        xc_ref[...] = xc
        r_ref[...] = r
        i_ref[...] = ig
        a_ref[...] = a
        beta_ref[...] = beta
        _scan_into(a, beta * ig * xc, hcar[SUBLANES - 1:SUBLANES, :], h_ref, False)
        h = h_ref[...]
        hcar[...] = h[tile - SUBLANES:tile, :]
        z = z_ref[...]
        y_ref[...] = (h * z * _sigmoid(z)).astype(BF16)

    vec = _full((1, d))
    return _call(
        body, comms, name="rglru_fwd", grid=(s // tile,),
        in_specs=[pl.BlockSpec((tile, d), lambda i: (i, 0)), pl.BlockSpec((tile, d), lambda i: (i, 1)),
                  _full(conv_w.shape), vec, _full(w_a.shape), vec, _full(w_x.shape), vec, vec],
        out_specs=[pl.BlockSpec((tile, d), lambda i: (i, 0))] * 7,
        out_shape=[jax.ShapeDtypeStruct((s, d), F32), jax.ShapeDtypeStruct((s, d), BF16)] + [jax.ShapeDtypeStruct((s, d), F32)] * 5,
        scratch_shapes=[pltpu.VMEM((SUBLANES, d), F32), pltpu.VMEM((SUBLANES, d), F32)],
        args=(u, u, conv_w, conv_b, w_a, b_a, w_x, b_x, lam))


def _ml_pre(u, d, conv_w, conv_b, w_q, w_k, w_v, wif, bif, tile, comms=None):
    s = u.shape[0]
    nh = w_q.shape[0]

    def body(x_ref, cw_ref, cb_ref, wq_ref, wk_ref, wv_ref, wif_ref, bif_ref, q_ref, k_ref, v_ref, g_ref, pre_ref, prev8):
        @pl.when(pl.program_id(0) == 0)
        def _():
            prev8[...] = jnp.zeros_like(prev8)

        x = x_ref[...]
        pre = _conv(x, prev8[...], cw_ref) + cb_ref[...]
        prev8[...] = x[tile - SUBLANES:tile, :]
        xc = pre * _sigmoid(pre)
        q = _blockdiag(xc, wq_ref)
        k = _blockdiag(xc, wk_ref)
        v = _blockdiag(x, wv_ref)
        pre_ref[...] = pre
        q_ref[...] = q
        k_ref[...] = k
        v_ref[...] = v
        g = _mm(q, wif_ref[0:d, :]) + _mm(k, wif_ref[d:2 * d, :]) + _mm(v, wif_ref[2 * d:3 * d, :]) + bif_ref[...]
        lane = _iota(g.shape, 1)
        gl = jnp.where(lane < 4, g, jnp.where(lane < 8, -_softplus(-g), 0.0))
        tri = jnp.where(_iota((ML_CHUNK, ML_CHUNK), 1) <= _iota((ML_CHUNK, ML_CHUNK), 0), 1.0, 0.0)
        cums = [_mm_hi(tri, gl[c * ML_CHUNK:(c + 1) * ML_CHUNK, :]) for c in range(tile // ML_CHUNK)]
        cum = cums[0] if len(cums) == 1 else jnp.concatenate(cums, axis=0)
        g_ref[...] = gl + jnp.where((lane >= 8) & (lane < 12), pltpu.roll(cum, 4, 1), 0.0)

    vec = _full((1, d))
    return _call(
        body, comms, name="mlstm_proj_fwd", grid=(s // tile,),
        in_specs=[pl.BlockSpec((tile, d), lambda i: (i, 2)), _full(conv_w.shape), vec,
                  _full(w_q.shape), _full(w_k.shape), _full(w_v.shape), _full(wif.shape), _full((1, LANES))],
        out_specs=[pl.BlockSpec((tile, d), lambda i: (i, 0))] * 3 + [pl.BlockSpec((tile, LANES), lambda i: (i, 0)),
                                                                     pl.BlockSpec((tile, d), lambda i: (i, 0))],
        out_shape=[jax.ShapeDtypeStruct((s, d), F32)] * 3 + [jax.ShapeDtypeStruct((s, LANES), F32),
                                                             jax.ShapeDtypeStruct((s, d), F32)],
        scratch_shapes=[pltpu.VMEM((SUBLANES, d), F32)],
        args=(u, conv_w, conv_b, w_q, w_k, w_v, wif, bif))


CELL_CHUNKS_PER_STEP = 4
CELL_BWD_CHUNKS_PER_STEP = 2


def _cell_chunk(h, nh, q_ref, k_ref, v_ref, gc, gr, m_prev, c_h, n_h, m_t=None, r0=0):
    lc = ML_CHUNK
    dh = q_ref.shape[1] // nh
    sl = slice(h * dh, (h + 1) * dh)
    qh = q_ref[r0:r0 + lc, sl]
    kh = k_ref[r0:r0 + lc, sl] * (dh ** -0.5)
    vh = v_ref[r0:r0 + lc, sl]
    li_c = _col(gc, h)
    b_c = _col(gc, 8 + h)
    lib_r = _row(gr, h) - _row(gr, 8 + h)
    b_last = _colsum(jnp.where(_iota((lc, 1), 0) == lc - 1, b_c, 0.0))
    causal = _iota((lc, lc), 1) <= _iota((lc, lc), 0)
    dmat = jnp.where(causal, b_c + lib_r, NEG_BIG)
    m_inter = b_c + m_prev
    if m_t is None:
        m_t = jnp.maximum(m_inter, jnp.max(dmat, axis=1, keepdims=True))
    w_intra = jnp.exp(dmat - m_t)
    w_inter = jnp.exp(m_inter - m_t)
    amat = _mm_nt(qh, kh)
    smat = amat * w_intra
    qc = _mm(qh, c_h)
    qn = _rowsum(qh * n_h)
    den = _rowsum(smat) + w_inter * qn
    gst = b_last - b_c + li_c
    m_new = jnp.maximum(b_last + m_prev, jnp.max(gst, axis=0, keepdims=True))
    w_state = jnp.exp(gst - m_new)
    decay = jnp.exp(b_last + m_prev - m_new)
    return dict(sl=sl, qh=qh, kh=kh, vh=vh, m_t=m_t, w_intra=w_intra, w_inter=w_inter, smat=smat, qc=qc, qn=qn,
                den=den, m_new=m_new, w_state=w_state, decay=decay)


def _ml_cell_fwd(q, k, v, gcol, grow, u, ng, nh, comms=None):
    s, d = q.shape
    lc = ML_CHUNK
    nc = s // lc
    dh = d // nh

    per = CELL_CHUNKS_PER_STEP if nc % CELL_CHUNKS_PER_STEP == 0 else 1

    def body(q_ref, k_ref, v_ref, gc_ref, gr_ref, o_ref, z_ref, ng_ref,
             cell_ref, y_ref, cs_ref, ns_ref, ms_ref, mt_ref, c_sc, n_sc, m_sc):
        @pl.when(pl.program_id(0) == 0)
        def _():
            c_sc[...] = jnp.zeros_like(c_sc)
            n_sc[...] = jnp.zeros_like(n_sc)
            m_sc[...] = jnp.zeros_like(m_sc)

        lane = _iota((lc, LANES), 1)
        for cc in range(per):
            rows = slice(cc * lc, (cc + 1) * lc)
            gc = gc_ref[rows, :]
            gr = gr_ref[:, rows]
            mt_acc = jnp.zeros((lc, LANES), F32)
            for h in range(nh):
                c_h = c_sc[h]
                n_h = n_sc[h, 0:1, :]
                m_prev = jnp.max(m_sc[h, 0:1, :], axis=1, keepdims=True)
                cs_ref[cc, h] = c_h
                ns_ref[cc, h] = n_sc[h]
                ms_ref[cc, h] = m_sc[h]
                t = _cell_chunk(h, nh, q_ref, k_ref, v_ref, gc, gr, m_prev, c_h, n_h, r0=cc * lc)
                sl = t["sl"]
                num = _mm(t["smat"], t["vh"]) + t["w_inter"] * t["qc"]
                cell_h = num / jnp.maximum(jnp.abs(t["den"]), jnp.exp(-t["m_t"]))
                mt_acc = jnp.where(lane == h, t["m_t"], mt_acc)
                kw = t["kh"] * t["w_state"]
                c_sc[h] = t["decay"] * c_h + _mm_tn(kw, t["vh"])
                n_sc[h] = _bcast8(t["decay"] * n_h + _colsum(kw))
                m_sc[h] = jnp.broadcast_to(t["m_new"], (SUBLANES, LANES))
                hg = _sigmoid(o_ref[rows, sl]) * cell_h
                hn = hg * lax.rsqrt(jnp.mean(hg * hg, axis=1, keepdims=True) + EPS)
                z = z_ref[rows, sl]
                cell_ref[rows, sl] = cell_h
                y_ref[rows, sl] = (hn * ng_ref[:, sl] * z * _sigmoid(z)).astype(BF16)
            mt_ref[rows, :] = mt_acc

    tok = pl.BlockSpec((per * lc, d), lambda c: (c, 0))
    return _call(
        body, comms, name="mlstm_cell_fwd", grid=(nc // per,),
        in_specs=[tok, tok, tok, pl.BlockSpec((per * lc, LANES), lambda c: (c, 0)),
                  pl.BlockSpec((16, per * lc), lambda c: (0, c)),
                  pl.BlockSpec((per * lc, d), lambda c: (c, 3)), pl.BlockSpec((per * lc, d), lambda c: (c, 4)), _full((1, d))],
        out_specs=[tok, tok, pl.BlockSpec((per, nh, dh, dh), lambda c: (c, 0, 0, 0)),
                   pl.BlockSpec((per, nh, SUBLANES, dh), lambda c: (c, 0, 0, 0)),
                   pl.BlockSpec((per, nh, SUBLANES, LANES), lambda c: (c, 0, 0, 0)),
                   pl.BlockSpec((per * lc, LANES), lambda c: (c, 0))],
        out_shape=[jax.ShapeDtypeStruct((s, d), F32), jax.ShapeDtypeStruct((s, d), BF16),
                   jax.ShapeDtypeStruct((nc, nh, dh, dh), F32), jax.ShapeDtypeStruct((nc, nh, SUBLANES, dh), F32),
                   jax.ShapeDtypeStruct((nc, nh, SUBLANES, LANES), F32), jax.ShapeDtypeStruct((s, LANES), F32)],
        scratch_shapes=[pltpu.VMEM((nh, dh, dh), F32), pltpu.VMEM((nh, SUBLANES, dh), F32),
                        pltpu.VMEM((nh, SUBLANES, LANES), F32)],
        args=(q, k, v, gcol, grow, u, u, ng))


def _out_fwd(x, y_rg, y_ml, gate, w_out_g, layer, tile, comms=None, head=None):
    s, d = x.shape
    nd, _, r, _ = w_out_g.shape

    def body(x_ref, yr_ref, ym_ref, g_ref, w_ref, *rest):
        ycat = jnp.concatenate([yr_ref[...].astype(BF16), ym_ref[...].astype(BF16)], axis=1)
        acc = jnp.dot(ycat, w_ref[...].reshape(nd * r, d), preferred_element_type=F32)
        xn = x_ref[...] + g_ref[...] * acc
        if head is None:
            xn_ref, y_ref = rest
            y_ref[...] = acc
            xn_ref[...] = xn
            return
        fg_ref, t_ref, y_ref, dx_ref, loss_ref, gg_ref = rest
        y_ref[...] = acc

        @pl.when(pl.program_id(0) == 0)
        def _():
            loss_ref[...] = jnp.zeros_like(loss_ref)
            gg_ref[...] = jnp.zeros_like(gg_ref)

        fg = fg_ref[...]
        rs = lax.rsqrt(jnp.mean(xn * xn, axis=1, keepdims=True) + EPS)
        xh = xn * rs
        e = xh * fg - t_ref[...]
        loss_ref[...] += jnp.broadcast_to(_colsum(_rowsum(e * e)) * (0.5 / d), loss_ref.shape)
        dy = e * (1.0 / d)
        gg_ref[...] += _bcast8(_colsum(dy * xh))
        dxh = dy * fg
        dx_ref[...] = rs * (dxh - xh * jnp.mean(dxh * xh, axis=1, keepdims=True))

    tok = pl.BlockSpec((tile, d), lambda i: (i, 0))
    in_specs = [tok, tok, tok, _full((1, d)), pl.BlockSpec((nd, 1, r, d), lambda i: (0, layer, 0, 0))]
    if head is None:
        return _call(body, comms, name="out_proj_fwd", grid=(s // tile,), in_specs=in_specs, out_specs=[tok, tok],
                     out_shape=[jax.ShapeDtypeStruct((s, d), F32)] * 2, args=(x, y_rg, y_ml, gate, w_out_g))
    return _call(
        body, comms, name="out_proj_loss", grid=(s // tile,),
        in_specs=in_specs + [_full((1, d)), tok],
        out_specs=[tok, tok, _full((SUBLANES, LANES)), _full((SUBLANES, d))],
        out_shape=[jax.ShapeDtypeStruct((s, d), F32)] * 2 + [jax.ShapeDtypeStruct((SUBLANES, LANES), F32),
                                                             jax.ShapeDtypeStruct((SUBLANES, d), F32)],
        args=(x, y_rg, y_ml, gate, w_out_g, *head))


def _ml_out_stage_bwd(dy, cell, o, z, ng):
    so = _sigmoid(o)
    hg = so * cell
    rinv = lax.rsqrt(jnp.mean(hg * hg, axis=1, keepdims=True) + EPS)
    hn = hg * rinv
    sz = _sigmoid(z)
    dz = dy * hn * ng * (sz + z * sz * (1.0 - sz))
    dymid = dy * z * sz
    dhn = dymid * ng
    dhg = rinv * (dhn - hn * jnp.mean(dhn * hn, axis=1, keepdims=True))
    return dz, dhg * cell * so * (1.0 - so), dhg * so, _colsum(dymid * hn)


def _out_bwd(dxo, gate, y, y_rg, y_ml, w_out_g, layer, tile, comms=None):
    s, d = dxo.shape
    nd, _, r, _ = w_out_g.shape

    def body(dx_ref, g_ref, y_ref, yr_ref, ym_ref, w_ref, dyr_ref, dym_ref, gw_ref, dg_ref):
        @pl.when(pl.program_id(0) == 0)
        def _():
            gw_ref[...] = jnp.zeros_like(gw_ref)
            dg_ref[...] = jnp.zeros_like(dg_ref)

        dxv = dx_ref[...]
        dg_ref[...] += _bcast8(_colsum(dxv * y_ref[...]))
        dyb = (dxv * g_ref[...]).astype(BF16)
        dycat = lax.dot_general(dyb, w_ref[...].reshape(nd * r, d), (((1,), (1,)), ((), ())), preferred_element_type=F32)
        dyr_ref[...] = dycat[:, 0:d]
        dym_ref[...] = dycat[:, d:2 * d]
        ycat = jnp.concatenate([yr_ref[...].astype(BF16), ym_ref[...].astype(BF16)], axis=1)
        gw_ref[...] += lax.dot_general(ycat, dyb, (((0,), (0,)), ((), ())), preferred_element_type=F32).reshape(nd, r, d)

    tok = pl.BlockSpec((tile, d), lambda i: (i, 0))
    return _call(
        body, comms, name="out_proj_bwd", grid=(s // tile,),
        in_specs=[tok, _full((1, d)), tok, tok, tok, pl.BlockSpec((nd, 1, r, d), lambda i: (0, layer, 0, 0))],
        out_specs=[tok, tok, _full((nd, r, d)), _full((SUBLANES, d))],
        out_shape=[jax.ShapeDtypeStruct((s, d), F32)] * 2 + [jax.ShapeDtypeStruct((nd, r, d), F32),
                                                             jax.ShapeDtypeStruct((SUBLANES, d), F32)],
        args=(dxo, gate, y, y_rg, y_ml, w_out_g))


def _ml_cell_bwd(dy_ml, u, cell, q, k, v, gcol, grow, mt, cs, ns, ms, ng, nh, comms=None):
    s, d = q.shape
    lc = ML_CHUNK
    nc = s // lc
    dh = d // nh

    per = CELL_BWD_CHUNKS_PER_STEP if nc % CELL_BWD_CHUNKS_PER_STEP == 0 else 1

    def body(*refs):
        gng_ref, dc_sc, dn_sc = refs[20], refs[21], refs[22]

        @pl.when(pl.program_id(0) == 0)
        def _():
            dc_sc[...] = jnp.zeros_like(dc_sc)
            dn_sc[...] = jnp.zeros_like(dn_sc)
            gng_ref[...] = jnp.zeros_like(gng_ref)

        for cc in reversed(range(per)):
            rows = slice(cc * lc, (cc + 1) * lc)
            views = [refs[at] if at == 13 else _PartOf(refs[at], cols=rows) if at == 8 else
                     _PartOf(refs[at], lead=cc) if at in (10, 11, 12) else _PartOf(refs[at], rows=rows) for at in range(20)]
            chunk(*views, gng_ref, dc_sc, dn_sc)

    def chunk(dy_ref, o_ref, z_ref, cell_ref, q_ref, k_ref, v_ref, gc_ref, gr_ref, mt_ref, cs_ref, ns_ref, ms_ref,
              ng_ref, dq_ref, dk_ref, dv_ref, dg_ref, do_ref, dz_ref, gng_ref, dc_sc, dn_sc):
        gc = gc_ref[...]
        gr = gr_ref[...]
        mtv = mt_ref[...]
        lane = _iota((lc, LANES), 1)
        rowv = _iota((lc, 1), 0)
        dg_acc = jnp.zeros((lc, LANES), F32)
        for h in range(nh):
            c_h = cs_ref[0, h]
            n_h = ns_ref[0, h, 0:1, :]
            m_prev = jnp.max(ms_ref[0, h, 0:1, :], axis=1, keepdims=True)
            t = _cell_chunk(h, nh, q_ref, k_ref, v_ref, gc, gr, m_prev, c_h, n_h, m_t=_col(mtv, h))
            sl, qh, kh, vh = t["sl"], t["qh"], t["kh"], t["vh"]
            w_intra, w_inter, smat, w_state, decay = t["w_intra"], t["w_inter"], t["smat"], t["w_state"], t["decay"]
            cell_h = cell_ref[:, sl]
            dz, do, dcell, gng = _ml_out_stage_bwd(dy_ref[:, sl], cell_h, o_ref[:, sl], z_ref[:, sl], ng_ref[:, sl])
            dz_ref[:, sl] = dz.astype(BF16)
            do_ref[:, sl] = do.astype(BF16)
            gng_ref[:, sl] += _bcast8(gng)
            eneg = jnp.exp(-t["m_t"])
            aden = jnp.abs(t["den"])
            nst = jnp.maximum(aden, eneg)
            dnum = dcell / nst
            dden = jnp.where(aden > eneg, -_rowsum(cell_h * dcell) / nst * jnp.sign(t["den"]), 0.0)
            pmat = _mm_nt(dnum, vh) + dden
            damat = pmat * w_intra
            gmat = pmat * smat
            wdn = w_inter * dnum
            wdd = w_inter * dden
            dqh = _mm(damat, kh) + _mm_nt(wdn, c_h) + wdd * n_h
            dkh = _mm_tn(damat, qh)
            dvh = _mm_tn(smat, dnum)
            dw_inter = _rowsum(dnum * t["qc"]) + dden * t["qn"]
            dcn = dc_sc[h]
            dnn = dn_sc[h, 0:1, :]
            kw = kh * w_state
            dkw = _mm_nt(vh, dcn) + dnn
            dvh = dvh + _mm(kw, dcn)
            dkh = dkh + dkw * w_state
            dgst = _rowsum(dkw * kh) * w_state
            ddecay = _colsum(_rowsum(dcn * c_h)) + _rowsum(dnn * n_h)
            db_last = _colsum(dgst) + ddecay * decay
            rs_g = _rowsum(gmat)
            cs_g = _rowsum(gmat.T)
            db = rs_g - cs_g + dw_inter * w_inter - dgst + jnp.where(rowv == lc - 1, db_last, 0.0)
            dli = cs_g + dgst
            dc_sc[h] = decay * dcn + _mm_tn(qh, wdn)
            dn_sc[h] = _bcast8(decay * dnn + _colsum(qh * wdd))
            dq_ref[:, sl] = dqh
            dk_ref[:, sl] = dkh * (dh ** -0.5)
            dv_ref[:, sl] = dvh
            dg_acc = jnp.where(lane == h, dli, jnp.where(lane == 4 + h, db, dg_acc))
        dg_ref[...] = dg_acc

    rev = lambda c: nc // per - 1 - c
    tok = pl.BlockSpec((per * lc, d), lambda c: (rev(c), 0))
    g128 = pl.BlockSpec((per * lc, LANES), lambda c: (rev(c), 0))
    return _call(
        body, comms, name="mlstm_cell_bwd", grid=(nc // per,),
        in_specs=[tok, pl.BlockSpec((per * lc, d), lambda c: (rev(c), 3)), pl.BlockSpec((per * lc, d), lambda c: (rev(c), 4)),
                  tok, tok, tok, tok, g128, pl.BlockSpec((16, per * lc), lambda c: (0, rev(c))), g128,
                  pl.BlockSpec((per, nh, dh, dh), lambda c: (rev(c), 0, 0, 0)),
                  pl.BlockSpec((per, nh, SUBLANES, dh), lambda c: (rev(c), 0, 0, 0)),
                  pl.BlockSpec((per, nh, SUBLANES, LANES), lambda c: (rev(c), 0, 0, 0)), _full((1, d))],
        out_specs=[tok, tok, tok, g128, tok, tok, _full((SUBLANES, d))],
        out_shape=[jax.ShapeDtypeStruct((s, d), F32)] * 3 + [jax.ShapeDtypeStruct((s, LANES), F32)]
        + [jax.ShapeDtypeStruct((s, d), BF16)] * 2 + [jax.ShapeDtypeStruct((SUBLANES, d), F32)],
        scratch_shapes=[pltpu.VMEM((nh, dh, dh), F32), pltpu.VMEM((nh, SUBLANES, dh), F32)],
        args=(dy_ml, u, u, cell, q, k, v, gcol, grow, mt, cs, ns, ms, ng))


def _halo_spec(d, tile, nt, col):
    per = tile // SUBLANES
    return pl.BlockSpec((SUBLANES, d), lambda i: (jnp.maximum((nt - 1 - i) * per - 1, 0), col))


def _ml_pre_bwd(dq, dk, dv, dgates, gcol, u, pre, q, k, v, conv_w, w_q, w_k, w_v, wif_t, tile, comms=None):
    s, d = dq.shape
    nt = s // tile
    nh, dh, _ = w_q.shape

    def body(dq_ref, dk_ref, dv_ref, dg_ref, gc_ref, x_ref, pre_ref, q_ref, k_ref, v_ref, cw_ref,
             wq_ref, wk_ref, wv_ref, wift_ref,
             dx_ref, gwq_ref, gwk_ref, gwv_ref, gwif_ref, gbif_ref, gcw_ref, gcb_ref, next8):
        @pl.when(pl.program_id(0) == 0)
        def _():
            next8[...] = jnp.zeros_like(next8)
            for ref in (gwq_ref, gwk_ref, gwv_ref, gwif_ref, gbif_ref, gcw_ref, gcb_ref):
                ref[...] = jnp.zeros_like(ref)

        x = x_ref[...]
        pre = pre_ref[...]
        sg = _sigmoid(pre)
        xc = pre * sg
        dgc = dg_ref[...]
        lane = _iota(dgc.shape, 1)
        utri = jnp.where(_iota((ML_CHUNK, ML_CHUNK), 0) <= _iota((ML_CHUNK, ML_CHUNK), 1), 1.0, 0.0)
        rcs = [_mm_hi(utri, dgc[c * ML_CHUNK:(c + 1) * ML_CHUNK, :]) for c in range(tile // ML_CHUNK)]
        rc = rcs[0] if len(rcs) == 1 else jnp.concatenate(rcs, axis=0)
        dgates_v = jnp.where(lane < 4, dgc, jnp.where(lane < 8, rc * (1.0 - jnp.exp(gc_ref[...])), 0.0))
        dgb = dgates_v.astype(BF16)
        gbif_ref[...] += jnp.broadcast_to(_colsum(dgates_v), gbif_ref.shape)
        ext = jnp.dot(dgb, wift_ref[...], preferred_element_type=F32)
        dqt = dq_ref[...] + ext[:, 0:d]
        dkt = dk_ref[...] + ext[:, d:2 * d]
        dvt = dv_ref[...] + ext[:, 2 * d:3 * d]
        gwif_ref[:, 0:d] += _mm_tn(dgb, q_ref[...])
        gwif_ref[:, d:2 * d] += _mm_tn(dgb, k_ref[...])
        gwif_ref[:, 2 * d:3 * d] += _mm_tn(dgb, v_ref[...])
        dxc_parts, dxv_parts = [], []
        for h in range(nh):
            sl = slice(h * dh, (h + 1) * dh)
            gwq_ref[h] += _mm_tn(xc[:, sl], dqt[:, sl])
            gwk_ref[h] += _mm_tn(xc[:, sl], dkt[:, sl])
            gwv_ref[h] += _mm_tn(x[:, sl], dvt[:, sl])
            dxc_parts.append(_mm_nt(dqt[:, sl], wq_ref[h]) + _mm_nt(dkt[:, sl], wk_ref[h]))
            dxv_parts.append(_mm_nt(dvt[:, sl], wv_ref[h]))
        dxc = jnp.concatenate(dxc_parts, axis=1)
        dxv = jnp.concatenate(dxv_parts, axis=1)
        dpre = dxc * (sg + pre * sg * (1.0 - sg))
        gcb_ref[...] += _bcast8(_colsum(dpre))
        dx_ref[...] = (dxv + _conv_bwd(dpre, x, next8[...], cw_ref, gcw_ref)).astype(BF16)
        next8[...] = dpre[0:SUBLANES, :]

    rev = lambda i: nt - 1 - i
    tok = pl.BlockSpec((tile, d), lambda i: (rev(i), 0))
    g128 = pl.BlockSpec((tile, LANES), lambda i: (rev(i), 0))
    wsh = (nh, dh, dh)
    return _call(
        body, comms, name="mlstm_proj_bwd", grid=(nt,),
        in_specs=[tok, tok, tok, g128, g128, pl.BlockSpec((tile, d), lambda i: (rev(i), 2)), tok,
                  tok, tok, tok, _full(conv_w.shape), _full(wsh), _full(wsh), _full(wsh), _full(wif_t.shape)],
        out_specs=[tok, _full(wsh), _full(wsh), _full(wsh), _full((LANES, 3 * d)), _full((SUBLANES, LANES)),
                   _full((SUBLANES, d)), _full((SUBLANES, d))],
        out_shape=[jax.ShapeDtypeStruct((s, d), BF16)] + [jax.ShapeDtypeStruct(wsh, F32)] * 3
        + [jax.ShapeDtypeStruct((LANES, 3 * d), F32), jax.ShapeDtypeStruct((SUBLANES, LANES), F32),
           jax.ShapeDtypeStruct((SUBLANES, d), F32), jax.ShapeDtypeStruct((SUBLANES, d), F32)],
        scratch_shapes=[pltpu.VMEM((SUBLANES, d), F32)],
        args=(dq, dk, dv, dgates, gcol, u, pre, q, k, v, conv_w, w_q, w_k, w_v, wif_t))


def _rg_bwd(dy_rg, u, h_rg, gates, conv_w, w_a, w_x, lam, tile, comms=None):
    s, d = dy_rg.shape
    nt = s // tile
    nh, dh, _ = w_a.shape

    def body(dy_ref, x_ref, z_ref, h_ref, hhalo_ref, xc_ref, r_ref, i_ref, a_ref, beta_ref, cw_ref, wa_ref,
             wx_ref, lam_ref,
             dx_ref, dz_ref, gwa_ref, gwx_ref, gba_ref, gbx_ref, glam_ref, gcw_ref, gcb_ref, next8, anext, dnext, dbuf):
        i = pl.program_id(0)

        @pl.when(i == 0)
        def _():
            for ref in (next8, anext, dnext, gwa_ref, gwx_ref, gba_ref, gbx_ref, glam_ref, gcw_ref, gcb_ref):
                ref[...] = jnp.zeros_like(ref)

        inner = jnp.where(i < nt - 1, 1.0, 0.0)
        xc, r, ig, a, beta = xc_ref[...], r_ref[...], i_ref[...], a_ref[...], beta_ref[...]
        sp = _softplus(-lam_ref[...])
        h = h_ref[...]
        row = _iota(h.shape, 0)
        hprev = jnp.where(row >= 1, pltpu.roll(h, 1, 0), hhalo_ref[SUBLANES - 1:SUBLANES, :] * inner)
        z = z_ref[...]
        sz = _sigmoid(z)
        dyv = dy_ref[...]
        dz_ref[...] = (dyv * h * (sz + z * sz * (1.0 - sz))).astype(BF16)
        a_up = jnp.where(row < tile - 1, pltpu.roll(a, tile - 1, 0), anext[0:1, :])
        _scan_into(a_up, dyv * z * sz, dnext[0:1, :], dbuf, True)
        delta = dbuf[...]
        anext[...] = a[0:SUBLANES, :]
        dnext[...] = delta[0:SUBLANES, :]
        dla = delta * hprev * a - delta * ig * xc * (a * a / beta)
        glam_ref[...] += _bcast8(_colsum(dla * r) * (RG_C * _sigmoid(-lam_ref[...])))
        dpa = dla * (-RG_C * sp) * r * (1.0 - r)
        dpx = delta * beta * xc * ig * (1.0 - ig)
        gba_ref[...] += _bcast8(_colsum(dpa))
        gbx_ref[...] += _bcast8(_colsum(dpx))
        parts = []
        for hh in range(nh):
            sl = slice(hh * dh, (hh + 1) * dh)
            gwa_ref[hh] += _mm_tn(xc[:, sl], dpa[:, sl])
            gwx_ref[hh] += _mm_tn(xc[:, sl], dpx[:, sl])
            parts.append(_mm_nt(dpa[:, sl], wa_ref[hh]) + _mm_nt(dpx[:, sl], wx_ref[hh]))
        dxc = delta * beta * ig + jnp.concatenate(parts, axis=1)
        gcb_ref[...] += _bcast8(_colsum(dxc))
        dx_ref[...] = _conv_bwd(dxc, x_ref[...], next8[...], cw_ref, gcw_ref).astype(BF16)
        next8[...] = dxc[0:SUBLANES, :]

    rev = lambda i: nt - 1 - i
    tok = pl.BlockSpec((tile, d), lambda i: (rev(i), 0))
    vec = _full((1, d))
    acc = _full((SUBLANES, d))
    wsh = (nh, dh, dh)
    return _call(
        body, comms, name="rglru_bwd", grid=(nt,),
        in_specs=[tok, tok, pl.BlockSpec((tile, d), lambda i: (rev(i), 1)), tok,
                  _halo_spec(d, tile, nt, 0)] + [tok] * 5 + [_full(conv_w.shape), _full(wsh), _full(wsh), vec],
        out_specs=[tok, tok, _full(wsh), _full(wsh), acc, acc, acc, acc, acc],
        out_shape=[jax.ShapeDtypeStruct((s, d), BF16)] * 2 + [jax.ShapeDtypeStruct(wsh, F32)] * 2
        + [jax.ShapeDtypeStruct((SUBLANES, d), F32)] * 5,
        scratch_shapes=[pltpu.VMEM((SUBLANES, d), F32)] * 3 + [pltpu.VMEM((tile, d), F32)],
        args=(dy_rg, u, u, h_rg, h_rg, *gates, conv_w, w_a, w_x, lam))


def _segments(d, w, n_pieces, n_slots):
    bounds = sorted({k * d for k in range(n_pieces + 1)} | {j * w for j in range(n_slots + 1)})
    return [(lo // d, lo % d, lo // w, lo % w, hi - lo) for lo, hi in zip(bounds[:-1], bounds[1:])]


def _in_bwd(pieces, x, dxo, ng, scale, w_in_g, layer, tile, comms=None, tiles=None, prev=None):
    s, d = x.shape
    nd, _, _, w = w_in_g.shape
    first, count = tiles or (0, s // tile)
    n_p = len(pieces)

    def body(*refs):
        p_refs = refs[:n_p]
        x_ref, dxo_ref, ng_ref, sc_ref, w_ref = refs[n_p:n_p + 5]
        dx_ref, dsc_ref, dsh_ref, gng_ref, wcat = refs[-5:]
        _join_columns(w_ref, wcat)

        @pl.when(pl.program_id(0) == 0)
        def _():
            for k, ref in enumerate((dsc_ref, dsh_ref, gng_ref)):
                ref[...] = jnp.zeros_like(ref) if prev is None else refs[n_p + 6 + k][...]

        du = jnp.concatenate([p[...] for p in p_refs], axis=1)
        dh = lax.dot_general(du, wcat[...], (((1,), (1,)), ((), ())), preferred_element_type=F32)
        xv = x_ref[...]
        g = ng_ref[...]
        rs = lax.rsqrt(jnp.mean(xv * xv, axis=1, keepdims=True) + EPS)
        xh = xv * rs
        dsh_ref[...] += _bcast8(_colsum(dh))
        dsc_ref[...] += _bcast8(_colsum(dh * xh * g))
        dhn = dh * (1.0 + sc_ref[...])
        gng_ref[...] += _bcast8(_colsum(dhn * xh))
        dxh = dhn * g
        dx_ref[...] = dxo_ref[...] + rs * (dxh - xh * jnp.mean(dxh * xh, axis=1, keepdims=True))

    tok = pl.BlockSpec((tile, d), lambda i: (i + first, 0))
    vec = _full((1, d))
    acc = _full((SUBLANES, d))
    more_specs = [] if prev is None else [pl.BlockSpec(memory_space=pl.ANY), acc, acc, acc]
    return _call(
        body, comms, name="in_proj_bwd_x", grid=(count,),
        in_specs=[tok] * n_p + [tok, tok, vec, vec, pl.BlockSpec((nd, 1, d, w), lambda i: (0, layer, 0, 0),
                                                               pipeline_mode=pl.Buffered(1))] + more_specs,
        out_specs=[tok, acc, acc, acc],
        out_shape=[jax.ShapeDtypeStruct((s, d), F32)] + [jax.ShapeDtypeStruct((SUBLANES, d), F32)] * 3,
        scratch_shapes=[pltpu.VMEM((d, nd * w), BF16)],
        args=(*pieces, x, dxo, ng, scale, w_in_g) + (() if prev is None else tuple(prev)),
        aliases={} if prev is None else {n_p + 5: 0})


def _in_bwd_w(pieces, hbf, w, slots, tile, comms=None):
    s, d = hbf.shape
    nd_all = len(pieces) * d // w
    segs = [sg for sg in _segments(d, w, len(pieces), nd_all) if sg[2] in slots]

    def body(*refs):
        p_refs = refs[:len(pieces)]
        h_ref, gw_ref = refs[len(pieces):]

        @pl.when(pl.program_id(0) == 0)
        def _():
            gw_ref[...] = jnp.zeros_like(gw_ref)

        hv = h_ref[...]
        for (kk, a, j, b, width) in segs:
            gw_ref[j - slots[0], :, b:b + width] += _mm_tn(hv, p_refs[kk][:, a:a + width])

    tok = pl.BlockSpec((tile, d), lambda i: (i, 0))
    return _call(
        body, comms, name="in_proj_bwd_w", grid=(s // tile,),
        in_specs=[tok] * len(pieces) + [tok],
        out_specs=[pl.BlockSpec((len(slots), d, w), lambda i: (0, 0, 0), pipeline_mode=pl.Buffered(1))],
        out_shape=[jax.ShapeDtypeStruct((len(slots), d, w), F32)],
        args=(*pieces, hbf))[0]


def _exchange(arrs, gather, name):
    return _run_comms([_exchange_comm(arrs, gather)], name)[0]


def _run_comms(comms, name):
    _call(lambda: None, comms, name=name, grid=(1,), in_specs=[], out_specs=[], out_shape=[], args=())
    return [cm.results for cm in comms]


def _exchange_comm(arrs, gather):
    n = len(arrs)
    per = N_DEV - 1

    def copies(ins, outs, sems):
        send_sems, recv_sems, local_sems = sems
        x, y, c = (lax.axis_index(ax) for ax in MESH_AXES)
        me = 4 * x + 2 * y + c
        sends, recvs = [], []
        for flip in range(1, N_DEV):
            px = x ^ ((flip >> 2) & 1)
            py = y ^ ((flip >> 1) & 1)
            pc = c ^ (flip & 1)
            peer = 4 * px + 2 * py + pc
            for kk in range(n):
                src = ins[kk] if gather else ins[kk].at[peer]
                sends.append(_remote(src, outs[kk].at[me], send_sems, recv_sems, kk * per + flip - 1, (px, py, pc)))
                recvs.append(_remote(src, outs[kk].at[peer], send_sems, recv_sems, kk * per + flip - 1, (px, py, pc)))
        local = [pltpu.make_async_copy(ins[kk] if gather else ins[kk].at[me], outs[kk].at[me], local_sems.at[kk])
                 for kk in range(n)]
        return local, sends, recvs

    def start(ins, outs, sems):
        local, sends, _ = copies(ins, outs, sems)
        for cp in sends + local:
            cp.start()

    def finish(ins, outs, sems):
        local, sends, recvs = copies(ins, outs, sems)
        for cp in recvs:
            cp.wait_recv()
        for cp in sends:
            cp.wait_send()
        for cp in local:
            cp.wait()

    return _Comm(arrs, [jax.ShapeDtypeStruct((N_DEV,) + a.shape if gather else a.shape, a.dtype) for a in arrs],
                 [pltpu.SemaphoreType.DMA((n * per,)), pltpu.SemaphoreType.DMA((n * per,)), pltpu.SemaphoreType.DMA((n,))],
                 start, finish)


def _mesh_place():
    x, y, c = (lax.axis_index(ax) for ax in MESH_AXES)
    return x, y, c, (x, y, 1 - c), [(1 - x, y), (x, 1 - y), (1 - x, 1 - y)]


def _remote(src, dst, send_sems, recv_sems, sem, to):
    return pltpu.make_async_remote_copy(src_ref=src, dst_ref=dst, send_sem=send_sems.at[sem], recv_sem=recv_sems.at[sem],
                                        device_id=to, device_id_type=pl.DeviceIdType.MESH)


N_CHIPS = N_DEV // 2


def _pair_sum(a, other, parity, name):
    _, r, c = a.shape
    tr = _row_tile(r, c, 3)

    def body(p_ref, a_ref, o_ref, s_ref):
        s_ref[...] = (a_ref[...] + o_ref[...]).astype(BF16)

    return pl.pallas_call(
        body, name=name,
        grid_spec=pltpu.PrefetchScalarGridSpec(
            num_scalar_prefetch=1, grid=(N_CHIPS, r // tr),
            in_specs=[pl.BlockSpec((1, tr, c), lambda q, i, p: (2 * q + p[0], i, 0)),
                      pl.BlockSpec((1, tr, c), lambda q, i, p: (q, i, 0))],
            out_specs=pl.BlockSpec((1, tr, c), lambda q, i, p: (q, i, 0))),
        out_shape=jax.ShapeDtypeStruct((N_CHIPS, r, c), BF16),
        compiler_params=_params(2),
    )(parity, a, other)


def _adam_math(w, g, m, v):
    m = ADAM_B1 * m + (1.0 - ADAM_B1) * g
    v = ADAM_B2 * v + (1.0 - ADAM_B2) * (g * g)
    m_hat = m / (1.0 - ADAM_B1 ** ADAM_STEP)
    v_hat = v / (1.0 - ADAM_B2 ** ADAM_STEP)
    delta = -ADAM_LR * (m_hat / (jnp.sqrt(v_hat) + ADAM_EPS) + ADAM_WD * w)
    return delta, m, v


def _sum_devices(r_ref):
    acc = r_ref[0].astype(F32)
    for p in range(1, r_ref.shape[0]):
        acc = acc + r_ref[p].astype(F32)
    return acc


def _row_tile(rows, cols, n_bufs):
    budget = 24 * 1024 * 1024 // (n_bufs * 2 * cols * 4)
    t = rows
    while t > budget and t % 2 == 0 and (t // 2) % SUBLANES == 0:
        t //= 2
    return t


def _reduce_adam(recvs, w, m, v, name, comms=None):
    nl, r, c = w.shape
    n_part = recvs[0].shape[0]
    tr = _row_tile(r, c, n_part * nl + 7)
    nt = r // tr

    def body(*refs):
        r_refs = refs[:nl]
        w_ref, m_ref, v_ref, g_ref, d_ref, mo_ref, vo_ref = refs[nl:]
        layer = pl.program_id(0) // nt
        g = _sum_devices(r_refs[0])
        for ll in range(1, nl):
            g = jnp.where(layer == ll, _sum_devices(r_refs[ll]), g)
        delta, m2, v2 = _adam_math(w_ref[0], g, m_ref[0], v_ref[0])
        g_ref[0] = g
        d_ref[0] = delta
        mo_ref[0] = m2
        vo_ref[0] = v2

    def rspec(ll):
        return pl.BlockSpec((n_part, tr, c),
                            lambda i: (0, jnp.where(i // nt == ll, i % nt, jnp.where(i // nt < ll, 0, nt - 1)), 0))

    blk = pl.BlockSpec((1, tr, c), lambda i: (i // nt, i % nt, 0))
    return _call(
        body, comms, name=name, grid=(nl * nt,),
        in_specs=[rspec(ll) for ll in range(nl)] + [blk, blk, blk],
        out_specs=[blk] * 4,
        out_shape=[jax.ShapeDtypeStruct((nl, r, c), F32)] * 4,
        args=(*recvs, w, m, v))


def _tile_for(s, want):
    return min(want, s)


REPLICATED = ("norm_g", "b_ada", "rg_conv_b", "rg_w_a", "rg_b_a", "rg_w_x", "rg_b_x", "rg_lambda", "ml_conv_b",
              "ml_b_if", "ml_norm_g", "final_g")


def _small_pack(rg_conv_w, ml_conv_w, ml_w_if):
    nl = rg_conv_w.shape[0]
    wif_t = jnp.swapaxes(ml_w_if, 1, 2).reshape(nl, -1, LANES)
    return jnp.concatenate([rg_conv_w, ml_conv_w, wif_t], axis=1)


def _small_unpack(p, if_rows):
    nl = p.shape[0]
    rg_cw = p[:, 0:CONV_WIDTH]
    ml_cw = p[:, CONV_WIDTH:2 * CONV_WIDTH]
    wif = jnp.swapaxes(p[:, 2 * CONV_WIDTH:].reshape(nl, 8, if_rows), 1, 2)
    return rg_cw, ml_cw, wif


def _qkv_slots(g_qkv, nd):
    three, nh, dh, _ = g_qkv.shape
    return g_qkv.reshape(three, nh, nd, dh // nd, dh).transpose(2, 0, 1, 3, 4).reshape(nd, three * nh * (dh // nd), dh)


def _small_slots(g):
    nd = N_DEV
    cw = jnp.stack([g["rg_conv_w"], g["ml_conv_w"]]).reshape(2, CONV_WIDTH, nd, LANES).transpose(2, 0, 1, 3)
    cw = cw.reshape(nd, 2 * CONV_WIDTH, LANES)
    wif = g["wif_t"].reshape(8, nd, -1).transpose(1, 0, 2).reshape(nd, -1, LANES)
    return jnp.concatenate([cw, wif], axis=1)


def _slot(block):
    return 4 * block[0] + 2 * block[1] + block[2]


def _dma_sems(*counts):
    return [pltpu.SemaphoreType.DMA((n,)) for n in counts]


def _start_all(copies):
    for cp in copies:
        cp.start()


def _gather_ici_comm(arrs):
    n = len(arrs)

    def copies(ins, outs, sems):
        send_sems, recv_sems, local_sems = sems
        x, y, c, sibling, chips = _mesh_place()
        me = (x, y, c)
        peers = [(*chip, c) for chip in chips] + [sibling]
        local = [pltpu.make_async_copy(ins[kk], outs[kk].at[_slot(me)], local_sems.at[kk]) for kk in range(n)]
        sends = [_remote(ins[kk], outs[kk].at[_slot(me)], send_sems, recv_sems, kk * 4 + j, peer)
                 for j, peer in enumerate(peers) for kk in range(n)]
        recvs = [_remote(ins[kk], outs[kk].at[_slot(peer)], send_sems, recv_sems, kk * 4 + j, peer)
                 for j, peer in enumerate(peers) for kk in range(n)]
        return local, sends, recvs

    def start(ins, outs, sems):
        local, sends, _ = copies(ins, outs, sems)
        _start_all(sends + local)

    def finish(ins, outs, sems):
        local, sends, recvs = copies(ins, outs, sems)
        for cp in recvs:
            cp.wait_recv()
        for cp in sends:
            cp.wait_send()
        for cp in local:
            cp.wait()

    return _Comm(arrs, [jax.ShapeDtypeStruct((N_DEV,) + a.shape, a.dtype) for a in arrs], _dma_sems(4 * n, 4 * n, n),
                 start, finish)


def _gather_fwd_comm(bufs):
    n = len(bufs)

    def copies(ins, outs, sems):
        send_sems, recv_sems = sems
        _, _, c, sibling, chips = _mesh_place()
        sends = [_remote(ins[kk].at[_slot((*chip, c))], outs[kk].at[_slot((*chip, c))], send_sems, recv_sems, kk * 3 + j, sibling)
                 for j, chip in enumerate(chips) for kk in range(n)]
        recvs = [_remote(ins[kk].at[_slot((*chip, c))], outs[kk].at[_slot((*chip, 1 - c))], send_sems, recv_sems, kk * 3 + j, sibling)
                 for j, chip in enumerate(chips) for kk in range(n)]
        return sends, recvs

    def start(ins, outs, sems):
        _start_all(copies(ins, outs, sems)[0])

    def finish(ins, outs, sems):
        sends, recvs = copies(ins, outs, sems)
        for cp in recvs:
            cp.wait_recv()
        for cp in sends:
            cp.wait_send()

    return _Comm(bufs, [jax.ShapeDtypeStruct(a.shape, a.dtype) for a in bufs], _dma_sems(3 * n, 3 * n), start, finish,
                 aliases=[(i, i) for i in range(n)])


def _core_swap_comm(arrs):
    n = len(arrs)

    def copies(ins, outs, sems):
        send_sems, recv_sems = sems
        _, _, c, sibling, _ = _mesh_place()
        return [_remote(ins[kk].at[2 * q + (1 - c)], outs[kk].at[q], send_sems, recv_sems, kk * N_CHIPS + q, sibling)
                for q in range(N_CHIPS) for kk in range(n)]

    def start(ins, outs, sems):
        _start_all(copies(ins, outs, sems))

    def finish(ins, outs, sems):
        cps = copies(ins, outs, sems)
        for cp in cps:
            cp.wait_recv()
        for cp in cps:
            cp.wait_send()

    return _Comm(arrs, [jax.ShapeDtypeStruct((N_CHIPS,) + a.shape[1:], a.dtype) for a in arrs],
                 _dma_sems(N_CHIPS * n, N_CHIPS * n), start, finish)


def _chip_swap_comm(arrs):
    n = len(arrs)
    per = N_CHIPS - 1

    def copies(ins, outs, sems):
        send_sems, recv_sems, local_sems = sems
        x, y, c, _, chips = _mesh_place()
        mine = 2 * x + y
        sends = [_remote(ins[kk].at[2 * chip[0] + chip[1]], outs[kk].at[mine], send_sems, recv_sems, kk * per + j, (*chip, c))
                 for j, chip in enumerate(chips) for kk in range(n)]
        recvs = [_remote(ins[kk].at[mine], outs[kk].at[2 * chip[0] + chip[1]], send_sems, recv_sems, kk * per + j, (*chip, c))
                 for j, chip in enumerate(chips) for kk in range(n)]
        local = [pltpu.make_async_copy(ins[kk].at[mine], outs[kk].at[mine], local_sems.at[kk]) for kk in range(n)]
        return local, sends, recvs

    def start(ins, outs, sems):
        local, sends, _ = copies(ins, outs, sems)
        _start_all(sends + local)

    def finish(ins, outs, sems):
        local, sends, recvs = copies(ins, outs, sems)
        for cp in recvs:
            cp.wait_recv()
        for cp in sends:
            cp.wait_send()
        for cp in local:
            cp.wait()

    return _Comm(arrs, [jax.ShapeDtypeStruct(a.shape, a.dtype) for a in arrs], _dma_sems(per * n, per * n, n), start, finish)


def _ada_mod(c_all, w_ada, b_cols, comms=None):
    nl, d, w = w_ada.shape

    def body(c_ref, w_ref, b_ref, m_ref, ca_ref):
        sub = _iota((SUBLANES, d), 0)
        cv = jnp.zeros((SUBLANES, d), F32)
        for b in range(N_DEV):
            cv = jnp.where(sub == b, c_ref[b], cv)
        ca = cv * _sigmoid(cv)
        ca_ref[...] = ca
        m_ref[...] = jnp.zeros_like(m_ref)
        for l in range(nl):
            ml = _mm_hi(ca, w_ref[l]) + b_ref[l:l + 1, :]
            for b in range(N_DEV):
                m_ref[b, l:l + 1, :] = _row(ml, b)

    return _call(
        body, comms, name="adaln_mod_columns", grid=(1,),
        in_specs=[_full(c_all.shape), _full(w_ada.shape), _full(b_cols.shape)],
        out_specs=[_full((N_DEV, SUBLANES, w)), _full((SUBLANES, d))],
        out_shape=[jax.ShapeDtypeStruct((N_DEV, SUBLANES, w), F32), jax.ShapeDtypeStruct((SUBLANES, d), F32)],
        args=(c_all, w_ada, b_cols))


def _ada_grad_adam(cact_t, dmods, w, m, v, comms=None):
    nl, d, wd = w.shape
    tr = _row_tile(d, wd, 8)
    nt = d // tr

    def body(c_ref, dm_ref, w_ref, m_ref, v_ref, g_ref, d_ref, mo_ref, vo_ref):
        cv = c_ref[...]
        dm = dm_ref[0]
        g = _col(cv, 0) * _row(dm, 0)
        for b in range(1, N_DEV):
            g = g + _col(cv, b) * _row(dm, b)
        delta, m2, v2 = _adam_math(w_ref[0], g, m_ref[0], v_ref[0])
        g_ref[0] = g
        d_ref[0] = delta
        mo_ref[0] = m2
        vo_ref[0] = v2

    blk = pl.BlockSpec((1, tr, wd), lambda i: (i // nt, i % nt, 0))
    return _call(
        body, comms, name="adaln_grad_adam", grid=(nl * nt,),
        in_specs=[pl.BlockSpec((tr, N_DEV), lambda i: (i % nt, 0)), pl.BlockSpec((1, N_DEV, wd), lambda i: (i // nt, 0, 0)),
                  blk, blk, blk],
        out_specs=[blk] * 4, out_shape=[jax.ShapeDtypeStruct((nl, d, wd), F32)] * 4,
        args=(cact_t, dmods, w, m, v))


REP_ROWS = ("norm_g", "dshift", "dscale", "dgate", "rg_conv_b", "rg_b_a", "rg_b_x", "rg_lambda", "ml_conv_b", "ml_norm_g",
            "ml_b_if")


def _sum_parts(recvs, name):
    def body(*refs):
        for r_ref, o_ref in zip(refs[:len(recvs)], refs[len(recvs):]):
            o_ref[...] = _sum_devices(r_ref).astype(o_ref.dtype)

    return pl.pallas_call(
        body, name=name, grid=(1,),
        in_specs=[_full(r.shape) for r in recvs], out_specs=[_full(r.shape[1:]) for r in recvs],
        out_shape=[jax.ShapeDtypeStruct(r.shape[1:], r.dtype) for r in recvs], compiler_params=_params(1),
    )(*recvs)


def _adam_replicated(vp, mp, params, nl):
    d = vp.shape[2]
    nr = len(REP_ROWS)
    names = list(params)
    mat_shape = params["rg_w_a"][0].shape[1:]
    mat_rows = mp.shape[0] // (2 * nl)

    def pieces(name):
        if name == "final_g":
            return [(lambda vp_ref, mp_ref: vp_ref[nl * nr:nl * nr + 1, :], (slice(0, 1), slice(None)))]
        out = []
        for l in range(nl):
            if name in ("rg_w_a", "rg_w_x"):
                at = (2 * l + (name == "rg_w_x")) * mat_rows
                out.append((lambda vp_ref, mp_ref, at=at: mp_ref[at:at + mat_rows, :].astype(F32).reshape(mat_shape), l))
            elif name == "b_ada":
                for j in range(3):
                    r = l * nr + 1 + j
                    out.append((lambda vp_ref, mp_ref, r=r: vp_ref[r:r + 1, :], (slice(l, l + 1), slice(j * d, (j + 1) * d))))
            else:
                r = l * nr + REP_ROWS.index(name)
                cols = slice(0, LANES) if name == "ml_b_if" else slice(None)
                out.append((lambda vp_ref, mp_ref, r=r, cols=cols: vp_ref[r:r + 1, cols], (slice(l, l + 1), slice(None))))
        return out

    def body(*refs):
        parts_ref, mp_ref, vp_ref = refs[0], refs[1], refs[-1]
        ins, outs = refs[2:2 + 3 * len(names)], refs[2 + 3 * len(names):-1]
        vp_ref[...] = _sum_devices(parts_ref)
        for pi, name in enumerate(names):
            w_ref, m_ref, v_ref = ins[3 * pi:3 * pi + 3]
            g_ref, d_ref, mo_ref, vo_ref = outs[4 * pi:4 * pi + 4]
            for get, idx in pieces(name):
                g = get(vp_ref, mp_ref)
                delta, m2, v2 = _adam_math(w_ref[idx], g, m_ref[idx], v_ref[idx])
                g_ref[idx] = g
                d_ref[idx] = delta
                mo_ref[idx] = m2
                vo_ref[idx] = v2

    flat = [a for name in names for a in params[name]]
    out_shape = [jax.ShapeDtypeStruct(params[name][0].shape, F32) for name in names for _ in range(4)]
    out_shape.append(jax.ShapeDtypeStruct(vp.shape[1:], F32))
    res = pl.pallas_call(
        body, name="adam_replicated", grid=(1,),
        in_specs=[_full(vp.shape), _full(mp.shape)] + [_full(a.shape) for a in flat],
        out_specs=[_full(o.shape) for o in out_shape], out_shape=out_shape, compiler_params=_params(1),
    )(vp, mp, *flat)
    return {name: res[4 * pi:4 * pi + 4] for pi, name in enumerate(names)}, res[-1]


class _Plan:
    def __init__(self):
        self.hosted, self.after = {}, {}

    def host(self, key, comm, then=None):
        self.hosted.setdefault(key, []).append(comm)
        if then is not None:
            self.after.setdefault(key, []).append(then)

    def comms(self, key):
        return self.hosted.pop(key, None)

    def done(self, key):
        for fn in self.after.pop(key, []):
            fn()

    def flush(self):
        while self.hosted:
            key = next(iter(self.hosted))
            _call(lambda: None, self.comms(key), name="exchange_after_%s_%d" % key, grid=(1,), in_specs=[], out_specs=[],
                  out_shape=[], args=())
            self.done(key)


VEC_TABLE = ("norm_g", "rg_conv_b", "rg_b_a", "rg_b_x", "rg_lambda", "ml_conv_b", "ml_norm_g")


def _vec_table(rep):
    rows = [rep[n] for n in VEC_TABLE]
    return jnp.stack(rows + [jnp.zeros_like(rows[0])] * (SUBLANES - len(rows)), axis=1)


def _layer_fwd(l, xl, mod3, wl, rep, plan, head=None):
    s, d = xl.shape
    t_big, t_mid = _tile_for(s, 512), _tile_for(s, 256)
    nh_ml = rep["ml_b_if"].shape[1] // 2
    vec = lambda name: _vec(rep["vecs"], l, VEC_TABLE.index(name))
    shift, scale, gate = (_vec(mod3, l, kk) for kk in range(3))
    hosted = lambda name: plan.comms((name, l)) if plan else None
    done = lambda name: plan.done((name, l)) if plan else None
    u, hbf = _in_fwd(xl, vec("norm_g"), scale, shift, wl["w_in_g"], 0, t_big, hosted("in_proj_fwd"))
    done("in_proj_fwd")
    h_rg, y_rg, *rg_gates = _rg_fwd(u, d, wl["rg_conv_w"], vec("rg_conv_b"), rep["rg_w_a_bf"][l], vec("rg_b_a"),
                                    rep["rg_w_x_bf"][l], vec("rg_b_x"), vec("rg_lambda"), t_mid, hosted("rglru_fwd"))
    done("rglru_fwd")
    q, k, v, gcol, pre = _ml_pre(u, d, wl["ml_conv_w"], vec("ml_conv_b"), wl["w_qkv"][0], wl["w_qkv"][1],
                                 wl["w_qkv"][2], wl["wif_pad"], wl["bif_pad"], t_mid, hosted("mlstm_proj_fwd"))
    done("mlstm_proj_fwd")
    grow = gcol[:, 0:16].T
    cell, y_ml, cs, ns, ms, mt = _ml_cell_fwd(q, k, v, gcol, grow, u, vec("ml_norm_g"), nh_ml, hosted("mlstm_cell_fwd"))
    done("mlstm_cell_fwd")
    res = _out_fwd(xl, y_rg, y_ml, gate, wl["w_out_g"], 0, t_big, hosted("out_proj_fwd"), head)
    done("out_proj_fwd")
    x_new, y = (res[0], res[1]) if head is None else (tuple(res[1:]), res[0])
    saved = dict(x=xl, u=u, hbf=hbf, h_rg=h_rg, y_rg=y_rg, q=q, k=k, v=v, gcol=gcol, grow=grow, cell=cell, y_ml=y_ml,
                 cs=cs, ns=ns, ms=ms, mt=mt, y=y, scale=scale, gate=gate, rg_gates=rg_gates, pre=pre)
    return x_new, saved


def _layer_bwd(l, dx, sv, wl, rep, plan, grads=None, split_last=False):
    s, d = dx.shape
    t_big, t_mid = _tile_for(s, 512), _tile_for(s, 256)
    nh_ml = rep["ml_b_if"].shape[1] // 2
    nd, _, _, w_cols = wl["w_in_g"].shape
    grads = {} if grads is None else grads
    vec = lambda name: _vec(rep["vecs"], l, VEC_TABLE.index(name))
    hosted = lambda name: plan.comms((name, l)) if plan else None
    done = lambda name: plan.done((name, l)) if plan else None
    dy_rg, dy_ml, gw_out, dgate = _out_bwd(dx, sv["gate"], sv["y"], sv["y_rg"], sv["y_ml"], wl["w_out_g"], 0, t_big,
                                           hosted("out_proj_bwd"))
    grads.update(w_out=gw_out)
    done("out_proj_bwd")
    dq, dk, dv, dgates, d_mlo, d_mlz, g_mlng = _ml_cell_bwd(
        dy_ml, sv["u"], sv["cell"], sv["q"], sv["k"], sv["v"], sv["gcol"], sv["grow"], sv["mt"], sv["cs"], sv["ns"],
        sv["ms"], vec("ml_norm_g"), nh_ml, hosted("mlstm_cell_bwd"))
    done("mlstm_cell_bwd")
    d_mlx, g_wq, g_wk, g_wv, g_wift, g_bif, g_mlcw, g_mlcb = _ml_pre_bwd(
        dq, dk, dv, dgates, sv["gcol"], sv["u"], sv["pre"], sv["q"], sv["k"], sv["v"], wl["ml_conv_w"],
        wl["w_qkv"][0], wl["w_qkv"][1], wl["w_qkv"][2], wl["wift_pad"], t_mid, hosted("mlstm_proj_bwd"))
    done("mlstm_proj_bwd")
    d_rgx, d_rgz, g_wa, g_wx, g_ba, g_bx, g_lam, g_rgcw, g_rgcb = _rg_bwd(
        dy_rg, sv["u"], sv["h_rg"], sv["rg_gates"], wl["rg_conv_w"], rep["rg_w_a_bf"][l], rep["rg_w_x_bf"][l],
        vec("rg_lambda"), t_mid, hosted("rglru_bwd"))
    grads.update(w_qkv=jnp.stack([g_wq, g_wk, g_wv]), rg_conv_w=g_rgcw[0:CONV_WIDTH], ml_conv_w=g_mlcw[0:CONV_WIDTH],
                 wif_t=g_wift[0:8], rg_w_a=g_wa, rg_w_x=g_wx)
    acc = dict(dgate=dgate, rg_conv_b=g_rgcb, rg_b_a=g_ba, rg_b_x=g_bx, rg_lambda=g_lam, ml_conv_b=g_mlcb,
               ml_b_if=g_bif, ml_norm_g=g_mlng)
    done("rglru_bwd")
    pieces = [d_rgx, d_rgz, d_mlx, d_mlo, d_mlz]
    grads.update(w_in=_in_bwd_w(pieces, sv["hbf"], w_cols, tuple(range(nd)), _tile_for(s, 1024), hosted("in_proj_bwd_w")))
    done("in_proj_bwd_w")
    n_tiles = s // t_mid
    counts = [n_tiles // 5, n_tiles - n_tiles // 5 - 1, 1] if split_last and n_tiles >= 5 else [n_tiles]
    in_args = (pieces, sv["x"], dx, vec("norm_g"), sv["scale"], wl["w_in_g"], 0, t_mid)
    res, at = None, 0
    for key, count in zip(("in_proj_bwd_x", "in_proj_bwd_x_rest", "in_proj_bwd_x_end"), counts):
        res = _in_bwd(*in_args, hosted(key), (at, count), res)
        done(key)
        at += count
    dx, dscale, dshift, g_ng = res
    acc.update(norm_g=g_ng, dshift=dshift, dscale=dscale)
    grads.update(acc=acc, dmod=jnp.concatenate([dshift[0:1], dscale[0:1], dgate[0:1]], axis=1))
    return dx, grads


def _full_qkv(qkv_g, d):
    nd, _, rows3, dh = qkv_g.shape
    nh = d // dh
    rsh = rows3 // (3 * nh)
    return qkv_g.reshape(nd, 3, nh, rsh, dh).transpose(1, 2, 0, 3, 4).reshape(3, nh, nd * rsh, dh)


def _small_weights(small, l, ml_b_if):
    nd = small.shape[0]
    sm = small[:, l]
    cw = sm[:, 0:2 * CONV_WIDTH].reshape(nd, 2, CONV_WIDTH, LANES).transpose(1, 2, 0, 3).reshape(2, CONV_WIDTH, nd * LANES)
    if_rows = (sm.shape[1] - 2 * CONV_WIDTH) * LANES // 8
    wif_t = sm[:, 2 * CONV_WIDTH:].reshape(nd, 8, if_rows).transpose(1, 0, 2).reshape(8, nd * if_rows)
    wift_pad = jnp.pad(wif_t, ((0, LANES - 8), (0, 0))).astype(BF16)
    return dict(rg_conv_w=cw[0], ml_conv_w=cw[1], wift_pad=wift_pad, wif_pad=wift_pad.T,
                bif_pad=jnp.pad(ml_b_if[l], (0, LANES - 8)).reshape(1, LANES))


def kernel(x, c, norm_g, w_ada, b_ada, w_in, rg_conv_w, rg_conv_b, rg_w_a, rg_b_a, rg_w_x, rg_b_x, rg_lambda, ml_conv_w, ml_conv_b, ml_w_q, ml_w_k, ml_w_v, ml_w_if, ml_b_if, ml_norm_g, w_out, final_g, loss_target, m_norm_g, m_w_ada, m_b_ada, m_w_in, m_rg_conv_w, m_rg_conv_b, m_rg_w_a, m_rg_b_a, m_rg_w_x, m_rg_b_x, m_rg_lambda, m_ml_conv_w, m_ml_conv_b, m_ml_w_q, m_ml_w_k, m_ml_w_v, m_ml_w_if, m_ml_b_if, m_ml_norm_g, m_w_out, m_final_g, v_norm_g, v_w_ada, v_b_ada, v_w_in, v_rg_conv_w, v_rg_conv_b, v_rg_w_a, v_rg_b_a, v_rg_w_x, v_rg_b_x, v_rg_lambda, v_ml_conv_w, v_ml_conv_b, v_ml_w_q, v_ml_w_k, v_ml_w_v, v_ml_w_if, v_ml_b_if, v_ml_norm_g, v_w_out, v_final_g):
    given = dict(locals())
    nl = w_in.shape[0]
    d = x.shape[2]
    rep = {n: given[n] for n in REPLICATED}
    rep.update(rg_w_a_bf=rg_w_a.astype(BF16), rg_w_x_bf=rg_w_x.astype(BF16))
    bf = lambda a: a.astype(BF16)

    def qkv_shard(prefix):
        return jnp.stack([given[prefix + "ml_w_q"], given[prefix + "ml_w_k"], given[prefix + "ml_w_v"]], axis=1).reshape(
            nl, -1, ml_w_q.shape[-1])

    def small_shard(prefix):
        return _small_pack(given[prefix + "rg_conv_w"], given[prefix + "ml_conv_w"], given[prefix + "ml_w_if"])

    plan = _Plan()
    qkv = qkv_shard("")
    first_ici = _gather_ici_comm([bf(w_in[0:1]), small_shard("")])
    condition = _exchange_comm([jnp.broadcast_to(c, (SUBLANES, d))], True)
    _run_comms([first_ici, condition], "gather_first")
    first_fwd = _gather_fwd_comm(first_ici.results)
    wcols = w_ada.shape[2]
    me = 4 * lax.axis_index("x") + 2 * lax.axis_index("y") + lax.axis_index("c")
    b_cols = jnp.pad(lax.dynamic_slice_in_dim(b_ada, me * wcols, wcols, axis=1), ((0, SUBLANES - nl), (0, 0)))
    mod_cols, cact_all = _ada_mod(condition.results[0], w_ada, b_cols, [first_fwd])
    w_in_first, small = first_fwd.results
    wl = [_small_weights(small, l, ml_b_if) for l in range(nl)]
    wl[0]["w_in_g"] = w_in_first

    def gather_behind(arrs, ici_host, fwd_host, then):
        ici = _gather_ici_comm(arrs)

        def pass_on():
            fwd = _gather_fwd_comm(ici.results)
            plan.host(fwd_host, fwd, lambda: then(fwd.results))

        plan.host(ici_host, ici, pass_on)

    def got_out(l):
        return lambda r: wl[l].update(w_out_g=r[0], w_qkv=_full_qkv(r[1], d))

    gather_behind([bf(w_out[0:1]), bf(qkv[0:1])], ("in_proj_fwd", 0), ("rglru_fwd", 0), got_out(0))
    for l in range(1, nl):
        gather_behind([bf(w_in[l:l + 1])], ("rglru_fwd", l - 1), ("mlstm_cell_fwd", l - 1),
                      lambda r, l=l: wl[l].update(w_in_g=r[0]))
        gather_behind([bf(w_out[l:l + 1]), bf(qkv[l:l + 1])], ("mlstm_cell_fwd", l - 1), ("out_proj_fwd", l - 1), got_out(l))

    mod_blocks = _exchange([mod_cols], False, "scatter_modulation")[0]
    mod3 = mod_blocks[:, 0:nl].transpose(1, 0, 2).reshape(nl, 3, d)
    rep["vecs"] = _vec_table(rep)
    saved, xl = [], x[0]
    for l in range(nl):
        head = (final_g.reshape(1, -1), loss_target[0]) if l == nl - 1 else None
        xl, sv = _layer_fwd(l, xl, mod3, wl[l], rep, plan, head)
        saved.append(sv)
    grad_x, loss_p, g_final = xl

    keys = ("w_in", "w_out", "w_qkv", "small")
    parity = lax.axis_index("c").astype(jnp.int32).reshape(1)
    grads, recv = [None] * nl, [None] * nl

    def small_parts(g):
        return [bf(_qkv_slots(g["w_qkv"], N_DEV)), bf(_small_slots(g))]

    def reduce_behind(l, host_layer):
        parts = [grads[l]["w_in"], grads[l]["w_out"]]
        swap = _core_swap_comm(parts)
        direct = _exchange_comm(small_parts(grads[l]), False)

        def summed():
            sums = [_pair_sum(a, o, parity, "pair_sum_%s_layer%d" % (key, l)) for key, a, o in zip(keys, parts, swap.results)]
            big = _chip_swap_comm([sums[0]])
            rest = _chip_swap_comm([sums[1]])
            plan.host(("mlstm_cell_bwd", host_layer), big)
            plan.host(("rglru_bwd", host_layer), rest,
                      lambda: recv.__setitem__(l, big.results + rest.results + direct.results))

        plan.host(("in_proj_bwd_x", l), swap, summed)
        plan.host(("in_proj_bwd_x", l), direct)

    first, own = {}, {}

    def reduce_own(names, parts_fn, ready_key, swap_key, chip_key):
        def go():
            parts = parts_fn()
            swap = _core_swap_comm(parts)

            def summed():
                sums = [_pair_sum(a, o, parity, "pair_sum_%s_layer0" % n) for n, a, o in zip(names, parts, swap.results)]
                chip = _chip_swap_comm(sums)
                plan.host(chip_key, chip, lambda: own.update(zip(names, chip.results)))

            plan.host(swap_key, swap, summed)

        plan.after.setdefault(ready_key, []).append(go)

    reduce_own(["w_out"], lambda: [first["w_out"]], ("out_proj_bwd", 0), ("mlstm_cell_bwd", 0), ("mlstm_proj_bwd", 0))
    reduce_own(["w_in"], lambda: [first["w_in"]], ("in_proj_bwd_w", 0), ("in_proj_bwd_x", 0), ("in_proj_bwd_x_rest", 0))

    def small_own():
        direct = _exchange_comm(small_parts(first), False)
        plan.host(("in_proj_bwd_w", 0), direct, lambda: own.update(w_qkv=direct.results[0], small=direct.results[1]))

    plan.after.setdefault(("rglru_bwd", 0), []).append(small_own)

    matrices = {}

    def reduce_matrices():
        layers = [grads[l] if l > 0 else first for l in range(nl)]
        mp = jnp.stack([jnp.stack([g["rg_w_a"], g["rg_w_x"]]) for g in layers]).reshape(N_DEV, -1, LANES).astype(BF16)
        scatter = _exchange_comm([mp], False)

        def summed():
            gather = _exchange_comm(_sum_parts(scatter.results, "sum_replicated_matrices"), True)
            plan.host(("in_proj_bwd_x", 0), gather, lambda: matrices.update(mp=gather.results[0].reshape(-1, LANES)))

        plan.host(("in_proj_bwd_w", 0), scatter, summed)

    plan.after.setdefault(("rglru_bwd", 0), []).append(reduce_matrices)

    for l in reversed(range(nl)):
        if l > 0:
            grads[l] = {}
            plan.after.setdefault(("in_proj_bwd_w", l), []).append(functools.partial(reduce_behind, l, l - 1))
            grad_x, _ = _layer_bwd(l, grad_x, saved[l], wl[l], rep, plan, grads[l])
        else:
            grad_x, grads[l] = _layer_bwd(l, grad_x, saved[l], wl[l], rep, plan, first, True)
    plan.flush()
    recv[0] = [own[key] for key in keys]

    shard = {p: dict(w_in=given[p + "w_in"], w_out=given[p + "w_out"], w_qkv=qkv_shard(p), small=small_shard(p))
             for p in ("", "m_", "v_")}
    res = {}
    for ki, key in enumerate(keys):
        res[key] = _reduce_adam([recv[l][ki] for l in range(nl)], shard[""][key], shard["m_"][key], shard["v_"][key],
                                "reduce_adam_" + key)

    dmods = jnp.concatenate([grads[l]["dmod"] for l in range(nl)], axis=0)
    dmod_blocks = jnp.pad(dmods.reshape(nl, N_DEV, wcols).transpose(1, 0, 2), ((0, 0), (0, SUBLANES - nl), (0, 0)))
    widen = lambda a: jnp.pad(a, ((0, 0), (0, d - a.shape[1])))
    rows = [widen(grads[l]["acc"][n][0:1]) for l in range(nl) for n in REP_ROWS] + [g_final[0:1], widen(loss_p[0:1])]
    vp = jnp.concatenate(rows + [jnp.zeros(((-len(rows)) % SUBLANES, d), F32)], axis=0)
    (dmod_recv,), (vp_all,) = _run_comms([_exchange_comm([dmod_blocks], False), _exchange_comm([vp], True)], "tail_exchange")
    res["w_ada"] = _ada_grad_adam(cact_all.T, dmod_recv[:, 0:nl].transpose(1, 0, 2), w_ada, m_w_ada, v_w_ada)
    mp_r = matrices["mp"]
    lanes = lambda a: jnp.pad(a, ((0, 0), (0, LANES - a.shape[1])))
    shaped = dict(ml_b_if=lanes, final_g=lambda a: a.reshape(1, d))
    names = [n for n in REPLICATED if n != "b_ada"] + ["b_ada"]
    rep_res, vp_r = _adam_replicated(vp_all, mp_r, {n: tuple(shaped.get(n, lambda a: a)(given[p + n]) for p in ("", "m_", "v_"))
                                                    for n in names}, nl)
    unshaped = dict(ml_b_if=lambda a: a[:, 0:ml_b_if.shape[1]], final_g=lambda a: a.reshape(d))
    rep_out = [{n: unshaped.get(n, lambda a: a)(rep_res[n][kind]) for n in names} for kind in range(4)]
    loss = vp_r[nl * len(REP_ROWS) + 1, 0]

    if_rows = ml_w_if.shape[1]
    order = ("norm_g", "w_ada", "b_ada", "w_in", "rg_conv_w", "rg_conv_b", "rg_w_a", "rg_b_a", "rg_w_x", "rg_b_x",
             "rg_lambda", "ml_conv_w", "ml_conv_b", "ml_w_q", "ml_w_k", "ml_w_v", "ml_w_if", "ml_b_if", "ml_norm_g",
             "w_out", "final_g")
    outs = [loss, grad_x[None]]
    for kind in range(4):
        qkv_k = res["w_qkv"][kind].reshape((nl, 3) + ml_w_q.shape[1:])
        rg_cw, ml_cw, wif = _small_unpack(res["small"][kind], if_rows)
        sharded = dict(w_ada=res["w_ada"][kind], w_in=res["w_in"][kind], w_out=res["w_out"][kind], ml_w_q=qkv_k[:, 0],
                       ml_w_k=qkv_k[:, 1], ml_w_v=qkv_k[:, 2], rg_conv_w=rg_cw, ml_conv_w=ml_cw, ml_w_if=wif)
        for n in order:
            outs.append(sharded[n] if n in sharded else rep_out[kind][n])
    return tuple(outs)
```

```python
import functools

import jax
import jax.numpy as jnp
from jax import lax
from jax.experimental import pallas as pl
from jax.experimental.pallas import tpu as pltpu

F32 = jnp.float32
BF16 = jnp.bfloat16
MESH_AXES = ("x", "y", "c")
N_DEV = 8
EPS = 1e-6
RG_C = 8.0
ML_CHUNK = 128
CONV_WIDTH = 4
ADAM_LR = 0.001
ADAM_B1 = 0.9
ADAM_B2 = 0.999
ADAM_EPS = 1e-08
ADAM_WD = 0.01
ADAM_STEP = 10
NEG_BIG = -1e30
LANES = 128
SUBLANES = 8
VMEM_LIMIT = 56 * 1024 * 1024
HI = lax.Precision.HIGHEST


def _params(n_grid):
    return pltpu.CompilerParams(dimension_semantics=("arbitrary",) * n_grid, vmem_limit_bytes=VMEM_LIMIT)


def _mm(a, b):
    return jnp.dot(a.astype(BF16), b.astype(BF16), preferred_element_type=F32)


def _mm_nt(a, b):
    return lax.dot_general(a.astype(BF16), b.astype(BF16), (((1,), (1,)), ((), ())), preferred_element_type=F32)


def _mm_tn(a, b):
    return lax.dot_general(a.astype(BF16), b.astype(BF16), (((0,), (0,)), ((), ())), preferred_element_type=F32)


def _mm_hi(a, b):
    return jnp.dot(a, b, precision=HI, preferred_element_type=F32)


def _sigmoid(x):
    return 1.0 / (1.0 + jnp.exp(-x))


def _softplus(x):
    return jnp.maximum(x, 0.0) + jnp.log(1.0 + jnp.exp(-jnp.abs(x)))


def _neg_expm1(x):
    poly = -x * (1.0 + x * (0.5 + x * (1.0 / 6.0 + x * (1.0 / 24.0 + x * (1.0 / 120.0)))))
    return jnp.where(jnp.abs(x) < 0.05, poly, 1.0 - jnp.exp(x))


def _iota(shape, dim):
    return lax.broadcasted_iota(jnp.int32, shape, dim)


def _colsum(x):
    return jnp.sum(x, axis=0, keepdims=True)


def _rowsum(x):
    return jnp.sum(x, axis=1, keepdims=True)


def _col(x, j):
    return _rowsum(jnp.where(_iota(x.shape, 1) == j, x, 0.0))


def _row(x, j):
    return _colsum(jnp.where(_iota(x.shape, 0) == j, x, 0.0))


def _shift_down(x, j, prev8):
    if j == 0:
        return x
    t = x.shape[0]
    main = jnp.where(_iota(x.shape, 0) >= j, pltpu.roll(x, j, 0), 0.0)
    fix = jnp.where(_iota(prev8.shape, 0) < j, pltpu.roll(prev8, j, 0), 0.0)
    return jnp.concatenate([main[0:SUBLANES] + fix, main[SUBLANES:t]], axis=0)


def _shift_up(x, j, next8):
    if j == 0:
        return x
    t = x.shape[0]
    main = jnp.where(_iota(x.shape, 0) < t - j, pltpu.roll(x, t - j, 0), 0.0)
    fix = jnp.where(_iota(next8.shape, 0) >= SUBLANES - j, pltpu.roll(next8, SUBLANES - j, 0), 0.0)
    return jnp.concatenate([main[0:t - SUBLANES], main[t - SUBLANES:t] + fix], axis=0)


def _conv(x, prev8, w_ref):
    y = w_ref[CONV_WIDTH - 1:CONV_WIDTH, :] * x
    for j in range(1, CONV_WIDTH):
        y = y + w_ref[CONV_WIDTH - 1 - j:CONV_WIDTH - j, :] * _shift_down(x, j, prev8)
    return y


def _conv_bwd(dy, x, next8, w_ref, gw_ref):
    dx = None
    for j in range(CONV_WIDTH):
        k = CONV_WIDTH - 1 - j
        up = _shift_up(dy, j, next8)
        gw_ref[k:k + 1, :] += _colsum(up * x)
        term = w_ref[k:k + 1, :] * up
        dx = term if dx is None else dx + term
    return dx


def _scan_into(a, b, carry, out_ref, reverse):
    t, c = a.shape
    groups = t // SUBLANES
    a3 = a.reshape(groups, SUBLANES, c)
    b3 = b.reshape(groups, SUBLANES, c)
    sub = _iota(a3.shape, 1)
    for step in (1, 2, 4):
        keep = sub < SUBLANES - step if reverse else sub >= step
        shift = SUBLANES - step if reverse else step
        a_s = jnp.where(keep, pltpu.roll(a3, shift, 1), 1.0)
        b_s = jnp.where(keep, pltpu.roll(b3, shift, 1), 0.0)
        b3 = a3 * b_s + b3
        a3 = a3 * a_s
    for g in (reversed(range(groups)) if reverse else range(groups)):
        rows = slice(g * SUBLANES, (g + 1) * SUBLANES)
        out_ref[rows, :] = b3[g] + a3[g] * carry
        edge = g * SUBLANES if reverse else (g + 1) * SUBLANES - 1
        carry = out_ref[edge:edge + 1, :]


def _blockdiag(x, w_ref, transpose_w=False):
    nh, dh, _ = w_ref.shape
    outs = []
    for h in range(nh):
        xs = x[:, h * dh:(h + 1) * dh]
        outs.append(_mm_nt(xs, w_ref[h]) if transpose_w else _mm(xs, w_ref[h]))
    return jnp.concatenate(outs, axis=1)


def _rg_gates(xc, wa_ref, ba_ref, wx_ref, bx_ref, lam_ref):
    r = _sigmoid(_blockdiag(xc, wa_ref) + ba_ref[...])
    ig = _sigmoid(_blockdiag(xc, wx_ref) + bx_ref[...])
    sp = _softplus(-lam_ref[...])
    log_a = -RG_C * r * sp
    a = jnp.exp(log_a)
    beta = jnp.sqrt(_neg_expm1(2.0 * log_a))
    return r, ig, sp, a, beta


def _bcast8(row):
    return jnp.broadcast_to(row, (SUBLANES, row.shape[1]))


def _full(shape):
    nd = len(shape)
    return pl.BlockSpec(shape, lambda *_: (0,) * nd)


class _Comm:
    def __init__(self, arrays, out_shapes, sems, start, finish, aliases=()):
        self.arrays, self.out_shapes, self.sems = list(arrays), list(out_shapes), list(sems)
        self.start, self.finish, self.aliases = start, finish, tuple(aliases)
        self.results = None


class _RowOf:
    def __init__(self, ref, k):
        self.ref, self.k = ref, k

    def __getitem__(self, idx):
        cols = slice(None) if idx is Ellipsis else idx[1]
        return self.ref[0, self.k:self.k + 1, cols]


class _PartOf:
    def __init__(self, ref, rows=None, cols=None, lead=None):
        self.ref, self.rows, self.cols, self.lead = ref, rows, cols, lead
        if rows is not None:
            self.shape = (rows.stop - rows.start,) + tuple(ref.shape[1:])

    def _at(self, idx):
        if self.lead is not None:
            return (self.lead,) + tuple(idx[1:])
        if self.cols is not None:
            return (slice(None), self.cols)
        return (self.rows, slice(None) if idx is Ellipsis else idx[1])

    def __getitem__(self, idx):
        return self.ref[self._at(idx)]

    def __setitem__(self, idx, value):
        self.ref[self._at(idx)] = value


def _vec(table, layer, k):
    return ("row", table, layer, k)


def _is_row(arg):
    return isinstance(arg, tuple) and len(arg) == 4 and arg[0] == "row"


def _call(body, comms, *, name, grid, in_specs, out_specs, out_shape, args, scratch_shapes=(), aliases=None):
    comms = [cm for cm in (comms or []) if cm is not None]
    rows = {i: a[3] for i, a in enumerate(args) if _is_row(a)}
    in_specs = [pl.BlockSpec((1,) + a[1].shape[1:], functools.partial(lambda layer, *_: (layer, 0, 0), a[2]))
                if _is_row(a) else sp for a, sp in zip(args, in_specs)]
    args = tuple(a[1] if _is_row(a) else a for a in args)
    n_in, n_out, n_sc = len(args), len(out_shape), len(scratch_shapes)
    c_arrays = [a for cm in comms for a in cm.arrays]
    c_outs = [o for cm in comms for o in cm.out_shapes]
    c_sems = [sm for cm in comms for sm in cm.sems]
    aliases, a_at, o_at = dict(aliases or {}), n_in, n_out
    for cm in comms:
        for (i, j) in cm.aliases:
            aliases[a_at + i] = o_at + j
        a_at += len(cm.arrays)
        o_at += len(cm.out_shapes)

    def wrapped(*refs):
        ins, c_in = refs[:n_in], refs[n_in:n_in + len(c_arrays)]
        ins = [_RowOf(r, rows[i]) if i in rows else r for i, r in enumerate(ins)]
        at = n_in + len(c_arrays)
        outs, c_out = refs[at:at + n_out], refs[at + n_out:at + n_out + len(c_outs)]
        at += n_out + len(c_outs)
        scr, sems = refs[at:at + n_sc], refs[at + n_sc:]
        views, ia, io, isem = [], 0, 0, 0
        for cm in comms:
            views.append((c_in[ia:ia + len(cm.arrays)], c_out[io:io + len(cm.out_shapes)], sems[isem:isem + len(cm.sems)]))
            ia, io, isem = ia + len(cm.arrays), io + len(cm.out_shapes), isem + len(cm.sems)
        if comms:
            @pl.when(pl.program_id(0) == 0)
            def _():
                for cm, view in zip(comms, views):
                    cm.start(*view)

        body(*ins, *outs, *scr)
        if comms:
            @pl.when(pl.program_id(0) == grid[0] - 1)
            def _():
                for cm, view in zip(comms, views):
                    cm.finish(*view)

    hbm = pl.BlockSpec(memory_space=pl.ANY)
    res = pl.pallas_call(
        wrapped, name=name, grid=grid,
        in_specs=list(in_specs) + [hbm] * len(c_arrays), out_specs=list(out_specs) + [hbm] * len(c_outs),
        out_shape=list(out_shape) + c_outs, scratch_shapes=list(scratch_shapes) + c_sems,
        input_output_aliases=aliases, compiler_params=_params(len(grid)),
    )(*args, *c_arrays)
    at = n_out
    for cm in comms:
        cm.results = list(res[at:at + len(cm.out_shapes)])
        at += len(cm.out_shapes)
    return list(res[:n_out])


def _join_columns(w_ref, wcat):
    nd, _, _, w = w_ref.shape

    @pl.when(pl.program_id(0) == 0)
    def _():
        for j in range(nd):
            wcat[:, j * w:(j + 1) * w] = w_ref[j, 0]


def _in_fwd(x, ng, scale, shift, w_in_g, layer, tile, comms=None):
    s, d = x.shape
    nd, _, _, w = w_in_g.shape

    def body(x_ref, ng_ref, sc_ref, sh_ref, w_ref, u_ref, h_ref, wcat):
        _join_columns(w_ref, wcat)
        xv = x_ref[...]
        rs = lax.rsqrt(jnp.mean(xv * xv, axis=1, keepdims=True) + EPS)
        hb = (xv * rs * ng_ref[...] * (1.0 + sc_ref[...]) + sh_ref[...]).astype(BF16)
        h_ref[...] = hb
        u_ref[...] = jnp.dot(hb, wcat[...], preferred_element_type=F32)

    return _call(
        body, comms, name="in_proj_fwd", grid=(s // tile,),
        in_specs=[pl.BlockSpec((tile, d), lambda i: (i, 0)), _full((1, d)), _full((1, d)), _full((1, d)),
                  pl.BlockSpec((nd, 1, d, w), lambda i: (0, layer, 0, 0), pipeline_mode=pl.Buffered(1))],
        out_specs=[pl.BlockSpec((tile, nd * w), lambda i: (i, 0)), pl.BlockSpec((tile, d), lambda i: (i, 0))],
        out_shape=[jax.ShapeDtypeStruct((s, nd * w), F32), jax.ShapeDtypeStruct((s, d), BF16)],
        scratch_shapes=[pltpu.VMEM((d, nd * w), BF16)],
        args=(x, ng, scale, shift, w_in_g))


def _rg_fwd(u, d, conv_w, conv_b, w_a, b_a, w_x, b_x, lam, tile, comms=None):
    s = u.shape[0]

    def body(x_ref, z_ref, cw_ref, cb_ref, wa_ref, ba_ref, wx_ref, bx_ref, lam_ref,
             h_ref, y_ref, xc_ref, r_ref, i_ref, a_ref, beta_ref, prev8, hcar):
        @pl.when(pl.program_id(0) == 0)
        def _():
            prev8[...] = jnp.zeros_like(prev8)
            hcar[...] = jnp.zeros_like(hcar)

        x = x_ref[...]
        xc = _conv(x, prev8[...], cw_ref) + cb_ref[...]
        prev8[...] = x[tile - SUBLANES:tile, :]
        r, ig, _, a, beta = _rg_gates(xc, wa_ref, ba_ref, wx_ref, bx_ref, lam_ref)
        xc_ref[...] = xc
        r_ref[...] = r
        i_ref[...] = ig
        a_ref[...] = a
        beta_ref[...] = beta
        _scan_into(a, beta * ig * xc, hcar[SUBLANES - 1:SUBLANES, :], h_ref, False)
        h = h_ref[...]
        hcar[...] = h[tile - SUBLANES:tile, :]
        z = z_ref[...]
        y_ref[...] = (h * z * _sigmoid(z)).astype(BF16)

    vec = _full((1, d))
    return _call(
        body, comms, name="rglru_fwd", grid=(s // tile,),
        in_specs=[pl.BlockSpec((tile, d), lambda i: (i, 0)), pl.BlockSpec((tile, d), lambda i: (i, 1)),
                  _full(conv_w.shape), vec, _full(w_a.shape), vec, _full(w_x.shape), vec, vec],
        out_specs=[pl.BlockSpec((tile, d), lambda i: (i, 0))] * 7,
        out_shape=[jax.ShapeDtypeStruct((s, d), F32), jax.ShapeDtypeStruct((s, d), BF16)] + [jax.ShapeDtypeStruct((s, d), F32)] * 5,
        scratch_shapes=[pltpu.VMEM((SUBLANES, d), F32), pltpu.VMEM((SUBLANES, d), F32)],
        args=(u, u, conv_w, conv_b, w_a, b_a, w_x, b_x, lam))


def _ml_pre(u, d, conv_w, conv_b, w_q, w_k, w_v, wif, bif, tile, comms=None):
    s = u.shape[0]
    nh = w_q.shape[0]

    def body(x_ref, cw_ref, cb_ref, wq_ref, wk_ref, wv_ref, wif_ref, bif_ref, q_ref, k_ref, v_ref, g_ref, pre_ref, prev8):
        @pl.when(pl.program_id(0) == 0)
        def _():
            prev8[...] = jnp.zeros_like(prev8)

        x = x_ref[...]
        pre = _conv(x, prev8[...], cw_ref) + cb_ref[...]
        prev8[...] = x[tile - SUBLANES:tile, :]
        xc = pre * _sigmoid(pre)
        q = _blockdiag(xc, wq_ref)
        k = _blockdiag(xc, wk_ref)
        v = _blockdiag(x, wv_ref)
        pre_ref[...] = pre
        q_ref[...] = q
        k_ref[...] = k
        v_ref[...] = v
        g = _mm(q, wif_ref[0:d, :]) + _mm(k, wif_ref[d:2 * d, :]) + _mm(v, wif_ref[2 * d:3 * d, :]) + bif_ref[...]
        lane = _iota(g.shape, 1)
        gl = jnp.where(lane < 4, g, jnp.where(lane < 8, -_softplus(-g), 0.0))
        tri = jnp.where(_iota((ML_CHUNK, ML_CHUNK), 1) <= _iota((ML_CHUNK, ML_CHUNK), 0), 1.0, 0.0)
        cums = [_mm_hi(tri, gl[c * ML_CHUNK:(c + 1) * ML_CHUNK, :]) for c in range(tile // ML_CHUNK)]
        cum = cums[0] if len(cums) == 1 else jnp.concatenate(cums, axis=0)
        g_ref[...] = gl + jnp.where((lane >= 8) & (lane < 12), pltpu.roll(cum, 4, 1), 0.0)

    vec = _full((1, d))
    return _call(
        body, comms, name="mlstm_proj_fwd", grid=(s // tile,),
        in_specs=[pl.BlockSpec((tile, d), lambda i: (i, 2)), _full(conv_w.shape), vec,
                  _full(w_q.shape), _full(w_k.shape), _full(w_v.shape), _full(wif.shape), _full((1, LANES))],
        out_specs=[pl.BlockSpec((tile, d), lambda i: (i, 0))] * 3 + [pl.BlockSpec((tile, LANES), lambda i: (i, 0)),
                                                                     pl.BlockSpec((tile, d), lambda i: (i, 0))],
        out_shape=[jax.ShapeDtypeStruct((s, d), F32)] * 3 + [jax.ShapeDtypeStruct((s, LANES), F32),
                                                             jax.ShapeDtypeStruct((s, d), F32)],
        scratch_shapes=[pltpu.VMEM((SUBLANES, d), F32)],
        args=(u, conv_w, conv_b, w_q, w_k, w_v, wif, bif))


CELL_CHUNKS_PER_STEP = 4
CELL_BWD_CHUNKS_PER_STEP = 2


def _cell_chunk(h, nh, q_ref, k_ref, v_ref, gc, gr, m_prev, c_h, n_h, m_t=None, r0=0):
    lc = ML_CHUNK
    dh = q_ref.shape[1] // nh
    sl = slice(h * dh, (h + 1) * dh)
    qh = q_ref[r0:r0 + lc, sl]
    kh = k_ref[r0:r0 + lc, sl] * (dh ** -0.5)
    vh = v_ref[r0:r0 + lc, sl]
    li_c = _col(gc, h)
    b_c = _col(gc, 8 + h)
    lib_r = _row(gr, h) - _row(gr, 8 + h)
    b_last = _colsum(jnp.where(_iota((lc, 1), 0) == lc - 1, b_c, 0.0))
    causal = _iota((lc, lc), 1) <= _iota((lc, lc), 0)
    dmat = jnp.where(causal, b_c + lib_r, NEG_BIG)
    m_inter = b_c + m_prev
    if m_t is None:
        m_t = jnp.maximum(m_inter, jnp.max(dmat, axis=1, keepdims=True))
    w_intra = jnp.exp(dmat - m_t)
    w_inter = jnp.exp(m_inter - m_t)
    amat = _mm_nt(qh, kh)
    smat = amat * w_intra
    qc = _mm(qh, c_h)
    qn = _rowsum(qh * n_h)
    den = _rowsum(smat) + w_inter * qn
    gst = b_last - b_c + li_c
    m_new = jnp.maximum(b_last + m_prev, jnp.max(gst, axis=0, keepdims=True))
    w_state = jnp.exp(gst - m_new)
    decay = jnp.exp(b_last + m_prev - m_new)
    return dict(sl=sl, qh=qh, kh=kh, vh=vh, m_t=m_t, w_intra=w_intra, w_inter=w_inter, smat=smat, qc=qc, qn=qn,
                den=den, m_new=m_new, w_state=w_state, decay=decay)


def _ml_cell_fwd(q, k, v, gcol, grow, u, ng, nh, comms=None):
    s, d = q.shape
    lc = ML_CHUNK
    nc = s // lc
    dh = d // nh

    per = CELL_CHUNKS_PER_STEP if nc % CELL_CHUNKS_PER_STEP == 0 else 1

    def body(q_ref, k_ref, v_ref, gc_ref, gr_ref, o_ref, z_ref, ng_ref,
             cell_ref, y_ref, cs_ref, ns_ref, ms_ref, mt_ref, c_sc, n_sc, m_sc):
        @pl.when(pl.program_id(0) == 0)
        def _():
            c_sc[...] = jnp.zeros_like(c_sc)
            n_sc[...] = jnp.zeros_like(n_sc)
            m_sc[...] = jnp.zeros_like(m_sc)

        lane = _iota((lc, LANES), 1)
        for cc in range(per):
            rows = slice(cc * lc, (cc + 1) * lc)
            gc = gc_ref[rows, :]
            gr = gr_ref[:, rows]
            mt_acc = jnp.zeros((lc, LANES), F32)
            for h in range(nh):
                c_h = c_sc[h]
                n_h = n_sc[h, 0:1, :]
                m_prev = jnp.max(m_sc[h, 0:1, :], axis=1, keepdims=True)
                cs_ref[cc, h] = c_h
                ns_ref[cc, h] = n_sc[h]
                ms_ref[cc, h] = m_sc[h]
                t = _cell_chunk(h, nh, q_ref, k_ref, v_ref, gc, gr, m_prev, c_h, n_h, r0=cc * lc)
                sl = t["sl"]
                num = _mm(t["smat"], t["vh"]) + t["w_inter"] * t["qc"]
                cell_h = num / jnp.maximum(jnp.abs(t["den"]), jnp.exp(-t["m_t"]))
                mt_acc = jnp.where(lane == h, t["m_t"], mt_acc)
                kw = t["kh"] * t["w_state"]
                c_sc[h] = t["decay"] * c_h + _mm_tn(kw, t["vh"])
                n_sc[h] = _bcast8(t["decay"] * n_h + _colsum(kw))
                m_sc[h] = jnp.broadcast_to(t["m_new"], (SUBLANES, LANES))
                hg = _sigmoid(o_ref[rows, sl]) * cell_h
                hn = hg * lax.rsqrt(jnp.mean(hg * hg, axis=1, keepdims=True) + EPS)
                z = z_ref[rows, sl]
                cell_ref[rows, sl] = cell_h
                y_ref[rows, sl] = (hn * ng_ref[:, sl] * z * _sigmoid(z)).astype(BF16)
            mt_ref[rows, :] = mt_acc

    tok = pl.BlockSpec((per * lc, d), lambda c: (c, 0))
    return _call(
        body, comms, name="mlstm_cell_fwd", grid=(nc // per,),
        in_specs=[tok, tok, tok, pl.BlockSpec((per * lc, LANES), lambda c: (c, 0)),
                  pl.BlockSpec((16, per * lc), lambda c: (0, c)),
                  pl.BlockSpec((per * lc, d), lambda c: (c, 3)), pl.BlockSpec((per * lc, d), lambda c: (c, 4)), _full((1, d))],
        out_specs=[tok, tok, pl.BlockSpec((per, nh, dh, dh), lambda c: (c, 0, 0, 0)),
                   pl.BlockSpec((per, nh, SUBLANES, dh), lambda c: (c, 0, 0, 0)),
                   pl.BlockSpec((per, nh, SUBLANES, LANES), lambda c: (c, 0, 0, 0)),
                   pl.BlockSpec((per * lc, LANES), lambda c: (c, 0))],
        out_shape=[jax.ShapeDtypeStruct((s, d), F32), jax.ShapeDtypeStruct((s, d), BF16),
                   jax.ShapeDtypeStruct((nc, nh, dh, dh), F32), jax.ShapeDtypeStruct((nc, nh, SUBLANES, dh), F32),
                   jax.ShapeDtypeStruct((nc, nh, SUBLANES, LANES), F32), jax.ShapeDtypeStruct((s, LANES), F32)],
        scratch_shapes=[pltpu.VMEM((nh, dh, dh), F32), pltpu.VMEM((nh, SUBLANES, dh), F32),
                        pltpu.VMEM((nh, SUBLANES, LANES), F32)],
        args=(q, k, v, gcol, grow, u, u, ng))


def _out_fwd(x, y_rg, y_ml, gate, w_out_g, layer, tile, comms=None, head=None):
    s, d = x.shape
    nd, _, r, _ = w_out_g.shape

    def body(x_ref, yr_ref, ym_ref, g_ref, w_ref, *rest):
        ycat = jnp.concatenate([yr_ref[...].astype(BF16), ym_ref[...].astype(BF16)], axis=1)
        acc = jnp.dot(ycat, w_ref[...].reshape(nd * r, d), preferred_element_type=F32)
        xn = x_ref[...] + g_ref[...] * acc
        if head is None:
            xn_ref, y_ref = rest
            y_ref[...] = acc
            xn_ref[...] = xn
            return
        fg_ref, t_ref, y_ref, dx_ref, loss_ref, gg_ref = rest
        y_ref[...] = acc

        @pl.when(pl.program_id(0) == 0)
        def _():
            loss_ref[...] = jnp.zeros_like(loss_ref)
            gg_ref[...] = jnp.zeros_like(gg_ref)

        fg = fg_ref[...]
        rs = lax.rsqrt(jnp.mean(xn * xn, axis=1, keepdims=True) + EPS)
        xh = xn * rs
        e = xh * fg - t_ref[...]
        loss_ref[...] += jnp.broadcast_to(_colsum(_rowsum(e * e)) * (0.5 / d), loss_ref.shape)
        dy = e * (1.0 / d)
        gg_ref[...] += _bcast8(_colsum(dy * xh))
        dxh = dy * fg
        dx_ref[...] = rs * (dxh - xh * jnp.mean(dxh * xh, axis=1, keepdims=True))

    tok = pl.BlockSpec((tile, d), lambda i: (i, 0))
    in_specs = [tok, tok, tok, _full((1, d)), pl.BlockSpec((nd, 1, r, d), lambda i: (0, layer, 0, 0))]
    if head is None:
        return _call(body, comms, name="out_proj_fwd", grid=(s // tile,), in_specs=in_specs, out_specs=[tok, tok],
                     out_shape=[jax.ShapeDtypeStruct((s, d), F32)] * 2, args=(x, y_rg, y_ml, gate, w_out_g))
    return _call(
        body, comms, name="out_proj_loss", grid=(s // tile,),
        in_specs=in_specs + [_full((1, d)), tok],
        out_specs=[tok, tok, _full((SUBLANES, LANES)), _full((SUBLANES, d))],
        out_shape=[jax.ShapeDtypeStruct((s, d), F32)] * 2 + [jax.ShapeDtypeStruct((SUBLANES, LANES), F32),
                                                             jax.ShapeDtypeStruct((SUBLANES, d), F32)],
        args=(x, y_rg, y_ml, gate, w_out_g, *head))


def _ml_out_stage_bwd(dy, cell, o, z, ng):
    so = _sigmoid(o)
    hg = so * cell
    rinv = lax.rsqrt(jnp.mean(hg * hg, axis=1, keepdims=True) + EPS)
    hn = hg * rinv
    sz = _sigmoid(z)
    dz = dy * hn * ng * (sz + z * sz * (1.0 - sz))
    dymid = dy * z * sz
    dhn = dymid * ng
    dhg = rinv * (dhn - hn * jnp.mean(dhn * hn, axis=1, keepdims=True))
    return dz, dhg * cell * so * (1.0 - so), dhg * so, _colsum(dymid * hn)


def _out_bwd(dxo, gate, y, y_rg, y_ml, w_out_g, layer, tile, comms=None):
    s, d = dxo.shape
    nd, _, r, _ = w_out_g.shape

    def body(dx_ref, g_ref, y_ref, yr_ref, ym_ref, w_ref, dyr_ref, dym_ref, gw_ref, dg_ref):
        @pl.when(pl.program_id(0) == 0)
        def _():
            gw_ref[...] = jnp.zeros_like(gw_ref)
            dg_ref[...] = jnp.zeros_like(dg_ref)

        dxv = dx_ref[...]
        dg_ref[...] += _bcast8(_colsum(dxv * y_ref[...]))
        dyb = (dxv * g_ref[...]).astype(BF16)
        dycat = lax.dot_general(dyb, w_ref[...].reshape(nd * r, d), (((1,), (1,)), ((), ())), preferred_element_type=F32)
        dyr_ref[...] = dycat[:, 0:d]
        dym_ref[...] = dycat[:, d:2 * d]
        ycat = jnp.concatenate([yr_ref[...].astype(BF16), ym_ref[...].astype(BF16)], axis=1)
        gw_ref[...] += lax.dot_general(ycat, dyb, (((0,), (0,)), ((), ())), preferred_element_type=F32).reshape(nd, r, d)

    tok = pl.BlockSpec((tile, d), lambda i: (i, 0))
    return _call(
        body, comms, name="out_proj_bwd", grid=(s // tile,),
        in_specs=[tok, _full((1, d)), tok, tok, tok, pl.BlockSpec((nd, 1, r, d), lambda i: (0, layer, 0, 0))],
        out_specs=[tok, tok, _full((nd, r, d)), _full((SUBLANES, d))],
        out_shape=[jax.ShapeDtypeStruct((s, d), F32)] * 2 + [jax.ShapeDtypeStruct((nd, r, d), F32),
                                                             jax.ShapeDtypeStruct((SUBLANES, d), F32)],
        args=(dxo, gate, y, y_rg, y_ml, w_out_g))


def _ml_cell_bwd(dy_ml, u, cell, q, k, v, gcol, grow, mt, cs, ns, ms, ng, nh, comms=None):
    s, d = q.shape
    lc = ML_CHUNK
    nc = s // lc
    dh = d // nh

    per = CELL_BWD_CHUNKS_PER_STEP if nc % CELL_BWD_CHUNKS_PER_STEP == 0 else 1

    def body(*refs):
        gng_ref, dc_sc, dn_sc = refs[20], refs[21], refs[22]

        @pl.when(pl.program_id(0) == 0)
        def _():
            dc_sc[...] = jnp.zeros_like(dc_sc)
            dn_sc[...] = jnp.zeros_like(dn_sc)
            gng_ref[...] = jnp.zeros_like(gng_ref)

        for cc in reversed(range(per)):
            rows = slice(cc * lc, (cc + 1) * lc)
            views = [refs[at] if at == 13 else _PartOf(refs[at], cols=rows) if at == 8 else
                     _PartOf(refs[at], lead=cc) if at in (10, 11, 12) else _PartOf(refs[at], rows=rows) for at in range(20)]
            chunk(*views, gng_ref, dc_sc, dn_sc)

    def chunk(dy_ref, o_ref, z_ref, cell_ref, q_ref, k_ref, v_ref, gc_ref, gr_ref, mt_ref, cs_ref, ns_ref, ms_ref,
              ng_ref, dq_ref, dk_ref, dv_ref, dg_ref, do_ref, dz_ref, gng_ref, dc_sc, dn_sc):
        gc = gc_ref[...]
        gr = gr_ref[...]
        mtv = mt_ref[...]
        lane = _iota((lc, LANES), 1)
        rowv = _iota((lc, 1), 0)
        dg_acc = jnp.zeros((lc, LANES), F32)
        for h in range(nh):
            c_h = cs_ref[0, h]
            n_h = ns_ref[0, h, 0:1, :]
            m_prev = jnp.max(ms_ref[0, h, 0:1, :], axis=1, keepdims=True)
            t = _cell_chunk(h, nh, q_ref, k_ref, v_ref, gc, gr, m_prev, c_h, n_h, m_t=_col(mtv, h))
            sl, qh, kh, vh = t["sl"], t["qh"], t["kh"], t["vh"]
            w_intra, w_inter, smat, w_state, decay = t["w_intra"], t["w_inter"], t["smat"], t["w_state"], t["decay"]
            cell_h = cell_ref[:, sl]
            dz, do, dcell, gng = _ml_out_stage_bwd(dy_ref[:, sl], cell_h, o_ref[:, sl], z_ref[:, sl], ng_ref[:, sl])
            dz_ref[:, sl] = dz.astype(BF16)
            do_ref[:, sl] = do.astype(BF16)
            gng_ref[:, sl] += _bcast8(gng)
            eneg = jnp.exp(-t["m_t"])
            aden = jnp.abs(t["den"])
            nst = jnp.maximum(aden, eneg)
            dnum = dcell / nst
            dden = jnp.where(aden > eneg, -_rowsum(cell_h * dcell) / nst * jnp.sign(t["den"]), 0.0)
            pmat = _mm_nt(dnum, vh) + dden
            damat = pmat * w_intra
            gmat = pmat * smat
            wdn = w_inter * dnum
            wdd = w_inter * dden
            dqh = _mm(damat, kh) + _mm_nt(wdn, c_h) + wdd * n_h
            dkh = _mm_tn(damat, qh)
            dvh = _mm_tn(smat, dnum)
            dw_inter = _rowsum(dnum * t["qc"]) + dden * t["qn"]
            dcn = dc_sc[h]
            dnn = dn_sc[h, 0:1, :]
            kw = kh * w_state
            dkw = _mm_nt(vh, dcn) + dnn
            dvh = dvh + _mm(kw, dcn)
            dkh = dkh + dkw * w_state
            dgst = _rowsum(dkw * kh) * w_state
            ddecay = _colsum(_rowsum(dcn * c_h)) + _rowsum(dnn * n_h)
            db_last = _colsum(dgst) + ddecay * decay
            rs_g = _rowsum(gmat)
            cs_g = _rowsum(gmat.T)
            db = rs_g - cs_g + dw_inter * w_inter - dgst + jnp.where(rowv == lc - 1, db_last, 0.0)
            dli = cs_g + dgst
            dc_sc[h] = decay * dcn + _mm_tn(qh, wdn)
            dn_sc[h] = _bcast8(decay * dnn + _colsum(qh * wdd))
            dq_ref[:, sl] = dqh
            dk_ref[:, sl] = dkh * (dh ** -0.5)
            dv_ref[:, sl] = dvh
            dg_acc = jnp.where(lane == h, dli, jnp.where(lane == 4 + h, db, dg_acc))
        dg_ref[...] = dg_acc

    rev = lambda c: nc // per - 1 - c
    tok = pl.BlockSpec((per * lc, d), lambda c: (rev(c), 0))
    g128 = pl.BlockSpec((per * lc, LANES), lambda c: (rev(c), 0))
    return _call(
        body, comms, name="mlstm_cell_bwd", grid=(nc // per,),
        in_specs=[tok, pl.BlockSpec((per * lc, d), lambda c: (rev(c), 3)), pl.BlockSpec((per * lc, d), lambda c: (rev(c), 4)),
                  tok, tok, tok, tok, g128, pl.BlockSpec((16, per * lc), lambda c: (0, rev(c))), g128,
                  pl.BlockSpec((per, nh, dh, dh), lambda c: (rev(c), 0, 0, 0)),
                  pl.BlockSpec((per, nh, SUBLANES, dh), lambda c: (rev(c), 0, 0, 0)),
                  pl.BlockSpec((per, nh, SUBLANES, LANES), lambda c: (rev(c), 0, 0, 0)), _full((1, d))],
        out_specs=[tok, tok, tok, g128, tok, tok, _full((SUBLANES, d))],
        out_shape=[jax.ShapeDtypeStruct((s, d), F32)] * 3 + [jax.ShapeDtypeStruct((s, LANES), F32)]
        + [jax.ShapeDtypeStruct((s, d), BF16)] * 2 + [jax.ShapeDtypeStruct((SUBLANES, d), F32)],
        scratch_shapes=[pltpu.VMEM((nh, dh, dh), F32), pltpu.VMEM((nh, SUBLANES, dh), F32)],
        args=(dy_ml, u, u, cell, q, k, v, gcol, grow, mt, cs, ns, ms, ng))


def _halo_spec(d, tile, nt, col):
    per = tile // SUBLANES
    return pl.BlockSpec((SUBLANES, d), lambda i: (jnp.maximum((nt - 1 - i) * per - 1, 0), col))


def _ml_pre_bwd(dq, dk, dv, dgates, gcol, u, pre, q, k, v, conv_w, w_q, w_k, w_v, wif_t, tile, comms=None):
    s, d = dq.shape
    nt = s // tile
    nh, dh, _ = w_q.shape

    def body(dq_ref, dk_ref, dv_ref, dg_ref, gc_ref, x_ref, pre_ref, q_ref, k_ref, v_ref, cw_ref,
             wq_ref, wk_ref, wv_ref, wift_ref,
             dx_ref, gwq_ref, gwk_ref, gwv_ref, gwif_ref, gbif_ref, gcw_ref, gcb_ref, next8):
        @pl.when(pl.program_id(0) == 0)
        def _():
            next8[...] = jnp.zeros_like(next8)
            for ref in (gwq_ref, gwk_ref, gwv_ref, gwif_ref, gbif_ref, gcw_ref, gcb_ref):
                ref[...] = jnp.zeros_like(ref)

        x = x_ref[...]
        pre = pre_ref[...]
        sg = _sigmoid(pre)
        xc = pre * sg
        dgc = dg_ref[...]
        lane = _iota(dgc.shape, 1)
        utri = jnp.where(_iota((ML_CHUNK, ML_CHUNK), 0) <= _iota((ML_CHUNK, ML_CHUNK), 1), 1.0, 0.0)
        rcs = [_mm_hi(utri, dgc[c * ML_CHUNK:(c + 1) * ML_CHUNK, :]) for c in range(tile // ML_CHUNK)]
        rc = rcs[0] if len(rcs) == 1 else jnp.concatenate(rcs, axis=0)
        dgates_v = jnp.where(lane < 4, dgc, jnp.where(lane < 8, rc * (1.0 - jnp.exp(gc_ref[...])), 0.0))
        dgb = dgates_v.astype(BF16)
        gbif_ref[...] += jnp.broadcast_to(_colsum(dgates_v), gbif_ref.shape)
        ext = jnp.dot(dgb, wift_ref[...], preferred_element_type=F32)
        dqt = dq_ref[...] + ext[:, 0:d]
        dkt = dk_ref[...] + ext[:, d:2 * d]
        dvt = dv_ref[...] + ext[:, 2 * d:3 * d]
        gwif_ref[:, 0:d] += _mm_tn(dgb, q_ref[...])
        gwif_ref[:, d:2 * d] += _mm_tn(dgb, k_ref[...])
        gwif_ref[:, 2 * d:3 * d] += _mm_tn(dgb, v_ref[...])
        dxc_parts, dxv_parts = [], []
        for h in range(nh):
            sl = slice(h * dh, (h + 1) * dh)
            gwq_ref[h] += _mm_tn(xc[:, sl], dqt[:, sl])
            gwk_ref[h] += _mm_tn(xc[:, sl], dkt[:, sl])
            gwv_ref[h] += _mm_tn(x[:, sl], dvt[:, sl])
            dxc_parts.append(_mm_nt(dqt[:, sl], wq_ref[h]) + _mm_nt(dkt[:, sl], wk_ref[h]))
            dxv_parts.append(_mm_nt(dvt[:, sl], wv_ref[h]))
        dxc = jnp.concatenate(dxc_parts, axis=1)
        dxv = jnp.concatenate(dxv_parts, axis=1)
        dpre = dxc * (sg + pre * sg * (1.0 - sg))
        gcb_ref[...] += _bcast8(_colsum(dpre))
        dx_ref[...] = (dxv + _conv_bwd(dpre, x, next8[...], cw_ref, gcw_ref)).astype(BF16)
        next8[...] = dpre[0:SUBLANES, :]

    rev = lambda i: nt - 1 - i
    tok = pl.BlockSpec((tile, d), lambda i: (rev(i), 0))
    g128 = pl.BlockSpec((tile, LANES), lambda i: (rev(i), 0))
    wsh = (nh, dh, dh)
    return _call(
        body, comms, name="mlstm_proj_bwd", grid=(nt,),
        in_specs=[tok, tok, tok, g128, g128, pl.BlockSpec((tile, d), lambda i: (rev(i), 2)), tok,
                  tok, tok, tok, _full(conv_w.shape), _full(wsh), _full(wsh), _full(wsh), _full(wif_t.shape)],
        out_specs=[tok, _full(wsh), _full(wsh), _full(wsh), _full((LANES, 3 * d)), _full((SUBLANES, LANES)),
                   _full((SUBLANES, d)), _full((SUBLANES, d))],
        out_shape=[jax.ShapeDtypeStruct((s, d), BF16)] + [jax.ShapeDtypeStruct(wsh, F32)] * 3
        + [jax.ShapeDtypeStruct((LANES, 3 * d), F32), jax.ShapeDtypeStruct((SUBLANES, LANES), F32),
           jax.ShapeDtypeStruct((SUBLANES, d), F32), jax.ShapeDtypeStruct((SUBLANES, d), F32)],
        scratch_shapes=[pltpu.VMEM((SUBLANES, d), F32)],
        args=(dq, dk, dv, dgates, gcol, u, pre, q, k, v, conv_w, w_q, w_k, w_v, wif_t))


def _rg_bwd(dy_rg, u, h_rg, gates, conv_w, w_a, w_x, lam, tile, comms=None):
    s, d = dy_rg.shape
    nt = s // tile
    nh, dh, _ = w_a.shape

    def body(dy_ref, x_ref, z_ref, h_ref, hhalo_ref, xc_ref, r_ref, i_ref, a_ref, beta_ref, cw_ref, wa_ref,
             wx_ref, lam_ref,
             dx_ref, dz_ref, gwa_ref, gwx_ref, gba_ref, gbx_ref, glam_ref, gcw_ref, gcb_ref, next8, anext, dnext, dbuf):
        i = pl.program_id(0)

        @pl.when(i == 0)
        def _():
            for ref in (next8, anext, dnext, gwa_ref, gwx_ref, gba_ref, gbx_ref, glam_ref, gcw_ref, gcb_ref):
                ref[...] = jnp.zeros_like(ref)

        inner = jnp.where(i < nt - 1, 1.0, 0.0)
        xc, r, ig, a, beta = xc_ref[...], r_ref[...], i_ref[...], a_ref[...], beta_ref[...]
        sp = _softplus(-lam_ref[...])
        h = h_ref[...]
        row = _iota(h.shape, 0)
        hprev = jnp.where(row >= 1, pltpu.roll(h, 1, 0), hhalo_ref[SUBLANES - 1:SUBLANES, :] * inner)
        z = z_ref[...]
        sz = _sigmoid(z)
        dyv = dy_ref[...]
        dz_ref[...] = (dyv * h * (sz + z * sz * (1.0 - sz))).astype(BF16)
        a_up = jnp.where(row < tile - 1, pltpu.roll(a, tile - 1, 0), anext[0:1, :])
        _scan_into(a_up, dyv * z * sz, dnext[0:1, :], dbuf, True)
        delta = dbuf[...]
        anext[...] = a[0:SUBLANES, :]
        dnext[...] = delta[0:SUBLANES, :]
        dla = delta * hprev * a - delta * ig * xc * (a * a / beta)
        glam_ref[...] += _bcast8(_colsum(dla * r) * (RG_C * _sigmoid(-lam_ref[...])))
        dpa = dla * (-RG_C * sp) * r * (1.0 - r)
        dpx = delta * beta * xc * ig * (1.0 - ig)
        gba_ref[...] += _bcast8(_colsum(dpa))
        gbx_ref[...] += _bcast8(_colsum(dpx))
        parts = []
        for hh in range(nh):
            sl = slice(hh * dh, (hh + 1) * dh)
            gwa_ref[hh] += _mm_tn(xc[:, sl], dpa[:, sl])
            gwx_ref[hh] += _mm_tn(xc[:, sl], dpx[:, sl])
            parts.append(_mm_nt(dpa[:, sl], wa_ref[hh]) + _mm_nt(dpx[:, sl], wx_ref[hh]))
        dxc = delta * beta * ig + jnp.concatenate(parts, axis=1)
        gcb_ref[...] += _bcast8(_colsum(dxc))
        dx_ref[...] = _conv_bwd(dxc, x_ref[...], next8[...], cw_ref, gcw_ref).astype(BF16)
        next8[...] = dxc[0:SUBLANES, :]

    rev = lambda i: nt - 1 - i
    tok = pl.BlockSpec((tile, d), lambda i: (rev(i), 0))
    vec = _full((1, d))
    acc = _full((SUBLANES, d))
    wsh = (nh, dh, dh)
    return _call(
        body, comms, name="rglru_bwd", grid=(nt,),
        in_specs=[tok, tok, pl.BlockSpec((tile, d), lambda i: (rev(i), 1)), tok,
                  _halo_spec(d, tile, nt, 0)] + [tok] * 5 + [_full(conv_w.shape), _full(wsh), _full(wsh), vec],
        out_specs=[tok, tok, _full(wsh), _full(wsh), acc, acc, acc, acc, acc],
        out_shape=[jax.ShapeDtypeStruct((s, d), BF16)] * 2 + [jax.ShapeDtypeStruct(wsh, F32)] * 2
        + [jax.ShapeDtypeStruct((SUBLANES, d), F32)] * 5,
        scratch_shapes=[pltpu.VMEM((SUBLANES, d), F32)] * 3 + [pltpu.VMEM((tile, d), F32)],
        args=(dy_rg, u, u, h_rg, h_rg, *gates, conv_w, w_a, w_x, lam))


def _segments(d, w, n_pieces, n_slots):
    bounds = sorted({k * d for k in range(n_pieces + 1)} | {j * w for j in range(n_slots + 1)})
    return [(lo // d, lo % d, lo // w, lo % w, hi - lo) for lo, hi in zip(bounds[:-1], bounds[1:])]


def _in_bwd(pieces, x, dxo, ng, scale, w_in_g, layer, tile, comms=None, tiles=None, prev=None):
    s, d = x.shape
    nd, _, _, w = w_in_g.shape
    first, count = tiles or (0, s // tile)
    n_p = len(pieces)

    def body(*refs):
        p_refs = refs[:n_p]
        x_ref, dxo_ref, ng_ref, sc_ref, w_ref = refs[n_p:n_p + 5]
        dx_ref, dsc_ref, dsh_ref, gng_ref, wcat = refs[-5:]
        _join_columns(w_ref, wcat)

        @pl.when(pl.program_id(0) == 0)
        def _():
            for k, ref in enumerate((dsc_ref, dsh_ref, gng_ref)):
                ref[...] = jnp.zeros_like(ref) if prev is None else refs[n_p + 6 + k][...]

        du = jnp.concatenate([p[...] for p in p_refs], axis=1)
        dh = lax.dot_general(du, wcat[...], (((1,), (1,)), ((), ())), preferred_element_type=F32)
        xv = x_ref[...]
        g = ng_ref[...]
        rs = lax.rsqrt(jnp.mean(xv * xv, axis=1, keepdims=True) + EPS)
        xh = xv * rs
        dsh_ref[...] += _bcast8(_colsum(dh))
        dsc_ref[...] += _bcast8(_colsum(dh * xh * g))
        dhn = dh * (1.0 + sc_ref[...])
        gng_ref[...] += _bcast8(_colsum(dhn * xh))
        dxh = dhn * g
        dx_ref[...] = dxo_ref[...] + rs * (dxh - xh * jnp.mean(dxh * xh, axis=1, keepdims=True))

    tok = pl.BlockSpec((tile, d), lambda i: (i + first, 0))
    vec = _full((1, d))
    acc = _full((SUBLANES, d))
    more_specs = [] if prev is None else [pl.BlockSpec(memory_space=pl.ANY), acc, acc, acc]
    return _call(
        body, comms, name="in_proj_bwd_x", grid=(count,),
        in_specs=[tok] * n_p + [tok, tok, vec, vec, pl.BlockSpec((nd, 1, d, w), lambda i: (0, layer, 0, 0),
                                                               pipeline_mode=pl.Buffered(1))] + more_specs,
        out_specs=[tok, acc, acc, acc],
        out_shape=[jax.ShapeDtypeStruct((s, d), F32)] + [jax.ShapeDtypeStruct((SUBLANES, d), F32)] * 3,
        scratch_shapes=[pltpu.VMEM((d, nd * w), BF16)],
        args=(*pieces, x, dxo, ng, scale, w_in_g) + (() if prev is None else tuple(prev)),
        aliases={} if prev is None else {n_p + 5: 0})


def _in_bwd_w(pieces, hbf, w, slots, tile, comms=None):
    s, d = hbf.shape
    nd_all = len(pieces) * d // w
    segs = [sg for sg in _segments(d, w, len(pieces), nd_all) if sg[2] in slots]

    def body(*refs):
        p_refs = refs[:len(pieces)]
        h_ref, gw_ref = refs[len(pieces):]

        @pl.when(pl.program_id(0) == 0)
        def _():
            gw_ref[...] = jnp.zeros_like(gw_ref)

        hv = h_ref[...]
        for (kk, a, j, b, width) in segs:
            gw_ref[j - slots[0], :, b:b + width] += _mm_tn(hv, p_refs[kk][:, a:a + width])

    tok = pl.BlockSpec((tile, d), lambda i: (i, 0))
    return _call(
        body, comms, name="in_proj_bwd_w", grid=(s // tile,),
        in_specs=[tok] * len(pieces) + [tok],
        out_specs=[pl.BlockSpec((len(slots), d, w), lambda i: (0, 0, 0), pipeline_mode=pl.Buffered(1))],
        out_shape=[jax.ShapeDtypeStruct((len(slots), d, w), F32)],
        args=(*pieces, hbf))[0]


def _exchange(arrs, gather, name):
    return _run_comms([_exchange_comm(arrs, gather)], name)[0]


def _run_comms(comms, name):
    _call(lambda: None, comms, name=name, grid=(1,), in_specs=[], out_specs=[], out_shape=[], args=())
    return [cm.results for cm in comms]


def _exchange_comm(arrs, gather):
    n = len(arrs)
    per = N_DEV - 1

    def copies(ins, outs, sems):
        send_sems, recv_sems, local_sems = sems
        x, y, c = (lax.axis_index(ax) for ax in MESH_AXES)
        me = 4 * x + 2 * y + c
        sends, recvs = [], []
        for flip in range(1, N_DEV):
            px = x ^ ((flip >> 2) & 1)
            py = y ^ ((flip >> 1) & 1)
            pc = c ^ (flip & 1)
            peer = 4 * px + 2 * py + pc
            for kk in range(n):
                src = ins[kk] if gather else ins[kk].at[peer]
                sends.append(_remote(src, outs[kk].at[me], send_sems, recv_sems, kk * per + flip - 1, (px, py, pc)))
                recvs.append(_remote(src, outs[kk].at[peer], send_sems, recv_sems, kk * per + flip - 1, (px, py, pc)))
        local = [pltpu.make_async_copy(ins[kk] if gather else ins[kk].at[me], outs[kk].at[me], local_sems.at[kk])
                 for kk in range(n)]
        return local, sends, recvs

    def start(ins, outs, sems):
        local, sends, _ = copies(ins, outs, sems)
        for cp in sends + local:
            cp.start()

    def finish(ins, outs, sems):
        local, sends, recvs = copies(ins, outs, sems)
        for cp in recvs:
            cp.wait_recv()
        for cp in sends:
            cp.wait_send()
        for cp in local:
            cp.wait()

    return _Comm(arrs, [jax.ShapeDtypeStruct((N_DEV,) + a.shape if gather else a.shape, a.dtype) for a in arrs],
                 [pltpu.SemaphoreType.DMA((n * per,)), pltpu.SemaphoreType.DMA((n * per,)), pltpu.SemaphoreType.DMA((n,))],
                 start, finish)


def _mesh_place():
    x, y, c = (lax.axis_index(ax) for ax in MESH_AXES)
    return x, y, c, (x, y, 1 - c), [(1 - x, y), (x, 1 - y), (1 - x, 1 - y)]


def _remote(src, dst, send_sems, recv_sems, sem, to):
    return pltpu.make_async_remote_copy(src_ref=src, dst_ref=dst, send_sem=send_sems.at[sem], recv_sem=recv_sems.at[sem],
                                        device_id=to, device_id_type=pl.DeviceIdType.MESH)


N_CHIPS = N_DEV // 2


def _pair_sum(a, other, parity, name):
    _, r, c = a.shape
    tr = _row_tile(r, c, 3)

    def body(p_ref, a_ref, o_ref, s_ref):
        s_ref[...] = (a_ref[...] + o_ref[...]).astype(BF16)

    return pl.pallas_call(
        body, name=name,
        grid_spec=pltpu.PrefetchScalarGridSpec(
            num_scalar_prefetch=1, grid=(N_CHIPS, r // tr),
            in_specs=[pl.BlockSpec((1, tr, c), lambda q, i, p: (2 * q + p[0], i, 0)),
                      pl.BlockSpec((1, tr, c), lambda q, i, p: (q, i, 0))],
            out_specs=pl.BlockSpec((1, tr, c), lambda q, i, p: (q, i, 0))),
        out_shape=jax.ShapeDtypeStruct((N_CHIPS, r, c), BF16),
        compiler_params=_params(2),
    )(parity, a, other)


def _adam_math(w, g, m, v):
    m = ADAM_B1 * m + (1.0 - ADAM_B1) * g
    v = ADAM_B2 * v + (1.0 - ADAM_B2) * (g * g)
    m_hat = m / (1.0 - ADAM_B1 ** ADAM_STEP)
    v_hat = v / (1.0 - ADAM_B2 ** ADAM_STEP)
    delta = -ADAM_LR * (m_hat / (jnp.sqrt(v_hat) + ADAM_EPS) + ADAM_WD * w)
    return delta, m, v


def _sum_devices(r_ref):
    acc = r_ref[0].astype(F32)
    for p in range(1, r_ref.shape[0]):
        acc = acc + r_ref[p].astype(F32)
    return acc


def _row_tile(rows, cols, n_bufs):
    budget = 24 * 1024 * 1024 // (n_bufs * 2 * cols * 4)
    t = rows
    while t > budget and t % 2 == 0 and (t // 2) % SUBLANES == 0:
        t //= 2
    return t


def _reduce_adam(recvs, w, m, v, name, comms=None):
    nl, r, c = w.shape
    n_part = recvs[0].shape[0]
    tr = _row_tile(r, c, n_part * nl + 7)
    nt = r // tr

    def body(*refs):
        r_refs = refs[:nl]
        w_ref, m_ref, v_ref, g_ref, d_ref, mo_ref, vo_ref = refs[nl:]
        layer = pl.program_id(0) // nt
        g = _sum_devices(r_refs[0])
        for ll in range(1, nl):
            g = jnp.where(layer == ll, _sum_devices(r_refs[ll]), g)
        delta, m2, v2 = _adam_math(w_ref[0], g, m_ref[0], v_ref[0])
        g_ref[0] = g
        d_ref[0] = delta
        mo_ref[0] = m2
        vo_ref[0] = v2

    def rspec(ll):
        return pl.BlockSpec((n_part, tr, c),
                            lambda i: (0, jnp.where(i // nt == ll, i % nt, jnp.where(i // nt < ll, 0, nt - 1)), 0))

    blk = pl.BlockSpec((1, tr, c), lambda i: (i // nt, i % nt, 0))
    return _call(
        body, comms, name=name, grid=(nl * nt,),
        in_specs=[rspec(ll) for ll in range(nl)] + [blk, blk, blk],
        out_specs=[blk] * 4,
        out_shape=[jax.ShapeDtypeStruct((nl, r, c), F32)] * 4,
        args=(*recvs, w, m, v))


def _tile_for(s, want):
    return min(want, s)


REPLICATED = ("norm_g", "b_ada", "rg_conv_b", "rg_w_a", "rg_b_a", "rg_w_x", "rg_b_x", "rg_lambda", "ml_conv_b",
              "ml_b_if", "ml_norm_g", "final_g")


def _small_pack(rg_conv_w, ml_conv_w, ml_w_if):
    nl = rg_conv_w.shape[0]
    wif_t = jnp.swapaxes(ml_w_if, 1, 2).reshape(nl, -1, LANES)
    return jnp.concatenate([rg_conv_w, ml_conv_w, wif_t], axis=1)


def _small_unpack(p, if_rows):
    nl = p.shape[0]
    rg_cw = p[:, 0:CONV_WIDTH]
    ml_cw = p[:, CONV_WIDTH:2 * CONV_WIDTH]
    wif = jnp.swapaxes(p[:, 2 * CONV_WIDTH:].reshape(nl, 8, if_rows), 1, 2)
    return rg_cw, ml_cw, wif


def _qkv_slots(g_qkv, nd):
    three, nh, dh, _ = g_qkv.shape
    return g_qkv.reshape(three, nh, nd, dh // nd, dh).transpose(2, 0, 1, 3, 4).reshape(nd, three * nh * (dh // nd), dh)


def _small_slots(g):
    nd = N_DEV
    cw = jnp.stack([g["rg_conv_w"], g["ml_conv_w"]]).reshape(2, CONV_WIDTH, nd, LANES).transpose(2, 0, 1, 3)
    cw = cw.reshape(nd, 2 * CONV_WIDTH, LANES)
    wif = g["wif_t"].reshape(8, nd, -1).transpose(1, 0, 2).reshape(nd, -1, LANES)
    return jnp.concatenate([cw, wif], axis=1)


def _slot(block):
    return 4 * block[0] + 2 * block[1] + block[2]


def _dma_sems(*counts):
    return [pltpu.SemaphoreType.DMA((n,)) for n in counts]


def _start_all(copies):
    for cp in copies:
        cp.start()


def _gather_ici_comm(arrs):
    n = len(arrs)

    def copies(ins, outs, sems):
        send_sems, recv_sems, local_sems = sems
        x, y, c, sibling, chips = _mesh_place()
        me = (x, y, c)
        peers = [(*chip, c) for chip in chips] + [sibling]
        local = [pltpu.make_async_copy(ins[kk], outs[kk].at[_slot(me)], local_sems.at[kk]) for kk in range(n)]
        sends = [_remote(ins[kk], outs[kk].at[_slot(me)], send_sems, recv_sems, kk * 4 + j, peer)
                 for j, peer in enumerate(peers) for kk in range(n)]
        recvs = [_remote(ins[kk], outs[kk].at[_slot(peer)], send_sems, recv_sems, kk * 4 + j, peer)
                 for j, peer in enumerate(peers) for kk in range(n)]
        return local, sends, recvs

    def start(ins, outs, sems):
        local, sends, _ = copies(ins, outs, sems)
        _start_all(sends + local)

    def finish(ins, outs, sems):
        local, sends, recvs = copies(ins, outs, sems)
        for cp in recvs:
            cp.wait_recv()
        for cp in sends:
            cp.wait_send()
        for cp in local:
            cp.wait()

    return _Comm(arrs, [jax.ShapeDtypeStruct((N_DEV,) + a.shape, a.dtype) for a in arrs], _dma_sems(4 * n, 4 * n, n),
                 start, finish)


def _gather_fwd_comm(bufs):
    n = len(bufs)

    def copies(ins, outs, sems):
        send_sems, recv_sems = sems
        _, _, c, sibling, chips = _mesh_place()
        sends = [_remote(ins[kk].at[_slot((*chip, c))], outs[kk].at[_slot((*chip, c))], send_sems, recv_sems, kk * 3 + j, sibling)
                 for j, chip in enumerate(chips) for kk in range(n)]
        recvs = [_remote(ins[kk].at[_slot((*chip, c))], outs[kk].at[_slot((*chip, 1 - c))], send_sems, recv_sems, kk * 3 + j, sibling)
                 for j, chip in enumerate(chips) for kk in range(n)]
        return sends, recvs

    def start(ins, outs, sems):
        _start_all(copies(ins, outs, sems)[0])

    def finish(ins, outs, sems):
        sends, recvs = copies(ins, outs, sems)
        for cp in recvs:
            cp.wait_recv()
        for cp in sends:
            cp.wait_send()

    return _Comm(bufs, [jax.ShapeDtypeStruct(a.shape, a.dtype) for a in bufs], _dma_sems(3 * n, 3 * n), start, finish,
                 aliases=[(i, i) for i in range(n)])


def _core_swap_comm(arrs):
    n = len(arrs)

    def copies(ins, outs, sems):
        send_sems, recv_sems = sems
        _, _, c, sibling, _ = _mesh_place()
        return [_remote(ins[kk].at[2 * q + (1 - c)], outs[kk].at[q], send_sems, recv_sems, kk * N_CHIPS + q, sibling)
                for q in range(N_CHIPS) for kk in range(n)]

    def start(ins, outs, sems):
        _start_all(copies(ins, outs, sems))

    def finish(ins, outs, sems):
        cps = copies(ins, outs, sems)
        for cp in cps:
            cp.wait_recv()
        for cp in cps:
            cp.wait_send()

    return _Comm(arrs, [jax.ShapeDtypeStruct((N_CHIPS,) + a.shape[1:], a.dtype) for a in arrs],
                 _dma_sems(N_CHIPS * n, N_CHIPS * n), start, finish)


def _chip_swap_comm(arrs):
    n = len(arrs)
    per = N_CHIPS - 1

    def copies(ins, outs, sems):
        send_sems, recv_sems, local_sems = sems
        x, y, c, _, chips = _mesh_place()
        mine = 2 * x + y
        sends = [_remote(ins[kk].at[2 * chip[0] + chip[1]], outs[kk].at[mine], send_sems, recv_sems, kk * per + j, (*chip, c))
                 for j, chip in enumerate(chips) for kk in range(n)]
        recvs = [_remote(ins[kk].at[mine], outs[kk].at[2 * chip[0] + chip[1]], send_sems, recv_sems, kk * per + j, (*chip, c))
                 for j, chip in enumerate(chips) for kk in range(n)]
        local = [pltpu.make_async_copy(ins[kk].at[mine], outs[kk].at[mine], local_sems.at[kk]) for kk in range(n)]
        return local, sends, recvs

    def start(ins, outs, sems):
        local, sends, _ = copies(ins, outs, sems)
        _start_all(sends + local)

    def finish(ins, outs, sems):
        local, sends, recvs = copies(ins, outs, sems)
        for cp in recvs:
            cp.wait_recv()
        for cp in sends:
            cp.wait_send()
        for cp in local:
            cp.wait()

    return _Comm(arrs, [jax.ShapeDtypeStruct(a.shape, a.dtype) for a in arrs], _dma_sems(per * n, per * n, n), start, finish)


def _ada_mod(c_all, w_ada, b_cols, comms=None):
    nl, d, w = w_ada.shape

    def body(c_ref, w_ref, b_ref, m_ref, ca_ref):
        sub = _iota((SUBLANES, d), 0)
        cv = jnp.zeros((SUBLANES, d), F32)
        for b in range(N_DEV):
            cv = jnp.where(sub == b, c_ref[b], cv)
        ca = cv * _sigmoid(cv)
        ca_ref[...] = ca
        m_ref[...] = jnp.zeros_like(m_ref)
        for l in range(nl):
            ml = _mm_hi(ca, w_ref[l]) + b_ref[l:l + 1, :]
            for b in range(N_DEV):
                m_ref[b, l:l + 1, :] = _row(ml, b)

    return _call(
        body, comms, name="adaln_mod_columns", grid=(1,),
        in_specs=[_full(c_all.shape), _full(w_ada.shape), _full(b_cols.shape)],
        out_specs=[_full((N_DEV, SUBLANES, w)), _full((SUBLANES, d))],
        out_shape=[jax.ShapeDtypeStruct((N_DEV, SUBLANES, w), F32), jax.ShapeDtypeStruct((SUBLANES, d), F32)],
        args=(c_all, w_ada, b_cols))


def _ada_grad_adam(cact_t, dmods, w, m, v, comms=None):
    nl, d, wd = w.shape
    tr = _row_tile(d, wd, 8)
    nt = d // tr

    def body(c_ref, dm_ref, w_ref, m_ref, v_ref, g_ref, d_ref, mo_ref, vo_ref):
        cv = c_ref[...]
        dm = dm_ref[0]
        g = _col(cv, 0) * _row(dm, 0)
        for b in range(1, N_DEV):
            g = g + _col(cv, b) * _row(dm, b)
        delta, m2, v2 = _adam_math(w_ref[0], g, m_ref[0], v_ref[0])
        g_ref[0] = g
        d_ref[0] = delta
        mo_ref[0] = m2
        vo_ref[0] = v2

    blk = pl.BlockSpec((1, tr, wd), lambda i: (i // nt, i % nt, 0))
    return _call(
        body, comms, name="adaln_grad_adam", grid=(nl * nt,),
        in_specs=[pl.BlockSpec((tr, N_DEV), lambda i: (i % nt, 0)), pl.BlockSpec((1, N_DEV, wd), lambda i: (i // nt, 0, 0)),
                  blk, blk, blk],
        out_specs=[blk] * 4, out_shape=[jax.ShapeDtypeStruct((nl, d, wd), F32)] * 4,
        args=(cact_t, dmods, w, m, v))


REP_ROWS = ("norm_g", "dshift", "dscale", "dgate", "rg_conv_b", "rg_b_a", "rg_b_x", "rg_lambda", "ml_conv_b", "ml_norm_g",
            "ml_b_if")


def _sum_parts(recvs, name):
    def body(*refs):
        for r_ref, o_ref in zip(refs[:len(recvs)], refs[len(recvs):]):
            o_ref[...] = _sum_devices(r_ref).astype(o_ref.dtype)

    return pl.pallas_call(
        body, name=name, grid=(1,),
        in_specs=[_full(r.shape) for r in recvs], out_specs=[_full(r.shape[1:]) for r in recvs],
        out_shape=[jax.ShapeDtypeStruct(r.shape[1:], r.dtype) for r in recvs], compiler_params=_params(1),
    )(*recvs)


def _adam_replicated(vp, mp, params, nl):
    d = vp.shape[2]
    nr = len(REP_ROWS)
    names = list(params)
    mat_shape = params["rg_w_a"][0].shape[1:]
    mat_rows = mp.shape[0] // (2 * nl)

    def pieces(name):
        if name == "final_g":
            return [(lambda vp_ref, mp_ref: vp_ref[nl * nr:nl * nr + 1, :], (slice(0, 1), slice(None)))]
        out = []
        for l in range(nl):
            if name in ("rg_w_a", "rg_w_x"):
                at = (2 * l + (name == "rg_w_x")) * mat_rows
                out.append((lambda vp_ref, mp_ref, at=at: mp_ref[at:at + mat_rows, :].astype(F32).reshape(mat_shape), l))
            elif name == "b_ada":
                for j in range(3):
                    r = l * nr + 1 + j
                    out.append((lambda vp_ref, mp_ref, r=r: vp_ref[r:r + 1, :], (slice(l, l + 1), slice(j * d, (j + 1) * d))))
            else:
                r = l * nr + REP_ROWS.index(name)
                cols = slice(0, LANES) if name == "ml_b_if" else slice(None)
                out.append((lambda vp_ref, mp_ref, r=r, cols=cols: vp_ref[r:r + 1, cols], (slice(l, l + 1), slice(None))))
        return out

    def body(*refs):
        parts_ref, mp_ref, vp_ref = refs[0], refs[1], refs[-1]
        ins, outs = refs[2:2 + 3 * len(names)], refs[2 + 3 * len(names):-1]
        vp_ref[...] = _sum_devices(parts_ref)
        for pi, name in enumerate(names):
            w_ref, m_ref, v_ref = ins[3 * pi:3 * pi + 3]
            g_ref, d_ref, mo_ref, vo_ref = outs[4 * pi:4 * pi + 4]
            for get, idx in pieces(name):
                g = get(vp_ref, mp_ref)
                delta, m2, v2 = _adam_math(w_ref[idx], g, m_ref[idx], v_ref[idx])
                g_ref[idx] = g
                d_ref[idx] = delta
                mo_ref[idx] = m2
                vo_ref[idx] = v2

    flat = [a for name in names for a in params[name]]
    out_shape = [jax.ShapeDtypeStruct(params[name][0].shape, F32) for name in names for _ in range(4)]
    out_shape.append(jax.ShapeDtypeStruct(vp.shape[1:], F32))
    res = pl.pallas_call(
        body, name="adam_replicated", grid=(1,),
        in_specs=[_full(vp.shape), _full(mp.shape)] + [_full(a.shape) for a in flat],
        out_specs=[_full(o.shape) for o in out_shape], out_shape=out_shape, compiler_params=_params(1),
    )(vp, mp, *flat)
    return {name: res[4 * pi:4 * pi + 4] for pi, name in enumerate(names)}, res[-1]


class _Plan:
    def __init__(self):
        self.hosted, self.after = {}, {}

    def host(self, key, comm, then=None):
        self.hosted.setdefault(key, []).append(comm)
        if then is not None:
            self.after.setdefault(key, []).append(then)

    def comms(self, key):
        return self.hosted.pop(key, None)

    def done(self, key):
        for fn in self.after.pop(key, []):
            fn()

    def flush(self):
        while self.hosted:
            key = next(iter(self.hosted))
            _call(lambda: None, self.comms(key), name="exchange_after_%s_%d" % key, grid=(1,), in_specs=[], out_specs=[],
                  out_shape=[], args=())
            self.done(key)


VEC_TABLE = ("norm_g", "rg_conv_b", "rg_b_a", "rg_b_x", "rg_lambda", "ml_conv_b", "ml_norm_g")


def _vec_table(rep):
    rows = [rep[n] for n in VEC_TABLE]
    return jnp.stack(rows + [jnp.zeros_like(rows[0])] * (SUBLANES - len(rows)), axis=1)


def _layer_fwd(l, xl, mod3, wl, rep, plan, head=None):
    s, d = xl.shape
    t_big, t_mid = _tile_for(s, 512), _tile_for(s, 256)
    nh_ml = rep["ml_b_if"].shape[1] // 2
    vec = lambda name: _vec(rep["vecs"], l, VEC_TABLE.index(name))
    shift, scale, gate = (_vec(mod3, l, kk) for kk in range(3))
    hosted = lambda name: plan.comms((name, l)) if plan else None
    done = lambda name: plan.done((name, l)) if plan else None
    u, hbf = _in_fwd(xl, vec("norm_g"), scale, shift, wl["w_in_g"], 0, t_big, hosted("in_proj_fwd"))
    done("in_proj_fwd")
    h_rg, y_rg, *rg_gates = _rg_fwd(u, d, wl["rg_conv_w"], vec("rg_conv_b"), rep["rg_w_a_bf"][l], vec("rg_b_a"),
                                    rep["rg_w_x_bf"][l], vec("rg_b_x"), vec("rg_lambda"), t_mid, hosted("rglru_fwd"))
    done("rglru_fwd")
    q, k, v, gcol, pre = _ml_pre(u, d, wl["ml_conv_w"], vec("ml_conv_b"), wl["w_qkv"][0], wl["w_qkv"][1],
                                 wl["w_qkv"][2], wl["wif_pad"], wl["bif_pad"], t_mid, hosted("mlstm_proj_fwd"))
    done("mlstm_proj_fwd")
    grow = gcol[:, 0:16].T
    cell, y_ml, cs, ns, ms, mt = _ml_cell_fwd(q, k, v, gcol, grow, u, vec("ml_norm_g"), nh_ml, hosted("mlstm_cell_fwd"))
    done("mlstm_cell_fwd")
    res = _out_fwd(xl, y_rg, y_ml, gate, wl["w_out_g"], 0, t_big, hosted("out_proj_fwd"), head)
    done("out_proj_fwd")
    x_new, y = (res[0], res[1]) if head is None else (tuple(res[1:]), res[0])
    saved = dict(x=xl, u=u, hbf=hbf, h_rg=h_rg, y_rg=y_rg, q=q, k=k, v=v, gcol=gcol, grow=grow, cell=cell, y_ml=y_ml,
                 cs=cs, ns=ns, ms=ms, mt=mt, y=y, scale=scale, gate=gate, rg_gates=rg_gates, pre=pre)
    return x_new, saved


def _layer_bwd(l, dx, sv, wl, rep, plan, grads=None, split_last=False):
    s, d = dx.shape
    t_big, t_mid = _tile_for(s, 512), _tile_for(s, 256)
    nh_ml = rep["ml_b_if"].shape[1] // 2
    nd, _, _, w_cols = wl["w_in_g"].shape
    grads = {} if grads is None else grads
    vec = lambda name: _vec(rep["vecs"], l, VEC_TABLE.index(name))
    hosted = lambda name: plan.comms((name, l)) if plan else None
    done = lambda name: plan.done((name, l)) if plan else None
    dy_rg, dy_ml, gw_out, dgate = _out_bwd(dx, sv["gate"], sv["y"], sv["y_rg"], sv["y_ml"], wl["w_out_g"], 0, t_big,
                                           hosted("out_proj_bwd"))
    grads.update(w_out=gw_out)
    done("out_proj_bwd")
    dq, dk, dv, dgates, d_mlo, d_mlz, g_mlng = _ml_cell_bwd(
        dy_ml, sv["u"], sv["cell"], sv["q"], sv["k"], sv["v"], sv["gcol"], sv["grow"], sv["mt"], sv["cs"], sv["ns"],
        sv["ms"], vec("ml_norm_g"), nh_ml, hosted("mlstm_cell_bwd"))
    done("mlstm_cell_bwd")
    d_mlx, g_wq, g_wk, g_wv, g_wift, g_bif, g_mlcw, g_mlcb = _ml_pre_bwd(
        dq, dk, dv, dgates, sv["gcol"], sv["u"], sv["pre"], sv["q"], sv["k"], sv["v"], wl["ml_conv_w"],
        wl["w_qkv"][0], wl["w_qkv"][1], wl["w_qkv"][2], wl["wift_pad"], t_mid, hosted("mlstm_proj_bwd"))
    done("mlstm_proj_bwd")
    d_rgx, d_rgz, g_wa, g_wx, g_ba, g_bx, g_lam, g_rgcw, g_rgcb = _rg_bwd(
        dy_rg, sv["u"], sv["h_rg"], sv["rg_gates"], wl["rg_conv_w"], rep["rg_w_a_bf"][l], rep["rg_w_x_bf"][l],
        vec("rg_lambda"), t_mid, hosted("rglru_bwd"))
    grads.update(w_qkv=jnp.stack([g_wq, g_wk, g_wv]), rg_conv_w=g_rgcw[0:CONV_WIDTH], ml_conv_w=g_mlcw[0:CONV_WIDTH],
                 wif_t=g_wift[0:8], rg_w_a=g_wa, rg_w_x=g_wx)
    acc = dict(dgate=dgate, rg_conv_b=g_rgcb, rg_b_a=g_ba, rg_b_x=g_bx, rg_lambda=g_lam, ml_conv_b=g_mlcb,
               ml_b_if=g_bif, ml_norm_g=g_mlng)
    done("rglru_bwd")
    pieces = [d_rgx, d_rgz, d_mlx, d_mlo, d_mlz]
    grads.update(w_in=_in_bwd_w(pieces, sv["hbf"], w_cols, tuple(range(nd)), _tile_for(s, 1024), hosted("in_proj_bwd_w")))
    done("in_proj_bwd_w")
    n_tiles = s // t_mid
    counts = [n_tiles // 5, n_tiles - n_tiles // 5] if split_last and n_tiles >= 5 else [n_tiles]
    in_args = (pieces, sv["x"], dx, vec("norm_g"), sv["scale"], wl["w_in_g"], 0, t_mid)
    res, at = None, 0
    for key, count in zip(("in_proj_bwd_x", "in_proj_bwd_x_rest"), counts):
        res = _in_bwd(*in_args, hosted(key), (at, count), res)
        done(key)
        at += count
    dx, dscale, dshift, g_ng = res
    acc.update(norm_g=g_ng, dshift=dshift, dscale=dscale)
    grads.update(acc=acc, dmod=jnp.concatenate([dshift[0:1], dscale[0:1], dgate[0:1]], axis=1))
    return dx, grads


def _full_qkv(qkv_g, d):
    nd, _, rows3, dh = qkv_g.shape
    nh = d // dh
    rsh = rows3 // (3 * nh)
    return qkv_g.reshape(nd, 3, nh, rsh, dh).transpose(1, 2, 0, 3, 4).reshape(3, nh, nd * rsh, dh)


def _small_weights(small, l, ml_b_if):
    nd = small.shape[0]
    sm = small[:, l]
    cw = sm[:, 0:2 * CONV_WIDTH].reshape(nd, 2, CONV_WIDTH, LANES).transpose(1, 2, 0, 3).reshape(2, CONV_WIDTH, nd * LANES)
    if_rows = (sm.shape[1] - 2 * CONV_WIDTH) * LANES // 8
    wif_t = sm[:, 2 * CONV_WIDTH:].reshape(nd, 8, if_rows).transpose(1, 0, 2).reshape(8, nd * if_rows)
    wift_pad = jnp.pad(wif_t, ((0, LANES - 8), (0, 0))).astype(BF16)
    return dict(rg_conv_w=cw[0], ml_conv_w=cw[1], wift_pad=wift_pad, wif_pad=wift_pad.T,
                bif_pad=jnp.pad(ml_b_if[l], (0, LANES - 8)).reshape(1, LANES))


def kernel(x, c, norm_g, w_ada, b_ada, w_in, rg_conv_w, rg_conv_b, rg_w_a, rg_b_a, rg_w_x, rg_b_x, rg_lambda, ml_conv_w, ml_conv_b, ml_w_q, ml_w_k, ml_w_v, ml_w_if, ml_b_if, ml_norm_g, w_out, final_g, loss_target, m_norm_g, m_w_ada, m_b_ada, m_w_in, m_rg_conv_w, m_rg_conv_b, m_rg_w_a, m_rg_b_a, m_rg_w_x, m_rg_b_x, m_rg_lambda, m_ml_conv_w, m_ml_conv_b, m_ml_w_q, m_ml_w_k, m_ml_w_v, m_ml_w_if, m_ml_b_if, m_ml_norm_g, m_w_out, m_final_g, v_norm_g, v_w_ada, v_b_ada, v_w_in, v_rg_conv_w, v_rg_conv_b, v_rg_w_a, v_rg_b_a, v_rg_w_x, v_rg_b_x, v_rg_lambda, v_ml_conv_w, v_ml_conv_b, v_ml_w_q, v_ml_w_k, v_ml_w_v, v_ml_w_if, v_ml_b_if, v_ml_norm_g, v_w_out, v_final_g):
    given = dict(locals())
    nl = w_in.shape[0]
    d = x.shape[2]
    rep = {n: given[n] for n in REPLICATED}
    rep.update(rg_w_a_bf=rg_w_a.astype(BF16), rg_w_x_bf=rg_w_x.astype(BF16))
    bf = lambda a: a.astype(BF16)

    def qkv_shard(prefix):
        return jnp.stack([given[prefix + "ml_w_q"], given[prefix + "ml_w_k"], given[prefix + "ml_w_v"]], axis=1).reshape(
            nl, -1, ml_w_q.shape[-1])

    def small_shard(prefix):
        return _small_pack(given[prefix + "rg_conv_w"], given[prefix + "ml_conv_w"], given[prefix + "ml_w_if"])

    plan = _Plan()
    qkv = qkv_shard("")
    first_ici = _gather_ici_comm([bf(w_in[0:1]), small_shard("")])
    condition = _exchange_comm([jnp.broadcast_to(c, (SUBLANES, d))], True)
    _run_comms([first_ici, condition], "gather_first")
    first_fwd = _gather_fwd_comm(first_ici.results)
    wcols = w_ada.shape[2]
    me = 4 * lax.axis_index("x") + 2 * lax.axis_index("y") + lax.axis_index("c")
    b_cols = jnp.pad(lax.dynamic_slice_in_dim(b_ada, me * wcols, wcols, axis=1), ((0, SUBLANES - nl), (0, 0)))
    mod_cols, cact_all = _ada_mod(condition.results[0], w_ada, b_cols, [first_fwd])
    w_in_first, small = first_fwd.results
    wl = [_small_weights(small, l, ml_b_if) for l in range(nl)]
    wl[0]["w_in_g"] = w_in_first

    def gather_behind(arrs, ici_host, fwd_host, then):
        ici = _gather_ici_comm(arrs)

        def pass_on():
            fwd = _gather_fwd_comm(ici.results)
            plan.host(fwd_host, fwd, lambda: then(fwd.results))

        plan.host(ici_host, ici, pass_on)

    def got_out(l):
        return lambda r: wl[l].update(w_out_g=r[0], w_qkv=_full_qkv(r[1], d))

    gather_behind([bf(w_out[0:1]), bf(qkv[0:1])], ("in_proj_fwd", 0), ("rglru_fwd", 0), got_out(0))
    for l in range(1, nl):
        gather_behind([bf(w_in[l:l + 1])], ("rglru_fwd", l - 1), ("mlstm_cell_fwd", l - 1),
                      lambda r, l=l: wl[l].update(w_in_g=r[0]))
        gather_behind([bf(w_out[l:l + 1]), bf(qkv[l:l + 1])], ("mlstm_cell_fwd", l - 1), ("out_proj_fwd", l - 1), got_out(l))

    mod_blocks = _exchange([mod_cols], False, "scatter_modulation")[0]
    mod3 = mod_blocks[:, 0:nl].transpose(1, 0, 2).reshape(nl, 3, d)
    rep["vecs"] = _vec_table(rep)
    saved, xl = [], x[0]
    for l in range(nl):
        head = (final_g.reshape(1, -1), loss_target[0]) if l == nl - 1 else None
        xl, sv = _layer_fwd(l, xl, mod3, wl[l], rep, plan, head)
        saved.append(sv)
    grad_x, loss_p, g_final = xl

    keys = ("w_in", "w_out", "w_qkv", "small")
    parity = lax.axis_index("c").astype(jnp.int32).reshape(1)
    grads, recv = [None] * nl, [None] * nl

    def small_parts(g):
        return [bf(_qkv_slots(g["w_qkv"], N_DEV)), bf(_small_slots(g))]

    def reduce_behind(l, host_layer):
        parts = [grads[l]["w_in"], grads[l]["w_out"]]
        swap = _core_swap_comm(parts)
        direct = _exchange_comm(small_parts(grads[l]), False)

        def summed():
            sums = [_pair_sum(a, o, parity, "pair_sum_%s_layer%d" % (key, l)) for key, a, o in zip(keys, parts, swap.results)]
            big = _chip_swap_comm([sums[0]])
            rest = _chip_swap_comm([sums[1]])
            plan.host(("mlstm_cell_bwd", host_layer), big)
            plan.host(("rglru_bwd", host_layer), rest,
                      lambda: recv.__setitem__(l, big.results + rest.results + direct.results))

        plan.host(("out_proj_bwd", host_layer), swap, summed)
        plan.host(("out_proj_bwd", host_layer), direct)

    first, own = {}, {}

    def reduce_own(names, parts_fn, ready_key, swap_key, chip_key):
        def go():
            parts = parts_fn()
            swap = _core_swap_comm(parts)

            def summed():
                sums = [_pair_sum(a, o, parity, "pair_sum_%s_layer0" % n) for n, a, o in zip(names, parts, swap.results)]
                chip = _chip_swap_comm(sums)
                plan.host(chip_key, chip, lambda: own.update(zip(names, chip.results)))

            plan.host(swap_key, swap, summed)

        plan.after.setdefault(ready_key, []).append(go)

    reduce_own(["w_out"], lambda: [first["w_out"]], ("out_proj_bwd", 0), ("mlstm_cell_bwd", 0), ("mlstm_proj_bwd", 0))
    reduce_own(["w_in"], lambda: [first["w_in"]], ("in_proj_bwd_w", 0), ("in_proj_bwd_x", 0), ("in_proj_bwd_x_rest", 0))

    def small_own():
        direct = _exchange_comm(small_parts(first), False)
        plan.host(("in_proj_bwd_w", 0), direct, lambda: own.update(w_qkv=direct.results[0], small=direct.results[1]))

    plan.after.setdefault(("rglru_bwd", 0), []).append(small_own)

    matrices = {}

    def reduce_matrices():
        layers = [grads[l] if l > 0 else first for l in range(nl)]
        mp = jnp.stack([jnp.stack([g["rg_w_a"], g["rg_w_x"]]) for g in layers]).reshape(N_DEV, -1, LANES).astype(BF16)
        scatter = _exchange_comm([mp], False)

        def summed():
            gather = _exchange_comm(_sum_parts(scatter.results, "sum_replicated_matrices"), True)
            plan.host(("in_proj_bwd_x", 0), gather, lambda: matrices.update(mp=gather.results[0].reshape(-1, LANES)))

        plan.host(("in_proj_bwd_w", 0), scatter, summed)

    plan.after.setdefault(("rglru_bwd", 0), []).append(reduce_matrices)

    for l in reversed(range(nl)):
        if l > 0:
            grad_x, grads[l] = _layer_bwd(l, grad_x, saved[l], wl[l], rep, plan)
            reduce_behind(l, l - 1)
        else:
            grad_x, grads[l] = _layer_bwd(l, grad_x, saved[l], wl[l], rep, plan, first, True)
    plan.flush()
    recv[0] = [own[key] for key in keys]

    shard = {p: dict(w_in=given[p + "w_in"], w_out=given[p + "w_out"], w_qkv=qkv_shard(p), small=small_shard(p))
             for p in ("", "m_", "v_")}
    res = {}
    for ki, key in enumerate(keys):
        res[key] = _reduce_adam([recv[l][ki] for l in range(nl)], shard[""][key], shard["m_"][key], shard["v_"][key],
                                "reduce_adam_" + key)

    dmods = jnp.concatenate([grads[l]["dmod"] for l in range(nl)], axis=0)
    dmod_blocks = jnp.pad(dmods.reshape(nl, N_DEV, wcols).transpose(1, 0, 2), ((0, 0), (0, SUBLANES - nl), (0, 0)))
    widen = lambda a: jnp.pad(a, ((0, 0), (0, d - a.shape[1])))
    rows = [widen(grads[l]["acc"][n][0:1]) for l in range(nl) for n in REP_ROWS] + [g_final[0:1], widen(loss_p[0:1])]
    vp = jnp.concatenate(rows + [jnp.zeros(((-len(rows)) % SUBLANES, d), F32)], axis=0)
    (dmod_recv,), (vp_all,) = _run_comms([_exchange_comm([dmod_blocks], False), _exchange_comm([vp], True)], "tail_exchange")
    res["w_ada"] = _ada_grad_adam(cact_all.T, dmod_recv[:, 0:nl].transpose(1, 0, 2), w_ada, m_w_ada, v_w_ada)
    mp_r = matrices["mp"]
    lanes = lambda a: jnp.pad(a, ((0, 0), (0, LANES - a.shape[1])))
    shaped = dict(ml_b_if=lanes, final_g=lambda a: a.reshape(1, d))
    names = [n for n in REPLICATED if n != "b_ada"] + ["b_ada"]
    rep_res, vp_r = _adam_replicated(vp_all, mp_r, {n: tuple(shaped.get(n, lambda a: a)(given[p + n]) for p in ("", "m_", "v_"))
                                                    for n in names}, nl)
    unshaped = dict(ml_b_if=lambda a: a[:, 0:ml_b_if.shape[1]], final_g=lambda a: a.reshape(d))
    rep_out = [{n: unshaped.get(n, lambda a: a)(rep_res[n][kind]) for n in names} for kind in range(4)]
    loss = vp_r[nl * len(REP_ROWS) + 1, 0]

    if_rows = ml_w_if.shape[1]
    order = ("norm_g", "w_ada", "b_ada", "w_in", "rg_conv_w", "rg_conv_b", "rg_w_a", "rg_b_a", "rg_w_x", "rg_b_x",
             "rg_lambda", "ml_conv_w", "ml_conv_b", "ml_w_q", "ml_w_k", "ml_w_v", "ml_w_if", "ml_b_if", "ml_norm_g",
             "w_out", "final_g")
    outs = [loss, grad_x[None]]
    for kind in range(4):
        qkv_k = res["w_qkv"][kind].reshape((nl, 3) + ml_w_q.shape[1:])
        rg_cw, ml_cw, wif = _small_unpack(res["small"][kind], if_rows)
        sharded = dict(w_ada=res["w_ada"][kind], w_in=res["w_in"][kind], w_out=res["w_out"][kind], ml_w_q=qkv_k[:, 0],
                       ml_w_k=qkv_k[:, 1], ml_w_v=qkv_k[:, 2], rg_conv_w=rg_cw, ml_conv_w=ml_cw, ml_w_if=wif)
        for n in order:
            outs.append(sharded[n] if n in sharded else rep_out[kind][n])
    return tuple(outs)
```

```python
import functools

import jax
import jax.numpy as jnp
from jax import lax
from jax.experimental import pallas as pl
from jax.experimental.pallas import tpu as pltpu

F32 = jnp.float32
BF16 = jnp.bfloat16
MESH_AXES = ("x", "y", "c")
N_DEV = 8
EPS = 1e-6
RG_C = 8.0
ML_CHUNK = 128
CONV_WIDTH = 4
ADAM_LR = 0.001
ADAM_B1 = 0.9
ADAM_B2 = 0.999
ADAM_EPS = 1e-08
ADAM_WD = 0.01
ADAM_STEP = 10
NEG_BIG = -1e30
LANES = 128
SUBLANES = 8
VMEM_LIMIT = 56 * 1024 * 1024
HI = lax.Precision.HIGHEST


def _params(n_grid):
    return pltpu.CompilerParams(dimension_semantics=("arbitrary",) * n_grid, vmem_limit_bytes=VMEM_LIMIT)


def _mm(a, b):
    return jnp.dot(a.astype(BF16), b.astype(BF16), preferred_element_type=F32)


def _mm_nt(a, b):
    return lax.dot_general(a.astype(BF16), b.astype(BF16), (((1,), (1,)), ((), ())), preferred_element_type=F32)


def _mm_tn(a, b):
    return lax.dot_general(a.astype(BF16), b.astype(BF16), (((0,), (0,)), ((), ())), preferred_element_type=F32)


def _mm_hi(a, b):
    return jnp.dot(a, b, precision=HI, preferred_element_type=F32)


def _sigmoid(x):
    return 1.0 / (1.0 + jnp.exp(-x))


def _softplus(x):
    return jnp.maximum(x, 0.0) + jnp.log(1.0 + jnp.exp(-jnp.abs(x)))


def _neg_expm1(x):
    poly = -x * (1.0 + x * (0.5 + x * (1.0 / 6.0 + x * (1.0 / 24.0 + x * (1.0 / 120.0)))))
    return jnp.where(jnp.abs(x) < 0.05, poly, 1.0 - jnp.exp(x))


def _iota(shape, dim):
    return lax.broadcasted_iota(jnp.int32, shape, dim)


def _colsum(x):
    return jnp.sum(x, axis=0, keepdims=True)


def _rowsum(x):
    return jnp.sum(x, axis=1, keepdims=True)


def _col(x, j):
    return _rowsum(jnp.where(_iota(x.shape, 1) == j, x, 0.0))


def _row(x, j):
    return _colsum(jnp.where(_iota(x.shape, 0) == j, x, 0.0))


def _shift_down(x, j, prev8):
    if j == 0:
        return x
    t = x.shape[0]
    main = jnp.where(_iota(x.shape, 0) >= j, pltpu.roll(x, j, 0), 0.0)
    fix = jnp.where(_iota(prev8.shape, 0) < j, pltpu.roll(prev8, j, 0), 0.0)
    return jnp.concatenate([main[0:SUBLANES] + fix, main[SUBLANES:t]], axis=0)


def _shift_up(x, j, next8):
    if j == 0:
        return x
    t = x.shape[0]
    main = jnp.where(_iota(x.shape, 0) < t - j, pltpu.roll(x, t - j, 0), 0.0)
    fix = jnp.where(_iota(next8.shape, 0) >= SUBLANES - j, pltpu.roll(next8, SUBLANES - j, 0), 0.0)
    return jnp.concatenate([main[0:t - SUBLANES], main[t - SUBLANES:t] + fix], axis=0)


def _conv(x, prev8, w_ref):
    y = w_ref[CONV_WIDTH - 1:CONV_WIDTH, :] * x
    for j in range(1, CONV_WIDTH):
        y = y + w_ref[CONV_WIDTH - 1 - j:CONV_WIDTH - j, :] * _shift_down(x, j, prev8)
    return y


def _conv_bwd(dy, x, next8, w_ref, gw_ref):
    dx = None
    for j in range(CONV_WIDTH):
        k = CONV_WIDTH - 1 - j
        up = _shift_up(dy, j, next8)
        gw_ref[k:k + 1, :] += _colsum(up * x)
        term = w_ref[k:k + 1, :] * up
        dx = term if dx is None else dx + term
    return dx


def _scan_into(a, b, carry, out_ref, reverse):
    t, c = a.shape
    groups = t // SUBLANES
    a3 = a.reshape(groups, SUBLANES, c)
    b3 = b.reshape(groups, SUBLANES, c)
    sub = _iota(a3.shape, 1)
    for step in (1, 2, 4):
        keep = sub < SUBLANES - step if reverse else sub >= step
        shift = SUBLANES - step if reverse else step
        a_s = jnp.where(keep, pltpu.roll(a3, shift, 1), 1.0)
        b_s = jnp.where(keep, pltpu.roll(b3, shift, 1), 0.0)
        b3 = a3 * b_s + b3
        a3 = a3 * a_s
    for g in (reversed(range(groups)) if reverse else range(groups)):
        rows = slice(g * SUBLANES, (g + 1) * SUBLANES)
        out_ref[rows, :] = b3[g] + a3[g] * carry
        edge = g * SUBLANES if reverse else (g + 1) * SUBLANES - 1
        carry = out_ref[edge:edge + 1, :]


def _blockdiag(x, w_ref, transpose_w=False):
    nh, dh, _ = w_ref.shape
    outs = []
    for h in range(nh):
        xs = x[:, h * dh:(h + 1) * dh]
        outs.append(_mm_nt(xs, w_ref[h]) if transpose_w else _mm(xs, w_ref[h]))
    return jnp.concatenate(outs, axis=1)


def _rg_gates(xc, wa_ref, ba_ref, wx_ref, bx_ref, lam_ref):
    r = _sigmoid(_blockdiag(xc, wa_ref) + ba_ref[...])
    ig = _sigmoid(_blockdiag(xc, wx_ref) + bx_ref[...])
    sp = _softplus(-lam_ref[...])
    log_a = -RG_C * r * sp
    a = jnp.exp(log_a)
    beta = jnp.sqrt(_neg_expm1(2.0 * log_a))
    return r, ig, sp, a, beta


def _bcast8(row):
    return jnp.broadcast_to(row, (SUBLANES, row.shape[1]))


def _full(shape):
    nd = len(shape)
    return pl.BlockSpec(shape, lambda *_: (0,) * nd)


class _Comm:
    def __init__(self, arrays, out_shapes, sems, start, finish, aliases=()):
        self.arrays, self.out_shapes, self.sems = list(arrays), list(out_shapes), list(sems)
        self.start, self.finish, self.aliases = start, finish, tuple(aliases)
        self.results = None


class _RowOf:
    def __init__(self, ref, k):
        self.ref, self.k = ref, k

    def __getitem__(self, idx):
        cols = slice(None) if idx is Ellipsis else idx[1]
        return self.ref[0, self.k:self.k + 1, cols]


class _PartOf:
    def __init__(self, ref, rows=None, cols=None, lead=None):
        self.ref, self.rows, self.cols, self.lead = ref, rows, cols, lead
        if rows is not None:
            self.shape = (rows.stop - rows.start,) + tuple(ref.shape[1:])

    def _at(self, idx):
        if self.lead is not None:
            return (self.lead,) + tuple(idx[1:])
        if self.cols is not None:
            return (slice(None), self.cols)
        return (self.rows, slice(None) if idx is Ellipsis else idx[1])

    def __getitem__(self, idx):
        return self.ref[self._at(idx)]

    def __setitem__(self, idx, value):
        self.ref[self._at(idx)] = value


def _vec(table, layer, k):
    return ("row", table, layer, k)


def _is_row(arg):
    return isinstance(arg, tuple) and len(arg) == 4 and arg[0] == "row"


def _call(body, comms, *, name, grid, in_specs, out_specs, out_shape, args, scratch_shapes=(), aliases=None):
    comms = [cm for cm in (comms or []) if cm is not None]
    rows = {i: a[3] for i, a in enumerate(args) if _is_row(a)}
    in_specs = [pl.BlockSpec((1,) + a[1].shape[1:], functools.partial(lambda layer, *_: (layer, 0, 0), a[2]))
                if _is_row(a) else sp for a, sp in zip(args, in_specs)]
    args = tuple(a[1] if _is_row(a) else a for a in args)
    n_in, n_out, n_sc = len(args), len(out_shape), len(scratch_shapes)
    c_arrays = [a for cm in comms for a in cm.arrays]
    c_outs = [o for cm in comms for o in cm.out_shapes]
    c_sems = [sm for cm in comms for sm in cm.sems]
    aliases, a_at, o_at = dict(aliases or {}), n_in, n_out
    for cm in comms:
        for (i, j) in cm.aliases:
            aliases[a_at + i] = o_at + j
        a_at += len(cm.arrays)
        o_at += len(cm.out_shapes)

    def wrapped(*refs):
        ins, c_in = refs[:n_in], refs[n_in:n_in + len(c_arrays)]
        ins = [_RowOf(r, rows[i]) if i in rows else r for i, r in enumerate(ins)]
        at = n_in + len(c_arrays)
        outs, c_out = refs[at:at + n_out], refs[at + n_out:at + n_out + len(c_outs)]
        at += n_out + len(c_outs)
        scr, sems = refs[at:at + n_sc], refs[at + n_sc:]
        views, ia, io, isem = [], 0, 0, 0
        for cm in comms:
            views.append((c_in[ia:ia + len(cm.arrays)], c_out[io:io + len(cm.out_shapes)], sems[isem:isem + len(cm.sems)]))
            ia, io, isem = ia + len(cm.arrays), io + len(cm.out_shapes), isem + len(cm.sems)
        if comms:
            @pl.when(pl.program_id(0) == 0)
            def _():
                for cm, view in zip(comms, views):
                    cm.start(*view)

        body(*ins, *outs, *scr)
        if comms:
            @pl.when(pl.program_id(0) == grid[0] - 1)
            def _():
                for cm, view in zip(comms, views):
                    cm.finish(*view)

    hbm = pl.BlockSpec(memory_space=pl.ANY)
    res = pl.pallas_call(
        wrapped, name=name, grid=grid,
        in_specs=list(in_specs) + [hbm] * len(c_arrays), out_specs=list(out_specs) + [hbm] * len(c_outs),
        out_shape=list(out_shape) + c_outs, scratch_shapes=list(scratch_shapes) + c_sems,
        input_output_aliases=aliases, compiler_params=_params(len(grid)),
    )(*args, *c_arrays)
    at = n_out
    for cm in comms:
        cm.results = list(res[at:at + len(cm.out_shapes)])
        at += len(cm.out_shapes)
    return list(res[:n_out])


def _join_columns(w_ref, wcat):
    nd, _, _, w = w_ref.shape

    @pl.when(pl.program_id(0) == 0)
    def _():
        for j in range(nd):
            wcat[:, j * w:(j + 1) * w] = w_ref[j, 0]


def _in_fwd(x, ng, scale, shift, w_in_g, layer, tile, comms=None):
    s, d = x.shape
    nd, _, _, w = w_in_g.shape

    def body(x_ref, ng_ref, sc_ref, sh_ref, w_ref, u_ref, h_ref, wcat):
        _join_columns(w_ref, wcat)
        xv = x_ref[...]
        rs = lax.rsqrt(jnp.mean(xv * xv, axis=1, keepdims=True) + EPS)
        hb = (xv * rs * ng_ref[...] * (1.0 + sc_ref[...]) + sh_ref[...]).astype(BF16)
        h_ref[...] = hb
        u_ref[...] = jnp.dot(hb, wcat[...], preferred_element_type=F32)

    return _call(
        body, comms, name="in_proj_fwd", grid=(s // tile,),
        in_specs=[pl.BlockSpec((tile, d), lambda i: (i, 0)), _full((1, d)), _full((1, d)), _full((1, d)),
                  pl.BlockSpec((nd, 1, d, w), lambda i: (0, layer, 0, 0), pipeline_mode=pl.Buffered(1))],
        out_specs=[pl.BlockSpec((tile, nd * w), lambda i: (i, 0)), pl.BlockSpec((tile, d), lambda i: (i, 0))],
        out_shape=[jax.ShapeDtypeStruct((s, nd * w), F32), jax.ShapeDtypeStruct((s, d), BF16)],
        scratch_shapes=[pltpu.VMEM((d, nd * w), BF16)],
        args=(x, ng, scale, shift, w_in_g))


def _rg_fwd(u, d, conv_w, conv_b, w_a, b_a, w_x, b_x, lam, tile, comms=None):
    s = u.shape[0]

    def body(x_ref, z_ref, cw_ref, cb_ref, wa_ref, ba_ref, wx_ref, bx_ref, lam_ref,
             h_ref, y_ref, xc_ref, r_ref, i_ref, a_ref, beta_ref, prev8, hcar):
        @pl.when(pl.program_id(0) == 0)
        def _():
            prev8[...] = jnp.zeros_like(prev8)
            hcar[...] = jnp.zeros_like(hcar)

        x = x_ref[...]
        xc = _conv(x, prev8[...], cw_ref) + cb_ref[...]
        prev8[...] = x[tile - SUBLANES:tile, :]
        r, ig, _, a, beta = _rg_gates(xc, wa_ref, ba_ref, wx_ref, bx_ref, lam_ref)
        xc_ref[...] = xc
        r_ref[...] = r
        i_ref[...] = ig
        a_ref[...] = a
        beta_ref[...] = beta
        _scan_into(a, beta * ig * xc, hcar[SUBLANES - 1:SUBLANES, :], h_ref, False)
        h = h_ref[...]
        hcar[...] = h[tile - SUBLANES:tile, :]
        z = z_ref[...]
        y_ref[...] = (h * z * _sigmoid(z)).astype(BF16)

    vec = _full((1, d))
    return _call(
        body, comms, name="rglru_fwd", grid=(s // tile,),
        in_specs=[pl.BlockSpec((tile, d), lambda i: (i, 0)), pl.BlockSpec((tile, d), lambda i: (i, 1)),
                  _full(conv_w.shape), vec, _full(w_a.shape), vec, _full(w_x.shape), vec, vec],
        out_specs=[pl.BlockSpec((tile, d), lambda i: (i, 0))] * 7,
        out_shape=[jax.ShapeDtypeStruct((s, d), F32), jax.ShapeDtypeStruct((s, d), BF16)] + [jax.ShapeDtypeStruct((s, d), F32)] * 5,
        scratch_shapes=[pltpu.VMEM((SUBLANES, d), F32), pltpu.VMEM((SUBLANES, d), F32)],
        args=(u, u, conv_w, conv_b, w_a, b_a, w_x, b_x, lam))


def _ml_pre(u, d, conv_w, conv_b, w_q, w_k, w_v, wif, bif, tile, comms=None):
    s = u.shape[0]
    nh = w_q.shape[0]

    def body(x_ref, cw_ref, cb_ref, wq_ref, wk_ref, wv_ref, wif_ref, bif_ref, q_ref, k_ref, v_ref, g_ref, pre_ref, gt_ref,
             prev8):
        @pl.when(pl.program_id(0) == 0)
        def _():
            prev8[...] = jnp.zeros_like(prev8)

        x = x_ref[...]
        pre = _conv(x, prev8[...], cw_ref) + cb_ref[...]
        prev8[...] = x[tile - SUBLANES:tile, :]
        xc = pre * _sigmoid(pre)
        q = _blockdiag(xc, wq_ref)
        k = _blockdiag(xc, wk_ref)
        v = _blockdiag(x, wv_ref)
        pre_ref[...] = pre
        q_ref[...] = q
        k_ref[...] = k
        v_ref[...] = v
        g = _mm(q, wif_ref[0:d, :]) + _mm(k, wif_ref[d:2 * d, :]) + _mm(v, wif_ref[2 * d:3 * d, :]) + bif_ref[...]
        lane = _iota(g.shape, 1)
        gl = jnp.where(lane < 4, g, jnp.where(lane < 8, -_softplus(-g), 0.0))
        tri = jnp.where(_iota((ML_CHUNK, ML_CHUNK), 1) <= _iota((ML_CHUNK, ML_CHUNK), 0), 1.0, 0.0)
        cums = [_mm_hi(tri, gl[c * ML_CHUNK:(c + 1) * ML_CHUNK, :]) for c in range(tile // ML_CHUNK)]
        cum = cums[0] if len(cums) == 1 else jnp.concatenate(cums, axis=0)
        gates = gl + jnp.where((lane >= 8) & (lane < 12), pltpu.roll(cum, 4, 1), 0.0)
        g_ref[...] = gates
        gt_ref[...] = gates.T[0:16, :]

    vec = _full((1, d))
    return _call(
        body, comms, name="mlstm_proj_fwd", grid=(s // tile,),
        in_specs=[pl.BlockSpec((tile, d), lambda i: (i, 2)), _full(conv_w.shape), vec,
                  _full(w_q.shape), _full(w_k.shape), _full(w_v.shape), _full(wif.shape), _full((1, LANES))],
        out_specs=[pl.BlockSpec((tile, d), lambda i: (i, 0))] * 3 + [pl.BlockSpec((tile, LANES), lambda i: (i, 0)),
                                                                     pl.BlockSpec((tile, d), lambda i: (i, 0)),
                                                                     pl.BlockSpec((16, tile), lambda i: (0, i))],
        out_shape=[jax.ShapeDtypeStruct((s, d), F32)] * 3 + [jax.ShapeDtypeStruct((s, LANES), F32),
                                                             jax.ShapeDtypeStruct((s, d), F32),
                                                             jax.ShapeDtypeStruct((16, s), F32)],
        scratch_shapes=[pltpu.VMEM((SUBLANES, d), F32)],
        args=(u, conv_w, conv_b, w_q, w_k, w_v, wif, bif))


CELL_CHUNKS_PER_STEP = 4
CELL_BWD_CHUNKS_PER_STEP = 2


def _cell_chunk(h, nh, q_ref, k_ref, v_ref, gc, gr, m_prev, c_h, n_h, m_t=None, r0=0):
    lc = ML_CHUNK
    dh = q_ref.shape[1] // nh
    sl = slice(h * dh, (h + 1) * dh)
    qh = q_ref[r0:r0 + lc, sl]
    kh = k_ref[r0:r0 + lc, sl] * (dh ** -0.5)
    vh = v_ref[r0:r0 + lc, sl]
    li_c = _col(gc, h)
    b_c = _col(gc, 8 + h)
    lib_r = _row(gr, h) - _row(gr, 8 + h)
    b_last = _colsum(jnp.where(_iota((lc, 1), 0) == lc - 1, b_c, 0.0))
    causal = _iota((lc, lc), 1) <= _iota((lc, lc), 0)
    dmat = jnp.where(causal, b_c + lib_r, NEG_BIG)
    m_inter = b_c + m_prev
    if m_t is None:
        m_t = jnp.maximum(m_inter, jnp.max(dmat, axis=1, keepdims=True))
    w_intra = jnp.exp(dmat - m_t)
    w_inter = jnp.exp(m_inter - m_t)
    amat = _mm_nt(qh, kh)
    smat = amat * w_intra
    qc = _mm(qh, c_h)
    qn = _rowsum(qh * n_h)
    den = _rowsum(smat) + w_inter * qn
    gst = b_last - b_c + li_c
    m_new = jnp.maximum(b_last + m_prev, jnp.max(gst, axis=0, keepdims=True))
    w_state = jnp.exp(gst - m_new)
    decay = jnp.exp(b_last + m_prev - m_new)
    return dict(sl=sl, qh=qh, kh=kh, vh=vh, m_t=m_t, w_intra=w_intra, w_inter=w_inter, smat=smat, qc=qc, qn=qn,
                den=den, m_new=m_new, w_state=w_state, decay=decay)


def _ml_cell_fwd(q, k, v, gcol, grow, u, ng, nh, comms=None):
    s, d = q.shape
    lc = ML_CHUNK
    nc = s // lc
    dh = d // nh

    per = CELL_CHUNKS_PER_STEP if nc % CELL_CHUNKS_PER_STEP == 0 else 1

    def body(q_ref, k_ref, v_ref, gc_ref, gr_ref, o_ref, z_ref, ng_ref,
             cell_ref, y_ref, cs_ref, ns_ref, ms_ref, mt_ref, c_sc, n_sc, m_sc):
        @pl.when(pl.program_id(0) == 0)
        def _():
            c_sc[...] = jnp.zeros_like(c_sc)
            n_sc[...] = jnp.zeros_like(n_sc)
            m_sc[...] = jnp.zeros_like(m_sc)

        lane = _iota((lc, LANES), 1)
        for cc in range(per):
            rows = slice(cc * lc, (cc + 1) * lc)
            gc = gc_ref[rows, :]
            gr = gr_ref[:, rows]
            mt_acc = jnp.zeros((lc, LANES), F32)
            for h in range(nh):
                c_h = c_sc[h]
                n_h = n_sc[h, 0:1, :]
                m_prev = jnp.max(m_sc[h, 0:1, :], axis=1, keepdims=True)
                cs_ref[cc, h] = c_h
                ns_ref[cc, h] = n_sc[h]
                ms_ref[cc, h] = m_sc[h]
                t = _cell_chunk(h, nh, q_ref, k_ref, v_ref, gc, gr, m_prev, c_h, n_h, r0=cc * lc)
                sl = t["sl"]
                num = _mm(t["smat"], t["vh"]) + t["w_inter"] * t["qc"]
                cell_h = num / jnp.maximum(jnp.abs(t["den"]), jnp.exp(-t["m_t"]))
                mt_acc = jnp.where(lane == h, t["m_t"], mt_acc)
                kw = t["kh"] * t["w_state"]
                c_sc[h] = t["decay"] * c_h + _mm_tn(kw, t["vh"])
                n_sc[h] = _bcast8(t["decay"] * n_h + _colsum(kw))
                m_sc[h] = jnp.broadcast_to(t["m_new"], (SUBLANES, LANES))
                hg = _sigmoid(o_ref[rows, sl]) * cell_h
                hn = hg * lax.rsqrt(jnp.mean(hg * hg, axis=1, keepdims=True) + EPS)
                z = z_ref[rows, sl]
                cell_ref[rows, sl] = cell_h
                y_ref[rows, sl] = (hn * ng_ref[:, sl] * z * _sigmoid(z)).astype(BF16)
            mt_ref[rows, :] = mt_acc

    tok = pl.BlockSpec((per * lc, d), lambda c: (c, 0))
    return _call(
        body, comms, name="mlstm_cell_fwd", grid=(nc // per,),
        in_specs=[tok, tok, tok, pl.BlockSpec((per * lc, LANES), lambda c: (c, 0)),
                  pl.BlockSpec((16, per * lc), lambda c: (0, c)),
                  pl.BlockSpec((per * lc, d), lambda c: (c, 3)), pl.BlockSpec((per * lc, d), lambda c: (c, 4)), _full((1, d))],
        out_specs=[tok, tok, pl.BlockSpec((per, nh, dh, dh), lambda c: (c, 0, 0, 0)),
                   pl.BlockSpec((per, nh, SUBLANES, dh), lambda c: (c, 0, 0, 0)),
                   pl.BlockSpec((per, nh, SUBLANES, LANES), lambda c: (c, 0, 0, 0)),
                   pl.BlockSpec((per * lc, LANES), lambda c: (c, 0))],
        out_shape=[jax.ShapeDtypeStruct((s, d), F32), jax.ShapeDtypeStruct((s, d), BF16),
                   jax.ShapeDtypeStruct((nc, nh, dh, dh), F32), jax.ShapeDtypeStruct((nc, nh, SUBLANES, dh), F32),
                   jax.ShapeDtypeStruct((nc, nh, SUBLANES, LANES), F32), jax.ShapeDtypeStruct((s, LANES), F32)],
        scratch_shapes=[pltpu.VMEM((nh, dh, dh), F32), pltpu.VMEM((nh, SUBLANES, dh), F32),
                        pltpu.VMEM((nh, SUBLANES, LANES), F32)],
        args=(q, k, v, gcol, grow, u, u, ng))


def _out_fwd(x, y_rg, y_ml, gate, w_out_g, layer, tile, comms=None, head=None):
    s, d = x.shape
    nd, _, r, _ = w_out_g.shape

    def body(x_ref, yr_ref, ym_ref, g_ref, w_ref, *rest):
        ycat = jnp.concatenate([yr_ref[...].astype(BF16), ym_ref[...].astype(BF16)], axis=1)
        acc = jnp.dot(ycat, w_ref[...].reshape(nd * r, d), preferred_element_type=F32)
        xn = x_ref[...] + g_ref[...] * acc
        if head is None:
            xn_ref, y_ref = rest
            y_ref[...] = acc
            xn_ref[...] = xn
            return
        fg_ref, t_ref, y_ref, dx_ref, loss_ref, gg_ref = rest
        y_ref[...] = acc

        @pl.when(pl.program_id(0) == 0)
        def _():
            loss_ref[...] = jnp.zeros_like(loss_ref)
            gg_ref[...] = jnp.zeros_like(gg_ref)

        fg = fg_ref[...]
        rs = lax.rsqrt(jnp.mean(xn * xn, axis=1, keepdims=True) + EPS)
        xh = xn * rs
        e = xh * fg - t_ref[...]
        loss_ref[...] += jnp.broadcast_to(_colsum(_rowsum(e * e)) * (0.5 / d), loss_ref.shape)
        dy = e * (1.0 / d)
        gg_ref[...] += _bcast8(_colsum(dy * xh))
        dxh = dy * fg
        dx_ref[...] = rs * (dxh - xh * jnp.mean(dxh * xh, axis=1, keepdims=True))

    tok = pl.BlockSpec((tile, d), lambda i: (i, 0))
    in_specs = [tok, tok, tok, _full((1, d)), pl.BlockSpec((nd, 1, r, d), lambda i: (0, layer, 0, 0))]
    if head is None:
        return _call(body, comms, name="out_proj_fwd", grid=(s // tile,), in_specs=in_specs, out_specs=[tok, tok],
                     out_shape=[jax.ShapeDtypeStruct((s, d), F32)] * 2, args=(x, y_rg, y_ml, gate, w_out_g))
    return _call(
        body, comms, name="out_proj_loss", grid=(s // tile,),
        in_specs=in_specs + [_full((1, d)), tok],
        out_specs=[tok, tok, _full((SUBLANES, LANES)), _full((SUBLANES, d))],
        out_shape=[jax.ShapeDtypeStruct((s, d), F32)] * 2 + [jax.ShapeDtypeStruct((SUBLANES, LANES), F32),
                                                             jax.ShapeDtypeStruct((SUBLANES, d), F32)],
        args=(x, y_rg, y_ml, gate, w_out_g, *head))


def _ml_out_stage_bwd(dy, cell, o, z, ng):
    so = _sigmoid(o)
    hg = so * cell
    rinv = lax.rsqrt(jnp.mean(hg * hg, axis=1, keepdims=True) + EPS)
    hn = hg * rinv
    sz = _sigmoid(z)
    dz = dy * hn * ng * (sz + z * sz * (1.0 - sz))
    dymid = dy * z * sz
    dhn = dymid * ng
    dhg = rinv * (dhn - hn * jnp.mean(dhn * hn, axis=1, keepdims=True))
    return dz, dhg * cell * so * (1.0 - so), dhg * so, _colsum(dymid * hn)


def _out_bwd(dxo, gate, y, y_rg, y_ml, w_out_g, layer, tile, comms=None):
    s, d = dxo.shape
    nd, _, r, _ = w_out_g.shape

    def body(dx_ref, g_ref, y_ref, yr_ref, ym_ref, w_ref, dyr_ref, dym_ref, gw_ref, dg_ref):
        @pl.when(pl.program_id(0) == 0)
        def _():
            gw_ref[...] = jnp.zeros_like(gw_ref)
            dg_ref[...] = jnp.zeros_like(dg_ref)

        dxv = dx_ref[...]
        dg_ref[...] += _bcast8(_colsum(dxv * y_ref[...]))
        dyb = (dxv * g_ref[...]).astype(BF16)
        dycat = lax.dot_general(dyb, w_ref[...].reshape(nd * r, d), (((1,), (1,)), ((), ())), preferred_element_type=F32)
        dyr_ref[...] = dycat[:, 0:d]
        dym_ref[...] = dycat[:, d:2 * d]
        ycat = jnp.concatenate([yr_ref[...].astype(BF16), ym_ref[...].astype(BF16)], axis=1)
        gw_ref[...] += lax.dot_general(ycat, dyb, (((0,), (0,)), ((), ())), preferred_element_type=F32).reshape(nd, r, d)

    tok = pl.BlockSpec((tile, d), lambda i: (i, 0))
    return _call(
        body, comms, name="out_proj_bwd", grid=(s // tile,),
        in_specs=[tok, _full((1, d)), tok, tok, tok, pl.BlockSpec((nd, 1, r, d), lambda i: (0, layer, 0, 0))],
        out_specs=[tok, tok, _full((nd, r, d)), _full((SUBLANES, d))],
        out_shape=[jax.ShapeDtypeStruct((s, d), F32)] * 2 + [jax.ShapeDtypeStruct((nd, r, d), F32),
                                                             jax.ShapeDtypeStruct((SUBLANES, d), F32)],
        args=(dxo, gate, y, y_rg, y_ml, w_out_g))


def _ml_cell_bwd(dy_ml, u, cell, q, k, v, gcol, grow, mt, cs, ns, ms, ng, nh, comms=None):
    s, d = q.shape
    lc = ML_CHUNK
    nc = s // lc
    dh = d // nh

    per = CELL_BWD_CHUNKS_PER_STEP if nc % CELL_BWD_CHUNKS_PER_STEP == 0 else 1

    def body(*refs):
        gng_ref, dc_sc, dn_sc = refs[20], refs[21], refs[22]

        @pl.when(pl.program_id(0) == 0)
        def _():
            dc_sc[...] = jnp.zeros_like(dc_sc)
            dn_sc[...] = jnp.zeros_like(dn_sc)
            gng_ref[...] = jnp.zeros_like(gng_ref)

        for cc in reversed(range(per)):
            rows = slice(cc * lc, (cc + 1) * lc)
            views = [refs[at] if at == 13 else _PartOf(refs[at], cols=rows) if at == 8 else
                     _PartOf(refs[at], lead=cc) if at in (10, 11, 12) else _PartOf(refs[at], rows=rows) for at in range(20)]
            chunk(*views, gng_ref, dc_sc, dn_sc)

    def chunk(dy_ref, o_ref, z_ref, cell_ref, q_ref, k_ref, v_ref, gc_ref, gr_ref, mt_ref, cs_ref, ns_ref, ms_ref,
              ng_ref, dq_ref, dk_ref, dv_ref, dg_ref, do_ref, dz_ref, gng_ref, dc_sc, dn_sc):
        gc = gc_ref[...]
        gr = gr_ref[...]
        mtv = mt_ref[...]
        lane = _iota((lc, LANES), 1)
        rowv = _iota((lc, 1), 0)
        dg_acc = jnp.zeros((lc, LANES), F32)
        for h in range(nh):
            c_h = cs_ref[0, h]
            n_h = ns_ref[0, h, 0:1, :]
            m_prev = jnp.max(ms_ref[0, h, 0:1, :], axis=1, keepdims=True)
            t = _cell_chunk(h, nh, q_ref, k_ref, v_ref, gc, gr, m_prev, c_h, n_h, m_t=_col(mtv, h))
            sl, qh, kh, vh = t["sl"], t["qh"], t["kh"], t["vh"]
            w_intra, w_inter, smat, w_state, decay = t["w_intra"], t["w_inter"], t["smat"], t["w_state"], t["decay"]
            cell_h = cell_ref[:, sl]
            dz, do, dcell, gng = _ml_out_stage_bwd(dy_ref[:, sl], cell_h, o_ref[:, sl], z_ref[:, sl], ng_ref[:, sl])
            dz_ref[:, sl] = dz.astype(BF16)
            do_ref[:, sl] = do.astype(BF16)
            gng_ref[:, sl] += _bcast8(gng)
            eneg = jnp.exp(-t["m_t"])
            aden = jnp.abs(t["den"])
            nst = jnp.maximum(aden, eneg)
            dnum = dcell / nst
            dden = jnp.where(aden > eneg, -_rowsum(cell_h * dcell) / nst * jnp.sign(t["den"]), 0.0)
            pmat = _mm_nt(dnum, vh) + dden
            damat = pmat * w_intra
            gmat = pmat * smat
            wdn = w_inter * dnum
            wdd = w_inter * dden
            dqh = _mm(damat, kh) + _mm_nt(wdn, c_h) + wdd * n_h
            dkh = _mm_tn(damat, qh)
            dvh = _mm_tn(smat, dnum)
            dw_inter = _rowsum(dnum * t["qc"]) + dden * t["qn"]
            dcn = dc_sc[h]
            dnn = dn_sc[h, 0:1, :]
            kw = kh * w_state
            dkw = _mm_nt(vh, dcn) + dnn
            dvh = dvh + _mm(kw, dcn)
            dkh = dkh + dkw * w_state
            dgst = _rowsum(dkw * kh) * w_state
            ddecay = _colsum(_rowsum(dcn * c_h)) + _rowsum(dnn * n_h)
            db_last = _colsum(dgst) + ddecay * decay
            rs_g = _rowsum(gmat)
            cs_g = _rowsum(gmat.T)
            db = rs_g - cs_g + dw_inter * w_inter - dgst + jnp.where(rowv == lc - 1, db_last, 0.0)
            dli = cs_g + dgst
            dc_sc[h] = decay * dcn + _mm_tn(qh, wdn)
            dn_sc[h] = _bcast8(decay * dnn + _colsum(qh * wdd))
            dq_ref[:, sl] = dqh
            dk_ref[:, sl] = dkh * (dh ** -0.5)
            dv_ref[:, sl] = dvh
            dg_acc = jnp.where(lane == h, dli, jnp.where(lane == 4 + h, db, dg_acc))
        dg_ref[...] = dg_acc

    rev = lambda c: nc // per - 1 - c
    tok = pl.BlockSpec((per * lc, d), lambda c: (rev(c), 0))
    g128 = pl.BlockSpec((per * lc, LANES), lambda c: (rev(c), 0))
    return _call(
        body, comms, name="mlstm_cell_bwd", grid=(nc // per,),
        in_specs=[tok, pl.BlockSpec((per * lc, d), lambda c: (rev(c), 3)), pl.BlockSpec((per * lc, d), lambda c: (rev(c), 4)),
                  tok, tok, tok, tok, g128, pl.BlockSpec((16, per * lc), lambda c: (0, rev(c))), g128,
                  pl.BlockSpec((per, nh, dh, dh), lambda c: (rev(c), 0, 0, 0)),
                  pl.BlockSpec((per, nh, SUBLANES, dh), lambda c: (rev(c), 0, 0, 0)),
                  pl.BlockSpec((per, nh, SUBLANES, LANES), lambda c: (rev(c), 0, 0, 0)), _full((1, d))],
        out_specs=[tok, tok, tok, g128, tok, tok, _full((SUBLANES, d))],
        out_shape=[jax.ShapeDtypeStruct((s, d), F32)] * 3 + [jax.ShapeDtypeStruct((s, LANES), F32)]
        + [jax.ShapeDtypeStruct((s, d), BF16)] * 2 + [jax.ShapeDtypeStruct((SUBLANES, d), F32)],
        scratch_shapes=[pltpu.VMEM((nh, dh, dh), F32), pltpu.VMEM((nh, SUBLANES, dh), F32)],
        args=(dy_ml, u, u, cell, q, k, v, gcol, grow, mt, cs, ns, ms, ng))


def _halo_spec(d, tile, nt, col):
    per = tile // SUBLANES
    return pl.BlockSpec((SUBLANES, d), lambda i: (jnp.maximum((nt - 1 - i) * per - 1, 0), col))


def _ml_pre_bwd(dq, dk, dv, dgates, gcol, u, pre, q, k, v, conv_w, w_q, w_k, w_v, wif_t, tile, comms=None):
    s, d = dq.shape
    nt = s // tile
    nh, dh, _ = w_q.shape

    def body(dq_ref, dk_ref, dv_ref, dg_ref, gc_ref, x_ref, pre_ref, q_ref, k_ref, v_ref, cw_ref,
             wq_ref, wk_ref, wv_ref, wift_ref,
             dx_ref, gwq_ref, gwk_ref, gwv_ref, gwif_ref, gbif_ref, gcw_ref, gcb_ref, next8):
        @pl.when(pl.program_id(0) == 0)
        def _():
            next8[...] = jnp.zeros_like(next8)
            for ref in (gwq_ref, gwk_ref, gwv_ref, gwif_ref, gbif_ref, gcw_ref, gcb_ref):
                ref[...] = jnp.zeros_like(ref)

        x = x_ref[...]
        pre = pre_ref[...]
        sg = _sigmoid(pre)
        xc = pre * sg
        dgc = dg_ref[...]
        lane = _iota(dgc.shape, 1)
        utri = jnp.where(_iota((ML_CHUNK, ML_CHUNK), 0) <= _iota((ML_CHUNK, ML_CHUNK), 1), 1.0, 0.0)
        rcs = [_mm_hi(utri, dgc[c * ML_CHUNK:(c + 1) * ML_CHUNK, :]) for c in range(tile // ML_CHUNK)]
        rc = rcs[0] if len(rcs) == 1 else jnp.concatenate(rcs, axis=0)
        dgates_v = jnp.where(lane < 4, dgc, jnp.where(lane < 8, rc * (1.0 - jnp.exp(gc_ref[...])), 0.0))
        dgb = dgates_v.astype(BF16)
        gbif_ref[...] += jnp.broadcast_to(_colsum(dgates_v), gbif_ref.shape)
        ext = jnp.dot(dgb, wift_ref[...], preferred_element_type=F32)
        dqt = dq_ref[...] + ext[:, 0:d]
        dkt = dk_ref[...] + ext[:, d:2 * d]
        dvt = dv_ref[...] + ext[:, 2 * d:3 * d]
        gwif_ref[:, 0:d] += _mm_tn(dgb, q_ref[...])
        gwif_ref[:, d:2 * d] += _mm_tn(dgb, k_ref[...])
        gwif_ref[:, 2 * d:3 * d] += _mm_tn(dgb, v_ref[...])
        dxc_parts, dxv_parts = [], []
        for h in range(nh):
            sl = slice(h * dh, (h + 1) * dh)
            gwq_ref[h] += _mm_tn(xc[:, sl], dqt[:, sl])
            gwk_ref[h] += _mm_tn(xc[:, sl], dkt[:, sl])
            gwv_ref[h] += _mm_tn(x[:, sl], dvt[:, sl])
            dxc_parts.append(_mm_nt(dqt[:, sl], wq_ref[h]) + _mm_nt(dkt[:, sl], wk_ref[h]))
            dxv_parts.append(_mm_nt(dvt[:, sl], wv_ref[h]))
        dxc = jnp.concatenate(dxc_parts, axis=1)
        dxv = jnp.concatenate(dxv_parts, axis=1)
        dpre = dxc * (sg + pre * sg * (1.0 - sg))
        gcb_ref[...] += _bcast8(_colsum(dpre))
        dx_ref[...] = (dxv + _conv_bwd(dpre, x, next8[...], cw_ref, gcw_ref)).astype(BF16)
        next8[...] = dpre[0:SUBLANES, :]

    rev = lambda i: nt - 1 - i
    tok = pl.BlockSpec((tile, d), lambda i: (rev(i), 0))
    g128 = pl.BlockSpec((tile, LANES), lambda i: (rev(i), 0))
    wsh = (nh, dh, dh)
    return _call(
        body, comms, name="mlstm_proj_bwd", grid=(nt,),
        in_specs=[tok, tok, tok, g128, g128, pl.BlockSpec((tile, d), lambda i: (rev(i), 2)), tok,
                  tok, tok, tok, _full(conv_w.shape), _full(wsh), _full(wsh), _full(wsh), _full(wif_t.shape)],
        out_specs=[tok, _full(wsh), _full(wsh), _full(wsh), _full((LANES, 3 * d)), _full((SUBLANES, LANES)),
                   _full((SUBLANES, d)), _full((SUBLANES, d))],
        out_shape=[jax.ShapeDtypeStruct((s, d), BF16)] + [jax.ShapeDtypeStruct(wsh, F32)] * 3
        + [jax.ShapeDtypeStruct((LANES, 3 * d), F32), jax.ShapeDtypeStruct((SUBLANES, LANES), F32),
           jax.ShapeDtypeStruct((SUBLANES, d), F32), jax.ShapeDtypeStruct((SUBLANES, d), F32)],
        scratch_shapes=[pltpu.VMEM((SUBLANES, d), F32)],
        args=(dq, dk, dv, dgates, gcol, u, pre, q, k, v, conv_w, w_q, w_k, w_v, wif_t))


def _rg_bwd(dy_rg, u, h_rg, gates, conv_w, w_a, w_x, lam, tile, comms=None):
    s, d = dy_rg.shape
    nt = s // tile
    nh, dh, _ = w_a.shape

    def body(dy_ref, x_ref, z_ref, h_ref, hhalo_ref, xc_ref, r_ref, i_ref, a_ref, beta_ref, cw_ref, wa_ref,
             wx_ref, lam_ref,
             dx_ref, dz_ref, gwa_ref, gwx_ref, gba_ref, gbx_ref, glam_ref, gcw_ref, gcb_ref, next8, anext, dnext, dbuf):
        i = pl.program_id(0)

        @pl.when(i == 0)
        def _():
            for ref in (next8, anext, dnext, gwa_ref, gwx_ref, gba_ref, gbx_ref, glam_ref, gcw_ref, gcb_ref):
                ref[...] = jnp.zeros_like(ref)

        inner = jnp.where(i < nt - 1, 1.0, 0.0)
        xc, r, ig, a, beta = xc_ref[...], r_ref[...], i_ref[...], a_ref[...], beta_ref[...]
        sp = _softplus(-lam_ref[...])
        h = h_ref[...]
        row = _iota(h.shape, 0)
        hprev = jnp.where(row >= 1, pltpu.roll(h, 1, 0), hhalo_ref[SUBLANES - 1:SUBLANES, :] * inner)
        z = z_ref[...]
        sz = _sigmoid(z)
        dyv = dy_ref[...]
        dz_ref[...] = (dyv * h * (sz + z * sz * (1.0 - sz))).astype(BF16)
        a_up = jnp.where(row < tile - 1, pltpu.roll(a, tile - 1, 0), anext[0:1, :])
        _scan_into(a_up, dyv * z * sz, dnext[0:1, :], dbuf, True)
        delta = dbuf[...]
        anext[...] = a[0:SUBLANES, :]
        dnext[...] = delta[0:SUBLANES, :]
        dla = delta * hprev * a - delta * ig * xc * (a * a / beta)
        glam_ref[...] += _bcast8(_colsum(dla * r) * (RG_C * _sigmoid(-lam_ref[...])))
        dpa = dla * (-RG_C * sp) * r * (1.0 - r)
        dpx = delta * beta * xc * ig * (1.0 - ig)
        gba_ref[...] += _bcast8(_colsum(dpa))
        gbx_ref[...] += _bcast8(_colsum(dpx))
        parts = []
        for hh in range(nh):
            sl = slice(hh * dh, (hh + 1) * dh)
            gwa_ref[hh] += _mm_tn(xc[:, sl], dpa[:, sl])
            gwx_ref[hh] += _mm_tn(xc[:, sl], dpx[:, sl])
            parts.append(_mm_nt(dpa[:, sl], wa_ref[hh]) + _mm_nt(dpx[:, sl], wx_ref[hh]))
        dxc = delta * beta * ig + jnp.concatenate(parts, axis=1)
        gcb_ref[...] += _bcast8(_colsum(dxc))
        dx_ref[...] = _conv_bwd(dxc, x_ref[...], next8[...], cw_ref, gcw_ref).astype(BF16)
        next8[...] = dxc[0:SUBLANES, :]

    rev = lambda i: nt - 1 - i
    tok = pl.BlockSpec((tile, d), lambda i: (rev(i), 0))
    vec = _full((1, d))
    acc = _full((SUBLANES, d))
    wsh = (nh, dh, dh)
    return _call(
        body, comms, name="rglru_bwd", grid=(nt,),
        in_specs=[tok, tok, pl.BlockSpec((tile, d), lambda i: (rev(i), 1)), tok,
                  _halo_spec(d, tile, nt, 0)] + [tok] * 5 + [_full(conv_w.shape), _full(wsh), _full(wsh), vec],
        out_specs=[tok, tok, _full(wsh), _full(wsh), acc, acc, acc, acc, acc],
        out_shape=[jax.ShapeDtypeStruct((s, d), BF16)] * 2 + [jax.ShapeDtypeStruct(wsh, F32)] * 2
        + [jax.ShapeDtypeStruct((SUBLANES, d), F32)] * 5,
        scratch_shapes=[pltpu.VMEM((SUBLANES, d), F32)] * 3 + [pltpu.VMEM((tile, d), F32)],
        args=(dy_rg, u, u, h_rg, h_rg, *gates, conv_w, w_a, w_x, lam))


def _segments(d, w, n_pieces, n_slots):
    bounds = sorted({k * d for k in range(n_pieces + 1)} | {j * w for j in range(n_slots + 1)})
    return [(lo // d, lo % d, lo // w, lo % w, hi - lo) for lo, hi in zip(bounds[:-1], bounds[1:])]


def _in_bwd(pieces, x, dxo, ng, scale, w_in_g, layer, tile, comms=None, tiles=None, prev=None):
    s, d = x.shape
    nd, _, _, w = w_in_g.shape
    first, count = tiles or (0, s // tile)
    n_p = len(pieces)

    def body(*refs):
        p_refs = refs[:n_p]
        x_ref, dxo_ref, ng_ref, sc_ref, w_ref = refs[n_p:n_p + 5]
        dx_ref, dsc_ref, dsh_ref, gng_ref, wcat = refs[-5:]
        _join_columns(w_ref, wcat)

        @pl.when(pl.program_id(0) == 0)
        def _():
            for k, ref in enumerate((dsc_ref, dsh_ref, gng_ref)):
                ref[...] = jnp.zeros_like(ref) if prev is None else refs[n_p + 6 + k][...]

        du = jnp.concatenate([p[...] for p in p_refs], axis=1)
        dh = lax.dot_general(du, wcat[...], (((1,), (1,)), ((), ())), preferred_element_type=F32)
        xv = x_ref[...]
        g = ng_ref[...]
        rs = lax.rsqrt(jnp.mean(xv * xv, axis=1, keepdims=True) + EPS)
        xh = xv * rs
        dsh_ref[...] += _bcast8(_colsum(dh))
        dsc_ref[...] += _bcast8(_colsum(dh * xh * g))
        dhn = dh * (1.0 + sc_ref[...])
        gng_ref[...] += _bcast8(_colsum(dhn * xh))
        dxh = dhn * g
        dx_ref[...] = dxo_ref[...] + rs * (dxh - xh * jnp.mean(dxh * xh, axis=1, keepdims=True))

    tok = pl.BlockSpec((tile, d), lambda i: (i + first, 0))
    vec = _full((1, d))
    acc = _full((SUBLANES, d))
    more_specs = [] if prev is None else [pl.BlockSpec(memory_space=pl.ANY), acc, acc, acc]
    return _call(
        body, comms, name="in_proj_bwd_x", grid=(count,),
        in_specs=[tok] * n_p + [tok, tok, vec, vec, pl.BlockSpec((nd, 1, d, w), lambda i: (0, layer, 0, 0),
                                                               pipeline_mode=pl.Buffered(1))] + more_specs,
        out_specs=[tok, acc, acc, acc],
        out_shape=[jax.ShapeDtypeStruct((s, d), F32)] + [jax.ShapeDtypeStruct((SUBLANES, d), F32)] * 3,
        scratch_shapes=[pltpu.VMEM((d, nd * w), BF16)],
        args=(*pieces, x, dxo, ng, scale, w_in_g) + (() if prev is None else tuple(prev)),
        aliases={} if prev is None else {n_p + 5: 0})


def _in_bwd_w(pieces, hbf, w, slots, tile, comms=None):
    s, d = hbf.shape
    nd_all = len(pieces) * d // w
    segs = [sg for sg in _segments(d, w, len(pieces), nd_all) if sg[2] in slots]

    def body(*refs):
        p_refs = refs[:len(pieces)]
        h_ref, gw_ref = refs[len(pieces):]

        @pl.when(pl.program_id(0) == 0)
        def _():
            gw_ref[...] = jnp.zeros_like(gw_ref)

        hv = h_ref[...]
        for (kk, a, j, b, width) in segs:
            gw_ref[j - slots[0], :, b:b + width] += _mm_tn(hv, p_refs[kk][:, a:a + width])

    tok = pl.BlockSpec((tile, d), lambda i: (i, 0))
    return _call(
        body, comms, name="in_proj_bwd_w", grid=(s // tile,),
        in_specs=[tok] * len(pieces) + [tok],
        out_specs=[pl.BlockSpec((len(slots), d, w), lambda i: (0, 0, 0), pipeline_mode=pl.Buffered(1))],
        out_shape=[jax.ShapeDtypeStruct((len(slots), d, w), F32)],
        args=(*pieces, hbf))[0]


def _exchange(arrs, gather, name):
    return _run_comms([_exchange_comm(arrs, gather)], name)[0]


def _run_comms(comms, name):
    _call(lambda: None, comms, name=name, grid=(1,), in_specs=[], out_specs=[], out_shape=[], args=())
    return [cm.results for cm in comms]


def _exchange_comm(arrs, gather):
    n = len(arrs)
    per = N_DEV - 1

    def copies(ins, outs, sems):
        send_sems, recv_sems, local_sems = sems
        x, y, c = (lax.axis_index(ax) for ax in MESH_AXES)
        me = 4 * x + 2 * y + c
        sends, recvs = [], []
        for flip in range(1, N_DEV):
            px = x ^ ((flip >> 2) & 1)
            py = y ^ ((flip >> 1) & 1)
            pc = c ^ (flip & 1)
            peer = 4 * px + 2 * py + pc
            for kk in range(n):
                src = ins[kk] if gather else ins[kk].at[peer]
                sends.append(_remote(src, outs[kk].at[me], send_sems, recv_sems, kk * per + flip - 1, (px, py, pc)))
                recvs.append(_remote(src, outs[kk].at[peer], send_sems, recv_sems, kk * per + flip - 1, (px, py, pc)))
        local = [pltpu.make_async_copy(ins[kk] if gather else ins[kk].at[me], outs[kk].at[me], local_sems.at[kk])
                 for kk in range(n)]
        return local, sends, recvs

    def start(ins, outs, sems):
        local, sends, _ = copies(ins, outs, sems)
        for cp in sends + local:
            cp.start()

    def finish(ins, outs, sems):
        local, sends, recvs = copies(ins, outs, sems)
        for cp in recvs:
            cp.wait_recv()
        for cp in sends:
            cp.wait_send()
        for cp in local:
            cp.wait()

    return _Comm(arrs, [jax.ShapeDtypeStruct((N_DEV,) + a.shape if gather else a.shape, a.dtype) for a in arrs],
                 [pltpu.SemaphoreType.DMA((n * per,)), pltpu.SemaphoreType.DMA((n * per,)), pltpu.SemaphoreType.DMA((n,))],
                 start, finish)


def _mesh_place():
    x, y, c = (lax.axis_index(ax) for ax in MESH_AXES)
    return x, y, c, (x, y, 1 - c), [(1 - x, y), (x, 1 - y), (1 - x, 1 - y)]


def _remote(src, dst, send_sems, recv_sems, sem, to):
    return pltpu.make_async_remote_copy(src_ref=src, dst_ref=dst, send_sem=send_sems.at[sem], recv_sem=recv_sems.at[sem],
                                        device_id=to, device_id_type=pl.DeviceIdType.MESH)


N_CHIPS = N_DEV // 2


def _pair_sum(a, other, parity, name):
    _, r, c = a.shape
    tr = _row_tile(r, c, 3)

    def body(p_ref, a_ref, o_ref, s_ref):
        s_ref[...] = (a_ref[...] + o_ref[...]).astype(BF16)

    return pl.pallas_call(
        body, name=name,
        grid_spec=pltpu.PrefetchScalarGridSpec(
            num_scalar_prefetch=1, grid=(N_CHIPS, r // tr),
            in_specs=[pl.BlockSpec((1, tr, c), lambda q, i, p: (2 * q + p[0], i, 0)),
                      pl.BlockSpec((1, tr, c), lambda q, i, p: (q, i, 0))],
            out_specs=pl.BlockSpec((1, tr, c), lambda q, i, p: (q, i, 0))),
        out_shape=jax.ShapeDtypeStruct((N_CHIPS, r, c), BF16),
        compiler_params=_params(2),
    )(parity, a, other)


def _adam_math(w, g, m, v):
    m = ADAM_B1 * m + (1.0 - ADAM_B1) * g
    v = ADAM_B2 * v + (1.0 - ADAM_B2) * (g * g)
    m_hat = m / (1.0 - ADAM_B1 ** ADAM_STEP)
    v_hat = v / (1.0 - ADAM_B2 ** ADAM_STEP)
    delta = -ADAM_LR * (m_hat / (jnp.sqrt(v_hat) + ADAM_EPS) + ADAM_WD * w)
    return delta, m, v


def _sum_devices(r_ref):
    acc = r_ref[0].astype(F32)
    for p in range(1, r_ref.shape[0]):
        acc = acc + r_ref[p].astype(F32)
    return acc


def _row_tile(rows, cols, n_bufs):
    budget = 24 * 1024 * 1024 // (n_bufs * 2 * cols * 4)
    t = rows
    while t > budget and t % 2 == 0 and (t // 2) % SUBLANES == 0:
        t //= 2
    return t


def _reduce_adam(recvs, w, m, v, name, comms=None):
    nl, r, c = w.shape
    n_part = recvs[0].shape[0]
    tr = _row_tile(r, c, n_part * nl + 7)
    nt = r // tr

    def body(*refs):
        r_refs = refs[:nl]
        w_ref, m_ref, v_ref, g_ref, d_ref, mo_ref, vo_ref = refs[nl:]
        layer = pl.program_id(0) // nt
        g = _sum_devices(r_refs[0])
        for ll in range(1, nl):
            g = jnp.where(layer == ll, _sum_devices(r_refs[ll]), g)
        delta, m2, v2 = _adam_math(w_ref[0], g, m_ref[0], v_ref[0])
        g_ref[0] = g
        d_ref[0] = delta
        mo_ref[0] = m2
        vo_ref[0] = v2

    def rspec(ll):
        return pl.BlockSpec((n_part, tr, c),
                            lambda i: (0, jnp.where(i // nt == ll, i % nt, jnp.where(i // nt < ll, 0, nt - 1)), 0))

    blk = pl.BlockSpec((1, tr, c), lambda i: (i // nt, i % nt, 0))
    return _call(
        body, comms, name=name, grid=(nl * nt,),
        in_specs=[rspec(ll) for ll in range(nl)] + [blk, blk, blk],
        out_specs=[blk] * 4,
        out_shape=[jax.ShapeDtypeStruct((nl, r, c), F32)] * 4,
        args=(*recvs, w, m, v))


def _tile_for(s, want):
    return min(want, s)


REPLICATED = ("norm_g", "b_ada", "rg_conv_b", "rg_w_a", "rg_b_a", "rg_w_x", "rg_b_x", "rg_lambda", "ml_conv_b",
              "ml_b_if", "ml_norm_g", "final_g")


def _small_pack(rg_conv_w, ml_conv_w, ml_w_if):
    nl = rg_conv_w.shape[0]
    wif_t = jnp.swapaxes(ml_w_if, 1, 2).reshape(nl, -1, LANES)
    return jnp.concatenate([rg_conv_w, ml_conv_w, wif_t], axis=1)


def _small_unpack(p, if_rows):
    nl = p.shape[0]
    rg_cw = p[:, 0:CONV_WIDTH]
    ml_cw = p[:, CONV_WIDTH:2 * CONV_WIDTH]
    wif = jnp.swapaxes(p[:, 2 * CONV_WIDTH:].reshape(nl, 8, if_rows), 1, 2)
    return rg_cw, ml_cw, wif


def _qkv_slots(g_qkv, nd):
    three, nh, dh, _ = g_qkv.shape
    return g_qkv.reshape(three, nh, nd, dh // nd, dh).transpose(2, 0, 1, 3, 4).reshape(nd, three * nh * (dh // nd), dh)


def _small_slots(g):
    nd = N_DEV
    cw = jnp.stack([g["rg_conv_w"], g["ml_conv_w"]]).reshape(2, CONV_WIDTH, nd, LANES).transpose(2, 0, 1, 3)
    cw = cw.reshape(nd, 2 * CONV_WIDTH, LANES)
    wif = g["wif_t"].reshape(8, nd, -1).transpose(1, 0, 2).reshape(nd, -1, LANES)
    return jnp.concatenate([cw, wif], axis=1)


def _slot(block):
    return 4 * block[0] + 2 * block[1] + block[2]


def _dma_sems(*counts):
    return [pltpu.SemaphoreType.DMA((n,)) for n in counts]


def _start_all(copies):
    for cp in copies:
        cp.start()


def _gather_ici_comm(arrs):
    n = len(arrs)

    def copies(ins, outs, sems):
        send_sems, recv_sems, local_sems = sems
        x, y, c, sibling, chips = _mesh_place()
        me = (x, y, c)
        peers = [(*chip, c) for chip in chips] + [sibling]
        local = [pltpu.make_async_copy(ins[kk], outs[kk].at[_slot(me)], local_sems.at[kk]) for kk in range(n)]
        sends = [_remote(ins[kk], outs[kk].at[_slot(me)], send_sems, recv_sems, kk * 4 + j, peer)
                 for j, peer in enumerate(peers) for kk in range(n)]
        recvs = [_remote(ins[kk], outs[kk].at[_slot(peer)], send_sems, recv_sems, kk * 4 + j, peer)
                 for j, peer in enumerate(peers) for kk in range(n)]
        return local, sends, recvs

    def start(ins, outs, sems):
        local, sends, _ = copies(ins, outs, sems)
        _start_all(sends + local)

    def finish(ins, outs, sems):
        local, sends, recvs = copies(ins, outs, sems)
        for cp in recvs:
            cp.wait_recv()
        for cp in sends:
            cp.wait_send()
        for cp in local:
            cp.wait()

    return _Comm(arrs, [jax.ShapeDtypeStruct((N_DEV,) + a.shape, a.dtype) for a in arrs], _dma_sems(4 * n, 4 * n, n),
                 start, finish)


def _gather_fwd_comm(bufs):
    n = len(bufs)

    def copies(ins, outs, sems):
        send_sems, recv_sems = sems
        _, _, c, sibling, chips = _mesh_place()
        sends = [_remote(ins[kk].at[_slot((*chip, c))], outs[kk].at[_slot((*chip, c))], send_sems, recv_sems, kk * 3 + j, sibling)
                 for j, chip in enumerate(chips) for kk in range(n)]
        recvs = [_remote(ins[kk].at[_slot((*chip, c))], outs[kk].at[_slot((*chip, 1 - c))], send_sems, recv_sems, kk * 3 + j, sibling)
                 for j, chip in enumerate(chips) for kk in range(n)]
        return sends, recvs

    def start(ins, outs, sems):
        _start_all(copies(ins, outs, sems)[0])

    def finish(ins, outs, sems):
        sends, recvs = copies(ins, outs, sems)
        for cp in recvs:
            cp.wait_recv()
        for cp in sends:
            cp.wait_send()

    return _Comm(bufs, [jax.ShapeDtypeStruct(a.shape, a.dtype) for a in bufs], _dma_sems(3 * n, 3 * n), start, finish,
                 aliases=[(i, i) for i in range(n)])


def _core_swap_comm(arrs):
    n = len(arrs)

    def copies(ins, outs, sems):
        send_sems, recv_sems = sems
        _, _, c, sibling, _ = _mesh_place()
        return [_remote(ins[kk].at[2 * q + (1 - c)], outs[kk].at[q], send_sems, recv_sems, kk * N_CHIPS + q, sibling)
                for q in range(N_CHIPS) for kk in range(n)]

    def start(ins, outs, sems):
        _start_all(copies(ins, outs, sems))

    def finish(ins, outs, sems):
        cps = copies(ins, outs, sems)
        for cp in cps:
            cp.wait_recv()
        for cp in cps:
            cp.wait_send()

    return _Comm(arrs, [jax.ShapeDtypeStruct((N_CHIPS,) + a.shape[1:], a.dtype) for a in arrs],
                 _dma_sems(N_CHIPS * n, N_CHIPS * n), start, finish)


def _chip_swap_comm(arrs):
    n = len(arrs)
    per = N_CHIPS - 1

    def copies(ins, outs, sems):
        send_sems, recv_sems, local_sems = sems
        x, y, c, _, chips = _mesh_place()
        mine = 2 * x + y
        sends = [_remote(ins[kk].at[2 * chip[0] + chip[1]], outs[kk].at[mine], send_sems, recv_sems, kk * per + j, (*chip, c))
                 for j, chip in enumerate(chips) for kk in range(n)]
        recvs = [_remote(ins[kk].at[mine], outs[kk].at[2 * chip[0] + chip[1]], send_sems, recv_sems, kk * per + j, (*chip, c))
                 for j, chip in enumerate(chips) for kk in range(n)]
        local = [pltpu.make_async_copy(ins[kk].at[mine], outs[kk].at[mine], local_sems.at[kk]) for kk in range(n)]
        return local, sends, recvs

    def start(ins, outs, sems):
        local, sends, _ = copies(ins, outs, sems)
        _start_all(sends + local)

    def finish(ins, outs, sems):
        local, sends, recvs = copies(ins, outs, sems)
        for cp in recvs:
            cp.wait_recv()
        for cp in sends:
            cp.wait_send()
        for cp in local:
            cp.wait()

    return _Comm(arrs, [jax.ShapeDtypeStruct(a.shape, a.dtype) for a in arrs], _dma_sems(per * n, per * n, n), start, finish)


def _ada_mod(c_all, w_ada, b_cols, comms=None):
    nl, d, w = w_ada.shape

    def body(c_ref, w_ref, b_ref, m_ref, ca_ref):
        sub = _iota((SUBLANES, d), 0)
        cv = jnp.zeros((SUBLANES, d), F32)
        for b in range(N_DEV):
            cv = jnp.where(sub == b, c_ref[b], cv)
        ca = cv * _sigmoid(cv)
        ca_ref[...] = ca
        m_ref[...] = jnp.zeros_like(m_ref)
        for l in range(nl):
            ml = _mm_hi(ca, w_ref[l]) + b_ref[l:l + 1, :]
            for b in range(N_DEV):
                m_ref[b, l:l + 1, :] = _row(ml, b)

    return _call(
        body, comms, name="adaln_mod_columns", grid=(1,),
        in_specs=[_full(c_all.shape), _full(w_ada.shape), _full(b_cols.shape)],
        out_specs=[_full((N_DEV, SUBLANES, w)), _full((SUBLANES, d))],
        out_shape=[jax.ShapeDtypeStruct((N_DEV, SUBLANES, w), F32), jax.ShapeDtypeStruct((SUBLANES, d), F32)],
        args=(c_all, w_ada, b_cols))


def _ada_grad_adam(cact_t, dmods, w, m, v, comms=None):
    nl, d, wd = w.shape
    tr = _row_tile(d, wd, 8)
    nt = d // tr

    def body(c_ref, dm_ref, w_ref, m_ref, v_ref, g_ref, d_ref, mo_ref, vo_ref):
        cv = c_ref[...]
        dm = dm_ref[0]
        g = _col(cv, 0) * _row(dm, 0)
        for b in range(1, N_DEV):
            g = g + _col(cv, b) * _row(dm, b)
        delta, m2, v2 = _adam_math(w_ref[0], g, m_ref[0], v_ref[0])
        g_ref[0] = g
        d_ref[0] = delta
        mo_ref[0] = m2
        vo_ref[0] = v2

    blk = pl.BlockSpec((1, tr, wd), lambda i: (i // nt, i % nt, 0))
    return _call(
        body, comms, name="adaln_grad_adam", grid=(nl * nt,),
        in_specs=[pl.BlockSpec((tr, N_DEV), lambda i: (i % nt, 0)), pl.BlockSpec((1, N_DEV, wd), lambda i: (i // nt, 0, 0)),
                  blk, blk, blk],
        out_specs=[blk] * 4, out_shape=[jax.ShapeDtypeStruct((nl, d, wd), F32)] * 4,
        args=(cact_t, dmods, w, m, v))


REP_ROWS = ("norm_g", "dshift", "dscale", "dgate", "rg_conv_b", "rg_b_a", "rg_b_x", "rg_lambda", "ml_conv_b", "ml_norm_g",
            "ml_b_if")


def _sum_parts(recvs, name):
    def body(*refs):
        for r_ref, o_ref in zip(refs[:len(recvs)], refs[len(recvs):]):
            o_ref[...] = _sum_devices(r_ref).astype(o_ref.dtype)

    return pl.pallas_call(
        body, name=name, grid=(1,),
        in_specs=[_full(r.shape) for r in recvs], out_specs=[_full(r.shape[1:]) for r in recvs],
        out_shape=[jax.ShapeDtypeStruct(r.shape[1:], r.dtype) for r in recvs], compiler_params=_params(1),
    )(*recvs)


def _adam_replicated(vp, mp, params, nl):
    d = vp.shape[2]
    nr = len(REP_ROWS)
    names = list(params)
    mat_shape = params["rg_w_a"][0].shape[1:]
    mat_rows = mp.shape[0] // (2 * nl)

    def pieces(name):
        if name == "final_g":
            return [(lambda vp_ref, mp_ref: vp_ref[nl * nr:nl * nr + 1, :], (slice(0, 1), slice(None)))]
        out = []
        for l in range(nl):
            if name in ("rg_w_a", "rg_w_x"):
                at = (2 * l + (name == "rg_w_x")) * mat_rows
                out.append((lambda vp_ref, mp_ref, at=at: mp_ref[at:at + mat_rows, :].astype(F32).reshape(mat_shape), l))
            elif name == "b_ada":
                for j in range(3):
                    r = l * nr + 1 + j
                    out.append((lambda vp_ref, mp_ref, r=r: vp_ref[r:r + 1, :], (slice(l, l + 1), slice(j * d, (j + 1) * d))))
            else:
                r = l * nr + REP_ROWS.index(name)
                cols = slice(0, LANES) if name == "ml_b_if" else slice(None)
                out.append((lambda vp_ref, mp_ref, r=r, cols=cols: vp_ref[r:r + 1, cols], (slice(l, l + 1), slice(None))))
        return out

    def body(*refs):
        parts_ref, mp_ref, vp_ref = refs[0], refs[1], refs[-1]
        ins, outs = refs[2:2 + 3 * len(names)], refs[2 + 3 * len(names):-1]
        vp_ref[...] = _sum_devices(parts_ref)
        for pi, name in enumerate(names):
            w_ref, m_ref, v_ref = ins[3 * pi:3 * pi + 3]
            g_ref, d_ref, mo_ref, vo_ref = outs[4 * pi:4 * pi + 4]
            for get, idx in pieces(name):
                g = get(vp_ref, mp_ref)
                delta, m2, v2 = _adam_math(w_ref[idx], g, m_ref[idx], v_ref[idx])
                g_ref[idx] = g
                d_ref[idx] = delta
                mo_ref[idx] = m2
                vo_ref[idx] = v2

    flat = [a for name in names for a in params[name]]
    out_shape = [jax.ShapeDtypeStruct(params[name][0].shape, F32) for name in names for _ in range(4)]
    out_shape.append(jax.ShapeDtypeStruct(vp.shape[1:], F32))
    res = pl.pallas_call(
        body, name="adam_replicated", grid=(1,),
        in_specs=[_full(vp.shape), _full(mp.shape)] + [_full(a.shape) for a in flat],
        out_specs=[_full(o.shape) for o in out_shape], out_shape=out_shape, compiler_params=_params(1),
    )(vp, mp, *flat)
    return {name: res[4 * pi:4 * pi + 4] for pi, name in enumerate(names)}, res[-1]


class _Plan:
    def __init__(self):
        self.hosted, self.after = {}, {}

    def host(self, key, comm, then=None):
        self.hosted.setdefault(key, []).append(comm)
        if then is not None:
            self.after.setdefault(key, []).append(then)

    def comms(self, key):
        return self.hosted.pop(key, None)

    def done(self, key):
        for fn in self.after.pop(key, []):
            fn()

    def flush(self):
        while self.hosted:
            key = next(iter(self.hosted))
            _call(lambda: None, self.comms(key), name="exchange_after_%s_%d" % key, grid=(1,), in_specs=[], out_specs=[],
                  out_shape=[], args=())
            self.done(key)


VEC_TABLE = ("norm_g", "rg_conv_b", "rg_b_a", "rg_b_x", "rg_lambda", "ml_conv_b", "ml_norm_g")


def _vec_table(rep):
    rows = [rep[n] for n in VEC_TABLE]
    return jnp.stack(rows + [jnp.zeros_like(rows[0])] * (SUBLANES - len(rows)), axis=1)


def _layer_fwd(l, xl, mod3, wl, rep, plan, head=None):
    s, d = xl.shape
    t_big, t_mid = _tile_for(s, 512), _tile_for(s, 256)
    nh_ml = rep["ml_b_if"].shape[1] // 2
    vec = lambda name: _vec(rep["vecs"], l, VEC_TABLE.index(name))
    shift, scale, gate = (_vec(mod3, l, kk) for kk in range(3))
    hosted = lambda name: plan.comms((name, l)) if plan else None
    done = lambda name: plan.done((name, l)) if plan else None
    u, hbf = _in_fwd(xl, vec("norm_g"), scale, shift, wl["w_in_g"], 0, t_big, hosted("in_proj_fwd"))
    done("in_proj_fwd")
    h_rg, y_rg, *rg_gates = _rg_fwd(u, d, wl["rg_conv_w"], vec("rg_conv_b"), rep["rg_w_a_bf"][l], vec("rg_b_a"),
                                    rep["rg_w_x_bf"][l], vec("rg_b_x"), vec("rg_lambda"), t_mid, hosted("rglru_fwd"))
    done("rglru_fwd")
    q, k, v, gcol, pre, grow = _ml_pre(u, d, wl["ml_conv_w"], vec("ml_conv_b"), wl["w_qkv"][0], wl["w_qkv"][1],
                                       wl["w_qkv"][2], wl["wif_pad"], wl["bif_pad"], t_mid, hosted("mlstm_proj_fwd"))
    done("mlstm_proj_fwd")
    cell, y_ml, cs, ns, ms, mt = _ml_cell_fwd(q, k, v, gcol, grow, u, vec("ml_norm_g"), nh_ml, hosted("mlstm_cell_fwd"))
    done("mlstm_cell_fwd")
    res = _out_fwd(xl, y_rg, y_ml, gate, wl["w_out_g"], 0, t_big, hosted("out_proj_fwd"), head)
    done("out_proj_fwd")
    x_new, y = (res[0], res[1]) if head is None else (tuple(res[1:]), res[0])
    saved = dict(x=xl, u=u, hbf=hbf, h_rg=h_rg, y_rg=y_rg, q=q, k=k, v=v, gcol=gcol, grow=grow, cell=cell, y_ml=y_ml,
                 cs=cs, ns=ns, ms=ms, mt=mt, y=y, scale=scale, gate=gate, rg_gates=rg_gates, pre=pre)
    return x_new, saved


def _layer_bwd(l, dx, sv, wl, rep, plan, grads=None, split_last=False):
    s, d = dx.shape
    t_big, t_mid = _tile_for(s, 512), _tile_for(s, 256)
    nh_ml = rep["ml_b_if"].shape[1] // 2
    nd, _, _, w_cols = wl["w_in_g"].shape
    grads = {} if grads is None else grads
    vec = lambda name: _vec(rep["vecs"], l, VEC_TABLE.index(name))
    hosted = lambda name: plan.comms((name, l)) if plan else None
    done = lambda name: plan.done((name, l)) if plan else None
    dy_rg, dy_ml, gw_out, dgate = _out_bwd(dx, sv["gate"], sv["y"], sv["y_rg"], sv["y_ml"], wl["w_out_g"], 0, t_big,
                                           hosted("out_proj_bwd"))
    grads.update(w_out=gw_out)
    done("out_proj_bwd")
    dq, dk, dv, dgates, d_mlo, d_mlz, g_mlng = _ml_cell_bwd(
        dy_ml, sv["u"], sv["cell"], sv["q"], sv["k"], sv["v"], sv["gcol"], sv["grow"], sv["mt"], sv["cs"], sv["ns"],
        sv["ms"], vec("ml_norm_g"), nh_ml, hosted("mlstm_cell_bwd"))
    done("mlstm_cell_bwd")
    d_mlx, g_wq, g_wk, g_wv, g_wift, g_bif, g_mlcw, g_mlcb = _ml_pre_bwd(
        dq, dk, dv, dgates, sv["gcol"], sv["u"], sv["pre"], sv["q"], sv["k"], sv["v"], wl["ml_conv_w"],
        wl["w_qkv"][0], wl["w_qkv"][1], wl["w_qkv"][2], wl["wift_pad"], t_mid, hosted("mlstm_proj_bwd"))
    done("mlstm_proj_bwd")
    d_rgx, d_rgz, g_wa, g_wx, g_ba, g_bx, g_lam, g_rgcw, g_rgcb = _rg_bwd(
        dy_rg, sv["u"], sv["h_rg"], sv["rg_gates"], wl["rg_conv_w"], rep["rg_w_a_bf"][l], rep["rg_w_x_bf"][l],
        vec("rg_lambda"), t_mid, hosted("rglru_bwd"))
    grads.update(w_qkv=jnp.stack([g_wq, g_wk, g_wv]), rg_conv_w=g_rgcw[0:CONV_WIDTH], ml_conv_w=g_mlcw[0:CONV_WIDTH],
                 wif_t=g_wift[0:8], rg_w_a=g_wa, rg_w_x=g_wx)
    acc = dict(dgate=dgate, rg_conv_b=g_rgcb, rg_b_a=g_ba, rg_b_x=g_bx, rg_lambda=g_lam, ml_conv_b=g_mlcb,
               ml_b_if=g_bif, ml_norm_g=g_mlng)
    done("rglru_bwd")
    pieces = [d_rgx, d_rgz, d_mlx, d_mlo, d_mlz]
    grads.update(w_in=_in_bwd_w(pieces, sv["hbf"], w_cols, tuple(range(nd)), _tile_for(s, 1024), hosted("in_proj_bwd_w")))
    done("in_proj_bwd_w")
    n_tiles = s // t_mid
    counts = [n_tiles // 5, n_tiles - n_tiles // 5] if split_last and n_tiles >= 5 else [n_tiles]
    in_args = (pieces, sv["x"], dx, vec("norm_g"), sv["scale"], wl["w_in_g"], 0, t_mid)
    res, at = None, 0
    for key, count in zip(("in_proj_bwd_x", "in_proj_bwd_x_rest"), counts):
        res = _in_bwd(*in_args, hosted(key), (at, count), res)
        done(key)
        at += count
    dx, dscale, dshift, g_ng = res
    acc.update(norm_g=g_ng, dshift=dshift, dscale=dscale)
    grads.update(acc=acc, dmod=jnp.concatenate([dshift[0:1], dscale[0:1], dgate[0:1]], axis=1))
    return dx, grads


def _full_qkv(qkv_g, d):
    nd, _, rows3, dh = qkv_g.shape
    nh = d // dh
    rsh = rows3 // (3 * nh)
    return qkv_g.reshape(nd, 3, nh, rsh, dh).transpose(1, 2, 0, 3, 4).reshape(3, nh, nd * rsh, dh)


def _small_weights(small, l, ml_b_if):
    nd = small.shape[0]
    sm = small[:, l]
    cw = sm[:, 0:2 * CONV_WIDTH].reshape(nd, 2, CONV_WIDTH, LANES).transpose(1, 2, 0, 3).reshape(2, CONV_WIDTH, nd * LANES)
    if_rows = (sm.shape[1] - 2 * CONV_WIDTH) * LANES // 8
    wif_t = sm[:, 2 * CONV_WIDTH:].reshape(nd, 8, if_rows).transpose(1, 0, 2).reshape(8, nd * if_rows)
    wift_pad = jnp.pad(wif_t, ((0, LANES - 8), (0, 0))).astype(BF16)
    return dict(rg_conv_w=cw[0], ml_conv_w=cw[1], wift_pad=wift_pad, wif_pad=wift_pad.T,
                bif_pad=jnp.pad(ml_b_if[l], (0, LANES - 8)).reshape(1, LANES))


def kernel(x, c, norm_g, w_ada, b_ada, w_in, rg_conv_w, rg_conv_b, rg_w_a, rg_b_a, rg_w_x, rg_b_x, rg_lambda, ml_conv_w, ml_conv_b, ml_w_q, ml_w_k, ml_w_v, ml_w_if, ml_b_if, ml_norm_g, w_out, final_g, loss_target, m_norm_g, m_w_ada, m_b_ada, m_w_in, m_rg_conv_w, m_rg_conv_b, m_rg_w_a, m_rg_b_a, m_rg_w_x, m_rg_b_x, m_rg_lambda, m_ml_conv_w, m_ml_conv_b, m_ml_w_q, m_ml_w_k, m_ml_w_v, m_ml_w_if, m_ml_b_if, m_ml_norm_g, m_w_out, m_final_g, v_norm_g, v_w_ada, v_b_ada, v_w_in, v_rg_conv_w, v_rg_conv_b, v_rg_w_a, v_rg_b_a, v_rg_w_x, v_rg_b_x, v_rg_lambda, v_ml_conv_w, v_ml_conv_b, v_ml_w_q, v_ml_w_k, v_ml_w_v, v_ml_w_if, v_ml_b_if, v_ml_norm_g, v_w_out, v_final_g):
    given = dict(locals())
    nl = w_in.shape[0]
    d = x.shape[2]
    rep = {n: given[n] for n in REPLICATED}
    rep.update(rg_w_a_bf=rg_w_a.astype(BF16), rg_w_x_bf=rg_w_x.astype(BF16))
    bf = lambda a: a.astype(BF16)

    def qkv_shard(prefix):
        return jnp.stack([given[prefix + "ml_w_q"], given[prefix + "ml_w_k"], given[prefix + "ml_w_v"]], axis=1).reshape(
            nl, -1, ml_w_q.shape[-1])

    def small_shard(prefix):
        return _small_pack(given[prefix + "rg_conv_w"], given[prefix + "ml_conv_w"], given[prefix + "ml_w_if"])

    plan = _Plan()
    qkv = qkv_shard("")
    first_ici = _gather_ici_comm([bf(w_in[0:1]), small_shard("")])
    condition = _exchange_comm([jnp.broadcast_to(c, (SUBLANES, d))], True)
    _run_comms([first_ici, condition], "gather_first")
    first_fwd = _gather_fwd_comm(first_ici.results)
    wcols = w_ada.shape[2]
    me = 4 * lax.axis_index("x") + 2 * lax.axis_index("y") + lax.axis_index("c")
    b_cols = jnp.pad(lax.dynamic_slice_in_dim(b_ada, me * wcols, wcols, axis=1), ((0, SUBLANES - nl), (0, 0)))
    mod_cols, cact_all = _ada_mod(condition.results[0], w_ada, b_cols, [first_fwd])
    w_in_first, small = first_fwd.results
    wl = [_small_weights(small, l, ml_b_if) for l in range(nl)]
    wl[0]["w_in_g"] = w_in_first

    def gather_behind(arrs, ici_host, fwd_host, then):
        ici = _gather_ici_comm(arrs)

        def pass_on():
            fwd = _gather_fwd_comm(ici.results)
            plan.host(fwd_host, fwd, lambda: then(fwd.results))

        plan.host(ici_host, ici, pass_on)

    def got_out(l):
        return lambda r: wl[l].update(w_out_g=r[0], w_qkv=_full_qkv(r[1], d))

    gather_behind([bf(w_out[0:1]), bf(qkv[0:1])], ("in_proj_fwd", 0), ("rglru_fwd", 0), got_out(0))
    for l in range(1, nl):
        gather_behind([bf(w_in[l:l + 1])], ("rglru_fwd", l - 1), ("mlstm_cell_fwd", l - 1),
                      lambda r, l=l: wl[l].update(w_in_g=r[0]))
        gather_behind([bf(w_out[l:l + 1]), bf(qkv[l:l + 1])], ("mlstm_cell_fwd", l - 1), ("out_proj_fwd", l - 1), got_out(l))

    mod_blocks = _exchange([mod_cols], False, "scatter_modulation")[0]
    mod3 = mod_blocks[:, 0:nl].transpose(1, 0, 2).reshape(nl, 3, d)
    rep["vecs"] = _vec_table(rep)
    saved, xl = [], x[0]
    for l in range(nl):
        head = (final_g.reshape(1, -1), loss_target[0]) if l == nl - 1 else None
        xl, sv = _layer_fwd(l, xl, mod3, wl[l], rep, plan, head)
        saved.append(sv)
    grad_x, loss_p, g_final = xl

    keys = ("w_in", "w_out", "w_qkv", "small")
    parity = lax.axis_index("c").astype(jnp.int32).reshape(1)
    grads, recv = [None] * nl, [None] * nl

    def small_parts(g):
        return [bf(_qkv_slots(g["w_qkv"], N_DEV)), bf(_small_slots(g))]

    def reduce_behind(l, host_layer):
        parts = [grads[l]["w_in"], grads[l]["w_out"]]
        swap = _core_swap_comm(parts)
        direct = _exchange_comm(small_parts(grads[l]), False)

        def summed():
            sums = [_pair_sum(a, o, parity, "pair_sum_%s_layer%d" % (key, l)) for key, a, o in zip(keys, parts, swap.results)]
            big = _chip_swap_comm([sums[0]])
            rest = _chip_swap_comm([sums[1]])
            plan.host(("mlstm_cell_bwd", host_layer), big)
            plan.host(("rglru_bwd", host_layer), rest,
                      lambda: recv.__setitem__(l, big.results + rest.results + direct.results))

        plan.host(("out_proj_bwd", host_layer), swap, summed)
        plan.host(("out_proj_bwd", host_layer), direct)

    first, own = {}, {}

    def reduce_own(names, parts_fn, ready_key, swap_key, chip_key):
        def go():
            parts = parts_fn()
            swap = _core_swap_comm(parts)

            def summed():
                sums = [_pair_sum(a, o, parity, "pair_sum_%s_layer0" % n) for n, a, o in zip(names, parts, swap.results)]
                chip = _chip_swap_comm(sums)
                plan.host(chip_key, chip, lambda: own.update(zip(names, chip.results)))

            plan.host(swap_key, swap, summed)

        plan.after.setdefault(ready_key, []).append(go)

    reduce_own(["w_out"], lambda: [first["w_out"]], ("out_proj_bwd", 0), ("mlstm_cell_bwd", 0), ("mlstm_proj_bwd", 0))
    reduce_own(["w_in"], lambda: [first["w_in"]], ("in_proj_bwd_w", 0), ("in_proj_bwd_x", 0), ("in_proj_bwd_x_rest", 0))

    def small_own():
        direct = _exchange_comm(small_parts(first), False)
        plan.host(("in_proj_bwd_w", 0), direct, lambda: own.update(w_qkv=direct.results[0], small=direct.results[1]))

    plan.after.setdefault(("rglru_bwd", 0), []).append(small_own)

    matrices = {}

    def reduce_matrices():
        layers = [grads[l] if l > 0 else first for l in range(nl)]
        mp = jnp.stack([jnp.stack([g["rg_w_a"], g["rg_w_x"]]) for g in layers]).reshape(N_DEV, -1, LANES).astype(BF16)
        scatter = _exchange_comm([mp], False)

        def summed():
            gather = _exchange_comm(_sum_parts(scatter.results, "sum_replicated_matrices"), True)
            plan.host(("in_proj_bwd_x", 0), gather, lambda: matrices.update(mp=gather.results[0].reshape(-1, LANES)))

        plan.host(("in_proj_bwd_w", 0), scatter, summed)

    plan.after.setdefault(("rglru_bwd", 0), []).append(reduce_matrices)

    for l in reversed(range(nl)):
        if l > 0:
            grad_x, grads[l] = _layer_bwd(l, grad_x, saved[l], wl[l], rep, plan)
            reduce_behind(l, l - 1)
        else:
            grad_x, grads[l] = _layer_bwd(l, grad_x, saved[l], wl[l], rep, plan, first, True)
    plan.flush()
    recv[0] = [own[key] for key in keys]

    shard = {p: dict(w_in=given[p + "w_in"], w_out=given[p + "w_out"], w_qkv=qkv_shard(p), small=small_shard(p))
             for p in ("", "m_", "v_")}
    res = {}
    for ki, key in enumerate(keys):
        res[key] = _reduce_adam([recv[l][ki] for l in range(nl)], shard[""][key], shard["m_"][key], shard["v_"][key],
                                "reduce_adam_" + key)

    dmods = jnp.concatenate([grads[l]["dmod"] for l in range(nl)], axis=0)
    dmod_blocks = jnp.pad(dmods.reshape(nl, N_DEV, wcols).transpose(1, 0, 2), ((0, 0), (0, SUBLANES - nl), (0, 0)))
    widen = lambda a: jnp.pad(a, ((0, 0), (0, d - a.shape[1])))
    rows = [widen(grads[l]["acc"][n][0:1]) for l in range(nl) for n in REP_ROWS] + [g_final[0:1], widen(loss_p[0:1])]
    vp = jnp.concatenate(rows + [jnp.zeros(((-len(rows)) % SUBLANES, d), F32)], axis=0)
    (dmod_recv,), (vp_all,) = _run_comms([_exchange_comm([dmod_blocks], False), _exchange_comm([vp], True)], "tail_exchange")
    res["w_ada"] = _ada_grad_adam(cact_all.T, dmod_recv[:, 0:nl].transpose(1, 0, 2), w_ada, m_w_ada, v_w_ada)
    mp_r = matrices["mp"]
    lanes = lambda a: jnp.pad(a, ((0, 0), (0, LANES - a.shape[1])))
    shaped = dict(ml_b_if=lanes, final_g=lambda a: a.reshape(1, d))
    names = [n for n in REPLICATED if n != "b_ada"] + ["b_ada"]
    rep_res, vp_r = _adam_replicated(vp_all, mp_r, {n: tuple(shaped.get(n, lambda a: a)(given[p + n]) for p in ("", "m_", "v_"))
                                                    for n in names}, nl)
    unshaped = dict(ml_b_if=lambda a: a[:, 0:ml_b_if.shape[1]], final_g=lambda a: a.reshape(d))
    rep_out = [{n: unshaped.get(n, lambda a: a)(rep_res[n][kind]) for n in names} for kind in range(4)]
    loss = vp_r[nl * len(REP_ROWS) + 1, 0]

    if_rows = ml_w_if.shape[1]
    order = ("norm_g", "w_ada", "b_ada", "w_in", "rg_conv_w", "rg_conv_b", "rg_w_a", "rg_b_a", "rg_w_x", "rg_b_x",
             "rg_lambda", "ml_conv_w", "ml_conv_b", "ml_w_q", "ml_w_k", "ml_w_v", "ml_w_if", "ml_b_if", "ml_norm_g",
             "w_out", "final_g")
    outs = [loss, grad_x[None]]
    for kind in range(4):
        qkv_k = res["w_qkv"][kind].reshape((nl, 3) + ml_w_q.shape[1:])
        rg_cw, ml_cw, wif = _small_unpack(res["small"][kind], if_rows)
        sharded = dict(w_ada=res["w_ada"][kind], w_in=res["w_in"][kind], w_out=res["w_out"][kind], ml_w_q=qkv_k[:, 0],
                       ml_w_k=qkv_k[:, 1], ml_w_v=qkv_k[:, 2], rg_conv_w=rg_cw, ml_conv_w=ml_cw, ml_w_if=wif)
        for n in order:
            outs.append(sharded[n] if n in sharded else rep_out[kind][n])
    return tuple(outs)
```

```python
import functools

import jax
import jax.numpy as jnp
from jax import lax
from jax.experimental import pallas as pl
from jax.experimental.pallas import tpu as pltpu

F32 = jnp.float32
BF16 = jnp.bfloat16
MESH_AXES = ("x", "y", "c")
N_DEV = 8
EPS = 1e-6
RG_C = 8.0
ML_CHUNK = 128
CONV_WIDTH = 4
ADAM_LR = 0.001
ADAM_B1 = 0.9
ADAM_B2 = 0.999
ADAM_EPS = 1e-08
ADAM_WD = 0.01
ADAM_STEP = 10
NEG_BIG = -1e30
LANES = 128
SUBLANES = 8
VMEM_LIMIT = 56 * 1024 * 1024
HI = lax.Precision.HIGHEST


def _params(n_grid):
    return pltpu.CompilerParams(dimension_semantics=("arbitrary",) * n_grid, vmem_limit_bytes=VMEM_LIMIT)


def _mm(a, b):
    return jnp.dot(a.astype(BF16), b.astype(BF16), preferred_element_type=F32)


def _mm_nt(a, b):
    return lax.dot_general(a.astype(BF16), b.astype(BF16), (((1,), (1,)), ((), ())), preferred_element_type=F32)


def _mm_tn(a, b):
    return lax.dot_general(a.astype(BF16), b.astype(BF16), (((0,), (0,)), ((), ())), preferred_element_type=F32)


def _mm_hi(a, b):
    return jnp.dot(a, b, precision=HI, preferred_element_type=F32)


def _sigmoid(x):
    return 1.0 / (1.0 + jnp.exp(-x))


def _softplus(x):
    return jnp.maximum(x, 0.0) + jnp.log(1.0 + jnp.exp(-jnp.abs(x)))


def _neg_expm1(x):
    poly = -x * (1.0 + x * (0.5 + x * (1.0 / 6.0 + x * (1.0 / 24.0 + x * (1.0 / 120.0)))))
    return jnp.where(jnp.abs(x) < 0.05, poly, 1.0 - jnp.exp(x))


def _iota(shape, dim):
    return lax.broadcasted_iota(jnp.int32, shape, dim)


def _colsum(x):
    return jnp.sum(x, axis=0, keepdims=True)


def _rowsum(x):
    return jnp.sum(x, axis=1, keepdims=True)


def _col(x, j):
    return _rowsum(jnp.where(_iota(x.shape, 1) == j, x, 0.0))


def _row(x, j):
    return _colsum(jnp.where(_iota(x.shape, 0) == j, x, 0.0))


def _shift_down(x, j, prev8):
    if j == 0:
        return x
    t = x.shape[0]
    main = jnp.where(_iota(x.shape, 0) >= j, pltpu.roll(x, j, 0), 0.0)
    fix = jnp.where(_iota(prev8.shape, 0) < j, pltpu.roll(prev8, j, 0), 0.0)
    return jnp.concatenate([main[0:SUBLANES] + fix, main[SUBLANES:t]], axis=0)


def _shift_up(x, j, next8):
    if j == 0:
        return x
    t = x.shape[0]
    main = jnp.where(_iota(x.shape, 0) < t - j, pltpu.roll(x, t - j, 0), 0.0)
    fix = jnp.where(_iota(next8.shape, 0) >= SUBLANES - j, pltpu.roll(next8, SUBLANES - j, 0), 0.0)
    return jnp.concatenate([main[0:t - SUBLANES], main[t - SUBLANES:t] + fix], axis=0)


def _conv(x, prev8, w_ref):
    y = w_ref[CONV_WIDTH - 1:CONV_WIDTH, :] * x
    for j in range(1, CONV_WIDTH):
        y = y + w_ref[CONV_WIDTH - 1 - j:CONV_WIDTH - j, :] * _shift_down(x, j, prev8)
    return y


def _conv_bwd(dy, x, next8, w_ref, gw_ref):
    dx = None
    for j in range(CONV_WIDTH):
        k = CONV_WIDTH - 1 - j
        up = _shift_up(dy, j, next8)
        gw_ref[k:k + 1, :] += _colsum(up * x)
        term = w_ref[k:k + 1, :] * up
        dx = term if dx is None else dx + term
    return dx


def _scan_into(a, b, carry, out_ref, reverse):
    t, c = a.shape
    groups = t // SUBLANES
    a3 = a.reshape(groups, SUBLANES, c)
    b3 = b.reshape(groups, SUBLANES, c)
    sub = _iota(a3.shape, 1)
    for step in (1, 2, 4):
        keep = sub < SUBLANES - step if reverse else sub >= step
        shift = SUBLANES - step if reverse else step
        a_s = jnp.where(keep, pltpu.roll(a3, shift, 1), 1.0)
        b_s = jnp.where(keep, pltpu.roll(b3, shift, 1), 0.0)
        b3 = a3 * b_s + b3
        a3 = a3 * a_s
    for g in (reversed(range(groups)) if reverse else range(groups)):
        rows = slice(g * SUBLANES, (g + 1) * SUBLANES)
        out_ref[rows, :] = b3[g] + a3[g] * carry
        edge = g * SUBLANES if reverse else (g + 1) * SUBLANES - 1
        carry = out_ref[edge:edge + 1, :]


def _blockdiag(x, w_ref, transpose_w=False):
    nh, dh, _ = w_ref.shape
    outs = []
    for h in range(nh):
        xs = x[:, h * dh:(h + 1) * dh]
        outs.append(_mm_nt(xs, w_ref[h]) if transpose_w else _mm(xs, w_ref[h]))
    return jnp.concatenate(outs, axis=1)


def _rg_gates(xc, wa_ref, ba_ref, wx_ref, bx_ref, lam_ref):
    r = _sigmoid(_blockdiag(xc, wa_ref) + ba_ref[...])
    ig = _sigmoid(_blockdiag(xc, wx_ref) + bx_ref[...])
    sp = _softplus(-lam_ref[...])
    log_a = -RG_C * r * sp
    a = jnp.exp(log_a)
    beta = jnp.sqrt(_neg_expm1(2.0 * log_a))
    return r, ig, sp, a, beta


def _bcast8(row):
    return jnp.broadcast_to(row, (SUBLANES, row.shape[1]))


def _full(shape):
    nd = len(shape)
    return pl.BlockSpec(shape, lambda *_: (0,) * nd)


class _Comm:
    def __init__(self, arrays, out_shapes, sems, start, finish, aliases=()):
        self.arrays, self.out_shapes, self.sems = list(arrays), list(out_shapes), list(sems)
        self.start, self.finish, self.aliases = start, finish, tuple(aliases)
        self.results = None


class _RowOf:
    def __init__(self, ref, k):
        self.ref, self.k = ref, k

    def __getitem__(self, idx):
        cols = slice(None) if idx is Ellipsis else idx[1]
        return self.ref[0, self.k:self.k + 1, cols]


class _PartOf:
    def __init__(self, ref, rows=None, cols=None, lead=None):
        self.ref, self.rows, self.cols, self.lead = ref, rows, cols, lead
        if rows is not None:
            self.shape = (rows.stop - rows.start,) + tuple(ref.shape[1:])

    def _at(self, idx):
        if self.lead is not None:
            return (self.lead,) + tuple(idx[1:])
        if self.cols is not None:
            return (slice(None), self.cols)
        return (self.rows, slice(None) if idx is Ellipsis else idx[1])

    def __getitem__(self, idx):
        return self.ref[self._at(idx)]

    def __setitem__(self, idx, value):
        self.ref[self._at(idx)] = value


def _vec(table, layer, k):
    return ("row", table, layer, k)


def _is_row(arg):
    return isinstance(arg, tuple) and len(arg) == 4 and arg[0] == "row"


def _call(body, comms, *, name, grid, in_specs, out_specs, out_shape, args, scratch_shapes=(), aliases=None):
    comms = [cm for cm in (comms or []) if cm is not None]
    rows = {i: a[3] for i, a in enumerate(args) if _is_row(a)}
    in_specs = [pl.BlockSpec((1,) + a[1].shape[1:], functools.partial(lambda layer, *_: (layer, 0, 0), a[2]))
                if _is_row(a) else sp for a, sp in zip(args, in_specs)]
    args = tuple(a[1] if _is_row(a) else a for a in args)
    n_in, n_out, n_sc = len(args), len(out_shape), len(scratch_shapes)
    c_arrays = [a for cm in comms for a in cm.arrays]
    c_outs = [o for cm in comms for o in cm.out_shapes]
    c_sems = [sm for cm in comms for sm in cm.sems]
    aliases, a_at, o_at = dict(aliases or {}), n_in, n_out
    for cm in comms:
        for (i, j) in cm.aliases:
            aliases[a_at + i] = o_at + j
        a_at += len(cm.arrays)
        o_at += len(cm.out_shapes)

    def wrapped(*refs):
        ins, c_in = refs[:n_in], refs[n_in:n_in + len(c_arrays)]
        ins = [_RowOf(r, rows[i]) if i in rows else r for i, r in enumerate(ins)]
        at = n_in + len(c_arrays)
        outs, c_out = refs[at:at + n_out], refs[at + n_out:at + n_out + len(c_outs)]
        at += n_out + len(c_outs)
        scr, sems = refs[at:at + n_sc], refs[at + n_sc:]
        views, ia, io, isem = [], 0, 0, 0
        for cm in comms:
            views.append((c_in[ia:ia + len(cm.arrays)], c_out[io:io + len(cm.out_shapes)], sems[isem:isem + len(cm.sems)]))
            ia, io, isem = ia + len(cm.arrays), io + len(cm.out_shapes), isem + len(cm.sems)
        if comms:
            @pl.when(pl.program_id(0) == 0)
            def _():
                for cm, view in zip(comms, views):
                    cm.start(*view)

        body(*ins, *outs, *scr)
        if comms:
            @pl.when(pl.program_id(0) == grid[0] - 1)
            def _():
                for cm, view in zip(comms, views):
                    cm.finish(*view)

    hbm = pl.BlockSpec(memory_space=pl.ANY)
    res = pl.pallas_call(
        wrapped, name=name, grid=grid,
        in_specs=list(in_specs) + [hbm] * len(c_arrays), out_specs=list(out_specs) + [hbm] * len(c_outs),
        out_shape=list(out_shape) + c_outs, scratch_shapes=list(scratch_shapes) + c_sems,
        input_output_aliases=aliases, compiler_params=_params(len(grid)),
    )(*args, *c_arrays)
    at = n_out
    for cm in comms:
        cm.results = list(res[at:at + len(cm.out_shapes)])
        at += len(cm.out_shapes)
    return list(res[:n_out])


def _join_columns(w_ref, wcat):
    nd, _, _, w = w_ref.shape

    @pl.when(pl.program_id(0) == 0)
    def _():
        for j in range(nd):
            wcat[:, j * w:(j + 1) * w] = w_ref[j, 0]


def _in_fwd(x, ng, scale, shift, w_in_g, layer, tile, comms=None):
    s, d = x.shape
    nd, _, _, w = w_in_g.shape

    def body(x_ref, ng_ref, sc_ref, sh_ref, w_ref, u_ref, h_ref, wcat):
        _join_columns(w_ref, wcat)
        xv = x_ref[...]
        rs = lax.rsqrt(jnp.mean(xv * xv, axis=1, keepdims=True) + EPS)
        hb = (xv * rs * ng_ref[...] * (1.0 + sc_ref[...]) + sh_ref[...]).astype(BF16)
        h_ref[...] = hb
        u_ref[...] = jnp.dot(hb, wcat[...], preferred_element_type=F32)

    return _call(
        body, comms, name="in_proj_fwd", grid=(s // tile,),
        in_specs=[pl.BlockSpec((tile, d), lambda i: (i, 0)), _full((1, d)), _full((1, d)), _full((1, d)),
                  pl.BlockSpec((nd, 1, d, w), lambda i: (0, layer, 0, 0), pipeline_mode=pl.Buffered(1))],
        out_specs=[pl.BlockSpec((tile, nd * w), lambda i: (i, 0)), pl.BlockSpec((tile, d), lambda i: (i, 0))],
        out_shape=[jax.ShapeDtypeStruct((s, nd * w), F32), jax.ShapeDtypeStruct((s, d), BF16)],
        scratch_shapes=[pltpu.VMEM((d, nd * w), BF16)],
        args=(x, ng, scale, shift, w_in_g))


def _rg_fwd(u, d, conv_w, conv_b, w_a, b_a, w_x, b_x, lam, tile, comms=None):
    s = u.shape[0]

    def body(x_ref, z_ref, cw_ref, cb_ref, wa_ref, ba_ref, wx_ref, bx_ref, lam_ref,
             h_ref, y_ref, xc_ref, r_ref, i_ref, a_ref, beta_ref, prev8, hcar):
        @pl.when(pl.program_id(0) == 0)
        def _():
            prev8[...] = jnp.zeros_like(prev8)
            hcar[...] = jnp.zeros_like(hcar)

        x = x_ref[...]
        xc = _conv(x, prev8[...], cw_ref) + cb_ref[...]
        prev8[...] = x[tile - SUBLANES:tile, :]
        r, ig, _, a, beta = _rg_gates(xc, wa_ref, ba_ref, wx_ref, bx_ref, lam_ref)
        xc_ref[...] = xc
        r_ref[...] = r
        i_ref[...] = ig
        a_ref[...] = a
        beta_ref[...] = beta
        _scan_into(a, beta * ig * xc, hcar[SUBLANES - 1:SUBLANES, :], h_ref, False)
        h = h_ref[...]
        hcar[...] = h[tile - SUBLANES:tile, :]
        z = z_ref[...]
        y_ref[...] = (h * z * _sigmoid(z)).astype(BF16)

    vec = _full((1, d))
    return _call(
        body, comms, name="rglru_fwd", grid=(s // tile,),
        in_specs=[pl.BlockSpec((tile, d), lambda i: (i, 0)), pl.BlockSpec((tile, d), lambda i: (i, 1)),
                  _full(conv_w.shape), vec, _full(w_a.shape), vec, _full(w_x.shape), vec, vec],
        out_specs=[pl.BlockSpec((tile, d), lambda i: (i, 0))] * 7,
        out_shape=[jax.ShapeDtypeStruct((s, d), F32), jax.ShapeDtypeStruct((s, d), BF16)] + [jax.ShapeDtypeStruct((s, d), F32)] * 5,
        scratch_shapes=[pltpu.VMEM((SUBLANES, d), F32), pltpu.VMEM((SUBLANES, d), F32)],
        args=(u, u, conv_w, conv_b, w_a, b_a, w_x, b_x, lam))


def _ml_pre(u, d, conv_w, conv_b, w_q, w_k, w_v, wif, bif, tile, comms=None):
    s = u.shape[0]
    nh = w_q.shape[0]

    def body(x_ref, cw_ref, cb_ref, wq_ref, wk_ref, wv_ref, wif_ref, bif_ref, q_ref, k_ref, v_ref, g_ref, pre_ref, prev8):
        @pl.when(pl.program_id(0) == 0)
        def _():
            prev8[...] = jnp.zeros_like(prev8)

        x = x_ref[...]
        pre = _conv(x, prev8[...], cw_ref) + cb_ref[...]
        prev8[...] = x[tile - SUBLANES:tile, :]
        xc = pre * _sigmoid(pre)
        q = _blockdiag(xc, wq_ref)
        k = _blockdiag(xc, wk_ref)
        v = _blockdiag(x, wv_ref)
        pre_ref[...] = pre
        q_ref[...] = q
        k_ref[...] = k
        v_ref[...] = v
        g = _mm(q, wif_ref[0:d, :]) + _mm(k, wif_ref[d:2 * d, :]) + _mm(v, wif_ref[2 * d:3 * d, :]) + bif_ref[...]
        lane = _iota(g.shape, 1)
        gl = jnp.where(lane < 4, g, jnp.where(lane < 8, -_softplus(-g), 0.0))
        tri = jnp.where(_iota((ML_CHUNK, ML_CHUNK), 1) <= _iota((ML_CHUNK, ML_CHUNK), 0), 1.0, 0.0)
        cums = [_mm_hi(tri, gl[c * ML_CHUNK:(c + 1) * ML_CHUNK, :]) for c in range(tile // ML_CHUNK)]
        cum = cums[0] if len(cums) == 1 else jnp.concatenate(cums, axis=0)
        g_ref[...] = gl + jnp.where((lane >= 8) & (lane < 12), pltpu.roll(cum, 4, 1), 0.0)

    vec = _full((1, d))
    return _call(
        body, comms, name="mlstm_proj_fwd", grid=(s // tile,),
        in_specs=[pl.BlockSpec((tile, d), lambda i: (i, 2)), _full(conv_w.shape), vec,
                  _full(w_q.shape), _full(w_k.shape), _full(w_v.shape), _full(wif.shape), _full((1, LANES))],
        out_specs=[pl.BlockSpec((tile, d), lambda i: (i, 0))] * 3 + [pl.BlockSpec((tile, LANES), lambda i: (i, 0)),
                                                                     pl.BlockSpec((tile, d), lambda i: (i, 0))],
        out_shape=[jax.ShapeDtypeStruct((s, d), F32)] * 3 + [jax.ShapeDtypeStruct((s, LANES), F32),
                                                             jax.ShapeDtypeStruct((s, d), F32)],
        scratch_shapes=[pltpu.VMEM((SUBLANES, d), F32)],
        args=(u, conv_w, conv_b, w_q, w_k, w_v, wif, bif))


CELL_CHUNKS_PER_STEP = 4
CELL_BWD_CHUNKS_PER_STEP = 2


def _cell_chunk(h, nh, q_ref, k_ref, v_ref, gc, gr, m_prev, c_h, n_h, m_t=None, r0=0):
    lc = ML_CHUNK
    dh = q_ref.shape[1] // nh
    sl = slice(h * dh, (h + 1) * dh)
    qh = q_ref[r0:r0 + lc, sl]
    kh = k_ref[r0:r0 + lc, sl] * (dh ** -0.5)
    vh = v_ref[r0:r0 + lc, sl]
    li_c = _col(gc, h)
    b_c = _col(gc, 8 + h)
    lib_r = _row(gr, h) - _row(gr, 8 + h)
    b_last = _colsum(jnp.where(_iota((lc, 1), 0) == lc - 1, b_c, 0.0))
    causal = _iota((lc, lc), 1) <= _iota((lc, lc), 0)
    dmat = jnp.where(causal, b_c + lib_r, NEG_BIG)
    m_inter = b_c + m_prev
    if m_t is None:
        m_t = jnp.maximum(m_inter, jnp.max(dmat, axis=1, keepdims=True))
    w_intra = jnp.exp(dmat - m_t)
    w_inter = jnp.exp(m_inter - m_t)
    amat = _mm_nt(qh, kh)
    smat = amat * w_intra
    qc = _mm(qh, c_h)
    qn = _rowsum(qh * n_h)
    den = _rowsum(smat) + w_inter * qn
    gst = b_last - b_c + li_c
    m_new = jnp.maximum(b_last + m_prev, jnp.max(gst, axis=0, keepdims=True))
    w_state = jnp.exp(gst - m_new)
    decay = jnp.exp(b_last + m_prev - m_new)
    return dict(sl=sl, qh=qh, kh=kh, vh=vh, m_t=m_t, w_intra=w_intra, w_inter=w_inter, smat=smat, qc=qc, qn=qn,
                den=den, m_new=m_new, w_state=w_state, decay=decay)


def _ml_cell_fwd(q, k, v, gcol, grow, u, ng, nh, comms=None):
    s, d = q.shape
    lc = ML_CHUNK
    nc = s // lc
    dh = d // nh

    per = CELL_CHUNKS_PER_STEP if nc % CELL_CHUNKS_PER_STEP == 0 else 1

    def body(q_ref, k_ref, v_ref, gc_ref, gr_ref, o_ref, z_ref, ng_ref,
             cell_ref, y_ref, cs_ref, ns_ref, ms_ref, mt_ref, c_sc, n_sc, m_sc):
        @pl.when(pl.program_id(0) == 0)
        def _():
            c_sc[...] = jnp.zeros_like(c_sc)
            n_sc[...] = jnp.zeros_like(n_sc)
            m_sc[...] = jnp.zeros_like(m_sc)

        lane = _iota((lc, LANES), 1)
        for cc in range(per):
            rows = slice(cc * lc, (cc + 1) * lc)
            gc = gc_ref[rows, :]
            gr = gr_ref[:, rows]
            mt_acc = jnp.zeros((lc, LANES), F32)
            for h in range(nh):
                c_h = c_sc[h]
                n_h = n_sc[h, 0:1, :]
                m_prev = jnp.max(m_sc[h, 0:1, :], axis=1, keepdims=True)
                cs_ref[cc, h] = c_h
                ns_ref[cc, h] = n_sc[h]
                ms_ref[cc, h] = m_sc[h]
                t = _cell_chunk(h, nh, q_ref, k_ref, v_ref, gc, gr, m_prev, c_h, n_h, r0=cc * lc)
                sl = t["sl"]
                num = _mm(t["smat"], t["vh"]) + t["w_inter"] * t["qc"]
                cell_h = num / jnp.maximum(jnp.abs(t["den"]), jnp.exp(-t["m_t"]))
                mt_acc = jnp.where(lane == h, t["m_t"], mt_acc)
                kw = t["kh"] * t["w_state"]
                c_sc[h] = t["decay"] * c_h + _mm_tn(kw, t["vh"])
                n_sc[h] = _bcast8(t["decay"] * n_h + _colsum(kw))
                m_sc[h] = jnp.broadcast_to(t["m_new"], (SUBLANES, LANES))
                hg = _sigmoid(o_ref[rows, sl]) * cell_h
                hn = hg * lax.rsqrt(jnp.mean(hg * hg, axis=1, keepdims=True) + EPS)
                z = z_ref[rows, sl]
                cell_ref[rows, sl] = cell_h
                y_ref[rows, sl] = (hn * ng_ref[:, sl] * z * _sigmoid(z)).astype(BF16)
            mt_ref[rows, :] = mt_acc

    tok = pl.BlockSpec((per * lc, d), lambda c: (c, 0))
    return _call(
        body, comms, name="mlstm_cell_fwd", grid=(nc // per,),
        in_specs=[tok, tok, tok, pl.BlockSpec((per * lc, LANES), lambda c: (c, 0)),
                  pl.BlockSpec((16, per * lc), lambda c: (0, c)),
                  pl.BlockSpec((per * lc, d), lambda c: (c, 3)), pl.BlockSpec((per * lc, d), lambda c: (c, 4)), _full((1, d))],
        out_specs=[tok, tok, pl.BlockSpec((per, nh, dh, dh), lambda c: (c, 0, 0, 0)),
                   pl.BlockSpec((per, nh, SUBLANES, dh), lambda c: (c, 0, 0, 0)),
                   pl.BlockSpec((per, nh, SUBLANES, LANES), lambda c: (c, 0, 0, 0)),
                   pl.BlockSpec((per * lc, LANES), lambda c: (c, 0))],
        out_shape=[jax.ShapeDtypeStruct((s, d), F32), jax.ShapeDtypeStruct((s, d), BF16),
                   jax.ShapeDtypeStruct((nc, nh, dh, dh), F32), jax.ShapeDtypeStruct((nc, nh, SUBLANES, dh), F32),
                   jax.ShapeDtypeStruct((nc, nh, SUBLANES, LANES), F32), jax.ShapeDtypeStruct((s, LANES), F32)],
        scratch_shapes=[pltpu.VMEM((nh, dh, dh), F32), pltpu.VMEM((nh, SUBLANES, dh), F32),
                        pltpu.VMEM((nh, SUBLANES, LANES), F32)],
        args=(q, k, v, gcol, grow, u, u, ng))


def _out_fwd(x, y_rg, y_ml, gate, w_out_g, layer, tile, comms=None, head=None):
    s, d = x.shape
    nd, _, r, _ = w_out_g.shape

    def body(x_ref, yr_ref, ym_ref, g_ref, w_ref, *rest):
        ycat = jnp.concatenate([yr_ref[...].astype(BF16), ym_ref[...].astype(BF16)], axis=1)
        acc = jnp.dot(ycat, w_ref[...].reshape(nd * r, d), preferred_element_type=F32)
        xn = x_ref[...] + g_ref[...] * acc
        if head is None:
            xn_ref, y_ref = rest
            y_ref[...] = acc
            xn_ref[...] = xn
            return
        fg_ref, t_ref, y_ref, dx_ref, loss_ref, gg_ref = rest
        y_ref[...] = acc

        @pl.when(pl.program_id(0) == 0)
        def _():
            loss_ref[...] = jnp.zeros_like(loss_ref)
            gg_ref[...] = jnp.zeros_like(gg_ref)

        fg = fg_ref[...]
        rs = lax.rsqrt(jnp.mean(xn * xn, axis=1, keepdims=True) + EPS)
        xh = xn * rs
        e = xh * fg - t_ref[...]
        loss_ref[...] += jnp.broadcast_to(_colsum(_rowsum(e * e)) * (0.5 / d), loss_ref.shape)
        dy = e * (1.0 / d)
        gg_ref[...] += _bcast8(_colsum(dy * xh))
        dxh = dy * fg
        dx_ref[...] = rs * (dxh - xh * jnp.mean(dxh * xh, axis=1, keepdims=True))

    tok = pl.BlockSpec((tile, d), lambda i: (i, 0))
    in_specs = [tok, tok, tok, _full((1, d)), pl.BlockSpec((nd, 1, r, d), lambda i: (0, layer, 0, 0))]
    if head is None:
        return _call(body, comms, name="out_proj_fwd", grid=(s // tile,), in_specs=in_specs, out_specs=[tok, tok],
                     out_shape=[jax.ShapeDtypeStruct((s, d), F32)] * 2, args=(x, y_rg, y_ml, gate, w_out_g))
    return _call(
        body, comms, name="out_proj_loss", grid=(s // tile,),
        in_specs=in_specs + [_full((1, d)), tok],
        out_specs=[tok, tok, _full((SUBLANES, LANES)), _full((SUBLANES, d))],
        out_shape=[jax.ShapeDtypeStruct((s, d), F32)] * 2 + [jax.ShapeDtypeStruct((SUBLANES, LANES), F32),
                                                             jax.ShapeDtypeStruct((SUBLANES, d), F32)],
        args=(x, y_rg, y_ml, gate, w_out_g, *head))


def _ml_out_stage_bwd(dy, cell, o, z, ng):
    so = _sigmoid(o)
    hg = so * cell
    rinv = lax.rsqrt(jnp.mean(hg * hg, axis=1, keepdims=True) + EPS)
    hn = hg * rinv
    sz = _sigmoid(z)
    dz = dy * hn * ng * (sz + z * sz * (1.0 - sz))
    dymid = dy * z * sz
    dhn = dymid * ng
    dhg = rinv * (dhn - hn * jnp.mean(dhn * hn, axis=1, keepdims=True))
    return dz, dhg * cell * so * (1.0 - so), dhg * so, _colsum(dymid * hn)


def _out_bwd(dxo, gate, y, y_rg, y_ml, w_out_g, layer, tile, comms=None):
    s, d = dxo.shape
    nd, _, r, _ = w_out_g.shape

    def body(dx_ref, g_ref, y_ref, yr_ref, ym_ref, w_ref, dyr_ref, dym_ref, gw_ref, dg_ref):
        @pl.when(pl.program_id(0) == 0)
        def _():
            gw_ref[...] = jnp.zeros_like(gw_ref)
            dg_ref[...] = jnp.zeros_like(dg_ref)

        dxv = dx_ref[...]
        dg_ref[...] += _bcast8(_colsum(dxv * y_ref[...]))
        dyb = (dxv * g_ref[...]).astype(BF16)
        dycat = lax.dot_general(dyb, w_ref[...].reshape(nd * r, d), (((1,), (1,)), ((), ())), preferred_element_type=F32)
        dyr_ref[...] = dycat[:, 0:d]
        dym_ref[...] = dycat[:, d:2 * d]
        ycat = jnp.concatenate([yr_ref[...].astype(BF16), ym_ref[...].astype(BF16)], axis=1)
        gw_ref[...] += lax.dot_general(ycat, dyb, (((0,), (0,)), ((), ())), preferred_element_type=F32).reshape(nd, r, d)

    tok = pl.BlockSpec((tile, d), lambda i: (i, 0))
    return _call(
        body, comms, name="out_proj_bwd", grid=(s // tile,),
        in_specs=[tok, _full((1, d)), tok, tok, tok, pl.BlockSpec((nd, 1, r, d), lambda i: (0, layer, 0, 0))],
        out_specs=[tok, tok, _full((nd, r, d)), _full((SUBLANES, d))],
        out_shape=[jax.ShapeDtypeStruct((s, d), F32)] * 2 + [jax.ShapeDtypeStruct((nd, r, d), F32),
                                                             jax.ShapeDtypeStruct((SUBLANES, d), F32)],
        args=(dxo, gate, y, y_rg, y_ml, w_out_g))


def _ml_cell_bwd(dy_ml, u, cell, q, k, v, gcol, grow, mt, cs, ns, ms, ng, nh, comms=None):
    s, d = q.shape
    lc = ML_CHUNK
    nc = s // lc
    dh = d // nh

    per = CELL_BWD_CHUNKS_PER_STEP if nc % CELL_BWD_CHUNKS_PER_STEP == 0 else 1

    def body(*refs):
        gng_ref, dc_sc, dn_sc = refs[20], refs[21], refs[22]

        @pl.when(pl.program_id(0) == 0)
        def _():
            dc_sc[...] = jnp.zeros_like(dc_sc)
            dn_sc[...] = jnp.zeros_like(dn_sc)
            gng_ref[...] = jnp.zeros_like(gng_ref)

        for cc in reversed(range(per)):
            rows = slice(cc * lc, (cc + 1) * lc)
            views = [refs[at] if at == 13 else _PartOf(refs[at], cols=rows) if at == 8 else
                     _PartOf(refs[at], lead=cc) if at in (10, 11, 12) else _PartOf(refs[at], rows=rows) for at in range(20)]
            chunk(*views, gng_ref, dc_sc, dn_sc)

    def chunk(dy_ref, o_ref, z_ref, cell_ref, q_ref, k_ref, v_ref, gc_ref, gr_ref, mt_ref, cs_ref, ns_ref, ms_ref,
              ng_ref, dq_ref, dk_ref, dv_ref, dg_ref, do_ref, dz_ref, gng_ref, dc_sc, dn_sc):
        gc = gc_ref[...]
        gr = gr_ref[...]
        mtv = mt_ref[...]
        lane = _iota((lc, LANES), 1)
        rowv = _iota((lc, 1), 0)
        dg_acc = jnp.zeros((lc, LANES), F32)
        for h in range(nh):
            c_h = cs_ref[0, h]
            n_h = ns_ref[0, h, 0:1, :]
            m_prev = jnp.max(ms_ref[0, h, 0:1, :], axis=1, keepdims=True)
            t = _cell_chunk(h, nh, q_ref, k_ref, v_ref, gc, gr, m_prev, c_h, n_h, m_t=_col(mtv, h))
            sl, qh, kh, vh = t["sl"], t["qh"], t["kh"], t["vh"]
            w_intra, w_inter, smat, w_state, decay = t["w_intra"], t["w_inter"], t["smat"], t["w_state"], t["decay"]
            cell_h = cell_ref[:, sl]
            dz, do, dcell, gng = _ml_out_stage_bwd(dy_ref[:, sl], cell_h, o_ref[:, sl], z_ref[:, sl], ng_ref[:, sl])
            dz_ref[:, sl] = dz.astype(BF16)
            do_ref[:, sl] = do.astype(BF16)
            gng_ref[:, sl] += _bcast8(gng)
            eneg = jnp.exp(-t["m_t"])
            aden = jnp.abs(t["den"])
            nst = jnp.maximum(aden, eneg)
            dnum = dcell / nst
            dden = jnp.where(aden > eneg, -_rowsum(cell_h * dcell) / nst * jnp.sign(t["den"]), 0.0)
            pmat = _mm_nt(dnum, vh) + dden
            damat = pmat * w_intra
            gmat = pmat * smat
            wdn = w_inter * dnum
            wdd = w_inter * dden
            dqh = _mm(damat, kh) + _mm_nt(wdn, c_h) + wdd * n_h
            dkh = _mm_tn(damat, qh)
            dvh = _mm_tn(smat, dnum)
            dw_inter = _rowsum(dnum * t["qc"]) + dden * t["qn"]
            dcn = dc_sc[h]
            dnn = dn_sc[h, 0:1, :]
            kw = kh * w_state
            dkw = _mm_nt(vh, dcn) + dnn
            dvh = dvh + _mm(kw, dcn)
            dkh = dkh + dkw * w_state
            dgst = _rowsum(dkw * kh) * w_state
            ddecay = _colsum(_rowsum(dcn * c_h)) + _rowsum(dnn * n_h)
            db_last = _colsum(dgst) + ddecay * decay
            rs_g = _rowsum(gmat)
            cs_g = _rowsum(gmat.T)
            db = rs_g - cs_g + dw_inter * w_inter - dgst + jnp.where(rowv == lc - 1, db_last, 0.0)
            dli = cs_g + dgst
            dc_sc[h] = decay * dcn + _mm_tn(qh, wdn)
            dn_sc[h] = _bcast8(decay * dnn + _colsum(qh * wdd))
            dq_ref[:, sl] = dqh
            dk_ref[:, sl] = dkh * (dh ** -0.5)
            dv_ref[:, sl] = dvh
            dg_acc = jnp.where(lane == h, dli, jnp.where(lane == 4 + h, db, dg_acc))
        dg_ref[...] = dg_acc

    rev = lambda c: nc // per - 1 - c
    tok = pl.BlockSpec((per * lc, d), lambda c: (rev(c), 0))
    g128 = pl.BlockSpec((per * lc, LANES), lambda c: (rev(c), 0))
    return _call(
        body, comms, name="mlstm_cell_bwd", grid=(nc // per,),
        in_specs=[tok, pl.BlockSpec((per * lc, d), lambda c: (rev(c), 3)), pl.BlockSpec((per * lc, d), lambda c: (rev(c), 4)),
                  tok, tok, tok, tok, g128, pl.BlockSpec((16, per * lc), lambda c: (0, rev(c))), g128,
                  pl.BlockSpec((per, nh, dh, dh), lambda c: (rev(c), 0, 0, 0)),
                  pl.BlockSpec((per, nh, SUBLANES, dh), lambda c: (rev(c), 0, 0, 0)),
                  pl.BlockSpec((per, nh, SUBLANES, LANES), lambda c: (rev(c), 0, 0, 0)), _full((1, d))],
        out_specs=[tok, tok, tok, g128, tok, tok, _full((SUBLANES, d))],
        out_shape=[jax.ShapeDtypeStruct((s, d), F32)] * 3 + [jax.ShapeDtypeStruct((s, LANES), F32)]
        + [jax.ShapeDtypeStruct((s, d), BF16)] * 2 + [jax.ShapeDtypeStruct((SUBLANES, d), F32)],
        scratch_shapes=[pltpu.VMEM((nh, dh, dh), F32), pltpu.VMEM((nh, SUBLANES, dh), F32)],
        args=(dy_ml, u, u, cell, q, k, v, gcol, grow, mt, cs, ns, ms, ng))


def _halo_spec(d, tile, nt, col):
    per = tile // SUBLANES
    return pl.BlockSpec((SUBLANES, d), lambda i: (jnp.maximum((nt - 1 - i) * per - 1, 0), col))


def _ml_pre_bwd(dq, dk, dv, dgates, gcol, u, pre, q, k, v, conv_w, w_q, w_k, w_v, wif_t, tile, comms=None):
    s, d = dq.shape
    nt = s // tile
    nh, dh, _ = w_q.shape

    def body(dq_ref, dk_ref, dv_ref, dg_ref, gc_ref, x_ref, pre_ref, q_ref, k_ref, v_ref, cw_ref,
             wq_ref, wk_ref, wv_ref, wift_ref,
             dx_ref, gwq_ref, gwk_ref, gwv_ref, gwif_ref, gbif_ref, gcw_ref, gcb_ref, next8):
        @pl.when(pl.program_id(0) == 0)
        def _():
            next8[...] = jnp.zeros_like(next8)
            for ref in (gwq_ref, gwk_ref, gwv_ref, gwif_ref, gbif_ref, gcw_ref, gcb_ref):
                ref[...] = jnp.zeros_like(ref)

        x = x_ref[...]
        pre = pre_ref[...]
        sg = _sigmoid(pre)
        xc = pre * sg
        dgc = dg_ref[...]
        lane = _iota(dgc.shape, 1)
        utri = jnp.where(_iota((ML_CHUNK, ML_CHUNK), 0) <= _iota((ML_CHUNK, ML_CHUNK), 1), 1.0, 0.0)
        rcs = [_mm_hi(utri, dgc[c * ML_CHUNK:(c + 1) * ML_CHUNK, :]) for c in range(tile // ML_CHUNK)]
        rc = rcs[0] if len(rcs) == 1 else jnp.concatenate(rcs, axis=0)
        dgates_v = jnp.where(lane < 4, dgc, jnp.where(lane < 8, rc * (1.0 - jnp.exp(gc_ref[...])), 0.0))
        dgb = dgates_v.astype(BF16)
        gbif_ref[...] += jnp.broadcast_to(_colsum(dgates_v), gbif_ref.shape)
        ext = jnp.dot(dgb, wift_ref[...], preferred_element_type=F32)
        dqt = dq_ref[...] + ext[:, 0:d]
        dkt = dk_ref[...] + ext[:, d:2 * d]
        dvt = dv_ref[...] + ext[:, 2 * d:3 * d]
        gwif_ref[:, 0:d] += _mm_tn(dgb, q_ref[...])
        gwif_ref[:, d:2 * d] += _mm_tn(dgb, k_ref[...])
        gwif_ref[:, 2 * d:3 * d] += _mm_tn(dgb, v_ref[...])
        dxc_parts, dxv_parts = [], []
        for h in range(nh):
            sl = slice(h * dh, (h + 1) * dh)
            gwq_ref[h] += _mm_tn(xc[:, sl], dqt[:, sl])
            gwk_ref[h] += _mm_tn(xc[:, sl], dkt[:, sl])
            gwv_ref[h] += _mm_tn(x[:, sl], dvt[:, sl])
            dxc_parts.append(_mm_nt(dqt[:, sl], wq_ref[h]) + _mm_nt(dkt[:, sl], wk_ref[h]))
            dxv_parts.append(_mm_nt(dvt[:, sl], wv_ref[h]))
        dxc = jnp.concatenate(dxc_parts, axis=1)
        dxv = jnp.concatenate(dxv_parts, axis=1)
        dpre = dxc * (sg + pre * sg * (1.0 - sg))
        gcb_ref[...] += _bcast8(_colsum(dpre))
        dx_ref[...] = (dxv + _conv_bwd(dpre, x, next8[...], cw_ref, gcw_ref)).astype(BF16)
        next8[...] = dpre[0:SUBLANES, :]

    rev = lambda i: nt - 1 - i
    tok = pl.BlockSpec((tile, d), lambda i: (rev(i), 0))
    g128 = pl.BlockSpec((tile, LANES), lambda i: (rev(i), 0))
    wsh = (nh, dh, dh)
    return _call(
        body, comms, name="mlstm_proj_bwd", grid=(nt,),
        in_specs=[tok, tok, tok, g128, g128, pl.BlockSpec((tile, d), lambda i: (rev(i), 2)), tok,
                  tok, tok, tok, _full(conv_w.shape), _full(wsh), _full(wsh), _full(wsh), _full(wif_t.shape)],
        out_specs=[tok, _full(wsh), _full(wsh), _full(wsh), _full((LANES, 3 * d)), _full((SUBLANES, LANES)),
                   _full((SUBLANES, d)), _full((SUBLANES, d))],
        out_shape=[jax.ShapeDtypeStruct((s, d), BF16)] + [jax.ShapeDtypeStruct(wsh, F32)] * 3
        + [jax.ShapeDtypeStruct((LANES, 3 * d), F32), jax.ShapeDtypeStruct((SUBLANES, LANES), F32),
           jax.ShapeDtypeStruct((SUBLANES, d), F32), jax.ShapeDtypeStruct((SUBLANES, d), F32)],
        scratch_shapes=[pltpu.VMEM((SUBLANES, d), F32)],
        args=(dq, dk, dv, dgates, gcol, u, pre, q, k, v, conv_w, w_q, w_k, w_v, wif_t))


def _rg_bwd(dy_rg, u, h_rg, gates, conv_w, w_a, w_x, lam, tile, comms=None):
    s, d = dy_rg.shape
    nt = s // tile
    nh, dh, _ = w_a.shape

    def body(dy_ref, x_ref, z_ref, h_ref, hhalo_ref, xc_ref, r_ref, i_ref, a_ref, beta_ref, cw_ref, wa_ref,
             wx_ref, lam_ref,
             dx_ref, dz_ref, gwa_ref, gwx_ref, gba_ref, gbx_ref, glam_ref, gcw_ref, gcb_ref, next8, anext, dnext, dbuf):
        i = pl.program_id(0)

        @pl.when(i == 0)
        def _():
            for ref in (next8, anext, dnext, gwa_ref, gwx_ref, gba_ref, gbx_ref, glam_ref, gcw_ref, gcb_ref):
                ref[...] = jnp.zeros_like(ref)

        inner = jnp.where(i < nt - 1, 1.0, 0.0)
        xc, r, ig, a, beta = xc_ref[...], r_ref[...], i_ref[...], a_ref[...], beta_ref[...]
        sp = _softplus(-lam_ref[...])
        h = h_ref[...]
        row = _iota(h.shape, 0)
        hprev = jnp.where(row >= 1, pltpu.roll(h, 1, 0), hhalo_ref[SUBLANES - 1:SUBLANES, :] * inner)
        z = z_ref[...]
        sz = _sigmoid(z)
        dyv = dy_ref[...]
        dz_ref[...] = (dyv * h * (sz + z * sz * (1.0 - sz))).astype(BF16)
        a_up = jnp.where(row < tile - 1, pltpu.roll(a, tile - 1, 0), anext[0:1, :])
        _scan_into(a_up, dyv * z * sz, dnext[0:1, :], dbuf, True)
        delta = dbuf[...]
        anext[...] = a[0:SUBLANES, :]
        dnext[...] = delta[0:SUBLANES, :]
        dla = delta * hprev * a - delta * ig * xc * (a * a / beta)
        glam_ref[...] += _bcast8(_colsum(dla * r) * (RG_C * _sigmoid(-lam_ref[...])))
        dpa = dla * (-RG_C * sp) * r * (1.0 - r)
        dpx = delta * beta * xc * ig * (1.0 - ig)
        gba_ref[...] += _bcast8(_colsum(dpa))
        gbx_ref[...] += _bcast8(_colsum(dpx))
        parts = []
        for hh in range(nh):
            sl = slice(hh * dh, (hh + 1) * dh)
            gwa_ref[hh] += _mm_tn(xc[:, sl], dpa[:, sl])
            gwx_ref[hh] += _mm_tn(xc[:, sl], dpx[:, sl])
            parts.append(_mm_nt(dpa[:, sl], wa_ref[hh]) + _mm_nt(dpx[:, sl], wx_ref[hh]))
        dxc = delta * beta * ig + jnp.concatenate(parts, axis=1)
        gcb_ref[...] += _bcast8(_colsum(dxc))
        dx_ref[...] = _conv_bwd(dxc, x_ref[...], next8[...], cw_ref, gcw_ref).astype(BF16)
        next8[...] = dxc[0:SUBLANES, :]

    rev = lambda i: nt - 1 - i
    tok = pl.BlockSpec((tile, d), lambda i: (rev(i), 0))
    vec = _full((1, d))
    acc = _full((SUBLANES, d))
    wsh = (nh, dh, dh)
    return _call(
        body, comms, name="rglru_bwd", grid=(nt,),
        in_specs=[tok, tok, pl.BlockSpec((tile, d), lambda i: (rev(i), 1)), tok,
                  _halo_spec(d, tile, nt, 0)] + [tok] * 5 + [_full(conv_w.shape), _full(wsh), _full(wsh), vec],
        out_specs=[tok, tok, _full(wsh), _full(wsh), acc, acc, acc, acc, acc],
        out_shape=[jax.ShapeDtypeStruct((s, d), BF16)] * 2 + [jax.ShapeDtypeStruct(wsh, F32)] * 2
        + [jax.ShapeDtypeStruct((SUBLANES, d), F32)] * 5,
        scratch_shapes=[pltpu.VMEM((SUBLANES, d), F32)] * 3 + [pltpu.VMEM((tile, d), F32)],
        args=(dy_rg, u, u, h_rg, h_rg, *gates, conv_w, w_a, w_x, lam))


def _segments(d, w, n_pieces, n_slots):
    bounds = sorted({k * d for k in range(n_pieces + 1)} | {j * w for j in range(n_slots + 1)})
    return [(lo // d, lo % d, lo // w, lo % w, hi - lo) for lo, hi in zip(bounds[:-1], bounds[1:])]


def _in_bwd(pieces, x, dxo, ng, scale, w_in_g, layer, tile, comms=None, tiles=None, prev=None):
    s, d = x.shape
    nd, _, _, w = w_in_g.shape
    first, count = tiles or (0, s // tile)
    n_p = len(pieces)

    def body(*refs):
        p_refs = refs[:n_p]
        x_ref, dxo_ref, ng_ref, sc_ref, w_ref = refs[n_p:n_p + 5]
        dx_ref, dsc_ref, dsh_ref, gng_ref, wcat = refs[-5:]
        _join_columns(w_ref, wcat)

        @pl.when(pl.program_id(0) == 0)
        def _():
            for k, ref in enumerate((dsc_ref, dsh_ref, gng_ref)):
                ref[...] = jnp.zeros_like(ref) if prev is None else refs[n_p + 6 + k][...]

        du = jnp.concatenate([p[...] for p in p_refs], axis=1)
        dh = lax.dot_general(du, wcat[...], (((1,), (1,)), ((), ())), preferred_element_type=F32)
        xv = x_ref[...]
        g = ng_ref[...]
        rs = lax.rsqrt(jnp.mean(xv * xv, axis=1, keepdims=True) + EPS)
        xh = xv * rs
        dsh_ref[...] += _bcast8(_colsum(dh))
        dsc_ref[...] += _bcast8(_colsum(dh * xh * g))
        dhn = dh * (1.0 + sc_ref[...])
        gng_ref[...] += _bcast8(_colsum(dhn * xh))
        dxh = dhn * g
        dx_ref[...] = dxo_ref[...] + rs * (dxh - xh * jnp.mean(dxh * xh, axis=1, keepdims=True))

    tok = pl.BlockSpec((tile, d), lambda i: (i + first, 0))
    vec = _full((1, d))
    acc = _full((SUBLANES, d))
    more_specs = [] if prev is None else [pl.BlockSpec(memory_space=pl.ANY), acc, acc, acc]
    return _call(
        body, comms, name="in_proj_bwd_x", grid=(count,),
        in_specs=[tok] * n_p + [tok, tok, vec, vec, pl.BlockSpec((nd, 1, d, w), lambda i: (0, layer, 0, 0),
                                                               pipeline_mode=pl.Buffered(1))] + more_specs,
        out_specs=[tok, acc, acc, acc],
        out_shape=[jax.ShapeDtypeStruct((s, d), F32)] + [jax.ShapeDtypeStruct((SUBLANES, d), F32)] * 3,
        scratch_shapes=[pltpu.VMEM((d, nd * w), BF16)],
        args=(*pieces, x, dxo, ng, scale, w_in_g) + (() if prev is None else tuple(prev)),
        aliases={} if prev is None else {n_p + 5: 0})


def _in_bwd_w(pieces, hbf, w, slots, tile, comms=None):
    s, d = hbf.shape
    nd_all = len(pieces) * d // w
    segs = [sg for sg in _segments(d, w, len(pieces), nd_all) if sg[2] in slots]

    def body(*refs):
        p_refs = refs[:len(pieces)]
        h_ref, gw_ref = refs[len(pieces):]

        @pl.when(pl.program_id(0) == 0)
        def _():
            gw_ref[...] = jnp.zeros_like(gw_ref)

        hv = h_ref[...]
        for (kk, a, j, b, width) in segs:
            gw_ref[j - slots[0], :, b:b + width] += _mm_tn(hv, p_refs[kk][:, a:a + width])

    tok = pl.BlockSpec((tile, d), lambda i: (i, 0))
    return _call(
        body, comms, name="in_proj_bwd_w", grid=(s // tile,),
        in_specs=[tok] * len(pieces) + [tok],
        out_specs=[pl.BlockSpec((len(slots), d, w), lambda i: (0, 0, 0), pipeline_mode=pl.Buffered(1))],
        out_shape=[jax.ShapeDtypeStruct((len(slots), d, w), F32)],
        args=(*pieces, hbf))[0]


def _exchange(arrs, gather, name):
    return _run_comms([_exchange_comm(arrs, gather)], name)[0]


def _run_comms(comms, name):
    _call(lambda: None, comms, name=name, grid=(1,), in_specs=[], out_specs=[], out_shape=[], args=())
    return [cm.results for cm in comms]


def _exchange_comm(arrs, gather):
    n = len(arrs)
    per = N_DEV - 1

    def copies(ins, outs, sems):
        send_sems, recv_sems, local_sems = sems
        x, y, c = (lax.axis_index(ax) for ax in MESH_AXES)
        me = 4 * x + 2 * y + c
        sends, recvs = [], []
        for flip in range(1, N_DEV):
            px = x ^ ((flip >> 2) & 1)
            py = y ^ ((flip >> 1) & 1)
            pc = c ^ (flip & 1)
            peer = 4 * px + 2 * py + pc
            for kk in range(n):
                src = ins[kk] if gather else ins[kk].at[peer]
                sends.append(_remote(src, outs[kk].at[me], send_sems, recv_sems, kk * per + flip - 1, (px, py, pc)))
                recvs.append(_remote(src, outs[kk].at[peer], send_sems, recv_sems, kk * per + flip - 1, (px, py, pc)))
        local = [pltpu.make_async_copy(ins[kk] if gather else ins[kk].at[me], outs[kk].at[me], local_sems.at[kk])
                 for kk in range(n)]
        return local, sends, recvs

    def start(ins, outs, sems):
        local, sends, _ = copies(ins, outs, sems)
        for cp in sends + local:
            cp.start()

    def finish(ins, outs, sems):
        local, sends, recvs = copies(ins, outs, sems)
        for cp in recvs:
            cp.wait_recv()
        for cp in sends:
            cp.wait_send()
        for cp in local:
            cp.wait()

    return _Comm(arrs, [jax.ShapeDtypeStruct((N_DEV,) + a.shape if gather else a.shape, a.dtype) for a in arrs],
                 [pltpu.SemaphoreType.DMA((n * per,)), pltpu.SemaphoreType.DMA((n * per,)), pltpu.SemaphoreType.DMA((n,))],
                 start, finish)


def _mesh_place():
    x, y, c = (lax.axis_index(ax) for ax in MESH_AXES)
    return x, y, c, (x, y, 1 - c), [(1 - x, y), (x, 1 - y), (1 - x, 1 - y)]


def _remote(src, dst, send_sems, recv_sems, sem, to):
    return pltpu.make_async_remote_copy(src_ref=src, dst_ref=dst, send_sem=send_sems.at[sem], recv_sem=recv_sems.at[sem],
                                        device_id=to, device_id_type=pl.DeviceIdType.MESH)


N_CHIPS = N_DEV // 2


def _pair_sum(a, other, parity, name):
    _, r, c = a.shape
    tr = _row_tile(r, c, 3)

    def body(p_ref, a_ref, o_ref, s_ref):
        s_ref[...] = (a_ref[...] + o_ref[...]).astype(BF16)

    return pl.pallas_call(
        body, name=name,
        grid_spec=pltpu.PrefetchScalarGridSpec(
            num_scalar_prefetch=1, grid=(N_CHIPS, r // tr),
            in_specs=[pl.BlockSpec((1, tr, c), lambda q, i, p: (2 * q + p[0], i, 0)),
                      pl.BlockSpec((1, tr, c), lambda q, i, p: (q, i, 0))],
            out_specs=pl.BlockSpec((1, tr, c), lambda q, i, p: (q, i, 0))),
        out_shape=jax.ShapeDtypeStruct((N_CHIPS, r, c), BF16),
        compiler_params=_params(2),
    )(parity, a, other)


def _adam_math(w, g, m, v):
    m = ADAM_B1 * m + (1.0 - ADAM_B1) * g
    v = ADAM_B2 * v + (1.0 - ADAM_B2) * (g * g)
    m_hat = m / (1.0 - ADAM_B1 ** ADAM_STEP)
    v_hat = v / (1.0 - ADAM_B2 ** ADAM_STEP)
    delta = -ADAM_LR * (m_hat / (jnp.sqrt(v_hat) + ADAM_EPS) + ADAM_WD * w)
    return delta, m, v


def _sum_devices(r_ref):
    acc = r_ref[0].astype(F32)
    for p in range(1, r_ref.shape[0]):
        acc = acc + r_ref[p].astype(F32)
    return acc


def _row_tile(rows, cols, n_bufs):
    budget = 24 * 1024 * 1024 // (n_bufs * 2 * cols * 4)
    t = rows
    while t > budget and t % 2 == 0 and (t // 2) % SUBLANES == 0:
        t //= 2
    return t


def _reduce_adam(recvs, w, m, v, name, comms=None):
    nl, r, c = w.shape
    n_part = recvs[0].shape[0]
    tr = _row_tile(r, c, n_part * nl + 7)
    nt = r // tr

    def body(*refs):
        r_refs = refs[:nl]
        w_ref, m_ref, v_ref, g_ref, d_ref, mo_ref, vo_ref = refs[nl:]
        layer = pl.program_id(0) // nt
        g = _sum_devices(r_refs[0])
        for ll in range(1, nl):
            g = jnp.where(layer == ll, _sum_devices(r_refs[ll]), g)
        delta, m2, v2 = _adam_math(w_ref[0], g, m_ref[0], v_ref[0])
        g_ref[0] = g
        d_ref[0] = delta
        mo_ref[0] = m2
        vo_ref[0] = v2

    def rspec(ll):
        return pl.BlockSpec((n_part, tr, c),
                            lambda i: (0, jnp.where(i // nt == ll, i % nt, jnp.where(i // nt < ll, 0, nt - 1)), 0))

    blk = pl.BlockSpec((1, tr, c), lambda i: (i // nt, i % nt, 0))
    return _call(
        body, comms, name=name, grid=(nl * nt,),
        in_specs=[rspec(ll) for ll in range(nl)] + [blk, blk, blk],
        out_specs=[blk] * 4,
        out_shape=[jax.ShapeDtypeStruct((nl, r, c), F32)] * 4,
        args=(*recvs, w, m, v))


def _tile_for(s, want):
    return min(want, s)


REPLICATED = ("norm_g", "b_ada", "rg_conv_b", "rg_w_a", "rg_b_a", "rg_w_x", "rg_b_x", "rg_lambda", "ml_conv_b",
              "ml_b_if", "ml_norm_g", "final_g")


def _small_pack(rg_conv_w, ml_conv_w, ml_w_if):
    nl = rg_conv_w.shape[0]
    wif_t = jnp.swapaxes(ml_w_if, 1, 2).reshape(nl, -1, LANES)
    return jnp.concatenate([rg_conv_w, ml_conv_w, wif_t], axis=1)


def _small_unpack(p, if_rows):
    nl = p.shape[0]
    rg_cw = p[:, 0:CONV_WIDTH]
    ml_cw = p[:, CONV_WIDTH:2 * CONV_WIDTH]
    wif = jnp.swapaxes(p[:, 2 * CONV_WIDTH:].reshape(nl, 8, if_rows), 1, 2)
    return rg_cw, ml_cw, wif


def _qkv_slots(g_qkv, nd):
    three, nh, dh, _ = g_qkv.shape
    return g_qkv.reshape(three, nh, nd, dh // nd, dh).transpose(2, 0, 1, 3, 4).reshape(nd, three * nh * (dh // nd), dh)


def _small_slots(g):
    nd = N_DEV
    cw = jnp.stack([g["rg_conv_w"], g["ml_conv_w"]]).reshape(2, CONV_WIDTH, nd, LANES).transpose(2, 0, 1, 3)
    cw = cw.reshape(nd, 2 * CONV_WIDTH, LANES)
    wif = g["wif_t"].reshape(8, nd, -1).transpose(1, 0, 2).reshape(nd, -1, LANES)
    return jnp.concatenate([cw, wif], axis=1)


def _slot(block):
    return 4 * block[0] + 2 * block[1] + block[2]


def _dma_sems(*counts):
    return [pltpu.SemaphoreType.DMA((n,)) for n in counts]


def _start_all(copies):
    for cp in copies:
        cp.start()


def _gather_ici_comm(arrs):
    n = len(arrs)

    def copies(ins, outs, sems):
        send_sems, recv_sems, local_sems = sems
        x, y, c, sibling, chips = _mesh_place()
        me = (x, y, c)
        peers = [(*chip, c) for chip in chips] + [sibling]
        local = [pltpu.make_async_copy(ins[kk], outs[kk].at[_slot(me)], local_sems.at[kk]) for kk in range(n)]
        sends = [_remote(ins[kk], outs[kk].at[_slot(me)], send_sems, recv_sems, kk * 4 + j, peer)
                 for j, peer in enumerate(peers) for kk in range(n)]
        recvs = [_remote(ins[kk], outs[kk].at[_slot(peer)], send_sems, recv_sems, kk * 4 + j, peer)
                 for j, peer in enumerate(peers) for kk in range(n)]
        return local, sends, recvs

    def start(ins, outs, sems):
        local, sends, _ = copies(ins, outs, sems)
        _start_all(sends + local)

    def finish(ins, outs, sems):
        local, sends, recvs = copies(ins, outs, sems)
        for cp in recvs:
            cp.wait_recv()
        for cp in sends:
            cp.wait_send()
        for cp in local:
            cp.wait()

    return _Comm(arrs, [jax.ShapeDtypeStruct((N_DEV,) + a.shape, a.dtype) for a in arrs], _dma_sems(4 * n, 4 * n, n),
                 start, finish)


def _gather_fwd_comm(bufs):
    n = len(bufs)

    def copies(ins, outs, sems):
        send_sems, recv_sems = sems
        _, _, c, sibling, chips = _mesh_place()
        sends = [_remote(ins[kk].at[_slot((*chip, c))], outs[kk].at[_slot((*chip, c))], send_sems, recv_sems, kk * 3 + j, sibling)
                 for j, chip in enumerate(chips) for kk in range(n)]
        recvs = [_remote(ins[kk].at[_slot((*chip, c))], outs[kk].at[_slot((*chip, 1 - c))], send_sems, recv_sems, kk * 3 + j, sibling)
                 for j, chip in enumerate(chips) for kk in range(n)]
        return sends, recvs

    def start(ins, outs, sems):
        _start_all(copies(ins, outs, sems)[0])

    def finish(ins, outs, sems):
        sends, recvs = copies(ins, outs, sems)
        for cp in recvs:
            cp.wait_recv()
        for cp in sends:
            cp.wait_send()

    return _Comm(bufs, [jax.ShapeDtypeStruct(a.shape, a.dtype) for a in bufs], _dma_sems(3 * n, 3 * n), start, finish,
                 aliases=[(i, i) for i in range(n)])


def _core_swap_comm(arrs):
    n = len(arrs)

    def copies(ins, outs, sems):
        send_sems, recv_sems = sems
        _, _, c, sibling, _ = _mesh_place()
        return [_remote(ins[kk].at[2 * q + (1 - c)], outs[kk].at[q], send_sems, recv_sems, kk * N_CHIPS + q, sibling)
                for q in range(N_CHIPS) for kk in range(n)]

    def start(ins, outs, sems):
        _start_all(copies(ins, outs, sems))

    def finish(ins, outs, sems):
        cps = copies(ins, outs, sems)
        for cp in cps:
            cp.wait_recv()
        for cp in cps:
            cp.wait_send()

    return _Comm(arrs, [jax.ShapeDtypeStruct((N_CHIPS,) + a.shape[1:], a.dtype) for a in arrs],
                 _dma_sems(N_CHIPS * n, N_CHIPS * n), start, finish)


def _chip_swap_comm(arrs):
    n = len(arrs)
    per = N_CHIPS - 1

    def copies(ins, outs, sems):
        send_sems, recv_sems, local_sems = sems
        x, y, c, _, chips = _mesh_place()
        mine = 2 * x + y
        sends = [_remote(ins[kk].at[2 * chip[0] + chip[1]], outs[kk].at[mine], send_sems, recv_sems, kk * per + j, (*chip, c))
                 for j, chip in enumerate(chips) for kk in range(n)]
        recvs = [_remote(ins[kk].at[mine], outs[kk].at[2 * chip[0] + chip[1]], send_sems, recv_sems, kk * per + j, (*chip, c))
                 for j, chip in enumerate(chips) for kk in range(n)]
        local = [pltpu.make_async_copy(ins[kk].at[mine], outs[kk].at[mine], local_sems.at[kk]) for kk in range(n)]
        return local, sends, recvs

    def start(ins, outs, sems):
        local, sends, _ = copies(ins, outs, sems)
        _start_all(sends + local)

    def finish(ins, outs, sems):
        local, sends, recvs = copies(ins, outs, sems)
        for cp in recvs:
            cp.wait_recv()
        for cp in sends:
            cp.wait_send()
        for cp in local:
            cp.wait()

    return _Comm(arrs, [jax.ShapeDtypeStruct(a.shape, a.dtype) for a in arrs], _dma_sems(per * n, per * n, n), start, finish)


def _ada_mod(c_all, w_ada, b_cols, comms=None):
    nl, d, w = w_ada.shape

    def body(c_ref, w_ref, b_ref, m_ref, ca_ref):
        sub = _iota((SUBLANES, d), 0)
        cv = jnp.zeros((SUBLANES, d), F32)
        for b in range(N_DEV):
            cv = jnp.where(sub == b, c_ref[b], cv)
        ca = cv * _sigmoid(cv)
        ca_ref[...] = ca
        m_ref[...] = jnp.zeros_like(m_ref)
        for l in range(nl):
            ml = _mm_hi(ca, w_ref[l]) + b_ref[l:l + 1, :]
            for b in range(N_DEV):
                m_ref[b, l:l + 1, :] = _row(ml, b)

    return _call(
        body, comms, name="adaln_mod_columns", grid=(1,),
        in_specs=[_full(c_all.shape), _full(w_ada.shape), _full(b_cols.shape)],
        out_specs=[_full((N_DEV, SUBLANES, w)), _full((SUBLANES, d))],
        out_shape=[jax.ShapeDtypeStruct((N_DEV, SUBLANES, w), F32), jax.ShapeDtypeStruct((SUBLANES, d), F32)],
        args=(c_all, w_ada, b_cols))


def _ada_grad_adam(cact_t, dmods, w, m, v, comms=None):
    nl, d, wd = w.shape
    tr = _row_tile(d, wd, 8)
    nt = d // tr

    def body(c_ref, dm_ref, w_ref, m_ref, v_ref, g_ref, d_ref, mo_ref, vo_ref):
        cv = c_ref[...]
        dm = dm_ref[0]
        g = _col(cv, 0) * _row(dm, 0)
        for b in range(1, N_DEV):
            g = g + _col(cv, b) * _row(dm, b)
        delta, m2, v2 = _adam_math(w_ref[0], g, m_ref[0], v_ref[0])
        g_ref[0] = g
        d_ref[0] = delta
        mo_ref[0] = m2
        vo_ref[0] = v2

    blk = pl.BlockSpec((1, tr, wd), lambda i: (i // nt, i % nt, 0))
    return _call(
        body, comms, name="adaln_grad_adam", grid=(nl * nt,),
        in_specs=[pl.BlockSpec((tr, N_DEV), lambda i: (i % nt, 0)), pl.BlockSpec((1, N_DEV, wd), lambda i: (i // nt, 0, 0)),
                  blk, blk, blk],
        out_specs=[blk] * 4, out_shape=[jax.ShapeDtypeStruct((nl, d, wd), F32)] * 4,
        args=(cact_t, dmods, w, m, v))


REP_ROWS = ("norm_g", "dshift", "dscale", "dgate", "rg_conv_b", "rg_b_a", "rg_b_x", "rg_lambda", "ml_conv_b", "ml_norm_g",
            "ml_b_if")


def _sum_parts(recvs, name):
    def body(*refs):
        for r_ref, o_ref in zip(refs[:len(recvs)], refs[len(recvs):]):
            o_ref[...] = _sum_devices(r_ref).astype(o_ref.dtype)

    return pl.pallas_call(
        body, name=name, grid=(1,),
        in_specs=[_full(r.shape) for r in recvs], out_specs=[_full(r.shape[1:]) for r in recvs],
        out_shape=[jax.ShapeDtypeStruct(r.shape[1:], r.dtype) for r in recvs], compiler_params=_params(1),
    )(*recvs)


def _adam_replicated(vp, mp, params, nl):
    d = vp.shape[2]
    nr = len(REP_ROWS)
    names = list(params)
    mat_shape = params["rg_w_a"][0].shape[1:]
    mat_rows = mp.shape[0] // (2 * nl)

    def pieces(name):
        if name == "final_g":
            return [(lambda vp_ref, mp_ref: vp_ref[nl * nr:nl * nr + 1, :], (slice(0, 1), slice(None)))]
        out = []
        for l in range(nl):
            if name in ("rg_w_a", "rg_w_x"):
                at = (2 * l + (name == "rg_w_x")) * mat_rows
                out.append((lambda vp_ref, mp_ref, at=at: mp_ref[at:at + mat_rows, :].astype(F32).reshape(mat_shape), l))
            elif name == "b_ada":
                for j in range(3):
                    r = l * nr + 1 + j
                    out.append((lambda vp_ref, mp_ref, r=r: vp_ref[r:r + 1, :], (slice(l, l + 1), slice(j * d, (j + 1) * d))))
            else:
                r = l * nr + REP_ROWS.index(name)
                cols = slice(0, LANES) if name == "ml_b_if" else slice(None)
                out.append((lambda vp_ref, mp_ref, r=r, cols=cols: vp_ref[r:r + 1, cols], (slice(l, l + 1), slice(None))))
        return out

    def body(*refs):
        parts_ref, mp_ref, vp_ref = refs[0], refs[1], refs[-1]
        ins, outs = refs[2:2 + 3 * len(names)], refs[2 + 3 * len(names):-1]
        vp_ref[...] = _sum_devices(parts_ref)
        for pi, name in enumerate(names):
            w_ref, m_ref, v_ref = ins[3 * pi:3 * pi + 3]
            g_ref, d_ref, mo_ref, vo_ref = outs[4 * pi:4 * pi + 4]
            for get, idx in pieces(name):
                g = get(vp_ref, mp_ref)
                delta, m2, v2 = _adam_math(w_ref[idx], g, m_ref[idx], v_ref[idx])
                g_ref[idx] = g
                d_ref[idx] = delta
                mo_ref[idx] = m2
                vo_ref[idx] = v2

    flat = [a for name in names for a in params[name]]
    out_shape = [jax.ShapeDtypeStruct(params[name][0].shape, F32) for name in names for _ in range(4)]
    out_shape.append(jax.ShapeDtypeStruct(vp.shape[1:], F32))
    res = pl.pallas_call(
        body, name="adam_replicated", grid=(1,),
        in_specs=[_full(vp.shape), _full(mp.shape)] + [_full(a.shape) for a in flat],
        out_specs=[_full(o.shape) for o in out_shape], out_shape=out_shape, compiler_params=_params(1),
    )(vp, mp, *flat)
    return {name: res[4 * pi:4 * pi + 4] for pi, name in enumerate(names)}, res[-1]


class _Plan:
    def __init__(self):
        self.hosted, self.after = {}, {}

    def host(self, key, comm, then=None):
        self.hosted.setdefault(key, []).append(comm)
        if then is not None:
            self.after.setdefault(key, []).append(then)

    def comms(self, key):
        return self.hosted.pop(key, None)

    def done(self, key):
        for fn in self.after.pop(key, []):
            fn()

    def flush(self):
        while self.hosted:
            key = next(iter(self.hosted))
            _call(lambda: None, self.comms(key), name="exchange_after_%s_%d" % key, grid=(1,), in_specs=[], out_specs=[],
                  out_shape=[], args=())
            self.done(key)


VEC_TABLE = ("norm_g", "rg_conv_b", "rg_b_a", "rg_b_x", "rg_lambda", "ml_conv_b", "ml_norm_g")


def _vec_table(rep):
    rows = [rep[n] for n in VEC_TABLE]
    return jnp.stack(rows + [jnp.zeros_like(rows[0])] * (SUBLANES - len(rows)), axis=1)


def _layer_fwd(l, xl, mod3, wl, rep, plan, head=None):
    s, d = xl.shape
    t_big, t_mid = _tile_for(s, 512), _tile_for(s, 256)
    nh_ml = rep["ml_b_if"].shape[1] // 2
    vec = lambda name: _vec(rep["vecs"], l, VEC_TABLE.index(name))
    shift, scale, gate = (_vec(mod3, l, kk) for kk in range(3))
    hosted = lambda name: plan.comms((name, l)) if plan else None
    done = lambda name: plan.done((name, l)) if plan else None
    u, hbf = _in_fwd(xl, vec("norm_g"), scale, shift, wl["w_in_g"], 0, t_big, hosted("in_proj_fwd"))
    done("in_proj_fwd")
    h_rg, y_rg, *rg_gates = _rg_fwd(u, d, wl["rg_conv_w"], vec("rg_conv_b"), rep["rg_w_a_bf"][l], vec("rg_b_a"),
                                    rep["rg_w_x_bf"][l], vec("rg_b_x"), vec("rg_lambda"), t_mid, hosted("rglru_fwd"))
    done("rglru_fwd")
    q, k, v, gcol, pre = _ml_pre(u, d, wl["ml_conv_w"], vec("ml_conv_b"), wl["w_qkv"][0], wl["w_qkv"][1],
                                 wl["w_qkv"][2], wl["wif_pad"], wl["bif_pad"], t_mid, hosted("mlstm_proj_fwd"))
    done("mlstm_proj_fwd")
    grow = gcol[:, 0:16].T
    cell, y_ml, cs, ns, ms, mt = _ml_cell_fwd(q, k, v, gcol, grow, u, vec("ml_norm_g"), nh_ml, hosted("mlstm_cell_fwd"))
    done("mlstm_cell_fwd")
    res = _out_fwd(xl, y_rg, y_ml, gate, wl["w_out_g"], 0, t_big, hosted("out_proj_fwd"), head)
    done("out_proj_fwd")
    x_new, y = (res[0], res[1]) if head is None else (tuple(res[1:]), res[0])
    saved = dict(x=xl, u=u, hbf=hbf, h_rg=h_rg, y_rg=y_rg, q=q, k=k, v=v, gcol=gcol, grow=grow, cell=cell, y_ml=y_ml,
                 cs=cs, ns=ns, ms=ms, mt=mt, y=y, scale=scale, gate=gate, rg_gates=rg_gates, pre=pre)
    return x_new, saved


def _layer_bwd(l, dx, sv, wl, rep, plan, grads=None, split_last=False):
    s, d = dx.shape
    t_big, t_mid = _tile_for(s, 512), _tile_for(s, 256)
    nh_ml = rep["ml_b_if"].shape[1] // 2
    nd, _, _, w_cols = wl["w_in_g"].shape
    grads = {} if grads is None else grads
    vec = lambda name: _vec(rep["vecs"], l, VEC_TABLE.index(name))
    hosted = lambda name: plan.comms((name, l)) if plan else None
    done = lambda name: plan.done((name, l)) if plan else None
    dy_rg, dy_ml, gw_out, dgate = _out_bwd(dx, sv["gate"], sv["y"], sv["y_rg"], sv["y_ml"], wl["w_out_g"], 0, t_big,
                                           hosted("out_proj_bwd"))
    grads.update(w_out=gw_out)
    done("out_proj_bwd")
    dq, dk, dv, dgates, d_mlo, d_mlz, g_mlng = _ml_cell_bwd(
        dy_ml, sv["u"], sv["cell"], sv["q"], sv["k"], sv["v"], sv["gcol"], sv["grow"], sv["mt"], sv["cs"], sv["ns"],
        sv["ms"], vec("ml_norm_g"), nh_ml, hosted("mlstm_cell_bwd"))
    done("mlstm_cell_bwd")
    d_mlx, g_wq, g_wk, g_wv, g_wift, g_bif, g_mlcw, g_mlcb = _ml_pre_bwd(
        dq, dk, dv, dgates, sv["gcol"], sv["u"], sv["pre"], sv["q"], sv["k"], sv["v"], wl["ml_conv_w"],
        wl["w_qkv"][0], wl["w_qkv"][1], wl["w_qkv"][2], wl["wift_pad"], t_mid, hosted("mlstm_proj_bwd"))
    done("mlstm_proj_bwd")
    d_rgx, d_rgz, g_wa, g_wx, g_ba, g_bx, g_lam, g_rgcw, g_rgcb = _rg_bwd(
        dy_rg, sv["u"], sv["h_rg"], sv["rg_gates"], wl["rg_conv_w"], rep["rg_w_a_bf"][l], rep["rg_w_x_bf"][l],
        vec("rg_lambda"), t_mid, hosted("rglru_bwd"))
    grads.update(w_qkv=jnp.stack([g_wq, g_wk, g_wv]), rg_conv_w=g_rgcw[0:CONV_WIDTH], ml_conv_w=g_mlcw[0:CONV_WIDTH],
                 wif_t=g_wift[0:8], rg_w_a=g_wa, rg_w_x=g_wx)
    acc = dict(dgate=dgate, rg_conv_b=g_rgcb, rg_b_a=g_ba, rg_b_x=g_bx, rg_lambda=g_lam, ml_conv_b=g_mlcb,
               ml_b_if=g_bif, ml_norm_g=g_mlng)
    done("rglru_bwd")
    pieces = [d_rgx, d_rgz, d_mlx, d_mlo, d_mlz]
    grads.update(w_in=_in_bwd_w(pieces, sv["hbf"], w_cols, tuple(range(nd)), _tile_for(s, 1024), hosted("in_proj_bwd_w")))
    done("in_proj_bwd_w")
    n_tiles = s // t_mid
    counts = [n_tiles // 8, n_tiles - n_tiles // 8] if split_last and n_tiles >= 8 else [n_tiles]
    in_args = (pieces, sv["x"], dx, vec("norm_g"), sv["scale"], wl["w_in_g"], 0, t_mid)
    res, at = None, 0
    for key, count in zip(("in_proj_bwd_x", "in_proj_bwd_x_rest"), counts):
        res = _in_bwd(*in_args, hosted(key), (at, count), res)
        done(key)
        at += count
    dx, dscale, dshift, g_ng = res
    acc.update(norm_g=g_ng, dshift=dshift, dscale=dscale)
    grads.update(acc=acc, dmod=jnp.concatenate([dshift[0:1], dscale[0:1], dgate[0:1]], axis=1))
    return dx, grads


def _full_qkv(qkv_g, d):
    nd, _, rows3, dh = qkv_g.shape
    nh = d // dh
    rsh = rows3 // (3 * nh)
    return qkv_g.reshape(nd, 3, nh, rsh, dh).transpose(1, 2, 0, 3, 4).reshape(3, nh, nd * rsh, dh)


def _small_weights(small, l, ml_b_if):
    nd = small.shape[0]
    sm = small[:, l]
    cw = sm[:, 0:2 * CONV_WIDTH].reshape(nd, 2, CONV_WIDTH, LANES).transpose(1, 2, 0, 3).reshape(2, CONV_WIDTH, nd * LANES)
    if_rows = (sm.shape[1] - 2 * CONV_WIDTH) * LANES // 8
    wif_t = sm[:, 2 * CONV_WIDTH:].reshape(nd, 8, if_rows).transpose(1, 0, 2).reshape(8, nd * if_rows)
    wift_pad = jnp.pad(wif_t, ((0, LANES - 8), (0, 0))).astype(BF16)
    return dict(rg_conv_w=cw[0], ml_conv_w=cw[1], wift_pad=wift_pad, wif_pad=wift_pad.T,
                bif_pad=jnp.pad(ml_b_if[l], (0, LANES - 8)).reshape(1, LANES))


def kernel(x, c, norm_g, w_ada, b_ada, w_in, rg_conv_w, rg_conv_b, rg_w_a, rg_b_a, rg_w_x, rg_b_x, rg_lambda, ml_conv_w, ml_conv_b, ml_w_q, ml_w_k, ml_w_v, ml_w_if, ml_b_if, ml_norm_g, w_out, final_g, loss_target, m_norm_g, m_w_ada, m_b_ada, m_w_in, m_rg_conv_w, m_rg_conv_b, m_rg_w_a, m_rg_b_a, m_rg_w_x, m_rg_b_x, m_rg_lambda, m_ml_conv_w, m_ml_conv_b, m_ml_w_q, m_ml_w_k, m_ml_w_v, m_ml_w_if, m_ml_b_if, m_ml_norm_g, m_w_out, m_final_g, v_norm_g, v_w_ada, v_b_ada, v_w_in, v_rg_conv_w, v_rg_conv_b, v_rg_w_a, v_rg_b_a, v_rg_w_x, v_rg_b_x, v_rg_lambda, v_ml_conv_w, v_ml_conv_b, v_ml_w_q, v_ml_w_k, v_ml_w_v, v_ml_w_if, v_ml_b_if, v_ml_norm_g, v_w_out, v_final_g):
    given = dict(locals())
    nl = w_in.shape[0]
    d = x.shape[2]
    rep = {n: given[n] for n in REPLICATED}
    rep.update(rg_w_a_bf=rg_w_a.astype(BF16), rg_w_x_bf=rg_w_x.astype(BF16))
    bf = lambda a: a.astype(BF16)

    def qkv_shard(prefix):
        return jnp.stack([given[prefix + "ml_w_q"], given[prefix + "ml_w_k"], given[prefix + "ml_w_v"]], axis=1).reshape(
            nl, -1, ml_w_q.shape[-1])

    def small_shard(prefix):
        return _small_pack(given[prefix + "rg_conv_w"], given[prefix + "ml_conv_w"], given[prefix + "ml_w_if"])

    plan = _Plan()
    qkv = qkv_shard("")
    first_ici = _gather_ici_comm([bf(w_in[0:1]), small_shard("")])
    condition = _exchange_comm([jnp.broadcast_to(c, (SUBLANES, d))], True)
    _run_comms([first_ici, condition], "gather_first")
    first_fwd = _gather_fwd_comm(first_ici.results)
    wcols = w_ada.shape[2]
    me = 4 * lax.axis_index("x") + 2 * lax.axis_index("y") + lax.axis_index("c")
    b_cols = jnp.pad(lax.dynamic_slice_in_dim(b_ada, me * wcols, wcols, axis=1), ((0, SUBLANES - nl), (0, 0)))
    mod_cols, cact_all = _ada_mod(condition.results[0], w_ada, b_cols, [first_fwd])
    w_in_first, small = first_fwd.results
    wl = [_small_weights(small, l, ml_b_if) for l in range(nl)]
    wl[0]["w_in_g"] = w_in_first

    def gather_behind(arrs, ici_host, fwd_host, then):
        ici = _gather_ici_comm(arrs)

        def pass_on():
            fwd = _gather_fwd_comm(ici.results)
            plan.host(fwd_host, fwd, lambda: then(fwd.results))

        plan.host(ici_host, ici, pass_on)

    def got_out(l):
        return lambda r: wl[l].update(w_out_g=r[0], w_qkv=_full_qkv(r[1], d))

    gather_behind([bf(w_out[0:1]), bf(qkv[0:1])], ("in_proj_fwd", 0), ("rglru_fwd", 0), got_out(0))
    for l in range(1, nl):
        gather_behind([bf(w_in[l:l + 1])], ("rglru_fwd", l - 1), ("mlstm_cell_fwd", l - 1),
                      lambda r, l=l: wl[l].update(w_in_g=r[0]))
        gather_behind([bf(w_out[l:l + 1]), bf(qkv[l:l + 1])], ("mlstm_cell_fwd", l - 1), ("out_proj_fwd", l - 1), got_out(l))

    mod_blocks = _exchange([mod_cols], False, "scatter_modulation")[0]
    mod3 = mod_blocks[:, 0:nl].transpose(1, 0, 2).reshape(nl, 3, d)
    rep["vecs"] = _vec_table(rep)
    saved, xl = [], x[0]
    for l in range(nl):
        head = (final_g.reshape(1, -1), loss_target[0]) if l == nl - 1 else None
        xl, sv = _layer_fwd(l, xl, mod3, wl[l], rep, plan, head)
        saved.append(sv)
    grad_x, loss_p, g_final = xl

    keys = ("w_in", "w_out", "w_qkv", "small")
    parity = lax.axis_index("c").astype(jnp.int32).reshape(1)
    grads, recv = [None] * nl, [None] * nl

    def small_parts(g):
        return [bf(_qkv_slots(g["w_qkv"], N_DEV)), bf(_small_slots(g))]

    def reduce_behind(l, host_layer):
        parts = [grads[l]["w_in"], grads[l]["w_out"]]
        swap = _core_swap_comm(parts)
        direct = _exchange_comm(small_parts(grads[l]), False)

        def summed():
            sums = [_pair_sum(a, o, parity, "pair_sum_%s_layer%d" % (key, l)) for key, a, o in zip(keys, parts, swap.results)]
            big = _chip_swap_comm([sums[0]])
            rest = _chip_swap_comm([sums[1]])
            plan.host(("mlstm_cell_bwd", host_layer), big)
            plan.host(("rglru_bwd", host_layer), rest,
                      lambda: recv.__setitem__(l, big.results + rest.results + direct.results))

        plan.host(("out_proj_bwd", host_layer), swap, summed)
        plan.host(("mlstm_proj_bwd", host_layer), direct)

    first, own = {}, {}

    def reduce_own(names, parts_fn, ready_key, swap_key, chip_key):
        def go():
            parts = parts_fn()
            swap = _core_swap_comm(parts)

            def summed():
                sums = [_pair_sum(a, o, parity, "pair_sum_%s_layer0" % n) for n, a, o in zip(names, parts, swap.results)]
                chip = _chip_swap_comm(sums)
                plan.host(chip_key, chip, lambda: own.update(zip(names, chip.results)))

            plan.host(swap_key, swap, summed)

        plan.after.setdefault(ready_key, []).append(go)

    reduce_own(["w_out"], lambda: [first["w_out"]], ("out_proj_bwd", 0), ("mlstm_cell_bwd", 0), ("mlstm_proj_bwd", 0))
    reduce_own(["w_in"], lambda: [first["w_in"]], ("in_proj_bwd_w", 0), ("in_proj_bwd_x", 0), ("in_proj_bwd_x_rest", 0))

    def small_own():
        direct = _exchange_comm(small_parts(first), False)
        plan.host(("in_proj_bwd_w", 0), direct, lambda: own.update(w_qkv=direct.results[0], small=direct.results[1]))

    plan.after.setdefault(("rglru_bwd", 0), []).append(small_own)

    matrices = {}

    def reduce_matrices():
        layers = [grads[l] if l > 0 else first for l in range(nl)]
        mp = jnp.stack([jnp.stack([g["rg_w_a"], g["rg_w_x"]]) for g in layers]).reshape(N_DEV, -1, LANES).astype(BF16)
        scatter = _exchange_comm([mp], False)

        def summed():
            gather = _exchange_comm(_sum_parts(scatter.results, "sum_replicated_matrices"), True)
            plan.host(("in_proj_bwd_x", 0), gather, lambda: matrices.update(mp=gather.results[0].reshape(-1, LANES)))

        plan.host(("in_proj_bwd_w", 0), scatter, summed)

    plan.after.setdefault(("rglru_bwd", 0), []).append(reduce_matrices)

    for l in reversed(range(nl)):
        if l > 0:
            grad_x, grads[l] = _layer_bwd(l, grad_x, saved[l], wl[l], rep, plan)
            reduce_behind(l, l - 1)
        else:
            grad_x, grads[l] = _layer_bwd(l, grad_x, saved[l], wl[l], rep, plan, first, True)
    plan.flush()
    recv[0] = [own[key] for key in keys]

    shard = {p: dict(w_in=given[p + "w_in"], w_out=given[p + "w_out"], w_qkv=qkv_shard(p), small=small_shard(p))
             for p in ("", "m_", "v_")}
    res = {}
    for ki, key in enumerate(keys):
        res[key] = _reduce_adam([recv[l][ki] for l in range(nl)], shard[""][key], shard["m_"][key], shard["v_"][key],
                                "reduce_adam_" + key)

    dmods = jnp.concatenate([grads[l]["dmod"] for l in range(nl)], axis=0)
    dmod_blocks = jnp.pad(dmods.reshape(nl, N_DEV, wcols).transpose(1, 0, 2), ((0, 0), (0, SUBLANES - nl), (0, 0)))
    widen = lambda a: jnp.pad(a, ((0, 0), (0, d - a.shape[1])))
    rows = [widen(grads[l]["acc"][n][0:1]) for l in range(nl) for n in REP_ROWS] + [g_final[0:1], widen(loss_p[0:1])]
    vp = jnp.concatenate(rows + [jnp.zeros(((-len(rows)) % SUBLANES, d), F32)], axis=0)
    (dmod_recv,), (vp_all,) = _run_comms([_exchange_comm([dmod_blocks], False), _exchange_comm([vp], True)], "tail_exchange")
    res["w_ada"] = _ada_grad_adam(cact_all.T, dmod_recv[:, 0:nl].transpose(1, 0, 2), w_ada, m_w_ada, v_w_ada)
    mp_r = matrices["mp"]
    lanes = lambda a: jnp.pad(a, ((0, 0), (0, LANES - a.shape[1])))
    shaped = dict(ml_b_if=lanes, final_g=lambda a: a.reshape(1, d))
    names = [n for n in REPLICATED if n != "b_ada"] + ["b_ada"]
    rep_res, vp_r = _adam_replicated(vp_all, mp_r, {n: tuple(shaped.get(n, lambda a: a)(given[p + n]) for p in ("", "m_", "v_"))
                                                    for n in names}, nl)
    unshaped = dict(ml_b_if=lambda a: a[:, 0:ml_b_if.shape[1]], final_g=lambda a: a.reshape(d))
    rep_out = [{n: unshaped.get(n, lambda a: a)(rep_res[n][kind]) for n in names} for kind in range(4)]
    loss = vp_r[nl * len(REP_ROWS) + 1, 0]

    if_rows = ml_w_if.shape[1]
    order = ("norm_g", "w_ada", "b_ada", "w_in", "rg_conv_w", "rg_conv_b", "rg_w_a", "rg_b_a", "rg_w_x", "rg_b_x",
             "rg_lambda", "ml_conv_w", "ml_conv_b", "ml_w_q", "ml_w_k", "ml_w_v", "ml_w_if", "ml_b_if", "ml_norm_g",
             "w_out", "final_g")
    outs = [loss, grad_x[None]]
    for kind in range(4):
        qkv_k = res["w_qkv"][kind].reshape((nl, 3) + ml_w_q.shape[1:])
        rg_cw, ml_cw, wif = _small_unpack(res["small"][kind], if_rows)
        sharded = dict(w_ada=res["w_ada"][kind], w_in=res["w_in"][kind], w_out=res["w_out"][kind], ml_w_q=qkv_k[:, 0],
                       ml_w_k=qkv_k[:, 1], ml_w_v=qkv_k[:, 2], rg_conv_w=rg_cw, ml_conv_w=ml_cw, ml_w_if=wif)
        for n in order:
            outs.append(sharded[n] if n in sharded else rep_out[kind][n])
    return tuple(outs)
```

```python
import functools

import jax
import jax.numpy as jnp
from jax import lax
from jax.experimental import pallas as pl
from jax.experimental.pallas import tpu as pltpu

F32 = jnp.float32
BF16 = jnp.bfloat16
MESH_AXES = ("x", "y", "c")
N_DEV = 8
EPS = 1e-6
RG_C = 8.0
ML_CHUNK = 128
CONV_WIDTH = 4
ADAM_LR = 0.001
ADAM_B1 = 0.9
ADAM_B2 = 0.999
ADAM_EPS = 1e-08
ADAM_WD = 0.01
ADAM_STEP = 10
NEG_BIG = -1e30
LANES = 128
SUBLANES = 8
VMEM_LIMIT = 56 * 1024 * 1024
HI = lax.Precision.HIGHEST


def _params(n_grid):
    return pltpu.CompilerParams(dimension_semantics=("arbitrary",) * n_grid, vmem_limit_bytes=VMEM_LIMIT)


def _mm(a, b):
    return jnp.dot(a.astype(BF16), b.astype(BF16), preferred_element_type=F32)


def _mm_nt(a, b):
    return lax.dot_general(a.astype(BF16), b.astype(BF16), (((1,), (1,)), ((), ())), preferred_element_type=F32)


def _mm_tn(a, b):
    return lax.dot_general(a.astype(BF16), b.astype(BF16), (((0,), (0,)), ((), ())), preferred_element_type=F32)


def _mm_hi(a, b):
    return jnp.dot(a, b, precision=HI, preferred_element_type=F32)


def _sigmoid(x):
    return 1.0 / (1.0 + jnp.exp(-x))


def _softplus(x):
    return jnp.maximum(x, 0.0) + jnp.log(1.0 + jnp.exp(-jnp.abs(x)))


def _neg_expm1(x):
    poly = -x * (1.0 + x * (0.5 + x * (1.0 / 6.0 + x * (1.0 / 24.0 + x * (1.0 / 120.0)))))
    return jnp.where(jnp.abs(x) < 0.05, poly, 1.0 - jnp.exp(x))


def _iota(shape, dim):
    return lax.broadcasted_iota(jnp.int32, shape, dim)


def _colsum(x):
    return jnp.sum(x, axis=0, keepdims=True)


def _rowsum(x):
    return jnp.sum(x, axis=1, keepdims=True)


def _col(x, j):
    return _rowsum(jnp.where(_iota(x.shape, 1) == j, x, 0.0))


def _row(x, j):
    return _colsum(jnp.where(_iota(x.shape, 0) == j, x, 0.0))


def _shift_down(x, j, prev8):
    if j == 0:
        return x
    t = x.shape[0]
    main = jnp.where(_iota(x.shape, 0) >= j, pltpu.roll(x, j, 0), 0.0)
    fix = jnp.where(_iota(prev8.shape, 0) < j, pltpu.roll(prev8, j, 0), 0.0)
    return jnp.concatenate([main[0:SUBLANES] + fix, main[SUBLANES:t]], axis=0)


def _shift_up(x, j, next8):
    if j == 0:
        return x
    t = x.shape[0]
    main = jnp.where(_iota(x.shape, 0) < t - j, pltpu.roll(x, t - j, 0), 0.0)
    fix = jnp.where(_iota(next8.shape, 0) >= SUBLANES - j, pltpu.roll(next8, SUBLANES - j, 0), 0.0)
    return jnp.concatenate([main[0:t - SUBLANES], main[t - SUBLANES:t] + fix], axis=0)


def _conv(x, prev8, w_ref):
    y = w_ref[CONV_WIDTH - 1:CONV_WIDTH, :] * x
    for j in range(1, CONV_WIDTH):
        y = y + w_ref[CONV_WIDTH - 1 - j:CONV_WIDTH - j, :] * _shift_down(x, j, prev8)
    return y


def _conv_bwd(dy, x, next8, w_ref, gw_ref):
    dx = None
    for j in range(CONV_WIDTH):
        k = CONV_WIDTH - 1 - j
        up = _shift_up(dy, j, next8)
        gw_ref[k:k + 1, :] += _colsum(up * x)
        term = w_ref[k:k + 1, :] * up
        dx = term if dx is None else dx + term
    return dx


def _scan_into(a, b, carry, out_ref, reverse):
    t, c = a.shape
    groups = t // SUBLANES
    a3 = a.reshape(groups, SUBLANES, c)
    b3 = b.reshape(groups, SUBLANES, c)
    sub = _iota(a3.shape, 1)
    for step in (1, 2, 4):
        keep = sub < SUBLANES - step if reverse else sub >= step
        shift = SUBLANES - step if reverse else step
        a_s = jnp.where(keep, pltpu.roll(a3, shift, 1), 1.0)
        b_s = jnp.where(keep, pltpu.roll(b3, shift, 1), 0.0)
        b3 = a3 * b_s + b3
        a3 = a3 * a_s
    for g in (reversed(range(groups)) if reverse else range(groups)):
        rows = slice(g * SUBLANES, (g + 1) * SUBLANES)
        out_ref[rows, :] = b3[g] + a3[g] * carry
        edge = g * SUBLANES if reverse else (g + 1) * SUBLANES - 1
        carry = out_ref[edge:edge + 1, :]


def _blockdiag(x, w_ref, transpose_w=False):
    nh, dh, _ = w_ref.shape
    outs = []
    for h in range(nh):
        xs = x[:, h * dh:(h + 1) * dh]
        outs.append(_mm_nt(xs, w_ref[h]) if transpose_w else _mm(xs, w_ref[h]))
    return jnp.concatenate(outs, axis=1)


def _rg_gates(xc, wa_ref, ba_ref, wx_ref, bx_ref, lam_ref):
    r = _sigmoid(_blockdiag(xc, wa_ref) + ba_ref[...])
    ig = _sigmoid(_blockdiag(xc, wx_ref) + bx_ref[...])
    sp = _softplus(-lam_ref[...])
    log_a = -RG_C * r * sp
    a = jnp.exp(log_a)
    beta = jnp.sqrt(_neg_expm1(2.0 * log_a))
    return r, ig, sp, a, beta


def _bcast8(row):
    return jnp.broadcast_to(row, (SUBLANES, row.shape[1]))


def _full(shape):
    nd = len(shape)
    return pl.BlockSpec(shape, lambda *_: (0,) * nd)


class _Comm:
    def __init__(self, arrays, out_shapes, sems, start, finish, aliases=()):
        self.arrays, self.out_shapes, self.sems = list(arrays), list(out_shapes), list(sems)
        self.start, self.finish, self.aliases = start, finish, tuple(aliases)
        self.results = None


class _RowOf:
    def __init__(self, ref, k):
        self.ref, self.k = ref, k

    def __getitem__(self, idx):
        cols = slice(None) if idx is Ellipsis else idx[1]
        return self.ref[0, self.k:self.k + 1, cols]


class _PartOf:
    def __init__(self, ref, rows=None, cols=None, lead=None):
        self.ref, self.rows, self.cols, self.lead = ref, rows, cols, lead
        if rows is not None:
            self.shape = (rows.stop - rows.start,) + tuple(ref.shape[1:])

    def _at(self, idx):
        if self.lead is not None:
            return (self.lead,) + tuple(idx[1:])
        if self.cols is not None:
            return (slice(None), self.cols)
        return (self.rows, slice(None) if idx is Ellipsis else idx[1])

    def __getitem__(self, idx):
        return self.ref[self._at(idx)]

    def __setitem__(self, idx, value):
        self.ref[self._at(idx)] = value


def _vec(table, layer, k):
    return ("row", table, layer, k)


def _is_row(arg):
    return isinstance(arg, tuple) and len(arg) == 4 and arg[0] == "row"


def _call(body, comms, *, name, grid, in_specs, out_specs, out_shape, args, scratch_shapes=(), aliases=None):
    comms = [cm for cm in (comms or []) if cm is not None]
    rows = {i: a[3] for i, a in enumerate(args) if _is_row(a)}
    in_specs = [pl.BlockSpec((1,) + a[1].shape[1:], functools.partial(lambda layer, *_: (layer, 0, 0), a[2]))
                if _is_row(a) else sp for a, sp in zip(args, in_specs)]
    args = tuple(a[1] if _is_row(a) else a for a in args)
    n_in, n_out, n_sc = len(args), len(out_shape), len(scratch_shapes)
    c_arrays = [a for cm in comms for a in cm.arrays]
    c_outs = [o for cm in comms for o in cm.out_shapes]
    c_sems = [sm for cm in comms for sm in cm.sems]
    aliases, a_at, o_at = dict(aliases or {}), n_in, n_out
    for cm in comms:
        for (i, j) in cm.aliases:
            aliases[a_at + i] = o_at + j
        a_at += len(cm.arrays)
        o_at += len(cm.out_shapes)

    def wrapped(*refs):
        ins, c_in = refs[:n_in], refs[n_in:n_in + len(c_arrays)]
        ins = [_RowOf(r, rows[i]) if i in rows else r for i, r in enumerate(ins)]
        at = n_in + len(c_arrays)
        outs, c_out = refs[at:at + n_out], refs[at + n_out:at + n_out + len(c_outs)]
        at += n_out + len(c_outs)
        scr, sems = refs[at:at + n_sc], refs[at + n_sc:]
        views, ia, io, isem = [], 0, 0, 0
        for cm in comms:
            views.append((c_in[ia:ia + len(cm.arrays)], c_out[io:io + len(cm.out_shapes)], sems[isem:isem + len(cm.sems)]))
            ia, io, isem = ia + len(cm.arrays), io + len(cm.out_shapes), isem + len(cm.sems)
        if comms:
            @pl.when(pl.program_id(0) == 0)
            def _():
                for cm, view in zip(comms, views):
                    cm.start(*view)

        body(*ins, *outs, *scr)
        if comms:
            @pl.when(pl.program_id(0) == grid[0] - 1)
            def _():
                for cm, view in zip(comms, views):
                    cm.finish(*view)

    hbm = pl.BlockSpec(memory_space=pl.ANY)
    res = pl.pallas_call(
        wrapped, name=name, grid=grid,
        in_specs=list(in_specs) + [hbm] * len(c_arrays), out_specs=list(out_specs) + [hbm] * len(c_outs),
        out_shape=list(out_shape) + c_outs, scratch_shapes=list(scratch_shapes) + c_sems,
        input_output_aliases=aliases, compiler_params=_params(len(grid)),
    )(*args, *c_arrays)
    at = n_out
    for cm in comms:
        cm.results = list(res[at:at + len(cm.out_shapes)])
        at += len(cm.out_shapes)
    return list(res[:n_out])


def _join_columns(w_ref, wcat):
    nd, _, _, w = w_ref.shape

    @pl.when(pl.program_id(0) == 0)
    def _():
        for j in range(nd):
            wcat[:, j * w:(j + 1) * w] = w_ref[j, 0]


def _in_fwd(x, ng, scale, shift, w_in_g, layer, tile, comms=None):
    s, d = x.shape
    nd, _, _, w = w_in_g.shape

    def body(x_ref, ng_ref, sc_ref, sh_ref, w_ref, u_ref, h_ref, wcat):
        _join_columns(w_ref, wcat)
        xv = x_ref[...]
        rs = lax.rsqrt(jnp.mean(xv * xv, axis=1, keepdims=True) + EPS)
        hb = (xv * rs * ng_ref[...] * (1.0 + sc_ref[...]) + sh_ref[...]).astype(BF16)
        h_ref[...] = hb
        u_ref[...] = jnp.dot(hb, wcat[...], preferred_element_type=F32)

    return _call(
        body, comms, name="in_proj_fwd", grid=(s // tile,),
        in_specs=[pl.BlockSpec((tile, d), lambda i: (i, 0)), _full((1, d)), _full((1, d)), _full((1, d)),
                  pl.BlockSpec((nd, 1, d, w), lambda i: (0, layer, 0, 0), pipeline_mode=pl.Buffered(1))],
        out_specs=[pl.BlockSpec((tile, nd * w), lambda i: (i, 0)), pl.BlockSpec((tile, d), lambda i: (i, 0))],
        out_shape=[jax.ShapeDtypeStruct((s, nd * w), F32), jax.ShapeDtypeStruct((s, d), BF16)],
        scratch_shapes=[pltpu.VMEM((d, nd * w), BF16)],
        args=(x, ng, scale, shift, w_in_g))


def _rg_fwd(u, d, conv_w, conv_b, w_a, b_a, w_x, b_x, lam, tile, comms=None):
    s = u.shape[0]

    def body(x_ref, z_ref, cw_ref, cb_ref, wa_ref, ba_ref, wx_ref, bx_ref, lam_ref,
             h_ref, y_ref, xc_ref, r_ref, i_ref, a_ref, beta_ref, prev8, hcar):
        @pl.when(pl.program_id(0) == 0)
        def _():
            prev8[...] = jnp.zeros_like(prev8)
            hcar[...] = jnp.zeros_like(hcar)

        x = x_ref[...]
        xc = _conv(x, prev8[...], cw_ref) + cb_ref[...]
        prev8[...] = x[tile - SUBLANES:tile, :]
        r, ig, _, a, beta = _rg_gates(xc, wa_ref, ba_ref, wx_ref, bx_ref, lam_ref)
        xc_ref[...] = xc
        r_ref[...] = r
        i_ref[...] = ig
        a_ref[...] = a
        beta_ref[...] = beta
        _scan_into(a, beta * ig * xc, hcar[SUBLANES - 1:SUBLANES, :], h_ref, False)
        h = h_ref[...]
        hcar[...] = h[tile - SUBLANES:tile, :]
        z = z_ref[...]
        y_ref[...] = (h * z * _sigmoid(z)).astype(BF16)

    vec = _full((1, d))
    return _call(
        body, comms, name="rglru_fwd", grid=(s // tile,),
        in_specs=[pl.BlockSpec((tile, d), lambda i: (i, 0)), pl.BlockSpec((tile, d), lambda i: (i, 1)),
                  _full(conv_w.shape), vec, _full(w_a.shape), vec, _full(w_x.shape), vec, vec],
        out_specs=[pl.BlockSpec((tile, d), lambda i: (i, 0))] * 7,
        out_shape=[jax.ShapeDtypeStruct((s, d), F32), jax.ShapeDtypeStruct((s, d), BF16)] + [jax.ShapeDtypeStruct((s, d), F32)] * 5,
        scratch_shapes=[pltpu.VMEM((SUBLANES, d), F32), pltpu.VMEM((SUBLANES, d), F32)],
        args=(u, u, conv_w, conv_b, w_a, b_a, w_x, b_x, lam))


def _ml_pre(u, d, conv_w, conv_b, w_q, w_k, w_v, wif, bif, tile, comms=None):
    s = u.shape[0]
    nh = w_q.shape[0]

    def body(x_ref, cw_ref, cb_ref, wq_ref, wk_ref, wv_ref, wif_ref, bif_ref, q_ref, k_ref, v_ref, g_ref, pre_ref, prev8):
        @pl.when(pl.program_id(0) == 0)
        def _():
            prev8[...] = jnp.zeros_like(prev8)

        x = x_ref[...]
        pre = _conv(x, prev8[...], cw_ref) + cb_ref[...]
        prev8[...] = x[tile - SUBLANES:tile, :]
        xc = pre * _sigmoid(pre)
        q = _blockdiag(xc, wq_ref)
        k = _blockdiag(xc, wk_ref)
        v = _blockdiag(x, wv_ref)
        pre_ref[...] = pre
        q_ref[...] = q
        k_ref[...] = k
        v_ref[...] = v
        g = _mm(q, wif_ref[0:d, :]) + _mm(k, wif_ref[d:2 * d, :]) + _mm(v, wif_ref[2 * d:3 * d, :]) + bif_ref[...]
        lane = _iota(g.shape, 1)
        gl = jnp.where(lane < 4, g, jnp.where(lane < 8, -_softplus(-g), 0.0))
        tri = jnp.where(_iota((ML_CHUNK, ML_CHUNK), 1) <= _iota((ML_CHUNK, ML_CHUNK), 0), 1.0, 0.0)
        cums = [_mm_hi(tri, gl[c * ML_CHUNK:(c + 1) * ML_CHUNK, :]) for c in range(tile // ML_CHUNK)]
        cum = cums[0] if len(cums) == 1 else jnp.concatenate(cums, axis=0)
        g_ref[...] = gl + jnp.where((lane >= 8) & (lane < 12), pltpu.roll(cum, 4, 1), 0.0)

    vec = _full((1, d))
    return _call(
        body, comms, name="mlstm_proj_fwd", grid=(s // tile,),
        in_specs=[pl.BlockSpec((tile, d), lambda i: (i, 2)), _full(conv_w.shape), vec,
                  _full(w_q.shape), _full(w_k.shape), _full(w_v.shape), _full(wif.shape), _full((1, LANES))],
        out_specs=[pl.BlockSpec((tile, d), lambda i: (i, 0))] * 3 + [pl.BlockSpec((tile, LANES), lambda i: (i, 0)),
                                                                     pl.BlockSpec((tile, d), lambda i: (i, 0))],
        out_shape=[jax.ShapeDtypeStruct((s, d), F32)] * 3 + [jax.ShapeDtypeStruct((s, LANES), F32),
                                                             jax.ShapeDtypeStruct((s, d), F32)],
        scratch_shapes=[pltpu.VMEM((SUBLANES, d), F32)],
        args=(u, conv_w, conv_b, w_q, w_k, w_v, wif, bif))


CELL_CHUNKS_PER_STEP = 4
CELL_BWD_CHUNKS_PER_STEP = 2


def _cell_chunk(h, nh, q_ref, k_ref, v_ref, gc, gr, m_prev, c_h, n_h, m_t=None, r0=0):
    lc = ML_CHUNK
    dh = q_ref.shape[1] // nh
    sl = slice(h * dh, (h + 1) * dh)
    qh = q_ref[r0:r0 + lc, sl]
    kh = k_ref[r0:r0 + lc, sl] * (dh ** -0.5)
    vh = v_ref[r0:r0 + lc, sl]
    li_c = _col(gc, h)
    b_c = _col(gc, 8 + h)
    lib_r = _row(gr, h) - _row(gr, 8 + h)
    b_last = _colsum(jnp.where(_iota((lc, 1), 0) == lc - 1, b_c, 0.0))
    causal = _iota((lc, lc), 1) <= _iota((lc, lc), 0)
    dmat = jnp.where(causal, b_c + lib_r, NEG_BIG)
    m_inter = b_c + m_prev
    if m_t is None:
        m_t = jnp.maximum(m_inter, jnp.max(dmat, axis=1, keepdims=True))
    w_intra = jnp.exp(dmat - m_t)
    w_inter = jnp.exp(m_inter - m_t)
    amat = _mm_nt(qh, kh)
    smat = amat * w_intra
    qc = _mm(qh, c_h)
    qn = _rowsum(qh * n_h)
    den = _rowsum(smat) + w_inter * qn
    gst = b_last - b_c + li_c
    m_new = jnp.maximum(b_last + m_prev, jnp.max(gst, axis=0, keepdims=True))
    w_state = jnp.exp(gst - m_new)
    decay = jnp.exp(b_last + m_prev - m_new)
    return dict(sl=sl, qh=qh, kh=kh, vh=vh, m_t=m_t, w_intra=w_intra, w_inter=w_inter, smat=smat, qc=qc, qn=qn,
                den=den, m_new=m_new, w_state=w_state, decay=decay)


def _ml_cell_fwd(q, k, v, gcol, grow, u, ng, nh, comms=None):
    s, d = q.shape
    lc = ML_CHUNK
    nc = s // lc
    dh = d // nh

    per = CELL_CHUNKS_PER_STEP if nc % CELL_CHUNKS_PER_STEP == 0 else 1

    def body(q_ref, k_ref, v_ref, gc_ref, gr_ref, o_ref, z_ref, ng_ref,
             cell_ref, y_ref, cs_ref, ns_ref, ms_ref, mt_ref, c_sc, n_sc, m_sc):
        @pl.when(pl.program_id(0) == 0)
        def _():
            c_sc[...] = jnp.zeros_like(c_sc)
            n_sc[...] = jnp.zeros_like(n_sc)
            m_sc[...] = jnp.zeros_like(m_sc)

        lane = _iota((lc, LANES), 1)
        for cc in range(per):
            rows = slice(cc * lc, (cc + 1) * lc)
            gc = gc_ref[rows, :]
            gr = gr_ref[:, rows]
            mt_acc = jnp.zeros((lc, LANES), F32)
            for h in range(nh):
                c_h = c_sc[h]
                n_h = n_sc[h, 0:1, :]
                m_prev = jnp.max(m_sc[h, 0:1, :], axis=1, keepdims=True)
                cs_ref[cc, h] = c_h
                ns_ref[cc, h] = n_sc[h]
                ms_ref[cc, h] = m_sc[h]
                t = _cell_chunk(h, nh, q_ref, k_ref, v_ref, gc, gr, m_prev, c_h, n_h, r0=cc * lc)
                sl = t["sl"]
                num = _mm(t["smat"], t["vh"]) + t["w_inter"] * t["qc"]
                cell_h = num / jnp.maximum(jnp.abs(t["den"]), jnp.exp(-t["m_t"]))
                mt_acc = jnp.where(lane == h, t["m_t"], mt_acc)
                kw = t["kh"] * t["w_state"]
                c_sc[h] = t["decay"] * c_h + _mm_tn(kw, t["vh"])
                n_sc[h] = _bcast8(t["decay"] * n_h + _colsum(kw))
                m_sc[h] = jnp.broadcast_to(t["m_new"], (SUBLANES, LANES))
                hg = _sigmoid(o_ref[rows, sl]) * cell_h
                hn = hg * lax.rsqrt(jnp.mean(hg * hg, axis=1, keepdims=True) + EPS)
                z = z_ref[rows, sl]
                cell_ref[rows, sl] = cell_h
                y_ref[rows, sl] = (hn * ng_ref[:, sl] * z * _sigmoid(z)).astype(BF16)
            mt_ref[rows, :] = mt_acc

    tok = pl.BlockSpec((per * lc, d), lambda c: (c, 0))
    return _call(
        body, comms, name="mlstm_cell_fwd", grid=(nc // per,),
        in_specs=[tok, tok, tok, pl.BlockSpec((per * lc, LANES), lambda c: (c, 0)),
                  pl.BlockSpec((16, per * lc), lambda c: (0, c)),
                  pl.BlockSpec((per * lc, d), lambda c: (c, 3)), pl.BlockSpec((per * lc, d), lambda c: (c, 4)), _full((1, d))],
        out_specs=[tok, tok, pl.BlockSpec((per, nh, dh, dh), lambda c: (c, 0, 0, 0)),
                   pl.BlockSpec((per, nh, SUBLANES, dh), lambda c: (c, 0, 0, 0)),
                   pl.BlockSpec((per, nh, SUBLANES, LANES), lambda c: (c, 0, 0, 0)),
                   pl.BlockSpec((per * lc, LANES), lambda c: (c, 0))],
        out_shape=[jax.ShapeDtypeStruct((s, d), F32), jax.ShapeDtypeStruct((s, d), BF16),
                   jax.ShapeDtypeStruct((nc, nh, dh, dh), F32), jax.ShapeDtypeStruct((nc, nh, SUBLANES, dh), F32),
                   jax.ShapeDtypeStruct((nc, nh, SUBLANES, LANES), F32), jax.ShapeDtypeStruct((s, LANES), F32)],
        scratch_shapes=[pltpu.VMEM((nh, dh, dh), F32), pltpu.VMEM((nh, SUBLANES, dh), F32),
                        pltpu.VMEM((nh, SUBLANES, LANES), F32)],
        args=(q, k, v, gcol, grow, u, u, ng))


def _out_fwd(x, y_rg, y_ml, gate, w_out_g, layer, tile, comms=None, head=None):
    s, d = x.shape
    nd, _, r, _ = w_out_g.shape

    def body(x_ref, yr_ref, ym_ref, g_ref, w_ref, *rest):
        ycat = jnp.concatenate([yr_ref[...].astype(BF16), ym_ref[...].astype(BF16)], axis=1)
        acc = jnp.dot(ycat, w_ref[...].reshape(nd * r, d), preferred_element_type=F32)
        xn = x_ref[...] + g_ref[...] * acc
        if head is None:
            xn_ref, y_ref = rest
            y_ref[...] = acc
            xn_ref[...] = xn
            return
        fg_ref, t_ref, y_ref, dx_ref, loss_ref, gg_ref = rest
        y_ref[...] = acc

        @pl.when(pl.program_id(0) == 0)
        def _():
            loss_ref[...] = jnp.zeros_like(loss_ref)
            gg_ref[...] = jnp.zeros_like(gg_ref)

        fg = fg_ref[...]
        rs = lax.rsqrt(jnp.mean(xn * xn, axis=1, keepdims=True) + EPS)
        xh = xn * rs
        e = xh * fg - t_ref[...]
        loss_ref[...] += jnp.broadcast_to(_colsum(_rowsum(e * e)) * (0.5 / d), loss_ref.shape)
        dy = e * (1.0 / d)
        gg_ref[...] += _bcast8(_colsum(dy * xh))
        dxh = dy * fg
        dx_ref[...] = rs * (dxh - xh * jnp.mean(dxh * xh, axis=1, keepdims=True))

    tok = pl.BlockSpec((tile, d), lambda i: (i, 0))
    in_specs = [tok, tok, tok, _full((1, d)), pl.BlockSpec((nd, 1, r, d), lambda i: (0, layer, 0, 0))]
    if head is None:
        return _call(body, comms, name="out_proj_fwd", grid=(s // tile,), in_specs=in_specs, out_specs=[tok, tok],
                     out_shape=[jax.ShapeDtypeStruct((s, d), F32)] * 2, args=(x, y_rg, y_ml, gate, w_out_g))
    return _call(
        body, comms, name="out_proj_loss", grid=(s // tile,),
        in_specs=in_specs + [_full((1, d)), tok],
        out_specs=[tok, tok, _full((SUBLANES, LANES)), _full((SUBLANES, d))],
        out_shape=[jax.ShapeDtypeStruct((s, d), F32)] * 2 + [jax.ShapeDtypeStruct((SUBLANES, LANES), F32),
                                                             jax.ShapeDtypeStruct((SUBLANES, d), F32)],
        args=(x, y_rg, y_ml, gate, w_out_g, *head))


def _ml_out_stage_bwd(dy, cell, o, z, ng):
    so = _sigmoid(o)
    hg = so * cell
    rinv = lax.rsqrt(jnp.mean(hg * hg, axis=1, keepdims=True) + EPS)
    hn = hg * rinv
    sz = _sigmoid(z)
    dz = dy * hn * ng * (sz + z * sz * (1.0 - sz))
    dymid = dy * z * sz
    dhn = dymid * ng
    dhg = rinv * (dhn - hn * jnp.mean(dhn * hn, axis=1, keepdims=True))
    return dz, dhg * cell * so * (1.0 - so), dhg * so, _colsum(dymid * hn)


def _out_bwd(dxo, gate, y, y_rg, y_ml, w_out_g, layer, tile, comms=None):
    s, d = dxo.shape
    nd, _, r, _ = w_out_g.shape

    def body(dx_ref, g_ref, y_ref, yr_ref, ym_ref, w_ref, dyr_ref, dym_ref, gw_ref, dg_ref):
        @pl.when(pl.program_id(0) == 0)
        def _():
            gw_ref[...] = jnp.zeros_like(gw_ref)
            dg_ref[...] = jnp.zeros_like(dg_ref)

        dxv = dx_ref[...]
        dg_ref[...] += _bcast8(_colsum(dxv * y_ref[...]))
        dyb = (dxv * g_ref[...]).astype(BF16)
        dycat = lax.dot_general(dyb, w_ref[...].reshape(nd * r, d), (((1,), (1,)), ((), ())), preferred_element_type=F32)
        dyr_ref[...] = dycat[:, 0:d]
        dym_ref[...] = dycat[:, d:2 * d]
        ycat = jnp.concatenate([yr_ref[...].astype(BF16), ym_ref[...].astype(BF16)], axis=1)
        gw_ref[...] += lax.dot_general(ycat, dyb, (((0,), (0,)), ((), ())), preferred_element_type=F32).reshape(nd, r, d)

    tok = pl.BlockSpec((tile, d), lambda i: (i, 0))
    return _call(
        body, comms, name="out_proj_bwd", grid=(s // tile,),
        in_specs=[tok, _full((1, d)), tok, tok, tok, pl.BlockSpec((nd, 1, r, d), lambda i: (0, layer, 0, 0))],
        out_specs=[tok, tok, _full((nd, r, d)), _full((SUBLANES, d))],
        out_shape=[jax.ShapeDtypeStruct((s, d), F32)] * 2 + [jax.ShapeDtypeStruct((nd, r, d), F32),
                                                             jax.ShapeDtypeStruct((SUBLANES, d), F32)],
        args=(dxo, gate, y, y_rg, y_ml, w_out_g))


def _ml_cell_bwd(dy_ml, u, cell, q, k, v, gcol, grow, mt, cs, ns, ms, ng, nh, comms=None):
    s, d = q.shape
    lc = ML_CHUNK
    nc = s // lc
    dh = d // nh

    per = CELL_BWD_CHUNKS_PER_STEP if nc % CELL_BWD_CHUNKS_PER_STEP == 0 else 1

    def body(*refs):
        gng_ref, dc_sc, dn_sc = refs[20], refs[21], refs[22]

        @pl.when(pl.program_id(0) == 0)
        def _():
            dc_sc[...] = jnp.zeros_like(dc_sc)
            dn_sc[...] = jnp.zeros_like(dn_sc)
            gng_ref[...] = jnp.zeros_like(gng_ref)

        for cc in reversed(range(per)):
            rows = slice(cc * lc, (cc + 1) * lc)
            views = [refs[at] if at == 13 else _PartOf(refs[at], cols=rows) if at == 8 else
                     _PartOf(refs[at], lead=cc) if at in (10, 11, 12) else _PartOf(refs[at], rows=rows) for at in range(20)]
            chunk(*views, gng_ref, dc_sc, dn_sc)

    def chunk(dy_ref, o_ref, z_ref, cell_ref, q_ref, k_ref, v_ref, gc_ref, gr_ref, mt_ref, cs_ref, ns_ref, ms_ref,
              ng_ref, dq_ref, dk_ref, dv_ref, dg_ref, do_ref, dz_ref, gng_ref, dc_sc, dn_sc):
        gc = gc_ref[...]
        gr = gr_ref[...]
        mtv = mt_ref[...]
        lane = _iota((lc, LANES), 1)
        rowv = _iota((lc, 1), 0)
        dg_acc = jnp.zeros((lc, LANES), F32)
        for h in range(nh):
            c_h = cs_ref[0, h]
            n_h = ns_ref[0, h, 0:1, :]
            m_prev = jnp.max(ms_ref[0, h, 0:1, :], axis=1, keepdims=True)
            t = _cell_chunk(h, nh, q_ref, k_ref, v_ref, gc, gr, m_prev, c_h, n_h, m_t=_col(mtv, h))
            sl, qh, kh, vh = t["sl"], t["qh"], t["kh"], t["vh"]
            w_intra, w_inter, smat, w_state, decay = t["w_intra"], t["w_inter"], t["smat"], t["w_state"], t["decay"]
            cell_h = cell_ref[:, sl]
            dz, do, dcell, gng = _ml_out_stage_bwd(dy_ref[:, sl], cell_h, o_ref[:, sl], z_ref[:, sl], ng_ref[:, sl])
            dz_ref[:, sl] = dz.astype(BF16)
            do_ref[:, sl] = do.astype(BF16)
            gng_ref[:, sl] += _bcast8(gng)
            eneg = jnp.exp(-t["m_t"])
            aden = jnp.abs(t["den"])
            nst = jnp.maximum(aden, eneg)
            dnum = dcell / nst
            dden = jnp.where(aden > eneg, -_rowsum(cell_h * dcell) / nst * jnp.sign(t["den"]), 0.0)
            pmat = _mm_nt(dnum, vh) + dden
            damat = pmat * w_intra
            gmat = pmat * smat
            wdn = w_inter * dnum
            wdd = w_inter * dden
            dqh = _mm(damat, kh) + _mm_nt(wdn, c_h) + wdd * n_h
            dkh = _mm_tn(damat, qh)
            dvh = _mm_tn(smat, dnum)
            dw_inter = _rowsum(dnum * t["qc"]) + dden * t["qn"]
            dcn = dc_sc[h]
            dnn = dn_sc[h, 0:1, :]
            kw = kh * w_state
            dkw = _mm_nt(vh, dcn) + dnn
            dvh = dvh + _mm(kw, dcn)
            dkh = dkh + dkw * w_state
            dgst = _rowsum(dkw * kh) * w_state
            ddecay = _colsum(_rowsum(dcn * c_h)) + _rowsum(dnn * n_h)
            db_last = _colsum(dgst) + ddecay * decay
            rs_g = _rowsum(gmat)
            cs_g = _rowsum(gmat.T)
            db = rs_g - cs_g + dw_inter * w_inter - dgst + jnp.where(rowv == lc - 1, db_last, 0.0)
            dli = cs_g + dgst
            dc_sc[h] = decay * dcn + _mm_tn(qh, wdn)
            dn_sc[h] = _bcast8(decay * dnn + _colsum(qh * wdd))
            dq_ref[:, sl] = dqh
            dk_ref[:, sl] = dkh * (dh ** -0.5)
            dv_ref[:, sl] = dvh
            dg_acc = jnp.where(lane == h, dli, jnp.where(lane == 4 + h, db, dg_acc))
        dg_ref[...] = dg_acc

    rev = lambda c: nc // per - 1 - c
    tok = pl.BlockSpec((per * lc, d), lambda c: (rev(c), 0))
    g128 = pl.BlockSpec((per * lc, LANES), lambda c: (rev(c), 0))
    return _call(
        body, comms, name="mlstm_cell_bwd", grid=(nc // per,),
        in_specs=[tok, pl.BlockSpec((per * lc, d), lambda c: (rev(c), 3)), pl.BlockSpec((per * lc, d), lambda c: (rev(c), 4)),
                  tok, tok, tok, tok, g128, pl.BlockSpec((16, per * lc), lambda c: (0, rev(c))), g128,
                  pl.BlockSpec((per, nh, dh, dh), lambda c: (rev(c), 0, 0, 0)),
                  pl.BlockSpec((per, nh, SUBLANES, dh), lambda c: (rev(c), 0, 0, 0)),
                  pl.BlockSpec((per, nh, SUBLANES, LANES), lambda c: (rev(c), 0, 0, 0)), _full((1, d))],
        out_specs=[tok, tok, tok, g128, tok, tok, _full((SUBLANES, d))],
        out_shape=[jax.ShapeDtypeStruct((s, d), F32)] * 3 + [jax.ShapeDtypeStruct((s, LANES), F32)]
        + [jax.ShapeDtypeStruct((s, d), BF16)] * 2 + [jax.ShapeDtypeStruct((SUBLANES, d), F32)],
        scratch_shapes=[pltpu.VMEM((nh, dh, dh), F32), pltpu.VMEM((nh, SUBLANES, dh), F32)],
        args=(dy_ml, u, u, cell, q, k, v, gcol, grow, mt, cs, ns, ms, ng))


def _halo_spec(d, tile, nt, col):
    per = tile // SUBLANES
    return pl.BlockSpec((SUBLANES, d), lambda i: (jnp.maximum((nt - 1 - i) * per - 1, 0), col))


def _ml_pre_bwd(dq, dk, dv, dgates, gcol, u, pre, q, k, v, conv_w, w_q, w_k, w_v, wif_t, tile, comms=None):
    s, d = dq.shape
    nt = s // tile
    nh, dh, _ = w_q.shape

    def body(dq_ref, dk_ref, dv_ref, dg_ref, gc_ref, x_ref, pre_ref, q_ref, k_ref, v_ref, cw_ref,
             wq_ref, wk_ref, wv_ref, wift_ref,
             dx_ref, gwq_ref, gwk_ref, gwv_ref, gwif_ref, gbif_ref, gcw_ref, gcb_ref, next8):
        @pl.when(pl.program_id(0) == 0)
        def _():
            next8[...] = jnp.zeros_like(next8)
            for ref in (gwq_ref, gwk_ref, gwv_ref, gwif_ref, gbif_ref, gcw_ref, gcb_ref):
                ref[...] = jnp.zeros_like(ref)

        x = x_ref[...]
        pre = pre_ref[...]
        sg = _sigmoid(pre)
        xc = pre * sg
        dgc = dg_ref[...]
        lane = _iota(dgc.shape, 1)
        utri = jnp.where(_iota((ML_CHUNK, ML_CHUNK), 0) <= _iota((ML_CHUNK, ML_CHUNK), 1), 1.0, 0.0)
        rcs = [_mm_hi(utri, dgc[c * ML_CHUNK:(c + 1) * ML_CHUNK, :]) for c in range(tile // ML_CHUNK)]
        rc = rcs[0] if len(rcs) == 1 else jnp.concatenate(rcs, axis=0)
        dgates_v = jnp.where(lane < 4, dgc, jnp.where(lane < 8, rc * (1.0 - jnp.exp(gc_ref[...])), 0.0))
        dgb = dgates_v.astype(BF16)
        gbif_ref[...] += jnp.broadcast_to(_colsum(dgates_v), gbif_ref.shape)
        ext = jnp.dot(dgb, wift_ref[...], preferred_element_type=F32)
        dqt = dq_ref[...] + ext[:, 0:d]
        dkt = dk_ref[...] + ext[:, d:2 * d]
        dvt = dv_ref[...] + ext[:, 2 * d:3 * d]
        gwif_ref[:, 0:d] += _mm_tn(dgb, q_ref[...])
        gwif_ref[:, d:2 * d] += _mm_tn(dgb, k_ref[...])
        gwif_ref[:, 2 * d:3 * d] += _mm_tn(dgb, v_ref[...])
        dxc_parts, dxv_parts = [], []
        for h in range(nh):
            sl = slice(h * dh, (h + 1) * dh)
            gwq_ref[h] += _mm_tn(xc[:, sl], dqt[:, sl])
            gwk_ref[h] += _mm_tn(xc[:, sl], dkt[:, sl])
            gwv_ref[h] += _mm_tn(x[:, sl], dvt[:, sl])
            dxc_parts.append(_mm_nt(dqt[:, sl], wq_ref[h]) + _mm_nt(dkt[:, sl], wk_ref[h]))
            dxv_parts.append(_mm_nt(dvt[:, sl], wv_ref[h]))
        dxc = jnp.concatenate(dxc_parts, axis=1)
        dxv = jnp.concatenate(dxv_parts, axis=1)
        dpre = dxc * (sg + pre * sg * (1.0 - sg))
        gcb_ref[...] += _bcast8(_colsum(dpre))
        dx_ref[...] = (dxv + _conv_bwd(dpre, x, next8[...], cw_ref, gcw_ref)).astype(BF16)
        next8[...] = dpre[0:SUBLANES, :]

    rev = lambda i: nt - 1 - i
    tok = pl.BlockSpec((tile, d), lambda i: (rev(i), 0))
    g128 = pl.BlockSpec((tile, LANES), lambda i: (rev(i), 0))
    wsh = (nh, dh, dh)
    return _call(
        body, comms, name="mlstm_proj_bwd", grid=(nt,),
        in_specs=[tok, tok, tok, g128, g128, pl.BlockSpec((tile, d), lambda i: (rev(i), 2)), tok,
                  tok, tok, tok, _full(conv_w.shape), _full(wsh), _full(wsh), _full(wsh), _full(wif_t.shape)],
        out_specs=[tok, _full(wsh), _full(wsh), _full(wsh), _full((LANES, 3 * d)), _full((SUBLANES, LANES)),
                   _full((SUBLANES, d)), _full((SUBLANES, d))],
        out_shape=[jax.ShapeDtypeStruct((s, d), BF16)] + [jax.ShapeDtypeStruct(wsh, F32)] * 3
        + [jax.ShapeDtypeStruct((LANES, 3 * d), F32), jax.ShapeDtypeStruct((SUBLANES, LANES), F32),
           jax.ShapeDtypeStruct((SUBLANES, d), F32), jax.ShapeDtypeStruct((SUBLANES, d), F32)],
        scratch_shapes=[pltpu.VMEM((SUBLANES, d), F32)],
        args=(dq, dk, dv, dgates, gcol, u, pre, q, k, v, conv_w, w_q, w_k, w_v, wif_t))


def _rg_bwd(dy_rg, u, h_rg, gates, conv_w, w_a, w_x, lam, tile, comms=None):
    s, d = dy_rg.shape
    nt = s // tile
    nh, dh, _ = w_a.shape

    def body(dy_ref, x_ref, z_ref, h_ref, hhalo_ref, xc_ref, r_ref, i_ref, a_ref, beta_ref, cw_ref, wa_ref,
             wx_ref, lam_ref,
             dx_ref, dz_ref, gwa_ref, gwx_ref, gba_ref, gbx_ref, glam_ref, gcw_ref, gcb_ref, next8, anext, dnext, dbuf):
        i = pl.program_id(0)

        @pl.when(i == 0)
        def _():
            for ref in (next8, anext, dnext, gwa_ref, gwx_ref, gba_ref, gbx_ref, glam_ref, gcw_ref, gcb_ref):
                ref[...] = jnp.zeros_like(ref)

        inner = jnp.where(i < nt - 1, 1.0, 0.0)
        xc, r, ig, a, beta = xc_ref[...], r_ref[...], i_ref[...], a_ref[...], beta_ref[...]
        sp = _softplus(-lam_ref[...])
        h = h_ref[...]
        row = _iota(h.shape, 0)
        hprev = jnp.where(row >= 1, pltpu.roll(h, 1, 0), hhalo_ref[SUBLANES - 1:SUBLANES, :] * inner)
        z = z_ref[...]
        sz = _sigmoid(z)
        dyv = dy_ref[...]
        dz_ref[...] = (dyv * h * (sz + z * sz * (1.0 - sz))).astype(BF16)
        a_up = jnp.where(row < tile - 1, pltpu.roll(a, tile - 1, 0), anext[0:1, :])
        _scan_into(a_up, dyv * z * sz, dnext[0:1, :], dbuf, True)
        delta = dbuf[...]
        anext[...] = a[0:SUBLANES, :]
        dnext[...] = delta[0:SUBLANES, :]
        dla = delta * hprev * a - delta * ig * xc * (a * a / beta)
        glam_ref[...] += _bcast8(_colsum(dla * r) * (RG_C * _sigmoid(-lam_ref[...])))
        dpa = dla * (-RG_C * sp) * r * (1.0 - r)
        dpx = delta * beta * xc * ig * (1.0 - ig)
        gba_ref[...] += _bcast8(_colsum(dpa))
        gbx_ref[...] += _bcast8(_colsum(dpx))
        parts = []
        for hh in range(nh):
            sl = slice(hh * dh, (hh + 1) * dh)
            gwa_ref[hh] += _mm_tn(xc[:, sl], dpa[:, sl])
            gwx_ref[hh] += _mm_tn(xc[:, sl], dpx[:, sl])
            parts.append(_mm_nt(dpa[:, sl], wa_ref[hh]) + _mm_nt(dpx[:, sl], wx_ref[hh]))
        dxc = delta * beta * ig + jnp.concatenate(parts, axis=1)
        gcb_ref[...] += _bcast8(_colsum(dxc))
        dx_ref[...] = _conv_bwd(dxc, x_ref[...], next8[...], cw_ref, gcw_ref).astype(BF16)
        next8[...] = dxc[0:SUBLANES, :]

    rev = lambda i: nt - 1 - i
    tok = pl.BlockSpec((tile, d), lambda i: (rev(i), 0))
    vec = _full((1, d))
    acc = _full((SUBLANES, d))
    wsh = (nh, dh, dh)
    return _call(
        body, comms, name="rglru_bwd", grid=(nt,),
        in_specs=[tok, tok, pl.BlockSpec((tile, d), lambda i: (rev(i), 1)), tok,
                  _halo_spec(d, tile, nt, 0)] + [tok] * 5 + [_full(conv_w.shape), _full(wsh), _full(wsh), vec],
        out_specs=[tok, tok, _full(wsh), _full(wsh), acc, acc, acc, acc, acc],
        out_shape=[jax.ShapeDtypeStruct((s, d), BF16)] * 2 + [jax.ShapeDtypeStruct(wsh, F32)] * 2
        + [jax.ShapeDtypeStruct((SUBLANES, d), F32)] * 5,
        scratch_shapes=[pltpu.VMEM((SUBLANES, d), F32)] * 3 + [pltpu.VMEM((tile, d), F32)],
        args=(dy_rg, u, u, h_rg, h_rg, *gates, conv_w, w_a, w_x, lam))


def _segments(d, w, n_pieces, n_slots):
    bounds = sorted({k * d for k in range(n_pieces + 1)} | {j * w for j in range(n_slots + 1)})
    return [(lo // d, lo % d, lo // w, lo % w, hi - lo) for lo, hi in zip(bounds[:-1], bounds[1:])]


def _in_bwd(pieces, x, dxo, ng, scale, w_in_g, layer, tile, comms=None, tiles=None, prev=None):
    s, d = x.shape
    nd, _, _, w = w_in_g.shape
    first, count = tiles or (0, s // tile)
    n_p = len(pieces)

    def body(*refs):
        p_refs = refs[:n_p]
        x_ref, dxo_ref, ng_ref, sc_ref, w_ref = refs[n_p:n_p + 5]
        dx_ref, dsc_ref, dsh_ref, gng_ref, wcat = refs[-5:]
        _join_columns(w_ref, wcat)

        @pl.when(pl.program_id(0) == 0)
        def _():
            for k, ref in enumerate((dsc_ref, dsh_ref, gng_ref)):
                ref[...] = jnp.zeros_like(ref) if prev is None else refs[n_p + 6 + k][...]

        du = jnp.concatenate([p[...] for p in p_refs], axis=1)
        dh = lax.dot_general(du, wcat[...], (((1,), (1,)), ((), ())), preferred_element_type=F32)
        xv = x_ref[...]
        g = ng_ref[...]
        rs = lax.rsqrt(jnp.mean(xv * xv, axis=1, keepdims=True) + EPS)
        xh = xv * rs
        dsh_ref[...] += _bcast8(_colsum(dh))
        dsc_ref[...] += _bcast8(_colsum(dh * xh * g))
        dhn = dh * (1.0 + sc_ref[...])
        gng_ref[...] += _bcast8(_colsum(dhn * xh))
        dxh = dhn * g
        dx_ref[...] = dxo_ref[...] + rs * (dxh - xh * jnp.mean(dxh * xh, axis=1, keepdims=True))

    tok = pl.BlockSpec((tile, d), lambda i: (i + first, 0))
    vec = _full((1, d))
    acc = _full((SUBLANES, d))
    more_specs = [] if prev is None else [pl.BlockSpec(memory_space=pl.ANY), acc, acc, acc]
    return _call(
        body, comms, name="in_proj_bwd_x", grid=(count,),
        in_specs=[tok] * n_p + [tok, tok, vec, vec, pl.BlockSpec((nd, 1, d, w), lambda i: (0, layer, 0, 0),
                                                               pipeline_mode=pl.Buffered(1))] + more_specs,
        out_specs=[tok, acc, acc, acc],
        out_shape=[jax.ShapeDtypeStruct((s, d), F32)] + [jax.ShapeDtypeStruct((SUBLANES, d), F32)] * 3,
        scratch_shapes=[pltpu.VMEM((d, nd * w), BF16)],
        args=(*pieces, x, dxo, ng, scale, w_in_g) + (() if prev is None else tuple(prev)),
        aliases={} if prev is None else {n_p + 5: 0})


def _in_bwd_w(pieces, hbf, w, slots, tile, comms=None):
    s, d = hbf.shape
    nd_all = len(pieces) * d // w
    segs = [sg for sg in _segments(d, w, len(pieces), nd_all) if sg[2] in slots]

    def body(*refs):
        p_refs = refs[:len(pieces)]
        h_ref, gw_ref = refs[len(pieces):]

        @pl.when(pl.program_id(0) == 0)
        def _():
            gw_ref[...] = jnp.zeros_like(gw_ref)

        hv = h_ref[...]
        for (kk, a, j, b, width) in segs:
            gw_ref[j - slots[0], :, b:b + width] += _mm_tn(hv, p_refs[kk][:, a:a + width])

    tok = pl.BlockSpec((tile, d), lambda i: (i, 0))
    return _call(
        body, comms, name="in_proj_bwd_w", grid=(s // tile,),
        in_specs=[tok] * len(pieces) + [tok],
        out_specs=[pl.BlockSpec((len(slots), d, w), lambda i: (0, 0, 0), pipeline_mode=pl.Buffered(1))],
        out_shape=[jax.ShapeDtypeStruct((len(slots), d, w), F32)],
        args=(*pieces, hbf))[0]


def _exchange(arrs, gather, name):
    return _run_comms([_exchange_comm(arrs, gather)], name)[0]


def _run_comms(comms, name):
    _call(lambda: None, comms, name=name, grid=(1,), in_specs=[], out_specs=[], out_shape=[], args=())
    return [cm.results for cm in comms]


def _exchange_comm(arrs, gather):
    n = len(arrs)
    per = N_DEV - 1

    def copies(ins, outs, sems):
        send_sems, recv_sems, local_sems = sems
        x, y, c = (lax.axis_index(ax) for ax in MESH_AXES)
        me = 4 * x + 2 * y + c
        sends, recvs = [], []
        for flip in range(1, N_DEV):
            px = x ^ ((flip >> 2) & 1)
            py = y ^ ((flip >> 1) & 1)
            pc = c ^ (flip & 1)
            peer = 4 * px + 2 * py + pc
            for kk in range(n):
                src = ins[kk] if gather else ins[kk].at[peer]
                sends.append(_remote(src, outs[kk].at[me], send_sems, recv_sems, kk * per + flip - 1, (px, py, pc)))
                recvs.append(_remote(src, outs[kk].at[peer], send_sems, recv_sems, kk * per + flip - 1, (px, py, pc)))
        local = [pltpu.make_async_copy(ins[kk] if gather else ins[kk].at[me], outs[kk].at[me], local_sems.at[kk])
                 for kk in range(n)]
        return local, sends, recvs

    def start(ins, outs, sems):
        local, sends, _ = copies(ins, outs, sems)
        for cp in sends + local:
            cp.start()

    def finish(ins, outs, sems):
        local, sends, recvs = copies(ins, outs, sems)
        for cp in recvs:
            cp.wait_recv()
        for cp in sends:
            cp.wait_send()
        for cp in local:
            cp.wait()

    return _Comm(arrs, [jax.ShapeDtypeStruct((N_DEV,) + a.shape if gather else a.shape, a.dtype) for a in arrs],
                 [pltpu.SemaphoreType.DMA((n * per,)), pltpu.SemaphoreType.DMA((n * per,)), pltpu.SemaphoreType.DMA((n,))],
                 start, finish)


def _mesh_place():
    x, y, c = (lax.axis_index(ax) for ax in MESH_AXES)
    return x, y, c, (x, y, 1 - c), [(1 - x, y), (x, 1 - y), (1 - x, 1 - y)]


def _remote(src, dst, send_sems, recv_sems, sem, to):
    return pltpu.make_async_remote_copy(src_ref=src, dst_ref=dst, send_sem=send_sems.at[sem], recv_sem=recv_sems.at[sem],
                                        device_id=to, device_id_type=pl.DeviceIdType.MESH)


N_CHIPS = N_DEV // 2


def _pair_sum(a, other, parity, name):
    _, r, c = a.shape
    tr = _row_tile(r, c, 3)

    def body(p_ref, a_ref, o_ref, s_ref):
        s_ref[...] = (a_ref[...] + o_ref[...]).astype(BF16)

    return pl.pallas_call(
        body, name=name,
        grid_spec=pltpu.PrefetchScalarGridSpec(
            num_scalar_prefetch=1, grid=(N_CHIPS, r // tr),
            in_specs=[pl.BlockSpec((1, tr, c), lambda q, i, p: (2 * q + p[0], i, 0)),
                      pl.BlockSpec((1, tr, c), lambda q, i, p: (q, i, 0))],
            out_specs=pl.BlockSpec((1, tr, c), lambda q, i, p: (q, i, 0))),
        out_shape=jax.ShapeDtypeStruct((N_CHIPS, r, c), BF16),
        compiler_params=_params(2),
    )(parity, a, other)


def _adam_math(w, g, m, v):
    m = ADAM_B1 * m + (1.0 - ADAM_B1) * g
    v = ADAM_B2 * v + (1.0 - ADAM_B2) * (g * g)
    m_hat = m / (1.0 - ADAM_B1 ** ADAM_STEP)
    v_hat = v / (1.0 - ADAM_B2 ** ADAM_STEP)
    delta = -ADAM_LR * (m_hat / (jnp.sqrt(v_hat) + ADAM_EPS) + ADAM_WD * w)
    return delta, m, v


def _sum_devices(r_ref):
    acc = r_ref[0].astype(F32)
    for p in range(1, r_ref.shape[0]):
        acc = acc + r_ref[p].astype(F32)
    return acc


def _row_tile(rows, cols, n_bufs):
    budget = 24 * 1024 * 1024 // (n_bufs * 2 * cols * 4)
    t = rows
    while t > budget and t % 2 == 0 and (t // 2) % SUBLANES == 0:
        t //= 2
    return t


def _reduce_adam(recvs, w, m, v, name, comms=None):
    nl, r, c = w.shape
    n_part = recvs[0].shape[0]
    tr = _row_tile(r, c, n_part * nl + 7)
    nt = r // tr

    def body(*refs):
        r_refs = refs[:nl]
        w_ref, m_ref, v_ref, g_ref, d_ref, mo_ref, vo_ref = refs[nl:]
        layer = pl.program_id(0) // nt
        g = _sum_devices(r_refs[0])
        for ll in range(1, nl):
            g = jnp.where(layer == ll, _sum_devices(r_refs[ll]), g)
        delta, m2, v2 = _adam_math(w_ref[0], g, m_ref[0], v_ref[0])
        g_ref[0] = g
        d_ref[0] = delta
        mo_ref[0] = m2
        vo_ref[0] = v2

    def rspec(ll):
        return pl.BlockSpec((n_part, tr, c),
                            lambda i: (0, jnp.where(i // nt == ll, i % nt, jnp.where(i // nt < ll, 0, nt - 1)), 0))

    blk = pl.BlockSpec((1, tr, c), lambda i: (i // nt, i % nt, 0))
    return _call(
        body, comms, name=name, grid=(nl * nt,),
        in_specs=[rspec(ll) for ll in range(nl)] + [blk, blk, blk],
        out_specs=[blk] * 4,
        out_shape=[jax.ShapeDtypeStruct((nl, r, c), F32)] * 4,
        args=(*recvs, w, m, v))


def _tile_for(s, want):
    return min(want, s)


REPLICATED = ("norm_g", "b_ada", "rg_conv_b", "rg_w_a", "rg_b_a", "rg_w_x", "rg_b_x", "rg_lambda", "ml_conv_b",
              "ml_b_if", "ml_norm_g", "final_g")


def _small_pack(rg_conv_w, ml_conv_w, ml_w_if):
    nl = rg_conv_w.shape[0]
    wif_t = jnp.swapaxes(ml_w_if, 1, 2).reshape(nl, -1, LANES)
    return jnp.concatenate([rg_conv_w, ml_conv_w, wif_t], axis=1)


def _small_unpack(p, if_rows):
    nl = p.shape[0]
    rg_cw = p[:, 0:CONV_WIDTH]
    ml_cw = p[:, CONV_WIDTH:2 * CONV_WIDTH]
    wif = jnp.swapaxes(p[:, 2 * CONV_WIDTH:].reshape(nl, 8, if_rows), 1, 2)
    return rg_cw, ml_cw, wif


def _qkv_slots(g_qkv, nd):
    three, nh, dh, _ = g_qkv.shape
    return g_qkv.reshape(three, nh, nd, dh // nd, dh).transpose(2, 0, 1, 3, 4).reshape(nd, three * nh * (dh // nd), dh)


def _small_slots(g):
    nd = N_DEV
    cw = jnp.stack([g["rg_conv_w"], g["ml_conv_w"]]).reshape(2, CONV_WIDTH, nd, LANES).transpose(2, 0, 1, 3)
    cw = cw.reshape(nd, 2 * CONV_WIDTH, LANES)
    wif = g["wif_t"].reshape(8, nd, -1).transpose(1, 0, 2).reshape(nd, -1, LANES)
    return jnp.concatenate([cw, wif], axis=1)


def _slot(block):
    return 4 * block[0] + 2 * block[1] + block[2]


def _dma_sems(*counts):
    return [pltpu.SemaphoreType.DMA((n,)) for n in counts]


def _start_all(copies):
    for cp in copies:
        cp.start()


def _gather_ici_comm(arrs):
    n = len(arrs)

    def copies(ins, outs, sems):
        send_sems, recv_sems, local_sems = sems
        x, y, c, sibling, chips = _mesh_place()
        me = (x, y, c)
        peers = [(*chip, c) for chip in chips] + [sibling]
        local = [pltpu.make_async_copy(ins[kk], outs[kk].at[_slot(me)], local_sems.at[kk]) for kk in range(n)]
        sends = [_remote(ins[kk], outs[kk].at[_slot(me)], send_sems, recv_sems, kk * 4 + j, peer)
                 for j, peer in enumerate(peers) for kk in range(n)]
        recvs = [_remote(ins[kk], outs[kk].at[_slot(peer)], send_sems, recv_sems, kk * 4 + j, peer)
                 for j, peer in enumerate(peers) for kk in range(n)]
        return local, sends, recvs

    def start(ins, outs, sems):
        local, sends, _ = copies(ins, outs, sems)
        _start_all(sends + local)

    def finish(ins, outs, sems):
        local, sends, recvs = copies(ins, outs, sems)
        for cp in recvs:
            cp.wait_recv()
        for cp in sends:
            cp.wait_send()
        for cp in local:
            cp.wait()

    return _Comm(arrs, [jax.ShapeDtypeStruct((N_DEV,) + a.shape, a.dtype) for a in arrs], _dma_sems(4 * n, 4 * n, n),
                 start, finish)


def _gather_fwd_comm(bufs):
    n = len(bufs)

    def copies(ins, outs, sems):
        send_sems, recv_sems = sems
        _, _, c, sibling, chips = _mesh_place()
        sends = [_remote(ins[kk].at[_slot((*chip, c))], outs[kk].at[_slot((*chip, c))], send_sems, recv_sems, kk * 3 + j, sibling)
                 for j, chip in enumerate(chips) for kk in range(n)]
        recvs = [_remote(ins[kk].at[_slot((*chip, c))], outs[kk].at[_slot((*chip, 1 - c))], send_sems, recv_sems, kk * 3 + j, sibling)
                 for j, chip in enumerate(chips) for kk in range(n)]
        return sends, recvs

    def start(ins, outs, sems):
        _start_all(copies(ins, outs, sems)[0])

    def finish(ins, outs, sems):
        sends, recvs = copies(ins, outs, sems)
        for cp in recvs:
            cp.wait_recv()
        for cp in sends:
            cp.wait_send()

    return _Comm(bufs, [jax.ShapeDtypeStruct(a.shape, a.dtype) for a in bufs], _dma_sems(3 * n, 3 * n), start, finish,
                 aliases=[(i, i) for i in range(n)])


def _core_swap_comm(arrs):
    n = len(arrs)

    def copies(ins, outs, sems):
        send_sems, recv_sems = sems
        _, _, c, sibling, _ = _mesh_place()
        return [_remote(ins[kk].at[2 * q + (1 - c)], outs[kk].at[q], send_sems, recv_sems, kk * N_CHIPS + q, sibling)
                for q in range(N_CHIPS) for kk in range(n)]

    def start(ins, outs, sems):
        _start_all(copies(ins, outs, sems))

    def finish(ins, outs, sems):
        cps = copies(ins, outs, sems)
        for cp in cps:
            cp.wait_recv()
        for cp in cps:
            cp.wait_send()

    return _Comm(arrs, [jax.ShapeDtypeStruct((N_CHIPS,) + a.shape[1:], a.dtype) for a in arrs],
                 _dma_sems(N_CHIPS * n, N_CHIPS * n), start, finish)


def _chip_swap_comm(arrs):
    n = len(arrs)
    per = N_CHIPS - 1

    def copies(ins, outs, sems):
        send_sems, recv_sems, local_sems = sems
        x, y, c, _, chips = _mesh_place()
        mine = 2 * x + y
        sends = [_remote(ins[kk].at[2 * chip[0] + chip[1]], outs[kk].at[mine], send_sems, recv_sems, kk * per + j, (*chip, c))
                 for j, chip in enumerate(chips) for kk in range(n)]
        recvs = [_remote(ins[kk].at[mine], outs[kk].at[2 * chip[0] + chip[1]], send_sems, recv_sems, kk * per + j, (*chip, c))
                 for j, chip in enumerate(chips) for kk in range(n)]
        local = [pltpu.make_async_copy(ins[kk].at[mine], outs[kk].at[mine], local_sems.at[kk]) for kk in range(n)]
        return local, sends, recvs

    def start(ins, outs, sems):
        local, sends, _ = copies(ins, outs, sems)
        _start_all(sends + local)

    def finish(ins, outs, sems):
        local, sends, recvs = copies(ins, outs, sems)
        for cp in recvs:
            cp.wait_recv()
        for cp in sends:
            cp.wait_send()
        for cp in local:
            cp.wait()

    return _Comm(arrs, [jax.ShapeDtypeStruct(a.shape, a.dtype) for a in arrs], _dma_sems(per * n, per * n, n), start, finish)


def _ada_mod(c_all, w_ada, b_cols, comms=None):
    nl, d, w = w_ada.shape

    def body(c_ref, w_ref, b_ref, m_ref, ca_ref):
        sub = _iota((SUBLANES, d), 0)
        cv = jnp.zeros((SUBLANES, d), F32)
        for b in range(N_DEV):
            cv = jnp.where(sub == b, c_ref[b], cv)
        ca = cv * _sigmoid(cv)
        ca_ref[...] = ca
        m_ref[...] = jnp.zeros_like(m_ref)
        for l in range(nl):
            ml = _mm_hi(ca, w_ref[l]) + b_ref[l:l + 1, :]
            for b in range(N_DEV):
                m_ref[b, l:l + 1, :] = _row(ml, b)

    return _call(
        body, comms, name="adaln_mod_columns", grid=(1,),
        in_specs=[_full(c_all.shape), _full(w_ada.shape), _full(b_cols.shape)],
        out_specs=[_full((N_DEV, SUBLANES, w)), _full((SUBLANES, d))],
        out_shape=[jax.ShapeDtypeStruct((N_DEV, SUBLANES, w), F32), jax.ShapeDtypeStruct((SUBLANES, d), F32)],
        args=(c_all, w_ada, b_cols))


def _ada_grad_adam(cact_t, dmods, w, m, v, comms=None):
    nl, d, wd = w.shape
    tr = _row_tile(d, wd, 8)
    nt = d // tr

    def body(c_ref, dm_ref, w_ref, m_ref, v_ref, g_ref, d_ref, mo_ref, vo_ref):
        cv = c_ref[...]
        dm = dm_ref[0]
        g = _col(cv, 0) * _row(dm, 0)
        for b in range(1, N_DEV):
            g = g + _col(cv, b) * _row(dm, b)
        delta, m2, v2 = _adam_math(w_ref[0], g, m_ref[0], v_ref[0])
        g_ref[0] = g
        d_ref[0] = delta
        mo_ref[0] = m2
        vo_ref[0] = v2

    blk = pl.BlockSpec((1, tr, wd), lambda i: (i // nt, i % nt, 0))
    return _call(
        body, comms, name="adaln_grad_adam", grid=(nl * nt,),
        in_specs=[pl.BlockSpec((tr, N_DEV), lambda i: (i % nt, 0)), pl.BlockSpec((1, N_DEV, wd), lambda i: (i // nt, 0, 0)),
                  blk, blk, blk],
        out_specs=[blk] * 4, out_shape=[jax.ShapeDtypeStruct((nl, d, wd), F32)] * 4,
        args=(cact_t, dmods, w, m, v))


REP_ROWS = ("norm_g", "dshift", "dscale", "dgate", "rg_conv_b", "rg_b_a", "rg_b_x", "rg_lambda", "ml_conv_b", "ml_norm_g",
            "ml_b_if")


def _sum_parts(recvs, name):
    def body(*refs):
        for r_ref, o_ref in zip(refs[:len(recvs)], refs[len(recvs):]):
            o_ref[...] = _sum_devices(r_ref).astype(o_ref.dtype)

    return pl.pallas_call(
        body, name=name, grid=(1,),
        in_specs=[_full(r.shape) for r in recvs], out_specs=[_full(r.shape[1:]) for r in recvs],
        out_shape=[jax.ShapeDtypeStruct(r.shape[1:], r.dtype) for r in recvs], compiler_params=_params(1),
    )(*recvs)


def _adam_replicated(vp, mp, params, nl):
    d = vp.shape[2]
    nr = len(REP_ROWS)
    names = list(params)
    mat_shape = params["rg_w_a"][0].shape[1:]
    mat_rows = mp.shape[0] // (2 * nl)

    def pieces(name):
        if name == "final_g":
            return [(lambda vp_ref, mp_ref: vp_ref[nl * nr:nl * nr + 1, :], (slice(0, 1), slice(None)))]
        out = []
        for l in range(nl):
            if name in ("rg_w_a", "rg_w_x"):
                at = (2 * l + (name == "rg_w_x")) * mat_rows
                out.append((lambda vp_ref, mp_ref, at=at: mp_ref[at:at + mat_rows, :].astype(F32).reshape(mat_shape), l))
            elif name == "b_ada":
                for j in range(3):
                    r = l * nr + 1 + j
                    out.append((lambda vp_ref, mp_ref, r=r: vp_ref[r:r + 1, :], (slice(l, l + 1), slice(j * d, (j + 1) * d))))
            else:
                r = l * nr + REP_ROWS.index(name)
                cols = slice(0, LANES) if name == "ml_b_if" else slice(None)
                out.append((lambda vp_ref, mp_ref, r=r, cols=cols: vp_ref[r:r + 1, cols], (slice(l, l + 1), slice(None))))
        return out

    def body(*refs):
        parts_ref, mp_ref, vp_ref = refs[0], refs[1], refs[-1]
        ins, outs = refs[2:2 + 3 * len(names)], refs[2 + 3 * len(names):-1]
        vp_ref[...] = _sum_devices(parts_ref)
        for pi, name in enumerate(names):
            w_ref, m_ref, v_ref = ins[3 * pi:3 * pi + 3]
            g_ref, d_ref, mo_ref, vo_ref = outs[4 * pi:4 * pi + 4]
            for get, idx in pieces(name):
                g = get(vp_ref, mp_ref)
                delta, m2, v2 = _adam_math(w_ref[idx], g, m_ref[idx], v_ref[idx])
                g_ref[idx] = g
                d_ref[idx] = delta
                mo_ref[idx] = m2
                vo_ref[idx] = v2

    flat = [a for name in names for a in params[name]]
    out_shape = [jax.ShapeDtypeStruct(params[name][0].shape, F32) for name in names for _ in range(4)]
    out_shape.append(jax.ShapeDtypeStruct(vp.shape[1:], F32))
    res = pl.pallas_call(
        body, name="adam_replicated", grid=(1,),
        in_specs=[_full(vp.shape), _full(mp.shape)] + [_full(a.shape) for a in flat],
        out_specs=[_full(o.shape) for o in out_shape], out_shape=out_shape, compiler_params=_params(1),
    )(vp, mp, *flat)
    return {name: res[4 * pi:4 * pi + 4] for pi, name in enumerate(names)}, res[-1]


class _Plan:
    def __init__(self):
        self.hosted, self.after = {}, {}

    def host(self, key, comm, then=None):
        self.hosted.setdefault(key, []).append(comm)
        if then is not None:
            self.after.setdefault(key, []).append(then)

    def comms(self, key):
        return self.hosted.pop(key, None)

    def done(self, key):
        for fn in self.after.pop(key, []):
            fn()

    def flush(self):
        while self.hosted:
            key = next(iter(self.hosted))
            _call(lambda: None, self.comms(key), name="exchange_after_%s_%d" % key, grid=(1,), in_specs=[], out_specs=[],
                  out_shape=[], args=())
            self.done(key)


VEC_TABLE = ("norm_g", "rg_conv_b", "rg_b_a", "rg_b_x", "rg_lambda", "ml_conv_b", "ml_norm_g")


def _vec_table(rep):
    rows = [rep[n] for n in VEC_TABLE]
    return jnp.stack(rows + [jnp.zeros_like(rows[0])] * (SUBLANES - len(rows)), axis=1)


def _layer_fwd(l, xl, mod3, wl, rep, plan, head=None):
    s, d = xl.shape
    t_big, t_mid = _tile_for(s, 512), _tile_for(s, 256)
    nh_ml = rep["ml_b_if"].shape[1] // 2
    vec = lambda name: _vec(rep["vecs"], l, VEC_TABLE.index(name))
    shift, scale, gate = (_vec(mod3, l, kk) for kk in range(3))
    hosted = lambda name: plan.comms((name, l)) if plan else None
    done = lambda name: plan.done((name, l)) if plan else None
    u, hbf = _in_fwd(xl, vec("norm_g"), scale, shift, wl["w_in_g"], 0, t_big, hosted("in_proj_fwd"))
    done("in_proj_fwd")
    h_rg, y_rg, *rg_gates = _rg_fwd(u, d, wl["rg_conv_w"], vec("rg_conv_b"), rep["rg_w_a_bf"][l], vec("rg_b_a"),
                                    rep["rg_w_x_bf"][l], vec("rg_b_x"), vec("rg_lambda"), t_mid, hosted("rglru_fwd"))
    done("rglru_fwd")
    q, k, v, gcol, pre = _ml_pre(u, d, wl["ml_conv_w"], vec("ml_conv_b"), wl["w_qkv"][0], wl["w_qkv"][1],
                                 wl["w_qkv"][2], wl["wif_pad"], wl["bif_pad"], t_mid, hosted("mlstm_proj_fwd"))
    done("mlstm_proj_fwd")
    grow = gcol[:, 0:16].T
    cell, y_ml, cs, ns, ms, mt = _ml_cell_fwd(q, k, v, gcol, grow, u, vec("ml_norm_g"), nh_ml, hosted("mlstm_cell_fwd"))
    done("mlstm_cell_fwd")
    res = _out_fwd(xl, y_rg, y_ml, gate, wl["w_out_g"], 0, t_big, hosted("out_proj_fwd"), head)
    done("out_proj_fwd")
    x_new, y = (res[0], res[1]) if head is None else (tuple(res[1:]), res[0])
    saved = dict(x=xl, u=u, hbf=hbf, h_rg=h_rg, y_rg=y_rg, q=q, k=k, v=v, gcol=gcol, grow=grow, cell=cell, y_ml=y_ml,
                 cs=cs, ns=ns, ms=ms, mt=mt, y=y, scale=scale, gate=gate, rg_gates=rg_gates, pre=pre)
    return x_new, saved


def _layer_bwd(l, dx, sv, wl, rep, plan, grads=None, split_last=False):
    s, d = dx.shape
    t_big, t_mid = _tile_for(s, 512), _tile_for(s, 256)
    nh_ml = rep["ml_b_if"].shape[1] // 2
    nd, _, _, w_cols = wl["w_in_g"].shape
    grads = {} if grads is None else grads
    vec = lambda name: _vec(rep["vecs"], l, VEC_TABLE.index(name))
    hosted = lambda name: plan.comms((name, l)) if plan else None
    done = lambda name: plan.done((name, l)) if plan else None
    dy_rg, dy_ml, gw_out, dgate = _out_bwd(dx, sv["gate"], sv["y"], sv["y_rg"], sv["y_ml"], wl["w_out_g"], 0, t_big,
                                           hosted("out_proj_bwd"))
    grads.update(w_out=gw_out)
    done("out_proj_bwd")
    dq, dk, dv, dgates, d_mlo, d_mlz, g_mlng = _ml_cell_bwd(
        dy_ml, sv["u"], sv["cell"], sv["q"], sv["k"], sv["v"], sv["gcol"], sv["grow"], sv["mt"], sv["cs"], sv["ns"],
        sv["ms"], vec("ml_norm_g"), nh_ml, hosted("mlstm_cell_bwd"))
    done("mlstm_cell_bwd")
    d_mlx, g_wq, g_wk, g_wv, g_wift, g_bif, g_mlcw, g_mlcb = _ml_pre_bwd(
        dq, dk, dv, dgates, sv["gcol"], sv["u"], sv["pre"], sv["q"], sv["k"], sv["v"], wl["ml_conv_w"],
        wl["w_qkv"][0], wl["w_qkv"][1], wl["w_qkv"][2], wl["wift_pad"], t_mid, hosted("mlstm_proj_bwd"))
    done("mlstm_proj_bwd")
    d_rgx, d_rgz, g_wa, g_wx, g_ba, g_bx, g_lam, g_rgcw, g_rgcb = _rg_bwd(
        dy_rg, sv["u"], sv["h_rg"], sv["rg_gates"], wl["rg_conv_w"], rep["rg_w_a_bf"][l], rep["rg_w_x_bf"][l],
        vec("rg_lambda"), t_mid, hosted("rglru_bwd"))
    grads.update(w_qkv=jnp.stack([g_wq, g_wk, g_wv]), rg_conv_w=g_rgcw[0:CONV_WIDTH], ml_conv_w=g_mlcw[0:CONV_WIDTH],
                 wif_t=g_wift[0:8], rg_w_a=g_wa, rg_w_x=g_wx)
    acc = dict(dgate=dgate, rg_conv_b=g_rgcb, rg_b_a=g_ba, rg_b_x=g_bx, rg_lambda=g_lam, ml_conv_b=g_mlcb,
               ml_b_if=g_bif, ml_norm_g=g_mlng)
    done("rglru_bwd")
    pieces = [d_rgx, d_rgz, d_mlx, d_mlo, d_mlz]
    grads.update(w_in=_in_bwd_w(pieces, sv["hbf"], w_cols, tuple(range(nd)), _tile_for(s, 1024), hosted("in_proj_bwd_w")))
    done("in_proj_bwd_w")
    n_tiles = s // t_mid
    counts = [1, n_tiles - 1] if split_last and n_tiles >= 8 else [n_tiles]
    in_args = (pieces, sv["x"], dx, vec("norm_g"), sv["scale"], wl["w_in_g"], 0, t_mid)
    res, at = None, 0
    for key, count in zip(("in_proj_bwd_x", "in_proj_bwd_x_rest"), counts):
        res = _in_bwd(*in_args, hosted(key), (at, count), res)
        done(key)
        at += count
    dx, dscale, dshift, g_ng = res
    acc.update(norm_g=g_ng, dshift=dshift, dscale=dscale)
    grads.update(acc=acc, dmod=jnp.concatenate([dshift[0:1], dscale[0:1], dgate[0:1]], axis=1))
    return dx, grads


def _full_qkv(qkv_g, d):
    nd, _, rows3, dh = qkv_g.shape
    nh = d // dh
    rsh = rows3 // (3 * nh)
    return qkv_g.reshape(nd, 3, nh, rsh, dh).transpose(1, 2, 0, 3, 4).reshape(3, nh, nd * rsh, dh)


def _small_weights(small, l, ml_b_if):
    nd = small.shape[0]
    sm = small[:, l]
    cw = sm[:, 0:2 * CONV_WIDTH].reshape(nd, 2, CONV_WIDTH, LANES).transpose(1, 2, 0, 3).reshape(2, CONV_WIDTH, nd * LANES)
    if_rows = (sm.shape[1] - 2 * CONV_WIDTH) * LANES // 8
    wif_t = sm[:, 2 * CONV_WIDTH:].reshape(nd, 8, if_rows).transpose(1, 0, 2).reshape(8, nd * if_rows)
    wift_pad = jnp.pad(wif_t, ((0, LANES - 8), (0, 0))).astype(BF16)
    return dict(rg_conv_w=cw[0], ml_conv_w=cw[1], wift_pad=wift_pad, wif_pad=wift_pad.T,
                bif_pad=jnp.pad(ml_b_if[l], (0, LANES - 8)).reshape(1, LANES))


def kernel(x, c, norm_g, w_ada, b_ada, w_in, rg_conv_w, rg_conv_b, rg_w_a, rg_b_a, rg_w_x, rg_b_x, rg_lambda, ml_conv_w, ml_conv_b, ml_w_q, ml_w_k, ml_w_v, ml_w_if, ml_b_if, ml_norm_g, w_out, final_g, loss_target, m_norm_g, m_w_ada, m_b_ada, m_w_in, m_rg_conv_w, m_rg_conv_b, m_rg_w_a, m_rg_b_a, m_rg_w_x, m_rg_b_x, m_rg_lambda, m_ml_conv_w, m_ml_conv_b, m_ml_w_q, m_ml_w_k, m_ml_w_v, m_ml_w_if, m_ml_b_if, m_ml_norm_g, m_w_out, m_final_g, v_norm_g, v_w_ada, v_b_ada, v_w_in, v_rg_conv_w, v_rg_conv_b, v_rg_w_a, v_rg_b_a, v_rg_w_x, v_rg_b_x, v_rg_lambda, v_ml_conv_w, v_ml_conv_b, v_ml_w_q, v_ml_w_k, v_ml_w_v, v_ml_w_if, v_ml_b_if, v_ml_norm_g, v_w_out, v_final_g):
    given = dict(locals())
    nl = w_in.shape[0]
    d = x.shape[2]
    rep = {n: given[n] for n in REPLICATED}
    rep.update(rg_w_a_bf=rg_w_a.astype(BF16), rg_w_x_bf=rg_w_x.astype(BF16))
    bf = lambda a: a.astype(BF16)

    def qkv_shard(prefix):
        return jnp.stack([given[prefix + "ml_w_q"], given[prefix + "ml_w_k"], given[prefix + "ml_w_v"]], axis=1).reshape(
            nl, -1, ml_w_q.shape[-1])

    def small_shard(prefix):
        return _small_pack(given[prefix + "rg_conv_w"], given[prefix + "ml_conv_w"], given[prefix + "ml_w_if"])

    plan = _Plan()
    qkv = qkv_shard("")
    first_ici = _gather_ici_comm([bf(w_in[0:1]), small_shard("")])
    condition = _exchange_comm([jnp.broadcast_to(c, (SUBLANES, d))], True)
    _run_comms([first_ici, condition], "gather_first")
    first_fwd = _gather_fwd_comm(first_ici.results)
    wcols = w_ada.shape[2]
    me = 4 * lax.axis_index("x") + 2 * lax.axis_index("y") + lax.axis_index("c")
    b_cols = jnp.pad(lax.dynamic_slice_in_dim(b_ada, me * wcols, wcols, axis=1), ((0, SUBLANES - nl), (0, 0)))
    mod_cols, cact_all = _ada_mod(condition.results[0], w_ada, b_cols, [first_fwd])
    w_in_first, small = first_fwd.results
    wl = [_small_weights(small, l, ml_b_if) for l in range(nl)]
    wl[0]["w_in_g"] = w_in_first

    def gather_behind(arrs, ici_host, fwd_host, then):
        ici = _gather_ici_comm(arrs)

        def pass_on():
            fwd = _gather_fwd_comm(ici.results)
            plan.host(fwd_host, fwd, lambda: then(fwd.results))

        plan.host(ici_host, ici, pass_on)

    def got_out(l):
        return lambda r: wl[l].update(w_out_g=r[0], w_qkv=_full_qkv(r[1], d))

    gather_behind([bf(w_out[0:1]), bf(qkv[0:1])], ("in_proj_fwd", 0), ("rglru_fwd", 0), got_out(0))
    for l in range(1, nl):
        gather_behind([bf(w_in[l:l + 1])], ("rglru_fwd", l - 1), ("mlstm_cell_fwd", l - 1),
                      lambda r, l=l: wl[l].update(w_in_g=r[0]))
        gather_behind([bf(w_out[l:l + 1]), bf(qkv[l:l + 1])], ("mlstm_cell_fwd", l - 1), ("out_proj_fwd", l - 1), got_out(l))

    mod_blocks = _exchange([mod_cols], False, "scatter_modulation")[0]
    mod3 = mod_blocks[:, 0:nl].transpose(1, 0, 2).reshape(nl, 3, d)
    rep["vecs"] = _vec_table(rep)
    saved, xl = [], x[0]
    for l in range(nl):
        head = (final_g.reshape(1, -1), loss_target[0]) if l == nl - 1 else None
        xl, sv = _layer_fwd(l, xl, mod3, wl[l], rep, plan, head)
        saved.append(sv)
    grad_x, loss_p, g_final = xl

    keys = ("w_in", "w_out", "w_qkv", "small")
    parity = lax.axis_index("c").astype(jnp.int32).reshape(1)
    grads, recv = [None] * nl, [None] * nl

    def small_parts(g):
        return [bf(_qkv_slots(g["w_qkv"], N_DEV)), bf(_small_slots(g))]

    def reduce_behind(l, host_layer):
        parts = [grads[l]["w_in"], grads[l]["w_out"]]
        swap = _core_swap_comm(parts)
        direct = _exchange_comm(small_parts(grads[l]), False)

        def summed():
            sums = [_pair_sum(a, o, parity, "pair_sum_%s_layer%d" % (key, l)) for key, a, o in zip(keys, parts, swap.results)]
            big = _chip_swap_comm([sums[0]])
            rest = _chip_swap_comm([sums[1]])
            plan.host(("mlstm_cell_bwd", host_layer), big)
            plan.host(("rglru_bwd", host_layer), rest,
                      lambda: recv.__setitem__(l, big.results + rest.results + direct.results))

        plan.host(("out_proj_bwd", host_layer), swap, summed)
        plan.host(("mlstm_proj_bwd", host_layer), direct)

    first, own = {}, {}

    def reduce_own(names, parts_fn, ready_key, swap_key, chip_key):
        def go():
            parts = parts_fn()
            swap = _core_swap_comm(parts)

            def summed():
                sums = [_pair_sum(a, o, parity, "pair_sum_%s_layer0" % n) for n, a, o in zip(names, parts, swap.results)]
                chip = _chip_swap_comm(sums)
                plan.host(chip_key, chip, lambda: own.update(zip(names, chip.results)))

            plan.host(swap_key, swap, summed)

        plan.after.setdefault(ready_key, []).append(go)

    reduce_own(["w_out"], lambda: [first["w_out"]], ("out_proj_bwd", 0), ("mlstm_cell_bwd", 0), ("mlstm_proj_bwd", 0))
    reduce_own(["w_in"], lambda: [first["w_in"]], ("in_proj_bwd_w", 0), ("in_proj_bwd_x", 0), ("in_proj_bwd_x_rest", 0))

    def small_own():
        direct = _exchange_comm(small_parts(first), False)
        plan.host(("in_proj_bwd_w", 0), direct, lambda: own.update(w_qkv=direct.results[0], small=direct.results[1]))

    plan.after.setdefault(("rglru_bwd", 0), []).append(small_own)

    matrices = {}

    def reduce_matrices():
        layers = [grads[l] if l > 0 else first for l in range(nl)]
        mp = jnp.stack([jnp.stack([g["rg_w_a"], g["rg_w_x"]]) for g in layers]).reshape(N_DEV, -1, LANES).astype(BF16)
        scatter = _exchange_comm([mp], False)

        def summed():
            matrices.update(sums=_sum_parts(scatter.results, "sum_replicated_matrices"))

        plan.host(("in_proj_bwd_w", 0), scatter, summed)

    plan.after.setdefault(("rglru_bwd", 0), []).append(reduce_matrices)

    for l in reversed(range(nl)):
        if l > 0:
            grad_x, grads[l] = _layer_bwd(l, grad_x, saved[l], wl[l], rep, plan)
            reduce_behind(l, l - 1)
        else:
            grad_x, grads[l] = _layer_bwd(l, grad_x, saved[l], wl[l], rep, plan, first, True)
    plan.flush()
    recv[0] = [own[key] for key in keys]

    shard = {p: dict(w_in=given[p + "w_in"], w_out=given[p + "w_out"], w_qkv=qkv_shard(p), small=small_shard(p))
             for p in ("", "m_", "v_")}
    res = {}
    for ki, key in enumerate(keys):
        res[key] = _reduce_adam([recv[l][ki] for l in range(nl)], shard[""][key], shard["m_"][key], shard["v_"][key],
                                "reduce_adam_" + key)

    dmods = jnp.concatenate([grads[l]["dmod"] for l in range(nl)], axis=0)
    dmod_blocks = jnp.pad(dmods.reshape(nl, N_DEV, wcols).transpose(1, 0, 2), ((0, 0), (0, SUBLANES - nl), (0, 0)))
    widen = lambda a: jnp.pad(a, ((0, 0), (0, d - a.shape[1])))
    rows = [widen(grads[l]["acc"][n][0:1]) for l in range(nl) for n in REP_ROWS] + [g_final[0:1], widen(loss_p[0:1])]
    vp = jnp.concatenate(rows + [jnp.zeros(((-len(rows)) % SUBLANES, d), F32)], axis=0)
    (dmod_recv,), (vp_all, mp_all) = _run_comms(
        [_exchange_comm([dmod_blocks], False), _exchange_comm([vp, *matrices["sums"]], True)], "tail_exchange")
    res["w_ada"] = _ada_grad_adam(cact_all.T, dmod_recv[:, 0:nl].transpose(1, 0, 2), w_ada, m_w_ada, v_w_ada)
    mp_r = mp_all.reshape(-1, LANES)
    lanes = lambda a: jnp.pad(a, ((0, 0), (0, LANES - a.shape[1])))
    shaped = dict(ml_b_if=lanes, final_g=lambda a: a.reshape(1, d))
    names = [n for n in REPLICATED if n != "b_ada"] + ["b_ada"]
    rep_res, vp_r = _adam_replicated(vp_all, mp_r, {n: tuple(shaped.get(n, lambda a: a)(given[p + n]) for p in ("", "m_", "v_"))
                                                    for n in names}, nl)
    unshaped = dict(ml_b_if=lambda a: a[:, 0:ml_b_if.shape[1]], final_g=lambda a: a.reshape(d))
    rep_out = [{n: unshaped.get(n, lambda a: a)(rep_res[n][kind]) for n in names} for kind in range(4)]
    loss = vp_r[nl * len(REP_ROWS) + 1, 0]

    if_rows = ml_w_if.shape[1]
    order = ("norm_g", "w_ada", "b_ada", "w_in", "rg_conv_w", "rg_conv_b", "rg_w_a", "rg_b_a", "rg_w_x", "rg_b_x",
             "rg_lambda", "ml_conv_w", "ml_conv_b", "ml_w_q", "ml_w_k", "ml_w_v", "ml_w_if", "ml_b_if", "ml_norm_g",
             "w_out", "final_g")
    outs = [loss, grad_x[None]]
    for kind in range(4):
        qkv_k = res["w_qkv"][kind].reshape((nl, 3) + ml_w_q.shape[1:])
        rg_cw, ml_cw, wif = _small_unpack(res["small"][kind], if_rows)
        sharded = dict(w_ada=res["w_ada"][kind], w_in=res["w_in"][kind], w_out=res["w_out"][kind], ml_w_q=qkv_k[:, 0],
                       ml_w_k=qkv_k[:, 1], ml_w_v=qkv_k[:, 2], rg_conv_w=rg_cw, ml_conv_w=ml_cw, ml_w_if=wif)
        for n in order:
            outs.append(sharded[n] if n in sharded else rep_out[kind][n])
    return tuple(outs)
```

```python
import functools

import jax
import jax.numpy as jnp
from jax import lax
from jax.experimental import pallas as pl
from jax.experimental.pallas import tpu as pltpu

F32 = jnp.float32
BF16 = jnp.bfloat16
MESH_AXES = ("x", "y", "c")
N_DEV = 8
EPS = 1e-6
RG_C = 8.0
ML_CHUNK = 128
CONV_WIDTH = 4
ADAM_LR = 0.001
ADAM_B1 = 0.9
ADAM_B2 = 0.999
ADAM_EPS = 1e-08
ADAM_WD = 0.01
ADAM_STEP = 10
NEG_BIG = -1e30
LANES = 128
SUBLANES = 8
VMEM_LIMIT = 56 * 1024 * 1024
HI = lax.Precision.HIGHEST


def _params(n_grid):
    return pltpu.CompilerParams(dimension_semantics=("arbitrary",) * n_grid, vmem_limit_bytes=VMEM_LIMIT)


def _mm(a, b):
    return jnp.dot(a.astype(BF16), b.astype(BF16), preferred_element_type=F32)


def _mm_nt(a, b):
    return lax.dot_general(a.astype(BF16), b.astype(BF16), (((1,), (1,)), ((), ())), preferred_element_type=F32)


def _mm_tn(a, b):
    return lax.dot_general(a.astype(BF16), b.astype(BF16), (((0,), (0,)), ((), ())), preferred_element_type=F32)


def _mm_hi(a, b):
    return jnp.dot(a, b, precision=HI, preferred_element_type=F32)


def _sigmoid(x):
    return 1.0 / (1.0 + jnp.exp(-x))


def _softplus(x):
    return jnp.maximum(x, 0.0) + jnp.log(1.0 + jnp.exp(-jnp.abs(x)))


def _neg_expm1(x):
    poly = -x * (1.0 + x * (0.5 + x * (1.0 / 6.0 + x * (1.0 / 24.0 + x * (1.0 / 120.0)))))
    return jnp.where(jnp.abs(x) < 0.05, poly, 1.0 - jnp.exp(x))


def _iota(shape, dim):
    return lax.broadcasted_iota(jnp.int32, shape, dim)


def _colsum(x):
    return jnp.sum(x, axis=0, keepdims=True)


def _rowsum(x):
    return jnp.sum(x, axis=1, keepdims=True)


def _col(x, j):
    return _rowsum(jnp.where(_iota(x.shape, 1) == j, x, 0.0))


def _row(x, j):
    return _colsum(jnp.where(_iota(x.shape, 0) == j, x, 0.0))


def _shift_down(x, j, prev8):
    if j == 0:
        return x
    t = x.shape[0]
    main = jnp.where(_iota(x.shape, 0) >= j, pltpu.roll(x, j, 0), 0.0)
    fix = jnp.where(_iota(prev8.shape, 0) < j, pltpu.roll(prev8, j, 0), 0.0)
    return jnp.concatenate([main[0:SUBLANES] + fix, main[SUBLANES:t]], axis=0)


def _shift_up(x, j, next8):
    if j == 0:
        return x
    t = x.shape[0]
    main = jnp.where(_iota(x.shape, 0) < t - j, pltpu.roll(x, t - j, 0), 0.0)
    fix = jnp.where(_iota(next8.shape, 0) >= SUBLANES - j, pltpu.roll(next8, SUBLANES - j, 0), 0.0)
    return jnp.concatenate([main[0:t - SUBLANES], main[t - SUBLANES:t] + fix], axis=0)


def _conv(x, prev8, w_ref):
    y = w_ref[CONV_WIDTH - 1:CONV_WIDTH, :] * x
    for j in range(1, CONV_WIDTH):
        y = y + w_ref[CONV_WIDTH - 1 - j:CONV_WIDTH - j, :] * _shift_down(x, j, prev8)
    return y


def _conv_bwd(dy, x, next8, w_ref, gw_ref):
    dx = None
    for j in range(CONV_WIDTH):
        k = CONV_WIDTH - 1 - j
        up = _shift_up(dy, j, next8)
        gw_ref[k:k + 1, :] += _colsum(up * x)
        term = w_ref[k:k + 1, :] * up
        dx = term if dx is None else dx + term
    return dx


def _scan_into(a, b, carry, out_ref, reverse):
    t, c = a.shape
    groups = t // SUBLANES
    a3 = a.reshape(groups, SUBLANES, c)
    b3 = b.reshape(groups, SUBLANES, c)
    sub = _iota(a3.shape, 1)
    for step in (1, 2, 4):
        keep = sub < SUBLANES - step if reverse else sub >= step
        shift = SUBLANES - step if reverse else step
        a_s = jnp.where(keep, pltpu.roll(a3, shift, 1), 1.0)
        b_s = jnp.where(keep, pltpu.roll(b3, shift, 1), 0.0)
        b3 = a3 * b_s + b3
        a3 = a3 * a_s
    for g in (reversed(range(groups)) if reverse else range(groups)):
        rows = slice(g * SUBLANES, (g + 1) * SUBLANES)
        out_ref[rows, :] = b3[g] + a3[g] * carry
        edge = g * SUBLANES if reverse else (g + 1) * SUBLANES - 1
        carry = out_ref[edge:edge + 1, :]


def _blockdiag(x, w_ref, transpose_w=False):
    nh, dh, _ = w_ref.shape
    outs = []
    for h in range(nh):
        xs = x[:, h * dh:(h + 1) * dh]
        outs.append(_mm_nt(xs, w_ref[h]) if transpose_w else _mm(xs, w_ref[h]))
    return jnp.concatenate(outs, axis=1)


def _rg_gates(xc, wa_ref, ba_ref, wx_ref, bx_ref, lam_ref):
    r = _sigmoid(_blockdiag(xc, wa_ref) + ba_ref[...])
    ig = _sigmoid(_blockdiag(xc, wx_ref) + bx_ref[...])
    sp = _softplus(-lam_ref[...])
    log_a = -RG_C * r * sp
    a = jnp.exp(log_a)
    beta = jnp.sqrt(_neg_expm1(2.0 * log_a))
    return r, ig, sp, a, beta


def _bcast8(row):
    return jnp.broadcast_to(row, (SUBLANES, row.shape[1]))


def _full(shape):
    nd = len(shape)
    return pl.BlockSpec(shape, lambda *_: (0,) * nd)


class _Comm:
    def __init__(self, arrays, out_shapes, sems, start, finish, aliases=()):
        self.arrays, self.out_shapes, self.sems = list(arrays), list(out_shapes), list(sems)
        self.start, self.finish, self.aliases = start, finish, tuple(aliases)
        self.results = None


class _RowOf:
    def __init__(self, ref, k):
        self.ref, self.k = ref, k

    def __getitem__(self, idx):
        cols = slice(None) if idx is Ellipsis else idx[1]
        return self.ref[0, self.k:self.k + 1, cols]


class _PartOf:
    def __init__(self, ref, rows=None, cols=None, lead=None):
        self.ref, self.rows, self.cols, self.lead = ref, rows, cols, lead
        if rows is not None:
            self.shape = (rows.stop - rows.start,) + tuple(ref.shape[1:])

    def _at(self, idx):
        if self.lead is not None:
            return (self.lead,) + tuple(idx[1:])
        if self.cols is not None:
            return (slice(None), self.cols)
        return (self.rows, slice(None) if idx is Ellipsis else idx[1])

    def __getitem__(self, idx):
        return self.ref[self._at(idx)]

    def __setitem__(self, idx, value):
        self.ref[self._at(idx)] = value


def _vec(table, layer, k):
    return ("row", table, layer, k)


def _is_row(arg):
    return isinstance(arg, tuple) and len(arg) == 4 and arg[0] == "row"


def _call(body, comms, *, name, grid, in_specs, out_specs, out_shape, args, scratch_shapes=(), aliases=None):
    comms = [cm for cm in (comms or []) if cm is not None]
    rows = {i: a[3] for i, a in enumerate(args) if _is_row(a)}
    in_specs = [pl.BlockSpec((1,) + a[1].shape[1:], functools.partial(lambda layer, *_: (layer, 0, 0), a[2]))
                if _is_row(a) else sp for a, sp in zip(args, in_specs)]
    args = tuple(a[1] if _is_row(a) else a for a in args)
    n_in, n_out, n_sc = len(args), len(out_shape), len(scratch_shapes)
    c_arrays = [a for cm in comms for a in cm.arrays]
    c_outs = [o for cm in comms for o in cm.out_shapes]
    c_sems = [sm for cm in comms for sm in cm.sems]
    aliases, a_at, o_at = dict(aliases or {}), n_in, n_out
    for cm in comms:
        for (i, j) in cm.aliases:
            aliases[a_at + i] = o_at + j
        a_at += len(cm.arrays)
        o_at += len(cm.out_shapes)

    def wrapped(*refs):
        ins, c_in = refs[:n_in], refs[n_in:n_in + len(c_arrays)]
        ins = [_RowOf(r, rows[i]) if i in rows else r for i, r in enumerate(ins)]
        at = n_in + len(c_arrays)
        outs, c_out = refs[at:at + n_out], refs[at + n_out:at + n_out + len(c_outs)]
        at += n_out + len(c_outs)
        scr, sems = refs[at:at + n_sc], refs[at + n_sc:]
        views, ia, io, isem = [], 0, 0, 0
        for cm in comms:
            views.append((c_in[ia:ia + len(cm.arrays)], c_out[io:io + len(cm.out_shapes)], sems[isem:isem + len(cm.sems)]))
            ia, io, isem = ia + len(cm.arrays), io + len(cm.out_shapes), isem + len(cm.sems)
        if comms:
            @pl.when(pl.program_id(0) == 0)
            def _():
                for cm, view in zip(comms, views):
                    cm.start(*view)

        body(*ins, *outs, *scr)
        if comms:
            @pl.when(pl.program_id(0) == grid[0] - 1)
            def _():
                for cm, view in zip(comms, views):
                    cm.finish(*view)

    hbm = pl.BlockSpec(memory_space=pl.ANY)
    res = pl.pallas_call(
        wrapped, name=name, grid=grid,
        in_specs=list(in_specs) + [hbm] * len(c_arrays), out_specs=list(out_specs) + [hbm] * len(c_outs),
        out_shape=list(out_shape) + c_outs, scratch_shapes=list(scratch_shapes) + c_sems,
        input_output_aliases=aliases, compiler_params=_params(len(grid)),
    )(*args, *c_arrays)
    at = n_out
    for cm in comms:
        cm.results = list(res[at:at + len(cm.out_shapes)])
        at += len(cm.out_shapes)
    return list(res[:n_out])


def _join_columns(w_ref, wcat):
    nd, _, _, w = w_ref.shape

    @pl.when(pl.program_id(0) == 0)
    def _():
        for j in range(nd):
            wcat[:, j * w:(j + 1) * w] = w_ref[j, 0]


def _in_fwd(x, ng, scale, shift, w_in_g, layer, tile, comms=None):
    s, d = x.shape
    nd, _, _, w = w_in_g.shape

    def body(x_ref, ng_ref, sc_ref, sh_ref, w_ref, u_ref, h_ref, wcat):
        _join_columns(w_ref, wcat)
        xv = x_ref[...]
        rs = lax.rsqrt(jnp.mean(xv * xv, axis=1, keepdims=True) + EPS)
        hb = (xv * rs * ng_ref[...] * (1.0 + sc_ref[...]) + sh_ref[...]).astype(BF16)
        h_ref[...] = hb
        u_ref[...] = jnp.dot(hb, wcat[...], preferred_element_type=F32)

    return _call(
        body, comms, name="in_proj_fwd", grid=(s // tile,),
        in_specs=[pl.BlockSpec((tile, d), lambda i: (i, 0)), _full((1, d)), _full((1, d)), _full((1, d)),
                  pl.BlockSpec((nd, 1, d, w), lambda i: (0, layer, 0, 0), pipeline_mode=pl.Buffered(1))],
        out_specs=[pl.BlockSpec((tile, nd * w), lambda i: (i, 0)), pl.BlockSpec((tile, d), lambda i: (i, 0))],
        out_shape=[jax.ShapeDtypeStruct((s, nd * w), F32), jax.ShapeDtypeStruct((s, d), BF16)],
        scratch_shapes=[pltpu.VMEM((d, nd * w), BF16)],
        args=(x, ng, scale, shift, w_in_g))


def _rg_fwd(u, d, conv_w, conv_b, w_a, b_a, w_x, b_x, lam, tile, comms=None):
    s = u.shape[0]

    def body(x_ref, z_ref, cw_ref, cb_ref, wa_ref, ba_ref, wx_ref, bx_ref, lam_ref,
             h_ref, y_ref, xc_ref, r_ref, i_ref, a_ref, beta_ref, prev8, hcar):
        @pl.when(pl.program_id(0) == 0)
        def _():
            prev8[...] = jnp.zeros_like(prev8)
            hcar[...] = jnp.zeros_like(hcar)

        x = x_ref[...]
        xc = _conv(x, prev8[...], cw_ref) + cb_ref[...]
        prev8[...] = x[tile - SUBLANES:tile, :]
        r, ig, _, a, beta = _rg_gates(xc, wa_ref, ba_ref, wx_ref, bx_ref, lam_ref)
        xc_ref[...] = xc
        r_ref[...] = r
        i_ref[...] = ig
        a_ref[...] = a
        beta_ref[...] = beta
        _scan_into(a, beta * ig * xc, hcar[SUBLANES - 1:SUBLANES, :], h_ref, False)
        h = h_ref[...]
        hcar[...] = h[tile - SUBLANES:tile, :]
        z = z_ref[...]
        y_ref[...] = (h * z * _sigmoid(z)).astype(BF16)

    vec = _full((1, d))
    return _call(
        body, comms, name="rglru_fwd", grid=(s // tile,),
        in_specs=[pl.BlockSpec((tile, d), lambda i: (i, 0)), pl.BlockSpec((tile, d), lambda i: (i, 1)),
                  _full(conv_w.shape), vec, _full(w_a.shape), vec, _full(w_x.shape), vec, vec],
        out_specs=[pl.BlockSpec((tile, d), lambda i: (i, 0))] * 7,
        out_shape=[jax.ShapeDtypeStruct((s, d), F32), jax.ShapeDtypeStruct((s, d), BF16)] + [jax.ShapeDtypeStruct((s, d), F32)] * 5,
        scratch_shapes=[pltpu.VMEM((SUBLANES, d), F32), pltpu.VMEM((SUBLANES, d), F32)],
        args=(u, u, conv_w, conv_b, w_a, b_a, w_x, b_x, lam))


def _ml_pre(u, d, conv_w, conv_b, w_q, w_k, w_v, wif, bif, tile, comms=None):
    s = u.shape[0]
    nh = w_q.shape[0]

    def body(x_ref, cw_ref, cb_ref, wq_ref, wk_ref, wv_ref, wif_ref, bif_ref, q_ref, k_ref, v_ref, g_ref, pre_ref, prev8):
        @pl.when(pl.program_id(0) == 0)
        def _():
            prev8[...] = jnp.zeros_like(prev8)

        x = x_ref[...]
        pre = _conv(x, prev8[...], cw_ref) + cb_ref[...]
        prev8[...] = x[tile - SUBLANES:tile, :]
        xc = pre * _sigmoid(pre)
        q = _blockdiag(xc, wq_ref)
        k = _blockdiag(xc, wk_ref)
        v = _blockdiag(x, wv_ref)
        pre_ref[...] = pre
        q_ref[...] = q
        k_ref[...] = k
        v_ref[...] = v
        g = _mm(q, wif_ref[0:d, :]) + _mm(k, wif_ref[d:2 * d, :]) + _mm(v, wif_ref[2 * d:3 * d, :]) + bif_ref[...]
        lane = _iota(g.shape, 1)
        gl = jnp.where(lane < 4, g, jnp.where(lane < 8, -_softplus(-g), 0.0))
        tri = jnp.where(_iota((ML_CHUNK, ML_CHUNK), 1) <= _iota((ML_CHUNK, ML_CHUNK), 0), 1.0, 0.0)
        cums = [_mm_hi(tri, gl[c * ML_CHUNK:(c + 1) * ML_CHUNK, :]) for c in range(tile // ML_CHUNK)]
        cum = cums[0] if len(cums) == 1 else jnp.concatenate(cums, axis=0)
        g_ref[...] = gl + jnp.where((lane >= 8) & (lane < 12), pltpu.roll(cum, 4, 1), 0.0)

    vec = _full((1, d))
    return _call(
        body, comms, name="mlstm_proj_fwd", grid=(s // tile,),
        in_specs=[pl.BlockSpec((tile, d), lambda i: (i, 2)), _full(conv_w.shape), vec,
                  _full(w_q.shape), _full(w_k.shape), _full(w_v.shape), _full(wif.shape), _full((1, LANES))],
        out_specs=[pl.BlockSpec((tile, d), lambda i: (i, 0))] * 3 + [pl.BlockSpec((tile, LANES), lambda i: (i, 0)),
                                                                     pl.BlockSpec((tile, d), lambda i: (i, 0))],
        out_shape=[jax.ShapeDtypeStruct((s, d), F32)] * 3 + [jax.ShapeDtypeStruct((s, LANES), F32),
                                                             jax.ShapeDtypeStruct((s, d), F32)],
        scratch_shapes=[pltpu.VMEM((SUBLANES, d), F32)],
        args=(u, conv_w, conv_b, w_q, w_k, w_v, wif, bif))


CELL_CHUNKS_PER_STEP = 4
CELL_BWD_CHUNKS_PER_STEP = 2


def _cell_chunk(h, nh, q_ref, k_ref, v_ref, gc, gr, m_prev, c_h, n_h, m_t=None, r0=0):
    lc = ML_CHUNK
    dh = q_ref.shape[1] // nh
    sl = slice(h * dh, (h + 1) * dh)
    qh = q_ref[r0:r0 + lc, sl]
    kh = k_ref[r0:r0 + lc, sl] * (dh ** -0.5)
    vh = v_ref[r0:r0 + lc, sl]
    li_c = _col(gc, h)
    b_c = _col(gc, 8 + h)
    lib_r = _row(gr, h) - _row(gr, 8 + h)
    b_last = _colsum(jnp.where(_iota((lc, 1), 0) == lc - 1, b_c, 0.0))
    causal = _iota((lc, lc), 1) <= _iota((lc, lc), 0)
    dmat = jnp.where(causal, b_c + lib_r, NEG_BIG)
    m_inter = b_c + m_prev
    if m_t is None:
        m_t = jnp.maximum(m_inter, jnp.max(dmat, axis=1, keepdims=True))
    w_intra = jnp.exp(dmat - m_t)
    w_inter = jnp.exp(m_inter - m_t)
    amat = _mm_nt(qh, kh)
    smat = amat * w_intra
    qc = _mm(qh, c_h)
    qn = _rowsum(qh * n_h)
    den = _rowsum(smat) + w_inter * qn
    gst = b_last - b_c + li_c
    m_new = jnp.maximum(b_last + m_prev, jnp.max(gst, axis=0, keepdims=True))
    w_state = jnp.exp(gst - m_new)
    decay = jnp.exp(b_last + m_prev - m_new)
    return dict(sl=sl, qh=qh, kh=kh, vh=vh, m_t=m_t, w_intra=w_intra, w_inter=w_inter, smat=smat, qc=qc, qn=qn,
                den=den, m_new=m_new, w_state=w_state, decay=decay)


def _ml_cell_fwd(q, k, v, gcol, grow, u, ng, nh, comms=None):
    s, d = q.shape
    lc = ML_CHUNK
    nc = s // lc
    dh = d // nh

    per = CELL_CHUNKS_PER_STEP if nc % CELL_CHUNKS_PER_STEP == 0 else 1

    def body(q_ref, k_ref, v_ref, gc_ref, gr_ref, o_ref, z_ref, ng_ref,
             cell_ref, y_ref, cs_ref, ns_ref, ms_ref, mt_ref, c_sc, n_sc, m_sc):
        @pl.when(pl.program_id(0) == 0)
        def _():
            c_sc[...] = jnp.zeros_like(c_sc)
            n_sc[...] = jnp.zeros_like(n_sc)
            m_sc[...] = jnp.zeros_like(m_sc)

        lane = _iota((lc, LANES), 1)
        for cc in range(per):
            rows = slice(cc * lc, (cc + 1) * lc)
            gc = gc_ref[rows, :]
            gr = gr_ref[:, rows]
            mt_acc = jnp.zeros((lc, LANES), F32)
            for h in range(nh):
                c_h = c_sc[h]
                n_h = n_sc[h, 0:1, :]
                m_prev = jnp.max(m_sc[h, 0:1, :], axis=1, keepdims=True)
                cs_ref[cc, h] = c_h
                ns_ref[cc, h] = n_sc[h]
                ms_ref[cc, h] = m_sc[h]
                t = _cell_chunk(h, nh, q_ref, k_ref, v_ref, gc, gr, m_prev, c_h, n_h, r0=cc * lc)
                sl = t["sl"]
                num = _mm(t["smat"], t["vh"]) + t["w_inter"] * t["qc"]
                cell_h = num / jnp.maximum(jnp.abs(t["den"]), jnp.exp(-t["m_t"]))
                mt_acc = jnp.where(lane == h, t["m_t"], mt_acc)
                kw = t["kh"] * t["w_state"]
                c_sc[h] = t["decay"] * c_h + _mm_tn(kw, t["vh"])
                n_sc[h] = _bcast8(t["decay"] * n_h + _colsum(kw))
                m_sc[h] = jnp.broadcast_to(t["m_new"], (SUBLANES, LANES))
                hg = _sigmoid(o_ref[rows, sl]) * cell_h
                hn = hg * lax.rsqrt(jnp.mean(hg * hg, axis=1, keepdims=True) + EPS)
                z = z_ref[rows, sl]
                cell_ref[rows, sl] = cell_h
                y_ref[rows, sl] = (hn * ng_ref[:, sl] * z * _sigmoid(z)).astype(BF16)
            mt_ref[rows, :] = mt_acc

    tok = pl.BlockSpec((per * lc, d), lambda c: (c, 0))
    return _call(
        body, comms, name="mlstm_cell_fwd", grid=(nc // per,),
        in_specs=[tok, tok, tok, pl.BlockSpec((per * lc, LANES), lambda c: (c, 0)),
                  pl.BlockSpec((16, per * lc), lambda c: (0, c)),
                  pl.BlockSpec((per * lc, d), lambda c: (c, 3)), pl.BlockSpec((per * lc, d), lambda c: (c, 4)), _full((1, d))],
        out_specs=[tok, tok, pl.BlockSpec((per, nh, dh, dh), lambda c: (c, 0, 0, 0)),
                   pl.BlockSpec((per, nh, SUBLANES, dh), lambda c: (c, 0, 0, 0)),
                   pl.BlockSpec((per, nh, SUBLANES, LANES), lambda c: (c, 0, 0, 0)),
                   pl.BlockSpec((per * lc, LANES), lambda c: (c, 0))],
        out_shape=[jax.ShapeDtypeStruct((s, d), F32), jax.ShapeDtypeStruct((s, d), BF16),
                   jax.ShapeDtypeStruct((nc, nh, dh, dh), F32), jax.ShapeDtypeStruct((nc, nh, SUBLANES, dh), F32),
                   jax.ShapeDtypeStruct((nc, nh, SUBLANES, LANES), F32), jax.ShapeDtypeStruct((s, LANES), F32)],
        scratch_shapes=[pltpu.VMEM((nh, dh, dh), F32), pltpu.VMEM((nh, SUBLANES, dh), F32),
                        pltpu.VMEM((nh, SUBLANES, LANES), F32)],
        args=(q, k, v, gcol, grow, u, u, ng))


def _out_fwd(x, y_rg, y_ml, gate, w_out_g, layer, tile, comms=None, head=None):
    s, d = x.shape
    nd, _, r, _ = w_out_g.shape

    def body(x_ref, yr_ref, ym_ref, g_ref, w_ref, *rest):
        ycat = jnp.concatenate([yr_ref[...].astype(BF16), ym_ref[...].astype(BF16)], axis=1)
        acc = jnp.dot(ycat, w_ref[...].reshape(nd * r, d), preferred_element_type=F32)
        xn = x_ref[...] + g_ref[...] * acc
        if head is None:
            xn_ref, y_ref = rest
            y_ref[...] = acc
            xn_ref[...] = xn
            return
        fg_ref, t_ref, y_ref, dx_ref, loss_ref, gg_ref = rest
        y_ref[...] = acc

        @pl.when(pl.program_id(0) == 0)
        def _():
            loss_ref[...] = jnp.zeros_like(loss_ref)
            gg_ref[...] = jnp.zeros_like(gg_ref)

        fg = fg_ref[...]
        rs = lax.rsqrt(jnp.mean(xn * xn, axis=1, keepdims=True) + EPS)
        xh = xn * rs
        e = xh * fg - t_ref[...]
        loss_ref[...] += jnp.broadcast_to(_colsum(_rowsum(e * e)) * (0.5 / d), loss_ref.shape)
        dy = e * (1.0 / d)
        gg_ref[...] += _bcast8(_colsum(dy * xh))
        dxh = dy * fg
        dx_ref[...] = rs * (dxh - xh * jnp.mean(dxh * xh, axis=1, keepdims=True))

    tok = pl.BlockSpec((tile, d), lambda i: (i, 0))
    in_specs = [tok, tok, tok, _full((1, d)), pl.BlockSpec((nd, 1, r, d), lambda i: (0, layer, 0, 0))]
    if head is None:
        return _call(body, comms, name="out_proj_fwd", grid=(s // tile,), in_specs=in_specs, out_specs=[tok, tok],
                     out_shape=[jax.ShapeDtypeStruct((s, d), F32)] * 2, args=(x, y_rg, y_ml, gate, w_out_g))
    return _call(
        body, comms, name="out_proj_loss", grid=(s // tile,),
        in_specs=in_specs + [_full((1, d)), tok],
        out_specs=[tok, tok, _full((SUBLANES, LANES)), _full((SUBLANES, d))],
        out_shape=[jax.ShapeDtypeStruct((s, d), F32)] * 2 + [jax.ShapeDtypeStruct((SUBLANES, LANES), F32),
                                                             jax.ShapeDtypeStruct((SUBLANES, d), F32)],
        args=(x, y_rg, y_ml, gate, w_out_g, *head))


def _ml_out_stage_bwd(dy, cell, o, z, ng):
    so = _sigmoid(o)
    hg = so * cell
    rinv = lax.rsqrt(jnp.mean(hg * hg, axis=1, keepdims=True) + EPS)
    hn = hg * rinv
    sz = _sigmoid(z)
    dz = dy * hn * ng * (sz + z * sz * (1.0 - sz))
    dymid = dy * z * sz
    dhn = dymid * ng
    dhg = rinv * (dhn - hn * jnp.mean(dhn * hn, axis=1, keepdims=True))
    return dz, dhg * cell * so * (1.0 - so), dhg * so, _colsum(dymid * hn)


def _out_bwd(dxo, gate, y, y_rg, y_ml, w_out_g, layer, tile, comms=None):
    s, d = dxo.shape
    nd, _, r, _ = w_out_g.shape

    def body(dx_ref, g_ref, y_ref, yr_ref, ym_ref, w_ref, dyr_ref, dym_ref, gw_ref, dg_ref):
        @pl.when(pl.program_id(0) == 0)
        def _():
            gw_ref[...] = jnp.zeros_like(gw_ref)
            dg_ref[...] = jnp.zeros_like(dg_ref)

        dxv = dx_ref[...]
        dg_ref[...] += _bcast8(_colsum(dxv * y_ref[...]))
        dyb = (dxv * g_ref[...]).astype(BF16)
        dycat = lax.dot_general(dyb, w_ref[...].reshape(nd * r, d), (((1,), (1,)), ((), ())), preferred_element_type=F32)
        dyr_ref[...] = dycat[:, 0:d]
        dym_ref[...] = dycat[:, d:2 * d]
        ycat = jnp.concatenate([yr_ref[...].astype(BF16), ym_ref[...].astype(BF16)], axis=1)
        gw_ref[...] += lax.dot_general(ycat, dyb, (((0,), (0,)), ((), ())), preferred_element_type=F32).reshape(nd, r, d)

    tok = pl.BlockSpec((tile, d), lambda i: (i, 0))
    return _call(
        body, comms, name="out_proj_bwd", grid=(s // tile,),
        in_specs=[tok, _full((1, d)), tok, tok, tok, pl.BlockSpec((nd, 1, r, d), lambda i: (0, layer, 0, 0))],
        out_specs=[tok, tok, _full((nd, r, d)), _full((SUBLANES, d))],
        out_shape=[jax.ShapeDtypeStruct((s, d), F32)] * 2 + [jax.ShapeDtypeStruct((nd, r, d), F32),
                                                             jax.ShapeDtypeStruct((SUBLANES, d), F32)],
        args=(dxo, gate, y, y_rg, y_ml, w_out_g))


def _ml_cell_bwd(dy_ml, u, cell, q, k, v, gcol, grow, mt, cs, ns, ms, ng, nh, comms=None):
    s, d = q.shape
    lc = ML_CHUNK
    nc = s // lc
    dh = d // nh

    per = CELL_BWD_CHUNKS_PER_STEP if nc % CELL_BWD_CHUNKS_PER_STEP == 0 else 1

    def body(*refs):
        gng_ref, dc_sc, dn_sc = refs[20], refs[21], refs[22]

        @pl.when(pl.program_id(0) == 0)
        def _():
            dc_sc[...] = jnp.zeros_like(dc_sc)
            dn_sc[...] = jnp.zeros_like(dn_sc)
            gng_ref[...] = jnp.zeros_like(gng_ref)

        for cc in reversed(range(per)):
            rows = slice(cc * lc, (cc + 1) * lc)
            views = [refs[at] if at == 13 else _PartOf(refs[at], cols=rows) if at == 8 else
                     _PartOf(refs[at], lead=cc) if at in (10, 11, 12) else _PartOf(refs[at], rows=rows) for at in range(20)]
            chunk(*views, gng_ref, dc_sc, dn_sc)

    def chunk(dy_ref, o_ref, z_ref, cell_ref, q_ref, k_ref, v_ref, gc_ref, gr_ref, mt_ref, cs_ref, ns_ref, ms_ref,
              ng_ref, dq_ref, dk_ref, dv_ref, dg_ref, do_ref, dz_ref, gng_ref, dc_sc, dn_sc):
        gc = gc_ref[...]
        gr = gr_ref[...]
        mtv = mt_ref[...]
        lane = _iota((lc, LANES), 1)
        rowv = _iota((lc, 1), 0)
        dg_acc = jnp.zeros((lc, LANES), F32)
        for h in range(nh):
            c_h = cs_ref[0, h]
            n_h = ns_ref[0, h, 0:1, :]
            m_prev = jnp.max(ms_ref[0, h, 0:1, :], axis=1, keepdims=True)
            t = _cell_chunk(h, nh, q_ref, k_ref, v_ref, gc, gr, m_prev, c_h, n_h, m_t=_col(mtv, h))
            sl, qh, kh, vh = t["sl"], t["qh"], t["kh"], t["vh"]
            w_intra, w_inter, smat, w_state, decay = t["w_intra"], t["w_inter"], t["smat"], t["w_state"], t["decay"]
            cell_h = cell_ref[:, sl]
            dz, do, dcell, gng = _ml_out_stage_bwd(dy_ref[:, sl], cell_h, o_ref[:, sl], z_ref[:, sl], ng_ref[:, sl])
            dz_ref[:, sl] = dz.astype(BF16)
            do_ref[:, sl] = do.astype(BF16)
            gng_ref[:, sl] += _bcast8(gng)
            eneg = jnp.exp(-t["m_t"])
            aden = jnp.abs(t["den"])
            nst = jnp.maximum(aden, eneg)
            dnum = dcell / nst
            dden = jnp.where(aden > eneg, -_rowsum(cell_h * dcell) / nst * jnp.sign(t["den"]), 0.0)
            pmat = _mm_nt(dnum, vh) + dden
            damat = pmat * w_intra
            gmat = pmat * smat
            wdn = w_inter * dnum
            wdd = w_inter * dden
            dqh = _mm(damat, kh) + _mm_nt(wdn, c_h) + wdd * n_h
            dkh = _mm_tn(damat, qh)
            dvh = _mm_tn(smat, dnum)
            dw_inter = _rowsum(dnum * t["qc"]) + dden * t["qn"]
            dcn = dc_sc[h]
            dnn = dn_sc[h, 0:1, :]
            kw = kh * w_state
            dkw = _mm_nt(vh, dcn) + dnn
            dvh = dvh + _mm(kw, dcn)
            dkh = dkh + dkw * w_state
            dgst = _rowsum(dkw * kh) * w_state
            ddecay = _colsum(_rowsum(dcn * c_h)) + _rowsum(dnn * n_h)
            db_last = _colsum(dgst) + ddecay * decay
            rs_g = _rowsum(gmat)
            cs_g = _rowsum(gmat.T)
            db = rs_g - cs_g + dw_inter * w_inter - dgst + jnp.where(rowv == lc - 1, db_last, 0.0)
            dli = cs_g + dgst
            dc_sc[h] = decay * dcn + _mm_tn(qh, wdn)
            dn_sc[h] = _bcast8(decay * dnn + _colsum(qh * wdd))
            dq_ref[:, sl] = dqh
            dk_ref[:, sl] = dkh * (dh ** -0.5)
            dv_ref[:, sl] = dvh
            dg_acc = jnp.where(lane == h, dli, jnp.where(lane == 4 + h, db, dg_acc))
        dg_ref[...] = dg_acc

    rev = lambda c: nc // per - 1 - c
    tok = pl.BlockSpec((per * lc, d), lambda c: (rev(c), 0))
    g128 = pl.BlockSpec((per * lc, LANES), lambda c: (rev(c), 0))
    return _call(
        body, comms, name="mlstm_cell_bwd", grid=(nc // per,),
        in_specs=[tok, pl.BlockSpec((per * lc, d), lambda c: (rev(c), 3)), pl.BlockSpec((per * lc, d), lambda c: (rev(c), 4)),
                  tok, tok, tok, tok, g128, pl.BlockSpec((16, per * lc), lambda c: (0, rev(c))), g128,
                  pl.BlockSpec((per, nh, dh, dh), lambda c: (rev(c), 0, 0, 0)),
                  pl.BlockSpec((per, nh, SUBLANES, dh), lambda c: (rev(c), 0, 0, 0)),
                  pl.BlockSpec((per, nh, SUBLANES, LANES), lambda c: (rev(c), 0, 0, 0)), _full((1, d))],
        out_specs=[tok, tok, tok, g128, tok, tok, _full((SUBLANES, d))],
        out_shape=[jax.ShapeDtypeStruct((s, d), F32)] * 3 + [jax.ShapeDtypeStruct((s, LANES), F32)]
        + [jax.ShapeDtypeStruct((s, d), BF16)] * 2 + [jax.ShapeDtypeStruct((SUBLANES, d), F32)],
        scratch_shapes=[pltpu.VMEM((nh, dh, dh), F32), pltpu.VMEM((nh, SUBLANES, dh), F32)],
        args=(dy_ml, u, u, cell, q, k, v, gcol, grow, mt, cs, ns, ms, ng))


def _halo_spec(d, tile, nt, col):
    per = tile // SUBLANES
    return pl.BlockSpec((SUBLANES, d), lambda i: (jnp.maximum((nt - 1 - i) * per - 1, 0), col))


def _ml_pre_bwd(dq, dk, dv, dgates, gcol, u, pre, q, k, v, conv_w, w_q, w_k, w_v, wif_t, tile, comms=None):
    s, d = dq.shape
    nt = s // tile
    nh, dh, _ = w_q.shape

    def body(dq_ref, dk_ref, dv_ref, dg_ref, gc_ref, x_ref, pre_ref, q_ref, k_ref, v_ref, cw_ref,
             wq_ref, wk_ref, wv_ref, wift_ref,
             dx_ref, gwq_ref, gwk_ref, gwv_ref, gwif_ref, gbif_ref, gcw_ref, gcb_ref, next8):
        @pl.when(pl.program_id(0) == 0)
        def _():
            next8[...] = jnp.zeros_like(next8)
            for ref in (gwq_ref, gwk_ref, gwv_ref, gwif_ref, gbif_ref, gcw_ref, gcb_ref):
                ref[...] = jnp.zeros_like(ref)

        x = x_ref[...]
        pre = pre_ref[...]
        sg = _sigmoid(pre)
        xc = pre * sg
        dgc = dg_ref[...]
        lane = _iota(dgc.shape, 1)
        utri = jnp.where(_iota((ML_CHUNK, ML_CHUNK), 0) <= _iota((ML_CHUNK, ML_CHUNK), 1), 1.0, 0.0)
        rcs = [_mm_hi(utri, dgc[c * ML_CHUNK:(c + 1) * ML_CHUNK, :]) for c in range(tile // ML_CHUNK)]
        rc = rcs[0] if len(rcs) == 1 else jnp.concatenate(rcs, axis=0)
        dgates_v = jnp.where(lane < 4, dgc, jnp.where(lane < 8, rc * (1.0 - jnp.exp(gc_ref[...])), 0.0))
        dgb = dgates_v.astype(BF16)
        gbif_ref[...] += jnp.broadcast_to(_colsum(dgates_v), gbif_ref.shape)
        ext = jnp.dot(dgb, wift_ref[...], preferred_element_type=F32)
        dqt = dq_ref[...] + ext[:, 0:d]
        dkt = dk_ref[...] + ext[:, d:2 * d]
        dvt = dv_ref[...] + ext[:, 2 * d:3 * d]
        gwif_ref[:, 0:d] += _mm_tn(dgb, q_ref[...])
        gwif_ref[:, d:2 * d] += _mm_tn(dgb, k_ref[...])
        gwif_ref[:, 2 * d:3 * d] += _mm_tn(dgb, v_ref[...])
        dxc_parts, dxv_parts = [], []
        for h in range(nh):
            sl = slice(h * dh, (h + 1) * dh)
            gwq_ref[h] += _mm_tn(xc[:, sl], dqt[:, sl])
            gwk_ref[h] += _mm_tn(xc[:, sl], dkt[:, sl])
            gwv_ref[h] += _mm_tn(x[:, sl], dvt[:, sl])
            dxc_parts.append(_mm_nt(dqt[:, sl], wq_ref[h]) + _mm_nt(dkt[:, sl], wk_ref[h]))
            dxv_parts.append(_mm_nt(dvt[:, sl], wv_ref[h]))
        dxc = jnp.concatenate(dxc_parts, axis=1)
        dxv = jnp.concatenate(dxv_parts, axis=1)
        dpre = dxc * (sg + pre * sg * (1.0 - sg))
        gcb_ref[...] += _bcast8(_colsum(dpre))
        dx_ref[...] = (dxv + _conv_bwd(dpre, x, next8[...], cw_ref, gcw_ref)).astype(BF16)
        next8[...] = dpre[0:SUBLANES, :]

    rev = lambda i: nt - 1 - i
    tok = pl.BlockSpec((tile, d), lambda i: (rev(i), 0))
    g128 = pl.BlockSpec((tile, LANES), lambda i: (rev(i), 0))
    wsh = (nh, dh, dh)
    return _call(
        body, comms, name="mlstm_proj_bwd", grid=(nt,),
        in_specs=[tok, tok, tok, g128, g128, pl.BlockSpec((tile, d), lambda i: (rev(i), 2)), tok,
                  tok, tok, tok, _full(conv_w.shape), _full(wsh), _full(wsh), _full(wsh), _full(wif_t.shape)],
        out_specs=[tok, _full(wsh), _full(wsh), _full(wsh), _full((LANES, 3 * d)), _full((SUBLANES, LANES)),
                   _full((SUBLANES, d)), _full((SUBLANES, d))],
        out_shape=[jax.ShapeDtypeStruct((s, d), BF16)] + [jax.ShapeDtypeStruct(wsh, F32)] * 3
        + [jax.ShapeDtypeStruct((LANES, 3 * d), F32), jax.ShapeDtypeStruct((SUBLANES, LANES), F32),
           jax.ShapeDtypeStruct((SUBLANES, d), F32), jax.ShapeDtypeStruct((SUBLANES, d), F32)],
        scratch_shapes=[pltpu.VMEM((SUBLANES, d), F32)],
        args=(dq, dk, dv, dgates, gcol, u, pre, q, k, v, conv_w, w_q, w_k, w_v, wif_t))


def _rg_bwd(dy_rg, u, h_rg, gates, conv_w, w_a, w_x, lam, tile, comms=None):
    s, d = dy_rg.shape
    nt = s // tile
    nh, dh, _ = w_a.shape

    def body(dy_ref, x_ref, z_ref, h_ref, hhalo_ref, xc_ref, r_ref, i_ref, a_ref, beta_ref, cw_ref, wa_ref,
             wx_ref, lam_ref,
             dx_ref, dz_ref, gwa_ref, gwx_ref, gba_ref, gbx_ref, glam_ref, gcw_ref, gcb_ref, next8, anext, dnext, dbuf):
        i = pl.program_id(0)

        @pl.when(i == 0)
        def _():
            for ref in (next8, anext, dnext, gwa_ref, gwx_ref, gba_ref, gbx_ref, glam_ref, gcw_ref, gcb_ref):
                ref[...] = jnp.zeros_like(ref)

        inner = jnp.where(i < nt - 1, 1.0, 0.0)
        xc, r, ig, a, beta = xc_ref[...], r_ref[...], i_ref[...], a_ref[...], beta_ref[...]
        sp = _softplus(-lam_ref[...])
        h = h_ref[...]
        row = _iota(h.shape, 0)
        hprev = jnp.where(row >= 1, pltpu.roll(h, 1, 0), hhalo_ref[SUBLANES - 1:SUBLANES, :] * inner)
        z = z_ref[...]
        sz = _sigmoid(z)
        dyv = dy_ref[...]
        dz_ref[...] = (dyv * h * (sz + z * sz * (1.0 - sz))).astype(BF16)
        a_up = jnp.where(row < tile - 1, pltpu.roll(a, tile - 1, 0), anext[0:1, :])
        _scan_into(a_up, dyv * z * sz, dnext[0:1, :], dbuf, True)
        delta = dbuf[...]
        anext[...] = a[0:SUBLANES, :]
        dnext[...] = delta[0:SUBLANES, :]
        dla = delta * hprev * a - delta * ig * xc * (a * a / beta)
        glam_ref[...] += _bcast8(_colsum(dla * r) * (RG_C * _sigmoid(-lam_ref[...])))
        dpa = dla * (-RG_C * sp) * r * (1.0 - r)
        dpx = delta * beta * xc * ig * (1.0 - ig)
        gba_ref[...] += _bcast8(_colsum(dpa))
        gbx_ref[...] += _bcast8(_colsum(dpx))
        parts = []
        for hh in range(nh):
            sl = slice(hh * dh, (hh + 1) * dh)
            gwa_ref[hh] += _mm_tn(xc[:, sl], dpa[:, sl])
            gwx_ref[hh] += _mm_tn(xc[:, sl], dpx[:, sl])
            parts.append(_mm_nt(dpa[:, sl], wa_ref[hh]) + _mm_nt(dpx[:, sl], wx_ref[hh]))
        dxc = delta * beta * ig + jnp.concatenate(parts, axis=1)
        gcb_ref[...] += _bcast8(_colsum(dxc))
        dx_ref[...] = _conv_bwd(dxc, x_ref[...], next8[...], cw_ref, gcw_ref).astype(BF16)
        next8[...] = dxc[0:SUBLANES, :]

    rev = lambda i: nt - 1 - i
    tok = pl.BlockSpec((tile, d), lambda i: (rev(i), 0))
    vec = _full((1, d))
    acc = _full((SUBLANES, d))
    wsh = (nh, dh, dh)
    return _call(
        body, comms, name="rglru_bwd", grid=(nt,),
        in_specs=[tok, tok, pl.BlockSpec((tile, d), lambda i: (rev(i), 1)), tok,
                  _halo_spec(d, tile, nt, 0)] + [tok] * 5 + [_full(conv_w.shape), _full(wsh), _full(wsh), vec],
        out_specs=[tok, tok, _full(wsh), _full(wsh), acc, acc, acc, acc, acc],
        out_shape=[jax.ShapeDtypeStruct((s, d), BF16)] * 2 + [jax.ShapeDtypeStruct(wsh, F32)] * 2
        + [jax.ShapeDtypeStruct((SUBLANES, d), F32)] * 5,
        scratch_shapes=[pltpu.VMEM((SUBLANES, d), F32)] * 3 + [pltpu.VMEM((tile, d), F32)],
        args=(dy_rg, u, u, h_rg, h_rg, *gates, conv_w, w_a, w_x, lam))


def _segments(d, w, n_pieces, n_slots):
    bounds = sorted({k * d for k in range(n_pieces + 1)} | {j * w for j in range(n_slots + 1)})
    return [(lo // d, lo % d, lo // w, lo % w, hi - lo) for lo, hi in zip(bounds[:-1], bounds[1:])]


def _in_bwd(pieces, x, dxo, ng, scale, w_in_g, layer, tile, comms=None, tiles=None, prev=None):
    s, d = x.shape
    nd, _, _, w = w_in_g.shape
    first, count = tiles or (0, s // tile)
    n_p = len(pieces)

    def body(*refs):
        p_refs = refs[:n_p]
        x_ref, dxo_ref, ng_ref, sc_ref, w_ref = refs[n_p:n_p + 5]
        dx_ref, dsc_ref, dsh_ref, gng_ref, wcat = refs[-5:]
        _join_columns(w_ref, wcat)

        @pl.when(pl.program_id(0) == 0)
        def _():
            for k, ref in enumerate((dsc_ref, dsh_ref, gng_ref)):
                ref[...] = jnp.zeros_like(ref) if prev is None else refs[n_p + 6 + k][...]

        du = jnp.concatenate([p[...] for p in p_refs], axis=1)
        dh = lax.dot_general(du, wcat[...], (((1,), (1,)), ((), ())), preferred_element_type=F32)
        xv = x_ref[...]
        g = ng_ref[...]
        rs = lax.rsqrt(jnp.mean(xv * xv, axis=1, keepdims=True) + EPS)
        xh = xv * rs
        dsh_ref[...] += _bcast8(_colsum(dh))
        dsc_ref[...] += _bcast8(_colsum(dh * xh * g))
        dhn = dh * (1.0 + sc_ref[...])
        gng_ref[...] += _bcast8(_colsum(dhn * xh))
        dxh = dhn * g
        dx_ref[...] = dxo_ref[...] + rs * (dxh - xh * jnp.mean(dxh * xh, axis=1, keepdims=True))

    tok = pl.BlockSpec((tile, d), lambda i: (i + first, 0))
    vec = _full((1, d))
    acc = _full((SUBLANES, d))
    more_specs = [] if prev is None else [pl.BlockSpec(memory_space=pl.ANY), acc, acc, acc]
    return _call(
        body, comms, name="in_proj_bwd_x", grid=(count,),
        in_specs=[tok] * n_p + [tok, tok, vec, vec, pl.BlockSpec((nd, 1, d, w), lambda i: (0, layer, 0, 0),
                                                               pipeline_mode=pl.Buffered(1))] + more_specs,
        out_specs=[tok, acc, acc, acc],
        out_shape=[jax.ShapeDtypeStruct((s, d), F32)] + [jax.ShapeDtypeStruct((SUBLANES, d), F32)] * 3,
        scratch_shapes=[pltpu.VMEM((d, nd * w), BF16)],
        args=(*pieces, x, dxo, ng, scale, w_in_g) + (() if prev is None else tuple(prev)),
        aliases={} if prev is None else {n_p + 5: 0})


def _in_bwd_w(pieces, hbf, w, slots, tile, comms=None):
    s, d = hbf.shape
    nd_all = len(pieces) * d // w
    segs = [sg for sg in _segments(d, w, len(pieces), nd_all) if sg[2] in slots]

    def body(*refs):
        p_refs = refs[:len(pieces)]
        h_ref, gw_ref = refs[len(pieces):]

        @pl.when(pl.program_id(0) == 0)
        def _():
            gw_ref[...] = jnp.zeros_like(gw_ref)

        hv = h_ref[...]
        for (kk, a, j, b, width) in segs:
            gw_ref[j - slots[0], :, b:b + width] += _mm_tn(hv, p_refs[kk][:, a:a + width])

    tok = pl.BlockSpec((tile, d), lambda i: (i, 0))
    return _call(
        body, comms, name="in_proj_bwd_w", grid=(s // tile,),
        in_specs=[tok] * len(pieces) + [tok],
        out_specs=[pl.BlockSpec((len(slots), d, w), lambda i: (0, 0, 0), pipeline_mode=pl.Buffered(1))],
        out_shape=[jax.ShapeDtypeStruct((len(slots), d, w), F32)],
        args=(*pieces, hbf))[0]


def _exchange(arrs, gather, name):
    return _run_comms([_exchange_comm(arrs, gather)], name)[0]


def _run_comms(comms, name):
    _call(lambda: None, comms, name=name, grid=(1,), in_specs=[], out_specs=[], out_shape=[], args=())
    return [cm.results for cm in comms]


def _exchange_comm(arrs, gather):
    n = len(arrs)
    per = N_DEV - 1

    def copies(ins, outs, sems):
        send_sems, recv_sems, local_sems = sems
        x, y, c = (lax.axis_index(ax) for ax in MESH_AXES)
        me = 4 * x + 2 * y + c
        sends, recvs = [], []
        for flip in range(1, N_DEV):
            px = x ^ ((flip >> 2) & 1)
            py = y ^ ((flip >> 1) & 1)
            pc = c ^ (flip & 1)
            peer = 4 * px + 2 * py + pc
            for kk in range(n):
                src = ins[kk] if gather else ins[kk].at[peer]
                sends.append(_remote(src, outs[kk].at[me], send_sems, recv_sems, kk * per + flip - 1, (px, py, pc)))
                recvs.append(_remote(src, outs[kk].at[peer], send_sems, recv_sems, kk * per + flip - 1, (px, py, pc)))
        local = [pltpu.make_async_copy(ins[kk] if gather else ins[kk].at[me], outs[kk].at[me], local_sems.at[kk])
                 for kk in range(n)]
        return local, sends, recvs

    def start(ins, outs, sems):
        local, sends, _ = copies(ins, outs, sems)
        for cp in sends + local:
            cp.start()

    def finish(ins, outs, sems):
        local, sends, recvs = copies(ins, outs, sems)
        for cp in recvs:
            cp.wait_recv()
        for cp in sends:
            cp.wait_send()
        for cp in local:
            cp.wait()

    return _Comm(arrs, [jax.ShapeDtypeStruct((N_DEV,) + a.shape if gather else a.shape, a.dtype) for a in arrs],
                 [pltpu.SemaphoreType.DMA((n * per,)), pltpu.SemaphoreType.DMA((n * per,)), pltpu.SemaphoreType.DMA((n,))],
                 start, finish)


def _mesh_place():
    x, y, c = (lax.axis_index(ax) for ax in MESH_AXES)
    return x, y, c, (x, y, 1 - c), [(1 - x, y), (x, 1 - y), (1 - x, 1 - y)]


def _remote(src, dst, send_sems, recv_sems, sem, to):
    return pltpu.make_async_remote_copy(src_ref=src, dst_ref=dst, send_sem=send_sems.at[sem], recv_sem=recv_sems.at[sem],
                                        device_id=to, device_id_type=pl.DeviceIdType.MESH)


N_CHIPS = N_DEV // 2


def _pair_sum(a, other, parity, name):
    _, r, c = a.shape
    tr = _row_tile(r, c, 3)

    def body(p_ref, a_ref, o_ref, s_ref):
        s_ref[...] = (a_ref[...] + o_ref[...]).astype(BF16)

    return pl.pallas_call(
        body, name=name,
        grid_spec=pltpu.PrefetchScalarGridSpec(
            num_scalar_prefetch=1, grid=(N_CHIPS, r // tr),
            in_specs=[pl.BlockSpec((1, tr, c), lambda q, i, p: (2 * q + p[0], i, 0)),
                      pl.BlockSpec((1, tr, c), lambda q, i, p: (q, i, 0))],
            out_specs=pl.BlockSpec((1, tr, c), lambda q, i, p: (q, i, 0))),
        out_shape=jax.ShapeDtypeStruct((N_CHIPS, r, c), BF16),
        compiler_params=_params(2),
    )(parity, a, other)


def _adam_math(w, g, m, v):
    m = ADAM_B1 * m + (1.0 - ADAM_B1) * g
    v = ADAM_B2 * v + (1.0 - ADAM_B2) * (g * g)
    m_hat = m / (1.0 - ADAM_B1 ** ADAM_STEP)
    v_hat = v / (1.0 - ADAM_B2 ** ADAM_STEP)
    delta = -ADAM_LR * (m_hat / (jnp.sqrt(v_hat) + ADAM_EPS) + ADAM_WD * w)
    return delta, m, v


def _sum_devices(r_ref):
    acc = r_ref[0].astype(F32)
    for p in range(1, r_ref.shape[0]):
        acc = acc + r_ref[p].astype(F32)
    return acc


def _row_tile(rows, cols, n_bufs):
    budget = 24 * 1024 * 1024 // (n_bufs * 2 * cols * 4)
    t = rows
    while t > budget and t % 2 == 0 and (t // 2) % SUBLANES == 0:
        t //= 2
    return t


def _reduce_adam(recvs, w, m, v, name, comms=None):
    nl, r, c = w.shape
    n_part = recvs[0].shape[0]
    tr = _row_tile(r, c, n_part * nl + 7)
    nt = r // tr

    def body(*refs):
        r_refs = refs[:nl]
        w_ref, m_ref, v_ref, g_ref, d_ref, mo_ref, vo_ref = refs[nl:]
        layer = pl.program_id(0) // nt
        g = _sum_devices(r_refs[0])
        for ll in range(1, nl):
            g = jnp.where(layer == ll, _sum_devices(r_refs[ll]), g)
        delta, m2, v2 = _adam_math(w_ref[0], g, m_ref[0], v_ref[0])
        g_ref[0] = g
        d_ref[0] = delta
        mo_ref[0] = m2
        vo_ref[0] = v2

    def rspec(ll):
        return pl.BlockSpec((n_part, tr, c),
                            lambda i: (0, jnp.where(i // nt == ll, i % nt, jnp.where(i // nt < ll, 0, nt - 1)), 0))

    blk = pl.BlockSpec((1, tr, c), lambda i: (i // nt, i % nt, 0))
    return _call(
        body, comms, name=name, grid=(nl * nt,),
        in_specs=[rspec(ll) for ll in range(nl)] + [blk, blk, blk],
        out_specs=[blk] * 4,
        out_shape=[jax.ShapeDtypeStruct((nl, r, c), F32)] * 4,
        args=(*recvs, w, m, v))


def _tile_for(s, want):
    return min(want, s)


REPLICATED = ("norm_g", "b_ada", "rg_conv_b", "rg_w_a", "rg_b_a", "rg_w_x", "rg_b_x", "rg_lambda", "ml_conv_b",
              "ml_b_if", "ml_norm_g", "final_g")


def _small_pack(rg_conv_w, ml_conv_w, ml_w_if):
    nl = rg_conv_w.shape[0]
    wif_t = jnp.swapaxes(ml_w_if, 1, 2).reshape(nl, -1, LANES)
    return jnp.concatenate([rg_conv_w, ml_conv_w, wif_t], axis=1)


def _small_unpack(p, if_rows):
    nl = p.shape[0]
    rg_cw = p[:, 0:CONV_WIDTH]
    ml_cw = p[:, CONV_WIDTH:2 * CONV_WIDTH]
    wif = jnp.swapaxes(p[:, 2 * CONV_WIDTH:].reshape(nl, 8, if_rows), 1, 2)
    return rg_cw, ml_cw, wif


def _qkv_slots(g_qkv, nd):
    three, nh, dh, _ = g_qkv.shape
    return g_qkv.reshape(three, nh, nd, dh // nd, dh).transpose(2, 0, 1, 3, 4).reshape(nd, three * nh * (dh // nd), dh)


def _small_slots(g):
    nd = N_DEV
    cw = jnp.stack([g["rg_conv_w"], g["ml_conv_w"]]).reshape(2, CONV_WIDTH, nd, LANES).transpose(2, 0, 1, 3)
    cw = cw.reshape(nd, 2 * CONV_WIDTH, LANES)
    wif = g["wif_t"].reshape(8, nd, -1).transpose(1, 0, 2).reshape(nd, -1, LANES)
    return jnp.concatenate([cw, wif], axis=1)


def _slot(block):
    return 4 * block[0] + 2 * block[1] + block[2]


def _dma_sems(*counts):
    return [pltpu.SemaphoreType.DMA((n,)) for n in counts]


def _start_all(copies):
    for cp in copies:
        cp.start()


def _gather_ici_comm(arrs):
    n = len(arrs)

    def copies(ins, outs, sems):
        send_sems, recv_sems, local_sems = sems
        x, y, c, sibling, chips = _mesh_place()
        me = (x, y, c)
        peers = [(*chip, c) for chip in chips] + [sibling]
        local = [pltpu.make_async_copy(ins[kk], outs[kk].at[_slot(me)], local_sems.at[kk]) for kk in range(n)]
        sends = [_remote(ins[kk], outs[kk].at[_slot(me)], send_sems, recv_sems, kk * 4 + j, peer)
                 for j, peer in enumerate(peers) for kk in range(n)]
        recvs = [_remote(ins[kk], outs[kk].at[_slot(peer)], send_sems, recv_sems, kk * 4 + j, peer)
                 for j, peer in enumerate(peers) for kk in range(n)]
        return local, sends, recvs

    def start(ins, outs, sems):
        local, sends, _ = copies(ins, outs, sems)
        _start_all(sends + local)

    def finish(ins, outs, sems):
        local, sends, recvs = copies(ins, outs, sems)
        for cp in recvs:
            cp.wait_recv()
        for cp in sends:
            cp.wait_send()
        for cp in local:
            cp.wait()

    return _Comm(arrs, [jax.ShapeDtypeStruct((N_DEV,) + a.shape, a.dtype) for a in arrs], _dma_sems(4 * n, 4 * n, n),
                 start, finish)


def _gather_fwd_comm(bufs):
    n = len(bufs)

    def copies(ins, outs, sems):
        send_sems, recv_sems = sems
        _, _, c, sibling, chips = _mesh_place()
        sends = [_remote(ins[kk].at[_slot((*chip, c))], outs[kk].at[_slot((*chip, c))], send_sems, recv_sems, kk * 3 + j, sibling)
                 for j, chip in enumerate(chips) for kk in range(n)]
        recvs = [_remote(ins[kk].at[_slot((*chip, c))], outs[kk].at[_slot((*chip, 1 - c))], send_sems, recv_sems, kk * 3 + j, sibling)
                 for j, chip in enumerate(chips) for kk in range(n)]
        return sends, recvs

    def start(ins, outs, sems):
        _start_all(copies(ins, outs, sems)[0])

    def finish(ins, outs, sems):
        sends, recvs = copies(ins, outs, sems)
        for cp in recvs:
            cp.wait_recv()
        for cp in sends:
            cp.wait_send()

    return _Comm(bufs, [jax.ShapeDtypeStruct(a.shape, a.dtype) for a in bufs], _dma_sems(3 * n, 3 * n), start, finish,
                 aliases=[(i, i) for i in range(n)])


def _core_swap_comm(arrs):
    n = len(arrs)

    def copies(ins, outs, sems):
        send_sems, recv_sems = sems
        _, _, c, sibling, _ = _mesh_place()
        return [_remote(ins[kk].at[2 * q + (1 - c)], outs[kk].at[q], send_sems, recv_sems, kk * N_CHIPS + q, sibling)
                for q in range(N_CHIPS) for kk in range(n)]

    def start(ins, outs, sems):
        _start_all(copies(ins, outs, sems))

    def finish(ins, outs, sems):
        cps = copies(ins, outs, sems)
        for cp in cps:
            cp.wait_recv()
        for cp in cps:
            cp.wait_send()

    return _Comm(arrs, [jax.ShapeDtypeStruct((N_CHIPS,) + a.shape[1:], a.dtype) for a in arrs],
                 _dma_sems(N_CHIPS * n, N_CHIPS * n), start, finish)


def _chip_swap_comm(arrs):
    n = len(arrs)
    per = N_CHIPS - 1

    def copies(ins, outs, sems):
        send_sems, recv_sems, local_sems = sems
        x, y, c, _, chips = _mesh_place()
        mine = 2 * x + y
        sends = [_remote(ins[kk].at[2 * chip[0] + chip[1]], outs[kk].at[mine], send_sems, recv_sems, kk * per + j, (*chip, c))
                 for j, chip in enumerate(chips) for kk in range(n)]
        recvs = [_remote(ins[kk].at[mine], outs[kk].at[2 * chip[0] + chip[1]], send_sems, recv_sems, kk * per + j, (*chip, c))
                 for j, chip in enumerate(chips) for kk in range(n)]
        local = [pltpu.make_async_copy(ins[kk].at[mine], outs[kk].at[mine], local_sems.at[kk]) for kk in range(n)]
        return local, sends, recvs

    def start(ins, outs, sems):
        local, sends, _ = copies(ins, outs, sems)
        _start_all(sends + local)

    def finish(ins, outs, sems):
        local, sends, recvs = copies(ins, outs, sems)
        for cp in recvs:
            cp.wait_recv()
        for cp in sends:
            cp.wait_send()
        for cp in local:
            cp.wait()

    return _Comm(arrs, [jax.ShapeDtypeStruct(a.shape, a.dtype) for a in arrs], _dma_sems(per * n, per * n, n), start, finish)


def _ada_mod(c_all, w_ada, b_cols, comms=None):
    nl, d, w = w_ada.shape

    def body(c_ref, w_ref, b_ref, m_ref, ca_ref):
        sub = _iota((SUBLANES, d), 0)
        cv = jnp.zeros((SUBLANES, d), F32)
        for b in range(N_DEV):
            cv = jnp.where(sub == b, c_ref[b], cv)
        ca = cv * _sigmoid(cv)
        ca_ref[...] = ca
        m_ref[...] = jnp.zeros_like(m_ref)
        for l in range(nl):
            ml = _mm_hi(ca, w_ref[l]) + b_ref[l:l + 1, :]
            for b in range(N_DEV):
                m_ref[b, l:l + 1, :] = _row(ml, b)

    return _call(
        body, comms, name="adaln_mod_columns", grid=(1,),
        in_specs=[_full(c_all.shape), _full(w_ada.shape), _full(b_cols.shape)],
        out_specs=[_full((N_DEV, SUBLANES, w)), _full((SUBLANES, d))],
        out_shape=[jax.ShapeDtypeStruct((N_DEV, SUBLANES, w), F32), jax.ShapeDtypeStruct((SUBLANES, d), F32)],
        args=(c_all, w_ada, b_cols))


def _ada_grad_adam(cact_t, dmods, w, m, v, comms=None):
    nl, d, wd = w.shape
    tr = _row_tile(d, wd, 8)
    nt = d // tr

    def body(c_ref, dm_ref, w_ref, m_ref, v_ref, g_ref, d_ref, mo_ref, vo_ref):
        cv = c_ref[...]
        dm = dm_ref[0]
        g = _col(cv, 0) * _row(dm, 0)
        for b in range(1, N_DEV):
            g = g + _col(cv, b) * _row(dm, b)
        delta, m2, v2 = _adam_math(w_ref[0], g, m_ref[0], v_ref[0])
        g_ref[0] = g
        d_ref[0] = delta
        mo_ref[0] = m2
        vo_ref[0] = v2

    blk = pl.BlockSpec((1, tr, wd), lambda i: (i // nt, i % nt, 0))
    return _call(
        body, comms, name="adaln_grad_adam", grid=(nl * nt,),
        in_specs=[pl.BlockSpec((tr, N_DEV), lambda i: (i % nt, 0)), pl.BlockSpec((1, N_DEV, wd), lambda i: (i // nt, 0, 0)),
                  blk, blk, blk],
        out_specs=[blk] * 4, out_shape=[jax.ShapeDtypeStruct((nl, d, wd), F32)] * 4,
        args=(cact_t, dmods, w, m, v))


REP_ROWS = ("norm_g", "dshift", "dscale", "dgate", "rg_conv_b", "rg_b_a", "rg_b_x", "rg_lambda", "ml_conv_b", "ml_norm_g",
            "ml_b_if")


def _pack_rows(arrays, d):
    n = len(arrays)
    rows = n + (-n) % SUBLANES

    def body(*refs):
        o_ref = refs[n]
        o_ref[...] = jnp.zeros(o_ref.shape, F32)
        for r, a_ref in enumerate(refs[:n]):
            o_ref[r:r + 1, 0:a_ref.shape[1]] = a_ref[0:1, :]

    return pl.pallas_call(
        body, name="pack_vectors", grid=(1,), in_specs=[_full(a.shape) for a in arrays], out_specs=_full((rows, d)),
        out_shape=jax.ShapeDtypeStruct((rows, d), F32), compiler_params=_params(1),
    )(*arrays)


def _sum_parts(recvs, name):
    def body(*refs):
        for r_ref, o_ref in zip(refs[:len(recvs)], refs[len(recvs):]):
            o_ref[...] = _sum_devices(r_ref).astype(o_ref.dtype)

    return pl.pallas_call(
        body, name=name, grid=(1,),
        in_specs=[_full(r.shape) for r in recvs], out_specs=[_full(r.shape[1:]) for r in recvs],
        out_shape=[jax.ShapeDtypeStruct(r.shape[1:], r.dtype) for r in recvs], compiler_params=_params(1),
    )(*recvs)


def _adam_replicated(vp, mp, params, nl):
    d = vp.shape[2]
    nr = len(REP_ROWS)
    names = list(params)
    mat_shape = params["rg_w_a"][0].shape[1:]
    mat_rows = mp.shape[0] // (2 * nl)

    def pieces(name):
        if name == "final_g":
            return [(lambda vp_ref, mp_ref: vp_ref[nl * nr:nl * nr + 1, :], (slice(0, 1), slice(None)))]
        out = []
        for l in range(nl):
            if name in ("rg_w_a", "rg_w_x"):
                at = (2 * l + (name == "rg_w_x")) * mat_rows
                out.append((lambda vp_ref, mp_ref, at=at: mp_ref[at:at + mat_rows, :].astype(F32).reshape(mat_shape), l))
            elif name == "b_ada":
                for j in range(3):
                    r = l * nr + 1 + j
                    out.append((lambda vp_ref, mp_ref, r=r: vp_ref[r:r + 1, :], (slice(l, l + 1), slice(j * d, (j + 1) * d))))
            else:
                r = l * nr + REP_ROWS.index(name)
                cols = slice(0, LANES) if name == "ml_b_if" else slice(None)
                out.append((lambda vp_ref, mp_ref, r=r, cols=cols: vp_ref[r:r + 1, cols], (slice(l, l + 1), slice(None))))
        return out

    def body(*refs):
        parts_ref, mp_ref, vp_ref = refs[0], refs[1], refs[-1]
        ins, outs = refs[2:2 + 3 * len(names)], refs[2 + 3 * len(names):-1]
        vp_ref[...] = _sum_devices(parts_ref)
        for pi, name in enumerate(names):
            w_ref, m_ref, v_ref = ins[3 * pi:3 * pi + 3]
            g_ref, d_ref, mo_ref, vo_ref = outs[4 * pi:4 * pi + 4]
            for get, idx in pieces(name):
                g = get(vp_ref, mp_ref)
                delta, m2, v2 = _adam_math(w_ref[idx], g, m_ref[idx], v_ref[idx])
                g_ref[idx] = g
                d_ref[idx] = delta
                mo_ref[idx] = m2
                vo_ref[idx] = v2

    flat = [a for name in names for a in params[name]]
    out_shape = [jax.ShapeDtypeStruct(params[name][0].shape, F32) for name in names for _ in range(4)]
    out_shape.append(jax.ShapeDtypeStruct(vp.shape[1:], F32))
    res = pl.pallas_call(
        body, name="adam_replicated", grid=(1,),
        in_specs=[_full(vp.shape), _full(mp.shape)] + [_full(a.shape) for a in flat],
        out_specs=[_full(o.shape) for o in out_shape], out_shape=out_shape, compiler_params=_params(1),
    )(vp, mp, *flat)
    return {name: res[4 * pi:4 * pi + 4] for pi, name in enumerate(names)}, res[-1]


class _Plan:
    def __init__(self):
        self.hosted, self.after = {}, {}

    def host(self, key, comm, then=None):
        self.hosted.setdefault(key, []).append(comm)
        if then is not None:
            self.after.setdefault(key, []).append(then)

    def comms(self, key):
        return self.hosted.pop(key, None)

    def done(self, key):
        for fn in self.after.pop(key, []):
            fn()

    def flush(self):
        while self.hosted:
            key = next(iter(self.hosted))
            _call(lambda: None, self.comms(key), name="exchange_after_%s_%d" % key, grid=(1,), in_specs=[], out_specs=[],
                  out_shape=[], args=())
            self.done(key)


VEC_TABLE = ("norm_g", "rg_conv_b", "rg_b_a", "rg_b_x", "rg_lambda", "ml_conv_b", "ml_norm_g")


def _vec_table(rep):
    rows = [rep[n] for n in VEC_TABLE]
    return jnp.stack(rows + [jnp.zeros_like(rows[0])] * (SUBLANES - len(rows)), axis=1)


def _layer_fwd(l, xl, mod3, wl, rep, plan, head=None):
    s, d = xl.shape
    t_big, t_mid = _tile_for(s, 512), _tile_for(s, 256)
    nh_ml = rep["ml_b_if"].shape[1] // 2
    vec = lambda name: _vec(rep["vecs"], l, VEC_TABLE.index(name))
    shift, scale, gate = (_vec(mod3, l, kk) for kk in range(3))
    hosted = lambda name: plan.comms((name, l)) if plan else None
    done = lambda name: plan.done((name, l)) if plan else None
    u, hbf = _in_fwd(xl, vec("norm_g"), scale, shift, wl["w_in_g"], 0, t_big, hosted("in_proj_fwd"))
    done("in_proj_fwd")
    h_rg, y_rg, *rg_gates = _rg_fwd(u, d, wl["rg_conv_w"], vec("rg_conv_b"), rep["rg_w_a_bf"][l], vec("rg_b_a"),
                                    rep["rg_w_x_bf"][l], vec("rg_b_x"), vec("rg_lambda"), t_mid, hosted("rglru_fwd"))
    done("rglru_fwd")
    q, k, v, gcol, pre = _ml_pre(u, d, wl["ml_conv_w"], vec("ml_conv_b"), wl["w_qkv"][0], wl["w_qkv"][1],
                                 wl["w_qkv"][2], wl["wif_pad"], wl["bif_pad"], t_mid, hosted("mlstm_proj_fwd"))
    done("mlstm_proj_fwd")
    grow = gcol[:, 0:16].T
    cell, y_ml, cs, ns, ms, mt = _ml_cell_fwd(q, k, v, gcol, grow, u, vec("ml_norm_g"), nh_ml, hosted("mlstm_cell_fwd"))
    done("mlstm_cell_fwd")
    res = _out_fwd(xl, y_rg, y_ml, gate, wl["w_out_g"], 0, t_big, hosted("out_proj_fwd"), head)
    done("out_proj_fwd")
    x_new, y = (res[0], res[1]) if head is None else (tuple(res[1:]), res[0])
    saved = dict(x=xl, u=u, hbf=hbf, h_rg=h_rg, y_rg=y_rg, q=q, k=k, v=v, gcol=gcol, grow=grow, cell=cell, y_ml=y_ml,
                 cs=cs, ns=ns, ms=ms, mt=mt, y=y, scale=scale, gate=gate, rg_gates=rg_gates, pre=pre)
    return x_new, saved


def _layer_bwd(l, dx, sv, wl, rep, plan, grads=None, split_last=False):
    s, d = dx.shape
    t_big, t_mid = _tile_for(s, 512), _tile_for(s, 256)
    nh_ml = rep["ml_b_if"].shape[1] // 2
    nd, _, _, w_cols = wl["w_in_g"].shape
    grads = {} if grads is None else grads
    vec = lambda name: _vec(rep["vecs"], l, VEC_TABLE.index(name))
    hosted = lambda name: plan.comms((name, l)) if plan else None
    done = lambda name: plan.done((name, l)) if plan else None
    dy_rg, dy_ml, gw_out, dgate = _out_bwd(dx, sv["gate"], sv["y"], sv["y_rg"], sv["y_ml"], wl["w_out_g"], 0, t_big,
                                           hosted("out_proj_bwd"))
    grads.update(w_out=gw_out)
    done("out_proj_bwd")
    dq, dk, dv, dgates, d_mlo, d_mlz, g_mlng = _ml_cell_bwd(
        dy_ml, sv["u"], sv["cell"], sv["q"], sv["k"], sv["v"], sv["gcol"], sv["grow"], sv["mt"], sv["cs"], sv["ns"],
        sv["ms"], vec("ml_norm_g"), nh_ml, hosted("mlstm_cell_bwd"))
    done("mlstm_cell_bwd")
    d_mlx, g_wq, g_wk, g_wv, g_wift, g_bif, g_mlcw, g_mlcb = _ml_pre_bwd(
        dq, dk, dv, dgates, sv["gcol"], sv["u"], sv["pre"], sv["q"], sv["k"], sv["v"], wl["ml_conv_w"],
        wl["w_qkv"][0], wl["w_qkv"][1], wl["w_qkv"][2], wl["wift_pad"], t_mid, hosted("mlstm_proj_bwd"))
    done("mlstm_proj_bwd")
    d_rgx, d_rgz, g_wa, g_wx, g_ba, g_bx, g_lam, g_rgcw, g_rgcb = _rg_bwd(
        dy_rg, sv["u"], sv["h_rg"], sv["rg_gates"], wl["rg_conv_w"], rep["rg_w_a_bf"][l], rep["rg_w_x_bf"][l],
        vec("rg_lambda"), t_mid, hosted("rglru_bwd"))
    grads.update(w_qkv=jnp.stack([g_wq, g_wk, g_wv]), rg_conv_w=g_rgcw[0:CONV_WIDTH], ml_conv_w=g_mlcw[0:CONV_WIDTH],
                 wif_t=g_wift[0:8], rg_w_a=g_wa, rg_w_x=g_wx)
    acc = dict(dgate=dgate, rg_conv_b=g_rgcb, rg_b_a=g_ba, rg_b_x=g_bx, rg_lambda=g_lam, ml_conv_b=g_mlcb,
               ml_b_if=g_bif, ml_norm_g=g_mlng)
    done("rglru_bwd")
    pieces = [d_rgx, d_rgz, d_mlx, d_mlo, d_mlz]
    grads.update(w_in=_in_bwd_w(pieces, sv["hbf"], w_cols, tuple(range(nd)), _tile_for(s, 1024), hosted("in_proj_bwd_w")))
    done("in_proj_bwd_w")
    n_tiles = s // t_mid
    counts = [n_tiles // 8, n_tiles - n_tiles // 8] if split_last and n_tiles >= 8 else [n_tiles]
    in_args = (pieces, sv["x"], dx, vec("norm_g"), sv["scale"], wl["w_in_g"], 0, t_mid)
    res, at = None, 0
    for key, count in zip(("in_proj_bwd_x", "in_proj_bwd_x_rest"), counts):
        res = _in_bwd(*in_args, hosted(key), (at, count), res)
        done(key)
        at += count
    dx, dscale, dshift, g_ng = res
    acc.update(norm_g=g_ng, dshift=dshift, dscale=dscale)
    grads.update(acc=acc, dmod=jnp.concatenate([dshift[0:1], dscale[0:1], dgate[0:1]], axis=1))
    return dx, grads


def _full_qkv(qkv_g, d):
    nd, _, rows3, dh = qkv_g.shape
    nh = d // dh
    rsh = rows3 // (3 * nh)
    return qkv_g.reshape(nd, 3, nh, rsh, dh).transpose(1, 2, 0, 3, 4).reshape(3, nh, nd * rsh, dh)


def _small_weights(small, l, ml_b_if):
    nd = small.shape[0]
    sm = small[:, l]
    cw = sm[:, 0:2 * CONV_WIDTH].reshape(nd, 2, CONV_WIDTH, LANES).transpose(1, 2, 0, 3).reshape(2, CONV_WIDTH, nd * LANES)
    if_rows = (sm.shape[1] - 2 * CONV_WIDTH) * LANES // 8
    wif_t = sm[:, 2 * CONV_WIDTH:].reshape(nd, 8, if_rows).transpose(1, 0, 2).reshape(8, nd * if_rows)
    wift_pad = jnp.pad(wif_t, ((0, LANES - 8), (0, 0))).astype(BF16)
    return dict(rg_conv_w=cw[0], ml_conv_w=cw[1], wift_pad=wift_pad, wif_pad=wift_pad.T,
                bif_pad=jnp.pad(ml_b_if[l], (0, LANES - 8)).reshape(1, LANES))


def kernel(x, c, norm_g, w_ada, b_ada, w_in, rg_conv_w, rg_conv_b, rg_w_a, rg_b_a, rg_w_x, rg_b_x, rg_lambda, ml_conv_w, ml_conv_b, ml_w_q, ml_w_k, ml_w_v, ml_w_if, ml_b_if, ml_norm_g, w_out, final_g, loss_target, m_norm_g, m_w_ada, m_b_ada, m_w_in, m_rg_conv_w, m_rg_conv_b, m_rg_w_a, m_rg_b_a, m_rg_w_x, m_rg_b_x, m_rg_lambda, m_ml_conv_w, m_ml_conv_b, m_ml_w_q, m_ml_w_k, m_ml_w_v, m_ml_w_if, m_ml_b_if, m_ml_norm_g, m_w_out, m_final_g, v_norm_g, v_w_ada, v_b_ada, v_w_in, v_rg_conv_w, v_rg_conv_b, v_rg_w_a, v_rg_b_a, v_rg_w_x, v_rg_b_x, v_rg_lambda, v_ml_conv_w, v_ml_conv_b, v_ml_w_q, v_ml_w_k, v_ml_w_v, v_ml_w_if, v_ml_b_if, v_ml_norm_g, v_w_out, v_final_g):
    given = dict(locals())
    nl = w_in.shape[0]
    d = x.shape[2]
    rep = {n: given[n] for n in REPLICATED}
    rep.update(rg_w_a_bf=rg_w_a.astype(BF16), rg_w_x_bf=rg_w_x.astype(BF16))
    bf = lambda a: a.astype(BF16)

    def qkv_shard(prefix):
        return jnp.stack([given[prefix + "ml_w_q"], given[prefix + "ml_w_k"], given[prefix + "ml_w_v"]], axis=1).reshape(
            nl, -1, ml_w_q.shape[-1])

    def small_shard(prefix):
        return _small_pack(given[prefix + "rg_conv_w"], given[prefix + "ml_conv_w"], given[prefix + "ml_w_if"])

    plan = _Plan()
    qkv = qkv_shard("")
    first_ici = _gather_ici_comm([bf(w_in[0:1]), small_shard("")])
    condition = _exchange_comm([jnp.broadcast_to(c, (SUBLANES, d))], True)
    _run_comms([first_ici, condition], "gather_first")
    first_fwd = _gather_fwd_comm(first_ici.results)
    wcols = w_ada.shape[2]
    me = 4 * lax.axis_index("x") + 2 * lax.axis_index("y") + lax.axis_index("c")
    b_cols = jnp.pad(lax.dynamic_slice_in_dim(b_ada, me * wcols, wcols, axis=1), ((0, SUBLANES - nl), (0, 0)))
    mod_cols, cact_all = _ada_mod(condition.results[0], w_ada, b_cols, [first_fwd])
    w_in_first, small = first_fwd.results
    wl = [_small_weights(small, l, ml_b_if) for l in range(nl)]
    wl[0]["w_in_g"] = w_in_first

    def gather_behind(arrs, ici_host, fwd_host, then):
        ici = _gather_ici_comm(arrs)

        def pass_on():
            fwd = _gather_fwd_comm(ici.results)
            plan.host(fwd_host, fwd, lambda: then(fwd.results))

        plan.host(ici_host, ici, pass_on)

    def got_out(l):
        return lambda r: wl[l].update(w_out_g=r[0], w_qkv=_full_qkv(r[1], d))

    gather_behind([bf(w_out[0:1]), bf(qkv[0:1])], ("in_proj_fwd", 0), ("rglru_fwd", 0), got_out(0))
    for l in range(1, nl):
        gather_behind([bf(w_in[l:l + 1])], ("rglru_fwd", l - 1), ("mlstm_cell_fwd", l - 1),
                      lambda r, l=l: wl[l].update(w_in_g=r[0]))
        gather_behind([bf(w_out[l:l + 1]), bf(qkv[l:l + 1])], ("mlstm_cell_fwd", l - 1), ("out_proj_fwd", l - 1), got_out(l))

    mod_blocks = _exchange([mod_cols], False, "scatter_modulation")[0]
    mod3 = mod_blocks[:, 0:nl].transpose(1, 0, 2).reshape(nl, 3, d)
    rep["vecs"] = _vec_table(rep)
    saved, xl = [], x[0]
    for l in range(nl):
        head = (final_g.reshape(1, -1), loss_target[0]) if l == nl - 1 else None
        xl, sv = _layer_fwd(l, xl, mod3, wl[l], rep, plan, head)
        saved.append(sv)
    grad_x, loss_p, g_final = xl

    keys = ("w_in", "w_out", "w_qkv", "small")
    parity = lax.axis_index("c").astype(jnp.int32).reshape(1)
    grads, recv = [None] * nl, [None] * nl

    def small_parts(g):
        return [bf(_qkv_slots(g["w_qkv"], N_DEV)), bf(_small_slots(g))]

    def reduce_behind(l, host_layer):
        parts = [grads[l]["w_in"], grads[l]["w_out"]]
        swap = _core_swap_comm(parts)
        direct = _exchange_comm(small_parts(grads[l]), False)

        def summed():
            sums = [_pair_sum(a, o, parity, "pair_sum_%s_layer%d" % (key, l)) for key, a, o in zip(keys, parts, swap.results)]
            big = _chip_swap_comm([sums[0]])
            rest = _chip_swap_comm([sums[1]])
            plan.host(("mlstm_cell_bwd", host_layer), big)
            plan.host(("rglru_bwd", host_layer), rest,
                      lambda: recv.__setitem__(l, big.results + rest.results + direct.results))

        plan.host(("out_proj_bwd", host_layer), swap, summed)
        plan.host(("mlstm_proj_bwd", host_layer), direct)

    first, own = {}, {}

    def reduce_own(names, parts_fn, ready_key, swap_key, chip_key):
        def go():
            parts = parts_fn()
            swap = _core_swap_comm(parts)

            def summed():
                sums = [_pair_sum(a, o, parity, "pair_sum_%s_layer0" % n) for n, a, o in zip(names, parts, swap.results)]
                chip = _chip_swap_comm(sums)
                plan.host(chip_key, chip, lambda: own.update(zip(names, chip.results)))

            plan.host(swap_key, swap, summed)

        plan.after.setdefault(ready_key, []).append(go)

    reduce_own(["w_out"], lambda: [first["w_out"]], ("out_proj_bwd", 0), ("mlstm_cell_bwd", 0), ("mlstm_proj_bwd", 0))
    reduce_own(["w_in"], lambda: [first["w_in"]], ("in_proj_bwd_w", 0), ("in_proj_bwd_x", 0), ("in_proj_bwd_x_rest", 0))

    def small_own():
        direct = _exchange_comm(small_parts(first), False)
        plan.host(("in_proj_bwd_w", 0), direct, lambda: own.update(w_qkv=direct.results[0], small=direct.results[1]))

    plan.after.setdefault(("rglru_bwd", 0), []).append(small_own)

    matrices = {}

    def reduce_matrices():
        layers = [grads[l] if l > 0 else first for l in range(nl)]
        mp = jnp.stack([jnp.stack([g["rg_w_a"], g["rg_w_x"]]) for g in layers]).reshape(N_DEV, -1, LANES).astype(BF16)
        scatter = _exchange_comm([mp], False)

        def summed():
            gather = _exchange_comm(_sum_parts(scatter.results, "sum_replicated_matrices"), True)
            plan.host(("in_proj_bwd_x", 0), gather, lambda: matrices.update(mp=gather.results[0].reshape(-1, LANES)))

        plan.host(("in_proj_bwd_w", 0), scatter, summed)

    plan.after.setdefault(("rglru_bwd", 0), []).append(reduce_matrices)

    for l in reversed(range(nl)):
        if l > 0:
            grad_x, grads[l] = _layer_bwd(l, grad_x, saved[l], wl[l], rep, plan)
            reduce_behind(l, l - 1)
        else:
            grad_x, grads[l] = _layer_bwd(l, grad_x, saved[l], wl[l], rep, plan, first, True)
    plan.flush()
    recv[0] = [own[key] for key in keys]

    shard = {p: dict(w_in=given[p + "w_in"], w_out=given[p + "w_out"], w_qkv=qkv_shard(p), small=small_shard(p))
             for p in ("", "m_", "v_")}
    res = {}
    for ki, key in enumerate(keys):
        res[key] = _reduce_adam([recv[l][ki] for l in range(nl)], shard[""][key], shard["m_"][key], shard["v_"][key],
                                "reduce_adam_" + key)

    dmods = jnp.concatenate([grads[l]["dmod"] for l in range(nl)], axis=0)
    dmod_blocks = jnp.pad(dmods.reshape(nl, N_DEV, wcols).transpose(1, 0, 2), ((0, 0), (0, SUBLANES - nl), (0, 0)))
    vp = _pack_rows([grads[l]["acc"][n] for l in range(nl) for n in REP_ROWS] + [g_final, loss_p], d)
    (dmod_recv,), (vp_all,) = _run_comms([_exchange_comm([dmod_blocks], False), _exchange_comm([vp], True)], "tail_exchange")
    res["w_ada"] = _ada_grad_adam(cact_all.T, dmod_recv[:, 0:nl].transpose(1, 0, 2), w_ada, m_w_ada, v_w_ada)
    mp_r = matrices["mp"]
    lanes = lambda a: jnp.pad(a, ((0, 0), (0, LANES - a.shape[1])))
    shaped = dict(ml_b_if=lanes, final_g=lambda a: a.reshape(1, d))
    names = [n for n in REPLICATED if n != "b_ada"] + ["b_ada"]
    rep_res, vp_r = _adam_replicated(vp_all, mp_r, {n: tuple(shaped.get(n, lambda a: a)(given[p + n]) for p in ("", "m_", "v_"))
                                                    for n in names}, nl)
    unshaped = dict(ml_b_if=lambda a: a[:, 0:ml_b_if.shape[1]], final_g=lambda a: a.reshape(d))
    rep_out = [{n: unshaped.get(n, lambda a: a)(rep_res[n][kind]) for n in names} for kind in range(4)]
    loss = vp_r[nl * len(REP_ROWS) + 1, 0]

    if_rows = ml_w_if.shape[1]
    order = ("norm_g", "w_ada", "b_ada", "w_in", "rg_conv_w", "rg_conv_b", "rg_w_a", "rg_b_a", "rg_w_x", "rg_b_x",
             "rg_lambda", "ml_conv_w", "ml_conv_b", "ml_w_q", "ml_w_k", "ml_w_v", "ml_w_if", "ml_b_if", "ml_norm_g",
             "w_out", "final_g")
    outs = [loss, grad_x[None]]
    for kind in range(4):
        qkv_k = res["w_qkv"][kind].reshape((nl, 3) + ml_w_q.shape[1:])
        rg_cw, ml_cw, wif = _small_unpack(res["small"][kind], if_rows)
        sharded = dict(w_ada=res["w_ada"][kind], w_in=res["w_in"][kind], w_out=res["w_out"][kind], ml_w_q=qkv_k[:, 0],
                       ml_w_k=qkv_k[:, 1], ml_w_v=qkv_k[:, 2], rg_conv_w=rg_cw, ml_conv_w=ml_cw, ml_w_if=wif)
        for n in order:
            outs.append(sharded[n] if n in sharded else rep_out[kind][n])
    return tuple(outs)
```

```python
import functools

import jax
import jax.numpy as jnp
from jax import lax
from jax.experimental import pallas as pl
from jax.experimental.pallas import tpu as pltpu

F32 = jnp.float32
BF16 = jnp.bfloat16
MESH_AXES = ("x", "y", "c")
N_DEV = 8
EPS = 1e-6
RG_C = 8.0
ML_CHUNK = 128
CONV_WIDTH = 4
ADAM_LR = 0.001
ADAM_B1 = 0.9
ADAM_B2 = 0.999
ADAM_EPS = 1e-08
ADAM_WD = 0.01
ADAM_STEP = 10
NEG_BIG = -1e30
LANES = 128
SUBLANES = 8
VMEM_LIMIT = 56 * 1024 * 1024
HI = lax.Precision.HIGHEST


def _params(n_grid):
    return pltpu.CompilerParams(dimension_semantics=("arbitrary",) * n_grid, vmem_limit_bytes=VMEM_LIMIT)


def _mm(a, b):
    return jnp.dot(a.astype(BF16), b.astype(BF16), preferred_element_type=F32)


def _mm_nt(a, b):
    return lax.dot_general(a.astype(BF16), b.astype(BF16), (((1,), (1,)), ((), ())), preferred_element_type=F32)


def _mm_tn(a, b):
    return lax.dot_general(a.astype(BF16), b.astype(BF16), (((0,), (0,)), ((), ())), preferred_element_type=F32)


def _mm_hi(a, b):
    return jnp.dot(a, b, precision=HI, preferred_element_type=F32)


def _sigmoid(x):
    return 1.0 / (1.0 + jnp.exp(-x))


def _softplus(x):
    return jnp.maximum(x, 0.0) + jnp.log(1.0 + jnp.exp(-jnp.abs(x)))


def _neg_expm1(x):
    poly = -x * (1.0 + x * (0.5 + x * (1.0 / 6.0 + x * (1.0 / 24.0 + x * (1.0 / 120.0)))))
    return jnp.where(jnp.abs(x) < 0.05, poly, 1.0 - jnp.exp(x))


def _iota(shape, dim):
    return lax.broadcasted_iota(jnp.int32, shape, dim)


def _colsum(x):
    return jnp.sum(x, axis=0, keepdims=True)


def _rowsum(x):
    return jnp.sum(x, axis=1, keepdims=True)


def _col(x, j):
    return _rowsum(jnp.where(_iota(x.shape, 1) == j, x, 0.0))


def _row(x, j):
    return _colsum(jnp.where(_iota(x.shape, 0) == j, x, 0.0))


def _shift_down(x, j, prev8):
    if j == 0:
        return x
    t = x.shape[0]
    main = jnp.where(_iota(x.shape, 0) >= j, pltpu.roll(x, j, 0), 0.0)
    fix = jnp.where(_iota(prev8.shape, 0) < j, pltpu.roll(prev8, j, 0), 0.0)
    return jnp.concatenate([main[0:SUBLANES] + fix, main[SUBLANES:t]], axis=0)


def _shift_up(x, j, next8):
    if j == 0:
        return x
    t = x.shape[0]
    main = jnp.where(_iota(x.shape, 0) < t - j, pltpu.roll(x, t - j, 0), 0.0)
    fix = jnp.where(_iota(next8.shape, 0) >= SUBLANES - j, pltpu.roll(next8, SUBLANES - j, 0), 0.0)
    return jnp.concatenate([main[0:t - SUBLANES], main[t - SUBLANES:t] + fix], axis=0)


def _conv(x, prev8, w_ref):
    y = w_ref[CONV_WIDTH - 1:CONV_WIDTH, :] * x
    for j in range(1, CONV_WIDTH):
        y = y + w_ref[CONV_WIDTH - 1 - j:CONV_WIDTH - j, :] * _shift_down(x, j, prev8)
    return y


def _conv_bwd(dy, x, next8, w_ref, gw_ref):
    dx = None
    for j in range(CONV_WIDTH):
        k = CONV_WIDTH - 1 - j
        up = _shift_up(dy, j, next8)
        gw_ref[k:k + 1, :] += _colsum(up * x)
        term = w_ref[k:k + 1, :] * up
        dx = term if dx is None else dx + term
    return dx


def _scan_into(a, b, carry, out_ref, reverse):
    t, c = a.shape
    groups = t // SUBLANES
    a3 = a.reshape(groups, SUBLANES, c)
    b3 = b.reshape(groups, SUBLANES, c)
    sub = _iota(a3.shape, 1)
    for step in (1, 2, 4):
        keep = sub < SUBLANES - step if reverse else sub >= step
        shift = SUBLANES - step if reverse else step
        a_s = jnp.where(keep, pltpu.roll(a3, shift, 1), 1.0)
        b_s = jnp.where(keep, pltpu.roll(b3, shift, 1), 0.0)
        b3 = a3 * b_s + b3
        a3 = a3 * a_s
    for g in (reversed(range(groups)) if reverse else range(groups)):
        rows = slice(g * SUBLANES, (g + 1) * SUBLANES)
        out_ref[rows, :] = b3[g] + a3[g] * carry
        edge = g * SUBLANES if reverse else (g + 1) * SUBLANES - 1
        carry = out_ref[edge:edge + 1, :]


def _blockdiag(x, w_ref, transpose_w=False):
    nh, dh, _ = w_ref.shape
    outs = []
    for h in range(nh):
        xs = x[:, h * dh:(h + 1) * dh]
        outs.append(_mm_nt(xs, w_ref[h]) if transpose_w else _mm(xs, w_ref[h]))
    return jnp.concatenate(outs, axis=1)


def _rg_gates(xc, wa_ref, ba_ref, wx_ref, bx_ref, lam_ref):
    r = _sigmoid(_blockdiag(xc, wa_ref) + ba_ref[...])
    ig = _sigmoid(_blockdiag(xc, wx_ref) + bx_ref[...])
    sp = _softplus(-lam_ref[...])
    log_a = -RG_C * r * sp
    a = jnp.exp(log_a)
    beta = jnp.sqrt(_neg_expm1(2.0 * log_a))
    return r, ig, sp, a, beta


def _bcast8(row):
    return jnp.broadcast_to(row, (SUBLANES, row.shape[1]))


def _full(shape):
    nd = len(shape)
    return pl.BlockSpec(shape, lambda *_: (0,) * nd)


class _Comm:
    def __init__(self, arrays, out_shapes, sems, start, finish, aliases=()):
        self.arrays, self.out_shapes, self.sems = list(arrays), list(out_shapes), list(sems)
        self.start, self.finish, self.aliases = start, finish, tuple(aliases)
        self.results = None


class _RowOf:
    def __init__(self, ref, k):
        self.ref, self.k = ref, k

    def __getitem__(self, idx):
        cols = slice(None) if idx is Ellipsis else idx[1]
        return self.ref[0, self.k:self.k + 1, cols]


class _PartOf:
    def __init__(self, ref, rows=None, cols=None, lead=None):
        self.ref, self.rows, self.cols, self.lead = ref, rows, cols, lead
        if rows is not None:
            self.shape = (rows.stop - rows.start,) + tuple(ref.shape[1:])

    def _at(self, idx):
        if self.lead is not None:
            return (self.lead,) + tuple(idx[1:])
        if self.cols is not None:
            return (slice(None), self.cols)
        return (self.rows, slice(None) if idx is Ellipsis else idx[1])

    def __getitem__(self, idx):
        return self.ref[self._at(idx)]

    def __setitem__(self, idx, value):
        self.ref[self._at(idx)] = value


def _vec(table, layer, k):
    return ("row", table, layer, k)


def _is_row(arg):
    return isinstance(arg, tuple) and len(arg) == 4 and arg[0] == "row"


def _call(body, comms, *, name, grid, in_specs, out_specs, out_shape, args, scratch_shapes=(), aliases=None):
    comms = [cm for cm in (comms or []) if cm is not None]
    rows = {i: a[3] for i, a in enumerate(args) if _is_row(a)}
    in_specs = [pl.BlockSpec((1,) + a[1].shape[1:], functools.partial(lambda layer, *_: (layer, 0, 0), a[2]))
                if _is_row(a) else sp for a, sp in zip(args, in_specs)]
    args = tuple(a[1] if _is_row(a) else a for a in args)
    n_in, n_out, n_sc = len(args), len(out_shape), len(scratch_shapes)
    c_arrays = [a for cm in comms for a in cm.arrays]
    c_outs = [o for cm in comms for o in cm.out_shapes]
    c_sems = [sm for cm in comms for sm in cm.sems]
    aliases, a_at, o_at = dict(aliases or {}), n_in, n_out
    for cm in comms:
        for (i, j) in cm.aliases:
            aliases[a_at + i] = o_at + j
        a_at += len(cm.arrays)
        o_at += len(cm.out_shapes)

    def wrapped(*refs):
        ins, c_in = refs[:n_in], refs[n_in:n_in + len(c_arrays)]
        ins = [_RowOf(r, rows[i]) if i in rows else r for i, r in enumerate(ins)]
        at = n_in + len(c_arrays)
        outs, c_out = refs[at:at + n_out], refs[at + n_out:at + n_out + len(c_outs)]
        at += n_out + len(c_outs)
        scr, sems = refs[at:at + n_sc], refs[at + n_sc:]
        views, ia, io, isem = [], 0, 0, 0
        for cm in comms:
            views.append((c_in[ia:ia + len(cm.arrays)], c_out[io:io + len(cm.out_shapes)], sems[isem:isem + len(cm.sems)]))
            ia, io, isem = ia + len(cm.arrays), io + len(cm.out_shapes), isem + len(cm.sems)
        if comms:
            @pl.when(pl.program_id(0) == 0)
            def _():
                for cm, view in zip(comms, views):
                    cm.start(*view)

        body(*ins, *outs, *scr)
        if comms:
            @pl.when(pl.program_id(0) == grid[0] - 1)
            def _():
                for cm, view in zip(comms, views):
                    cm.finish(*view)

    hbm = pl.BlockSpec(memory_space=pl.ANY)
    res = pl.pallas_call(
        wrapped, name=name, grid=grid,
        in_specs=list(in_specs) + [hbm] * len(c_arrays), out_specs=list(out_specs) + [hbm] * len(c_outs),
        out_shape=list(out_shape) + c_outs, scratch_shapes=list(scratch_shapes) + c_sems,
        input_output_aliases=aliases, compiler_params=_params(len(grid)),
    )(*args, *c_arrays)
    at = n_out
    for cm in comms:
        cm.results = list(res[at:at + len(cm.out_shapes)])
        at += len(cm.out_shapes)
    return list(res[:n_out])


def _join_columns(w_ref, wcat):
    nd, _, _, w = w_ref.shape

    @pl.when(pl.program_id(0) == 0)
    def _():
        for j in range(nd):
            wcat[:, j * w:(j + 1) * w] = w_ref[j, 0]


def _in_fwd(x, ng, scale, shift, w_in_g, layer, tile, comms=None):
    s, d = x.shape
    nd, _, _, w = w_in_g.shape

    def body(x_ref, ng_ref, sc_ref, sh_ref, w_ref, u_ref, h_ref, wcat):
        _join_columns(w_ref, wcat)
        xv = x_ref[...]
        rs = lax.rsqrt(jnp.mean(xv * xv, axis=1, keepdims=True) + EPS)
        hb = (xv * rs * ng_ref[...] * (1.0 + sc_ref[...]) + sh_ref[...]).astype(BF16)
        h_ref[...] = hb
        u_ref[...] = jnp.dot(hb, wcat[...], preferred_element_type=F32)

    return _call(
        body, comms, name="in_proj_fwd", grid=(s // tile,),
        in_specs=[pl.BlockSpec((tile, d), lambda i: (i, 0)), _full((1, d)), _full((1, d)), _full((1, d)),
                  pl.BlockSpec((nd, 1, d, w), lambda i: (0, layer, 0, 0), pipeline_mode=pl.Buffered(1))],
        out_specs=[pl.BlockSpec((tile, nd * w), lambda i: (i, 0)), pl.BlockSpec((tile, d), lambda i: (i, 0))],
        out_shape=[jax.ShapeDtypeStruct((s, nd * w), F32), jax.ShapeDtypeStruct((s, d), BF16)],
        scratch_shapes=[pltpu.VMEM((d, nd * w), BF16)],
        args=(x, ng, scale, shift, w_in_g))


def _rg_fwd(u, d, conv_w, conv_b, w_a, b_a, w_x, b_x, lam, tile, comms=None):
    s = u.shape[0]

    def body(x_ref, z_ref, cw_ref, cb_ref, wa_ref, ba_ref, wx_ref, bx_ref, lam_ref,
             h_ref, y_ref, xc_ref, r_ref, i_ref, a_ref, beta_ref, prev8, hcar):
        @pl.when(pl.program_id(0) == 0)
        def _():
            prev8[...] = jnp.zeros_like(prev8)
            hcar[...] = jnp.zeros_like(hcar)

        x = x_ref[...]
        xc = _conv(x, prev8[...], cw_ref) + cb_ref[...]
        prev8[...] = x[tile - SUBLANES:tile, :]
        r, ig, _, a, beta = _rg_gates(xc, wa_ref, ba_ref, wx_ref, bx_ref, lam_ref)
        xc_ref[...] = xc
        r_ref[...] = r
        i_ref[...] = ig
        a_ref[...] = a
        beta_ref[...] = beta
        _scan_into(a, beta * ig * xc, hcar[SUBLANES - 1:SUBLANES, :], h_ref, False)
        h = h_ref[...]
        hcar[...] = h[tile - SUBLANES:tile, :]
        z = z_ref[...]
        y_ref[...] = (h * z * _sigmoid(z)).astype(BF16)

    vec = _full((1, d))
    return _call(
        body, comms, name="rglru_fwd", grid=(s // tile,),
        in_specs=[pl.BlockSpec((tile, d), lambda i: (i, 0)), pl.BlockSpec((tile, d), lambda i: (i, 1)),
                  _full(conv_w.shape), vec, _full(w_a.shape), vec, _full(w_x.shape), vec, vec],
        out_specs=[pl.BlockSpec((tile, d), lambda i: (i, 0))] * 7,
        out_shape=[jax.ShapeDtypeStruct((s, d), F32), jax.ShapeDtypeStruct((s, d), BF16)] + [jax.ShapeDtypeStruct((s, d), F32)] * 5,
        scratch_shapes=[pltpu.VMEM((SUBLANES, d), F32), pltpu.VMEM((SUBLANES, d), F32)],
        args=(u, u, conv_w, conv_b, w_a, b_a, w_x, b_x, lam))


def _ml_pre(u, d, conv_w, conv_b, w_q, w_k, w_v, wif, bif, tile, comms=None):
    s = u.shape[0]
    nh = w_q.shape[0]

    def body(x_ref, cw_ref, cb_ref, wq_ref, wk_ref, wv_ref, wif_ref, bif_ref, q_ref, k_ref, v_ref, g_ref, pre_ref, prev8):
        @pl.when(pl.program_id(0) == 0)
        def _():
            prev8[...] = jnp.zeros_like(prev8)

        x = x_ref[...]
        pre = _conv(x, prev8[...], cw_ref) + cb_ref[...]
        prev8[...] = x[tile - SUBLANES:tile, :]
        xc = pre * _sigmoid(pre)
        q = _blockdiag(xc, wq_ref)
        k = _blockdiag(xc, wk_ref)
        v = _blockdiag(x, wv_ref)
        pre_ref[...] = pre
        q_ref[...] = q
        k_ref[...] = k
        v_ref[...] = v
        g = _mm(q, wif_ref[0:d, :]) + _mm(k, wif_ref[d:2 * d, :]) + _mm(v, wif_ref[2 * d:3 * d, :]) + bif_ref[...]
        lane = _iota(g.shape, 1)
        gl = jnp.where(lane < 4, g, jnp.where(lane < 8, -_softplus(-g), 0.0))
        tri = jnp.where(_iota((ML_CHUNK, ML_CHUNK), 1) <= _iota((ML_CHUNK, ML_CHUNK), 0), 1.0, 0.0)
        cums = [_mm_hi(tri, gl[c * ML_CHUNK:(c + 1) * ML_CHUNK, :]) for c in range(tile // ML_CHUNK)]
        cum = cums[0] if len(cums) == 1 else jnp.concatenate(cums, axis=0)
        g_ref[...] = gl + jnp.where((lane >= 8) & (lane < 12), pltpu.roll(cum, 4, 1), 0.0)

    vec = _full((1, d))
    return _call(
        body, comms, name="mlstm_proj_fwd", grid=(s // tile,),
        in_specs=[pl.BlockSpec((tile, d), lambda i: (i, 2)), _full(conv_w.shape), vec,
                  _full(w_q.shape), _full(w_k.shape), _full(w_v.shape), _full(wif.shape), _full((1, LANES))],
        out_specs=[pl.BlockSpec((tile, d), lambda i: (i, 0))] * 3 + [pl.BlockSpec((tile, LANES), lambda i: (i, 0)),
                                                                     pl.BlockSpec((tile, d), lambda i: (i, 0))],
        out_shape=[jax.ShapeDtypeStruct((s, d), F32)] * 3 + [jax.ShapeDtypeStruct((s, LANES), F32),
                                                             jax.ShapeDtypeStruct((s, d), F32)],
        scratch_shapes=[pltpu.VMEM((SUBLANES, d), F32)],
        args=(u, conv_w, conv_b, w_q, w_k, w_v, wif, bif))


CELL_CHUNKS_PER_STEP = 4
CELL_BWD_CHUNKS_PER_STEP = 2


def _cell_chunk(h, nh, q_ref, k_ref, v_ref, gc, gr, m_prev, c_h, n_h, m_t=None, r0=0):
    lc = ML_CHUNK
    dh = q_ref.shape[1] // nh
    sl = slice(h * dh, (h + 1) * dh)
    qh = q_ref[r0:r0 + lc, sl]
    kh = k_ref[r0:r0 + lc, sl] * (dh ** -0.5)
    vh = v_ref[r0:r0 + lc, sl]
    li_c = _col(gc, h)
    b_c = _col(gc, 8 + h)
    lib_r = _row(gr, h) - _row(gr, 8 + h)
    b_last = _colsum(jnp.where(_iota((lc, 1), 0) == lc - 1, b_c, 0.0))
    causal = _iota((lc, lc), 1) <= _iota((lc, lc), 0)
    dmat = jnp.where(causal, b_c + lib_r, NEG_BIG)
    m_inter = b_c + m_prev
    if m_t is None:
        m_t = jnp.maximum(m_inter, jnp.max(dmat, axis=1, keepdims=True))
    w_intra = jnp.exp(dmat - m_t)
    w_inter = jnp.exp(m_inter - m_t)
    amat = _mm_nt(qh, kh)
    smat = amat * w_intra
    qc = _mm(qh, c_h)
    qn = _rowsum(qh * n_h)
    den = _rowsum(smat) + w_inter * qn
    gst = b_last - b_c + li_c
    m_new = jnp.maximum(b_last + m_prev, jnp.max(gst, axis=0, keepdims=True))
    w_state = jnp.exp(gst - m_new)
    decay = jnp.exp(b_last + m_prev - m_new)
    return dict(sl=sl, qh=qh, kh=kh, vh=vh, m_t=m_t, w_intra=w_intra, w_inter=w_inter, smat=smat, qc=qc, qn=qn,
                den=den, m_new=m_new, w_state=w_state, decay=decay)


def _ml_cell_fwd(q, k, v, gcol, grow, u, ng, nh, comms=None):
    s, d = q.shape
    lc = ML_CHUNK
    nc = s // lc
    dh = d // nh

    per = CELL_CHUNKS_PER_STEP if nc % CELL_CHUNKS_PER_STEP == 0 else 1

    def body(q_ref, k_ref, v_ref, gc_ref, gr_ref, o_ref, z_ref, ng_ref,
             cell_ref, y_ref, cs_ref, ns_ref, ms_ref, mt_ref, c_sc, n_sc, m_sc):
        @pl.when(pl.program_id(0) == 0)
        def _():
            c_sc[...] = jnp.zeros_like(c_sc)
            n_sc[...] = jnp.zeros_like(n_sc)
            m_sc[...] = jnp.zeros_like(m_sc)

        lane = _iota((lc, LANES), 1)
        for cc in range(per):
            rows = slice(cc * lc, (cc + 1) * lc)
            gc = gc_ref[rows, :]
            gr = gr_ref[:, rows]
            mt_acc = jnp.zeros((lc, LANES), F32)
            for h in range(nh):
                c_h = c_sc[h]
                n_h = n_sc[h, 0:1, :]
                m_prev = jnp.max(m_sc[h, 0:1, :], axis=1, keepdims=True)
                cs_ref[cc, h] = c_h
                ns_ref[cc, h] = n_sc[h]
                ms_ref[cc, h] = m_sc[h]
                t = _cell_chunk(h, nh, q_ref, k_ref, v_ref, gc, gr, m_prev, c_h, n_h, r0=cc * lc)
                sl = t["sl"]
                num = _mm(t["smat"], t["vh"]) + t["w_inter"] * t["qc"]
                cell_h = num / jnp.maximum(jnp.abs(t["den"]), jnp.exp(-t["m_t"]))
                mt_acc = jnp.where(lane == h, t["m_t"], mt_acc)
                kw = t["kh"] * t["w_state"]
                c_sc[h] = t["decay"] * c_h + _mm_tn(kw, t["vh"])
                n_sc[h] = _bcast8(t["decay"] * n_h + _colsum(kw))
                m_sc[h] = jnp.broadcast_to(t["m_new"], (SUBLANES, LANES))
                hg = _sigmoid(o_ref[rows, sl]) * cell_h
                hn = hg * lax.rsqrt(jnp.mean(hg * hg, axis=1, keepdims=True) + EPS)
                z = z_ref[rows, sl]
                cell_ref[rows, sl] = cell_h
                y_ref[rows, sl] = (hn * ng_ref[:, sl] * z * _sigmoid(z)).astype(BF16)
            mt_ref[rows, :] = mt_acc

    tok = pl.BlockSpec((per * lc, d), lambda c: (c, 0))
    return _call(
        body, comms, name="mlstm_cell_fwd", grid=(nc // per,),
        in_specs=[tok, tok, tok, pl.BlockSpec((per * lc, LANES), lambda c: (c, 0)),
                  pl.BlockSpec((16, per * lc), lambda c: (0, c)),
                  pl.BlockSpec((per * lc, d), lambda c: (c, 3)), pl.BlockSpec((per * lc, d), lambda c: (c, 4)), _full((1, d))],
        out_specs=[tok, tok, pl.BlockSpec((per, nh, dh, dh), lambda c: (c, 0, 0, 0)),
                   pl.BlockSpec((per, nh, SUBLANES, dh), lambda c: (c, 0, 0, 0)),
                   pl.BlockSpec((per, nh, SUBLANES, LANES), lambda c: (c, 0, 0, 0)),
                   pl.BlockSpec((per * lc, LANES), lambda c: (c, 0))],
        out_shape=[jax.ShapeDtypeStruct((s, d), F32), jax.ShapeDtypeStruct((s, d), BF16),
                   jax.ShapeDtypeStruct((nc, nh, dh, dh), F32), jax.ShapeDtypeStruct((nc, nh, SUBLANES, dh), F32),
                   jax.ShapeDtypeStruct((nc, nh, SUBLANES, LANES), F32), jax.ShapeDtypeStruct((s, LANES), F32)],
        scratch_shapes=[pltpu.VMEM((nh, dh, dh), F32), pltpu.VMEM((nh, SUBLANES, dh), F32),
                        pltpu.VMEM((nh, SUBLANES, LANES), F32)],
        args=(q, k, v, gcol, grow, u, u, ng))


def _out_fwd(x, y_rg, y_ml, gate, w_out_g, layer, tile, comms=None, head=None):
    s, d = x.shape
    nd, _, r, _ = w_out_g.shape

    def body(x_ref, yr_ref, ym_ref, g_ref, w_ref, *rest):
        ycat = jnp.concatenate([yr_ref[...].astype(BF16), ym_ref[...].astype(BF16)], axis=1)
        acc = jnp.dot(ycat, w_ref[...].reshape(nd * r, d), preferred_element_type=F32)
        xn = x_ref[...] + g_ref[...] * acc
        if head is None:
            xn_ref, y_ref = rest
            y_ref[...] = acc
            xn_ref[...] = xn
            return
        fg_ref, t_ref, y_ref, dx_ref, loss_ref, gg_ref = rest
        y_ref[...] = acc

        @pl.when(pl.program_id(0) == 0)
        def _():
            loss_ref[...] = jnp.zeros_like(loss_ref)
            gg_ref[...] = jnp.zeros_like(gg_ref)

        fg = fg_ref[...]
        rs = lax.rsqrt(jnp.mean(xn * xn, axis=1, keepdims=True) + EPS)
        xh = xn * rs
        e = xh * fg - t_ref[...]
        loss_ref[...] += jnp.broadcast_to(_colsum(_rowsum(e * e)) * (0.5 / d), loss_ref.shape)
        dy = e * (1.0 / d)
        gg_ref[...] += _bcast8(_colsum(dy * xh))
        dxh = dy * fg
        dx_ref[...] = rs * (dxh - xh * jnp.mean(dxh * xh, axis=1, keepdims=True))

    tok = pl.BlockSpec((tile, d), lambda i: (i, 0))
    in_specs = [tok, tok, tok, _full((1, d)), pl.BlockSpec((nd, 1, r, d), lambda i: (0, layer, 0, 0))]
    if head is None:
        return _call(body, comms, name="out_proj_fwd", grid=(s // tile,), in_specs=in_specs, out_specs=[tok, tok],
                     out_shape=[jax.ShapeDtypeStruct((s, d), F32)] * 2, args=(x, y_rg, y_ml, gate, w_out_g))
    return _call(
        body, comms, name="out_proj_loss", grid=(s // tile,),
        in_specs=in_specs + [_full((1, d)), tok],
        out_specs=[tok, tok, _full((SUBLANES, LANES)), _full((SUBLANES, d))],
        out_shape=[jax.ShapeDtypeStruct((s, d), F32)] * 2 + [jax.ShapeDtypeStruct((SUBLANES, LANES), F32),
                                                             jax.ShapeDtypeStruct((SUBLANES, d), F32)],
        args=(x, y_rg, y_ml, gate, w_out_g, *head))


def _ml_out_stage_bwd(dy, cell, o, z, ng):
    so = _sigmoid(o)
    hg = so * cell
    rinv = lax.rsqrt(jnp.mean(hg * hg, axis=1, keepdims=True) + EPS)
    hn = hg * rinv
    sz = _sigmoid(z)
    dz = dy * hn * ng * (sz + z * sz * (1.0 - sz))
    dymid = dy * z * sz
    dhn = dymid * ng
    dhg = rinv * (dhn - hn * jnp.mean(dhn * hn, axis=1, keepdims=True))
    return dz, dhg * cell * so * (1.0 - so), dhg * so, _colsum(dymid * hn)


def _out_bwd(dxo, gate, y, y_rg, y_ml, w_out_g, layer, tile, comms=None):
    s, d = dxo.shape
    nd, _, r, _ = w_out_g.shape

    def body(dx_ref, g_ref, y_ref, yr_ref, ym_ref, w_ref, dyr_ref, dym_ref, gw_ref, dg_ref):
        @pl.when(pl.program_id(0) == 0)
        def _():
            gw_ref[...] = jnp.zeros_like(gw_ref)
            dg_ref[...] = jnp.zeros_like(dg_ref)

        dxv = dx_ref[...]
        dg_ref[...] += _bcast8(_colsum(dxv * y_ref[...]))
        dyb = (dxv * g_ref[...]).astype(BF16)
        dycat = lax.dot_general(dyb, w_ref[...].reshape(nd * r, d), (((1,), (1,)), ((), ())), preferred_element_type=F32)
        dyr_ref[...] = dycat[:, 0:d]
        dym_ref[...] = dycat[:, d:2 * d]
        ycat = jnp.concatenate([yr_ref[...].astype(BF16), ym_ref[...].astype(BF16)], axis=1)
        gw_ref[...] += lax.dot_general(ycat, dyb, (((0,), (0,)), ((), ())), preferred_element_type=F32).reshape(nd, r, d)

    tok = pl.BlockSpec((tile, d), lambda i: (i, 0))
    return _call(
        body, comms, name="out_proj_bwd", grid=(s // tile,),
        in_specs=[tok, _full((1, d)), tok, tok, tok, pl.BlockSpec((nd, 1, r, d), lambda i: (0, layer, 0, 0))],
        out_specs=[tok, tok, _full((nd, r, d)), _full((SUBLANES, d))],
        out_shape=[jax.ShapeDtypeStruct((s, d), F32)] * 2 + [jax.ShapeDtypeStruct((nd, r, d), F32),
                                                             jax.ShapeDtypeStruct((SUBLANES, d), F32)],
        args=(dxo, gate, y, y_rg, y_ml, w_out_g))


def _ml_cell_bwd(dy_ml, u, cell, q, k, v, gcol, grow, mt, cs, ns, ms, ng, nh, comms=None):
    s, d = q.shape
    lc = ML_CHUNK
    nc = s // lc
    dh = d // nh

    per = CELL_BWD_CHUNKS_PER_STEP if nc % CELL_BWD_CHUNKS_PER_STEP == 0 else 1

    def body(*refs):
        gng_ref, dc_sc, dn_sc = refs[20], refs[21], refs[22]

        @pl.when(pl.program_id(0) == 0)
        def _():
            dc_sc[...] = jnp.zeros_like(dc_sc)
            dn_sc[...] = jnp.zeros_like(dn_sc)
            gng_ref[...] = jnp.zeros_like(gng_ref)

        for cc in reversed(range(per)):
            rows = slice(cc * lc, (cc + 1) * lc)
            views = [refs[at] if at == 13 else _PartOf(refs[at], cols=rows) if at == 8 else
                     _PartOf(refs[at], lead=cc) if at in (10, 11, 12) else _PartOf(refs[at], rows=rows) for at in range(20)]
            chunk(*views, gng_ref, dc_sc, dn_sc)

    def chunk(dy_ref, o_ref, z_ref, cell_ref, q_ref, k_ref, v_ref, gc_ref, gr_ref, mt_ref, cs_ref, ns_ref, ms_ref,
              ng_ref, dq_ref, dk_ref, dv_ref, dg_ref, do_ref, dz_ref, gng_ref, dc_sc, dn_sc):
        gc = gc_ref[...]
        gr = gr_ref[...]
        mtv = mt_ref[...]
        lane = _iota((lc, LANES), 1)
        rowv = _iota((lc, 1), 0)
        dg_acc = jnp.zeros((lc, LANES), F32)
        for h in range(nh):
            c_h = cs_ref[0, h]
            n_h = ns_ref[0, h, 0:1, :]
            m_prev = jnp.max(ms_ref[0, h, 0:1, :], axis=1, keepdims=True)
            t = _cell_chunk(h, nh, q_ref, k_ref, v_ref, gc, gr, m_prev, c_h, n_h, m_t=_col(mtv, h))
            sl, qh, kh, vh = t["sl"], t["qh"], t["kh"], t["vh"]
            w_intra, w_inter, smat, w_state, decay = t["w_intra"], t["w_inter"], t["smat"], t["w_state"], t["decay"]
            cell_h = cell_ref[:, sl]
            dz, do, dcell, gng = _ml_out_stage_bwd(dy_ref[:, sl], cell_h, o_ref[:, sl], z_ref[:, sl], ng_ref[:, sl])
            dz_ref[:, sl] = dz.astype(BF16)
            do_ref[:, sl] = do.astype(BF16)
            gng_ref[:, sl] += _bcast8(gng)
            eneg = jnp.exp(-t["m_t"])
            aden = jnp.abs(t["den"])
            nst = jnp.maximum(aden, eneg)
            dnum = dcell / nst
            dden = jnp.where(aden > eneg, -_rowsum(cell_h * dcell) / nst * jnp.sign(t["den"]), 0.0)
            pmat = _mm_nt(dnum, vh) + dden
            damat = pmat * w_intra
            gmat = pmat * smat
            wdn = w_inter * dnum
            wdd = w_inter * dden
            dqh = _mm(damat, kh) + _mm_nt(wdn, c_h) + wdd * n_h
            dkh = _mm_tn(damat, qh)
            dvh = _mm_tn(smat, dnum)
            dw_inter = _rowsum(dnum * t["qc"]) + dden * t["qn"]
            dcn = dc_sc[h]
            dnn = dn_sc[h, 0:1, :]
            kw = kh * w_state
            dkw = _mm_nt(vh, dcn) + dnn
            dvh = dvh + _mm(kw, dcn)
            dkh = dkh + dkw * w_state
            dgst = _rowsum(dkw * kh) * w_state
            ddecay = _colsum(_rowsum(dcn * c_h)) + _rowsum(dnn * n_h)
            db_last = _colsum(dgst) + ddecay * decay
            rs_g = _rowsum(gmat)
            cs_g = _rowsum(gmat.T)
            db = rs_g - cs_g + dw_inter * w_inter - dgst + jnp.where(rowv == lc - 1, db_last, 0.0)
            dli = cs_g + dgst
            dc_sc[h] = decay * dcn + _mm_tn(qh, wdn)
            dn_sc[h] = _bcast8(decay * dnn + _colsum(qh * wdd))
            dq_ref[:, sl] = dqh
            dk_ref[:, sl] = dkh * (dh ** -0.5)
            dv_ref[:, sl] = dvh
            dg_acc = jnp.where(lane == h, dli, jnp.where(lane == 4 + h, db, dg_acc))
        dg_ref[...] = dg_acc

    rev = lambda c: nc // per - 1 - c
    tok = pl.BlockSpec((per * lc, d), lambda c: (rev(c), 0))
    g128 = pl.BlockSpec((per * lc, LANES), lambda c: (rev(c), 0))
    return _call(
        body, comms, name="mlstm_cell_bwd", grid=(nc // per,),
        in_specs=[tok, pl.BlockSpec((per * lc, d), lambda c: (rev(c), 3)), pl.BlockSpec((per * lc, d), lambda c: (rev(c), 4)),
                  tok, tok, tok, tok, g128, pl.BlockSpec((16, per * lc), lambda c: (0, rev(c))), g128,
                  pl.BlockSpec((per, nh, dh, dh), lambda c: (rev(c), 0, 0, 0)),
                  pl.BlockSpec((per, nh, SUBLANES, dh), lambda c: (rev(c), 0, 0, 0)),
                  pl.BlockSpec((per, nh, SUBLANES, LANES), lambda c: (rev(c), 0, 0, 0)), _full((1, d))],
        out_specs=[tok, tok, tok, g128, tok, tok, _full((SUBLANES, d))],
        out_shape=[jax.ShapeDtypeStruct((s, d), F32)] * 3 + [jax.ShapeDtypeStruct((s, LANES), F32)]
        + [jax.ShapeDtypeStruct((s, d), BF16)] * 2 + [jax.ShapeDtypeStruct((SUBLANES, d), F32)],
        scratch_shapes=[pltpu.VMEM((nh, dh, dh), F32), pltpu.VMEM((nh, SUBLANES, dh), F32)],
        args=(dy_ml, u, u, cell, q, k, v, gcol, grow, mt, cs, ns, ms, ng))


def _halo_spec(d, tile, nt, col):
    per = tile // SUBLANES
    return pl.BlockSpec((SUBLANES, d), lambda i: (jnp.maximum((nt - 1 - i) * per - 1, 0), col))


def _ml_pre_bwd(dq, dk, dv, dgates, gcol, u, pre, q, k, v, conv_w, w_q, w_k, w_v, wif_t, tile, comms=None):
    s, d = dq.shape
    nt = s // tile
    nh, dh, _ = w_q.shape

    def body(dq_ref, dk_ref, dv_ref, dg_ref, gc_ref, x_ref, pre_ref, q_ref, k_ref, v_ref, cw_ref,
             wq_ref, wk_ref, wv_ref, wift_ref,
             dx_ref, gwq_ref, gwk_ref, gwv_ref, gwif_ref, gbif_ref, gcw_ref, gcb_ref, next8):
        @pl.when(pl.program_id(0) == 0)
        def _():
            next8[...] = jnp.zeros_like(next8)
            for ref in (gwq_ref, gwk_ref, gwv_ref, gwif_ref, gbif_ref, gcw_ref, gcb_ref):
                ref[...] = jnp.zeros_like(ref)

        x = x_ref[...]
        pre = pre_ref[...]
        sg = _sigmoid(pre)
        xc = pre * sg
        dgc = dg_ref[...]
        lane = _iota(dgc.shape, 1)
        utri = jnp.where(_iota((ML_CHUNK, ML_CHUNK), 0) <= _iota((ML_CHUNK, ML_CHUNK), 1), 1.0, 0.0)
        rcs = [_mm_hi(utri, dgc[c * ML_CHUNK:(c + 1) * ML_CHUNK, :]) for c in range(tile // ML_CHUNK)]
        rc = rcs[0] if len(rcs) == 1 else jnp.concatenate(rcs, axis=0)
        dgates_v = jnp.where(lane < 4, dgc, jnp.where(lane < 8, rc * (1.0 - jnp.exp(gc_ref[...])), 0.0))
        dgb = dgates_v.astype(BF16)
        gbif_ref[...] += jnp.broadcast_to(_colsum(dgates_v), gbif_ref.shape)
        ext = jnp.dot(dgb, wift_ref[...], preferred_element_type=F32)
        dqt = dq_ref[...] + ext[:, 0:d]
        dkt = dk_ref[...] + ext[:, d:2 * d]
        dvt = dv_ref[...] + ext[:, 2 * d:3 * d]
        gwif_ref[:, 0:d] += _mm_tn(dgb, q_ref[...])
        gwif_ref[:, d:2 * d] += _mm_tn(dgb, k_ref[...])
        gwif_ref[:, 2 * d:3 * d] += _mm_tn(dgb, v_ref[...])
        dxc_parts, dxv_parts = [], []
        for h in range(nh):
            sl = slice(h * dh, (h + 1) * dh)
            gwq_ref[h] += _mm_tn(xc[:, sl], dqt[:, sl])
            gwk_ref[h] += _mm_tn(xc[:, sl], dkt[:, sl])
            gwv_ref[h] += _mm_tn(x[:, sl], dvt[:, sl])
            dxc_parts.append(_mm_nt(dqt[:, sl], wq_ref[h]) + _mm_nt(dkt[:, sl], wk_ref[h]))
            dxv_parts.append(_mm_nt(dvt[:, sl], wv_ref[h]))
        dxc = jnp.concatenate(dxc_parts, axis=1)
        dxv = jnp.concatenate(dxv_parts, axis=1)
        dpre = dxc * (sg + pre * sg * (1.0 - sg))
        gcb_ref[...] += _bcast8(_colsum(dpre))
        dx_ref[...] = (dxv + _conv_bwd(dpre, x, next8[...], cw_ref, gcw_ref)).astype(BF16)
        next8[...] = dpre[0:SUBLANES, :]

    rev = lambda i: nt - 1 - i
    tok = pl.BlockSpec((tile, d), lambda i: (rev(i), 0))
    g128 = pl.BlockSpec((tile, LANES), lambda i: (rev(i), 0))
    wsh = (nh, dh, dh)
    return _call(
        body, comms, name="mlstm_proj_bwd", grid=(nt,),
        in_specs=[tok, tok, tok, g128, g128, pl.BlockSpec((tile, d), lambda i: (rev(i), 2)), tok,
                  tok, tok, tok, _full(conv_w.shape), _full(wsh), _full(wsh), _full(wsh), _full(wif_t.shape)],
        out_specs=[tok, _full(wsh), _full(wsh), _full(wsh), _full((LANES, 3 * d)), _full((SUBLANES, LANES)),
                   _full((SUBLANES, d)), _full((SUBLANES, d))],
        out_shape=[jax.ShapeDtypeStruct((s, d), BF16)] + [jax.ShapeDtypeStruct(wsh, F32)] * 3
        + [jax.ShapeDtypeStruct((LANES, 3 * d), F32), jax.ShapeDtypeStruct((SUBLANES, LANES), F32),
           jax.ShapeDtypeStruct((SUBLANES, d), F32), jax.ShapeDtypeStruct((SUBLANES, d), F32)],
        scratch_shapes=[pltpu.VMEM((SUBLANES, d), F32)],
        args=(dq, dk, dv, dgates, gcol, u, pre, q, k, v, conv_w, w_q, w_k, w_v, wif_t))


def _rg_bwd(dy_rg, u, h_rg, gates, conv_w, w_a, w_x, lam, tile, comms=None):
    s, d = dy_rg.shape
    nt = s // tile
    nh, dh, _ = w_a.shape

    def body(dy_ref, x_ref, z_ref, h_ref, hhalo_ref, xc_ref, r_ref, i_ref, a_ref, beta_ref, cw_ref, wa_ref,
             wx_ref, lam_ref,
             dx_ref, dz_ref, gwa_ref, gwx_ref, gba_ref, gbx_ref, glam_ref, gcw_ref, gcb_ref, next8, anext, dnext, dbuf):
        i = pl.program_id(0)

        @pl.when(i == 0)
        def _():
            for ref in (next8, anext, dnext, gwa_ref, gwx_ref, gba_ref, gbx_ref, glam_ref, gcw_ref, gcb_ref):
                ref[...] = jnp.zeros_like(ref)

        inner = jnp.where(i < nt - 1, 1.0, 0.0)
        xc, r, ig, a, beta = xc_ref[...], r_ref[...], i_ref[...], a_ref[...], beta_ref[...]
        sp = _softplus(-lam_ref[...])
        h = h_ref[...]
        row = _iota(h.shape, 0)
        hprev = jnp.where(row >= 1, pltpu.roll(h, 1, 0), hhalo_ref[SUBLANES - 1:SUBLANES, :] * inner)
        z = z_ref[...]
        sz = _sigmoid(z)
        dyv = dy_ref[...]
        dz_ref[...] = (dyv * h * (sz + z * sz * (1.0 - sz))).astype(BF16)
        a_up = jnp.where(row < tile - 1, pltpu.roll(a, tile - 1, 0), anext[0:1, :])
        _scan_into(a_up, dyv * z * sz, dnext[0:1, :], dbuf, True)
        delta = dbuf[...]
        anext[...] = a[0:SUBLANES, :]
        dnext[...] = delta[0:SUBLANES, :]
        dla = delta * hprev * a - delta * ig * xc * (a * a / beta)
        glam_ref[...] += _bcast8(_colsum(dla * r) * (RG_C * _sigmoid(-lam_ref[...])))
        dpa = dla * (-RG_C * sp) * r * (1.0 - r)
        dpx = delta * beta * xc * ig * (1.0 - ig)
        gba_ref[...] += _bcast8(_colsum(dpa))
        gbx_ref[...] += _bcast8(_colsum(dpx))
        parts = []
        for hh in range(nh):
            sl = slice(hh * dh, (hh + 1) * dh)
            gwa_ref[hh] += _mm_tn(xc[:, sl], dpa[:, sl])
            gwx_ref[hh] += _mm_tn(xc[:, sl], dpx[:, sl])
            parts.append(_mm_nt(dpa[:, sl], wa_ref[hh]) + _mm_nt(dpx[:, sl], wx_ref[hh]))
        dxc = delta * beta * ig + jnp.concatenate(parts, axis=1)
        gcb_ref[...] += _bcast8(_colsum(dxc))
        dx_ref[...] = _conv_bwd(dxc, x_ref[...], next8[...], cw_ref, gcw_ref).astype(BF16)
        next8[...] = dxc[0:SUBLANES, :]

    rev = lambda i: nt - 1 - i
    tok = pl.BlockSpec((tile, d), lambda i: (rev(i), 0))
    vec = _full((1, d))
    acc = _full((SUBLANES, d))
    wsh = (nh, dh, dh)
    return _call(
        body, comms, name="rglru_bwd", grid=(nt,),
        in_specs=[tok, tok, pl.BlockSpec((tile, d), lambda i: (rev(i), 1)), tok,
                  _halo_spec(d, tile, nt, 0)] + [tok] * 5 + [_full(conv_w.shape), _full(wsh), _full(wsh), vec],
        out_specs=[tok, tok, _full(wsh), _full(wsh), acc, acc, acc, acc, acc],
        out_shape=[jax.ShapeDtypeStruct((s, d), BF16)] * 2 + [jax.ShapeDtypeStruct(wsh, F32)] * 2
        + [jax.ShapeDtypeStruct((SUBLANES, d), F32)] * 5,
        scratch_shapes=[pltpu.VMEM((SUBLANES, d), F32)] * 3 + [pltpu.VMEM((tile, d), F32)],
        args=(dy_rg, u, u, h_rg, h_rg, *gates, conv_w, w_a, w_x, lam))


def _segments(d, w, n_pieces, n_slots):
    bounds = sorted({k * d for k in range(n_pieces + 1)} | {j * w for j in range(n_slots + 1)})
    return [(lo // d, lo % d, lo // w, lo % w, hi - lo) for lo, hi in zip(bounds[:-1], bounds[1:])]


def _in_bwd(pieces, x, dxo, ng, scale, w_in_g, layer, tile, comms=None, tiles=None, prev=None):
    s, d = x.shape
    nd, _, _, w = w_in_g.shape
    first, count = tiles or (0, s // tile)
    n_p = len(pieces)

    def body(*refs):
        p_refs = refs[:n_p]
        x_ref, dxo_ref, ng_ref, sc_ref, w_ref = refs[n_p:n_p + 5]
        dx_ref, dsc_ref, dsh_ref, gng_ref, wcat = refs[-5:]
        _join_columns(w_ref, wcat)

        @pl.when(pl.program_id(0) == 0)
        def _():
            for k, ref in enumerate((dsc_ref, dsh_ref, gng_ref)):
                ref[...] = jnp.zeros_like(ref) if prev is None else refs[n_p + 6 + k][...]

        du = jnp.concatenate([p[...] for p in p_refs], axis=1)
        dh = lax.dot_general(du, wcat[...], (((1,), (1,)), ((), ())), preferred_element_type=F32)
        xv = x_ref[...]
        g = ng_ref[...]
        rs = lax.rsqrt(jnp.mean(xv * xv, axis=1, keepdims=True) + EPS)
        xh = xv * rs
        dsh_ref[...] += _bcast8(_colsum(dh))
        dsc_ref[...] += _bcast8(_colsum(dh * xh * g))
        dhn = dh * (1.0 + sc_ref[...])
        gng_ref[...] += _bcast8(_colsum(dhn * xh))
        dxh = dhn * g
        dx_ref[...] = dxo_ref[...] + rs * (dxh - xh * jnp.mean(dxh * xh, axis=1, keepdims=True))

    tok = pl.BlockSpec((tile, d), lambda i: (i + first, 0))
    vec = _full((1, d))
    acc = _full((SUBLANES, d))
    more_specs = [] if prev is None else [pl.BlockSpec(memory_space=pl.ANY), acc, acc, acc]
    return _call(
        body, comms, name="in_proj_bwd_x", grid=(count,),
        in_specs=[tok] * n_p + [tok, tok, vec, vec, pl.BlockSpec((nd, 1, d, w), lambda i: (0, layer, 0, 0),
                                                               pipeline_mode=pl.Buffered(1))] + more_specs,
        out_specs=[tok, acc, acc, acc],
        out_shape=[jax.ShapeDtypeStruct((s, d), F32)] + [jax.ShapeDtypeStruct((SUBLANES, d), F32)] * 3,
        scratch_shapes=[pltpu.VMEM((d, nd * w), BF16)],
        args=(*pieces, x, dxo, ng, scale, w_in_g) + (() if prev is None else tuple(prev)),
        aliases={} if prev is None else {n_p + 5: 0})


def _in_bwd_w(pieces, hbf, w, slots, tile, comms=None):
    s, d = hbf.shape
    nd_all = len(pieces) * d // w
    segs = [sg for sg in _segments(d, w, len(pieces), nd_all) if sg[2] in slots]

    def body(*refs):
        p_refs = refs[:len(pieces)]
        h_ref, gw_ref = refs[len(pieces):]

        @pl.when(pl.program_id(0) == 0)
        def _():
            gw_ref[...] = jnp.zeros_like(gw_ref)

        hv = h_ref[...]
        for (kk, a, j, b, width) in segs:
            gw_ref[j - slots[0], :, b:b + width] += _mm_tn(hv, p_refs[kk][:, a:a + width])

    tok = pl.BlockSpec((tile, d), lambda i: (i, 0))
    return _call(
        body, comms, name="in_proj_bwd_w", grid=(s // tile,),
        in_specs=[tok] * len(pieces) + [tok],
        out_specs=[pl.BlockSpec((len(slots), d, w), lambda i: (0, 0, 0), pipeline_mode=pl.Buffered(1))],
        out_shape=[jax.ShapeDtypeStruct((len(slots), d, w), F32)],
        args=(*pieces, hbf))[0]


def _exchange(arrs, gather, name):
    return _run_comms([_exchange_comm(arrs, gather)], name)[0]


def _run_comms(comms, name):
    _call(lambda: None, comms, name=name, grid=(1,), in_specs=[], out_specs=[], out_shape=[], args=())
    return [cm.results for cm in comms]


def _exchange_comm(arrs, gather):
    n = len(arrs)
    per = N_DEV - 1

    def copies(ins, outs, sems):
        send_sems, recv_sems, local_sems = sems
        x, y, c = (lax.axis_index(ax) for ax in MESH_AXES)
        me = 4 * x + 2 * y + c
        sends, recvs = [], []
        for flip in range(1, N_DEV):
            px = x ^ ((flip >> 2) & 1)
            py = y ^ ((flip >> 1) & 1)
            pc = c ^ (flip & 1)
            peer = 4 * px + 2 * py + pc
            for kk in range(n):
                src = ins[kk] if gather else ins[kk].at[peer]
                sends.append(_remote(src, outs[kk].at[me], send_sems, recv_sems, kk * per + flip - 1, (px, py, pc)))
                recvs.append(_remote(src, outs[kk].at[peer], send_sems, recv_sems, kk * per + flip - 1, (px, py, pc)))
        local = [pltpu.make_async_copy(ins[kk] if gather else ins[kk].at[me], outs[kk].at[me], local_sems.at[kk])
                 for kk in range(n)]
        return local, sends, recvs

    def start(ins, outs, sems):
        local, sends, _ = copies(ins, outs, sems)
        for cp in sends + local:
            cp.start()

    def finish(ins, outs, sems):
        local, sends, recvs = copies(ins, outs, sems)
        for cp in recvs:
            cp.wait_recv()
        for cp in sends:
            cp.wait_send()
        for cp in local:
            cp.wait()

    return _Comm(arrs, [jax.ShapeDtypeStruct((N_DEV,) + a.shape if gather else a.shape, a.dtype) for a in arrs],
                 [pltpu.SemaphoreType.DMA((n * per,)), pltpu.SemaphoreType.DMA((n * per,)), pltpu.SemaphoreType.DMA((n,))],
                 start, finish)


def _mesh_place():
    x, y, c = (lax.axis_index(ax) for ax in MESH_AXES)
    return x, y, c, (x, y, 1 - c), [(1 - x, y), (x, 1 - y), (1 - x, 1 - y)]


def _remote(src, dst, send_sems, recv_sems, sem, to):
    return pltpu.make_async_remote_copy(src_ref=src, dst_ref=dst, send_sem=send_sems.at[sem], recv_sem=recv_sems.at[sem],
                                        device_id=to, device_id_type=pl.DeviceIdType.MESH)


N_CHIPS = N_DEV // 2


def _pair_sum(a, other, parity, name):
    _, r, c = a.shape
    tr = _row_tile(r, c, 3)

    def body(p_ref, a_ref, o_ref, s_ref):
        s_ref[...] = (a_ref[...] + o_ref[...]).astype(BF16)

    return pl.pallas_call(
        body, name=name,
        grid_spec=pltpu.PrefetchScalarGridSpec(
            num_scalar_prefetch=1, grid=(N_CHIPS, r // tr),
            in_specs=[pl.BlockSpec((1, tr, c), lambda q, i, p: (2 * q + p[0], i, 0)),
                      pl.BlockSpec((1, tr, c), lambda q, i, p: (q, i, 0))],
            out_specs=pl.BlockSpec((1, tr, c), lambda q, i, p: (q, i, 0))),
        out_shape=jax.ShapeDtypeStruct((N_CHIPS, r, c), BF16),
        compiler_params=_params(2),
    )(parity, a, other)


def _adam_math(w, g, m, v):
    m = ADAM_B1 * m + (1.0 - ADAM_B1) * g
    v = ADAM_B2 * v + (1.0 - ADAM_B2) * (g * g)
    m_hat = m / (1.0 - ADAM_B1 ** ADAM_STEP)
    v_hat = v / (1.0 - ADAM_B2 ** ADAM_STEP)
    delta = -ADAM_LR * (m_hat / (jnp.sqrt(v_hat) + ADAM_EPS) + ADAM_WD * w)
    return delta, m, v


def _sum_devices(r_ref):
    acc = r_ref[0].astype(F32)
    for p in range(1, r_ref.shape[0]):
        acc = acc + r_ref[p].astype(F32)
    return acc


def _row_tile(rows, cols, n_bufs):
    budget = 24 * 1024 * 1024 // (n_bufs * 2 * cols * 4)
    t = rows
    while t > budget and t % 2 == 0 and (t // 2) % SUBLANES == 0:
        t //= 2
    return t


def _reduce_adam(recvs, w, m, v, name, comms=None):
    nl, r, c = w.shape
    n_part = recvs[0].shape[0]
    tr = _row_tile(r, c, n_part * nl + 7)
    nt = r // tr

    def body(*refs):
        r_refs = refs[:nl]
        w_ref, m_ref, v_ref, g_ref, d_ref, mo_ref, vo_ref = refs[nl:]
        layer = pl.program_id(0) // nt
        g = _sum_devices(r_refs[0])
        for ll in range(1, nl):
            g = jnp.where(layer == ll, _sum_devices(r_refs[ll]), g)
        delta, m2, v2 = _adam_math(w_ref[0], g, m_ref[0], v_ref[0])
        g_ref[0] = g
        d_ref[0] = delta
        mo_ref[0] = m2
        vo_ref[0] = v2

    def rspec(ll):
        return pl.BlockSpec((n_part, tr, c),
                            lambda i: (0, jnp.where(i // nt == ll, i % nt, jnp.where(i // nt < ll, 0, nt - 1)), 0))

    blk = pl.BlockSpec((1, tr, c), lambda i: (i // nt, i % nt, 0))
    return _call(
        body, comms, name=name, grid=(nl * nt,),
        in_specs=[rspec(ll) for ll in range(nl)] + [blk, blk, blk],
        out_specs=[blk] * 4,
        out_shape=[jax.ShapeDtypeStruct((nl, r, c), F32)] * 4,
        args=(*recvs, w, m, v))


def _tile_for(s, want):
    return min(want, s)


REPLICATED = ("norm_g", "b_ada", "rg_conv_b", "rg_w_a", "rg_b_a", "rg_w_x", "rg_b_x", "rg_lambda", "ml_conv_b",
              "ml_b_if", "ml_norm_g", "final_g")


def _small_pack(rg_conv_w, ml_conv_w, ml_w_if):
    nl = rg_conv_w.shape[0]
    wif_t = jnp.swapaxes(ml_w_if, 1, 2).reshape(nl, -1, LANES)
    return jnp.concatenate([rg_conv_w, ml_conv_w, wif_t], axis=1)


def _small_unpack(p, if_rows):
    nl = p.shape[0]
    rg_cw = p[:, 0:CONV_WIDTH]
    ml_cw = p[:, CONV_WIDTH:2 * CONV_WIDTH]
    wif = jnp.swapaxes(p[:, 2 * CONV_WIDTH:].reshape(nl, 8, if_rows), 1, 2)
    return rg_cw, ml_cw, wif


def _qkv_slots(g_qkv, nd):
    three, nh, dh, _ = g_qkv.shape
    return g_qkv.reshape(three, nh, nd, dh // nd, dh).transpose(2, 0, 1, 3, 4).reshape(nd, three * nh * (dh // nd), dh)


def _small_slots(g):
    nd = N_DEV
    cw = jnp.stack([g["rg_conv_w"], g["ml_conv_w"]]).reshape(2, CONV_WIDTH, nd, LANES).transpose(2, 0, 1, 3)
    cw = cw.reshape(nd, 2 * CONV_WIDTH, LANES)
    wif = g["wif_t"].reshape(8, nd, -1).transpose(1, 0, 2).reshape(nd, -1, LANES)
    return jnp.concatenate([cw, wif], axis=1)


def _small_grad_slots(g_wq, g_wk, g_wv, g_rgcw, g_mlcw, g_wift):
    nd = N_DEV
    nh, dh, _ = g_wq.shape
    rows = dh // nd
    cols = g_wift.shape[1] // nd
    per_slot = 2 * CONV_WIDTH + SUBLANES * (cols // LANES)

    def body(q_ref, k_ref, v_ref, rg_ref, ml_ref, if_ref, qkv_out, small_out, sc):
        for k in range(nd):
            for j, g_ref in enumerate((q_ref, k_ref, v_ref)):
                for h in range(nh):
                    at = (j * nh + h) * rows
                    qkv_out[k, at:at + rows, :] = g_ref[h, k * rows:(k + 1) * rows, :].astype(BF16)
            for c, g_ref in enumerate((rg_ref, ml_ref)):
                sc[k, c * CONV_WIDTH:(c + 1) * CONV_WIDTH, :] = g_ref[0:CONV_WIDTH, k * LANES:(k + 1) * LANES]
            at = 2 * CONV_WIDTH
            for r in range(SUBLANES):
                for c in range(k * cols, (k + 1) * cols, LANES):
                    sc[k, at:at + 1, :] = if_ref[r:r + 1, c:c + LANES]
                    at += 1
        small_out[...] = sc[...].astype(BF16)

    return pl.pallas_call(
        body, name="pack_small_grads", grid=(1,),
        in_specs=[_full(g_wq.shape)] * 3 + [_full(g_rgcw.shape), _full(g_mlcw.shape),
                                            pl.BlockSpec((SUBLANES, g_wift.shape[1]), lambda i: (0, 0))],
        out_specs=[_full((nd, 3 * nh * rows, dh)), _full((nd, per_slot, LANES))],
        out_shape=[jax.ShapeDtypeStruct((nd, 3 * nh * rows, dh), BF16), jax.ShapeDtypeStruct((nd, per_slot, LANES), BF16)],
        scratch_shapes=[pltpu.VMEM((nd, per_slot, LANES), F32)], compiler_params=_params(1),
    )(g_wq, g_wk, g_wv, g_rgcw, g_mlcw, g_wift)


def _slot(block):
    return 4 * block[0] + 2 * block[1] + block[2]


def _dma_sems(*counts):
    return [pltpu.SemaphoreType.DMA((n,)) for n in counts]


def _start_all(copies):
    for cp in copies:
        cp.start()


def _gather_ici_comm(arrs):
    n = len(arrs)

    def copies(ins, outs, sems):
        send_sems, recv_sems, local_sems = sems
        x, y, c, sibling, chips = _mesh_place()
        me = (x, y, c)
        peers = [(*chip, c) for chip in chips] + [sibling]
        local = [pltpu.make_async_copy(ins[kk], outs[kk].at[_slot(me)], local_sems.at[kk]) for kk in range(n)]
        sends = [_remote(ins[kk], outs[kk].at[_slot(me)], send_sems, recv_sems, kk * 4 + j, peer)
                 for j, peer in enumerate(peers) for kk in range(n)]
        recvs = [_remote(ins[kk], outs[kk].at[_slot(peer)], send_sems, recv_sems, kk * 4 + j, peer)
                 for j, peer in enumerate(peers) for kk in range(n)]
        return local, sends, recvs

    def start(ins, outs, sems):
        local, sends, _ = copies(ins, outs, sems)
        _start_all(sends + local)

    def finish(ins, outs, sems):
        local, sends, recvs = copies(ins, outs, sems)
        for cp in recvs:
            cp.wait_recv()
        for cp in sends:
            cp.wait_send()
        for cp in local:
            cp.wait()

    return _Comm(arrs, [jax.ShapeDtypeStruct((N_DEV,) + a.shape, a.dtype) for a in arrs], _dma_sems(4 * n, 4 * n, n),
                 start, finish)


def _gather_fwd_comm(bufs):
    n = len(bufs)

    def copies(ins, outs, sems):
        send_sems, recv_sems = sems
        _, _, c, sibling, chips = _mesh_place()
        sends = [_remote(ins[kk].at[_slot((*chip, c))], outs[kk].at[_slot((*chip, c))], send_sems, recv_sems, kk * 3 + j, sibling)
                 for j, chip in enumerate(chips) for kk in range(n)]
        recvs = [_remote(ins[kk].at[_slot((*chip, c))], outs[kk].at[_slot((*chip, 1 - c))], send_sems, recv_sems, kk * 3 + j, sibling)
                 for j, chip in enumerate(chips) for kk in range(n)]
        return sends, recvs

    def start(ins, outs, sems):
        _start_all(copies(ins, outs, sems)[0])

    def finish(ins, outs, sems):
        sends, recvs = copies(ins, outs, sems)
        for cp in recvs:
            cp.wait_recv()
        for cp in sends:
            cp.wait_send()

    return _Comm(bufs, [jax.ShapeDtypeStruct(a.shape, a.dtype) for a in bufs], _dma_sems(3 * n, 3 * n), start, finish,
                 aliases=[(i, i) for i in range(n)])


def _core_swap_comm(arrs):
    n = len(arrs)

    def copies(ins, outs, sems):
        send_sems, recv_sems = sems
        _, _, c, sibling, _ = _mesh_place()
        return [_remote(ins[kk].at[2 * q + (1 - c)], outs[kk].at[q], send_sems, recv_sems, kk * N_CHIPS + q, sibling)
                for q in range(N_CHIPS) for kk in range(n)]

    def start(ins, outs, sems):
        _start_all(copies(ins, outs, sems))

    def finish(ins, outs, sems):
        cps = copies(ins, outs, sems)
        for cp in cps:
            cp.wait_recv()
        for cp in cps:
            cp.wait_send()

    return _Comm(arrs, [jax.ShapeDtypeStruct((N_CHIPS,) + a.shape[1:], a.dtype) for a in arrs],
                 _dma_sems(N_CHIPS * n, N_CHIPS * n), start, finish)


def _chip_swap_comm(arrs):
    n = len(arrs)
    per = N_CHIPS - 1

    def copies(ins, outs, sems):
        send_sems, recv_sems, local_sems = sems
        x, y, c, _, chips = _mesh_place()
        mine = 2 * x + y
        sends = [_remote(ins[kk].at[2 * chip[0] + chip[1]], outs[kk].at[mine], send_sems, recv_sems, kk * per + j, (*chip, c))
                 for j, chip in enumerate(chips) for kk in range(n)]
        recvs = [_remote(ins[kk].at[mine], outs[kk].at[2 * chip[0] + chip[1]], send_sems, recv_sems, kk * per + j, (*chip, c))
                 for j, chip in enumerate(chips) for kk in range(n)]
        local = [pltpu.make_async_copy(ins[kk].at[mine], outs[kk].at[mine], local_sems.at[kk]) for kk in range(n)]
        return local, sends, recvs

    def start(ins, outs, sems):
        local, sends, _ = copies(ins, outs, sems)
        _start_all(sends + local)

    def finish(ins, outs, sems):
        local, sends, recvs = copies(ins, outs, sems)
        for cp in recvs:
            cp.wait_recv()
        for cp in sends:
            cp.wait_send()
        for cp in local:
            cp.wait()

    return _Comm(arrs, [jax.ShapeDtypeStruct(a.shape, a.dtype) for a in arrs], _dma_sems(per * n, per * n, n), start, finish)


def _ada_mod(c_all, w_ada, b_cols, comms=None):
    nl, d, w = w_ada.shape

    def body(c_ref, w_ref, b_ref, m_ref, ca_ref):
        sub = _iota((SUBLANES, d), 0)
        cv = jnp.zeros((SUBLANES, d), F32)
        for b in range(N_DEV):
            cv = jnp.where(sub == b, c_ref[b], cv)
        ca = cv * _sigmoid(cv)
        ca_ref[...] = ca
        m_ref[...] = jnp.zeros_like(m_ref)
        for l in range(nl):
            ml = _mm_hi(ca, w_ref[l]) + b_ref[l:l + 1, :]
            for b in range(N_DEV):
                m_ref[b, l:l + 1, :] = _row(ml, b)

    return _call(
        body, comms, name="adaln_mod_columns", grid=(1,),
        in_specs=[_full(c_all.shape), _full(w_ada.shape), _full(b_cols.shape)],
        out_specs=[_full((N_DEV, SUBLANES, w)), _full((SUBLANES, d))],
        out_shape=[jax.ShapeDtypeStruct((N_DEV, SUBLANES, w), F32), jax.ShapeDtypeStruct((SUBLANES, d), F32)],
        args=(c_all, w_ada, b_cols))


def _ada_grad_adam(cact_t, dmods, w, m, v, comms=None):
    nl, d, wd = w.shape
    tr = _row_tile(d, wd, 8)
    nt = d // tr

    def body(c_ref, dm_ref, w_ref, m_ref, v_ref, g_ref, d_ref, mo_ref, vo_ref):
        cv = c_ref[...]
        dm = dm_ref[0]
        g = _col(cv, 0) * _row(dm, 0)
        for b in range(1, N_DEV):
            g = g + _col(cv, b) * _row(dm, b)
        delta, m2, v2 = _adam_math(w_ref[0], g, m_ref[0], v_ref[0])
        g_ref[0] = g
        d_ref[0] = delta
        mo_ref[0] = m2
        vo_ref[0] = v2

    blk = pl.BlockSpec((1, tr, wd), lambda i: (i // nt, i % nt, 0))
    return _call(
        body, comms, name="adaln_grad_adam", grid=(nl * nt,),
        in_specs=[pl.BlockSpec((tr, N_DEV), lambda i: (i % nt, 0)), pl.BlockSpec((1, N_DEV, wd), lambda i: (i // nt, 0, 0)),
                  blk, blk, blk],
        out_specs=[blk] * 4, out_shape=[jax.ShapeDtypeStruct((nl, d, wd), F32)] * 4,
        args=(cact_t, dmods, w, m, v))


REP_ROWS = ("norm_g", "dshift", "dscale", "dgate", "rg_conv_b", "rg_b_a", "rg_b_x", "rg_lambda", "ml_conv_b", "ml_norm_g",
            "ml_b_if")


def _pack_rows(arrays, d):
    n = len(arrays)
    rows = n + (-n) % SUBLANES

    def body(*refs):
        o_ref = refs[n]
        o_ref[...] = jnp.zeros(o_ref.shape, F32)
        for r, a_ref in enumerate(refs[:n]):
            o_ref[r:r + 1, 0:a_ref.shape[1]] = a_ref[0:1, :]

    return pl.pallas_call(
        body, name="pack_vectors", grid=(1,), in_specs=[_full(a.shape) for a in arrays], out_specs=_full((rows, d)),
        out_shape=jax.ShapeDtypeStruct((rows, d), F32), compiler_params=_params(1),
    )(*arrays)


def _sum_parts(recvs, name):
    def body(*refs):
        for r_ref, o_ref in zip(refs[:len(recvs)], refs[len(recvs):]):
            o_ref[...] = _sum_devices(r_ref).astype(o_ref.dtype)

    return pl.pallas_call(
        body, name=name, grid=(1,),
        in_specs=[_full(r.shape) for r in recvs], out_specs=[_full(r.shape[1:]) for r in recvs],
        out_shape=[jax.ShapeDtypeStruct(r.shape[1:], r.dtype) for r in recvs], compiler_params=_params(1),
    )(*recvs)


def _adam_replicated(vp, mp, params, nl):
    d = vp.shape[2]
    nr = len(REP_ROWS)
    names = list(params)
    mat_shape = params["rg_w_a"][0].shape[1:]
    mat_rows = mp.shape[0] // (2 * nl)

    def pieces(name):
        if name == "final_g":
            return [(lambda vp_ref, mp_ref: vp_ref[nl * nr:nl * nr + 1, :], (slice(0, 1), slice(None)))]
        out = []
        for l in range(nl):
            if name in ("rg_w_a", "rg_w_x"):
                at = (2 * l + (name == "rg_w_x")) * mat_rows
                out.append((lambda vp_ref, mp_ref, at=at: mp_ref[at:at + mat_rows, :].astype(F32).reshape(mat_shape), l))
            elif name == "b_ada":
                for j in range(3):
                    r = l * nr + 1 + j
                    out.append((lambda vp_ref, mp_ref, r=r: vp_ref[r:r + 1, :], (slice(l, l + 1), slice(j * d, (j + 1) * d))))
            else:
                r = l * nr + REP_ROWS.index(name)
                cols = slice(0, LANES) if name == "ml_b_if" else slice(None)
                out.append((lambda vp_ref, mp_ref, r=r, cols=cols: vp_ref[r:r + 1, cols], (slice(l, l + 1), slice(None))))
        return out

    def body(*refs):
        parts_ref, mp_ref, vp_ref = refs[0], refs[1], refs[-1]
        ins, outs = refs[2:2 + 3 * len(names)], refs[2 + 3 * len(names):-1]
        vp_ref[...] = _sum_devices(parts_ref)
        for pi, name in enumerate(names):
            w_ref, m_ref, v_ref = ins[3 * pi:3 * pi + 3]
            g_ref, d_ref, mo_ref, vo_ref = outs[4 * pi:4 * pi + 4]
            for get, idx in pieces(name):
                g = get(vp_ref, mp_ref)
                delta, m2, v2 = _adam_math(w_ref[idx], g, m_ref[idx], v_ref[idx])
                g_ref[idx] = g
                d_ref[idx] = delta
                mo_ref[idx] = m2
                vo_ref[idx] = v2

    flat = [a for name in names for a in params[name]]
    out_shape = [jax.ShapeDtypeStruct(params[name][0].shape, F32) for name in names for _ in range(4)]
    out_shape.append(jax.ShapeDtypeStruct(vp.shape[1:], F32))
    res = pl.pallas_call(
        body, name="adam_replicated", grid=(1,),
        in_specs=[_full(vp.shape), _full(mp.shape)] + [_full(a.shape) for a in flat],
        out_specs=[_full(o.shape) for o in out_shape], out_shape=out_shape, compiler_params=_params(1),
    )(vp, mp, *flat)
    return {name: res[4 * pi:4 * pi + 4] for pi, name in enumerate(names)}, res[-1]


class _Plan:
    def __init__(self):
        self.hosted, self.after = {}, {}

    def host(self, key, comm, then=None):
        self.hosted.setdefault(key, []).append(comm)
        if then is not None:
            self.after.setdefault(key, []).append(then)

    def comms(self, key):
        return self.hosted.pop(key, None)

    def done(self, key):
        for fn in self.after.pop(key, []):
            fn()

    def flush(self):
        while self.hosted:
            key = next(iter(self.hosted))
            _call(lambda: None, self.comms(key), name="exchange_after_%s_%d" % key, grid=(1,), in_specs=[], out_specs=[],
                  out_shape=[], args=())
            self.done(key)


VEC_TABLE = ("norm_g", "rg_conv_b", "rg_b_a", "rg_b_x", "rg_lambda", "ml_conv_b", "ml_norm_g")


def _vec_table(rep):
    rows = [rep[n] for n in VEC_TABLE]
    return jnp.stack(rows + [jnp.zeros_like(rows[0])] * (SUBLANES - len(rows)), axis=1)


def _layer_fwd(l, xl, mod3, wl, rep, plan, head=None):
    s, d = xl.shape
    t_big, t_mid = _tile_for(s, 512), _tile_for(s, 256)
    nh_ml = rep["ml_b_if"].shape[1] // 2
    vec = lambda name: _vec(rep["vecs"], l, VEC_TABLE.index(name))
    shift, scale, gate = (_vec(mod3, l, kk) for kk in range(3))
    hosted = lambda name: plan.comms((name, l)) if plan else None
    done = lambda name: plan.done((name, l)) if plan else None
    u, hbf = _in_fwd(xl, vec("norm_g"), scale, shift, wl["w_in_g"], 0, t_big, hosted("in_proj_fwd"))
    done("in_proj_fwd")
    h_rg, y_rg, *rg_gates = _rg_fwd(u, d, wl["rg_conv_w"], vec("rg_conv_b"), rep["rg_w_a_bf"][l], vec("rg_b_a"),
                                    rep["rg_w_x_bf"][l], vec("rg_b_x"), vec("rg_lambda"), t_mid, hosted("rglru_fwd"))
    done("rglru_fwd")
    q, k, v, gcol, pre = _ml_pre(u, d, wl["ml_conv_w"], vec("ml_conv_b"), wl["w_qkv"][0], wl["w_qkv"][1],
                                 wl["w_qkv"][2], wl["wif_pad"], wl["bif_pad"], t_mid, hosted("mlstm_proj_fwd"))
    done("mlstm_proj_fwd")
    grow = gcol[:, 0:16].T
    cell, y_ml, cs, ns, ms, mt = _ml_cell_fwd(q, k, v, gcol, grow, u, vec("ml_norm_g"), nh_ml, hosted("mlstm_cell_fwd"))
    done("mlstm_cell_fwd")
    res = _out_fwd(xl, y_rg, y_ml, gate, wl["w_out_g"], 0, t_big, hosted("out_proj_fwd"), head)
    done("out_proj_fwd")
    x_new, y = (res[0], res[1]) if head is None else (tuple(res[1:]), res[0])
    saved = dict(x=xl, u=u, hbf=hbf, h_rg=h_rg, y_rg=y_rg, q=q, k=k, v=v, gcol=gcol, grow=grow, cell=cell, y_ml=y_ml,
                 cs=cs, ns=ns, ms=ms, mt=mt, y=y, scale=scale, gate=gate, rg_gates=rg_gates, pre=pre)
    return x_new, saved


def _layer_bwd(l, dx, sv, wl, rep, plan, grads=None, split_last=False):
    s, d = dx.shape
    t_big, t_mid = _tile_for(s, 512), _tile_for(s, 256)
    nh_ml = rep["ml_b_if"].shape[1] // 2
    nd, _, _, w_cols = wl["w_in_g"].shape
    grads = {} if grads is None else grads
    vec = lambda name: _vec(rep["vecs"], l, VEC_TABLE.index(name))
    hosted = lambda name: plan.comms((name, l)) if plan else None
    done = lambda name: plan.done((name, l)) if plan else None
    dy_rg, dy_ml, gw_out, dgate = _out_bwd(dx, sv["gate"], sv["y"], sv["y_rg"], sv["y_ml"], wl["w_out_g"], 0, t_big,
                                           hosted("out_proj_bwd"))
    grads.update(w_out=gw_out)
    done("out_proj_bwd")
    dq, dk, dv, dgates, d_mlo, d_mlz, g_mlng = _ml_cell_bwd(
        dy_ml, sv["u"], sv["cell"], sv["q"], sv["k"], sv["v"], sv["gcol"], sv["grow"], sv["mt"], sv["cs"], sv["ns"],
        sv["ms"], vec("ml_norm_g"), nh_ml, hosted("mlstm_cell_bwd"))
    done("mlstm_cell_bwd")
    d_mlx, g_wq, g_wk, g_wv, g_wift, g_bif, g_mlcw, g_mlcb = _ml_pre_bwd(
        dq, dk, dv, dgates, sv["gcol"], sv["u"], sv["pre"], sv["q"], sv["k"], sv["v"], wl["ml_conv_w"],
        wl["w_qkv"][0], wl["w_qkv"][1], wl["w_qkv"][2], wl["wift_pad"], t_mid, hosted("mlstm_proj_bwd"))
    done("mlstm_proj_bwd")
    d_rgx, d_rgz, g_wa, g_wx, g_ba, g_bx, g_lam, g_rgcw, g_rgcb = _rg_bwd(
        dy_rg, sv["u"], sv["h_rg"], sv["rg_gates"], wl["rg_conv_w"], rep["rg_w_a_bf"][l], rep["rg_w_x_bf"][l],
        vec("rg_lambda"), t_mid, hosted("rglru_bwd"))
    grads.update(w_qkv=jnp.stack([g_wq, g_wk, g_wv]), rg_conv_w=g_rgcw[0:CONV_WIDTH], ml_conv_w=g_mlcw[0:CONV_WIDTH],
                 wif_t=g_wift[0:8], rg_w_a=g_wa, rg_w_x=g_wx, small_raw=(g_wq, g_wk, g_wv, g_rgcw, g_mlcw, g_wift))
    acc = dict(dgate=dgate, rg_conv_b=g_rgcb, rg_b_a=g_ba, rg_b_x=g_bx, rg_lambda=g_lam, ml_conv_b=g_mlcb,
               ml_b_if=g_bif, ml_norm_g=g_mlng)
    done("rglru_bwd")
    pieces = [d_rgx, d_rgz, d_mlx, d_mlo, d_mlz]
    grads.update(w_in=_in_bwd_w(pieces, sv["hbf"], w_cols, tuple(range(nd)), _tile_for(s, 1024), hosted("in_proj_bwd_w")))
    done("in_proj_bwd_w")
    n_tiles = s // t_mid
    counts = [n_tiles // 8, n_tiles - n_tiles // 8] if split_last and n_tiles >= 8 else [n_tiles]
    in_args = (pieces, sv["x"], dx, vec("norm_g"), sv["scale"], wl["w_in_g"], 0, t_mid)
    res, at = None, 0
    for key, count in zip(("in_proj_bwd_x", "in_proj_bwd_x_rest"), counts):
        res = _in_bwd(*in_args, hosted(key), (at, count), res)
        done(key)
        at += count
    dx, dscale, dshift, g_ng = res
    acc.update(norm_g=g_ng, dshift=dshift, dscale=dscale)
    grads.update(acc=acc, dmod=jnp.concatenate([dshift[0:1], dscale[0:1], dgate[0:1]], axis=1))
    return dx, grads


def _full_qkv(qkv_g, d):
    nd, _, rows3, dh = qkv_g.shape
    nh = d // dh
    rsh = rows3 // (3 * nh)
    return qkv_g.reshape(nd, 3, nh, rsh, dh).transpose(1, 2, 0, 3, 4).reshape(3, nh, nd * rsh, dh)


def _small_weights(small, l, ml_b_if):
    nd = small.shape[0]
    sm = small[:, l]
    cw = sm[:, 0:2 * CONV_WIDTH].reshape(nd, 2, CONV_WIDTH, LANES).transpose(1, 2, 0, 3).reshape(2, CONV_WIDTH, nd * LANES)
    if_rows = (sm.shape[1] - 2 * CONV_WIDTH) * LANES // 8
    wif_t = sm[:, 2 * CONV_WIDTH:].reshape(nd, 8, if_rows).transpose(1, 0, 2).reshape(8, nd * if_rows)
    wift_pad = jnp.pad(wif_t, ((0, LANES - 8), (0, 0))).astype(BF16)
    return dict(rg_conv_w=cw[0], ml_conv_w=cw[1], wift_pad=wift_pad, wif_pad=wift_pad.T,
                bif_pad=jnp.pad(ml_b_if[l], (0, LANES - 8)).reshape(1, LANES))


def kernel(x, c, norm_g, w_ada, b_ada, w_in, rg_conv_w, rg_conv_b, rg_w_a, rg_b_a, rg_w_x, rg_b_x, rg_lambda, ml_conv_w, ml_conv_b, ml_w_q, ml_w_k, ml_w_v, ml_w_if, ml_b_if, ml_norm_g, w_out, final_g, loss_target, m_norm_g, m_w_ada, m_b_ada, m_w_in, m_rg_conv_w, m_rg_conv_b, m_rg_w_a, m_rg_b_a, m_rg_w_x, m_rg_b_x, m_rg_lambda, m_ml_conv_w, m_ml_conv_b, m_ml_w_q, m_ml_w_k, m_ml_w_v, m_ml_w_if, m_ml_b_if, m_ml_norm_g, m_w_out, m_final_g, v_norm_g, v_w_ada, v_b_ada, v_w_in, v_rg_conv_w, v_rg_conv_b, v_rg_w_a, v_rg_b_a, v_rg_w_x, v_rg_b_x, v_rg_lambda, v_ml_conv_w, v_ml_conv_b, v_ml_w_q, v_ml_w_k, v_ml_w_v, v_ml_w_if, v_ml_b_if, v_ml_norm_g, v_w_out, v_final_g):
    given = dict(locals())
    nl = w_in.shape[0]
    d = x.shape[2]
    rep = {n: given[n] for n in REPLICATED}
    rep.update(rg_w_a_bf=rg_w_a.astype(BF16), rg_w_x_bf=rg_w_x.astype(BF16))
    bf = lambda a: a.astype(BF16)

    def qkv_shard(prefix):
        return jnp.stack([given[prefix + "ml_w_q"], given[prefix + "ml_w_k"], given[prefix + "ml_w_v"]], axis=1).reshape(
            nl, -1, ml_w_q.shape[-1])

    def small_shard(prefix):
        return _small_pack(given[prefix + "rg_conv_w"], given[prefix + "ml_conv_w"], given[prefix + "ml_w_if"])

    plan = _Plan()
    qkv = qkv_shard("")
    first_ici = _gather_ici_comm([bf(w_in[0:1]), small_shard("")])
    condition = _exchange_comm([jnp.broadcast_to(c, (SUBLANES, d))], True)
    _run_comms([first_ici, condition], "gather_first")
    first_fwd = _gather_fwd_comm(first_ici.results)
    wcols = w_ada.shape[2]
    me = 4 * lax.axis_index("x") + 2 * lax.axis_index("y") + lax.axis_index("c")
    b_cols = jnp.pad(lax.dynamic_slice_in_dim(b_ada, me * wcols, wcols, axis=1), ((0, SUBLANES - nl), (0, 0)))
    mod_cols, cact_all = _ada_mod(condition.results[0], w_ada, b_cols, [first_fwd])
    w_in_first, small = first_fwd.results
    wl = [_small_weights(small, l, ml_b_if) for l in range(nl)]
    wl[0]["w_in_g"] = w_in_first

    def gather_behind(arrs, ici_host, fwd_host, then):
        ici = _gather_ici_comm(arrs)

        def pass_on():
            fwd = _gather_fwd_comm(ici.results)
            plan.host(fwd_host, fwd, lambda: then(fwd.results))

        plan.host(ici_host, ici, pass_on)

    def got_out(l):
        return lambda r: wl[l].update(w_out_g=r[0], w_qkv=_full_qkv(r[1], d))

    gather_behind([bf(w_out[0:1]), bf(qkv[0:1])], ("in_proj_fwd", 0), ("rglru_fwd", 0), got_out(0))
    for l in range(1, nl):
        gather_behind([bf(w_in[l:l + 1])], ("rglru_fwd", l - 1), ("mlstm_cell_fwd", l - 1),
                      lambda r, l=l: wl[l].update(w_in_g=r[0]))
        gather_behind([bf(w_out[l:l + 1]), bf(qkv[l:l + 1])], ("mlstm_cell_fwd", l - 1), ("out_proj_fwd", l - 1), got_out(l))

    mod_blocks = _exchange([mod_cols], False, "scatter_modulation")[0]
    mod3 = mod_blocks[:, 0:nl].transpose(1, 0, 2).reshape(nl, 3, d)
    rep["vecs"] = _vec_table(rep)
    saved, xl = [], x[0]
    for l in range(nl):
        head = (final_g.reshape(1, -1), loss_target[0]) if l == nl - 1 else None
        xl, sv = _layer_fwd(l, xl, mod3, wl[l], rep, plan, head)
        saved.append(sv)
    grad_x, loss_p, g_final = xl

    keys = ("w_in", "w_out", "w_qkv", "small")
    parity = lax.axis_index("c").astype(jnp.int32).reshape(1)
    grads, recv = [None] * nl, [None] * nl

    def small_parts(g):
        return list(_small_grad_slots(*g["small_raw"]))

    def reduce_behind(l, host_layer):
        parts = [grads[l]["w_in"], grads[l]["w_out"]]
        swap = _core_swap_comm(parts)
        direct = _exchange_comm(small_parts(grads[l]), False)

        def summed():
            sums = [_pair_sum(a, o, parity, "pair_sum_%s_layer%d" % (key, l)) for key, a, o in zip(keys, parts, swap.results)]
            big = _chip_swap_comm([sums[0]])
            rest = _chip_swap_comm([sums[1]])
            plan.host(("mlstm_cell_bwd", host_layer), big)
            plan.host(("rglru_bwd", host_layer), rest,
                      lambda: recv.__setitem__(l, big.results + rest.results + direct.results))

        plan.host(("out_proj_bwd", host_layer), swap, summed)
        plan.host(("mlstm_proj_bwd", host_layer), direct)

    first, own = {}, {}

    def reduce_own(names, parts_fn, ready_key, swap_key, chip_key):
        def go():
            parts = parts_fn()
            swap = _core_swap_comm(parts)

            def summed():
                sums = [_pair_sum(a, o, parity, "pair_sum_%s_layer0" % n) for n, a, o in zip(names, parts, swap.results)]
                chip = _chip_swap_comm(sums)
                plan.host(chip_key, chip, lambda: own.update(zip(names, chip.results)))

            plan.host(swap_key, swap, summed)

        plan.after.setdefault(ready_key, []).append(go)

    reduce_own(["w_out"], lambda: [first["w_out"]], ("out_proj_bwd", 0), ("mlstm_cell_bwd", 0), ("mlstm_proj_bwd", 0))
    reduce_own(["w_in"], lambda: [first["w_in"]], ("in_proj_bwd_w", 0), ("in_proj_bwd_x", 0), ("in_proj_bwd_x_rest", 0))

    def small_own():
        direct = _exchange_comm(small_parts(first), False)
        plan.host(("in_proj_bwd_w", 0), direct, lambda: own.update(w_qkv=direct.results[0], small=direct.results[1]))

    plan.after.setdefault(("rglru_bwd", 0), []).append(small_own)

    matrices = {}

    def reduce_matrices():
        layers = [grads[l] if l > 0 else first for l in range(nl)]
        mp = jnp.stack([jnp.stack([g["rg_w_a"], g["rg_w_x"]]) for g in layers]).reshape(N_DEV, -1, LANES).astype(BF16)
        scatter = _exchange_comm([mp], False)

        def summed():
            gather = _exchange_comm(_sum_parts(scatter.results, "sum_replicated_matrices"), True)
            plan.host(("in_proj_bwd_x", 0), gather, lambda: matrices.update(mp=gather.results[0].reshape(-1, LANES)))

        plan.host(("in_proj_bwd_w", 0), scatter, summed)

    plan.after.setdefault(("rglru_bwd", 0), []).append(reduce_matrices)

    for l in reversed(range(nl)):
        if l > 0:
            grad_x, grads[l] = _layer_bwd(l, grad_x, saved[l], wl[l], rep, plan)
            reduce_behind(l, l - 1)
        else:
            grad_x, grads[l] = _layer_bwd(l, grad_x, saved[l], wl[l], rep, plan, first, True)
    plan.flush()
    recv[0] = [own[key] for key in keys]

    shard = {p: dict(w_in=given[p + "w_in"], w_out=given[p + "w_out"], w_qkv=qkv_shard(p), small=small_shard(p))
             for p in ("", "m_", "v_")}
    res = {}
    for ki, key in enumerate(keys):
        res[key] = _reduce_adam([recv[l][ki] for l in range(nl)], shard[""][key], shard["m_"][key], shard["v_"][key],
                                "reduce_adam_" + key)

    dmods = jnp.concatenate([grads[l]["dmod"] for l in range(nl)], axis=0)
    dmod_blocks = jnp.pad(dmods.reshape(nl, N_DEV, wcols).transpose(1, 0, 2), ((0, 0), (0, SUBLANES - nl), (0, 0)))
    vp = _pack_rows([grads[l]["acc"][n] for l in range(nl) for n in REP_ROWS] + [g_final, loss_p], d)
    (dmod_recv,), (vp_all,) = _run_comms([_exchange_comm([dmod_blocks], False), _exchange_comm([vp], True)], "tail_exchange")
    res["w_ada"] = _ada_grad_adam(cact_all.T, dmod_recv[:, 0:nl].transpose(1, 0, 2), w_ada, m_w_ada, v_w_ada)
    mp_r = matrices["mp"]
    lanes = lambda a: jnp.pad(a, ((0, 0), (0, LANES - a.shape[1])))
    shaped = dict(ml_b_if=lanes, final_g=lambda a: a.reshape(1, d))
    names = [n for n in REPLICATED if n != "b_ada"] + ["b_ada"]
    rep_res, vp_r = _adam_replicated(vp_all, mp_r, {n: tuple(shaped.get(n, lambda a: a)(given[p + n]) for p in ("", "m_", "v_"))
                                                    for n in names}, nl)
    unshaped = dict(ml_b_if=lambda a: a[:, 0:ml_b_if.shape[1]], final_g=lambda a: a.reshape(d))
    rep_out = [{n: unshaped.get(n, lambda a: a)(rep_res[n][kind]) for n in names} for kind in range(4)]
    loss = vp_r[nl * len(REP_ROWS) + 1, 0]

    if_rows = ml_w_if.shape[1]
    order = ("norm_g", "w_ada", "b_ada", "w_in", "rg_conv_w", "rg_conv_b", "rg_w_a", "rg_b_a", "rg_w_x", "rg_b_x",
             "rg_lambda", "ml_conv_w", "ml_conv_b", "ml_w_q", "ml_w_k", "ml_w_v", "ml_w_if", "ml_b_if", "ml_norm_g",
             "w_out", "final_g")
    outs = [loss, grad_x[None]]
    for kind in range(4):
        qkv_k = res["w_qkv"][kind].reshape((nl, 3) + ml_w_q.shape[1:])
        rg_cw, ml_cw, wif = _small_unpack(res["small"][kind], if_rows)
        sharded = dict(w_ada=res["w_ada"][kind], w_in=res["w_in"][kind], w_out=res["w_out"][kind], ml_w_q=qkv_k[:, 0],
                       ml_w_k=qkv_k[:, 1], ml_w_v=qkv_k[:, 2], rg_conv_w=rg_cw, ml_conv_w=ml_cw, ml_w_if=wif)
        for n in order:
            outs.append(sharded[n] if n in sharded else rep_out[kind][n])
    return tuple(outs)
```

```python
import functools

import jax
import jax.numpy as jnp
from jax import lax
from jax.experimental import pallas as pl
from jax.experimental.pallas import tpu as pltpu

F32 = jnp.float32
BF16 = jnp.bfloat16
MESH_AXES = ("x", "y", "c")
N_DEV = 8
EPS = 1e-6
RG_C = 8.0
ML_CHUNK = 128
CONV_WIDTH = 4
ADAM_LR = 0.001
ADAM_B1 = 0.9
ADAM_B2 = 0.999
ADAM_EPS = 1e-08
ADAM_WD = 0.01
ADAM_STEP = 10
NEG_BIG = -1e30
LANES = 128
SUBLANES = 8
VMEM_LIMIT = 56 * 1024 * 1024
HI = lax.Precision.HIGHEST


def _params(n_grid):
    return pltpu.CompilerParams(dimension_semantics=("arbitrary",) * n_grid, vmem_limit_bytes=VMEM_LIMIT)


def _mm(a, b):
    return jnp.dot(a.astype(BF16), b.astype(BF16), preferred_element_type=F32)


def _mm_nt(a, b):
    return lax.dot_general(a.astype(BF16), b.astype(BF16), (((1,), (1,)), ((), ())), preferred_element_type=F32)


def _mm_tn(a, b):
    return lax.dot_general(a.astype(BF16), b.astype(BF16), (((0,), (0,)), ((), ())), preferred_element_type=F32)


def _mm_hi(a, b):
    return jnp.dot(a, b, precision=HI, preferred_element_type=F32)


def _sigmoid(x):
    return 1.0 / (1.0 + jnp.exp(-x))


def _softplus(x):
    return jnp.maximum(x, 0.0) + jnp.log(1.0 + jnp.exp(-jnp.abs(x)))


def _neg_expm1(x):
    poly = -x * (1.0 + x * (0.5 + x * (1.0 / 6.0 + x * (1.0 / 24.0 + x * (1.0 / 120.0)))))
    return jnp.where(jnp.abs(x) < 0.05, poly, 1.0 - jnp.exp(x))


def _iota(shape, dim):
    return lax.broadcasted_iota(jnp.int32, shape, dim)


def _colsum(x):
    return jnp.sum(x, axis=0, keepdims=True)


def _rowsum(x):
    return jnp.sum(x, axis=1, keepdims=True)


def _col(x, j):
    return _rowsum(jnp.where(_iota(x.shape, 1) == j, x, 0.0))


def _row(x, j):
    return _colsum(jnp.where(_iota(x.shape, 0) == j, x, 0.0))


def _shift_down(x, j, prev8):
    if j == 0:
        return x
    t = x.shape[0]
    main = jnp.where(_iota(x.shape, 0) >= j, pltpu.roll(x, j, 0), 0.0)
    fix = jnp.where(_iota(prev8.shape, 0) < j, pltpu.roll(prev8, j, 0), 0.0)
    return jnp.concatenate([main[0:SUBLANES] + fix, main[SUBLANES:t]], axis=0)


def _shift_up(x, j, next8):
    if j == 0:
        return x
    t = x.shape[0]
    main = jnp.where(_iota(x.shape, 0) < t - j, pltpu.roll(x, t - j, 0), 0.0)
    fix = jnp.where(_iota(next8.shape, 0) >= SUBLANES - j, pltpu.roll(next8, SUBLANES - j, 0), 0.0)
    return jnp.concatenate([main[0:t - SUBLANES], main[t - SUBLANES:t] + fix], axis=0)


def _conv(x, prev8, w_ref):
    y = w_ref[CONV_WIDTH - 1:CONV_WIDTH, :] * x
    for j in range(1, CONV_WIDTH):
        y = y + w_ref[CONV_WIDTH - 1 - j:CONV_WIDTH - j, :] * _shift_down(x, j, prev8)
    return y


def _conv_bwd(dy, x, next8, w_ref, gw_ref):
    dx = None
    for j in range(CONV_WIDTH):
        k = CONV_WIDTH - 1 - j
        up = _shift_up(dy, j, next8)
        gw_ref[k:k + 1, :] += _colsum(up * x)
        term = w_ref[k:k + 1, :] * up
        dx = term if dx is None else dx + term
    return dx


def _scan_into(a, b, carry, out_ref, reverse):
    t, c = a.shape
    groups = t // SUBLANES
    a3 = a.reshape(groups, SUBLANES, c)
    b3 = b.reshape(groups, SUBLANES, c)
    sub = _iota(a3.shape, 1)
    for step in (1, 2, 4):
        keep = sub < SUBLANES - step if reverse else sub >= step
        shift = SUBLANES - step if reverse else step
        a_s = jnp.where(keep, pltpu.roll(a3, shift, 1), 1.0)
        b_s = jnp.where(keep, pltpu.roll(b3, shift, 1), 0.0)
        b3 = a3 * b_s + b3
        a3 = a3 * a_s
    for g in (reversed(range(groups)) if reverse else range(groups)):
        rows = slice(g * SUBLANES, (g + 1) * SUBLANES)
        out_ref[rows, :] = b3[g] + a3[g] * carry
        edge = g * SUBLANES if reverse else (g + 1) * SUBLANES - 1
        carry = out_ref[edge:edge + 1, :]


def _blockdiag(x, w_ref, transpose_w=False):
    nh, dh, _ = w_ref.shape
    outs = []
    for h in range(nh):
        xs = x[:, h * dh:(h + 1) * dh]
        outs.append(_mm_nt(xs, w_ref[h]) if transpose_w else _mm(xs, w_ref[h]))
    return jnp.concatenate(outs, axis=1)


def _rg_gates(xc, wa_ref, ba_ref, wx_ref, bx_ref, lam_ref):
    r = _sigmoid(_blockdiag(xc, wa_ref) + ba_ref[...])
    ig = _sigmoid(_blockdiag(xc, wx_ref) + bx_ref[...])
    sp = _softplus(-lam_ref[...])
    log_a = -RG_C * r * sp
    a = jnp.exp(log_a)
    beta = jnp.sqrt(_neg_expm1(2.0 * log_a))
    return r, ig, sp, a, beta


def _bcast8(row):
    return jnp.broadcast_to(row, (SUBLANES, row.shape[1]))


def _full(shape):
    nd = len(shape)
    return pl.BlockSpec(shape, lambda *_: (0,) * nd)


class _Comm:
    def __init__(self, arrays, out_shapes, sems, start, finish, aliases=()):
        self.arrays, self.out_shapes, self.sems = list(arrays), list(out_shapes), list(sems)
        self.start, self.finish, self.aliases = start, finish, tuple(aliases)
        self.results = None


class _RowOf:
    def __init__(self, ref, k):
        self.ref, self.k = ref, k

    def __getitem__(self, idx):
        cols = slice(None) if idx is Ellipsis else idx[1]
        return self.ref[0, self.k:self.k + 1, cols]


class _PartOf:
    def __init__(self, ref, rows=None, cols=None, lead=None):
        self.ref, self.rows, self.cols, self.lead = ref, rows, cols, lead
        if rows is not None:
            self.shape = (rows.stop - rows.start,) + tuple(ref.shape[1:])

    def _at(self, idx):
        if self.lead is not None:
            return (self.lead,) + tuple(idx[1:])
        if self.cols is not None:
            return (slice(None), self.cols)
        return (self.rows, slice(None) if idx is Ellipsis else idx[1])

    def __getitem__(self, idx):
        return self.ref[self._at(idx)]

    def __setitem__(self, idx, value):
        self.ref[self._at(idx)] = value


def _vec(table, layer, k):
    return ("row", table, layer, k)


def _is_row(arg):
    return isinstance(arg, tuple) and len(arg) == 4 and arg[0] == "row"


def _call(body, comms, *, name, grid, in_specs, out_specs, out_shape, args, scratch_shapes=(), aliases=None):
    comms = [cm for cm in (comms or []) if cm is not None]
    rows = {i: a[3] for i, a in enumerate(args) if _is_row(a)}
    in_specs = [pl.BlockSpec((1,) + a[1].shape[1:], functools.partial(lambda layer, *_: (layer, 0, 0), a[2]))
                if _is_row(a) else sp for a, sp in zip(args, in_specs)]
    args = tuple(a[1] if _is_row(a) else a for a in args)
    n_in, n_out, n_sc = len(args), len(out_shape), len(scratch_shapes)
    c_arrays = [a for cm in comms for a in cm.arrays]
    c_outs = [o for cm in comms for o in cm.out_shapes]
    c_sems = [sm for cm in comms for sm in cm.sems]
    aliases, a_at, o_at = dict(aliases or {}), n_in, n_out
    for cm in comms:
        for (i, j) in cm.aliases:
            aliases[a_at + i] = o_at + j
        a_at += len(cm.arrays)
        o_at += len(cm.out_shapes)

    def wrapped(*refs):
        ins, c_in = refs[:n_in], refs[n_in:n_in + len(c_arrays)]
        ins = [_RowOf(r, rows[i]) if i in rows else r for i, r in enumerate(ins)]
        at = n_in + len(c_arrays)
        outs, c_out = refs[at:at + n_out], refs[at + n_out:at + n_out + len(c_outs)]
        at += n_out + len(c_outs)
        scr, sems = refs[at:at + n_sc], refs[at + n_sc:]
        views, ia, io, isem = [], 0, 0, 0
        for cm in comms:
            views.append((c_in[ia:ia + len(cm.arrays)], c_out[io:io + len(cm.out_shapes)], sems[isem:isem + len(cm.sems)]))
            ia, io, isem = ia + len(cm.arrays), io + len(cm.out_shapes), isem + len(cm.sems)
        if comms:
            @pl.when(pl.program_id(0) == 0)
            def _():
                for cm, view in zip(comms, views):
                    cm.start(*view)

        body(*ins, *outs, *scr)
        if comms:
            @pl.when(pl.program_id(0) == grid[0] - 1)
            def _():
                for cm, view in zip(comms, views):
                    cm.finish(*view)

    hbm = pl.BlockSpec(memory_space=pl.ANY)
    res = pl.pallas_call(
        wrapped, name=name, grid=grid,
        in_specs=list(in_specs) + [hbm] * len(c_arrays), out_specs=list(out_specs) + [hbm] * len(c_outs),
        out_shape=list(out_shape) + c_outs, scratch_shapes=list(scratch_shapes) + c_sems,
        input_output_aliases=aliases, compiler_params=_params(len(grid)),
    )(*args, *c_arrays)
    at = n_out
    for cm in comms:
        cm.results = list(res[at:at + len(cm.out_shapes)])
        at += len(cm.out_shapes)
    return list(res[:n_out])


def _join_columns(w_ref, wcat):
    nd, _, _, w = w_ref.shape

    @pl.when(pl.program_id(0) == 0)
    def _():
        for j in range(nd):
            wcat[:, j * w:(j + 1) * w] = w_ref[j, 0]


def _in_fwd(x, ng, scale, shift, w_in_g, layer, tile, comms=None):
    s, d = x.shape
    nd, _, _, w = w_in_g.shape

    def body(x_ref, ng_ref, sc_ref, sh_ref, w_ref, u_ref, h_ref, wcat):
        _join_columns(w_ref, wcat)
        xv = x_ref[...]
        rs = lax.rsqrt(jnp.mean(xv * xv, axis=1, keepdims=True) + EPS)
        hb = (xv * rs * ng_ref[...] * (1.0 + sc_ref[...]) + sh_ref[...]).astype(BF16)
        h_ref[...] = hb
        u_ref[...] = jnp.dot(hb, wcat[...], preferred_element_type=F32)

    return _call(
        body, comms, name="in_proj_fwd", grid=(s // tile,),
        in_specs=[pl.BlockSpec((tile, d), lambda i: (i, 0)), _full((1, d)), _full((1, d)), _full((1, d)),
                  pl.BlockSpec((nd, 1, d, w), lambda i: (0, layer, 0, 0), pipeline_mode=pl.Buffered(1))],
        out_specs=[pl.BlockSpec((tile, nd * w), lambda i: (i, 0)), pl.BlockSpec((tile, d), lambda i: (i, 0))],
        out_shape=[jax.ShapeDtypeStruct((s, nd * w), F32), jax.ShapeDtypeStruct((s, d), BF16)],
        scratch_shapes=[pltpu.VMEM((d, nd * w), BF16)],
        args=(x, ng, scale, shift, w_in_g))


def _rg_fwd(u, d, conv_w, conv_b, w_a, b_a, w_x, b_x, lam, tile, comms=None):
    s = u.shape[0]

    def body(x_ref, z_ref, cw_ref, cb_ref, wa_ref, ba_ref, wx_ref, bx_ref, lam_ref,
             h_ref, y_ref, xc_ref, r_ref, i_ref, a_ref, beta_ref, prev8, hcar):
        @pl.when(pl.program_id(0) == 0)
        def _():
            prev8[...] = jnp.zeros_like(prev8)
            hcar[...] = jnp.zeros_like(hcar)

        x = x_ref[...]
        xc = _conv(x, prev8[...], cw_ref) + cb_ref[...]
        prev8[...] = x[tile - SUBLANES:tile, :]
        r, ig, _, a, beta = _rg_gates(xc, wa_ref, ba_ref, wx_ref, bx_ref, lam_ref)
        xc_ref[...] = xc
        r_ref[...] = r
        i_ref[...] = ig
        a_ref[...] = a
        beta_ref[...] = beta
        _scan_into(a, beta * ig * xc, hcar[SUBLANES - 1:SUBLANES, :], h_ref, False)
        h = h_ref[...]
        hcar[...] = h[tile - SUBLANES:tile, :]
        z = z_ref[...]
        y_ref[...] = (h * z * _sigmoid(z)).astype(BF16)

    vec = _full((1, d))
    return _call(
        body, comms, name="rglru_fwd", grid=(s // tile,),
        in_specs=[pl.BlockSpec((tile, d), lambda i: (i, 0)), pl.BlockSpec((tile, d), lambda i: (i, 1)),
                  _full(conv_w.shape), vec, _full(w_a.shape), vec, _full(w_x.shape), vec, vec],
        out_specs=[pl.BlockSpec((tile, d), lambda i: (i, 0))] * 7,
        out_shape=[jax.ShapeDtypeStruct((s, d), F32), jax.ShapeDtypeStruct((s, d), BF16)] + [jax.ShapeDtypeStruct((s, d), F32)] * 5,
        scratch_shapes=[pltpu.VMEM((SUBLANES, d), F32), pltpu.VMEM((SUBLANES, d), F32)],
        args=(u, u, conv_w, conv_b, w_a, b_a, w_x, b_x, lam))


def _ml_pre(u, d, conv_w, conv_b, w_q, w_k, w_v, wif, bif, tile, comms=None):
    s = u.shape[0]
    nh = w_q.shape[0]

    def body(x_ref, cw_ref, cb_ref, wq_ref, wk_ref, wv_ref, wif_ref, bif_ref, q_ref, k_ref, v_ref, g_ref, pre_ref, prev8):
        @pl.when(pl.program_id(0) == 0)
        def _():
            prev8[...] = jnp.zeros_like(prev8)

        x = x_ref[...]
        pre = _conv(x, prev8[...], cw_ref) + cb_ref[...]
        prev8[...] = x[tile - SUBLANES:tile, :]
        xc = pre * _sigmoid(pre)
        q = _blockdiag(xc, wq_ref)
        k = _blockdiag(xc, wk_ref)
        v = _blockdiag(x, wv_ref)
        pre_ref[...] = pre
        q_ref[...] = q
        k_ref[...] = k
        v_ref[...] = v
        g = _mm(q, wif_ref[0:d, :]) + _mm(k, wif_ref[d:2 * d, :]) + _mm(v, wif_ref[2 * d:3 * d, :]) + bif_ref[...]
        lane = _iota(g.shape, 1)
        gl = jnp.where(lane < 4, g, jnp.where(lane < 8, -_softplus(-g), 0.0))
        tri = jnp.where(_iota((ML_CHUNK, ML_CHUNK), 1) <= _iota((ML_CHUNK, ML_CHUNK), 0), 1.0, 0.0)
        cums = [_mm_hi(tri, gl[c * ML_CHUNK:(c + 1) * ML_CHUNK, :]) for c in range(tile // ML_CHUNK)]
        cum = cums[0] if len(cums) == 1 else jnp.concatenate(cums, axis=0)
        g_ref[...] = gl + jnp.where((lane >= 8) & (lane < 12), pltpu.roll(cum, 4, 1), 0.0)

    vec = _full((1, d))
    return _call(
        body, comms, name="mlstm_proj_fwd", grid=(s // tile,),
        in_specs=[pl.BlockSpec((tile, d), lambda i: (i, 2)), _full(conv_w.shape), vec,
                  _full(w_q.shape), _full(w_k.shape), _full(w_v.shape), _full(wif.shape), _full((1, LANES))],
        out_specs=[pl.BlockSpec((tile, d), lambda i: (i, 0))] * 3 + [pl.BlockSpec((tile, LANES), lambda i: (i, 0)),
                                                                     pl.BlockSpec((tile, d), lambda i: (i, 0))],
        out_shape=[jax.ShapeDtypeStruct((s, d), F32)] * 3 + [jax.ShapeDtypeStruct((s, LANES), F32),
                                                             jax.ShapeDtypeStruct((s, d), F32)],
        scratch_shapes=[pltpu.VMEM((SUBLANES, d), F32)],
        args=(u, conv_w, conv_b, w_q, w_k, w_v, wif, bif))


CELL_CHUNKS_PER_STEP = 4
CELL_BWD_CHUNKS_PER_STEP = 2


def _cell_chunk(h, nh, q_ref, k_ref, v_ref, gc, gr, m_prev, c_h, n_h, m_t=None, r0=0):
    lc = ML_CHUNK
    dh = q_ref.shape[1] // nh
    sl = slice(h * dh, (h + 1) * dh)
    qh = q_ref[r0:r0 + lc, sl]
    kh = k_ref[r0:r0 + lc, sl] * (dh ** -0.5)
    vh = v_ref[r0:r0 + lc, sl]
    li_c = _col(gc, h)
    b_c = _col(gc, 8 + h)
    lib_r = _row(gr, h) - _row(gr, 8 + h)
    b_last = _colsum(jnp.where(_iota((lc, 1), 0) == lc - 1, b_c, 0.0))
    causal = _iota((lc, lc), 1) <= _iota((lc, lc), 0)
    dmat = jnp.where(causal, b_c + lib_r, NEG_BIG)
    m_inter = b_c + m_prev
    if m_t is None:
        m_t = jnp.maximum(m_inter, jnp.max(dmat, axis=1, keepdims=True))
    w_intra = jnp.exp(dmat - m_t)
    w_inter = jnp.exp(m_inter - m_t)
    amat = _mm_nt(qh, kh)
    smat = amat * w_intra
    qc = _mm(qh, c_h)
    qn = _rowsum(qh * n_h)
    den = _rowsum(smat) + w_inter * qn
    gst = b_last - b_c + li_c
    m_new = jnp.maximum(b_last + m_prev, jnp.max(gst, axis=0, keepdims=True))
    w_state = jnp.exp(gst - m_new)
    decay = jnp.exp(b_last + m_prev - m_new)
    return dict(sl=sl, qh=qh, kh=kh, vh=vh, m_t=m_t, w_intra=w_intra, w_inter=w_inter, smat=smat, qc=qc, qn=qn,
                den=den, m_new=m_new, w_state=w_state, decay=decay)


def _ml_cell_fwd(q, k, v, gcol, grow, u, ng, nh, comms=None):
    s, d = q.shape
    lc = ML_CHUNK
    nc = s // lc
    dh = d // nh

    per = CELL_CHUNKS_PER_STEP if nc % CELL_CHUNKS_PER_STEP == 0 else 1

    def body(q_ref, k_ref, v_ref, gc_ref, gr_ref, o_ref, z_ref, ng_ref,
             cell_ref, y_ref, cs_ref, ns_ref, ms_ref, mt_ref, c_sc, n_sc, m_sc):
        @pl.when(pl.program_id(0) == 0)
        def _():
            c_sc[...] = jnp.zeros_like(c_sc)
            n_sc[...] = jnp.zeros_like(n_sc)
            m_sc[...] = jnp.zeros_like(m_sc)

        lane = _iota((lc, LANES), 1)
        for cc in range(per):
            rows = slice(cc * lc, (cc + 1) * lc)
            gc = gc_ref[rows, :]
            gr = gr_ref[:, rows]
            mt_acc = jnp.zeros((lc, LANES), F32)
            for h in range(nh):
                c_h = c_sc[h]
                n_h = n_sc[h, 0:1, :]
                m_prev = jnp.max(m_sc[h, 0:1, :], axis=1, keepdims=True)
                cs_ref[cc, h] = c_h
                ns_ref[cc, h] = n_sc[h]
                ms_ref[cc, h] = m_sc[h]
                t = _cell_chunk(h, nh, q_ref, k_ref, v_ref, gc, gr, m_prev, c_h, n_h, r0=cc * lc)
                sl = t["sl"]
                num = _mm(t["smat"], t["vh"]) + t["w_inter"] * t["qc"]
                cell_h = num / jnp.maximum(jnp.abs(t["den"]), jnp.exp(-t["m_t"]))
                mt_acc = jnp.where(lane == h, t["m_t"], mt_acc)
                kw = t["kh"] * t["w_state"]
                c_sc[h] = t["decay"] * c_h + _mm_tn(kw, t["vh"])
                n_sc[h] = _bcast8(t["decay"] * n_h + _colsum(kw))
                m_sc[h] = jnp.broadcast_to(t["m_new"], (SUBLANES, LANES))
                hg = _sigmoid(o_ref[rows, sl]) * cell_h
                hn = hg * lax.rsqrt(jnp.mean(hg * hg, axis=1, keepdims=True) + EPS)
                z = z_ref[rows, sl]
                cell_ref[rows, sl] = cell_h
                y_ref[rows, sl] = (hn * ng_ref[:, sl] * z * _sigmoid(z)).astype(BF16)
            mt_ref[rows, :] = mt_acc

    tok = pl.BlockSpec((per * lc, d), lambda c: (c, 0))
    return _call(
        body, comms, name="mlstm_cell_fwd", grid=(nc // per,),
        in_specs=[tok, tok, tok, pl.BlockSpec((per * lc, LANES), lambda c: (c, 0)),
                  pl.BlockSpec((16, per * lc), lambda c: (0, c)),
                  pl.BlockSpec((per * lc, d), lambda c: (c, 3)), pl.BlockSpec((per * lc, d), lambda c: (c, 4)), _full((1, d))],
        out_specs=[tok, tok, pl.BlockSpec((per, nh, dh, dh), lambda c: (c, 0, 0, 0)),
                   pl.BlockSpec((per, nh, SUBLANES, dh), lambda c: (c, 0, 0, 0)),
                   pl.BlockSpec((per, nh, SUBLANES, LANES), lambda c: (c, 0, 0, 0)),
                   pl.BlockSpec((per * lc, LANES), lambda c: (c, 0))],
        out_shape=[jax.ShapeDtypeStruct((s, d), F32), jax.ShapeDtypeStruct((s, d), BF16),
                   jax.ShapeDtypeStruct((nc, nh, dh, dh), F32), jax.ShapeDtypeStruct((nc, nh, SUBLANES, dh), F32),
                   jax.ShapeDtypeStruct((nc, nh, SUBLANES, LANES), F32), jax.ShapeDtypeStruct((s, LANES), F32)],
        scratch_shapes=[pltpu.VMEM((nh, dh, dh), F32), pltpu.VMEM((nh, SUBLANES, dh), F32),
                        pltpu.VMEM((nh, SUBLANES, LANES), F32)],
        args=(q, k, v, gcol, grow, u, u, ng))


def _out_fwd(x, y_rg, y_ml, gate, w_out_g, layer, tile, comms=None, head=None):
    s, d = x.shape
    nd, _, r, _ = w_out_g.shape

    def body(x_ref, yr_ref, ym_ref, g_ref, w_ref, *rest):
        ycat = jnp.concatenate([yr_ref[...].astype(BF16), ym_ref[...].astype(BF16)], axis=1)
        acc = jnp.dot(ycat, w_ref[...].reshape(nd * r, d), preferred_element_type=F32)
        xn = x_ref[...] + g_ref[...] * acc
        if head is None:
            xn_ref, y_ref = rest
            y_ref[...] = acc
            xn_ref[...] = xn
            return
        fg_ref, t_ref, y_ref, dx_ref, loss_ref, gg_ref = rest
        y_ref[...] = acc

        @pl.when(pl.program_id(0) == 0)
        def _():
            loss_ref[...] = jnp.zeros_like(loss_ref)
            gg_ref[...] = jnp.zeros_like(gg_ref)

        fg = fg_ref[...]
        rs = lax.rsqrt(jnp.mean(xn * xn, axis=1, keepdims=True) + EPS)
        xh = xn * rs
        e = xh * fg - t_ref[...]
        loss_ref[...] += jnp.broadcast_to(_colsum(_rowsum(e * e)) * (0.5 / d), loss_ref.shape)
        dy = e * (1.0 / d)
        gg_ref[...] += _bcast8(_colsum(dy * xh))
        dxh = dy * fg
        dx_ref[...] = rs * (dxh - xh * jnp.mean(dxh * xh, axis=1, keepdims=True))

    tok = pl.BlockSpec((tile, d), lambda i: (i, 0))
    in_specs = [tok, tok, tok, _full((1, d)), pl.BlockSpec((nd, 1, r, d), lambda i: (0, layer, 0, 0))]
    if head is None:
        return _call(body, comms, name="out_proj_fwd", grid=(s // tile,), in_specs=in_specs, out_specs=[tok, tok],
                     out_shape=[jax.ShapeDtypeStruct((s, d), F32)] * 2, args=(x, y_rg, y_ml, gate, w_out_g))
    return _call(
        body, comms, name="out_proj_loss", grid=(s // tile,),
        in_specs=in_specs + [_full((1, d)), tok],
        out_specs=[tok, tok, _full((SUBLANES, LANES)), _full((SUBLANES, d))],
        out_shape=[jax.ShapeDtypeStruct((s, d), F32)] * 2 + [jax.ShapeDtypeStruct((SUBLANES, LANES), F32),
                                                             jax.ShapeDtypeStruct((SUBLANES, d), F32)],
        args=(x, y_rg, y_ml, gate, w_out_g, *head))


def _ml_out_stage_bwd(dy, cell, o, z, ng):
    so = _sigmoid(o)
    hg = so * cell
    rinv = lax.rsqrt(jnp.mean(hg * hg, axis=1, keepdims=True) + EPS)
    hn = hg * rinv
    sz = _sigmoid(z)
    dz = dy * hn * ng * (sz + z * sz * (1.0 - sz))
    dymid = dy * z * sz
    dhn = dymid * ng
    dhg = rinv * (dhn - hn * jnp.mean(dhn * hn, axis=1, keepdims=True))
    return dz, dhg * cell * so * (1.0 - so), dhg * so, _colsum(dymid * hn)


def _out_bwd(dxo, gate, y, y_rg, y_ml, w_out_g, layer, tile, comms=None):
    s, d = dxo.shape
    nd, _, r, _ = w_out_g.shape

    def body(dx_ref, g_ref, y_ref, yr_ref, ym_ref, w_ref, dyr_ref, dym_ref, gw_ref, dg_ref):
        @pl.when(pl.program_id(0) == 0)
        def _():
            gw_ref[...] = jnp.zeros_like(gw_ref)
            dg_ref[...] = jnp.zeros_like(dg_ref)

        dxv = dx_ref[...]
        dg_ref[...] += _bcast8(_colsum(dxv * y_ref[...]))
        dyb = (dxv * g_ref[...]).astype(BF16)
        dycat = lax.dot_general(dyb, w_ref[...].reshape(nd * r, d), (((1,), (1,)), ((), ())), preferred_element_type=F32)
        dyr_ref[...] = dycat[:, 0:d]
        dym_ref[...] = dycat[:, d:2 * d]
        ycat = jnp.concatenate([yr_ref[...].astype(BF16), ym_ref[...].astype(BF16)], axis=1)
        gw_ref[...] += lax.dot_general(ycat, dyb, (((0,), (0,)), ((), ())), preferred_element_type=F32).reshape(nd, r, d)

    tok = pl.BlockSpec((tile, d), lambda i: (i, 0))
    return _call(
        body, comms, name="out_proj_bwd", grid=(s // tile,),
        in_specs=[tok, _full((1, d)), tok, tok, tok, pl.BlockSpec((nd, 1, r, d), lambda i: (0, layer, 0, 0))],
        out_specs=[tok, tok, _full((nd, r, d)), _full((SUBLANES, d))],
        out_shape=[jax.ShapeDtypeStruct((s, d), F32)] * 2 + [jax.ShapeDtypeStruct((nd, r, d), F32),
                                                             jax.ShapeDtypeStruct((SUBLANES, d), F32)],
        args=(dxo, gate, y, y_rg, y_ml, w_out_g))


def _ml_cell_bwd(dy_ml, u, cell, q, k, v, gcol, grow, mt, cs, ns, ms, ng, nh, comms=None):
    s, d = q.shape
    lc = ML_CHUNK
    nc = s // lc
    dh = d // nh

    per = CELL_BWD_CHUNKS_PER_STEP if nc % CELL_BWD_CHUNKS_PER_STEP == 0 else 1

    def body(*refs):
        gng_ref, dc_sc, dn_sc = refs[20], refs[21], refs[22]

        @pl.when(pl.program_id(0) == 0)
        def _():
            dc_sc[...] = jnp.zeros_like(dc_sc)
            dn_sc[...] = jnp.zeros_like(dn_sc)
            gng_ref[...] = jnp.zeros_like(gng_ref)

        for cc in reversed(range(per)):
            rows = slice(cc * lc, (cc + 1) * lc)
            views = [refs[at] if at == 13 else _PartOf(refs[at], cols=rows) if at == 8 else
                     _PartOf(refs[at], lead=cc) if at in (10, 11, 12) else _PartOf(refs[at], rows=rows) for at in range(20)]
            chunk(*views, gng_ref, dc_sc, dn_sc)

    def chunk(dy_ref, o_ref, z_ref, cell_ref, q_ref, k_ref, v_ref, gc_ref, gr_ref, mt_ref, cs_ref, ns_ref, ms_ref,
              ng_ref, dq_ref, dk_ref, dv_ref, dg_ref, do_ref, dz_ref, gng_ref, dc_sc, dn_sc):
        gc = gc_ref[...]
        gr = gr_ref[...]
        mtv = mt_ref[...]
        lane = _iota((lc, LANES), 1)
        rowv = _iota((lc, 1), 0)
        dg_acc = jnp.zeros((lc, LANES), F32)
        for h in range(nh):
            c_h = cs_ref[0, h]
            n_h = ns_ref[0, h, 0:1, :]
            m_prev = jnp.max(ms_ref[0, h, 0:1, :], axis=1, keepdims=True)
            t = _cell_chunk(h, nh, q_ref, k_ref, v_ref, gc, gr, m_prev, c_h, n_h, m_t=_col(mtv, h))
            sl, qh, kh, vh = t["sl"], t["qh"], t["kh"], t["vh"]
            w_intra, w_inter, smat, w_state, decay = t["w_intra"], t["w_inter"], t["smat"], t["w_state"], t["decay"]
            cell_h = cell_ref[:, sl]
            dz, do, dcell, gng = _ml_out_stage_bwd(dy_ref[:, sl], cell_h, o_ref[:, sl], z_ref[:, sl], ng_ref[:, sl])
            dz_ref[:, sl] = dz.astype(BF16)
            do_ref[:, sl] = do.astype(BF16)
            gng_ref[:, sl] += _bcast8(gng)
            eneg = jnp.exp(-t["m_t"])
            aden = jnp.abs(t["den"])
            nst = jnp.maximum(aden, eneg)
            dnum = dcell / nst
            dden = jnp.where(aden > eneg, -_rowsum(cell_h * dcell) / nst * jnp.sign(t["den"]), 0.0)
            pmat = _mm_nt(dnum, vh) + dden
            damat = pmat * w_intra
            gmat = pmat * smat
            wdn = w_inter * dnum
            wdd = w_inter * dden
            dqh = _mm(damat, kh) + _mm_nt(wdn, c_h) + wdd * n_h
            dkh = _mm_tn(damat, qh)
            dvh = _mm_tn(smat, dnum)
            dw_inter = _rowsum(dnum * t["qc"]) + dden * t["qn"]
            dcn = dc_sc[h]
            dnn = dn_sc[h, 0:1, :]
            kw = kh * w_state
            dkw = _mm_nt(vh, dcn) + dnn
            dvh = dvh + _mm(kw, dcn)
            dkh = dkh + dkw * w_state
            dgst = _rowsum(dkw * kh) * w_state
            ddecay = _colsum(_rowsum(dcn * c_h)) + _rowsum(dnn * n_h)
            db_last = _colsum(dgst) + ddecay * decay
            rs_g = _rowsum(gmat)
            cs_g = _rowsum(gmat.T)
            db = rs_g - cs_g + dw_inter * w_inter - dgst + jnp.where(rowv == lc - 1, db_last, 0.0)
            dli = cs_g + dgst
            dc_sc[h] = decay * dcn + _mm_tn(qh, wdn)
            dn_sc[h] = _bcast8(decay * dnn + _colsum(qh * wdd))
            dq_ref[:, sl] = dqh
            dk_ref[:, sl] = dkh * (dh ** -0.5)
            dv_ref[:, sl] = dvh
            dg_acc = jnp.where(lane == h, dli, jnp.where(lane == 4 + h, db, dg_acc))
        dg_ref[...] = dg_acc

    rev = lambda c: nc // per - 1 - c
    tok = pl.BlockSpec((per * lc, d), lambda c: (rev(c), 0))
    g128 = pl.BlockSpec((per * lc, LANES), lambda c: (rev(c), 0))
    return _call(
        body, comms, name="mlstm_cell_bwd", grid=(nc // per,),
        in_specs=[tok, pl.BlockSpec((per * lc, d), lambda c: (rev(c), 3)), pl.BlockSpec((per * lc, d), lambda c: (rev(c), 4)),
                  tok, tok, tok, tok, g128, pl.BlockSpec((16, per * lc), lambda c: (0, rev(c))), g128,
                  pl.BlockSpec((per, nh, dh, dh), lambda c: (rev(c), 0, 0, 0)),
                  pl.BlockSpec((per, nh, SUBLANES, dh), lambda c: (rev(c), 0, 0, 0)),
                  pl.BlockSpec((per, nh, SUBLANES, LANES), lambda c: (rev(c), 0, 0, 0)), _full((1, d))],
        out_specs=[tok, tok, tok, g128, tok, tok, _full((SUBLANES, d))],
        out_shape=[jax.ShapeDtypeStruct((s, d), F32)] * 3 + [jax.ShapeDtypeStruct((s, LANES), F32)]
        + [jax.ShapeDtypeStruct((s, d), BF16)] * 2 + [jax.ShapeDtypeStruct((SUBLANES, d), F32)],
        scratch_shapes=[pltpu.VMEM((nh, dh, dh), F32), pltpu.VMEM((nh, SUBLANES, dh), F32)],
        args=(dy_ml, u, u, cell, q, k, v, gcol, grow, mt, cs, ns, ms, ng))


def _halo_spec(d, tile, nt, col):
    per = tile // SUBLANES
    return pl.BlockSpec((SUBLANES, d), lambda i: (jnp.maximum((nt - 1 - i) * per - 1, 0), col))


def _ml_pre_bwd(dq, dk, dv, dgates, gcol, u, pre, q, k, v, conv_w, w_q, w_k, w_v, wif_t, tile, comms=None):
    s, d = dq.shape
    nt = s // tile
    nh, dh, _ = w_q.shape

    def body(dq_ref, dk_ref, dv_ref, dg_ref, gc_ref, x_ref, pre_ref, q_ref, k_ref, v_ref, cw_ref,
             wq_ref, wk_ref, wv_ref, wift_ref,
             dx_ref, gwq_ref, gwk_ref, gwv_ref, gwif_ref, gbif_ref, gcw_ref, gcb_ref, next8):
        @pl.when(pl.program_id(0) == 0)
        def _():
            next8[...] = jnp.zeros_like(next8)
            for ref in (gwq_ref, gwk_ref, gwv_ref, gwif_ref, gbif_ref, gcw_ref, gcb_ref):
                ref[...] = jnp.zeros_like(ref)

        x = x_ref[...]
        pre = pre_ref[...]
        sg = _sigmoid(pre)
        xc = pre * sg
        dgc = dg_ref[...]
        lane = _iota(dgc.shape, 1)
        utri = jnp.where(_iota((ML_CHUNK, ML_CHUNK), 0) <= _iota((ML_CHUNK, ML_CHUNK), 1), 1.0, 0.0)
        rcs = [_mm_hi(utri, dgc[c * ML_CHUNK:(c + 1) * ML_CHUNK, :]) for c in range(tile // ML_CHUNK)]
        rc = rcs[0] if len(rcs) == 1 else jnp.concatenate(rcs, axis=0)
        dgates_v = jnp.where(lane < 4, dgc, jnp.where(lane < 8, rc * (1.0 - jnp.exp(gc_ref[...])), 0.0))
        dgb = dgates_v.astype(BF16)
        gbif_ref[...] += jnp.broadcast_to(_colsum(dgates_v), gbif_ref.shape)
        ext = jnp.dot(dgb, wift_ref[...], preferred_element_type=F32)
        dqt = dq_ref[...] + ext[:, 0:d]
        dkt = dk_ref[...] + ext[:, d:2 * d]
        dvt = dv_ref[...] + ext[:, 2 * d:3 * d]
        gwif_ref[:, 0:d] += _mm_tn(dgb, q_ref[...])
        gwif_ref[:, d:2 * d] += _mm_tn(dgb, k_ref[...])
        gwif_ref[:, 2 * d:3 * d] += _mm_tn(dgb, v_ref[...])
        dxc_parts, dxv_parts = [], []
        for h in range(nh):
            sl = slice(h * dh, (h + 1) * dh)
            gwq_ref[h] += _mm_tn(xc[:, sl], dqt[:, sl])
            gwk_ref[h] += _mm_tn(xc[:, sl], dkt[:, sl])
            gwv_ref[h] += _mm_tn(x[:, sl], dvt[:, sl])
            dxc_parts.append(_mm_nt(dqt[:, sl], wq_ref[h]) + _mm_nt(dkt[:, sl], wk_ref[h]))
            dxv_parts.append(_mm_nt(dvt[:, sl], wv_ref[h]))
        dxc = jnp.concatenate(dxc_parts, axis=1)
        dxv = jnp.concatenate(dxv_parts, axis=1)
        dpre = dxc * (sg + pre * sg * (1.0 - sg))
        gcb_ref[...] += _bcast8(_colsum(dpre))
        dx_ref[...] = (dxv + _conv_bwd(dpre, x, next8[...], cw_ref, gcw_ref)).astype(BF16)
        next8[...] = dpre[0:SUBLANES, :]

    rev = lambda i: nt - 1 - i
    tok = pl.BlockSpec((tile, d), lambda i: (rev(i), 0))
    g128 = pl.BlockSpec((tile, LANES), lambda i: (rev(i), 0))
    wsh = (nh, dh, dh)
    return _call(
        body, comms, name="mlstm_proj_bwd", grid=(nt,),
        in_specs=[tok, tok, tok, g128, g128, pl.BlockSpec((tile, d), lambda i: (rev(i), 2)), tok,
                  tok, tok, tok, _full(conv_w.shape), _full(wsh), _full(wsh), _full(wsh), _full(wif_t.shape)],
        out_specs=[tok, _full(wsh), _full(wsh), _full(wsh), _full((LANES, 3 * d)), _full((SUBLANES, LANES)),
                   _full((SUBLANES, d)), _full((SUBLANES, d))],
        out_shape=[jax.ShapeDtypeStruct((s, d), BF16)] + [jax.ShapeDtypeStruct(wsh, F32)] * 3
        + [jax.ShapeDtypeStruct((LANES, 3 * d), F32), jax.ShapeDtypeStruct((SUBLANES, LANES), F32),
           jax.ShapeDtypeStruct((SUBLANES, d), F32), jax.ShapeDtypeStruct((SUBLANES, d), F32)],
        scratch_shapes=[pltpu.VMEM((SUBLANES, d), F32)],
        args=(dq, dk, dv, dgates, gcol, u, pre, q, k, v, conv_w, w_q, w_k, w_v, wif_t))


def _rg_bwd(dy_rg, u, h_rg, gates, conv_w, w_a, w_x, lam, tile, comms=None):
    s, d = dy_rg.shape
    nt = s // tile
    nh, dh, _ = w_a.shape

    def body(dy_ref, x_ref, z_ref, h_ref, hhalo_ref, xc_ref, r_ref, i_ref, a_ref, beta_ref, cw_ref, wa_ref,
             wx_ref, lam_ref,
             dx_ref, dz_ref, gwa_ref, gwx_ref, gba_ref, gbx_ref, glam_ref, gcw_ref, gcb_ref, next8, anext, dnext, dbuf):
        i = pl.program_id(0)

        @pl.when(i == 0)
        def _():
            for ref in (next8, anext, dnext, gwa_ref, gwx_ref, gba_ref, gbx_ref, glam_ref, gcw_ref, gcb_ref):
                ref[...] = jnp.zeros_like(ref)

        inner = jnp.where(i < nt - 1, 1.0, 0.0)
        xc, r, ig, a, beta = xc_ref[...], r_ref[...], i_ref[...], a_ref[...], beta_ref[...]
        sp = _softplus(-lam_ref[...])
        h = h_ref[...]
        row = _iota(h.shape, 0)
        hprev = jnp.where(row >= 1, pltpu.roll(h, 1, 0), hhalo_ref[SUBLANES - 1:SUBLANES, :] * inner)
        z = z_ref[...]
        sz = _sigmoid(z)
        dyv = dy_ref[...]
        dz_ref[...] = (dyv * h * (sz + z * sz * (1.0 - sz))).astype(BF16)
        a_up = jnp.where(row < tile - 1, pltpu.roll(a, tile - 1, 0), anext[0:1, :])
        _scan_into(a_up, dyv * z * sz, dnext[0:1, :], dbuf, True)
        delta = dbuf[...]
        anext[...] = a[0:SUBLANES, :]
        dnext[...] = delta[0:SUBLANES, :]
        dla = delta * hprev * a - delta * ig * xc * (a * a / beta)
        glam_ref[...] += _bcast8(_colsum(dla * r) * (RG_C * _sigmoid(-lam_ref[...])))
        dpa = dla * (-RG_C * sp) * r * (1.0 - r)
        dpx = delta * beta * xc * ig * (1.0 - ig)
        gba_ref[...] += _bcast8(_colsum(dpa))
        gbx_ref[...] += _bcast8(_colsum(dpx))
        parts = []
        for hh in range(nh):
            sl = slice(hh * dh, (hh + 1) * dh)
            gwa_ref[hh] += _mm_tn(xc[:, sl], dpa[:, sl])
            gwx_ref[hh] += _mm_tn(xc[:, sl], dpx[:, sl])
            parts.append(_mm_nt(dpa[:, sl], wa_ref[hh]) + _mm_nt(dpx[:, sl], wx_ref[hh]))
        dxc = delta * beta * ig + jnp.concatenate(parts, axis=1)
        gcb_ref[...] += _bcast8(_colsum(dxc))
        dx_ref[...] = _conv_bwd(dxc, x_ref[...], next8[...], cw_ref, gcw_ref).astype(BF16)
        next8[...] = dxc[0:SUBLANES, :]

    rev = lambda i: nt - 1 - i
    tok = pl.BlockSpec((tile, d), lambda i: (rev(i), 0))
    vec = _full((1, d))
    acc = _full((SUBLANES, d))
    wsh = (nh, dh, dh)
    return _call(
        body, comms, name="rglru_bwd", grid=(nt,),
        in_specs=[tok, tok, pl.BlockSpec((tile, d), lambda i: (rev(i), 1)), tok,
                  _halo_spec(d, tile, nt, 0)] + [tok] * 5 + [_full(conv_w.shape), _full(wsh), _full(wsh), vec],
        out_specs=[tok, tok, _full(wsh), _full(wsh), acc, acc, acc, acc, acc],
        out_shape=[jax.ShapeDtypeStruct((s, d), BF16)] * 2 + [jax.ShapeDtypeStruct(wsh, F32)] * 2
        + [jax.ShapeDtypeStruct((SUBLANES, d), F32)] * 5,
        scratch_shapes=[pltpu.VMEM((SUBLANES, d), F32)] * 3 + [pltpu.VMEM((tile, d), F32)],
        args=(dy_rg, u, u, h_rg, h_rg, *gates, conv_w, w_a, w_x, lam))


def _segments(d, w, n_pieces, n_slots):
    bounds = sorted({k * d for k in range(n_pieces + 1)} | {j * w for j in range(n_slots + 1)})
    return [(lo // d, lo % d, lo // w, lo % w, hi - lo) for lo, hi in zip(bounds[:-1], bounds[1:])]


def _in_bwd(pieces, x, dxo, ng, scale, w_in_g, layer, tile, comms=None, tiles=None, prev=None):
    s, d = x.shape
    nd, _, _, w = w_in_g.shape
    first, count = tiles or (0, s // tile)
    n_p = len(pieces)

    def body(*refs):
        p_refs = refs[:n_p]
        x_ref, dxo_ref, ng_ref, sc_ref, w_ref = refs[n_p:n_p + 5]
        dx_ref, dsc_ref, dsh_ref, gng_ref, wcat = refs[-5:]
        _join_columns(w_ref, wcat)

        @pl.when(pl.program_id(0) == 0)
        def _():
            for k, ref in enumerate((dsc_ref, dsh_ref, gng_ref)):
                ref[...] = jnp.zeros_like(ref) if prev is None else refs[n_p + 6 + k][...]

        du = jnp.concatenate([p[...] for p in p_refs], axis=1)
        dh = lax.dot_general(du, wcat[...], (((1,), (1,)), ((), ())), preferred_element_type=F32)
        xv = x_ref[...]
        g = ng_ref[...]
        rs = lax.rsqrt(jnp.mean(xv * xv, axis=1, keepdims=True) + EPS)
        xh = xv * rs
        dsh_ref[...] += _bcast8(_colsum(dh))
        dsc_ref[...] += _bcast8(_colsum(dh * xh * g))
        dhn = dh * (1.0 + sc_ref[...])
        gng_ref[...] += _bcast8(_colsum(dhn * xh))
        dxh = dhn * g
        dx_ref[...] = dxo_ref[...] + rs * (dxh - xh * jnp.mean(dxh * xh, axis=1, keepdims=True))

    tok = pl.BlockSpec((tile, d), lambda i: (i + first, 0))
    vec = _full((1, d))
    acc = _full((SUBLANES, d))
    more_specs = [] if prev is None else [pl.BlockSpec(memory_space=pl.ANY), acc, acc, acc]
    return _call(
        body, comms, name="in_proj_bwd_x", grid=(count,),
        in_specs=[tok] * n_p + [tok, tok, vec, vec, pl.BlockSpec((nd, 1, d, w), lambda i: (0, layer, 0, 0),
                                                               pipeline_mode=pl.Buffered(1))] + more_specs,
        out_specs=[tok, acc, acc, acc],
        out_shape=[jax.ShapeDtypeStruct((s, d), F32)] + [jax.ShapeDtypeStruct((SUBLANES, d), F32)] * 3,
        scratch_shapes=[pltpu.VMEM((d, nd * w), BF16)],
        args=(*pieces, x, dxo, ng, scale, w_in_g) + (() if prev is None else tuple(prev)),
        aliases={} if prev is None else {n_p + 5: 0})


def _in_bwd_w(pieces, hbf, w, slots, tile, comms=None):
    s, d = hbf.shape
    nd_all = len(pieces) * d // w
    segs = [sg for sg in _segments(d, w, len(pieces), nd_all) if sg[2] in slots]

    def body(*refs):
        p_refs = refs[:len(pieces)]
        h_ref, gw_ref = refs[len(pieces):]

        @pl.when(pl.program_id(0) == 0)
        def _():
            gw_ref[...] = jnp.zeros_like(gw_ref)

        hv = h_ref[...]
        for (kk, a, j, b, width) in segs:
            gw_ref[j - slots[0], :, b:b + width] += _mm_tn(hv, p_refs[kk][:, a:a + width])

    tok = pl.BlockSpec((tile, d), lambda i: (i, 0))
    return _call(
        body, comms, name="in_proj_bwd_w", grid=(s // tile,),
        in_specs=[tok] * len(pieces) + [tok],
        out_specs=[pl.BlockSpec((len(slots), d, w), lambda i: (0, 0, 0), pipeline_mode=pl.Buffered(1))],
        out_shape=[jax.ShapeDtypeStruct((len(slots), d, w), F32)],
        args=(*pieces, hbf))[0]


def _exchange(arrs, gather, name):
    return _run_comms([_exchange_comm(arrs, gather)], name)[0]


def _run_comms(comms, name):
    _call(lambda: None, comms, name=name, grid=(1,), in_specs=[], out_specs=[], out_shape=[], args=())
    return [cm.results for cm in comms]


def _exchange_comm(arrs, gather):
    n = len(arrs)
    per = N_DEV - 1

    def copies(ins, outs, sems):
        send_sems, recv_sems, local_sems = sems
        x, y, c = (lax.axis_index(ax) for ax in MESH_AXES)
        me = 4 * x + 2 * y + c
        sends, recvs = [], []
        for flip in range(1, N_DEV):
            px = x ^ ((flip >> 2) & 1)
            py = y ^ ((flip >> 1) & 1)
            pc = c ^ (flip & 1)
            peer = 4 * px + 2 * py + pc
            for kk in range(n):
                src = ins[kk] if gather else ins[kk].at[peer]
                sends.append(_remote(src, outs[kk].at[me], send_sems, recv_sems, kk * per + flip - 1, (px, py, pc)))
                recvs.append(_remote(src, outs[kk].at[peer], send_sems, recv_sems, kk * per + flip - 1, (px, py, pc)))
        local = [pltpu.make_async_copy(ins[kk] if gather else ins[kk].at[me], outs[kk].at[me], local_sems.at[kk])
                 for kk in range(n)]
        return local, sends, recvs

    def start(ins, outs, sems):
        local, sends, _ = copies(ins, outs, sems)
        for cp in sends + local:
            cp.start()

    def finish(ins, outs, sems):
        local, sends, recvs = copies(ins, outs, sems)
        for cp in recvs:
            cp.wait_recv()
        for cp in sends:
            cp.wait_send()
        for cp in local:
            cp.wait()

    return _Comm(arrs, [jax.ShapeDtypeStruct((N_DEV,) + a.shape if gather else a.shape, a.dtype) for a in arrs],
                 [pltpu.SemaphoreType.DMA((n * per,)), pltpu.SemaphoreType.DMA((n * per,)), pltpu.SemaphoreType.DMA((n,))],
                 start, finish)


def _mesh_place():
    x, y, c = (lax.axis_index(ax) for ax in MESH_AXES)
    return x, y, c, (x, y, 1 - c), [(1 - x, y), (x, 1 - y), (1 - x, 1 - y)]


def _remote(src, dst, send_sems, recv_sems, sem, to):
    return pltpu.make_async_remote_copy(src_ref=src, dst_ref=dst, send_sem=send_sems.at[sem], recv_sem=recv_sems.at[sem],
                                        device_id=to, device_id_type=pl.DeviceIdType.MESH)


N_CHIPS = N_DEV // 2


def _pair_sum(a, other, parity, name):
    _, r, c = a.shape
    tr = _row_tile(r, c, 3)

    def body(p_ref, a_ref, o_ref, s_ref):
        s_ref[...] = (a_ref[...] + o_ref[...]).astype(BF16)

    return pl.pallas_call(
        body, name=name,
        grid_spec=pltpu.PrefetchScalarGridSpec(
            num_scalar_prefetch=1, grid=(N_CHIPS, r // tr),
            in_specs=[pl.BlockSpec((1, tr, c), lambda q, i, p: (2 * q + p[0], i, 0)),
                      pl.BlockSpec((1, tr, c), lambda q, i, p: (q, i, 0))],
            out_specs=pl.BlockSpec((1, tr, c), lambda q, i, p: (q, i, 0))),
        out_shape=jax.ShapeDtypeStruct((N_CHIPS, r, c), BF16),
        compiler_params=_params(2),
    )(parity, a, other)


def _adam_math(w, g, m, v):
    m = ADAM_B1 * m + (1.0 - ADAM_B1) * g
    v = ADAM_B2 * v + (1.0 - ADAM_B2) * (g * g)
    m_hat = m / (1.0 - ADAM_B1 ** ADAM_STEP)
    v_hat = v / (1.0 - ADAM_B2 ** ADAM_STEP)
    delta = -ADAM_LR * (m_hat / (jnp.sqrt(v_hat) + ADAM_EPS) + ADAM_WD * w)
    return delta, m, v


def _sum_devices(r_ref):
    acc = r_ref[0].astype(F32)
    for p in range(1, r_ref.shape[0]):
        acc = acc + r_ref[p].astype(F32)
    return acc


def _row_tile(rows, cols, n_bufs):
    budget = 24 * 1024 * 1024 // (n_bufs * 2 * cols * 4)
    t = rows
    while t > budget and t % 2 == 0 and (t // 2) % SUBLANES == 0:
        t //= 2
    return t


def _reduce_adam(recvs, w, m, v, name, comms=None):
    nl, r, c = w.shape
    n_part = recvs[0].shape[0]
    tr = _row_tile(r, c, n_part * nl + 7)
    nt = r // tr

    def body(*refs):
        r_refs = refs[:nl]
        w_ref, m_ref, v_ref, g_ref, d_ref, mo_ref, vo_ref = refs[nl:]
        layer = pl.program_id(0) // nt
        g = _sum_devices(r_refs[0])
        for ll in range(1, nl):
            g = jnp.where(layer == ll, _sum_devices(r_refs[ll]), g)
        delta, m2, v2 = _adam_math(w_ref[0], g, m_ref[0], v_ref[0])
        g_ref[0] = g
        d_ref[0] = delta
        mo_ref[0] = m2
        vo_ref[0] = v2

    def rspec(ll):
        return pl.BlockSpec((n_part, tr, c),
                            lambda i: (0, jnp.where(i // nt == ll, i % nt, jnp.where(i // nt < ll, 0, nt - 1)), 0))

    blk = pl.BlockSpec((1, tr, c), lambda i: (i // nt, i % nt, 0))
    return _call(
        body, comms, name=name, grid=(nl * nt,),
        in_specs=[rspec(ll) for ll in range(nl)] + [blk, blk, blk],
        out_specs=[blk] * 4,
        out_shape=[jax.ShapeDtypeStruct((nl, r, c), F32)] * 4,
        args=(*recvs, w, m, v))


def _tile_for(s, want):
    return min(want, s)


REPLICATED = ("norm_g", "b_ada", "rg_conv_b", "rg_w_a", "rg_b_a", "rg_w_x", "rg_b_x", "rg_lambda", "ml_conv_b",
              "ml_b_if", "ml_norm_g", "final_g")


def _small_pack(rg_conv_w, ml_conv_w, ml_w_if):
    nl = rg_conv_w.shape[0]
    wif_t = jnp.swapaxes(ml_w_if, 1, 2).reshape(nl, -1, LANES)
    return jnp.concatenate([rg_conv_w, ml_conv_w, wif_t], axis=1)


def _small_unpack(p, if_rows):
    nl = p.shape[0]
    rg_cw = p[:, 0:CONV_WIDTH]
    ml_cw = p[:, CONV_WIDTH:2 * CONV_WIDTH]
    wif = jnp.swapaxes(p[:, 2 * CONV_WIDTH:].reshape(nl, 8, if_rows), 1, 2)
    return rg_cw, ml_cw, wif


def _qkv_slots(g_qkv, nd):
    three, nh, dh, _ = g_qkv.shape
    return g_qkv.reshape(three, nh, nd, dh // nd, dh).transpose(2, 0, 1, 3, 4).reshape(nd, three * nh * (dh // nd), dh)


def _small_slots(g):
    nd = N_DEV
    cw = jnp.stack([g["rg_conv_w"], g["ml_conv_w"]]).reshape(2, CONV_WIDTH, nd, LANES).transpose(2, 0, 1, 3)
    cw = cw.reshape(nd, 2 * CONV_WIDTH, LANES)
    wif = g["wif_t"].reshape(8, nd, -1).transpose(1, 0, 2).reshape(nd, -1, LANES)
    return jnp.concatenate([cw, wif], axis=1)


def _slot(block):
    return 4 * block[0] + 2 * block[1] + block[2]


def _dma_sems(*counts):
    return [pltpu.SemaphoreType.DMA((n,)) for n in counts]


def _start_all(copies):
    for cp in copies:
        cp.start()


def _gather_ici_comm(arrs):
    n = len(arrs)

    def copies(ins, outs, sems):
        send_sems, recv_sems, local_sems = sems
        x, y, c, sibling, chips = _mesh_place()
        me = (x, y, c)
        peers = [(*chip, c) for chip in chips] + [sibling]
        local = [pltpu.make_async_copy(ins[kk], outs[kk].at[_slot(me)], local_sems.at[kk]) for kk in range(n)]
        sends = [_remote(ins[kk], outs[kk].at[_slot(me)], send_sems, recv_sems, kk * 4 + j, peer)
                 for j, peer in enumerate(peers) for kk in range(n)]
        recvs = [_remote(ins[kk], outs[kk].at[_slot(peer)], send_sems, recv_sems, kk * 4 + j, peer)
                 for j, peer in enumerate(peers) for kk in range(n)]
        return local, sends, recvs

    def start(ins, outs, sems):
        local, sends, _ = copies(ins, outs, sems)
        _start_all(sends + local)

    def finish(ins, outs, sems):
        local, sends, recvs = copies(ins, outs, sems)
        for cp in recvs:
            cp.wait_recv()
        for cp in sends:
            cp.wait_send()
        for cp in local:
            cp.wait()

    return _Comm(arrs, [jax.ShapeDtypeStruct((N_DEV,) + a.shape, a.dtype) for a in arrs], _dma_sems(4 * n, 4 * n, n),
                 start, finish)


def _gather_fwd_comm(bufs):
    n = len(bufs)

    def copies(ins, outs, sems):
        send_sems, recv_sems = sems
        _, _, c, sibling, chips = _mesh_place()
        sends = [_remote(ins[kk].at[_slot((*chip, c))], outs[kk].at[_slot((*chip, c))], send_sems, recv_sems, kk * 3 + j, sibling)
                 for j, chip in enumerate(chips) for kk in range(n)]
        recvs = [_remote(ins[kk].at[_slot((*chip, c))], outs[kk].at[_slot((*chip, 1 - c))], send_sems, recv_sems, kk * 3 + j, sibling)
                 for j, chip in enumerate(chips) for kk in range(n)]
        return sends, recvs

    def start(ins, outs, sems):
        _start_all(copies(ins, outs, sems)[0])

    def finish(ins, outs, sems):
        sends, recvs = copies(ins, outs, sems)
        for cp in recvs:
            cp.wait_recv()
        for cp in sends:
            cp.wait_send()

    return _Comm(bufs, [jax.ShapeDtypeStruct(a.shape, a.dtype) for a in bufs], _dma_sems(3 * n, 3 * n), start, finish,
                 aliases=[(i, i) for i in range(n)])


def _core_swap_comm(arrs):
    n = len(arrs)

    def copies(ins, outs, sems):
        send_sems, recv_sems = sems
        _, _, c, sibling, _ = _mesh_place()
        return [_remote(ins[kk].at[2 * q + (1 - c)], outs[kk].at[q], send_sems, recv_sems, kk * N_CHIPS + q, sibling)
                for q in range(N_CHIPS) for kk in range(n)]

    def start(ins, outs, sems):
        _start_all(copies(ins, outs, sems))

    def finish(ins, outs, sems):
        cps = copies(ins, outs, sems)
        for cp in cps:
            cp.wait_recv()
        for cp in cps:
            cp.wait_send()

    return _Comm(arrs, [jax.ShapeDtypeStruct((N_CHIPS,) + a.shape[1:], a.dtype) for a in arrs],
                 _dma_sems(N_CHIPS * n, N_CHIPS * n), start, finish)


def _chip_swap_comm(arrs):
    n = len(arrs)
    per = N_CHIPS - 1

    def copies(ins, outs, sems):
        send_sems, recv_sems, local_sems = sems
        x, y, c, _, chips = _mesh_place()
        mine = 2 * x + y
        sends = [_remote(ins[kk].at[2 * chip[0] + chip[1]], outs[kk].at[mine], send_sems, recv_sems, kk * per + j, (*chip, c))
                 for j, chip in enumerate(chips) for kk in range(n)]
        recvs = [_remote(ins[kk].at[mine], outs[kk].at[2 * chip[0] + chip[1]], send_sems, recv_sems, kk * per + j, (*chip, c))
                 for j, chip in enumerate(chips) for kk in range(n)]
        local = [pltpu.make_async_copy(ins[kk].at[mine], outs[kk].at[mine], local_sems.at[kk]) for kk in range(n)]
        return local, sends, recvs

    def start(ins, outs, sems):
        local, sends, _ = copies(ins, outs, sems)
        _start_all(sends + local)

    def finish(ins, outs, sems):
        local, sends, recvs = copies(ins, outs, sems)
        for cp in recvs:
            cp.wait_recv()
        for cp in sends:
            cp.wait_send()
        for cp in local:
            cp.wait()

    return _Comm(arrs, [jax.ShapeDtypeStruct(a.shape, a.dtype) for a in arrs], _dma_sems(per * n, per * n, n), start, finish)


def _ada_mod(c_all, w_ada, b_cols, comms=None):
    nl, d, w = w_ada.shape

    def body(c_ref, w_ref, b_ref, m_ref, ca_ref):
        sub = _iota((SUBLANES, d), 0)
        cv = jnp.zeros((SUBLANES, d), F32)
        for b in range(N_DEV):
            cv = jnp.where(sub == b, c_ref[b], cv)
        ca = cv * _sigmoid(cv)
        ca_ref[...] = ca
        m_ref[...] = jnp.zeros_like(m_ref)
        for l in range(nl):
            ml = _mm_hi(ca, w_ref[l]) + b_ref[l:l + 1, :]
            for b in range(N_DEV):
                m_ref[b, l:l + 1, :] = _row(ml, b)

    return _call(
        body, comms, name="adaln_mod_columns", grid=(1,),
        in_specs=[_full(c_all.shape), _full(w_ada.shape), _full(b_cols.shape)],
        out_specs=[_full((N_DEV, SUBLANES, w)), _full((SUBLANES, d))],
        out_shape=[jax.ShapeDtypeStruct((N_DEV, SUBLANES, w), F32), jax.ShapeDtypeStruct((SUBLANES, d), F32)],
        args=(c_all, w_ada, b_cols))


def _ada_grad_adam(cact_t, dmods, w, m, v, comms=None):
    nl, d, wd = w.shape
    tr = _row_tile(d, wd, 8)
    nt = d // tr

    def body(c_ref, dm_ref, w_ref, m_ref, v_ref, g_ref, d_ref, mo_ref, vo_ref):
        cv = c_ref[...]
        dm = dm_ref[0]
        g = _col(cv, 0) * _row(dm, 0)
        for b in range(1, N_DEV):
            g = g + _col(cv, b) * _row(dm, b)
        delta, m2, v2 = _adam_math(w_ref[0], g, m_ref[0], v_ref[0])
        g_ref[0] = g
        d_ref[0] = delta
        mo_ref[0] = m2
        vo_ref[0] = v2

    blk = pl.BlockSpec((1, tr, wd), lambda i: (i // nt, i % nt, 0))
    return _call(
        body, comms, name="adaln_grad_adam", grid=(nl * nt,),
        in_specs=[pl.BlockSpec((tr, N_DEV), lambda i: (i % nt, 0)), pl.BlockSpec((1, N_DEV, wd), lambda i: (i // nt, 0, 0)),
                  blk, blk, blk],
        out_specs=[blk] * 4, out_shape=[jax.ShapeDtypeStruct((nl, d, wd), F32)] * 4,
        args=(cact_t, dmods, w, m, v))


REP_ROWS = ("norm_g", "dshift", "dscale", "dgate", "rg_conv_b", "rg_b_a", "rg_b_x", "rg_lambda", "ml_conv_b", "ml_norm_g",
            "ml_b_if")


def _pack_rows(arrays, d, mod_rows, wcols):
    n = len(arrays)
    rows = n + (-n) % SUBLANES
    assert d % LANES == 0 and wcols % LANES == 0 and all(arrays[i].shape[1] == d for srcs in mod_rows for i in srcs)

    def body(*refs):
        o_ref, b_ref = refs[n], refs[n + 1]
        o_ref[...] = jnp.zeros(o_ref.shape, F32)
        b_ref[...] = jnp.zeros(b_ref.shape, F32)
        for r, a_ref in enumerate(refs[:n]):
            o_ref[r:r + 1, 0:a_ref.shape[1]] = a_ref[0:1, :]
        for l, srcs in enumerate(mod_rows):
            for k in range(N_DEV):
                for c in range(0, wcols, LANES):
                    at = k * wcols + c
                    b_ref[k, l:l + 1, c:c + LANES] = refs[srcs[at // d]][0:1, at % d:at % d + LANES]

    blocks = (N_DEV, SUBLANES, wcols)
    return pl.pallas_call(
        body, name="pack_vectors", grid=(1,), in_specs=[_full(a.shape) for a in arrays],
        out_specs=[_full((rows, d)), _full(blocks)],
        out_shape=[jax.ShapeDtypeStruct((rows, d), F32), jax.ShapeDtypeStruct(blocks, F32)], compiler_params=_params(1),
    )(*arrays)


def _sum_parts(recvs, name):
    def body(*refs):
        for r_ref, o_ref in zip(refs[:len(recvs)], refs[len(recvs):]):
            o_ref[...] = _sum_devices(r_ref).astype(o_ref.dtype)

    return pl.pallas_call(
        body, name=name, grid=(1,),
        in_specs=[_full(r.shape) for r in recvs], out_specs=[_full(r.shape[1:]) for r in recvs],
        out_shape=[jax.ShapeDtypeStruct(r.shape[1:], r.dtype) for r in recvs], compiler_params=_params(1),
    )(*recvs)


def _adam_replicated(vp, mp, params, nl):
    d = vp.shape[2]
    nr = len(REP_ROWS)
    names = list(params)
    mat_shape = params["rg_w_a"][0].shape[1:]
    mat_rows = mp.shape[0] // (2 * nl)

    def pieces(name):
        if name == "final_g":
            return [(lambda vp_ref, mp_ref: vp_ref[nl * nr:nl * nr + 1, :], (slice(0, 1), slice(None)))]
        out = []
        for l in range(nl):
            if name in ("rg_w_a", "rg_w_x"):
                at = (2 * l + (name == "rg_w_x")) * mat_rows
                out.append((lambda vp_ref, mp_ref, at=at: mp_ref[at:at + mat_rows, :].astype(F32).reshape(mat_shape), l))
            elif name == "b_ada":
                for j in range(3):
                    r = l * nr + 1 + j
                    out.append((lambda vp_ref, mp_ref, r=r: vp_ref[r:r + 1, :], (slice(l, l + 1), slice(j * d, (j + 1) * d))))
            else:
                r = l * nr + REP_ROWS.index(name)
                cols = slice(0, LANES) if name == "ml_b_if" else slice(None)
                out.append((lambda vp_ref, mp_ref, r=r, cols=cols: vp_ref[r:r + 1, cols], (slice(l, l + 1), slice(None))))
        return out

    def body(*refs):
        parts_ref, mp_ref, vp_ref = refs[0], refs[1], refs[-1]
        ins, outs = refs[2:2 + 3 * len(names)], refs[2 + 3 * len(names):-1]
        vp_ref[...] = _sum_devices(parts_ref)
        for pi, name in enumerate(names):
            w_ref, m_ref, v_ref = ins[3 * pi:3 * pi + 3]
            g_ref, d_ref, mo_ref, vo_ref = outs[4 * pi:4 * pi + 4]
            for get, idx in pieces(name):
                g = get(vp_ref, mp_ref)
                delta, m2, v2 = _adam_math(w_ref[idx], g, m_ref[idx], v_ref[idx])
                g_ref[idx] = g
                d_ref[idx] = delta
                mo_ref[idx] = m2
                vo_ref[idx] = v2

    flat = [a for name in names for a in params[name]]
    out_shape = [jax.ShapeDtypeStruct(params[name][0].shape, F32) for name in names for _ in range(4)]
    out_shape.append(jax.ShapeDtypeStruct(vp.shape[1:], F32))
    res = pl.pallas_call(
        body, name="adam_replicated", grid=(1,),
        in_specs=[_full(vp.shape), _full(mp.shape)] + [_full(a.shape) for a in flat],
        out_specs=[_full(o.shape) for o in out_shape], out_shape=out_shape, compiler_params=_params(1),
    )(vp, mp, *flat)
    return {name: res[4 * pi:4 * pi + 4] for pi, name in enumerate(names)}, res[-1]


class _Plan:
    def __init__(self):
        self.hosted, self.after = {}, {}

    def host(self, key, comm, then=None):
        self.hosted.setdefault(key, []).append(comm)
        if then is not None:
            self.after.setdefault(key, []).append(then)

    def comms(self, key):
        return self.hosted.pop(key, None)

    def done(self, key):
        for fn in self.after.pop(key, []):
            fn()

    def flush(self):
        while self.hosted:
            key = next(iter(self.hosted))
            _call(lambda: None, self.comms(key), name="exchange_after_%s_%d" % key, grid=(1,), in_specs=[], out_specs=[],
                  out_shape=[], args=())
            self.done(key)


VEC_TABLE = ("norm_g", "rg_conv_b", "rg_b_a", "rg_b_x", "rg_lambda", "ml_conv_b", "ml_norm_g")


def _vec_table(rep):
    rows = [rep[n] for n in VEC_TABLE]
    return jnp.stack(rows + [jnp.zeros_like(rows[0])] * (SUBLANES - len(rows)), axis=1)


def _layer_fwd(l, xl, mod3, wl, rep, plan, head=None):
    s, d = xl.shape
    t_big, t_mid = _tile_for(s, 512), _tile_for(s, 256)
    nh_ml = rep["ml_b_if"].shape[1] // 2
    vec = lambda name: _vec(rep["vecs"], l, VEC_TABLE.index(name))
    shift, scale, gate = (_vec(mod3, l, kk) for kk in range(3))
    hosted = lambda name: plan.comms((name, l)) if plan else None
    done = lambda name: plan.done((name, l)) if plan else None
    u, hbf = _in_fwd(xl, vec("norm_g"), scale, shift, wl["w_in_g"], 0, t_big, hosted("in_proj_fwd"))
    done("in_proj_fwd")
    h_rg, y_rg, *rg_gates = _rg_fwd(u, d, wl["rg_conv_w"], vec("rg_conv_b"), rep["rg_w_a_bf"][l], vec("rg_b_a"),
                                    rep["rg_w_x_bf"][l], vec("rg_b_x"), vec("rg_lambda"), t_mid, hosted("rglru_fwd"))
    done("rglru_fwd")
    q, k, v, gcol, pre = _ml_pre(u, d, wl["ml_conv_w"], vec("ml_conv_b"), wl["w_qkv"][0], wl["w_qkv"][1],
                                 wl["w_qkv"][2], wl["wif_pad"], wl["bif_pad"], t_mid, hosted("mlstm_proj_fwd"))
    done("mlstm_proj_fwd")
    grow = gcol[:, 0:16].T
    cell, y_ml, cs, ns, ms, mt = _ml_cell_fwd(q, k, v, gcol, grow, u, vec("ml_norm_g"), nh_ml, hosted("mlstm_cell_fwd"))
    done("mlstm_cell_fwd")
    res = _out_fwd(xl, y_rg, y_ml, gate, wl["w_out_g"], 0, t_big, hosted("out_proj_fwd"), head)
    done("out_proj_fwd")
    x_new, y = (res[0], res[1]) if head is None else (tuple(res[1:]), res[0])
    saved = dict(x=xl, u=u, hbf=hbf, h_rg=h_rg, y_rg=y_rg, q=q, k=k, v=v, gcol=gcol, grow=grow, cell=cell, y_ml=y_ml,
                 cs=cs, ns=ns, ms=ms, mt=mt, y=y, scale=scale, gate=gate, rg_gates=rg_gates, pre=pre)
    return x_new, saved


def _layer_bwd(l, dx, sv, wl, rep, plan, grads=None, split_last=False):
    s, d = dx.shape
    t_big, t_mid = _tile_for(s, 512), _tile_for(s, 256)
    nh_ml = rep["ml_b_if"].shape[1] // 2
    nd, _, _, w_cols = wl["w_in_g"].shape
    grads = {} if grads is None else grads
    vec = lambda name: _vec(rep["vecs"], l, VEC_TABLE.index(name))
    hosted = lambda name: plan.comms((name, l)) if plan else None
    done = lambda name: plan.done((name, l)) if plan else None
    dy_rg, dy_ml, gw_out, dgate = _out_bwd(dx, sv["gate"], sv["y"], sv["y_rg"], sv["y_ml"], wl["w_out_g"], 0, t_big,
                                           hosted("out_proj_bwd"))
    grads.update(w_out=gw_out)
    done("out_proj_bwd")
    dq, dk, dv, dgates, d_mlo, d_mlz, g_mlng = _ml_cell_bwd(
        dy_ml, sv["u"], sv["cell"], sv["q"], sv["k"], sv["v"], sv["gcol"], sv["grow"], sv["mt"], sv["cs"], sv["ns"],
        sv["ms"], vec("ml_norm_g"), nh_ml, hosted("mlstm_cell_bwd"))
    done("mlstm_cell_bwd")
    d_mlx, g_wq, g_wk, g_wv, g_wift, g_bif, g_mlcw, g_mlcb = _ml_pre_bwd(
        dq, dk, dv, dgates, sv["gcol"], sv["u"], sv["pre"], sv["q"], sv["k"], sv["v"], wl["ml_conv_w"],
        wl["w_qkv"][0], wl["w_qkv"][1], wl["w_qkv"][2], wl["wift_pad"], t_mid, hosted("mlstm_proj_bwd"))
    done("mlstm_proj_bwd")
    d_rgx, d_rgz, g_wa, g_wx, g_ba, g_bx, g_lam, g_rgcw, g_rgcb = _rg_bwd(
        dy_rg, sv["u"], sv["h_rg"], sv["rg_gates"], wl["rg_conv_w"], rep["rg_w_a_bf"][l], rep["rg_w_x_bf"][l],
        vec("rg_lambda"), t_mid, hosted("rglru_bwd"))
    grads.update(w_qkv=jnp.stack([g_wq, g_wk, g_wv]), rg_conv_w=g_rgcw[0:CONV_WIDTH], ml_conv_w=g_mlcw[0:CONV_WIDTH],
                 wif_t=g_wift[0:8], rg_w_a=g_wa, rg_w_x=g_wx)
    acc = dict(dgate=dgate, rg_conv_b=g_rgcb, rg_b_a=g_ba, rg_b_x=g_bx, rg_lambda=g_lam, ml_conv_b=g_mlcb,
               ml_b_if=g_bif, ml_norm_g=g_mlng)
    done("rglru_bwd")
    pieces = [d_rgx, d_rgz, d_mlx, d_mlo, d_mlz]
    grads.update(w_in=_in_bwd_w(pieces, sv["hbf"], w_cols, tuple(range(nd)), _tile_for(s, 1024), hosted("in_proj_bwd_w")))
    done("in_proj_bwd_w")
    n_tiles = s // t_mid
    counts = [n_tiles // 8, n_tiles - n_tiles // 8] if split_last and n_tiles >= 8 else [n_tiles]
    in_args = (pieces, sv["x"], dx, vec("norm_g"), sv["scale"], wl["w_in_g"], 0, t_mid)
    res, at = None, 0
    for key, count in zip(("in_proj_bwd_x", "in_proj_bwd_x_rest"), counts):
        res = _in_bwd(*in_args, hosted(key), (at, count), res)
        done(key)
        at += count
    dx, dscale, dshift, g_ng = res
    acc.update(norm_g=g_ng, dshift=dshift, dscale=dscale)
    grads.update(acc=acc, dmod=jnp.concatenate([dshift[0:1], dscale[0:1], dgate[0:1]], axis=1))
    return dx, grads


def _full_qkv(qkv_g, d):
    nd, _, rows3, dh = qkv_g.shape
    nh = d // dh
    rsh = rows3 // (3 * nh)
    return qkv_g.reshape(nd, 3, nh, rsh, dh).transpose(1, 2, 0, 3, 4).reshape(3, nh, nd * rsh, dh)


def _small_weights(small, l, ml_b_if):
    nd = small.shape[0]
    sm = small[:, l]
    cw = sm[:, 0:2 * CONV_WIDTH].reshape(nd, 2, CONV_WIDTH, LANES).transpose(1, 2, 0, 3).reshape(2, CONV_WIDTH, nd * LANES)
    if_rows = (sm.shape[1] - 2 * CONV_WIDTH) * LANES // 8
    wif_t = sm[:, 2 * CONV_WIDTH:].reshape(nd, 8, if_rows).transpose(1, 0, 2).reshape(8, nd * if_rows)
    wift_pad = jnp.pad(wif_t, ((0, LANES - 8), (0, 0))).astype(BF16)
    return dict(rg_conv_w=cw[0], ml_conv_w=cw[1], wift_pad=wift_pad, wif_pad=wift_pad.T,
                bif_pad=jnp.pad(ml_b_if[l], (0, LANES - 8)).reshape(1, LANES))


def kernel(x, c, norm_g, w_ada, b_ada, w_in, rg_conv_w, rg_conv_b, rg_w_a, rg_b_a, rg_w_x, rg_b_x, rg_lambda, ml_conv_w, ml_conv_b, ml_w_q, ml_w_k, ml_w_v, ml_w_if, ml_b_if, ml_norm_g, w_out, final_g, loss_target, m_norm_g, m_w_ada, m_b_ada, m_w_in, m_rg_conv_w, m_rg_conv_b, m_rg_w_a, m_rg_b_a, m_rg_w_x, m_rg_b_x, m_rg_lambda, m_ml_conv_w, m_ml_conv_b, m_ml_w_q, m_ml_w_k, m_ml_w_v, m_ml_w_if, m_ml_b_if, m_ml_norm_g, m_w_out, m_final_g, v_norm_g, v_w_ada, v_b_ada, v_w_in, v_rg_conv_w, v_rg_conv_b, v_rg_w_a, v_rg_b_a, v_rg_w_x, v_rg_b_x, v_rg_lambda, v_ml_conv_w, v_ml_conv_b, v_ml_w_q, v_ml_w_k, v_ml_w_v, v_ml_w_if, v_ml_b_if, v_ml_norm_g, v_w_out, v_final_g):
    given = dict(locals())
    nl = w_in.shape[0]
    d = x.shape[2]
    rep = {n: given[n] for n in REPLICATED}
    rep.update(rg_w_a_bf=rg_w_a.astype(BF16), rg_w_x_bf=rg_w_x.astype(BF16))
    bf = lambda a: a.astype(BF16)

    def qkv_shard(prefix):
        return jnp.stack([given[prefix + "ml_w_q"], given[prefix + "ml_w_k"], given[prefix + "ml_w_v"]], axis=1).reshape(
            nl, -1, ml_w_q.shape[-1])

    def small_shard(prefix):
        return _small_pack(given[prefix + "rg_conv_w"], given[prefix + "ml_conv_w"], given[prefix + "ml_w_if"])

    plan = _Plan()
    qkv = qkv_shard("")
    first_ici = _gather_ici_comm([bf(w_in[0:1]), small_shard("")])
    condition = _exchange_comm([jnp.broadcast_to(c, (SUBLANES, d))], True)
    _run_comms([first_ici, condition], "gather_first")
    first_fwd = _gather_fwd_comm(first_ici.results)
    wcols = w_ada.shape[2]
    me = 4 * lax.axis_index("x") + 2 * lax.axis_index("y") + lax.axis_index("c")
    b_cols = jnp.pad(lax.dynamic_slice_in_dim(b_ada, me * wcols, wcols, axis=1), ((0, SUBLANES - nl), (0, 0)))
    mod_cols, cact_all = _ada_mod(condition.results[0], w_ada, b_cols, [first_fwd])
    w_in_first, small = first_fwd.results
    wl = [_small_weights(small, l, ml_b_if) for l in range(nl)]
    wl[0]["w_in_g"] = w_in_first

    def gather_behind(arrs, ici_host, fwd_host, then):
        ici = _gather_ici_comm(arrs)

        def pass_on():
            fwd = _gather_fwd_comm(ici.results)
            plan.host(fwd_host, fwd, lambda: then(fwd.results))

        plan.host(ici_host, ici, pass_on)

    def got_out(l):
        return lambda r: wl[l].update(w_out_g=r[0], w_qkv=_full_qkv(r[1], d))

    gather_behind([bf(w_out[0:1]), bf(qkv[0:1])], ("in_proj_fwd", 0), ("rglru_fwd", 0), got_out(0))
    for l in range(1, nl):
        gather_behind([bf(w_in[l:l + 1])], ("rglru_fwd", l - 1), ("mlstm_cell_fwd", l - 1),
                      lambda r, l=l: wl[l].update(w_in_g=r[0]))
        gather_behind([bf(w_out[l:l + 1]), bf(qkv[l:l + 1])], ("mlstm_cell_fwd", l - 1), ("out_proj_fwd", l - 1), got_out(l))

    mod_blocks = _exchange([mod_cols], False, "scatter_modulation")[0]
    mod3 = mod_blocks[:, 0:nl].transpose(1, 0, 2).reshape(nl, 3, d)
    rep["vecs"] = _vec_table(rep)
    saved, xl = [], x[0]
    for l in range(nl):
        head = (final_g.reshape(1, -1), loss_target[0]) if l == nl - 1 else None
        xl, sv = _layer_fwd(l, xl, mod3, wl[l], rep, plan, head)
        saved.append(sv)
    grad_x, loss_p, g_final = xl

    keys = ("w_in", "w_out", "w_qkv", "small")
    parity = lax.axis_index("c").astype(jnp.int32).reshape(1)
    grads, recv = [None] * nl, [None] * nl

    def small_parts(g):
        return [bf(_qkv_slots(g["w_qkv"], N_DEV)), bf(_small_slots(g))]

    def reduce_behind(l, host_layer):
        parts = [grads[l]["w_in"], grads[l]["w_out"]]
        swap = _core_swap_comm(parts)
        direct = _exchange_comm(small_parts(grads[l]), False)

        def summed():
            sums = [_pair_sum(a, o, parity, "pair_sum_%s_layer%d" % (key, l)) for key, a, o in zip(keys, parts, swap.results)]
            big = _chip_swap_comm([sums[0]])
            rest = _chip_swap_comm([sums[1]])
            plan.host(("mlstm_cell_bwd", host_layer), big)
            plan.host(("rglru_bwd", host_layer), rest,
                      lambda: recv.__setitem__(l, big.results + rest.results + direct.results))

        plan.host(("out_proj_bwd", host_layer), swap, summed)
        plan.host(("mlstm_proj_bwd", host_layer), direct)

    first, own = {}, {}

    def reduce_own(names, parts_fn, ready_key, swap_key, chip_key):
        def go():
            parts = parts_fn()
            swap = _core_swap_comm(parts)

            def summed():
                sums = [_pair_sum(a, o, parity, "pair_sum_%s_layer0" % n) for n, a, o in zip(names, parts, swap.results)]
                chip = _chip_swap_comm(sums)
                plan.host(chip_key, chip, lambda: own.update(zip(names, chip.results)))

            plan.host(swap_key, swap, summed)

        plan.after.setdefault(ready_key, []).append(go)

    reduce_own(["w_out"], lambda: [first["w_out"]], ("out_proj_bwd", 0), ("mlstm_cell_bwd", 0), ("mlstm_proj_bwd", 0))
    reduce_own(["w_in"], lambda: [first["w_in"]], ("in_proj_bwd_w", 0), ("in_proj_bwd_x", 0), ("in_proj_bwd_x_rest", 0))

    def small_own():
        direct = _exchange_comm(small_parts(first), False)
        plan.host(("in_proj_bwd_w", 0), direct, lambda: own.update(w_qkv=direct.results[0], small=direct.results[1]))

    plan.after.setdefault(("rglru_bwd", 0), []).append(small_own)

    matrices = {}

    def reduce_matrices():
        layers = [grads[l] if l > 0 else first for l in range(nl)]
        mp = jnp.stack([jnp.stack([g["rg_w_a"], g["rg_w_x"]]) for g in layers]).reshape(N_DEV, -1, LANES).astype(BF16)
        scatter = _exchange_comm([mp], False)

        def summed():
            gather = _exchange_comm(_sum_parts(scatter.results, "sum_replicated_matrices"), True)
            plan.host(("in_proj_bwd_x", 0), gather, lambda: matrices.update(mp=gather.results[0].reshape(-1, LANES)))

        plan.host(("in_proj_bwd_w", 0), scatter, summed)

    plan.after.setdefault(("rglru_bwd", 0), []).append(reduce_matrices)

    for l in reversed(range(nl)):
        if l > 0:
            grad_x, grads[l] = _layer_bwd(l, grad_x, saved[l], wl[l], rep, plan)
            reduce_behind(l, l - 1)
        else:
            grad_x, grads[l] = _layer_bwd(l, grad_x, saved[l], wl[l], rep, plan, first, True)
    plan.flush()
    recv[0] = [own[key] for key in keys]

    shard = {p: dict(w_in=given[p + "w_in"], w_out=given[p + "w_out"], w_qkv=qkv_shard(p), small=small_shard(p))
             for p in ("", "m_", "v_")}
    res = {}
    for ki, key in enumerate(keys):
        res[key] = _reduce_adam([recv[l][ki] for l in range(nl)], shard[""][key], shard["m_"][key], shard["v_"][key],
                                "reduce_adam_" + key)

    mod_rows = [[l * len(REP_ROWS) + REP_ROWS.index(n) for n in ("dshift", "dscale", "dgate")] for l in range(nl)]
    vp, dmod_blocks = _pack_rows([grads[l]["acc"][n] for l in range(nl) for n in REP_ROWS] + [g_final, loss_p], d,
                                 mod_rows, wcols)
    (dmod_recv,), (vp_all,) = _run_comms([_exchange_comm([dmod_blocks], False), _exchange_comm([vp], True)], "tail_exchange")
    res["w_ada"] = _ada_grad_adam(cact_all.T, dmod_recv[:, 0:nl].transpose(1, 0, 2), w_ada, m_w_ada, v_w_ada)
    mp_r = matrices["mp"]
    lanes = lambda a: jnp.pad(a, ((0, 0), (0, LANES - a.shape[1])))
    shaped = dict(ml_b_if=lanes, final_g=lambda a: a.reshape(1, d))
    names = [n for n in REPLICATED if n != "b_ada"] + ["b_ada"]
    rep_res, vp_r = _adam_replicated(vp_all, mp_r, {n: tuple(shaped.get(n, lambda a: a)(given[p + n]) for p in ("", "m_", "v_"))
                                                    for n in names}, nl)
    unshaped = dict(ml_b_if=lambda a: a[:, 0:ml_b_if.shape[1]], final_g=lambda a: a.reshape(d))
    rep_out = [{n: unshaped.get(n, lambda a: a)(rep_res[n][kind]) for n in names} for kind in range(4)]
    loss = vp_r[nl * len(REP_ROWS) + 1, 0]

    if_rows = ml_w_if.shape[1]
    order = ("norm_g", "w_ada", "b_ada", "w_in", "rg_conv_w", "rg_conv_b", "rg_w_a", "rg_b_a", "rg_w_x", "rg_b_x",
             "rg_lambda", "ml_conv_w", "ml_conv_b", "ml_w_q", "ml_w_k", "ml_w_v", "ml_w_if", "ml_b_if", "ml_norm_g",
             "w_out", "final_g")
    outs = [loss, grad_x[None]]
    for kind in range(4):
        qkv_k = res["w_qkv"][kind].reshape((nl, 3) + ml_w_q.shape[1:])
        rg_cw, ml_cw, wif = _small_unpack(res["small"][kind], if_rows)
        sharded = dict(w_ada=res["w_ada"][kind], w_in=res["w_in"][kind], w_out=res["w_out"][kind], ml_w_q=qkv_k[:, 0],
                       ml_w_k=qkv_k[:, 1], ml_w_v=qkv_k[:, 2], rg_conv_w=rg_cw, ml_conv_w=ml_cw, ml_w_if=wif)
        for n in order:
            outs.append(sharded[n] if n in sharded else rep_out[kind][n])
    return tuple(outs)
```
